```python
import jax, jax.numpy as jnp
from jax import lax
import numpy as np

D_MODEL = 1024
BATCH = 8
SEQ = 8192
DEPTH = 1

PLE_DIM = 256
ML_HEADS = 4
ML_DQK = 128
ML_DV = 256
ML_CONV = 4
ML_CHUNK = 64
SW_Q_HEADS = 16
SW_KV_HEADS = 4
SW_HEAD_DIM = 64
SW_WINDOW = 128
D_FF = 4 * D_MODEL
EPS = 1e-6

ML_QK_W = ML_HEADS * ML_DQK
ML_V_W = ML_HEADS * ML_DV
SW_Q_W = SW_Q_HEADS * SW_HEAD_DIM
SW_KV_W = SW_KV_HEADS * SW_HEAD_DIM
SPLIT_SIZES = (2 * ML_QK_W, ML_V_W, ML_V_W, 2 * ML_HEADS, SW_Q_W, SW_KV_W, SW_KV_W, D_MODEL, D_MODEL)
N_IN = sum(SPLIT_SIZES)

kernel_name = "hybrid_mlstm_swa_sink_parallel_block"


def rmsnorm(x, g):
    xf = x.astype(jnp.float32)
    xf = xf * lax.rsqrt(jnp.mean(xf * xf, axis=-1, keepdims=True) + EPS)
    return xf.astype(x.dtype) * g.astype(x.dtype)


def causal_depthwise_conv(x, w):
    c = x.shape[-1]
    return lax.conv_general_dilated(
        x, w.astype(x.dtype)[:, None, :], window_strides=(1,), padding=((ML_CONV - 1, 0),),
        dimension_numbers=("NWC", "WIO", "NWC"), feature_group_count=c)


def mlstm_chunkwise(q, k, v, ig, lf):
    B, S = q.shape[:2]
    L = ML_CHUNK
    nc = S // L

    def to_chunks(t):
        t = t.astype(jnp.float32).reshape((B, nc, L) + t.shape[2:])
        return jnp.moveaxis(jnp.moveaxis(t, 1, 0), 3, 2)

    xs = (to_chunks(q), to_chunks(k), to_chunks(v), to_chunks(ig), to_chunks(lf))
    causal = jnp.tril(jnp.ones((L, L), dtype=bool))

    def step(carry, chunk):
        C, n, m = carry
        qc, kc, vc, ic, fc = chunk
        b = jnp.cumsum(fc, axis=-1)
        log_d = jnp.where(causal, b[..., :, None] - b[..., None, :] + ic[..., None, :], -jnp.inf)
        inter = b + m[..., None]
        m_t = jnp.maximum(inter, jnp.max(log_d, axis=-1))
        scores = jnp.einsum("bhtk,bhsk->bhts", qc, kc) * jnp.exp(log_d - m_t[..., None])
        w_inter = jnp.exp(inter - m_t)
        num = (w_inter[..., None] * jnp.einsum("bhvk,bhtk->bhtv", C, qc)
               + jnp.einsum("bhts,bhsv->bhtv", scores, vc))
        den = w_inter * jnp.einsum("bhk,bhtk->bht", n, qc) + jnp.sum(scores, axis=-1)
        h = num / jnp.maximum(jnp.abs(den), jnp.exp(-m_t))[..., None]
        b_last = b[..., -1]
        log_w = b_last[..., None] - b + ic
        m_new = jnp.maximum(b_last + m, jnp.max(log_w, axis=-1))
        w = jnp.exp(log_w - m_new[..., None])
        decay = jnp.exp(b_last + m - m_new)
        C_new = decay[..., None, None] * C + jnp.einsum("bhs,bhsv,bhsk->bhvk", w, vc, kc)
        n_new = decay[..., None] * n + jnp.einsum("bhs,bhsk->bhk", w, kc)
        return (C_new, n_new, m_new), h

    H, dk, dv = q.shape[2], q.shape[3], v.shape[3]
    init = (jnp.zeros((B, H, dv, dk), jnp.float32), jnp.zeros((B, H, dk), jnp.float32),
            jnp.zeros((B, H), jnp.float32))
    _, h = lax.scan(step, init, xs)
    h = jnp.moveaxis(jnp.moveaxis(h, 2, 3), 0, 1)
    return h.reshape(B, S, H, dv)


def swa_with_sinks(q, k, v, sinks):
    B, S = q.shape[:2]
    W = SW_WINDOW
    nb = S // W
    G = SW_Q_HEADS // SW_KV_HEADS
    qb = q.reshape(B, nb, W, SW_KV_HEADS, G, SW_HEAD_DIM)

    def band(t):
        tb = t.reshape(B, nb, W, SW_KV_HEADS, SW_HEAD_DIM)
        prev = jnp.pad(tb, ((0, 0), (1, 0), (0, 0), (0, 0), (0, 0)))[:, :-1]
        return jnp.concatenate([prev, tb], axis=2)

    kb, vb = band(k), band(v)
    logits = jnp.einsum("bnqhgd,bnkhd->bnhgqk", qb, kb).astype(jnp.float32) * (SW_HEAD_DIM ** -0.5)
    qi = jnp.arange(W)[:, None]
    ki = jnp.arange(2 * W)[None, :]
    diff = qi + W - ki
    band_mask = (diff >= 0) & (diff < W)
    valid = band_mask[None] & ((jnp.arange(nb)[:, None, None] > 0) | (ki >= W)[None])
    logits = jnp.where(valid[None, :, None, None], logits, -jnp.inf)
    sink = sinks.astype(jnp.float32).reshape(SW_KV_HEADS, G)[None, None, :, :, None, None]
    m = jnp.maximum(jnp.max(logits, axis=-1, keepdims=True), sink)
    pexp = jnp.exp(logits - m)
    probs = pexp / (jnp.sum(pexp, axis=-1, keepdims=True) + jnp.exp(sink - m))
    out = jnp.einsum("bnhgqk,bnkhd->bnqhgd", probs.astype(v.dtype), vb)
    return out.reshape(B, S, SW_Q_W)


def _fwd_setup_inputs(seed: int = 0) -> dict:
    key = jax.random.key(seed)
    ks = jax.random.split(key, 20)
    f32 = jnp.float32
    nrm = lambda k, shape, s: jax.random.normal(k, shape, f32) * s
    gain = lambda k, shape: 1.0 + 0.05 * jax.random.normal(k, shape, f32)
    b_if = jnp.concatenate([
        0.1 * jax.random.normal(ks[3], (DEPTH, ML_HEADS), f32),
        3.0 + 0.5 * jax.random.normal(ks[4], (DEPTH, ML_HEADS), f32),
    ], axis=-1)
    return {
        "x": jax.random.normal(ks[0], (BATCH, SEQ, D_MODEL), f32),
        "p": jax.random.normal(ks[1], (DEPTH, BATCH, SEQ, PLE_DIM), f32),
        "norm_mix_g": gain(ks[2], (DEPTH, D_MODEL)),
        "w_in": nrm(ks[5], (DEPTH, D_MODEL, N_IN), D_MODEL ** -0.5),
        "conv_qk": nrm(ks[6], (DEPTH, ML_CONV, 2 * ML_QK_W), ML_CONV ** -0.5),
        "b_if": b_if,
        "mlstm_norm_g": gain(ks[7], (DEPTH, ML_V_W)),
        "sinks": nrm(ks[8], (DEPTH, SW_Q_HEADS), 0.5),
        "w_branch_a": nrm(ks[9], (DEPTH, ML_V_W, D_MODEL), ML_V_W ** -0.5),
        "w_branch_b": nrm(ks[10], (DEPTH, SW_Q_W, D_MODEL), SW_Q_W ** -0.5),
        "w_out": nrm(ks[11], (DEPTH, D_MODEL, D_MODEL), D_MODEL ** -0.5),
        "norm_mlp_g": gain(ks[12], (DEPTH, D_MODEL)),
        "w_up": nrm(ks[13], (DEPTH, D_MODEL, D_FF), D_MODEL ** -0.5),
        "w_down": nrm(ks[14], (DEPTH, D_FF, D_MODEL), D_FF ** -0.5),
        "norm_ple_g": gain(ks[15], (DEPTH, D_MODEL)),
        "w_ple_gate": nrm(ks[16], (DEPTH, D_MODEL, D_MODEL), D_MODEL ** -0.5),
        "w_ple_proj": nrm(ks[17], (DEPTH, PLE_DIM, D_MODEL), PLE_DIM ** -0.5),
        "final_norm_g": gain(ks[18], (D_MODEL,)),
    }


def _fwd_reference(x, p, norm_mix_g, w_in, conv_qk, b_if, mlstm_norm_g, sinks, w_branch_a, w_branch_b,
              w_out, norm_mlp_g, w_up, w_down, norm_ple_g, w_ple_gate, w_ple_proj, final_norm_g):
    B, S, _ = x.shape
    split_idx = [int(v) for v in np.cumsum(SPLIT_SIZES)[:-1]]
    for i in range(DEPTH):
        h = rmsnorm(x, norm_mix_g[i])
        proj = h @ w_in[i]
        qk_ml, v_ml, o_ml, if_pre, q_sw, k_sw, v_sw, g_a, g_b = jnp.split(proj, split_idx, axis=-1)

        qk_ml = jax.nn.silu(causal_depthwise_conv(qk_ml, conv_qk[i]))
        q_ml, k_ml = jnp.split(qk_ml, 2, axis=-1)
        q_ml = q_ml.reshape(B, S, ML_HEADS, ML_DQK) * (ML_DQK ** -0.5)
        k_ml = k_ml.reshape(B, S, ML_HEADS, ML_DQK)
        gates = (if_pre + b_if[i].astype(if_pre.dtype)).astype(jnp.float32)
        ig = gates[..., :ML_HEADS]
        lf = jax.nn.log_sigmoid(gates[..., ML_HEADS:])
        h_ml = mlstm_chunkwise(q_ml, k_ml, v_ml.reshape(B, S, ML_HEADS, ML_DV), ig, lf)
        h_ml = h_ml * lax.rsqrt(jnp.mean(h_ml * h_ml, axis=-1, keepdims=True) + EPS)
        h_ml = h_ml.reshape(B, S, ML_V_W).astype(x.dtype) * mlstm_norm_g[i].astype(x.dtype)
        y_a = jax.nn.sigmoid(o_ml) * h_ml

        y_b = swa_with_sinks(q_sw.reshape(B, S, SW_Q_HEADS, SW_HEAD_DIM),
                             k_sw.reshape(B, S, SW_KV_HEADS, SW_HEAD_DIM),
                             v_sw.reshape(B, S, SW_KV_HEADS, SW_HEAD_DIM), sinks[i])

        merged = jax.nn.sigmoid(g_a) * (y_a @ w_branch_a[i]) + jax.nn.sigmoid(g_b) * (y_b @ w_branch_b[i])
        x = x + merged @ w_out[i]

        u = rmsnorm(x, norm_mlp_g[i]) @ w_up[i]
        x = x + jnp.square(jax.nn.relu(u)) @ w_down[i]

        gate = jax.nn.sigmoid(rmsnorm(x, norm_ple_g[i]) @ w_ple_gate[i])
        x = x + gate * (p[i].astype(x.dtype) @ w_ple_proj[i])
    return rmsnorm(x, final_norm_g)


import jax as _jax
import jax.numpy as _jnp

TWIN_FORMAT = 'train_step'
FWD_PARAMS = ['x', 'p', 'norm_mix_g', 'w_in', 'conv_qk', 'b_if', 'mlstm_norm_g', 'sinks', 'w_branch_a', 'w_branch_b', 'w_out', 'norm_mlp_g', 'w_up', 'w_down', 'norm_ple_g', 'w_ple_gate', 'w_ple_proj', 'final_norm_g']
TWIN_WEIGHTS = ['norm_mix_g', 'w_in', 'conv_qk', 'b_if', 'mlstm_norm_g', 'sinks', 'w_branch_a', 'w_branch_b', 'w_out', 'norm_mlp_g', 'w_up', 'w_down', 'norm_ple_g', 'w_ple_gate', 'w_ple_proj', 'final_norm_g']
TWIN_DIFF_INPUT = 'x'
TWIN_INPUTS = ['x', 'p', 'norm_mix_g', 'w_in', 'conv_qk', 'b_if', 'mlstm_norm_g', 'sinks', 'w_branch_a', 'w_branch_b', 'w_out', 'norm_mlp_g', 'w_up', 'w_down', 'norm_ple_g', 'w_ple_gate', 'w_ple_proj', 'final_norm_g', 'loss_target', 'm_norm_mix_g', 'm_w_in', 'm_conv_qk', 'm_b_if', 'm_mlstm_norm_g', 'm_sinks', 'm_w_branch_a', 'm_w_branch_b', 'm_w_out', 'm_norm_mlp_g', 'm_w_up', 'm_w_down', 'm_norm_ple_g', 'm_w_ple_gate', 'm_w_ple_proj', 'm_final_norm_g', 'v_norm_mix_g', 'v_w_in', 'v_conv_qk', 'v_b_if', 'v_mlstm_norm_g', 'v_sinks', 'v_w_branch_a', 'v_w_branch_b', 'v_w_out', 'v_norm_mlp_g', 'v_w_up', 'v_w_down', 'v_norm_ple_g', 'v_w_ple_gate', 'v_w_ple_proj', 'v_final_norm_g']
TWIN_OUTPUTS = ['loss', 'grad_x', 'grad_norm_mix_g', 'grad_w_in', 'grad_conv_qk', 'grad_b_if', 'grad_mlstm_norm_g', 'grad_sinks', 'grad_w_branch_a', 'grad_w_branch_b', 'grad_w_out', 'grad_norm_mlp_g', 'grad_w_up', 'grad_w_down', 'grad_norm_ple_g', 'grad_w_ple_gate', 'grad_w_ple_proj', 'grad_final_norm_g', 'delta_norm_mix_g', 'delta_w_in', 'delta_conv_qk', 'delta_b_if', 'delta_mlstm_norm_g', 'delta_sinks', 'delta_w_branch_a', 'delta_w_branch_b', 'delta_w_out', 'delta_norm_mlp_g', 'delta_w_up', 'delta_w_down', 'delta_norm_ple_g', 'delta_w_ple_gate', 'delta_w_ple_proj', 'delta_final_norm_g', 'new_m_norm_mix_g', 'new_m_w_in', 'new_m_conv_qk', 'new_m_b_if', 'new_m_mlstm_norm_g', 'new_m_sinks', 'new_m_w_branch_a', 'new_m_w_branch_b', 'new_m_w_out', 'new_m_norm_mlp_g', 'new_m_w_up', 'new_m_w_down', 'new_m_norm_ple_g', 'new_m_w_ple_gate', 'new_m_w_ple_proj', 'new_m_final_norm_g', 'new_v_norm_mix_g', 'new_v_w_in', 'new_v_conv_qk', 'new_v_b_if', 'new_v_mlstm_norm_g', 'new_v_sinks', 'new_v_w_branch_a', 'new_v_w_branch_b', 'new_v_w_out', 'new_v_norm_mlp_g', 'new_v_w_up', 'new_v_w_down', 'new_v_norm_ple_g', 'new_v_w_ple_gate', 'new_v_w_ple_proj', 'new_v_final_norm_g']
TWIN_LEAF_KINDS = {'loss': 'loss', 'grad_x': 'grad_x', 'grad_norm_mix_g': 'grad_w', 'grad_w_in': 'grad_w', 'grad_conv_qk': 'grad_w', 'grad_b_if': 'grad_w', 'grad_mlstm_norm_g': 'grad_w', 'grad_sinks': 'grad_w', 'grad_w_branch_a': 'grad_w', 'grad_w_branch_b': 'grad_w', 'grad_w_out': 'grad_w', 'grad_norm_mlp_g': 'grad_w', 'grad_w_up': 'grad_w', 'grad_w_down': 'grad_w', 'grad_norm_ple_g': 'grad_w', 'grad_w_ple_gate': 'grad_w', 'grad_w_ple_proj': 'grad_w', 'grad_final_norm_g': 'grad_w', 'delta_norm_mix_g': 'delta_w', 'delta_w_in': 'delta_w', 'delta_conv_qk': 'delta_w', 'delta_b_if': 'delta_w', 'delta_mlstm_norm_g': 'delta_w', 'delta_sinks': 'delta_w', 'delta_w_branch_a': 'delta_w', 'delta_w_branch_b': 'delta_w', 'delta_w_out': 'delta_w', 'delta_norm_mlp_g': 'delta_w', 'delta_w_up': 'delta_w', 'delta_w_down': 'delta_w', 'delta_norm_ple_g': 'delta_w', 'delta_w_ple_gate': 'delta_w', 'delta_w_ple_proj': 'delta_w', 'delta_final_norm_g': 'delta_w', 'new_m_norm_mix_g': 'new_m', 'new_m_w_in': 'new_m', 'new_m_conv_qk': 'new_m', 'new_m_b_if': 'new_m', 'new_m_mlstm_norm_g': 'new_m', 'new_m_sinks': 'new_m', 'new_m_w_branch_a': 'new_m', 'new_m_w_branch_b': 'new_m', 'new_m_w_out': 'new_m', 'new_m_norm_mlp_g': 'new_m', 'new_m_w_up': 'new_m', 'new_m_w_down': 'new_m', 'new_m_norm_ple_g': 'new_m', 'new_m_w_ple_gate': 'new_m', 'new_m_w_ple_proj': 'new_m', 'new_m_final_norm_g': 'new_m', 'new_v_norm_mix_g': 'new_v', 'new_v_w_in': 'new_v', 'new_v_conv_qk': 'new_v', 'new_v_b_if': 'new_v', 'new_v_mlstm_norm_g': 'new_v', 'new_v_sinks': 'new_v', 'new_v_w_branch_a': 'new_v', 'new_v_w_branch_b': 'new_v', 'new_v_w_out': 'new_v', 'new_v_norm_mlp_g': 'new_v', 'new_v_w_up': 'new_v', 'new_v_w_down': 'new_v', 'new_v_norm_ple_g': 'new_v', 'new_v_w_ple_gate': 'new_v', 'new_v_w_ple_proj': 'new_v', 'new_v_final_norm_g': 'new_v'}


def _forward(args):
    return _fwd_reference(*[args[k] for k in FWD_PARAMS])


def _output_shape():
    def fwd():
        inp = _fwd_setup_inputs(0)
        return _fwd_reference(*[inp[k] for k in FWD_PARAMS])
    out = _jax.eval_shape(fwd)
    return out.shape, out.dtype

N_MICROBATCH = 1
ADAM_LR = 0.001
ADAM_B1 = 0.9
ADAM_B2 = 0.999
ADAM_EPS = 1e-08
ADAM_WD = 0.01
ADAM_STEP = 10
PER_EXAMPLE_BATCH_AXIS = {'x': 0, 'p': 1, 'loss_target': 0}
SHARED_INPUTS = []
_WEIGHT_DTYPES = {'norm_mix_g': _jnp.float32, 'w_in': _jnp.float32, 'conv_qk': _jnp.float32, 'b_if': _jnp.float32, 'mlstm_norm_g': _jnp.float32, 'sinks': _jnp.float32, 'w_branch_a': _jnp.float32, 'w_branch_b': _jnp.float32, 'w_out': _jnp.float32, 'norm_mlp_g': _jnp.float32, 'w_up': _jnp.float32, 'w_down': _jnp.float32, 'norm_ple_g': _jnp.float32, 'w_ple_gate': _jnp.float32, 'w_ple_proj': _jnp.float32, 'final_norm_g': _jnp.float32}
MOMENT_SCALE = {'norm_mix_g': 1.639257e-01, 'w_in': 6.470158e-02, 'conv_qk': 1.232969e-01, 'b_if': 3.083546e-01, 'mlstm_norm_g': 7.748391e-02, 'sinks': 1.404859e-02, 'w_branch_a': 7.571872e-02, 'w_branch_b': 2.827776e-02, 'w_out': 7.937076e-02, 'norm_mlp_g': 2.308652e-01, 'w_up': 1.085298e-01, 'w_down': 4.077145e-01, 'norm_ple_g': 5.232569e-02, 'w_ple_gate': 5.222296e-02, 'w_ple_proj': 8.287300e-02, 'final_norm_g': 6.458716e+01}


def _to_microbatches(a, axis):
    t = _jnp.moveaxis(a, axis, 0)
    t = t.reshape((N_MICROBATCH, t.shape[0] // N_MICROBATCH) + t.shape[1:])
    return _jnp.moveaxis(t, 1, axis + 1)


def setup_inputs(seed: int = 0) -> dict:
    inp = _fwd_setup_inputs(seed)
    key = _jax.random.fold_in(_jax.random.key(seed), 7919)
    shape, _ = _output_shape()
    out = dict(inp)
    out["loss_target"] = _jax.random.normal(_jax.random.fold_in(key, 0), shape, _jnp.float32)
    for i, name in enumerate(TWIN_WEIGHTS):
        w = inp[name].astype(_jnp.float32)
        if MOMENT_SCALE is None:
            s = _jnp.sqrt(_jnp.mean(_jnp.square(w)) + 1e-30)
        else:
            s = MOMENT_SCALE[name]
        km, kv = _jax.random.split(_jax.random.fold_in(key, i + 1))
        out[name] = w
        out["m_" + name] = s * _jax.random.normal(km, w.shape, _jnp.float32)
        out["v_" + name] = (s * s) * _jax.random.uniform(kv, w.shape, _jnp.float32, 0.5, 1.5)
    if N_MICROBATCH > 1:
        for name, axis in PER_EXAMPLE_BATCH_AXIS.items():
            out[name] = _to_microbatches(out[name], axis)
    return {'x': out['x'], 'p': out['p'], 'norm_mix_g': out['norm_mix_g'], 'w_in': out['w_in'], 'conv_qk': out['conv_qk'], 'b_if': out['b_if'], 'mlstm_norm_g': out['mlstm_norm_g'], 'sinks': out['sinks'], 'w_branch_a': out['w_branch_a'], 'w_branch_b': out['w_branch_b'], 'w_out': out['w_out'], 'norm_mlp_g': out['norm_mlp_g'], 'w_up': out['w_up'], 'w_down': out['w_down'], 'norm_ple_g': out['norm_ple_g'], 'w_ple_gate': out['w_ple_gate'], 'w_ple_proj': out['w_ple_proj'], 'final_norm_g': out['final_norm_g'], 'loss_target': out['loss_target'], 'm_norm_mix_g': out['m_norm_mix_g'], 'm_w_in': out['m_w_in'], 'm_conv_qk': out['m_conv_qk'], 'm_b_if': out['m_b_if'], 'm_mlstm_norm_g': out['m_mlstm_norm_g'], 'm_sinks': out['m_sinks'], 'm_w_branch_a': out['m_w_branch_a'], 'm_w_branch_b': out['m_w_branch_b'], 'm_w_out': out['m_w_out'], 'm_norm_mlp_g': out['m_norm_mlp_g'], 'm_w_up': out['m_w_up'], 'm_w_down': out['m_w_down'], 'm_norm_ple_g': out['m_norm_ple_g'], 'm_w_ple_gate': out['m_w_ple_gate'], 'm_w_ple_proj': out['m_w_ple_proj'], 'm_final_norm_g': out['m_final_norm_g'], 'v_norm_mix_g': out['v_norm_mix_g'], 'v_w_in': out['v_w_in'], 'v_conv_qk': out['v_conv_qk'], 'v_b_if': out['v_b_if'], 'v_mlstm_norm_g': out['v_mlstm_norm_g'], 'v_sinks': out['v_sinks'], 'v_w_branch_a': out['v_w_branch_a'], 'v_w_branch_b': out['v_w_branch_b'], 'v_w_out': out['v_w_out'], 'v_norm_mlp_g': out['v_norm_mlp_g'], 'v_w_up': out['v_w_up'], 'v_w_down': out['v_w_down'], 'v_norm_ple_g': out['v_norm_ple_g'], 'v_w_ple_gate': out['v_w_ple_gate'], 'v_w_ple_proj': out['v_w_ple_proj'], 'v_final_norm_g': out['v_final_norm_g']}


def _loss(weights, diff, rest, loss_target):
    with _jax.named_scope("forward"):
        args = {**rest, TWIN_DIFF_INPUT: diff, **{k: w.astype(_WEIGHT_DTYPES[k]) for k, w in weights.items()}}
        y = _forward(args)
    with _jax.named_scope("loss_head"):
        err = _jnp.square(y.astype(_jnp.float32) - loss_target)
        return 0.5 * _jnp.sum(_jnp.mean(err, axis=-1)) if err.ndim else 0.5 * err


def _adamw(w, g, m, v):
    m = ADAM_B1 * m + (1.0 - ADAM_B1) * g
    v = ADAM_B2 * v + (1.0 - ADAM_B2) * _jnp.square(g)
    m_hat = m / (1.0 - ADAM_B1 ** ADAM_STEP)
    v_hat = v / (1.0 - ADAM_B2 ** ADAM_STEP)
    delta = -ADAM_LR * (m_hat / (_jnp.sqrt(v_hat) + ADAM_EPS) + ADAM_WD * w)
    return delta, m, v


def reference(x, p, norm_mix_g, w_in, conv_qk, b_if, mlstm_norm_g, sinks, w_branch_a, w_branch_b, w_out, norm_mlp_g, w_up, w_down, norm_ple_g, w_ple_gate, w_ple_proj, final_norm_g, loss_target, m_norm_mix_g, m_w_in, m_conv_qk, m_b_if, m_mlstm_norm_g, m_sinks, m_w_branch_a, m_w_branch_b, m_w_out, m_norm_mlp_g, m_w_up, m_w_down, m_norm_ple_g, m_w_ple_gate, m_w_ple_proj, m_final_norm_g, v_norm_mix_g, v_w_in, v_conv_qk, v_b_if, v_mlstm_norm_g, v_sinks, v_w_branch_a, v_w_branch_b, v_w_out, v_norm_mlp_g, v_w_up, v_w_down, v_norm_ple_g, v_w_ple_gate, v_w_ple_proj, v_final_norm_g):
    given = dict(x=x, p=p, norm_mix_g=norm_mix_g, w_in=w_in, conv_qk=conv_qk, b_if=b_if, mlstm_norm_g=mlstm_norm_g, sinks=sinks, w_branch_a=w_branch_a, w_branch_b=w_branch_b, w_out=w_out, norm_mlp_g=norm_mlp_g, w_up=w_up, w_down=w_down, norm_ple_g=norm_ple_g, w_ple_gate=w_ple_gate, w_ple_proj=w_ple_proj, final_norm_g=final_norm_g, loss_target=loss_target, m_norm_mix_g=m_norm_mix_g, m_w_in=m_w_in, m_conv_qk=m_conv_qk, m_b_if=m_b_if, m_mlstm_norm_g=m_mlstm_norm_g, m_sinks=m_sinks, m_w_branch_a=m_w_branch_a, m_w_branch_b=m_w_branch_b, m_w_out=m_w_out, m_norm_mlp_g=m_norm_mlp_g, m_w_up=m_w_up, m_w_down=m_w_down, m_norm_ple_g=m_norm_ple_g, m_w_ple_gate=m_w_ple_gate, m_w_ple_proj=m_w_ple_proj, m_final_norm_g=m_final_norm_g, v_norm_mix_g=v_norm_mix_g, v_w_in=v_w_in, v_conv_qk=v_conv_qk, v_b_if=v_b_if, v_mlstm_norm_g=v_mlstm_norm_g, v_sinks=v_sinks, v_w_branch_a=v_w_branch_a, v_w_branch_b=v_w_branch_b, v_w_out=v_w_out, v_norm_mlp_g=v_norm_mlp_g, v_w_up=v_w_up, v_w_down=v_w_down, v_norm_ple_g=v_norm_ple_g, v_w_ple_gate=v_w_ple_gate, v_w_ple_proj=v_w_ple_proj, v_final_norm_g=v_final_norm_g)
    weights = {n: given[n] for n in TWIN_WEIGHTS}
    shared = {n: given[n] for n in SHARED_INPUTS}
    per_example = {n: given[n] for n in ['x', 'p']}
    grad_fn = _jax.value_and_grad(_loss, argnums=(0, 1))

    def one_microbatch(ex, loss_target):
        ex = dict(ex)
        diff = ex.pop(TWIN_DIFF_INPUT)
        return grad_fn(weights, diff, {**shared, **ex}, loss_target)

    if N_MICROBATCH == 1:
        loss, (grad_w, grad_x) = one_microbatch(per_example, given["loss_target"])
    else:
        def body(carry, xs):
            loss_sum, grad_sum = carry
            l_k, (gw_k, gx_k) = one_microbatch(xs[0], xs[1])
            with _jax.named_scope("update"):
                return (loss_sum + l_k, _jax.tree.map(_jnp.add, grad_sum, gw_k)), gx_k

        init = (_jnp.zeros((), _jnp.float32), _jax.tree.map(_jnp.zeros_like, weights))
        (loss, grad_w), grad_x = _jax.lax.scan(body, init, (per_example, given["loss_target"]))
    with _jax.named_scope("update"):
        delta_w, new_m, new_v = {}, {}, {}
        for n in TWIN_WEIGHTS:
            delta_w[n], new_m[n], new_v[n] = _adamw(weights[n], grad_w[n], given["m_" + n], given["v_" + n])
    return (loss, grad_x, *[grad_w[n] for n in TWIN_WEIGHTS], *[delta_w[n] for n in TWIN_WEIGHTS],
            *[new_m[n] for n in TWIN_WEIGHTS], *[new_v[n] for n in TWIN_WEIGHTS])
```

```python
import functools

import jax
import jax.numpy as jnp
from jax import lax
from jax.experimental import pallas as pl
from jax.experimental.pallas import tpu as pltpu

F32 = jnp.float32
BF16 = jnp.bfloat16

D = 1024
PLE = 256
MLH = 4
DQK = 128
DV = 256
CONV = 4
CHUNK = 128
SWH = 16
SWKV = 4
SWG = SWH // SWKV
HD = 64
WIN = 128
DFF = 4096
EPS = 1e-6
N_IN = 6664
NP = 7168
C_QK, C_V, C_O, C_QSW, C_GA, C_GB, C_KV, C_IF = 0, 1024, 2048, 3072, 4096, 5120, 6144, 6656
IFW = NP - C_IF

ADAM_LR = 0.001
ADAM_B1 = 0.9
ADAM_B2 = 0.999
ADAM_EPS = 1e-08
ADAM_WD = 0.01
ADAM_STEP = 10

TOK_TILE = 256
VMEM_LIMIT = 48 * 1024 * 1024


def _params(**kw):
    return pltpu.CompilerParams(vmem_limit_bytes=VMEM_LIMIT, **kw)


def _pick(n, cap):
    if n <= cap:
        return n
    t = cap - cap % 128
    while t > 128 and n % t:
        t -= 128
    assert n % t == 0, (n, cap)
    return t


def _dot(a, b, dims):
    return lax.dot_general(a, b, (dims, ((), ())), preferred_element_type=F32)


def _dot_nn(a, b):
    return _dot(a, b, ((1,), (0,)))


def _dot_nt(a, b):
    return _dot(a, b, ((1,), (1,)))


def _dot_tn(a, b):
    return _dot(a, b, ((0,), (0,)))


def _sigmoid(x):
    return 1.0 / (1.0 + jnp.exp(-x))


def _mm(a, b, mode, out_dtype, name):
    if mode == "nn":
        (m, k), (k2, n) = a.shape, b.shape
    elif mode == "nt":
        (m, k), (n, k2) = a.shape, b.shape
    else:
        (k, m), (k2, n) = a.shape, b.shape
    assert k == k2, (a.shape, b.shape, mode)
    tm, tn, tk = _pick(m, 1024), _pick(n, 512), _pick(k, 1024)
    nk = k // tk
    if mode == "nn":
        a_spec = pl.BlockSpec((tm, tk), lambda i, j, kk: (i, kk))
        b_spec = pl.BlockSpec((tk, tn), lambda i, j, kk: (kk, j))
        dot = _dot_nn
    elif mode == "nt":
        a_spec = pl.BlockSpec((tm, tk), lambda i, j, kk: (i, kk))
        b_spec = pl.BlockSpec((tn, tk), lambda i, j, kk: (j, kk))
        dot = _dot_nt
    else:
        a_spec = pl.BlockSpec((tk, tm), lambda i, j, kk: (kk, i))
        b_spec = pl.BlockSpec((tk, tn), lambda i, j, kk: (kk, j))
        dot = _dot_tn

    def body(a_ref, b_ref, o_ref, acc_ref):
        kk = pl.program_id(2)

        @pl.when(kk == 0)
        def _():
            acc_ref[...] = jnp.zeros_like(acc_ref)

        acc_ref[...] += dot(a_ref[...], b_ref[...])

        @pl.when(kk == nk - 1)
        def _():
            o_ref[...] = acc_ref[...].astype(out_dtype)

    return pl.pallas_call(
        body, name=name, grid=(m // tm, n // tn, nk),
        in_specs=[a_spec, b_spec],
        out_specs=pl.BlockSpec((tm, tn), lambda i, j, kk: (i, j)),
        out_shape=jax.ShapeDtypeStruct((m, n), out_dtype),
        scratch_shapes=[pltpu.VMEM((tm, tn), F32)],
        compiler_params=_params(dimension_semantics=("parallel", "parallel", "arbitrary")),
    )(a, b)


def _tok(w, j=0):
    return pl.BlockSpec((TOK_TILE, w), lambda i: (i, j))


def _rep(shape):
    return pl.BlockSpec(shape, lambda i: (0,) * len(shape))


def _rms(x):
    rstd = lax.rsqrt(jnp.mean(x * x, axis=-1, keepdims=True) + EPS)
    return x * rstd, rstd


def _rms_bwd(xn, rstd, dxn):
    return rstd * (dxn - xn * jnp.mean(dxn * xn, axis=-1, keepdims=True))


def _norm_fwd(x, g, name):
    t = x.shape[0]

    def body(x_ref, g_ref, h_ref):
        xn, _ = _rms(x_ref[...])
        h_ref[...] = (xn * g_ref[...]).astype(BF16)

    return pl.pallas_call(
        body, name=name, grid=(t // TOK_TILE,), in_specs=[_tok(D), _rep((1, D))], out_specs=_tok(D),
        out_shape=jax.ShapeDtypeStruct((t, D), BF16), compiler_params=_params(),
    )(x, g)


def _resid_norm_fwd(x, o, g, name):
    t = x.shape[0]

    def body(x_ref, o_ref, g_ref, x1_ref, h_ref):
        x1 = x_ref[...] + o_ref[...]
        x1_ref[...] = x1
        xn, _ = _rms(x1)
        h_ref[...] = (xn * g_ref[...]).astype(BF16)

    return pl.pallas_call(
        body, name=name, grid=(t // TOK_TILE,), in_specs=[_tok(D), _tok(D), _rep((1, D))],
        out_specs=[_tok(D), _tok(D)],
        out_shape=[jax.ShapeDtypeStruct((t, D), F32), jax.ShapeDtypeStruct((t, D), BF16)],
        compiler_params=_params(),
    )(x, o, g)


def _norm_bwd_resid(x, g, dh, dres, name):
    t = x.shape[0]

    def body(x_ref, g_ref, dh_ref, dres_ref, dx_ref, dxb_ref, dg_ref):
        @pl.when(pl.program_id(0) == 0)
        def _():
            dg_ref[...] = jnp.zeros_like(dg_ref)

        xn, rstd = _rms(x_ref[...])
        dh_ = dh_ref[...]
        dg_ref[...] += jnp.sum(dh_ * xn, axis=0, keepdims=True)
        dx = dres_ref[...] + _rms_bwd(xn, rstd, dh_ * g_ref[...])
        dx_ref[...] = dx
        dxb_ref[...] = dx.astype(BF16)

    return pl.pallas_call(
        body, name=name, grid=(t // TOK_TILE,), in_specs=[_tok(D), _rep((1, D)), _tok(D), _tok(D)],
        out_specs=[_tok(D), _tok(D), _rep((1, D))],
        out_shape=[jax.ShapeDtypeStruct((t, D), F32), jax.ShapeDtypeStruct((t, D), BF16),
                   jax.ShapeDtypeStruct((1, D), F32)],
        compiler_params=_params(),
    )(x, g, dh, dres)


def _halo_prev(w, j=0):
    r = TOK_TILE // 8
    return pl.BlockSpec((8, w), lambda i: (jnp.maximum(i * r - 1, 0), j))


def _halo_next(w, nt, j=0):
    r = TOK_TILE // 8
    return pl.BlockSpec((8, w), lambda i: (jnp.minimum((i + 1) * r, nt * r - 1), j))


def _shift_down(x, halo, s):
    if s == 0:
        return x
    r = pltpu.roll(x, s, 0)
    hs = pltpu.roll(halo, s, 0)
    row = lax.broadcasted_iota(jnp.int32, hs.shape, 0)
    top = jnp.where(row < s, hs, r[0:8])
    return jnp.concatenate([top, r[8:]], axis=0)


def _shift_up(x, halo, s):
    if s == 0:
        return x
    n = x.shape[0]
    r = pltpu.roll(x, n - s, 0)
    hs = pltpu.roll(halo, 8 - s, 0)
    row = lax.broadcasted_iota(jnp.int32, hs.shape, 0)
    bot = jnp.where(row >= 8 - s, hs, r[n - 8:])
    return jnp.concatenate([r[:n - 8], bot], axis=0)


def _conv_taps(x, halo, w):
    acc = x * w[CONV - 1:CONV, :]
    for j in range(CONV - 1):
        acc = acc + _shift_down(x, halo, CONV - 1 - j) * w[j:j + 1, :]
    return acc


_Q_SCALE = DQK ** -0.5


def _qscale_row():
    lane = lax.broadcasted_iota(jnp.int32, (1, D), 1)
    return jnp.where(lane < MLH * DQK, _Q_SCALE, 1.0).astype(F32)


def _conv_silu_fwd(proj, conv_w):
    t = proj.shape[0]

    def body(x_ref, halo_ref, w_ref, o_ref):
        halo = jnp.where(pl.program_id(0) > 0, halo_ref[...], 0.0)
        c = _conv_taps(x_ref[...], halo, w_ref[...])
        o_ref[...] = (c * _sigmoid(c) * _qscale_row()).astype(BF16)

    return pl.pallas_call(
        body, name="conv_silu_fwd", grid=(t // TOK_TILE,),
        in_specs=[_tok(D, C_QK // D), _halo_prev(D, C_QK // D), _rep((CONV, D))], out_specs=_tok(D),
        out_shape=jax.ShapeDtypeStruct((t, D), BF16), compiler_params=_params(),
    )(proj, proj, conv_w)


def _conv_silu_bwd_a(proj, conv_w, dqk):
    t = proj.shape[0]

    def body(x_ref, halo_ref, w_ref, d_ref, dc_ref, dw_ref):
        @pl.when(pl.program_id(0) == 0)
        def _():
            dw_ref[...] = jnp.zeros_like(dw_ref)

        halo = jnp.where(pl.program_id(0) > 0, halo_ref[...], 0.0)
        x = x_ref[...]
        c = _conv_taps(x, halo, w_ref[...])
        s = _sigmoid(c)
        dc = d_ref[...] * _qscale_row() * (s * (1.0 + c * (1.0 - s)))
        dc_ref[...] = dc
        for j in range(CONV):
            dw_ref[j:j + 1, :] += jnp.sum(dc * _shift_down(x, halo, CONV - 1 - j), axis=0, keepdims=True)

    return pl.pallas_call(
        body, name="conv_silu_bwd_a", grid=(t // TOK_TILE,),
        in_specs=[_tok(D, C_QK // D), _halo_prev(D, C_QK // D), _rep((CONV, D)), _tok(D)],
        out_specs=[_tok(D), _rep((CONV, D))],
        out_shape=[jax.ShapeDtypeStruct((t, D), F32), jax.ShapeDtypeStruct((CONV, D), F32)],
        compiler_params=_params(),
    )(proj, proj, conv_w, dqk)


def _conv_silu_bwd_b(dc, conv_w):
    t = dc.shape[0]
    nt = t // TOK_TILE

    def body(dc_ref, halo_ref, w_ref, dx_ref):
        halo = jnp.where(pl.program_id(0) < nt - 1, halo_ref[...], 0.0)
        dcv = dc_ref[...]
        w = w_ref[...]
        acc = dcv * w[CONV - 1:CONV, :]
        for j in range(CONV - 1):
            acc = acc + _shift_up(dcv, halo, CONV - 1 - j) * w[j:j + 1, :]
        dx_ref[...] = acc.astype(BF16)

    return pl.pallas_call(
        body, name="conv_silu_bwd_b", grid=(nt,), in_specs=[_tok(D), _halo_next(D, nt), _rep((CONV, D))],
        out_specs=_tok(D), out_shape=jax.ShapeDtypeStruct((t, D), BF16), compiler_params=_params(),
    )(dc, dc, conv_w)


def _gates_fwd(pre_rows, bias_col):
    t = pre_rows.shape[1]

    def body(p_ref, b_ref, g_ref, s_ref):
        z = p_ref[...] + b_ref[...]
        lf = jnp.minimum(z, 0.0) - jnp.log(1.0 + jnp.exp(-jnp.abs(z)))
        lane = lax.broadcasted_iota(jnp.int32, z.shape, 1) % CHUNK
        cum = lf
        s = 1
        while s < CHUNK:
            cum = cum + jnp.where(lane >= s, pltpu.roll(cum, s, 1), 0.0)
            s *= 2
        sub = lax.broadcasted_iota(jnp.int32, z.shape, 0)
        g_ref[...] = jnp.where(sub < MLH, z, cum)
        s_ref[...] = _sigmoid(-z)

    return pl.pallas_call(
        body, name="gates_fwd",
        out_shape=[jax.ShapeDtypeStruct((8, t), F32), jax.ShapeDtypeStruct((8, t), F32)],
        compiler_params=_params(),
    )(pre_rows, bias_col)


def _chunk_terms(grow, gcol, h, m0):
    i_row, b_row = grow[h:h + 1, :], grow[MLH + h:MLH + h + 1, :]
    i_col, b_col = gcol[:, h:h + 1], gcol[:, MLH + h:MLH + h + 1]
    b_last = b_row[:, CHUNK - 1:CHUNK]
    tt = lax.broadcasted_iota(jnp.int32, (CHUNK, CHUNK), 0)
    ss = lax.broadcasted_iota(jnp.int32, (CHUNK, CHUNK), 1)
    log_d = jnp.where(tt >= ss, b_col - b_row + i_row, -jnp.inf)
    m_t = jnp.maximum(b_col + m0, jnp.max(log_d, axis=1, keepdims=True))
    dm = jnp.exp(log_d - m_t)
    wi = jnp.exp(b_col + m0 - m_t)
    m1 = jnp.maximum(b_last + m0, jnp.max(b_last - b_row + i_row, axis=1, keepdims=True))
    ws = jnp.exp(b_last - b_col + i_col - m1)
    dec = jnp.exp(b_last + m0 - m1)
    return dm, wi, m_t, ws, dec, m1


def _mlstm_fwd(qk, proj, grow, gcol):
    t = qk.shape[0]
    nc = t // CHUNK

    def body(qk_ref, v_ref, grow_ref, gcol_ref, h_ref, cs_ref, st_ref, c_scr, st_scr):
        @pl.when(pl.program_id(0) == 0)
        def _():
            c_scr[...] = jnp.zeros_like(c_scr)
            st_scr[...] = jnp.zeros_like(st_scr)

        grow_v, gcol_v = grow_ref[...], gcol_ref[...]
        for h in range(MLH):
            q = qk_ref[:, h * DQK:(h + 1) * DQK]
            k = qk_ref[:, MLH * DQK + h * DQK:MLH * DQK + (h + 1) * DQK]
            v = v_ref[:, h * DV:(h + 1) * DV]
            c0 = c_scr[h]
            n0 = st_scr[h, 0:1, :]
            m0 = st_scr[h, 1:2, 0:1]
            cs_ref[0, h] = c0
            st_ref[0, h] = st_scr[h]
            dm, wi, m_t, ws, dec, m1 = _chunk_terms(grow_v, gcol_v, h, m0)
            s = _dot_nt(q, k) * dm
            num = wi * _dot_nt(q, c0.astype(BF16)) + _dot_nn(s.astype(BF16), v.astype(BF16))
            den = wi * jnp.sum(q.astype(F32) * n0, axis=1, keepdims=True) + jnp.sum(s, axis=1, keepdims=True)
            h_ref[:, h * DV:(h + 1) * DV] = num / jnp.maximum(jnp.abs(den), jnp.exp(-m_t))
            c_scr[h] = dec * c0 + _dot_tn((ws * v).astype(BF16), k)
            st_scr[h, 0:1, :] = dec * n0 + jnp.sum(ws * k.astype(F32), axis=0, keepdims=True)
            st_scr[h, 1:2, :] = jnp.broadcast_to(m1, (1, DQK))

    return pl.pallas_call(
        body, name="mlstm_fwd", grid=(nc,),
        in_specs=[pl.BlockSpec((CHUNK, D), lambda c: (c, 0)), pl.BlockSpec((CHUNK, D), lambda c: (c, C_V // D)),
                  pl.BlockSpec((8, CHUNK), lambda c: (0, c)), pl.BlockSpec((CHUNK, 8), lambda c: (c, 0))],
        out_specs=[pl.BlockSpec((CHUNK, D), lambda c: (c, 0)),
                   pl.BlockSpec((1, MLH, DV, DQK), lambda c: (c, 0, 0, 0)),
                   pl.BlockSpec((1, MLH, 8, DQK), lambda c: (c, 0, 0, 0))],
        out_shape=[jax.ShapeDtypeStruct((t, D), F32), jax.ShapeDtypeStruct((nc, MLH, DV, DQK), F32),
                   jax.ShapeDtypeStruct((nc, MLH, 8, DQK), F32)],
        scratch_shapes=[pltpu.VMEM((MLH, DV, DQK), F32), pltpu.VMEM((MLH, 8, DQK), F32)],
        compiler_params=_params(dimension_semantics=("arbitrary",)),
    )(qk, proj, grow, gcol)


def _mlstm_bwd(qk, proj, grow, gcol, sneg_col, cs, st, hraw, dh):
    t = qk.shape[0]
    nc = t // CHUNK

    def rev(c):
        return nc - 1 - c

    def nxt(c):
        return jnp.minimum(nc - c, nc - 1)

    def body(qk_ref, v_ref, grow_ref, gcol_ref, sneg_ref, cs_ref, st_ref, cs1_ref, st1_ref, h_ref, dh_ref,
             dqk_ref, dv_ref, dif_ref, dbif_ref, dc_scr, dn_scr):
        @pl.when(pl.program_id(0) == 0)
        def _():
            dc_scr[...] = jnp.zeros_like(dc_scr)
            dn_scr[...] = jnp.zeros_like(dn_scr)
            dbif_ref[...] = jnp.zeros_like(dbif_ref)

        grow_v, gcol_v, sneg = grow_ref[...], gcol_ref[...], sneg_ref[...]
        tt = lax.broadcasted_iota(jnp.int32, (CHUNK, CHUNK), 0)
        ss = lax.broadcasted_iota(jnp.int32, (CHUNK, CHUNK), 1)
        lane8 = lax.broadcasted_iota(jnp.int32, (CHUNK, 8), 1)
        dif = jnp.zeros((CHUNK, 8), F32)
        for h in range(MLH):
            q = qk_ref[:, h * DQK:(h + 1) * DQK]
            k = qk_ref[:, MLH * DQK + h * DQK:MLH * DQK + (h + 1) * DQK]
            qf, kf = q.astype(F32), k.astype(F32)
            v = v_ref[:, h * DV:(h + 1) * DV]
            vb = v.astype(BF16)
            c0 = cs_ref[0, h]
            n0 = st_ref[0, h, 0:1, :]
            m0 = st_ref[0, h, 1:2, 0:1]
            dc1 = dc_scr[h]
            dn1 = dn_scr[h, 0:1, :]
            dm, wi, m_t, ws, dec, _ = _chunk_terms(grow_v, gcol_v, h, m0)
            s = _dot_nt(q, k) * dm
            den = wi * jnp.sum(qf * n0, axis=1, keepdims=True) + jnp.sum(s, axis=1, keepdims=True)
            floor = jnp.exp(-m_t)
            g = jnp.maximum(jnp.abs(den), floor)
            dh_v = dh_ref[:, h * DV:(h + 1) * DV]
            dnum = dh_v / g
            dden = -jnp.sum(dh_v * h_ref[:, h * DV:(h + 1) * DV], axis=1, keepdims=True) / g
            dden = jnp.where(jnp.abs(den) > floor, dden * jnp.sign(den), 0.0)
            dnum_b = dnum.astype(BF16)
            da = ((_dot_nt(dnum_b, vb) + dden) * dm).astype(BF16)
            dc1_b = dc1.astype(BF16)
            dq = _dot_nn(da, k) + wi * (_dot_nn(dnum_b, c0.astype(BF16)) + dden * n0)
            dk = _dot_tn(da, q) + ws * (_dot_nn(vb, dc1_b) + dn1)
            dv = _dot_tn(s.astype(BF16), dnum_b) + ws * _dot_nt(k, dc1_b)
            dqk_ref[:, h * DQK:(h + 1) * DQK] = dq
            dqk_ref[:, MLH * DQK + h * DQK:MLH * DQK + (h + 1) * DQK] = dk
            dv_ref[:, h * DV:(h + 1) * DV] = dv.astype(BF16)
            rk = jnp.sum(kf * dk, axis=1, keepdims=True)
            df = jnp.sum(qf * dq, axis=1, keepdims=True) - rk
            df_row = jnp.sum(jnp.where(tt == ss, df, 0.0), axis=0, keepdims=True)
            suffix = jnp.sum(jnp.where(ss >= tt, df_row, 0.0), axis=1, keepdims=True)
            cross = (jnp.sum(jnp.sum(dc1 * cs1_ref[0, h], axis=1, keepdims=True), axis=0, keepdims=True)
                     + jnp.sum(dn1 * st1_ref[0, h, 0:1, :], axis=1, keepdims=True))
            dpf = (suffix + cross) * sneg[:, MLH + h:MLH + h + 1]
            dif = dif + jnp.where(lane8 == h, rk, 0.0) + jnp.where(lane8 == MLH + h, dpf, 0.0)
            dc_scr[h] = dec * dc1 + _dot_tn((wi * dnum).astype(BF16), q)
            dn_scr[h, 0:1, :] = dec * dn1 + jnp.sum(wi * dden * qf, axis=0, keepdims=True)
        dif_ref[...] = dif
        dbif_ref[...] += jnp.sum(dif, axis=0, keepdims=True)

    return pl.pallas_call(
        body, name="mlstm_bwd", grid=(nc,),
        in_specs=[pl.BlockSpec((CHUNK, D), lambda c: (rev(c), 0)),
                  pl.BlockSpec((CHUNK, D), lambda c: (rev(c), C_V // D)),
                  pl.BlockSpec((8, CHUNK), lambda c: (0, rev(c))),
                  pl.BlockSpec((CHUNK, 8), lambda c: (rev(c), 0)),
                  pl.BlockSpec((CHUNK, 8), lambda c: (rev(c), 0)),
                  pl.BlockSpec((1, MLH, DV, DQK), lambda c: (rev(c), 0, 0, 0)),
                  pl.BlockSpec((1, MLH, 8, DQK), lambda c: (rev(c), 0, 0, 0)),
                  pl.BlockSpec((1, MLH, DV, DQK), lambda c: (nxt(c), 0, 0, 0)),
                  pl.BlockSpec((1, MLH, 8, DQK), lambda c: (nxt(c), 0, 0, 0)),
                  pl.BlockSpec((CHUNK, D), lambda c: (rev(c), 0)),
                  pl.BlockSpec((CHUNK, D), lambda c: (rev(c), 0))],
        out_specs=[pl.BlockSpec((CHUNK, D), lambda c: (rev(c), 0)),
                   pl.BlockSpec((CHUNK, D), lambda c: (rev(c), 0)),
                   pl.BlockSpec((CHUNK, 8), lambda c: (rev(c), 0)),
                   pl.BlockSpec((1, 8), lambda c: (0, 0))],
        out_shape=[jax.ShapeDtypeStruct((t, D), F32), jax.ShapeDtypeStruct((t, D), BF16),
                   jax.ShapeDtypeStruct((t, 8), F32), jax.ShapeDtypeStruct((1, 8), F32)],
        scratch_shapes=[pltpu.VMEM((MLH, DV, DQK), F32), pltpu.VMEM((MLH, 8, DQK), F32)],
        compiler_params=_params(dimension_semantics=("arbitrary",)),
    )(qk, proj, grow, gcol, sneg_col, cs, st, cs, st, hraw, dh)


def _ya_fwd(hraw, proj, g):
    t = hraw.shape[0]

    def body(h_ref, o_ref, g_ref, y_ref):
        so = _sigmoid(o_ref[...])
        for h in range(MLH):
            sl = slice(h * DV, (h + 1) * DV)
            xn, _ = _rms(h_ref[:, sl])
            y_ref[:, sl] = (so[:, sl] * xn * g_ref[:, sl]).astype(BF16)

    return pl.pallas_call(
        body, name="ya_fwd", grid=(t // TOK_TILE,), in_specs=[_tok(D), _tok(D, C_O // D), _rep((1, D))],
        out_specs=_tok(D), out_shape=jax.ShapeDtypeStruct((t, D), BF16), compiler_params=_params(),
    )(hraw, proj, g)


def _ya_bwd(hraw, proj, g, dya):
    t = hraw.shape[0]

    def body(h_ref, o_ref, g_ref, dy_ref, dh_ref, do_ref, dg_ref):
        @pl.when(pl.program_id(0) == 0)
        def _():
            dg_ref[...] = jnp.zeros_like(dg_ref)

        so = _sigmoid(o_ref[...])
        dy = dy_ref[...]
        for h in range(MLH):
            sl = slice(h * DV, (h + 1) * DV)
            xn, rstd = _rms(h_ref[:, sl])
            gs = g_ref[:, sl]
            do_ref[:, sl] = (dy[:, sl] * xn * gs * so[:, sl] * (1.0 - so[:, sl])).astype(BF16)
            dhn = dy[:, sl] * so[:, sl]
            dg_ref[:, sl] += jnp.sum(dhn * xn, axis=0, keepdims=True)
            dh_ref[:, sl] = _rms_bwd(xn, rstd, dhn * gs)

    return pl.pallas_call(
        body, name="ya_bwd", grid=(t // TOK_TILE,),
        in_specs=[_tok(D), _tok(D, C_O // D), _rep((1, D)), _tok(D)],
        out_specs=[_tok(D), _tok(D), _rep((1, D))],
        out_shape=[jax.ShapeDtypeStruct((t, D), F32), jax.ShapeDtypeStruct((t, D), BF16),
                   jax.ShapeDtypeStruct((1, D), F32)],
        compiler_params=_params(),
    )(hraw, proj, g, dya)


_SW_SCALE = HD ** -0.5
_KVB = C_KV // (2 * SWKV * HD)


def _swa_mask(n):
    qi = lax.broadcasted_iota(jnp.int32, (WIN, 2 * WIN), 0)
    ki = lax.broadcasted_iota(jnp.int32, (WIN, 2 * WIN), 1)
    return (ki > qi) & (ki <= qi + WIN) & ((n > 0) | (ki >= WIN))


def _swa_fwd(proj, sinks):
    t = proj.shape[0]
    nb = t // WIN

    def body(q_ref, kvc_ref, kvp_ref, sink_ref, y_ref, lse_ref):
        valid = _swa_mask(pl.program_id(0))
        for hk in range(SWKV):
            ks = slice(hk * HD, (hk + 1) * HD)
            vs = slice(SWKV * HD + hk * HD, SWKV * HD + (hk + 1) * HD)
            kb = jnp.concatenate([kvp_ref[:, ks], kvc_ref[:, ks]], axis=0).astype(BF16)
            vb = jnp.concatenate([kvp_ref[:, vs], kvc_ref[:, vs]], axis=0).astype(BF16)
            for g in range(SWG):
                hq = hk * SWG + g
                qs = slice(hq * HD, (hq + 1) * HD)
                logits = jnp.where(valid, _dot_nt(q_ref[:, qs].astype(BF16), kb) * _SW_SCALE, -jnp.inf)
                sink = sink_ref[:, hq:hq + 1]
                m = jnp.maximum(jnp.max(logits, axis=1, keepdims=True), sink)
                p = jnp.exp(logits - m)
                denom = jnp.sum(p, axis=1, keepdims=True) + jnp.exp(sink - m)
                y_ref[:, qs] = _dot_nn((p / denom).astype(BF16), vb).astype(BF16)
                lse_ref[:, hq:hq + 1] = m + jnp.log(denom)

    return pl.pallas_call(
        body, name="swa_fwd", grid=(nb,),
        in_specs=[pl.BlockSpec((WIN, D), lambda n: (n, C_QSW // D)),
                  pl.BlockSpec((WIN, 512), lambda n: (n, _KVB)),
                  pl.BlockSpec((WIN, 512), lambda n: (jnp.maximum(n - 1, 0), _KVB)),
                  pl.BlockSpec((1, SWH), lambda n: (0, 0))],
        out_specs=[pl.BlockSpec((WIN, D), lambda n: (n, 0)), pl.BlockSpec((WIN, SWH), lambda n: (n, 0))],
        out_shape=[jax.ShapeDtypeStruct((t, D), BF16), jax.ShapeDtypeStruct((t, SWH), F32)],
        compiler_params=_params(),
    )(proj, proj, proj, sinks)


def _swa_bwd(proj, sinks, lse, yb, dyb):
    t = proj.shape[0]
    nb = t // WIN

    def body(q_ref, kvc_ref, kvp_ref, sink_ref, lse_ref, y_ref, dy_ref, dq_ref, dself_ref, dprev_ref, ds_ref):
        @pl.when(pl.program_id(0) == 0)
        def _():
            ds_ref[...] = jnp.zeros_like(ds_ref)

        valid = _swa_mask(pl.program_id(0))
        for hk in range(SWKV):
            ks = slice(hk * HD, (hk + 1) * HD)
            vs = slice(SWKV * HD + hk * HD, SWKV * HD + (hk + 1) * HD)
            kb = jnp.concatenate([kvp_ref[:, ks], kvc_ref[:, ks]], axis=0).astype(BF16)
            vb = jnp.concatenate([kvp_ref[:, vs], kvc_ref[:, vs]], axis=0).astype(BF16)
            dkb = jnp.zeros((2 * WIN, HD), F32)
            dvb = jnp.zeros((2 * WIN, HD), F32)
            for g in range(SWG):
                hq = hk * SWG + g
                qs = slice(hq * HD, (hq + 1) * HD)
                qb = q_ref[:, qs].astype(BF16)
                lse_c = lse_ref[:, hq:hq + 1]
                logits = jnp.where(valid, _dot_nt(qb, kb) * _SW_SCALE, -jnp.inf)
                p = jnp.exp(logits - lse_c)
                dy = dy_ref[:, qs]
                dyb_ = dy.astype(BF16)
                delta = jnp.sum(dy * y_ref[:, qs].astype(F32), axis=1, keepdims=True)
                dsm = (p * (_dot_nt(dyb_, vb) - delta)).astype(BF16)
                dq_ref[:, qs] = (_dot_nn(dsm, kb) * _SW_SCALE).astype(BF16)
                dkb = dkb + _dot_tn(dsm, qb) * _SW_SCALE
                dvb = dvb + _dot_tn(p.astype(BF16), dyb_)
                ds_ref[:, hq:hq + 1] += -jnp.sum(jnp.exp(sink_ref[:, hq:hq + 1] - lse_c) * delta, axis=0,
                                                 keepdims=True)
            dprev_ref[:, ks] = dkb[:WIN]
            dself_ref[:, ks] = dkb[WIN:]
            dprev_ref[:, vs] = dvb[:WIN]
            dself_ref[:, vs] = dvb[WIN:]

    return pl.pallas_call(
        body, name="swa_bwd", grid=(nb,),
        in_specs=[pl.BlockSpec((WIN, D), lambda n: (n, C_QSW // D)),
                  pl.BlockSpec((WIN, 512), lambda n: (n, _KVB)),
                  pl.BlockSpec((WIN, 512), lambda n: (jnp.maximum(n - 1, 0), _KVB)),
                  pl.BlockSpec((1, SWH), lambda n: (0, 0)),
                  pl.BlockSpec((WIN, SWH), lambda n: (n, 0)),
                  pl.BlockSpec((WIN, D), lambda n: (n, 0)),
                  pl.BlockSpec((WIN, D), lambda n: (n, 0))],
        out_specs=[pl.BlockSpec((WIN, D), lambda n: (n, 0)), pl.BlockSpec((WIN, 512), lambda n: (n, 0)),
                   pl.BlockSpec((WIN, 512), lambda n: (n, 0)), pl.BlockSpec((1, SWH), lambda n: (0, 0))],
        out_shape=[jax.ShapeDtypeStruct((t, D), BF16), jax.ShapeDtypeStruct((t, 512), F32),
                   jax.ShapeDtypeStruct((t, 512), F32), jax.ShapeDtypeStruct((1, SWH), F32)],
        compiler_params=_params(),
    )(proj, proj, proj, sinks, lse, yb, dyb)


def _kv_combine(dself, dprev):
    t = dself.shape[0]
    nb = t // WIN

    def body(a_ref, b_ref, o_ref):
        nxt = jnp.where(pl.program_id(0) < nb - 1, b_ref[...], 0.0)
        o_ref[...] = (a_ref[...] + nxt).astype(BF16)

    return pl.pallas_call(
        body, name="kv_combine", grid=(nb,),
        in_specs=[pl.BlockSpec((WIN, 512), lambda n: (n, 0)),
                  pl.BlockSpec((WIN, 512), lambda n: (jnp.minimum(n + 1, nb - 1), 0))],
        out_specs=pl.BlockSpec((WIN, 512), lambda n: (n, 0)),
        out_shape=jax.ShapeDtypeStruct((t, 512), BF16), compiler_params=_params(),
    )(dself, dprev)


def _merge_fwd(proj, za, zb):
    t = proj.shape[0]

    def body(ga_ref, gb_ref, za_ref, zb_ref, o_ref):
        o_ref[...] = (_sigmoid(ga_ref[...]) * za_ref[...] + _sigmoid(gb_ref[...]) * zb_ref[...]).astype(BF16)

    return pl.pallas_call(
        body, name="merge_fwd", grid=(t // TOK_TILE,),
        in_specs=[_tok(D, C_GA // D), _tok(D, C_GB // D), _tok(D), _tok(D)], out_specs=_tok(D),
        out_shape=jax.ShapeDtypeStruct((t, D), BF16), compiler_params=_params(),
    )(proj, proj, za, zb)


def _merge_bwd(proj, za, zb, dmerged):
    t = proj.shape[0]

    def body(ga_ref, gb_ref, za_ref, zb_ref, dm_ref, dza_ref, dzb_ref, dga_ref, dgb_ref):
        dm = dm_ref[...]
        sa, sb = _sigmoid(ga_ref[...]), _sigmoid(gb_ref[...])
        dza_ref[...] = (dm * sa).astype(BF16)
        dzb_ref[...] = (dm * sb).astype(BF16)
        dga_ref[...] = (dm * za_ref[...] * sa * (1.0 - sa)).astype(BF16)
        dgb_ref[...] = (dm * zb_ref[...] * sb * (1.0 - sb)).astype(BF16)

    return pl.pallas_call(
        body, name="merge_bwd", grid=(t // TOK_TILE,),
        in_specs=[_tok(D, C_GA // D), _tok(D, C_GB // D), _tok(D), _tok(D), _tok(D)],
        out_specs=[_tok(D)] * 4, out_shape=[jax.ShapeDtypeStruct((t, D), BF16)] * 4, compiler_params=_params(),
    )(proj, proj, za, zb, dmerged)


def _act_fwd(u):
    t = u.shape[0]

    def body(u_ref, a_ref):
        r = jnp.maximum(u_ref[...], 0.0)
        a_ref[...] = (r * r).astype(BF16)

    return pl.pallas_call(
        body, name="act_fwd", grid=(t // TOK_TILE,), in_specs=[_tok(DFF)], out_specs=_tok(DFF),
        out_shape=jax.ShapeDtypeStruct((t, DFF), BF16), compiler_params=_params(),
    )(u)


def _act_bwd(u, da):
    t = u.shape[0]

    def body(u_ref, da_ref, du_ref):
        du_ref[...] = (da_ref[...] * 2.0 * jnp.maximum(u_ref[...], 0.0)).astype(BF16)

    return pl.pallas_call(
        body, name="act_bwd", grid=(t // TOK_TILE,), in_specs=[_tok(DFF), _tok(DFF)], out_specs=_tok(DFF),
        out_shape=jax.ShapeDtypeStruct((t, DFF), BF16), compiler_params=_params(),
    )(u, da)


def _ple_final(x2, gpre, pp, target, gf):
    t = x2.shape[0]

    def body(x_ref, gp_ref, pp_ref, t_ref, g_ref, loss_ref, dg_ref, dx_ref, dpp_ref, dgp_ref):
        @pl.when(pl.program_id(0) == 0)
        def _():
            loss_ref[...] = jnp.zeros_like(loss_ref)
            dg_ref[...] = jnp.zeros_like(dg_ref)

        gate = _sigmoid(gp_ref[...])
        pp_v = pp_ref[...]
        x3 = x_ref[...] + gate * pp_v
        xn, rstd = _rms(x3)
        gf_v = g_ref[...]
        err = xn * gf_v - t_ref[...]
        loss_ref[...] += (0.5 / D) * jnp.sum(jnp.sum(err * err, axis=1, keepdims=True), axis=0, keepdims=True)
        dy = err * (1.0 / D)
        dg_ref[...] += jnp.sum(dy * xn, axis=0, keepdims=True)
        dx3 = _rms_bwd(xn, rstd, dy * gf_v)
        dx_ref[...] = dx3
        dpp_ref[...] = (dx3 * gate).astype(BF16)
        dgp_ref[...] = (dx3 * pp_v * gate * (1.0 - gate)).astype(BF16)

    return pl.pallas_call(
        body, name="ple_final", grid=(t // TOK_TILE,),
        in_specs=[_tok(D), _tok(D), _tok(D), _tok(D), _rep((1, D))],
        out_specs=[_rep((1, 1)), _rep((1, D)), _tok(D), _tok(D), _tok(D)],
        out_shape=[jax.ShapeDtypeStruct((1, 1), F32), jax.ShapeDtypeStruct((1, D), F32),
                   jax.ShapeDtypeStruct((t, D), F32), jax.ShapeDtypeStruct((t, D), BF16),
                   jax.ShapeDtypeStruct((t, D), BF16)],
        compiler_params=_params(),
    )(x2, gpre, pp, target, gf)


def _win_pad(w):
    zeros = jnp.zeros((w.shape[0], IFW - 8), w.dtype)
    return jnp.concatenate([w[:, 0:3072], w[:, 3080:4104], w[:, 4616:6664], w[:, 4104:4616], w[:, 3072:3080], zeros],
                           axis=1)


def _win_unpad(wp):
    return jnp.concatenate([wp[:, 0:3072], wp[:, C_IF:C_IF + 8], wp[:, C_QSW:C_QSW + 1024], wp[:, C_KV:C_KV + 512],
                            wp[:, C_GA:C_GA + 2048]], axis=1)


def _local_step(x, p, target, w):
    t = x.shape[0]
    pb = p.astype(BF16)

    h0 = _norm_fwd(x, w["norm_mix_g"], "norm_mix")
    proj = _mm(h0, w["w_in"], "nn", F32, "mm_in")
    qk = _conv_silu_fwd(proj, w["conv_qk"])
    grow, sneg_row = _gates_fwd(proj[:, C_IF:C_IF + 8].T, w["b_if"].reshape(8, 1))
    gcol, sneg_col = grow.T, sneg_row.T
    hraw, cs, st = _mlstm_fwd(qk, proj, grow, gcol)
    ya = _ya_fwd(hraw, proj, w["mlstm_norm_g"])
    yb, lse = _swa_fwd(proj, w["sinks"])
    za = _mm(ya, w["w_branch_a"], "nn", F32, "mm_branch_a")
    zb = _mm(yb, w["w_branch_b"], "nn", F32, "mm_branch_b")
    merged = _merge_fwd(proj, za, zb)
    o1 = _mm(merged, w["w_out"], "nn", F32, "mm_out")
    x1, hn1 = _resid_norm_fwd(x, o1, w["norm_mlp_g"], "resid_norm_mlp")
    u = _mm(hn1, w["w_up"], "nn", F32, "mm_up")
    act = _act_fwd(u)
    o2 = _mm(act, w["w_down"], "nn", F32, "mm_down")
    x2, hn2 = _resid_norm_fwd(x1, o2, w["norm_ple_g"], "resid_norm_ple")
    gpre = _mm(hn2, w["w_ple_gate"], "nn", F32, "mm_ple_gate")
    pp = _mm(pb, w["w_ple_proj"], "nn", F32, "mm_ple_proj")
    loss, d_final_g, dx3, dpp, dgpre = _ple_final(x2, gpre, pp, target, w["final_norm_g"])

    g = {"final_norm_g": d_final_g}
    g["w_ple_proj"] = _mm(pb, dpp, "tn", F32, "mm_d_ple_proj")
    g["w_ple_gate"] = _mm(hn2, dgpre, "tn", F32, "mm_d_ple_gate")
    dhn2 = _mm(dgpre, w["w_ple_gate"], "nt", F32, "mm_dhn2")
    dx2, dx2b, g["norm_ple_g"] = _norm_bwd_resid(x2, w["norm_ple_g"], dhn2, dx3, "norm_bwd_ple")
    g["w_down"] = _mm(act, dx2b, "tn", F32, "mm_d_down")
    da = _mm(dx2b, w["w_down"], "nt", F32, "mm_da")
    du = _act_bwd(u, da)
    g["w_up"] = _mm(hn1, du, "tn", F32, "mm_d_up")
    dhn1 = _mm(du, w["w_up"], "nt", F32, "mm_dhn1")
    dx1, dx1b, g["norm_mlp_g"] = _norm_bwd_resid(x1, w["norm_mlp_g"], dhn1, dx2, "norm_bwd_mlp")
    g["w_out"] = _mm(merged, dx1b, "tn", F32, "mm_d_out")
    dmerged = _mm(dx1b, w["w_out"], "nt", F32, "mm_dmerged")
    dza, dzb, dga, dgb = _merge_bwd(proj, za, zb, dmerged)
    g["w_branch_a"] = _mm(ya, dza, "tn", F32, "mm_d_branch_a")
    g["w_branch_b"] = _mm(yb, dzb, "tn", F32, "mm_d_branch_b")
    dya = _mm(dza, w["w_branch_a"], "nt", F32, "mm_dya")
    dyb = _mm(dzb, w["w_branch_b"], "nt", F32, "mm_dyb")
    dhraw, do, g["mlstm_norm_g"] = _ya_bwd(hraw, proj, w["mlstm_norm_g"], dya)
    dqk, dv, dif, g["b_if"] = _mlstm_bwd(qk, proj, grow, gcol, sneg_col, cs, st, hraw, dhraw)
    dc, g["conv_qk"] = _conv_silu_bwd_a(proj, w["conv_qk"], dqk)
    dqk_pre = _conv_silu_bwd_b(dc, w["conv_qk"])
    dq_sw, dkv_self, dkv_prev, g["sinks"] = _swa_bwd(proj, w["sinks"], lse, yb, dyb)
    dkv = _kv_combine(dkv_self, dkv_prev)
    dif_pad = jnp.pad(dif.astype(BF16), ((0, 0), (0, IFW - 8)))
    dproj = jnp.concatenate([dqk_pre, dv, do, dq_sw, dga, dgb, dkv, dif_pad], axis=1)
    g["w_in"] = _mm(h0, dproj, "tn", F32, "mm_d_in")
    dh0 = _mm(dproj, w["w_in"], "nt", F32, "mm_dh0")
    grad_x, _, g["norm_mix_g"] = _norm_bwd_resid(x, w["norm_mix_g"], dh0, dx1, "norm_bwd_mix")
    return loss, grad_x, g


_SHARDED = (("w_in", (D, N_IN // 4)), ("w_branch_a", (D // 4, D)), ("w_branch_b", (D // 4, D)), ("w_out", (D // 4, D)),
            ("w_up", (D, DFF // 4)), ("w_down", (DFF // 4, D)), ("w_ple_gate", (D // 4, D)),
            ("w_ple_proj", (PLE, D // 4)), ("conv_qk", (CONV, 2 * MLH * DQK // 4)))
_HALF_BLOCK = 1024
_HALF_ROWS = 19 * _HALF_BLOCK
PACK_ROWS = 2 * _HALF_ROWS
_SMALL_ROWS = 8


def _seg_rows(shape):
    n = shape[0] * shape[1]
    return -(-n // (16 * 128)) * 16


assert sum(_seg_rows(s) for _, s in _SHARDED) <= PACK_ROWS


def _pack(shards, dtype):
    parts, rows = [], 0
    for name, shape in _SHARDED:
        flat = shards[name].astype(dtype).reshape(-1)
        n = _seg_rows(shape)
        flat = jnp.pad(flat, (0, n * 128 - flat.shape[0]))
        parts.append(flat.reshape(n, 128))
        rows += n
    parts.append(jnp.zeros((PACK_ROWS - rows, 128), dtype))
    return jnp.concatenate(parts, axis=0)


def _unpack(slab):
    out, r = {}, 0
    lead = slab.shape[:-2]
    for name, shape in _SHARDED:
        n = _seg_rows(shape)
        seg = slab[..., r:r + n, :].reshape(lead + (n * 128,))[..., :shape[0] * shape[1]]
        out[name] = seg.reshape(lead + shape)
        r += n
    return out


def _full_from_chips(name, a):
    if name in ("w_in", "w_up", "w_ple_proj", "conv_qk"):
        return jnp.swapaxes(a, 0, 1).reshape(a.shape[1], 4 * a.shape[2])
    return a.reshape(4 * a.shape[1], a.shape[2])


def _chips_from_full(name, a):
    if name in ("w_in", "w_up", "w_ple_proj", "conv_qk"):
        return jnp.swapaxes(a.reshape(a.shape[0], 4, a.shape[1] // 4), 0, 1)
    return a.reshape(4, a.shape[0] // 4, a.shape[1])


_SMALL = (("norm_mix_g", D), ("mlstm_norm_g", D), ("norm_mlp_g", D), ("norm_ple_g", D), ("final_norm_g", D))


def _pack_small(vals, extra=None):
    rows = [vals[n].reshape(1, D) for n, _ in _SMALL]
    tail = [vals["b_if"].reshape(1, 8), vals["sinks"].reshape(1, SWH)]
    used = 8 + SWH
    if extra is not None:
        tail.append(extra.reshape(1, 1))
        used += 1
    tail.append(jnp.zeros((1, D - used), F32))
    rows.append(jnp.concatenate(tail, axis=1))
    rows.append(jnp.zeros((_SMALL_ROWS - len(rows), D), F32))
    return jnp.concatenate(rows, axis=0)


def _unpack_small(slab, shapes):
    out = {n: slab[i].reshape(shapes[n]) for i, (n, _) in enumerate(_SMALL)}
    out["b_if"] = slab[5, 0:8].reshape(shapes["b_if"])
    out["sinks"] = slab[5, 8:8 + SWH].reshape(shapes["sinks"])
    return out


_MESH = pl.DeviceIdType.MESH
_HBM = pl.BlockSpec(memory_space=pltpu.HBM)
_VMEM = pl.BlockSpec(memory_space=pltpu.VMEM)


def _place():
    x, y, c = lax.axis_index("x"), lax.axis_index("y"), lax.axis_index("c")
    return x, y, c, 2 * x + y


def _chip_peer(x, y, r):
    return (x ^ (r >> 1), y ^ (r & 1))


def _allgather_weights(slab, conv):
    def body(slab_ref, conv_ref, out_ref, conv_out_ref, send_a, recv_a, send_b, recv_b, send_c, recv_c, local_sems):
        x, y, c, j = _place()
        sibling = (x, y, 1 - c)
        half = pl.ds(c * _HALF_ROWS, _HALF_ROWS)
        other = pl.ds((1 - c) * _HALF_ROWS, _HALF_ROWS)
        mine = pltpu.make_async_copy(slab_ref, out_ref.at[j], local_sems.at[0])
        mine_c = pltpu.make_async_copy(conv_ref, conv_out_ref.at[j], local_sems.at[1])
        mine.start()
        mine_c.start()

        def copy_a(r, chip):
            return pltpu.make_async_remote_copy(
                src_ref=slab_ref.at[half], dst_ref=out_ref.at[chip, half], send_sem=send_a.at[r - 1],
                recv_sem=recv_a.at[r - 1], device_id=(*_chip_peer(x, y, r), c), device_id_type=_MESH)

        def copy_b(r, chip, rows):
            return pltpu.make_async_remote_copy(
                src_ref=out_ref.at[chip, rows], dst_ref=out_ref.at[chip, rows], send_sem=send_b.at[r - 1],
                recv_sem=recv_b.at[r - 1], device_id=sibling, device_id_type=_MESH)

        def copy_c(r, chip):
            return pltpu.make_async_remote_copy(
                src_ref=conv_ref, dst_ref=conv_out_ref.at[chip], send_sem=send_c.at[r - 1],
                recv_sem=recv_c.at[r - 1], device_id=(*_chip_peer(x, y, r), c), device_id_type=_MESH)

        for r in (1, 2, 3):
            copy_a(r, j).start()
            copy_c(r, j).start()
        for r in (1, 2, 3):
            copy_a(r, j ^ r).wait_recv()
            copy_b(r, j ^ r, half).start()
        for r in (1, 2, 3):
            copy_b(r, j ^ r, other).wait_recv()
            copy_c(r, j ^ r).wait_recv()
        for r in (1, 2, 3):
            copy_a(r, j).wait_send()
            copy_b(r, j ^ r, half).wait_send()
            copy_c(r, j).wait_send()
        mine.wait()
        mine_c.wait()

    return pl.pallas_call(
        body, name="allgather_weights",
        out_shape=[jax.ShapeDtypeStruct((4, PACK_ROWS, 128), BF16), jax.ShapeDtypeStruct((4,) + conv.shape, F32)],
        in_specs=[_HBM, _HBM], out_specs=[_HBM, _HBM],
        scratch_shapes=[pltpu.SemaphoreType.DMA((3,))] * 6 + [pltpu.SemaphoreType.DMA((2,))],
    )(slab, conv)


def _pair_exchange(g):
    def body(g_ref, out_ref, send_sem, recv_sem):
        x, y, c, _ = _place()
        theirs = pl.ds((1 - c) * _HALF_ROWS, _HALF_ROWS)
        cp = pltpu.make_async_remote_copy(
            src_ref=g_ref.at[:, theirs], dst_ref=out_ref, send_sem=send_sem, recv_sem=recv_sem,
            device_id=(x, y, 1 - c), device_id_type=_MESH)
        cp.start()
        cp.wait()

    return pl.pallas_call(
        body, name="pair_exchange", out_shape=jax.ShapeDtypeStruct((4, _HALF_ROWS, 128), F32),
        in_specs=[_HBM], out_specs=_HBM, scratch_shapes=[pltpu.SemaphoreType.DMA, pltpu.SemaphoreType.DMA],
    )(g)


def _pair_sum(g, theirs, c):
    nblk = _HALF_ROWS // _HALF_BLOCK

    def body(c_ref, a_ref, b_ref, o_ref):
        o_ref[...] = a_ref[...] + b_ref[...]

    return pl.pallas_call(
        body, name="pair_sum",
        grid_spec=pltpu.PrefetchScalarGridSpec(
            num_scalar_prefetch=1, grid=(4, nblk),
            in_specs=[pl.BlockSpec((1, _HALF_BLOCK, 128), lambda j, i, c_ref: (j, c_ref[0] * nblk + i, 0)),
                      pl.BlockSpec((1, _HALF_BLOCK, 128), lambda j, i, c_ref: (j, i, 0))],
            out_specs=pl.BlockSpec((1, _HALF_BLOCK, 128), lambda j, i, c_ref: (j, i, 0))),
        out_shape=jax.ShapeDtypeStruct((4, _HALF_ROWS, 128), F32), compiler_params=_params(),
    )(c.reshape(1).astype(jnp.int32), g, theirs)


def _chip_exchange(s):
    def body(s_ref, out_ref, send_sems, recv_sems, local_sem):
        x, y, c, j = _place()
        mine = pltpu.make_async_copy(s_ref.at[j], out_ref.at[0], local_sem)
        mine.start()

        def copy(r):
            return pltpu.make_async_remote_copy(
                src_ref=s_ref.at[j ^ r], dst_ref=out_ref.at[r], send_sem=send_sems.at[r - 1],
                recv_sem=recv_sems.at[r - 1], device_id=(*_chip_peer(x, y, r), c), device_id_type=_MESH)

        for r in (1, 2, 3):
            copy(r).start()
        for r in (1, 2, 3):
            copy(r).wait()
        mine.wait()

    return pl.pallas_call(
        body, name="chip_exchange", out_shape=jax.ShapeDtypeStruct((4, _HALF_ROWS, 128), F32),
        in_specs=[_HBM], out_specs=_HBM,
        scratch_shapes=[pltpu.SemaphoreType.DMA((3,)), pltpu.SemaphoreType.DMA((3,)), pltpu.SemaphoreType.DMA],
    )(s)


def _adamw(w, g, m, v):
    m1 = ADAM_B1 * m + (1.0 - ADAM_B1) * g
    v1 = ADAM_B2 * v + (1.0 - ADAM_B2) * (g * g)
    m_hat = m1 / (1.0 - ADAM_B1 ** ADAM_STEP)
    v_hat = v1 / (1.0 - ADAM_B2 ** ADAM_STEP)
    delta = -ADAM_LR * (m_hat / (jnp.sqrt(v_hat) + ADAM_EPS) + ADAM_WD * w)
    return delta, m1, v1


def _reduce_adamw(parts, w, m, v, j, c):
    nblk = _HALF_ROWS // _HALF_BLOCK
    idx = jnp.stack([j ^ 0, j ^ 1, j ^ 2, j ^ 3, c]).astype(jnp.int32)

    def body(idx_ref, p0, p1, p2, p3, w_ref, m_ref, v_ref, g_out, d_out, m_out, v_out):
        g = ((p0[0] + p1[0]) + p2[0]) + p3[0]
        delta, m1, v1 = _adamw(w_ref[...], g, m_ref[...], v_ref[...])
        g_out[...] = g
        d_out[...] = delta
        m_out[...] = m1
        v_out[...] = v1

    def part(k):
        return pl.BlockSpec((1, _HALF_BLOCK, 128), lambda i, idx_ref: (idx_ref[k], i, 0))

    half = pl.BlockSpec((_HALF_BLOCK, 128), lambda i, idx_ref: (idx_ref[4] * nblk + i, 0))
    out = pl.BlockSpec((_HALF_BLOCK, 128), lambda i, idx_ref: (i, 0))
    return pl.pallas_call(
        body, name="reduce_adamw",
        grid_spec=pltpu.PrefetchScalarGridSpec(
            num_scalar_prefetch=1, grid=(nblk,), in_specs=[part(0), part(1), part(2), part(3), half, half, half],
            out_specs=[out] * 4),
        out_shape=[jax.ShapeDtypeStruct((_HALF_ROWS, 128), F32)] * 4, compiler_params=_params(),
    )(idx, parts, parts, parts, parts, w, m, v)


def _sibling_share(halves):
    n = len(halves)

    def body(*refs):
        ins, outs = refs[:n], refs[n:2 * n]
        send_sems, recv_sems, local_sems = refs[2 * n:]
        x, y, c, _ = _place()
        rows = pl.ds(c * _HALF_ROWS, _HALF_ROWS)
        local = [pltpu.make_async_copy(ins[k], outs[k].at[rows], local_sems.at[k]) for k in range(n)]
        remote = [pltpu.make_async_remote_copy(
            src_ref=ins[k], dst_ref=outs[k].at[rows], send_sem=send_sems.at[k], recv_sem=recv_sems.at[k],
            device_id=(x, y, 1 - c), device_id_type=_MESH) for k in range(n)]
        for cp in local + remote:
            cp.start()
        for cp in remote + local:
            cp.wait()

    return pl.pallas_call(
        body, name="sibling_share", out_shape=[jax.ShapeDtypeStruct((PACK_ROWS, 128), F32)] * n,
        in_specs=[_HBM] * n, out_specs=[_HBM] * n,
        scratch_shapes=[pltpu.SemaphoreType.DMA((n,))] * 3,
    )(*halves)


def _small_allreduce(vals):
    def body(v_ref, out_ref, buf, send_sems, recv_sems):
        x, y, c, j = _place()
        me = 2 * j + c
        buf[0] = v_ref[...]

        def copy(r):
            return pltpu.make_async_remote_copy(
                src_ref=v_ref, dst_ref=buf.at[r], send_sem=send_sems.at[r - 1], recv_sem=recv_sems.at[r - 1],
                device_id=(x ^ (r >> 2), y ^ ((r >> 1) & 1), c ^ (r & 1)), device_id_type=_MESH)

        for r in range(1, 8):
            copy(r).start()
        for r in range(1, 8):
            copy(r).wait()
        acc = buf[me ^ 0]
        for d in range(1, 8):
            acc = acc + buf[me ^ d]
        out_ref[...] = acc

    return pl.pallas_call(
        body, name="small_allreduce", out_shape=jax.ShapeDtypeStruct((_SMALL_ROWS, D), F32),
        in_specs=[_VMEM], out_specs=_VMEM,
        scratch_shapes=[pltpu.VMEM((8, _SMALL_ROWS, D), F32), pltpu.SemaphoreType.DMA((7,)),
                        pltpu.SemaphoreType.DMA((7,))],
    )(vals)


def _adamw_small(w, g, m, v):
    def body(w_ref, g_ref, m_ref, v_ref, d_out, m_out, v_out):
        delta, m1, v1 = _adamw(w_ref[...], g_ref[...], m_ref[...], v_ref[...])
        d_out[...] = delta
        m_out[...] = m1
        v_out[...] = v1

    return pl.pallas_call(
        body, name="adamw_small", out_shape=[jax.ShapeDtypeStruct((_SMALL_ROWS, D), F32)] * 3,
    )(w, g, m, v)


_NAMES = ("norm_mix_g", "w_in", "conv_qk", "b_if", "mlstm_norm_g", "sinks", "w_branch_a", "w_branch_b", "w_out",
          "norm_mlp_g", "w_up", "w_down", "norm_ple_g", "w_ple_gate", "w_ple_proj", "final_norm_g")
_SHARDED_NAMES = tuple(n for n, _ in _SHARDED)


def _step(x, p, target, w, m, v):
    c = lax.axis_index("c")
    j = 2 * lax.axis_index("x") + lax.axis_index("y")
    shard = {n: (a[0] if n in _SHARDED_NAMES else a) for n, a in w.items()}

    gathered, conv_all = _allgather_weights(_pack(shard, BF16), shard["conv_qk"])
    by_chip = _unpack(gathered)
    full = {n: _full_from_chips(n, by_chip[n]) for n in _SHARDED_NAMES if n != "conv_qk"}
    full["w_in"] = _win_pad(full["w_in"])
    full["conv_qk"] = _full_from_chips("conv_qk", conv_all)
    for n in ("norm_mix_g", "mlstm_norm_g", "norm_mlp_g", "norm_ple_g", "b_if", "sinks"):
        full[n] = w[n]
    full["final_norm_g"] = w["final_norm_g"].reshape(1, D)

    loss, grad_x, g = _local_step(x[0], p[0, 0], target[0], full)

    g["w_in"] = _win_unpad(g["w_in"])
    by_dest = jnp.concatenate(
        [_pack({n: _chips_from_full(n, g[n])[k] for n in _SHARDED_NAMES}, F32)[None] for k in range(4)], axis=0)
    pair = _pair_sum(by_dest, _pair_exchange(by_dest), c)
    parts = _chip_exchange(pair)
    halves = _reduce_adamw(parts, _pack(shard, F32), _pack({n: m[n][0] for n in _SHARDED_NAMES}, F32),
                           _pack({n: v[n][0] for n in _SHARDED_NAMES}, F32), j, c)
    res = [_unpack(a) for a in _sibling_share(halves)]

    small_g = _small_allreduce(_pack_small(g, extra=loss))
    small = [small_g] + list(_adamw_small(_pack_small(w), small_g, _pack_small(m), _pack_small(v)))
    shapes = {n: w[n].shape for n in _NAMES}
    loss_sum = small_g[5, 8 + SWH]
    small = [_unpack_small(a, shapes) for a in small]

    def leaf(k, n):
        return res[k][n].reshape(shapes[n]) if n in _SHARDED_NAMES else small[k][n]

    out = [loss_sum, grad_x[None]]
    for k in range(4):
        out += [leaf(k, n) for n in _NAMES]
    return tuple(out)


def kernel(x, p, norm_mix_g, w_in, conv_qk, b_if, mlstm_norm_g, sinks, w_branch_a, w_branch_b, w_out, norm_mlp_g, w_up, w_down, norm_ple_g, w_ple_gate, w_ple_proj, final_norm_g, loss_target, m_norm_mix_g, m_w_in, m_conv_qk, m_b_if, m_mlstm_norm_g, m_sinks, m_w_branch_a, m_w_branch_b, m_w_out, m_norm_mlp_g, m_w_up, m_w_down, m_norm_ple_g, m_w_ple_gate, m_w_ple_proj, m_final_norm_g, v_norm_mix_g, v_w_in, v_conv_qk, v_b_if, v_mlstm_norm_g, v_sinks, v_w_branch_a, v_w_branch_b, v_w_out, v_norm_mlp_g, v_w_up, v_w_down, v_norm_ple_g, v_w_ple_gate, v_w_ple_proj, v_final_norm_g):
    w = dict(zip(_NAMES, (norm_mix_g, w_in, conv_qk, b_if, mlstm_norm_g, sinks, w_branch_a, w_branch_b, w_out,
                          norm_mlp_g, w_up, w_down, norm_ple_g, w_ple_gate, w_ple_proj, final_norm_g)))
    m = dict(zip(_NAMES, (m_norm_mix_g, m_w_in, m_conv_qk, m_b_if, m_mlstm_norm_g, m_sinks, m_w_branch_a,
                          m_w_branch_b, m_w_out, m_norm_mlp_g, m_w_up, m_w_down, m_norm_ple_g, m_w_ple_gate,
                          m_w_ple_proj, m_final_norm_g)))
    v = dict(zip(_NAMES, (v_norm_mix_g, v_w_in, v_conv_qk, v_b_if, v_mlstm_norm_g, v_sinks, v_w_branch_a,
                          v_w_branch_b, v_w_out, v_norm_mlp_g, v_w_up, v_w_down, v_norm_ple_g, v_w_ple_gate,
                          v_w_ple_proj, v_final_norm_g)))
    return _step(x, p, loss_target, w, m, v)
```

```python
import jax
import jax.numpy as jnp
from jax import lax
from jax.experimental import pallas as pl
from jax.experimental.pallas import tpu as pltpu

F32 = jnp.float32
BF16 = jnp.bfloat16

D = 1024
PLE = 256
MLH = 4
DQK = 128
DV = 256
CONV = 4
CHUNK = 128
SWH = 16
SWKV = 4
SWG = SWH // SWKV
HD = 64
WIN = 128
DFF = 4096
EPS = 1e-6
N_IN = 6664
NP = 7168
C_QK, C_V, C_O, C_QSW, C_GA, C_GB, C_KV, C_IF = 0, 1024, 2048, 3072, 4096, 5120, 6144, 6656
IFW = NP - C_IF

ADAM_LR = 0.001
ADAM_B1 = 0.9
ADAM_B2 = 0.999
ADAM_EPS = 1e-08
ADAM_WD = 0.01
ADAM_STEP = 10

TOK_TILE = 256
VMEM_LIMIT = 48 * 1024 * 1024


def _params(**kw):
    return pltpu.CompilerParams(vmem_limit_bytes=VMEM_LIMIT, **kw)


def _pick(n, cap):
    if n <= cap:
        return n
    t = cap - cap % 128
    while t > 128 and n % t:
        t -= 128
    assert n % t == 0, (n, cap)
    return t


def _dot(a, b, dims):
    return lax.dot_general(a, b, (dims, ((), ())), preferred_element_type=F32)


def _dot_nn(a, b):
    return _dot(a, b, ((1,), (0,)))


def _dot_nt(a, b):
    return _dot(a, b, ((1,), (1,)))


def _dot_tn(a, b):
    return _dot(a, b, ((0,), (0,)))


def _sigmoid(x):
    return 1.0 / (1.0 + jnp.exp(-x))


def _mm(a, b, mode, out_dtype, name, out_chunks=1):
    bch = b.shape[0] if b.ndim == 3 else 1
    brows, bcols = b.shape[-2], b.shape[-1] * bch
    if mode == "nn":
        (m, k), (k2, n) = a.shape, (brows, bcols)
    elif mode == "nt":
        (m, k), (n, k2) = a.shape, (brows, bcols)
    else:
        (k, m), (k2, n) = a.shape, (brows, bcols)
    assert k == k2, (a.shape, b.shape, mode)
    n_cap = n // max(out_chunks, 1 if mode == "nt" else bch)
    k_cap = k // bch if mode == "nt" else k
    tm, tn, tk = _pick(m, 1024), _pick(n_cap, 512), _pick(k_cap, 1024)
    nk = k // tk
    if mode == "nn":
        a_spec = pl.BlockSpec((tm, tk), lambda i, j, kk: (i, kk))
        if bch > 1:
            bpc = (n // bch) // tn
            b_spec = pl.BlockSpec((None, tk, tn), lambda i, j, kk: (j // bpc, kk, j % bpc))
        else:
            b_spec = pl.BlockSpec((tk, tn), lambda i, j, kk: (kk, j))
        dot = _dot_nn
    elif mode == "nt":
        a_spec = pl.BlockSpec((tm, tk), lambda i, j, kk: (i, kk))
        if bch > 1:
            bpc = (k // bch) // tk
            b_spec = pl.BlockSpec((None, tn, tk), lambda i, j, kk: (kk // bpc, j, kk % bpc))
        else:
            b_spec = pl.BlockSpec((tn, tk), lambda i, j, kk: (j, kk))
        dot = _dot_nt
    else:
        assert bch == 1
        a_spec = pl.BlockSpec((tk, tm), lambda i, j, kk: (kk, i))
        b_spec = pl.BlockSpec((tk, tn), lambda i, j, kk: (kk, j))
        dot = _dot_tn
    if out_chunks > 1:
        npc = (n // out_chunks) // tn
        out_spec = pl.BlockSpec((None, tm, tn), lambda i, j, kk: (j // npc, i, j % npc))
        out_shape = jax.ShapeDtypeStruct((out_chunks, m, n // out_chunks), out_dtype)
    else:
        out_spec = pl.BlockSpec((tm, tn), lambda i, j, kk: (i, j))
        out_shape = jax.ShapeDtypeStruct((m, n), out_dtype)

    def body(a_ref, b_ref, o_ref, acc_ref):
        kk = pl.program_id(2)

        @pl.when(kk == 0)
        def _():
            acc_ref[...] = jnp.zeros_like(acc_ref)

        acc_ref[...] += dot(a_ref[...], b_ref[...])

        @pl.when(kk == nk - 1)
        def _():
            o_ref[...] = acc_ref[...].astype(out_dtype)

    return pl.pallas_call(
        body, name=name, grid=(m // tm, n // tn, nk),
        in_specs=[a_spec, b_spec], out_specs=out_spec, out_shape=out_shape,
        scratch_shapes=[pltpu.VMEM((tm, tn), F32)],
        compiler_params=_params(dimension_semantics=("parallel", "parallel", "arbitrary")),
    )(a, b)


def _tok(w, j=0):
    return pl.BlockSpec((TOK_TILE, w), lambda i: (i, j))


def _rep(shape):
    return pl.BlockSpec(shape, lambda i: (0,) * len(shape))


def _rms(x):
    rstd = lax.rsqrt(jnp.mean(x * x, axis=-1, keepdims=True) + EPS)
    return x * rstd, rstd


def _rms_bwd(xn, rstd, dxn):
    return rstd * (dxn - xn * jnp.mean(dxn * xn, axis=-1, keepdims=True))


def _norm_fwd(x, g, name):
    t = x.shape[0]

    def body(x_ref, g_ref, h_ref):
        xn, _ = _rms(x_ref[...])
        h_ref[...] = (xn * g_ref[...]).astype(BF16)

    return pl.pallas_call(
        body, name=name, grid=(t // TOK_TILE,), in_specs=[_tok(D), _rep((1, D))], out_specs=_tok(D),
        out_shape=jax.ShapeDtypeStruct((t, D), BF16), compiler_params=_params(),
    )(x, g)


def _resid_norm_fwd(x, o, g, name):
    t = x.shape[0]

    def body(x_ref, o_ref, g_ref, x1_ref, h_ref):
        x1 = x_ref[...] + o_ref[...]
        x1_ref[...] = x1
        xn, _ = _rms(x1)
        h_ref[...] = (xn * g_ref[...]).astype(BF16)

    return pl.pallas_call(
        body, name=name, grid=(t // TOK_TILE,), in_specs=[_tok(D), _tok(D), _rep((1, D))],
        out_specs=[_tok(D), _tok(D)],
        out_shape=[jax.ShapeDtypeStruct((t, D), F32), jax.ShapeDtypeStruct((t, D), BF16)],
        compiler_params=_params(),
    )(x, o, g)


def _norm_bwd_resid(x, g, dh, dres, name):
    t = x.shape[0]

    def body(x_ref, g_ref, dh_ref, dres_ref, dx_ref, dxb_ref, dg_ref):
        @pl.when(pl.program_id(0) == 0)
        def _():
            dg_ref[...] = jnp.zeros_like(dg_ref)

        xn, rstd = _rms(x_ref[...])
        dh_ = dh_ref[...]
        dg_ref[...] += jnp.sum(dh_ * xn, axis=0, keepdims=True)
        dx = dres_ref[...] + _rms_bwd(xn, rstd, dh_ * g_ref[...])
        dx_ref[...] = dx
        dxb_ref[...] = dx.astype(BF16)

    return pl.pallas_call(
        body, name=name, grid=(t // TOK_TILE,), in_specs=[_tok(D), _rep((1, D)), _tok(D), _tok(D)],
        out_specs=[_tok(D), _tok(D), _rep((1, D))],
        out_shape=[jax.ShapeDtypeStruct((t, D), F32), jax.ShapeDtypeStruct((t, D), BF16),
                   jax.ShapeDtypeStruct((1, D), F32)],
        compiler_params=_params(),
    )(x, g, dh, dres)


def _halo_prev(w, j=0):
    r = TOK_TILE // 8
    return pl.BlockSpec((8, w), lambda i: (jnp.maximum(i * r - 1, 0), j))


def _halo_next(w, nt, j=0):
    r = TOK_TILE // 8
    return pl.BlockSpec((8, w), lambda i: (jnp.minimum((i + 1) * r, nt * r - 1), j))


def _shift_down(x, halo, s):
    if s == 0:
        return x
    r = pltpu.roll(x, s, 0)
    hs = pltpu.roll(halo, s, 0)
    row = lax.broadcasted_iota(jnp.int32, hs.shape, 0)
    top = jnp.where(row < s, hs, r[0:8])
    return jnp.concatenate([top, r[8:]], axis=0)


def _shift_up(x, halo, s):
    if s == 0:
        return x
    n = x.shape[0]
    r = pltpu.roll(x, n - s, 0)
    hs = pltpu.roll(halo, 8 - s, 0)
    row = lax.broadcasted_iota(jnp.int32, hs.shape, 0)
    bot = jnp.where(row >= 8 - s, hs, r[n - 8:])
    return jnp.concatenate([r[:n - 8], bot], axis=0)


def _bf(x):
    return x.astype(BF16).astype(F32)


def _conv_taps(x, halo, w):
    x, halo, w = _bf(x), _bf(halo), _bf(w)
    acc = x * w[CONV - 1:CONV, :]
    for j in range(CONV - 1):
        acc = acc + _shift_down(x, halo, CONV - 1 - j) * w[j:j + 1, :]
    return acc


_Q_SCALE = DQK ** -0.5


def _qscale_row():
    lane = lax.broadcasted_iota(jnp.int32, (1, D), 1)
    return jnp.where(lane < MLH * DQK, _Q_SCALE, 1.0).astype(F32)


def _conv_silu_fwd(proj, conv_w):
    t = proj.shape[0]

    def body(x_ref, halo_ref, w_ref, o_ref):
        halo = jnp.where(pl.program_id(0) > 0, halo_ref[...], 0.0)
        c = _conv_taps(x_ref[...], halo, w_ref[...])
        o_ref[...] = (c * _sigmoid(c) * _qscale_row()).astype(BF16)

    return pl.pallas_call(
        body, name="conv_silu_fwd", grid=(t // TOK_TILE,),
        in_specs=[_tok(D, C_QK // D), _halo_prev(D, C_QK // D), _rep((CONV, D))], out_specs=_tok(D),
        out_shape=jax.ShapeDtypeStruct((t, D), BF16), compiler_params=_params(),
    )(proj, proj, conv_w)


def _conv_silu_bwd_a(proj, conv_w, dqk):
    t = proj.shape[0]

    def body(x_ref, halo_ref, w_ref, d_ref, dc_ref, dw_ref):
        @pl.when(pl.program_id(0) == 0)
        def _():
            dw_ref[...] = jnp.zeros_like(dw_ref)

        halo = jnp.where(pl.program_id(0) > 0, halo_ref[...], 0.0)
        x = x_ref[...]
        c = _conv_taps(x, halo, w_ref[...])
        s = _sigmoid(c)
        dc = d_ref[...] * _qscale_row() * (s * (1.0 + c * (1.0 - s)))
        dc_ref[...] = dc
        dcb, xb, halo_b = _bf(dc), _bf(x), _bf(halo)
        for j in range(CONV):
            dw_ref[j:j + 1, :] += jnp.sum(dcb * _shift_down(xb, halo_b, CONV - 1 - j), axis=0, keepdims=True)

    return pl.pallas_call(
        body, name="conv_silu_bwd_a", grid=(t // TOK_TILE,),
        in_specs=[_tok(D, C_QK // D), _halo_prev(D, C_QK // D), _rep((CONV, D)), _tok(D)],
        out_specs=[_tok(D), _rep((CONV, D))],
        out_shape=[jax.ShapeDtypeStruct((t, D), F32), jax.ShapeDtypeStruct((CONV, D), F32)],
        compiler_params=_params(),
    )(proj, proj, conv_w, dqk)


def _conv_silu_bwd_b(dc, conv_w, dproj):
    t = dc.shape[0]
    nt = t // TOK_TILE

    def body(dc_ref, halo_ref, w_ref, _, dx_ref):
        halo = _bf(jnp.where(pl.program_id(0) < nt - 1, halo_ref[...], 0.0))
        dcv = _bf(dc_ref[...])
        w = _bf(w_ref[...])
        acc = dcv * w[CONV - 1:CONV, :]
        for j in range(CONV - 1):
            acc = acc + _shift_up(dcv, halo, CONV - 1 - j) * w[j:j + 1, :]
        dx_ref[...] = acc.astype(BF16)

    return pl.pallas_call(
        body, name="conv_silu_bwd_b", grid=(nt,), in_specs=[_tok(D), _halo_next(D, nt), _rep((CONV, D)), _ANY],
        out_specs=_tok(D, C_QK // D), out_shape=jax.ShapeDtypeStruct((t, NP), BF16),
        input_output_aliases={3: 0}, compiler_params=_params(),
    )(dc, dc, conv_w, dproj)


def _gates_fwd(pre_rows, bias_col):
    t = pre_rows.shape[1]

    def body(p_ref, b_ref, g_ref, s_ref):
        z = p_ref[...] + b_ref[...]
        lf = jnp.minimum(z, 0.0) - jnp.log(1.0 + jnp.exp(-jnp.abs(z)))
        lane = lax.broadcasted_iota(jnp.int32, z.shape, 1) % CHUNK
        cum = lf
        s = 1
        while s < CHUNK:
            cum = cum + jnp.where(lane >= s, pltpu.roll(cum, s, 1), 0.0)
            s *= 2
        sub = lax.broadcasted_iota(jnp.int32, z.shape, 0)
        g_ref[...] = jnp.where(sub < MLH, z, cum)
        s_ref[...] = _sigmoid(-z)

    return pl.pallas_call(
        body, name="gates_fwd",
        out_shape=[jax.ShapeDtypeStruct((8, t), F32), jax.ShapeDtypeStruct((8, t), F32)],
        compiler_params=_params(),
    )(pre_rows, bias_col)


def _chunk_terms(grow, gcol, h, m0):
    i_row, b_row = grow[h:h + 1, :], grow[MLH + h:MLH + h + 1, :]
    i_col, b_col = gcol[:, h:h + 1], gcol[:, MLH + h:MLH + h + 1]
    b_last = b_row[:, CHUNK - 1:CHUNK]
    tt = lax.broadcasted_iota(jnp.int32, (CHUNK, CHUNK), 0)
    ss = lax.broadcasted_iota(jnp.int32, (CHUNK, CHUNK), 1)
    log_d = jnp.where(tt >= ss, b_col - b_row + i_row, -jnp.inf)
    m_t = jnp.maximum(b_col + m0, jnp.max(log_d, axis=1, keepdims=True))
    dm = jnp.exp(log_d - m_t)
    wi = jnp.exp(b_col + m0 - m_t)
    m1 = jnp.maximum(b_last + m0, jnp.max(b_last - b_row + i_row, axis=1, keepdims=True))
    ws = jnp.exp(b_last - b_col + i_col - m1)
    dec = jnp.exp(b_last + m0 - m1)
    return dm, wi, m_t, ws, dec, m1


def _mlstm_fwd(qk, proj, grow, gcol):
    t = qk.shape[0]
    nc = t // CHUNK

    def body(qk_ref, v_ref, grow_ref, gcol_ref, h_ref, cs_ref, st_ref, c_scr, st_scr):
        @pl.when(pl.program_id(0) == 0)
        def _():
            c_scr[...] = jnp.zeros_like(c_scr)
            st_scr[...] = jnp.zeros_like(st_scr)

        grow_v, gcol_v = grow_ref[...], gcol_ref[...]
        for h in range(MLH):
            q = qk_ref[:, h * DQK:(h + 1) * DQK]
            k = qk_ref[:, MLH * DQK + h * DQK:MLH * DQK + (h + 1) * DQK]
            v = v_ref[:, h * DV:(h + 1) * DV]
            c0 = c_scr[h]
            n0 = st_scr[h, 0:1, :]
            m0 = st_scr[h, 1:2, 0:1]
            cs_ref[0, h] = c0
            st_ref[0, h] = st_scr[h]
            dm, wi, m_t, ws, dec, m1 = _chunk_terms(grow_v, gcol_v, h, m0)
            s = _dot_nt(q, k) * dm
            num = wi * _dot_nt(q, c0.astype(BF16)) + _dot_nn(s.astype(BF16), v.astype(BF16))
            den = wi * jnp.sum(q.astype(F32) * n0, axis=1, keepdims=True) + jnp.sum(s, axis=1, keepdims=True)
            h_ref[:, h * DV:(h + 1) * DV] = num / jnp.maximum(jnp.abs(den), jnp.exp(-m_t))
            c_scr[h] = dec * c0 + _dot_tn((ws * v).astype(BF16), k)
            st_scr[h, 0:1, :] = dec * n0 + jnp.sum(ws * k.astype(F32), axis=0, keepdims=True)
            st_scr[h, 1:2, :] = jnp.broadcast_to(m1, (1, DQK))

    return pl.pallas_call(
        body, name="mlstm_fwd", grid=(nc,),
        in_specs=[pl.BlockSpec((CHUNK, D), lambda c: (c, 0)), pl.BlockSpec((CHUNK, D), lambda c: (c, C_V // D)),
                  pl.BlockSpec((8, CHUNK), lambda c: (0, c)), pl.BlockSpec((CHUNK, 8), lambda c: (c, 0))],
        out_specs=[pl.BlockSpec((CHUNK, D), lambda c: (c, 0)),
                   pl.BlockSpec((1, MLH, DV, DQK), lambda c: (c, 0, 0, 0)),
                   pl.BlockSpec((1, MLH, 8, DQK), lambda c: (c, 0, 0, 0))],
        out_shape=[jax.ShapeDtypeStruct((t, D), F32), jax.ShapeDtypeStruct((nc, MLH, DV, DQK), F32),
                   jax.ShapeDtypeStruct((nc, MLH, 8, DQK), F32)],
        scratch_shapes=[pltpu.VMEM((MLH, DV, DQK), F32), pltpu.VMEM((MLH, 8, DQK), F32)],
        compiler_params=_params(dimension_semantics=("arbitrary",)),
    )(qk, proj, grow, gcol)


def _mlstm_bwd(qk, proj, grow, gcol, sneg_col, cs, st, hraw, dh, dproj):
    t = qk.shape[0]
    nc = t // CHUNK

    def rev(c):
        return nc - 1 - c

    def nxt(c):
        return jnp.minimum(nc - c, nc - 1)

    def body(qk_ref, v_ref, grow_ref, gcol_ref, sneg_ref, cs_ref, st_ref, cs1_ref, st1_ref, h_ref, dh_ref, _,
             dqk_ref, dv_ref, dif_ref, dbif_ref, dc_scr, dn_scr):
        @pl.when(pl.program_id(0) == 0)
        def _():
            dc_scr[...] = jnp.zeros_like(dc_scr)
            dn_scr[...] = jnp.zeros_like(dn_scr)
            dbif_ref[...] = jnp.zeros_like(dbif_ref)

        grow_v, gcol_v, sneg = grow_ref[...], gcol_ref[...], sneg_ref[...]
        tt = lax.broadcasted_iota(jnp.int32, (CHUNK, CHUNK), 0)
        ss = lax.broadcasted_iota(jnp.int32, (CHUNK, CHUNK), 1)
        lane8 = lax.broadcasted_iota(jnp.int32, (CHUNK, 8), 1)
        dif = jnp.zeros((CHUNK, 8), F32)
        for h in range(MLH):
            q = qk_ref[:, h * DQK:(h + 1) * DQK]
            k = qk_ref[:, MLH * DQK + h * DQK:MLH * DQK + (h + 1) * DQK]
            qf, kf = q.astype(F32), k.astype(F32)
            v = v_ref[:, h * DV:(h + 1) * DV]
            vb = v.astype(BF16)
            c0 = cs_ref[0, h]
            n0 = st_ref[0, h, 0:1, :]
            m0 = st_ref[0, h, 1:2, 0:1]
            dc1 = dc_scr[h]
            dn1 = dn_scr[h, 0:1, :]
            dm, wi, m_t, ws, dec, _ = _chunk_terms(grow_v, gcol_v, h, m0)
            s = _dot_nt(q, k) * dm
            den = wi * jnp.sum(qf * n0, axis=1, keepdims=True) + jnp.sum(s, axis=1, keepdims=True)
            floor = jnp.exp(-m_t)
            g = jnp.maximum(jnp.abs(den), floor)
            dh_v = dh_ref[:, h * DV:(h + 1) * DV]
            dnum = dh_v / g
            dden = -jnp.sum(dh_v * h_ref[:, h * DV:(h + 1) * DV], axis=1, keepdims=True) / g
            dden = jnp.where(jnp.abs(den) > floor, dden * jnp.sign(den), 0.0)
            dnum_b = dnum.astype(BF16)
            da = ((_dot_nt(dnum_b, vb) + dden) * dm).astype(BF16)
            dc1_b = dc1.astype(BF16)
            dq = _dot_nn(da, k) + wi * (_dot_nn(dnum_b, c0.astype(BF16)) + dden * n0)
            dk = _dot_tn(da, q) + ws * (_dot_nn(vb, dc1_b) + dn1)
            dv = _dot_tn(s.astype(BF16), dnum_b) + ws * _dot_nt(k, dc1_b)
            dqk_ref[:, h * DQK:(h + 1) * DQK] = dq
            dqk_ref[:, MLH * DQK + h * DQK:MLH * DQK + (h + 1) * DQK] = dk
            dv_ref[:, h * DV:(h + 1) * DV] = dv.astype(BF16)
            rk = jnp.sum(kf * dk, axis=1, keepdims=True)
            df = jnp.sum(qf * dq, axis=1, keepdims=True) - rk
            df_row = jnp.sum(jnp.where(tt == ss, df, 0.0), axis=0, keepdims=True)
            suffix = jnp.sum(jnp.where(ss >= tt, df_row, 0.0), axis=1, keepdims=True)
            cross = (jnp.sum(jnp.sum(dc1 * cs1_ref[0, h], axis=1, keepdims=True), axis=0, keepdims=True)
                     + jnp.sum(dn1 * st1_ref[0, h, 0:1, :], axis=1, keepdims=True))
            dpf = (suffix + cross) * sneg[:, MLH + h:MLH + h + 1]
            dif = dif + jnp.where(lane8 == h, rk, 0.0) + jnp.where(lane8 == MLH + h, dpf, 0.0)
            dc_scr[h] = dec * dc1 + _dot_tn((wi * dnum).astype(BF16), q)
            dn_scr[h, 0:1, :] = dec * dn1 + jnp.sum(wi * dden * qf, axis=0, keepdims=True)
        dif_ref[...] = dif
        dbif_ref[...] += jnp.sum(dif, axis=0, keepdims=True)

    return pl.pallas_call(
        body, name="mlstm_bwd", grid=(nc,),
        in_specs=[pl.BlockSpec((CHUNK, D), lambda c: (rev(c), 0)),
                  pl.BlockSpec((CHUNK, D), lambda c: (rev(c), C_V // D)),
                  pl.BlockSpec((8, CHUNK), lambda c: (0, rev(c))),
                  pl.BlockSpec((CHUNK, 8), lambda c: (rev(c), 0)),
                  pl.BlockSpec((CHUNK, 8), lambda c: (rev(c), 0)),
                  pl.BlockSpec((1, MLH, DV, DQK), lambda c: (rev(c), 0, 0, 0)),
                  pl.BlockSpec((1, MLH, 8, DQK), lambda c: (rev(c), 0, 0, 0)),
                  pl.BlockSpec((1, MLH, DV, DQK), lambda c: (nxt(c), 0, 0, 0)),
                  pl.BlockSpec((1, MLH, 8, DQK), lambda c: (nxt(c), 0, 0, 0)),
                  pl.BlockSpec((CHUNK, D), lambda c: (rev(c), 0)),
                  pl.BlockSpec((CHUNK, D), lambda c: (rev(c), 0)), _ANY],
        out_specs=[pl.BlockSpec((CHUNK, D), lambda c: (rev(c), 0)),
                   pl.BlockSpec((CHUNK, D), lambda c: (rev(c), C_V // D)),
                   pl.BlockSpec((CHUNK, 8), lambda c: (rev(c), 0)),
                   pl.BlockSpec((1, 8), lambda c: (0, 0))],
        out_shape=[jax.ShapeDtypeStruct((t, D), F32), jax.ShapeDtypeStruct((t, NP), BF16),
                   jax.ShapeDtypeStruct((t, 8), F32), jax.ShapeDtypeStruct((1, 8), F32)],
        scratch_shapes=[pltpu.VMEM((MLH, DV, DQK), F32), pltpu.VMEM((MLH, 8, DQK), F32)],
        input_output_aliases={11: 1}, compiler_params=_params(dimension_semantics=("arbitrary",)),
    )(qk, proj, grow, gcol, sneg_col, cs, st, cs, st, hraw, dh, dproj)


def _ya_fwd(hraw, proj, g):
    t = hraw.shape[0]

    def body(h_ref, o_ref, g_ref, y_ref):
        so = _sigmoid(o_ref[...])
        for h in range(MLH):
            sl = slice(h * DV, (h + 1) * DV)
            xn, _ = _rms(h_ref[:, sl])
            y_ref[:, sl] = (so[:, sl] * xn * g_ref[:, sl]).astype(BF16)

    return pl.pallas_call(
        body, name="ya_fwd", grid=(t // TOK_TILE,), in_specs=[_tok(D), _tok(D, C_O // D), _rep((1, D))],
        out_specs=_tok(D), out_shape=jax.ShapeDtypeStruct((t, D), BF16), compiler_params=_params(),
    )(hraw, proj, g)


_ANY = pl.BlockSpec(memory_space=pl.ANY)


def _ya_bwd(hraw, proj, g, dya, dproj):
    t = hraw.shape[0]

    def body(h_ref, o_ref, g_ref, dy_ref, _, dh_ref, do_ref, dg_ref):
        @pl.when(pl.program_id(0) == 0)
        def _():
            dg_ref[...] = jnp.zeros_like(dg_ref)

        so = _sigmoid(o_ref[...])
        dy = dy_ref[...]
        for h in range(MLH):
            sl = slice(h * DV, (h + 1) * DV)
            xn, rstd = _rms(h_ref[:, sl])
            gs = g_ref[:, sl]
            do_ref[:, sl] = (dy[:, sl] * xn * gs * so[:, sl] * (1.0 - so[:, sl])).astype(BF16)
            dhn = dy[:, sl] * so[:, sl]
            dg_ref[:, sl] += jnp.sum(dhn * xn, axis=0, keepdims=True)
            dh_ref[:, sl] = _rms_bwd(xn, rstd, dhn * gs)

    return pl.pallas_call(
        body, name="ya_bwd", grid=(t // TOK_TILE,),
        in_specs=[_tok(D), _tok(D, C_O // D), _rep((1, D)), _tok(D), _ANY],
        out_specs=[_tok(D), _tok(D, C_O // D), _rep((1, D))],
        out_shape=[jax.ShapeDtypeStruct((t, D), F32), jax.ShapeDtypeStruct((t, NP), BF16),
                   jax.ShapeDtypeStruct((1, D), F32)],
        input_output_aliases={4: 1}, compiler_params=_params(),
    )(hraw, proj, g, dya, dproj)


_SW_SCALE = HD ** -0.5
_KVB = C_KV // (2 * SWKV * HD)


def _swa_mask(n):
    qi = lax.broadcasted_iota(jnp.int32, (WIN, 2 * WIN), 0)
    ki = lax.broadcasted_iota(jnp.int32, (WIN, 2 * WIN), 1)
    return (ki > qi) & (ki <= qi + WIN) & ((n > 0) | (ki >= WIN))


def _swa_fwd(proj, sinks):
    t = proj.shape[0]
    nb = t // WIN

    def body(q_ref, kvc_ref, kvp_ref, sink_ref, y_ref, lse_ref):
        valid = _swa_mask(pl.program_id(0))
        for hk in range(SWKV):
            ks = slice(hk * HD, (hk + 1) * HD)
            vs = slice(SWKV * HD + hk * HD, SWKV * HD + (hk + 1) * HD)
            kb = jnp.concatenate([kvp_ref[:, ks], kvc_ref[:, ks]], axis=0).astype(BF16)
            vb = jnp.concatenate([kvp_ref[:, vs], kvc_ref[:, vs]], axis=0).astype(BF16)
            for g in range(SWG):
                hq = hk * SWG + g
                qs = slice(hq * HD, (hq + 1) * HD)
                logits = jnp.where(valid, _dot_nt(q_ref[:, qs].astype(BF16), kb) * _SW_SCALE, -jnp.inf)
                sink = sink_ref[:, hq:hq + 1]
                m = jnp.maximum(jnp.max(logits, axis=1, keepdims=True), sink)
                p = jnp.exp(logits - m)
                denom = jnp.sum(p, axis=1, keepdims=True) + jnp.exp(sink - m)
                y_ref[:, qs] = _dot_nn((p / denom).astype(BF16), vb).astype(BF16)
                lse_ref[:, hq:hq + 1] = m + jnp.log(denom)

    return pl.pallas_call(
        body, name="swa_fwd", grid=(nb,),
        in_specs=[pl.BlockSpec((WIN, D), lambda n: (n, C_QSW // D)),
                  pl.BlockSpec((WIN, 512), lambda n: (n, _KVB)),
                  pl.BlockSpec((WIN, 512), lambda n: (jnp.maximum(n - 1, 0), _KVB)),
                  pl.BlockSpec((1, SWH), lambda n: (0, 0))],
        out_specs=[pl.BlockSpec((WIN, D), lambda n: (n, 0)), pl.BlockSpec((WIN, SWH), lambda n: (n, 0))],
        out_shape=[jax.ShapeDtypeStruct((t, D), BF16), jax.ShapeDtypeStruct((t, SWH), F32)],
        compiler_params=_params(),
    )(proj, proj, proj, sinks)


def _swa_bwd(proj, sinks, lse, yb, dyb, dproj):
    t = proj.shape[0]
    nb = t // WIN

    def body(q_ref, kvc_ref, kvp_ref, sink_ref, lse_ref, y_ref, dy_ref, _, dq_ref, dself_ref, dprev_ref, ds_ref):
        @pl.when(pl.program_id(0) == 0)
        def _():
            ds_ref[...] = jnp.zeros_like(ds_ref)

        valid = _swa_mask(pl.program_id(0))
        for hk in range(SWKV):
            ks = slice(hk * HD, (hk + 1) * HD)
            vs = slice(SWKV * HD + hk * HD, SWKV * HD + (hk + 1) * HD)
            kb = jnp.concatenate([kvp_ref[:, ks], kvc_ref[:, ks]], axis=0).astype(BF16)
            vb = jnp.concatenate([kvp_ref[:, vs], kvc_ref[:, vs]], axis=0).astype(BF16)
            dkb = jnp.zeros((2 * WIN, HD), F32)
            dvb = jnp.zeros((2 * WIN, HD), F32)
            for g in range(SWG):
                hq = hk * SWG + g
                qs = slice(hq * HD, (hq + 1) * HD)
                qb = q_ref[:, qs].astype(BF16)
                lse_c = lse_ref[:, hq:hq + 1]
                logits = jnp.where(valid, _dot_nt(qb, kb) * _SW_SCALE, -jnp.inf)
                p = jnp.exp(logits - lse_c)
                dy = dy_ref[:, qs]
                dyb_ = dy.astype(BF16)
                delta = jnp.sum(dy * y_ref[:, qs].astype(F32), axis=1, keepdims=True)
                dsm = (p * (_dot_nt(dyb_, vb) - delta)).astype(BF16)
                dq_ref[:, qs] = (_dot_nn(dsm, kb) * _SW_SCALE).astype(BF16)
                dkb = dkb + _dot_tn(dsm, qb) * _SW_SCALE
                dvb = dvb + _dot_tn(p.astype(BF16), dyb_)
                ds_ref[:, hq:hq + 1] += -jnp.sum(jnp.exp(sink_ref[:, hq:hq + 1] - lse_c) * delta, axis=0,
                                                 keepdims=True)
            dprev_ref[:, ks] = dkb[:WIN]
            dself_ref[:, ks] = dkb[WIN:]
            dprev_ref[:, vs] = dvb[:WIN]
            dself_ref[:, vs] = dvb[WIN:]

    return pl.pallas_call(
        body, name="swa_bwd", grid=(nb,),
        in_specs=[pl.BlockSpec((WIN, D), lambda n: (n, C_QSW // D)),
                  pl.BlockSpec((WIN, 512), lambda n: (n, _KVB)),
                  pl.BlockSpec((WIN, 512), lambda n: (jnp.maximum(n - 1, 0), _KVB)),
                  pl.BlockSpec((1, SWH), lambda n: (0, 0)),
                  pl.BlockSpec((WIN, SWH), lambda n: (n, 0)),
                  pl.BlockSpec((WIN, D), lambda n: (n, 0)),
                  pl.BlockSpec((WIN, D), lambda n: (n, 0)), _ANY],
        out_specs=[pl.BlockSpec((WIN, D), lambda n: (n, C_QSW // D)), pl.BlockSpec((WIN, 512), lambda n: (n, 0)),
                   pl.BlockSpec((WIN, 512), lambda n: (n, 0)), pl.BlockSpec((1, SWH), lambda n: (0, 0))],
        out_shape=[jax.ShapeDtypeStruct((t, NP), BF16), jax.ShapeDtypeStruct((t, 512), F32),
                   jax.ShapeDtypeStruct((t, 512), F32), jax.ShapeDtypeStruct((1, SWH), F32)],
        input_output_aliases={7: 0}, compiler_params=_params(),
    )(proj, proj, proj, sinks, lse, yb, dyb, dproj)


def _kv_combine(dself, dprev, dif, dproj):
    t = dself.shape[0]
    nb = t // WIN

    def body(a_ref, b_ref, dif_ref, _, o_ref):
        nxt = jnp.where(pl.program_id(0) < nb - 1, b_ref[...], 0.0)
        o_ref[:, 0:512] = (a_ref[...] + nxt).astype(BF16)
        lane = lax.broadcasted_iota(jnp.int32, (WIN, 128), 1)
        dif_v = dif_ref[...]
        first = jnp.zeros((WIN, 128), F32)
        for col in range(8):
            first = first + jnp.where(lane == col, dif_v[:, col:col + 1], 0.0)
        o_ref[:, 512:640] = first.astype(BF16)
        o_ref[:, 640:512 + IFW] = jnp.zeros((WIN, IFW - 128), BF16)

    return pl.pallas_call(
        body, name="kv_combine", grid=(nb,),
        in_specs=[pl.BlockSpec((WIN, 512), lambda n: (n, 0)),
                  pl.BlockSpec((WIN, 512), lambda n: (jnp.minimum(n + 1, nb - 1), 0)),
                  pl.BlockSpec((WIN, 8), lambda n: (n, 0)), _ANY],
        out_specs=pl.BlockSpec((WIN, 512 + IFW), lambda n: (n, C_KV // (512 + IFW))),
        out_shape=jax.ShapeDtypeStruct((t, NP), BF16), input_output_aliases={3: 0}, compiler_params=_params(),
    )(dself, dprev, dif, dproj)


def _merge_fwd(proj, za, zb):
    t = proj.shape[0]

    def body(ga_ref, gb_ref, za_ref, zb_ref, o_ref):
        o_ref[...] = (_sigmoid(ga_ref[...]) * za_ref[...] + _sigmoid(gb_ref[...]) * zb_ref[...]).astype(BF16)

    return pl.pallas_call(
        body, name="merge_fwd", grid=(t // TOK_TILE,),
        in_specs=[_tok(D, C_GA // D), _tok(D, C_GB // D), _tok(D), _tok(D)], out_specs=_tok(D),
        out_shape=jax.ShapeDtypeStruct((t, D), BF16), compiler_params=_params(),
    )(proj, proj, za, zb)


def _merge_bwd(proj, za, zb, dmerged):
    t = proj.shape[0]

    def body(ga_ref, gb_ref, za_ref, zb_ref, dm_ref, dza_ref, dzb_ref, dp_ref):
        dm = dm_ref[...]
        sa, sb = _sigmoid(ga_ref[...]), _sigmoid(gb_ref[...])
        dza_ref[...] = (dm * sa).astype(BF16)
        dzb_ref[...] = (dm * sb).astype(BF16)
        dp_ref[:, 0:D] = (dm * za_ref[...] * sa * (1.0 - sa)).astype(BF16)
        dp_ref[:, D:2 * D] = (dm * zb_ref[...] * sb * (1.0 - sb)).astype(BF16)

    return pl.pallas_call(
        body, name="merge_bwd", grid=(t // TOK_TILE,),
        in_specs=[_tok(D, C_GA // D), _tok(D, C_GB // D), _tok(D), _tok(D), _tok(D)],
        out_specs=[_tok(D), _tok(D), _tok(2 * D, C_GA // (2 * D))],
        out_shape=[jax.ShapeDtypeStruct((t, D), BF16)] * 2 + [jax.ShapeDtypeStruct((t, NP), BF16)],
        compiler_params=_params(),
    )(proj, proj, za, zb, dmerged)


def _act_fwd(u):
    t = u.shape[0]

    def body(u_ref, a_ref):
        r = jnp.maximum(u_ref[...], 0.0)
        a_ref[...] = (r * r).astype(BF16)

    return pl.pallas_call(
        body, name="act_fwd", grid=(t // TOK_TILE,), in_specs=[_tok(DFF)], out_specs=_tok(DFF),
        out_shape=jax.ShapeDtypeStruct((t, DFF), BF16), compiler_params=_params(),
    )(u)


def _act_bwd(u, da):
    t = u.shape[0]

    def body(u_ref, da_ref, du_ref):
        du_ref[...] = (da_ref[...] * 2.0 * jnp.maximum(u_ref[...], 0.0)).astype(BF16)

    return pl.pallas_call(
        body, name="act_bwd", grid=(t // TOK_TILE,), in_specs=[_tok(DFF), _tok(DFF)], out_specs=_tok(DFF),
        out_shape=jax.ShapeDtypeStruct((t, DFF), BF16), compiler_params=_params(),
    )(u, da)


def _ple_final(x2, gpre, pp, target, gf):
    t = x2.shape[0]

    def body(x_ref, gp_ref, pp_ref, t_ref, g_ref, loss_ref, dg_ref, dx_ref, dpp_ref, dgp_ref):
        @pl.when(pl.program_id(0) == 0)
        def _():
            loss_ref[...] = jnp.zeros_like(loss_ref)
            dg_ref[...] = jnp.zeros_like(dg_ref)

        gate = _sigmoid(gp_ref[...])
        pp_v = pp_ref[...]
        x3 = x_ref[...] + gate * pp_v
        xn, rstd = _rms(x3)
        gf_v = g_ref[...]
        err = xn * gf_v - t_ref[...]
        loss_ref[...] += (0.5 / D) * jnp.sum(jnp.sum(err * err, axis=1, keepdims=True), axis=0, keepdims=True)
        dy = err * (1.0 / D)
        dg_ref[...] += jnp.sum(dy * xn, axis=0, keepdims=True)
        dx3 = _rms_bwd(xn, rstd, dy * gf_v)
        dx_ref[...] = dx3
        dpp_ref[...] = (dx3 * gate).astype(BF16)
        dgp_ref[...] = (dx3 * pp_v * gate * (1.0 - gate)).astype(BF16)

    return pl.pallas_call(
        body, name="ple_final", grid=(t // TOK_TILE,),
        in_specs=[_tok(D), _tok(D), _tok(D), _tok(D), _rep((1, D))],
        out_specs=[_rep((1, 1)), _rep((1, D)), _tok(D), _tok(D), _tok(D)],
        out_shape=[jax.ShapeDtypeStruct((1, 1), F32), jax.ShapeDtypeStruct((1, D), F32),
                   jax.ShapeDtypeStruct((t, D), F32), jax.ShapeDtypeStruct((t, D), BF16),
                   jax.ShapeDtypeStruct((t, D), BF16)],
        compiler_params=_params(),
    )(x2, gpre, pp, target, gf)


def _win_pad(w):
    zeros = jnp.zeros((w.shape[0], IFW - 8), w.dtype)
    return jnp.concatenate([w[:, 0:3072], w[:, 3080:4104], w[:, 4616:6664], w[:, 4104:4616], w[:, 3072:3080], zeros],
                           axis=1)


def _win_unpad(wp):
    return jnp.concatenate([wp[:, 0:3072], wp[:, C_IF:C_IF + 8], wp[:, C_QSW:C_QSW + 1024], wp[:, C_KV:C_KV + 512],
                            wp[:, C_GA:C_GA + 2048]], axis=1)


def _local_step(x, p, target, w):
    t = x.shape[0]
    pb = p.astype(BF16)

    h0 = _norm_fwd(x, w["norm_mix_g"], "norm_mix")
    proj = _mm(h0, w["w_in"], "nn", F32, "mm_in")
    qk = _conv_silu_fwd(proj, w["conv_qk"])
    grow, sneg_row = _gates_fwd(proj[:, C_IF:C_IF + 8].T, w["b_if"].reshape(8, 1))
    gcol, sneg_col = grow.T, sneg_row.T
    hraw, cs, st = _mlstm_fwd(qk, proj, grow, gcol)
    ya = _ya_fwd(hraw, proj, w["mlstm_norm_g"])
    yb, lse = _swa_fwd(proj, w["sinks"])
    za = _mm(ya, w["w_branch_a"], "nn", F32, "mm_branch_a")
    zb = _mm(yb, w["w_branch_b"], "nn", F32, "mm_branch_b")
    merged = _merge_fwd(proj, za, zb)
    o1 = _mm(merged, w["w_out"], "nn", F32, "mm_out")
    x1, hn1 = _resid_norm_fwd(x, o1, w["norm_mlp_g"], "resid_norm_mlp")
    u = _mm(hn1, w["w_up"], "nn", F32, "mm_up")
    act = _act_fwd(u)
    o2 = _mm(act, w["w_down"], "nn", F32, "mm_down")
    x2, hn2 = _resid_norm_fwd(x1, o2, w["norm_ple_g"], "resid_norm_ple")
    gpre = _mm(hn2, w["w_ple_gate"], "nn", F32, "mm_ple_gate")
    pp = _mm(pb, w["w_ple_proj"], "nn", F32, "mm_ple_proj")
    loss, d_final_g, dx3, dpp, dgpre = _ple_final(x2, gpre, pp, target, w["final_norm_g"])

    g = {"final_norm_g": d_final_g}
    g["w_ple_proj"] = _mm(pb, dpp, "tn", F32, "mm_d_ple_proj", out_chunks=4)
    g["w_ple_gate"] = _mm(hn2, dgpre, "tn", F32, "mm_d_ple_gate")
    dhn2 = _mm(dgpre, w["w_ple_gate"], "nt", F32, "mm_dhn2")
    dx2, dx2b, g["norm_ple_g"] = _norm_bwd_resid(x2, w["norm_ple_g"], dhn2, dx3, "norm_bwd_ple")
    g["w_down"] = _mm(act, dx2b, "tn", F32, "mm_d_down")
    da = _mm(dx2b, w["w_down"], "nt", F32, "mm_da")
    du = _act_bwd(u, da)
    g["w_up"] = _mm(hn1, du, "tn", F32, "mm_d_up", out_chunks=4)
    dhn1 = _mm(du, w["w_up"], "nt", F32, "mm_dhn1")
    dx1, dx1b, g["norm_mlp_g"] = _norm_bwd_resid(x1, w["norm_mlp_g"], dhn1, dx2, "norm_bwd_mlp")
    g["w_out"] = _mm(merged, dx1b, "tn", F32, "mm_d_out")
    dmerged = _mm(dx1b, w["w_out"], "nt", F32, "mm_dmerged")
    dza, dzb, dproj = _merge_bwd(proj, za, zb, dmerged)
    g["w_branch_a"] = _mm(ya, dza, "tn", F32, "mm_d_branch_a")
    g["w_branch_b"] = _mm(yb, dzb, "tn", F32, "mm_d_branch_b")
    dya = _mm(dza, w["w_branch_a"], "nt", F32, "mm_dya")
    dyb = _mm(dzb, w["w_branch_b"], "nt", F32, "mm_dyb")
    dhraw, dproj, g["mlstm_norm_g"] = _ya_bwd(hraw, proj, w["mlstm_norm_g"], dya, dproj)
    dqk, dproj, dif, g["b_if"] = _mlstm_bwd(qk, proj, grow, gcol, sneg_col, cs, st, hraw, dhraw, dproj)
    dc, g["conv_qk"] = _conv_silu_bwd_a(proj, w["conv_qk"], dqk)
    dproj = _conv_silu_bwd_b(dc, w["conv_qk"], dproj)
    dproj, dkv_self, dkv_prev, g["sinks"] = _swa_bwd(proj, w["sinks"], lse, yb, dyb, dproj)
    dproj = _kv_combine(dkv_self, dkv_prev, dif, dproj)
    g["w_in"] = _mm(h0, dproj, "tn", F32, "mm_d_in")
    dh0 = _mm(dproj, w["w_in"], "nt", F32, "mm_dh0")
    grad_x, _, g["norm_mix_g"] = _norm_bwd_resid(x, w["norm_mix_g"], dh0, dx1, "norm_bwd_mix")
    return loss, grad_x, g


_W4 = ("w_branch_a", "w_branch_b", "w_out", "w_ple_gate")
_SHARDED_NAMES = ("w_in", "w_up", "w_down", "w_ple_proj", "conv_qk") + _W4
_SMALL_ROWS = 16
_CONV_ROW = 8


def _group(s):
    return [s["w_in"], jnp.concatenate([s[n] for n in _W4], axis=0), s["w_up"], s["w_down"], s["w_ple_proj"]]


def _ungroup(arrs):
    out = {"w_in": arrs[0], "w_up": arrs[2], "w_down": arrs[3], "w_ple_proj": arrs[4]}
    rows = arrs[1].shape[0] // len(_W4)
    for i, n in enumerate(_W4):
        out[n] = arrs[1][i * rows:(i + 1) * rows]
    return out


def _rows_tile(rows):
    return 256 if rows % 256 == 0 else rows


_SMALL = ("norm_mix_g", "mlstm_norm_g", "norm_mlp_g", "norm_ple_g", "final_norm_g")


def _pack_small(vals, extra=None, conv=None):
    rows = [vals[n].reshape(1, D) for n in _SMALL]
    tail = [vals["b_if"].reshape(1, 8), vals["sinks"].reshape(1, SWH)]
    used = 8 + SWH
    if extra is not None:
        tail.append(extra.reshape(1, 1))
        used += 1
    tail.append(jnp.zeros((1, D - used), F32))
    rows.append(jnp.concatenate(tail, axis=1))
    rows.append(jnp.zeros((_CONV_ROW - len(rows), D), F32))
    rows.append(jnp.zeros((CONV, D), F32) if conv is None else conv)
    rows.append(jnp.zeros((_SMALL_ROWS - _CONV_ROW - CONV, D), F32))
    return jnp.concatenate(rows, axis=0)


def _unpack_small(slab, shapes):
    out = {n: slab[i].reshape(shapes[n]) for i, n in enumerate(_SMALL)}
    out["b_if"] = slab[5, 0:8].reshape(shapes["b_if"])
    out["sinks"] = slab[5, 8:8 + SWH].reshape(shapes["sinks"])
    return out


_MESH = pl.DeviceIdType.MESH
_HBM = pl.BlockSpec(memory_space=pltpu.HBM)
_VMEM = pl.BlockSpec(memory_space=pltpu.VMEM)


def _place():
    x, y, c = lax.axis_index("x"), lax.axis_index("y"), lax.axis_index("c")
    return x, y, c, 2 * x + y


def _chip_peer(x, y, r):
    return (x ^ (r >> 1), y ^ (r & 1))


def _half(ref, which):
    h = ref.shape[-2] // 2
    return pl.ds(which * h, h)


def _allgather_weights(shards, conv):
    n = len(shards)

    def body(*refs):
        ins, conv_ref = refs[:n], refs[n]
        outs, conv_out = refs[n + 1:2 * n + 1], refs[2 * n + 1]
        send_a, recv_a, send_b, recv_b, send_c, recv_c, local_sems = refs[2 * n + 2:]
        x, y, c, j = _place()
        sibling = (x, y, 1 - c)
        local = [pltpu.make_async_copy(ins[k], outs[k].at[j], local_sems.at[k]) for k in range(n)]
        local.append(pltpu.make_async_copy(conv_ref, conv_out.at[j], local_sems.at[n]))
        for cp in local:
            cp.start()

        def copy_a(k, r, chip):
            rows = _half(ins[k], c)
            return pltpu.make_async_remote_copy(
                src_ref=ins[k].at[rows], dst_ref=outs[k].at[chip, rows], send_sem=send_a.at[3 * k + r - 1],
                recv_sem=recv_a.at[3 * k + r - 1], device_id=(*_chip_peer(x, y, r), c), device_id_type=_MESH)

        def copy_b(k, r, chip, which):
            rows = _half(ins[k], which)
            return pltpu.make_async_remote_copy(
                src_ref=outs[k].at[chip, rows], dst_ref=outs[k].at[chip, rows], send_sem=send_b.at[3 * k + r - 1],
                recv_sem=recv_b.at[3 * k + r - 1], device_id=sibling, device_id_type=_MESH)

        def copy_c(r, chip):
            return pltpu.make_async_remote_copy(
                src_ref=conv_ref, dst_ref=conv_out.at[chip], send_sem=send_c.at[r - 1],
                recv_sem=recv_c.at[r - 1], device_id=(*_chip_peer(x, y, r), c), device_id_type=_MESH)

        for k in range(n):
            for r in (1, 2, 3):
                copy_a(k, r, j).start()
        for r in (1, 2, 3):
            copy_c(r, j).start()
        for k in range(n):
            for r in (1, 2, 3):
                copy_a(k, r, j ^ r).wait_recv()
                copy_b(k, r, j ^ r, c).start()
        for k in range(n):
            for r in (1, 2, 3):
                copy_b(k, r, j ^ r, 1 - c).wait_recv()
        for r in (1, 2, 3):
            copy_c(r, j ^ r).wait_recv()
        for k in range(n):
            for r in (1, 2, 3):
                copy_a(k, r, j).wait_send()
                copy_b(k, r, j ^ r, c).wait_send()
        for r in (1, 2, 3):
            copy_c(r, j).wait_send()
        for cp in local:
            cp.wait()

    return pl.pallas_call(
        body, name="allgather_weights",
        out_shape=[jax.ShapeDtypeStruct((4,) + s.shape, s.dtype) for s in shards]
        + [jax.ShapeDtypeStruct((4,) + conv.shape, F32)],
        in_specs=[_HBM] * (n + 1), out_specs=[_HBM] * (n + 1),
        scratch_shapes=[pltpu.SemaphoreType.DMA((3 * n,))] * 4 + [pltpu.SemaphoreType.DMA((3,))] * 2
        + [pltpu.SemaphoreType.DMA((n + 1,))],
    )(*shards, conv)


def _pair_exchange(gs):
    n = len(gs)

    def body(*refs):
        ins, outs, send_sems, recv_sems = refs[:n], refs[n:2 * n], refs[2 * n], refs[2 * n + 1]
        x, y, c, _ = _place()
        cps = [pltpu.make_async_remote_copy(
            src_ref=ins[k].at[:, _half(ins[k], 1 - c)], dst_ref=outs[k], send_sem=send_sems.at[k],
            recv_sem=recv_sems.at[k], device_id=(x, y, 1 - c), device_id_type=_MESH) for k in range(n)]
        for cp in cps:
            cp.start()
        for cp in cps:
            cp.wait()

    return pl.pallas_call(
        body, name="pair_exchange",
        out_shape=[jax.ShapeDtypeStruct((4, g.shape[1] // 2, g.shape[2]), F32) for g in gs],
        in_specs=[_HBM] * n, out_specs=[_HBM] * n, scratch_shapes=[pltpu.SemaphoreType.DMA((n,))] * 2,
    )(*gs)


def _pair_sum(g, theirs, c, name):
    _, h, cols = theirs.shape
    tr = _rows_tile(h)
    nb = h // tr

    def body(c_ref, a_ref, b_ref, o_ref):
        o_ref[...] = a_ref[...] + b_ref[...]

    return pl.pallas_call(
        body, name=name,
        grid_spec=pltpu.PrefetchScalarGridSpec(
            num_scalar_prefetch=1, grid=(4, nb),
            in_specs=[pl.BlockSpec((1, tr, cols), lambda k, i, c_ref: (k, c_ref[0] * nb + i, 0)),
                      pl.BlockSpec((1, tr, cols), lambda k, i, c_ref: (k, i, 0))],
            out_specs=pl.BlockSpec((1, tr, cols), lambda k, i, c_ref: (k, i, 0))),
        out_shape=jax.ShapeDtypeStruct(theirs.shape, F32), compiler_params=_params(),
    )(c.reshape(1).astype(jnp.int32), g, theirs)


def _chip_exchange(ss):
    n = len(ss)

    def body(*refs):
        ins, outs, send_sems, recv_sems = refs[:n], refs[n:2 * n], refs[2 * n], refs[2 * n + 1]
        x, y, c, j = _place()
        cps = [pltpu.make_async_remote_copy(
            src_ref=ins[k].at[j ^ r], dst_ref=outs[k].at[r - 1], send_sem=send_sems.at[3 * k + r - 1],
            recv_sem=recv_sems.at[3 * k + r - 1], device_id=(*_chip_peer(x, y, r), c), device_id_type=_MESH)
            for k in range(n) for r in (1, 2, 3)]
        for cp in cps:
            cp.start()
        for cp in cps:
            cp.wait()

    return pl.pallas_call(
        body, name="chip_exchange", out_shape=[jax.ShapeDtypeStruct((3,) + s.shape[1:], F32) for s in ss],
        in_specs=[_HBM] * n, out_specs=[_HBM] * n, scratch_shapes=[pltpu.SemaphoreType.DMA((3 * n,))] * 2,
    )(*ss)


def _reduce4(own, others, j, c, name):
    _, h, cols = own.shape
    tr = _rows_tile(h)
    nb = h // tr

    def body(idx_ref, s_ref, a0, a1, a2, o_ref):
        o_ref[...] = ((s_ref[0] + a0[0]) + a1[0]) + a2[0]

    def other(r):
        return pl.BlockSpec((1, tr, cols), lambda i, idx_ref: (r, i, 0))

    return pl.pallas_call(
        body, name=name,
        grid_spec=pltpu.PrefetchScalarGridSpec(
            num_scalar_prefetch=1, grid=(nb,),
            in_specs=[pl.BlockSpec((1, tr, cols), lambda i, idx_ref: (idx_ref[0], i, 0)), other(0), other(1), other(2)],
            out_specs=pl.BlockSpec((tr, cols), lambda i, idx_ref: (idx_ref[1] * nb + i, 0))),
        out_shape=jax.ShapeDtypeStruct((2 * h, cols), F32), compiler_params=_params(),
    )(jnp.stack([j, c]).astype(jnp.int32), own, others, others, others)


def _sibling_share(fulls):
    n = len(fulls)

    def body(*refs):
        outs, send_sems, recv_sems = refs[n:2 * n], refs[2 * n], refs[2 * n + 1]
        x, y, c, _ = _place()
        cps = [pltpu.make_async_remote_copy(
            src_ref=outs[k].at[_half(outs[k], c)], dst_ref=outs[k].at[_half(outs[k], c)], send_sem=send_sems.at[k],
            recv_sem=recv_sems.at[k], device_id=(x, y, 1 - c), device_id_type=_MESH) for k in range(n)]
        for cp in cps:
            cp.start()
        for cp in cps:
            cp.wait()

    return pl.pallas_call(
        body, name="sibling_share", out_shape=[jax.ShapeDtypeStruct(f.shape, F32) for f in fulls],
        in_specs=[_HBM] * n, out_specs=[_HBM] * n, input_output_aliases={k: k for k in range(n)},
        scratch_shapes=[pltpu.SemaphoreType.DMA((n,))] * 2,
    )(*fulls)


def _adamw(w, g, m, v):
    m1 = ADAM_B1 * m + (1.0 - ADAM_B1) * g
    v1 = ADAM_B2 * v + (1.0 - ADAM_B2) * (g * g)
    m_hat = m1 / (1.0 - ADAM_B1 ** ADAM_STEP)
    v_hat = v1 / (1.0 - ADAM_B2 ** ADAM_STEP)
    delta = -ADAM_LR * (m_hat / (jnp.sqrt(v_hat) + ADAM_EPS) + ADAM_WD * w)
    return delta, m1, v1


def _adamw_call(w, g, m, v, name):
    rows, cols = w.shape
    tr = _rows_tile(rows)

    def body(w_ref, g_ref, m_ref, v_ref, d_out, m_out, v_out):
        delta, m1, v1 = _adamw(w_ref[...], g_ref[...], m_ref[...], v_ref[...])
        d_out[...] = delta
        m_out[...] = m1
        v_out[...] = v1

    blk = pl.BlockSpec((tr, cols), lambda i: (i, 0))
    return pl.pallas_call(
        body, name=name, grid=(rows // tr,), in_specs=[blk] * 4, out_specs=[blk] * 3,
        out_shape=[jax.ShapeDtypeStruct((rows, cols), F32)] * 3, compiler_params=_params(),
    )(w, g, m, v)


def _small_allreduce(vals):
    def body(v_ref, out_ref, buf, send_sems, recv_sems):
        x, y, c, j = _place()
        me = 2 * j + c
        buf[0] = v_ref[...]

        def copy(r):
            return pltpu.make_async_remote_copy(
                src_ref=v_ref, dst_ref=buf.at[r], send_sem=send_sems.at[r - 1], recv_sem=recv_sems.at[r - 1],
                device_id=(x ^ (r >> 2), y ^ ((r >> 1) & 1), c ^ (r & 1)), device_id_type=_MESH)

        for r in range(1, 8):
            copy(r).start()
        for r in range(1, 8):
            copy(r).wait()
        acc = buf[me ^ 0]
        for d in range(1, 8):
            acc = acc + buf[me ^ d]
        out_ref[...] = acc

    return pl.pallas_call(
        body, name="small_allreduce", out_shape=jax.ShapeDtypeStruct((_SMALL_ROWS, D), F32),
        in_specs=[_VMEM], out_specs=_VMEM,
        scratch_shapes=[pltpu.VMEM((8, _SMALL_ROWS, D), F32), pltpu.SemaphoreType.DMA((7,)),
                        pltpu.SemaphoreType.DMA((7,))],
    )(vals)


_NAMES = ("norm_mix_g", "w_in", "conv_qk", "b_if", "mlstm_norm_g", "sinks", "w_branch_a", "w_branch_b", "w_out",
          "norm_mlp_g", "w_up", "w_down", "norm_ple_g", "w_ple_gate", "w_ple_proj", "final_norm_g")
_GROUP_NAMES = ("w_in", "w4", "w_up", "w_down", "w_ple_proj")


def _step(x, p, target, w, m, v):
    c = lax.axis_index("c")
    j = 2 * lax.axis_index("x") + lax.axis_index("y")

    def shards(d):
        return {n: d[n][0] for n in _SHARDED_NAMES}

    ws = shards(w)
    gathered = _allgather_weights([a.astype(BF16) for a in _group(ws)], ws["conv_qk"])
    w_in_all, w4_all, w_up_all, w_down_all, w_pp_all, conv_all = gathered
    full = {n: w[n] for n in ("norm_mix_g", "mlstm_norm_g", "norm_mlp_g", "norm_ple_g", "b_if", "sinks")}
    full["final_norm_g"] = w["final_norm_g"].reshape(1, D)
    full["w_in"] = _win_pad(jnp.swapaxes(w_in_all, 0, 1).reshape(D, N_IN))
    w4_all = w4_all.reshape(4, len(_W4), D // 4, D)
    for i, n in enumerate(_W4):
        full[n] = w4_all[:, i].reshape(D, D)
    full["w_up"] = w_up_all
    full["w_down"] = w_down_all.reshape(DFF, D)
    full["w_ple_proj"] = w_pp_all
    full["conv_qk"] = jnp.swapaxes(conv_all, 0, 1).reshape(CONV, D)

    loss, grad_x, g = _local_step(x[0], p[0, 0], target[0], full)

    w_in_g = _win_unpad(g["w_in"])
    by_dest = [jnp.swapaxes(w_in_g.reshape(D, 4, N_IN // 4), 0, 1),
               jnp.stack([g[n].reshape(4, D // 4, D) for n in _W4], axis=1).reshape(4, D, D),
               g["w_up"], g["w_down"].reshape(4, DFF // 4, D), g["w_ple_proj"]]
    theirs = _pair_exchange(by_dest)
    sums = [_pair_sum(a, b, c, "pair_sum_" + n) for a, b, n in zip(by_dest, theirs, _GROUP_NAMES)]
    others = _chip_exchange(sums)
    halves = [_reduce4(a, b, j, c, "reduce4_" + n) for a, b, n in zip(sums, others, _GROUP_NAMES)]
    grads = _sibling_share(halves)

    small_g = _small_allreduce(_pack_small(g, extra=loss, conv=g["conv_qk"]))
    conv_g = lax.dynamic_slice(small_g[_CONV_ROW:_CONV_ROW + CONV], (0, j * (D // 4)), (CONV, D // 4))

    ms, vs = shards(m), shards(v)
    upd = [_adamw_call(wa, ga, ma, va, "adamw_" + n)
           for wa, ga, ma, va, n in zip(_group(ws), grads, _group(ms), _group(vs), _GROUP_NAMES)]
    conv_upd = _adamw_call(ws["conv_qk"], conv_g, ms["conv_qk"], vs["conv_qk"], "adamw_conv")
    small_upd = _adamw_call(_pack_small(w), small_g, _pack_small(m), _pack_small(v), "adamw_small")

    shapes = {n: w[n].shape for n in _NAMES}
    res = []
    for k in range(4):
        big = _ungroup(list(grads) if k == 0 else [u[k - 1] for u in upd])
        big["conv_qk"] = conv_g if k == 0 else conv_upd[k - 1]
        leaves = _unpack_small(small_g if k == 0 else small_upd[k - 1], shapes)
        leaves.update({n: a.reshape(shapes[n]) for n, a in big.items()})
        res.append(leaves)

    out = [small_g[5, 8 + SWH], grad_x[None]]
    for k in range(4):
        out += [res[k][n] for n in _NAMES]
    return tuple(out)


def kernel(x, p, norm_mix_g, w_in, conv_qk, b_if, mlstm_norm_g, sinks, w_branch_a, w_branch_b, w_out, norm_mlp_g, w_up, w_down, norm_ple_g, w_ple_gate, w_ple_proj, final_norm_g, loss_target, m_norm_mix_g, m_w_in, m_conv_qk, m_b_if, m_mlstm_norm_g, m_sinks, m_w_branch_a, m_w_branch_b, m_w_out, m_norm_mlp_g, m_w_up, m_w_down, m_norm_ple_g, m_w_ple_gate, m_w_ple_proj, m_final_norm_g, v_norm_mix_g, v_w_in, v_conv_qk, v_b_if, v_mlstm_norm_g, v_sinks, v_w_branch_a, v_w_branch_b, v_w_out, v_norm_mlp_g, v_w_up, v_w_down, v_norm_ple_g, v_w_ple_gate, v_w_ple_proj, v_final_norm_g):
    w = dict(zip(_NAMES, (norm_mix_g, w_in, conv_qk, b_if, mlstm_norm_g, sinks, w_branch_a, w_branch_b, w_out,
                          norm_mlp_g, w_up, w_down, norm_ple_g, w_ple_gate, w_ple_proj, final_norm_g)))
    m = dict(zip(_NAMES, (m_norm_mix_g, m_w_in, m_conv_qk, m_b_if, m_mlstm_norm_g, m_sinks, m_w_branch_a,
                          m_w_branch_b, m_w_out, m_norm_mlp_g, m_w_up, m_w_down, m_norm_ple_g, m_w_ple_gate,
                          m_w_ple_proj, m_final_norm_g)))
    v = dict(zip(_NAMES, (v_norm_mix_g, v_w_in, v_conv_qk, v_b_if, v_mlstm_norm_g, v_sinks, v_w_branch_a,
                          v_w_branch_b, v_w_out, v_norm_mlp_g, v_w_up, v_w_down, v_norm_ple_g, v_w_ple_gate,
                          v_w_ple_proj, v_final_norm_g)))
    return _step(x, p, loss_target, w, m, v)
```

```python
import jax
import jax.numpy as jnp
from jax import lax
from jax.experimental import pallas as pl
from jax.experimental.pallas import tpu as pltpu

F32 = jnp.float32
BF16 = jnp.bfloat16

D = 1024
PLE = 256
MLH = 4
DQK = 128
DV = 256
CONV = 4
CHUNK = 128
SWH = 16
SWKV = 4
SWG = SWH // SWKV
HD = 64
WIN = 128
DFF = 4096
EPS = 1e-6
N_IN = 6664
NP = 7168
C_QK, C_V, C_O, C_QSW, C_GA, C_GB, C_KV, C_IF = 0, 1024, 2048, 3072, 4096, 5120, 6144, 6656
IFW = NP - C_IF

ADAM_LR = 0.001
ADAM_B1 = 0.9
ADAM_B2 = 0.999
ADAM_EPS = 1e-08
ADAM_WD = 0.01
ADAM_STEP = 10

TOK_TILE = 256
VMEM_LIMIT = 48 * 1024 * 1024


def _params(**kw):
    return pltpu.CompilerParams(vmem_limit_bytes=VMEM_LIMIT, **kw)


def _pick(n, cap):
    if n <= cap:
        return n
    t = cap - cap % 128
    while t > 128 and n % t:
        t -= 128
    assert n % t == 0, (n, cap)
    return t


def _dot(a, b, dims):
    return lax.dot_general(a, b, (dims, ((), ())), preferred_element_type=F32)


def _dot_nn(a, b):
    return _dot(a, b, ((1,), (0,)))


def _dot_nt(a, b):
    return _dot(a, b, ((1,), (1,)))


def _dot_tn(a, b):
    return _dot(a, b, ((0,), (0,)))


def _sigmoid(x):
    return 1.0 / (1.0 + jnp.exp(-x))


def _mm(a, b, mode, out_dtype, name, out_chunks=1):
    bch = b.shape[0] if b.ndim == 3 else 1
    brows, bcols = b.shape[-2], b.shape[-1] * bch
    if mode == "nn":
        (m, k), (k2, n) = a.shape, (brows, bcols)
    elif mode == "nt":
        (m, k), (n, k2) = a.shape, (brows, bcols)
    else:
        (k, m), (k2, n) = a.shape, (brows, bcols)
    assert k == k2, (a.shape, b.shape, mode)
    n_cap = n // max(out_chunks, 1 if mode == "nt" else bch)
    k_cap = k // bch if mode == "nt" else k
    tm, tn, tk = _pick(m, 1024), _pick(n_cap, 512), _pick(k_cap, 1024)
    nk = k // tk
    if mode == "nn":
        a_spec = pl.BlockSpec((tm, tk), lambda i, j, kk: (i, kk))
        if bch > 1:
            bpc = (n // bch) // tn
            b_spec = pl.BlockSpec((None, tk, tn), lambda i, j, kk: (j // bpc, kk, j % bpc))
        else:
            b_spec = pl.BlockSpec((tk, tn), lambda i, j, kk: (kk, j))
        dot = _dot_nn
    elif mode == "nt":
        a_spec = pl.BlockSpec((tm, tk), lambda i, j, kk: (i, kk))
        if bch > 1:
            bpc = (k // bch) // tk
            b_spec = pl.BlockSpec((None, tn, tk), lambda i, j, kk: (kk // bpc, j, kk % bpc))
        else:
            b_spec = pl.BlockSpec((tn, tk), lambda i, j, kk: (j, kk))
        dot = _dot_nt
    else:
        assert bch == 1
        a_spec = pl.BlockSpec((tk, tm), lambda i, j, kk: (kk, i))
        b_spec = pl.BlockSpec((tk, tn), lambda i, j, kk: (kk, j))
        dot = _dot_tn
    if out_chunks > 1:
        npc = (n // out_chunks) // tn
        out_spec = pl.BlockSpec((None, tm, tn), lambda i, j, kk: (j // npc, i, j % npc))
        out_shape = jax.ShapeDtypeStruct((out_chunks, m, n // out_chunks), out_dtype)
    else:
        out_spec = pl.BlockSpec((tm, tn), lambda i, j, kk: (i, j))
        out_shape = jax.ShapeDtypeStruct((m, n), out_dtype)

    def body(a_ref, b_ref, o_ref, acc_ref):
        kk = pl.program_id(2)

        @pl.when(kk == 0)
        def _():
            acc_ref[...] = jnp.zeros_like(acc_ref)

        acc_ref[...] += dot(a_ref[...], b_ref[...])

        @pl.when(kk == nk - 1)
        def _():
            o_ref[...] = acc_ref[...].astype(out_dtype)

    return pl.pallas_call(
        body, name=name, grid=(m // tm, n // tn, nk),
        in_specs=[a_spec, b_spec], out_specs=out_spec, out_shape=out_shape,
        scratch_shapes=[pltpu.VMEM((tm, tn), F32)],
        compiler_params=_params(dimension_semantics=("parallel", "parallel", "arbitrary")),
    )(a, b)


def _tok(w, j=0):
    return pl.BlockSpec((TOK_TILE, w), lambda i: (i, j))


def _rep(shape):
    return pl.BlockSpec(shape, lambda i: (0,) * len(shape))


def _rms(x):
    rstd = lax.rsqrt(jnp.mean(x * x, axis=-1, keepdims=True) + EPS)
    return x * rstd, rstd


def _rms_bwd(xn, rstd, dxn):
    return rstd * (dxn - xn * jnp.mean(dxn * xn, axis=-1, keepdims=True))


def _norm_fwd(x, g, name):
    t = x.shape[0]

    def body(x_ref, g_ref, h_ref):
        xn, _ = _rms(x_ref[...])
        h_ref[...] = (xn * g_ref[...]).astype(BF16)

    return pl.pallas_call(
        body, name=name, grid=(t // TOK_TILE,), in_specs=[_tok(D), _rep((1, D))], out_specs=_tok(D),
        out_shape=jax.ShapeDtypeStruct((t, D), BF16), compiler_params=_params(),
    )(x, g)


def _resid_norm_fwd(x, o, g, name):
    t = x.shape[0]

    def body(x_ref, o_ref, g_ref, x1_ref, h_ref):
        x1 = x_ref[...] + o_ref[...]
        x1_ref[...] = x1
        xn, _ = _rms(x1)
        h_ref[...] = (xn * g_ref[...]).astype(BF16)

    return pl.pallas_call(
        body, name=name, grid=(t // TOK_TILE,), in_specs=[_tok(D), _tok(D), _rep((1, D))],
        out_specs=[_tok(D), _tok(D)],
        out_shape=[jax.ShapeDtypeStruct((t, D), F32), jax.ShapeDtypeStruct((t, D), BF16)],
        compiler_params=_params(),
    )(x, o, g)


def _norm_bwd_resid(x, g, dh, dres, name):
    t = x.shape[0]

    def body(x_ref, g_ref, dh_ref, dres_ref, dx_ref, dxb_ref, dg_ref):
        @pl.when(pl.program_id(0) == 0)
        def _():
            dg_ref[...] = jnp.zeros_like(dg_ref)

        xn, rstd = _rms(x_ref[...])
        dh_ = dh_ref[...]
        dg_ref[...] += jnp.sum(dh_ * xn, axis=0, keepdims=True)
        dx = dres_ref[...] + _rms_bwd(xn, rstd, dh_ * g_ref[...])
        dx_ref[...] = dx
        dxb_ref[...] = dx.astype(BF16)

    return pl.pallas_call(
        body, name=name, grid=(t // TOK_TILE,), in_specs=[_tok(D), _rep((1, D)), _tok(D), _tok(D)],
        out_specs=[_tok(D), _tok(D), _rep((1, D))],
        out_shape=[jax.ShapeDtypeStruct((t, D), F32), jax.ShapeDtypeStruct((t, D), BF16),
                   jax.ShapeDtypeStruct((1, D), F32)],
        compiler_params=_params(),
    )(x, g, dh, dres)


def _halo_prev(w, j=0):
    r = TOK_TILE // 8
    return pl.BlockSpec((8, w), lambda i: (jnp.maximum(i * r - 1, 0), j))


def _halo_next(w, nt, j=0):
    r = TOK_TILE // 8
    return pl.BlockSpec((8, w), lambda i: (jnp.minimum((i + 1) * r, nt * r - 1), j))


def _shift_down(x, halo, s):
    if s == 0:
        return x
    r = pltpu.roll(x, s, 0)
    hs = pltpu.roll(halo, s, 0)
    row = lax.broadcasted_iota(jnp.int32, hs.shape, 0)
    top = jnp.where(row < s, hs, r[0:8])
    return jnp.concatenate([top, r[8:]], axis=0)


def _shift_up(x, halo, s):
    if s == 0:
        return x
    n = x.shape[0]
    r = pltpu.roll(x, n - s, 0)
    hs = pltpu.roll(halo, 8 - s, 0)
    row = lax.broadcasted_iota(jnp.int32, hs.shape, 0)
    bot = jnp.where(row >= 8 - s, hs, r[n - 8:])
    return jnp.concatenate([r[:n - 8], bot], axis=0)


def _bf(x):
    return x.astype(BF16).astype(F32)


def _conv_taps(x, halo, w):
    x, halo, w = _bf(x), _bf(halo), _bf(w)
    acc = x * w[CONV - 1:CONV, :]
    for j in range(CONV - 1):
        acc = acc + _shift_down(x, halo, CONV - 1 - j) * w[j:j + 1, :]
    return acc


_Q_SCALE = DQK ** -0.5


def _qscale_row():
    lane = lax.broadcasted_iota(jnp.int32, (1, D), 1)
    return jnp.where(lane < MLH * DQK, _Q_SCALE, 1.0).astype(F32)


def _conv_silu_fwd(proj, conv_w):
    t = proj.shape[0]

    def body(x_ref, halo_ref, w_ref, o_ref):
        halo = jnp.where(pl.program_id(0) > 0, halo_ref[...], 0.0)
        c = _conv_taps(x_ref[...], halo, w_ref[...])
        o_ref[...] = (c * _sigmoid(c) * _qscale_row()).astype(BF16)

    return pl.pallas_call(
        body, name="conv_silu_fwd", grid=(t // TOK_TILE,),
        in_specs=[_tok(D, C_QK // D), _halo_prev(D, C_QK // D), _rep((CONV, D))], out_specs=_tok(D),
        out_shape=jax.ShapeDtypeStruct((t, D), BF16), compiler_params=_params(),
    )(proj, proj, conv_w)


def _conv_silu_bwd_a(proj, conv_w, dqk):
    t = proj.shape[0]

    def body(x_ref, halo_ref, w_ref, d_ref, dc_ref, dw_ref):
        @pl.when(pl.program_id(0) == 0)
        def _():
            dw_ref[...] = jnp.zeros_like(dw_ref)

        halo = jnp.where(pl.program_id(0) > 0, halo_ref[...], 0.0)
        x = x_ref[...]
        c = _conv_taps(x, halo, w_ref[...])
        s = _sigmoid(c)
        dc = d_ref[...] * _qscale_row() * (s * (1.0 + c * (1.0 - s)))
        dc_ref[...] = dc
        dcb, xb, halo_b = _bf(dc), _bf(x), _bf(halo)
        for j in range(CONV):
            dw_ref[j:j + 1, :] += jnp.sum(dcb * _shift_down(xb, halo_b, CONV - 1 - j), axis=0, keepdims=True)

    return pl.pallas_call(
        body, name="conv_silu_bwd_a", grid=(t // TOK_TILE,),
        in_specs=[_tok(D, C_QK // D), _halo_prev(D, C_QK // D), _rep((CONV, D)), _tok(D)],
        out_specs=[_tok(D), _rep((CONV, D))],
        out_shape=[jax.ShapeDtypeStruct((t, D), F32), jax.ShapeDtypeStruct((CONV, D), F32)],
        compiler_params=_params(),
    )(proj, proj, conv_w, dqk)


def _conv_silu_bwd_b(dc, conv_w, dproj):
    t = dc.shape[0]
    nt = t // TOK_TILE

    def body(dc_ref, halo_ref, w_ref, _, dx_ref):
        halo = _bf(jnp.where(pl.program_id(0) < nt - 1, halo_ref[...], 0.0))
        dcv = _bf(dc_ref[...])
        w = _bf(w_ref[...])
        acc = dcv * w[CONV - 1:CONV, :]
        for j in range(CONV - 1):
            acc = acc + _shift_up(dcv, halo, CONV - 1 - j) * w[j:j + 1, :]
        dx_ref[...] = acc.astype(BF16)

    return pl.pallas_call(
        body, name="conv_silu_bwd_b", grid=(nt,), in_specs=[_tok(D), _halo_next(D, nt), _rep((CONV, D)), _ANY],
        out_specs=_tok(D, C_QK // D), out_shape=jax.ShapeDtypeStruct((t, NP), BF16),
        input_output_aliases={3: 0}, compiler_params=_params(),
    )(dc, dc, conv_w, dproj)


def _gates_fwd(pre_rows, bias_col):
    t = pre_rows.shape[1]

    def body(p_ref, b_ref, g_ref, s_ref):
        z = p_ref[...] + b_ref[...]
        lf = jnp.minimum(z, 0.0) - jnp.log(1.0 + jnp.exp(-jnp.abs(z)))
        lane = lax.broadcasted_iota(jnp.int32, z.shape, 1) % CHUNK
        cum = lf
        s = 1
        while s < CHUNK:
            cum = cum + jnp.where(lane >= s, pltpu.roll(cum, s, 1), 0.0)
            s *= 2
        sub = lax.broadcasted_iota(jnp.int32, z.shape, 0)
        g_ref[...] = jnp.where(sub < MLH, z, cum)
        s_ref[...] = _sigmoid(-z)

    return pl.pallas_call(
        body, name="gates_fwd",
        out_shape=[jax.ShapeDtypeStruct((8, t), F32), jax.ShapeDtypeStruct((8, t), F32)],
        compiler_params=_params(),
    )(pre_rows, bias_col)


def _chunk_terms(grow, gcol, h, m0):
    i_row, b_row = grow[h:h + 1, :], grow[MLH + h:MLH + h + 1, :]
    i_col, b_col = gcol[:, h:h + 1], gcol[:, MLH + h:MLH + h + 1]
    b_last = b_row[:, CHUNK - 1:CHUNK]
    tt = lax.broadcasted_iota(jnp.int32, (CHUNK, CHUNK), 0)
    ss = lax.broadcasted_iota(jnp.int32, (CHUNK, CHUNK), 1)
    log_d = jnp.where(tt >= ss, b_col - b_row + i_row, -jnp.inf)
    m_t = jnp.maximum(b_col + m0, jnp.max(log_d, axis=1, keepdims=True))
    dm = jnp.exp(log_d - m_t)
    wi = jnp.exp(b_col + m0 - m_t)
    m1 = jnp.maximum(b_last + m0, jnp.max(b_last - b_row + i_row, axis=1, keepdims=True))
    ws = jnp.exp(b_last - b_col + i_col - m1)
    dec = jnp.exp(b_last + m0 - m1)
    return dm, wi, m_t, ws, dec, m1


def _mlstm_fwd(qk, proj, grow, gcol):
    t = qk.shape[0]
    nc = t // CHUNK

    def body(qk_ref, v_ref, grow_ref, gcol_ref, h_ref, cs_ref, st_ref, c_scr, st_scr):
        @pl.when(pl.program_id(0) == 0)
        def _():
            c_scr[...] = jnp.zeros_like(c_scr)
            st_scr[...] = jnp.zeros_like(st_scr)

        grow_v, gcol_v = grow_ref[...], gcol_ref[...]
        for h in range(MLH):
            q = qk_ref[:, h * DQK:(h + 1) * DQK]
            k = qk_ref[:, MLH * DQK + h * DQK:MLH * DQK + (h + 1) * DQK]
            v = v_ref[:, h * DV:(h + 1) * DV]
            c0 = c_scr[h]
            n0 = st_scr[h, 0:1, :]
            m0 = st_scr[h, 1:2, 0:1]
            cs_ref[0, h] = c0
            st_ref[0, h] = st_scr[h]
            dm, wi, m_t, ws, dec, m1 = _chunk_terms(grow_v, gcol_v, h, m0)
            s = _dot_nt(q, k) * dm
            num = wi * _dot_nt(q, c0.astype(BF16)) + _dot_nn(s.astype(BF16), v.astype(BF16))
            den = wi * jnp.sum(q.astype(F32) * n0, axis=1, keepdims=True) + jnp.sum(s, axis=1, keepdims=True)
            h_ref[:, h * DV:(h + 1) * DV] = num / jnp.maximum(jnp.abs(den), jnp.exp(-m_t))
            c_scr[h] = dec * c0 + _dot_tn((ws * v).astype(BF16), k)
            st_scr[h, 0:1, :] = dec * n0 + jnp.sum(ws * k.astype(F32), axis=0, keepdims=True)
            st_scr[h, 1:2, :] = jnp.broadcast_to(m1, (1, DQK))

    return pl.pallas_call(
        body, name="mlstm_fwd", grid=(nc,),
        in_specs=[pl.BlockSpec((CHUNK, D), lambda c: (c, 0)), pl.BlockSpec((CHUNK, D), lambda c: (c, C_V // D)),
                  pl.BlockSpec((8, CHUNK), lambda c: (0, c)), pl.BlockSpec((CHUNK, 8), lambda c: (c, 0))],
        out_specs=[pl.BlockSpec((CHUNK, D), lambda c: (c, 0)),
                   pl.BlockSpec((1, MLH, DV, DQK), lambda c: (c, 0, 0, 0)),
                   pl.BlockSpec((1, MLH, 8, DQK), lambda c: (c, 0, 0, 0))],
        out_shape=[jax.ShapeDtypeStruct((t, D), F32), jax.ShapeDtypeStruct((nc, MLH, DV, DQK), F32),
                   jax.ShapeDtypeStruct((nc, MLH, 8, DQK), F32)],
        scratch_shapes=[pltpu.VMEM((MLH, DV, DQK), F32), pltpu.VMEM((MLH, 8, DQK), F32)],
        compiler_params=_params(dimension_semantics=("arbitrary",)),
    )(qk, proj, grow, gcol)


def _mlstm_bwd(qk, proj, grow, gcol, sneg_col, cs, st, hraw, dh, dproj):
    t = qk.shape[0]
    nc = t // CHUNK

    def rev(c):
        return nc - 1 - c

    def nxt(c):
        return jnp.minimum(nc - c, nc - 1)

    def body(qk_ref, v_ref, grow_ref, gcol_ref, sneg_ref, cs_ref, st_ref, cs1_ref, st1_ref, h_ref, dh_ref, _,
             dqk_ref, dv_ref, dif_ref, dbif_ref, dc_scr, dn_scr):
        @pl.when(pl.program_id(0) == 0)
        def _():
            dc_scr[...] = jnp.zeros_like(dc_scr)
            dn_scr[...] = jnp.zeros_like(dn_scr)
            dbif_ref[...] = jnp.zeros_like(dbif_ref)

        grow_v, gcol_v, sneg = grow_ref[...], gcol_ref[...], sneg_ref[...]
        tt = lax.broadcasted_iota(jnp.int32, (CHUNK, CHUNK), 0)
        ss = lax.broadcasted_iota(jnp.int32, (CHUNK, CHUNK), 1)
        lane8 = lax.broadcasted_iota(jnp.int32, (CHUNK, 8), 1)
        dif = jnp.zeros((CHUNK, 8), F32)
        for h in range(MLH):
            q = qk_ref[:, h * DQK:(h + 1) * DQK]
            k = qk_ref[:, MLH * DQK + h * DQK:MLH * DQK + (h + 1) * DQK]
            qf, kf = q.astype(F32), k.astype(F32)
            v = v_ref[:, h * DV:(h + 1) * DV]
            vb = v.astype(BF16)
            c0 = cs_ref[0, h]
            n0 = st_ref[0, h, 0:1, :]
            m0 = st_ref[0, h, 1:2, 0:1]
            dc1 = dc_scr[h]
            dn1 = dn_scr[h, 0:1, :]
            dm, wi, m_t, ws, dec, _ = _chunk_terms(grow_v, gcol_v, h, m0)
            s = _dot_nt(q, k) * dm
            den = wi * jnp.sum(qf * n0, axis=1, keepdims=True) + jnp.sum(s, axis=1, keepdims=True)
            floor = jnp.exp(-m_t)
            g = jnp.maximum(jnp.abs(den), floor)
            dh_v = dh_ref[:, h * DV:(h + 1) * DV]
            dnum = dh_v / g
            dden = -jnp.sum(dh_v * h_ref[:, h * DV:(h + 1) * DV], axis=1, keepdims=True) / g
            dden = jnp.where(jnp.abs(den) > floor, dden * jnp.sign(den), 0.0)
            dnum_b = dnum.astype(BF16)
            da = ((_dot_nt(dnum_b, vb) + dden) * dm).astype(BF16)
            dc1_b = dc1.astype(BF16)
            dq = _dot_nn(da, k) + wi * (_dot_nn(dnum_b, c0.astype(BF16)) + dden * n0)
            dk = _dot_tn(da, q) + ws * (_dot_nn(vb, dc1_b) + dn1)
            dv = _dot_tn(s.astype(BF16), dnum_b) + ws * _dot_nt(k, dc1_b)
            dqk_ref[:, h * DQK:(h + 1) * DQK] = dq
            dqk_ref[:, MLH * DQK + h * DQK:MLH * DQK + (h + 1) * DQK] = dk
            dv_ref[:, h * DV:(h + 1) * DV] = dv.astype(BF16)
            rk = jnp.sum(kf * dk, axis=1, keepdims=True)
            df = jnp.sum(qf * dq, axis=1, keepdims=True) - rk
            df_row = jnp.sum(jnp.where(tt == ss, df, 0.0), axis=0, keepdims=True)
            suffix = jnp.sum(jnp.where(ss >= tt, df_row, 0.0), axis=1, keepdims=True)
            cross = (jnp.sum(jnp.sum(dc1 * cs1_ref[0, h], axis=1, keepdims=True), axis=0, keepdims=True)
                     + jnp.sum(dn1 * st1_ref[0, h, 0:1, :], axis=1, keepdims=True))
            dpf = (suffix + cross) * sneg[:, MLH + h:MLH + h + 1]
            dif = dif + jnp.where(lane8 == h, rk, 0.0) + jnp.where(lane8 == MLH + h, dpf, 0.0)
            dc_scr[h] = dec * dc1 + _dot_tn((wi * dnum).astype(BF16), q)
            dn_scr[h, 0:1, :] = dec * dn1 + jnp.sum(wi * dden * qf, axis=0, keepdims=True)
        dif_ref[...] = dif
        dbif_ref[...] += jnp.sum(dif, axis=0, keepdims=True)

    return pl.pallas_call(
        body, name="mlstm_bwd", grid=(nc,),
        in_specs=[pl.BlockSpec((CHUNK, D), lambda c: (rev(c), 0)),
                  pl.BlockSpec((CHUNK, D), lambda c: (rev(c), C_V // D)),
                  pl.BlockSpec((8, CHUNK), lambda c: (0, rev(c))),
                  pl.BlockSpec((CHUNK, 8), lambda c: (rev(c), 0)),
                  pl.BlockSpec((CHUNK, 8), lambda c: (rev(c), 0)),
                  pl.BlockSpec((1, MLH, DV, DQK), lambda c: (rev(c), 0, 0, 0)),
                  pl.BlockSpec((1, MLH, 8, DQK), lambda c: (rev(c), 0, 0, 0)),
                  pl.BlockSpec((1, MLH, DV, DQK), lambda c: (nxt(c), 0, 0, 0)),
                  pl.BlockSpec((1, MLH, 8, DQK), lambda c: (nxt(c), 0, 0, 0)),
                  pl.BlockSpec((CHUNK, D), lambda c: (rev(c), 0)),
                  pl.BlockSpec((CHUNK, D), lambda c: (rev(c), 0)), _ANY],
        out_specs=[pl.BlockSpec((CHUNK, D), lambda c: (rev(c), 0)),
                   pl.BlockSpec((CHUNK, D), lambda c: (rev(c), C_V // D)),
                   pl.BlockSpec((CHUNK, 8), lambda c: (rev(c), 0)),
                   pl.BlockSpec((1, 8), lambda c: (0, 0))],
        out_shape=[jax.ShapeDtypeStruct((t, D), F32), jax.ShapeDtypeStruct((t, NP), BF16),
                   jax.ShapeDtypeStruct((t, 8), F32), jax.ShapeDtypeStruct((1, 8), F32)],
        scratch_shapes=[pltpu.VMEM((MLH, DV, DQK), F32), pltpu.VMEM((MLH, 8, DQK), F32)],
        input_output_aliases={11: 1}, compiler_params=_params(dimension_semantics=("arbitrary",)),
    )(qk, proj, grow, gcol, sneg_col, cs, st, cs, st, hraw, dh, dproj)


def _ya_fwd(hraw, proj, g):
    t = hraw.shape[0]

    def body(h_ref, o_ref, g_ref, y_ref):
        so = _sigmoid(o_ref[...])
        for h in range(MLH):
            sl = slice(h * DV, (h + 1) * DV)
            xn, _ = _rms(h_ref[:, sl])
            y_ref[:, sl] = (so[:, sl] * xn * g_ref[:, sl]).astype(BF16)

    return pl.pallas_call(
        body, name="ya_fwd", grid=(t // TOK_TILE,), in_specs=[_tok(D), _tok(D, C_O // D), _rep((1, D))],
        out_specs=_tok(D), out_shape=jax.ShapeDtypeStruct((t, D), BF16), compiler_params=_params(),
    )(hraw, proj, g)


_ANY = pl.BlockSpec(memory_space=pl.ANY)


def _ya_bwd(hraw, proj, g, dya, dproj):
    t = hraw.shape[0]

    def body(h_ref, o_ref, g_ref, dy_ref, _, dh_ref, do_ref, dg_ref):
        @pl.when(pl.program_id(0) == 0)
        def _():
            dg_ref[...] = jnp.zeros_like(dg_ref)

        so = _sigmoid(o_ref[...])
        dy = dy_ref[...]
        for h in range(MLH):
            sl = slice(h * DV, (h + 1) * DV)
            xn, rstd = _rms(h_ref[:, sl])
            gs = g_ref[:, sl]
            do_ref[:, sl] = (dy[:, sl] * xn * gs * so[:, sl] * (1.0 - so[:, sl])).astype(BF16)
            dhn = dy[:, sl] * so[:, sl]
            dg_ref[:, sl] += jnp.sum(dhn * xn, axis=0, keepdims=True)
            dh_ref[:, sl] = _rms_bwd(xn, rstd, dhn * gs)

    return pl.pallas_call(
        body, name="ya_bwd", grid=(t // TOK_TILE,),
        in_specs=[_tok(D), _tok(D, C_O // D), _rep((1, D)), _tok(D), _ANY],
        out_specs=[_tok(D), _tok(D, C_O // D), _rep((1, D))],
        out_shape=[jax.ShapeDtypeStruct((t, D), F32), jax.ShapeDtypeStruct((t, NP), BF16),
                   jax.ShapeDtypeStruct((1, D), F32)],
        input_output_aliases={4: 1}, compiler_params=_params(),
    )(hraw, proj, g, dya, dproj)


_SW_SCALE = HD ** -0.5
_KVB = C_KV // (2 * SWKV * HD)


def _swa_mask(n):
    ki = lax.broadcasted_iota(jnp.int32, (2 * WIN, SWG * WIN), 0)
    qi = lax.broadcasted_iota(jnp.int32, (2 * WIN, SWG * WIN), 1) % WIN
    return (ki > qi) & (ki <= qi + WIN) & ((n > 0) | (ki >= WIN))


def _group_rows(x_ref, hk):
    return jnp.concatenate([x_ref[:, (hk * SWG + g) * HD:(hk * SWG + g + 1) * HD] for g in range(SWG)], axis=0)


def _group_lanes(x_ref, hk):
    return jnp.concatenate([x_ref[hk * SWG + g:hk * SWG + g + 1, :] for g in range(SWG)], axis=1)


def _sink_lanes(sink_ref, hk):
    return jnp.concatenate([jnp.broadcast_to(sink_ref[:, hk * SWG + g:hk * SWG + g + 1], (1, WIN))
                            for g in range(SWG)], axis=1)


def _swa_fwd(proj, sinks):
    t = proj.shape[0]
    nb = t // WIN

    def body(q_ref, kvc_ref, kvp_ref, sink_ref, y_ref, lse_ref):
        valid = _swa_mask(pl.program_id(0))
        for hk in range(SWKV):
            ks = slice(hk * HD, (hk + 1) * HD)
            vs = slice(SWKV * HD + hk * HD, SWKV * HD + (hk + 1) * HD)
            kb = jnp.concatenate([kvp_ref[:, ks], kvc_ref[:, ks]], axis=0).astype(BF16)
            vb = jnp.concatenate([kvp_ref[:, vs], kvc_ref[:, vs]], axis=0).astype(BF16)
            q4 = _group_rows(q_ref, hk).astype(BF16)
            sink = _sink_lanes(sink_ref, hk)
            logits = jnp.where(valid, _dot_nt(kb, q4) * _SW_SCALE, -jnp.inf)
            m = jnp.maximum(jnp.max(logits, axis=0, keepdims=True), sink)
            p = jnp.exp(logits - m)
            denom = jnp.sum(p, axis=0, keepdims=True) + jnp.exp(sink - m)
            y4 = _dot_tn((p / denom).astype(BF16), vb).astype(BF16)
            lse4 = m + jnp.log(denom)
            for g in range(SWG):
                hq = hk * SWG + g
                y_ref[:, hq * HD:(hq + 1) * HD] = y4[g * WIN:(g + 1) * WIN]
                lse_ref[hq:hq + 1, :] = lse4[:, g * WIN:(g + 1) * WIN]

    return pl.pallas_call(
        body, name="swa_fwd", grid=(nb,),
        in_specs=[pl.BlockSpec((WIN, D), lambda n: (n, C_QSW // D)),
                  pl.BlockSpec((WIN, 512), lambda n: (n, _KVB)),
                  pl.BlockSpec((WIN, 512), lambda n: (jnp.maximum(n - 1, 0), _KVB)),
                  pl.BlockSpec((1, SWH), lambda n: (0, 0))],
        out_specs=[pl.BlockSpec((WIN, D), lambda n: (n, 0)), pl.BlockSpec((SWH, WIN), lambda n: (0, n))],
        out_shape=[jax.ShapeDtypeStruct((t, D), BF16), jax.ShapeDtypeStruct((SWH, t), F32)],
        compiler_params=_params(),
    )(proj, proj, proj, sinks)


def _swa_bwd(proj, sinks, lse, dyb, dproj):
    t = proj.shape[0]
    nb = t // WIN

    def body(q_ref, kvc_ref, kvp_ref, sink_ref, lse_ref, dy_ref, _, dq_ref, dself_ref, dprev_ref, ds_ref):
        @pl.when(pl.program_id(0) == 0)
        def _():
            ds_ref[...] = jnp.zeros_like(ds_ref)

        valid = _swa_mask(pl.program_id(0))
        for hk in range(SWKV):
            ks = slice(hk * HD, (hk + 1) * HD)
            vs = slice(SWKV * HD + hk * HD, SWKV * HD + (hk + 1) * HD)
            kb = jnp.concatenate([kvp_ref[:, ks], kvc_ref[:, ks]], axis=0).astype(BF16)
            vb = jnp.concatenate([kvp_ref[:, vs], kvc_ref[:, vs]], axis=0).astype(BF16)
            dy4 = _group_rows(dy_ref, hk)
            qb, dyb_ = _group_rows(q_ref, hk).astype(BF16), dy4.astype(BF16)
            lse4 = _group_lanes(lse_ref, hk)
            logits = jnp.where(valid, _dot_nt(kb, qb) * _SW_SCALE, -jnp.inf)
            p = jnp.exp(logits - lse4)
            dpt = _dot_nt(vb, dyb_)
            delta = jnp.sum(p * dpt, axis=0, keepdims=True)
            dsm = (p * (dpt - delta)).astype(BF16)
            dq4 = (_dot_tn(dsm, kb) * _SW_SCALE).astype(BF16)
            dkb = _dot_nn(dsm, qb) * _SW_SCALE
            dvb = _dot_nn(p.astype(BF16), dyb_)
            dsink4 = jnp.exp(_sink_lanes(sink_ref, hk) - lse4) * delta
            for g in range(SWG):
                hq = hk * SWG + g
                dq_ref[:, hq * HD:(hq + 1) * HD] = dq4[g * WIN:(g + 1) * WIN]
                ds_ref[:, hq:hq + 1] += -jnp.sum(dsink4[:, g * WIN:(g + 1) * WIN], axis=1, keepdims=True)
            dprev_ref[:, ks] = dkb[:WIN]
            dself_ref[:, ks] = dkb[WIN:]
            dprev_ref[:, vs] = dvb[:WIN]
            dself_ref[:, vs] = dvb[WIN:]

    return pl.pallas_call(
        body, name="swa_bwd", grid=(nb,),
        in_specs=[pl.BlockSpec((WIN, D), lambda n: (n, C_QSW // D)),
                  pl.BlockSpec((WIN, 512), lambda n: (n, _KVB)),
                  pl.BlockSpec((WIN, 512), lambda n: (jnp.maximum(n - 1, 0), _KVB)),
                  pl.BlockSpec((1, SWH), lambda n: (0, 0)),
                  pl.BlockSpec((SWH, WIN), lambda n: (0, n)),
                  pl.BlockSpec((WIN, D), lambda n: (n, 0)), _ANY],
        out_specs=[pl.BlockSpec((WIN, D), lambda n: (n, C_QSW // D)), pl.BlockSpec((WIN, 512), lambda n: (n, 0)),
                   pl.BlockSpec((WIN, 512), lambda n: (n, 0)), pl.BlockSpec((1, SWH), lambda n: (0, 0))],
        out_shape=[jax.ShapeDtypeStruct((t, NP), BF16), jax.ShapeDtypeStruct((t, 512), F32),
                   jax.ShapeDtypeStruct((t, 512), F32), jax.ShapeDtypeStruct((1, SWH), F32)],
        input_output_aliases={6: 0}, compiler_params=_params(),
    )(proj, proj, proj, sinks, lse, dyb, dproj)


def _kv_combine(dself, dprev, dif, dproj):
    t = dself.shape[0]
    nb = t // WIN

    def body(a_ref, b_ref, dif_ref, _, o_ref):
        nxt = jnp.where(pl.program_id(0) < nb - 1, b_ref[...], 0.0)
        o_ref[:, 0:512] = (a_ref[...] + nxt).astype(BF16)
        lane = lax.broadcasted_iota(jnp.int32, (WIN, 128), 1)
        dif_v = dif_ref[...]
        first = jnp.zeros((WIN, 128), F32)
        for col in range(8):
            first = first + jnp.where(lane == col, dif_v[:, col:col + 1], 0.0)
        o_ref[:, 512:640] = first.astype(BF16)
        o_ref[:, 640:512 + IFW] = jnp.zeros((WIN, IFW - 128), BF16)

    return pl.pallas_call(
        body, name="kv_combine", grid=(nb,),
        in_specs=[pl.BlockSpec((WIN, 512), lambda n: (n, 0)),
                  pl.BlockSpec((WIN, 512), lambda n: (jnp.minimum(n + 1, nb - 1), 0)),
                  pl.BlockSpec((WIN, 8), lambda n: (n, 0)), _ANY],
        out_specs=pl.BlockSpec((WIN, 512 + IFW), lambda n: (n, C_KV // (512 + IFW))),
        out_shape=jax.ShapeDtypeStruct((t, NP), BF16), input_output_aliases={3: 0}, compiler_params=_params(),
    )(dself, dprev, dif, dproj)


def _merge_fwd(proj, za, zb):
    t = proj.shape[0]

    def body(ga_ref, gb_ref, za_ref, zb_ref, o_ref):
        o_ref[...] = (_sigmoid(ga_ref[...]) * za_ref[...] + _sigmoid(gb_ref[...]) * zb_ref[...]).astype(BF16)

    return pl.pallas_call(
        body, name="merge_fwd", grid=(t // TOK_TILE,),
        in_specs=[_tok(D, C_GA // D), _tok(D, C_GB // D), _tok(D), _tok(D)], out_specs=_tok(D),
        out_shape=jax.ShapeDtypeStruct((t, D), BF16), compiler_params=_params(),
    )(proj, proj, za, zb)


def _merge_bwd(proj, za, zb, dmerged):
    t = proj.shape[0]

    def body(ga_ref, gb_ref, za_ref, zb_ref, dm_ref, dza_ref, dzb_ref, dp_ref):
        dm = dm_ref[...]
        sa, sb = _sigmoid(ga_ref[...]), _sigmoid(gb_ref[...])
        dza_ref[...] = (dm * sa).astype(BF16)
        dzb_ref[...] = (dm * sb).astype(BF16)
        dp_ref[:, 0:D] = (dm * za_ref[...] * sa * (1.0 - sa)).astype(BF16)
        dp_ref[:, D:2 * D] = (dm * zb_ref[...] * sb * (1.0 - sb)).astype(BF16)

    return pl.pallas_call(
        body, name="merge_bwd", grid=(t // TOK_TILE,),
        in_specs=[_tok(D, C_GA // D), _tok(D, C_GB // D), _tok(D), _tok(D), _tok(D)],
        out_specs=[_tok(D), _tok(D), _tok(2 * D, C_GA // (2 * D))],
        out_shape=[jax.ShapeDtypeStruct((t, D), BF16)] * 2 + [jax.ShapeDtypeStruct((t, NP), BF16)],
        compiler_params=_params(),
    )(proj, proj, za, zb, dmerged)


def _act_fwd(u):
    t = u.shape[0]

    def body(u_ref, a_ref):
        r = jnp.maximum(u_ref[...], 0.0)
        a_ref[...] = (r * r).astype(BF16)

    return pl.pallas_call(
        body, name="act_fwd", grid=(t // TOK_TILE,), in_specs=[_tok(DFF)], out_specs=_tok(DFF),
        out_shape=jax.ShapeDtypeStruct((t, DFF), BF16), compiler_params=_params(),
    )(u)


def _act_bwd(u, da):
    t = u.shape[0]

    def body(u_ref, da_ref, du_ref):
        du_ref[...] = (da_ref[...] * 2.0 * jnp.maximum(u_ref[...], 0.0)).astype(BF16)

    return pl.pallas_call(
        body, name="act_bwd", grid=(t // TOK_TILE,), in_specs=[_tok(DFF), _tok(DFF)], out_specs=_tok(DFF),
        out_shape=jax.ShapeDtypeStruct((t, DFF), BF16), compiler_params=_params(),
    )(u, da)


def _ple_final(x2, gpre, pp, target, gf):
    t = x2.shape[0]

    def body(x_ref, gp_ref, pp_ref, t_ref, g_ref, loss_ref, dg_ref, dx_ref, dpp_ref, dgp_ref):
        @pl.when(pl.program_id(0) == 0)
        def _():
            loss_ref[...] = jnp.zeros_like(loss_ref)
            dg_ref[...] = jnp.zeros_like(dg_ref)

        gate = _sigmoid(gp_ref[...])
        pp_v = pp_ref[...]
        x3 = x_ref[...] + gate * pp_v
        xn, rstd = _rms(x3)
        gf_v = g_ref[...]
        err = xn * gf_v - t_ref[...]
        loss_ref[...] += (0.5 / D) * jnp.sum(jnp.sum(err * err, axis=1, keepdims=True), axis=0, keepdims=True)
        dy = err * (1.0 / D)
        dg_ref[...] += jnp.sum(dy * xn, axis=0, keepdims=True)
        dx3 = _rms_bwd(xn, rstd, dy * gf_v)
        dx_ref[...] = dx3
        dpp_ref[...] = (dx3 * gate).astype(BF16)
        dgp_ref[...] = (dx3 * pp_v * gate * (1.0 - gate)).astype(BF16)

    return pl.pallas_call(
        body, name="ple_final", grid=(t // TOK_TILE,),
        in_specs=[_tok(D), _tok(D), _tok(D), _tok(D), _rep((1, D))],
        out_specs=[_rep((1, 1)), _rep((1, D)), _tok(D), _tok(D), _tok(D)],
        out_shape=[jax.ShapeDtypeStruct((1, 1), F32), jax.ShapeDtypeStruct((1, D), F32),
                   jax.ShapeDtypeStruct((t, D), F32), jax.ShapeDtypeStruct((t, D), BF16),
                   jax.ShapeDtypeStruct((t, D), BF16)],
        compiler_params=_params(),
    )(x2, gpre, pp, target, gf)


def _win_pad(w):
    zeros = jnp.zeros((w.shape[0], IFW - 8), w.dtype)
    return jnp.concatenate([w[:, 0:3072], w[:, 3080:4104], w[:, 4616:6664], w[:, 4104:4616], w[:, 3072:3080], zeros],
                           axis=1)


def _win_unpad(wp):
    return jnp.concatenate([wp[:, 0:3072], wp[:, C_IF:C_IF + 8], wp[:, C_QSW:C_QSW + 1024], wp[:, C_KV:C_KV + 512],
                            wp[:, C_GA:C_GA + 2048]], axis=1)


def _local_step(x, p, target, w):
    t = x.shape[0]
    pb = p.astype(BF16)

    h0 = _norm_fwd(x, w["norm_mix_g"], "norm_mix")
    proj = _mm(h0, w["w_in"], "nn", F32, "mm_in")
    qk = _conv_silu_fwd(proj, w["conv_qk"])
    grow, sneg_row = _gates_fwd(proj[:, C_IF:C_IF + 8].T, w["b_if"].reshape(8, 1))
    gcol, sneg_col = grow.T, sneg_row.T
    hraw, cs, st = _mlstm_fwd(qk, proj, grow, gcol)
    ya = _ya_fwd(hraw, proj, w["mlstm_norm_g"])
    yb, lse = _swa_fwd(proj, w["sinks"])
    za = _mm(ya, w["w_branch_a"], "nn", F32, "mm_branch_a")
    zb = _mm(yb, w["w_branch_b"], "nn", F32, "mm_branch_b")
    merged = _merge_fwd(proj, za, zb)
    o1 = _mm(merged, w["w_out"], "nn", F32, "mm_out")
    x1, hn1 = _resid_norm_fwd(x, o1, w["norm_mlp_g"], "resid_norm_mlp")
    u = _mm(hn1, w["w_up"], "nn", F32, "mm_up")
    act = _act_fwd(u)
    o2 = _mm(act, w["w_down"], "nn", F32, "mm_down")
    x2, hn2 = _resid_norm_fwd(x1, o2, w["norm_ple_g"], "resid_norm_ple")
    gpre = _mm(hn2, w["w_ple_gate"], "nn", F32, "mm_ple_gate")
    pp = _mm(pb, w["w_ple_proj"], "nn", F32, "mm_ple_proj")
    loss, d_final_g, dx3, dpp, dgpre = _ple_final(x2, gpre, pp, target, w["final_norm_g"])

    g = {"final_norm_g": d_final_g}
    g["w_ple_proj"] = _mm(pb, dpp, "tn", F32, "mm_d_ple_proj", out_chunks=4)
    g["w_ple_gate"] = _mm(hn2, dgpre, "tn", F32, "mm_d_ple_gate")
    dhn2 = _mm(dgpre, w["w_ple_gate"], "nt", F32, "mm_dhn2")
    dx2, dx2b, g["norm_ple_g"] = _norm_bwd_resid(x2, w["norm_ple_g"], dhn2, dx3, "norm_bwd_ple")
    g["w_down"] = _mm(act, dx2b, "tn", F32, "mm_d_down")
    da = _mm(dx2b, w["w_down"], "nt", F32, "mm_da")
    du = _act_bwd(u, da)
    g["w_up"] = _mm(hn1, du, "tn", F32, "mm_d_up", out_chunks=4)
    dhn1 = _mm(du, w["w_up"], "nt", F32, "mm_dhn1")
    dx1, dx1b, g["norm_mlp_g"] = _norm_bwd_resid(x1, w["norm_mlp_g"], dhn1, dx2, "norm_bwd_mlp")
    g["w_out"] = _mm(merged, dx1b, "tn", F32, "mm_d_out")
    dmerged = _mm(dx1b, w["w_out"], "nt", F32, "mm_dmerged")
    dza, dzb, dproj = _merge_bwd(proj, za, zb, dmerged)
    g["w_branch_a"] = _mm(ya, dza, "tn", F32, "mm_d_branch_a")
    g["w_branch_b"] = _mm(yb, dzb, "tn", F32, "mm_d_branch_b")
    dya = _mm(dza, w["w_branch_a"], "nt", F32, "mm_dya")
    dyb = _mm(dzb, w["w_branch_b"], "nt", F32, "mm_dyb")
    dhraw, dproj, g["mlstm_norm_g"] = _ya_bwd(hraw, proj, w["mlstm_norm_g"], dya, dproj)
    dqk, dproj, dif, g["b_if"] = _mlstm_bwd(qk, proj, grow, gcol, sneg_col, cs, st, hraw, dhraw, dproj)
    dc, g["conv_qk"] = _conv_silu_bwd_a(proj, w["conv_qk"], dqk)
    dproj = _conv_silu_bwd_b(dc, w["conv_qk"], dproj)
    dproj, dkv_self, dkv_prev, g["sinks"] = _swa_bwd(proj, w["sinks"], lse, dyb, dproj)
    dproj = _kv_combine(dkv_self, dkv_prev, dif, dproj)
    g["w_in"] = _mm(h0, dproj, "tn", F32, "mm_d_in")
    dh0 = _mm(dproj, w["w_in"], "nt", F32, "mm_dh0")
    grad_x, _, g["norm_mix_g"] = _norm_bwd_resid(x, w["norm_mix_g"], dh0, dx1, "norm_bwd_mix")
    return loss, grad_x, g


_W4 = ("w_branch_a", "w_branch_b", "w_out", "w_ple_gate")
_SHARDED_NAMES = ("w_in", "w_up", "w_down", "w_ple_proj", "conv_qk") + _W4
_SMALL_ROWS = 16
_CONV_ROW = 8


def _group(s):
    return [s["w_in"], jnp.concatenate([s[n] for n in _W4], axis=0), s["w_up"], s["w_down"], s["w_ple_proj"]]


def _ungroup(arrs):
    out = {"w_in": arrs[0], "w_up": arrs[2], "w_down": arrs[3], "w_ple_proj": arrs[4]}
    rows = arrs[1].shape[0] // len(_W4)
    for i, n in enumerate(_W4):
        out[n] = arrs[1][i * rows:(i + 1) * rows]
    return out


def _rows_tile(rows):
    return 256 if rows % 256 == 0 else rows


_SMALL = ("norm_mix_g", "mlstm_norm_g", "norm_mlp_g", "norm_ple_g", "final_norm_g")


def _pack_small(vals, extra=None, conv=None):
    rows = [vals[n].reshape(1, D) for n in _SMALL]
    tail = [vals["b_if"].reshape(1, 8), vals["sinks"].reshape(1, SWH)]
    used = 8 + SWH
    if extra is not None:
        tail.append(extra.reshape(1, 1))
        used += 1
    tail.append(jnp.zeros((1, D - used), F32))
    rows.append(jnp.concatenate(tail, axis=1))
    rows.append(jnp.zeros((_CONV_ROW - len(rows), D), F32))
    rows.append(jnp.zeros((CONV, D), F32) if conv is None else conv)
    rows.append(jnp.zeros((_SMALL_ROWS - _CONV_ROW - CONV, D), F32))
    return jnp.concatenate(rows, axis=0)


def _unpack_small(slab, shapes):
    out = {n: slab[i].reshape(shapes[n]) for i, n in enumerate(_SMALL)}
    out["b_if"] = slab[5, 0:8].reshape(shapes["b_if"])
    out["sinks"] = slab[5, 8:8 + SWH].reshape(shapes["sinks"])
    return out


_MESH = pl.DeviceIdType.MESH
_HBM = pl.BlockSpec(memory_space=pltpu.HBM)
_VMEM = pl.BlockSpec(memory_space=pltpu.VMEM)


def _place():
    x, y, c = lax.axis_index("x"), lax.axis_index("y"), lax.axis_index("c")
    return x, y, c, 2 * x + y


def _chip_peer(x, y, r):
    return (x ^ (r >> 1), y ^ (r & 1))


def _half(ref, which):
    h = ref.shape[-2] // 2
    return pl.ds(which * h, h)


def _allgather_weights(shards, conv):
    n = len(shards)

    def body(*refs):
        ins, conv_ref = refs[:n], refs[n]
        outs, conv_out = refs[n + 1:2 * n + 1], refs[2 * n + 1]
        send_a, recv_a, send_b, recv_b, send_c, recv_c, local_sems = refs[2 * n + 2:]
        x, y, c, j = _place()
        sibling = (x, y, 1 - c)
        local = [pltpu.make_async_copy(ins[k], outs[k].at[j], local_sems.at[k]) for k in range(n)]
        local.append(pltpu.make_async_copy(conv_ref, conv_out.at[j], local_sems.at[n]))
        for cp in local:
            cp.start()

        def copy_a(k, r, chip):
            rows = _half(ins[k], c)
            return pltpu.make_async_remote_copy(
                src_ref=ins[k].at[rows], dst_ref=outs[k].at[chip, rows], send_sem=send_a.at[3 * k + r - 1],
                recv_sem=recv_a.at[3 * k + r - 1], device_id=(*_chip_peer(x, y, r), c), device_id_type=_MESH)

        def copy_b(k, r, chip, which):
            rows = _half(ins[k], which)
            return pltpu.make_async_remote_copy(
                src_ref=outs[k].at[chip, rows], dst_ref=outs[k].at[chip, rows], send_sem=send_b.at[3 * k + r - 1],
                recv_sem=recv_b.at[3 * k + r - 1], device_id=sibling, device_id_type=_MESH)

        def copy_c(r, chip):
            return pltpu.make_async_remote_copy(
                src_ref=conv_ref, dst_ref=conv_out.at[chip], send_sem=send_c.at[r - 1],
                recv_sem=recv_c.at[r - 1], device_id=(*_chip_peer(x, y, r), c), device_id_type=_MESH)

        for k in range(n):
            for r in (1, 2, 3):
                copy_a(k, r, j).start()
        for r in (1, 2, 3):
            copy_c(r, j).start()
        for k in range(n):
            for r in (1, 2, 3):
                copy_a(k, r, j ^ r).wait_recv()
                copy_b(k, r, j ^ r, c).start()
        for k in range(n):
            for r in (1, 2, 3):
                copy_b(k, r, j ^ r, 1 - c).wait_recv()
        for r in (1, 2, 3):
            copy_c(r, j ^ r).wait_recv()
        for k in range(n):
            for r in (1, 2, 3):
                copy_a(k, r, j).wait_send()
                copy_b(k, r, j ^ r, c).wait_send()
        for r in (1, 2, 3):
            copy_c(r, j).wait_send()
        for cp in local:
            cp.wait()

    return pl.pallas_call(
        body, name="allgather_weights",
        out_shape=[jax.ShapeDtypeStruct((4,) + s.shape, s.dtype) for s in shards]
        + [jax.ShapeDtypeStruct((4,) + conv.shape, F32)],
        in_specs=[_HBM] * (n + 1), out_specs=[_HBM] * (n + 1),
        scratch_shapes=[pltpu.SemaphoreType.DMA((3 * n,))] * 4 + [pltpu.SemaphoreType.DMA((3,))] * 2
        + [pltpu.SemaphoreType.DMA((n + 1,))],
    )(*shards, conv)


def _pair_exchange(gs):
    n = len(gs)

    def body(*refs):
        ins, outs, send_sems, recv_sems = refs[:n], refs[n:2 * n], refs[2 * n], refs[2 * n + 1]
        x, y, c, _ = _place()
        cps = [pltpu.make_async_remote_copy(
            src_ref=ins[k].at[:, _half(ins[k], 1 - c)], dst_ref=outs[k], send_sem=send_sems.at[k],
            recv_sem=recv_sems.at[k], device_id=(x, y, 1 - c), device_id_type=_MESH) for k in range(n)]
        for cp in cps:
            cp.start()
        for cp in cps:
            cp.wait()

    return pl.pallas_call(
        body, name="pair_exchange",
        out_shape=[jax.ShapeDtypeStruct((4, g.shape[1] // 2, g.shape[2]), F32) for g in gs],
        in_specs=[_HBM] * n, out_specs=[_HBM] * n, scratch_shapes=[pltpu.SemaphoreType.DMA((n,))] * 2,
    )(*gs)


def _pair_sum(g, theirs, c, name):
    _, h, cols = theirs.shape
    tr = _rows_tile(h)
    nb = h // tr

    def body(c_ref, a_ref, b_ref, o_ref, ob_ref):
        s = a_ref[...] + b_ref[...]
        o_ref[...] = s
        ob_ref[...] = s.astype(BF16)

    blk = pl.BlockSpec((1, tr, cols), lambda k, i, c_ref: (k, i, 0))
    return pl.pallas_call(
        body, name=name,
        grid_spec=pltpu.PrefetchScalarGridSpec(
            num_scalar_prefetch=1, grid=(4, nb),
            in_specs=[pl.BlockSpec((1, tr, cols), lambda k, i, c_ref: (k, c_ref[0] * nb + i, 0)), blk],
            out_specs=[blk, blk]),
        out_shape=[jax.ShapeDtypeStruct(theirs.shape, F32), jax.ShapeDtypeStruct(theirs.shape, BF16)],
        compiler_params=_params(),
    )(c.reshape(1).astype(jnp.int32), g, theirs)


def _chip_exchange(ss):
    n = len(ss)

    def body(*refs):
        ins, outs, send_sems, recv_sems = refs[:n], refs[n:2 * n], refs[2 * n], refs[2 * n + 1]
        x, y, c, j = _place()
        cps = [pltpu.make_async_remote_copy(
            src_ref=ins[k].at[j ^ r], dst_ref=outs[k].at[r - 1], send_sem=send_sems.at[3 * k + r - 1],
            recv_sem=recv_sems.at[3 * k + r - 1], device_id=(*_chip_peer(x, y, r), c), device_id_type=_MESH)
            for k in range(n) for r in (1, 2, 3)]
        for cp in cps:
            cp.start()
        for cp in cps:
            cp.wait()

    return pl.pallas_call(
        body, name="chip_exchange", out_shape=[jax.ShapeDtypeStruct((3,) + s.shape[1:], s.dtype) for s in ss],
        in_specs=[_HBM] * n, out_specs=[_HBM] * n, scratch_shapes=[pltpu.SemaphoreType.DMA((3 * n,))] * 2,
    )(*ss)


def _reduce4(own, others, j, c, name):
    _, h, cols = own.shape
    tr = _rows_tile(h)
    nb = h // tr

    def body(idx_ref, s_ref, a0, a1, a2, o_ref):
        o_ref[...] = ((s_ref[0] + a0[0].astype(F32)) + a1[0].astype(F32)) + a2[0].astype(F32)

    def other(r):
        return pl.BlockSpec((1, tr, cols), lambda i, idx_ref: (r, i, 0))

    return pl.pallas_call(
        body, name=name,
        grid_spec=pltpu.PrefetchScalarGridSpec(
            num_scalar_prefetch=1, grid=(nb,),
            in_specs=[pl.BlockSpec((1, tr, cols), lambda i, idx_ref: (idx_ref[0], i, 0)), other(0), other(1), other(2)],
            out_specs=pl.BlockSpec((tr, cols), lambda i, idx_ref: (idx_ref[1] * nb + i, 0))),
        out_shape=jax.ShapeDtypeStruct((2 * h, cols), F32), compiler_params=_params(),
    )(jnp.stack([j, c]).astype(jnp.int32), own, others, others, others)


def _sibling_share(fulls):
    n = len(fulls)

    def body(*refs):
        outs, send_sems, recv_sems = refs[n:2 * n], refs[2 * n], refs[2 * n + 1]
        x, y, c, _ = _place()
        cps = [pltpu.make_async_remote_copy(
            src_ref=outs[k].at[_half(outs[k], c)], dst_ref=outs[k].at[_half(outs[k], c)], send_sem=send_sems.at[k],
            recv_sem=recv_sems.at[k], device_id=(x, y, 1 - c), device_id_type=_MESH) for k in range(n)]
        for cp in cps:
            cp.start()
        for cp in cps:
            cp.wait()

    return pl.pallas_call(
        body, name="sibling_share", out_shape=[jax.ShapeDtypeStruct(f.shape, F32) for f in fulls],
        in_specs=[_HBM] * n, out_specs=[_HBM] * n, input_output_aliases={k: k for k in range(n)},
        scratch_shapes=[pltpu.SemaphoreType.DMA((n,))] * 2,
    )(*fulls)


def _adamw(w, g, m, v):
    m1 = ADAM_B1 * m + (1.0 - ADAM_B1) * g
    v1 = ADAM_B2 * v + (1.0 - ADAM_B2) * (g * g)
    m_hat = m1 / (1.0 - ADAM_B1 ** ADAM_STEP)
    v_hat = v1 / (1.0 - ADAM_B2 ** ADAM_STEP)
    delta = -ADAM_LR * (m_hat / (jnp.sqrt(v_hat) + ADAM_EPS) + ADAM_WD * w)
    return delta, m1, v1


def _adamw_call(w, g, m, v, name):
    rows, cols = w.shape
    tr = _rows_tile(rows)

    def body(w_ref, g_ref, m_ref, v_ref, d_out, m_out, v_out):
        delta, m1, v1 = _adamw(w_ref[...], g_ref[...], m_ref[...], v_ref[...])
        d_out[...] = delta
        m_out[...] = m1
        v_out[...] = v1

    blk = pl.BlockSpec((tr, cols), lambda i: (i, 0))
    return pl.pallas_call(
        body, name=name, grid=(rows // tr,), in_specs=[blk] * 4, out_specs=[blk] * 3,
        out_shape=[jax.ShapeDtypeStruct((rows, cols), F32)] * 3, compiler_params=_params(),
    )(w, g, m, v)


def _small_allreduce(vals):
    def body(v_ref, out_ref, buf, send_sems, recv_sems):
        x, y, c, j = _place()
        me = 2 * j + c
        buf[0] = v_ref[...]

        def copy(r):
            return pltpu.make_async_remote_copy(
                src_ref=v_ref, dst_ref=buf.at[r], send_sem=send_sems.at[r - 1], recv_sem=recv_sems.at[r - 1],
                device_id=(x ^ (r >> 2), y ^ ((r >> 1) & 1), c ^ (r & 1)), device_id_type=_MESH)

        for r in range(1, 8):
            copy(r).start()
        for r in range(1, 8):
            copy(r).wait()
        acc = buf[me ^ 0]
        for d in range(1, 8):
            acc = acc + buf[me ^ d]
        out_ref[...] = acc

    return pl.pallas_call(
        body, name="small_allreduce", out_shape=jax.ShapeDtypeStruct((_SMALL_ROWS, D), F32),
        in_specs=[_VMEM], out_specs=_VMEM,
        scratch_shapes=[pltpu.VMEM((8, _SMALL_ROWS, D), F32), pltpu.SemaphoreType.DMA((7,)),
                        pltpu.SemaphoreType.DMA((7,))],
    )(vals)


_NAMES = ("norm_mix_g", "w_in", "conv_qk", "b_if", "mlstm_norm_g", "sinks", "w_branch_a", "w_branch_b", "w_out",
          "norm_mlp_g", "w_up", "w_down", "norm_ple_g", "w_ple_gate", "w_ple_proj", "final_norm_g")
_GROUP_NAMES = ("w_in", "w4", "w_up", "w_down", "w_ple_proj")


def _step(x, p, target, w, m, v):
    c = lax.axis_index("c")
    j = 2 * lax.axis_index("x") + lax.axis_index("y")

    def shards(d):
        return {n: d[n][0] for n in _SHARDED_NAMES}

    ws = shards(w)
    gathered = _allgather_weights([a.astype(BF16) for a in _group(ws)], ws["conv_qk"])
    w_in_all, w4_all, w_up_all, w_down_all, w_pp_all, conv_all = gathered
    full = {n: w[n] for n in ("norm_mix_g", "mlstm_norm_g", "norm_mlp_g", "norm_ple_g", "b_if", "sinks")}
    full["final_norm_g"] = w["final_norm_g"].reshape(1, D)
    full["w_in"] = _win_pad(jnp.swapaxes(w_in_all, 0, 1).reshape(D, N_IN))
    w4_all = w4_all.reshape(4, len(_W4), D // 4, D)
    for i, n in enumerate(_W4):
        full[n] = w4_all[:, i].reshape(D, D)
    full["w_up"] = w_up_all
    full["w_down"] = w_down_all.reshape(DFF, D)
    full["w_ple_proj"] = w_pp_all
    full["conv_qk"] = jnp.swapaxes(conv_all, 0, 1).reshape(CONV, D)

    loss, grad_x, g = _local_step(x[0], p[0, 0], target[0], full)

    w_in_g = _win_unpad(g["w_in"])
    by_dest = [jnp.swapaxes(w_in_g.reshape(D, 4, N_IN // 4), 0, 1),
               jnp.stack([g[n].reshape(4, D // 4, D) for n in _W4], axis=1).reshape(4, D, D),
               g["w_up"], g["w_down"].reshape(4, DFF // 4, D), g["w_ple_proj"]]
    theirs = _pair_exchange(by_dest)
    sums = [_pair_sum(a, b, c, "pair_sum_" + n) for a, b, n in zip(by_dest, theirs, _GROUP_NAMES)]
    others = _chip_exchange([s[1] for s in sums])
    halves = [_reduce4(s[0], b, j, c, "reduce4_" + n) for s, b, n in zip(sums, others, _GROUP_NAMES)]
    grads = _sibling_share(halves)

    small_g = _small_allreduce(_pack_small(g, extra=loss, conv=g["conv_qk"]))
    conv_g = lax.dynamic_slice(small_g[_CONV_ROW:_CONV_ROW + CONV], (0, j * (D // 4)), (CONV, D // 4))

    ms, vs = shards(m), shards(v)
    upd = [_adamw_call(wa, ga, ma, va, "adamw_" + n)
           for wa, ga, ma, va, n in zip(_group(ws), grads, _group(ms), _group(vs), _GROUP_NAMES)]
    conv_upd = _adamw_call(ws["conv_qk"], conv_g, ms["conv_qk"], vs["conv_qk"], "adamw_conv")
    small_upd = _adamw_call(_pack_small(w), small_g, _pack_small(m), _pack_small(v), "adamw_small")

    shapes = {n: w[n].shape for n in _NAMES}
    res = []
    for k in range(4):
        big = _ungroup(list(grads) if k == 0 else [u[k - 1] for u in upd])
        big["conv_qk"] = conv_g if k == 0 else conv_upd[k - 1]
        leaves = _unpack_small(small_g if k == 0 else small_upd[k - 1], shapes)
        leaves.update({n: a.reshape(shapes[n]) for n, a in big.items()})
        res.append(leaves)

    out = [small_g[5, 8 + SWH], grad_x[None]]
    for k in range(4):
        out += [res[k][n] for n in _NAMES]
    return tuple(out)


def kernel(x, p, norm_mix_g, w_in, conv_qk, b_if, mlstm_norm_g, sinks, w_branch_a, w_branch_b, w_out, norm_mlp_g, w_up, w_down, norm_ple_g, w_ple_gate, w_ple_proj, final_norm_g, loss_target, m_norm_mix_g, m_w_in, m_conv_qk, m_b_if, m_mlstm_norm_g, m_sinks, m_w_branch_a, m_w_branch_b, m_w_out, m_norm_mlp_g, m_w_up, m_w_down, m_norm_ple_g, m_w_ple_gate, m_w_ple_proj, m_final_norm_g, v_norm_mix_g, v_w_in, v_conv_qk, v_b_if, v_mlstm_norm_g, v_sinks, v_w_branch_a, v_w_branch_b, v_w_out, v_norm_mlp_g, v_w_up, v_w_down, v_norm_ple_g, v_w_ple_gate, v_w_ple_proj, v_final_norm_g):
    w = dict(zip(_NAMES, (norm_mix_g, w_in, conv_qk, b_if, mlstm_norm_g, sinks, w_branch_a, w_branch_b, w_out,
                          norm_mlp_g, w_up, w_down, norm_ple_g, w_ple_gate, w_ple_proj, final_norm_g)))
    m = dict(zip(_NAMES, (m_norm_mix_g, m_w_in, m_conv_qk, m_b_if, m_mlstm_norm_g, m_sinks, m_w_branch_a,
                          m_w_branch_b, m_w_out, m_norm_mlp_g, m_w_up, m_w_down, m_norm_ple_g, m_w_ple_gate,
                          m_w_ple_proj, m_final_norm_g)))
    v = dict(zip(_NAMES, (v_norm_mix_g, v_w_in, v_conv_qk, v_b_if, v_mlstm_norm_g, v_sinks, v_w_branch_a,
                          v_w_branch_b, v_w_out, v_norm_mlp_g, v_w_up, v_w_down, v_norm_ple_g, v_w_ple_gate,
                          v_w_ple_proj, v_final_norm_g)))
    return _step(x, p, loss_target, w, m, v)
```

```python
import jax
import jax.numpy as jnp
from jax import lax
from jax.experimental import pallas as pl
from jax.experimental.pallas import tpu as pltpu

F32 = jnp.float32
BF16 = jnp.bfloat16

D = 1024
PLE = 256
MLH = 4
DQK = 128
DV = 256
CONV = 4
CHUNK = 128
SWH = 16
SWKV = 4
SWG = SWH // SWKV
HD = 64
WIN = 128
DFF = 4096
EPS = 1e-6
N_IN = 6664
NP = 7168
C_QK, C_V, C_O, C_QSW, C_GA, C_GB, C_KV, C_IF = 0, 1024, 2048, 3072, 4096, 5120, 6144, 6656
IFW = NP - C_IF

ADAM_LR = 0.001
ADAM_B1 = 0.9
ADAM_B2 = 0.999
ADAM_EPS = 1e-08
ADAM_WD = 0.01
ADAM_STEP = 10

TOK_TILE = 256
VMEM_LIMIT = 48 * 1024 * 1024


def _params(**kw):
    return pltpu.CompilerParams(vmem_limit_bytes=VMEM_LIMIT, **kw)


def _pick(n, cap):
    if n <= cap:
        return n
    t = cap - cap % 128
    while t > 128 and n % t:
        t -= 128
    assert n % t == 0, (n, cap)
    return t


def _dot(a, b, dims):
    return lax.dot_general(a, b, (dims, ((), ())), preferred_element_type=F32)


def _dot_nn(a, b):
    return _dot(a, b, ((1,), (0,)))


def _dot_nt(a, b):
    return _dot(a, b, ((1,), (1,)))


def _dot_tn(a, b):
    return _dot(a, b, ((0,), (0,)))


def _sigmoid(x):
    return 1.0 / (1.0 + jnp.exp(-x))


def _mm(a, b, mode, out_dtype, name, out_chunks=1):
    bch = b.shape[0] if b.ndim == 3 else 1
    brows, bcols = b.shape[-2], b.shape[-1] * bch
    if mode == "nn":
        (m, k), (k2, n) = a.shape, (brows, bcols)
    elif mode == "nt":
        (m, k), (n, k2) = a.shape, (brows, bcols)
    else:
        (k, m), (k2, n) = a.shape, (brows, bcols)
    assert k == k2, (a.shape, b.shape, mode)
    n_cap = n // max(out_chunks, 1 if mode == "nt" else bch)
    k_cap = k // bch if mode == "nt" else k
    tm, tn, tk = _pick(m, 1024), _pick(n_cap, 512), _pick(k_cap, 1024)
    nk = k // tk
    if mode == "nn":
        a_spec = pl.BlockSpec((tm, tk), lambda i, j, kk: (i, kk))
        if bch > 1:
            bpc = (n // bch) // tn
            b_spec = pl.BlockSpec((None, tk, tn), lambda i, j, kk: (j // bpc, kk, j % bpc))
        else:
            b_spec = pl.BlockSpec((tk, tn), lambda i, j, kk: (kk, j))
        dot = _dot_nn
    elif mode == "nt":
        a_spec = pl.BlockSpec((tm, tk), lambda i, j, kk: (i, kk))
        if bch > 1:
            bpc = (k // bch) // tk
            b_spec = pl.BlockSpec((None, tn, tk), lambda i, j, kk: (kk // bpc, j, kk % bpc))
        else:
            b_spec = pl.BlockSpec((tn, tk), lambda i, j, kk: (j, kk))
        dot = _dot_nt
    else:
        assert bch == 1
        a_spec = pl.BlockSpec((tk, tm), lambda i, j, kk: (kk, i))
        b_spec = pl.BlockSpec((tk, tn), lambda i, j, kk: (kk, j))
        dot = _dot_tn
    if out_chunks > 1:
        npc = (n // out_chunks) // tn
        out_spec = pl.BlockSpec((None, tm, tn), lambda i, j, kk: (j // npc, i, j % npc))
        out_shape = jax.ShapeDtypeStruct((out_chunks, m, n // out_chunks), out_dtype)
    else:
        out_spec = pl.BlockSpec((tm, tn), lambda i, j, kk: (i, j))
        out_shape = jax.ShapeDtypeStruct((m, n), out_dtype)

    def body(a_ref, b_ref, o_ref, acc_ref):
        kk = pl.program_id(2)

        @pl.when(kk == 0)
        def _():
            acc_ref[...] = jnp.zeros_like(acc_ref)

        acc_ref[...] += dot(a_ref[...], b_ref[...])

        @pl.when(kk == nk - 1)
        def _():
            o_ref[...] = acc_ref[...].astype(out_dtype)

    return pl.pallas_call(
        body, name=name, grid=(m // tm, n // tn, nk),
        in_specs=[a_spec, b_spec], out_specs=out_spec, out_shape=out_shape,
        scratch_shapes=[pltpu.VMEM((tm, tn), F32)],
        compiler_params=_params(dimension_semantics=("parallel", "parallel", "arbitrary")),
    )(a, b)


def _tile(col0=0):
    return lambda tm, tn: pl.BlockSpec((tm, tn), lambda i, j, kk: (i, col0 // tn + j))


def _row():
    return lambda tm, tn: pl.BlockSpec((1, tn), lambda i, j, kk: (0, j))


def _mm_ep(pairs, mode, name, epilogue, ins, outs, tm, tn, aliases=None):
    a0, b0 = pairs[0]
    bch = b0.shape[0] if b0.ndim == 3 else 1
    m, k = a0.shape
    tm = _pick(m, tm)
    n = b0.shape[-1] * bch if mode == "nn" else b0.shape[-2]
    tk = _pick(k // bch if mode == "nt" else k, 1024)
    nk = k // tk
    a_spec = pl.BlockSpec((tm, tk), lambda i, j, kk: (i, kk))
    if mode == "nn":
        dot = _dot_nn
        if bch > 1:
            bpc = (n // bch) // tn
            b_spec = pl.BlockSpec((None, tk, tn), lambda i, j, kk: (j // bpc, kk, j % bpc))
        else:
            b_spec = pl.BlockSpec((tk, tn), lambda i, j, kk: (kk, j))
    else:
        dot = _dot_nt
        if bch > 1:
            bpc = (k // bch) // tk
            b_spec = pl.BlockSpec((None, tn, tk), lambda i, j, kk: (kk // bpc, j, kk % bpc))
        else:
            b_spec = pl.BlockSpec((tn, tk), lambda i, j, kk: (j, kk))
    npair, nin, nout = len(pairs), len(ins), len(outs)

    def body(*refs):
        ab = refs[:2 * npair]
        in_refs = refs[2 * npair:2 * npair + nin]
        out_refs = refs[2 * npair + nin:2 * npair + nin + nout]
        accs = refs[2 * npair + nin + nout:]
        i, j, kk = pl.program_id(0), pl.program_id(1), pl.program_id(2)
        for p in range(npair):
            prod = dot(ab[2 * p][...], ab[2 * p + 1][...])

            @pl.when(kk == 0)
            def _():
                accs[p][...] = prod

            @pl.when(kk > 0)
            def _():
                accs[p][...] += prod

        @pl.when(kk == nk - 1)
        def _():
            epilogue([acc[...] for acc in accs], in_refs, out_refs, i, j)

    operands = [x for pair in pairs for x in pair] + [a for a, _ in ins]
    io_alias = {2 * npair + i: o for i, o in (aliases or {}).items()}
    return pl.pallas_call(
        body, name=name, grid=(m // tm, n // tn, nk),
        in_specs=[a_spec, b_spec] * npair + [mk(tm, tn) for _, mk in ins],
        out_specs=[mk(tm, tn) for _, mk in outs], out_shape=[s for s, _ in outs],
        scratch_shapes=[pltpu.VMEM((tm, tn), F32)] * npair, input_output_aliases=io_alias,
        compiler_params=_params(dimension_semantics=("arbitrary", "arbitrary", "arbitrary")),
    )(*operands)


def _tok(w, j=0):
    return pl.BlockSpec((TOK_TILE, w), lambda i: (i, j))


def _rep(shape):
    return pl.BlockSpec(shape, lambda i: (0,) * len(shape))


def _rms(x):
    rstd = lax.rsqrt(jnp.mean(x * x, axis=-1, keepdims=True) + EPS)
    return x * rstd, rstd


def _rms_bwd(xn, rstd, dxn):
    return rstd * (dxn - xn * jnp.mean(dxn * xn, axis=-1, keepdims=True))


def _norm_fwd(x, g, name):
    t = x.shape[0]

    def body(x_ref, g_ref, h_ref):
        xn, _ = _rms(x_ref[...])
        h_ref[...] = (xn * g_ref[...]).astype(BF16)

    return pl.pallas_call(
        body, name=name, grid=(t // TOK_TILE,), in_specs=[_tok(D), _rep((1, D))], out_specs=_tok(D),
        out_shape=jax.ShapeDtypeStruct((t, D), BF16), compiler_params=_params(),
    )(x, g)


def _resid_norm_fwd(x, o, g, name):
    t = x.shape[0]

    def body(x_ref, o_ref, g_ref, x1_ref, h_ref):
        x1 = x_ref[...] + o_ref[...]
        x1_ref[...] = x1
        xn, _ = _rms(x1)
        h_ref[...] = (xn * g_ref[...]).astype(BF16)

    return pl.pallas_call(
        body, name=name, grid=(t // TOK_TILE,), in_specs=[_tok(D), _tok(D), _rep((1, D))],
        out_specs=[_tok(D), _tok(D)],
        out_shape=[jax.ShapeDtypeStruct((t, D), F32), jax.ShapeDtypeStruct((t, D), BF16)],
        compiler_params=_params(),
    )(x, o, g)


def _norm_bwd_resid(x, g, dh, dres, name):
    t = x.shape[0]

    def body(x_ref, g_ref, dh_ref, dres_ref, dx_ref, dxb_ref, dg_ref):
        @pl.when(pl.program_id(0) == 0)
        def _():
            dg_ref[...] = jnp.zeros_like(dg_ref)

        xn, rstd = _rms(x_ref[...])
        dh_ = dh_ref[...]
        dg_ref[...] += jnp.sum(dh_ * xn, axis=0, keepdims=True)
        dx = dres_ref[...] + _rms_bwd(xn, rstd, dh_ * g_ref[...])
        dx_ref[...] = dx
        dxb_ref[...] = dx.astype(BF16)

    return pl.pallas_call(
        body, name=name, grid=(t // TOK_TILE,), in_specs=[_tok(D), _rep((1, D)), _tok(D), _tok(D)],
        out_specs=[_tok(D), _tok(D), _rep((1, D))],
        out_shape=[jax.ShapeDtypeStruct((t, D), F32), jax.ShapeDtypeStruct((t, D), BF16),
                   jax.ShapeDtypeStruct((1, D), F32)],
        compiler_params=_params(),
    )(x, g, dh, dres)


def _halo_prev(w, j=0, rows=8):
    r = TOK_TILE // rows
    return pl.BlockSpec((rows, w), lambda i: (jnp.maximum(i * r - 1, 0), j))


def _last8(halo_ref):
    return halo_ref[...].astype(F32)[halo_ref.shape[0] - 8:]


def _halo_next(w, nt, j=0):
    r = TOK_TILE // 8
    return pl.BlockSpec((8, w), lambda i: (jnp.minimum((i + 1) * r, nt * r - 1), j))


def _shift_down(x, halo, s):
    if s == 0:
        return x
    r = pltpu.roll(x, s, 0)
    hs = pltpu.roll(halo, s, 0)
    row = lax.broadcasted_iota(jnp.int32, hs.shape, 0)
    top = jnp.where(row < s, hs, r[0:8])
    return jnp.concatenate([top, r[8:]], axis=0)


def _shift_up(x, halo, s):
    if s == 0:
        return x
    n = x.shape[0]
    r = pltpu.roll(x, n - s, 0)
    hs = pltpu.roll(halo, 8 - s, 0)
    row = lax.broadcasted_iota(jnp.int32, hs.shape, 0)
    bot = jnp.where(row >= 8 - s, hs, r[n - 8:])
    return jnp.concatenate([r[:n - 8], bot], axis=0)


def _bf(x):
    return x.astype(BF16).astype(F32)


def _conv_taps(x, halo, w):
    x, halo, w = _bf(x), _bf(halo), _bf(w)
    acc = x * w[CONV - 1:CONV, :]
    for j in range(CONV - 1):
        acc = acc + _shift_down(x, halo, CONV - 1 - j) * w[j:j + 1, :]
    return acc


_Q_SCALE = DQK ** -0.5


def _qscale_row():
    lane = lax.broadcasted_iota(jnp.int32, (1, D), 1)
    return jnp.where(lane < MLH * DQK, _Q_SCALE, 1.0).astype(F32)


def _conv_silu_fwd(proj, conv_w):
    t = proj.shape[0]

    def body(x_ref, halo_ref, w_ref, o_ref):
        halo = jnp.where(pl.program_id(0) > 0, _last8(halo_ref), 0.0)
        c = _conv_taps(x_ref[...].astype(F32), halo, w_ref[...])
        o_ref[...] = (c * _sigmoid(c) * _qscale_row()).astype(BF16)

    return pl.pallas_call(
        body, name="conv_silu_fwd", grid=(t // TOK_TILE,),
        in_specs=[_tok(D, C_QK // D), _halo_prev(D, C_QK // D, 16), _rep((CONV, D))], out_specs=_tok(D),
        out_shape=jax.ShapeDtypeStruct((t, D), BF16), compiler_params=_params(),
    )(proj, proj, conv_w)


def _conv_silu_bwd_a(proj, conv_w, dqk):
    t = proj.shape[0]

    def body(x_ref, halo_ref, w_ref, d_ref, dc_ref, dw_ref):
        @pl.when(pl.program_id(0) == 0)
        def _():
            dw_ref[...] = jnp.zeros_like(dw_ref)

        halo = jnp.where(pl.program_id(0) > 0, _last8(halo_ref), 0.0)
        x = x_ref[...].astype(F32)
        c = _conv_taps(x, halo, w_ref[...])
        s = _sigmoid(c)
        dc = d_ref[...] * _qscale_row() * (s * (1.0 + c * (1.0 - s)))
        dc_ref[...] = dc
        dcb, xb, halo_b = _bf(dc), _bf(x), _bf(halo)
        for j in range(CONV):
            dw_ref[j:j + 1, :] += jnp.sum(dcb * _shift_down(xb, halo_b, CONV - 1 - j), axis=0, keepdims=True)

    return pl.pallas_call(
        body, name="conv_silu_bwd_a", grid=(t // TOK_TILE,),
        in_specs=[_tok(D, C_QK // D), _halo_prev(D, C_QK // D, 16), _rep((CONV, D)), _tok(D)],
        out_specs=[_tok(D), _rep((CONV, D))],
        out_shape=[jax.ShapeDtypeStruct((t, D), F32), jax.ShapeDtypeStruct((CONV, D), F32)],
        compiler_params=_params(),
    )(proj, proj, conv_w, dqk)


def _conv_silu_bwd_b(dc, conv_w, dproj):
    t = dc.shape[0]
    nt = t // TOK_TILE

    def body(dc_ref, halo_ref, w_ref, _, dx_ref):
        halo = _bf(jnp.where(pl.program_id(0) < nt - 1, halo_ref[...], 0.0))
        dcv = _bf(dc_ref[...])
        w = _bf(w_ref[...])
        acc = dcv * w[CONV - 1:CONV, :]
        for j in range(CONV - 1):
            acc = acc + _shift_up(dcv, halo, CONV - 1 - j) * w[j:j + 1, :]
        dx_ref[...] = acc.astype(BF16)

    return pl.pallas_call(
        body, name="conv_silu_bwd_b", grid=(nt,), in_specs=[_tok(D), _halo_next(D, nt), _rep((CONV, D)), _ANY],
        out_specs=_tok(D, C_QK // D), out_shape=jax.ShapeDtypeStruct((t, NP), BF16),
        input_output_aliases={3: 0}, compiler_params=_params(),
    )(dc, dc, conv_w, dproj)


def _gates_fwd(pre_rows, bias_col):
    t = pre_rows.shape[1]

    def body(p_ref, b_ref, g_ref, s_ref):
        z = p_ref[...] + b_ref[...]
        lf = jnp.minimum(z, 0.0) - jnp.log(1.0 + jnp.exp(-jnp.abs(z)))
        lane = lax.broadcasted_iota(jnp.int32, z.shape, 1) % CHUNK
        cum = lf
        s = 1
        while s < CHUNK:
            cum = cum + jnp.where(lane >= s, pltpu.roll(cum, s, 1), 0.0)
            s *= 2
        sub = lax.broadcasted_iota(jnp.int32, z.shape, 0)
        g_ref[...] = jnp.where(sub < MLH, z, cum)
        s_ref[...] = _sigmoid(-z)

    return pl.pallas_call(
        body, name="gates_fwd",
        out_shape=[jax.ShapeDtypeStruct((8, t), F32), jax.ShapeDtypeStruct((8, t), F32)],
        compiler_params=_params(),
    )(pre_rows, bias_col)


def _chunk_terms(grow, gcol, h, m0):
    i_row, b_row = grow[h:h + 1, :], grow[MLH + h:MLH + h + 1, :]
    i_col, b_col = gcol[:, h:h + 1], gcol[:, MLH + h:MLH + h + 1]
    b_last = b_row[:, CHUNK - 1:CHUNK]
    tt = lax.broadcasted_iota(jnp.int32, (CHUNK, CHUNK), 0)
    ss = lax.broadcasted_iota(jnp.int32, (CHUNK, CHUNK), 1)
    log_d = jnp.where(tt >= ss, b_col - b_row + i_row, -jnp.inf)
    m_t = jnp.maximum(b_col + m0, jnp.max(log_d, axis=1, keepdims=True))
    dm = jnp.exp(log_d - m_t)
    wi = jnp.exp(b_col + m0 - m_t)
    m1 = jnp.maximum(b_last + m0, jnp.max(b_last - b_row + i_row, axis=1, keepdims=True))
    ws = jnp.exp(b_last - b_col + i_col - m1)
    dec = jnp.exp(b_last + m0 - m1)
    return dm, wi, m_t, ws, dec, m1


def _mlstm_fwd(qk, proj, grow, gcol):
    t = qk.shape[0]
    nc = t // CHUNK

    def body(qk_ref, v_ref, grow_ref, gcol_ref, h_ref, cs_ref, st_ref, c_scr, st_scr):
        @pl.when(pl.program_id(0) == 0)
        def _():
            c_scr[...] = jnp.zeros_like(c_scr)
            st_scr[...] = jnp.zeros_like(st_scr)

        grow_v, gcol_v = grow_ref[...], gcol_ref[...]
        for h in range(MLH):
            q = qk_ref[:, h * DQK:(h + 1) * DQK]
            k = qk_ref[:, MLH * DQK + h * DQK:MLH * DQK + (h + 1) * DQK]
            v = v_ref[:, h * DV:(h + 1) * DV]
            c0 = c_scr[h]
            n0 = st_scr[h, 0:1, :]
            m0 = st_scr[h, 1:2, 0:1]
            cs_ref[0, h] = c0
            st_ref[0, h] = st_scr[h]
            dm, wi, m_t, ws, dec, m1 = _chunk_terms(grow_v, gcol_v, h, m0)
            s = _dot_nt(q, k) * dm
            num = wi * _dot_nt(q, c0.astype(BF16)) + _dot_nn(s.astype(BF16), v.astype(BF16))
            den = wi * jnp.sum(q.astype(F32) * n0, axis=1, keepdims=True) + jnp.sum(s, axis=1, keepdims=True)
            h_ref[:, h * DV:(h + 1) * DV] = num / jnp.maximum(jnp.abs(den), jnp.exp(-m_t))
            c_scr[h] = dec * c0 + _dot_tn((ws * v).astype(BF16), k)
            st_scr[h, 0:1, :] = dec * n0 + jnp.sum(ws * k.astype(F32), axis=0, keepdims=True)
            st_scr[h, 1:2, :] = jnp.broadcast_to(m1, (1, DQK))

    return pl.pallas_call(
        body, name="mlstm_fwd", grid=(nc,),
        in_specs=[pl.BlockSpec((CHUNK, D), lambda c: (c, 0)), pl.BlockSpec((CHUNK, D), lambda c: (c, C_V // D)),
                  pl.BlockSpec((8, CHUNK), lambda c: (0, c)), pl.BlockSpec((CHUNK, 8), lambda c: (c, 0))],
        out_specs=[pl.BlockSpec((CHUNK, D), lambda c: (c, 0)),
                   pl.BlockSpec((1, MLH, DV, DQK), lambda c: (c, 0, 0, 0)),
                   pl.BlockSpec((1, MLH, 8, DQK), lambda c: (c, 0, 0, 0))],
        out_shape=[jax.ShapeDtypeStruct((t, D), F32), jax.ShapeDtypeStruct((nc, MLH, DV, DQK), F32),
                   jax.ShapeDtypeStruct((nc, MLH, 8, DQK), F32)],
        scratch_shapes=[pltpu.VMEM((MLH, DV, DQK), F32), pltpu.VMEM((MLH, 8, DQK), F32)],
        compiler_params=_params(dimension_semantics=("arbitrary",)),
    )(qk, proj, grow, gcol)


def _mlstm_bwd(qk, proj, grow, gcol, sneg_col, cs, st, hraw, dh, dproj):
    t = qk.shape[0]
    nc = t // CHUNK

    def rev(c):
        return nc - 1 - c

    def nxt(c):
        return jnp.minimum(nc - c, nc - 1)

    def body(qk_ref, v_ref, grow_ref, gcol_ref, sneg_ref, cs_ref, st_ref, cs1_ref, st1_ref, h_ref, dh_ref, _,
             dqk_ref, dv_ref, dif_ref, dbif_ref, dc_scr, dn_scr):
        @pl.when(pl.program_id(0) == 0)
        def _():
            dc_scr[...] = jnp.zeros_like(dc_scr)
            dn_scr[...] = jnp.zeros_like(dn_scr)
            dbif_ref[...] = jnp.zeros_like(dbif_ref)

        grow_v, gcol_v, sneg = grow_ref[...], gcol_ref[...], sneg_ref[...]
        tt = lax.broadcasted_iota(jnp.int32, (CHUNK, CHUNK), 0)
        ss = lax.broadcasted_iota(jnp.int32, (CHUNK, CHUNK), 1)
        lane8 = lax.broadcasted_iota(jnp.int32, (CHUNK, 8), 1)
        dif = jnp.zeros((CHUNK, 8), F32)
        for h in range(MLH):
            q = qk_ref[:, h * DQK:(h + 1) * DQK]
            k = qk_ref[:, MLH * DQK + h * DQK:MLH * DQK + (h + 1) * DQK]
            qf, kf = q.astype(F32), k.astype(F32)
            v = v_ref[:, h * DV:(h + 1) * DV]
            vb = v.astype(BF16)
            c0 = cs_ref[0, h]
            n0 = st_ref[0, h, 0:1, :]
            m0 = st_ref[0, h, 1:2, 0:1]
            dc1 = dc_scr[h]
            dn1 = dn_scr[h, 0:1, :]
            dm, wi, m_t, ws, dec, _ = _chunk_terms(grow_v, gcol_v, h, m0)
            s = _dot_nt(q, k) * dm
            den = wi * jnp.sum(qf * n0, axis=1, keepdims=True) + jnp.sum(s, axis=1, keepdims=True)
            floor = jnp.exp(-m_t)
            g = jnp.maximum(jnp.abs(den), floor)
            dh_v = dh_ref[:, h * DV:(h + 1) * DV]
            dnum = dh_v / g
            dden = -jnp.sum(dh_v * h_ref[:, h * DV:(h + 1) * DV], axis=1, keepdims=True) / g
            dden = jnp.where(jnp.abs(den) > floor, dden * jnp.sign(den), 0.0)
            dnum_b = dnum.astype(BF16)
            da = ((_dot_nt(dnum_b, vb) + dden) * dm).astype(BF16)
            dc1_b = dc1.astype(BF16)
            dq = _dot_nn(da, k) + wi * (_dot_nn(dnum_b, c0.astype(BF16)) + dden * n0)
            dk = _dot_tn(da, q) + ws * (_dot_nn(vb, dc1_b) + dn1)
            dv = _dot_tn(s.astype(BF16), dnum_b) + ws * _dot_nt(k, dc1_b)
            dqk_ref[:, h * DQK:(h + 1) * DQK] = dq
            dqk_ref[:, MLH * DQK + h * DQK:MLH * DQK + (h + 1) * DQK] = dk
            dv_ref[:, h * DV:(h + 1) * DV] = dv.astype(BF16)
            rk = jnp.sum(kf * dk, axis=1, keepdims=True)
            df = jnp.sum(qf * dq, axis=1, keepdims=True) - rk
            df_row = jnp.sum(jnp.where(tt == ss, df, 0.0), axis=0, keepdims=True)
            suffix = jnp.sum(jnp.where(ss >= tt, df_row, 0.0), axis=1, keepdims=True)
            cross = (jnp.sum(jnp.sum(dc1 * cs1_ref[0, h], axis=1, keepdims=True), axis=0, keepdims=True)
                     + jnp.sum(dn1 * st1_ref[0, h, 0:1, :], axis=1, keepdims=True))
            dpf = (suffix + cross) * sneg[:, MLH + h:MLH + h + 1]
            dif = dif + jnp.where(lane8 == h, rk, 0.0) + jnp.where(lane8 == MLH + h, dpf, 0.0)
            dc_scr[h] = dec * dc1 + _dot_tn((wi * dnum).astype(BF16), q)
            dn_scr[h, 0:1, :] = dec * dn1 + jnp.sum(wi * dden * qf, axis=0, keepdims=True)
        dif_ref[...] = dif
        dbif_ref[...] += jnp.sum(dif, axis=0, keepdims=True)

    return pl.pallas_call(
        body, name="mlstm_bwd", grid=(nc,),
        in_specs=[pl.BlockSpec((CHUNK, D), lambda c: (rev(c), 0)),
                  pl.BlockSpec((CHUNK, D), lambda c: (rev(c), C_V // D)),
                  pl.BlockSpec((8, CHUNK), lambda c: (0, rev(c))),
                  pl.BlockSpec((CHUNK, 8), lambda c: (rev(c), 0)),
                  pl.BlockSpec((CHUNK, 8), lambda c: (rev(c), 0)),
                  pl.BlockSpec((1, MLH, DV, DQK), lambda c: (rev(c), 0, 0, 0)),
                  pl.BlockSpec((1, MLH, 8, DQK), lambda c: (rev(c), 0, 0, 0)),
                  pl.BlockSpec((1, MLH, DV, DQK), lambda c: (nxt(c), 0, 0, 0)),
                  pl.BlockSpec((1, MLH, 8, DQK), lambda c: (nxt(c), 0, 0, 0)),
                  pl.BlockSpec((CHUNK, D), lambda c: (rev(c), 0)),
                  pl.BlockSpec((CHUNK, D), lambda c: (rev(c), 0)), _ANY],
        out_specs=[pl.BlockSpec((CHUNK, D), lambda c: (rev(c), 0)),
                   pl.BlockSpec((CHUNK, D), lambda c: (rev(c), C_V // D)),
                   pl.BlockSpec((CHUNK, 8), lambda c: (rev(c), 0)),
                   pl.BlockSpec((1, 8), lambda c: (0, 0))],
        out_shape=[jax.ShapeDtypeStruct((t, D), F32), jax.ShapeDtypeStruct((t, NP), BF16),
                   jax.ShapeDtypeStruct((t, 8), F32), jax.ShapeDtypeStruct((1, 8), F32)],
        scratch_shapes=[pltpu.VMEM((MLH, DV, DQK), F32), pltpu.VMEM((MLH, 8, DQK), F32)],
        input_output_aliases={11: 1}, compiler_params=_params(dimension_semantics=("arbitrary",)),
    )(qk, proj, grow, gcol, sneg_col, cs, st, cs, st, hraw, dh, dproj)


def _ya_fwd(hraw, proj, g):
    t = hraw.shape[0]

    def body(h_ref, o_ref, g_ref, y_ref):
        so = _sigmoid(o_ref[...].astype(F32))
        for h in range(MLH):
            sl = slice(h * DV, (h + 1) * DV)
            xn, _ = _rms(h_ref[:, sl])
            y_ref[:, sl] = (so[:, sl] * xn * g_ref[:, sl]).astype(BF16)

    return pl.pallas_call(
        body, name="ya_fwd", grid=(t // TOK_TILE,), in_specs=[_tok(D), _tok(D, C_O // D), _rep((1, D))],
        out_specs=_tok(D), out_shape=jax.ShapeDtypeStruct((t, D), BF16), compiler_params=_params(),
    )(hraw, proj, g)


_ANY = pl.BlockSpec(memory_space=pl.ANY)


def _ya_bwd(hraw, proj, g, dya, dproj):
    t = hraw.shape[0]

    def body(h_ref, o_ref, g_ref, dy_ref, _, dh_ref, do_ref, dg_ref):
        @pl.when(pl.program_id(0) == 0)
        def _():
            dg_ref[...] = jnp.zeros_like(dg_ref)

        so = _sigmoid(o_ref[...])
        dy = dy_ref[...]
        for h in range(MLH):
            sl = slice(h * DV, (h + 1) * DV)
            xn, rstd = _rms(h_ref[:, sl])
            gs = g_ref[:, sl]
            do_ref[:, sl] = (dy[:, sl] * xn * gs * so[:, sl] * (1.0 - so[:, sl])).astype(BF16)
            dhn = dy[:, sl] * so[:, sl]
            dg_ref[:, sl] += jnp.sum(dhn * xn, axis=0, keepdims=True)
            dh_ref[:, sl] = _rms_bwd(xn, rstd, dhn * gs)

    return pl.pallas_call(
        body, name="ya_bwd", grid=(t // TOK_TILE,),
        in_specs=[_tok(D), _tok(D, C_O // D), _rep((1, D)), _tok(D), _ANY],
        out_specs=[_tok(D), _tok(D, C_O // D), _rep((1, D))],
        out_shape=[jax.ShapeDtypeStruct((t, D), F32), jax.ShapeDtypeStruct((t, NP), BF16),
                   jax.ShapeDtypeStruct((1, D), F32)],
        input_output_aliases={4: 1}, compiler_params=_params(),
    )(hraw, proj, g, dya, dproj)


_SW_SCALE = HD ** -0.5
_KVB = C_KV // (2 * SWKV * HD)


def _swa_mask(n):
    ki = lax.broadcasted_iota(jnp.int32, (2 * WIN, SWG * WIN), 0)
    qi = lax.broadcasted_iota(jnp.int32, (2 * WIN, SWG * WIN), 1) % WIN
    return (ki > qi) & (ki <= qi + WIN) & ((n > 0) | (ki >= WIN))


def _group_rows(x_ref, hk):
    return jnp.concatenate([x_ref[:, (hk * SWG + g) * HD:(hk * SWG + g + 1) * HD] for g in range(SWG)], axis=0)


def _group_lanes(x_ref, hk):
    return jnp.concatenate([x_ref[hk * SWG + g:hk * SWG + g + 1, :] for g in range(SWG)], axis=1)


def _sink_lanes(sink_ref, hk):
    return jnp.concatenate([jnp.broadcast_to(sink_ref[:, hk * SWG + g:hk * SWG + g + 1], (1, WIN))
                            for g in range(SWG)], axis=1)


def _swa_fwd(proj, sinks):
    t = proj.shape[0]
    nb = t // WIN

    def body(q_ref, kvc_ref, kvp_ref, sink_ref, y_ref, lse_ref):
        valid = _swa_mask(pl.program_id(0))
        for hk in range(SWKV):
            ks = slice(hk * HD, (hk + 1) * HD)
            vs = slice(SWKV * HD + hk * HD, SWKV * HD + (hk + 1) * HD)
            kb = jnp.concatenate([kvp_ref[:, ks], kvc_ref[:, ks]], axis=0).astype(BF16)
            vb = jnp.concatenate([kvp_ref[:, vs], kvc_ref[:, vs]], axis=0).astype(BF16)
            q4 = _group_rows(q_ref, hk).astype(BF16)
            sink = _sink_lanes(sink_ref, hk)
            logits = jnp.where(valid, _dot_nt(kb, q4) * _SW_SCALE, -jnp.inf)
            m = jnp.maximum(jnp.max(logits, axis=0, keepdims=True), sink)
            p = jnp.exp(logits - m)
            denom = jnp.sum(p, axis=0, keepdims=True) + jnp.exp(sink - m)
            y4 = _dot_tn((p / denom).astype(BF16), vb).astype(BF16)
            lse4 = m + jnp.log(denom)
            for g in range(SWG):
                hq = hk * SWG + g
                y_ref[:, hq * HD:(hq + 1) * HD] = y4[g * WIN:(g + 1) * WIN]
                lse_ref[hq:hq + 1, :] = lse4[:, g * WIN:(g + 1) * WIN]

    return pl.pallas_call(
        body, name="swa_fwd", grid=(nb,),
        in_specs=[pl.BlockSpec((WIN, D), lambda n: (n, C_QSW // D)),
                  pl.BlockSpec((WIN, 512), lambda n: (n, _KVB)),
                  pl.BlockSpec((WIN, 512), lambda n: (jnp.maximum(n - 1, 0), _KVB)),
                  pl.BlockSpec((1, SWH), lambda n: (0, 0))],
        out_specs=[pl.BlockSpec((WIN, D), lambda n: (n, 0)), pl.BlockSpec((SWH, WIN), lambda n: (0, n))],
        out_shape=[jax.ShapeDtypeStruct((t, D), BF16), jax.ShapeDtypeStruct((SWH, t), F32)],
        compiler_params=_params(),
    )(proj, proj, proj, sinks)


def _swa_bwd(proj, sinks, lse, dyb, dproj):
    t = proj.shape[0]
    nb = t // WIN

    def body(q_ref, kvc_ref, kvp_ref, sink_ref, lse_ref, dy_ref, _, dq_ref, dself_ref, dprev_ref, ds_ref):
        @pl.when(pl.program_id(0) == 0)
        def _():
            ds_ref[...] = jnp.zeros_like(ds_ref)

        valid = _swa_mask(pl.program_id(0))
        for hk in range(SWKV):
            ks = slice(hk * HD, (hk + 1) * HD)
            vs = slice(SWKV * HD + hk * HD, SWKV * HD + (hk + 1) * HD)
            kb = jnp.concatenate([kvp_ref[:, ks], kvc_ref[:, ks]], axis=0).astype(BF16)
            vb = jnp.concatenate([kvp_ref[:, vs], kvc_ref[:, vs]], axis=0).astype(BF16)
            dy4 = _group_rows(dy_ref, hk)
            qb, dyb_ = _group_rows(q_ref, hk).astype(BF16), dy4.astype(BF16)
            lse4 = _group_lanes(lse_ref, hk)
            logits = jnp.where(valid, _dot_nt(kb, qb) * _SW_SCALE, -jnp.inf)
            p = jnp.exp(logits - lse4)
            dpt = _dot_nt(vb, dyb_)
            delta = jnp.sum(p * dpt, axis=0, keepdims=True)
            dsm = (p * (dpt - delta)).astype(BF16)
            dq4 = (_dot_tn(dsm, kb) * _SW_SCALE).astype(BF16)
            dkb = _dot_nn(dsm, qb) * _SW_SCALE
            dvb = _dot_nn(p.astype(BF16), dyb_)
            dsink4 = jnp.exp(_sink_lanes(sink_ref, hk) - lse4) * delta
            for g in range(SWG):
                hq = hk * SWG + g
                dq_ref[:, hq * HD:(hq + 1) * HD] = dq4[g * WIN:(g + 1) * WIN]
                ds_ref[:, hq:hq + 1] += -jnp.sum(dsink4[:, g * WIN:(g + 1) * WIN], axis=1, keepdims=True)
            dprev_ref[:, ks] = dkb[:WIN]
            dself_ref[:, ks] = dkb[WIN:]
            dprev_ref[:, vs] = dvb[:WIN]
            dself_ref[:, vs] = dvb[WIN:]

    return pl.pallas_call(
        body, name="swa_bwd", grid=(nb,),
        in_specs=[pl.BlockSpec((WIN, D), lambda n: (n, C_QSW // D)),
                  pl.BlockSpec((WIN, 512), lambda n: (n, _KVB)),
                  pl.BlockSpec((WIN, 512), lambda n: (jnp.maximum(n - 1, 0), _KVB)),
                  pl.BlockSpec((1, SWH), lambda n: (0, 0)),
                  pl.BlockSpec((SWH, WIN), lambda n: (0, n)),
                  pl.BlockSpec((WIN, D), lambda n: (n, 0)), _ANY],
        out_specs=[pl.BlockSpec((WIN, D), lambda n: (n, C_QSW // D)), pl.BlockSpec((WIN, 512), lambda n: (n, 0)),
                   pl.BlockSpec((WIN, 512), lambda n: (n, 0)), pl.BlockSpec((1, SWH), lambda n: (0, 0))],
        out_shape=[jax.ShapeDtypeStruct((t, NP), BF16), jax.ShapeDtypeStruct((t, 512), F32),
                   jax.ShapeDtypeStruct((t, 512), F32), jax.ShapeDtypeStruct((1, SWH), F32)],
        input_output_aliases={6: 0}, compiler_params=_params(),
    )(proj, proj, proj, sinks, lse, dyb, dproj)


def _kv_combine(dself, dprev, dif, dproj):
    t = dself.shape[0]
    nb = t // WIN

    def body(a_ref, b_ref, dif_ref, _, o_ref):
        nxt = jnp.where(pl.program_id(0) < nb - 1, b_ref[...], 0.0)
        o_ref[:, 0:512] = (a_ref[...] + nxt).astype(BF16)
        lane = lax.broadcasted_iota(jnp.int32, (WIN, 128), 1)
        dif_v = dif_ref[...]
        first = jnp.zeros((WIN, 128), F32)
        for col in range(8):
            first = first + jnp.where(lane == col, dif_v[:, col:col + 1], 0.0)
        o_ref[:, 512:640] = first.astype(BF16)
        o_ref[:, 640:512 + IFW] = jnp.zeros((WIN, IFW - 128), BF16)

    return pl.pallas_call(
        body, name="kv_combine", grid=(nb,),
        in_specs=[pl.BlockSpec((WIN, 512), lambda n: (n, 0)),
                  pl.BlockSpec((WIN, 512), lambda n: (jnp.minimum(n + 1, nb - 1), 0)),
                  pl.BlockSpec((WIN, 8), lambda n: (n, 0)), _ANY],
        out_specs=pl.BlockSpec((WIN, 512 + IFW), lambda n: (n, C_KV // (512 + IFW))),
        out_shape=jax.ShapeDtypeStruct((t, NP), BF16), input_output_aliases={3: 0}, compiler_params=_params(),
    )(dself, dprev, dif, dproj)


def _merge_fwd(proj, za, zb):
    t = proj.shape[0]

    def body(ga_ref, gb_ref, za_ref, zb_ref, o_ref):
        o_ref[...] = (_sigmoid(ga_ref[...]) * za_ref[...] + _sigmoid(gb_ref[...]) * zb_ref[...]).astype(BF16)

    return pl.pallas_call(
        body, name="merge_fwd", grid=(t // TOK_TILE,),
        in_specs=[_tok(D, C_GA // D), _tok(D, C_GB // D), _tok(D), _tok(D)], out_specs=_tok(D),
        out_shape=jax.ShapeDtypeStruct((t, D), BF16), compiler_params=_params(),
    )(proj, proj, za, zb)


def _merge_bwd(proj, za, zb, dmerged):
    t = proj.shape[0]

    def body(ga_ref, gb_ref, za_ref, zb_ref, dm_ref, dza_ref, dzb_ref, dp_ref):
        dm = dm_ref[...]
        sa, sb = _sigmoid(ga_ref[...]), _sigmoid(gb_ref[...])
        dza_ref[...] = (dm * sa).astype(BF16)
        dzb_ref[...] = (dm * sb).astype(BF16)
        dp_ref[:, 0:D] = (dm * za_ref[...] * sa * (1.0 - sa)).astype(BF16)
        dp_ref[:, D:2 * D] = (dm * zb_ref[...] * sb * (1.0 - sb)).astype(BF16)

    return pl.pallas_call(
        body, name="merge_bwd", grid=(t // TOK_TILE,),
        in_specs=[_tok(D, C_GA // D), _tok(D, C_GB // D), _tok(D), _tok(D), _tok(D)],
        out_specs=[_tok(D), _tok(D), _tok(2 * D, C_GA // (2 * D))],
        out_shape=[jax.ShapeDtypeStruct((t, D), BF16)] * 2 + [jax.ShapeDtypeStruct((t, NP), BF16)],
        compiler_params=_params(),
    )(proj, proj, za, zb, dmerged)


def _act_fwd(u):
    t = u.shape[0]

    def body(u_ref, a_ref):
        r = jnp.maximum(u_ref[...], 0.0)
        a_ref[...] = (r * r).astype(BF16)

    return pl.pallas_call(
        body, name="act_fwd", grid=(t // TOK_TILE,), in_specs=[_tok(DFF)], out_specs=_tok(DFF),
        out_shape=jax.ShapeDtypeStruct((t, DFF), BF16), compiler_params=_params(),
    )(u)


def _act_bwd(u, da):
    t = u.shape[0]

    def body(u_ref, da_ref, du_ref):
        du_ref[...] = (da_ref[...] * 2.0 * jnp.maximum(u_ref[...], 0.0)).astype(BF16)

    return pl.pallas_call(
        body, name="act_bwd", grid=(t // TOK_TILE,), in_specs=[_tok(DFF), _tok(DFF)], out_specs=_tok(DFF),
        out_shape=jax.ShapeDtypeStruct((t, DFF), BF16), compiler_params=_params(),
    )(u, da)


def _ple_final(x2, gpre, pp, target, gf):
    t = x2.shape[0]

    def body(x_ref, gp_ref, pp_ref, t_ref, g_ref, loss_ref, dg_ref, dx_ref, dpp_ref, dgp_ref):
        @pl.when(pl.program_id(0) == 0)
        def _():
            loss_ref[...] = jnp.zeros_like(loss_ref)
            dg_ref[...] = jnp.zeros_like(dg_ref)

        gate = _sigmoid(gp_ref[...])
        pp_v = pp_ref[...]
        x3 = x_ref[...] + gate * pp_v
        xn, rstd = _rms(x3)
        gf_v = g_ref[...]
        err = xn * gf_v - t_ref[...]
        loss_ref[...] += (0.5 / D) * jnp.sum(jnp.sum(err * err, axis=1, keepdims=True), axis=0, keepdims=True)
        dy = err * (1.0 / D)
        dg_ref[...] += jnp.sum(dy * xn, axis=0, keepdims=True)
        dx3 = _rms_bwd(xn, rstd, dy * gf_v)
        dx_ref[...] = dx3
        dpp_ref[...] = (dx3 * gate).astype(BF16)
        dgp_ref[...] = (dx3 * pp_v * gate * (1.0 - gate)).astype(BF16)

    return pl.pallas_call(
        body, name="ple_final", grid=(t // TOK_TILE,),
        in_specs=[_tok(D), _tok(D), _tok(D), _tok(D), _rep((1, D))],
        out_specs=[_rep((1, 1)), _rep((1, D)), _tok(D), _tok(D), _tok(D)],
        out_shape=[jax.ShapeDtypeStruct((1, 1), F32), jax.ShapeDtypeStruct((1, D), F32),
                   jax.ShapeDtypeStruct((t, D), F32), jax.ShapeDtypeStruct((t, D), BF16),
                   jax.ShapeDtypeStruct((t, D), BF16)],
        compiler_params=_params(),
    )(x2, gpre, pp, target, gf)


def _sds(t, n, dtype):
    return jax.ShapeDtypeStruct((t, n), dtype)


def _proj_in(h0, w_in):
    t = h0.shape[0]

    def epilogue(accs, ins, outs, i, j):
        outs[0][...] = accs[0].astype(BF16)

        @pl.when(j == C_IF // IFW)
        def _():
            outs[1][...] = accs[0]

    gate_cols = lambda tm, tn: pl.BlockSpec((tm, IFW), lambda i, j, kk: (i, 0))
    return _mm_ep([(h0, w_in)], "nn", "mm_in", epilogue, [],
                  [(_sds(t, NP, BF16), _tile()), (_sds(t, IFW, F32), gate_cols)], 1024, IFW)


def _branch_merge(ya, yb, wa, wb, proj):
    t = ya.shape[0]

    def epilogue(accs, ins, outs, i, j):
        za, zb = accs
        merged = _sigmoid(ins[0][...].astype(F32)) * za + _sigmoid(ins[1][...].astype(F32)) * zb
        outs[0][...] = merged.astype(BF16)
        outs[1][...] = za.astype(BF16)
        outs[2][...] = zb.astype(BF16)

    return _mm_ep([(ya, wa), (yb, wb)], "nn", "mm_branch_merge", epilogue, [(proj, _tile(C_GA)), (proj, _tile(C_GB))],
                  [(_sds(t, D, BF16), _tile())] * 3, 1024, 512)


def _dmerged_bwd(dxb, w_out, proj, za, zb):
    t = dxb.shape[0]

    def epilogue(accs, ins, outs, i, j):
        dm = accs[0]
        sa, sb = _sigmoid(ins[0][...].astype(F32)), _sigmoid(ins[1][...].astype(F32))
        outs[0][...] = (dm * sa).astype(BF16)
        outs[1][...] = (dm * sb).astype(BF16)
        outs[2][:, 0:D] = (dm * ins[2][...].astype(F32) * sa * (1.0 - sa)).astype(BF16)
        outs[2][:, D:2 * D] = (dm * ins[3][...].astype(F32) * sb * (1.0 - sb)).astype(BF16)

    gate_cols = lambda tm, tn: pl.BlockSpec((tm, 2 * D), lambda i, j, kk: (i, C_GA // (2 * D)))
    return _mm_ep([(dxb, w_out)], "nt", "mm_dmerged_bwd", epilogue,
                  [(proj, _tile(C_GA)), (proj, _tile(C_GB)), (za, _tile()), (zb, _tile())],
                  [(_sds(t, D, BF16), _tile()), (_sds(t, D, BF16), _tile()), (_sds(t, NP, BF16), gate_cols)], 512, D)


def _dya_bwd(dza, wa, hraw, proj, g, dproj):
    t = dza.shape[0]

    def epilogue(accs, ins, outs, i, j):
        h_ref, o_ref, g_ref, _ = ins
        dh_ref, do_ref, dg_ref = outs

        @pl.when(i == 0)
        def _():
            dg_ref[...] = jnp.zeros_like(dg_ref)

        dy = accs[0]
        so = _sigmoid(o_ref[...].astype(F32))
        for h in range(MLH):
            sl = slice(h * DV, (h + 1) * DV)
            xn, rstd = _rms(h_ref[:, sl])
            gs = g_ref[:, sl]
            do_ref[:, sl] = (dy[:, sl] * xn * gs * so[:, sl] * (1.0 - so[:, sl])).astype(BF16)
            dhn = dy[:, sl] * so[:, sl]
            dg_ref[:, sl] += jnp.sum(dhn * xn, axis=0, keepdims=True)
            dh_ref[:, sl] = _rms_bwd(xn, rstd, dhn * gs)

    return _mm_ep([(dza, wa)], "nt", "mm_dya_bwd", epilogue,
                  [(hraw, _tile()), (proj, _tile(C_O)), (g, _row()), (dproj, lambda tm, tn: _ANY)],
                  [(_sds(t, D, F32), _tile()), (_sds(t, NP, BF16), _tile(C_O)), (_sds(1, D, F32), _row())],
                  512, D, aliases={3: 1})


def _up_act(hn, w_up):
    t = hn.shape[0]

    def epilogue(accs, ins, outs, i, j):
        r = jnp.maximum(accs[0], 0.0)
        outs[0][...] = (r * r).astype(BF16)
        outs[1][...] = accs[0].astype(BF16)

    return _mm_ep([(hn, w_up)], "nn", "mm_up_act", epilogue, [],
                  [(_sds(t, DFF, BF16), _tile()), (_sds(t, DFF, BF16), _tile())], 1024, 512)


def _da_du(dxb, w_down, u):
    t = dxb.shape[0]

    def epilogue(accs, ins, outs, i, j):
        outs[0][...] = (accs[0] * 2.0 * jnp.maximum(ins[0][...].astype(F32), 0.0)).astype(BF16)

    return _mm_ep([(dxb, w_down)], "nt", "mm_da_du", epilogue, [(u, _tile())], [(_sds(t, DFF, BF16), _tile())],
                  1024, 512)[0]


def _resid_norm_mm(a, w, x, g, name):
    t = x.shape[0]

    def epilogue(accs, ins, outs, i, j):
        x1 = ins[0][...] + accs[0]
        outs[0][...] = x1
        xn, _ = _rms(x1)
        outs[1][...] = (xn * ins[1][...]).astype(BF16)

    return _mm_ep([(a, w)], "nn", name, epilogue, [(x, _tile()), (g, _row())],
                  [(_sds(t, D, F32), _tile()), (_sds(t, D, BF16), _tile())], 512, D)


def _norm_bwd_mm(dy, w, x, g, dres, name):
    t = x.shape[0]

    def epilogue(accs, ins, outs, i, j):
        @pl.when(i == 0)
        def _():
            outs[2][...] = jnp.zeros_like(outs[2])

        dh = accs[0]
        xn, rstd = _rms(ins[0][...])
        outs[2][...] += jnp.sum(dh * xn, axis=0, keepdims=True)
        dx = ins[2][...] + _rms_bwd(xn, rstd, dh * ins[1][...])
        outs[0][...] = dx
        outs[1][...] = dx.astype(BF16)

    return _mm_ep([(dy, w)], "nt", name, epilogue, [(x, _tile()), (g, _row()), (dres, _tile())],
                  [(_sds(t, D, F32), _tile()), (_sds(t, D, BF16), _tile()), (_sds(1, D, F32), _row())], 512, D)


def _ple_final_mm(hn2, w_gate, x2, pp, target, gf):
    t = x2.shape[0]

    def epilogue(accs, ins, outs, i, j):
        loss_ref, dg_ref, dx_ref, dpp_ref, dgp_ref = outs

        @pl.when(i == 0)
        def _():
            loss_ref[...] = jnp.zeros_like(loss_ref)
            dg_ref[...] = jnp.zeros_like(dg_ref)

        gate = _sigmoid(accs[0])
        pp_v = ins[1][...]
        x3 = ins[0][...] + gate * pp_v
        xn, rstd = _rms(x3)
        gf_v = ins[3][...]
        err = xn * gf_v - ins[2][...]
        loss_ref[...] += (0.5 / D) * jnp.sum(jnp.sum(err * err, axis=1, keepdims=True), axis=0, keepdims=True)
        dy = err * (1.0 / D)
        dg_ref[...] += jnp.sum(dy * xn, axis=0, keepdims=True)
        dx3 = _rms_bwd(xn, rstd, dy * gf_v)
        dx_ref[...] = dx3
        dpp_ref[...] = (dx3 * gate).astype(BF16)
        dgp_ref[...] = (dx3 * pp_v * gate * (1.0 - gate)).astype(BF16)

    one = lambda tm, tn: pl.BlockSpec((1, 1), lambda i, j, kk: (0, 0))
    return _mm_ep([(hn2, w_gate)], "nn", "mm_ple_final", epilogue,
                  [(x2, _tile()), (pp, _tile()), (target, _tile()), (gf, _row())],
                  [(_sds(1, 1, F32), one), (_sds(1, D, F32), _row()), (_sds(t, D, F32), _tile()),
                   (_sds(t, D, BF16), _tile()), (_sds(t, D, BF16), _tile())], 512, D)


def _win_pad(w):
    zeros = jnp.zeros((w.shape[0], IFW - 8), w.dtype)
    return jnp.concatenate([w[:, 0:3072], w[:, 3080:4104], w[:, 4616:6664], w[:, 4104:4616], w[:, 3072:3080], zeros],
                           axis=1)


def _win_unpad(wp):
    return jnp.concatenate([wp[:, 0:3072], wp[:, C_IF:C_IF + 8], wp[:, C_QSW:C_QSW + 1024], wp[:, C_KV:C_KV + 512],
                            wp[:, C_GA:C_GA + 2048]], axis=1)


def _local_step(x, p, target, w):
    t = x.shape[0]
    pb = p.astype(BF16)

    h0 = _norm_fwd(x, w["norm_mix_g"], "norm_mix")
    proj, gates = _proj_in(h0, w["w_in"])
    qk = _conv_silu_fwd(proj, w["conv_qk"])
    grow, sneg_row = _gates_fwd(gates[:, 0:8].T, w["b_if"].reshape(8, 1))
    gcol, sneg_col = grow.T, sneg_row.T
    hraw, cs, st = _mlstm_fwd(qk, proj, grow, gcol)
    ya = _ya_fwd(hraw, proj, w["mlstm_norm_g"])
    yb, lse = _swa_fwd(proj, w["sinks"])
    merged, za, zb = _branch_merge(ya, yb, w["w_branch_a"], w["w_branch_b"], proj)
    x1, hn1 = _resid_norm_mm(merged, w["w_out"], x, w["norm_mlp_g"], "mm_out_norm")
    act, u = _up_act(hn1, w["w_up"])
    x2, hn2 = _resid_norm_mm(act, w["w_down"], x1, w["norm_ple_g"], "mm_down_norm")
    pp = _mm(pb, w["w_ple_proj"], "nn", F32, "mm_ple_proj")
    loss, d_final_g, dx3, dpp, dgpre = _ple_final_mm(hn2, w["w_ple_gate"], x2, pp, target, w["final_norm_g"])

    g = {"final_norm_g": d_final_g}
    g["w_ple_proj"] = _mm(pb, dpp, "tn", F32, "mm_d_ple_proj", out_chunks=4)
    g["w_ple_gate"] = _mm(hn2, dgpre, "tn", F32, "mm_d_ple_gate")
    dx2, dx2b, g["norm_ple_g"] = _norm_bwd_mm(dgpre, w["w_ple_gate"], x2, w["norm_ple_g"], dx3, "mm_dhn2_norm")
    g["w_down"] = _mm(act, dx2b, "tn", F32, "mm_d_down")
    du = _da_du(dx2b, w["w_down"], u)
    g["w_up"] = _mm(hn1, du, "tn", F32, "mm_d_up", out_chunks=4)
    dx1, dx1b, g["norm_mlp_g"] = _norm_bwd_mm(du, w["w_up"], x1, w["norm_mlp_g"], dx2, "mm_dhn1_norm")
    g["w_out"] = _mm(merged, dx1b, "tn", F32, "mm_d_out")
    dza, dzb, dproj = _dmerged_bwd(dx1b, w["w_out"], proj, za, zb)
    g["w_branch_a"] = _mm(ya, dza, "tn", F32, "mm_d_branch_a")
    g["w_branch_b"] = _mm(yb, dzb, "tn", F32, "mm_d_branch_b")
    dyb = _mm(dzb, w["w_branch_b"], "nt", F32, "mm_dyb")
    dhraw, dproj, g["mlstm_norm_g"] = _dya_bwd(dza, w["w_branch_a"], hraw, proj, w["mlstm_norm_g"], dproj)
    dqk, dproj, dif, g["b_if"] = _mlstm_bwd(qk, proj, grow, gcol, sneg_col, cs, st, hraw, dhraw, dproj)
    dc, g["conv_qk"] = _conv_silu_bwd_a(proj, w["conv_qk"], dqk)
    dproj = _conv_silu_bwd_b(dc, w["conv_qk"], dproj)
    dproj, dkv_self, dkv_prev, g["sinks"] = _swa_bwd(proj, w["sinks"], lse, dyb, dproj)
    dproj = _kv_combine(dkv_self, dkv_prev, dif, dproj)
    g["w_in"] = _mm(h0, dproj, "tn", F32, "mm_d_in")
    grad_x, _, g["norm_mix_g"] = _norm_bwd_mm(dproj, w["w_in"], x, w["norm_mix_g"], dx1, "mm_dh0_norm")
    return loss, grad_x, g


_W4 = ("w_branch_a", "w_branch_b", "w_out", "w_ple_gate")
_SHARDED_NAMES = ("w_in", "w_up", "w_down", "w_ple_proj", "conv_qk") + _W4
_SMALL_ROWS = 16
_CONV_ROW = 8


def _group(s):
    return [s["w_in"], jnp.concatenate([s[n] for n in _W4], axis=0), s["w_up"], s["w_down"], s["w_ple_proj"]]


def _ungroup(arrs):
    out = {"w_in": arrs[0], "w_up": arrs[2], "w_down": arrs[3], "w_ple_proj": arrs[4]}
    rows = arrs[1].shape[0] // len(_W4)
    for i, n in enumerate(_W4):
        out[n] = arrs[1][i * rows:(i + 1) * rows]
    return out


def _rows_tile(rows):
    return 256 if rows % 256 == 0 else rows


_SMALL = ("norm_mix_g", "mlstm_norm_g", "norm_mlp_g", "norm_ple_g", "final_norm_g")


def _pack_small(vals, extra=None, conv=None):
    rows = [vals[n].reshape(1, D) for n in _SMALL]
    tail = [vals["b_if"].reshape(1, 8), vals["sinks"].reshape(1, SWH)]
    used = 8 + SWH
    if extra is not None:
        tail.append(extra.reshape(1, 1))
        used += 1
    tail.append(jnp.zeros((1, D - used), F32))
    rows.append(jnp.concatenate(tail, axis=1))
    rows.append(jnp.zeros((_CONV_ROW - len(rows), D), F32))
    rows.append(jnp.zeros((CONV, D), F32) if conv is None else conv)
    rows.append(jnp.zeros((_SMALL_ROWS - _CONV_ROW - CONV, D), F32))
    return jnp.concatenate(rows, axis=0)


def _unpack_small(slab, shapes):
    out = {n: slab[i].reshape(shapes[n]) for i, n in enumerate(_SMALL)}
    out["b_if"] = slab[5, 0:8].reshape(shapes["b_if"])
    out["sinks"] = slab[5, 8:8 + SWH].reshape(shapes["sinks"])
    return out


_MESH = pl.DeviceIdType.MESH
_HBM = pl.BlockSpec(memory_space=pltpu.HBM)
_VMEM = pl.BlockSpec(memory_space=pltpu.VMEM)


def _place():
    x, y, c = lax.axis_index("x"), lax.axis_index("y"), lax.axis_index("c")
    return x, y, c, 2 * x + y


def _chip_peer(x, y, r):
    return (x ^ (r >> 1), y ^ (r & 1))


def _half(ref, which):
    h = ref.shape[-2] // 2
    return pl.ds(which * h, h)


def _allgather_weights(shards, conv):
    n = len(shards)

    def body(*refs):
        ins, conv_ref = refs[:n], refs[n]
        outs, conv_out = refs[n + 1:2 * n + 1], refs[2 * n + 1]
        send_a, recv_a, send_b, recv_b, send_c, recv_c, local_sems = refs[2 * n + 2:]
        x, y, c, j = _place()
        sibling = (x, y, 1 - c)
        local = [pltpu.make_async_copy(ins[k], outs[k].at[j], local_sems.at[k]) for k in range(n)]
        local.append(pltpu.make_async_copy(conv_ref, conv_out.at[j], local_sems.at[n]))
        for cp in local:
            cp.start()

        def copy_a(k, r, chip):
            rows = _half(ins[k], c)
            return pltpu.make_async_remote_copy(
                src_ref=ins[k].at[rows], dst_ref=outs[k].at[chip, rows], send_sem=send_a.at[3 * k + r - 1],
                recv_sem=recv_a.at[3 * k + r - 1], device_id=(*_chip_peer(x, y, r), c), device_id_type=_MESH)

        def copy_b(k, r, chip, which):
            rows = _half(ins[k], which)
            return pltpu.make_async_remote_copy(
                src_ref=outs[k].at[chip, rows], dst_ref=outs[k].at[chip, rows], send_sem=send_b.at[3 * k + r - 1],
                recv_sem=recv_b.at[3 * k + r - 1], device_id=sibling, device_id_type=_MESH)

        def copy_c(r, chip):
            return pltpu.make_async_remote_copy(
                src_ref=conv_ref, dst_ref=conv_out.at[chip], send_sem=send_c.at[r - 1],
                recv_sem=recv_c.at[r - 1], device_id=(*_chip_peer(x, y, r), c), device_id_type=_MESH)

        for k in range(n):
            for r in (1, 2, 3):
                copy_a(k, r, j).start()
        for r in (1, 2, 3):
            copy_c(r, j).start()
        for k in range(n):
            for r in (1, 2, 3):
                copy_a(k, r, j ^ r).wait_recv()
                copy_b(k, r, j ^ r, c).start()
        for k in range(n):
            for r in (1, 2, 3):
                copy_b(k, r, j ^ r, 1 - c).wait_recv()
        for r in (1, 2, 3):
            copy_c(r, j ^ r).wait_recv()
        for k in range(n):
            for r in (1, 2, 3):
                copy_a(k, r, j).wait_send()
                copy_b(k, r, j ^ r, c).wait_send()
        for r in (1, 2, 3):
            copy_c(r, j).wait_send()
        for cp in local:
            cp.wait()

    return pl.pallas_call(
        body, name="allgather_weights",
        out_shape=[jax.ShapeDtypeStruct((4,) + s.shape, s.dtype) for s in shards]
        + [jax.ShapeDtypeStruct((4,) + conv.shape, F32)],
        in_specs=[_HBM] * (n + 1), out_specs=[_HBM] * (n + 1),
        scratch_shapes=[pltpu.SemaphoreType.DMA((3 * n,))] * 4 + [pltpu.SemaphoreType.DMA((3,))] * 2
        + [pltpu.SemaphoreType.DMA((n + 1,))],
    )(*shards, conv)


def _pair_exchange(gs):
    n = len(gs)

    def body(*refs):
        ins, outs, send_sems, recv_sems = refs[:n], refs[n:2 * n], refs[2 * n], refs[2 * n + 1]
        x, y, c, _ = _place()
        cps = [pltpu.make_async_remote_copy(
            src_ref=ins[k].at[:, _half(ins[k], 1 - c)], dst_ref=outs[k], send_sem=send_sems.at[k],
            recv_sem=recv_sems.at[k], device_id=(x, y, 1 - c), device_id_type=_MESH) for k in range(n)]
        for cp in cps:
            cp.start()
        for cp in cps:
            cp.wait()

    return pl.pallas_call(
        body, name="pair_exchange",
        out_shape=[jax.ShapeDtypeStruct((4, g.shape[1] // 2, g.shape[2]), F32) for g in gs],
        in_specs=[_HBM] * n, out_specs=[_HBM] * n, scratch_shapes=[pltpu.SemaphoreType.DMA((n,))] * 2,
    )(*gs)


def _pair_sum(g, theirs, c, name):
    _, h, cols = theirs.shape
    tr = _rows_tile(h)
    nb = h // tr

    def body(c_ref, a_ref, b_ref, o_ref, ob_ref):
        s = a_ref[...] + b_ref[...]
        o_ref[...] = s
        ob_ref[...] = s.astype(BF16)

    blk = pl.BlockSpec((1, tr, cols), lambda k, i, c_ref: (k, i, 0))
    return pl.pallas_call(
        body, name=name,
        grid_spec=pltpu.PrefetchScalarGridSpec(
            num_scalar_prefetch=1, grid=(4, nb),
            in_specs=[pl.BlockSpec((1, tr, cols), lambda k, i, c_ref: (k, c_ref[0] * nb + i, 0)), blk],
            out_specs=[blk, blk]),
        out_shape=[jax.ShapeDtypeStruct(theirs.shape, F32), jax.ShapeDtypeStruct(theirs.shape, BF16)],
        compiler_params=_params(),
    )(c.reshape(1).astype(jnp.int32), g, theirs)


def _chip_exchange(ss):
    n = len(ss)

    def body(*refs):
        ins, outs, send_sems, recv_sems = refs[:n], refs[n:2 * n], refs[2 * n], refs[2 * n + 1]
        x, y, c, j = _place()
        cps = [pltpu.make_async_remote_copy(
            src_ref=ins[k].at[j ^ r], dst_ref=outs[k].at[r - 1], send_sem=send_sems.at[3 * k + r - 1],
            recv_sem=recv_sems.at[3 * k + r - 1], device_id=(*_chip_peer(x, y, r), c), device_id_type=_MESH)
            for k in range(n) for r in (1, 2, 3)]
        for cp in cps:
            cp.start()
        for cp in cps:
            cp.wait()

    return pl.pallas_call(
        body, name="chip_exchange", out_shape=[jax.ShapeDtypeStruct((3,) + s.shape[1:], s.dtype) for s in ss],
        in_specs=[_HBM] * n, out_specs=[_HBM] * n, scratch_shapes=[pltpu.SemaphoreType.DMA((3 * n,))] * 2,
    )(*ss)


def _reduce4(own, others, j, c, name):
    _, h, cols = own.shape
    tr = _rows_tile(h)
    nb = h // tr

    def body(idx_ref, s_ref, a0, a1, a2, o_ref):
        o_ref[...] = ((s_ref[0] + a0[0].astype(F32)) + a1[0].astype(F32)) + a2[0].astype(F32)

    def other(r):
        return pl.BlockSpec((1, tr, cols), lambda i, idx_ref: (r, i, 0))

    return pl.pallas_call(
        body, name=name,
        grid_spec=pltpu.PrefetchScalarGridSpec(
            num_scalar_prefetch=1, grid=(nb,),
            in_specs=[pl.BlockSpec((1, tr, cols), lambda i, idx_ref: (idx_ref[0], i, 0)), other(0), other(1), other(2)],
            out_specs=pl.BlockSpec((tr, cols), lambda i, idx_ref: (idx_ref[1] * nb + i, 0))),
        out_shape=jax.ShapeDtypeStruct((2 * h, cols), F32), compiler_params=_params(),
    )(jnp.stack([j, c]).astype(jnp.int32), own, others, others, others)


def _sibling_share(fulls):
    n = len(fulls)

    def body(*refs):
        outs, send_sems, recv_sems = refs[n:2 * n], refs[2 * n], refs[2 * n + 1]
        x, y, c, _ = _place()
        cps = [pltpu.make_async_remote_copy(
            src_ref=outs[k].at[_half(outs[k], c)], dst_ref=outs[k].at[_half(outs[k], c)], send_sem=send_sems.at[k],
            recv_sem=recv_sems.at[k], device_id=(x, y, 1 - c), device_id_type=_MESH) for k in range(n)]
        for cp in cps:
            cp.start()
        for cp in cps:
            cp.wait()

    return pl.pallas_call(
        body, name="sibling_share", out_shape=[jax.ShapeDtypeStruct(f.shape, F32) for f in fulls],
        in_specs=[_HBM] * n, out_specs=[_HBM] * n, input_output_aliases={k: k for k in range(n)},
        scratch_shapes=[pltpu.SemaphoreType.DMA((n,))] * 2,
    )(*fulls)


def _adamw(w, g, m, v):
    m1 = ADAM_B1 * m + (1.0 - ADAM_B1) * g
    v1 = ADAM_B2 * v + (1.0 - ADAM_B2) * (g * g)
    m_hat = m1 / (1.0 - ADAM_B1 ** ADAM_STEP)
    v_hat = v1 / (1.0 - ADAM_B2 ** ADAM_STEP)
    delta = -ADAM_LR * (m_hat / (jnp.sqrt(v_hat) + ADAM_EPS) + ADAM_WD * w)
    return delta, m1, v1


def _adamw_call(w, g, m, v, name):
    rows, cols = w.shape
    tr = _rows_tile(rows)

    def body(w_ref, g_ref, m_ref, v_ref, d_out, m_out, v_out):
        delta, m1, v1 = _adamw(w_ref[...], g_ref[...], m_ref[...], v_ref[...])
        d_out[...] = delta
        m_out[...] = m1
        v_out[...] = v1

    blk = pl.BlockSpec((tr, cols), lambda i: (i, 0))
    return pl.pallas_call(
        body, name=name, grid=(rows // tr,), in_specs=[blk] * 4, out_specs=[blk] * 3,
        out_shape=[jax.ShapeDtypeStruct((rows, cols), F32)] * 3, compiler_params=_params(),
    )(w, g, m, v)


def _small_allreduce(vals):
    def body(v_ref, out_ref, buf, send_sems, recv_sems):
        x, y, c, j = _place()
        me = 2 * j + c
        buf[0] = v_ref[...]

        def copy(r):
            return pltpu.make_async_remote_copy(
                src_ref=v_ref, dst_ref=buf.at[r], send_sem=send_sems.at[r - 1], recv_sem=recv_sems.at[r - 1],
                device_id=(x ^ (r >> 2), y ^ ((r >> 1) & 1), c ^ (r & 1)), device_id_type=_MESH)

        for r in range(1, 8):
            copy(r).start()
        for r in range(1, 8):
            copy(r).wait()
        acc = buf[me ^ 0]
        for d in range(1, 8):
            acc = acc + buf[me ^ d]
        out_ref[...] = acc

    return pl.pallas_call(
        body, name="small_allreduce", out_shape=jax.ShapeDtypeStruct((_SMALL_ROWS, D), F32),
        in_specs=[_VMEM], out_specs=_VMEM,
        scratch_shapes=[pltpu.VMEM((8, _SMALL_ROWS, D), F32), pltpu.SemaphoreType.DMA((7,)),
                        pltpu.SemaphoreType.DMA((7,))],
    )(vals)


_NAMES = ("norm_mix_g", "w_in", "conv_qk", "b_if", "mlstm_norm_g", "sinks", "w_branch_a", "w_branch_b", "w_out",
          "norm_mlp_g", "w_up", "w_down", "norm_ple_g", "w_ple_gate", "w_ple_proj", "final_norm_g")
_GROUP_NAMES = ("w_in", "w4", "w_up", "w_down", "w_ple_proj")


def _step(x, p, target, w, m, v):
    c = lax.axis_index("c")
    j = 2 * lax.axis_index("x") + lax.axis_index("y")

    def shards(d):
        return {n: d[n][0] for n in _SHARDED_NAMES}

    ws = shards(w)
    gathered = _allgather_weights([a.astype(BF16) for a in _group(ws)], ws["conv_qk"])
    w_in_all, w4_all, w_up_all, w_down_all, w_pp_all, conv_all = gathered
    full = {n: w[n] for n in ("norm_mix_g", "mlstm_norm_g", "norm_mlp_g", "norm_ple_g", "b_if", "sinks")}
    full["final_norm_g"] = w["final_norm_g"].reshape(1, D)
    full["w_in"] = _win_pad(jnp.swapaxes(w_in_all, 0, 1).reshape(D, N_IN))
    w4_all = w4_all.reshape(4, len(_W4), D // 4, D)
    for i, n in enumerate(_W4):
        full[n] = w4_all[:, i].reshape(D, D)
    full["w_up"] = w_up_all
    full["w_down"] = w_down_all.reshape(DFF, D)
    full["w_ple_proj"] = w_pp_all
    full["conv_qk"] = jnp.swapaxes(conv_all, 0, 1).reshape(CONV, D)

    loss, grad_x, g = _local_step(x[0], p[0, 0], target[0], full)

    w_in_g = _win_unpad(g["w_in"])
    by_dest = [jnp.swapaxes(w_in_g.reshape(D, 4, N_IN // 4), 0, 1),
               jnp.stack([g[n].reshape(4, D // 4, D) for n in _W4], axis=1).reshape(4, D, D),
               g["w_up"], g["w_down"].reshape(4, DFF // 4, D), g["w_ple_proj"]]
    theirs = _pair_exchange(by_dest)
    sums = [_pair_sum(a, b, c, "pair_sum_" + n) for a, b, n in zip(by_dest, theirs, _GROUP_NAMES)]
    others = _chip_exchange([s[1] for s in sums])
    halves = [_reduce4(s[0], b, j, c, "reduce4_" + n) for s, b, n in zip(sums, others, _GROUP_NAMES)]
    grads = _sibling_share(halves)

    small_g = _small_allreduce(_pack_small(g, extra=loss, conv=g["conv_qk"]))
    conv_g = lax.dynamic_slice(small_g[_CONV_ROW:_CONV_ROW + CONV], (0, j * (D // 4)), (CONV, D // 4))

    ms, vs = shards(m), shards(v)
    upd = [_adamw_call(wa, ga, ma, va, "adamw_" + n)
           for wa, ga, ma, va, n in zip(_group(ws), grads, _group(ms), _group(vs), _GROUP_NAMES)]
    conv_upd = _adamw_call(ws["conv_qk"], conv_g, ms["conv_qk"], vs["conv_qk"], "adamw_conv")
    small_upd = _adamw_call(_pack_small(w), small_g, _pack_small(m), _pack_small(v), "adamw_small")

    shapes = {n: w[n].shape for n in _NAMES}
    res = []
    for k in range(4):
        big = _ungroup(list(grads) if k == 0 else [u[k - 1] for u in upd])
        big["conv_qk"] = conv_g if k == 0 else conv_upd[k - 1]
        leaves = _unpack_small(small_g if k == 0 else small_upd[k - 1], shapes)
        leaves.update({n: a.reshape(shapes[n]) for n, a in big.items()})
        res.append(leaves)

    out = [small_g[5, 8 + SWH], grad_x[None]]
    for k in range(4):
        out += [res[k][n] for n in _NAMES]
    return tuple(out)


def kernel(x, p, norm_mix_g, w_in, conv_qk, b_if, mlstm_norm_g, sinks, w_branch_a, w_branch_b, w_out, norm_mlp_g, w_up, w_down, norm_ple_g, w_ple_gate, w_ple_proj, final_norm_g, loss_target, m_norm_mix_g, m_w_in, m_conv_qk, m_b_if, m_mlstm_norm_g, m_sinks, m_w_branch_a, m_w_branch_b, m_w_out, m_norm_mlp_g, m_w_up, m_w_down, m_norm_ple_g, m_w_ple_gate, m_w_ple_proj, m_final_norm_g, v_norm_mix_g, v_w_in, v_conv_qk, v_b_if, v_mlstm_norm_g, v_sinks, v_w_branch_a, v_w_branch_b, v_w_out, v_norm_mlp_g, v_w_up, v_w_down, v_norm_ple_g, v_w_ple_gate, v_w_ple_proj, v_final_norm_g):
    w = dict(zip(_NAMES, (norm_mix_g, w_in, conv_qk, b_if, mlstm_norm_g, sinks, w_branch_a, w_branch_b, w_out,
                          norm_mlp_g, w_up, w_down, norm_ple_g, w_ple_gate, w_ple_proj, final_norm_g)))
    m = dict(zip(_NAMES, (m_norm_mix_g, m_w_in, m_conv_qk, m_b_if, m_mlstm_norm_g, m_sinks, m_w_branch_a,
                          m_w_branch_b, m_w_out, m_norm_mlp_g, m_w_up, m_w_down, m_norm_ple_g, m_w_ple_gate,
                          m_w_ple_proj, m_final_norm_g)))
    v = dict(zip(_NAMES, (v_norm_mix_g, v_w_in, v_conv_qk, v_b_if, v_mlstm_norm_g, v_sinks, v_w_branch_a,
                          v_w_branch_b, v_w_out, v_norm_mlp_g, v_w_up, v_w_down, v_norm_ple_g, v_w_ple_gate,
                          v_w_ple_proj, v_final_norm_g)))
    return _step(x, p, loss_target, w, m, v)
```

```python
import jax
import jax.numpy as jnp
from jax import lax
from jax.experimental import pallas as pl
from jax.experimental.pallas import tpu as pltpu

F32 = jnp.float32
BF16 = jnp.bfloat16

D = 1024
PLE = 256
MLH = 4
DQK = 128
DV = 256
CONV = 4
CHUNK = 128
SWH = 16
SWKV = 4
SWG = SWH // SWKV
HD = 64
WIN = 128
DFF = 4096
EPS = 1e-6
N_IN = 6664
NP = 7168
C_QK, C_V, C_O, C_QSW, C_GA, C_GB, C_KV, C_IF = 0, 1024, 2048, 3072, 4096, 5120, 6144, 6656
IFW = NP - C_IF

ADAM_LR = 0.001
ADAM_B1 = 0.9
ADAM_B2 = 0.999
ADAM_EPS = 1e-08
ADAM_WD = 0.01
ADAM_STEP = 10

TOK_TILE = 256
VMEM_LIMIT = 48 * 1024 * 1024


def _params(**kw):
    return pltpu.CompilerParams(vmem_limit_bytes=VMEM_LIMIT, **kw)


def _pick(n, cap):
    if n <= cap:
        return n
    t = cap - cap % 128
    while t > 128 and n % t:
        t -= 128
    assert n % t == 0, (n, cap)
    return t


def _dot(a, b, dims):
    return lax.dot_general(a, b, (dims, ((), ())), preferred_element_type=F32)


def _dot_nn(a, b):
    return _dot(a, b, ((1,), (0,)))


def _dot_nt(a, b):
    return _dot(a, b, ((1,), (1,)))


def _dot_tn(a, b):
    return _dot(a, b, ((0,), (0,)))


def _sigmoid(x):
    return 1.0 / (1.0 + jnp.exp(-x))


def _mm(a, b, mode, out_dtype, name, out_chunks=1):
    bch = b.shape[0] if b.ndim == 3 else 1
    brows, bcols = b.shape[-2], b.shape[-1] * bch
    if mode == "nn":
        (m, k), (k2, n) = a.shape, (brows, bcols)
    elif mode == "nt":
        (m, k), (n, k2) = a.shape, (brows, bcols)
    else:
        (k, m), (k2, n) = a.shape, (brows, bcols)
    assert k == k2, (a.shape, b.shape, mode)
    n_cap = n // max(out_chunks, 1 if mode == "nt" else bch)
    k_cap = k // bch if mode == "nt" else k
    tm, tn, tk = _pick(m, 1024), _pick(n_cap, 1024), _pick(k_cap, 2048)
    nk = k // tk
    if mode == "nn":
        a_spec = pl.BlockSpec((tm, tk), lambda i, j, kk: (i, kk))
        if bch > 1:
            bpc = (n // bch) // tn
            b_spec = pl.BlockSpec((None, tk, tn), lambda i, j, kk: (j // bpc, kk, j % bpc))
        else:
            b_spec = pl.BlockSpec((tk, tn), lambda i, j, kk: (kk, j))
        dot = _dot_nn
    elif mode == "nt":
        a_spec = pl.BlockSpec((tm, tk), lambda i, j, kk: (i, kk))
        if bch > 1:
            bpc = (k // bch) // tk
            b_spec = pl.BlockSpec((None, tn, tk), lambda i, j, kk: (kk // bpc, j, kk % bpc))
        else:
            b_spec = pl.BlockSpec((tn, tk), lambda i, j, kk: (j, kk))
        dot = _dot_nt
    else:
        assert bch == 1
        a_spec = pl.BlockSpec((tk, tm), lambda i, j, kk: (kk, i))
        b_spec = pl.BlockSpec((tk, tn), lambda i, j, kk: (kk, j))
        dot = _dot_tn
    if out_chunks > 1:
        npc = (n // out_chunks) // tn
        out_spec = pl.BlockSpec((None, tm, tn), lambda i, j, kk: (j // npc, i, j % npc))
        out_shape = jax.ShapeDtypeStruct((out_chunks, m, n // out_chunks), out_dtype)
    else:
        out_spec = pl.BlockSpec((tm, tn), lambda i, j, kk: (i, j))
        out_shape = jax.ShapeDtypeStruct((m, n), out_dtype)

    def body(a_ref, b_ref, o_ref, acc_ref):
        kk = pl.program_id(2)

        @pl.when(kk == 0)
        def _():
            acc_ref[...] = jnp.zeros_like(acc_ref)

        acc_ref[...] += dot(a_ref[...], b_ref[...])

        @pl.when(kk == nk - 1)
        def _():
            o_ref[...] = acc_ref[...].astype(out_dtype)

    return pl.pallas_call(
        body, name=name, grid=(m // tm, n // tn, nk),
        in_specs=[a_spec, b_spec], out_specs=out_spec, out_shape=out_shape,
        scratch_shapes=[pltpu.VMEM((tm, tn), F32)],
        compiler_params=_params(dimension_semantics=("parallel", "parallel", "arbitrary")),
    )(a, b)


def _tile(col0=0):
    return lambda tm, tn: pl.BlockSpec((tm, tn), lambda i, j, kk: (i, col0 // tn + j))


def _row():
    return lambda tm, tn: pl.BlockSpec((1, tn), lambda i, j, kk: (0, j))


def _mm_ep(pairs, mode, name, epilogue, ins, outs, tm, tn, aliases=None):
    a0, b0 = pairs[0]
    bch = b0.shape[0] if b0.ndim == 3 else 1
    m, k = a0.shape
    tm = _pick(m, tm)
    n = b0.shape[-1] * bch if mode == "nn" else b0.shape[-2]
    tk = _pick(k // bch if mode == "nt" else k, 2048)
    nk = k // tk
    a_spec = pl.BlockSpec((tm, tk), lambda i, j, kk: (i, kk))
    if mode == "nn":
        dot = _dot_nn
        if bch > 1:
            bpc = (n // bch) // tn
            b_spec = pl.BlockSpec((None, tk, tn), lambda i, j, kk: (j // bpc, kk, j % bpc))
        else:
            b_spec = pl.BlockSpec((tk, tn), lambda i, j, kk: (kk, j))
    else:
        dot = _dot_nt
        if bch > 1:
            bpc = (k // bch) // tk
            b_spec = pl.BlockSpec((None, tn, tk), lambda i, j, kk: (kk // bpc, j, kk % bpc))
        else:
            b_spec = pl.BlockSpec((tn, tk), lambda i, j, kk: (j, kk))
    npair, nin, nout = len(pairs), len(ins), len(outs)

    def body(*refs):
        ab = refs[:2 * npair]
        in_refs = refs[2 * npair:2 * npair + nin]
        out_refs = refs[2 * npair + nin:2 * npair + nin + nout]
        accs = refs[2 * npair + nin + nout:]
        i, j, kk = pl.program_id(0), pl.program_id(1), pl.program_id(2)
        for p in range(npair):
            prod = dot(ab[2 * p][...], ab[2 * p + 1][...])

            @pl.when(kk == 0)
            def _():
                accs[p][...] = prod

            @pl.when(kk > 0)
            def _():
                accs[p][...] += prod

        @pl.when(kk == nk - 1)
        def _():
            epilogue([acc[...] for acc in accs], in_refs, out_refs, i, j)

    operands = [x for pair in pairs for x in pair] + [a for a, _ in ins]
    io_alias = {2 * npair + i: o for i, o in (aliases or {}).items()}
    return pl.pallas_call(
        body, name=name, grid=(m // tm, n // tn, nk),
        in_specs=[a_spec, b_spec] * npair + [mk(tm, tn) for _, mk in ins],
        out_specs=[mk(tm, tn) for _, mk in outs], out_shape=[s for s, _ in outs],
        scratch_shapes=[pltpu.VMEM((tm, tn), F32)] * npair, input_output_aliases=io_alias,
        compiler_params=_params(dimension_semantics=("arbitrary", "arbitrary", "arbitrary")),
    )(*operands)


def _tok(w, j=0):
    return pl.BlockSpec((TOK_TILE, w), lambda i: (i, j))


def _rep(shape):
    return pl.BlockSpec(shape, lambda i: (0,) * len(shape))


def _rms(x):
    rstd = lax.rsqrt(jnp.mean(x * x, axis=-1, keepdims=True) + EPS)
    return x * rstd, rstd


def _rms_bwd(xn, rstd, dxn):
    return rstd * (dxn - xn * jnp.mean(dxn * xn, axis=-1, keepdims=True))


def _norm_fwd(x, g, name):
    t = x.shape[0]

    def body(x_ref, g_ref, h_ref):
        xn, _ = _rms(x_ref[...])
        h_ref[...] = (xn * g_ref[...]).astype(BF16)

    return pl.pallas_call(
        body, name=name, grid=(t // TOK_TILE,), in_specs=[_tok(D), _rep((1, D))], out_specs=_tok(D),
        out_shape=jax.ShapeDtypeStruct((t, D), BF16), compiler_params=_params(),
    )(x, g)


def _halo_prev(w, j=0, rows=8):
    r = TOK_TILE // rows
    return pl.BlockSpec((rows, w), lambda i: (jnp.maximum(i * r - 1, 0), j))


def _last8(halo_ref):
    return halo_ref[...].astype(F32)[halo_ref.shape[0] - 8:]


def _halo_next(w, nt, j=0):
    r = TOK_TILE // 8
    return pl.BlockSpec((8, w), lambda i: (jnp.minimum((i + 1) * r, nt * r - 1), j))


def _shift_down(x, halo, s):
    if s == 0:
        return x
    r = pltpu.roll(x, s, 0)
    hs = pltpu.roll(halo, s, 0)
    row = lax.broadcasted_iota(jnp.int32, hs.shape, 0)
    top = jnp.where(row < s, hs, r[0:8])
    return jnp.concatenate([top, r[8:]], axis=0)


def _shift_up(x, halo, s):
    if s == 0:
        return x
    n = x.shape[0]
    r = pltpu.roll(x, n - s, 0)
    hs = pltpu.roll(halo, 8 - s, 0)
    row = lax.broadcasted_iota(jnp.int32, hs.shape, 0)
    bot = jnp.where(row >= 8 - s, hs, r[n - 8:])
    return jnp.concatenate([r[:n - 8], bot], axis=0)


def _bf(x):
    return x.astype(BF16).astype(F32)


def _conv_taps(x, halo, w):
    x, halo, w = _bf(x), _bf(halo), _bf(w)
    acc = x * w[CONV - 1:CONV, :]
    for j in range(CONV - 1):
        acc = acc + _shift_down(x, halo, CONV - 1 - j) * w[j:j + 1, :]
    return acc


_Q_SCALE = DQK ** -0.5


def _qscale_row():
    lane = lax.broadcasted_iota(jnp.int32, (1, D), 1)
    return jnp.where(lane < MLH * DQK, _Q_SCALE, 1.0).astype(F32)


def _conv_silu_fwd(proj, conv_w):
    t = proj.shape[0]

    def body(x_ref, halo_ref, w_ref, o_ref):
        halo = jnp.where(pl.program_id(0) > 0, _last8(halo_ref), 0.0)
        c = _conv_taps(x_ref[...].astype(F32), halo, w_ref[...])
        o_ref[...] = (c * _sigmoid(c) * _qscale_row()).astype(BF16)

    return pl.pallas_call(
        body, name="conv_silu_fwd", grid=(t // TOK_TILE,),
        in_specs=[_tok(D, C_QK // D), _halo_prev(D, C_QK // D, 16), _rep((CONV, D))], out_specs=_tok(D),
        out_shape=jax.ShapeDtypeStruct((t, D), BF16), compiler_params=_params(),
    )(proj, proj, conv_w)


def _conv_silu_bwd_a(proj, conv_w, dqk):
    t = proj.shape[0]

    def body(x_ref, halo_ref, w_ref, d_ref, dc_ref, dw_ref):
        @pl.when(pl.program_id(0) == 0)
        def _():
            dw_ref[...] = jnp.zeros_like(dw_ref)

        halo = jnp.where(pl.program_id(0) > 0, _last8(halo_ref), 0.0)
        x = x_ref[...].astype(F32)
        c = _conv_taps(x, halo, w_ref[...])
        s = _sigmoid(c)
        dc = d_ref[...] * _qscale_row() * (s * (1.0 + c * (1.0 - s)))
        dc_ref[...] = dc
        dcb, xb, halo_b = _bf(dc), _bf(x), _bf(halo)
        for j in range(CONV):
            dw_ref[j:j + 1, :] += jnp.sum(dcb * _shift_down(xb, halo_b, CONV - 1 - j), axis=0, keepdims=True)

    return pl.pallas_call(
        body, name="conv_silu_bwd_a", grid=(t // TOK_TILE,),
        in_specs=[_tok(D, C_QK // D), _halo_prev(D, C_QK // D, 16), _rep((CONV, D)), _tok(D)],
        out_specs=[_tok(D), _rep((CONV, D))],
        out_shape=[jax.ShapeDtypeStruct((t, D), F32), jax.ShapeDtypeStruct((CONV, D), F32)],
        compiler_params=_params(),
    )(proj, proj, conv_w, dqk)


def _conv_silu_bwd_b(dc, conv_w, dproj):
    t = dc.shape[0]
    nt = t // TOK_TILE

    def body(dc_ref, halo_ref, w_ref, _, dx_ref):
        halo = _bf(jnp.where(pl.program_id(0) < nt - 1, halo_ref[...], 0.0))
        dcv = _bf(dc_ref[...])
        w = _bf(w_ref[...])
        acc = dcv * w[CONV - 1:CONV, :]
        for j in range(CONV - 1):
            acc = acc + _shift_up(dcv, halo, CONV - 1 - j) * w[j:j + 1, :]
        dx_ref[...] = acc.astype(BF16)

    return pl.pallas_call(
        body, name="conv_silu_bwd_b", grid=(nt,), in_specs=[_tok(D), _halo_next(D, nt), _rep((CONV, D)), _ANY],
        out_specs=_tok(D, C_QK // D), out_shape=jax.ShapeDtypeStruct((t, NP), BF16),
        input_output_aliases={3: 0}, compiler_params=_params(),
    )(dc, dc, conv_w, dproj)


def _gates_fwd(pre_rows, bias_col):
    t = pre_rows.shape[1]

    def body(p_ref, b_ref, g_ref, s_ref):
        z = p_ref[...] + b_ref[...]
        lf = jnp.minimum(z, 0.0) - jnp.log(1.0 + jnp.exp(-jnp.abs(z)))
        lane = lax.broadcasted_iota(jnp.int32, z.shape, 1) % CHUNK
        cum = lf
        s = 1
        while s < CHUNK:
            cum = cum + jnp.where(lane >= s, pltpu.roll(cum, s, 1), 0.0)
            s *= 2
        sub = lax.broadcasted_iota(jnp.int32, z.shape, 0)
        g_ref[...] = jnp.where(sub < MLH, z, cum)
        s_ref[...] = _sigmoid(-z)

    return pl.pallas_call(
        body, name="gates_fwd",
        out_shape=[jax.ShapeDtypeStruct((8, t), F32), jax.ShapeDtypeStruct((8, t), F32)],
        compiler_params=_params(),
    )(pre_rows, bias_col)


def _chunk_terms(grow, gcol, h, m0):
    i_row, b_row = grow[h:h + 1, :], grow[MLH + h:MLH + h + 1, :]
    i_col, b_col = gcol[:, h:h + 1], gcol[:, MLH + h:MLH + h + 1]
    b_last = b_row[:, CHUNK - 1:CHUNK]
    tt = lax.broadcasted_iota(jnp.int32, (CHUNK, CHUNK), 0)
    ss = lax.broadcasted_iota(jnp.int32, (CHUNK, CHUNK), 1)
    log_d = jnp.where(tt >= ss, b_col - b_row + i_row, -jnp.inf)
    m_t = jnp.maximum(b_col + m0, jnp.max(log_d, axis=1, keepdims=True))
    dm = jnp.exp(log_d - m_t)
    wi = jnp.exp(b_col + m0 - m_t)
    m1 = jnp.maximum(b_last + m0, jnp.max(b_last - b_row + i_row, axis=1, keepdims=True))
    ws = jnp.exp(b_last - b_col + i_col - m1)
    dec = jnp.exp(b_last + m0 - m1)
    return dm, wi, m_t, ws, dec, m1


def _mlstm_fwd(qk, proj, grow, gcol):
    t = qk.shape[0]
    nc = t // CHUNK

    def body(qk_ref, v_ref, grow_ref, gcol_ref, h_ref, cs_ref, st_ref, c_scr, st_scr):
        @pl.when(pl.program_id(0) == 0)
        def _():
            c_scr[...] = jnp.zeros_like(c_scr)
            st_scr[...] = jnp.zeros_like(st_scr)

        grow_v, gcol_v = grow_ref[...], gcol_ref[...]
        for h in range(MLH):
            q = qk_ref[:, h * DQK:(h + 1) * DQK]
            k = qk_ref[:, MLH * DQK + h * DQK:MLH * DQK + (h + 1) * DQK]
            v = v_ref[:, h * DV:(h + 1) * DV]
            c0 = c_scr[h]
            n0 = st_scr[h, 0:1, :]
            m0 = st_scr[h, 1:2, 0:1]
            cs_ref[0, h] = c0
            st_ref[0, h] = st_scr[h]
            dm, wi, m_t, ws, dec, m1 = _chunk_terms(grow_v, gcol_v, h, m0)
            s = _dot_nt(q, k) * dm
            num = wi * _dot_nt(q, c0.astype(BF16)) + _dot_nn(s.astype(BF16), v.astype(BF16))
            den = wi * jnp.sum(q.astype(F32) * n0, axis=1, keepdims=True) + jnp.sum(s, axis=1, keepdims=True)
            h_ref[:, h * DV:(h + 1) * DV] = num / jnp.maximum(jnp.abs(den), jnp.exp(-m_t))
            c_scr[h] = dec * c0 + _dot_tn((ws * v).astype(BF16), k)
            st_scr[h, 0:1, :] = dec * n0 + jnp.sum(ws * k.astype(F32), axis=0, keepdims=True)
            st_scr[h, 1:2, :] = jnp.broadcast_to(m1, (1, DQK))

    return pl.pallas_call(
        body, name="mlstm_fwd", grid=(nc,),
        in_specs=[pl.BlockSpec((CHUNK, D), lambda c: (c, 0)), pl.BlockSpec((CHUNK, D), lambda c: (c, C_V // D)),
                  pl.BlockSpec((8, CHUNK), lambda c: (0, c)), pl.BlockSpec((CHUNK, 8), lambda c: (c, 0))],
        out_specs=[pl.BlockSpec((CHUNK, D), lambda c: (c, 0)),
                   pl.BlockSpec((1, MLH, DV, DQK), lambda c: (c, 0, 0, 0)),
                   pl.BlockSpec((1, MLH, 8, DQK), lambda c: (c, 0, 0, 0))],
        out_shape=[jax.ShapeDtypeStruct((t, D), F32), jax.ShapeDtypeStruct((nc, MLH, DV, DQK), F32),
                   jax.ShapeDtypeStruct((nc, MLH, 8, DQK), F32)],
        scratch_shapes=[pltpu.VMEM((MLH, DV, DQK), F32), pltpu.VMEM((MLH, 8, DQK), F32)],
        compiler_params=_params(dimension_semantics=("arbitrary",)),
    )(qk, proj, grow, gcol)


def _mlstm_bwd(qk, proj, grow, gcol, sneg_col, cs, st, hraw, dh, dproj):
    t = qk.shape[0]
    nc = t // CHUNK

    def rev(c):
        return nc - 1 - c

    def nxt(c):
        return jnp.minimum(nc - c, nc - 1)

    def body(qk_ref, v_ref, grow_ref, gcol_ref, sneg_ref, cs_ref, st_ref, cs1_ref, st1_ref, h_ref, dh_ref, _,
             dqk_ref, dv_ref, dif_ref, dbif_ref, dc_scr, dn_scr):
        @pl.when(pl.program_id(0) == 0)
        def _():
            dc_scr[...] = jnp.zeros_like(dc_scr)
            dn_scr[...] = jnp.zeros_like(dn_scr)
            dbif_ref[...] = jnp.zeros_like(dbif_ref)

        grow_v, gcol_v, sneg = grow_ref[...], gcol_ref[...], sneg_ref[...]
        tt = lax.broadcasted_iota(jnp.int32, (CHUNK, CHUNK), 0)
        ss = lax.broadcasted_iota(jnp.int32, (CHUNK, CHUNK), 1)
        lane8 = lax.broadcasted_iota(jnp.int32, (CHUNK, 8), 1)
        dif = jnp.zeros((CHUNK, 8), F32)
        for h in range(MLH):
            q = qk_ref[:, h * DQK:(h + 1) * DQK]
            k = qk_ref[:, MLH * DQK + h * DQK:MLH * DQK + (h + 1) * DQK]
            qf, kf = q.astype(F32), k.astype(F32)
            v = v_ref[:, h * DV:(h + 1) * DV]
            vb = v.astype(BF16)
            c0 = cs_ref[0, h]
            n0 = st_ref[0, h, 0:1, :]
            m0 = st_ref[0, h, 1:2, 0:1]
            dc1 = dc_scr[h]
            dn1 = dn_scr[h, 0:1, :]
            dm, wi, m_t, ws, dec, _ = _chunk_terms(grow_v, gcol_v, h, m0)
            s = _dot_nt(q, k) * dm
            den = wi * jnp.sum(qf * n0, axis=1, keepdims=True) + jnp.sum(s, axis=1, keepdims=True)
            floor = jnp.exp(-m_t)
            g = jnp.maximum(jnp.abs(den), floor)
            dh_v = dh_ref[:, h * DV:(h + 1) * DV]
            dnum = dh_v / g
            dden = -jnp.sum(dh_v * h_ref[:, h * DV:(h + 1) * DV], axis=1, keepdims=True) / g
            dden = jnp.where(jnp.abs(den) > floor, dden * jnp.sign(den), 0.0)
            dnum_b = dnum.astype(BF16)
            da = ((_dot_nt(dnum_b, vb) + dden) * dm).astype(BF16)
            dc1_b = dc1.astype(BF16)
            dq = _dot_nn(da, k) + wi * (_dot_nn(dnum_b, c0.astype(BF16)) + dden * n0)
            dk = _dot_tn(da, q) + ws * (_dot_nn(vb, dc1_b) + dn1)
            dv = _dot_tn(s.astype(BF16), dnum_b) + ws * _dot_nt(k, dc1_b)
            dqk_ref[:, h * DQK:(h + 1) * DQK] = dq
            dqk_ref[:, MLH * DQK + h * DQK:MLH * DQK + (h + 1) * DQK] = dk
            dv_ref[:, h * DV:(h + 1) * DV] = dv.astype(BF16)
            rk = jnp.sum(kf * dk, axis=1, keepdims=True)
            df = jnp.sum(qf * dq, axis=1, keepdims=True) - rk
            df_row = jnp.sum(jnp.where(tt == ss, df, 0.0), axis=0, keepdims=True)
            suffix = jnp.sum(jnp.where(ss >= tt, df_row, 0.0), axis=1, keepdims=True)
            cross = (jnp.sum(jnp.sum(dc1 * cs1_ref[0, h], axis=1, keepdims=True), axis=0, keepdims=True)
                     + jnp.sum(dn1 * st1_ref[0, h, 0:1, :], axis=1, keepdims=True))
            dpf = (suffix + cross) * sneg[:, MLH + h:MLH + h + 1]
            dif = dif + jnp.where(lane8 == h, rk, 0.0) + jnp.where(lane8 == MLH + h, dpf, 0.0)
            dc_scr[h] = dec * dc1 + _dot_tn((wi * dnum).astype(BF16), q)
            dn_scr[h, 0:1, :] = dec * dn1 + jnp.sum(wi * dden * qf, axis=0, keepdims=True)
        dif_ref[...] = dif
        dbif_ref[...] += jnp.sum(dif, axis=0, keepdims=True)

    return pl.pallas_call(
        body, name="mlstm_bwd", grid=(nc,),
        in_specs=[pl.BlockSpec((CHUNK, D), lambda c: (rev(c), 0)),
                  pl.BlockSpec((CHUNK, D), lambda c: (rev(c), C_V // D)),
                  pl.BlockSpec((8, CHUNK), lambda c: (0, rev(c))),
                  pl.BlockSpec((CHUNK, 8), lambda c: (rev(c), 0)),
                  pl.BlockSpec((CHUNK, 8), lambda c: (rev(c), 0)),
                  pl.BlockSpec((1, MLH, DV, DQK), lambda c: (rev(c), 0, 0, 0)),
                  pl.BlockSpec((1, MLH, 8, DQK), lambda c: (rev(c), 0, 0, 0)),
                  pl.BlockSpec((1, MLH, DV, DQK), lambda c: (nxt(c), 0, 0, 0)),
                  pl.BlockSpec((1, MLH, 8, DQK), lambda c: (nxt(c), 0, 0, 0)),
                  pl.BlockSpec((CHUNK, D), lambda c: (rev(c), 0)),
                  pl.BlockSpec((CHUNK, D), lambda c: (rev(c), 0)), _ANY],
        out_specs=[pl.BlockSpec((CHUNK, D), lambda c: (rev(c), 0)),
                   pl.BlockSpec((CHUNK, D), lambda c: (rev(c), C_V // D)),
                   pl.BlockSpec((CHUNK, 8), lambda c: (rev(c), 0)),
                   pl.BlockSpec((1, 8), lambda c: (0, 0))],
        out_shape=[jax.ShapeDtypeStruct((t, D), F32), jax.ShapeDtypeStruct((t, NP), BF16),
                   jax.ShapeDtypeStruct((t, 8), F32), jax.ShapeDtypeStruct((1, 8), F32)],
        scratch_shapes=[pltpu.VMEM((MLH, DV, DQK), F32), pltpu.VMEM((MLH, 8, DQK), F32)],
        input_output_aliases={11: 1}, compiler_params=_params(dimension_semantics=("arbitrary",)),
    )(qk, proj, grow, gcol, sneg_col, cs, st, cs, st, hraw, dh, dproj)


def _ya_fwd(hraw, proj, g):
    t = hraw.shape[0]

    def body(h_ref, o_ref, g_ref, y_ref):
        so = _sigmoid(o_ref[...].astype(F32))
        for h in range(MLH):
            sl = slice(h * DV, (h + 1) * DV)
            xn, _ = _rms(h_ref[:, sl])
            y_ref[:, sl] = (so[:, sl] * xn * g_ref[:, sl]).astype(BF16)

    return pl.pallas_call(
        body, name="ya_fwd", grid=(t // TOK_TILE,), in_specs=[_tok(D), _tok(D, C_O // D), _rep((1, D))],
        out_specs=_tok(D), out_shape=jax.ShapeDtypeStruct((t, D), BF16), compiler_params=_params(),
    )(hraw, proj, g)


_ANY = pl.BlockSpec(memory_space=pl.ANY)


_SW_SCALE = HD ** -0.5
_KVB = C_KV // (2 * SWKV * HD)


def _swa_mask(n):
    ki = lax.broadcasted_iota(jnp.int32, (2 * WIN, SWG * WIN), 0)
    qi = lax.broadcasted_iota(jnp.int32, (2 * WIN, SWG * WIN), 1) % WIN
    return (ki > qi) & (ki <= qi + WIN) & ((n > 0) | (ki >= WIN))


def _group_rows(x_ref, hk):
    return jnp.concatenate([x_ref[:, (hk * SWG + g) * HD:(hk * SWG + g + 1) * HD] for g in range(SWG)], axis=0)


def _group_lanes(x_ref, hk):
    return jnp.concatenate([x_ref[hk * SWG + g:hk * SWG + g + 1, :] for g in range(SWG)], axis=1)


def _sink_lanes(sink_ref, hk):
    return jnp.concatenate([jnp.broadcast_to(sink_ref[:, hk * SWG + g:hk * SWG + g + 1], (1, WIN))
                            for g in range(SWG)], axis=1)


def _swa_fwd(proj, sinks):
    t = proj.shape[0]
    nb = t // WIN

    def body(q_ref, kvc_ref, kvp_ref, sink_ref, y_ref, lse_ref):
        valid = _swa_mask(pl.program_id(0))
        for hk in range(SWKV):
            ks = slice(hk * HD, (hk + 1) * HD)
            vs = slice(SWKV * HD + hk * HD, SWKV * HD + (hk + 1) * HD)
            kb = jnp.concatenate([kvp_ref[:, ks], kvc_ref[:, ks]], axis=0).astype(BF16)
            vb = jnp.concatenate([kvp_ref[:, vs], kvc_ref[:, vs]], axis=0).astype(BF16)
            q4 = _group_rows(q_ref, hk).astype(BF16)
            sink = _sink_lanes(sink_ref, hk)
            logits = jnp.where(valid, _dot_nt(kb, q4) * _SW_SCALE, -jnp.inf)
            m = jnp.maximum(jnp.max(logits, axis=0, keepdims=True), sink)
            p = jnp.exp(logits - m)
            denom = jnp.sum(p, axis=0, keepdims=True) + jnp.exp(sink - m)
            y4 = _dot_tn((p / denom).astype(BF16), vb).astype(BF16)
            lse4 = m + jnp.log(denom)
            for g in range(SWG):
                hq = hk * SWG + g
                y_ref[:, hq * HD:(hq + 1) * HD] = y4[g * WIN:(g + 1) * WIN]
                lse_ref[hq:hq + 1, :] = lse4[:, g * WIN:(g + 1) * WIN]

    return pl.pallas_call(
        body, name="swa_fwd", grid=(nb,),
        in_specs=[pl.BlockSpec((WIN, D), lambda n: (n, C_QSW // D)),
                  pl.BlockSpec((WIN, 512), lambda n: (n, _KVB)),
                  pl.BlockSpec((WIN, 512), lambda n: (jnp.maximum(n - 1, 0), _KVB)),
                  pl.BlockSpec((1, SWH), lambda n: (0, 0))],
        out_specs=[pl.BlockSpec((WIN, D), lambda n: (n, 0)), pl.BlockSpec((SWH, WIN), lambda n: (0, n))],
        out_shape=[jax.ShapeDtypeStruct((t, D), BF16), jax.ShapeDtypeStruct((SWH, t), F32)],
        compiler_params=_params(),
    )(proj, proj, proj, sinks)


def _swa_bwd(proj, sinks, lse, dyb, dproj):
    t = proj.shape[0]
    nb = t // WIN

    def body(q_ref, kvc_ref, kvp_ref, sink_ref, lse_ref, dy_ref, _, dq_ref, dself_ref, dprev_ref, ds_ref):
        @pl.when(pl.program_id(0) == 0)
        def _():
            ds_ref[...] = jnp.zeros_like(ds_ref)

        valid = _swa_mask(pl.program_id(0))
        for hk in range(SWKV):
            ks = slice(hk * HD, (hk + 1) * HD)
            vs = slice(SWKV * HD + hk * HD, SWKV * HD + (hk + 1) * HD)
            kb = jnp.concatenate([kvp_ref[:, ks], kvc_ref[:, ks]], axis=0).astype(BF16)
            vb = jnp.concatenate([kvp_ref[:, vs], kvc_ref[:, vs]], axis=0).astype(BF16)
            dy4 = _group_rows(dy_ref, hk)
            qb, dyb_ = _group_rows(q_ref, hk).astype(BF16), dy4.astype(BF16)
            lse4 = _group_lanes(lse_ref, hk)
            logits = jnp.where(valid, _dot_nt(kb, qb) * _SW_SCALE, -jnp.inf)
            p = jnp.exp(logits - lse4)
            dpt = _dot_nt(vb, dyb_)
            delta = jnp.sum(p * dpt, axis=0, keepdims=True)
            dsm = (p * (dpt - delta)).astype(BF16)
            dq4 = (_dot_tn(dsm, kb) * _SW_SCALE).astype(BF16)
            dkb = _dot_nn(dsm, qb) * _SW_SCALE
            dvb = _dot_nn(p.astype(BF16), dyb_)
            dsink4 = jnp.exp(_sink_lanes(sink_ref, hk) - lse4) * delta
            for g in range(SWG):
                hq = hk * SWG + g
                dq_ref[:, hq * HD:(hq + 1) * HD] = dq4[g * WIN:(g + 1) * WIN]
                ds_ref[:, hq:hq + 1] += -jnp.sum(dsink4[:, g * WIN:(g + 1) * WIN], axis=1, keepdims=True)
            dprev_ref[:, ks] = dkb[:WIN]
            dself_ref[:, ks] = dkb[WIN:]
            dprev_ref[:, vs] = dvb[:WIN]
            dself_ref[:, vs] = dvb[WIN:]

    return pl.pallas_call(
        body, name="swa_bwd", grid=(nb,),
        in_specs=[pl.BlockSpec((WIN, D), lambda n: (n, C_QSW // D)),
                  pl.BlockSpec((WIN, 512), lambda n: (n, _KVB)),
                  pl.BlockSpec((WIN, 512), lambda n: (jnp.maximum(n - 1, 0), _KVB)),
                  pl.BlockSpec((1, SWH), lambda n: (0, 0)),
                  pl.BlockSpec((SWH, WIN), lambda n: (0, n)),
                  pl.BlockSpec((WIN, D), lambda n: (n, 0)), _ANY],
        out_specs=[pl.BlockSpec((WIN, D), lambda n: (n, C_QSW // D)), pl.BlockSpec((WIN, 512), lambda n: (n, 0)),
                   pl.BlockSpec((WIN, 512), lambda n: (n, 0)), pl.BlockSpec((1, SWH), lambda n: (0, 0))],
        out_shape=[jax.ShapeDtypeStruct((t, NP), BF16), jax.ShapeDtypeStruct((t, 512), F32),
                   jax.ShapeDtypeStruct((t, 512), F32), jax.ShapeDtypeStruct((1, SWH), F32)],
        input_output_aliases={6: 0}, compiler_params=_params(),
    )(proj, proj, proj, sinks, lse, dyb, dproj)


def _kv_combine(dself, dprev, dif, dproj):
    t = dself.shape[0]
    nb = t // WIN

    def body(a_ref, b_ref, dif_ref, _, o_ref):
        nxt = jnp.where(pl.program_id(0) < nb - 1, b_ref[...], 0.0)
        o_ref[:, 0:512] = (a_ref[...] + nxt).astype(BF16)
        lane = lax.broadcasted_iota(jnp.int32, (WIN, 128), 1)
        dif_v = dif_ref[...]
        first = jnp.zeros((WIN, 128), F32)
        for col in range(8):
            first = first + jnp.where(lane == col, dif_v[:, col:col + 1], 0.0)
        o_ref[:, 512:640] = first.astype(BF16)
        o_ref[:, 640:512 + IFW] = jnp.zeros((WIN, IFW - 128), BF16)

    return pl.pallas_call(
        body, name="kv_combine", grid=(nb,),
        in_specs=[pl.BlockSpec((WIN, 512), lambda n: (n, 0)),
                  pl.BlockSpec((WIN, 512), lambda n: (jnp.minimum(n + 1, nb - 1), 0)),
                  pl.BlockSpec((WIN, 8), lambda n: (n, 0)), _ANY],
        out_specs=pl.BlockSpec((WIN, 512 + IFW), lambda n: (n, C_KV // (512 + IFW))),
        out_shape=jax.ShapeDtypeStruct((t, NP), BF16), input_output_aliases={3: 0}, compiler_params=_params(),
    )(dself, dprev, dif, dproj)


def _sds(t, n, dtype):
    return jax.ShapeDtypeStruct((t, n), dtype)


def _proj_in(h0, w_in):
    t = h0.shape[0]

    tn = 2 * IFW

    def epilogue(accs, ins, outs, i, j):
        outs[0][...] = accs[0].astype(BF16)

        @pl.when(j == C_IF // tn)
        def _():
            outs[1][...] = accs[0][:, C_IF % tn:]

    gate_cols = lambda tm, tn: pl.BlockSpec((tm, IFW), lambda i, j, kk: (i, 0))
    return _mm_ep([(h0, w_in)], "nn", "mm_in", epilogue, [],
                  [(_sds(t, NP, BF16), _tile()), (_sds(t, IFW, F32), gate_cols)], 1024, tn)


def _branch_merge(ya, yb, wa, wb, proj):
    t = ya.shape[0]

    def epilogue(accs, ins, outs, i, j):
        za, zb = accs
        merged = _sigmoid(ins[0][...].astype(F32)) * za + _sigmoid(ins[1][...].astype(F32)) * zb
        outs[0][...] = merged.astype(BF16)
        outs[1][...] = za.astype(BF16)
        outs[2][...] = zb.astype(BF16)

    return _mm_ep([(ya, wa), (yb, wb)], "nn", "mm_branch_merge", epilogue, [(proj, _tile(C_GA)), (proj, _tile(C_GB))],
                  [(_sds(t, D, BF16), _tile())] * 3, 1024, 512)


def _dmerged_bwd(dxb, w_out, proj, za, zb):
    t = dxb.shape[0]

    def epilogue(accs, ins, outs, i, j):
        dm = accs[0]
        sa, sb = _sigmoid(ins[0][...].astype(F32)), _sigmoid(ins[1][...].astype(F32))
        outs[0][...] = (dm * sa).astype(BF16)
        outs[1][...] = (dm * sb).astype(BF16)
        outs[2][:, 0:D] = (dm * ins[2][...].astype(F32) * sa * (1.0 - sa)).astype(BF16)
        outs[2][:, D:2 * D] = (dm * ins[3][...].astype(F32) * sb * (1.0 - sb)).astype(BF16)

    gate_cols = lambda tm, tn: pl.BlockSpec((tm, 2 * D), lambda i, j, kk: (i, C_GA // (2 * D)))
    return _mm_ep([(dxb, w_out)], "nt", "mm_dmerged_bwd", epilogue,
                  [(proj, _tile(C_GA)), (proj, _tile(C_GB)), (za, _tile()), (zb, _tile())],
                  [(_sds(t, D, BF16), _tile()), (_sds(t, D, BF16), _tile()), (_sds(t, NP, BF16), gate_cols)], 512, D)


def _dya_bwd(dza, wa, hraw, proj, g, dproj):
    t = dza.shape[0]

    def epilogue(accs, ins, outs, i, j):
        h_ref, o_ref, g_ref, _ = ins
        dh_ref, do_ref, dg_ref = outs

        @pl.when(i == 0)
        def _():
            dg_ref[...] = jnp.zeros_like(dg_ref)

        dy = accs[0]
        so = _sigmoid(o_ref[...].astype(F32))
        for h in range(MLH):
            sl = slice(h * DV, (h + 1) * DV)
            xn, rstd = _rms(h_ref[:, sl])
            gs = g_ref[:, sl]
            do_ref[:, sl] = (dy[:, sl] * xn * gs * so[:, sl] * (1.0 - so[:, sl])).astype(BF16)
            dhn = dy[:, sl] * so[:, sl]
            dg_ref[:, sl] += jnp.sum(dhn * xn, axis=0, keepdims=True)
            dh_ref[:, sl] = _rms_bwd(xn, rstd, dhn * gs)

    return _mm_ep([(dza, wa)], "nt", "mm_dya_bwd", epilogue,
                  [(hraw, _tile()), (proj, _tile(C_O)), (g, _row()), (dproj, lambda tm, tn: _ANY)],
                  [(_sds(t, D, F32), _tile()), (_sds(t, NP, BF16), _tile(C_O)), (_sds(1, D, F32), _row())],
                  512, D, aliases={3: 1})


def _up_act(hn, w_up):
    t = hn.shape[0]

    def epilogue(accs, ins, outs, i, j):
        r = jnp.maximum(accs[0], 0.0)
        outs[0][...] = (r * r).astype(BF16)
        outs[1][...] = accs[0].astype(BF16)

    return _mm_ep([(hn, w_up)], "nn", "mm_up_act", epilogue, [],
                  [(_sds(t, DFF, BF16), _tile()), (_sds(t, DFF, BF16), _tile())], 1024, 1024)


def _da_du(dxb, w_down, u):
    t = dxb.shape[0]

    def epilogue(accs, ins, outs, i, j):
        outs[0][...] = (accs[0] * 2.0 * jnp.maximum(ins[0][...].astype(F32), 0.0)).astype(BF16)

    return _mm_ep([(dxb, w_down)], "nt", "mm_da_du", epilogue, [(u, _tile())], [(_sds(t, DFF, BF16), _tile())],
                  1024, 1024)[0]


def _resid_norm_mm(a, w, x, g, name):
    t = x.shape[0]

    def epilogue(accs, ins, outs, i, j):
        x1 = ins[0][...] + accs[0]
        outs[0][...] = x1
        xn, _ = _rms(x1)
        outs[1][...] = (xn * ins[1][...]).astype(BF16)

    return _mm_ep([(a, w)], "nn", name, epilogue, [(x, _tile()), (g, _row())],
                  [(_sds(t, D, F32), _tile()), (_sds(t, D, BF16), _tile())], 512, D)


def _norm_bwd_mm(dy, w, x, g, dres, name):
    t = x.shape[0]

    def epilogue(accs, ins, outs, i, j):
        @pl.when(i == 0)
        def _():
            outs[2][...] = jnp.zeros_like(outs[2])

        dh = accs[0]
        xn, rstd = _rms(ins[0][...])
        outs[2][...] += jnp.sum(dh * xn, axis=0, keepdims=True)
        dx = ins[2][...] + _rms_bwd(xn, rstd, dh * ins[1][...])
        outs[0][...] = dx
        outs[1][...] = dx.astype(BF16)

    return _mm_ep([(dy, w)], "nt", name, epilogue, [(x, _tile()), (g, _row()), (dres, _tile())],
                  [(_sds(t, D, F32), _tile()), (_sds(t, D, BF16), _tile()), (_sds(1, D, F32), _row())], 512, D)


def _ple_final_mm(hn2, w_gate, x2, pp, target, gf):
    t = x2.shape[0]

    def epilogue(accs, ins, outs, i, j):
        loss_ref, dg_ref, dx_ref, dpp_ref, dgp_ref = outs

        @pl.when(i == 0)
        def _():
            loss_ref[...] = jnp.zeros_like(loss_ref)
            dg_ref[...] = jnp.zeros_like(dg_ref)

        gate = _sigmoid(accs[0])
        pp_v = ins[1][...]
        x3 = ins[0][...] + gate * pp_v
        xn, rstd = _rms(x3)
        gf_v = ins[3][...]
        err = xn * gf_v - ins[2][...]
        loss_ref[...] += (0.5 / D) * jnp.sum(jnp.sum(err * err, axis=1, keepdims=True), axis=0, keepdims=True)
        dy = err * (1.0 / D)
        dg_ref[...] += jnp.sum(dy * xn, axis=0, keepdims=True)
        dx3 = _rms_bwd(xn, rstd, dy * gf_v)
        dx_ref[...] = dx3
        dpp_ref[...] = (dx3 * gate).astype(BF16)
        dgp_ref[...] = (dx3 * pp_v * gate * (1.0 - gate)).astype(BF16)

    one = lambda tm, tn: pl.BlockSpec((1, 1), lambda i, j, kk: (0, 0))
    return _mm_ep([(hn2, w_gate)], "nn", "mm_ple_final", epilogue,
                  [(x2, _tile()), (pp, _tile()), (target, _tile()), (gf, _row())],
                  [(_sds(1, 1, F32), one), (_sds(1, D, F32), _row()), (_sds(t, D, F32), _tile()),
                   (_sds(t, D, BF16), _tile()), (_sds(t, D, BF16), _tile())], 512, D)


def _win_pad(w):
    zeros = jnp.zeros((w.shape[0], IFW - 8), w.dtype)
    return jnp.concatenate([w[:, 0:3072], w[:, 3080:4104], w[:, 4616:6664], w[:, 4104:4616], w[:, 3072:3080], zeros],
                           axis=1)


def _win_unpad(wp):
    return jnp.concatenate([wp[:, 0:3072], wp[:, C_IF:C_IF + 8], wp[:, C_QSW:C_QSW + 1024], wp[:, C_KV:C_KV + 512],
                            wp[:, C_GA:C_GA + 2048]], axis=1)


def _local_step(x, p, target, w):
    t = x.shape[0]
    pb = p.astype(BF16)

    h0 = _norm_fwd(x, w["norm_mix_g"], "norm_mix")
    proj, gates = _proj_in(h0, w["w_in"])
    qk = _conv_silu_fwd(proj, w["conv_qk"])
    grow, sneg_row = _gates_fwd(gates[:, 0:8].T, w["b_if"].reshape(8, 1))
    gcol, sneg_col = grow.T, sneg_row.T
    hraw, cs, st = _mlstm_fwd(qk, proj, grow, gcol)
    ya = _ya_fwd(hraw, proj, w["mlstm_norm_g"])
    yb, lse = _swa_fwd(proj, w["sinks"])
    merged, za, zb = _branch_merge(ya, yb, w["w_branch_a"], w["w_branch_b"], proj)
    x1, hn1 = _resid_norm_mm(merged, w["w_out"], x, w["norm_mlp_g"], "mm_out_norm")
    act, u = _up_act(hn1, w["w_up"])
    x2, hn2 = _resid_norm_mm(act, w["w_down"], x1, w["norm_ple_g"], "mm_down_norm")
    pp = _mm(pb, w["w_ple_proj"], "nn", F32, "mm_ple_proj")
    loss, d_final_g, dx3, dpp, dgpre = _ple_final_mm(hn2, w["w_ple_gate"], x2, pp, target, w["final_norm_g"])

    g = {"final_norm_g": d_final_g}
    g["w_ple_proj"] = _mm(pb, dpp, "tn", F32, "mm_d_ple_proj", out_chunks=4)
    g["w_ple_gate"] = _mm(hn2, dgpre, "tn", F32, "mm_d_ple_gate")
    dx2, dx2b, g["norm_ple_g"] = _norm_bwd_mm(dgpre, w["w_ple_gate"], x2, w["norm_ple_g"], dx3, "mm_dhn2_norm")
    g["w_down"] = _mm(act, dx2b, "tn", F32, "mm_d_down")
    du = _da_du(dx2b, w["w_down"], u)
    g["w_up"] = _mm(hn1, du, "tn", F32, "mm_d_up", out_chunks=4)
    dx1, dx1b, g["norm_mlp_g"] = _norm_bwd_mm(du, w["w_up"], x1, w["norm_mlp_g"], dx2, "mm_dhn1_norm")
    g["w_out"] = _mm(merged, dx1b, "tn", F32, "mm_d_out")
    dza, dzb, dproj = _dmerged_bwd(dx1b, w["w_out"], proj, za, zb)
    g["w_branch_a"] = _mm(ya, dza, "tn", F32, "mm_d_branch_a")
    g["w_branch_b"] = _mm(yb, dzb, "tn", F32, "mm_d_branch_b")
    dyb = _mm(dzb, w["w_branch_b"], "nt", F32, "mm_dyb")
    dhraw, dproj, g["mlstm_norm_g"] = _dya_bwd(dza, w["w_branch_a"], hraw, proj, w["mlstm_norm_g"], dproj)
    dqk, dproj, dif, g["b_if"] = _mlstm_bwd(qk, proj, grow, gcol, sneg_col, cs, st, hraw, dhraw, dproj)
    dc, g["conv_qk"] = _conv_silu_bwd_a(proj, w["conv_qk"], dqk)
    dproj = _conv_silu_bwd_b(dc, w["conv_qk"], dproj)
    dproj, dkv_self, dkv_prev, g["sinks"] = _swa_bwd(proj, w["sinks"], lse, dyb, dproj)
    dproj = _kv_combine(dkv_self, dkv_prev, dif, dproj)
    g["w_in"] = _mm(h0, dproj, "tn", F32, "mm_d_in")
    grad_x, _, g["norm_mix_g"] = _norm_bwd_mm(dproj, w["w_in"], x, w["norm_mix_g"], dx1, "mm_dh0_norm")
    return loss, grad_x, g


_W4 = ("w_branch_a", "w_branch_b", "w_out", "w_ple_gate")
_SHARDED_NAMES = ("w_in", "w_up", "w_down", "w_ple_proj", "conv_qk") + _W4
_SMALL_ROWS = 16
_CONV_ROW = 8


def _group(s):
    return [s["w_in"], jnp.concatenate([s[n] for n in _W4], axis=0), s["w_up"], s["w_down"], s["w_ple_proj"]]


def _ungroup(arrs):
    out = {"w_in": arrs[0], "w_up": arrs[2], "w_down": arrs[3], "w_ple_proj": arrs[4]}
    rows = arrs[1].shape[0] // len(_W4)
    for i, n in enumerate(_W4):
        out[n] = arrs[1][i * rows:(i + 1) * rows]
    return out


def _rows_tile(rows):
    return 256 if rows % 256 == 0 else rows


_SMALL = ("norm_mix_g", "mlstm_norm_g", "norm_mlp_g", "norm_ple_g", "final_norm_g")


def _pack_small(vals, extra=None, conv=None):
    rows = [vals[n].reshape(1, D) for n in _SMALL]
    tail = [vals["b_if"].reshape(1, 8), vals["sinks"].reshape(1, SWH)]
    used = 8 + SWH
    if extra is not None:
        tail.append(extra.reshape(1, 1))
        used += 1
    tail.append(jnp.zeros((1, D - used), F32))
    rows.append(jnp.concatenate(tail, axis=1))
    rows.append(jnp.zeros((_CONV_ROW - len(rows), D), F32))
    rows.append(jnp.zeros((CONV, D), F32) if conv is None else conv)
    rows.append(jnp.zeros((_SMALL_ROWS - _CONV_ROW - CONV, D), F32))
    return jnp.concatenate(rows, axis=0)


def _unpack_small(slab, shapes):
    out = {n: slab[i].reshape(shapes[n]) for i, n in enumerate(_SMALL)}
    out["b_if"] = slab[5, 0:8].reshape(shapes["b_if"])
    out["sinks"] = slab[5, 8:8 + SWH].reshape(shapes["sinks"])
    return out


_MESH = pl.DeviceIdType.MESH
_HBM = pl.BlockSpec(memory_space=pltpu.HBM)
_VMEM = pl.BlockSpec(memory_space=pltpu.VMEM)


def _place():
    x, y, c = lax.axis_index("x"), lax.axis_index("y"), lax.axis_index("c")
    return x, y, c, 2 * x + y


def _chip_peer(x, y, r):
    return (x ^ (r >> 1), y ^ (r & 1))


def _half(ref, which):
    h = ref.shape[-2] // 2
    return pl.ds(which * h, h)


def _allgather_weights(shards, conv):
    n = len(shards)

    def body(*refs):
        ins, conv_ref = refs[:n], refs[n]
        outs, conv_out = refs[n + 1:2 * n + 1], refs[2 * n + 1]
        send_a, recv_a, send_b, recv_b, send_c, recv_c, local_sems = refs[2 * n + 2:]
        x, y, c, j = _place()
        sibling = (x, y, 1 - c)
        local = [pltpu.make_async_copy(ins[k], outs[k].at[j], local_sems.at[k]) for k in range(n)]
        local.append(pltpu.make_async_copy(conv_ref, conv_out.at[j], local_sems.at[n]))
        for cp in local:
            cp.start()

        def copy_a(k, r, chip):
            rows = _half(ins[k], c)
            return pltpu.make_async_remote_copy(
                src_ref=ins[k].at[rows], dst_ref=outs[k].at[chip, rows], send_sem=send_a.at[3 * k + r - 1],
                recv_sem=recv_a.at[3 * k + r - 1], device_id=(*_chip_peer(x, y, r), c), device_id_type=_MESH)

        def copy_b(k, r, chip, which):
            rows = _half(ins[k], which)
            return pltpu.make_async_remote_copy(
                src_ref=outs[k].at[chip, rows], dst_ref=outs[k].at[chip, rows], send_sem=send_b.at[3 * k + r - 1],
                recv_sem=recv_b.at[3 * k + r - 1], device_id=sibling, device_id_type=_MESH)

        def copy_c(r, chip):
            return pltpu.make_async_remote_copy(
                src_ref=conv_ref, dst_ref=conv_out.at[chip], send_sem=send_c.at[r - 1],
                recv_sem=recv_c.at[r - 1], device_id=(*_chip_peer(x, y, r), c), device_id_type=_MESH)

        for k in range(n):
            for r in (1, 2, 3):
                copy_a(k, r, j).start()
        for r in (1, 2, 3):
            copy_c(r, j).start()
        for k in range(n):
            for r in (1, 2, 3):
                copy_a(k, r, j ^ r).wait_recv()
                copy_b(k, r, j ^ r, c).start()
        for k in range(n):
            for r in (1, 2, 3):
                copy_b(k, r, j ^ r, 1 - c).wait_recv()
        for r in (1, 2, 3):
            copy_c(r, j ^ r).wait_recv()
        for k in range(n):
            for r in (1, 2, 3):
                copy_a(k, r, j).wait_send()
                copy_b(k, r, j ^ r, c).wait_send()
        for r in (1, 2, 3):
            copy_c(r, j).wait_send()
        for cp in local:
            cp.wait()

    return pl.pallas_call(
        body, name="allgather_weights",
        out_shape=[jax.ShapeDtypeStruct((4,) + s.shape, s.dtype) for s in shards]
        + [jax.ShapeDtypeStruct((4,) + conv.shape, F32)],
        in_specs=[_HBM] * (n + 1), out_specs=[_HBM] * (n + 1),
        scratch_shapes=[pltpu.SemaphoreType.DMA((3 * n,))] * 4 + [pltpu.SemaphoreType.DMA((3,))] * 2
        + [pltpu.SemaphoreType.DMA((n + 1,))],
    )(*shards, conv)


def _pair_exchange(gs):
    n = len(gs)

    def body(*refs):
        ins, outs, send_sems, recv_sems = refs[:n], refs[n:2 * n], refs[2 * n], refs[2 * n + 1]
        x, y, c, _ = _place()
        cps = [pltpu.make_async_remote_copy(
            src_ref=ins[k].at[:, _half(ins[k], 1 - c)], dst_ref=outs[k], send_sem=send_sems.at[k],
            recv_sem=recv_sems.at[k], device_id=(x, y, 1 - c), device_id_type=_MESH) for k in range(n)]
        for cp in cps:
            cp.start()
        for cp in cps:
            cp.wait()

    return pl.pallas_call(
        body, name="pair_exchange",
        out_shape=[jax.ShapeDtypeStruct((4, g.shape[1] // 2, g.shape[2]), F32) for g in gs],
        in_specs=[_HBM] * n, out_specs=[_HBM] * n, scratch_shapes=[pltpu.SemaphoreType.DMA((n,))] * 2,
    )(*gs)


def _pair_sum(g, theirs, c, name):
    _, h, cols = theirs.shape
    tr = _rows_tile(h)
    nb = h // tr

    def body(c_ref, a_ref, b_ref, o_ref, ob_ref):
        s = a_ref[...] + b_ref[...]
        o_ref[...] = s
        ob_ref[...] = s.astype(BF16)

    blk = pl.BlockSpec((1, tr, cols), lambda k, i, c_ref: (k, i, 0))
    return pl.pallas_call(
        body, name=name,
        grid_spec=pltpu.PrefetchScalarGridSpec(
            num_scalar_prefetch=1, grid=(4, nb),
            in_specs=[pl.BlockSpec((1, tr, cols), lambda k, i, c_ref: (k, c_ref[0] * nb + i, 0)), blk],
            out_specs=[blk, blk]),
        out_shape=[jax.ShapeDtypeStruct(theirs.shape, F32), jax.ShapeDtypeStruct(theirs.shape, BF16)],
        compiler_params=_params(),
    )(c.reshape(1).astype(jnp.int32), g, theirs)


def _chip_exchange(ss):
    n = len(ss)

    def body(*refs):
        ins, outs, send_sems, recv_sems = refs[:n], refs[n:2 * n], refs[2 * n], refs[2 * n + 1]
        x, y, c, j = _place()
        cps = [pltpu.make_async_remote_copy(
            src_ref=ins[k].at[j ^ r], dst_ref=outs[k].at[r - 1], send_sem=send_sems.at[3 * k + r - 1],
            recv_sem=recv_sems.at[3 * k + r - 1], device_id=(*_chip_peer(x, y, r), c), device_id_type=_MESH)
            for k in range(n) for r in (1, 2, 3)]
        for cp in cps:
            cp.start()
        for cp in cps:
            cp.wait()

    return pl.pallas_call(
        body, name="chip_exchange", out_shape=[jax.ShapeDtypeStruct((3,) + s.shape[1:], s.dtype) for s in ss],
        in_specs=[_HBM] * n, out_specs=[_HBM] * n, scratch_shapes=[pltpu.SemaphoreType.DMA((3 * n,))] * 2,
    )(*ss)


def _reduce4(own, others, j, c, name):
    _, h, cols = own.shape
    tr = _rows_tile(h)
    nb = h // tr

    def body(idx_ref, s_ref, a0, a1, a2, o_ref):
        o_ref[...] = ((s_ref[0] + a0[0].astype(F32)) + a1[0].astype(F32)) + a2[0].astype(F32)

    def other(r):
        return pl.BlockSpec((1, tr, cols), lambda i, idx_ref: (r, i, 0))

    return pl.pallas_call(
        body, name=name,
        grid_spec=pltpu.PrefetchScalarGridSpec(
            num_scalar_prefetch=1, grid=(nb,),
            in_specs=[pl.BlockSpec((1, tr, cols), lambda i, idx_ref: (idx_ref[0], i, 0)), other(0), other(1), other(2)],
            out_specs=pl.BlockSpec((tr, cols), lambda i, idx_ref: (idx_ref[1] * nb + i, 0))),
        out_shape=jax.ShapeDtypeStruct((2 * h, cols), F32), compiler_params=_params(),
    )(jnp.stack([j, c]).astype(jnp.int32), own, others, others, others)


def _sibling_share(fulls):
    n = len(fulls)

    def body(*refs):
        outs, send_sems, recv_sems = refs[n:2 * n], refs[2 * n], refs[2 * n + 1]
        x, y, c, _ = _place()
        cps = [pltpu.make_async_remote_copy(
            src_ref=outs[k].at[_half(outs[k], c)], dst_ref=outs[k].at[_half(outs[k], c)], send_sem=send_sems.at[k],
            recv_sem=recv_sems.at[k], device_id=(x, y, 1 - c), device_id_type=_MESH) for k in range(n)]
        for cp in cps:
            cp.start()
        for cp in cps:
            cp.wait()

    return pl.pallas_call(
        body, name="sibling_share", out_shape=[jax.ShapeDtypeStruct(f.shape, F32) for f in fulls],
        in_specs=[_HBM] * n, out_specs=[_HBM] * n, input_output_aliases={k: k for k in range(n)},
        scratch_shapes=[pltpu.SemaphoreType.DMA((n,))] * 2,
    )(*fulls)


def _adamw(w, g, m, v):
    m1 = ADAM_B1 * m + (1.0 - ADAM_B1) * g
    v1 = ADAM_B2 * v + (1.0 - ADAM_B2) * (g * g)
    m_hat = m1 / (1.0 - ADAM_B1 ** ADAM_STEP)
    v_hat = v1 / (1.0 - ADAM_B2 ** ADAM_STEP)
    delta = -ADAM_LR * (m_hat / (jnp.sqrt(v_hat) + ADAM_EPS) + ADAM_WD * w)
    return delta, m1, v1


def _adamw_call(w, g, m, v, name):
    rows, cols = w.shape
    tr = _rows_tile(rows)

    def body(w_ref, g_ref, m_ref, v_ref, d_out, m_out, v_out):
        delta, m1, v1 = _adamw(w_ref[...], g_ref[...], m_ref[...], v_ref[...])
        d_out[...] = delta
        m_out[...] = m1
        v_out[...] = v1

    blk = pl.BlockSpec((tr, cols), lambda i: (i, 0))
    return pl.pallas_call(
        body, name=name, grid=(rows // tr,), in_specs=[blk] * 4, out_specs=[blk] * 3,
        out_shape=[jax.ShapeDtypeStruct((rows, cols), F32)] * 3, compiler_params=_params(),
    )(w, g, m, v)


def _small_allreduce(vals):
    def body(v_ref, out_ref, buf, send_sems, recv_sems):
        x, y, c, j = _place()
        me = 2 * j + c
        buf[0] = v_ref[...]

        def copy(r):
            return pltpu.make_async_remote_copy(
                src_ref=v_ref, dst_ref=buf.at[r], send_sem=send_sems.at[r - 1], recv_sem=recv_sems.at[r - 1],
                device_id=(x ^ (r >> 2), y ^ ((r >> 1) & 1), c ^ (r & 1)), device_id_type=_MESH)

        for r in range(1, 8):
            copy(r).start()
        for r in range(1, 8):
            copy(r).wait()
        acc = buf[me ^ 0]
        for d in range(1, 8):
            acc = acc + buf[me ^ d]
        out_ref[...] = acc

    return pl.pallas_call(
        body, name="small_allreduce", out_shape=jax.ShapeDtypeStruct((_SMALL_ROWS, D), F32),
        in_specs=[_VMEM], out_specs=_VMEM,
        scratch_shapes=[pltpu.VMEM((8, _SMALL_ROWS, D), F32), pltpu.SemaphoreType.DMA((7,)),
                        pltpu.SemaphoreType.DMA((7,))],
    )(vals)


_NAMES = ("norm_mix_g", "w_in", "conv_qk", "b_if", "mlstm_norm_g", "sinks", "w_branch_a", "w_branch_b", "w_out",
          "norm_mlp_g", "w_up", "w_down", "norm_ple_g", "w_ple_gate", "w_ple_proj", "final_norm_g")
_GROUP_NAMES = ("w_in", "w4", "w_up", "w_down", "w_ple_proj")


def _step(x, p, target, w, m, v):
    c = lax.axis_index("c")
    j = 2 * lax.axis_index("x") + lax.axis_index("y")

    def shards(d):
        return {n: d[n][0] for n in _SHARDED_NAMES}

    ws = shards(w)
    gathered = _allgather_weights([a.astype(BF16) for a in _group(ws)], ws["conv_qk"])
    w_in_all, w4_all, w_up_all, w_down_all, w_pp_all, conv_all = gathered
    full = {n: w[n] for n in ("norm_mix_g", "mlstm_norm_g", "norm_mlp_g", "norm_ple_g", "b_if", "sinks")}
    full["final_norm_g"] = w["final_norm_g"].reshape(1, D)
    full["w_in"] = _win_pad(jnp.swapaxes(w_in_all, 0, 1).reshape(D, N_IN))
    w4_all = w4_all.reshape(4, len(_W4), D // 4, D)
    for i, n in enumerate(_W4):
        full[n] = w4_all[:, i].reshape(D, D)
    full["w_up"] = w_up_all
    full["w_down"] = w_down_all.reshape(DFF, D)
    full["w_ple_proj"] = w_pp_all
    full["conv_qk"] = jnp.swapaxes(conv_all, 0, 1).reshape(CONV, D)

    loss, grad_x, g = _local_step(x[0], p[0, 0], target[0], full)

    w_in_g = _win_unpad(g["w_in"])
    by_dest = [jnp.swapaxes(w_in_g.reshape(D, 4, N_IN // 4), 0, 1),
               jnp.stack([g[n].reshape(4, D // 4, D) for n in _W4], axis=1).reshape(4, D, D),
               g["w_up"], g["w_down"].reshape(4, DFF // 4, D), g["w_ple_proj"]]
    theirs = _pair_exchange(by_dest)
    sums = [_pair_sum(a, b, c, "pair_sum_" + n) for a, b, n in zip(by_dest, theirs, _GROUP_NAMES)]
    others = _chip_exchange([s[1] for s in sums])
    halves = [_reduce4(s[0], b, j, c, "reduce4_" + n) for s, b, n in zip(sums, others, _GROUP_NAMES)]
    grads = _sibling_share(halves)

    small_g = _small_allreduce(_pack_small(g, extra=loss, conv=g["conv_qk"]))
    conv_g = lax.dynamic_slice(small_g[_CONV_ROW:_CONV_ROW + CONV], (0, j * (D // 4)), (CONV, D // 4))

    ms, vs = shards(m), shards(v)
    upd = [_adamw_call(wa, ga, ma, va, "adamw_" + n)
           for wa, ga, ma, va, n in zip(_group(ws), grads, _group(ms), _group(vs), _GROUP_NAMES)]
    conv_upd = _adamw_call(ws["conv_qk"], conv_g, ms["conv_qk"], vs["conv_qk"], "adamw_conv")
    small_upd = _adamw_call(_pack_small(w), small_g, _pack_small(m), _pack_small(v), "adamw_small")

    shapes = {n: w[n].shape for n in _NAMES}
    res = []
    for k in range(4):
        big = _ungroup(list(grads) if k == 0 else [u[k - 1] for u in upd])
        big["conv_qk"] = conv_g if k == 0 else conv_upd[k - 1]
        leaves = _unpack_small(small_g if k == 0 else small_upd[k - 1], shapes)
        leaves.update({n: a.reshape(shapes[n]) for n, a in big.items()})
        res.append(leaves)

    out = [small_g[5, 8 + SWH], grad_x[None]]
    for k in range(4):
        out += [res[k][n] for n in _NAMES]
    return tuple(out)


def kernel(x, p, norm_mix_g, w_in, conv_qk, b_if, mlstm_norm_g, sinks, w_branch_a, w_branch_b, w_out, norm_mlp_g, w_up, w_down, norm_ple_g, w_ple_gate, w_ple_proj, final_norm_g, loss_target, m_norm_mix_g, m_w_in, m_conv_qk, m_b_if, m_mlstm_norm_g, m_sinks, m_w_branch_a, m_w_branch_b, m_w_out, m_norm_mlp_g, m_w_up, m_w_down, m_norm_ple_g, m_w_ple_gate, m_w_ple_proj, m_final_norm_g, v_norm_mix_g, v_w_in, v_conv_qk, v_b_if, v_mlstm_norm_g, v_sinks, v_w_branch_a, v_w_branch_b, v_w_out, v_norm_mlp_g, v_w_up, v_w_down, v_norm_ple_g, v_w_ple_gate, v_w_ple_proj, v_final_norm_g):
    w = dict(zip(_NAMES, (norm_mix_g, w_in, conv_qk, b_if, mlstm_norm_g, sinks, w_branch_a, w_branch_b, w_out,
                          norm_mlp_g, w_up, w_down, norm_ple_g, w_ple_gate, w_ple_proj, final_norm_g)))
    m = dict(zip(_NAMES, (m_norm_mix_g, m_w_in, m_conv_qk, m_b_if, m_mlstm_norm_g, m_sinks, m_w_branch_a,
                          m_w_branch_b, m_w_out, m_norm_mlp_g, m_w_up, m_w_down, m_norm_ple_g, m_w_ple_gate,
                          m_w_ple_proj, m_final_norm_g)))
    v = dict(zip(_NAMES, (v_norm_mix_g, v_w_in, v_conv_qk, v_b_if, v_mlstm_norm_g, v_sinks, v_w_branch_a,
                          v_w_branch_b, v_w_out, v_norm_mlp_g, v_w_up, v_w_down, v_norm_ple_g, v_w_ple_gate,
                          v_w_ple_proj, v_final_norm_g)))
    return _step(x, p, loss_target, w, m, v)
```

```python
import jax
import jax.numpy as jnp
from jax import lax
from jax.experimental import pallas as pl
from jax.experimental.pallas import tpu as pltpu

F32 = jnp.float32
BF16 = jnp.bfloat16

D = 1024
PLE = 256
MLH = 4
DQK = 128
DV = 256
CONV = 4
CHUNK = 128
SWH = 16
SWKV = 4
SWG = SWH // SWKV
HD = 64
WIN = 128
DFF = 4096
EPS = 1e-6
N_IN = 6664
NP = 7168
C_QK, C_V, C_O, C_QSW, C_GA, C_GB, C_KV, C_IF = 0, 1024, 2048, 3072, 4096, 5120, 6144, 6656
IFW = NP - C_IF

ADAM_LR = 0.001
ADAM_B1 = 0.9
ADAM_B2 = 0.999
ADAM_EPS = 1e-08
ADAM_WD = 0.01
ADAM_STEP = 10

TOK_TILE = 256
VMEM_LIMIT = 48 * 1024 * 1024


def _params(**kw):
    return pltpu.CompilerParams(vmem_limit_bytes=VMEM_LIMIT, **kw)


def _pick(n, cap):
    if n <= cap:
        return n
    t = cap - cap % 128
    while t > 128 and n % t:
        t -= 128
    assert n % t == 0, (n, cap)
    return t


def _dot(a, b, dims):
    return lax.dot_general(a, b, (dims, ((), ())), preferred_element_type=F32)


def _dot_nn(a, b):
    return _dot(a, b, ((1,), (0,)))


def _dot_nt(a, b):
    return _dot(a, b, ((1,), (1,)))


def _dot_tn(a, b):
    return _dot(a, b, ((0,), (0,)))


def _sigmoid(x):
    return 1.0 / (1.0 + jnp.exp(-x))


def _mm(a, b, mode, out_dtype, name, out_chunks=1):
    bch = b.shape[0] if b.ndim == 3 else 1
    brows, bcols = b.shape[-2], b.shape[-1] * bch
    if mode == "nn":
        (m, k), (k2, n) = a.shape, (brows, bcols)
    elif mode == "nt":
        (m, k), (n, k2) = a.shape, (brows, bcols)
    else:
        (k, m), (k2, n) = a.shape, (brows, bcols)
    assert k == k2, (a.shape, b.shape, mode)
    n_cap = n // max(out_chunks, 1 if mode == "nt" else bch)
    k_cap = k // bch if mode == "nt" else k
    tm, tn, tk = _pick(m, 1024), _pick(n_cap, 1024), _pick(k_cap, 2048)
    nk = k // tk
    if mode == "nn":
        a_spec = pl.BlockSpec((tm, tk), lambda i, j, kk: (i, kk))
        if bch > 1:
            bpc = (n // bch) // tn
            b_spec = pl.BlockSpec((None, tk, tn), lambda i, j, kk: (j // bpc, kk, j % bpc))
        else:
            b_spec = pl.BlockSpec((tk, tn), lambda i, j, kk: (kk, j))
        dot = _dot_nn
    elif mode == "nt":
        a_spec = pl.BlockSpec((tm, tk), lambda i, j, kk: (i, kk))
        if bch > 1:
            bpc = (k // bch) // tk
            b_spec = pl.BlockSpec((None, tn, tk), lambda i, j, kk: (kk // bpc, j, kk % bpc))
        else:
            b_spec = pl.BlockSpec((tn, tk), lambda i, j, kk: (j, kk))
        dot = _dot_nt
    else:
        assert bch == 1
        a_spec = pl.BlockSpec((tk, tm), lambda i, j, kk: (kk, i))
        b_spec = pl.BlockSpec((tk, tn), lambda i, j, kk: (kk, j))
        dot = _dot_tn
    if out_chunks > 1:
        npc = (n // out_chunks) // tn
        out_spec = pl.BlockSpec((None, tm, tn), lambda i, j, kk: (j // npc, i, j % npc))
        out_shape = jax.ShapeDtypeStruct((out_chunks, m, n // out_chunks), out_dtype)
    else:
        out_spec = pl.BlockSpec((tm, tn), lambda i, j, kk: (i, j))
        out_shape = jax.ShapeDtypeStruct((m, n), out_dtype)

    def body(a_ref, b_ref, o_ref, acc_ref):
        kk = pl.program_id(2)

        @pl.when(kk == 0)
        def _():
            acc_ref[...] = jnp.zeros_like(acc_ref)

        acc_ref[...] += dot(a_ref[...], b_ref[...])

        @pl.when(kk == nk - 1)
        def _():
            o_ref[...] = acc_ref[...].astype(out_dtype)

    return pl.pallas_call(
        body, name=name, grid=(m // tm, n // tn, nk),
        in_specs=[a_spec, b_spec], out_specs=out_spec, out_shape=out_shape,
        scratch_shapes=[pltpu.VMEM((tm, tn), F32)],
        compiler_params=_params(dimension_semantics=("parallel", "parallel", "arbitrary")),
    )(a, b)


def _tile(col0=0):
    return lambda tm, tn: pl.BlockSpec((tm, tn), lambda i, j, kk: (i, col0 // tn + j))


def _row():
    return lambda tm, tn: pl.BlockSpec((1, tn), lambda i, j, kk: (0, j))


def _mm_ep(pairs, mode, name, epilogue, ins, outs, tm, tn, aliases=None):
    a0, b0 = pairs[0]
    bch = b0.shape[0] if b0.ndim == 3 else 1
    m, k = a0.shape
    tm = _pick(m, tm)
    n = b0.shape[-1] * bch if mode == "nn" else b0.shape[-2]
    tk = _pick(k // bch if mode == "nt" else k, 2048)
    nk = k // tk
    a_spec = pl.BlockSpec((tm, tk), lambda i, j, kk: (i, kk))
    if mode == "nn":
        dot = _dot_nn
        if bch > 1:
            bpc = (n // bch) // tn
            b_spec = pl.BlockSpec((None, tk, tn), lambda i, j, kk: (j // bpc, kk, j % bpc))
        else:
            b_spec = pl.BlockSpec((tk, tn), lambda i, j, kk: (kk, j))
    else:
        dot = _dot_nt
        if bch > 1:
            bpc = (k // bch) // tk
            b_spec = pl.BlockSpec((None, tn, tk), lambda i, j, kk: (kk // bpc, j, kk % bpc))
        else:
            b_spec = pl.BlockSpec((tn, tk), lambda i, j, kk: (j, kk))
    npair, nin, nout = len(pairs), len(ins), len(outs)

    def body(*refs):
        ab = refs[:2 * npair]
        in_refs = refs[2 * npair:2 * npair + nin]
        out_refs = refs[2 * npair + nin:2 * npair + nin + nout]
        accs = refs[2 * npair + nin + nout:]
        i, j, kk = pl.program_id(0), pl.program_id(1), pl.program_id(2)
        for p in range(npair):
            prod = dot(ab[2 * p][...], ab[2 * p + 1][...])

            @pl.when(kk == 0)
            def _():
                accs[p][...] = prod

            @pl.when(kk > 0)
            def _():
                accs[p][...] += prod

        @pl.when(kk == nk - 1)
        def _():
            epilogue([acc[...] for acc in accs], in_refs, out_refs, i, j)

    operands = [x for pair in pairs for x in pair] + [a for a, _ in ins]
    io_alias = {2 * npair + i: o for i, o in (aliases or {}).items()}
    return pl.pallas_call(
        body, name=name, grid=(m // tm, n // tn, nk),
        in_specs=[a_spec, b_spec] * npair + [mk(tm, tn) for _, mk in ins],
        out_specs=[mk(tm, tn) for _, mk in outs], out_shape=[s for s, _ in outs],
        scratch_shapes=[pltpu.VMEM((tm, tn), F32)] * npair, input_output_aliases=io_alias,
        compiler_params=_params(dimension_semantics=("arbitrary", "arbitrary", "arbitrary")),
    )(*operands)


def _tok(w, j=0):
    return pl.BlockSpec((TOK_TILE, w), lambda i: (i, j))


def _rep(shape):
    return pl.BlockSpec(shape, lambda i: (0,) * len(shape))


def _rms(x):
    rstd = lax.rsqrt(jnp.mean(x * x, axis=-1, keepdims=True) + EPS)
    return x * rstd, rstd


def _rms_bwd(xn, rstd, dxn):
    return rstd * (dxn - xn * jnp.mean(dxn * xn, axis=-1, keepdims=True))


def _norm_fwd(x, g, name):
    t = x.shape[0]

    def body(x_ref, g_ref, h_ref):
        xn, _ = _rms(x_ref[...])
        h_ref[...] = (xn * g_ref[...]).astype(BF16)

    return pl.pallas_call(
        body, name=name, grid=(t // TOK_TILE,), in_specs=[_tok(D), _rep((1, D))], out_specs=_tok(D),
        out_shape=jax.ShapeDtypeStruct((t, D), BF16), compiler_params=_params(),
    )(x, g)


def _halo_prev(w, j=0, rows=8):
    r = TOK_TILE // rows
    return pl.BlockSpec((rows, w), lambda i: (jnp.maximum(i * r - 1, 0), j))


def _last8(halo_ref):
    return halo_ref[...].astype(F32)[halo_ref.shape[0] - 8:]


def _halo_next(w, nt, j=0):
    r = TOK_TILE // 8
    return pl.BlockSpec((8, w), lambda i: (jnp.minimum((i + 1) * r, nt * r - 1), j))


def _shift_down(x, halo, s):
    if s == 0:
        return x
    r = pltpu.roll(x, s, 0)
    hs = pltpu.roll(halo, s, 0)
    row = lax.broadcasted_iota(jnp.int32, hs.shape, 0)
    top = jnp.where(row < s, hs, r[0:8])
    return jnp.concatenate([top, r[8:]], axis=0)


def _shift_up(x, halo, s):
    if s == 0:
        return x
    n = x.shape[0]
    r = pltpu.roll(x, n - s, 0)
    hs = pltpu.roll(halo, 8 - s, 0)
    row = lax.broadcasted_iota(jnp.int32, hs.shape, 0)
    bot = jnp.where(row >= 8 - s, hs, r[n - 8:])
    return jnp.concatenate([r[:n - 8], bot], axis=0)


def _bf(x):
    return x.astype(BF16).astype(F32)


def _conv_taps(x, halo, w):
    x, halo, w = _bf(x), _bf(halo), _bf(w)
    acc = x * w[CONV - 1:CONV, :]
    for j in range(CONV - 1):
        acc = acc + _shift_down(x, halo, CONV - 1 - j) * w[j:j + 1, :]
    return acc


_Q_SCALE = DQK ** -0.5


def _qscale_row():
    lane = lax.broadcasted_iota(jnp.int32, (1, D), 1)
    return jnp.where(lane < MLH * DQK, _Q_SCALE, 1.0).astype(F32)


def _conv_silu_fwd(proj, conv_w):
    t = proj.shape[0]

    def body(x_ref, halo_ref, w_ref, o_ref):
        halo = jnp.where(pl.program_id(0) > 0, _last8(halo_ref), 0.0)
        c = _conv_taps(x_ref[...].astype(F32), halo, w_ref[...])
        o_ref[...] = (c * _sigmoid(c) * _qscale_row()).astype(BF16)

    return pl.pallas_call(
        body, name="conv_silu_fwd", grid=(t // TOK_TILE,),
        in_specs=[_tok(D, C_QK // D), _halo_prev(D, C_QK // D, 16), _rep((CONV, D))], out_specs=_tok(D),
        out_shape=jax.ShapeDtypeStruct((t, D), BF16), compiler_params=_params(),
    )(proj, proj, conv_w)


def _conv_silu_bwd_a(proj, conv_w, dqk):
    t = proj.shape[0]

    def body(x_ref, halo_ref, w_ref, d_ref, dc_ref, dw_ref):
        @pl.when(pl.program_id(0) == 0)
        def _():
            dw_ref[...] = jnp.zeros_like(dw_ref)

        halo = jnp.where(pl.program_id(0) > 0, _last8(halo_ref), 0.0)
        x = x_ref[...].astype(F32)
        c = _conv_taps(x, halo, w_ref[...])
        s = _sigmoid(c)
        dc = d_ref[...] * _qscale_row() * (s * (1.0 + c * (1.0 - s)))
        dc_ref[...] = dc
        dcb, xb, halo_b = _bf(dc), _bf(x), _bf(halo)
        for j in range(CONV):
            dw_ref[j:j + 1, :] += jnp.sum(dcb * _shift_down(xb, halo_b, CONV - 1 - j), axis=0, keepdims=True)

    return pl.pallas_call(
        body, name="conv_silu_bwd_a", grid=(t // TOK_TILE,),
        in_specs=[_tok(D, C_QK // D), _halo_prev(D, C_QK // D, 16), _rep((CONV, D)), _tok(D)],
        out_specs=[_tok(D), _rep((CONV, D))],
        out_shape=[jax.ShapeDtypeStruct((t, D), F32), jax.ShapeDtypeStruct((CONV, D), F32)],
        compiler_params=_params(),
    )(proj, proj, conv_w, dqk)


def _conv_silu_bwd_b(dc, conv_w, dproj):
    t = dc.shape[0]
    nt = t // TOK_TILE

    def body(dc_ref, halo_ref, w_ref, _, dx_ref):
        halo = _bf(jnp.where(pl.program_id(0) < nt - 1, halo_ref[...], 0.0))
        dcv = _bf(dc_ref[...])
        w = _bf(w_ref[...])
        acc = dcv * w[CONV - 1:CONV, :]
        for j in range(CONV - 1):
            acc = acc + _shift_up(dcv, halo, CONV - 1 - j) * w[j:j + 1, :]
        dx_ref[...] = acc.astype(BF16)

    return pl.pallas_call(
        body, name="conv_silu_bwd_b", grid=(nt,), in_specs=[_tok(D), _halo_next(D, nt), _rep((CONV, D)), _ANY],
        out_specs=_tok(D, C_QK // D), out_shape=jax.ShapeDtypeStruct((t, NP), BF16),
        input_output_aliases={3: 0}, compiler_params=_params(),
    )(dc, dc, conv_w, dproj)


def _gates_fwd(pre_rows, bias_col):
    t = pre_rows.shape[1]

    def body(p_ref, b_ref, g_ref, s_ref):
        z = p_ref[...] + b_ref[...]
        lf = jnp.minimum(z, 0.0) - jnp.log(1.0 + jnp.exp(-jnp.abs(z)))
        lane = lax.broadcasted_iota(jnp.int32, z.shape, 1) % CHUNK
        cum = lf
        s = 1
        while s < CHUNK:
            cum = cum + jnp.where(lane >= s, pltpu.roll(cum, s, 1), 0.0)
            s *= 2
        sub = lax.broadcasted_iota(jnp.int32, z.shape, 0)
        g_ref[...] = jnp.where(sub < MLH, z, cum)
        s_ref[...] = _sigmoid(-z)

    return pl.pallas_call(
        body, name="gates_fwd",
        out_shape=[jax.ShapeDtypeStruct((8, t), F32), jax.ShapeDtypeStruct((8, t), F32)],
        compiler_params=_params(),
    )(pre_rows, bias_col)


def _chunk_terms(grow, gcol, h, m0):
    i_row, b_row = grow[h:h + 1, :], grow[MLH + h:MLH + h + 1, :]
    i_col, b_col = gcol[:, h:h + 1], gcol[:, MLH + h:MLH + h + 1]
    b_last = b_row[:, CHUNK - 1:CHUNK]
    tt = lax.broadcasted_iota(jnp.int32, (CHUNK, CHUNK), 0)
    ss = lax.broadcasted_iota(jnp.int32, (CHUNK, CHUNK), 1)
    log_d = jnp.where(tt >= ss, b_col - b_row + i_row, -jnp.inf)
    m_t = jnp.maximum(b_col + m0, jnp.max(log_d, axis=1, keepdims=True))
    dm = jnp.exp(log_d - m_t)
    wi = jnp.exp(b_col + m0 - m_t)
    m1 = jnp.maximum(b_last + m0, jnp.max(b_last - b_row + i_row, axis=1, keepdims=True))
    ws = jnp.exp(b_last - b_col + i_col - m1)
    dec = jnp.exp(b_last + m0 - m1)
    return dm, wi, m_t, ws, dec, m1


def _mlstm_fwd(qk, proj, grow, gcol):
    t = qk.shape[0]
    nc = t // CHUNK

    def body(qk_ref, v_ref, grow_ref, gcol_ref, h_ref, cs_ref, st_ref, c_scr, st_scr):
        @pl.when(pl.program_id(0) == 0)
        def _():
            c_scr[...] = jnp.zeros_like(c_scr)
            st_scr[...] = jnp.zeros_like(st_scr)

        grow_v, gcol_v = grow_ref[...], gcol_ref[...]
        for h in range(MLH):
            q = qk_ref[:, h * DQK:(h + 1) * DQK]
            k = qk_ref[:, MLH * DQK + h * DQK:MLH * DQK + (h + 1) * DQK]
            v = v_ref[:, h * DV:(h + 1) * DV]
            c0 = c_scr[h]
            n0 = st_scr[h, 0:1, :]
            m0 = st_scr[h, 1:2, 0:1]
            cs_ref[0, h] = c0
            st_ref[0, h] = st_scr[h]
            dm, wi, m_t, ws, dec, m1 = _chunk_terms(grow_v, gcol_v, h, m0)
            s = _dot_nt(q, k) * dm
            num = wi * _dot_nt(q, c0.astype(BF16)) + _dot_nn(s.astype(BF16), v.astype(BF16))
            den = wi * jnp.sum(q.astype(F32) * n0, axis=1, keepdims=True) + jnp.sum(s, axis=1, keepdims=True)
            h_ref[:, h * DV:(h + 1) * DV] = num / jnp.maximum(jnp.abs(den), jnp.exp(-m_t))
            c_scr[h] = dec * c0 + _dot_tn((ws * v).astype(BF16), k)
            st_scr[h, 0:1, :] = dec * n0 + jnp.sum(ws * k.astype(F32), axis=0, keepdims=True)
            st_scr[h, 1:2, :] = jnp.broadcast_to(m1, (1, DQK))

    return pl.pallas_call(
        body, name="mlstm_fwd", grid=(nc,),
        in_specs=[pl.BlockSpec((CHUNK, D), lambda c: (c, 0)), pl.BlockSpec((CHUNK, D), lambda c: (c, C_V // D)),
                  pl.BlockSpec((8, CHUNK), lambda c: (0, c)), pl.BlockSpec((CHUNK, 8), lambda c: (c, 0))],
        out_specs=[pl.BlockSpec((CHUNK, D), lambda c: (c, 0)),
                   pl.BlockSpec((1, MLH, DV, DQK), lambda c: (c, 0, 0, 0)),
                   pl.BlockSpec((1, MLH, 8, DQK), lambda c: (c, 0, 0, 0))],
        out_shape=[jax.ShapeDtypeStruct((t, D), F32), jax.ShapeDtypeStruct((nc, MLH, DV, DQK), F32),
                   jax.ShapeDtypeStruct((nc, MLH, 8, DQK), F32)],
        scratch_shapes=[pltpu.VMEM((MLH, DV, DQK), F32), pltpu.VMEM((MLH, 8, DQK), F32)],
        compiler_params=_params(dimension_semantics=("arbitrary",)),
    )(qk, proj, grow, gcol)


def _mlstm_bwd(qk, proj, grow, gcol, sneg_col, cs, st, hraw, dh, dproj):
    t = qk.shape[0]
    nc = t // CHUNK

    def rev(c):
        return nc - 1 - c

    def nxt(c):
        return jnp.minimum(nc - c, nc - 1)

    def body(qk_ref, v_ref, grow_ref, gcol_ref, sneg_ref, cs_ref, st_ref, cs1_ref, st1_ref, h_ref, dh_ref, _,
             dqk_ref, dv_ref, dif_ref, dbif_ref, dc_scr, dn_scr):
        @pl.when(pl.program_id(0) == 0)
        def _():
            dc_scr[...] = jnp.zeros_like(dc_scr)
            dn_scr[...] = jnp.zeros_like(dn_scr)
            dbif_ref[...] = jnp.zeros_like(dbif_ref)

        grow_v, gcol_v, sneg = grow_ref[...], gcol_ref[...], sneg_ref[...]
        tt = lax.broadcasted_iota(jnp.int32, (CHUNK, CHUNK), 0)
        ss = lax.broadcasted_iota(jnp.int32, (CHUNK, CHUNK), 1)
        lane8 = lax.broadcasted_iota(jnp.int32, (CHUNK, 8), 1)
        dif = jnp.zeros((CHUNK, 8), F32)
        for h in range(MLH):
            q = qk_ref[:, h * DQK:(h + 1) * DQK]
            k = qk_ref[:, MLH * DQK + h * DQK:MLH * DQK + (h + 1) * DQK]
            qf, kf = q.astype(F32), k.astype(F32)
            v = v_ref[:, h * DV:(h + 1) * DV]
            vb = v.astype(BF16)
            c0 = cs_ref[0, h]
            n0 = st_ref[0, h, 0:1, :]
            m0 = st_ref[0, h, 1:2, 0:1]
            dc1 = dc_scr[h]
            dn1 = dn_scr[h, 0:1, :]
            dm, wi, m_t, ws, dec, _ = _chunk_terms(grow_v, gcol_v, h, m0)
            s = _dot_nt(q, k) * dm
            den = wi * jnp.sum(qf * n0, axis=1, keepdims=True) + jnp.sum(s, axis=1, keepdims=True)
            floor = jnp.exp(-m_t)
            g = jnp.maximum(jnp.abs(den), floor)
            dh_v = dh_ref[:, h * DV:(h + 1) * DV]
            dnum = dh_v / g
            dden = -jnp.sum(dh_v * h_ref[:, h * DV:(h + 1) * DV], axis=1, keepdims=True) / g
            dden = jnp.where(jnp.abs(den) > floor, dden * jnp.sign(den), 0.0)
            dnum_b = dnum.astype(BF16)
            da = ((_dot_nt(dnum_b, vb) + dden) * dm).astype(BF16)
            dc1_b = dc1.astype(BF16)
            dq = _dot_nn(da, k) + wi * (_dot_nn(dnum_b, c0.astype(BF16)) + dden * n0)
            dk = _dot_tn(da, q) + ws * (_dot_nn(vb, dc1_b) + dn1)
            dv = _dot_tn(s.astype(BF16), dnum_b) + ws * _dot_nt(k, dc1_b)
            dqk_ref[:, h * DQK:(h + 1) * DQK] = dq
            dqk_ref[:, MLH * DQK + h * DQK:MLH * DQK + (h + 1) * DQK] = dk
            dv_ref[:, h * DV:(h + 1) * DV] = dv.astype(BF16)
            rk = jnp.sum(kf * dk, axis=1, keepdims=True)
            df = jnp.sum(qf * dq, axis=1, keepdims=True) - rk
            df_row = jnp.sum(jnp.where(tt == ss, df, 0.0), axis=0, keepdims=True)
            suffix = jnp.sum(jnp.where(ss >= tt, df_row, 0.0), axis=1, keepdims=True)
            cross = (jnp.sum(jnp.sum(dc1 * cs1_ref[0, h], axis=1, keepdims=True), axis=0, keepdims=True)
                     + jnp.sum(dn1 * st1_ref[0, h, 0:1, :], axis=1, keepdims=True))
            dpf = (suffix + cross) * sneg[:, MLH + h:MLH + h + 1]
            dif = dif + jnp.where(lane8 == h, rk, 0.0) + jnp.where(lane8 == MLH + h, dpf, 0.0)
            dc_scr[h] = dec * dc1 + _dot_tn((wi * dnum).astype(BF16), q)
            dn_scr[h, 0:1, :] = dec * dn1 + jnp.sum(wi * dden * qf, axis=0, keepdims=True)
        dif_ref[...] = dif
        dbif_ref[...] += jnp.sum(dif, axis=0, keepdims=True)

    return pl.pallas_call(
        body, name="mlstm_bwd", grid=(nc,),
        in_specs=[pl.BlockSpec((CHUNK, D), lambda c: (rev(c), 0)),
                  pl.BlockSpec((CHUNK, D), lambda c: (rev(c), C_V // D)),
                  pl.BlockSpec((8, CHUNK), lambda c: (0, rev(c))),
                  pl.BlockSpec((CHUNK, 8), lambda c: (rev(c), 0)),
                  pl.BlockSpec((CHUNK, 8), lambda c: (rev(c), 0)),
                  pl.BlockSpec((1, MLH, DV, DQK), lambda c: (rev(c), 0, 0, 0)),
                  pl.BlockSpec((1, MLH, 8, DQK), lambda c: (rev(c), 0, 0, 0)),
                  pl.BlockSpec((1, MLH, DV, DQK), lambda c: (nxt(c), 0, 0, 0)),
                  pl.BlockSpec((1, MLH, 8, DQK), lambda c: (nxt(c), 0, 0, 0)),
                  pl.BlockSpec((CHUNK, D), lambda c: (rev(c), 0)),
                  pl.BlockSpec((CHUNK, D), lambda c: (rev(c), 0)), _ANY],
        out_specs=[pl.BlockSpec((CHUNK, D), lambda c: (rev(c), 0)),
                   pl.BlockSpec((CHUNK, D), lambda c: (rev(c), C_V // D)),
                   pl.BlockSpec((CHUNK, 8), lambda c: (rev(c), 0)),
                   pl.BlockSpec((1, 8), lambda c: (0, 0))],
        out_shape=[jax.ShapeDtypeStruct((t, D), F32), jax.ShapeDtypeStruct((t, NP), BF16),
                   jax.ShapeDtypeStruct((t, 8), F32), jax.ShapeDtypeStruct((1, 8), F32)],
        scratch_shapes=[pltpu.VMEM((MLH, DV, DQK), F32), pltpu.VMEM((MLH, 8, DQK), F32)],
        input_output_aliases={11: 1}, compiler_params=_params(dimension_semantics=("arbitrary",)),
    )(qk, proj, grow, gcol, sneg_col, cs, st, cs, st, hraw, dh, dproj)


def _ya_fwd(hraw, proj, g):
    t = hraw.shape[0]

    def body(h_ref, o_ref, g_ref, y_ref):
        so = _sigmoid(o_ref[...].astype(F32))
        for h in range(MLH):
            sl = slice(h * DV, (h + 1) * DV)
            xn, _ = _rms(h_ref[:, sl])
            y_ref[:, sl] = (so[:, sl] * xn * g_ref[:, sl]).astype(BF16)

    return pl.pallas_call(
        body, name="ya_fwd", grid=(t // TOK_TILE,), in_specs=[_tok(D), _tok(D, C_O // D), _rep((1, D))],
        out_specs=_tok(D), out_shape=jax.ShapeDtypeStruct((t, D), BF16), compiler_params=_params(),
    )(hraw, proj, g)


_ANY = pl.BlockSpec(memory_space=pl.ANY)


_SW_SCALE = HD ** -0.5
_KVB = C_KV // (2 * SWKV * HD)


def _swa_mask(n):
    ki = lax.broadcasted_iota(jnp.int32, (2 * WIN, SWG * WIN), 0)
    qi = lax.broadcasted_iota(jnp.int32, (2 * WIN, SWG * WIN), 1) % WIN
    return (ki > qi) & (ki <= qi + WIN) & ((n > 0) | (ki >= WIN))


def _group_rows(x_ref, hk):
    return jnp.concatenate([x_ref[:, (hk * SWG + g) * HD:(hk * SWG + g + 1) * HD] for g in range(SWG)], axis=0)


def _group_lanes(x_ref, hk):
    return jnp.concatenate([x_ref[hk * SWG + g:hk * SWG + g + 1, :] for g in range(SWG)], axis=1)


def _sink_lanes(sink_ref, hk):
    return jnp.concatenate([jnp.broadcast_to(sink_ref[:, hk * SWG + g:hk * SWG + g + 1], (1, WIN))
                            for g in range(SWG)], axis=1)


def _swa_fwd(proj, sinks):
    t = proj.shape[0]
    nb = t // WIN

    def body(q_ref, kvc_ref, kvp_ref, sink_ref, y_ref, lse_ref):
        valid = _swa_mask(pl.program_id(0))
        for hk in range(SWKV):
            ks = slice(hk * HD, (hk + 1) * HD)
            vs = slice(SWKV * HD + hk * HD, SWKV * HD + (hk + 1) * HD)
            kb = jnp.concatenate([kvp_ref[:, ks], kvc_ref[:, ks]], axis=0).astype(BF16)
            vb = jnp.concatenate([kvp_ref[:, vs], kvc_ref[:, vs]], axis=0).astype(BF16)
            q4 = _group_rows(q_ref, hk).astype(BF16)
            sink = _sink_lanes(sink_ref, hk)
            logits = jnp.where(valid, _dot_nt(kb, q4) * _SW_SCALE, -jnp.inf)
            m = jnp.maximum(jnp.max(logits, axis=0, keepdims=True), sink)
            p = jnp.exp(logits - m)
            denom = jnp.sum(p, axis=0, keepdims=True) + jnp.exp(sink - m)
            y4 = _dot_tn((p / denom).astype(BF16), vb).astype(BF16)
            lse4 = m + jnp.log(denom)
            for g in range(SWG):
                hq = hk * SWG + g
                y_ref[:, hq * HD:(hq + 1) * HD] = y4[g * WIN:(g + 1) * WIN]
                lse_ref[hq:hq + 1, :] = lse4[:, g * WIN:(g + 1) * WIN]

    return pl.pallas_call(
        body, name="swa_fwd", grid=(nb,),
        in_specs=[pl.BlockSpec((WIN, D), lambda n: (n, C_QSW // D)),
                  pl.BlockSpec((WIN, 512), lambda n: (n, _KVB)),
                  pl.BlockSpec((WIN, 512), lambda n: (jnp.maximum(n - 1, 0), _KVB)),
                  pl.BlockSpec((1, SWH), lambda n: (0, 0))],
        out_specs=[pl.BlockSpec((WIN, D), lambda n: (n, 0)), pl.BlockSpec((SWH, WIN), lambda n: (0, n))],
        out_shape=[jax.ShapeDtypeStruct((t, D), BF16), jax.ShapeDtypeStruct((SWH, t), F32)],
        compiler_params=_params(),
    )(proj, proj, proj, sinks)


def _swa_bwd(proj, sinks, lse, dyb, dproj):
    t = proj.shape[0]
    nb = t // WIN

    def body(q_ref, kvc_ref, kvp_ref, sink_ref, lse_ref, dy_ref, _, dq_ref, dself_ref, dprev_ref, ds_ref):
        @pl.when(pl.program_id(0) == 0)
        def _():
            ds_ref[...] = jnp.zeros_like(ds_ref)

        valid = _swa_mask(pl.program_id(0))
        for hk in range(SWKV):
            ks = slice(hk * HD, (hk + 1) * HD)
            vs = slice(SWKV * HD + hk * HD, SWKV * HD + (hk + 1) * HD)
            kb = jnp.concatenate([kvp_ref[:, ks], kvc_ref[:, ks]], axis=0).astype(BF16)
            vb = jnp.concatenate([kvp_ref[:, vs], kvc_ref[:, vs]], axis=0).astype(BF16)
            dy4 = _group_rows(dy_ref, hk)
            qb, dyb_ = _group_rows(q_ref, hk).astype(BF16), dy4.astype(BF16)
            lse4 = _group_lanes(lse_ref, hk)
            logits = jnp.where(valid, _dot_nt(kb, qb) * _SW_SCALE, -jnp.inf)
            p = jnp.exp(logits - lse4)
            dpt = _dot_nt(vb, dyb_)
            delta = jnp.sum(p * dpt, axis=0, keepdims=True)
            dsm = (p * (dpt - delta)).astype(BF16)
            dq4 = (_dot_tn(dsm, kb) * _SW_SCALE).astype(BF16)
            dkb = _dot_nn(dsm, qb) * _SW_SCALE
            dvb = _dot_nn(p.astype(BF16), dyb_)
            dsink4 = jnp.exp(_sink_lanes(sink_ref, hk) - lse4) * delta
            for g in range(SWG):
                hq = hk * SWG + g
                dq_ref[:, hq * HD:(hq + 1) * HD] = dq4[g * WIN:(g + 1) * WIN]
                ds_ref[:, hq:hq + 1] += -jnp.sum(dsink4[:, g * WIN:(g + 1) * WIN], axis=1, keepdims=True)
            dprev_ref[:, ks] = dkb[:WIN]
            dself_ref[:, ks] = dkb[WIN:]
            dprev_ref[:, vs] = dvb[:WIN]
            dself_ref[:, vs] = dvb[WIN:]

    return pl.pallas_call(
        body, name="swa_bwd", grid=(nb,),
        in_specs=[pl.BlockSpec((WIN, D), lambda n: (n, C_QSW // D)),
                  pl.BlockSpec((WIN, 512), lambda n: (n, _KVB)),
                  pl.BlockSpec((WIN, 512), lambda n: (jnp.maximum(n - 1, 0), _KVB)),
                  pl.BlockSpec((1, SWH), lambda n: (0, 0)),
                  pl.BlockSpec((SWH, WIN), lambda n: (0, n)),
                  pl.BlockSpec((WIN, D), lambda n: (n, 0)), _ANY],
        out_specs=[pl.BlockSpec((WIN, D), lambda n: (n, C_QSW // D)), pl.BlockSpec((WIN, 512), lambda n: (n, 0)),
                   pl.BlockSpec((WIN, 512), lambda n: (n, 0)), pl.BlockSpec((1, SWH), lambda n: (0, 0))],
        out_shape=[jax.ShapeDtypeStruct((t, NP), BF16), jax.ShapeDtypeStruct((t, 512), F32),
                   jax.ShapeDtypeStruct((t, 512), F32), jax.ShapeDtypeStruct((1, SWH), F32)],
        input_output_aliases={6: 0}, compiler_params=_params(),
    )(proj, proj, proj, sinks, lse, dyb, dproj)


def _kv_combine(dself, dprev, dif, dproj):
    t = dself.shape[0]
    nb = t // WIN

    def body(a_ref, b_ref, dif_ref, _, o_ref):
        nxt = jnp.where(pl.program_id(0) < nb - 1, b_ref[...], 0.0)
        o_ref[:, 0:512] = (a_ref[...] + nxt).astype(BF16)
        lane = lax.broadcasted_iota(jnp.int32, (WIN, 128), 1)
        dif_v = dif_ref[...]
        first = jnp.zeros((WIN, 128), F32)
        for col in range(8):
            first = first + jnp.where(lane == col, dif_v[:, col:col + 1], 0.0)
        o_ref[:, 512:640] = first.astype(BF16)
        o_ref[:, 640:512 + IFW] = jnp.zeros((WIN, IFW - 128), BF16)

    return pl.pallas_call(
        body, name="kv_combine", grid=(nb,),
        in_specs=[pl.BlockSpec((WIN, 512), lambda n: (n, 0)),
                  pl.BlockSpec((WIN, 512), lambda n: (jnp.minimum(n + 1, nb - 1), 0)),
                  pl.BlockSpec((WIN, 8), lambda n: (n, 0)), _ANY],
        out_specs=pl.BlockSpec((WIN, 512 + IFW), lambda n: (n, C_KV // (512 + IFW))),
        out_shape=jax.ShapeDtypeStruct((t, NP), BF16), input_output_aliases={3: 0}, compiler_params=_params(),
    )(dself, dprev, dif, dproj)


def _sds(t, n, dtype):
    return jax.ShapeDtypeStruct((t, n), dtype)


def _proj_in(h0, w_in):
    t = h0.shape[0]

    tn = 2 * IFW

    def epilogue(accs, ins, outs, i, j):
        outs[0][...] = accs[0].astype(BF16)

        @pl.when(j == C_IF // tn)
        def _():
            outs[1][...] = accs[0][:, C_IF % tn:]

    gate_cols = lambda tm, tn: pl.BlockSpec((tm, IFW), lambda i, j, kk: (i, 0))
    return _mm_ep([(h0, w_in)], "nn", "mm_in", epilogue, [],
                  [(_sds(t, NP, BF16), _tile()), (_sds(t, IFW, F32), gate_cols)], 1024, tn)


def _branch_merge(ya, yb, wa, wb, proj):
    t = ya.shape[0]

    def epilogue(accs, ins, outs, i, j):
        za, zb = accs
        merged = _sigmoid(ins[0][...].astype(F32)) * za + _sigmoid(ins[1][...].astype(F32)) * zb
        outs[0][...] = merged.astype(BF16)
        outs[1][...] = za.astype(BF16)
        outs[2][...] = zb.astype(BF16)

    return _mm_ep([(ya, wa), (yb, wb)], "nn", "mm_branch_merge", epilogue, [(proj, _tile(C_GA)), (proj, _tile(C_GB))],
                  [(_sds(t, D, BF16), _tile())] * 3, 1024, 512)


def _dmerged_bwd(dxb, w_out, proj, za, zb):
    t = dxb.shape[0]

    def epilogue(accs, ins, outs, i, j):
        dm = accs[0]
        sa, sb = _sigmoid(ins[0][...].astype(F32)), _sigmoid(ins[1][...].astype(F32))
        outs[0][...] = (dm * sa).astype(BF16)
        outs[1][...] = (dm * sb).astype(BF16)
        outs[2][:, 0:D] = (dm * ins[2][...].astype(F32) * sa * (1.0 - sa)).astype(BF16)
        outs[2][:, D:2 * D] = (dm * ins[3][...].astype(F32) * sb * (1.0 - sb)).astype(BF16)

    gate_cols = lambda tm, tn: pl.BlockSpec((tm, 2 * D), lambda i, j, kk: (i, C_GA // (2 * D)))
    return _mm_ep([(dxb, w_out)], "nt", "mm_dmerged_bwd", epilogue,
                  [(proj, _tile(C_GA)), (proj, _tile(C_GB)), (za, _tile()), (zb, _tile())],
                  [(_sds(t, D, BF16), _tile()), (_sds(t, D, BF16), _tile()), (_sds(t, NP, BF16), gate_cols)], 512, D)


def _dya_bwd(dza, wa, hraw, proj, g, dproj):
    t = dza.shape[0]

    def epilogue(accs, ins, outs, i, j):
        h_ref, o_ref, g_ref, _ = ins
        dh_ref, do_ref, dg_ref = outs

        @pl.when(i == 0)
        def _():
            dg_ref[...] = jnp.zeros_like(dg_ref)

        dy = accs[0]
        so = _sigmoid(o_ref[...].astype(F32))
        for h in range(MLH):
            sl = slice(h * DV, (h + 1) * DV)
            xn, rstd = _rms(h_ref[:, sl])
            gs = g_ref[:, sl]
            do_ref[:, sl] = (dy[:, sl] * xn * gs * so[:, sl] * (1.0 - so[:, sl])).astype(BF16)
            dhn = dy[:, sl] * so[:, sl]
            dg_ref[:, sl] += jnp.sum(dhn * xn, axis=0, keepdims=True)
            dh_ref[:, sl] = _rms_bwd(xn, rstd, dhn * gs)

    return _mm_ep([(dza, wa)], "nt", "mm_dya_bwd", epilogue,
                  [(hraw, _tile()), (proj, _tile(C_O)), (g, _row()), (dproj, lambda tm, tn: _ANY)],
                  [(_sds(t, D, F32), _tile()), (_sds(t, NP, BF16), _tile(C_O)), (_sds(1, D, F32), _row())],
                  512, D, aliases={3: 1})


def _up_act(hn, w_up):
    t = hn.shape[0]

    def epilogue(accs, ins, outs, i, j):
        r = jnp.maximum(accs[0], 0.0)
        outs[0][...] = (r * r).astype(BF16)
        outs[1][...] = accs[0].astype(BF16)

    return _mm_ep([(hn, w_up)], "nn", "mm_up_act", epilogue, [],
                  [(_sds(t, DFF, BF16), _tile()), (_sds(t, DFF, BF16), _tile())], 1024, 1024)


def _da_du(dxb, w_down, u):
    t = dxb.shape[0]

    def epilogue(accs, ins, outs, i, j):
        outs[0][...] = (accs[0] * 2.0 * jnp.maximum(ins[0][...].astype(F32), 0.0)).astype(BF16)

    return _mm_ep([(dxb, w_down)], "nt", "mm_da_du", epilogue, [(u, _tile())], [(_sds(t, DFF, BF16), _tile())],
                  1024, 1024)[0]


def _resid_norm_mm(a, w, x, g, name):
    t = x.shape[0]

    def epilogue(accs, ins, outs, i, j):
        x1 = ins[0][...] + accs[0]
        outs[0][...] = x1
        xn, _ = _rms(x1)
        outs[1][...] = (xn * ins[1][...]).astype(BF16)

    return _mm_ep([(a, w)], "nn", name, epilogue, [(x, _tile()), (g, _row())],
                  [(_sds(t, D, F32), _tile()), (_sds(t, D, BF16), _tile())], 512, D)


def _norm_bwd_mm(dy, w, x, g, dres, name):
    t = x.shape[0]

    def epilogue(accs, ins, outs, i, j):
        @pl.when(i == 0)
        def _():
            outs[2][...] = jnp.zeros_like(outs[2])

        dh = accs[0]
        xn, rstd = _rms(ins[0][...])
        outs[2][...] += jnp.sum(dh * xn, axis=0, keepdims=True)
        dx = ins[2][...] + _rms_bwd(xn, rstd, dh * ins[1][...])
        outs[0][...] = dx
        outs[1][...] = dx.astype(BF16)

    return _mm_ep([(dy, w)], "nt", name, epilogue, [(x, _tile()), (g, _row()), (dres, _tile())],
                  [(_sds(t, D, F32), _tile()), (_sds(t, D, BF16), _tile()), (_sds(1, D, F32), _row())], 512, D)


def _ple_final_mm(hn2, w_gate, x2, pp, target, gf):
    t = x2.shape[0]

    def epilogue(accs, ins, outs, i, j):
        loss_ref, dg_ref, dx_ref, dpp_ref, dgp_ref = outs

        @pl.when(i == 0)
        def _():
            loss_ref[...] = jnp.zeros_like(loss_ref)
            dg_ref[...] = jnp.zeros_like(dg_ref)

        gate = _sigmoid(accs[0])
        pp_v = ins[1][...]
        x3 = ins[0][...] + gate * pp_v
        xn, rstd = _rms(x3)
        gf_v = ins[3][...]
        err = xn * gf_v - ins[2][...]
        loss_ref[...] += (0.5 / D) * jnp.sum(jnp.sum(err * err, axis=1, keepdims=True), axis=0, keepdims=True)
        dy = err * (1.0 / D)
        dg_ref[...] += jnp.sum(dy * xn, axis=0, keepdims=True)
        dx3 = _rms_bwd(xn, rstd, dy * gf_v)
        dx_ref[...] = dx3
        dpp_ref[...] = (dx3 * gate).astype(BF16)
        dgp_ref[...] = (dx3 * pp_v * gate * (1.0 - gate)).astype(BF16)

    one = lambda tm, tn: pl.BlockSpec((1, 1), lambda i, j, kk: (0, 0))
    return _mm_ep([(hn2, w_gate)], "nn", "mm_ple_final", epilogue,
                  [(x2, _tile()), (pp, _tile()), (target, _tile()), (gf, _row())],
                  [(_sds(1, 1, F32), one), (_sds(1, D, F32), _row()), (_sds(t, D, F32), _tile()),
                   (_sds(t, D, BF16), _tile()), (_sds(t, D, BF16), _tile())], 512, D)


def _win_pad(w):
    zeros = jnp.zeros((w.shape[0], IFW - 8), w.dtype)
    return jnp.concatenate([w[:, 0:3072], w[:, 3080:4104], w[:, 4616:6664], w[:, 4104:4616], w[:, 3072:3080], zeros],
                           axis=1)


def _win_unpad(wp):
    return jnp.concatenate([wp[:, 0:3072], wp[:, C_IF:C_IF + 8], wp[:, C_QSW:C_QSW + 1024], wp[:, C_KV:C_KV + 512],
                            wp[:, C_GA:C_GA + 2048]], axis=1)


def _local_step(x, p, target, w, late_weights=None):
    t = x.shape[0]
    pb = p.astype(BF16)
    w = dict(w)

    h0 = _norm_fwd(x, w["norm_mix_g"], "norm_mix")
    proj, gates = _proj_in(h0, w["w_in"])
    qk = _conv_silu_fwd(proj, w["conv_qk"])
    grow, sneg_row = _gates_fwd(gates[:, 0:8].T, w["b_if"].reshape(8, 1))
    gcol, sneg_col = grow.T, sneg_row.T
    hraw, cs, st = _mlstm_fwd(qk, proj, grow, gcol)
    ya = _ya_fwd(hraw, proj, w["mlstm_norm_g"])
    yb, lse = _swa_fwd(proj, w["sinks"])
    if late_weights is not None:
        w.update(late_weights(yb))
    merged, za, zb = _branch_merge(ya, yb, w["w_branch_a"], w["w_branch_b"], proj)
    x1, hn1 = _resid_norm_mm(merged, w["w_out"], x, w["norm_mlp_g"], "mm_out_norm")
    act, u = _up_act(hn1, w["w_up"])
    x2, hn2 = _resid_norm_mm(act, w["w_down"], x1, w["norm_ple_g"], "mm_down_norm")
    pp = _mm(pb, w["w_ple_proj"], "nn", F32, "mm_ple_proj")
    loss, d_final_g, dx3, dpp, dgpre = _ple_final_mm(hn2, w["w_ple_gate"], x2, pp, target, w["final_norm_g"])

    g = {"final_norm_g": d_final_g}
    g["w_ple_proj"] = _mm(pb, dpp, "tn", F32, "mm_d_ple_proj", out_chunks=4)
    g["w_ple_gate"] = _mm(hn2, dgpre, "tn", F32, "mm_d_ple_gate")
    dx2, dx2b, g["norm_ple_g"] = _norm_bwd_mm(dgpre, w["w_ple_gate"], x2, w["norm_ple_g"], dx3, "mm_dhn2_norm")
    g["w_down"] = _mm(act, dx2b, "tn", F32, "mm_d_down")
    du = _da_du(dx2b, w["w_down"], u)
    g["w_up"] = _mm(hn1, du, "tn", F32, "mm_d_up", out_chunks=4)
    dx1, dx1b, g["norm_mlp_g"] = _norm_bwd_mm(du, w["w_up"], x1, w["norm_mlp_g"], dx2, "mm_dhn1_norm")
    g["w_out"] = _mm(merged, dx1b, "tn", F32, "mm_d_out")
    dza, dzb, dproj = _dmerged_bwd(dx1b, w["w_out"], proj, za, zb)
    g["w_branch_a"] = _mm(ya, dza, "tn", F32, "mm_d_branch_a")
    g["w_branch_b"] = _mm(yb, dzb, "tn", F32, "mm_d_branch_b")
    dyb = _mm(dzb, w["w_branch_b"], "nt", F32, "mm_dyb")
    dhraw, dproj, g["mlstm_norm_g"] = _dya_bwd(dza, w["w_branch_a"], hraw, proj, w["mlstm_norm_g"], dproj)
    dqk, dproj, dif, g["b_if"] = _mlstm_bwd(qk, proj, grow, gcol, sneg_col, cs, st, hraw, dhraw, dproj)
    dc, g["conv_qk"] = _conv_silu_bwd_a(proj, w["conv_qk"], dqk)
    dproj = _conv_silu_bwd_b(dc, w["conv_qk"], dproj)
    dproj, dkv_self, dkv_prev, g["sinks"] = _swa_bwd(proj, w["sinks"], lse, dyb, dproj)
    dproj = _kv_combine(dkv_self, dkv_prev, dif, dproj)
    g["w_in"] = _mm(h0, dproj, "tn", F32, "mm_d_in")
    grad_x, _, g["norm_mix_g"] = _norm_bwd_mm(dproj, w["w_in"], x, w["norm_mix_g"], dx1, "mm_dh0_norm")
    return loss, grad_x, g


_W4 = ("w_branch_a", "w_branch_b", "w_out", "w_ple_gate")
_SHARDED_NAMES = ("w_in", "w_up", "w_down", "w_ple_proj", "conv_qk") + _W4
_SMALL_ROWS = 16
_CONV_ROW = 8


def _group(s):
    return [s["w_in"], jnp.concatenate([s[n] for n in _W4], axis=0), s["w_up"], s["w_down"], s["w_ple_proj"]]


def _ungroup(arrs):
    out = {"w_in": arrs[0], "w_up": arrs[2], "w_down": arrs[3], "w_ple_proj": arrs[4]}
    rows = arrs[1].shape[0] // len(_W4)
    for i, n in enumerate(_W4):
        out[n] = arrs[1][i * rows:(i + 1) * rows]
    return out


def _rows_tile(rows):
    return 256 if rows % 256 == 0 else rows


_SMALL = ("norm_mix_g", "mlstm_norm_g", "norm_mlp_g", "norm_ple_g", "final_norm_g")


def _pack_small(vals, extra=None, conv=None):
    rows = [vals[n].reshape(1, D) for n in _SMALL]
    tail = [vals["b_if"].reshape(1, 8), vals["sinks"].reshape(1, SWH)]
    used = 8 + SWH
    if extra is not None:
        tail.append(extra.reshape(1, 1))
        used += 1
    tail.append(jnp.zeros((1, D - used), F32))
    rows.append(jnp.concatenate(tail, axis=1))
    rows.append(jnp.zeros((_CONV_ROW - len(rows), D), F32))
    rows.append(jnp.zeros((CONV, D), F32) if conv is None else conv)
    rows.append(jnp.zeros((_SMALL_ROWS - _CONV_ROW - CONV, D), F32))
    return jnp.concatenate(rows, axis=0)


def _unpack_small(slab, shapes):
    out = {n: slab[i].reshape(shapes[n]) for i, n in enumerate(_SMALL)}
    out["b_if"] = slab[5, 0:8].reshape(shapes["b_if"])
    out["sinks"] = slab[5, 8:8 + SWH].reshape(shapes["sinks"])
    return out


_MESH = pl.DeviceIdType.MESH
_HBM = pl.BlockSpec(memory_space=pltpu.HBM)
_VMEM = pl.BlockSpec(memory_space=pltpu.VMEM)


def _place():
    x, y, c = lax.axis_index("x"), lax.axis_index("y"), lax.axis_index("c")
    return x, y, c, 2 * x + y


def _chip_peer(x, y, r):
    return (x ^ (r >> 1), y ^ (r & 1))


def _half(ref, which):
    h = ref.shape[-2] // 2
    return pl.ds(which * h, h)


def _allgather_weights(shards, conv):
    n = len(shards)

    def body(*refs):
        ins, conv_ref = refs[:n], refs[n]
        outs, conv_out = refs[n + 1:2 * n + 1], refs[2 * n + 1]
        send_a, recv_a, send_b, recv_b, send_c, recv_c, local_sems = refs[2 * n + 2:]
        x, y, c, j = _place()
        sibling = (x, y, 1 - c)
        local = [pltpu.make_async_copy(ins[k], outs[k].at[j], local_sems.at[k]) for k in range(n)]
        local.append(pltpu.make_async_copy(conv_ref, conv_out.at[j], local_sems.at[n]))
        for cp in local:
            cp.start()

        def copy_a(k, r, chip):
            rows = _half(ins[k], c)
            return pltpu.make_async_remote_copy(
                src_ref=ins[k].at[rows], dst_ref=outs[k].at[chip, rows], send_sem=send_a.at[3 * k + r - 1],
                recv_sem=recv_a.at[3 * k + r - 1], device_id=(*_chip_peer(x, y, r), c), device_id_type=_MESH)

        def copy_b(k, r, chip, which):
            rows = _half(ins[k], which)
            return pltpu.make_async_remote_copy(
                src_ref=outs[k].at[chip, rows], dst_ref=outs[k].at[chip, rows], send_sem=send_b.at[3 * k + r - 1],
                recv_sem=recv_b.at[3 * k + r - 1], device_id=sibling, device_id_type=_MESH)

        def copy_c(r, chip):
            return pltpu.make_async_remote_copy(
                src_ref=conv_ref, dst_ref=conv_out.at[chip], send_sem=send_c.at[r - 1],
                recv_sem=recv_c.at[r - 1], device_id=(*_chip_peer(x, y, r), c), device_id_type=_MESH)

        for k in range(n):
            for r in (1, 2, 3):
                copy_a(k, r, j).start()
        for r in (1, 2, 3):
            copy_c(r, j).start()
        for k in range(n):
            for r in (1, 2, 3):
                copy_a(k, r, j ^ r).wait_recv()
                copy_b(k, r, j ^ r, c).start()
        for k in range(n):
            for r in (1, 2, 3):
                copy_b(k, r, j ^ r, 1 - c).wait_recv()
        for r in (1, 2, 3):
            copy_c(r, j ^ r).wait_recv()
        for k in range(n):
            for r in (1, 2, 3):
                copy_a(k, r, j).wait_send()
                copy_b(k, r, j ^ r, c).wait_send()
        for r in (1, 2, 3):
            copy_c(r, j).wait_send()
        for cp in local:
            cp.wait()

    return pl.pallas_call(
        body, name="allgather_weights",
        out_shape=[jax.ShapeDtypeStruct((4,) + s.shape, s.dtype) for s in shards]
        + [jax.ShapeDtypeStruct((4,) + conv.shape, F32)],
        in_specs=[_HBM] * (n + 1), out_specs=[_HBM] * (n + 1),
        scratch_shapes=[pltpu.SemaphoreType.DMA((3 * n,))] * 4 + [pltpu.SemaphoreType.DMA((3,))] * 2
        + [pltpu.SemaphoreType.DMA((n + 1,))],
    )(*shards, conv)


_SEM = pl.BlockSpec(memory_space=pltpu.SEMAPHORE)
_DATAFLOW = pltpu.SideEffectType.DATAFLOW_SIDE_EFFECTING


def _late_peer_copy(src_ref, land_ref, send_sems, recv_sems, x, y, c, j, r, chip):
    return pltpu.make_async_remote_copy(
        src_ref=src_ref, dst_ref=land_ref.at[chip], send_sem=send_sems.at[r - 1], recv_sem=recv_sems.at[r - 1],
        device_id=(*_chip_peer(x, y, r), c), device_id_type=_MESH)


def _late_gather_start(rest):
    def body(rest_ref, land_ref, send_sems, recv_sems, rest_thru, land_thru, token):
        x, y, c, j = _place()
        for r in (1, 2, 3):
            _late_peer_copy(rest_ref, land_ref, send_sems, recv_sems, x, y, c, j, r, j).start()
        token[...] = jnp.zeros_like(token)

    j = 2 * lax.axis_index("x") + lax.axis_index("y")
    land = lax.dynamic_update_slice(lax.empty((4,) + rest.shape, rest.dtype), rest[None], (j, 0, 0))
    return pl.pallas_call(
        body, name="late_gather_start",
        out_shape=(pltpu.SemaphoreType.DMA((3,)), pltpu.SemaphoreType.DMA((3,)), pltpu.HBM(rest.shape, rest.dtype),
                   pltpu.HBM(land.shape, land.dtype), jax.ShapeDtypeStruct((8, 128), F32)),
        in_specs=(_HBM, _HBM), out_specs=(_SEM, _SEM, _HBM, _HBM, _VMEM), input_output_aliases={0: 2, 1: 3},
        compiler_params=pltpu.CompilerParams(has_side_effects=_DATAFLOW),
    )(pltpu.with_memory_space_constraint(rest, pltpu.HBM), pltpu.with_memory_space_constraint(land, pltpu.HBM))


def _late_gather_wait(send_sems, recv_sems, rest_thru, land_thru, after):
    def body(rest_ref, land_ref, send_sems, recv_sems, after_ref, rest_dead, got_ref):
        x, y, c, j = _place()
        for r in (1, 2, 3):
            cp = _late_peer_copy(rest_ref, land_ref, send_sems, recv_sems, x, y, c, j, r, j ^ r)
            cp.wait_send()
            cp.wait_recv()

    return pl.pallas_call(
        body, name="late_gather_wait",
        out_shape=(pltpu.HBM(rest_thru.shape, rest_thru.dtype), pltpu.HBM(land_thru.shape, land_thru.dtype)),
        in_specs=(_HBM, _HBM, _SEM, _SEM, _ANY), out_specs=(_HBM, _HBM), input_output_aliases={0: 0, 1: 1},
        compiler_params=pltpu.CompilerParams(has_side_effects=_DATAFLOW),
    )(rest_thru, land_thru, send_sems, recv_sems, after)[1]


def _pair_exchange(gs):
    n = len(gs)

    def body(*refs):
        ins, outs, send_sems, recv_sems = refs[:n], refs[n:2 * n], refs[2 * n], refs[2 * n + 1]
        x, y, c, _ = _place()
        cps = [pltpu.make_async_remote_copy(
            src_ref=ins[k].at[:, _half(ins[k], 1 - c)], dst_ref=outs[k], send_sem=send_sems.at[k],
            recv_sem=recv_sems.at[k], device_id=(x, y, 1 - c), device_id_type=_MESH) for k in range(n)]
        for cp in cps:
            cp.start()
        for cp in cps:
            cp.wait()

    return pl.pallas_call(
        body, name="pair_exchange",
        out_shape=[jax.ShapeDtypeStruct((4, g.shape[1] // 2, g.shape[2]), F32) for g in gs],
        in_specs=[_HBM] * n, out_specs=[_HBM] * n, scratch_shapes=[pltpu.SemaphoreType.DMA((n,))] * 2,
    )(*gs)


def _pair_sum(g, theirs, c, name):
    _, h, cols = theirs.shape
    tr = _rows_tile(h)
    nb = h // tr

    def body(c_ref, a_ref, b_ref, o_ref, ob_ref):
        s = a_ref[...] + b_ref[...]
        o_ref[...] = s
        ob_ref[...] = s.astype(BF16)

    blk = pl.BlockSpec((1, tr, cols), lambda k, i, c_ref: (k, i, 0))
    return pl.pallas_call(
        body, name=name,
        grid_spec=pltpu.PrefetchScalarGridSpec(
            num_scalar_prefetch=1, grid=(4, nb),
            in_specs=[pl.BlockSpec((1, tr, cols), lambda k, i, c_ref: (k, c_ref[0] * nb + i, 0)), blk],
            out_specs=[blk, blk]),
        out_shape=[jax.ShapeDtypeStruct(theirs.shape, F32), jax.ShapeDtypeStruct(theirs.shape, BF16)],
        compiler_params=_params(),
    )(c.reshape(1).astype(jnp.int32), g, theirs)


def _chip_exchange(ss):
    n = len(ss)

    def body(*refs):
        ins, outs, send_sems, recv_sems = refs[:n], refs[n:2 * n], refs[2 * n], refs[2 * n + 1]
        x, y, c, j = _place()
        cps = [pltpu.make_async_remote_copy(
            src_ref=ins[k].at[j ^ r], dst_ref=outs[k].at[r - 1], send_sem=send_sems.at[3 * k + r - 1],
            recv_sem=recv_sems.at[3 * k + r - 1], device_id=(*_chip_peer(x, y, r), c), device_id_type=_MESH)
            for k in range(n) for r in (1, 2, 3)]
        for cp in cps:
            cp.start()
        for cp in cps:
            cp.wait()

    return pl.pallas_call(
        body, name="chip_exchange", out_shape=[jax.ShapeDtypeStruct((3,) + s.shape[1:], s.dtype) for s in ss],
        in_specs=[_HBM] * n, out_specs=[_HBM] * n, scratch_shapes=[pltpu.SemaphoreType.DMA((3 * n,))] * 2,
    )(*ss)


def _reduce4(own, others, j, c, name):
    _, h, cols = own.shape
    tr = _rows_tile(h)
    nb = h // tr

    def body(idx_ref, s_ref, a0, a1, a2, o_ref):
        o_ref[...] = ((s_ref[0] + a0[0].astype(F32)) + a1[0].astype(F32)) + a2[0].astype(F32)

    def other(r):
        return pl.BlockSpec((1, tr, cols), lambda i, idx_ref: (r, i, 0))

    return pl.pallas_call(
        body, name=name,
        grid_spec=pltpu.PrefetchScalarGridSpec(
            num_scalar_prefetch=1, grid=(nb,),
            in_specs=[pl.BlockSpec((1, tr, cols), lambda i, idx_ref: (idx_ref[0], i, 0)), other(0), other(1), other(2)],
            out_specs=pl.BlockSpec((tr, cols), lambda i, idx_ref: (idx_ref[1] * nb + i, 0))),
        out_shape=jax.ShapeDtypeStruct((2 * h, cols), F32), compiler_params=_params(),
    )(jnp.stack([j, c]).astype(jnp.int32), own, others, others, others)


def _sibling_share(fulls):
    n = len(fulls)

    def body(*refs):
        outs, send_sems, recv_sems = refs[n:2 * n], refs[2 * n], refs[2 * n + 1]
        x, y, c, _ = _place()
        cps = [pltpu.make_async_remote_copy(
            src_ref=outs[k].at[_half(outs[k], c)], dst_ref=outs[k].at[_half(outs[k], c)], send_sem=send_sems.at[k],
            recv_sem=recv_sems.at[k], device_id=(x, y, 1 - c), device_id_type=_MESH) for k in range(n)]
        for cp in cps:
            cp.start()
        for cp in cps:
            cp.wait()

    return pl.pallas_call(
        body, name="sibling_share", out_shape=[jax.ShapeDtypeStruct(f.shape, F32) for f in fulls],
        in_specs=[_HBM] * n, out_specs=[_HBM] * n, input_output_aliases={k: k for k in range(n)},
        scratch_shapes=[pltpu.SemaphoreType.DMA((n,))] * 2,
    )(*fulls)


def _adamw(w, g, m, v):
    m1 = ADAM_B1 * m + (1.0 - ADAM_B1) * g
    v1 = ADAM_B2 * v + (1.0 - ADAM_B2) * (g * g)
    m_hat = m1 / (1.0 - ADAM_B1 ** ADAM_STEP)
    v_hat = v1 / (1.0 - ADAM_B2 ** ADAM_STEP)
    delta = -ADAM_LR * (m_hat / (jnp.sqrt(v_hat) + ADAM_EPS) + ADAM_WD * w)
    return delta, m1, v1


def _adamw_call(w, g, m, v, name):
    rows, cols = w.shape
    tr = _rows_tile(rows)

    def body(w_ref, g_ref, m_ref, v_ref, d_out, m_out, v_out):
        delta, m1, v1 = _adamw(w_ref[...], g_ref[...], m_ref[...], v_ref[...])
        d_out[...] = delta
        m_out[...] = m1
        v_out[...] = v1

    blk = pl.BlockSpec((tr, cols), lambda i: (i, 0))
    return pl.pallas_call(
        body, name=name, grid=(rows // tr,), in_specs=[blk] * 4, out_specs=[blk] * 3,
        out_shape=[jax.ShapeDtypeStruct((rows, cols), F32)] * 3, compiler_params=_params(),
    )(w, g, m, v)


def _small_allreduce(vals):
    def body(v_ref, out_ref, buf, send_sems, recv_sems):
        x, y, c, j = _place()
        me = 2 * j + c
        buf[0] = v_ref[...]

        def copy(r):
            return pltpu.make_async_remote_copy(
                src_ref=v_ref, dst_ref=buf.at[r], send_sem=send_sems.at[r - 1], recv_sem=recv_sems.at[r - 1],
                device_id=(x ^ (r >> 2), y ^ ((r >> 1) & 1), c ^ (r & 1)), device_id_type=_MESH)

        for r in range(1, 8):
            copy(r).start()
        for r in range(1, 8):
            copy(r).wait()
        acc = buf[me ^ 0]
        for d in range(1, 8):
            acc = acc + buf[me ^ d]
        out_ref[...] = acc

    return pl.pallas_call(
        body, name="small_allreduce", out_shape=jax.ShapeDtypeStruct((_SMALL_ROWS, D), F32),
        in_specs=[_VMEM], out_specs=_VMEM,
        scratch_shapes=[pltpu.VMEM((8, _SMALL_ROWS, D), F32), pltpu.SemaphoreType.DMA((7,)),
                        pltpu.SemaphoreType.DMA((7,))],
    )(vals)


_NAMES = ("norm_mix_g", "w_in", "conv_qk", "b_if", "mlstm_norm_g", "sinks", "w_branch_a", "w_branch_b", "w_out",
          "norm_mlp_g", "w_up", "w_down", "norm_ple_g", "w_ple_gate", "w_ple_proj", "final_norm_g")
_GROUP_NAMES = ("w_in", "w4", "w_up", "w_down", "w_ple_proj")


def _step(x, p, target, w, m, v):
    c = lax.axis_index("c")
    j = 2 * lax.axis_index("x") + lax.axis_index("y")

    def shards(d):
        return {n: d[n][0] for n in _SHARDED_NAMES}

    ws = shards(w)
    w_in_all, conv_all = _allgather_weights([ws["w_in"].astype(BF16)], ws["conv_qk"])
    rows_pp = PLE * (D // 4) // D
    rest = jnp.concatenate([ws[n] for n in _W4] + [ws["w_up"], ws["w_down"], ws["w_ple_proj"].reshape(rows_pp, D)],
                           axis=0)
    rest = (rest + 0.0 * conv_all[0, 0, 0]).astype(BF16)
    send_sems, recv_sems, rest_thru, land_thru, token = _late_gather_start(rest)
    full = {n: w[n] for n in ("mlstm_norm_g", "norm_mlp_g", "norm_ple_g", "b_if", "sinks")}
    full["norm_mix_g"] = w["norm_mix_g"] + token[0, 0]
    full["final_norm_g"] = w["final_norm_g"].reshape(1, D)
    full["w_in"] = _win_pad(jnp.swapaxes(w_in_all, 0, 1).reshape(D, N_IN))
    full["conv_qk"] = jnp.swapaxes(conv_all, 0, 1).reshape(CONV, D)

    def late_weights(after):
        land = _late_gather_wait(send_sems, recv_sems, rest_thru, land_thru, after)
        out = {n: land[:, i * (D // 4):(i + 1) * (D // 4)].reshape(D, D) for i, n in enumerate(_W4)}
        out["w_up"] = land[:, D:2 * D]
        out["w_down"] = land[:, 2 * D:3 * D].reshape(DFF, D)
        out["w_ple_proj"] = land[:, 3 * D:3 * D + rows_pp].reshape(4, PLE, D // 4)
        return out

    loss, grad_x, g = _local_step(x[0], p[0, 0], target[0], full, late_weights)

    w_in_g = _win_unpad(g["w_in"])
    by_dest = [jnp.swapaxes(w_in_g.reshape(D, 4, N_IN // 4), 0, 1),
               jnp.stack([g[n].reshape(4, D // 4, D) for n in _W4], axis=1).reshape(4, D, D),
               g["w_up"], g["w_down"].reshape(4, DFF // 4, D), g["w_ple_proj"]]
    theirs = _pair_exchange(by_dest)
    sums = [_pair_sum(a, b, c, "pair_sum_" + n) for a, b, n in zip(by_dest, theirs, _GROUP_NAMES)]
    others = _chip_exchange([s[1] for s in sums])
    halves = [_reduce4(s[0], b, j, c, "reduce4_" + n) for s, b, n in zip(sums, others, _GROUP_NAMES)]
    grads = _sibling_share(halves)

    small_g = _small_allreduce(_pack_small(g, extra=loss, conv=g["conv_qk"]))
    conv_g = lax.dynamic_slice(small_g[_CONV_ROW:_CONV_ROW + CONV], (0, j * (D // 4)), (CONV, D // 4))

    ms, vs = shards(m), shards(v)
    upd = [_adamw_call(wa, ga, ma, va, "adamw_" + n)
           for wa, ga, ma, va, n in zip(_group(ws), grads, _group(ms), _group(vs), _GROUP_NAMES)]
    conv_upd = _adamw_call(ws["conv_qk"], conv_g, ms["conv_qk"], vs["conv_qk"], "adamw_conv")
    small_upd = _adamw_call(_pack_small(w), small_g, _pack_small(m), _pack_small(v), "adamw_small")

    shapes = {n: w[n].shape for n in _NAMES}
    res = []
    for k in range(4):
        big = _ungroup(list(grads) if k == 0 else [u[k - 1] for u in upd])
        big["conv_qk"] = conv_g if k == 0 else conv_upd[k - 1]
        leaves = _unpack_small(small_g if k == 0 else small_upd[k - 1], shapes)
        leaves.update({n: a.reshape(shapes[n]) for n, a in big.items()})
        res.append(leaves)

    out = [small_g[5, 8 + SWH], grad_x[None]]
    for k in range(4):
        out += [res[k][n] for n in _NAMES]
    return tuple(out)


def kernel(x, p, norm_mix_g, w_in, conv_qk, b_if, mlstm_norm_g, sinks, w_branch_a, w_branch_b, w_out, norm_mlp_g, w_up, w_down, norm_ple_g, w_ple_gate, w_ple_proj, final_norm_g, loss_target, m_norm_mix_g, m_w_in, m_conv_qk, m_b_if, m_mlstm_norm_g, m_sinks, m_w_branch_a, m_w_branch_b, m_w_out, m_norm_mlp_g, m_w_up, m_w_down, m_norm_ple_g, m_w_ple_gate, m_w_ple_proj, m_final_norm_g, v_norm_mix_g, v_w_in, v_conv_qk, v_b_if, v_mlstm_norm_g, v_sinks, v_w_branch_a, v_w_branch_b, v_w_out, v_norm_mlp_g, v_w_up, v_w_down, v_norm_ple_g, v_w_ple_gate, v_w_ple_proj, v_final_norm_g):
    w = dict(zip(_NAMES, (norm_mix_g, w_in, conv_qk, b_if, mlstm_norm_g, sinks, w_branch_a, w_branch_b, w_out,
                          norm_mlp_g, w_up, w_down, norm_ple_g, w_ple_gate, w_ple_proj, final_norm_g)))
    m = dict(zip(_NAMES, (m_norm_mix_g, m_w_in, m_conv_qk, m_b_if, m_mlstm_norm_g, m_sinks, m_w_branch_a,
                          m_w_branch_b, m_w_out, m_norm_mlp_g, m_w_up, m_w_down, m_norm_ple_g, m_w_ple_gate,
                          m_w_ple_proj, m_final_norm_g)))
    v = dict(zip(_NAMES, (v_norm_mix_g, v_w_in, v_conv_qk, v_b_if, v_mlstm_norm_g, v_sinks, v_w_branch_a,
                          v_w_branch_b, v_w_out, v_norm_mlp_g, v_w_up, v_w_down, v_norm_ple_g, v_w_ple_gate,
                          v_w_ple_proj, v_final_norm_g)))
    return _step(x, p, loss_target, w, m, v)
```

```python
import jax
import jax.numpy as jnp
from jax import lax
from jax.experimental import pallas as pl
from jax.experimental.pallas import tpu as pltpu

F32 = jnp.float32
BF16 = jnp.bfloat16

D = 1024
PLE = 256
MLH = 4
DQK = 128
DV = 256
CONV = 4
CHUNK = 128
SWH = 16
SWKV = 4
SWG = SWH // SWKV
HD = 64
WIN = 128
DFF = 4096
EPS = 1e-6
N_IN = 6664
NP = 7168
C_QK, C_V, C_O, C_QSW, C_GA, C_GB, C_KV, C_IF = 0, 1024, 2048, 3072, 4096, 5120, 6144, 6656
IFW = NP - C_IF

ADAM_LR = 0.001
ADAM_B1 = 0.9
ADAM_B2 = 0.999
ADAM_EPS = 1e-08
ADAM_WD = 0.01
ADAM_STEP = 10

TOK_TILE = 256
VMEM_LIMIT = 48 * 1024 * 1024


def _params(**kw):
    return pltpu.CompilerParams(vmem_limit_bytes=VMEM_LIMIT, **kw)


def _pick(n, cap):
    if n <= cap:
        return n
    t = cap - cap % 128
    while t > 128 and n % t:
        t -= 128
    assert n % t == 0, (n, cap)
    return t


def _dot(a, b, dims):
    return lax.dot_general(a, b, (dims, ((), ())), preferred_element_type=F32)


def _dot_nn(a, b):
    return _dot(a, b, ((1,), (0,)))


def _dot_nt(a, b):
    return _dot(a, b, ((1,), (1,)))


def _dot_tn(a, b):
    return _dot(a, b, ((0,), (0,)))


def _sigmoid(x):
    return 1.0 / (1.0 + jnp.exp(-x))


def _mm(a, b, mode, out_dtype, name, out_chunks=1):
    bch = b.shape[0] if b.ndim == 3 else 1
    brows, bcols = b.shape[-2], b.shape[-1] * bch
    if mode == "nn":
        (m, k), (k2, n) = a.shape, (brows, bcols)
    elif mode == "nt":
        (m, k), (n, k2) = a.shape, (brows, bcols)
    else:
        (k, m), (k2, n) = a.shape, (brows, bcols)
    assert k == k2, (a.shape, b.shape, mode)
    n_cap = n // max(out_chunks, 1 if mode == "nt" else bch)
    k_cap = k // bch if mode == "nt" else k
    tm, tn, tk = _pick(m, 1024), _pick(n_cap, 1024), _pick(k_cap, 2048)
    nk = k // tk
    if mode == "nn":
        a_spec = pl.BlockSpec((tm, tk), lambda i, j, kk: (i, kk))
        if bch > 1:
            bpc = (n // bch) // tn
            b_spec = pl.BlockSpec((None, tk, tn), lambda i, j, kk: (j // bpc, kk, j % bpc))
        else:
            b_spec = pl.BlockSpec((tk, tn), lambda i, j, kk: (kk, j))
        dot = _dot_nn
    elif mode == "nt":
        a_spec = pl.BlockSpec((tm, tk), lambda i, j, kk: (i, kk))
        if bch > 1:
            bpc = (k // bch) // tk
            b_spec = pl.BlockSpec((None, tn, tk), lambda i, j, kk: (kk // bpc, j, kk % bpc))
        else:
            b_spec = pl.BlockSpec((tn, tk), lambda i, j, kk: (j, kk))
        dot = _dot_nt
    else:
        assert bch == 1
        a_spec = pl.BlockSpec((tk, tm), lambda i, j, kk: (kk, i))
        b_spec = pl.BlockSpec((tk, tn), lambda i, j, kk: (kk, j))
        dot = _dot_tn
    if out_chunks > 1:
        npc = (n // out_chunks) // tn
        out_spec = pl.BlockSpec((None, tm, tn), lambda i, j, kk: (j // npc, i, j % npc))
        out_shape = jax.ShapeDtypeStruct((out_chunks, m, n // out_chunks), out_dtype)
    else:
        out_spec = pl.BlockSpec((tm, tn), lambda i, j, kk: (i, j))
        out_shape = jax.ShapeDtypeStruct((m, n), out_dtype)

    def body(a_ref, b_ref, o_ref, acc_ref):
        kk = pl.program_id(2)

        @pl.when(kk == 0)
        def _():
            acc_ref[...] = jnp.zeros_like(acc_ref)

        acc_ref[...] += dot(a_ref[...], b_ref[...])

        @pl.when(kk == nk - 1)
        def _():
            o_ref[...] = acc_ref[...].astype(out_dtype)

    return pl.pallas_call(
        body, name=name, grid=(m // tm, n // tn, nk),
        in_specs=[a_spec, b_spec], out_specs=out_spec, out_shape=out_shape,
        scratch_shapes=[pltpu.VMEM((tm, tn), F32)],
        compiler_params=_params(dimension_semantics=("parallel", "parallel", "arbitrary")),
    )(a, b)


def _tile(col0=0):
    return lambda tm, tn: pl.BlockSpec((tm, tn), lambda i, j, kk: (i, col0 // tn + j))


def _row():
    return lambda tm, tn: pl.BlockSpec((1, tn), lambda i, j, kk: (0, j))


def _mm_ep(pairs, mode, name, epilogue, ins, outs, tm, tn, aliases=None):
    a0, b0 = pairs[0]
    bch = b0.shape[0] if b0.ndim == 3 else 1
    m, k = a0.shape
    tm = _pick(m, tm)
    n = b0.shape[-1] * bch if mode == "nn" else b0.shape[-2]
    tk = _pick(k // bch if mode == "nt" else k, 2048)
    nk = k // tk
    a_spec = pl.BlockSpec((tm, tk), lambda i, j, kk: (i, kk))
    if mode == "nn":
        dot = _dot_nn
        if bch > 1:
            bpc = (n // bch) // tn
            b_spec = pl.BlockSpec((None, tk, tn), lambda i, j, kk: (j // bpc, kk, j % bpc))
        else:
            b_spec = pl.BlockSpec((tk, tn), lambda i, j, kk: (kk, j))
    else:
        dot = _dot_nt
        if bch > 1:
            bpc = (k // bch) // tk
            b_spec = pl.BlockSpec((None, tn, tk), lambda i, j, kk: (kk // bpc, j, kk % bpc))
        else:
            b_spec = pl.BlockSpec((tn, tk), lambda i, j, kk: (j, kk))
    npair, nin, nout = len(pairs), len(ins), len(outs)

    def body(*refs):
        ab = refs[:2 * npair]
        in_refs = refs[2 * npair:2 * npair + nin]
        out_refs = refs[2 * npair + nin:2 * npair + nin + nout]
        accs = refs[2 * npair + nin + nout:]
        i, j, kk = pl.program_id(0), pl.program_id(1), pl.program_id(2)
        for p in range(npair):
            prod = dot(ab[2 * p][...], ab[2 * p + 1][...])

            @pl.when(kk == 0)
            def _():
                accs[p][...] = prod

            @pl.when(kk > 0)
            def _():
                accs[p][...] += prod

        @pl.when(kk == nk - 1)
        def _():
            epilogue([acc[...] for acc in accs], in_refs, out_refs, i, j)

    operands = [x for pair in pairs for x in pair] + [a for a, _ in ins]
    io_alias = {2 * npair + i: o for i, o in (aliases or {}).items()}
    return pl.pallas_call(
        body, name=name, grid=(m // tm, n // tn, nk),
        in_specs=[a_spec, b_spec] * npair + [mk(tm, tn) for _, mk in ins],
        out_specs=[mk(tm, tn) for _, mk in outs], out_shape=[s for s, _ in outs],
        scratch_shapes=[pltpu.VMEM((tm, tn), F32)] * npair, input_output_aliases=io_alias,
        compiler_params=_params(dimension_semantics=("arbitrary", "arbitrary", "arbitrary")),
    )(*operands)


def _tok(w, j=0):
    return pl.BlockSpec((TOK_TILE, w), lambda i: (i, j))


def _rep(shape):
    return pl.BlockSpec(shape, lambda i: (0,) * len(shape))


def _rms(x):
    rstd = lax.rsqrt(jnp.mean(x * x, axis=-1, keepdims=True) + EPS)
    return x * rstd, rstd


def _rms_bwd(xn, rstd, dxn):
    return rstd * (dxn - xn * jnp.mean(dxn * xn, axis=-1, keepdims=True))


def _norm_fwd(x, g, name):
    t = x.shape[0]

    def body(x_ref, g_ref, h_ref):
        xn, _ = _rms(x_ref[...])
        h_ref[...] = (xn * g_ref[...]).astype(BF16)

    return pl.pallas_call(
        body, name=name, grid=(t // TOK_TILE,), in_specs=[_tok(D), _rep((1, D))], out_specs=_tok(D),
        out_shape=jax.ShapeDtypeStruct((t, D), BF16), compiler_params=_params(),
    )(x, g)


def _halo_prev(w, j=0, rows=8):
    r = TOK_TILE // rows
    return pl.BlockSpec((rows, w), lambda i: (jnp.maximum(i * r - 1, 0), j))


def _last8(halo_ref):
    return halo_ref[...].astype(F32)[halo_ref.shape[0] - 8:]


def _halo_next(w, nt, j=0):
    r = TOK_TILE // 8
    return pl.BlockSpec((8, w), lambda i: (jnp.minimum((i + 1) * r, nt * r - 1), j))


def _shift_down(x, halo, s):
    if s == 0:
        return x
    r = pltpu.roll(x, s, 0)
    hs = pltpu.roll(halo, s, 0)
    row = lax.broadcasted_iota(jnp.int32, hs.shape, 0)
    top = jnp.where(row < s, hs, r[0:8])
    return jnp.concatenate([top, r[8:]], axis=0)


def _shift_up(x, halo, s):
    if s == 0:
        return x
    n = x.shape[0]
    r = pltpu.roll(x, n - s, 0)
    hs = pltpu.roll(halo, 8 - s, 0)
    row = lax.broadcasted_iota(jnp.int32, hs.shape, 0)
    bot = jnp.where(row >= 8 - s, hs, r[n - 8:])
    return jnp.concatenate([r[:n - 8], bot], axis=0)


def _bf(x):
    return x.astype(BF16).astype(F32)


def _conv_taps(x, halo, w):
    x, halo, w = _bf(x), _bf(halo), _bf(w)
    acc = x * w[CONV - 1:CONV, :]
    for j in range(CONV - 1):
        acc = acc + _shift_down(x, halo, CONV - 1 - j) * w[j:j + 1, :]
    return acc


_Q_SCALE = DQK ** -0.5


def _qscale_row():
    lane = lax.broadcasted_iota(jnp.int32, (1, D), 1)
    return jnp.where(lane < MLH * DQK, _Q_SCALE, 1.0).astype(F32)


def _conv_silu_fwd(proj, conv_w):
    t = proj.shape[0]

    def body(x_ref, halo_ref, w_ref, o_ref):
        halo = jnp.where(pl.program_id(0) > 0, _last8(halo_ref), 0.0)
        c = _conv_taps(x_ref[...].astype(F32), halo, w_ref[...])
        o_ref[...] = (c * _sigmoid(c) * _qscale_row()).astype(BF16)

    return pl.pallas_call(
        body, name="conv_silu_fwd", grid=(t // TOK_TILE,),
        in_specs=[_tok(D, C_QK // D), _halo_prev(D, C_QK // D, 16), _rep((CONV, D))], out_specs=_tok(D),
        out_shape=jax.ShapeDtypeStruct((t, D), BF16), compiler_params=_params(),
    )(proj, proj, conv_w)


def _conv_silu_bwd_a(proj, conv_w, dqk):
    t = proj.shape[0]

    def body(x_ref, halo_ref, w_ref, d_ref, dc_ref, dw_ref):
        @pl.when(pl.program_id(0) == 0)
        def _():
            dw_ref[...] = jnp.zeros_like(dw_ref)

        halo = jnp.where(pl.program_id(0) > 0, _last8(halo_ref), 0.0)
        x = x_ref[...].astype(F32)
        c = _conv_taps(x, halo, w_ref[...])
        s = _sigmoid(c)
        dc = d_ref[...] * _qscale_row() * (s * (1.0 + c * (1.0 - s)))
        dc_ref[...] = dc
        dcb, xb, halo_b = _bf(dc), _bf(x), _bf(halo)
        for j in range(CONV):
            dw_ref[j:j + 1, :] += jnp.sum(dcb * _shift_down(xb, halo_b, CONV - 1 - j), axis=0, keepdims=True)

    return pl.pallas_call(
        body, name="conv_silu_bwd_a", grid=(t // TOK_TILE,),
        in_specs=[_tok(D, C_QK // D), _halo_prev(D, C_QK // D, 16), _rep((CONV, D)), _tok(D)],
        out_specs=[_tok(D), _rep((CONV, D))],
        out_shape=[jax.ShapeDtypeStruct((t, D), F32), jax.ShapeDtypeStruct((CONV, D), F32)],
        compiler_params=_params(),
    )(proj, proj, conv_w, dqk)


def _conv_silu_bwd_b(dc, conv_w, dproj):
    t = dc.shape[0]
    nt = t // TOK_TILE

    def body(dc_ref, halo_ref, w_ref, _, dx_ref):
        halo = _bf(jnp.where(pl.program_id(0) < nt - 1, halo_ref[...], 0.0))
        dcv = _bf(dc_ref[...])
        w = _bf(w_ref[...])
        acc = dcv * w[CONV - 1:CONV, :]
        for j in range(CONV - 1):
            acc = acc + _shift_up(dcv, halo, CONV - 1 - j) * w[j:j + 1, :]
        dx_ref[...] = acc.astype(BF16)

    return pl.pallas_call(
        body, name="conv_silu_bwd_b", grid=(nt,), in_specs=[_tok(D), _halo_next(D, nt), _rep((CONV, D)), _ANY],
        out_specs=_tok(D, C_QK // D), out_shape=jax.ShapeDtypeStruct((t, NP), BF16),
        input_output_aliases={3: 0}, compiler_params=_params(),
    )(dc, dc, conv_w, dproj)


def _gates_fwd(pre_rows, bias_col):
    t = pre_rows.shape[1]

    def body(p_ref, b_ref, g_ref, s_ref):
        z = p_ref[...] + b_ref[...]
        lf = jnp.minimum(z, 0.0) - jnp.log(1.0 + jnp.exp(-jnp.abs(z)))
        lane = lax.broadcasted_iota(jnp.int32, z.shape, 1) % CHUNK
        cum = lf
        s = 1
        while s < CHUNK:
            cum = cum + jnp.where(lane >= s, pltpu.roll(cum, s, 1), 0.0)
            s *= 2
        sub = lax.broadcasted_iota(jnp.int32, z.shape, 0)
        g_ref[...] = jnp.where(sub < MLH, z, cum)
        s_ref[...] = _sigmoid(-z)

    return pl.pallas_call(
        body, name="gates_fwd",
        out_shape=[jax.ShapeDtypeStruct((8, t), F32), jax.ShapeDtypeStruct((8, t), F32)],
        compiler_params=_params(),
    )(pre_rows, bias_col)


def _chunk_terms(grow, gcol, h, m0):
    i_row, b_row = grow[h:h + 1, :], grow[MLH + h:MLH + h + 1, :]
    i_col, b_col = gcol[:, h:h + 1], gcol[:, MLH + h:MLH + h + 1]
    b_last = b_row[:, CHUNK - 1:CHUNK]
    tt = lax.broadcasted_iota(jnp.int32, (CHUNK, CHUNK), 0)
    ss = lax.broadcasted_iota(jnp.int32, (CHUNK, CHUNK), 1)
    log_d = jnp.where(tt >= ss, b_col - b_row + i_row, -jnp.inf)
    m_t = jnp.maximum(b_col + m0, jnp.max(log_d, axis=1, keepdims=True))
    dm = jnp.exp(log_d - m_t)
    wi = jnp.exp(b_col + m0 - m_t)
    m1 = jnp.maximum(b_last + m0, jnp.max(b_last - b_row + i_row, axis=1, keepdims=True))
    ws = jnp.exp(b_last - b_col + i_col - m1)
    dec = jnp.exp(b_last + m0 - m1)
    return dm, wi, m_t, ws, dec, m1


def _mlstm_fwd(qk, proj, grow, gcol):
    t = qk.shape[0]
    nc = t // CHUNK

    def body(qk_ref, v_ref, grow_ref, gcol_ref, h_ref, cs_ref, st_ref, c_scr, st_scr):
        @pl.when(pl.program_id(0) == 0)
        def _():
            c_scr[...] = jnp.zeros_like(c_scr)
            st_scr[...] = jnp.zeros_like(st_scr)

        grow_v, gcol_v = grow_ref[...], gcol_ref[...]
        for h in range(MLH):
            q = qk_ref[:, h * DQK:(h + 1) * DQK]
            k = qk_ref[:, MLH * DQK + h * DQK:MLH * DQK + (h + 1) * DQK]
            v = v_ref[:, h * DV:(h + 1) * DV]
            c0 = c_scr[h]
            n0 = st_scr[h, 0:1, :]
            m0 = st_scr[h, 1:2, 0:1]
            cs_ref[0, h] = c0
            st_ref[0, h] = st_scr[h]
            dm, wi, m_t, ws, dec, m1 = _chunk_terms(grow_v, gcol_v, h, m0)
            s = _dot_nt(q, k) * dm
            num = wi * _dot_nt(q, c0.astype(BF16)) + _dot_nn(s.astype(BF16), v.astype(BF16))
            den = wi * jnp.sum(q.astype(F32) * n0, axis=1, keepdims=True) + jnp.sum(s, axis=1, keepdims=True)
            h_ref[:, h * DV:(h + 1) * DV] = num / jnp.maximum(jnp.abs(den), jnp.exp(-m_t))
            c_scr[h] = dec * c0 + _dot_tn((ws * v).astype(BF16), k)
            st_scr[h, 0:1, :] = dec * n0 + jnp.sum(ws * k.astype(F32), axis=0, keepdims=True)
            st_scr[h, 1:2, :] = jnp.broadcast_to(m1, (1, DQK))

    return pl.pallas_call(
        body, name="mlstm_fwd", grid=(nc,),
        in_specs=[pl.BlockSpec((CHUNK, D), lambda c: (c, 0)), pl.BlockSpec((CHUNK, D), lambda c: (c, C_V // D)),
                  pl.BlockSpec((8, CHUNK), lambda c: (0, c)), pl.BlockSpec((CHUNK, 8), lambda c: (c, 0))],
        out_specs=[pl.BlockSpec((CHUNK, D), lambda c: (c, 0)),
                   pl.BlockSpec((1, MLH, DV, DQK), lambda c: (c, 0, 0, 0)),
                   pl.BlockSpec((1, MLH, 8, DQK), lambda c: (c, 0, 0, 0))],
        out_shape=[jax.ShapeDtypeStruct((t, D), F32), jax.ShapeDtypeStruct((nc, MLH, DV, DQK), F32),
                   jax.ShapeDtypeStruct((nc, MLH, 8, DQK), F32)],
        scratch_shapes=[pltpu.VMEM((MLH, DV, DQK), F32), pltpu.VMEM((MLH, 8, DQK), F32)],
        compiler_params=_params(dimension_semantics=("arbitrary",)),
    )(qk, proj, grow, gcol)


def _mlstm_bwd(qk, proj, grow, gcol, sneg_col, cs, st, hraw, dh, dproj):
    t = qk.shape[0]
    nc = t // CHUNK

    def rev(c):
        return nc - 1 - c

    def nxt(c):
        return jnp.minimum(nc - c, nc - 1)

    def body(qk_ref, v_ref, grow_ref, gcol_ref, sneg_ref, cs_ref, st_ref, cs1_ref, st1_ref, h_ref, dh_ref, _,
             dqk_ref, dv_ref, dif_ref, dbif_ref, dc_scr, dn_scr):
        @pl.when(pl.program_id(0) == 0)
        def _():
            dc_scr[...] = jnp.zeros_like(dc_scr)
            dn_scr[...] = jnp.zeros_like(dn_scr)
            dbif_ref[...] = jnp.zeros_like(dbif_ref)

        grow_v, gcol_v, sneg = grow_ref[...], gcol_ref[...], sneg_ref[...]
        tt = lax.broadcasted_iota(jnp.int32, (CHUNK, CHUNK), 0)
        ss = lax.broadcasted_iota(jnp.int32, (CHUNK, CHUNK), 1)
        lane8 = lax.broadcasted_iota(jnp.int32, (CHUNK, 8), 1)
        dif = jnp.zeros((CHUNK, 8), F32)
        for h in range(MLH):
            q = qk_ref[:, h * DQK:(h + 1) * DQK]
            k = qk_ref[:, MLH * DQK + h * DQK:MLH * DQK + (h + 1) * DQK]
            qf, kf = q.astype(F32), k.astype(F32)
            v = v_ref[:, h * DV:(h + 1) * DV]
            vb = v.astype(BF16)
            c0 = cs_ref[0, h]
            n0 = st_ref[0, h, 0:1, :]
            m0 = st_ref[0, h, 1:2, 0:1]
            dc1 = dc_scr[h]
            dn1 = dn_scr[h, 0:1, :]
            dm, wi, m_t, ws, dec, _ = _chunk_terms(grow_v, gcol_v, h, m0)
            s = _dot_nt(q, k) * dm
            den = wi * jnp.sum(qf * n0, axis=1, keepdims=True) + jnp.sum(s, axis=1, keepdims=True)
            floor = jnp.exp(-m_t)
            g = jnp.maximum(jnp.abs(den), floor)
            dh_v = dh_ref[:, h * DV:(h + 1) * DV]
            dnum = dh_v / g
            dden = -jnp.sum(dh_v * h_ref[:, h * DV:(h + 1) * DV], axis=1, keepdims=True) / g
            dden = jnp.where(jnp.abs(den) > floor, dden * jnp.sign(den), 0.0)
            dnum_b = dnum.astype(BF16)
            da = ((_dot_nt(dnum_b, vb) + dden) * dm).astype(BF16)
            dc1_b = dc1.astype(BF16)
            dq = _dot_nn(da, k) + wi * (_dot_nn(dnum_b, c0.astype(BF16)) + dden * n0)
            dk = _dot_tn(da, q) + ws * (_dot_nn(vb, dc1_b) + dn1)
            dv = _dot_tn(s.astype(BF16), dnum_b) + ws * _dot_nt(k, dc1_b)
            dqk_ref[:, h * DQK:(h + 1) * DQK] = dq
            dqk_ref[:, MLH * DQK + h * DQK:MLH * DQK + (h + 1) * DQK] = dk
            dv_ref[:, h * DV:(h + 1) * DV] = dv.astype(BF16)
            rk = jnp.sum(kf * dk, axis=1, keepdims=True)
            df = jnp.sum(qf * dq, axis=1, keepdims=True) - rk
            df_row = jnp.sum(jnp.where(tt == ss, df, 0.0), axis=0, keepdims=True)
            suffix = jnp.sum(jnp.where(ss >= tt, df_row, 0.0), axis=1, keepdims=True)
            cross = (jnp.sum(jnp.sum(dc1 * cs1_ref[0, h], axis=1, keepdims=True), axis=0, keepdims=True)
                     + jnp.sum(dn1 * st1_ref[0, h, 0:1, :], axis=1, keepdims=True))
            dpf = (suffix + cross) * sneg[:, MLH + h:MLH + h + 1]
            dif = dif + jnp.where(lane8 == h, rk, 0.0) + jnp.where(lane8 == MLH + h, dpf, 0.0)
            dc_scr[h] = dec * dc1 + _dot_tn((wi * dnum).astype(BF16), q)
            dn_scr[h, 0:1, :] = dec * dn1 + jnp.sum(wi * dden * qf, axis=0, keepdims=True)
        dif_ref[...] = dif
        dbif_ref[...] += jnp.sum(dif, axis=0, keepdims=True)

    return pl.pallas_call(
        body, name="mlstm_bwd", grid=(nc,),
        in_specs=[pl.BlockSpec((CHUNK, D), lambda c: (rev(c), 0)),
                  pl.BlockSpec((CHUNK, D), lambda c: (rev(c), C_V // D)),
                  pl.BlockSpec((8, CHUNK), lambda c: (0, rev(c))),
                  pl.BlockSpec((CHUNK, 8), lambda c: (rev(c), 0)),
                  pl.BlockSpec((CHUNK, 8), lambda c: (rev(c), 0)),
                  pl.BlockSpec((1, MLH, DV, DQK), lambda c: (rev(c), 0, 0, 0)),
                  pl.BlockSpec((1, MLH, 8, DQK), lambda c: (rev(c), 0, 0, 0)),
                  pl.BlockSpec((1, MLH, DV, DQK), lambda c: (nxt(c), 0, 0, 0)),
                  pl.BlockSpec((1, MLH, 8, DQK), lambda c: (nxt(c), 0, 0, 0)),
                  pl.BlockSpec((CHUNK, D), lambda c: (rev(c), 0)),
                  pl.BlockSpec((CHUNK, D), lambda c: (rev(c), 0)), _ANY],
        out_specs=[pl.BlockSpec((CHUNK, D), lambda c: (rev(c), 0)),
                   pl.BlockSpec((CHUNK, D), lambda c: (rev(c), C_V // D)),
                   pl.BlockSpec((CHUNK, 8), lambda c: (rev(c), 0)),
                   pl.BlockSpec((1, 8), lambda c: (0, 0))],
        out_shape=[jax.ShapeDtypeStruct((t, D), F32), jax.ShapeDtypeStruct((t, NP), BF16),
                   jax.ShapeDtypeStruct((t, 8), F32), jax.ShapeDtypeStruct((1, 8), F32)],
        scratch_shapes=[pltpu.VMEM((MLH, DV, DQK), F32), pltpu.VMEM((MLH, 8, DQK), F32)],
        input_output_aliases={11: 1}, compiler_params=_params(dimension_semantics=("arbitrary",)),
    )(qk, proj, grow, gcol, sneg_col, cs, st, cs, st, hraw, dh, dproj)


def _ya_fwd(hraw, proj, g):
    t = hraw.shape[0]

    def body(h_ref, o_ref, g_ref, y_ref):
        so = _sigmoid(o_ref[...].astype(F32))
        for h in range(MLH):
            sl = slice(h * DV, (h + 1) * DV)
            xn, _ = _rms(h_ref[:, sl])
            y_ref[:, sl] = (so[:, sl] * xn * g_ref[:, sl]).astype(BF16)

    return pl.pallas_call(
        body, name="ya_fwd", grid=(t // TOK_TILE,), in_specs=[_tok(D), _tok(D, C_O // D), _rep((1, D))],
        out_specs=_tok(D), out_shape=jax.ShapeDtypeStruct((t, D), BF16), compiler_params=_params(),
    )(hraw, proj, g)


_ANY = pl.BlockSpec(memory_space=pl.ANY)


_SW_SCALE = HD ** -0.5
_KVB = C_KV // (2 * SWKV * HD)


def _swa_mask(n):
    ki = lax.broadcasted_iota(jnp.int32, (2 * WIN, SWG * WIN), 0)
    qi = lax.broadcasted_iota(jnp.int32, (2 * WIN, SWG * WIN), 1) % WIN
    return (ki > qi) & (ki <= qi + WIN) & ((n > 0) | (ki >= WIN))


def _group_rows(x_ref, hk):
    return jnp.concatenate([x_ref[:, (hk * SWG + g) * HD:(hk * SWG + g + 1) * HD] for g in range(SWG)], axis=0)


def _group_lanes(x_ref, hk):
    return jnp.concatenate([x_ref[hk * SWG + g:hk * SWG + g + 1, :] for g in range(SWG)], axis=1)


def _sink_lanes(sink_ref, hk):
    return jnp.concatenate([jnp.broadcast_to(sink_ref[:, hk * SWG + g:hk * SWG + g + 1], (1, WIN))
                            for g in range(SWG)], axis=1)


def _swa_fwd(proj, sinks):
    t = proj.shape[0]
    nb = t // WIN

    def body(q_ref, kvc_ref, kvp_ref, sink_ref, y_ref, lse_ref):
        valid = _swa_mask(pl.program_id(0))
        for hk in range(SWKV):
            ks = slice(hk * HD, (hk + 1) * HD)
            vs = slice(SWKV * HD + hk * HD, SWKV * HD + (hk + 1) * HD)
            kb = jnp.concatenate([kvp_ref[:, ks], kvc_ref[:, ks]], axis=0).astype(BF16)
            vb = jnp.concatenate([kvp_ref[:, vs], kvc_ref[:, vs]], axis=0).astype(BF16)
            q4 = _group_rows(q_ref, hk).astype(BF16)
            sink = _sink_lanes(sink_ref, hk)
            logits = jnp.where(valid, _dot_nt(kb, q4) * _SW_SCALE, -jnp.inf)
            m = jnp.maximum(jnp.max(logits, axis=0, keepdims=True), sink)
            p = jnp.exp(logits - m)
            denom = jnp.sum(p, axis=0, keepdims=True) + jnp.exp(sink - m)
            y4 = _dot_tn((p / denom).astype(BF16), vb).astype(BF16)
            lse4 = m + jnp.log(denom)
            for g in range(SWG):
                hq = hk * SWG + g
                y_ref[:, hq * HD:(hq + 1) * HD] = y4[g * WIN:(g + 1) * WIN]
                lse_ref[hq:hq + 1, :] = lse4[:, g * WIN:(g + 1) * WIN]

    return pl.pallas_call(
        body, name="swa_fwd", grid=(nb,),
        in_specs=[pl.BlockSpec((WIN, D), lambda n: (n, C_QSW // D)),
                  pl.BlockSpec((WIN, 512), lambda n: (n, _KVB)),
                  pl.BlockSpec((WIN, 512), lambda n: (jnp.maximum(n - 1, 0), _KVB)),
                  pl.BlockSpec((1, SWH), lambda n: (0, 0))],
        out_specs=[pl.BlockSpec((WIN, D), lambda n: (n, 0)), pl.BlockSpec((SWH, WIN), lambda n: (0, n))],
        out_shape=[jax.ShapeDtypeStruct((t, D), BF16), jax.ShapeDtypeStruct((SWH, t), F32)],
        compiler_params=_params(),
    )(proj, proj, proj, sinks)


def _swa_bwd(proj, sinks, lse, dyb, dproj):
    t = proj.shape[0]
    nb = t // WIN

    def body(q_ref, kvc_ref, kvp_ref, sink_ref, lse_ref, dy_ref, _, dq_ref, dself_ref, dprev_ref, ds_ref):
        @pl.when(pl.program_id(0) == 0)
        def _():
            ds_ref[...] = jnp.zeros_like(ds_ref)

        valid = _swa_mask(pl.program_id(0))
        for hk in range(SWKV):
            ks = slice(hk * HD, (hk + 1) * HD)
            vs = slice(SWKV * HD + hk * HD, SWKV * HD + (hk + 1) * HD)
            kb = jnp.concatenate([kvp_ref[:, ks], kvc_ref[:, ks]], axis=0).astype(BF16)
            vb = jnp.concatenate([kvp_ref[:, vs], kvc_ref[:, vs]], axis=0).astype(BF16)
            dy4 = _group_rows(dy_ref, hk)
            qb, dyb_ = _group_rows(q_ref, hk).astype(BF16), dy4.astype(BF16)
            lse4 = _group_lanes(lse_ref, hk)
            logits = jnp.where(valid, _dot_nt(kb, qb) * _SW_SCALE, -jnp.inf)
            p = jnp.exp(logits - lse4)
            dpt = _dot_nt(vb, dyb_)
            delta = jnp.sum(p * dpt, axis=0, keepdims=True)
            dsm = (p * (dpt - delta)).astype(BF16)
            dq4 = (_dot_tn(dsm, kb) * _SW_SCALE).astype(BF16)
            dkb = _dot_nn(dsm, qb) * _SW_SCALE
            dvb = _dot_nn(p.astype(BF16), dyb_)
            dsink4 = jnp.exp(_sink_lanes(sink_ref, hk) - lse4) * delta
            for g in range(SWG):
                hq = hk * SWG + g
                dq_ref[:, hq * HD:(hq + 1) * HD] = dq4[g * WIN:(g + 1) * WIN]
                ds_ref[:, hq:hq + 1] += -jnp.sum(dsink4[:, g * WIN:(g + 1) * WIN], axis=1, keepdims=True)
            dprev_ref[:, ks] = dkb[:WIN]
            dself_ref[:, ks] = dkb[WIN:]
            dprev_ref[:, vs] = dvb[:WIN]
            dself_ref[:, vs] = dvb[WIN:]

    return pl.pallas_call(
        body, name="swa_bwd", grid=(nb,),
        in_specs=[pl.BlockSpec((WIN, D), lambda n: (n, C_QSW // D)),
                  pl.BlockSpec((WIN, 512), lambda n: (n, _KVB)),
                  pl.BlockSpec((WIN, 512), lambda n: (jnp.maximum(n - 1, 0), _KVB)),
                  pl.BlockSpec((1, SWH), lambda n: (0, 0)),
                  pl.BlockSpec((SWH, WIN), lambda n: (0, n)),
                  pl.BlockSpec((WIN, D), lambda n: (n, 0)), _ANY],
        out_specs=[pl.BlockSpec((WIN, D), lambda n: (n, C_QSW // D)), pl.BlockSpec((WIN, 512), lambda n: (n, 0)),
                   pl.BlockSpec((WIN, 512), lambda n: (n, 0)), pl.BlockSpec((1, SWH), lambda n: (0, 0))],
        out_shape=[jax.ShapeDtypeStruct((t, NP), BF16), jax.ShapeDtypeStruct((t, 512), F32),
                   jax.ShapeDtypeStruct((t, 512), F32), jax.ShapeDtypeStruct((1, SWH), F32)],
        input_output_aliases={6: 0}, compiler_params=_params(),
    )(proj, proj, proj, sinks, lse, dyb, dproj)


def _kv_combine(dself, dprev, dif, dproj):
    t = dself.shape[0]
    nb = t // WIN

    def body(a_ref, b_ref, dif_ref, _, o_ref):
        nxt = jnp.where(pl.program_id(0) < nb - 1, b_ref[...], 0.0)
        o_ref[:, 0:512] = (a_ref[...] + nxt).astype(BF16)
        lane = lax.broadcasted_iota(jnp.int32, (WIN, 128), 1)
        dif_v = dif_ref[...]
        first = jnp.zeros((WIN, 128), F32)
        for col in range(8):
            first = first + jnp.where(lane == col, dif_v[:, col:col + 1], 0.0)
        o_ref[:, 512:640] = first.astype(BF16)
        o_ref[:, 640:512 + IFW] = jnp.zeros((WIN, IFW - 128), BF16)

    return pl.pallas_call(
        body, name="kv_combine", grid=(nb,),
        in_specs=[pl.BlockSpec((WIN, 512), lambda n: (n, 0)),
                  pl.BlockSpec((WIN, 512), lambda n: (jnp.minimum(n + 1, nb - 1), 0)),
                  pl.BlockSpec((WIN, 8), lambda n: (n, 0)), _ANY],
        out_specs=pl.BlockSpec((WIN, 512 + IFW), lambda n: (n, C_KV // (512 + IFW))),
        out_shape=jax.ShapeDtypeStruct((t, NP), BF16), input_output_aliases={3: 0}, compiler_params=_params(),
    )(dself, dprev, dif, dproj)


def _sds(t, n, dtype):
    return jax.ShapeDtypeStruct((t, n), dtype)


def _proj_in(h0, w_in):
    t = h0.shape[0]

    tn = 2 * IFW

    def epilogue(accs, ins, outs, i, j):
        outs[0][...] = accs[0].astype(BF16)

        @pl.when(j == C_IF // tn)
        def _():
            outs[1][...] = accs[0][:, C_IF % tn:]

    gate_cols = lambda tm, tn: pl.BlockSpec((tm, IFW), lambda i, j, kk: (i, 0))
    return _mm_ep([(h0, w_in)], "nn", "mm_in", epilogue, [],
                  [(_sds(t, NP, BF16), _tile()), (_sds(t, IFW, F32), gate_cols)], 1024, tn)


def _branch_merge(ya, yb, wa, wb, proj):
    t = ya.shape[0]

    def epilogue(accs, ins, outs, i, j):
        za, zb = accs
        merged = _sigmoid(ins[0][...].astype(F32)) * za + _sigmoid(ins[1][...].astype(F32)) * zb
        outs[0][...] = merged.astype(BF16)
        outs[1][...] = za.astype(BF16)
        outs[2][...] = zb.astype(BF16)

    return _mm_ep([(ya, wa), (yb, wb)], "nn", "mm_branch_merge", epilogue, [(proj, _tile(C_GA)), (proj, _tile(C_GB))],
                  [(_sds(t, D, BF16), _tile())] * 3, 1024, 512)


def _dmerged_bwd(dxb, w_out, proj, za, zb):
    t = dxb.shape[0]

    def epilogue(accs, ins, outs, i, j):
        dm = accs[0]
        sa, sb = _sigmoid(ins[0][...].astype(F32)), _sigmoid(ins[1][...].astype(F32))
        outs[0][...] = (dm * sa).astype(BF16)
        outs[1][...] = (dm * sb).astype(BF16)
        outs[2][:, 0:D] = (dm * ins[2][...].astype(F32) * sa * (1.0 - sa)).astype(BF16)
        outs[2][:, D:2 * D] = (dm * ins[3][...].astype(F32) * sb * (1.0 - sb)).astype(BF16)

    gate_cols = lambda tm, tn: pl.BlockSpec((tm, 2 * D), lambda i, j, kk: (i, C_GA // (2 * D)))
    return _mm_ep([(dxb, w_out)], "nt", "mm_dmerged_bwd", epilogue,
                  [(proj, _tile(C_GA)), (proj, _tile(C_GB)), (za, _tile()), (zb, _tile())],
                  [(_sds(t, D, BF16), _tile()), (_sds(t, D, BF16), _tile()), (_sds(t, NP, BF16), gate_cols)], 512, D)


def _dya_bwd(dza, wa, hraw, proj, g, dproj):
    t = dza.shape[0]

    def epilogue(accs, ins, outs, i, j):
        h_ref, o_ref, g_ref, _ = ins
        dh_ref, do_ref, dg_ref = outs

        @pl.when(i == 0)
        def _():
            dg_ref[...] = jnp.zeros_like(dg_ref)

        dy = accs[0]
        so = _sigmoid(o_ref[...].astype(F32))
        for h in range(MLH):
            sl = slice(h * DV, (h + 1) * DV)
            xn, rstd = _rms(h_ref[:, sl])
            gs = g_ref[:, sl]
            do_ref[:, sl] = (dy[:, sl] * xn * gs * so[:, sl] * (1.0 - so[:, sl])).astype(BF16)
            dhn = dy[:, sl] * so[:, sl]
            dg_ref[:, sl] += jnp.sum(dhn * xn, axis=0, keepdims=True)
            dh_ref[:, sl] = _rms_bwd(xn, rstd, dhn * gs)

    return _mm_ep([(dza, wa)], "nt", "mm_dya_bwd", epilogue,
                  [(hraw, _tile()), (proj, _tile(C_O)), (g, _row()), (dproj, lambda tm, tn: _ANY)],
                  [(_sds(t, D, F32), _tile()), (_sds(t, NP, BF16), _tile(C_O)), (_sds(1, D, F32), _row())],
                  512, D, aliases={3: 1})


def _up_act(hn, w_up):
    t = hn.shape[0]

    def epilogue(accs, ins, outs, i, j):
        r = jnp.maximum(accs[0], 0.0)
        outs[0][...] = (r * r).astype(BF16)
        outs[1][...] = accs[0].astype(BF16)

    return _mm_ep([(hn, w_up)], "nn", "mm_up_act", epilogue, [],
                  [(_sds(t, DFF, BF16), _tile()), (_sds(t, DFF, BF16), _tile())], 1024, 1024)


def _da_du(dxb, w_down, u):
    t = dxb.shape[0]

    def epilogue(accs, ins, outs, i, j):
        outs[0][...] = (accs[0] * 2.0 * jnp.maximum(ins[0][...].astype(F32), 0.0)).astype(BF16)

    return _mm_ep([(dxb, w_down)], "nt", "mm_da_du", epilogue, [(u, _tile())], [(_sds(t, DFF, BF16), _tile())],
                  1024, 1024)[0]


def _resid_norm_mm(a, w, x, g, name):
    t = x.shape[0]

    def epilogue(accs, ins, outs, i, j):
        x1 = ins[0][...] + accs[0]
        outs[0][...] = x1
        xn, _ = _rms(x1)
        outs[1][...] = (xn * ins[1][...]).astype(BF16)

    return _mm_ep([(a, w)], "nn", name, epilogue, [(x, _tile()), (g, _row())],
                  [(_sds(t, D, F32), _tile()), (_sds(t, D, BF16), _tile())], 512, D)


def _norm_bwd_mm(dy, w, x, g, dres, name):
    t = x.shape[0]

    def epilogue(accs, ins, outs, i, j):
        @pl.when(i == 0)
        def _():
            outs[2][...] = jnp.zeros_like(outs[2])

        dh = accs[0]
        xn, rstd = _rms(ins[0][...])
        outs[2][...] += jnp.sum(dh * xn, axis=0, keepdims=True)
        dx = ins[2][...] + _rms_bwd(xn, rstd, dh * ins[1][...])
        outs[0][...] = dx
        outs[1][...] = dx.astype(BF16)

    return _mm_ep([(dy, w)], "nt", name, epilogue, [(x, _tile()), (g, _row()), (dres, _tile())],
                  [(_sds(t, D, F32), _tile()), (_sds(t, D, BF16), _tile()), (_sds(1, D, F32), _row())], 512, D)


def _ple_final_mm(hn2, w_gate, x2, pp, target, gf):
    t = x2.shape[0]

    def epilogue(accs, ins, outs, i, j):
        loss_ref, dg_ref, dx_ref, dpp_ref, dgp_ref = outs

        @pl.when(i == 0)
        def _():
            loss_ref[...] = jnp.zeros_like(loss_ref)
            dg_ref[...] = jnp.zeros_like(dg_ref)

        gate = _sigmoid(accs[0])
        pp_v = ins[1][...]
        x3 = ins[0][...] + gate * pp_v
        xn, rstd = _rms(x3)
        gf_v = ins[3][...]
        err = xn * gf_v - ins[2][...]
        loss_ref[...] += (0.5 / D) * jnp.sum(jnp.sum(err * err, axis=1, keepdims=True), axis=0, keepdims=True)
        dy = err * (1.0 / D)
        dg_ref[...] += jnp.sum(dy * xn, axis=0, keepdims=True)
        dx3 = _rms_bwd(xn, rstd, dy * gf_v)
        dx_ref[...] = dx3
        dpp_ref[...] = (dx3 * gate).astype(BF16)
        dgp_ref[...] = (dx3 * pp_v * gate * (1.0 - gate)).astype(BF16)

    one = lambda tm, tn: pl.BlockSpec((1, 1), lambda i, j, kk: (0, 0))
    return _mm_ep([(hn2, w_gate)], "nn", "mm_ple_final", epilogue,
                  [(x2, _tile()), (pp, _tile()), (target, _tile()), (gf, _row())],
                  [(_sds(1, 1, F32), one), (_sds(1, D, F32), _row()), (_sds(t, D, F32), _tile()),
                   (_sds(t, D, BF16), _tile()), (_sds(t, D, BF16), _tile())], 512, D)


def _win_pad(w):
    zeros = jnp.zeros((w.shape[0], IFW - 8), w.dtype)
    return jnp.concatenate([w[:, 0:3072], w[:, 3080:4104], w[:, 4616:6664], w[:, 4104:4616], w[:, 3072:3080], zeros],
                           axis=1)


def _win_unpad(wp):
    return jnp.concatenate([wp[:, 0:3072], wp[:, C_IF:C_IF + 8], wp[:, C_QSW:C_QSW + 1024], wp[:, C_KV:C_KV + 512],
                            wp[:, C_GA:C_GA + 2048]], axis=1)


def _local_step(x, p, target, w, late_weights=None, early_grads=None):
    t = x.shape[0]
    pb = p.astype(BF16)
    w = dict(w)

    h0 = _norm_fwd(x, w["norm_mix_g"], "norm_mix")
    proj, gates = _proj_in(h0, w["w_in"])
    qk = _conv_silu_fwd(proj, w["conv_qk"])
    grow, sneg_row = _gates_fwd(gates[:, 0:8].T, w["b_if"].reshape(8, 1))
    gcol, sneg_col = grow.T, sneg_row.T
    hraw, cs, st = _mlstm_fwd(qk, proj, grow, gcol)
    ya = _ya_fwd(hraw, proj, w["mlstm_norm_g"])
    yb, lse = _swa_fwd(proj, w["sinks"])
    if late_weights is not None:
        w.update(late_weights(yb))
    merged, za, zb = _branch_merge(ya, yb, w["w_branch_a"], w["w_branch_b"], proj)
    x1, hn1 = _resid_norm_mm(merged, w["w_out"], x, w["norm_mlp_g"], "mm_out_norm")
    act, u = _up_act(hn1, w["w_up"])
    x2, hn2 = _resid_norm_mm(act, w["w_down"], x1, w["norm_ple_g"], "mm_down_norm")
    pp = _mm(pb, w["w_ple_proj"], "nn", F32, "mm_ple_proj")
    loss, d_final_g, dx3, dpp, dgpre = _ple_final_mm(hn2, w["w_ple_gate"], x2, pp, target, w["final_norm_g"])

    g = {"final_norm_g": d_final_g}
    g["w_ple_proj"] = _mm(pb, dpp, "tn", F32, "mm_d_ple_proj", out_chunks=4)
    g["w_ple_gate"] = _mm(hn2, dgpre, "tn", F32, "mm_d_ple_gate")
    dx2, dx2b, g["norm_ple_g"] = _norm_bwd_mm(dgpre, w["w_ple_gate"], x2, w["norm_ple_g"], dx3, "mm_dhn2_norm")
    g["w_down"] = _mm(act, dx2b, "tn", F32, "mm_d_down")
    du = _da_du(dx2b, w["w_down"], u)
    g["w_up"] = _mm(hn1, du, "tn", F32, "mm_d_up", out_chunks=4)
    dx1, dx1b, g["norm_mlp_g"] = _norm_bwd_mm(du, w["w_up"], x1, w["norm_mlp_g"], dx2, "mm_dhn1_norm")
    g["w_out"] = _mm(merged, dx1b, "tn", F32, "mm_d_out")
    dza, dzb, dproj = _dmerged_bwd(dx1b, w["w_out"], proj, za, zb)
    g["w_branch_a"] = _mm(ya, dza, "tn", F32, "mm_d_branch_a")
    g["w_branch_b"] = _mm(yb, dzb, "tn", F32, "mm_d_branch_b")
    gain = w["mlstm_norm_g"] if early_grads is None else w["mlstm_norm_g"] + early_grads(g)
    dyb = _mm(dzb, w["w_branch_b"], "nt", F32, "mm_dyb")
    dhraw, dproj, g["mlstm_norm_g"] = _dya_bwd(dza, w["w_branch_a"], hraw, proj, gain, dproj)
    dqk, dproj, dif, g["b_if"] = _mlstm_bwd(qk, proj, grow, gcol, sneg_col, cs, st, hraw, dhraw, dproj)
    dc, g["conv_qk"] = _conv_silu_bwd_a(proj, w["conv_qk"], dqk)
    dproj = _conv_silu_bwd_b(dc, w["conv_qk"], dproj)
    dproj, dkv_self, dkv_prev, g["sinks"] = _swa_bwd(proj, w["sinks"], lse, dyb, dproj)
    dproj = _kv_combine(dkv_self, dkv_prev, dif, dproj)
    g["w_in"] = _mm(h0, dproj, "tn", F32, "mm_d_in")
    grad_x, _, g["norm_mix_g"] = _norm_bwd_mm(dproj, w["w_in"], x, w["norm_mix_g"], dx1, "mm_dh0_norm")
    return loss, grad_x, g


_W4 = ("w_branch_a", "w_branch_b", "w_out", "w_ple_gate")
_SHARDED_NAMES = ("w_in", "w_up", "w_down", "w_ple_proj", "conv_qk") + _W4
_SMALL_ROWS = 16
_CONV_ROW = 8


def _group(s):
    return [s["w_in"], jnp.concatenate([s[n] for n in _W4], axis=0), s["w_up"], s["w_down"], s["w_ple_proj"]]


def _ungroup(arrs):
    out = {"w_in": arrs[0], "w_up": arrs[2], "w_down": arrs[3], "w_ple_proj": arrs[4]}
    rows = arrs[1].shape[0] // len(_W4)
    for i, n in enumerate(_W4):
        out[n] = arrs[1][i * rows:(i + 1) * rows]
    return out


def _rows_tile(rows):
    return 256 if rows % 256 == 0 else rows


_SMALL = ("norm_mix_g", "mlstm_norm_g", "norm_mlp_g", "norm_ple_g", "final_norm_g")


def _pack_small(vals, extra=None, conv=None):
    rows = [vals[n].reshape(1, D) for n in _SMALL]
    tail = [vals["b_if"].reshape(1, 8), vals["sinks"].reshape(1, SWH)]
    used = 8 + SWH
    if extra is not None:
        tail.append(extra.reshape(1, 1))
        used += 1
    tail.append(jnp.zeros((1, D - used), F32))
    rows.append(jnp.concatenate(tail, axis=1))
    rows.append(jnp.zeros((_CONV_ROW - len(rows), D), F32))
    rows.append(jnp.zeros((CONV, D), F32) if conv is None else conv)
    rows.append(jnp.zeros((_SMALL_ROWS - _CONV_ROW - CONV, D), F32))
    return jnp.concatenate(rows, axis=0)


def _unpack_small(slab, shapes):
    out = {n: slab[i].reshape(shapes[n]) for i, n in enumerate(_SMALL)}
    out["b_if"] = slab[5, 0:8].reshape(shapes["b_if"])
    out["sinks"] = slab[5, 8:8 + SWH].reshape(shapes["sinks"])
    return out


_MESH = pl.DeviceIdType.MESH
_HBM = pl.BlockSpec(memory_space=pltpu.HBM)
_VMEM = pl.BlockSpec(memory_space=pltpu.VMEM)


def _place():
    x, y, c = lax.axis_index("x"), lax.axis_index("y"), lax.axis_index("c")
    return x, y, c, 2 * x + y


def _chip_peer(x, y, r):
    return (x ^ (r >> 1), y ^ (r & 1))


def _half(ref, which):
    h = ref.shape[-2] // 2
    return pl.ds(which * h, h)


def _allgather_weights(shards, conv):
    n = len(shards)

    def body(*refs):
        ins, conv_ref = refs[:n], refs[n]
        outs, conv_out = refs[n + 1:2 * n + 1], refs[2 * n + 1]
        send_a, recv_a, send_b, recv_b, send_c, recv_c, local_sems = refs[2 * n + 2:]
        x, y, c, j = _place()
        sibling = (x, y, 1 - c)
        local = [pltpu.make_async_copy(ins[k], outs[k].at[j], local_sems.at[k]) for k in range(n)]
        local.append(pltpu.make_async_copy(conv_ref, conv_out.at[j], local_sems.at[n]))
        for cp in local:
            cp.start()

        def copy_a(k, r, chip):
            rows = _half(ins[k], c)
            return pltpu.make_async_remote_copy(
                src_ref=ins[k].at[rows], dst_ref=outs[k].at[chip, rows], send_sem=send_a.at[3 * k + r - 1],
                recv_sem=recv_a.at[3 * k + r - 1], device_id=(*_chip_peer(x, y, r), c), device_id_type=_MESH)

        def copy_b(k, r, chip, which):
            rows = _half(ins[k], which)
            return pltpu.make_async_remote_copy(
                src_ref=outs[k].at[chip, rows], dst_ref=outs[k].at[chip, rows], send_sem=send_b.at[3 * k + r - 1],
                recv_sem=recv_b.at[3 * k + r - 1], device_id=sibling, device_id_type=_MESH)

        def copy_c(r, chip):
            return pltpu.make_async_remote_copy(
                src_ref=conv_ref, dst_ref=conv_out.at[chip], send_sem=send_c.at[r - 1],
                recv_sem=recv_c.at[r - 1], device_id=(*_chip_peer(x, y, r), c), device_id_type=_MESH)

        for k in range(n):
            for r in (1, 2, 3):
                copy_a(k, r, j).start()
        for r in (1, 2, 3):
            copy_c(r, j).start()
        for k in range(n):
            for r in (1, 2, 3):
                copy_a(k, r, j ^ r).wait_recv()
                copy_b(k, r, j ^ r, c).start()
        for k in range(n):
            for r in (1, 2, 3):
                copy_b(k, r, j ^ r, 1 - c).wait_recv()
        for r in (1, 2, 3):
            copy_c(r, j ^ r).wait_recv()
        for k in range(n):
            for r in (1, 2, 3):
                copy_a(k, r, j).wait_send()
                copy_b(k, r, j ^ r, c).wait_send()
        for r in (1, 2, 3):
            copy_c(r, j).wait_send()
        for cp in local:
            cp.wait()

    return pl.pallas_call(
        body, name="allgather_weights",
        out_shape=[jax.ShapeDtypeStruct((4,) + s.shape, s.dtype) for s in shards]
        + [jax.ShapeDtypeStruct((4,) + conv.shape, F32)],
        in_specs=[_HBM] * (n + 1), out_specs=[_HBM] * (n + 1),
        scratch_shapes=[pltpu.SemaphoreType.DMA((3 * n,))] * 4 + [pltpu.SemaphoreType.DMA((3,))] * 2
        + [pltpu.SemaphoreType.DMA((n + 1,))],
    )(*shards, conv)


_SEM = pl.BlockSpec(memory_space=pltpu.SEMAPHORE)
_DATAFLOW = pltpu.SideEffectType.DATAFLOW_SIDE_EFFECTING


def _late_peer_copy(src_ref, land_ref, send_sems, recv_sems, x, y, c, j, r, chip):
    return pltpu.make_async_remote_copy(
        src_ref=src_ref, dst_ref=land_ref.at[chip], send_sem=send_sems.at[r - 1], recv_sem=recv_sems.at[r - 1],
        device_id=(*_chip_peer(x, y, r), c), device_id_type=_MESH)


def _late_gather_start(rest):
    def body(rest_ref, land_ref, send_sems, recv_sems, rest_thru, land_thru, token):
        x, y, c, j = _place()
        for r in (1, 2, 3):
            _late_peer_copy(rest_ref, land_ref, send_sems, recv_sems, x, y, c, j, r, j).start()
        token[...] = jnp.zeros_like(token)

    j = 2 * lax.axis_index("x") + lax.axis_index("y")
    land = lax.dynamic_update_slice(lax.empty((4,) + rest.shape, rest.dtype), rest[None], (j, 0, 0))
    return pl.pallas_call(
        body, name="late_gather_start",
        out_shape=(pltpu.SemaphoreType.DMA((3,)), pltpu.SemaphoreType.DMA((3,)), pltpu.HBM(rest.shape, rest.dtype),
                   pltpu.HBM(land.shape, land.dtype), jax.ShapeDtypeStruct((8, 128), F32)),
        in_specs=(_HBM, _HBM), out_specs=(_SEM, _SEM, _HBM, _HBM, _VMEM), input_output_aliases={0: 2, 1: 3},
        compiler_params=pltpu.CompilerParams(has_side_effects=_DATAFLOW),
    )(pltpu.with_memory_space_constraint(rest, pltpu.HBM), pltpu.with_memory_space_constraint(land, pltpu.HBM))


def _late_gather_wait(send_sems, recv_sems, rest_thru, land_thru, after):
    def body(rest_ref, land_ref, send_sems, recv_sems, after_ref, rest_dead, got_ref):
        x, y, c, j = _place()
        for r in (1, 2, 3):
            cp = _late_peer_copy(rest_ref, land_ref, send_sems, recv_sems, x, y, c, j, r, j ^ r)
            cp.wait_send()
            cp.wait_recv()

    return pl.pallas_call(
        body, name="late_gather_wait",
        out_shape=(pltpu.HBM(rest_thru.shape, rest_thru.dtype), pltpu.HBM(land_thru.shape, land_thru.dtype)),
        in_specs=(_HBM, _HBM, _SEM, _SEM, _ANY), out_specs=(_HBM, _HBM), input_output_aliases={0: 0, 1: 1},
        compiler_params=pltpu.CompilerParams(has_side_effects=_DATAFLOW),
    )(rest_thru, land_thru, send_sems, recv_sems, after)[1]


def _pair_exchange(gs, name):
    n = len(gs)

    def body(*refs):
        ins, outs, send_sems, recv_sems = refs[:n], refs[n:2 * n], refs[2 * n], refs[2 * n + 1]
        x, y, c, _ = _place()
        cps = [pltpu.make_async_remote_copy(
            src_ref=ins[k].at[:, _half(ins[k], 1 - c)], dst_ref=outs[k], send_sem=send_sems.at[k],
            recv_sem=recv_sems.at[k], device_id=(x, y, 1 - c), device_id_type=_MESH) for k in range(n)]
        for cp in cps:
            cp.start()
        for cp in cps:
            cp.wait()

    return pl.pallas_call(
        body, name=name,
        out_shape=[jax.ShapeDtypeStruct((4, g.shape[1] // 2, g.shape[2]), F32) for g in gs],
        in_specs=[_HBM] * n, out_specs=[_HBM] * n, scratch_shapes=[pltpu.SemaphoreType.DMA((n,))] * 2,
    )(*gs)


def _pair_sum(g, theirs, c, name):
    _, h, cols = theirs.shape
    tr = _rows_tile(h)
    nb = h // tr

    def body(c_ref, a_ref, b_ref, o_ref, ob_ref):
        s = a_ref[...] + b_ref[...]
        o_ref[...] = s
        ob_ref[...] = s.astype(BF16)

    blk = pl.BlockSpec((1, tr, cols), lambda k, i, c_ref: (k, i, 0))
    return pl.pallas_call(
        body, name=name,
        grid_spec=pltpu.PrefetchScalarGridSpec(
            num_scalar_prefetch=1, grid=(4, nb),
            in_specs=[pl.BlockSpec((1, tr, cols), lambda k, i, c_ref: (k, c_ref[0] * nb + i, 0)), blk],
            out_specs=[blk, blk]),
        out_shape=[jax.ShapeDtypeStruct(theirs.shape, F32), jax.ShapeDtypeStruct(theirs.shape, BF16)],
        compiler_params=_params(),
    )(c.reshape(1).astype(jnp.int32), g, theirs)


def _chip_exchange(ss):
    n = len(ss)

    def body(*refs):
        ins, outs, send_sems, recv_sems = refs[:n], refs[n:2 * n], refs[2 * n], refs[2 * n + 1]
        x, y, c, j = _place()
        cps = [pltpu.make_async_remote_copy(
            src_ref=ins[k].at[j ^ r], dst_ref=outs[k].at[r - 1], send_sem=send_sems.at[3 * k + r - 1],
            recv_sem=recv_sems.at[3 * k + r - 1], device_id=(*_chip_peer(x, y, r), c), device_id_type=_MESH)
            for k in range(n) for r in (1, 2, 3)]
        for cp in cps:
            cp.start()
        for cp in cps:
            cp.wait()

    return pl.pallas_call(
        body, name="chip_exchange", out_shape=[jax.ShapeDtypeStruct((3,) + s.shape[1:], s.dtype) for s in ss],
        in_specs=[_HBM] * n, out_specs=[_HBM] * n, scratch_shapes=[pltpu.SemaphoreType.DMA((3 * n,))] * 2,
    )(*ss)


def _chip_copies(srcs, lands, send_sems, recv_sems):
    x, y, c, j = _place()
    return [pltpu.make_async_remote_copy(
        src_ref=srcs[k].at[j ^ r], dst_ref=lands[k].at[r - 1], send_sem=send_sems.at[3 * k + r - 1],
        recv_sem=recv_sems.at[3 * k + r - 1], device_id=(*_chip_peer(x, y, r), c), device_id_type=_MESH)
        for k in range(len(srcs)) for r in (1, 2, 3)]


def _chip_exchange_start(ss):
    n = len(ss)

    def body(*refs):
        srcs, lands, send_sems, recv_sems, token = refs[:n], refs[n:2 * n], refs[2 * n], refs[2 * n + 1], refs[-1]
        for cp in _chip_copies(srcs, lands, send_sems, recv_sems):
            cp.start()
        token[...] = jnp.zeros_like(token)

    lands = [lax.empty((3,) + s.shape[1:], s.dtype) for s in ss]
    hbm = [pltpu.HBM(a.shape, a.dtype) for a in list(ss) + lands]
    out = pl.pallas_call(
        body, name="chip_exchange_start",
        out_shape=(pltpu.SemaphoreType.DMA((3 * n,)), pltpu.SemaphoreType.DMA((3 * n,)), *hbm,
                   jax.ShapeDtypeStruct((8, 128), F32)),
        in_specs=[_HBM] * (2 * n), out_specs=(_SEM, _SEM, *([_HBM] * (2 * n)), _VMEM),
        input_output_aliases={k: 2 + k for k in range(2 * n)},
        compiler_params=pltpu.CompilerParams(has_side_effects=_DATAFLOW),
    )(*[pltpu.with_memory_space_constraint(a, pltpu.HBM) for a in list(ss) + lands])
    return out[0], out[1], list(out[2:2 + n]), list(out[2 + n:2 + 2 * n]), out[-1]


def _chip_exchange_wait(send_sems, recv_sems, ss_thru, lands_thru, after):
    n = len(ss_thru)

    def body(*refs):
        srcs, lands, send_sems, recv_sems = refs[:n], refs[n:2 * n], refs[2 * n], refs[2 * n + 1]
        for cp in _chip_copies(srcs, lands, send_sems, recv_sems):
            cp.wait_send()
            cp.wait_recv()

    hbm = [pltpu.HBM(a.shape, a.dtype) for a in list(ss_thru) + list(lands_thru)]
    out = pl.pallas_call(
        body, name="chip_exchange_wait", out_shape=tuple(hbm),
        in_specs=[_HBM] * (2 * n) + [_SEM, _SEM, _ANY], out_specs=tuple([_HBM] * (2 * n)),
        input_output_aliases={k: k for k in range(2 * n)},
        compiler_params=pltpu.CompilerParams(has_side_effects=_DATAFLOW),
    )(*ss_thru, *lands_thru, send_sems, recv_sems, after)
    return list(out[n:])


def _reduce4(own, others, j, c, name):
    _, h, cols = own.shape
    tr = _rows_tile(h)
    nb = h // tr

    def body(idx_ref, s_ref, a0, a1, a2, o_ref):
        o_ref[...] = ((s_ref[0] + a0[0].astype(F32)) + a1[0].astype(F32)) + a2[0].astype(F32)

    def other(r):
        return pl.BlockSpec((1, tr, cols), lambda i, idx_ref: (r, i, 0))

    return pl.pallas_call(
        body, name=name,
        grid_spec=pltpu.PrefetchScalarGridSpec(
            num_scalar_prefetch=1, grid=(nb,),
            in_specs=[pl.BlockSpec((1, tr, cols), lambda i, idx_ref: (idx_ref[0], i, 0)), other(0), other(1), other(2)],
            out_specs=pl.BlockSpec((tr, cols), lambda i, idx_ref: (idx_ref[1] * nb + i, 0))),
        out_shape=jax.ShapeDtypeStruct((2 * h, cols), F32), compiler_params=_params(),
    )(jnp.stack([j, c]).astype(jnp.int32), own, others, others, others)


def _sibling_share(fulls):
    n = len(fulls)

    def body(*refs):
        outs, send_sems, recv_sems = refs[n:2 * n], refs[2 * n], refs[2 * n + 1]
        x, y, c, _ = _place()
        cps = [pltpu.make_async_remote_copy(
            src_ref=outs[k].at[_half(outs[k], c)], dst_ref=outs[k].at[_half(outs[k], c)], send_sem=send_sems.at[k],
            recv_sem=recv_sems.at[k], device_id=(x, y, 1 - c), device_id_type=_MESH) for k in range(n)]
        for cp in cps:
            cp.start()
        for cp in cps:
            cp.wait()

    return pl.pallas_call(
        body, name="sibling_share", out_shape=[jax.ShapeDtypeStruct(f.shape, F32) for f in fulls],
        in_specs=[_HBM] * n, out_specs=[_HBM] * n, input_output_aliases={k: k for k in range(n)},
        scratch_shapes=[pltpu.SemaphoreType.DMA((n,))] * 2,
    )(*fulls)


def _adamw(w, g, m, v):
    m1 = ADAM_B1 * m + (1.0 - ADAM_B1) * g
    v1 = ADAM_B2 * v + (1.0 - ADAM_B2) * (g * g)
    m_hat = m1 / (1.0 - ADAM_B1 ** ADAM_STEP)
    v_hat = v1 / (1.0 - ADAM_B2 ** ADAM_STEP)
    delta = -ADAM_LR * (m_hat / (jnp.sqrt(v_hat) + ADAM_EPS) + ADAM_WD * w)
    return delta, m1, v1


def _adamw_call(w, g, m, v, name):
    rows, cols = w.shape
    tr = _rows_tile(rows)

    def body(w_ref, g_ref, m_ref, v_ref, d_out, m_out, v_out):
        delta, m1, v1 = _adamw(w_ref[...], g_ref[...], m_ref[...], v_ref[...])
        d_out[...] = delta
        m_out[...] = m1
        v_out[...] = v1

    blk = pl.BlockSpec((tr, cols), lambda i: (i, 0))
    return pl.pallas_call(
        body, name=name, grid=(rows // tr,), in_specs=[blk] * 4, out_specs=[blk] * 3,
        out_shape=[jax.ShapeDtypeStruct((rows, cols), F32)] * 3, compiler_params=_params(),
    )(w, g, m, v)


def _small_allreduce(vals):
    def body(v_ref, out_ref, buf, send_sems, recv_sems):
        x, y, c, j = _place()
        me = 2 * j + c
        buf[0] = v_ref[...]

        def copy(r):
            return pltpu.make_async_remote_copy(
                src_ref=v_ref, dst_ref=buf.at[r], send_sem=send_sems.at[r - 1], recv_sem=recv_sems.at[r - 1],
                device_id=(x ^ (r >> 2), y ^ ((r >> 1) & 1), c ^ (r & 1)), device_id_type=_MESH)

        for r in range(1, 8):
            copy(r).start()
        for r in range(1, 8):
            copy(r).wait()
        acc = buf[me ^ 0]
        for d in range(1, 8):
            acc = acc + buf[me ^ d]
        out_ref[...] = acc

    return pl.pallas_call(
        body, name="small_allreduce", out_shape=jax.ShapeDtypeStruct((_SMALL_ROWS, D), F32),
        in_specs=[_VMEM], out_specs=_VMEM,
        scratch_shapes=[pltpu.VMEM((8, _SMALL_ROWS, D), F32), pltpu.SemaphoreType.DMA((7,)),
                        pltpu.SemaphoreType.DMA((7,))],
    )(vals)


_NAMES = ("norm_mix_g", "w_in", "conv_qk", "b_if", "mlstm_norm_g", "sinks", "w_branch_a", "w_branch_b", "w_out",
          "norm_mlp_g", "w_up", "w_down", "norm_ple_g", "w_ple_gate", "w_ple_proj", "final_norm_g")
_GROUP_NAMES = ("w_in", "w4", "w_up", "w_down", "w_ple_proj")


def _step(x, p, target, w, m, v):
    c = lax.axis_index("c")
    j = 2 * lax.axis_index("x") + lax.axis_index("y")

    def shards(d):
        return {n: d[n][0] for n in _SHARDED_NAMES}

    ws = shards(w)
    w_in_all, conv_all = _allgather_weights([ws["w_in"].astype(BF16)], ws["conv_qk"])
    rows_pp = PLE * (D // 4) // D
    rest = jnp.concatenate([ws[n] for n in _W4] + [ws["w_up"], ws["w_down"], ws["w_ple_proj"].reshape(rows_pp, D)],
                           axis=0)
    rest = (rest + 0.0 * conv_all[0, 0, 0]).astype(BF16)
    send_sems, recv_sems, rest_thru, land_thru, token = _late_gather_start(rest)
    full = {n: w[n] for n in ("mlstm_norm_g", "norm_mlp_g", "norm_ple_g", "b_if", "sinks")}
    full["norm_mix_g"] = w["norm_mix_g"] + token[0, 0]
    full["final_norm_g"] = w["final_norm_g"].reshape(1, D)
    full["w_in"] = _win_pad(jnp.swapaxes(w_in_all, 0, 1).reshape(D, N_IN))
    full["conv_qk"] = jnp.swapaxes(conv_all, 0, 1).reshape(CONV, D)

    def late_weights(after):
        land = _late_gather_wait(send_sems, recv_sems, rest_thru, land_thru, after)
        out = {n: land[:, i * (D // 4):(i + 1) * (D // 4)].reshape(D, D) for i, n in enumerate(_W4)}
        out["w_up"] = land[:, D:2 * D]
        out["w_down"] = land[:, 2 * D:3 * D].reshape(DFF, D)
        out["w_ple_proj"] = land[:, 3 * D:3 * D + rows_pp].reshape(4, PLE, D // 4)
        return out

    def pair_sums(by_dest, names, tag):
        theirs = _pair_exchange(by_dest, "pair_exchange_" + tag)
        return [_pair_sum(a, b, c, "pair_sum_" + n) for a, b, n in zip(by_dest, theirs, names)]

    early = {}

    def early_grads(g):
        by_dest = [jnp.stack([g[n].reshape(4, D // 4, D) for n in _W4], axis=1).reshape(4, D, D),
                   g["w_up"], g["w_down"].reshape(4, DFF // 4, D), g["w_ple_proj"]]
        early["sums"] = pair_sums(by_dest, _GROUP_NAMES[1:], "early")
        *early["flight"], token = _chip_exchange_start([s[1] for s in early["sums"]])
        return token[0, 0]

    loss, grad_x, g = _local_step(x[0], p[0, 0], target[0], full, late_weights, early_grads)

    w_in_g = _win_unpad(g["w_in"])
    sums = pair_sums([jnp.swapaxes(w_in_g.reshape(D, 4, N_IN // 4), 0, 1)], _GROUP_NAMES[:1], "w_in")
    others = list(_chip_exchange([s[1] for s in sums]))
    others += _chip_exchange_wait(*early["flight"], others[0])
    sums += early["sums"]
    halves = [_reduce4(s[0], b, j, c, "reduce4_" + n) for s, b, n in zip(sums, others, _GROUP_NAMES)]
    grads = _sibling_share(halves)

    small_g = _small_allreduce(_pack_small(g, extra=loss, conv=g["conv_qk"]))
    conv_g = lax.dynamic_slice(small_g[_CONV_ROW:_CONV_ROW + CONV], (0, j * (D // 4)), (CONV, D // 4))

    ms, vs = shards(m), shards(v)
    upd = [_adamw_call(wa, ga, ma, va, "adamw_" + n)
           for wa, ga, ma, va, n in zip(_group(ws), grads, _group(ms), _group(vs), _GROUP_NAMES)]
    conv_upd = _adamw_call(ws["conv_qk"], conv_g, ms["conv_qk"], vs["conv_qk"], "adamw_conv")
    small_upd = _adamw_call(_pack_small(w), small_g, _pack_small(m), _pack_small(v), "adamw_small")

    shapes = {n: w[n].shape for n in _NAMES}
    res = []
    for k in range(4):
        big = _ungroup(list(grads) if k == 0 else [u[k - 1] for u in upd])
        big["conv_qk"] = conv_g if k == 0 else conv_upd[k - 1]
        leaves = _unpack_small(small_g if k == 0 else small_upd[k - 1], shapes)
        leaves.update({n: a.reshape(shapes[n]) for n, a in big.items()})
        res.append(leaves)

    out = [small_g[5, 8 + SWH], grad_x[None]]
    for k in range(4):
        out += [res[k][n] for n in _NAMES]
    return tuple(out)


def kernel(x, p, norm_mix_g, w_in, conv_qk, b_if, mlstm_norm_g, sinks, w_branch_a, w_branch_b, w_out, norm_mlp_g, w_up, w_down, norm_ple_g, w_ple_gate, w_ple_proj, final_norm_g, loss_target, m_norm_mix_g, m_w_in, m_conv_qk, m_b_if, m_mlstm_norm_g, m_sinks, m_w_branch_a, m_w_branch_b, m_w_out, m_norm_mlp_g, m_w_up, m_w_down, m_norm_ple_g, m_w_ple_gate, m_w_ple_proj, m_final_norm_g, v_norm_mix_g, v_w_in, v_conv_qk, v_b_if, v_mlstm_norm_g, v_sinks, v_w_branch_a, v_w_branch_b, v_w_out, v_norm_mlp_g, v_w_up, v_w_down, v_norm_ple_g, v_w_ple_gate, v_w_ple_proj, v_final_norm_g):
    w = dict(zip(_NAMES, (norm_mix_g, w_in, conv_qk, b_if, mlstm_norm_g, sinks, w_branch_a, w_branch_b, w_out,
                          norm_mlp_g, w_up, w_down, norm_ple_g, w_ple_gate, w_ple_proj, final_norm_g)))
    m = dict(zip(_NAMES, (m_norm_mix_g, m_w_in, m_conv_qk, m_b_if, m_mlstm_norm_g, m_sinks, m_w_branch_a,
                          m_w_branch_b, m_w_out, m_norm_mlp_g, m_w_up, m_w_down, m_norm_ple_g, m_w_ple_gate,
                          m_w_ple_proj, m_final_norm_g)))
    v = dict(zip(_NAMES, (v_norm_mix_g, v_w_in, v_conv_qk, v_b_if, v_mlstm_norm_g, v_sinks, v_w_branch_a,
                          v_w_branch_b, v_w_out, v_norm_mlp_g, v_w_up, v_w_down, v_norm_ple_g, v_w_ple_gate,
                          v_w_ple_proj, v_final_norm_g)))
    return _step(x, p, loss_target, w, m, v)
```

```python
import jax
import jax.numpy as jnp
from jax import lax
from jax.experimental import pallas as pl
from jax.experimental.pallas import tpu as pltpu

F32 = jnp.float32
BF16 = jnp.bfloat16

D = 1024
PLE = 256
MLH = 4
DQK = 128
DV = 256
CONV = 4
CHUNK = 128
SWH = 16
SWKV = 4
SWG = SWH // SWKV
HD = 64
WIN = 128
DFF = 4096
EPS = 1e-6
N_IN = 6664
NP = 7168
C_QK, C_V, C_O, C_QSW, C_GA, C_GB, C_KV, C_IF = 0, 1024, 2048, 3072, 4096, 5120, 6144, 6656
IFW = NP - C_IF

ADAM_LR = 0.001
ADAM_B1 = 0.9
ADAM_B2 = 0.999
ADAM_EPS = 1e-08
ADAM_WD = 0.01
ADAM_STEP = 10

TOK_TILE = 256
VMEM_LIMIT = 48 * 1024 * 1024


def _params(**kw):
    return pltpu.CompilerParams(vmem_limit_bytes=VMEM_LIMIT, **kw)


def _pick(n, cap):
    if n <= cap:
        return n
    t = cap - cap % 128
    while t > 128 and n % t:
        t -= 128
    assert n % t == 0, (n, cap)
    return t


def _dot(a, b, dims):
    return lax.dot_general(a, b, (dims, ((), ())), preferred_element_type=F32)


def _dot_nn(a, b):
    return _dot(a, b, ((1,), (0,)))


def _dot_nt(a, b):
    return _dot(a, b, ((1,), (1,)))


def _dot_tn(a, b):
    return _dot(a, b, ((0,), (0,)))


def _sigmoid(x):
    return 1.0 / (1.0 + jnp.exp(-x))


def _mm(a, b, mode, out_dtype, name, out_chunks=1):
    bch = b.shape[0] if b.ndim == 3 else 1
    brows, bcols = b.shape[-2], b.shape[-1] * bch
    if mode == "nn":
        (m, k), (k2, n) = a.shape, (brows, bcols)
    elif mode == "nt":
        (m, k), (n, k2) = a.shape, (brows, bcols)
    else:
        (k, m), (k2, n) = a.shape, (brows, bcols)
    assert k == k2, (a.shape, b.shape, mode)
    n_cap = n // max(out_chunks, 1 if mode == "nt" else bch)
    k_cap = k // bch if mode == "nt" else k
    tm, tn, tk = _pick(m, 1024), _pick(n_cap, 1024), _pick(k_cap, 2048)
    nk = k // tk
    if mode == "nn":
        a_spec = pl.BlockSpec((tm, tk), lambda i, j, kk: (i, kk))
        if bch > 1:
            bpc = (n // bch) // tn
            b_spec = pl.BlockSpec((None, tk, tn), lambda i, j, kk: (j // bpc, kk, j % bpc))
        else:
            b_spec = pl.BlockSpec((tk, tn), lambda i, j, kk: (kk, j))
        dot = _dot_nn
    elif mode == "nt":
        a_spec = pl.BlockSpec((tm, tk), lambda i, j, kk: (i, kk))
        if bch > 1:
            bpc = (k // bch) // tk
            b_spec = pl.BlockSpec((None, tn, tk), lambda i, j, kk: (kk // bpc, j, kk % bpc))
        else:
            b_spec = pl.BlockSpec((tn, tk), lambda i, j, kk: (j, kk))
        dot = _dot_nt
    else:
        assert bch == 1
        a_spec = pl.BlockSpec((tk, tm), lambda i, j, kk: (kk, i))
        b_spec = pl.BlockSpec((tk, tn), lambda i, j, kk: (kk, j))
        dot = _dot_tn
    if out_chunks > 1:
        npc = (n // out_chunks) // tn
        out_spec = pl.BlockSpec((None, tm, tn), lambda i, j, kk: (j // npc, i, j % npc))
        out_shape = jax.ShapeDtypeStruct((out_chunks, m, n // out_chunks), out_dtype)
    else:
        out_spec = pl.BlockSpec((tm, tn), lambda i, j, kk: (i, j))
        out_shape = jax.ShapeDtypeStruct((m, n), out_dtype)

    def body(a_ref, b_ref, o_ref, acc_ref):
        kk = pl.program_id(2)

        @pl.when(kk == 0)
        def _():
            acc_ref[...] = jnp.zeros_like(acc_ref)

        acc_ref[...] += dot(a_ref[...], b_ref[...])

        @pl.when(kk == nk - 1)
        def _():
            o_ref[...] = acc_ref[...].astype(out_dtype)

    return pl.pallas_call(
        body, name=name, grid=(m // tm, n // tn, nk),
        in_specs=[a_spec, b_spec], out_specs=out_spec, out_shape=out_shape,
        scratch_shapes=[pltpu.VMEM((tm, tn), F32)],
        compiler_params=_params(dimension_semantics=("parallel", "parallel", "arbitrary")),
    )(a, b)


def _tile(col0=0):
    return lambda tm, tn: pl.BlockSpec((tm, tn), lambda i, j, kk: (i, col0 // tn + j))


def _row():
    return lambda tm, tn: pl.BlockSpec((1, tn), lambda i, j, kk: (0, j))


def _mm_ep(pairs, mode, name, epilogue, ins, outs, tm, tn, aliases=None):
    a0, b0 = pairs[0]
    bch = b0.shape[0] if b0.ndim == 3 else 1
    m, k = a0.shape
    tm = _pick(m, tm)
    n = b0.shape[-1] * bch if mode == "nn" else b0.shape[-2]
    tk = _pick(k // bch if mode == "nt" else k, 2048)
    nk = k // tk
    a_spec = pl.BlockSpec((tm, tk), lambda i, j, kk: (i, kk))
    if mode == "nn":
        dot = _dot_nn
        if bch > 1:
            bpc = (n // bch) // tn
            b_spec = pl.BlockSpec((None, tk, tn), lambda i, j, kk: (j // bpc, kk, j % bpc))
        else:
            b_spec = pl.BlockSpec((tk, tn), lambda i, j, kk: (kk, j))
    else:
        dot = _dot_nt
        if bch > 1:
            bpc = (k // bch) // tk
            b_spec = pl.BlockSpec((None, tn, tk), lambda i, j, kk: (kk // bpc, j, kk % bpc))
        else:
            b_spec = pl.BlockSpec((tn, tk), lambda i, j, kk: (j, kk))
    npair, nin, nout = len(pairs), len(ins), len(outs)

    def body(*refs):
        ab = refs[:2 * npair]
        in_refs = refs[2 * npair:2 * npair + nin]
        out_refs = refs[2 * npair + nin:2 * npair + nin + nout]
        accs = refs[2 * npair + nin + nout:]
        i, j, kk = pl.program_id(0), pl.program_id(1), pl.program_id(2)
        for p in range(npair):
            prod = dot(ab[2 * p][...], ab[2 * p + 1][...])

            @pl.when(kk == 0)
            def _():
                accs[p][...] = prod

            @pl.when(kk > 0)
            def _():
                accs[p][...] += prod

        @pl.when(kk == nk - 1)
        def _():
            epilogue([acc[...] for acc in accs], in_refs, out_refs, i, j)

    operands = [x for pair in pairs for x in pair] + [a for a, _ in ins]
    io_alias = {2 * npair + i: o for i, o in (aliases or {}).items()}
    return pl.pallas_call(
        body, name=name, grid=(m // tm, n // tn, nk),
        in_specs=[a_spec, b_spec] * npair + [mk(tm, tn) for _, mk in ins],
        out_specs=[mk(tm, tn) for _, mk in outs], out_shape=[s for s, _ in outs],
        scratch_shapes=[pltpu.VMEM((tm, tn), F32)] * npair, input_output_aliases=io_alias,
        compiler_params=_params(dimension_semantics=("arbitrary", "arbitrary", "arbitrary")),
    )(*operands)


def _tok(w, j=0):
    return pl.BlockSpec((TOK_TILE, w), lambda i: (i, j))


def _rep(shape):
    return pl.BlockSpec(shape, lambda i: (0,) * len(shape))


def _rms(x):
    rstd = lax.rsqrt(jnp.mean(x * x, axis=-1, keepdims=True) + EPS)
    return x * rstd, rstd


def _rms_bwd(xn, rstd, dxn):
    return rstd * (dxn - xn * jnp.mean(dxn * xn, axis=-1, keepdims=True))


def _norm_fwd(x, g, name):
    t = x.shape[0]

    def body(x_ref, g_ref, h_ref):
        xn, _ = _rms(x_ref[...])
        h_ref[...] = (xn * g_ref[...]).astype(BF16)

    return pl.pallas_call(
        body, name=name, grid=(t // TOK_TILE,), in_specs=[_tok(D), _rep((1, D))], out_specs=_tok(D),
        out_shape=jax.ShapeDtypeStruct((t, D), BF16), compiler_params=_params(),
    )(x, g)


def _halo_prev(w, j=0, rows=8):
    r = TOK_TILE // rows
    return pl.BlockSpec((rows, w), lambda i: (jnp.maximum(i * r - 1, 0), j))


def _last8(halo_ref):
    return halo_ref[...].astype(F32)[halo_ref.shape[0] - 8:]


def _halo_next(w, nt, j=0):
    r = TOK_TILE // 8
    return pl.BlockSpec((8, w), lambda i: (jnp.minimum((i + 1) * r, nt * r - 1), j))


def _shift_down(x, halo, s):
    if s == 0:
        return x
    r = pltpu.roll(x, s, 0)
    hs = pltpu.roll(halo, s, 0)
    row = lax.broadcasted_iota(jnp.int32, hs.shape, 0)
    top = jnp.where(row < s, hs, r[0:8])
    return jnp.concatenate([top, r[8:]], axis=0)


def _shift_up(x, halo, s):
    if s == 0:
        return x
    n = x.shape[0]
    r = pltpu.roll(x, n - s, 0)
    hs = pltpu.roll(halo, 8 - s, 0)
    row = lax.broadcasted_iota(jnp.int32, hs.shape, 0)
    bot = jnp.where(row >= 8 - s, hs, r[n - 8:])
    return jnp.concatenate([r[:n - 8], bot], axis=0)


def _bf(x):
    return x.astype(BF16).astype(F32)


def _conv_taps(x, halo, w):
    x, halo, w = _bf(x), _bf(halo), _bf(w)
    acc = x * w[CONV - 1:CONV, :]
    for j in range(CONV - 1):
        acc = acc + _shift_down(x, halo, CONV - 1 - j) * w[j:j + 1, :]
    return acc


_Q_SCALE = DQK ** -0.5


def _qscale_row():
    lane = lax.broadcasted_iota(jnp.int32, (1, D), 1)
    return jnp.where(lane < MLH * DQK, _Q_SCALE, 1.0).astype(F32)


def _conv_silu_fwd(proj, conv_w):
    t = proj.shape[0]

    def body(x_ref, halo_ref, w_ref, o_ref):
        halo = jnp.where(pl.program_id(0) > 0, _last8(halo_ref), 0.0)
        c = _conv_taps(x_ref[...].astype(F32), halo, w_ref[...])
        o_ref[...] = (c * _sigmoid(c) * _qscale_row()).astype(BF16)

    return pl.pallas_call(
        body, name="conv_silu_fwd", grid=(t // TOK_TILE,),
        in_specs=[_tok(D, C_QK // D), _halo_prev(D, C_QK // D, 16), _rep((CONV, D))], out_specs=_tok(D),
        out_shape=jax.ShapeDtypeStruct((t, D), BF16), compiler_params=_params(),
    )(proj, proj, conv_w)


def _conv_silu_bwd_a(proj, conv_w, dqk):
    t = proj.shape[0]

    def body(x_ref, halo_ref, w_ref, d_ref, dc_ref, dw_ref):
        @pl.when(pl.program_id(0) == 0)
        def _():
            dw_ref[...] = jnp.zeros_like(dw_ref)

        halo = jnp.where(pl.program_id(0) > 0, _last8(halo_ref), 0.0)
        x = x_ref[...].astype(F32)
        c = _conv_taps(x, halo, w_ref[...])
        s = _sigmoid(c)
        dc = d_ref[...] * _qscale_row() * (s * (1.0 + c * (1.0 - s)))
        dc_ref[...] = dc
        dcb, xb, halo_b = _bf(dc), _bf(x), _bf(halo)
        for j in range(CONV):
            dw_ref[j:j + 1, :] += jnp.sum(dcb * _shift_down(xb, halo_b, CONV - 1 - j), axis=0, keepdims=True)

    return pl.pallas_call(
        body, name="conv_silu_bwd_a", grid=(t // TOK_TILE,),
        in_specs=[_tok(D, C_QK // D), _halo_prev(D, C_QK // D, 16), _rep((CONV, D)), _tok(D)],
        out_specs=[_tok(D), _rep((CONV, D))],
        out_shape=[jax.ShapeDtypeStruct((t, D), F32), jax.ShapeDtypeStruct((CONV, D), F32)],
        compiler_params=_params(),
    )(proj, proj, conv_w, dqk)


def _conv_silu_bwd_b(dc, conv_w, dproj):
    t = dc.shape[0]
    nt = t // TOK_TILE

    def body(dc_ref, halo_ref, w_ref, _, dx_ref):
        halo = _bf(jnp.where(pl.program_id(0) < nt - 1, halo_ref[...], 0.0))
        dcv = _bf(dc_ref[...])
        w = _bf(w_ref[...])
        acc = dcv * w[CONV - 1:CONV, :]
        for j in range(CONV - 1):
            acc = acc + _shift_up(dcv, halo, CONV - 1 - j) * w[j:j + 1, :]
        dx_ref[...] = acc.astype(BF16)

    return pl.pallas_call(
        body, name="conv_silu_bwd_b", grid=(nt,), in_specs=[_tok(D), _halo_next(D, nt), _rep((CONV, D)), _ANY],
        out_specs=_tok(D, C_QK // D), out_shape=jax.ShapeDtypeStruct((t, NP), BF16),
        input_output_aliases={3: 0}, compiler_params=_params(),
    )(dc, dc, conv_w, dproj)


def _gates_fwd(pre_rows, bias_col):
    t = pre_rows.shape[1]

    def body(p_ref, b_ref, g_ref, s_ref):
        z = p_ref[...] + b_ref[...]
        lf = jnp.minimum(z, 0.0) - jnp.log(1.0 + jnp.exp(-jnp.abs(z)))
        lane = lax.broadcasted_iota(jnp.int32, z.shape, 1) % CHUNK
        cum = lf
        s = 1
        while s < CHUNK:
            cum = cum + jnp.where(lane >= s, pltpu.roll(cum, s, 1), 0.0)
            s *= 2
        sub = lax.broadcasted_iota(jnp.int32, z.shape, 0)
        g_ref[...] = jnp.where(sub < MLH, z, cum)
        s_ref[...] = _sigmoid(-z)

    return pl.pallas_call(
        body, name="gates_fwd",
        out_shape=[jax.ShapeDtypeStruct((8, t), F32), jax.ShapeDtypeStruct((8, t), F32)],
        compiler_params=_params(),
    )(pre_rows, bias_col)


def _chunk_terms(grow, gcol, h, m0):
    i_row, b_row = grow[h:h + 1, :], grow[MLH + h:MLH + h + 1, :]
    i_col, b_col = gcol[:, h:h + 1], gcol[:, MLH + h:MLH + h + 1]
    b_last = b_row[:, CHUNK - 1:CHUNK]
    tt = lax.broadcasted_iota(jnp.int32, (CHUNK, CHUNK), 0)
    ss = lax.broadcasted_iota(jnp.int32, (CHUNK, CHUNK), 1)
    log_d = jnp.where(tt >= ss, b_col - b_row + i_row, -jnp.inf)
    m_t = jnp.maximum(b_col + m0, jnp.max(log_d, axis=1, keepdims=True))
    dm = jnp.exp(log_d - m_t)
    wi = jnp.exp(b_col + m0 - m_t)
    m1 = jnp.maximum(b_last + m0, jnp.max(b_last - b_row + i_row, axis=1, keepdims=True))
    ws = jnp.exp(b_last - b_col + i_col - m1)
    dec = jnp.exp(b_last + m0 - m1)
    return dm, wi, m_t, ws, dec, m1


def _mlstm_fwd(qk, proj, grow, gcol):
    t = qk.shape[0]
    nc = t // CHUNK

    def body(qk_ref, v_ref, grow_ref, gcol_ref, h_ref, cs_ref, st_ref, c_scr, st_scr):
        @pl.when(pl.program_id(0) == 0)
        def _():
            c_scr[...] = jnp.zeros_like(c_scr)
            st_scr[...] = jnp.zeros_like(st_scr)

        grow_v, gcol_v = grow_ref[...], gcol_ref[...]
        for h in range(MLH):
            q = qk_ref[:, h * DQK:(h + 1) * DQK]
            k = qk_ref[:, MLH * DQK + h * DQK:MLH * DQK + (h + 1) * DQK]
            v = v_ref[:, h * DV:(h + 1) * DV]
            c0 = c_scr[h]
            n0 = st_scr[h, 0:1, :]
            m0 = st_scr[h, 1:2, 0:1]
            cs_ref[0, h] = c0
            st_ref[0, h] = st_scr[h]
            dm, wi, m_t, ws, dec, m1 = _chunk_terms(grow_v, gcol_v, h, m0)
            s = _dot_nt(q, k) * dm
            num = wi * _dot_nt(q, c0.astype(BF16)) + _dot_nn(s.astype(BF16), v.astype(BF16))
            den = wi * jnp.sum(q.astype(F32) * n0, axis=1, keepdims=True) + jnp.sum(s, axis=1, keepdims=True)
            h_ref[:, h * DV:(h + 1) * DV] = num / jnp.maximum(jnp.abs(den), jnp.exp(-m_t))
            c_scr[h] = dec * c0 + _dot_tn((ws * v).astype(BF16), k)
            st_scr[h, 0:1, :] = dec * n0 + jnp.sum(ws * k.astype(F32), axis=0, keepdims=True)
            st_scr[h, 1:2, :] = jnp.broadcast_to(m1, (1, DQK))

    return pl.pallas_call(
        body, name="mlstm_fwd", grid=(nc,),
        in_specs=[pl.BlockSpec((CHUNK, D), lambda c: (c, 0)), pl.BlockSpec((CHUNK, D), lambda c: (c, C_V // D)),
                  pl.BlockSpec((8, CHUNK), lambda c: (0, c)), pl.BlockSpec((CHUNK, 8), lambda c: (c, 0))],
        out_specs=[pl.BlockSpec((CHUNK, D), lambda c: (c, 0)),
                   pl.BlockSpec((1, MLH, DV, DQK), lambda c: (c, 0, 0, 0)),
                   pl.BlockSpec((1, MLH, 8, DQK), lambda c: (c, 0, 0, 0))],
        out_shape=[jax.ShapeDtypeStruct((t, D), F32), jax.ShapeDtypeStruct((nc, MLH, DV, DQK), F32),
                   jax.ShapeDtypeStruct((nc, MLH, 8, DQK), F32)],
        scratch_shapes=[pltpu.VMEM((MLH, DV, DQK), F32), pltpu.VMEM((MLH, 8, DQK), F32)],
        compiler_params=_params(dimension_semantics=("arbitrary",)),
    )(qk, proj, grow, gcol)


def _mlstm_bwd(qk, proj, grow, gcol, sneg_col, cs, st, hraw, dh, dproj):
    t = qk.shape[0]
    nc = t // CHUNK

    def rev(c):
        return nc - 1 - c

    def nxt(c):
        return jnp.minimum(nc - c, nc - 1)

    def body(qk_ref, v_ref, grow_ref, gcol_ref, sneg_ref, cs_ref, st_ref, cs1_ref, st1_ref, h_ref, dh_ref, _,
             dqk_ref, dv_ref, dif_ref, dbif_ref, dc_scr, dn_scr):
        @pl.when(pl.program_id(0) == 0)
        def _():
            dc_scr[...] = jnp.zeros_like(dc_scr)
            dn_scr[...] = jnp.zeros_like(dn_scr)
            dbif_ref[...] = jnp.zeros_like(dbif_ref)

        grow_v, gcol_v, sneg = grow_ref[...], gcol_ref[...], sneg_ref[...]
        tt = lax.broadcasted_iota(jnp.int32, (CHUNK, CHUNK), 0)
        ss = lax.broadcasted_iota(jnp.int32, (CHUNK, CHUNK), 1)
        lane8 = lax.broadcasted_iota(jnp.int32, (CHUNK, 8), 1)
        dif = jnp.zeros((CHUNK, 8), F32)
        for h in range(MLH):
            q = qk_ref[:, h * DQK:(h + 1) * DQK]
            k = qk_ref[:, MLH * DQK + h * DQK:MLH * DQK + (h + 1) * DQK]
            qf, kf = q.astype(F32), k.astype(F32)
            v = v_ref[:, h * DV:(h + 1) * DV]
            vb = v.astype(BF16)
            c0 = cs_ref[0, h]
            n0 = st_ref[0, h, 0:1, :]
            m0 = st_ref[0, h, 1:2, 0:1]
            dc1 = dc_scr[h]
            dn1 = dn_scr[h, 0:1, :]
            dm, wi, m_t, ws, dec, _ = _chunk_terms(grow_v, gcol_v, h, m0)
            s = _dot_nt(q, k) * dm
            den = wi * jnp.sum(qf * n0, axis=1, keepdims=True) + jnp.sum(s, axis=1, keepdims=True)
            floor = jnp.exp(-m_t)
            g = jnp.maximum(jnp.abs(den), floor)
            dh_v = dh_ref[:, h * DV:(h + 1) * DV]
            dnum = dh_v / g
            dden = -jnp.sum(dh_v * h_ref[:, h * DV:(h + 1) * DV], axis=1, keepdims=True) / g
            dden = jnp.where(jnp.abs(den) > floor, dden * jnp.sign(den), 0.0)
            dnum_b = dnum.astype(BF16)
            da = ((_dot_nt(dnum_b, vb) + dden) * dm).astype(BF16)
            dc1_b = dc1.astype(BF16)
            dq = _dot_nn(da, k) + wi * (_dot_nn(dnum_b, c0.astype(BF16)) + dden * n0)
            dk = _dot_tn(da, q) + ws * (_dot_nn(vb, dc1_b) + dn1)
            dv = _dot_tn(s.astype(BF16), dnum_b) + ws * _dot_nt(k, dc1_b)
            dqk_ref[:, h * DQK:(h + 1) * DQK] = dq
            dqk_ref[:, MLH * DQK + h * DQK:MLH * DQK + (h + 1) * DQK] = dk
            dv_ref[:, h * DV:(h + 1) * DV] = dv.astype(BF16)
            rk = jnp.sum(kf * dk, axis=1, keepdims=True)
            df = jnp.sum(qf * dq, axis=1, keepdims=True) - rk
            df_row = jnp.sum(jnp.where(tt == ss, df, 0.0), axis=0, keepdims=True)
            suffix = jnp.sum(jnp.where(ss >= tt, df_row, 0.0), axis=1, keepdims=True)
            cross = (jnp.sum(jnp.sum(dc1 * cs1_ref[0, h], axis=1, keepdims=True), axis=0, keepdims=True)
                     + jnp.sum(dn1 * st1_ref[0, h, 0:1, :], axis=1, keepdims=True))
            dpf = (suffix + cross) * sneg[:, MLH + h:MLH + h + 1]
            dif = dif + jnp.where(lane8 == h, rk, 0.0) + jnp.where(lane8 == MLH + h, dpf, 0.0)
            dc_scr[h] = dec * dc1 + _dot_tn((wi * dnum).astype(BF16), q)
            dn_scr[h, 0:1, :] = dec * dn1 + jnp.sum(wi * dden * qf, axis=0, keepdims=True)
        dif_ref[...] = dif
        dbif_ref[...] += jnp.sum(dif, axis=0, keepdims=True)

    return pl.pallas_call(
        body, name="mlstm_bwd", grid=(nc,),
        in_specs=[pl.BlockSpec((CHUNK, D), lambda c: (rev(c), 0)),
                  pl.BlockSpec((CHUNK, D), lambda c: (rev(c), C_V // D)),
                  pl.BlockSpec((8, CHUNK), lambda c: (0, rev(c))),
                  pl.BlockSpec((CHUNK, 8), lambda c: (rev(c), 0)),
                  pl.BlockSpec((CHUNK, 8), lambda c: (rev(c), 0)),
                  pl.BlockSpec((1, MLH, DV, DQK), lambda c: (rev(c), 0, 0, 0)),
                  pl.BlockSpec((1, MLH, 8, DQK), lambda c: (rev(c), 0, 0, 0)),
                  pl.BlockSpec((1, MLH, DV, DQK), lambda c: (nxt(c), 0, 0, 0)),
                  pl.BlockSpec((1, MLH, 8, DQK), lambda c: (nxt(c), 0, 0, 0)),
                  pl.BlockSpec((CHUNK, D), lambda c: (rev(c), 0)),
                  pl.BlockSpec((CHUNK, D), lambda c: (rev(c), 0)), _ANY],
        out_specs=[pl.BlockSpec((CHUNK, D), lambda c: (rev(c), 0)),
                   pl.BlockSpec((CHUNK, D), lambda c: (rev(c), C_V // D)),
                   pl.BlockSpec((CHUNK, 8), lambda c: (rev(c), 0)),
                   pl.BlockSpec((1, 8), lambda c: (0, 0))],
        out_shape=[jax.ShapeDtypeStruct((t, D), F32), jax.ShapeDtypeStruct((t, NP), BF16),
                   jax.ShapeDtypeStruct((t, 8), F32), jax.ShapeDtypeStruct((1, 8), F32)],
        scratch_shapes=[pltpu.VMEM((MLH, DV, DQK), F32), pltpu.VMEM((MLH, 8, DQK), F32)],
        input_output_aliases={11: 1}, compiler_params=_params(dimension_semantics=("arbitrary",)),
    )(qk, proj, grow, gcol, sneg_col, cs, st, cs, st, hraw, dh, dproj)


def _ya_fwd(hraw, proj, g):
    t = hraw.shape[0]

    def body(h_ref, o_ref, g_ref, y_ref):
        so = _sigmoid(o_ref[...].astype(F32))
        for h in range(MLH):
            sl = slice(h * DV, (h + 1) * DV)
            xn, _ = _rms(h_ref[:, sl])
            y_ref[:, sl] = (so[:, sl] * xn * g_ref[:, sl]).astype(BF16)

    return pl.pallas_call(
        body, name="ya_fwd", grid=(t // TOK_TILE,), in_specs=[_tok(D), _tok(D, C_O // D), _rep((1, D))],
        out_specs=_tok(D), out_shape=jax.ShapeDtypeStruct((t, D), BF16), compiler_params=_params(),
    )(hraw, proj, g)


_ANY = pl.BlockSpec(memory_space=pl.ANY)


_SW_SCALE = HD ** -0.5
_KVB = C_KV // (2 * SWKV * HD)


def _swa_mask(n):
    ki = lax.broadcasted_iota(jnp.int32, (2 * WIN, SWG * WIN), 0)
    qi = lax.broadcasted_iota(jnp.int32, (2 * WIN, SWG * WIN), 1) % WIN
    return (ki > qi) & (ki <= qi + WIN) & ((n > 0) | (ki >= WIN))


def _group_rows(x_ref, hk):
    return jnp.concatenate([x_ref[:, (hk * SWG + g) * HD:(hk * SWG + g + 1) * HD] for g in range(SWG)], axis=0)


def _group_lanes(x_ref, hk):
    return jnp.concatenate([x_ref[hk * SWG + g:hk * SWG + g + 1, :] for g in range(SWG)], axis=1)


def _sink_lanes(sink_ref, hk):
    return jnp.concatenate([jnp.broadcast_to(sink_ref[:, hk * SWG + g:hk * SWG + g + 1], (1, WIN))
                            for g in range(SWG)], axis=1)


def _swa_fwd(proj, sinks):
    t = proj.shape[0]
    nb = t // WIN

    def body(q_ref, kvc_ref, kvp_ref, sink_ref, y_ref, lse_ref):
        valid = _swa_mask(pl.program_id(0))
        for hk in range(SWKV):
            ks = slice(hk * HD, (hk + 1) * HD)
            vs = slice(SWKV * HD + hk * HD, SWKV * HD + (hk + 1) * HD)
            kb = jnp.concatenate([kvp_ref[:, ks], kvc_ref[:, ks]], axis=0).astype(BF16)
            vb = jnp.concatenate([kvp_ref[:, vs], kvc_ref[:, vs]], axis=0).astype(BF16)
            q4 = _group_rows(q_ref, hk).astype(BF16)
            sink = _sink_lanes(sink_ref, hk)
            logits = jnp.where(valid, _dot_nt(kb, q4) * _SW_SCALE, -jnp.inf)
            m = jnp.maximum(jnp.max(logits, axis=0, keepdims=True), sink)
            p = jnp.exp(logits - m)
            denom = jnp.sum(p, axis=0, keepdims=True) + jnp.exp(sink - m)
            y4 = _dot_tn((p / denom).astype(BF16), vb).astype(BF16)
            lse4 = m + jnp.log(denom)
            for g in range(SWG):
                hq = hk * SWG + g
                y_ref[:, hq * HD:(hq + 1) * HD] = y4[g * WIN:(g + 1) * WIN]
                lse_ref[hq:hq + 1, :] = lse4[:, g * WIN:(g + 1) * WIN]

    return pl.pallas_call(
        body, name="swa_fwd", grid=(nb,),
        in_specs=[pl.BlockSpec((WIN, D), lambda n: (n, C_QSW // D)),
                  pl.BlockSpec((WIN, 512), lambda n: (n, _KVB)),
                  pl.BlockSpec((WIN, 512), lambda n: (jnp.maximum(n - 1, 0), _KVB)),
                  pl.BlockSpec((1, SWH), lambda n: (0, 0))],
        out_specs=[pl.BlockSpec((WIN, D), lambda n: (n, 0)), pl.BlockSpec((SWH, WIN), lambda n: (0, n))],
        out_shape=[jax.ShapeDtypeStruct((t, D), BF16), jax.ShapeDtypeStruct((SWH, t), F32)],
        compiler_params=_params(),
    )(proj, proj, proj, sinks)


def _swa_bwd(proj, sinks, lse, dyb, dproj):
    t = proj.shape[0]
    nb = t // WIN

    def body(q_ref, kvc_ref, kvp_ref, sink_ref, lse_ref, dy_ref, _, dq_ref, dself_ref, dprev_ref, ds_ref):
        @pl.when(pl.program_id(0) == 0)
        def _():
            ds_ref[...] = jnp.zeros_like(ds_ref)

        valid = _swa_mask(pl.program_id(0))
        for hk in range(SWKV):
            ks = slice(hk * HD, (hk + 1) * HD)
            vs = slice(SWKV * HD + hk * HD, SWKV * HD + (hk + 1) * HD)
            kb = jnp.concatenate([kvp_ref[:, ks], kvc_ref[:, ks]], axis=0).astype(BF16)
            vb = jnp.concatenate([kvp_ref[:, vs], kvc_ref[:, vs]], axis=0).astype(BF16)
            dy4 = _group_rows(dy_ref, hk)
            qb, dyb_ = _group_rows(q_ref, hk).astype(BF16), dy4.astype(BF16)
            lse4 = _group_lanes(lse_ref, hk)
            logits = jnp.where(valid, _dot_nt(kb, qb) * _SW_SCALE, -jnp.inf)
            p = jnp.exp(logits - lse4)
            dpt = _dot_nt(vb, dyb_)
            delta = jnp.sum(p * dpt, axis=0, keepdims=True)
            dsm = (p * (dpt - delta)).astype(BF16)
            dq4 = (_dot_tn(dsm, kb) * _SW_SCALE).astype(BF16)
            dkb = _dot_nn(dsm, qb) * _SW_SCALE
            dvb = _dot_nn(p.astype(BF16), dyb_)
            dsink4 = jnp.exp(_sink_lanes(sink_ref, hk) - lse4) * delta
            for g in range(SWG):
                hq = hk * SWG + g
                dq_ref[:, hq * HD:(hq + 1) * HD] = dq4[g * WIN:(g + 1) * WIN]
                ds_ref[:, hq:hq + 1] += -jnp.sum(dsink4[:, g * WIN:(g + 1) * WIN], axis=1, keepdims=True)
            dprev_ref[:, ks] = dkb[:WIN]
            dself_ref[:, ks] = dkb[WIN:]
            dprev_ref[:, vs] = dvb[:WIN]
            dself_ref[:, vs] = dvb[WIN:]

    return pl.pallas_call(
        body, name="swa_bwd", grid=(nb,),
        in_specs=[pl.BlockSpec((WIN, D), lambda n: (n, C_QSW // D)),
                  pl.BlockSpec((WIN, 512), lambda n: (n, _KVB)),
                  pl.BlockSpec((WIN, 512), lambda n: (jnp.maximum(n - 1, 0), _KVB)),
                  pl.BlockSpec((1, SWH), lambda n: (0, 0)),
                  pl.BlockSpec((SWH, WIN), lambda n: (0, n)),
                  pl.BlockSpec((WIN, D), lambda n: (n, 0)), _ANY],
        out_specs=[pl.BlockSpec((WIN, D), lambda n: (n, C_QSW // D)), pl.BlockSpec((WIN, 512), lambda n: (n, 0)),
                   pl.BlockSpec((WIN, 512), lambda n: (n, 0)), pl.BlockSpec((1, SWH), lambda n: (0, 0))],
        out_shape=[jax.ShapeDtypeStruct((t, NP), BF16), jax.ShapeDtypeStruct((t, 512), F32),
                   jax.ShapeDtypeStruct((t, 512), F32), jax.ShapeDtypeStruct((1, SWH), F32)],
        input_output_aliases={6: 0}, compiler_params=_params(),
    )(proj, proj, proj, sinks, lse, dyb, dproj)


def _kv_combine(dself, dprev, dif, dproj):
    t = dself.shape[0]
    nb = t // WIN

    def body(a_ref, b_ref, dif_ref, _, o_ref):
        nxt = jnp.where(pl.program_id(0) < nb - 1, b_ref[...], 0.0)
        o_ref[:, 0:512] = (a_ref[...] + nxt).astype(BF16)
        lane = lax.broadcasted_iota(jnp.int32, (WIN, 128), 1)
        dif_v = dif_ref[...]
        first = jnp.zeros((WIN, 128), F32)
        for col in range(8):
            first = first + jnp.where(lane == col, dif_v[:, col:col + 1], 0.0)
        o_ref[:, 512:640] = first.astype(BF16)
        o_ref[:, 640:512 + IFW] = jnp.zeros((WIN, IFW - 128), BF16)

    return pl.pallas_call(
        body, name="kv_combine", grid=(nb,),
        in_specs=[pl.BlockSpec((WIN, 512), lambda n: (n, 0)),
                  pl.BlockSpec((WIN, 512), lambda n: (jnp.minimum(n + 1, nb - 1), 0)),
                  pl.BlockSpec((WIN, 8), lambda n: (n, 0)), _ANY],
        out_specs=pl.BlockSpec((WIN, 512 + IFW), lambda n: (n, C_KV // (512 + IFW))),
        out_shape=jax.ShapeDtypeStruct((t, NP), BF16), input_output_aliases={3: 0}, compiler_params=_params(),
    )(dself, dprev, dif, dproj)


def _sds(t, n, dtype):
    return jax.ShapeDtypeStruct((t, n), dtype)


def _proj_in(h0, w_in):
    t = h0.shape[0]

    tn = 2 * IFW

    def epilogue(accs, ins, outs, i, j):
        outs[0][...] = accs[0].astype(BF16)

        @pl.when(j == C_IF // tn)
        def _():
            outs[1][...] = accs[0][:, C_IF % tn:]

    gate_cols = lambda tm, tn: pl.BlockSpec((tm, IFW), lambda i, j, kk: (i, 0))
    return _mm_ep([(h0, w_in)], "nn", "mm_in", epilogue, [],
                  [(_sds(t, NP, BF16), _tile()), (_sds(t, IFW, F32), gate_cols)], 1024, tn)


def _branch_merge(ya, yb, wa, wb, proj):
    t = ya.shape[0]

    def epilogue(accs, ins, outs, i, j):
        za, zb = accs
        merged = _sigmoid(ins[0][...].astype(F32)) * za + _sigmoid(ins[1][...].astype(F32)) * zb
        outs[0][...] = merged.astype(BF16)
        outs[1][...] = za.astype(BF16)
        outs[2][...] = zb.astype(BF16)

    return _mm_ep([(ya, wa), (yb, wb)], "nn", "mm_branch_merge", epilogue, [(proj, _tile(C_GA)), (proj, _tile(C_GB))],
                  [(_sds(t, D, BF16), _tile())] * 3, 1024, 512)


def _dmerged_bwd(dxb, w_out, proj, za, zb):
    t = dxb.shape[0]

    def epilogue(accs, ins, outs, i, j):
        dm = accs[0]
        sa, sb = _sigmoid(ins[0][...].astype(F32)), _sigmoid(ins[1][...].astype(F32))
        outs[0][...] = (dm * sa).astype(BF16)
        outs[1][...] = (dm * sb).astype(BF16)
        outs[2][:, 0:D] = (dm * ins[2][...].astype(F32) * sa * (1.0 - sa)).astype(BF16)
        outs[2][:, D:2 * D] = (dm * ins[3][...].astype(F32) * sb * (1.0 - sb)).astype(BF16)

    gate_cols = lambda tm, tn: pl.BlockSpec((tm, 2 * D), lambda i, j, kk: (i, C_GA // (2 * D)))
    return _mm_ep([(dxb, w_out)], "nt", "mm_dmerged_bwd", epilogue,
                  [(proj, _tile(C_GA)), (proj, _tile(C_GB)), (za, _tile()), (zb, _tile())],
                  [(_sds(t, D, BF16), _tile()), (_sds(t, D, BF16), _tile()), (_sds(t, NP, BF16), gate_cols)], 512, D)


def _dya_bwd(dza, wa, hraw, proj, g, dproj):
    t = dza.shape[0]

    def epilogue(accs, ins, outs, i, j):
        h_ref, o_ref, g_ref, _ = ins
        dh_ref, do_ref, dg_ref = outs

        @pl.when(i == 0)
        def _():
            dg_ref[...] = jnp.zeros_like(dg_ref)

        dy = accs[0]
        so = _sigmoid(o_ref[...].astype(F32))
        for h in range(MLH):
            sl = slice(h * DV, (h + 1) * DV)
            xn, rstd = _rms(h_ref[:, sl])
            gs = g_ref[:, sl]
            do_ref[:, sl] = (dy[:, sl] * xn * gs * so[:, sl] * (1.0 - so[:, sl])).astype(BF16)
            dhn = dy[:, sl] * so[:, sl]
            dg_ref[:, sl] += jnp.sum(dhn * xn, axis=0, keepdims=True)
            dh_ref[:, sl] = _rms_bwd(xn, rstd, dhn * gs)

    return _mm_ep([(dza, wa)], "nt", "mm_dya_bwd", epilogue,
                  [(hraw, _tile()), (proj, _tile(C_O)), (g, _row()), (dproj, lambda tm, tn: _ANY)],
                  [(_sds(t, D, F32), _tile()), (_sds(t, NP, BF16), _tile(C_O)), (_sds(1, D, F32), _row())],
                  512, D, aliases={3: 1})


def _up_act(hn, w_up):
    t = hn.shape[0]

    def epilogue(accs, ins, outs, i, j):
        r = jnp.maximum(accs[0], 0.0)
        outs[0][...] = (r * r).astype(BF16)
        outs[1][...] = accs[0].astype(BF16)

    return _mm_ep([(hn, w_up)], "nn", "mm_up_act", epilogue, [],
                  [(_sds(t, DFF, BF16), _tile()), (_sds(t, DFF, BF16), _tile())], 1024, 1024)


def _da_du(dxb, w_down, u):
    t = dxb.shape[0]

    def epilogue(accs, ins, outs, i, j):
        outs[0][...] = (accs[0] * 2.0 * jnp.maximum(ins[0][...].astype(F32), 0.0)).astype(BF16)

    return _mm_ep([(dxb, w_down)], "nt", "mm_da_du", epilogue, [(u, _tile())], [(_sds(t, DFF, BF16), _tile())],
                  1024, 1024)[0]


def _resid_norm_mm(a, w, x, g, name):
    t = x.shape[0]

    def epilogue(accs, ins, outs, i, j):
        x1 = ins[0][...] + accs[0]
        outs[0][...] = x1
        xn, _ = _rms(x1)
        outs[1][...] = (xn * ins[1][...]).astype(BF16)

    return _mm_ep([(a, w)], "nn", name, epilogue, [(x, _tile()), (g, _row())],
                  [(_sds(t, D, F32), _tile()), (_sds(t, D, BF16), _tile())], 512, D)


def _norm_bwd_mm(dy, w, x, g, dres, name):
    t = x.shape[0]

    def epilogue(accs, ins, outs, i, j):
        @pl.when(i == 0)
        def _():
            outs[2][...] = jnp.zeros_like(outs[2])

        dh = accs[0]
        xn, rstd = _rms(ins[0][...])
        outs[2][...] += jnp.sum(dh * xn, axis=0, keepdims=True)
        dx = ins[2][...] + _rms_bwd(xn, rstd, dh * ins[1][...])
        outs[0][...] = dx
        outs[1][...] = dx.astype(BF16)

    return _mm_ep([(dy, w)], "nt", name, epilogue, [(x, _tile()), (g, _row()), (dres, _tile())],
                  [(_sds(t, D, F32), _tile()), (_sds(t, D, BF16), _tile()), (_sds(1, D, F32), _row())], 512, D)


def _ple_final_mm(hn2, w_gate, x2, pp, target, gf):
    t = x2.shape[0]

    def epilogue(accs, ins, outs, i, j):
        loss_ref, dg_ref, dx_ref, dpp_ref, dgp_ref = outs

        @pl.when(i == 0)
        def _():
            loss_ref[...] = jnp.zeros_like(loss_ref)
            dg_ref[...] = jnp.zeros_like(dg_ref)

        gate = _sigmoid(accs[0])
        pp_v = ins[1][...]
        x3 = ins[0][...] + gate * pp_v
        xn, rstd = _rms(x3)
        gf_v = ins[3][...]
        err = xn * gf_v - ins[2][...]
        loss_ref[...] += (0.5 / D) * jnp.sum(jnp.sum(err * err, axis=1, keepdims=True), axis=0, keepdims=True)
        dy = err * (1.0 / D)
        dg_ref[...] += jnp.sum(dy * xn, axis=0, keepdims=True)
        dx3 = _rms_bwd(xn, rstd, dy * gf_v)
        dx_ref[...] = dx3
        dpp_ref[...] = (dx3 * gate).astype(BF16)
        dgp_ref[...] = (dx3 * pp_v * gate * (1.0 - gate)).astype(BF16)

    one = lambda tm, tn: pl.BlockSpec((1, 1), lambda i, j, kk: (0, 0))
    return _mm_ep([(hn2, w_gate)], "nn", "mm_ple_final", epilogue,
                  [(x2, _tile()), (pp, _tile()), (target, _tile()), (gf, _row())],
                  [(_sds(1, 1, F32), one), (_sds(1, D, F32), _row()), (_sds(t, D, F32), _tile()),
                   (_sds(t, D, BF16), _tile()), (_sds(t, D, BF16), _tile())], 512, D)


def _win_pad(w):
    zeros = jnp.zeros((w.shape[0], IFW - 8), w.dtype)
    return jnp.concatenate([w[:, 0:3072], w[:, 3080:4104], w[:, 4616:6664], w[:, 4104:4616], w[:, 3072:3080], zeros],
                           axis=1)


def _win_unpad(wp):
    return jnp.concatenate([wp[:, 0:3072], wp[:, C_IF:C_IF + 8], wp[:, C_QSW:C_QSW + 1024], wp[:, C_KV:C_KV + 512],
                            wp[:, C_GA:C_GA + 2048]], axis=1)


def _local_step(x, p, target, w, late_weights=None, early_grads=None, last_grad=None):
    t = x.shape[0]
    pb = p.astype(BF16)
    w = dict(w)

    h0 = _norm_fwd(x, w["norm_mix_g"], "norm_mix")
    proj, gates = _proj_in(h0, w["w_in"])
    qk = _conv_silu_fwd(proj, w["conv_qk"])
    grow, sneg_row = _gates_fwd(gates[:, 0:8].T, w["b_if"].reshape(8, 1))
    gcol, sneg_col = grow.T, sneg_row.T
    hraw, cs, st = _mlstm_fwd(qk, proj, grow, gcol)
    ya = _ya_fwd(hraw, proj, w["mlstm_norm_g"])
    yb, lse = _swa_fwd(proj, w["sinks"])
    if late_weights is not None:
        w.update(late_weights(yb))
    merged, za, zb = _branch_merge(ya, yb, w["w_branch_a"], w["w_branch_b"], proj)
    x1, hn1 = _resid_norm_mm(merged, w["w_out"], x, w["norm_mlp_g"], "mm_out_norm")
    act, u = _up_act(hn1, w["w_up"])
    x2, hn2 = _resid_norm_mm(act, w["w_down"], x1, w["norm_ple_g"], "mm_down_norm")
    pp = _mm(pb, w["w_ple_proj"], "nn", F32, "mm_ple_proj")
    loss, d_final_g, dx3, dpp, dgpre = _ple_final_mm(hn2, w["w_ple_gate"], x2, pp, target, w["final_norm_g"])

    g = {"final_norm_g": d_final_g}
    g["w_ple_proj"] = _mm(pb, dpp, "tn", F32, "mm_d_ple_proj", out_chunks=4)
    g["w_ple_gate"] = _mm(hn2, dgpre, "tn", F32, "mm_d_ple_gate")
    dx2, dx2b, g["norm_ple_g"] = _norm_bwd_mm(dgpre, w["w_ple_gate"], x2, w["norm_ple_g"], dx3, "mm_dhn2_norm")
    g["w_down"] = _mm(act, dx2b, "tn", F32, "mm_d_down")
    du = _da_du(dx2b, w["w_down"], u)
    g["w_up"] = _mm(hn1, du, "tn", F32, "mm_d_up", out_chunks=4)
    dx1, dx1b, g["norm_mlp_g"] = _norm_bwd_mm(du, w["w_up"], x1, w["norm_mlp_g"], dx2, "mm_dhn1_norm")
    g["w_out"] = _mm(merged, dx1b, "tn", F32, "mm_d_out")
    dza, dzb, dproj = _dmerged_bwd(dx1b, w["w_out"], proj, za, zb)
    g["w_branch_a"] = _mm(ya, dza, "tn", F32, "mm_d_branch_a")
    g["w_branch_b"] = _mm(yb, dzb, "tn", F32, "mm_d_branch_b")
    gain = w["mlstm_norm_g"] if early_grads is None else w["mlstm_norm_g"] + early_grads(g)
    dyb = _mm(dzb, w["w_branch_b"], "nt", F32, "mm_dyb")
    dhraw, dproj, g["mlstm_norm_g"] = _dya_bwd(dza, w["w_branch_a"], hraw, proj, gain, dproj)
    dqk, dproj, dif, g["b_if"] = _mlstm_bwd(qk, proj, grow, gcol, sneg_col, cs, st, hraw, dhraw, dproj)
    dc, g["conv_qk"] = _conv_silu_bwd_a(proj, w["conv_qk"], dqk)
    dproj = _conv_silu_bwd_b(dc, w["conv_qk"], dproj)
    dproj, dkv_self, dkv_prev, g["sinks"] = _swa_bwd(proj, w["sinks"], lse, dyb, dproj)
    dproj = _kv_combine(dkv_self, dkv_prev, dif, dproj)
    g["w_in"] = _mm(h0, dproj, "tn", F32, "mm_d_in")
    gain = w["norm_mix_g"] if last_grad is None else w["norm_mix_g"] + last_grad(g)
    grad_x, _, g["norm_mix_g"] = _norm_bwd_mm(dproj, w["w_in"], x, gain, dx1, "mm_dh0_norm")
    return loss, grad_x, g


_W4 = ("w_branch_a", "w_branch_b", "w_out", "w_ple_gate")
_SHARDED_NAMES = ("w_in", "w_up", "w_down", "w_ple_proj", "conv_qk") + _W4
_SMALL_ROWS = 16
_CONV_ROW = 8


def _group(s):
    return [s["w_in"], jnp.concatenate([s[n] for n in _W4], axis=0), s["w_up"], s["w_down"], s["w_ple_proj"]]


def _ungroup(arrs):
    out = {"w_in": arrs[0], "w_up": arrs[2], "w_down": arrs[3], "w_ple_proj": arrs[4]}
    rows = arrs[1].shape[0] // len(_W4)
    for i, n in enumerate(_W4):
        out[n] = arrs[1][i * rows:(i + 1) * rows]
    return out


def _rows_tile(rows):
    return 256 if rows % 256 == 0 else rows


_SMALL = ("norm_mix_g", "mlstm_norm_g", "norm_mlp_g", "norm_ple_g", "final_norm_g")


def _pack_small(vals, extra=None, conv=None):
    rows = [vals[n].reshape(1, D) for n in _SMALL]
    tail = [vals["b_if"].reshape(1, 8), vals["sinks"].reshape(1, SWH)]
    used = 8 + SWH
    if extra is not None:
        tail.append(extra.reshape(1, 1))
        used += 1
    tail.append(jnp.zeros((1, D - used), F32))
    rows.append(jnp.concatenate(tail, axis=1))
    rows.append(jnp.zeros((_CONV_ROW - len(rows), D), F32))
    rows.append(jnp.zeros((CONV, D), F32) if conv is None else conv)
    rows.append(jnp.zeros((_SMALL_ROWS - _CONV_ROW - CONV, D), F32))
    return jnp.concatenate(rows, axis=0)


def _unpack_small(slab, shapes):
    out = {n: slab[i].reshape(shapes[n]) for i, n in enumerate(_SMALL)}
    out["b_if"] = slab[5, 0:8].reshape(shapes["b_if"])
    out["sinks"] = slab[5, 8:8 + SWH].reshape(shapes["sinks"])
    return out


_MESH = pl.DeviceIdType.MESH
_HBM = pl.BlockSpec(memory_space=pltpu.HBM)
_VMEM = pl.BlockSpec(memory_space=pltpu.VMEM)


def _place():
    x, y, c = lax.axis_index("x"), lax.axis_index("y"), lax.axis_index("c")
    return x, y, c, 2 * x + y


def _chip_peer(x, y, r):
    return (x ^ (r >> 1), y ^ (r & 1))


def _half(ref, which):
    h = ref.shape[-2] // 2
    return pl.ds(which * h, h)


def _allgather_weights(shards, conv):
    n = len(shards)

    def body(*refs):
        ins, conv_ref = refs[:n], refs[n]
        outs, conv_out = refs[n + 1:2 * n + 1], refs[2 * n + 1]
        send_a, recv_a, send_b, recv_b, send_c, recv_c, local_sems = refs[2 * n + 2:]
        x, y, c, j = _place()
        sibling = (x, y, 1 - c)
        local = [pltpu.make_async_copy(ins[k], outs[k].at[j], local_sems.at[k]) for k in range(n)]
        local.append(pltpu.make_async_copy(conv_ref, conv_out.at[j], local_sems.at[n]))
        for cp in local:
            cp.start()

        def copy_a(k, r, chip):
            rows = _half(ins[k], c)
            return pltpu.make_async_remote_copy(
                src_ref=ins[k].at[rows], dst_ref=outs[k].at[chip, rows], send_sem=send_a.at[3 * k + r - 1],
                recv_sem=recv_a.at[3 * k + r - 1], device_id=(*_chip_peer(x, y, r), c), device_id_type=_MESH)

        def copy_b(k, r, chip, which):
            rows = _half(ins[k], which)
            return pltpu.make_async_remote_copy(
                src_ref=outs[k].at[chip, rows], dst_ref=outs[k].at[chip, rows], send_sem=send_b.at[3 * k + r - 1],
                recv_sem=recv_b.at[3 * k + r - 1], device_id=sibling, device_id_type=_MESH)

        def copy_c(r, chip):
            return pltpu.make_async_remote_copy(
                src_ref=conv_ref, dst_ref=conv_out.at[chip], send_sem=send_c.at[r - 1],
                recv_sem=recv_c.at[r - 1], device_id=(*_chip_peer(x, y, r), c), device_id_type=_MESH)

        for k in range(n):
            for r in (1, 2, 3):
                copy_a(k, r, j).start()
        for r in (1, 2, 3):
            copy_c(r, j).start()
        for k in range(n):
            for r in (1, 2, 3):
                copy_a(k, r, j ^ r).wait_recv()
                copy_b(k, r, j ^ r, c).start()
        for k in range(n):
            for r in (1, 2, 3):
                copy_b(k, r, j ^ r, 1 - c).wait_recv()
        for r in (1, 2, 3):
            copy_c(r, j ^ r).wait_recv()
        for k in range(n):
            for r in (1, 2, 3):
                copy_a(k, r, j).wait_send()
                copy_b(k, r, j ^ r, c).wait_send()
        for r in (1, 2, 3):
            copy_c(r, j).wait_send()
        for cp in local:
            cp.wait()

    return pl.pallas_call(
        body, name="allgather_weights",
        out_shape=[jax.ShapeDtypeStruct((4,) + s.shape, s.dtype) for s in shards]
        + [jax.ShapeDtypeStruct((4,) + conv.shape, F32)],
        in_specs=[_HBM] * (n + 1), out_specs=[_HBM] * (n + 1),
        scratch_shapes=[pltpu.SemaphoreType.DMA((3 * n,))] * 4 + [pltpu.SemaphoreType.DMA((3,))] * 2
        + [pltpu.SemaphoreType.DMA((n + 1,))],
    )(*shards, conv)


_SEM = pl.BlockSpec(memory_space=pltpu.SEMAPHORE)
_DATAFLOW = pltpu.SideEffectType.DATAFLOW_SIDE_EFFECTING


def _late_peer_copy(src_ref, land_ref, send_sems, recv_sems, x, y, c, j, r, chip):
    return pltpu.make_async_remote_copy(
        src_ref=src_ref, dst_ref=land_ref.at[chip], send_sem=send_sems.at[r - 1], recv_sem=recv_sems.at[r - 1],
        device_id=(*_chip_peer(x, y, r), c), device_id_type=_MESH)


def _late_gather_start(rest):
    def body(rest_ref, land_ref, send_sems, recv_sems, rest_thru, land_thru, token):
        x, y, c, j = _place()
        for r in (1, 2, 3):
            _late_peer_copy(rest_ref, land_ref, send_sems, recv_sems, x, y, c, j, r, j).start()
        token[...] = jnp.zeros_like(token)

    j = 2 * lax.axis_index("x") + lax.axis_index("y")
    land = lax.dynamic_update_slice(lax.empty((4,) + rest.shape, rest.dtype), rest[None], (j, 0, 0))
    return pl.pallas_call(
        body, name="late_gather_start",
        out_shape=(pltpu.SemaphoreType.DMA((3,)), pltpu.SemaphoreType.DMA((3,)), pltpu.HBM(rest.shape, rest.dtype),
                   pltpu.HBM(land.shape, land.dtype), jax.ShapeDtypeStruct((8, 128), F32)),
        in_specs=(_HBM, _HBM), out_specs=(_SEM, _SEM, _HBM, _HBM, _VMEM), input_output_aliases={0: 2, 1: 3},
        compiler_params=pltpu.CompilerParams(has_side_effects=_DATAFLOW),
    )(pltpu.with_memory_space_constraint(rest, pltpu.HBM), pltpu.with_memory_space_constraint(land, pltpu.HBM))


def _late_gather_wait(send_sems, recv_sems, rest_thru, land_thru, after):
    def body(rest_ref, land_ref, send_sems, recv_sems, after_ref, rest_dead, got_ref):
        x, y, c, j = _place()
        for r in (1, 2, 3):
            cp = _late_peer_copy(rest_ref, land_ref, send_sems, recv_sems, x, y, c, j, r, j ^ r)
            cp.wait_send()
            cp.wait_recv()

    return pl.pallas_call(
        body, name="late_gather_wait",
        out_shape=(pltpu.HBM(rest_thru.shape, rest_thru.dtype), pltpu.HBM(land_thru.shape, land_thru.dtype)),
        in_specs=(_HBM, _HBM, _SEM, _SEM, _ANY), out_specs=(_HBM, _HBM), input_output_aliases={0: 0, 1: 1},
        compiler_params=pltpu.CompilerParams(has_side_effects=_DATAFLOW),
    )(rest_thru, land_thru, send_sems, recv_sems, after)[1]


def _pair_exchange(gs, name):
    n = len(gs)

    def body(*refs):
        ins, outs, send_sems, recv_sems = refs[:n], refs[n:2 * n], refs[2 * n], refs[2 * n + 1]
        x, y, c, _ = _place()
        cps = [pltpu.make_async_remote_copy(
            src_ref=ins[k].at[:, _half(ins[k], 1 - c)], dst_ref=outs[k], send_sem=send_sems.at[k],
            recv_sem=recv_sems.at[k], device_id=(x, y, 1 - c), device_id_type=_MESH) for k in range(n)]
        for cp in cps:
            cp.start()
        for cp in cps:
            cp.wait()

    return pl.pallas_call(
        body, name=name,
        out_shape=[jax.ShapeDtypeStruct((4, g.shape[1] // 2, g.shape[2]), F32) for g in gs],
        in_specs=[_HBM] * n, out_specs=[_HBM] * n, scratch_shapes=[pltpu.SemaphoreType.DMA((n,))] * 2,
    )(*gs)


def _pair_sum(g, theirs, c, name):
    _, h, cols = theirs.shape
    tr = _rows_tile(h)
    nb = h // tr

    def body(c_ref, a_ref, b_ref, o_ref, ob_ref):
        s = a_ref[...] + b_ref[...]
        o_ref[...] = s
        ob_ref[...] = s.astype(BF16)

    blk = pl.BlockSpec((1, tr, cols), lambda k, i, c_ref: (k, i, 0))
    return pl.pallas_call(
        body, name=name,
        grid_spec=pltpu.PrefetchScalarGridSpec(
            num_scalar_prefetch=1, grid=(4, nb),
            in_specs=[pl.BlockSpec((1, tr, cols), lambda k, i, c_ref: (k, c_ref[0] * nb + i, 0)), blk],
            out_specs=[blk, blk]),
        out_shape=[jax.ShapeDtypeStruct(theirs.shape, F32), jax.ShapeDtypeStruct(theirs.shape, BF16)],
        compiler_params=_params(),
    )(c.reshape(1).astype(jnp.int32), g, theirs)


def _chip_copies(srcs, lands, send_sems, recv_sems):
    x, y, c, j = _place()
    return [pltpu.make_async_remote_copy(
        src_ref=srcs[k].at[j ^ r], dst_ref=lands[k].at[r - 1], send_sem=send_sems.at[3 * k + r - 1],
        recv_sem=recv_sems.at[3 * k + r - 1], device_id=(*_chip_peer(x, y, r), c), device_id_type=_MESH)
        for k in range(len(srcs)) for r in (1, 2, 3)]


def _chip_exchange_start(ss, tag):
    n = len(ss)

    def body(*refs):
        srcs, lands, send_sems, recv_sems, token = refs[:n], refs[n:2 * n], refs[2 * n], refs[2 * n + 1], refs[-1]
        for cp in _chip_copies(srcs, lands, send_sems, recv_sems):
            cp.start()
        token[...] = jnp.zeros_like(token)

    lands = [lax.empty((3,) + s.shape[1:], s.dtype) for s in ss]
    hbm = [pltpu.HBM(a.shape, a.dtype) for a in list(ss) + lands]
    out = pl.pallas_call(
        body, name="chip_exchange_start_" + tag,
        out_shape=(pltpu.SemaphoreType.DMA((3 * n,)), pltpu.SemaphoreType.DMA((3 * n,)), *hbm,
                   jax.ShapeDtypeStruct((8, 128), F32)),
        in_specs=[_HBM] * (2 * n), out_specs=(_SEM, _SEM, *([_HBM] * (2 * n)), _VMEM),
        input_output_aliases={k: 2 + k for k in range(2 * n)},
        compiler_params=pltpu.CompilerParams(has_side_effects=_DATAFLOW),
    )(*[pltpu.with_memory_space_constraint(a, pltpu.HBM) for a in list(ss) + lands])
    return out[0], out[1], list(out[2:2 + n]), list(out[2 + n:2 + 2 * n]), out[-1]


def _chip_exchange_wait(send_sems, recv_sems, ss_thru, lands_thru, after, tag):
    n = len(ss_thru)

    def body(*refs):
        srcs, lands, send_sems, recv_sems = refs[:n], refs[n:2 * n], refs[2 * n], refs[2 * n + 1]
        for cp in _chip_copies(srcs, lands, send_sems, recv_sems):
            cp.wait_send()
            cp.wait_recv()

    hbm = [pltpu.HBM(a.shape, a.dtype) for a in list(ss_thru) + list(lands_thru)]
    out = pl.pallas_call(
        body, name="chip_exchange_wait_" + tag, out_shape=tuple(hbm),
        in_specs=[_HBM] * (2 * n) + [_SEM, _SEM, _ANY], out_specs=tuple([_HBM] * (2 * n)),
        input_output_aliases={k: k for k in range(2 * n)},
        compiler_params=pltpu.CompilerParams(has_side_effects=_DATAFLOW),
    )(*ss_thru, *lands_thru, send_sems, recv_sems, after)
    return list(out[n:])


def _reduce4(own, others, j, c, name):
    _, h, cols = own.shape
    tr = _rows_tile(h)
    nb = h // tr

    def body(idx_ref, s_ref, a0, a1, a2, o_ref):
        o_ref[...] = ((s_ref[0] + a0[0].astype(F32)) + a1[0].astype(F32)) + a2[0].astype(F32)

    def other(r):
        return pl.BlockSpec((1, tr, cols), lambda i, idx_ref: (r, i, 0))

    return pl.pallas_call(
        body, name=name,
        grid_spec=pltpu.PrefetchScalarGridSpec(
            num_scalar_prefetch=1, grid=(nb,),
            in_specs=[pl.BlockSpec((1, tr, cols), lambda i, idx_ref: (idx_ref[0], i, 0)), other(0), other(1), other(2)],
            out_specs=pl.BlockSpec((tr, cols), lambda i, idx_ref: (idx_ref[1] * nb + i, 0))),
        out_shape=jax.ShapeDtypeStruct((2 * h, cols), F32), compiler_params=_params(),
    )(jnp.stack([j, c]).astype(jnp.int32), own, others, others, others)


def _sibling_share(fulls):
    n = len(fulls)

    def body(*refs):
        outs, send_sems, recv_sems = refs[n:2 * n], refs[2 * n], refs[2 * n + 1]
        x, y, c, _ = _place()
        cps = [pltpu.make_async_remote_copy(
            src_ref=outs[k].at[_half(outs[k], c)], dst_ref=outs[k].at[_half(outs[k], c)], send_sem=send_sems.at[k],
            recv_sem=recv_sems.at[k], device_id=(x, y, 1 - c), device_id_type=_MESH) for k in range(n)]
        for cp in cps:
            cp.start()
        for cp in cps:
            cp.wait()

    return pl.pallas_call(
        body, name="sibling_share", out_shape=[jax.ShapeDtypeStruct(f.shape, F32) for f in fulls],
        in_specs=[_HBM] * n, out_specs=[_HBM] * n, input_output_aliases={k: k for k in range(n)},
        scratch_shapes=[pltpu.SemaphoreType.DMA((n,))] * 2,
    )(*fulls)


def _adamw(w, g, m, v):
    m1 = ADAM_B1 * m + (1.0 - ADAM_B1) * g
    v1 = ADAM_B2 * v + (1.0 - ADAM_B2) * (g * g)
    m_hat = m1 / (1.0 - ADAM_B1 ** ADAM_STEP)
    v_hat = v1 / (1.0 - ADAM_B2 ** ADAM_STEP)
    delta = -ADAM_LR * (m_hat / (jnp.sqrt(v_hat) + ADAM_EPS) + ADAM_WD * w)
    return delta, m1, v1


def _adamw_call(w, g, m, v, name):
    rows, cols = w.shape
    tr = _rows_tile(rows)

    def body(w_ref, g_ref, m_ref, v_ref, d_out, m_out, v_out):
        delta, m1, v1 = _adamw(w_ref[...], g_ref[...], m_ref[...], v_ref[...])
        d_out[...] = delta
        m_out[...] = m1
        v_out[...] = v1

    blk = pl.BlockSpec((tr, cols), lambda i: (i, 0))
    return pl.pallas_call(
        body, name=name, grid=(rows // tr,), in_specs=[blk] * 4, out_specs=[blk] * 3,
        out_shape=[jax.ShapeDtypeStruct((rows, cols), F32)] * 3, compiler_params=_params(),
    )(w, g, m, v)


def _small_allreduce(vals):
    def body(v_ref, out_ref, buf, send_sems, recv_sems):
        x, y, c, j = _place()
        me = 2 * j + c
        buf[0] = v_ref[...]

        def copy(r):
            return pltpu.make_async_remote_copy(
                src_ref=v_ref, dst_ref=buf.at[r], send_sem=send_sems.at[r - 1], recv_sem=recv_sems.at[r - 1],
                device_id=(x ^ (r >> 2), y ^ ((r >> 1) & 1), c ^ (r & 1)), device_id_type=_MESH)

        for r in range(1, 8):
            copy(r).start()
        for r in range(1, 8):
            copy(r).wait()
        acc = buf[me ^ 0]
        for d in range(1, 8):
            acc = acc + buf[me ^ d]
        out_ref[...] = acc

    return pl.pallas_call(
        body, name="small_allreduce", out_shape=jax.ShapeDtypeStruct((_SMALL_ROWS, D), F32),
        in_specs=[_VMEM], out_specs=_VMEM,
        scratch_shapes=[pltpu.VMEM((8, _SMALL_ROWS, D), F32), pltpu.SemaphoreType.DMA((7,)),
                        pltpu.SemaphoreType.DMA((7,))],
    )(vals)


_NAMES = ("norm_mix_g", "w_in", "conv_qk", "b_if", "mlstm_norm_g", "sinks", "w_branch_a", "w_branch_b", "w_out",
          "norm_mlp_g", "w_up", "w_down", "norm_ple_g", "w_ple_gate", "w_ple_proj", "final_norm_g")
_GROUP_NAMES = ("w_in", "w4", "w_up", "w_down", "w_ple_proj")


def _step(x, p, target, w, m, v):
    c = lax.axis_index("c")
    j = 2 * lax.axis_index("x") + lax.axis_index("y")

    def shards(d):
        return {n: d[n][0] for n in _SHARDED_NAMES}

    ws = shards(w)
    w_in_all, conv_all = _allgather_weights([ws["w_in"].astype(BF16)], ws["conv_qk"])
    rows_pp = PLE * (D // 4) // D
    rest = jnp.concatenate([ws[n] for n in _W4] + [ws["w_up"], ws["w_down"], ws["w_ple_proj"].reshape(rows_pp, D)],
                           axis=0)
    rest = (rest + 0.0 * conv_all[0, 0, 0]).astype(BF16)
    send_sems, recv_sems, rest_thru, land_thru, token = _late_gather_start(rest)
    full = {n: w[n] for n in ("mlstm_norm_g", "norm_mlp_g", "norm_ple_g", "b_if", "sinks")}
    full["norm_mix_g"] = w["norm_mix_g"] + token[0, 0]
    full["final_norm_g"] = w["final_norm_g"].reshape(1, D)
    full["w_in"] = _win_pad(jnp.swapaxes(w_in_all, 0, 1).reshape(D, N_IN))
    full["conv_qk"] = jnp.swapaxes(conv_all, 0, 1).reshape(CONV, D)

    def late_weights(after):
        land = _late_gather_wait(send_sems, recv_sems, rest_thru, land_thru, after)
        out = {n: land[:, i * (D // 4):(i + 1) * (D // 4)].reshape(D, D) for i, n in enumerate(_W4)}
        out["w_up"] = land[:, D:2 * D]
        out["w_down"] = land[:, 2 * D:3 * D].reshape(DFF, D)
        out["w_ple_proj"] = land[:, 3 * D:3 * D + rows_pp].reshape(4, PLE, D // 4)
        return out

    def pair_sums(by_dest, names, tag):
        theirs = _pair_exchange(by_dest, "pair_exchange_" + tag)
        return [_pair_sum(a, b, c, "pair_sum_" + n) for a, b, n in zip(by_dest, theirs, names)]

    early, last = {}, {}

    def early_grads(g):
        by_dest = [jnp.stack([g[n].reshape(4, D // 4, D) for n in _W4], axis=1).reshape(4, D, D),
                   g["w_up"], g["w_down"].reshape(4, DFF // 4, D), g["w_ple_proj"]]
        early["sums"] = pair_sums(by_dest, _GROUP_NAMES[1:], "early")
        *early["flight"], token = _chip_exchange_start([s[1] for s in early["sums"]], "early")
        return token[0, 0]

    def last_grad(g):
        w_in_g = _win_unpad(g["w_in"])
        last["sums"] = pair_sums([jnp.swapaxes(w_in_g.reshape(D, 4, N_IN // 4), 0, 1)], _GROUP_NAMES[:1], "w_in")
        *last["flight"], token = _chip_exchange_start([s[1] for s in last["sums"]], "w_in")
        return token[0, 0]

    loss, grad_x, g = _local_step(x[0], p[0, 0], target[0], full, late_weights, early_grads, last_grad)

    others = _chip_exchange_wait(*last["flight"], grad_x, "w_in")
    others += _chip_exchange_wait(*early["flight"], others[0], "early")
    sums = last["sums"] + early["sums"]
    halves = [_reduce4(s[0], b, j, c, "reduce4_" + n) for s, b, n in zip(sums, others, _GROUP_NAMES)]
    grads = _sibling_share(halves)

    small_g = _small_allreduce(_pack_small(g, extra=loss, conv=g["conv_qk"]))
    conv_g = lax.dynamic_slice(small_g[_CONV_ROW:_CONV_ROW + CONV], (0, j * (D // 4)), (CONV, D // 4))

    ms, vs = shards(m), shards(v)
    upd = [_adamw_call(wa, ga, ma, va, "adamw_" + n)
           for wa, ga, ma, va, n in zip(_group(ws), grads, _group(ms), _group(vs), _GROUP_NAMES)]
    conv_upd = _adamw_call(ws["conv_qk"], conv_g, ms["conv_qk"], vs["conv_qk"], "adamw_conv")
    small_upd = _adamw_call(_pack_small(w), small_g, _pack_small(m), _pack_small(v), "adamw_small")

    shapes = {n: w[n].shape for n in _NAMES}
    res = []
    for k in range(4):
        big = _ungroup(list(grads) if k == 0 else [u[k - 1] for u in upd])
        big["conv_qk"] = conv_g if k == 0 else conv_upd[k - 1]
        leaves = _unpack_small(small_g if k == 0 else small_upd[k - 1], shapes)
        leaves.update({n: a.reshape(shapes[n]) for n, a in big.items()})
        res.append(leaves)

    out = [small_g[5, 8 + SWH], grad_x[None]]
    for k in range(4):
        out += [res[k][n] for n in _NAMES]
    return tuple(out)


def kernel(x, p, norm_mix_g, w_in, conv_qk, b_if, mlstm_norm_g, sinks, w_branch_a, w_branch_b, w_out, norm_mlp_g, w_up, w_down, norm_ple_g, w_ple_gate, w_ple_proj, final_norm_g, loss_target, m_norm_mix_g, m_w_in, m_conv_qk, m_b_if, m_mlstm_norm_g, m_sinks, m_w_branch_a, m_w_branch_b, m_w_out, m_norm_mlp_g, m_w_up, m_w_down, m_norm_ple_g, m_w_ple_gate, m_w_ple_proj, m_final_norm_g, v_norm_mix_g, v_w_in, v_conv_qk, v_b_if, v_mlstm_norm_g, v_sinks, v_w_branch_a, v_w_branch_b, v_w_out, v_norm_mlp_g, v_w_up, v_w_down, v_norm_ple_g, v_w_ple_gate, v_w_ple_proj, v_final_norm_g):
    w = dict(zip(_NAMES, (norm_mix_g, w_in, conv_qk, b_if, mlstm_norm_g, sinks, w_branch_a, w_branch_b, w_out,
                          norm_mlp_g, w_up, w_down, norm_ple_g, w_ple_gate, w_ple_proj, final_norm_g)))
    m = dict(zip(_NAMES, (m_norm_mix_g, m_w_in, m_conv_qk, m_b_if, m_mlstm_norm_g, m_sinks, m_w_branch_a,
                          m_w_branch_b, m_w_out, m_norm_mlp_g, m_w_up, m_w_down, m_norm_ple_g, m_w_ple_gate,
                          m_w_ple_proj, m_final_norm_g)))
    v = dict(zip(_NAMES, (v_norm_mix_g, v_w_in, v_conv_qk, v_b_if, v_mlstm_norm_g, v_sinks, v_w_branch_a,
                          v_w_branch_b, v_w_out, v_norm_mlp_g, v_w_up, v_w_down, v_norm_ple_g, v_w_ple_gate,
                          v_w_ple_proj, v_final_norm_g)))
    return _step(x, p, loss_target, w, m, v)
```

```python
import jax
import jax.numpy as jnp
from jax import lax
from jax.experimental import pallas as pl
from jax.experimental.pallas import tpu as pltpu

F32 = jnp.float32
BF16 = jnp.bfloat16

D = 1024
PLE = 256
MLH = 4
DQK = 128
DV = 256
CONV = 4
CHUNK = 128
SWH = 16
SWKV = 4
SWG = SWH // SWKV
HD = 64
WIN = 128
DFF = 4096
EPS = 1e-6
N_IN = 6664
NP = 7168
C_QK, C_V, C_O, C_QSW, C_GA, C_GB, C_KV, C_IF = 0, 1024, 2048, 3072, 4096, 5120, 6144, 6656
IFW = NP - C_IF

ADAM_LR = 0.001
ADAM_B1 = 0.9
ADAM_B2 = 0.999
ADAM_EPS = 1e-08
ADAM_WD = 0.01
ADAM_STEP = 10

TOK_TILE = 256
VMEM_LIMIT = 58 * 1024 * 1024


def _params(**kw):
    return pltpu.CompilerParams(vmem_limit_bytes=VMEM_LIMIT, **kw)


def _pick(n, cap):
    if n <= cap:
        return n
    t = cap - cap % 128
    while t > 128 and n % t:
        t -= 128
    assert n % t == 0, (n, cap)
    return t


def _dot(a, b, dims):
    return lax.dot_general(a, b, (dims, ((), ())), preferred_element_type=F32)


def _dot_nn(a, b):
    return _dot(a, b, ((1,), (0,)))


def _dot_nt(a, b):
    return _dot(a, b, ((1,), (1,)))


def _dot_tn(a, b):
    return _dot(a, b, ((0,), (0,)))


def _sigmoid(x):
    return 1.0 / (1.0 + jnp.exp(-x))


def _mm(a, b, mode, out_dtype, name, out_chunks=1):
    bch = b.shape[0] if b.ndim == 3 else 1
    brows, bcols = b.shape[-2], b.shape[-1] * bch
    if mode == "nn":
        (m, k), (k2, n) = a.shape, (brows, bcols)
    elif mode == "nt":
        (m, k), (n, k2) = a.shape, (brows, bcols)
    else:
        (k, m), (k2, n) = a.shape, (brows, bcols)
    assert k == k2, (a.shape, b.shape, mode)
    n_cap = n // max(out_chunks, 1 if mode == "nt" else bch)
    k_cap = k // bch if mode == "nt" else k
    tm, tn, tk = _pick(m, 1024), _pick(n_cap, 1024), _pick(k_cap, 2048)
    nk = k // tk
    if mode == "nn":
        a_spec = pl.BlockSpec((tm, tk), lambda i, j, kk: (i, kk))
        if bch > 1:
            bpc = (n // bch) // tn
            b_spec = pl.BlockSpec((None, tk, tn), lambda i, j, kk: (j // bpc, kk, j % bpc))
        else:
            b_spec = pl.BlockSpec((tk, tn), lambda i, j, kk: (kk, j))
        dot = _dot_nn
    elif mode == "nt":
        a_spec = pl.BlockSpec((tm, tk), lambda i, j, kk: (i, kk))
        if bch > 1:
            bpc = (k // bch) // tk
            b_spec = pl.BlockSpec((None, tn, tk), lambda i, j, kk: (kk // bpc, j, kk % bpc))
        else:
            b_spec = pl.BlockSpec((tn, tk), lambda i, j, kk: (j, kk))
        dot = _dot_nt
    else:
        assert bch == 1
        a_spec = pl.BlockSpec((tk, tm), lambda i, j, kk: (kk, i))
        b_spec = pl.BlockSpec((tk, tn), lambda i, j, kk: (kk, j))
        dot = _dot_tn
    if out_chunks > 1:
        npc = (n // out_chunks) // tn
        out_spec = pl.BlockSpec((None, tm, tn), lambda i, j, kk: (j // npc, i, j % npc))
        out_shape = jax.ShapeDtypeStruct((out_chunks, m, n // out_chunks), out_dtype)
    else:
        out_spec = pl.BlockSpec((tm, tn), lambda i, j, kk: (i, j))
        out_shape = jax.ShapeDtypeStruct((m, n), out_dtype)

    def body(a_ref, b_ref, o_ref, acc_ref):
        kk = pl.program_id(2)

        @pl.when(kk == 0)
        def _():
            acc_ref[...] = jnp.zeros_like(acc_ref)

        acc_ref[...] += dot(a_ref[...], b_ref[...])

        @pl.when(kk == nk - 1)
        def _():
            o_ref[...] = acc_ref[...].astype(out_dtype)

    return pl.pallas_call(
        body, name=name, grid=(m // tm, n // tn, nk),
        in_specs=[a_spec, b_spec], out_specs=out_spec, out_shape=out_shape,
        scratch_shapes=[pltpu.VMEM((tm, tn), F32)],
        compiler_params=_params(dimension_semantics=("parallel", "parallel", "arbitrary")),
    )(a, b)


def _tile(col0=0):
    return lambda tm, tn: pl.BlockSpec((tm, tn), lambda i, j, kk: (i, col0 // tn + j))


def _row():
    return lambda tm, tn: pl.BlockSpec((1, tn), lambda i, j, kk: (0, j))


def _mm_ep(pairs, mode, name, epilogue, ins, outs, tm, tn, aliases=None):
    a0, b0 = pairs[0]
    bch = b0.shape[0] if b0.ndim == 3 else 1
    m, k = a0.shape
    tm = _pick(m, tm)
    n = b0.shape[-1] * bch if mode == "nn" else b0.shape[-2]
    tk = _pick(k // bch if mode == "nt" else k, 2048)
    nk = k // tk
    a_spec = pl.BlockSpec((tm, tk), lambda i, j, kk: (i, kk))
    if mode == "nn":
        dot = _dot_nn
        if bch > 1:
            bpc = (n // bch) // tn
            b_spec = pl.BlockSpec((None, tk, tn), lambda i, j, kk: (j // bpc, kk, j % bpc))
        else:
            b_spec = pl.BlockSpec((tk, tn), lambda i, j, kk: (kk, j))
    else:
        dot = _dot_nt
        if bch > 1:
            bpc = (k // bch) // tk
            b_spec = pl.BlockSpec((None, tn, tk), lambda i, j, kk: (kk // bpc, j, kk % bpc))
        else:
            b_spec = pl.BlockSpec((tn, tk), lambda i, j, kk: (j, kk))
    npair, nin, nout = len(pairs), len(ins), len(outs)

    def body(*refs):
        ab = refs[:2 * npair]
        in_refs = refs[2 * npair:2 * npair + nin]
        out_refs = refs[2 * npair + nin:2 * npair + nin + nout]
        accs = refs[2 * npair + nin + nout:]
        i, j, kk = pl.program_id(0), pl.program_id(1), pl.program_id(2)
        for p in range(npair):
            prod = dot(ab[2 * p][...], ab[2 * p + 1][...])

            @pl.when(kk == 0)
            def _():
                accs[p][...] = prod

            @pl.when(kk > 0)
            def _():
                accs[p][...] += prod

        @pl.when(kk == nk - 1)
        def _():
            epilogue([acc[...] for acc in accs], in_refs, out_refs, i, j)

    operands = [x for pair in pairs for x in pair] + [a for a, _ in ins]
    io_alias = {2 * npair + i: o for i, o in (aliases or {}).items()}
    return pl.pallas_call(
        body, name=name, grid=(m // tm, n // tn, nk),
        in_specs=[a_spec, b_spec] * npair + [mk(tm, tn) for _, mk in ins],
        out_specs=[mk(tm, tn) for _, mk in outs], out_shape=[s for s, _ in outs],
        scratch_shapes=[pltpu.VMEM((tm, tn), F32)] * npair, input_output_aliases=io_alias,
        compiler_params=_params(dimension_semantics=("arbitrary", "arbitrary", "arbitrary")),
    )(*operands)


def _tok(w, j=0):
    return pl.BlockSpec((TOK_TILE, w), lambda i: (i, j))


def _rep(shape):
    return pl.BlockSpec(shape, lambda i: (0,) * len(shape))


def _rms(x):
    rstd = lax.rsqrt(jnp.mean(x * x, axis=-1, keepdims=True) + EPS)
    return x * rstd, rstd


def _rms_bwd(xn, rstd, dxn):
    return rstd * (dxn - xn * jnp.mean(dxn * xn, axis=-1, keepdims=True))


def _norm_fwd(x, g, name):
    t = x.shape[0]

    def body(x_ref, g_ref, h_ref):
        xn, _ = _rms(x_ref[...])
        h_ref[...] = (xn * g_ref[...]).astype(BF16)

    return pl.pallas_call(
        body, name=name, grid=(t // TOK_TILE,), in_specs=[_tok(D), _rep((1, D))], out_specs=_tok(D),
        out_shape=jax.ShapeDtypeStruct((t, D), BF16), compiler_params=_params(),
    )(x, g)


def _halo_prev(w, j=0, rows=8):
    r = TOK_TILE // rows
    return pl.BlockSpec((rows, w), lambda i: (jnp.maximum(i * r - 1, 0), j))


def _last8(halo_ref):
    return halo_ref[...].astype(F32)[halo_ref.shape[0] - 8:]


def _halo_next(w, nt, j=0):
    r = TOK_TILE // 8
    return pl.BlockSpec((8, w), lambda i: (jnp.minimum((i + 1) * r, nt * r - 1), j))


def _shift_down(x, halo, s):
    if s == 0:
        return x
    r = pltpu.roll(x, s, 0)
    hs = pltpu.roll(halo, s, 0)
    row = lax.broadcasted_iota(jnp.int32, hs.shape, 0)
    top = jnp.where(row < s, hs, r[0:8])
    return jnp.concatenate([top, r[8:]], axis=0)


def _shift_up(x, halo, s):
    if s == 0:
        return x
    n = x.shape[0]
    r = pltpu.roll(x, n - s, 0)
    hs = pltpu.roll(halo, 8 - s, 0)
    row = lax.broadcasted_iota(jnp.int32, hs.shape, 0)
    bot = jnp.where(row >= 8 - s, hs, r[n - 8:])
    return jnp.concatenate([r[:n - 8], bot], axis=0)


def _bf(x):
    return x.astype(BF16).astype(F32)


def _conv_taps(x, halo, w):
    x, halo, w = _bf(x), _bf(halo), _bf(w)
    acc = x * w[CONV - 1:CONV, :]
    for j in range(CONV - 1):
        acc = acc + _shift_down(x, halo, CONV - 1 - j) * w[j:j + 1, :]
    return acc


_Q_SCALE = DQK ** -0.5


def _qscale_row():
    lane = lax.broadcasted_iota(jnp.int32, (1, D), 1)
    return jnp.where(lane < MLH * DQK, _Q_SCALE, 1.0).astype(F32)


def _conv_silu_fwd(proj, conv_w):
    t = proj.shape[0]

    def body(x_ref, halo_ref, w_ref, o_ref):
        halo = jnp.where(pl.program_id(0) > 0, _last8(halo_ref), 0.0)
        c = _conv_taps(x_ref[...].astype(F32), halo, w_ref[...])
        o_ref[...] = (c * _sigmoid(c) * _qscale_row()).astype(BF16)

    return pl.pallas_call(
        body, name="conv_silu_fwd", grid=(t // TOK_TILE,),
        in_specs=[_tok(D, C_QK // D), _halo_prev(D, C_QK // D, 16), _rep((CONV, D))], out_specs=_tok(D),
        out_shape=jax.ShapeDtypeStruct((t, D), BF16), compiler_params=_params(),
    )(proj, proj, conv_w)


def _conv_silu_bwd_a(proj, conv_w, dqk):
    t = proj.shape[0]

    def body(x_ref, halo_ref, w_ref, d_ref, dc_ref, dw_ref):
        @pl.when(pl.program_id(0) == 0)
        def _():
            dw_ref[...] = jnp.zeros_like(dw_ref)

        halo = jnp.where(pl.program_id(0) > 0, _last8(halo_ref), 0.0)
        x = x_ref[...].astype(F32)
        c = _conv_taps(x, halo, w_ref[...])
        s = _sigmoid(c)
        dc = d_ref[...] * _qscale_row() * (s * (1.0 + c * (1.0 - s)))
        dc_ref[...] = dc
        dcb, xb, halo_b = _bf(dc), _bf(x), _bf(halo)
        for j in range(CONV):
            dw_ref[j:j + 1, :] += jnp.sum(dcb * _shift_down(xb, halo_b, CONV - 1 - j), axis=0, keepdims=True)

    return pl.pallas_call(
        body, name="conv_silu_bwd_a", grid=(t // TOK_TILE,),
        in_specs=[_tok(D, C_QK // D), _halo_prev(D, C_QK // D, 16), _rep((CONV, D)), _tok(D)],
        out_specs=[_tok(D), _rep((CONV, D))],
        out_shape=[jax.ShapeDtypeStruct((t, D), F32), jax.ShapeDtypeStruct((CONV, D), F32)],
        compiler_params=_params(),
    )(proj, proj, conv_w, dqk)


def _conv_silu_bwd_b(dc, conv_w, dproj):
    t = dc.shape[0]
    nt = t // TOK_TILE

    def body(dc_ref, halo_ref, w_ref, _, dx_ref):
        halo = _bf(jnp.where(pl.program_id(0) < nt - 1, halo_ref[...], 0.0))
        dcv = _bf(dc_ref[...])
        w = _bf(w_ref[...])
        acc = dcv * w[CONV - 1:CONV, :]
        for j in range(CONV - 1):
            acc = acc + _shift_up(dcv, halo, CONV - 1 - j) * w[j:j + 1, :]
        dx_ref[...] = acc.astype(BF16)

    return pl.pallas_call(
        body, name="conv_silu_bwd_b", grid=(nt,), in_specs=[_tok(D), _halo_next(D, nt), _rep((CONV, D)), _ANY],
        out_specs=_tok(D, C_QK // D), out_shape=jax.ShapeDtypeStruct((t, NP), BF16),
        input_output_aliases={3: 0}, compiler_params=_params(),
    )(dc, dc, conv_w, dproj)


def _gates_fwd(pre_rows, bias_col):
    t = pre_rows.shape[1]

    def body(p_ref, b_ref, g_ref, s_ref):
        z = p_ref[...] + b_ref[...]
        lf = jnp.minimum(z, 0.0) - jnp.log(1.0 + jnp.exp(-jnp.abs(z)))
        lane = lax.broadcasted_iota(jnp.int32, z.shape, 1) % CHUNK
        cum = lf
        s = 1
        while s < CHUNK:
            cum = cum + jnp.where(lane >= s, pltpu.roll(cum, s, 1), 0.0)
            s *= 2
        sub = lax.broadcasted_iota(jnp.int32, z.shape, 0)
        g_ref[...] = jnp.where(sub < MLH, z, cum)
        s_ref[...] = _sigmoid(-z)

    return pl.pallas_call(
        body, name="gates_fwd",
        out_shape=[jax.ShapeDtypeStruct((8, t), F32), jax.ShapeDtypeStruct((8, t), F32)],
        compiler_params=_params(),
    )(pre_rows, bias_col)


def _chunk_terms(grow, gcol, h, m0):
    i_row, b_row = grow[h:h + 1, :], grow[MLH + h:MLH + h + 1, :]
    i_col, b_col = gcol[:, h:h + 1], gcol[:, MLH + h:MLH + h + 1]
    b_last = b_row[:, CHUNK - 1:CHUNK]
    tt = lax.broadcasted_iota(jnp.int32, (CHUNK, CHUNK), 0)
    ss = lax.broadcasted_iota(jnp.int32, (CHUNK, CHUNK), 1)
    log_d = jnp.where(tt >= ss, b_col - b_row + i_row, -jnp.inf)
    m_t = jnp.maximum(b_col + m0, jnp.max(log_d, axis=1, keepdims=True))
    dm = jnp.exp(log_d - m_t)
    wi = jnp.exp(b_col + m0 - m_t)
    m1 = jnp.maximum(b_last + m0, jnp.max(b_last - b_row + i_row, axis=1, keepdims=True))
    ws = jnp.exp(b_last - b_col + i_col - m1)
    dec = jnp.exp(b_last + m0 - m1)
    return dm, wi, m_t, ws, dec, m1


def _mlstm_fwd(qk, proj, grow, gcol):
    t = qk.shape[0]
    nc = t // CHUNK

    def body(qk_ref, v_ref, grow_ref, gcol_ref, h_ref, cs_ref, st_ref, c_scr, st_scr):
        @pl.when(pl.program_id(0) == 0)
        def _():
            c_scr[...] = jnp.zeros_like(c_scr)
            st_scr[...] = jnp.zeros_like(st_scr)

        grow_v, gcol_v = grow_ref[...], gcol_ref[...]
        for h in range(MLH):
            q = qk_ref[:, h * DQK:(h + 1) * DQK]
            k = qk_ref[:, MLH * DQK + h * DQK:MLH * DQK + (h + 1) * DQK]
            v = v_ref[:, h * DV:(h + 1) * DV]
            c0 = c_scr[h]
            n0 = st_scr[h, 0:1, :]
            m0 = st_scr[h, 1:2, 0:1]
            cs_ref[0, h] = c0
            st_ref[0, h] = st_scr[h]
            dm, wi, m_t, ws, dec, m1 = _chunk_terms(grow_v, gcol_v, h, m0)
            s = _dot_nt(q, k) * dm
            num = wi * _dot_nt(q, c0.astype(BF16)) + _dot_nn(s.astype(BF16), v.astype(BF16))
            den = wi * jnp.sum(q.astype(F32) * n0, axis=1, keepdims=True) + jnp.sum(s, axis=1, keepdims=True)
            h_ref[:, h * DV:(h + 1) * DV] = num / jnp.maximum(jnp.abs(den), jnp.exp(-m_t))
            c_scr[h] = dec * c0 + _dot_tn((ws * v).astype(BF16), k)
            st_scr[h, 0:1, :] = dec * n0 + jnp.sum(ws * k.astype(F32), axis=0, keepdims=True)
            st_scr[h, 1:2, :] = jnp.broadcast_to(m1, (1, DQK))

    return pl.pallas_call(
        body, name="mlstm_fwd", grid=(nc,),
        in_specs=[pl.BlockSpec((CHUNK, D), lambda c: (c, 0)), pl.BlockSpec((CHUNK, D), lambda c: (c, C_V // D)),
                  pl.BlockSpec((8, CHUNK), lambda c: (0, c)), pl.BlockSpec((CHUNK, 8), lambda c: (c, 0))],
        out_specs=[pl.BlockSpec((CHUNK, D), lambda c: (c, 0)),
                   pl.BlockSpec((1, MLH, DV, DQK), lambda c: (c, 0, 0, 0)),
                   pl.BlockSpec((1, MLH, 8, DQK), lambda c: (c, 0, 0, 0))],
        out_shape=[jax.ShapeDtypeStruct((t, D), F32), jax.ShapeDtypeStruct((nc, MLH, DV, DQK), F32),
                   jax.ShapeDtypeStruct((nc, MLH, 8, DQK), F32)],
        scratch_shapes=[pltpu.VMEM((MLH, DV, DQK), F32), pltpu.VMEM((MLH, 8, DQK), F32)],
        compiler_params=_params(dimension_semantics=("arbitrary",)),
    )(qk, proj, grow, gcol)


def _mlstm_bwd(qk, proj, grow, gcol, sneg_col, cs, st, hraw, dh, dproj):
    t = qk.shape[0]
    nc = t // CHUNK

    def rev(c):
        return nc - 1 - c

    def nxt(c):
        return jnp.minimum(nc - c, nc - 1)

    def body(qk_ref, v_ref, grow_ref, gcol_ref, sneg_ref, cs_ref, st_ref, cs1_ref, st1_ref, h_ref, dh_ref, _,
             dqk_ref, dv_ref, dif_ref, dbif_ref, dc_scr, dn_scr):
        @pl.when(pl.program_id(0) == 0)
        def _():
            dc_scr[...] = jnp.zeros_like(dc_scr)
            dn_scr[...] = jnp.zeros_like(dn_scr)
            dbif_ref[...] = jnp.zeros_like(dbif_ref)

        grow_v, gcol_v, sneg = grow_ref[...], gcol_ref[...], sneg_ref[...]
        tt = lax.broadcasted_iota(jnp.int32, (CHUNK, CHUNK), 0)
        ss = lax.broadcasted_iota(jnp.int32, (CHUNK, CHUNK), 1)
        lane8 = lax.broadcasted_iota(jnp.int32, (CHUNK, 8), 1)
        dif = jnp.zeros((CHUNK, 8), F32)
        for h in range(MLH):
            q = qk_ref[:, h * DQK:(h + 1) * DQK]
            k = qk_ref[:, MLH * DQK + h * DQK:MLH * DQK + (h + 1) * DQK]
            qf, kf = q.astype(F32), k.astype(F32)
            v = v_ref[:, h * DV:(h + 1) * DV]
            vb = v.astype(BF16)
            c0 = cs_ref[0, h]
            n0 = st_ref[0, h, 0:1, :]
            m0 = st_ref[0, h, 1:2, 0:1]
            dc1 = dc_scr[h]
            dn1 = dn_scr[h, 0:1, :]
            dm, wi, m_t, ws, dec, _ = _chunk_terms(grow_v, gcol_v, h, m0)
            s = _dot_nt(q, k) * dm
            den = wi * jnp.sum(qf * n0, axis=1, keepdims=True) + jnp.sum(s, axis=1, keepdims=True)
            floor = jnp.exp(-m_t)
            g = jnp.maximum(jnp.abs(den), floor)
            dh_v = dh_ref[:, h * DV:(h + 1) * DV]
            dnum = dh_v / g
            dden = -jnp.sum(dh_v * h_ref[:, h * DV:(h + 1) * DV], axis=1, keepdims=True) / g
            dden = jnp.where(jnp.abs(den) > floor, dden * jnp.sign(den), 0.0)
            dnum_b = dnum.astype(BF16)
            da = ((_dot_nt(dnum_b, vb) + dden) * dm).astype(BF16)
            dc1_b = dc1.astype(BF16)
            dq = _dot_nn(da, k) + wi * (_dot_nn(dnum_b, c0.astype(BF16)) + dden * n0)
            dk = _dot_tn(da, q) + ws * (_dot_nn(vb, dc1_b) + dn1)
            dv = _dot_tn(s.astype(BF16), dnum_b) + ws * _dot_nt(k, dc1_b)
            dqk_ref[:, h * DQK:(h + 1) * DQK] = dq
            dqk_ref[:, MLH * DQK + h * DQK:MLH * DQK + (h + 1) * DQK] = dk
            dv_ref[:, h * DV:(h + 1) * DV] = dv.astype(BF16)
            rk = jnp.sum(kf * dk, axis=1, keepdims=True)
            df = jnp.sum(qf * dq, axis=1, keepdims=True) - rk
            df_row = jnp.sum(jnp.where(tt == ss, df, 0.0), axis=0, keepdims=True)
            suffix = jnp.sum(jnp.where(ss >= tt, df_row, 0.0), axis=1, keepdims=True)
            cross = (jnp.sum(jnp.sum(dc1 * cs1_ref[0, h], axis=1, keepdims=True), axis=0, keepdims=True)
                     + jnp.sum(dn1 * st1_ref[0, h, 0:1, :], axis=1, keepdims=True))
            dpf = (suffix + cross) * sneg[:, MLH + h:MLH + h + 1]
            dif = dif + jnp.where(lane8 == h, rk, 0.0) + jnp.where(lane8 == MLH + h, dpf, 0.0)
            dc_scr[h] = dec * dc1 + _dot_tn((wi * dnum).astype(BF16), q)
            dn_scr[h, 0:1, :] = dec * dn1 + jnp.sum(wi * dden * qf, axis=0, keepdims=True)
        dif_ref[...] = dif
        dbif_ref[...] += jnp.sum(dif, axis=0, keepdims=True)

    return pl.pallas_call(
        body, name="mlstm_bwd", grid=(nc,),
        in_specs=[pl.BlockSpec((CHUNK, D), lambda c: (rev(c), 0)),
                  pl.BlockSpec((CHUNK, D), lambda c: (rev(c), C_V // D)),
                  pl.BlockSpec((8, CHUNK), lambda c: (0, rev(c))),
                  pl.BlockSpec((CHUNK, 8), lambda c: (rev(c), 0)),
                  pl.BlockSpec((CHUNK, 8), lambda c: (rev(c), 0)),
                  pl.BlockSpec((1, MLH, DV, DQK), lambda c: (rev(c), 0, 0, 0)),
                  pl.BlockSpec((1, MLH, 8, DQK), lambda c: (rev(c), 0, 0, 0)),
                  pl.BlockSpec((1, MLH, DV, DQK), lambda c: (nxt(c), 0, 0, 0)),
                  pl.BlockSpec((1, MLH, 8, DQK), lambda c: (nxt(c), 0, 0, 0)),
                  pl.BlockSpec((CHUNK, D), lambda c: (rev(c), 0)),
                  pl.BlockSpec((CHUNK, D), lambda c: (rev(c), 0)), _ANY],
        out_specs=[pl.BlockSpec((CHUNK, D), lambda c: (rev(c), 0)),
                   pl.BlockSpec((CHUNK, D), lambda c: (rev(c), C_V // D)),
                   pl.BlockSpec((CHUNK, 8), lambda c: (rev(c), 0)),
                   pl.BlockSpec((1, 8), lambda c: (0, 0))],
        out_shape=[jax.ShapeDtypeStruct((t, D), F32), jax.ShapeDtypeStruct((t, NP), BF16),
                   jax.ShapeDtypeStruct((t, 8), F32), jax.ShapeDtypeStruct((1, 8), F32)],
        scratch_shapes=[pltpu.VMEM((MLH, DV, DQK), F32), pltpu.VMEM((MLH, 8, DQK), F32)],
        input_output_aliases={11: 1}, compiler_params=_params(dimension_semantics=("arbitrary",)),
    )(qk, proj, grow, gcol, sneg_col, cs, st, cs, st, hraw, dh, dproj)


def _ya_fwd(hraw, proj, g):
    t = hraw.shape[0]

    def body(h_ref, o_ref, g_ref, y_ref):
        so = _sigmoid(o_ref[...].astype(F32))
        for h in range(MLH):
            sl = slice(h * DV, (h + 1) * DV)
            xn, _ = _rms(h_ref[:, sl])
            y_ref[:, sl] = (so[:, sl] * xn * g_ref[:, sl]).astype(BF16)

    return pl.pallas_call(
        body, name="ya_fwd", grid=(t // TOK_TILE,), in_specs=[_tok(D), _tok(D, C_O // D), _rep((1, D))],
        out_specs=_tok(D), out_shape=jax.ShapeDtypeStruct((t, D), BF16), compiler_params=_params(),
    )(hraw, proj, g)


_ANY = pl.BlockSpec(memory_space=pl.ANY)


_SW_SCALE = HD ** -0.5
_KVB = C_KV // (2 * SWKV * HD)


def _swa_mask(n):
    ki = lax.broadcasted_iota(jnp.int32, (2 * WIN, SWG * WIN), 0)
    qi = lax.broadcasted_iota(jnp.int32, (2 * WIN, SWG * WIN), 1) % WIN
    return (ki > qi) & (ki <= qi + WIN) & ((n > 0) | (ki >= WIN))


def _group_rows(x_ref, hk):
    return jnp.concatenate([x_ref[:, (hk * SWG + g) * HD:(hk * SWG + g + 1) * HD] for g in range(SWG)], axis=0)


def _group_lanes(x_ref, hk):
    return jnp.concatenate([x_ref[hk * SWG + g:hk * SWG + g + 1, :] for g in range(SWG)], axis=1)


def _sink_lanes(sink_ref, hk):
    return jnp.concatenate([jnp.broadcast_to(sink_ref[:, hk * SWG + g:hk * SWG + g + 1], (1, WIN))
                            for g in range(SWG)], axis=1)


def _swa_fwd(proj, sinks):
    t = proj.shape[0]
    nb = t // WIN

    def body(q_ref, kvc_ref, kvp_ref, sink_ref, y_ref, lse_ref):
        valid = _swa_mask(pl.program_id(0))
        for hk in range(SWKV):
            ks = slice(hk * HD, (hk + 1) * HD)
            vs = slice(SWKV * HD + hk * HD, SWKV * HD + (hk + 1) * HD)
            kb = jnp.concatenate([kvp_ref[:, ks], kvc_ref[:, ks]], axis=0).astype(BF16)
            vb = jnp.concatenate([kvp_ref[:, vs], kvc_ref[:, vs]], axis=0).astype(BF16)
            q4 = _group_rows(q_ref, hk).astype(BF16)
            sink = _sink_lanes(sink_ref, hk)
            logits = jnp.where(valid, _dot_nt(kb, q4) * _SW_SCALE, -jnp.inf)
            m = jnp.maximum(jnp.max(logits, axis=0, keepdims=True), sink)
            p = jnp.exp(logits - m)
            denom = jnp.sum(p, axis=0, keepdims=True) + jnp.exp(sink - m)
            y4 = _dot_tn((p / denom).astype(BF16), vb).astype(BF16)
            lse4 = m + jnp.log(denom)
            for g in range(SWG):
                hq = hk * SWG + g
                y_ref[:, hq * HD:(hq + 1) * HD] = y4[g * WIN:(g + 1) * WIN]
                lse_ref[hq:hq + 1, :] = lse4[:, g * WIN:(g + 1) * WIN]

    return pl.pallas_call(
        body, name="swa_fwd", grid=(nb,),
        in_specs=[pl.BlockSpec((WIN, D), lambda n: (n, C_QSW // D)),
                  pl.BlockSpec((WIN, 512), lambda n: (n, _KVB)),
                  pl.BlockSpec((WIN, 512), lambda n: (jnp.maximum(n - 1, 0), _KVB)),
                  pl.BlockSpec((1, SWH), lambda n: (0, 0))],
        out_specs=[pl.BlockSpec((WIN, D), lambda n: (n, 0)), pl.BlockSpec((SWH, WIN), lambda n: (0, n))],
        out_shape=[jax.ShapeDtypeStruct((t, D), BF16), jax.ShapeDtypeStruct((SWH, t), F32)],
        compiler_params=_params(),
    )(proj, proj, proj, sinks)


def _swa_bwd(proj, sinks, lse, dyb, dproj):
    t = proj.shape[0]
    nb = t // WIN

    def body(q_ref, kvc_ref, kvp_ref, sink_ref, lse_ref, dy_ref, _, dq_ref, dself_ref, dprev_ref, ds_ref):
        @pl.when(pl.program_id(0) == 0)
        def _():
            ds_ref[...] = jnp.zeros_like(ds_ref)

        valid = _swa_mask(pl.program_id(0))
        for hk in range(SWKV):
            ks = slice(hk * HD, (hk + 1) * HD)
            vs = slice(SWKV * HD + hk * HD, SWKV * HD + (hk + 1) * HD)
            kb = jnp.concatenate([kvp_ref[:, ks], kvc_ref[:, ks]], axis=0).astype(BF16)
            vb = jnp.concatenate([kvp_ref[:, vs], kvc_ref[:, vs]], axis=0).astype(BF16)
            dy4 = _group_rows(dy_ref, hk)
            qb, dyb_ = _group_rows(q_ref, hk).astype(BF16), dy4.astype(BF16)
            lse4 = _group_lanes(lse_ref, hk)
            logits = jnp.where(valid, _dot_nt(kb, qb) * _SW_SCALE, -jnp.inf)
            p = jnp.exp(logits - lse4)
            dpt = _dot_nt(vb, dyb_)
            delta = jnp.sum(p * dpt, axis=0, keepdims=True)
            dsm = (p * (dpt - delta)).astype(BF16)
            dq4 = (_dot_tn(dsm, kb) * _SW_SCALE).astype(BF16)
            dkb = _dot_nn(dsm, qb) * _SW_SCALE
            dvb = _dot_nn(p.astype(BF16), dyb_)
            dsink4 = jnp.exp(_sink_lanes(sink_ref, hk) - lse4) * delta
            for g in range(SWG):
                hq = hk * SWG + g
                dq_ref[:, hq * HD:(hq + 1) * HD] = dq4[g * WIN:(g + 1) * WIN]
                ds_ref[:, hq:hq + 1] += -jnp.sum(dsink4[:, g * WIN:(g + 1) * WIN], axis=1, keepdims=True)
            dprev_ref[:, ks] = dkb[:WIN]
            dself_ref[:, ks] = dkb[WIN:]
            dprev_ref[:, vs] = dvb[:WIN]
            dself_ref[:, vs] = dvb[WIN:]

    return pl.pallas_call(
        body, name="swa_bwd", grid=(nb,),
        in_specs=[pl.BlockSpec((WIN, D), lambda n: (n, C_QSW // D)),
                  pl.BlockSpec((WIN, 512), lambda n: (n, _KVB)),
                  pl.BlockSpec((WIN, 512), lambda n: (jnp.maximum(n - 1, 0), _KVB)),
                  pl.BlockSpec((1, SWH), lambda n: (0, 0)),
                  pl.BlockSpec((SWH, WIN), lambda n: (0, n)),
                  pl.BlockSpec((WIN, D), lambda n: (n, 0)), _ANY],
        out_specs=[pl.BlockSpec((WIN, D), lambda n: (n, C_QSW // D)), pl.BlockSpec((WIN, 512), lambda n: (n, 0)),
                   pl.BlockSpec((WIN, 512), lambda n: (n, 0)), pl.BlockSpec((1, SWH), lambda n: (0, 0))],
        out_shape=[jax.ShapeDtypeStruct((t, NP), BF16), jax.ShapeDtypeStruct((t, 512), F32),
                   jax.ShapeDtypeStruct((t, 512), F32), jax.ShapeDtypeStruct((1, SWH), F32)],
        input_output_aliases={6: 0}, compiler_params=_params(),
    )(proj, proj, proj, sinks, lse, dyb, dproj)


def _kv_combine(dself, dprev, dif, dproj):
    t = dself.shape[0]
    nb = t // WIN

    def body(a_ref, b_ref, dif_ref, _, o_ref):
        nxt = jnp.where(pl.program_id(0) < nb - 1, b_ref[...], 0.0)
        o_ref[:, 0:512] = (a_ref[...] + nxt).astype(BF16)
        lane = lax.broadcasted_iota(jnp.int32, (WIN, 128), 1)
        dif_v = dif_ref[...]
        first = jnp.zeros((WIN, 128), F32)
        for col in range(8):
            first = first + jnp.where(lane == col, dif_v[:, col:col + 1], 0.0)
        o_ref[:, 512:640] = first.astype(BF16)
        o_ref[:, 640:512 + IFW] = jnp.zeros((WIN, IFW - 128), BF16)

    return pl.pallas_call(
        body, name="kv_combine", grid=(nb,),
        in_specs=[pl.BlockSpec((WIN, 512), lambda n: (n, 0)),
                  pl.BlockSpec((WIN, 512), lambda n: (jnp.minimum(n + 1, nb - 1), 0)),
                  pl.BlockSpec((WIN, 8), lambda n: (n, 0)), _ANY],
        out_specs=pl.BlockSpec((WIN, 512 + IFW), lambda n: (n, C_KV // (512 + IFW))),
        out_shape=jax.ShapeDtypeStruct((t, NP), BF16), input_output_aliases={3: 0}, compiler_params=_params(),
    )(dself, dprev, dif, dproj)


def _sds(t, n, dtype):
    return jax.ShapeDtypeStruct((t, n), dtype)


def _proj_in(h0, w_in):
    t = h0.shape[0]

    tn = 2 * IFW

    def epilogue(accs, ins, outs, i, j):
        outs[0][...] = accs[0].astype(BF16)

        @pl.when(j == C_IF // tn)
        def _():
            outs[1][...] = accs[0][:, C_IF % tn:]

    gate_cols = lambda tm, tn: pl.BlockSpec((tm, IFW), lambda i, j, kk: (i, 0))
    return _mm_ep([(h0, w_in)], "nn", "mm_in", epilogue, [],
                  [(_sds(t, NP, BF16), _tile()), (_sds(t, IFW, F32), gate_cols)], 1024, tn)


def _branch_merge(ya, yb, wa, wb, proj):
    t = ya.shape[0]

    def epilogue(accs, ins, outs, i, j):
        za, zb = accs
        merged = _sigmoid(ins[0][...].astype(F32)) * za + _sigmoid(ins[1][...].astype(F32)) * zb
        outs[0][...] = merged.astype(BF16)
        outs[1][...] = za.astype(BF16)
        outs[2][...] = zb.astype(BF16)

    return _mm_ep([(ya, wa), (yb, wb)], "nn", "mm_branch_merge", epilogue, [(proj, _tile(C_GA)), (proj, _tile(C_GB))],
                  [(_sds(t, D, BF16), _tile())] * 3, 1024, 512)


def _dmerged_bwd(dxb, w_out, proj, za, zb):
    t = dxb.shape[0]

    def epilogue(accs, ins, outs, i, j):
        dm = accs[0]
        sa, sb = _sigmoid(ins[0][...].astype(F32)), _sigmoid(ins[1][...].astype(F32))
        outs[0][...] = (dm * sa).astype(BF16)
        outs[1][...] = (dm * sb).astype(BF16)
        outs[2][:, 0:D] = (dm * ins[2][...].astype(F32) * sa * (1.0 - sa)).astype(BF16)
        outs[2][:, D:2 * D] = (dm * ins[3][...].astype(F32) * sb * (1.0 - sb)).astype(BF16)

    gate_cols = lambda tm, tn: pl.BlockSpec((tm, 2 * D), lambda i, j, kk: (i, C_GA // (2 * D)))
    return _mm_ep([(dxb, w_out)], "nt", "mm_dmerged_bwd", epilogue,
                  [(proj, _tile(C_GA)), (proj, _tile(C_GB)), (za, _tile()), (zb, _tile())],
                  [(_sds(t, D, BF16), _tile()), (_sds(t, D, BF16), _tile()), (_sds(t, NP, BF16), gate_cols)], 1024, D)


def _dya_bwd(dza, wa, hraw, proj, g, dproj):
    t = dza.shape[0]

    def epilogue(accs, ins, outs, i, j):
        h_ref, o_ref, g_ref, _ = ins
        dh_ref, do_ref, dg_ref = outs

        @pl.when(i == 0)
        def _():
            dg_ref[...] = jnp.zeros_like(dg_ref)

        dy = accs[0]
        so = _sigmoid(o_ref[...].astype(F32))
        for h in range(MLH):
            sl = slice(h * DV, (h + 1) * DV)
            xn, rstd = _rms(h_ref[:, sl])
            gs = g_ref[:, sl]
            do_ref[:, sl] = (dy[:, sl] * xn * gs * so[:, sl] * (1.0 - so[:, sl])).astype(BF16)
            dhn = dy[:, sl] * so[:, sl]
            dg_ref[:, sl] += jnp.sum(dhn * xn, axis=0, keepdims=True)
            dh_ref[:, sl] = _rms_bwd(xn, rstd, dhn * gs)

    return _mm_ep([(dza, wa)], "nt", "mm_dya_bwd", epilogue,
                  [(hraw, _tile()), (proj, _tile(C_O)), (g, _row()), (dproj, lambda tm, tn: _ANY)],
                  [(_sds(t, D, F32), _tile()), (_sds(t, NP, BF16), _tile(C_O)), (_sds(1, D, F32), _row())],
                  1024, D, aliases={3: 1})


def _up_act(hn, w_up):
    t = hn.shape[0]

    def epilogue(accs, ins, outs, i, j):
        r = jnp.maximum(accs[0], 0.0)
        outs[0][...] = (r * r).astype(BF16)
        outs[1][...] = accs[0].astype(BF16)

    return _mm_ep([(hn, w_up)], "nn", "mm_up_act", epilogue, [],
                  [(_sds(t, DFF, BF16), _tile()), (_sds(t, DFF, BF16), _tile())], 1024, 1024)


def _da_du(dxb, w_down, u):
    t = dxb.shape[0]

    def epilogue(accs, ins, outs, i, j):
        outs[0][...] = (accs[0] * 2.0 * jnp.maximum(ins[0][...].astype(F32), 0.0)).astype(BF16)

    return _mm_ep([(dxb, w_down)], "nt", "mm_da_du", epilogue, [(u, _tile())], [(_sds(t, DFF, BF16), _tile())],
                  1024, 1024)[0]


def _resid_norm_mm(a, w, x, g, name):
    t = x.shape[0]

    def epilogue(accs, ins, outs, i, j):
        x1 = ins[0][...] + accs[0]
        outs[0][...] = x1
        xn, _ = _rms(x1)
        outs[1][...] = (xn * ins[1][...]).astype(BF16)

    return _mm_ep([(a, w)], "nn", name, epilogue, [(x, _tile()), (g, _row())],
                  [(_sds(t, D, F32), _tile()), (_sds(t, D, BF16), _tile())], 1024, D)


def _norm_bwd_mm(dy, w, x, g, dres, name):
    t = x.shape[0]

    def epilogue(accs, ins, outs, i, j):
        @pl.when(i == 0)
        def _():
            outs[2][...] = jnp.zeros_like(outs[2])

        dh = accs[0]
        xn, rstd = _rms(ins[0][...])
        outs[2][...] += jnp.sum(dh * xn, axis=0, keepdims=True)
        dx = ins[2][...] + _rms_bwd(xn, rstd, dh * ins[1][...])
        outs[0][...] = dx
        outs[1][...] = dx.astype(BF16)

    return _mm_ep([(dy, w)], "nt", name, epilogue, [(x, _tile()), (g, _row()), (dres, _tile())],
                  [(_sds(t, D, F32), _tile()), (_sds(t, D, BF16), _tile()), (_sds(1, D, F32), _row())], 1024, D)


def _ple_final_mm(hn2, w_gate, x2, pp, target, gf):
    t = x2.shape[0]

    def epilogue(accs, ins, outs, i, j):
        loss_ref, dg_ref, dx_ref, dpp_ref, dgp_ref = outs

        @pl.when(i == 0)
        def _():
            loss_ref[...] = jnp.zeros_like(loss_ref)
            dg_ref[...] = jnp.zeros_like(dg_ref)

        gate = _sigmoid(accs[0])
        pp_v = ins[1][...]
        x3 = ins[0][...] + gate * pp_v
        xn, rstd = _rms(x3)
        gf_v = ins[3][...]
        err = xn * gf_v - ins[2][...]
        loss_ref[...] += (0.5 / D) * jnp.sum(jnp.sum(err * err, axis=1, keepdims=True), axis=0, keepdims=True)
        dy = err * (1.0 / D)
        dg_ref[...] += jnp.sum(dy * xn, axis=0, keepdims=True)
        dx3 = _rms_bwd(xn, rstd, dy * gf_v)
        dx_ref[...] = dx3
        dpp_ref[...] = (dx3 * gate).astype(BF16)
        dgp_ref[...] = (dx3 * pp_v * gate * (1.0 - gate)).astype(BF16)

    one = lambda tm, tn: pl.BlockSpec((1, 1), lambda i, j, kk: (0, 0))
    return _mm_ep([(hn2, w_gate)], "nn", "mm_ple_final", epilogue,
                  [(x2, _tile()), (pp, _tile()), (target, _tile()), (gf, _row())],
                  [(_sds(1, 1, F32), one), (_sds(1, D, F32), _row()), (_sds(t, D, F32), _tile()),
                   (_sds(t, D, BF16), _tile()), (_sds(t, D, BF16), _tile())], 512, D)


def _win_pad(w):
    zeros = jnp.zeros((w.shape[0], IFW - 8), w.dtype)
    return jnp.concatenate([w[:, 0:3072], w[:, 3080:4104], w[:, 4616:6664], w[:, 4104:4616], w[:, 3072:3080], zeros],
                           axis=1)


def _win_unpad(wp):
    return jnp.concatenate([wp[:, 0:3072], wp[:, C_IF:C_IF + 8], wp[:, C_QSW:C_QSW + 1024], wp[:, C_KV:C_KV + 512],
                            wp[:, C_GA:C_GA + 2048]], axis=1)


def _local_step(x, p, target, w, late_weights=None, early_grads=None, last_grad=None):
    t = x.shape[0]
    pb = p.astype(BF16)
    w = dict(w)

    h0 = _norm_fwd(x, w["norm_mix_g"], "norm_mix")
    proj, gates = _proj_in(h0, w["w_in"])
    qk = _conv_silu_fwd(proj, w["conv_qk"])
    grow, sneg_row = _gates_fwd(gates[:, 0:8].T, w["b_if"].reshape(8, 1))
    gcol, sneg_col = grow.T, sneg_row.T
    hraw, cs, st = _mlstm_fwd(qk, proj, grow, gcol)
    ya = _ya_fwd(hraw, proj, w["mlstm_norm_g"])
    yb, lse = _swa_fwd(proj, w["sinks"])
    if late_weights is not None:
        w.update(late_weights(yb))
    merged, za, zb = _branch_merge(ya, yb, w["w_branch_a"], w["w_branch_b"], proj)
    x1, hn1 = _resid_norm_mm(merged, w["w_out"], x, w["norm_mlp_g"], "mm_out_norm")
    act, u = _up_act(hn1, w["w_up"])
    x2, hn2 = _resid_norm_mm(act, w["w_down"], x1, w["norm_ple_g"], "mm_down_norm")
    pp = _mm(pb, w["w_ple_proj"], "nn", F32, "mm_ple_proj")
    loss, d_final_g, dx3, dpp, dgpre = _ple_final_mm(hn2, w["w_ple_gate"], x2, pp, target, w["final_norm_g"])

    g = {"final_norm_g": d_final_g}
    g["w_ple_proj"] = _mm(pb, dpp, "tn", F32, "mm_d_ple_proj", out_chunks=4)
    g["w_ple_gate"] = _mm(hn2, dgpre, "tn", F32, "mm_d_ple_gate")
    dx2, dx2b, g["norm_ple_g"] = _norm_bwd_mm(dgpre, w["w_ple_gate"], x2, w["norm_ple_g"], dx3, "mm_dhn2_norm")
    g["w_down"] = _mm(act, dx2b, "tn", F32, "mm_d_down")
    du = _da_du(dx2b, w["w_down"], u)
    g["w_up"] = _mm(hn1, du, "tn", F32, "mm_d_up", out_chunks=4)
    dx1, dx1b, g["norm_mlp_g"] = _norm_bwd_mm(du, w["w_up"], x1, w["norm_mlp_g"], dx2, "mm_dhn1_norm")
    g["w_out"] = _mm(merged, dx1b, "tn", F32, "mm_d_out")
    dza, dzb, dproj = _dmerged_bwd(dx1b, w["w_out"], proj, za, zb)
    g["w_branch_a"] = _mm(ya, dza, "tn", F32, "mm_d_branch_a")
    g["w_branch_b"] = _mm(yb, dzb, "tn", F32, "mm_d_branch_b")
    gain = w["mlstm_norm_g"] if early_grads is None else w["mlstm_norm_g"] + early_grads(g)
    dyb = _mm(dzb, w["w_branch_b"], "nt", F32, "mm_dyb")
    dhraw, dproj, g["mlstm_norm_g"] = _dya_bwd(dza, w["w_branch_a"], hraw, proj, gain, dproj)
    dqk, dproj, dif, g["b_if"] = _mlstm_bwd(qk, proj, grow, gcol, sneg_col, cs, st, hraw, dhraw, dproj)
    dc, g["conv_qk"] = _conv_silu_bwd_a(proj, w["conv_qk"], dqk)
    dproj = _conv_silu_bwd_b(dc, w["conv_qk"], dproj)
    dproj, dkv_self, dkv_prev, g["sinks"] = _swa_bwd(proj, w["sinks"], lse, dyb, dproj)
    dproj = _kv_combine(dkv_self, dkv_prev, dif, dproj)
    g["w_in"] = _mm(h0, dproj, "tn", F32, "mm_d_in")
    gain = w["norm_mix_g"] if last_grad is None else w["norm_mix_g"] + last_grad(g)
    grad_x, _, g["norm_mix_g"] = _norm_bwd_mm(dproj, w["w_in"], x, gain, dx1, "mm_dh0_norm")
    return loss, grad_x, g


_W4 = ("w_branch_a", "w_branch_b", "w_out", "w_ple_gate")
_SHARDED_NAMES = ("w_in", "w_up", "w_down", "w_ple_proj", "conv_qk") + _W4
_SMALL_ROWS = 16
_CONV_ROW = 8


def _group(s):
    return [s["w_in"], jnp.concatenate([s[n] for n in _W4], axis=0), s["w_up"], s["w_down"], s["w_ple_proj"]]


def _ungroup(arrs):
    out = {"w_in": arrs[0], "w_up": arrs[2], "w_down": arrs[3], "w_ple_proj": arrs[4]}
    rows = arrs[1].shape[0] // len(_W4)
    for i, n in enumerate(_W4):
        out[n] = arrs[1][i * rows:(i + 1) * rows]
    return out


def _rows_tile(rows):
    return 256 if rows % 256 == 0 else rows


_SMALL = ("norm_mix_g", "mlstm_norm_g", "norm_mlp_g", "norm_ple_g", "final_norm_g")


def _pack_small(vals, extra=None, conv=None):
    rows = [vals[n].reshape(1, D) for n in _SMALL]
    tail = [vals["b_if"].reshape(1, 8), vals["sinks"].reshape(1, SWH)]
    used = 8 + SWH
    if extra is not None:
        tail.append(extra.reshape(1, 1))
        used += 1
    tail.append(jnp.zeros((1, D - used), F32))
    rows.append(jnp.concatenate(tail, axis=1))
    rows.append(jnp.zeros((_CONV_ROW - len(rows), D), F32))
    rows.append(jnp.zeros((CONV, D), F32) if conv is None else conv)
    rows.append(jnp.zeros((_SMALL_ROWS - _CONV_ROW - CONV, D), F32))
    return jnp.concatenate(rows, axis=0)


def _unpack_small(slab, shapes):
    out = {n: slab[i].reshape(shapes[n]) for i, n in enumerate(_SMALL)}
    out["b_if"] = slab[5, 0:8].reshape(shapes["b_if"])
    out["sinks"] = slab[5, 8:8 + SWH].reshape(shapes["sinks"])
    return out


_MESH = pl.DeviceIdType.MESH
_HBM = pl.BlockSpec(memory_space=pltpu.HBM)
_VMEM = pl.BlockSpec(memory_space=pltpu.VMEM)


def _place():
    x, y, c = lax.axis_index("x"), lax.axis_index("y"), lax.axis_index("c")
    return x, y, c, 2 * x + y


def _chip_peer(x, y, r):
    return (x ^ (r >> 1), y ^ (r & 1))


def _half(ref, which):
    h = ref.shape[-2] // 2
    return pl.ds(which * h, h)


def _allgather_weights(shards, conv):
    n = len(shards)

    def body(*refs):
        ins, conv_ref = refs[:n], refs[n]
        outs, conv_out = refs[n + 1:2 * n + 1], refs[2 * n + 1]
        send_a, recv_a, send_b, recv_b, send_c, recv_c, local_sems = refs[2 * n + 2:]
        x, y, c, j = _place()
        sibling = (x, y, 1 - c)
        local = [pltpu.make_async_copy(ins[k], outs[k].at[j], local_sems.at[k]) for k in range(n)]
        local.append(pltpu.make_async_copy(conv_ref, conv_out.at[j], local_sems.at[n]))
        for cp in local:
            cp.start()

        def copy_a(k, r, chip):
            rows = _half(ins[k], c)
            return pltpu.make_async_remote_copy(
                src_ref=ins[k].at[rows], dst_ref=outs[k].at[chip, rows], send_sem=send_a.at[3 * k + r - 1],
                recv_sem=recv_a.at[3 * k + r - 1], device_id=(*_chip_peer(x, y, r), c), device_id_type=_MESH)

        def copy_b(k, r, chip, which):
            rows = _half(ins[k], which)
            return pltpu.make_async_remote_copy(
                src_ref=outs[k].at[chip, rows], dst_ref=outs[k].at[chip, rows], send_sem=send_b.at[3 * k + r - 1],
                recv_sem=recv_b.at[3 * k + r - 1], device_id=sibling, device_id_type=_MESH)

        def copy_c(r, chip):
            return pltpu.make_async_remote_copy(
                src_ref=conv_ref, dst_ref=conv_out.at[chip], send_sem=send_c.at[r - 1],
                recv_sem=recv_c.at[r - 1], device_id=(*_chip_peer(x, y, r), c), device_id_type=_MESH)

        for k in range(n):
            for r in (1, 2, 3):
                copy_a(k, r, j).start()
        for r in (1, 2, 3):
            copy_c(r, j).start()
        for k in range(n):
            for r in (1, 2, 3):
                copy_a(k, r, j ^ r).wait_recv()
                copy_b(k, r, j ^ r, c).start()
        for k in range(n):
            for r in (1, 2, 3):
                copy_b(k, r, j ^ r, 1 - c).wait_recv()
        for r in (1, 2, 3):
            copy_c(r, j ^ r).wait_recv()
        for k in range(n):
            for r in (1, 2, 3):
                copy_a(k, r, j).wait_send()
                copy_b(k, r, j ^ r, c).wait_send()
        for r in (1, 2, 3):
            copy_c(r, j).wait_send()
        for cp in local:
            cp.wait()

    return pl.pallas_call(
        body, name="allgather_weights",
        out_shape=[jax.ShapeDtypeStruct((4,) + s.shape, s.dtype) for s in shards]
        + [jax.ShapeDtypeStruct((4,) + conv.shape, F32)],
        in_specs=[_HBM] * (n + 1), out_specs=[_HBM] * (n + 1),
        scratch_shapes=[pltpu.SemaphoreType.DMA((3 * n,))] * 4 + [pltpu.SemaphoreType.DMA((3,))] * 2
        + [pltpu.SemaphoreType.DMA((n + 1,))],
    )(*shards, conv)


_SEM = pl.BlockSpec(memory_space=pltpu.SEMAPHORE)
_DATAFLOW = pltpu.SideEffectType.DATAFLOW_SIDE_EFFECTING


def _late_peer_copy(src_ref, land_ref, send_sems, recv_sems, x, y, c, j, r, chip):
    return pltpu.make_async_remote_copy(
        src_ref=src_ref, dst_ref=land_ref.at[chip], send_sem=send_sems.at[r - 1], recv_sem=recv_sems.at[r - 1],
        device_id=(*_chip_peer(x, y, r), c), device_id_type=_MESH)


def _late_gather_start(rest):
    def body(rest_ref, land_ref, send_sems, recv_sems, rest_thru, land_thru, token):
        x, y, c, j = _place()
        for r in (1, 2, 3):
            _late_peer_copy(rest_ref, land_ref, send_sems, recv_sems, x, y, c, j, r, j).start()
        token[...] = jnp.zeros_like(token)

    j = 2 * lax.axis_index("x") + lax.axis_index("y")
    land = lax.dynamic_update_slice(lax.empty((4,) + rest.shape, rest.dtype), rest[None], (j, 0, 0))
    return pl.pallas_call(
        body, name="late_gather_start",
        out_shape=(pltpu.SemaphoreType.DMA((3,)), pltpu.SemaphoreType.DMA((3,)), pltpu.HBM(rest.shape, rest.dtype),
                   pltpu.HBM(land.shape, land.dtype), jax.ShapeDtypeStruct((8, 128), F32)),
        in_specs=(_HBM, _HBM), out_specs=(_SEM, _SEM, _HBM, _HBM, _VMEM), input_output_aliases={0: 2, 1: 3},
        compiler_params=pltpu.CompilerParams(has_side_effects=_DATAFLOW),
    )(pltpu.with_memory_space_constraint(rest, pltpu.HBM), pltpu.with_memory_space_constraint(land, pltpu.HBM))


def _late_gather_wait(send_sems, recv_sems, rest_thru, land_thru, after):
    def body(rest_ref, land_ref, send_sems, recv_sems, after_ref, rest_dead, got_ref):
        x, y, c, j = _place()
        for r in (1, 2, 3):
            cp = _late_peer_copy(rest_ref, land_ref, send_sems, recv_sems, x, y, c, j, r, j ^ r)
            cp.wait_send()
            cp.wait_recv()

    return pl.pallas_call(
        body, name="late_gather_wait",
        out_shape=(pltpu.HBM(rest_thru.shape, rest_thru.dtype), pltpu.HBM(land_thru.shape, land_thru.dtype)),
        in_specs=(_HBM, _HBM, _SEM, _SEM, _ANY), out_specs=(_HBM, _HBM), input_output_aliases={0: 0, 1: 1},
        compiler_params=pltpu.CompilerParams(has_side_effects=_DATAFLOW),
    )(rest_thru, land_thru, send_sems, recv_sems, after)[1]


def _pair_exchange(gs, name):
    n = len(gs)

    def body(*refs):
        ins, outs, send_sems, recv_sems = refs[:n], refs[n:2 * n], refs[2 * n], refs[2 * n + 1]
        x, y, c, _ = _place()
        cps = [pltpu.make_async_remote_copy(
            src_ref=ins[k].at[:, _half(ins[k], 1 - c)], dst_ref=outs[k], send_sem=send_sems.at[k],
            recv_sem=recv_sems.at[k], device_id=(x, y, 1 - c), device_id_type=_MESH) for k in range(n)]
        for cp in cps:
            cp.start()
        for cp in cps:
            cp.wait()

    return pl.pallas_call(
        body, name=name,
        out_shape=[jax.ShapeDtypeStruct((4, g.shape[1] // 2, g.shape[2]), F32) for g in gs],
        in_specs=[_HBM] * n, out_specs=[_HBM] * n, scratch_shapes=[pltpu.SemaphoreType.DMA((n,))] * 2,
    )(*gs)


def _pair_sum(g, theirs, c, name):
    _, h, cols = theirs.shape
    tr = _rows_tile(h)
    nb = h // tr

    def body(c_ref, a_ref, b_ref, o_ref, ob_ref):
        s = a_ref[...] + b_ref[...]
        o_ref[...] = s
        ob_ref[...] = s.astype(BF16)

    blk = pl.BlockSpec((1, tr, cols), lambda k, i, c_ref: (k, i, 0))
    return pl.pallas_call(
        body, name=name,
        grid_spec=pltpu.PrefetchScalarGridSpec(
            num_scalar_prefetch=1, grid=(4, nb),
            in_specs=[pl.BlockSpec((1, tr, cols), lambda k, i, c_ref: (k, c_ref[0] * nb + i, 0)), blk],
            out_specs=[blk, blk]),
        out_shape=[jax.ShapeDtypeStruct(theirs.shape, F32), jax.ShapeDtypeStruct(theirs.shape, BF16)],
        compiler_params=_params(),
    )(c.reshape(1).astype(jnp.int32), g, theirs)


def _chip_copies(srcs, lands, send_sems, recv_sems):
    x, y, c, j = _place()
    return [pltpu.make_async_remote_copy(
        src_ref=srcs[k].at[j ^ r], dst_ref=lands[k].at[r - 1], send_sem=send_sems.at[3 * k + r - 1],
        recv_sem=recv_sems.at[3 * k + r - 1], device_id=(*_chip_peer(x, y, r), c), device_id_type=_MESH)
        for k in range(len(srcs)) for r in (1, 2, 3)]


def _chip_exchange_start(ss, tag):
    n = len(ss)

    def body(*refs):
        srcs, lands, send_sems, recv_sems, token = refs[:n], refs[n:2 * n], refs[2 * n], refs[2 * n + 1], refs[-1]
        for cp in _chip_copies(srcs, lands, send_sems, recv_sems):
            cp.start()
        token[...] = jnp.zeros_like(token)

    lands = [lax.empty((3,) + s.shape[1:], s.dtype) for s in ss]
    hbm = [pltpu.HBM(a.shape, a.dtype) for a in list(ss) + lands]
    out = pl.pallas_call(
        body, name="chip_exchange_start_" + tag,
        out_shape=(pltpu.SemaphoreType.DMA((3 * n,)), pltpu.SemaphoreType.DMA((3 * n,)), *hbm,
                   jax.ShapeDtypeStruct((8, 128), F32)),
        in_specs=[_HBM] * (2 * n), out_specs=(_SEM, _SEM, *([_HBM] * (2 * n)), _VMEM),
        input_output_aliases={k: 2 + k for k in range(2 * n)},
        compiler_params=pltpu.CompilerParams(has_side_effects=_DATAFLOW),
    )(*[pltpu.with_memory_space_constraint(a, pltpu.HBM) for a in list(ss) + lands])
    return out[0], out[1], list(out[2:2 + n]), list(out[2 + n:2 + 2 * n]), out[-1]


def _chip_exchange_wait(send_sems, recv_sems, ss_thru, lands_thru, after, tag):
    n = len(ss_thru)

    def body(*refs):
        srcs, lands, send_sems, recv_sems = refs[:n], refs[n:2 * n], refs[2 * n], refs[2 * n + 1]
        for cp in _chip_copies(srcs, lands, send_sems, recv_sems):
            cp.wait_send()
            cp.wait_recv()

    hbm = [pltpu.HBM(a.shape, a.dtype) for a in list(ss_thru) + list(lands_thru)]
    out = pl.pallas_call(
        body, name="chip_exchange_wait_" + tag, out_shape=tuple(hbm),
        in_specs=[_HBM] * (2 * n) + [_SEM, _SEM, _ANY], out_specs=tuple([_HBM] * (2 * n)),
        input_output_aliases={k: k for k in range(2 * n)},
        compiler_params=pltpu.CompilerParams(has_side_effects=_DATAFLOW),
    )(*ss_thru, *lands_thru, send_sems, recv_sems, after)
    return list(out[n:])


def _reduce4(own, others, j, c, name):
    _, h, cols = own.shape
    tr = _rows_tile(h)
    nb = h // tr

    def body(idx_ref, s_ref, a0, a1, a2, o_ref):
        o_ref[...] = ((s_ref[0] + a0[0].astype(F32)) + a1[0].astype(F32)) + a2[0].astype(F32)

    def other(r):
        return pl.BlockSpec((1, tr, cols), lambda i, idx_ref: (r, i, 0))

    return pl.pallas_call(
        body, name=name,
        grid_spec=pltpu.PrefetchScalarGridSpec(
            num_scalar_prefetch=1, grid=(nb,),
            in_specs=[pl.BlockSpec((1, tr, cols), lambda i, idx_ref: (idx_ref[0], i, 0)), other(0), other(1), other(2)],
            out_specs=pl.BlockSpec((tr, cols), lambda i, idx_ref: (idx_ref[1] * nb + i, 0))),
        out_shape=jax.ShapeDtypeStruct((2 * h, cols), F32), compiler_params=_params(),
    )(jnp.stack([j, c]).astype(jnp.int32), own, others, others, others)


def _sibling_share(fulls):
    n = len(fulls)

    def body(*refs):
        outs, send_sems, recv_sems = refs[n:2 * n], refs[2 * n], refs[2 * n + 1]
        x, y, c, _ = _place()
        cps = [pltpu.make_async_remote_copy(
            src_ref=outs[k].at[_half(outs[k], c)], dst_ref=outs[k].at[_half(outs[k], c)], send_sem=send_sems.at[k],
            recv_sem=recv_sems.at[k], device_id=(x, y, 1 - c), device_id_type=_MESH) for k in range(n)]
        for cp in cps:
            cp.start()
        for cp in cps:
            cp.wait()

    return pl.pallas_call(
        body, name="sibling_share", out_shape=[jax.ShapeDtypeStruct(f.shape, F32) for f in fulls],
        in_specs=[_HBM] * n, out_specs=[_HBM] * n, input_output_aliases={k: k for k in range(n)},
        scratch_shapes=[pltpu.SemaphoreType.DMA((n,))] * 2,
    )(*fulls)


def _adamw(w, g, m, v):
    m1 = ADAM_B1 * m + (1.0 - ADAM_B1) * g
    v1 = ADAM_B2 * v + (1.0 - ADAM_B2) * (g * g)
    m_hat = m1 / (1.0 - ADAM_B1 ** ADAM_STEP)
    v_hat = v1 / (1.0 - ADAM_B2 ** ADAM_STEP)
    delta = -ADAM_LR * (m_hat / (jnp.sqrt(v_hat) + ADAM_EPS) + ADAM_WD * w)
    return delta, m1, v1


def _adamw_call(w, g, m, v, name):
    rows, cols = w.shape
    tr = _rows_tile(rows)

    def body(w_ref, g_ref, m_ref, v_ref, d_out, m_out, v_out):
        delta, m1, v1 = _adamw(w_ref[...], g_ref[...], m_ref[...], v_ref[...])
        d_out[...] = delta
        m_out[...] = m1
        v_out[...] = v1

    blk = pl.BlockSpec((tr, cols), lambda i: (i, 0))
    return pl.pallas_call(
        body, name=name, grid=(rows // tr,), in_specs=[blk] * 4, out_specs=[blk] * 3,
        out_shape=[jax.ShapeDtypeStruct((rows, cols), F32)] * 3, compiler_params=_params(),
    )(w, g, m, v)


def _small_allreduce(vals):
    def body(v_ref, out_ref, buf, send_sems, recv_sems):
        x, y, c, j = _place()
        me = 2 * j + c
        buf[0] = v_ref[...]

        def copy(r):
            return pltpu.make_async_remote_copy(
                src_ref=v_ref, dst_ref=buf.at[r], send_sem=send_sems.at[r - 1], recv_sem=recv_sems.at[r - 1],
                device_id=(x ^ (r >> 2), y ^ ((r >> 1) & 1), c ^ (r & 1)), device_id_type=_MESH)

        for r in range(1, 8):
            copy(r).start()
        for r in range(1, 8):
            copy(r).wait()
        acc = buf[me ^ 0]
        for d in range(1, 8):
            acc = acc + buf[me ^ d]
        out_ref[...] = acc

    return pl.pallas_call(
        body, name="small_allreduce", out_shape=jax.ShapeDtypeStruct((_SMALL_ROWS, D), F32),
        in_specs=[_VMEM], out_specs=_VMEM,
        scratch_shapes=[pltpu.VMEM((8, _SMALL_ROWS, D), F32), pltpu.SemaphoreType.DMA((7,)),
                        pltpu.SemaphoreType.DMA((7,))],
    )(vals)


_NAMES = ("norm_mix_g", "w_in", "conv_qk", "b_if", "mlstm_norm_g", "sinks", "w_branch_a", "w_branch_b", "w_out",
          "norm_mlp_g", "w_up", "w_down", "norm_ple_g", "w_ple_gate", "w_ple_proj", "final_norm_g")
_GROUP_NAMES = ("w_in", "w4", "w_up", "w_down", "w_ple_proj")


def _step(x, p, target, w, m, v):
    c = lax.axis_index("c")
    j = 2 * lax.axis_index("x") + lax.axis_index("y")

    def shards(d):
        return {n: d[n][0] for n in _SHARDED_NAMES}

    ws = shards(w)
    w_in_all, conv_all = _allgather_weights([ws["w_in"].astype(BF16)], ws["conv_qk"])
    rows_pp = PLE * (D // 4) // D
    rest = jnp.concatenate([ws[n] for n in _W4] + [ws["w_up"], ws["w_down"], ws["w_ple_proj"].reshape(rows_pp, D)],
                           axis=0)
    rest = (rest + 0.0 * conv_all[0, 0, 0]).astype(BF16)
    send_sems, recv_sems, rest_thru, land_thru, token = _late_gather_start(rest)
    full = {n: w[n] for n in ("mlstm_norm_g", "norm_mlp_g", "norm_ple_g", "b_if", "sinks")}
    full["norm_mix_g"] = w["norm_mix_g"] + token[0, 0]
    full["final_norm_g"] = w["final_norm_g"].reshape(1, D)
    full["w_in"] = _win_pad(jnp.swapaxes(w_in_all, 0, 1).reshape(D, N_IN))
    full["conv_qk"] = jnp.swapaxes(conv_all, 0, 1).reshape(CONV, D)

    def late_weights(after):
        land = _late_gather_wait(send_sems, recv_sems, rest_thru, land_thru, after)
        out = {n: land[:, i * (D // 4):(i + 1) * (D // 4)].reshape(D, D) for i, n in enumerate(_W4)}
        out["w_up"] = land[:, D:2 * D]
        out["w_down"] = land[:, 2 * D:3 * D].reshape(DFF, D)
        out["w_ple_proj"] = land[:, 3 * D:3 * D + rows_pp].reshape(4, PLE, D // 4)
        return out

    def pair_sums(by_dest, names, tag):
        theirs = _pair_exchange(by_dest, "pair_exchange_" + tag)
        return [_pair_sum(a, b, c, "pair_sum_" + n) for a, b, n in zip(by_dest, theirs, names)]

    early, last = {}, {}

    def early_grads(g):
        by_dest = [jnp.stack([g[n].reshape(4, D // 4, D) for n in _W4], axis=1).reshape(4, D, D),
                   g["w_up"], g["w_down"].reshape(4, DFF // 4, D), g["w_ple_proj"]]
        early["sums"] = pair_sums(by_dest, _GROUP_NAMES[1:], "early")
        *early["flight"], token = _chip_exchange_start([s[1] for s in early["sums"]], "early")
        return token[0, 0]

    def last_grad(g):
        w_in_g = _win_unpad(g["w_in"])
        last["sums"] = pair_sums([jnp.swapaxes(w_in_g.reshape(D, 4, N_IN // 4), 0, 1)], _GROUP_NAMES[:1], "w_in")
        *last["flight"], token = _chip_exchange_start([s[1] for s in last["sums"]], "w_in")
        return token[0, 0]

    loss, grad_x, g = _local_step(x[0], p[0, 0], target[0], full, late_weights, early_grads, last_grad)

    others = _chip_exchange_wait(*last["flight"], grad_x, "w_in")
    others += _chip_exchange_wait(*early["flight"], others[0], "early")
    sums = last["sums"] + early["sums"]
    halves = [_reduce4(s[0], b, j, c, "reduce4_" + n) for s, b, n in zip(sums, others, _GROUP_NAMES)]
    grads = _sibling_share(halves)

    small_g = _small_allreduce(_pack_small(g, extra=loss, conv=g["conv_qk"]))
    conv_g = lax.dynamic_slice(small_g[_CONV_ROW:_CONV_ROW + CONV], (0, j * (D // 4)), (CONV, D // 4))

    ms, vs = shards(m), shards(v)
    upd = [_adamw_call(wa, ga, ma, va, "adamw_" + n)
           for wa, ga, ma, va, n in zip(_group(ws), grads, _group(ms), _group(vs), _GROUP_NAMES)]
    conv_upd = _adamw_call(ws["conv_qk"], conv_g, ms["conv_qk"], vs["conv_qk"], "adamw_conv")
    small_upd = _adamw_call(_pack_small(w), small_g, _pack_small(m), _pack_small(v), "adamw_small")

    shapes = {n: w[n].shape for n in _NAMES}
    res = []
    for k in range(4):
        big = _ungroup(list(grads) if k == 0 else [u[k - 1] for u in upd])
        big["conv_qk"] = conv_g if k == 0 else conv_upd[k - 1]
        leaves = _unpack_small(small_g if k == 0 else small_upd[k - 1], shapes)
        leaves.update({n: a.reshape(shapes[n]) for n, a in big.items()})
        res.append(leaves)

    out = [small_g[5, 8 + SWH], grad_x[None]]
    for k in range(4):
        out += [res[k][n] for n in _NAMES]
    return tuple(out)


def kernel(x, p, norm_mix_g, w_in, conv_qk, b_if, mlstm_norm_g, sinks, w_branch_a, w_branch_b, w_out, norm_mlp_g, w_up, w_down, norm_ple_g, w_ple_gate, w_ple_proj, final_norm_g, loss_target, m_norm_mix_g, m_w_in, m_conv_qk, m_b_if, m_mlstm_norm_g, m_sinks, m_w_branch_a, m_w_branch_b, m_w_out, m_norm_mlp_g, m_w_up, m_w_down, m_norm_ple_g, m_w_ple_gate, m_w_ple_proj, m_final_norm_g, v_norm_mix_g, v_w_in, v_conv_qk, v_b_if, v_mlstm_norm_g, v_sinks, v_w_branch_a, v_w_branch_b, v_w_out, v_norm_mlp_g, v_w_up, v_w_down, v_norm_ple_g, v_w_ple_gate, v_w_ple_proj, v_final_norm_g):
    w = dict(zip(_NAMES, (norm_mix_g, w_in, conv_qk, b_if, mlstm_norm_g, sinks, w_branch_a, w_branch_b, w_out,
                          norm_mlp_g, w_up, w_down, norm_ple_g, w_ple_gate, w_ple_proj, final_norm_g)))
    m = dict(zip(_NAMES, (m_norm_mix_g, m_w_in, m_conv_qk, m_b_if, m_mlstm_norm_g, m_sinks, m_w_branch_a,
                          m_w_branch_b, m_w_out, m_norm_mlp_g, m_w_up, m_w_down, m_norm_ple_g, m_w_ple_gate,
                          m_w_ple_proj, m_final_norm_g)))
    v = dict(zip(_NAMES, (v_norm_mix_g, v_w_in, v_conv_qk, v_b_if, v_mlstm_norm_g, v_sinks, v_w_branch_a,
                          v_w_branch_b, v_w_out, v_norm_mlp_g, v_w_up, v_w_down, v_norm_ple_g, v_w_ple_gate,
                          v_w_ple_proj, v_final_norm_g)))
    return _step(x, p, loss_target, w, m, v)
```

```python
import jax
import jax.numpy as jnp
from jax import lax
from jax.experimental import pallas as pl
from jax.experimental.pallas import tpu as pltpu

F32 = jnp.float32
BF16 = jnp.bfloat16

D = 1024
PLE = 256
MLH = 4
DQK = 128
DV = 256
CONV = 4
CHUNK = 128
SWH = 16
SWKV = 4
SWG = SWH // SWKV
HD = 64
WIN = 128
DFF = 4096
EPS = 1e-6
N_IN = 6664
NP = 7168
C_QK, C_V, C_O, C_QSW, C_GA, C_GB, C_KV, C_IF = 0, 1024, 2048, 3072, 4096, 5120, 6144, 6656
IFW = NP - C_IF

ADAM_LR = 0.001
ADAM_B1 = 0.9
ADAM_B2 = 0.999
ADAM_EPS = 1e-08
ADAM_WD = 0.01
ADAM_STEP = 10

TOK_TILE = 512
VMEM_LIMIT = 58 * 1024 * 1024


def _params(**kw):
    return pltpu.CompilerParams(vmem_limit_bytes=VMEM_LIMIT, **kw)


def _pick(n, cap):
    if n <= cap:
        return n
    t = cap - cap % 128
    while t > 128 and n % t:
        t -= 128
    assert n % t == 0, (n, cap)
    return t


def _dot(a, b, dims):
    return lax.dot_general(a, b, (dims, ((), ())), preferred_element_type=F32)


def _dot_nn(a, b):
    return _dot(a, b, ((1,), (0,)))


def _dot_nt(a, b):
    return _dot(a, b, ((1,), (1,)))


def _dot_tn(a, b):
    return _dot(a, b, ((0,), (0,)))


def _sigmoid(x):
    return 1.0 / (1.0 + jnp.exp(-x))


def _mm(a, b, mode, out_dtype, name, out_chunks=1):
    bch = b.shape[0] if b.ndim == 3 else 1
    brows, bcols = b.shape[-2], b.shape[-1] * bch
    if mode == "nn":
        (m, k), (k2, n) = a.shape, (brows, bcols)
    elif mode == "nt":
        (m, k), (n, k2) = a.shape, (brows, bcols)
    else:
        (k, m), (k2, n) = a.shape, (brows, bcols)
    assert k == k2, (a.shape, b.shape, mode)
    n_cap = n // max(out_chunks, 1 if mode == "nt" else bch)
    k_cap = k // bch if mode == "nt" else k
    tm, tn, tk = _pick(m, 1024), _pick(n_cap, 1024), _pick(k_cap, 2048)
    nk = k // tk
    if mode == "nn":
        a_spec = pl.BlockSpec((tm, tk), lambda i, j, kk: (i, kk))
        if bch > 1:
            bpc = (n // bch) // tn
            b_spec = pl.BlockSpec((None, tk, tn), lambda i, j, kk: (j // bpc, kk, j % bpc))
        else:
            b_spec = pl.BlockSpec((tk, tn), lambda i, j, kk: (kk, j))
        dot = _dot_nn
    elif mode == "nt":
        a_spec = pl.BlockSpec((tm, tk), lambda i, j, kk: (i, kk))
        if bch > 1:
            bpc = (k // bch) // tk
            b_spec = pl.BlockSpec((None, tn, tk), lambda i, j, kk: (kk // bpc, j, kk % bpc))
        else:
            b_spec = pl.BlockSpec((tn, tk), lambda i, j, kk: (j, kk))
        dot = _dot_nt
    else:
        assert bch == 1
        a_spec = pl.BlockSpec((tk, tm), lambda i, j, kk: (kk, i))
        b_spec = pl.BlockSpec((tk, tn), lambda i, j, kk: (kk, j))
        dot = _dot_tn
    if out_chunks > 1:
        npc = (n // out_chunks) // tn
        out_spec = pl.BlockSpec((None, tm, tn), lambda i, j, kk: (j // npc, i, j % npc))
        out_shape = jax.ShapeDtypeStruct((out_chunks, m, n // out_chunks), out_dtype)
    else:
        out_spec = pl.BlockSpec((tm, tn), lambda i, j, kk: (i, j))
        out_shape = jax.ShapeDtypeStruct((m, n), out_dtype)

    def body(a_ref, b_ref, o_ref, acc_ref):
        kk = pl.program_id(2)

        @pl.when(kk == 0)
        def _():
            acc_ref[...] = jnp.zeros_like(acc_ref)

        acc_ref[...] += dot(a_ref[...], b_ref[...])

        @pl.when(kk == nk - 1)
        def _():
            o_ref[...] = acc_ref[...].astype(out_dtype)

    return pl.pallas_call(
        body, name=name, grid=(m // tm, n // tn, nk),
        in_specs=[a_spec, b_spec], out_specs=out_spec, out_shape=out_shape,
        scratch_shapes=[pltpu.VMEM((tm, tn), F32)],
        compiler_params=_params(dimension_semantics=("parallel", "parallel", "arbitrary")),
    )(a, b)


def _tile(col0=0):
    return lambda tm, tn: pl.BlockSpec((tm, tn), lambda i, j, kk: (i, col0 // tn + j))


def _row():
    return lambda tm, tn: pl.BlockSpec((1, tn), lambda i, j, kk: (0, j))


def _mm_ep(pairs, mode, name, epilogue, ins, outs, tm, tn, aliases=None):
    a0, b0 = pairs[0]
    bch = b0.shape[0] if b0.ndim == 3 else 1
    m, k = a0.shape
    tm = _pick(m, tm)
    n = b0.shape[-1] * bch if mode == "nn" else b0.shape[-2]
    tk = _pick(k // bch if mode == "nt" else k, 2048)
    nk = k // tk
    a_spec = pl.BlockSpec((tm, tk), lambda i, j, kk: (i, kk))
    if mode == "nn":
        dot = _dot_nn
        if bch > 1:
            bpc = (n // bch) // tn
            b_spec = pl.BlockSpec((None, tk, tn), lambda i, j, kk: (j // bpc, kk, j % bpc))
        else:
            b_spec = pl.BlockSpec((tk, tn), lambda i, j, kk: (kk, j))
    else:
        dot = _dot_nt
        if bch > 1:
            bpc = (k // bch) // tk
            b_spec = pl.BlockSpec((None, tn, tk), lambda i, j, kk: (kk // bpc, j, kk % bpc))
        else:
            b_spec = pl.BlockSpec((tn, tk), lambda i, j, kk: (j, kk))
    npair, nin, nout = len(pairs), len(ins), len(outs)

    def body(*refs):
        ab = refs[:2 * npair]
        in_refs = refs[2 * npair:2 * npair + nin]
        out_refs = refs[2 * npair + nin:2 * npair + nin + nout]
        accs = refs[2 * npair + nin + nout:]
        i, j, kk = pl.program_id(0), pl.program_id(1), pl.program_id(2)
        for p in range(npair):
            prod = dot(ab[2 * p][...], ab[2 * p + 1][...])

            @pl.when(kk == 0)
            def _():
                accs[p][...] = prod

            @pl.when(kk > 0)
            def _():
                accs[p][...] += prod

        @pl.when(kk == nk - 1)
        def _():
            epilogue([acc[...] for acc in accs], in_refs, out_refs, i, j)

    operands = [x for pair in pairs for x in pair] + [a for a, _ in ins]
    io_alias = {2 * npair + i: o for i, o in (aliases or {}).items()}
    return pl.pallas_call(
        body, name=name, grid=(m // tm, n // tn, nk),
        in_specs=[a_spec, b_spec] * npair + [mk(tm, tn) for _, mk in ins],
        out_specs=[mk(tm, tn) for _, mk in outs], out_shape=[s for s, _ in outs],
        scratch_shapes=[pltpu.VMEM((tm, tn), F32)] * npair, input_output_aliases=io_alias,
        compiler_params=_params(dimension_semantics=("arbitrary", "arbitrary", "arbitrary")),
    )(*operands)


def _tok(w, j=0):
    return pl.BlockSpec((TOK_TILE, w), lambda i: (i, j))


def _rep(shape):
    return pl.BlockSpec(shape, lambda i: (0,) * len(shape))


def _rms(x):
    rstd = lax.rsqrt(jnp.mean(x * x, axis=-1, keepdims=True) + EPS)
    return x * rstd, rstd


def _rms_bwd(xn, rstd, dxn):
    return rstd * (dxn - xn * jnp.mean(dxn * xn, axis=-1, keepdims=True))


def _norm_fwd(x, g, name):
    t = x.shape[0]

    def body(x_ref, g_ref, h_ref):
        xn, _ = _rms(x_ref[...])
        h_ref[...] = (xn * g_ref[...]).astype(BF16)

    return pl.pallas_call(
        body, name=name, grid=(t // TOK_TILE,), in_specs=[_tok(D), _rep((1, D))], out_specs=_tok(D),
        out_shape=jax.ShapeDtypeStruct((t, D), BF16), compiler_params=_params(),
    )(x, g)


def _halo_prev(w, j=0, rows=8):
    r = TOK_TILE // rows
    return pl.BlockSpec((rows, w), lambda i: (jnp.maximum(i * r - 1, 0), j))


def _last8(halo_ref):
    return halo_ref[...].astype(F32)[halo_ref.shape[0] - 8:]


def _halo_next(w, nt, j=0):
    r = TOK_TILE // 8
    return pl.BlockSpec((8, w), lambda i: (jnp.minimum((i + 1) * r, nt * r - 1), j))


def _shift_down(x, halo, s):
    if s == 0:
        return x
    r = pltpu.roll(x, s, 0)
    hs = pltpu.roll(halo, s, 0)
    row = lax.broadcasted_iota(jnp.int32, hs.shape, 0)
    top = jnp.where(row < s, hs, r[0:8])
    return jnp.concatenate([top, r[8:]], axis=0)


def _shift_up(x, halo, s):
    if s == 0:
        return x
    n = x.shape[0]
    r = pltpu.roll(x, n - s, 0)
    hs = pltpu.roll(halo, 8 - s, 0)
    row = lax.broadcasted_iota(jnp.int32, hs.shape, 0)
    bot = jnp.where(row >= 8 - s, hs, r[n - 8:])
    return jnp.concatenate([r[:n - 8], bot], axis=0)


def _bf(x):
    return x.astype(BF16).astype(F32)


def _conv_taps(x, halo, w):
    x, halo, w = _bf(x), _bf(halo), _bf(w)
    acc = x * w[CONV - 1:CONV, :]
    for j in range(CONV - 1):
        acc = acc + _shift_down(x, halo, CONV - 1 - j) * w[j:j + 1, :]
    return acc


_Q_SCALE = DQK ** -0.5


def _qscale_row():
    lane = lax.broadcasted_iota(jnp.int32, (1, D), 1)
    return jnp.where(lane < MLH * DQK, _Q_SCALE, 1.0).astype(F32)


def _conv_silu_fwd(proj, conv_w):
    t = proj.shape[0]

    def body(x_ref, halo_ref, w_ref, o_ref):
        halo = jnp.where(pl.program_id(0) > 0, _last8(halo_ref), 0.0)
        c = _conv_taps(x_ref[...].astype(F32), halo, w_ref[...])
        o_ref[...] = (c * _sigmoid(c) * _qscale_row()).astype(BF16)

    return pl.pallas_call(
        body, name="conv_silu_fwd", grid=(t // TOK_TILE,),
        in_specs=[_tok(D, C_QK // D), _halo_prev(D, C_QK // D, 16), _rep((CONV, D))], out_specs=_tok(D),
        out_shape=jax.ShapeDtypeStruct((t, D), BF16), compiler_params=_params(),
    )(proj, proj, conv_w)


def _conv_silu_bwd_a(proj, conv_w, dqk):
    t = proj.shape[0]

    def body(x_ref, halo_ref, w_ref, d_ref, dc_ref, dw_ref):
        @pl.when(pl.program_id(0) == 0)
        def _():
            dw_ref[...] = jnp.zeros_like(dw_ref)

        halo = jnp.where(pl.program_id(0) > 0, _last8(halo_ref), 0.0)
        x = x_ref[...].astype(F32)
        c = _conv_taps(x, halo, w_ref[...])
        s = _sigmoid(c)
        dc = d_ref[...] * _qscale_row() * (s * (1.0 + c * (1.0 - s)))
        dc_ref[...] = dc
        dcb, xb, halo_b = _bf(dc), _bf(x), _bf(halo)
        for j in range(CONV):
            dw_ref[j:j + 1, :] += jnp.sum(dcb * _shift_down(xb, halo_b, CONV - 1 - j), axis=0, keepdims=True)

    return pl.pallas_call(
        body, name="conv_silu_bwd_a", grid=(t // TOK_TILE,),
        in_specs=[_tok(D, C_QK // D), _halo_prev(D, C_QK // D, 16), _rep((CONV, D)), _tok(D)],
        out_specs=[_tok(D), _rep((CONV, D))],
        out_shape=[jax.ShapeDtypeStruct((t, D), F32), jax.ShapeDtypeStruct((CONV, D), F32)],
        compiler_params=_params(),
    )(proj, proj, conv_w, dqk)


def _conv_silu_bwd_b(dc, conv_w, dproj):
    t = dc.shape[0]
    nt = t // TOK_TILE

    def body(dc_ref, halo_ref, w_ref, _, dx_ref):
        halo = _bf(jnp.where(pl.program_id(0) < nt - 1, halo_ref[...], 0.0))
        dcv = _bf(dc_ref[...])
        w = _bf(w_ref[...])
        acc = dcv * w[CONV - 1:CONV, :]
        for j in range(CONV - 1):
            acc = acc + _shift_up(dcv, halo, CONV - 1 - j) * w[j:j + 1, :]
        dx_ref[...] = acc.astype(BF16)

    return pl.pallas_call(
        body, name="conv_silu_bwd_b", grid=(nt,), in_specs=[_tok(D), _halo_next(D, nt), _rep((CONV, D)), _ANY],
        out_specs=_tok(D, C_QK // D), out_shape=jax.ShapeDtypeStruct((t, NP), BF16),
        input_output_aliases={3: 0}, compiler_params=_params(),
    )(dc, dc, conv_w, dproj)


def _gates_fwd(pre_rows, bias_col):
    t = pre_rows.shape[1]

    def body(p_ref, b_ref, g_ref, s_ref):
        z = p_ref[...] + b_ref[...]
        lf = jnp.minimum(z, 0.0) - jnp.log(1.0 + jnp.exp(-jnp.abs(z)))
        lane = lax.broadcasted_iota(jnp.int32, z.shape, 1) % CHUNK
        cum = lf
        s = 1
        while s < CHUNK:
            cum = cum + jnp.where(lane >= s, pltpu.roll(cum, s, 1), 0.0)
            s *= 2
        sub = lax.broadcasted_iota(jnp.int32, z.shape, 0)
        g_ref[...] = jnp.where(sub < MLH, z, cum)
        s_ref[...] = _sigmoid(-z)

    return pl.pallas_call(
        body, name="gates_fwd",
        out_shape=[jax.ShapeDtypeStruct((8, t), F32), jax.ShapeDtypeStruct((8, t), F32)],
        compiler_params=_params(),
    )(pre_rows, bias_col)


def _chunk_terms(grow, gcol, h, m0):
    i_row, b_row = grow[h:h + 1, :], grow[MLH + h:MLH + h + 1, :]
    i_col, b_col = gcol[:, h:h + 1], gcol[:, MLH + h:MLH + h + 1]
    b_last = b_row[:, CHUNK - 1:CHUNK]
    tt = lax.broadcasted_iota(jnp.int32, (CHUNK, CHUNK), 0)
    ss = lax.broadcasted_iota(jnp.int32, (CHUNK, CHUNK), 1)
    log_d = jnp.where(tt >= ss, b_col - b_row + i_row, -jnp.inf)
    m_t = jnp.maximum(b_col + m0, jnp.max(log_d, axis=1, keepdims=True))
    dm = jnp.exp(log_d - m_t)
    wi = jnp.exp(b_col + m0 - m_t)
    m1 = jnp.maximum(b_last + m0, jnp.max(b_last - b_row + i_row, axis=1, keepdims=True))
    ws = jnp.exp(b_last - b_col + i_col - m1)
    dec = jnp.exp(b_last + m0 - m1)
    return dm, wi, m_t, ws, dec, m1


def _mlstm_fwd(qk, proj, grow, gcol, gain):
    t = qk.shape[0]
    nc = t // CHUNK

    def body(qk_ref, v_ref, o_ref, grow_ref, gcol_ref, g_ref, h_ref, y_ref, cs_ref, st_ref, c_scr, st_scr):
        @pl.when(pl.program_id(0) == 0)
        def _():
            c_scr[...] = jnp.zeros_like(c_scr)
            st_scr[...] = jnp.zeros_like(st_scr)

        grow_v, gcol_v = grow_ref[...], gcol_ref[...]
        for h in range(MLH):
            q = qk_ref[:, h * DQK:(h + 1) * DQK]
            k = qk_ref[:, MLH * DQK + h * DQK:MLH * DQK + (h + 1) * DQK]
            v = v_ref[:, h * DV:(h + 1) * DV]
            c0 = c_scr[h]
            n0 = st_scr[h, 0:1, :]
            m0 = st_scr[h, 1:2, 0:1]
            cs_ref[0, h] = c0
            st_ref[0, h] = st_scr[h]
            dm, wi, m_t, ws, dec, m1 = _chunk_terms(grow_v, gcol_v, h, m0)
            s = _dot_nt(q, k) * dm
            num = wi * _dot_nt(q, c0.astype(BF16)) + _dot_nn(s.astype(BF16), v.astype(BF16))
            den = wi * jnp.sum(q.astype(F32) * n0, axis=1, keepdims=True) + jnp.sum(s, axis=1, keepdims=True)
            sl = slice(h * DV, (h + 1) * DV)
            hv = num / jnp.maximum(jnp.abs(den), jnp.exp(-m_t))
            h_ref[:, sl] = hv
            xn, _ = _rms(hv)
            y_ref[:, sl] = (_sigmoid(o_ref[:, sl].astype(F32)) * xn * g_ref[:, sl]).astype(BF16)
            c_scr[h] = dec * c0 + _dot_tn((ws * v).astype(BF16), k)
            st_scr[h, 0:1, :] = dec * n0 + jnp.sum(ws * k.astype(F32), axis=0, keepdims=True)
            st_scr[h, 1:2, :] = jnp.broadcast_to(m1, (1, DQK))

    return pl.pallas_call(
        body, name="mlstm_fwd", grid=(nc,),
        in_specs=[pl.BlockSpec((CHUNK, D), lambda c: (c, 0)), pl.BlockSpec((CHUNK, D), lambda c: (c, C_V // D)),
                  pl.BlockSpec((CHUNK, D), lambda c: (c, C_O // D)),
                  pl.BlockSpec((8, CHUNK), lambda c: (0, c)), pl.BlockSpec((CHUNK, 8), lambda c: (c, 0)),
                  pl.BlockSpec((1, D), lambda c: (0, 0))],
        out_specs=[pl.BlockSpec((CHUNK, D), lambda c: (c, 0)), pl.BlockSpec((CHUNK, D), lambda c: (c, 0)),
                   pl.BlockSpec((1, MLH, DV, DQK), lambda c: (c, 0, 0, 0)),
                   pl.BlockSpec((1, MLH, 8, DQK), lambda c: (c, 0, 0, 0))],
        out_shape=[jax.ShapeDtypeStruct((t, D), F32), jax.ShapeDtypeStruct((t, D), BF16),
                   jax.ShapeDtypeStruct((nc, MLH, DV, DQK), F32), jax.ShapeDtypeStruct((nc, MLH, 8, DQK), F32)],
        scratch_shapes=[pltpu.VMEM((MLH, DV, DQK), F32), pltpu.VMEM((MLH, 8, DQK), F32)],
        compiler_params=_params(dimension_semantics=("arbitrary",)),
    )(qk, proj, proj, grow, gcol, gain)


def _mlstm_bwd(qk, proj, grow, gcol, sneg_col, cs, st, hraw, dh, dproj):
    t = qk.shape[0]
    nc = t // CHUNK

    def rev(c):
        return nc - 1 - c

    def nxt(c):
        return jnp.minimum(nc - c, nc - 1)

    def body(qk_ref, v_ref, grow_ref, gcol_ref, sneg_ref, cs_ref, st_ref, cs1_ref, st1_ref, h_ref, dh_ref, _,
             dqk_ref, dv_ref, dif_ref, dbif_ref, dc_scr, dn_scr):
        @pl.when(pl.program_id(0) == 0)
        def _():
            dc_scr[...] = jnp.zeros_like(dc_scr)
            dn_scr[...] = jnp.zeros_like(dn_scr)
            dbif_ref[...] = jnp.zeros_like(dbif_ref)

        grow_v, gcol_v, sneg = grow_ref[...], gcol_ref[...], sneg_ref[...]
        tt = lax.broadcasted_iota(jnp.int32, (CHUNK, CHUNK), 0)
        ss = lax.broadcasted_iota(jnp.int32, (CHUNK, CHUNK), 1)
        lane8 = lax.broadcasted_iota(jnp.int32, (CHUNK, 8), 1)
        dif = jnp.zeros((CHUNK, 8), F32)
        for h in range(MLH):
            q = qk_ref[:, h * DQK:(h + 1) * DQK]
            k = qk_ref[:, MLH * DQK + h * DQK:MLH * DQK + (h + 1) * DQK]
            qf, kf = q.astype(F32), k.astype(F32)
            v = v_ref[:, h * DV:(h + 1) * DV]
            vb = v.astype(BF16)
            c0 = cs_ref[0, h]
            n0 = st_ref[0, h, 0:1, :]
            m0 = st_ref[0, h, 1:2, 0:1]
            dc1 = dc_scr[h]
            dn1 = dn_scr[h, 0:1, :]
            dm, wi, m_t, ws, dec, _ = _chunk_terms(grow_v, gcol_v, h, m0)
            s = _dot_nt(q, k) * dm
            den = wi * jnp.sum(qf * n0, axis=1, keepdims=True) + jnp.sum(s, axis=1, keepdims=True)
            floor = jnp.exp(-m_t)
            g = jnp.maximum(jnp.abs(den), floor)
            dh_v = dh_ref[:, h * DV:(h + 1) * DV]
            dnum = dh_v / g
            dden = -jnp.sum(dh_v * h_ref[:, h * DV:(h + 1) * DV], axis=1, keepdims=True) / g
            dden = jnp.where(jnp.abs(den) > floor, dden * jnp.sign(den), 0.0)
            dnum_b = dnum.astype(BF16)
            da = ((_dot_nt(dnum_b, vb) + dden) * dm).astype(BF16)
            dc1_b = dc1.astype(BF16)
            dq = _dot_nn(da, k) + wi * (_dot_nn(dnum_b, c0.astype(BF16)) + dden * n0)
            dk = _dot_tn(da, q) + ws * (_dot_nn(vb, dc1_b) + dn1)
            dv = _dot_tn(s.astype(BF16), dnum_b) + ws * _dot_nt(k, dc1_b)
            dqk_ref[:, h * DQK:(h + 1) * DQK] = dq
            dqk_ref[:, MLH * DQK + h * DQK:MLH * DQK + (h + 1) * DQK] = dk
            dv_ref[:, h * DV:(h + 1) * DV] = dv.astype(BF16)
            rk = jnp.sum(kf * dk, axis=1, keepdims=True)
            df = jnp.sum(qf * dq, axis=1, keepdims=True) - rk
            df_row = jnp.sum(jnp.where(tt == ss, df, 0.0), axis=0, keepdims=True)
            suffix = jnp.sum(jnp.where(ss >= tt, df_row, 0.0), axis=1, keepdims=True)
            cross = (jnp.sum(jnp.sum(dc1 * cs1_ref[0, h], axis=1, keepdims=True), axis=0, keepdims=True)
                     + jnp.sum(dn1 * st1_ref[0, h, 0:1, :], axis=1, keepdims=True))
            dpf = (suffix + cross) * sneg[:, MLH + h:MLH + h + 1]
            dif = dif + jnp.where(lane8 == h, rk, 0.0) + jnp.where(lane8 == MLH + h, dpf, 0.0)
            dc_scr[h] = dec * dc1 + _dot_tn((wi * dnum).astype(BF16), q)
            dn_scr[h, 0:1, :] = dec * dn1 + jnp.sum(wi * dden * qf, axis=0, keepdims=True)
        dif_ref[...] = dif
        dbif_ref[...] += jnp.sum(dif, axis=0, keepdims=True)

    return pl.pallas_call(
        body, name="mlstm_bwd", grid=(nc,),
        in_specs=[pl.BlockSpec((CHUNK, D), lambda c: (rev(c), 0)),
                  pl.BlockSpec((CHUNK, D), lambda c: (rev(c), C_V // D)),
                  pl.BlockSpec((8, CHUNK), lambda c: (0, rev(c))),
                  pl.BlockSpec((CHUNK, 8), lambda c: (rev(c), 0)),
                  pl.BlockSpec((CHUNK, 8), lambda c: (rev(c), 0)),
                  pl.BlockSpec((1, MLH, DV, DQK), lambda c: (rev(c), 0, 0, 0)),
                  pl.BlockSpec((1, MLH, 8, DQK), lambda c: (rev(c), 0, 0, 0)),
                  pl.BlockSpec((1, MLH, DV, DQK), lambda c: (nxt(c), 0, 0, 0)),
                  pl.BlockSpec((1, MLH, 8, DQK), lambda c: (nxt(c), 0, 0, 0)),
                  pl.BlockSpec((CHUNK, D), lambda c: (rev(c), 0)),
                  pl.BlockSpec((CHUNK, D), lambda c: (rev(c), 0)), _ANY],
        out_specs=[pl.BlockSpec((CHUNK, D), lambda c: (rev(c), 0)),
                   pl.BlockSpec((CHUNK, D), lambda c: (rev(c), C_V // D)),
                   pl.BlockSpec((CHUNK, 8), lambda c: (rev(c), 0)),
                   pl.BlockSpec((1, 8), lambda c: (0, 0))],
        out_shape=[jax.ShapeDtypeStruct((t, D), F32), jax.ShapeDtypeStruct((t, NP), BF16),
                   jax.ShapeDtypeStruct((t, 8), F32), jax.ShapeDtypeStruct((1, 8), F32)],
        scratch_shapes=[pltpu.VMEM((MLH, DV, DQK), F32), pltpu.VMEM((MLH, 8, DQK), F32)],
        input_output_aliases={11: 1}, compiler_params=_params(dimension_semantics=("arbitrary",)),
    )(qk, proj, grow, gcol, sneg_col, cs, st, cs, st, hraw, dh, dproj)


_ANY = pl.BlockSpec(memory_space=pl.ANY)


_SW_SCALE = HD ** -0.5
_KVB = C_KV // (2 * SWKV * HD)


def _swa_mask(n):
    ki = lax.broadcasted_iota(jnp.int32, (2 * WIN, SWG * WIN), 0)
    qi = lax.broadcasted_iota(jnp.int32, (2 * WIN, SWG * WIN), 1) % WIN
    return (ki > qi) & (ki <= qi + WIN) & ((n > 0) | (ki >= WIN))


def _group_rows(x_ref, hk):
    return jnp.concatenate([x_ref[:, (hk * SWG + g) * HD:(hk * SWG + g + 1) * HD] for g in range(SWG)], axis=0)


def _group_lanes(x_ref, hk):
    return jnp.concatenate([x_ref[hk * SWG + g:hk * SWG + g + 1, :] for g in range(SWG)], axis=1)


def _sink_lanes(sink_ref, hk):
    return jnp.concatenate([jnp.broadcast_to(sink_ref[:, hk * SWG + g:hk * SWG + g + 1], (1, WIN))
                            for g in range(SWG)], axis=1)


def _swa_fwd(proj, sinks):
    t = proj.shape[0]
    nb = t // WIN

    def body(q_ref, kvc_ref, kvp_ref, sink_ref, y_ref, lse_ref):
        valid = _swa_mask(pl.program_id(0))
        for hk in range(SWKV):
            ks = slice(hk * HD, (hk + 1) * HD)
            vs = slice(SWKV * HD + hk * HD, SWKV * HD + (hk + 1) * HD)
            kb = jnp.concatenate([kvp_ref[:, ks], kvc_ref[:, ks]], axis=0).astype(BF16)
            vb = jnp.concatenate([kvp_ref[:, vs], kvc_ref[:, vs]], axis=0).astype(BF16)
            q4 = _group_rows(q_ref, hk).astype(BF16)
            sink = _sink_lanes(sink_ref, hk)
            logits = jnp.where(valid, _dot_nt(kb, q4) * _SW_SCALE, -jnp.inf)
            m = jnp.maximum(jnp.max(logits, axis=0, keepdims=True), sink)
            p = jnp.exp(logits - m)
            denom = jnp.sum(p, axis=0, keepdims=True) + jnp.exp(sink - m)
            y4 = _dot_tn((p / denom).astype(BF16), vb).astype(BF16)
            lse4 = m + jnp.log(denom)
            for g in range(SWG):
                hq = hk * SWG + g
                y_ref[:, hq * HD:(hq + 1) * HD] = y4[g * WIN:(g + 1) * WIN]
                lse_ref[hq:hq + 1, :] = lse4[:, g * WIN:(g + 1) * WIN]

    return pl.pallas_call(
        body, name="swa_fwd", grid=(nb,),
        in_specs=[pl.BlockSpec((WIN, D), lambda n: (n, C_QSW // D)),
                  pl.BlockSpec((WIN, 512), lambda n: (n, _KVB)),
                  pl.BlockSpec((WIN, 512), lambda n: (jnp.maximum(n - 1, 0), _KVB)),
                  pl.BlockSpec((1, SWH), lambda n: (0, 0))],
        out_specs=[pl.BlockSpec((WIN, D), lambda n: (n, 0)), pl.BlockSpec((SWH, WIN), lambda n: (0, n))],
        out_shape=[jax.ShapeDtypeStruct((t, D), BF16), jax.ShapeDtypeStruct((SWH, t), F32)],
        compiler_params=_params(),
    )(proj, proj, proj, sinks)


def _swa_bwd(proj, sinks, lse, dyb, dproj):
    t = proj.shape[0]
    nb = t // WIN

    def body(q_ref, kvc_ref, kvp_ref, sink_ref, lse_ref, dy_ref, _, dq_ref, dself_ref, dprev_ref, ds_ref):
        @pl.when(pl.program_id(0) == 0)
        def _():
            ds_ref[...] = jnp.zeros_like(ds_ref)

        valid = _swa_mask(pl.program_id(0))
        for hk in range(SWKV):
            ks = slice(hk * HD, (hk + 1) * HD)
            vs = slice(SWKV * HD + hk * HD, SWKV * HD + (hk + 1) * HD)
            kb = jnp.concatenate([kvp_ref[:, ks], kvc_ref[:, ks]], axis=0).astype(BF16)
            vb = jnp.concatenate([kvp_ref[:, vs], kvc_ref[:, vs]], axis=0).astype(BF16)
            dy4 = _group_rows(dy_ref, hk)
            qb, dyb_ = _group_rows(q_ref, hk).astype(BF16), dy4.astype(BF16)
            lse4 = _group_lanes(lse_ref, hk)
            logits = jnp.where(valid, _dot_nt(kb, qb) * _SW_SCALE, -jnp.inf)
            p = jnp.exp(logits - lse4)
            dpt = _dot_nt(vb, dyb_)
            delta = jnp.sum(p * dpt, axis=0, keepdims=True)
            dsm = (p * (dpt - delta)).astype(BF16)
            dq4 = (_dot_tn(dsm, kb) * _SW_SCALE).astype(BF16)
            dkb = _dot_nn(dsm, qb) * _SW_SCALE
            dvb = _dot_nn(p.astype(BF16), dyb_)
            dsink4 = jnp.exp(_sink_lanes(sink_ref, hk) - lse4) * delta
            for g in range(SWG):
                hq = hk * SWG + g
                dq_ref[:, hq * HD:(hq + 1) * HD] = dq4[g * WIN:(g + 1) * WIN]
                ds_ref[:, hq:hq + 1] += -jnp.sum(dsink4[:, g * WIN:(g + 1) * WIN], axis=1, keepdims=True)
            dprev_ref[:, ks] = dkb[:WIN]
            dself_ref[:, ks] = dkb[WIN:]
            dprev_ref[:, vs] = dvb[:WIN]
            dself_ref[:, vs] = dvb[WIN:]

    return pl.pallas_call(
        body, name="swa_bwd", grid=(nb,),
        in_specs=[pl.BlockSpec((WIN, D), lambda n: (n, C_QSW // D)),
                  pl.BlockSpec((WIN, 512), lambda n: (n, _KVB)),
                  pl.BlockSpec((WIN, 512), lambda n: (jnp.maximum(n - 1, 0), _KVB)),
                  pl.BlockSpec((1, SWH), lambda n: (0, 0)),
                  pl.BlockSpec((SWH, WIN), lambda n: (0, n)),
                  pl.BlockSpec((WIN, D), lambda n: (n, 0)), _ANY],
        out_specs=[pl.BlockSpec((WIN, D), lambda n: (n, C_QSW // D)), pl.BlockSpec((WIN, 512), lambda n: (n, 0)),
                   pl.BlockSpec((WIN, 512), lambda n: (jnp.maximum(n - 1, 0), 0)),
                   pl.BlockSpec((1, SWH), lambda n: (0, 0))],
        out_shape=[jax.ShapeDtypeStruct((t, NP), BF16), jax.ShapeDtypeStruct((t, 512), F32),
                   jax.ShapeDtypeStruct((t, 512), F32), jax.ShapeDtypeStruct((1, SWH), F32)],
        input_output_aliases={6: 0}, compiler_params=_params(),
    )(proj, proj, proj, sinks, lse, dyb, dproj)


def _kv_combine(dself, dnext, dif, dproj):
    t = dself.shape[0]
    rows = _pick(t, 512)

    def body(a_ref, b_ref, dif_ref, _, o_ref):
        row = pl.program_id(0) * rows + lax.broadcasted_iota(jnp.int32, (rows, 1), 0)
        o_ref[:, 0:512] = (a_ref[...] + jnp.where(row < t - WIN, b_ref[...], 0.0)).astype(BF16)
        lane = lax.broadcasted_iota(jnp.int32, (rows, 128), 1)
        dif_v = dif_ref[...]
        first = jnp.zeros((rows, 128), F32)
        for col in range(8):
            first = first + jnp.where(lane == col, dif_v[:, col:col + 1], 0.0)
        o_ref[:, 512:640] = first.astype(BF16)
        o_ref[:, 640:512 + IFW] = jnp.zeros((rows, IFW - 128), BF16)

    return pl.pallas_call(
        body, name="kv_combine", grid=(t // rows,),
        in_specs=[pl.BlockSpec((rows, 512), lambda n: (n, 0)), pl.BlockSpec((rows, 512), lambda n: (n, 0)),
                  pl.BlockSpec((rows, 8), lambda n: (n, 0)), _ANY],
        out_specs=pl.BlockSpec((rows, 512 + IFW), lambda n: (n, C_KV // (512 + IFW))),
        out_shape=jax.ShapeDtypeStruct((t, NP), BF16), input_output_aliases={3: 0}, compiler_params=_params(),
    )(dself, dnext, dif, dproj)


def _sds(t, n, dtype):
    return jax.ShapeDtypeStruct((t, n), dtype)


def _proj_in(h0, w_in):
    t = h0.shape[0]
    tn = 2 * IFW

    def epilogue(accs, ins, outs, i, j):
        outs[0][...] = accs[0].astype(BF16)

        @pl.when(j == C_IF // tn)
        def _():
            outs[1][...] = accs[0][:, C_IF % tn:C_IF % tn + 128]

    gate_cols = lambda tm, tn: pl.BlockSpec((tm, 128), lambda i, j, kk: (i, 0))
    return _mm_ep([(h0, w_in)], "nn", "mm_in", epilogue, [],
                  [(_sds(t, NP, BF16), _tile()), (_sds(t, 128, F32), gate_cols)], 1024, tn)


def _branch_merge(ya, yb, wa, wb, proj):
    t = ya.shape[0]

    def epilogue(accs, ins, outs, i, j):
        za, zb = accs
        merged = _sigmoid(ins[0][...].astype(F32)) * za + _sigmoid(ins[1][...].astype(F32)) * zb
        outs[0][...] = merged.astype(BF16)
        outs[1][...] = za.astype(BF16)
        outs[2][...] = zb.astype(BF16)

    return _mm_ep([(ya, wa), (yb, wb)], "nn", "mm_branch_merge", epilogue, [(proj, _tile(C_GA)), (proj, _tile(C_GB))],
                  [(_sds(t, D, BF16), _tile())] * 3, 1024, 512)


def _dmerged_bwd(dxb, w_out, proj, za, zb):
    t = dxb.shape[0]

    def epilogue(accs, ins, outs, i, j):
        dm = accs[0]
        sa, sb = _sigmoid(ins[0][...].astype(F32)), _sigmoid(ins[1][...].astype(F32))
        outs[0][...] = (dm * sa).astype(BF16)
        outs[1][...] = (dm * sb).astype(BF16)
        outs[2][:, 0:D] = (dm * ins[2][...].astype(F32) * sa * (1.0 - sa)).astype(BF16)
        outs[2][:, D:2 * D] = (dm * ins[3][...].astype(F32) * sb * (1.0 - sb)).astype(BF16)

    gate_cols = lambda tm, tn: pl.BlockSpec((tm, 2 * D), lambda i, j, kk: (i, C_GA // (2 * D)))
    return _mm_ep([(dxb, w_out)], "nt", "mm_dmerged_bwd", epilogue,
                  [(proj, _tile(C_GA)), (proj, _tile(C_GB)), (za, _tile()), (zb, _tile())],
                  [(_sds(t, D, BF16), _tile()), (_sds(t, D, BF16), _tile()), (_sds(t, NP, BF16), gate_cols)], 1024, D)


def _dya_bwd(dza, wa, hraw, proj, g, dproj):
    t = dza.shape[0]

    def epilogue(accs, ins, outs, i, j):
        h_ref, o_ref, g_ref, _ = ins
        dh_ref, do_ref, dg_ref = outs

        @pl.when(i == 0)
        def _():
            dg_ref[...] = jnp.zeros_like(dg_ref)

        dy = accs[0]
        so = _sigmoid(o_ref[...].astype(F32))
        for h in range(MLH):
            sl = slice(h * DV, (h + 1) * DV)
            xn, rstd = _rms(h_ref[:, sl])
            gs = g_ref[:, sl]
            do_ref[:, sl] = (dy[:, sl] * xn * gs * so[:, sl] * (1.0 - so[:, sl])).astype(BF16)
            dhn = dy[:, sl] * so[:, sl]
            dg_ref[:, sl] += jnp.sum(dhn * xn, axis=0, keepdims=True)
            dh_ref[:, sl] = _rms_bwd(xn, rstd, dhn * gs)

    return _mm_ep([(dza, wa)], "nt", "mm_dya_bwd", epilogue,
                  [(hraw, _tile()), (proj, _tile(C_O)), (g, _row()), (dproj, lambda tm, tn: _ANY)],
                  [(_sds(t, D, F32), _tile()), (_sds(t, NP, BF16), _tile(C_O)), (_sds(1, D, F32), _row())],
                  1024, D, aliases={3: 1})


def _up_act(hn, w_up):
    t = hn.shape[0]

    def epilogue(accs, ins, outs, i, j):
        r = jnp.maximum(accs[0], 0.0)
        outs[0][...] = (r * r).astype(BF16)
        outs[1][...] = accs[0].astype(BF16)

    return _mm_ep([(hn, w_up)], "nn", "mm_up_act", epilogue, [],
                  [(_sds(t, DFF, BF16), _tile()), (_sds(t, DFF, BF16), _tile())], 1024, 1024)


def _da_du(dxb, w_down, u):
    t = dxb.shape[0]

    def epilogue(accs, ins, outs, i, j):
        outs[0][...] = (accs[0] * 2.0 * jnp.maximum(ins[0][...].astype(F32), 0.0)).astype(BF16)

    return _mm_ep([(dxb, w_down)], "nt", "mm_da_du", epilogue, [(u, _tile())], [(_sds(t, DFF, BF16), _tile())],
                  1024, 1024)[0]


def _resid_norm_mm(a, w, x, g, name):
    t = x.shape[0]

    def epilogue(accs, ins, outs, i, j):
        x1 = ins[0][...] + accs[0]
        outs[0][...] = x1
        xn, _ = _rms(x1)
        outs[1][...] = (xn * ins[1][...]).astype(BF16)

    return _mm_ep([(a, w)], "nn", name, epilogue, [(x, _tile()), (g, _row())],
                  [(_sds(t, D, F32), _tile()), (_sds(t, D, BF16), _tile())], 1024, D)


def _norm_bwd_mm(dy, w, x, g, dres, name):
    t = x.shape[0]

    def epilogue(accs, ins, outs, i, j):
        @pl.when(i == 0)
        def _():
            outs[2][...] = jnp.zeros_like(outs[2])

        dh = accs[0]
        xn, rstd = _rms(ins[0][...])
        outs[2][...] += jnp.sum(dh * xn, axis=0, keepdims=True)
        dx = ins[2][...] + _rms_bwd(xn, rstd, dh * ins[1][...])
        outs[0][...] = dx
        outs[1][...] = dx.astype(BF16)

    return _mm_ep([(dy, w)], "nt", name, epilogue, [(x, _tile()), (g, _row()), (dres, _tile())],
                  [(_sds(t, D, F32), _tile()), (_sds(t, D, BF16), _tile()), (_sds(1, D, F32), _row())], 1024, D)


def _ple_final_mm(hn2, w_gate, x2, pp, target, gf):
    t = x2.shape[0]

    def epilogue(accs, ins, outs, i, j):
        loss_ref, dg_ref, dx_ref, dpp_ref, dgp_ref = outs

        @pl.when(i == 0)
        def _():
            loss_ref[...] = jnp.zeros_like(loss_ref)
            dg_ref[...] = jnp.zeros_like(dg_ref)

        gate = _sigmoid(accs[0])
        pp_v = ins[1][...]
        x3 = ins[0][...] + gate * pp_v
        xn, rstd = _rms(x3)
        gf_v = ins[3][...]
        err = xn * gf_v - ins[2][...]
        loss_ref[...] += (0.5 / D) * jnp.sum(jnp.sum(err * err, axis=1, keepdims=True), axis=0, keepdims=True)
        dy = err * (1.0 / D)
        dg_ref[...] += jnp.sum(dy * xn, axis=0, keepdims=True)
        dx3 = _rms_bwd(xn, rstd, dy * gf_v)
        dx_ref[...] = dx3
        dpp_ref[...] = (dx3 * gate).astype(BF16)
        dgp_ref[...] = (dx3 * pp_v * gate * (1.0 - gate)).astype(BF16)

    one = lambda tm, tn: pl.BlockSpec((1, 1), lambda i, j, kk: (0, 0))
    return _mm_ep([(hn2, w_gate)], "nn", "mm_ple_final", epilogue,
                  [(x2, _tile()), (pp, _tile()), (target, _tile()), (gf, _row())],
                  [(_sds(1, 1, F32), one), (_sds(1, D, F32), _row()), (_sds(t, D, F32), _tile()),
                   (_sds(t, D, BF16), _tile()), (_sds(t, D, BF16), _tile())], 512, D)


def _win_pad(w):
    zeros = jnp.zeros((w.shape[0], IFW - 8), w.dtype)
    return jnp.concatenate([w[:, 0:3072], w[:, 3080:4104], w[:, 4616:6664], w[:, 4104:4616], w[:, 3072:3080], zeros],
                           axis=1)


def _win_unpad(wp):
    return jnp.concatenate([wp[:, 0:3072], wp[:, C_IF:C_IF + 8], wp[:, C_QSW:C_QSW + 1024], wp[:, C_KV:C_KV + 512],
                            wp[:, C_GA:C_GA + 2048]], axis=1)


def _local_step(x, p, target, w, late_weights=None, early_grads=None, last_grad=None):
    t = x.shape[0]
    pb = p.astype(BF16)
    w = dict(w)

    h0 = _norm_fwd(x, w["norm_mix_g"], "norm_mix")
    proj, gates = _proj_in(h0, w["w_in"])
    qk = _conv_silu_fwd(proj, w["conv_qk"])
    grow, sneg_row = _gates_fwd(gates[:, 0:8].T, w["b_if"].reshape(8, 1))
    gcol, sneg_col = grow.T, sneg_row.T
    hraw, ya, cs, st = _mlstm_fwd(qk, proj, grow, gcol, w["mlstm_norm_g"])
    yb, lse = _swa_fwd(proj, w["sinks"])
    if late_weights is not None:
        w.update(late_weights(yb))
    merged, za, zb = _branch_merge(ya, yb, w["w_branch_a"], w["w_branch_b"], proj)
    x1, hn1 = _resid_norm_mm(merged, w["w_out"], x, w["norm_mlp_g"], "mm_out_norm")
    act, u = _up_act(hn1, w["w_up"])
    x2, hn2 = _resid_norm_mm(act, w["w_down"], x1, w["norm_ple_g"], "mm_down_norm")
    pp = _mm(pb, w["w_ple_proj"], "nn", F32, "mm_ple_proj")
    loss, d_final_g, dx3, dpp, dgpre = _ple_final_mm(hn2, w["w_ple_gate"], x2, pp, target, w["final_norm_g"])

    g = {"final_norm_g": d_final_g}
    g["w_ple_proj"] = _mm(pb, dpp, "tn", F32, "mm_d_ple_proj", out_chunks=4)
    g["w_ple_gate"] = _mm(hn2, dgpre, "tn", F32, "mm_d_ple_gate")
    dx2, dx2b, g["norm_ple_g"] = _norm_bwd_mm(dgpre, w["w_ple_gate"], x2, w["norm_ple_g"], dx3, "mm_dhn2_norm")
    g["w_down"] = _mm(act, dx2b, "tn", F32, "mm_d_down")
    du = _da_du(dx2b, w["w_down"], u)
    g["w_up"] = _mm(hn1, du, "tn", F32, "mm_d_up", out_chunks=4)
    dx1, dx1b, g["norm_mlp_g"] = _norm_bwd_mm(du, w["w_up"], x1, w["norm_mlp_g"], dx2, "mm_dhn1_norm")
    g["w_out"] = _mm(merged, dx1b, "tn", F32, "mm_d_out")
    dza, dzb, dproj = _dmerged_bwd(dx1b, w["w_out"], proj, za, zb)
    g["w_branch_a"] = _mm(ya, dza, "tn", F32, "mm_d_branch_a")
    g["w_branch_b"] = _mm(yb, dzb, "tn", F32, "mm_d_branch_b")
    gain = w["mlstm_norm_g"] if early_grads is None else w["mlstm_norm_g"] + early_grads(g)
    dyb = _mm(dzb, w["w_branch_b"], "nt", F32, "mm_dyb")
    dhraw, dproj, g["mlstm_norm_g"] = _dya_bwd(dza, w["w_branch_a"], hraw, proj, gain, dproj)
    dqk, dproj, dif, g["b_if"] = _mlstm_bwd(qk, proj, grow, gcol, sneg_col, cs, st, hraw, dhraw, dproj)
    dc, g["conv_qk"] = _conv_silu_bwd_a(proj, w["conv_qk"], dqk)
    dproj = _conv_silu_bwd_b(dc, w["conv_qk"], dproj)
    dproj, dkv_self, dkv_prev, g["sinks"] = _swa_bwd(proj, w["sinks"], lse, dyb, dproj)
    dproj = _kv_combine(dkv_self, dkv_prev, dif, dproj)
    g["w_in"] = _mm(h0, dproj, "tn", F32, "mm_d_in")
    gain = w["norm_mix_g"] if last_grad is None else w["norm_mix_g"] + last_grad(g)
    grad_x, _, g["norm_mix_g"] = _norm_bwd_mm(dproj, w["w_in"], x, gain, dx1, "mm_dh0_norm")
    return loss, grad_x, g


_W4 = ("w_branch_a", "w_branch_b", "w_out", "w_ple_gate")
_SHARDED_NAMES = ("w_in", "w_up", "w_down", "w_ple_proj", "conv_qk") + _W4
_SMALL_ROWS = 16
_CONV_ROW = 8


def _group(s):
    return [s["w_in"], jnp.concatenate([s[n] for n in _W4], axis=0), s["w_up"], s["w_down"], s["w_ple_proj"]]


def _ungroup(arrs):
    out = {"w_in": arrs[0], "w_up": arrs[2], "w_down": arrs[3], "w_ple_proj": arrs[4]}
    rows = arrs[1].shape[0] // len(_W4)
    for i, n in enumerate(_W4):
        out[n] = arrs[1][i * rows:(i + 1) * rows]
    return out


def _rows_tile(rows):
    return 256 if rows % 256 == 0 else rows


_SMALL = ("norm_mix_g", "mlstm_norm_g", "norm_mlp_g", "norm_ple_g", "final_norm_g")


def _pack_small(vals, extra=None, conv=None):
    rows = [vals[n].reshape(1, D) for n in _SMALL]
    tail = [vals["b_if"].reshape(1, 8), vals["sinks"].reshape(1, SWH)]
    used = 8 + SWH
    if extra is not None:
        tail.append(extra.reshape(1, 1))
        used += 1
    tail.append(jnp.zeros((1, D - used), F32))
    rows.append(jnp.concatenate(tail, axis=1))
    rows.append(jnp.zeros((_CONV_ROW - len(rows), D), F32))
    rows.append(jnp.zeros((CONV, D), F32) if conv is None else conv)
    rows.append(jnp.zeros((_SMALL_ROWS - _CONV_ROW - CONV, D), F32))
    return jnp.concatenate(rows, axis=0)


def _unpack_small(slab, shapes):
    out = {n: slab[i].reshape(shapes[n]) for i, n in enumerate(_SMALL)}
    out["b_if"] = slab[5, 0:8].reshape(shapes["b_if"])
    out["sinks"] = slab[5, 8:8 + SWH].reshape(shapes["sinks"])
    return out


_MESH = pl.DeviceIdType.MESH
_HBM = pl.BlockSpec(memory_space=pltpu.HBM)
_VMEM = pl.BlockSpec(memory_space=pltpu.VMEM)


def _place():
    x, y, c = lax.axis_index("x"), lax.axis_index("y"), lax.axis_index("c")
    return x, y, c, 2 * x + y


def _chip_peer(x, y, r):
    return (x ^ (r >> 1), y ^ (r & 1))


def _half(ref, which):
    h = ref.shape[-2] // 2
    return pl.ds(which * h, h)


def _allgather_weights(shards, conv):
    n = len(shards)

    def body(*refs):
        ins, conv_ref = refs[:n], refs[n]
        outs, conv_out = refs[n + 1:2 * n + 1], refs[2 * n + 1]
        send_a, recv_a, send_b, recv_b, send_c, recv_c, local_sems = refs[2 * n + 2:]
        x, y, c, j = _place()
        sibling = (x, y, 1 - c)
        local = [pltpu.make_async_copy(ins[k], outs[k].at[j], local_sems.at[k]) for k in range(n)]
        local.append(pltpu.make_async_copy(conv_ref, conv_out.at[j], local_sems.at[n]))
        for cp in local:
            cp.start()

        def copy_a(k, r, chip):
            rows = _half(ins[k], c)
            return pltpu.make_async_remote_copy(
                src_ref=ins[k].at[rows], dst_ref=outs[k].at[chip, rows], send_sem=send_a.at[3 * k + r - 1],
                recv_sem=recv_a.at[3 * k + r - 1], device_id=(*_chip_peer(x, y, r), c), device_id_type=_MESH)

        def copy_b(k, r, chip, which):
            rows = _half(ins[k], which)
            return pltpu.make_async_remote_copy(
                src_ref=outs[k].at[chip, rows], dst_ref=outs[k].at[chip, rows], send_sem=send_b.at[3 * k + r - 1],
                recv_sem=recv_b.at[3 * k + r - 1], device_id=sibling, device_id_type=_MESH)

        def copy_c(r, chip):
            return pltpu.make_async_remote_copy(
                src_ref=conv_ref, dst_ref=conv_out.at[chip], send_sem=send_c.at[r - 1],
                recv_sem=recv_c.at[r - 1], device_id=(*_chip_peer(x, y, r), c), device_id_type=_MESH)

        for k in range(n):
            for r in (1, 2, 3):
                copy_a(k, r, j).start()
        for r in (1, 2, 3):
            copy_c(r, j).start()
        for k in range(n):
            for r in (1, 2, 3):
                copy_a(k, r, j ^ r).wait_recv()
                copy_b(k, r, j ^ r, c).start()
        for k in range(n):
            for r in (1, 2, 3):
                copy_b(k, r, j ^ r, 1 - c).wait_recv()
        for r in (1, 2, 3):
            copy_c(r, j ^ r).wait_recv()
        for k in range(n):
            for r in (1, 2, 3):
                copy_a(k, r, j).wait_send()
                copy_b(k, r, j ^ r, c).wait_send()
        for r in (1, 2, 3):
            copy_c(r, j).wait_send()
        for cp in local:
            cp.wait()

    return pl.pallas_call(
        body, name="allgather_weights",
        out_shape=[jax.ShapeDtypeStruct((4,) + s.shape, s.dtype) for s in shards]
        + [jax.ShapeDtypeStruct((4,) + conv.shape, F32)],
        in_specs=[_HBM] * (n + 1), out_specs=[_HBM] * (n + 1),
        scratch_shapes=[pltpu.SemaphoreType.DMA((3 * n,))] * 4 + [pltpu.SemaphoreType.DMA((3,))] * 2
        + [pltpu.SemaphoreType.DMA((n + 1,))],
    )(*shards, conv)


_SEM = pl.BlockSpec(memory_space=pltpu.SEMAPHORE)
_DATAFLOW = pltpu.SideEffectType.DATAFLOW_SIDE_EFFECTING


def _late_peer_copy(src_ref, land_ref, send_sems, recv_sems, x, y, c, j, r, chip):
    return pltpu.make_async_remote_copy(
        src_ref=src_ref, dst_ref=land_ref.at[chip], send_sem=send_sems.at[r - 1], recv_sem=recv_sems.at[r - 1],
        device_id=(*_chip_peer(x, y, r), c), device_id_type=_MESH)


def _late_gather_start(rest):
    def body(rest_ref, land_ref, send_sems, recv_sems, rest_thru, land_thru, token):
        x, y, c, j = _place()
        for r in (1, 2, 3):
            _late_peer_copy(rest_ref, land_ref, send_sems, recv_sems, x, y, c, j, r, j).start()
        token[...] = jnp.zeros_like(token)

    j = 2 * lax.axis_index("x") + lax.axis_index("y")
    land = lax.dynamic_update_slice(lax.empty((4,) + rest.shape, rest.dtype), rest[None], (j, 0, 0))
    return pl.pallas_call(
        body, name="late_gather_start",
        out_shape=(pltpu.SemaphoreType.DMA((3,)), pltpu.SemaphoreType.DMA((3,)), pltpu.HBM(rest.shape, rest.dtype),
                   pltpu.HBM(land.shape, land.dtype), jax.ShapeDtypeStruct((8, 128), F32)),
        in_specs=(_HBM, _HBM), out_specs=(_SEM, _SEM, _HBM, _HBM, _VMEM), input_output_aliases={0: 2, 1: 3},
        compiler_params=pltpu.CompilerParams(has_side_effects=_DATAFLOW),
    )(pltpu.with_memory_space_constraint(rest, pltpu.HBM), pltpu.with_memory_space_constraint(land, pltpu.HBM))


def _late_gather_wait(send_sems, recv_sems, rest_thru, land_thru, after):
    def body(rest_ref, land_ref, send_sems, recv_sems, after_ref, rest_dead, got_ref):
        x, y, c, j = _place()
        for r in (1, 2, 3):
            cp = _late_peer_copy(rest_ref, land_ref, send_sems, recv_sems, x, y, c, j, r, j ^ r)
            cp.wait_send()
            cp.wait_recv()

    return pl.pallas_call(
        body, name="late_gather_wait",
        out_shape=(pltpu.HBM(rest_thru.shape, rest_thru.dtype), pltpu.HBM(land_thru.shape, land_thru.dtype)),
        in_specs=(_HBM, _HBM, _SEM, _SEM, _ANY), out_specs=(_HBM, _HBM), input_output_aliases={0: 0, 1: 1},
        compiler_params=pltpu.CompilerParams(has_side_effects=_DATAFLOW),
    )(rest_thru, land_thru, send_sems, recv_sems, after)[1]


def _pair_exchange(gs, name):
    n = len(gs)

    def body(*refs):
        ins, outs, send_sems, recv_sems = refs[:n], refs[n:2 * n], refs[2 * n], refs[2 * n + 1]
        x, y, c, _ = _place()
        cps = [pltpu.make_async_remote_copy(
            src_ref=ins[k].at[:, _half(ins[k], 1 - c)], dst_ref=outs[k], send_sem=send_sems.at[k],
            recv_sem=recv_sems.at[k], device_id=(x, y, 1 - c), device_id_type=_MESH) for k in range(n)]
        for cp in cps:
            cp.start()
        for cp in cps:
            cp.wait()

    return pl.pallas_call(
        body, name=name,
        out_shape=[jax.ShapeDtypeStruct((4, g.shape[1] // 2, g.shape[2]), F32) for g in gs],
        in_specs=[_HBM] * n, out_specs=[_HBM] * n, scratch_shapes=[pltpu.SemaphoreType.DMA((n,))] * 2,
    )(*gs)


def _pair_sum(g, theirs, c, name):
    _, h, cols = theirs.shape
    tr = _rows_tile(h)
    nb = h // tr

    def body(c_ref, a_ref, b_ref, o_ref, ob_ref):
        s = a_ref[...] + b_ref[...]
        o_ref[...] = s
        ob_ref[...] = s.astype(BF16)

    blk = pl.BlockSpec((1, tr, cols), lambda k, i, c_ref: (k, i, 0))
    return pl.pallas_call(
        body, name=name,
        grid_spec=pltpu.PrefetchScalarGridSpec(
            num_scalar_prefetch=1, grid=(4, nb),
            in_specs=[pl.BlockSpec((1, tr, cols), lambda k, i, c_ref: (k, c_ref[0] * nb + i, 0)), blk],
            out_specs=[blk, blk]),
        out_shape=[jax.ShapeDtypeStruct(theirs.shape, F32), jax.ShapeDtypeStruct(theirs.shape, BF16)],
        compiler_params=_params(),
    )(c.reshape(1).astype(jnp.int32), g, theirs)


def _chip_copies(srcs, lands, send_sems, recv_sems):
    x, y, c, j = _place()
    return [pltpu.make_async_remote_copy(
        src_ref=srcs[k].at[j ^ r], dst_ref=lands[k].at[r - 1], send_sem=send_sems.at[3 * k + r - 1],
        recv_sem=recv_sems.at[3 * k + r - 1], device_id=(*_chip_peer(x, y, r), c), device_id_type=_MESH)
        for k in range(len(srcs)) for r in (1, 2, 3)]


def _chip_exchange_start(ss, tag):
    n = len(ss)

    def body(*refs):
        srcs, lands, send_sems, recv_sems, token = refs[:n], refs[n:2 * n], refs[2 * n], refs[2 * n + 1], refs[-1]
        for cp in _chip_copies(srcs, lands, send_sems, recv_sems):
            cp.start()
        token[...] = jnp.zeros_like(token)

    lands = [lax.empty((3,) + s.shape[1:], s.dtype) for s in ss]
    hbm = [pltpu.HBM(a.shape, a.dtype) for a in list(ss) + lands]
    out = pl.pallas_call(
        body, name="chip_exchange_start_" + tag,
        out_shape=(pltpu.SemaphoreType.DMA((3 * n,)), pltpu.SemaphoreType.DMA((3 * n,)), *hbm,
                   jax.ShapeDtypeStruct((8, 128), F32)),
        in_specs=[_HBM] * (2 * n), out_specs=(_SEM, _SEM, *([_HBM] * (2 * n)), _VMEM),
        input_output_aliases={k: 2 + k for k in range(2 * n)},
        compiler_params=pltpu.CompilerParams(has_side_effects=_DATAFLOW),
    )(*[pltpu.with_memory_space_constraint(a, pltpu.HBM) for a in list(ss) + lands])
    return out[0], out[1], list(out[2:2 + n]), list(out[2 + n:2 + 2 * n]), out[-1]


def _chip_exchange_wait(send_sems, recv_sems, ss_thru, lands_thru, after, tag):
    n = len(ss_thru)

    def body(*refs):
        srcs, lands, send_sems, recv_sems = refs[:n], refs[n:2 * n], refs[2 * n], refs[2 * n + 1]
        for cp in _chip_copies(srcs, lands, send_sems, recv_sems):
            cp.wait_send()
            cp.wait_recv()

    hbm = [pltpu.HBM(a.shape, a.dtype) for a in list(ss_thru) + list(lands_thru)]
    out = pl.pallas_call(
        body, name="chip_exchange_wait_" + tag, out_shape=tuple(hbm),
        in_specs=[_HBM] * (2 * n) + [_SEM, _SEM, _ANY], out_specs=tuple([_HBM] * (2 * n)),
        input_output_aliases={k: k for k in range(2 * n)},
        compiler_params=pltpu.CompilerParams(has_side_effects=_DATAFLOW),
    )(*ss_thru, *lands_thru, send_sems, recv_sems, after)
    return list(out[n:])


def _reduce4(own, others, j, c, name):
    _, h, cols = own.shape
    tr = _rows_tile(h)
    nb = h // tr

    def body(idx_ref, s_ref, a0, a1, a2, o_ref):
        o_ref[...] = ((s_ref[0] + a0[0].astype(F32)) + a1[0].astype(F32)) + a2[0].astype(F32)

    def other(r):
        return pl.BlockSpec((1, tr, cols), lambda i, idx_ref: (r, i, 0))

    return pl.pallas_call(
        body, name=name,
        grid_spec=pltpu.PrefetchScalarGridSpec(
            num_scalar_prefetch=1, grid=(nb,),
            in_specs=[pl.BlockSpec((1, tr, cols), lambda i, idx_ref: (idx_ref[0], i, 0)), other(0), other(1), other(2)],
            out_specs=pl.BlockSpec((tr, cols), lambda i, idx_ref: (idx_ref[1] * nb + i, 0))),
        out_shape=jax.ShapeDtypeStruct((2 * h, cols), F32), compiler_params=_params(),
    )(jnp.stack([j, c]).astype(jnp.int32), own, others, others, others)


def _sibling_share(fulls):
    n = len(fulls)

    def body(*refs):
        outs, send_sems, recv_sems = refs[n:2 * n], refs[2 * n], refs[2 * n + 1]
        x, y, c, _ = _place()
        cps = [pltpu.make_async_remote_copy(
            src_ref=outs[k].at[_half(outs[k], c)], dst_ref=outs[k].at[_half(outs[k], c)], send_sem=send_sems.at[k],
            recv_sem=recv_sems.at[k], device_id=(x, y, 1 - c), device_id_type=_MESH) for k in range(n)]
        for cp in cps:
            cp.start()
        for cp in cps:
            cp.wait()

    return pl.pallas_call(
        body, name="sibling_share", out_shape=[jax.ShapeDtypeStruct(f.shape, F32) for f in fulls],
        in_specs=[_HBM] * n, out_specs=[_HBM] * n, input_output_aliases={k: k for k in range(n)},
        scratch_shapes=[pltpu.SemaphoreType.DMA((n,))] * 2,
    )(*fulls)


def _adamw(w, g, m, v):
    m1 = ADAM_B1 * m + (1.0 - ADAM_B1) * g
    v1 = ADAM_B2 * v + (1.0 - ADAM_B2) * (g * g)
    m_hat = m1 / (1.0 - ADAM_B1 ** ADAM_STEP)
    v_hat = v1 / (1.0 - ADAM_B2 ** ADAM_STEP)
    delta = -ADAM_LR * (m_hat / (jnp.sqrt(v_hat) + ADAM_EPS) + ADAM_WD * w)
    return delta, m1, v1


def _adamw_call(w, g, m, v, name):
    rows, cols = w.shape
    tr = _rows_tile(rows)

    def body(w_ref, g_ref, m_ref, v_ref, d_out, m_out, v_out):
        delta, m1, v1 = _adamw(w_ref[...], g_ref[...], m_ref[...], v_ref[...])
        d_out[...] = delta
        m_out[...] = m1
        v_out[...] = v1

    blk = pl.BlockSpec((tr, cols), lambda i: (i, 0))
    return pl.pallas_call(
        body, name=name, grid=(rows // tr,), in_specs=[blk] * 4, out_specs=[blk] * 3,
        out_shape=[jax.ShapeDtypeStruct((rows, cols), F32)] * 3, compiler_params=_params(),
    )(w, g, m, v)


def _small_allreduce(vals):
    def body(v_ref, out_ref, buf, send_sems, recv_sems):
        x, y, c, j = _place()
        me = 2 * j + c
        buf[0] = v_ref[...]

        def copy(r):
            return pltpu.make_async_remote_copy(
                src_ref=v_ref, dst_ref=buf.at[r], send_sem=send_sems.at[r - 1], recv_sem=recv_sems.at[r - 1],
                device_id=(x ^ (r >> 2), y ^ ((r >> 1) & 1), c ^ (r & 1)), device_id_type=_MESH)

        for r in range(1, 8):
            copy(r).start()
        for r in range(1, 8):
            copy(r).wait()
        acc = buf[me ^ 0]
        for d in range(1, 8):
            acc = acc + buf[me ^ d]
        out_ref[...] = acc

    return pl.pallas_call(
        body, name="small_allreduce", out_shape=jax.ShapeDtypeStruct((_SMALL_ROWS, D), F32),
        in_specs=[_VMEM], out_specs=_VMEM,
        scratch_shapes=[pltpu.VMEM((8, _SMALL_ROWS, D), F32), pltpu.SemaphoreType.DMA((7,)),
                        pltpu.SemaphoreType.DMA((7,))],
    )(vals)


_NAMES = ("norm_mix_g", "w_in", "conv_qk", "b_if", "mlstm_norm_g", "sinks", "w_branch_a", "w_branch_b", "w_out",
          "norm_mlp_g", "w_up", "w_down", "norm_ple_g", "w_ple_gate", "w_ple_proj", "final_norm_g")
_GROUP_NAMES = ("w_in", "w4", "w_up", "w_down", "w_ple_proj")


def _step(x, p, target, w, m, v):
    c = lax.axis_index("c")
    j = 2 * lax.axis_index("x") + lax.axis_index("y")

    def shards(d):
        return {n: d[n][0] for n in _SHARDED_NAMES}

    ws = shards(w)
    w_in_all, conv_all = _allgather_weights([ws["w_in"].astype(BF16)], ws["conv_qk"])
    rows_pp = PLE * (D // 4) // D
    rest = jnp.concatenate([ws[n] for n in _W4] + [ws["w_up"], ws["w_down"], ws["w_ple_proj"].reshape(rows_pp, D)],
                           axis=0)
    rest = (rest + 0.0 * conv_all[0, 0, 0]).astype(BF16)
    send_sems, recv_sems, rest_thru, land_thru, token = _late_gather_start(rest)
    full = {n: w[n] for n in ("mlstm_norm_g", "norm_mlp_g", "norm_ple_g", "b_if", "sinks")}
    full["norm_mix_g"] = w["norm_mix_g"] + token[0, 0]
    full["final_norm_g"] = w["final_norm_g"].reshape(1, D)
    full["w_in"] = _win_pad(jnp.swapaxes(w_in_all, 0, 1).reshape(D, N_IN))
    full["conv_qk"] = jnp.swapaxes(conv_all, 0, 1).reshape(CONV, D)

    def late_weights(after):
        land = _late_gather_wait(send_sems, recv_sems, rest_thru, land_thru, after)
        out = {n: land[:, i * (D // 4):(i + 1) * (D // 4)].reshape(D, D) for i, n in enumerate(_W4)}
        out["w_up"] = land[:, D:2 * D]
        out["w_down"] = land[:, 2 * D:3 * D].reshape(DFF, D)
        out["w_ple_proj"] = land[:, 3 * D:3 * D + rows_pp].reshape(4, PLE, D // 4)
        return out

    def pair_sums(by_dest, names, tag):
        theirs = _pair_exchange(by_dest, "pair_exchange_" + tag)
        return [_pair_sum(a, b, c, "pair_sum_" + n) for a, b, n in zip(by_dest, theirs, names)]

    early, last = {}, {}

    def early_grads(g):
        by_dest = [jnp.stack([g[n].reshape(4, D // 4, D) for n in _W4], axis=1).reshape(4, D, D),
                   g["w_up"], g["w_down"].reshape(4, DFF // 4, D), g["w_ple_proj"]]
        early["sums"] = pair_sums(by_dest, _GROUP_NAMES[1:], "early")
        *early["flight"], token = _chip_exchange_start([s[1] for s in early["sums"]], "early")
        return token[0, 0]

    def last_grad(g):
        w_in_g = _win_unpad(g["w_in"])
        last["sums"] = pair_sums([jnp.swapaxes(w_in_g.reshape(D, 4, N_IN // 4), 0, 1)], _GROUP_NAMES[:1], "w_in")
        *last["flight"], token = _chip_exchange_start([s[1] for s in last["sums"]], "w_in")
        return token[0, 0]

    loss, grad_x, g = _local_step(x[0], p[0, 0], target[0], full, late_weights, early_grads, last_grad)

    others = _chip_exchange_wait(*last["flight"], grad_x, "w_in")
    others += _chip_exchange_wait(*early["flight"], others[0], "early")
    sums = last["sums"] + early["sums"]
    halves = [_reduce4(s[0], b, j, c, "reduce4_" + n) for s, b, n in zip(sums, others, _GROUP_NAMES)]
    grads = _sibling_share(halves)

    small_g = _small_allreduce(_pack_small(g, extra=loss, conv=g["conv_qk"]))
    conv_g = lax.dynamic_slice(small_g[_CONV_ROW:_CONV_ROW + CONV], (0, j * (D // 4)), (CONV, D // 4))

    ms, vs = shards(m), shards(v)
    upd = [_adamw_call(wa, ga, ma, va, "adamw_" + n)
           for wa, ga, ma, va, n in zip(_group(ws), grads, _group(ms), _group(vs), _GROUP_NAMES)]
    conv_upd = _adamw_call(ws["conv_qk"], conv_g, ms["conv_qk"], vs["conv_qk"], "adamw_conv")
    small_upd = _adamw_call(_pack_small(w), small_g, _pack_small(m), _pack_small(v), "adamw_small")

    shapes = {n: w[n].shape for n in _NAMES}
    res = []
    for k in range(4):
        big = _ungroup(list(grads) if k == 0 else [u[k - 1] for u in upd])
        big["conv_qk"] = conv_g if k == 0 else conv_upd[k - 1]
        leaves = _unpack_small(small_g if k == 0 else small_upd[k - 1], shapes)
        leaves.update({n: a.reshape(shapes[n]) for n, a in big.items()})
        res.append(leaves)

    out = [small_g[5, 8 + SWH], grad_x[None]]
    for k in range(4):
        out += [res[k][n] for n in _NAMES]
    return tuple(out)


def kernel(x, p, norm_mix_g, w_in, conv_qk, b_if, mlstm_norm_g, sinks, w_branch_a, w_branch_b, w_out, norm_mlp_g, w_up, w_down, norm_ple_g, w_ple_gate, w_ple_proj, final_norm_g, loss_target, m_norm_mix_g, m_w_in, m_conv_qk, m_b_if, m_mlstm_norm_g, m_sinks, m_w_branch_a, m_w_branch_b, m_w_out, m_norm_mlp_g, m_w_up, m_w_down, m_norm_ple_g, m_w_ple_gate, m_w_ple_proj, m_final_norm_g, v_norm_mix_g, v_w_in, v_conv_qk, v_b_if, v_mlstm_norm_g, v_sinks, v_w_branch_a, v_w_branch_b, v_w_out, v_norm_mlp_g, v_w_up, v_w_down, v_norm_ple_g, v_w_ple_gate, v_w_ple_proj, v_final_norm_g):
    w = dict(zip(_NAMES, (norm_mix_g, w_in, conv_qk, b_if, mlstm_norm_g, sinks, w_branch_a, w_branch_b, w_out,
                          norm_mlp_g, w_up, w_down, norm_ple_g, w_ple_gate, w_ple_proj, final_norm_g)))
    m = dict(zip(_NAMES, (m_norm_mix_g, m_w_in, m_conv_qk, m_b_if, m_mlstm_norm_g, m_sinks, m_w_branch_a,
                          m_w_branch_b, m_w_out, m_norm_mlp_g, m_w_up, m_w_down, m_norm_ple_g, m_w_ple_gate,
                          m_w_ple_proj, m_final_norm_g)))
    v = dict(zip(_NAMES, (v_norm_mix_g, v_w_in, v_conv_qk, v_b_if, v_mlstm_norm_g, v_sinks, v_w_branch_a,
                          v_w_branch_b, v_w_out, v_norm_mlp_g, v_w_up, v_w_down, v_norm_ple_g, v_w_ple_gate,
                          v_w_ple_proj, v_final_norm_g)))
    return _step(x, p, loss_target, w, m, v)
```

```python
import jax
import jax.numpy as jnp
from jax import lax
from jax.experimental import pallas as pl
from jax.experimental.pallas import tpu as pltpu

F32 = jnp.float32
BF16 = jnp.bfloat16

D = 1024
PLE = 256
MLH = 4
DQK = 128
DV = 256
CONV = 4
CHUNK = 128
SWH = 16
SWKV = 4
SWG = SWH // SWKV
HD = 64
WIN = 128
DFF = 4096
EPS = 1e-6
N_IN = 6664
NP = 7168
C_QK, C_V, C_O, C_QSW, C_GA, C_GB, C_KV, C_IF = 0, 1024, 2048, 3072, 4096, 5120, 6144, 6656
IFW = NP - C_IF

ADAM_LR = 0.001
ADAM_B1 = 0.9
ADAM_B2 = 0.999
ADAM_EPS = 1e-08
ADAM_WD = 0.01
ADAM_STEP = 10

TOK_TILE = 512
VMEM_LIMIT = 58 * 1024 * 1024


def _params(**kw):
    return pltpu.CompilerParams(vmem_limit_bytes=VMEM_LIMIT, **kw)


def _pick(n, cap):
    if n <= cap:
        return n
    t = cap - cap % 128
    while t > 128 and n % t:
        t -= 128
    assert n % t == 0, (n, cap)
    return t


def _dot(a, b, dims):
    return lax.dot_general(a, b, (dims, ((), ())), preferred_element_type=F32)


def _dot_nn(a, b):
    return _dot(a, b, ((1,), (0,)))


def _dot_nt(a, b):
    return _dot(a, b, ((1,), (1,)))


def _dot_tn(a, b):
    return _dot(a, b, ((0,), (0,)))


def _sigmoid(x):
    return 1.0 / (1.0 + jnp.exp(-x))


def _mm(a, b, mode, out_dtype, name, out_chunks=1):
    bch = b.shape[0] if b.ndim == 3 else 1
    brows, bcols = b.shape[-2], b.shape[-1] * bch
    if mode == "nn":
        (m, k), (k2, n) = a.shape, (brows, bcols)
    elif mode == "nt":
        (m, k), (n, k2) = a.shape, (brows, bcols)
    else:
        (k, m), (k2, n) = a.shape, (brows, bcols)
    assert k == k2, (a.shape, b.shape, mode)
    n_cap = n // max(out_chunks, 1 if mode == "nt" else bch)
    k_cap = k // bch if mode == "nt" else k
    tm, tn, tk = _pick(m, 1024), _pick(n_cap, 1024), _pick(k_cap, 2048)
    nk = k // tk
    if mode == "nn":
        a_spec = pl.BlockSpec((tm, tk), lambda i, j, kk: (i, kk))
        if bch > 1:
            bpc = (n // bch) // tn
            b_spec = pl.BlockSpec((None, tk, tn), lambda i, j, kk: (j // bpc, kk, j % bpc))
        else:
            b_spec = pl.BlockSpec((tk, tn), lambda i, j, kk: (kk, j))
        dot = _dot_nn
    elif mode == "nt":
        a_spec = pl.BlockSpec((tm, tk), lambda i, j, kk: (i, kk))
        if bch > 1:
            bpc = (k // bch) // tk
            b_spec = pl.BlockSpec((None, tn, tk), lambda i, j, kk: (kk // bpc, j, kk % bpc))
        else:
            b_spec = pl.BlockSpec((tn, tk), lambda i, j, kk: (j, kk))
        dot = _dot_nt
    else:
        assert bch == 1
        a_spec = pl.BlockSpec((tk, tm), lambda i, j, kk: (kk, i))
        b_spec = pl.BlockSpec((tk, tn), lambda i, j, kk: (kk, j))
        dot = _dot_tn
    if out_chunks > 1:
        npc = (n // out_chunks) // tn
        out_spec = pl.BlockSpec((None, tm, tn), lambda i, j, kk: (j // npc, i, j % npc))
        out_shape = jax.ShapeDtypeStruct((out_chunks, m, n // out_chunks), out_dtype)
    else:
        out_spec = pl.BlockSpec((tm, tn), lambda i, j, kk: (i, j))
        out_shape = jax.ShapeDtypeStruct((m, n), out_dtype)

    def body(a_ref, b_ref, o_ref, acc_ref):
        kk = pl.program_id(2)

        @pl.when(kk == 0)
        def _():
            acc_ref[...] = jnp.zeros_like(acc_ref)

        acc_ref[...] += dot(a_ref[...], b_ref[...])

        @pl.when(kk == nk - 1)
        def _():
            o_ref[...] = acc_ref[...].astype(out_dtype)

    return pl.pallas_call(
        body, name=name, grid=(m // tm, n // tn, nk),
        in_specs=[a_spec, b_spec], out_specs=out_spec, out_shape=out_shape,
        scratch_shapes=[pltpu.VMEM((tm, tn), F32)],
        compiler_params=_params(dimension_semantics=("parallel", "parallel", "arbitrary")),
    )(a, b)


def _tile(col0=0):
    return lambda tm, tn: pl.BlockSpec((tm, tn), lambda i, j, kk: (i, col0 // tn + j))


def _row():
    return lambda tm, tn: pl.BlockSpec((1, tn), lambda i, j, kk: (0, j))


def _mm_ep(pairs, mode, name, epilogue, ins, outs, tm, tn, aliases=None):
    a0, b0 = pairs[0]
    bch = b0.shape[0] if b0.ndim == 3 else 1
    m, k = a0.shape
    tm = _pick(m, tm)
    n = b0.shape[-1] * bch if mode == "nn" else b0.shape[-2]
    tk = _pick(k // bch if mode == "nt" else k, 2048)
    nk = k // tk
    a_spec = pl.BlockSpec((tm, tk), lambda i, j, kk: (i, kk))
    if mode == "nn":
        dot = _dot_nn
        if bch > 1:
            bpc = (n // bch) // tn
            b_spec = pl.BlockSpec((None, tk, tn), lambda i, j, kk: (j // bpc, kk, j % bpc))
        else:
            b_spec = pl.BlockSpec((tk, tn), lambda i, j, kk: (kk, j))
    else:
        dot = _dot_nt
        if bch > 1:
            bpc = (k // bch) // tk
            b_spec = pl.BlockSpec((None, tn, tk), lambda i, j, kk: (kk // bpc, j, kk % bpc))
        else:
            b_spec = pl.BlockSpec((tn, tk), lambda i, j, kk: (j, kk))
    npair, nin, nout = len(pairs), len(ins), len(outs)

    def body(*refs):
        ab = refs[:2 * npair]
        in_refs = refs[2 * npair:2 * npair + nin]
        out_refs = refs[2 * npair + nin:2 * npair + nin + nout]
        accs = refs[2 * npair + nin + nout:]
        i, j, kk = pl.program_id(0), pl.program_id(1), pl.program_id(2)
        for p in range(npair):
            prod = dot(ab[2 * p][...], ab[2 * p + 1][...])

            @pl.when(kk == 0)
            def _():
                accs[p][...] = prod

            @pl.when(kk > 0)
            def _():
                accs[p][...] += prod

        @pl.when(kk == nk - 1)
        def _():
            epilogue([acc[...] for acc in accs], in_refs, out_refs, i, j)

    operands = [x for pair in pairs for x in pair] + [a for a, _ in ins]
    io_alias = {2 * npair + i: o for i, o in (aliases or {}).items()}
    return pl.pallas_call(
        body, name=name, grid=(m // tm, n // tn, nk),
        in_specs=[a_spec, b_spec] * npair + [mk(tm, tn) for _, mk in ins],
        out_specs=[mk(tm, tn) for _, mk in outs], out_shape=[s for s, _ in outs],
        scratch_shapes=[pltpu.VMEM((tm, tn), F32)] * npair, input_output_aliases=io_alias,
        compiler_params=_params(dimension_semantics=("arbitrary", "arbitrary", "arbitrary")),
    )(*operands)


def _tok(w, j=0):
    return pl.BlockSpec((TOK_TILE, w), lambda i: (i, j))


def _rep(shape):
    return pl.BlockSpec(shape, lambda i: (0,) * len(shape))


def _rms(x):
    rstd = lax.rsqrt(jnp.mean(x * x, axis=-1, keepdims=True) + EPS)
    return x * rstd, rstd


def _rms_bwd(xn, rstd, dxn):
    return rstd * (dxn - xn * jnp.mean(dxn * xn, axis=-1, keepdims=True))


def _norm_fwd(x, g, name):
    t = x.shape[0]

    def body(x_ref, g_ref, h_ref):
        xn, _ = _rms(x_ref[...])
        h_ref[...] = (xn * g_ref[...]).astype(BF16)

    return pl.pallas_call(
        body, name=name, grid=(t // TOK_TILE,), in_specs=[_tok(D), _rep((1, D))], out_specs=_tok(D),
        out_shape=jax.ShapeDtypeStruct((t, D), BF16), compiler_params=_params(),
    )(x, g)


def _halo_prev(w, j=0, rows=8):
    r = TOK_TILE // rows
    return pl.BlockSpec((rows, w), lambda i: (jnp.maximum(i * r - 1, 0), j))


def _last8(halo_ref):
    return halo_ref[...].astype(F32)[halo_ref.shape[0] - 8:]


def _halo_next(w, nt, j=0):
    r = TOK_TILE // 8
    return pl.BlockSpec((8, w), lambda i: (jnp.minimum((i + 1) * r, nt * r - 1), j))


def _shift_down(x, halo, s):
    if s == 0:
        return x
    r = pltpu.roll(x, s, 0)
    hs = pltpu.roll(halo, s, 0)
    row = lax.broadcasted_iota(jnp.int32, hs.shape, 0)
    top = jnp.where(row < s, hs, r[0:8])
    return jnp.concatenate([top, r[8:]], axis=0)


def _shift_up(x, halo, s):
    if s == 0:
        return x
    n = x.shape[0]
    r = pltpu.roll(x, n - s, 0)
    hs = pltpu.roll(halo, 8 - s, 0)
    row = lax.broadcasted_iota(jnp.int32, hs.shape, 0)
    bot = jnp.where(row >= 8 - s, hs, r[n - 8:])
    return jnp.concatenate([r[:n - 8], bot], axis=0)


def _bf(x):
    return x.astype(BF16).astype(F32)


def _conv_taps(x, halo, w):
    x, halo, w = _bf(x), _bf(halo), _bf(w)
    acc = x * w[CONV - 1:CONV, :]
    for j in range(CONV - 1):
        acc = acc + _shift_down(x, halo, CONV - 1 - j) * w[j:j + 1, :]
    return acc


_Q_SCALE = DQK ** -0.5


def _qscale_row():
    lane = lax.broadcasted_iota(jnp.int32, (1, D), 1)
    return jnp.where(lane < MLH * DQK, _Q_SCALE, 1.0).astype(F32)


def _conv_silu_fwd(proj, conv_w):
    t = proj.shape[0]

    def body(x_ref, halo_ref, w_ref, o_ref):
        halo = jnp.where(pl.program_id(0) > 0, _last8(halo_ref), 0.0)
        c = _conv_taps(x_ref[...].astype(F32), halo, w_ref[...])
        o_ref[...] = (c * _sigmoid(c) * _qscale_row()).astype(BF16)

    return pl.pallas_call(
        body, name="conv_silu_fwd", grid=(t // TOK_TILE,),
        in_specs=[_tok(D, C_QK // D), _halo_prev(D, C_QK // D, 16), _rep((CONV, D))], out_specs=_tok(D),
        out_shape=jax.ShapeDtypeStruct((t, D), BF16), compiler_params=_params(),
    )(proj, proj, conv_w)


def _conv_silu_bwd_a(proj, conv_w, dqk):
    t = proj.shape[0]

    def body(x_ref, halo_ref, w_ref, d_ref, dc_ref, dw_ref):
        @pl.when(pl.program_id(0) == 0)
        def _():
            dw_ref[...] = jnp.zeros_like(dw_ref)

        halo = jnp.where(pl.program_id(0) > 0, _last8(halo_ref), 0.0)
        x = x_ref[...].astype(F32)
        c = _conv_taps(x, halo, w_ref[...])
        s = _sigmoid(c)
        dc = d_ref[...] * _qscale_row() * (s * (1.0 + c * (1.0 - s)))
        dc_ref[...] = dc
        dcb, xb, halo_b = _bf(dc), _bf(x), _bf(halo)
        for j in range(CONV):
            dw_ref[j:j + 1, :] += jnp.sum(dcb * _shift_down(xb, halo_b, CONV - 1 - j), axis=0, keepdims=True)

    return pl.pallas_call(
        body, name="conv_silu_bwd_a", grid=(t // TOK_TILE,),
        in_specs=[_tok(D, C_QK // D), _halo_prev(D, C_QK // D, 16), _rep((CONV, D)), _tok(D)],
        out_specs=[_tok(D), _rep((CONV, D))],
        out_shape=[jax.ShapeDtypeStruct((t, D), F32), jax.ShapeDtypeStruct((CONV, D), F32)],
        compiler_params=_params(),
    )(proj, proj, conv_w, dqk)


def _conv_silu_bwd_b(dc, conv_w, dproj):
    t = dc.shape[0]
    nt = t // TOK_TILE

    def body(dc_ref, halo_ref, w_ref, _, dx_ref):
        halo = _bf(jnp.where(pl.program_id(0) < nt - 1, halo_ref[...], 0.0))
        dcv = _bf(dc_ref[...])
        w = _bf(w_ref[...])
        acc = dcv * w[CONV - 1:CONV, :]
        for j in range(CONV - 1):
            acc = acc + _shift_up(dcv, halo, CONV - 1 - j) * w[j:j + 1, :]
        dx_ref[...] = acc.astype(BF16)

    return pl.pallas_call(
        body, name="conv_silu_bwd_b", grid=(nt,), in_specs=[_tok(D), _halo_next(D, nt), _rep((CONV, D)), _ANY],
        out_specs=_tok(D, C_QK // D), out_shape=jax.ShapeDtypeStruct((t, NP), BF16),
        input_output_aliases={3: 0}, compiler_params=_params(),
    )(dc, dc, conv_w, dproj)


def _gates_fwd(pre_rows, bias_col):
    t = pre_rows.shape[1]

    def body(p_ref, b_ref, g_ref, s_ref):
        z = p_ref[...] + b_ref[...]
        lf = jnp.minimum(z, 0.0) - jnp.log(1.0 + jnp.exp(-jnp.abs(z)))
        lane = lax.broadcasted_iota(jnp.int32, z.shape, 1) % CHUNK
        cum = lf
        s = 1
        while s < CHUNK:
            cum = cum + jnp.where(lane >= s, pltpu.roll(cum, s, 1), 0.0)
            s *= 2
        sub = lax.broadcasted_iota(jnp.int32, z.shape, 0)
        g_ref[...] = jnp.where(sub < MLH, z, cum)
        s_ref[...] = _sigmoid(-z)

    return pl.pallas_call(
        body, name="gates_fwd",
        out_shape=[jax.ShapeDtypeStruct((8, t), F32), jax.ShapeDtypeStruct((8, t), F32)],
        compiler_params=_params(),
    )(pre_rows, bias_col)


def _chunk_terms(grow, gcol, h, m0):
    i_row, b_row = grow[h:h + 1, :], grow[MLH + h:MLH + h + 1, :]
    i_col, b_col = gcol[:, h:h + 1], gcol[:, MLH + h:MLH + h + 1]
    b_last = b_row[:, CHUNK - 1:CHUNK]
    tt = lax.broadcasted_iota(jnp.int32, (CHUNK, CHUNK), 0)
    ss = lax.broadcasted_iota(jnp.int32, (CHUNK, CHUNK), 1)
    log_d = jnp.where(tt >= ss, b_col - b_row + i_row, -jnp.inf)
    m_t = jnp.maximum(b_col + m0, jnp.max(log_d, axis=1, keepdims=True))
    dm = jnp.exp(log_d - m_t)
    wi = jnp.exp(b_col + m0 - m_t)
    m1 = jnp.maximum(b_last + m0, jnp.max(b_last - b_row + i_row, axis=1, keepdims=True))
    ws = jnp.exp(b_last - b_col + i_col - m1)
    dec = jnp.exp(b_last + m0 - m1)
    return dm, wi, m_t, ws, dec, m1


def _mlstm_fwd(qk, proj, grow, gcol, gain):
    t = qk.shape[0]
    nc = t // CHUNK

    def body(qk_ref, v_ref, o_ref, grow_ref, gcol_ref, g_ref, h_ref, y_ref, cs_ref, st_ref, c_scr, st_scr):
        @pl.when(pl.program_id(0) == 0)
        def _():
            c_scr[...] = jnp.zeros_like(c_scr)
            st_scr[...] = jnp.zeros_like(st_scr)

        grow_v, gcol_v = grow_ref[...], gcol_ref[...]
        for h in range(MLH):
            q = qk_ref[:, h * DQK:(h + 1) * DQK]
            k = qk_ref[:, MLH * DQK + h * DQK:MLH * DQK + (h + 1) * DQK]
            v = v_ref[:, h * DV:(h + 1) * DV]
            c0 = c_scr[h]
            n0 = st_scr[h, 0:1, :]
            m0 = st_scr[h, 1:2, 0:1]
            cs_ref[0, h] = c0
            st_ref[0, h] = st_scr[h]
            dm, wi, m_t, ws, dec, m1 = _chunk_terms(grow_v, gcol_v, h, m0)
            s = _dot_nt(q, k) * dm
            num = wi * _dot_nt(q, c0.astype(BF16)) + _dot_nn(s.astype(BF16), v.astype(BF16))
            den = wi * jnp.sum(q.astype(F32) * n0, axis=1, keepdims=True) + jnp.sum(s, axis=1, keepdims=True)
            sl = slice(h * DV, (h + 1) * DV)
            hv = num / jnp.maximum(jnp.abs(den), jnp.exp(-m_t))
            h_ref[:, sl] = hv
            xn, _ = _rms(hv)
            y_ref[:, sl] = (_sigmoid(o_ref[:, sl].astype(F32)) * xn * g_ref[:, sl]).astype(BF16)
            c_scr[h] = dec * c0 + _dot_tn((ws * v).astype(BF16), k)
            st_scr[h, 0:1, :] = dec * n0 + jnp.sum(ws * k.astype(F32), axis=0, keepdims=True)
            st_scr[h, 1:2, :] = jnp.broadcast_to(m1, (1, DQK))

    return pl.pallas_call(
        body, name="mlstm_fwd", grid=(nc,),
        in_specs=[pl.BlockSpec((CHUNK, D), lambda c: (c, 0)), pl.BlockSpec((CHUNK, D), lambda c: (c, C_V // D)),
                  pl.BlockSpec((CHUNK, D), lambda c: (c, C_O // D)),
                  pl.BlockSpec((8, CHUNK), lambda c: (0, c)), pl.BlockSpec((CHUNK, 8), lambda c: (c, 0)),
                  pl.BlockSpec((1, D), lambda c: (0, 0))],
        out_specs=[pl.BlockSpec((CHUNK, D), lambda c: (c, 0)), pl.BlockSpec((CHUNK, D), lambda c: (c, 0)),
                   pl.BlockSpec((1, MLH, DV, DQK), lambda c: (c, 0, 0, 0)),
                   pl.BlockSpec((1, MLH, 8, DQK), lambda c: (c, 0, 0, 0))],
        out_shape=[jax.ShapeDtypeStruct((t, D), F32), jax.ShapeDtypeStruct((t, D), BF16),
                   jax.ShapeDtypeStruct((nc, MLH, DV, DQK), F32), jax.ShapeDtypeStruct((nc, MLH, 8, DQK), F32)],
        scratch_shapes=[pltpu.VMEM((MLH, DV, DQK), F32), pltpu.VMEM((MLH, 8, DQK), F32)],
        compiler_params=_params(dimension_semantics=("arbitrary",)),
    )(qk, proj, proj, grow, gcol, gain)


def _mlstm_bwd(qk, proj, grow, gcol, sneg_col, cs, st, hraw, dh, dproj):
    t = qk.shape[0]
    nc = t // CHUNK

    def rev(c):
        return nc - 1 - c

    def nxt(c):
        return jnp.minimum(nc - c, nc - 1)

    def body(qk_ref, v_ref, grow_ref, gcol_ref, sneg_ref, cs_ref, st_ref, cs1_ref, st1_ref, h_ref, dh_ref, _,
             dqk_ref, dv_ref, dif_ref, dbif_ref, dc_scr, dn_scr):
        @pl.when(pl.program_id(0) == 0)
        def _():
            dc_scr[...] = jnp.zeros_like(dc_scr)
            dn_scr[...] = jnp.zeros_like(dn_scr)
            dbif_ref[...] = jnp.zeros_like(dbif_ref)

        grow_v, gcol_v, sneg = grow_ref[...], gcol_ref[...], sneg_ref[...]
        tt = lax.broadcasted_iota(jnp.int32, (CHUNK, CHUNK), 0)
        ss = lax.broadcasted_iota(jnp.int32, (CHUNK, CHUNK), 1)
        lane8 = lax.broadcasted_iota(jnp.int32, (CHUNK, 8), 1)
        dif = jnp.zeros((CHUNK, 8), F32)
        for h in range(MLH):
            q = qk_ref[:, h * DQK:(h + 1) * DQK]
            k = qk_ref[:, MLH * DQK + h * DQK:MLH * DQK + (h + 1) * DQK]
            qf, kf = q.astype(F32), k.astype(F32)
            v = v_ref[:, h * DV:(h + 1) * DV]
            vb = v.astype(BF16)
            c0 = cs_ref[0, h]
            n0 = st_ref[0, h, 0:1, :]
            m0 = st_ref[0, h, 1:2, 0:1]
            dc1 = dc_scr[h]
            dn1 = dn_scr[h, 0:1, :]
            dm, wi, m_t, ws, dec, _ = _chunk_terms(grow_v, gcol_v, h, m0)
            s = _dot_nt(q, k) * dm
            den = wi * jnp.sum(qf * n0, axis=1, keepdims=True) + jnp.sum(s, axis=1, keepdims=True)
            floor = jnp.exp(-m_t)
            g = jnp.maximum(jnp.abs(den), floor)
            dh_v = dh_ref[:, h * DV:(h + 1) * DV]
            dnum = dh_v / g
            dden = -jnp.sum(dh_v * h_ref[:, h * DV:(h + 1) * DV], axis=1, keepdims=True) / g
            dden = jnp.where(jnp.abs(den) > floor, dden * jnp.sign(den), 0.0)
            dnum_b = dnum.astype(BF16)
            da = ((_dot_nt(dnum_b, vb) + dden) * dm).astype(BF16)
            dc1_b = dc1.astype(BF16)
            dq = _dot_nn(da, k) + wi * (_dot_nn(dnum_b, c0.astype(BF16)) + dden * n0)
            dk = _dot_tn(da, q) + ws * (_dot_nn(vb, dc1_b) + dn1)
            dv = _dot_tn(s.astype(BF16), dnum_b) + ws * _dot_nt(k, dc1_b)
            dqk_ref[:, h * DQK:(h + 1) * DQK] = dq
            dqk_ref[:, MLH * DQK + h * DQK:MLH * DQK + (h + 1) * DQK] = dk
            dv_ref[:, h * DV:(h + 1) * DV] = dv.astype(BF16)
            rk = jnp.sum(kf * dk, axis=1, keepdims=True)
            df = jnp.sum(qf * dq, axis=1, keepdims=True) - rk
            df_row = jnp.sum(jnp.where(tt == ss, df, 0.0), axis=0, keepdims=True)
            suffix = jnp.sum(jnp.where(ss >= tt, df_row, 0.0), axis=1, keepdims=True)
            cross = (jnp.sum(jnp.sum(dc1 * cs1_ref[0, h], axis=1, keepdims=True), axis=0, keepdims=True)
                     + jnp.sum(dn1 * st1_ref[0, h, 0:1, :], axis=1, keepdims=True))
            dpf = (suffix + cross) * sneg[:, MLH + h:MLH + h + 1]
            dif = dif + jnp.where(lane8 == h, rk, 0.0) + jnp.where(lane8 == MLH + h, dpf, 0.0)
            dc_scr[h] = dec * dc1 + _dot_tn((wi * dnum).astype(BF16), q)
            dn_scr[h, 0:1, :] = dec * dn1 + jnp.sum(wi * dden * qf, axis=0, keepdims=True)
        dif_ref[...] = dif
        dbif_ref[...] += jnp.sum(dif, axis=0, keepdims=True)

    return pl.pallas_call(
        body, name="mlstm_bwd", grid=(nc,),
        in_specs=[pl.BlockSpec((CHUNK, D), lambda c: (rev(c), 0)),
                  pl.BlockSpec((CHUNK, D), lambda c: (rev(c), C_V // D)),
                  pl.BlockSpec((8, CHUNK), lambda c: (0, rev(c))),
                  pl.BlockSpec((CHUNK, 8), lambda c: (rev(c), 0)),
                  pl.BlockSpec((CHUNK, 8), lambda c: (rev(c), 0)),
                  pl.BlockSpec((1, MLH, DV, DQK), lambda c: (rev(c), 0, 0, 0)),
                  pl.BlockSpec((1, MLH, 8, DQK), lambda c: (rev(c), 0, 0, 0)),
                  pl.BlockSpec((1, MLH, DV, DQK), lambda c: (nxt(c), 0, 0, 0)),
                  pl.BlockSpec((1, MLH, 8, DQK), lambda c: (nxt(c), 0, 0, 0)),
                  pl.BlockSpec((CHUNK, D), lambda c: (rev(c), 0)),
                  pl.BlockSpec((CHUNK, D), lambda c: (rev(c), 0)), _ANY],
        out_specs=[pl.BlockSpec((CHUNK, D), lambda c: (rev(c), 0)),
                   pl.BlockSpec((CHUNK, D), lambda c: (rev(c), C_V // D)),
                   pl.BlockSpec((CHUNK, 8), lambda c: (rev(c), 0)),
                   pl.BlockSpec((1, 8), lambda c: (0, 0))],
        out_shape=[jax.ShapeDtypeStruct((t, D), F32), jax.ShapeDtypeStruct((t, NP), BF16),
                   jax.ShapeDtypeStruct((t, 8), F32), jax.ShapeDtypeStruct((1, 8), F32)],
        scratch_shapes=[pltpu.VMEM((MLH, DV, DQK), F32), pltpu.VMEM((MLH, 8, DQK), F32)],
        input_output_aliases={11: 1}, compiler_params=_params(dimension_semantics=("arbitrary",)),
    )(qk, proj, grow, gcol, sneg_col, cs, st, cs, st, hraw, dh, dproj)


_ANY = pl.BlockSpec(memory_space=pl.ANY)


_SW_SCALE = HD ** -0.5
_KVB = C_KV // (2 * SWKV * HD)


def _swa_mask(n):
    ki = lax.broadcasted_iota(jnp.int32, (2 * WIN, SWG * WIN), 0)
    qi = lax.broadcasted_iota(jnp.int32, (2 * WIN, SWG * WIN), 1) % WIN
    return (ki > qi) & (ki <= qi + WIN) & ((n > 0) | (ki >= WIN))


def _group_rows(x_ref, hk):
    return jnp.concatenate([x_ref[:, (hk * SWG + g) * HD:(hk * SWG + g + 1) * HD] for g in range(SWG)], axis=0)


def _group_lanes(x_ref, hk):
    return jnp.concatenate([x_ref[hk * SWG + g:hk * SWG + g + 1, :] for g in range(SWG)], axis=1)


def _sink_lanes(sink_ref, hk):
    return jnp.concatenate([jnp.broadcast_to(sink_ref[:, hk * SWG + g:hk * SWG + g + 1], (1, WIN))
                            for g in range(SWG)], axis=1)


def _swa_fwd(proj, sinks):
    t = proj.shape[0]
    nb = t // WIN

    def body(q_ref, kvc_ref, kvp_ref, sink_ref, y_ref, lse_ref):
        valid = _swa_mask(pl.program_id(0))
        for hk in range(SWKV):
            ks = slice(hk * HD, (hk + 1) * HD)
            vs = slice(SWKV * HD + hk * HD, SWKV * HD + (hk + 1) * HD)
            kb = jnp.concatenate([kvp_ref[:, ks], kvc_ref[:, ks]], axis=0).astype(BF16)
            vb = jnp.concatenate([kvp_ref[:, vs], kvc_ref[:, vs]], axis=0).astype(BF16)
            q4 = _group_rows(q_ref, hk).astype(BF16)
            sink = _sink_lanes(sink_ref, hk)
            logits = jnp.where(valid, _dot_nt(kb, q4) * _SW_SCALE, -jnp.inf)
            m = jnp.maximum(jnp.max(logits, axis=0, keepdims=True), sink)
            p = jnp.exp(logits - m)
            denom = jnp.sum(p, axis=0, keepdims=True) + jnp.exp(sink - m)
            y4 = _dot_tn((p / denom).astype(BF16), vb).astype(BF16)
            lse4 = m + jnp.log(denom)
            for g in range(SWG):
                hq = hk * SWG + g
                y_ref[:, hq * HD:(hq + 1) * HD] = y4[g * WIN:(g + 1) * WIN]
                lse_ref[hq:hq + 1, :] = lse4[:, g * WIN:(g + 1) * WIN]

    return pl.pallas_call(
        body, name="swa_fwd", grid=(nb,),
        in_specs=[pl.BlockSpec((WIN, D), lambda n: (n, C_QSW // D)),
                  pl.BlockSpec((WIN, 512), lambda n: (n, _KVB)),
                  pl.BlockSpec((WIN, 512), lambda n: (jnp.maximum(n - 1, 0), _KVB)),
                  pl.BlockSpec((1, SWH), lambda n: (0, 0))],
        out_specs=[pl.BlockSpec((WIN, D), lambda n: (n, 0)), pl.BlockSpec((SWH, WIN), lambda n: (0, n))],
        out_shape=[jax.ShapeDtypeStruct((t, D), BF16), jax.ShapeDtypeStruct((SWH, t), F32)],
        compiler_params=_params(),
    )(proj, proj, proj, sinks)


def _swa_bwd(proj, sinks, lse, dyb, dproj):
    t = proj.shape[0]
    nb = t // WIN

    def body(q_ref, kvc_ref, kvp_ref, sink_ref, lse_ref, dy_ref, _, dq_ref, dself_ref, dprev_ref, ds_ref):
        @pl.when(pl.program_id(0) == 0)
        def _():
            ds_ref[...] = jnp.zeros_like(ds_ref)

        valid = _swa_mask(pl.program_id(0))
        for hk in range(SWKV):
            ks = slice(hk * HD, (hk + 1) * HD)
            vs = slice(SWKV * HD + hk * HD, SWKV * HD + (hk + 1) * HD)
            kb = jnp.concatenate([kvp_ref[:, ks], kvc_ref[:, ks]], axis=0).astype(BF16)
            vb = jnp.concatenate([kvp_ref[:, vs], kvc_ref[:, vs]], axis=0).astype(BF16)
            dy4 = _group_rows(dy_ref, hk)
            qb, dyb_ = _group_rows(q_ref, hk).astype(BF16), dy4.astype(BF16)
            lse4 = _group_lanes(lse_ref, hk)
            logits = jnp.where(valid, _dot_nt(kb, qb) * _SW_SCALE, -jnp.inf)
            p = jnp.exp(logits - lse4)
            dpt = _dot_nt(vb, dyb_)
            delta = jnp.sum(p * dpt, axis=0, keepdims=True)
            dsm = (p * (dpt - delta)).astype(BF16)
            dq4 = (_dot_tn(dsm, kb) * _SW_SCALE).astype(BF16)
            dkb = _dot_nn(dsm, qb) * _SW_SCALE
            dvb = _dot_nn(p.astype(BF16), dyb_)
            dsink4 = jnp.exp(_sink_lanes(sink_ref, hk) - lse4) * delta
            for g in range(SWG):
                hq = hk * SWG + g
                dq_ref[:, hq * HD:(hq + 1) * HD] = dq4[g * WIN:(g + 1) * WIN]
                ds_ref[:, hq:hq + 1] += -jnp.sum(dsink4[:, g * WIN:(g + 1) * WIN], axis=1, keepdims=True)
            dprev_ref[:, ks] = dkb[:WIN]
            dself_ref[:, ks] = dkb[WIN:]
            dprev_ref[:, vs] = dvb[:WIN]
            dself_ref[:, vs] = dvb[WIN:]

    return pl.pallas_call(
        body, name="swa_bwd", grid=(nb,),
        in_specs=[pl.BlockSpec((WIN, D), lambda n: (n, C_QSW // D)),
                  pl.BlockSpec((WIN, 512), lambda n: (n, _KVB)),
                  pl.BlockSpec((WIN, 512), lambda n: (jnp.maximum(n - 1, 0), _KVB)),
                  pl.BlockSpec((1, SWH), lambda n: (0, 0)),
                  pl.BlockSpec((SWH, WIN), lambda n: (0, n)),
                  pl.BlockSpec((WIN, D), lambda n: (n, 0)), _ANY],
        out_specs=[pl.BlockSpec((WIN, D), lambda n: (n, C_QSW // D)), pl.BlockSpec((WIN, 512), lambda n: (n, 0)),
                   pl.BlockSpec((WIN, 512), lambda n: (jnp.maximum(n - 1, 0), 0)),
                   pl.BlockSpec((1, SWH), lambda n: (0, 0))],
        out_shape=[jax.ShapeDtypeStruct((t, NP), BF16), jax.ShapeDtypeStruct((t, 512), F32),
                   jax.ShapeDtypeStruct((t, 512), F32), jax.ShapeDtypeStruct((1, SWH), F32)],
        input_output_aliases={6: 0}, compiler_params=_params(),
    )(proj, proj, proj, sinks, lse, dyb, dproj)


def _kv_combine(dself, dnext, dif, dproj):
    t = dself.shape[0]
    rows = _pick(t, 512)

    def body(a_ref, b_ref, dif_ref, _, o_ref):
        row = pl.program_id(0) * rows + lax.broadcasted_iota(jnp.int32, (rows, 1), 0)
        o_ref[:, 0:512] = (a_ref[...] + jnp.where(row < t - WIN, b_ref[...], 0.0)).astype(BF16)
        lane = lax.broadcasted_iota(jnp.int32, (rows, 128), 1)
        dif_v = dif_ref[...]
        first = jnp.zeros((rows, 128), F32)
        for col in range(8):
            first = first + jnp.where(lane == col, dif_v[:, col:col + 1], 0.0)
        o_ref[:, 512:640] = first.astype(BF16)
        o_ref[:, 640:512 + IFW] = jnp.zeros((rows, IFW - 128), BF16)

    return pl.pallas_call(
        body, name="kv_combine", grid=(t // rows,),
        in_specs=[pl.BlockSpec((rows, 512), lambda n: (n, 0)), pl.BlockSpec((rows, 512), lambda n: (n, 0)),
                  pl.BlockSpec((rows, 8), lambda n: (n, 0)), _ANY],
        out_specs=pl.BlockSpec((rows, 512 + IFW), lambda n: (n, C_KV // (512 + IFW))),
        out_shape=jax.ShapeDtypeStruct((t, NP), BF16), input_output_aliases={3: 0}, compiler_params=_params(),
    )(dself, dnext, dif, dproj)


def _sds(t, n, dtype):
    return jax.ShapeDtypeStruct((t, n), dtype)


def _proj_in(h0, w_in):
    t = h0.shape[0]
    tn = 2 * IFW

    def epilogue(accs, ins, outs, i, j):
        outs[0][...] = accs[0].astype(BF16)

        @pl.when(j == C_IF // tn)
        def _():
            outs[1][...] = accs[0][:, C_IF % tn:C_IF % tn + 128]

    gate_cols = lambda tm, tn: pl.BlockSpec((tm, 128), lambda i, j, kk: (i, 0))
    return _mm_ep([(h0, w_in)], "nn", "mm_in", epilogue, [],
                  [(_sds(t, NP, BF16), _tile()), (_sds(t, 128, F32), gate_cols)], 1024, tn)


def _branch_merge(ya, yb, wa, wb, proj):
    t = ya.shape[0]

    def epilogue(accs, ins, outs, i, j):
        za, zb = accs
        merged = _sigmoid(ins[0][...].astype(F32)) * za + _sigmoid(ins[1][...].astype(F32)) * zb
        outs[0][...] = merged.astype(BF16)
        outs[1][...] = za.astype(BF16)
        outs[2][...] = zb.astype(BF16)

    return _mm_ep([(ya, wa), (yb, wb)], "nn", "mm_branch_merge", epilogue, [(proj, _tile(C_GA)), (proj, _tile(C_GB))],
                  [(_sds(t, D, BF16), _tile())] * 3, 1024, 512)


def _dmerged_bwd(dxb, w_out, proj, za, zb):
    t = dxb.shape[0]

    def epilogue(accs, ins, outs, i, j):
        dm = accs[0]
        sa, sb = _sigmoid(ins[0][...].astype(F32)), _sigmoid(ins[1][...].astype(F32))
        outs[0][...] = (dm * sa).astype(BF16)
        outs[1][...] = (dm * sb).astype(BF16)
        outs[2][:, 0:D] = (dm * ins[2][...].astype(F32) * sa * (1.0 - sa)).astype(BF16)
        outs[2][:, D:2 * D] = (dm * ins[3][...].astype(F32) * sb * (1.0 - sb)).astype(BF16)

    gate_cols = lambda tm, tn: pl.BlockSpec((tm, 2 * D), lambda i, j, kk: (i, C_GA // (2 * D)))
    return _mm_ep([(dxb, w_out)], "nt", "mm_dmerged_bwd", epilogue,
                  [(proj, _tile(C_GA)), (proj, _tile(C_GB)), (za, _tile()), (zb, _tile())],
                  [(_sds(t, D, BF16), _tile()), (_sds(t, D, BF16), _tile()), (_sds(t, NP, BF16), gate_cols)], 1024, D)


def _dya_bwd(dza, wa, hraw, proj, g, dproj):
    t = dza.shape[0]

    def epilogue(accs, ins, outs, i, j):
        h_ref, o_ref, g_ref, _ = ins
        dh_ref, do_ref, dg_ref = outs

        @pl.when(i == 0)
        def _():
            dg_ref[...] = jnp.zeros_like(dg_ref)

        dy = accs[0]
        so = _sigmoid(o_ref[...].astype(F32))
        for h in range(MLH):
            sl = slice(h * DV, (h + 1) * DV)
            xn, rstd = _rms(h_ref[:, sl])
            gs = g_ref[:, sl]
            do_ref[:, sl] = (dy[:, sl] * xn * gs * so[:, sl] * (1.0 - so[:, sl])).astype(BF16)
            dhn = dy[:, sl] * so[:, sl]
            dg_ref[:, sl] += jnp.sum(dhn * xn, axis=0, keepdims=True)
            dh_ref[:, sl] = _rms_bwd(xn, rstd, dhn * gs)

    return _mm_ep([(dza, wa)], "nt", "mm_dya_bwd", epilogue,
                  [(hraw, _tile()), (proj, _tile(C_O)), (g, _row()), (dproj, lambda tm, tn: _ANY)],
                  [(_sds(t, D, F32), _tile()), (_sds(t, NP, BF16), _tile(C_O)), (_sds(1, D, F32), _row())],
                  1024, D, aliases={3: 1})


def _up_act(hn, w_up):
    t = hn.shape[0]

    def epilogue(accs, ins, outs, i, j):
        r = jnp.maximum(accs[0], 0.0)
        outs[0][...] = (r * r).astype(BF16)
        outs[1][...] = accs[0].astype(BF16)

    return _mm_ep([(hn, w_up)], "nn", "mm_up_act", epilogue, [],
                  [(_sds(t, DFF, BF16), _tile()), (_sds(t, DFF, BF16), _tile())], 1024, 1024)


def _da_du(dxb, w_down, u):
    t = dxb.shape[0]

    def epilogue(accs, ins, outs, i, j):
        outs[0][...] = (accs[0] * 2.0 * jnp.maximum(ins[0][...].astype(F32), 0.0)).astype(BF16)

    return _mm_ep([(dxb, w_down)], "nt", "mm_da_du", epilogue, [(u, _tile())], [(_sds(t, DFF, BF16), _tile())],
                  1024, 1024)[0]


def _resid_norm_mm(a, w, x, g, name):
    t = x.shape[0]

    def epilogue(accs, ins, outs, i, j):
        x1 = ins[0][...] + accs[0]
        outs[0][...] = x1
        xn, _ = _rms(x1)
        outs[1][...] = (xn * ins[1][...]).astype(BF16)

    return _mm_ep([(a, w)], "nn", name, epilogue, [(x, _tile()), (g, _row())],
                  [(_sds(t, D, F32), _tile()), (_sds(t, D, BF16), _tile())], 1024, D)


def _norm_bwd_mm(dy, w, x, g, dres, name):
    t = x.shape[0]

    def epilogue(accs, ins, outs, i, j):
        @pl.when(i == 0)
        def _():
            outs[2][...] = jnp.zeros_like(outs[2])

        dh = accs[0]
        xn, rstd = _rms(ins[0][...])
        outs[2][...] += jnp.sum(dh * xn, axis=0, keepdims=True)
        dx = ins[2][...] + _rms_bwd(xn, rstd, dh * ins[1][...])
        outs[0][...] = dx
        outs[1][...] = dx.astype(BF16)

    return _mm_ep([(dy, w)], "nt", name, epilogue, [(x, _tile()), (g, _row()), (dres, _tile())],
                  [(_sds(t, D, F32), _tile()), (_sds(t, D, BF16), _tile()), (_sds(1, D, F32), _row())], 1024, D)


def _ple_final_mm(hn2, w_gate, x2, pp, target, gf):
    t = x2.shape[0]

    def epilogue(accs, ins, outs, i, j):
        loss_ref, dg_ref, dx_ref, dpp_ref, dgp_ref = outs

        @pl.when(i == 0)
        def _():
            loss_ref[...] = jnp.zeros_like(loss_ref)
            dg_ref[...] = jnp.zeros_like(dg_ref)

        gate = _sigmoid(accs[0])
        pp_v = ins[1][...]
        x3 = ins[0][...] + gate * pp_v
        xn, rstd = _rms(x3)
        gf_v = ins[3][...]
        err = xn * gf_v - ins[2][...]
        loss_ref[...] += (0.5 / D) * jnp.sum(jnp.sum(err * err, axis=1, keepdims=True), axis=0, keepdims=True)
        dy = err * (1.0 / D)
        dg_ref[...] += jnp.sum(dy * xn, axis=0, keepdims=True)
        dx3 = _rms_bwd(xn, rstd, dy * gf_v)
        dx_ref[...] = dx3
        dpp_ref[...] = (dx3 * gate).astype(BF16)
        dgp_ref[...] = (dx3 * pp_v * gate * (1.0 - gate)).astype(BF16)

    one = lambda tm, tn: pl.BlockSpec((1, 1), lambda i, j, kk: (0, 0))
    return _mm_ep([(hn2, w_gate)], "nn", "mm_ple_final", epilogue,
                  [(x2, _tile()), (pp, _tile()), (target, _tile()), (gf, _row())],
                  [(_sds(1, 1, F32), one), (_sds(1, D, F32), _row()), (_sds(t, D, F32), _tile()),
                   (_sds(t, D, BF16), _tile()), (_sds(t, D, BF16), _tile())], 512, D)


_WIN_SEGMENTS = ((0, 3072, C_QK), (3072, 8, C_IF), (3080, 1024, C_QSW), (4104, 256, C_KV), (4360, 256, C_KV + 256),
                 (4616, 1024, C_GA), (5640, 1024, C_GB))
_WIN_SHARD = N_IN // 4


def _win_pieces():
    out = []
    for src, width, dst in _WIN_SEGMENTS:
        while width:
            chip, col = divmod(src, _WIN_SHARD)
            n = min(width, _WIN_SHARD - col)
            out.append((chip, col, n, dst))
            src, dst, width = src + n, dst + n, width - n
    return out


def _win_pad(shards):
    rows = shards.shape[1]
    tr = _pick(rows, 256)

    def body(s_ref, o_ref):
        for chip, col, n, dst in _win_pieces():
            o_ref[:, dst:dst + n] = s_ref[chip, :, col:col + n]
        o_ref[:, C_IF + 8:NP] = jnp.zeros((tr, NP - C_IF - 8), shards.dtype)

    return pl.pallas_call(
        body, name="win_pad", grid=(rows // tr,), in_specs=[pl.BlockSpec((4, tr, _WIN_SHARD), lambda i: (0, i, 0))],
        out_specs=pl.BlockSpec((tr, NP), lambda i: (i, 0)), out_shape=jax.ShapeDtypeStruct((rows, NP), shards.dtype),
        compiler_params=_params(),
    )(shards)


def _win_unpad(wp):
    rows = wp.shape[0]
    tr = _pick(rows, 256)

    def body(p_ref, o_ref):
        for chip, col, n, dst in _win_pieces():
            o_ref[chip, :, col:col + n] = p_ref[:, dst:dst + n]

    return pl.pallas_call(
        body, name="win_unpad", grid=(rows // tr,), in_specs=[pl.BlockSpec((tr, NP), lambda i: (i, 0))],
        out_specs=pl.BlockSpec((4, tr, _WIN_SHARD), lambda i: (0, i, 0)),
        out_shape=jax.ShapeDtypeStruct((4, rows, _WIN_SHARD), wp.dtype), compiler_params=_params(),
    )(wp)


def _local_step(x, p, target, w, late_weights=None, early_grads=None, last_grad=None):
    t = x.shape[0]
    pb = p.astype(BF16)
    w = dict(w)

    h0 = _norm_fwd(x, w["norm_mix_g"], "norm_mix")
    proj, gates = _proj_in(h0, w["w_in"])
    qk = _conv_silu_fwd(proj, w["conv_qk"])
    grow, sneg_row = _gates_fwd(gates[:, 0:8].T, w["b_if"].reshape(8, 1))
    gcol, sneg_col = grow.T, sneg_row.T
    hraw, ya, cs, st = _mlstm_fwd(qk, proj, grow, gcol, w["mlstm_norm_g"])
    yb, lse = _swa_fwd(proj, w["sinks"])
    if late_weights is not None:
        w.update(late_weights(yb))
    merged, za, zb = _branch_merge(ya, yb, w["w_branch_a"], w["w_branch_b"], proj)
    x1, hn1 = _resid_norm_mm(merged, w["w_out"], x, w["norm_mlp_g"], "mm_out_norm")
    act, u = _up_act(hn1, w["w_up"])
    x2, hn2 = _resid_norm_mm(act, w["w_down"], x1, w["norm_ple_g"], "mm_down_norm")
    pp = _mm(pb, w["w_ple_proj"], "nn", F32, "mm_ple_proj")
    loss, d_final_g, dx3, dpp, dgpre = _ple_final_mm(hn2, w["w_ple_gate"], x2, pp, target, w["final_norm_g"])

    g = {"final_norm_g": d_final_g}
    g["w_ple_proj"] = _mm(pb, dpp, "tn", F32, "mm_d_ple_proj", out_chunks=4)
    g["w_ple_gate"] = _mm(hn2, dgpre, "tn", F32, "mm_d_ple_gate")
    dx2, dx2b, g["norm_ple_g"] = _norm_bwd_mm(dgpre, w["w_ple_gate"], x2, w["norm_ple_g"], dx3, "mm_dhn2_norm")
    g["w_down"] = _mm(act, dx2b, "tn", F32, "mm_d_down")
    du = _da_du(dx2b, w["w_down"], u)
    g["w_up"] = _mm(hn1, du, "tn", F32, "mm_d_up", out_chunks=4)
    dx1, dx1b, g["norm_mlp_g"] = _norm_bwd_mm(du, w["w_up"], x1, w["norm_mlp_g"], dx2, "mm_dhn1_norm")
    g["w_out"] = _mm(merged, dx1b, "tn", F32, "mm_d_out")
    dza, dzb, dproj = _dmerged_bwd(dx1b, w["w_out"], proj, za, zb)
    g["w_branch_a"] = _mm(ya, dza, "tn", F32, "mm_d_branch_a")
    g["w_branch_b"] = _mm(yb, dzb, "tn", F32, "mm_d_branch_b")
    gain = w["mlstm_norm_g"] if early_grads is None else w["mlstm_norm_g"] + early_grads(g)
    dyb = _mm(dzb, w["w_branch_b"], "nt", F32, "mm_dyb")
    dhraw, dproj, g["mlstm_norm_g"] = _dya_bwd(dza, w["w_branch_a"], hraw, proj, gain, dproj)
    dqk, dproj, dif, g["b_if"] = _mlstm_bwd(qk, proj, grow, gcol, sneg_col, cs, st, hraw, dhraw, dproj)
    dc, g["conv_qk"] = _conv_silu_bwd_a(proj, w["conv_qk"], dqk)
    dproj = _conv_silu_bwd_b(dc, w["conv_qk"], dproj)
    dproj, dkv_self, dkv_prev, g["sinks"] = _swa_bwd(proj, w["sinks"], lse, dyb, dproj)
    dproj = _kv_combine(dkv_self, dkv_prev, dif, dproj)
    g["w_in"] = _mm(h0, dproj, "tn", F32, "mm_d_in")
    gain = w["norm_mix_g"] if last_grad is None else w["norm_mix_g"] + last_grad(g)
    grad_x, _, g["norm_mix_g"] = _norm_bwd_mm(dproj, w["w_in"], x, gain, dx1, "mm_dh0_norm")
    return loss, grad_x, g


_W4 = ("w_branch_a", "w_branch_b", "w_out", "w_ple_gate")
_SHARDED_NAMES = ("w_in", "w_up", "w_down", "w_ple_proj", "conv_qk") + _W4
_SMALL_ROWS = 16
_CONV_ROW = 8


def _group(s):
    return [s["w_in"], jnp.concatenate([s[n] for n in _W4], axis=0), s["w_up"], s["w_down"], s["w_ple_proj"]]


def _ungroup(arrs):
    out = {"w_in": arrs[0], "w_up": arrs[2], "w_down": arrs[3], "w_ple_proj": arrs[4]}
    rows = arrs[1].shape[0] // len(_W4)
    for i, n in enumerate(_W4):
        out[n] = arrs[1][i * rows:(i + 1) * rows]
    return out


def _rows_tile(rows):
    return 256 if rows % 256 == 0 else rows


_SMALL = ("norm_mix_g", "mlstm_norm_g", "norm_mlp_g", "norm_ple_g", "final_norm_g")


def _pack_small(vals, extra=None, conv=None):
    rows = [vals[n].reshape(1, D) for n in _SMALL]
    tail = [vals["b_if"].reshape(1, 8), vals["sinks"].reshape(1, SWH)]
    used = 8 + SWH
    if extra is not None:
        tail.append(extra.reshape(1, 1))
        used += 1
    tail.append(jnp.zeros((1, D - used), F32))
    rows.append(jnp.concatenate(tail, axis=1))
    rows.append(jnp.zeros((_CONV_ROW - len(rows), D), F32))
    rows.append(jnp.zeros((CONV, D), F32) if conv is None else conv)
    rows.append(jnp.zeros((_SMALL_ROWS - _CONV_ROW - CONV, D), F32))
    return jnp.concatenate(rows, axis=0)


def _unpack_small(slab, shapes):
    out = {n: slab[i].reshape(shapes[n]) for i, n in enumerate(_SMALL)}
    out["b_if"] = slab[5, 0:8].reshape(shapes["b_if"])
    out["sinks"] = slab[5, 8:8 + SWH].reshape(shapes["sinks"])
    return out


_MESH = pl.DeviceIdType.MESH
_HBM = pl.BlockSpec(memory_space=pltpu.HBM)
_VMEM = pl.BlockSpec(memory_space=pltpu.VMEM)


def _place():
    x, y, c = lax.axis_index("x"), lax.axis_index("y"), lax.axis_index("c")
    return x, y, c, 2 * x + y


def _chip_peer(x, y, r):
    return (x ^ (r >> 1), y ^ (r & 1))


def _half(ref, which):
    h = ref.shape[-2] // 2
    return pl.ds(which * h, h)


def _allgather_weights(shards, conv):
    n = len(shards)

    def body(*refs):
        ins, conv_ref = refs[:n], refs[n]
        outs, conv_out = refs[n + 1:2 * n + 1], refs[2 * n + 1]
        send_a, recv_a, send_b, recv_b, send_c, recv_c, local_sems = refs[2 * n + 2:]
        x, y, c, j = _place()
        sibling = (x, y, 1 - c)
        local = [pltpu.make_async_copy(ins[k], outs[k].at[j], local_sems.at[k]) for k in range(n)]
        local.append(pltpu.make_async_copy(conv_ref, conv_out.at[j], local_sems.at[n]))
        for cp in local:
            cp.start()

        def copy_a(k, r, chip):
            rows = _half(ins[k], c)
            return pltpu.make_async_remote_copy(
                src_ref=ins[k].at[rows], dst_ref=outs[k].at[chip, rows], send_sem=send_a.at[3 * k + r - 1],
                recv_sem=recv_a.at[3 * k + r - 1], device_id=(*_chip_peer(x, y, r), c), device_id_type=_MESH)

        def copy_b(k, r, chip, which):
            rows = _half(ins[k], which)
            return pltpu.make_async_remote_copy(
                src_ref=outs[k].at[chip, rows], dst_ref=outs[k].at[chip, rows], send_sem=send_b.at[3 * k + r - 1],
                recv_sem=recv_b.at[3 * k + r - 1], device_id=sibling, device_id_type=_MESH)

        def copy_c(r, chip):
            return pltpu.make_async_remote_copy(
                src_ref=conv_ref, dst_ref=conv_out.at[chip], send_sem=send_c.at[r - 1],
                recv_sem=recv_c.at[r - 1], device_id=(*_chip_peer(x, y, r), c), device_id_type=_MESH)

        for k in range(n):
            for r in (1, 2, 3):
                copy_a(k, r, j).start()
        for r in (1, 2, 3):
            copy_c(r, j).start()
        for k in range(n):
            for r in (1, 2, 3):
                copy_a(k, r, j ^ r).wait_recv()
                copy_b(k, r, j ^ r, c).start()
        for k in range(n):
            for r in (1, 2, 3):
                copy_b(k, r, j ^ r, 1 - c).wait_recv()
        for r in (1, 2, 3):
            copy_c(r, j ^ r).wait_recv()
        for k in range(n):
            for r in (1, 2, 3):
                copy_a(k, r, j).wait_send()
                copy_b(k, r, j ^ r, c).wait_send()
        for r in (1, 2, 3):
            copy_c(r, j).wait_send()
        for cp in local:
            cp.wait()

    return pl.pallas_call(
        body, name="allgather_weights",
        out_shape=[jax.ShapeDtypeStruct((4,) + s.shape, s.dtype) for s in shards]
        + [jax.ShapeDtypeStruct((4,) + conv.shape, F32)],
        in_specs=[_HBM] * (n + 1), out_specs=[_HBM] * (n + 1),
        scratch_shapes=[pltpu.SemaphoreType.DMA((3 * n,))] * 4 + [pltpu.SemaphoreType.DMA((3,))] * 2
        + [pltpu.SemaphoreType.DMA((n + 1,))],
    )(*shards, conv)


_SEM = pl.BlockSpec(memory_space=pltpu.SEMAPHORE)
_DATAFLOW = pltpu.SideEffectType.DATAFLOW_SIDE_EFFECTING


def _late_peer_copy(src_ref, land_ref, send_sems, recv_sems, x, y, c, j, r, chip):
    return pltpu.make_async_remote_copy(
        src_ref=src_ref, dst_ref=land_ref.at[chip], send_sem=send_sems.at[r - 1], recv_sem=recv_sems.at[r - 1],
        device_id=(*_chip_peer(x, y, r), c), device_id_type=_MESH)


def _late_gather_start(rest):
    def body(rest_ref, land_ref, send_sems, recv_sems, rest_thru, land_thru, token):
        x, y, c, j = _place()
        for r in (1, 2, 3):
            _late_peer_copy(rest_ref, land_ref, send_sems, recv_sems, x, y, c, j, r, j).start()
        token[...] = jnp.zeros_like(token)

    j = 2 * lax.axis_index("x") + lax.axis_index("y")
    land = lax.dynamic_update_slice(lax.empty((4,) + rest.shape, rest.dtype), rest[None], (j, 0, 0))
    return pl.pallas_call(
        body, name="late_gather_start",
        out_shape=(pltpu.SemaphoreType.DMA((3,)), pltpu.SemaphoreType.DMA((3,)), pltpu.HBM(rest.shape, rest.dtype),
                   pltpu.HBM(land.shape, land.dtype), jax.ShapeDtypeStruct((8, 128), F32)),
        in_specs=(_HBM, _HBM), out_specs=(_SEM, _SEM, _HBM, _HBM, _VMEM), input_output_aliases={0: 2, 1: 3},
        compiler_params=pltpu.CompilerParams(has_side_effects=_DATAFLOW),
    )(pltpu.with_memory_space_constraint(rest, pltpu.HBM), pltpu.with_memory_space_constraint(land, pltpu.HBM))


def _late_gather_wait(send_sems, recv_sems, rest_thru, land_thru, after):
    def body(rest_ref, land_ref, send_sems, recv_sems, after_ref, rest_dead, got_ref):
        x, y, c, j = _place()
        for r in (1, 2, 3):
            cp = _late_peer_copy(rest_ref, land_ref, send_sems, recv_sems, x, y, c, j, r, j ^ r)
            cp.wait_send()
            cp.wait_recv()

    return pl.pallas_call(
        body, name="late_gather_wait",
        out_shape=(pltpu.HBM(rest_thru.shape, rest_thru.dtype), pltpu.HBM(land_thru.shape, land_thru.dtype)),
        in_specs=(_HBM, _HBM, _SEM, _SEM, _ANY), out_specs=(_HBM, _HBM), input_output_aliases={0: 0, 1: 1},
        compiler_params=pltpu.CompilerParams(has_side_effects=_DATAFLOW),
    )(rest_thru, land_thru, send_sems, recv_sems, after)[1]


def _pair_exchange(gs, name):
    n = len(gs)

    def body(*refs):
        ins, outs, send_sems, recv_sems = refs[:n], refs[n:2 * n], refs[2 * n], refs[2 * n + 1]
        x, y, c, _ = _place()
        cps = [pltpu.make_async_remote_copy(
            src_ref=ins[k].at[:, _half(ins[k], 1 - c)], dst_ref=outs[k], send_sem=send_sems.at[k],
            recv_sem=recv_sems.at[k], device_id=(x, y, 1 - c), device_id_type=_MESH) for k in range(n)]
        for cp in cps:
            cp.start()
        for cp in cps:
            cp.wait()

    return pl.pallas_call(
        body, name=name,
        out_shape=[jax.ShapeDtypeStruct((4, g.shape[1] // 2, g.shape[2]), F32) for g in gs],
        in_specs=[_HBM] * n, out_specs=[_HBM] * n, scratch_shapes=[pltpu.SemaphoreType.DMA((n,))] * 2,
    )(*gs)


def _pair_sum(g, theirs, c, name):
    _, h, cols = theirs.shape
    tr = _rows_tile(h)
    nb = h // tr

    def body(c_ref, a_ref, b_ref, o_ref, ob_ref):
        s = a_ref[...] + b_ref[...]
        o_ref[...] = s
        ob_ref[...] = s.astype(BF16)

    blk = pl.BlockSpec((1, tr, cols), lambda k, i, c_ref: (k, i, 0))
    return pl.pallas_call(
        body, name=name,
        grid_spec=pltpu.PrefetchScalarGridSpec(
            num_scalar_prefetch=1, grid=(4, nb),
            in_specs=[pl.BlockSpec((1, tr, cols), lambda k, i, c_ref: (k, c_ref[0] * nb + i, 0)), blk],
            out_specs=[blk, blk]),
        out_shape=[jax.ShapeDtypeStruct(theirs.shape, F32), jax.ShapeDtypeStruct(theirs.shape, BF16)],
        compiler_params=_params(),
    )(c.reshape(1).astype(jnp.int32), g, theirs)


def _chip_copies(srcs, lands, send_sems, recv_sems):
    x, y, c, j = _place()
    return [pltpu.make_async_remote_copy(
        src_ref=srcs[k].at[j ^ r], dst_ref=lands[k].at[r - 1], send_sem=send_sems.at[3 * k + r - 1],
        recv_sem=recv_sems.at[3 * k + r - 1], device_id=(*_chip_peer(x, y, r), c), device_id_type=_MESH)
        for k in range(len(srcs)) for r in (1, 2, 3)]


def _chip_exchange_start(ss, tag):
    n = len(ss)

    def body(*refs):
        srcs, lands, send_sems, recv_sems, token = refs[:n], refs[n:2 * n], refs[2 * n], refs[2 * n + 1], refs[-1]
        for cp in _chip_copies(srcs, lands, send_sems, recv_sems):
            cp.start()
        token[...] = jnp.zeros_like(token)

    lands = [lax.empty((3,) + s.shape[1:], s.dtype) for s in ss]
    hbm = [pltpu.HBM(a.shape, a.dtype) for a in list(ss) + lands]
    out = pl.pallas_call(
        body, name="chip_exchange_start_" + tag,
        out_shape=(pltpu.SemaphoreType.DMA((3 * n,)), pltpu.SemaphoreType.DMA((3 * n,)), *hbm,
                   jax.ShapeDtypeStruct((8, 128), F32)),
        in_specs=[_HBM] * (2 * n), out_specs=(_SEM, _SEM, *([_HBM] * (2 * n)), _VMEM),
        input_output_aliases={k: 2 + k for k in range(2 * n)},
        compiler_params=pltpu.CompilerParams(has_side_effects=_DATAFLOW),
    )(*[pltpu.with_memory_space_constraint(a, pltpu.HBM) for a in list(ss) + lands])
    return out[0], out[1], list(out[2:2 + n]), list(out[2 + n:2 + 2 * n]), out[-1]


def _chip_exchange_wait(send_sems, recv_sems, ss_thru, lands_thru, after, tag):
    n = len(ss_thru)

    def body(*refs):
        srcs, lands, send_sems, recv_sems = refs[:n], refs[n:2 * n], refs[2 * n], refs[2 * n + 1]
        for cp in _chip_copies(srcs, lands, send_sems, recv_sems):
            cp.wait_send()
            cp.wait_recv()

    hbm = [pltpu.HBM(a.shape, a.dtype) for a in list(ss_thru) + list(lands_thru)]
    out = pl.pallas_call(
        body, name="chip_exchange_wait_" + tag, out_shape=tuple(hbm),
        in_specs=[_HBM] * (2 * n) + [_SEM, _SEM, _ANY], out_specs=tuple([_HBM] * (2 * n)),
        input_output_aliases={k: k for k in range(2 * n)},
        compiler_params=pltpu.CompilerParams(has_side_effects=_DATAFLOW),
    )(*ss_thru, *lands_thru, send_sems, recv_sems, after)
    return list(out[n:])


def _reduce4(own, others, j, c, name):
    _, h, cols = own.shape
    tr = _rows_tile(h)
    nb = h // tr

    def body(idx_ref, s_ref, a0, a1, a2, o_ref):
        o_ref[...] = ((s_ref[0] + a0[0].astype(F32)) + a1[0].astype(F32)) + a2[0].astype(F32)

    def other(r):
        return pl.BlockSpec((1, tr, cols), lambda i, idx_ref: (r, i, 0))

    return pl.pallas_call(
        body, name=name,
        grid_spec=pltpu.PrefetchScalarGridSpec(
            num_scalar_prefetch=1, grid=(nb,),
            in_specs=[pl.BlockSpec((1, tr, cols), lambda i, idx_ref: (idx_ref[0], i, 0)), other(0), other(1), other(2)],
            out_specs=pl.BlockSpec((tr, cols), lambda i, idx_ref: (idx_ref[1] * nb + i, 0))),
        out_shape=jax.ShapeDtypeStruct((2 * h, cols), F32), compiler_params=_params(),
    )(jnp.stack([j, c]).astype(jnp.int32), own, others, others, others)


def _sibling_share(fulls):
    n = len(fulls)

    def body(*refs):
        outs, send_sems, recv_sems = refs[n:2 * n], refs[2 * n], refs[2 * n + 1]
        x, y, c, _ = _place()
        cps = [pltpu.make_async_remote_copy(
            src_ref=outs[k].at[_half(outs[k], c)], dst_ref=outs[k].at[_half(outs[k], c)], send_sem=send_sems.at[k],
            recv_sem=recv_sems.at[k], device_id=(x, y, 1 - c), device_id_type=_MESH) for k in range(n)]
        for cp in cps:
            cp.start()
        for cp in cps:
            cp.wait()

    return pl.pallas_call(
        body, name="sibling_share", out_shape=[jax.ShapeDtypeStruct(f.shape, F32) for f in fulls],
        in_specs=[_HBM] * n, out_specs=[_HBM] * n, input_output_aliases={k: k for k in range(n)},
        scratch_shapes=[pltpu.SemaphoreType.DMA((n,))] * 2,
    )(*fulls)


def _adamw(w, g, m, v):
    m1 = ADAM_B1 * m + (1.0 - ADAM_B1) * g
    v1 = ADAM_B2 * v + (1.0 - ADAM_B2) * (g * g)
    m_hat = m1 / (1.0 - ADAM_B1 ** ADAM_STEP)
    v_hat = v1 / (1.0 - ADAM_B2 ** ADAM_STEP)
    delta = -ADAM_LR * (m_hat / (jnp.sqrt(v_hat) + ADAM_EPS) + ADAM_WD * w)
    return delta, m1, v1


def _adamw_call(w, g, m, v, name):
    rows, cols = w.shape

    def body(w_ref, g_ref, m_ref, v_ref, d_out, m_out, v_out):
        delta, m1, v1 = _adamw(w_ref[...], g_ref[...], m_ref[...], v_ref[...])
        d_out[...] = delta
        m_out[...] = m1
        v_out[...] = v1

    if rows % 8 == 0:
        tr = _rows_tile(rows)
        blk, grid = pl.BlockSpec((tr, cols), lambda i: (i, 0)), (rows // tr,)
    else:
        blk, grid = pl.BlockSpec((rows, 128), lambda i: (0, i)), (cols // 128,)
    return pl.pallas_call(
        body, name=name, grid=grid, in_specs=[blk] * 4, out_specs=[blk] * 3,
        out_shape=[jax.ShapeDtypeStruct((rows, cols), F32)] * 3, compiler_params=_params(),
    )(w, g, m, v)


def _small_allreduce(vals):
    def body(v_ref, out_ref, buf, send_sems, recv_sems):
        x, y, c, j = _place()
        me = 2 * j + c
        buf[0] = v_ref[...]

        def copy(r):
            return pltpu.make_async_remote_copy(
                src_ref=v_ref, dst_ref=buf.at[r], send_sem=send_sems.at[r - 1], recv_sem=recv_sems.at[r - 1],
                device_id=(x ^ (r >> 2), y ^ ((r >> 1) & 1), c ^ (r & 1)), device_id_type=_MESH)

        for r in range(1, 8):
            copy(r).start()
        for r in range(1, 8):
            copy(r).wait()
        acc = buf[me ^ 0]
        for d in range(1, 8):
            acc = acc + buf[me ^ d]
        out_ref[...] = acc

    return pl.pallas_call(
        body, name="small_allreduce", out_shape=jax.ShapeDtypeStruct((_SMALL_ROWS, D), F32),
        in_specs=[_VMEM], out_specs=_VMEM,
        scratch_shapes=[pltpu.VMEM((8, _SMALL_ROWS, D), F32), pltpu.SemaphoreType.DMA((7,)),
                        pltpu.SemaphoreType.DMA((7,))],
    )(vals)


_NAMES = ("norm_mix_g", "w_in", "conv_qk", "b_if", "mlstm_norm_g", "sinks", "w_branch_a", "w_branch_b", "w_out",
          "norm_mlp_g", "w_up", "w_down", "norm_ple_g", "w_ple_gate", "w_ple_proj", "final_norm_g")
_GROUP_NAMES = ("w_in", "w4", "w_up", "w_down", "w_ple_proj")


def _step(x, p, target, w, m, v):
    c = lax.axis_index("c")
    j = 2 * lax.axis_index("x") + lax.axis_index("y")

    def shards(d):
        return {n: d[n][0] for n in _SHARDED_NAMES}

    ws = shards(w)
    w_in_all, conv_all = _allgather_weights([ws["w_in"].astype(BF16)], ws["conv_qk"])
    rows_pp = PLE * (D // 4) // D
    rest = jnp.concatenate([ws[n] for n in _W4] + [ws["w_up"], ws["w_down"], ws["w_ple_proj"].reshape(rows_pp, D)],
                           axis=0)
    rest = (rest + 0.0 * conv_all[0, 0, 0]).astype(BF16)
    send_sems, recv_sems, rest_thru, land_thru, token = _late_gather_start(rest)
    full = {n: w[n] for n in ("mlstm_norm_g", "norm_mlp_g", "norm_ple_g", "b_if", "sinks")}
    full["norm_mix_g"] = w["norm_mix_g"] + token[0, 0]
    full["final_norm_g"] = w["final_norm_g"].reshape(1, D)
    full["w_in"] = _win_pad(w_in_all)
    full["conv_qk"] = jnp.swapaxes(conv_all, 0, 1).reshape(CONV, D)

    def late_weights(after):
        land = _late_gather_wait(send_sems, recv_sems, rest_thru, land_thru, after)
        out = {n: land[:, i * (D // 4):(i + 1) * (D // 4)].reshape(D, D) for i, n in enumerate(_W4)}
        out["w_up"] = land[:, D:2 * D]
        out["w_down"] = land[:, 2 * D:3 * D].reshape(DFF, D)
        out["w_ple_proj"] = land[:, 3 * D:3 * D + rows_pp].reshape(4, PLE, D // 4)
        return out

    def pair_sums(by_dest, names, tag):
        theirs = _pair_exchange(by_dest, "pair_exchange_" + tag)
        return [_pair_sum(a, b, c, "pair_sum_" + n) for a, b, n in zip(by_dest, theirs, names)]

    early, last = {}, {}

    def early_grads(g):
        by_dest = [jnp.stack([g[n].reshape(4, D // 4, D) for n in _W4], axis=1).reshape(4, D, D),
                   g["w_up"], g["w_down"].reshape(4, DFF // 4, D), g["w_ple_proj"]]
        early["sums"] = pair_sums(by_dest, _GROUP_NAMES[1:], "early")
        *early["flight"], token = _chip_exchange_start([s[1] for s in early["sums"]], "early")
        return token[0, 0]

    def last_grad(g):
        last["sums"] = pair_sums([_win_unpad(g["w_in"])], _GROUP_NAMES[:1], "w_in")
        *last["flight"], token = _chip_exchange_start([s[1] for s in last["sums"]], "w_in")
        return token[0, 0]

    loss, grad_x, g = _local_step(x[0], p[0, 0], target[0], full, late_weights, early_grads, last_grad)

    others = _chip_exchange_wait(*last["flight"], grad_x, "w_in")
    others += _chip_exchange_wait(*early["flight"], others[0], "early")
    sums = last["sums"] + early["sums"]
    halves = [_reduce4(s[0], b, j, c, "reduce4_" + n) for s, b, n in zip(sums, others, _GROUP_NAMES)]
    grads = _sibling_share(halves)

    small_g = _small_allreduce(_pack_small(g, extra=loss, conv=g["conv_qk"]))
    conv_g = lax.dynamic_slice(small_g[_CONV_ROW:_CONV_ROW + CONV], (0, j * (D // 4)), (CONV, D // 4))

    ms, vs = shards(m), shards(v)
    upd = [_adamw_call(wa, ga, ma, va, "adamw_" + n)
           for wa, ga, ma, va, n in list(zip(_group(ws), grads, _group(ms), _group(vs), _GROUP_NAMES))[1:]]
    upd_in = _adamw_call(*[jnp.swapaxes(a, 0, 1) for a in (ws["w_in"], grads[0], ms["w_in"], vs["w_in"])], "adamw_w_in")
    upd = [[jnp.swapaxes(a, 0, 1) for a in upd_in]] + upd
    conv_upd = _adamw_call(ws["conv_qk"], conv_g, ms["conv_qk"], vs["conv_qk"], "adamw_conv")
    small_upd = _adamw_call(_pack_small(w), small_g, _pack_small(m), _pack_small(v), "adamw_small")

    shapes = {n: w[n].shape for n in _NAMES}
    res = []
    for k in range(4):
        big = _ungroup(list(grads) if k == 0 else [u[k - 1] for u in upd])
        big["conv_qk"] = conv_g if k == 0 else conv_upd[k - 1]
        leaves = _unpack_small(small_g if k == 0 else small_upd[k - 1], shapes)
        leaves.update({n: a.reshape(shapes[n]) for n, a in big.items()})
        res.append(leaves)

    out = [small_g[5, 8 + SWH], grad_x[None]]
    for k in range(4):
        out += [res[k][n] for n in _NAMES]
    return tuple(out)


def kernel(x, p, norm_mix_g, w_in, conv_qk, b_if, mlstm_norm_g, sinks, w_branch_a, w_branch_b, w_out, norm_mlp_g, w_up, w_down, norm_ple_g, w_ple_gate, w_ple_proj, final_norm_g, loss_target, m_norm_mix_g, m_w_in, m_conv_qk, m_b_if, m_mlstm_norm_g, m_sinks, m_w_branch_a, m_w_branch_b, m_w_out, m_norm_mlp_g, m_w_up, m_w_down, m_norm_ple_g, m_w_ple_gate, m_w_ple_proj, m_final_norm_g, v_norm_mix_g, v_w_in, v_conv_qk, v_b_if, v_mlstm_norm_g, v_sinks, v_w_branch_a, v_w_branch_b, v_w_out, v_norm_mlp_g, v_w_up, v_w_down, v_norm_ple_g, v_w_ple_gate, v_w_ple_proj, v_final_norm_g):
    w = dict(zip(_NAMES, (norm_mix_g, w_in, conv_qk, b_if, mlstm_norm_g, sinks, w_branch_a, w_branch_b, w_out,
                          norm_mlp_g, w_up, w_down, norm_ple_g, w_ple_gate, w_ple_proj, final_norm_g)))
    m = dict(zip(_NAMES, (m_norm_mix_g, m_w_in, m_conv_qk, m_b_if, m_mlstm_norm_g, m_sinks, m_w_branch_a,
                          m_w_branch_b, m_w_out, m_norm_mlp_g, m_w_up, m_w_down, m_norm_ple_g, m_w_ple_gate,
                          m_w_ple_proj, m_final_norm_g)))
    v = dict(zip(_NAMES, (v_norm_mix_g, v_w_in, v_conv_qk, v_b_if, v_mlstm_norm_g, v_sinks, v_w_branch_a,
                          v_w_branch_b, v_w_out, v_norm_mlp_g, v_w_up, v_w_down, v_norm_ple_g, v_w_ple_gate,
                          v_w_ple_proj, v_final_norm_g)))
    return _step(x, p, loss_target, w, m, v)
```

```python
import jax
import jax.numpy as jnp
from jax import lax
from jax.experimental import pallas as pl
from jax.experimental.pallas import tpu as pltpu

F32 = jnp.float32
BF16 = jnp.bfloat16

D = 1024
PLE = 256
MLH = 4
DQK = 128
DV = 256
CONV = 4
CHUNK = 128
SWH = 16
SWKV = 4
SWG = SWH // SWKV
HD = 64
WIN = 128
DFF = 4096
EPS = 1e-6
N_IN = 6664
NP = 7168
C_QK, C_V, C_O, C_QSW, C_GA, C_GB, C_KV, C_IF = 0, 1024, 2048, 3072, 4096, 5120, 6144, 6656
IFW = NP - C_IF

ADAM_LR = 0.001
ADAM_B1 = 0.9
ADAM_B2 = 0.999
ADAM_EPS = 1e-08
ADAM_WD = 0.01
ADAM_STEP = 10

TOK_TILE = 512
VMEM_LIMIT = 58 * 1024 * 1024


def _params(**kw):
    return pltpu.CompilerParams(vmem_limit_bytes=VMEM_LIMIT, **kw)


def _pick(n, cap):
    if n <= cap:
        return n
    t = cap - cap % 128
    while t > 128 and n % t:
        t -= 128
    assert n % t == 0, (n, cap)
    return t


def _dot(a, b, dims):
    return lax.dot_general(a, b, (dims, ((), ())), preferred_element_type=F32)


def _dot_nn(a, b):
    return _dot(a, b, ((1,), (0,)))


def _dot_nt(a, b):
    return _dot(a, b, ((1,), (1,)))


def _dot_tn(a, b):
    return _dot(a, b, ((0,), (0,)))


def _sigmoid(x):
    return 1.0 / (1.0 + jnp.exp(-x))


def _mm(a, b, mode, out_dtype, name, out_chunks=1):
    bch = b.shape[0] if b.ndim == 3 else 1
    brows, bcols = b.shape[-2], b.shape[-1] * bch
    if mode == "nn":
        (m, k), (k2, n) = a.shape, (brows, bcols)
    elif mode == "nt":
        (m, k), (n, k2) = a.shape, (brows, bcols)
    else:
        (k, m), (k2, n) = a.shape, (brows, bcols)
    assert k == k2, (a.shape, b.shape, mode)
    n_cap = n // max(out_chunks, 1 if mode == "nt" else bch)
    k_cap = k // bch if mode == "nt" else k
    tm, tn, tk = _pick(m, 1024), _pick(n_cap, 1024), _pick(k_cap, 2048)
    nk = k // tk
    if mode == "nn":
        a_spec = pl.BlockSpec((tm, tk), lambda i, j, kk: (i, kk))
        if bch > 1:
            bpc = (n // bch) // tn
            b_spec = pl.BlockSpec((None, tk, tn), lambda i, j, kk: (j // bpc, kk, j % bpc))
        else:
            b_spec = pl.BlockSpec((tk, tn), lambda i, j, kk: (kk, j))
        dot = _dot_nn
    elif mode == "nt":
        a_spec = pl.BlockSpec((tm, tk), lambda i, j, kk: (i, kk))
        if bch > 1:
            bpc = (k // bch) // tk
            b_spec = pl.BlockSpec((None, tn, tk), lambda i, j, kk: (kk // bpc, j, kk % bpc))
        else:
            b_spec = pl.BlockSpec((tn, tk), lambda i, j, kk: (j, kk))
        dot = _dot_nt
    else:
        assert bch == 1
        a_spec = pl.BlockSpec((tk, tm), lambda i, j, kk: (kk, i))
        b_spec = pl.BlockSpec((tk, tn), lambda i, j, kk: (kk, j))
        dot = _dot_tn
    if out_chunks > 1:
        npc = (n // out_chunks) // tn
        out_spec = pl.BlockSpec((None, tm, tn), lambda i, j, kk: (j // npc, i, j % npc))
        out_shape = jax.ShapeDtypeStruct((out_chunks, m, n // out_chunks), out_dtype)
    else:
        out_spec = pl.BlockSpec((tm, tn), lambda i, j, kk: (i, j))
        out_shape = jax.ShapeDtypeStruct((m, n), out_dtype)

    def body(a_ref, b_ref, o_ref, acc_ref):
        kk = pl.program_id(2)

        @pl.when(kk == 0)
        def _():
            acc_ref[...] = jnp.zeros_like(acc_ref)

        acc_ref[...] += dot(a_ref[...], b_ref[...])

        @pl.when(kk == nk - 1)
        def _():
            o_ref[...] = acc_ref[...].astype(out_dtype)

    return pl.pallas_call(
        body, name=name, grid=(m // tm, n // tn, nk),
        in_specs=[a_spec, b_spec], out_specs=out_spec, out_shape=out_shape,
        scratch_shapes=[pltpu.VMEM((tm, tn), F32)],
        compiler_params=_params(dimension_semantics=("parallel", "parallel", "arbitrary")),
    )(a, b)


def _tile(col0=0):
    return lambda tm, tn: pl.BlockSpec((tm, tn), lambda i, j, kk: (i, col0 // tn + j))


def _row():
    return lambda tm, tn: pl.BlockSpec((1, tn), lambda i, j, kk: (0, j))


def _mm_ep(pairs, mode, name, epilogue, ins, outs, tm, tn, aliases=None):
    a0, b0 = pairs[0]
    bch = b0.shape[0] if b0.ndim == 3 else 1
    m, k = a0.shape
    tm = _pick(m, tm)
    n = b0.shape[-1] * bch if mode == "nn" else b0.shape[-2]
    tk = _pick(k // bch if mode == "nt" else k, 2048)
    nk = k // tk
    a_spec = pl.BlockSpec((tm, tk), lambda i, j, kk: (i, kk))
    if mode == "nn":
        dot = _dot_nn
        if bch > 1:
            bpc = (n // bch) // tn
            b_spec = pl.BlockSpec((None, tk, tn), lambda i, j, kk: (j // bpc, kk, j % bpc))
        else:
            b_spec = pl.BlockSpec((tk, tn), lambda i, j, kk: (kk, j))
    else:
        dot = _dot_nt
        if bch > 1:
            bpc = (k // bch) // tk
            b_spec = pl.BlockSpec((None, tn, tk), lambda i, j, kk: (kk // bpc, j, kk % bpc))
        else:
            b_spec = pl.BlockSpec((tn, tk), lambda i, j, kk: (j, kk))
    npair, nin, nout = len(pairs), len(ins), len(outs)

    def body(*refs):
        ab = refs[:2 * npair]
        in_refs = refs[2 * npair:2 * npair + nin]
        out_refs = refs[2 * npair + nin:2 * npair + nin + nout]
        accs = refs[2 * npair + nin + nout:]
        i, j, kk = pl.program_id(0), pl.program_id(1), pl.program_id(2)
        for p in range(npair):
            prod = dot(ab[2 * p][...], ab[2 * p + 1][...])

            @pl.when(kk == 0)
            def _():
                accs[p][...] = prod

            @pl.when(kk > 0)
            def _():
                accs[p][...] += prod

        @pl.when(kk == nk - 1)
        def _():
            epilogue([acc[...] for acc in accs], in_refs, out_refs, i, j)

    operands = [x for pair in pairs for x in pair] + [a for a, _ in ins]
    io_alias = {2 * npair + i: o for i, o in (aliases or {}).items()}
    return pl.pallas_call(
        body, name=name, grid=(m // tm, n // tn, nk),
        in_specs=[a_spec, b_spec] * npair + [mk(tm, tn) for _, mk in ins],
        out_specs=[mk(tm, tn) for _, mk in outs], out_shape=[s for s, _ in outs],
        scratch_shapes=[pltpu.VMEM((tm, tn), F32)] * npair, input_output_aliases=io_alias,
        compiler_params=_params(dimension_semantics=("arbitrary", "arbitrary", "arbitrary")),
    )(*operands)


def _tok(w, j=0):
    return pl.BlockSpec((TOK_TILE, w), lambda i: (i, j))


def _rep(shape):
    return pl.BlockSpec(shape, lambda i: (0,) * len(shape))


def _rms(x):
    rstd = lax.rsqrt(jnp.mean(x * x, axis=-1, keepdims=True) + EPS)
    return x * rstd, rstd


def _rms_bwd(xn, rstd, dxn):
    return rstd * (dxn - xn * jnp.mean(dxn * xn, axis=-1, keepdims=True))


def _norm_fwd(x, g, name):
    t = x.shape[0]

    def body(x_ref, g_ref, h_ref):
        xn, _ = _rms(x_ref[...])
        h_ref[...] = (xn * g_ref[...]).astype(BF16)

    return pl.pallas_call(
        body, name=name, grid=(t // TOK_TILE,), in_specs=[_tok(D), _rep((1, D))], out_specs=_tok(D),
        out_shape=jax.ShapeDtypeStruct((t, D), BF16), compiler_params=_params(),
    )(x, g)


def _halo_prev(w, j=0, rows=8):
    r = TOK_TILE // rows
    return pl.BlockSpec((rows, w), lambda i: (jnp.maximum(i * r - 1, 0), j))


def _last8(halo_ref):
    return halo_ref[...].astype(F32)[halo_ref.shape[0] - 8:]


def _halo_next(w, nt, j=0):
    r = TOK_TILE // 8
    return pl.BlockSpec((8, w), lambda i: (jnp.minimum((i + 1) * r, nt * r - 1), j))


def _shift_down(x, halo, s):
    if s == 0:
        return x
    r = pltpu.roll(x, s, 0)
    hs = pltpu.roll(halo, s, 0)
    row = lax.broadcasted_iota(jnp.int32, hs.shape, 0)
    top = jnp.where(row < s, hs, r[0:8])
    return jnp.concatenate([top, r[8:]], axis=0)


def _shift_up(x, halo, s):
    if s == 0:
        return x
    n = x.shape[0]
    r = pltpu.roll(x, n - s, 0)
    hs = pltpu.roll(halo, 8 - s, 0)
    row = lax.broadcasted_iota(jnp.int32, hs.shape, 0)
    bot = jnp.where(row >= 8 - s, hs, r[n - 8:])
    return jnp.concatenate([r[:n - 8], bot], axis=0)


def _bf(x):
    return x.astype(BF16).astype(F32)


def _conv_taps(x, halo, w):
    x, halo, w = _bf(x), _bf(halo), _bf(w)
    acc = x * w[CONV - 1:CONV, :]
    for j in range(CONV - 1):
        acc = acc + _shift_down(x, halo, CONV - 1 - j) * w[j:j + 1, :]
    return acc


_Q_SCALE = DQK ** -0.5


def _qscale_row():
    lane = lax.broadcasted_iota(jnp.int32, (1, D), 1)
    return jnp.where(lane < MLH * DQK, _Q_SCALE, 1.0).astype(F32)


def _conv_silu_fwd(proj, conv_w):
    t = proj.shape[0]

    def body(x_ref, halo_ref, w_ref, o_ref):
        halo = jnp.where(pl.program_id(0) > 0, _last8(halo_ref), 0.0)
        c = _conv_taps(x_ref[...].astype(F32), halo, w_ref[...])
        o_ref[...] = (c * _sigmoid(c) * _qscale_row()).astype(BF16)

    return pl.pallas_call(
        body, name="conv_silu_fwd", grid=(t // TOK_TILE,),
        in_specs=[_tok(D, C_QK // D), _halo_prev(D, C_QK // D, 16), _rep((CONV, D))], out_specs=_tok(D),
        out_shape=jax.ShapeDtypeStruct((t, D), BF16), compiler_params=_params(),
    )(proj, proj, conv_w)


def _conv_silu_bwd_a(proj, conv_w, dqk):
    t = proj.shape[0]

    def body(x_ref, halo_ref, w_ref, d_ref, dc_ref, dw_ref):
        @pl.when(pl.program_id(0) == 0)
        def _():
            dw_ref[...] = jnp.zeros_like(dw_ref)

        halo = jnp.where(pl.program_id(0) > 0, _last8(halo_ref), 0.0)
        x = x_ref[...].astype(F32)
        c = _conv_taps(x, halo, w_ref[...])
        s = _sigmoid(c)
        dc = d_ref[...] * _qscale_row() * (s * (1.0 + c * (1.0 - s)))
        dc_ref[...] = dc
        dcb, xb, halo_b = _bf(dc), _bf(x), _bf(halo)
        for j in range(CONV):
            dw_ref[j:j + 1, :] += jnp.sum(dcb * _shift_down(xb, halo_b, CONV - 1 - j), axis=0, keepdims=True)

    return pl.pallas_call(
        body, name="conv_silu_bwd_a", grid=(t // TOK_TILE,),
        in_specs=[_tok(D, C_QK // D), _halo_prev(D, C_QK // D, 16), _rep((CONV, D)), _tok(D)],
        out_specs=[_tok(D), _rep((CONV, D))],
        out_shape=[jax.ShapeDtypeStruct((t, D), F32), jax.ShapeDtypeStruct((CONV, D), F32)],
        compiler_params=_params(),
    )(proj, proj, conv_w, dqk)


def _conv_silu_bwd_b(dc, conv_w, dproj):
    t = dc.shape[0]
    nt = t // TOK_TILE

    def body(dc_ref, halo_ref, w_ref, _, dx_ref):
        halo = _bf(jnp.where(pl.program_id(0) < nt - 1, halo_ref[...], 0.0))
        dcv = _bf(dc_ref[...])
        w = _bf(w_ref[...])
        acc = dcv * w[CONV - 1:CONV, :]
        for j in range(CONV - 1):
            acc = acc + _shift_up(dcv, halo, CONV - 1 - j) * w[j:j + 1, :]
        dx_ref[...] = acc.astype(BF16)

    return pl.pallas_call(
        body, name="conv_silu_bwd_b", grid=(nt,), in_specs=[_tok(D), _halo_next(D, nt), _rep((CONV, D)), _ANY],
        out_specs=_tok(D, C_QK // D), out_shape=jax.ShapeDtypeStruct((t, NP), BF16),
        input_output_aliases={3: 0}, compiler_params=_params(),
    )(dc, dc, conv_w, dproj)


def _gates_fwd(pre_rows, bias_col):
    t = pre_rows.shape[1]

    def body(p_ref, b_ref, g_ref, s_ref):
        z = p_ref[...] + b_ref[...]
        lf = jnp.minimum(z, 0.0) - jnp.log(1.0 + jnp.exp(-jnp.abs(z)))
        lane = lax.broadcasted_iota(jnp.int32, z.shape, 1) % CHUNK
        cum = lf
        s = 1
        while s < CHUNK:
            cum = cum + jnp.where(lane >= s, pltpu.roll(cum, s, 1), 0.0)
            s *= 2
        sub = lax.broadcasted_iota(jnp.int32, z.shape, 0)
        g_ref[...] = jnp.where(sub < MLH, z, cum)
        s_ref[...] = _sigmoid(-z)

    return pl.pallas_call(
        body, name="gates_fwd",
        out_shape=[jax.ShapeDtypeStruct((8, t), F32), jax.ShapeDtypeStruct((8, t), F32)],
        compiler_params=_params(),
    )(pre_rows, bias_col)


def _chunk_terms(grow, gcol, h, m0):
    i_row, b_row = grow[h:h + 1, :], grow[MLH + h:MLH + h + 1, :]
    i_col, b_col = gcol[:, h:h + 1], gcol[:, MLH + h:MLH + h + 1]
    b_last = b_row[:, CHUNK - 1:CHUNK]
    tt = lax.broadcasted_iota(jnp.int32, (CHUNK, CHUNK), 0)
    ss = lax.broadcasted_iota(jnp.int32, (CHUNK, CHUNK), 1)
    log_d = jnp.where(tt >= ss, b_col - b_row + i_row, -jnp.inf)
    m_t = jnp.maximum(b_col + m0, jnp.max(log_d, axis=1, keepdims=True))
    dm = jnp.exp(log_d - m_t)
    wi = jnp.exp(b_col + m0 - m_t)
    m1 = jnp.maximum(b_last + m0, jnp.max(b_last - b_row + i_row, axis=1, keepdims=True))
    ws = jnp.exp(b_last - b_col + i_col - m1)
    dec = jnp.exp(b_last + m0 - m1)
    return dm, wi, m_t, ws, dec, m1


def _mlstm_fwd(qk, proj, grow, gcol, gain):
    t = qk.shape[0]
    nc = t // CHUNK

    def body(qk_ref, v_ref, o_ref, grow_ref, gcol_ref, g_ref, h_ref, y_ref, cs_ref, st_ref, c_scr, st_scr):
        @pl.when(pl.program_id(0) == 0)
        def _():
            c_scr[...] = jnp.zeros_like(c_scr)
            st_scr[...] = jnp.zeros_like(st_scr)

        grow_v, gcol_v = grow_ref[...], gcol_ref[...]
        for h in range(MLH):
            q = qk_ref[:, h * DQK:(h + 1) * DQK]
            k = qk_ref[:, MLH * DQK + h * DQK:MLH * DQK + (h + 1) * DQK]
            v = v_ref[:, h * DV:(h + 1) * DV]
            c0 = c_scr[h]
            n0 = st_scr[h, 0:1, :]
            m0 = st_scr[h, 1:2, 0:1]
            cs_ref[0, h] = c0
            st_ref[0, h] = st_scr[h]
            dm, wi, m_t, ws, dec, m1 = _chunk_terms(grow_v, gcol_v, h, m0)
            s = _dot_nt(q, k) * dm
            num = wi * _dot_nt(q, c0.astype(BF16)) + _dot_nn(s.astype(BF16), v.astype(BF16))
            den = wi * jnp.sum(q.astype(F32) * n0, axis=1, keepdims=True) + jnp.sum(s, axis=1, keepdims=True)
            sl = slice(h * DV, (h + 1) * DV)
            hv = num / jnp.maximum(jnp.abs(den), jnp.exp(-m_t))
            h_ref[:, sl] = hv
            xn, _ = _rms(hv)
            y_ref[:, sl] = (_sigmoid(o_ref[:, sl].astype(F32)) * xn * g_ref[:, sl]).astype(BF16)
            c_scr[h] = dec * c0 + _dot_tn((ws * v).astype(BF16), k)
            st_scr[h, 0:1, :] = dec * n0 + jnp.sum(ws * k.astype(F32), axis=0, keepdims=True)
            st_scr[h, 1:2, :] = jnp.broadcast_to(m1, (1, DQK))

    return pl.pallas_call(
        body, name="mlstm_fwd", grid=(nc,),
        in_specs=[pl.BlockSpec((CHUNK, D), lambda c: (c, 0)), pl.BlockSpec((CHUNK, D), lambda c: (c, C_V // D)),
                  pl.BlockSpec((CHUNK, D), lambda c: (c, C_O // D)),
                  pl.BlockSpec((8, CHUNK), lambda c: (0, c)), pl.BlockSpec((CHUNK, 8), lambda c: (c, 0)),
                  pl.BlockSpec((1, D), lambda c: (0, 0))],
        out_specs=[pl.BlockSpec((CHUNK, D), lambda c: (c, 0)), pl.BlockSpec((CHUNK, D), lambda c: (c, 0)),
                   pl.BlockSpec((1, MLH, DV, DQK), lambda c: (c, 0, 0, 0)),
                   pl.BlockSpec((1, MLH, 8, DQK), lambda c: (c, 0, 0, 0))],
        out_shape=[jax.ShapeDtypeStruct((t, D), F32), jax.ShapeDtypeStruct((t, D), BF16),
                   jax.ShapeDtypeStruct((nc, MLH, DV, DQK), F32), jax.ShapeDtypeStruct((nc, MLH, 8, DQK), F32)],
        scratch_shapes=[pltpu.VMEM((MLH, DV, DQK), F32), pltpu.VMEM((MLH, 8, DQK), F32)],
        compiler_params=_params(dimension_semantics=("arbitrary",)),
    )(qk, proj, proj, grow, gcol, gain)


def _mlstm_bwd(qk, proj, grow, gcol, sneg_col, cs, st, hraw, dh, dproj):
    t = qk.shape[0]
    nc = t // CHUNK

    def rev(c):
        return nc - 1 - c

    def nxt(c):
        return jnp.minimum(nc - c, nc - 1)

    def body(qk_ref, v_ref, grow_ref, gcol_ref, sneg_ref, cs_ref, st_ref, cs1_ref, st1_ref, h_ref, dh_ref, _,
             dqk_ref, dv_ref, dif_ref, dbif_ref, dc_scr, dn_scr):
        @pl.when(pl.program_id(0) == 0)
        def _():
            dc_scr[...] = jnp.zeros_like(dc_scr)
            dn_scr[...] = jnp.zeros_like(dn_scr)
            dbif_ref[...] = jnp.zeros_like(dbif_ref)

        grow_v, gcol_v, sneg = grow_ref[...], gcol_ref[...], sneg_ref[...]
        tt = lax.broadcasted_iota(jnp.int32, (CHUNK, CHUNK), 0)
        ss = lax.broadcasted_iota(jnp.int32, (CHUNK, CHUNK), 1)
        lane8 = lax.broadcasted_iota(jnp.int32, (CHUNK, 8), 1)
        dif = jnp.zeros((CHUNK, 8), F32)
        for h in range(MLH):
            q = qk_ref[:, h * DQK:(h + 1) * DQK]
            k = qk_ref[:, MLH * DQK + h * DQK:MLH * DQK + (h + 1) * DQK]
            qf, kf = q.astype(F32), k.astype(F32)
            v = v_ref[:, h * DV:(h + 1) * DV]
            vb = v.astype(BF16)
            c0 = cs_ref[0, h]
            n0 = st_ref[0, h, 0:1, :]
            m0 = st_ref[0, h, 1:2, 0:1]
            dc1 = dc_scr[h]
            dn1 = dn_scr[h, 0:1, :]
            dm, wi, m_t, ws, dec, _ = _chunk_terms(grow_v, gcol_v, h, m0)
            s = _dot_nt(q, k) * dm
            den = wi * jnp.sum(qf * n0, axis=1, keepdims=True) + jnp.sum(s, axis=1, keepdims=True)
            floor = jnp.exp(-m_t)
            g = jnp.maximum(jnp.abs(den), floor)
            dh_v = dh_ref[:, h * DV:(h + 1) * DV]
            dnum = dh_v / g
            dden = -jnp.sum(dh_v * h_ref[:, h * DV:(h + 1) * DV], axis=1, keepdims=True) / g
            dden = jnp.where(jnp.abs(den) > floor, dden * jnp.sign(den), 0.0)
            dnum_b = dnum.astype(BF16)
            da = ((_dot_nt(dnum_b, vb) + dden) * dm).astype(BF16)
            dc1_b = dc1.astype(BF16)
            dq = _dot_nn(da, k) + wi * (_dot_nn(dnum_b, c0.astype(BF16)) + dden * n0)
            dk = _dot_tn(da, q) + ws * (_dot_nn(vb, dc1_b) + dn1)
            dv = _dot_tn(s.astype(BF16), dnum_b) + ws * _dot_nt(k, dc1_b)
            dqk_ref[:, h * DQK:(h + 1) * DQK] = dq
            dqk_ref[:, MLH * DQK + h * DQK:MLH * DQK + (h + 1) * DQK] = dk
            dv_ref[:, h * DV:(h + 1) * DV] = dv.astype(BF16)
            rk = jnp.sum(kf * dk, axis=1, keepdims=True)
            df = jnp.sum(qf * dq, axis=1, keepdims=True) - rk
            df_row = jnp.sum(jnp.where(tt == ss, df, 0.0), axis=0, keepdims=True)
            suffix = jnp.sum(jnp.where(ss >= tt, df_row, 0.0), axis=1, keepdims=True)
            cross = (jnp.sum(jnp.sum(dc1 * cs1_ref[0, h], axis=1, keepdims=True), axis=0, keepdims=True)
                     + jnp.sum(dn1 * st1_ref[0, h, 0:1, :], axis=1, keepdims=True))
            dpf = (suffix + cross) * sneg[:, MLH + h:MLH + h + 1]
            dif = dif + jnp.where(lane8 == h, rk, 0.0) + jnp.where(lane8 == MLH + h, dpf, 0.0)
            dc_scr[h] = dec * dc1 + _dot_tn((wi * dnum).astype(BF16), q)
            dn_scr[h, 0:1, :] = dec * dn1 + jnp.sum(wi * dden * qf, axis=0, keepdims=True)
        dif_ref[...] = dif
        dbif_ref[...] += jnp.sum(dif, axis=0, keepdims=True)

    return pl.pallas_call(
        body, name="mlstm_bwd", grid=(nc,),
        in_specs=[pl.BlockSpec((CHUNK, D), lambda c: (rev(c), 0)),
                  pl.BlockSpec((CHUNK, D), lambda c: (rev(c), C_V // D)),
                  pl.BlockSpec((8, CHUNK), lambda c: (0, rev(c))),
                  pl.BlockSpec((CHUNK, 8), lambda c: (rev(c), 0)),
                  pl.BlockSpec((CHUNK, 8), lambda c: (rev(c), 0)),
                  pl.BlockSpec((1, MLH, DV, DQK), lambda c: (rev(c), 0, 0, 0)),
                  pl.BlockSpec((1, MLH, 8, DQK), lambda c: (rev(c), 0, 0, 0)),
                  pl.BlockSpec((1, MLH, DV, DQK), lambda c: (nxt(c), 0, 0, 0)),
                  pl.BlockSpec((1, MLH, 8, DQK), lambda c: (nxt(c), 0, 0, 0)),
                  pl.BlockSpec((CHUNK, D), lambda c: (rev(c), 0)),
                  pl.BlockSpec((CHUNK, D), lambda c: (rev(c), 0)), _ANY],
        out_specs=[pl.BlockSpec((CHUNK, D), lambda c: (rev(c), 0)),
                   pl.BlockSpec((CHUNK, D), lambda c: (rev(c), C_V // D)),
                   pl.BlockSpec((CHUNK, 8), lambda c: (rev(c), 0)),
                   pl.BlockSpec((1, 8), lambda c: (0, 0))],
        out_shape=[jax.ShapeDtypeStruct((t, D), F32), jax.ShapeDtypeStruct((t, NP), BF16),
                   jax.ShapeDtypeStruct((t, 8), F32), jax.ShapeDtypeStruct((1, 8), F32)],
        scratch_shapes=[pltpu.VMEM((MLH, DV, DQK), F32), pltpu.VMEM((MLH, 8, DQK), F32)],
        input_output_aliases={11: 1}, compiler_params=_params(dimension_semantics=("arbitrary",)),
    )(qk, proj, grow, gcol, sneg_col, cs, st, cs, st, hraw, dh, dproj)


_ANY = pl.BlockSpec(memory_space=pl.ANY)


_SW_SCALE = HD ** -0.5
_KVB = C_KV // (2 * SWKV * HD)


def _swa_mask(n):
    ki = lax.broadcasted_iota(jnp.int32, (2 * WIN, SWG * WIN), 0)
    qi = lax.broadcasted_iota(jnp.int32, (2 * WIN, SWG * WIN), 1) % WIN
    return (ki > qi) & (ki <= qi + WIN) & ((n > 0) | (ki >= WIN))


def _group_rows(x_ref, hk):
    return jnp.concatenate([x_ref[:, (hk * SWG + g) * HD:(hk * SWG + g + 1) * HD] for g in range(SWG)], axis=0)


def _group_lanes(x_ref, hk):
    return jnp.concatenate([x_ref[hk * SWG + g:hk * SWG + g + 1, :] for g in range(SWG)], axis=1)


def _sink_lanes(sink_ref, hk):
    return jnp.concatenate([jnp.broadcast_to(sink_ref[:, hk * SWG + g:hk * SWG + g + 1], (1, WIN))
                            for g in range(SWG)], axis=1)


def _swa_fwd(proj, sinks):
    t = proj.shape[0]
    nb = t // WIN

    def body(q_ref, kvc_ref, kvp_ref, sink_ref, y_ref, lse_ref):
        valid = _swa_mask(pl.program_id(0))
        for hk in range(SWKV):
            ks = slice(hk * HD, (hk + 1) * HD)
            vs = slice(SWKV * HD + hk * HD, SWKV * HD + (hk + 1) * HD)
            kb = jnp.concatenate([kvp_ref[:, ks], kvc_ref[:, ks]], axis=0).astype(BF16)
            vb = jnp.concatenate([kvp_ref[:, vs], kvc_ref[:, vs]], axis=0).astype(BF16)
            q4 = _group_rows(q_ref, hk).astype(BF16)
            sink = _sink_lanes(sink_ref, hk)
            logits = jnp.where(valid, _dot_nt(kb, q4) * _SW_SCALE, -jnp.inf)
            m = jnp.maximum(jnp.max(logits, axis=0, keepdims=True), sink)
            p = jnp.exp(logits - m)
            denom = jnp.sum(p, axis=0, keepdims=True) + jnp.exp(sink - m)
            y4 = _dot_tn((p / denom).astype(BF16), vb).astype(BF16)
            lse4 = m + jnp.log(denom)
            for g in range(SWG):
                hq = hk * SWG + g
                y_ref[:, hq * HD:(hq + 1) * HD] = y4[g * WIN:(g + 1) * WIN]
                lse_ref[hq:hq + 1, :] = lse4[:, g * WIN:(g + 1) * WIN]

    return pl.pallas_call(
        body, name="swa_fwd", grid=(nb,),
        in_specs=[pl.BlockSpec((WIN, D), lambda n: (n, C_QSW // D)),
                  pl.BlockSpec((WIN, 512), lambda n: (n, _KVB)),
                  pl.BlockSpec((WIN, 512), lambda n: (jnp.maximum(n - 1, 0), _KVB)),
                  pl.BlockSpec((1, SWH), lambda n: (0, 0))],
        out_specs=[pl.BlockSpec((WIN, D), lambda n: (n, 0)), pl.BlockSpec((SWH, WIN), lambda n: (0, n))],
        out_shape=[jax.ShapeDtypeStruct((t, D), BF16), jax.ShapeDtypeStruct((SWH, t), F32)],
        compiler_params=_params(),
    )(proj, proj, proj, sinks)


def _swa_bwd(proj, sinks, lse, dyb, dproj):
    t = proj.shape[0]
    nb = t // WIN

    def body(q_ref, kvc_ref, kvp_ref, sink_ref, lse_ref, dy_ref, _, dq_ref, dself_ref, dprev_ref, ds_ref):
        @pl.when(pl.program_id(0) == 0)
        def _():
            ds_ref[...] = jnp.zeros_like(ds_ref)

        valid = _swa_mask(pl.program_id(0))
        for hk in range(SWKV):
            ks = slice(hk * HD, (hk + 1) * HD)
            vs = slice(SWKV * HD + hk * HD, SWKV * HD + (hk + 1) * HD)
            kb = jnp.concatenate([kvp_ref[:, ks], kvc_ref[:, ks]], axis=0).astype(BF16)
            vb = jnp.concatenate([kvp_ref[:, vs], kvc_ref[:, vs]], axis=0).astype(BF16)
            dy4 = _group_rows(dy_ref, hk)
            qb, dyb_ = _group_rows(q_ref, hk).astype(BF16), dy4.astype(BF16)
            lse4 = _group_lanes(lse_ref, hk)
            logits = jnp.where(valid, _dot_nt(kb, qb) * _SW_SCALE, -jnp.inf)
            p = jnp.exp(logits - lse4)
            dpt = _dot_nt(vb, dyb_)
            delta = jnp.sum(p * dpt, axis=0, keepdims=True)
            dsm = (p * (dpt - delta)).astype(BF16)
            dq4 = (_dot_tn(dsm, kb) * _SW_SCALE).astype(BF16)
            dkb = _dot_nn(dsm, qb) * _SW_SCALE
            dvb = _dot_nn(p.astype(BF16), dyb_)
            dsink4 = jnp.exp(_sink_lanes(sink_ref, hk) - lse4) * delta
            for g in range(SWG):
                hq = hk * SWG + g
                dq_ref[:, hq * HD:(hq + 1) * HD] = dq4[g * WIN:(g + 1) * WIN]
                ds_ref[:, hq:hq + 1] += -jnp.sum(dsink4[:, g * WIN:(g + 1) * WIN], axis=1, keepdims=True)
            dprev_ref[:, ks] = dkb[:WIN]
            dself_ref[:, ks] = dkb[WIN:]
            dprev_ref[:, vs] = dvb[:WIN]
            dself_ref[:, vs] = dvb[WIN:]

    return pl.pallas_call(
        body, name="swa_bwd", grid=(nb,),
        in_specs=[pl.BlockSpec((WIN, D), lambda n: (n, C_QSW // D)),
                  pl.BlockSpec((WIN, 512), lambda n: (n, _KVB)),
                  pl.BlockSpec((WIN, 512), lambda n: (jnp.maximum(n - 1, 0), _KVB)),
                  pl.BlockSpec((1, SWH), lambda n: (0, 0)),
                  pl.BlockSpec((SWH, WIN), lambda n: (0, n)),
                  pl.BlockSpec((WIN, D), lambda n: (n, 0)), _ANY],
        out_specs=[pl.BlockSpec((WIN, D), lambda n: (n, C_QSW // D)), pl.BlockSpec((WIN, 512), lambda n: (n, 0)),
                   pl.BlockSpec((WIN, 512), lambda n: (jnp.maximum(n - 1, 0), 0)),
                   pl.BlockSpec((1, SWH), lambda n: (0, 0))],
        out_shape=[jax.ShapeDtypeStruct((t, NP), BF16), jax.ShapeDtypeStruct((t, 512), F32),
                   jax.ShapeDtypeStruct((t, 512), F32), jax.ShapeDtypeStruct((1, SWH), F32)],
        input_output_aliases={6: 0}, compiler_params=_params(),
    )(proj, proj, proj, sinks, lse, dyb, dproj)


def _kv_combine(dself, dnext, dif, dproj):
    t = dself.shape[0]
    rows = _pick(t, 512)

    def body(a_ref, b_ref, dif_ref, _, o_ref):
        row = pl.program_id(0) * rows + lax.broadcasted_iota(jnp.int32, (rows, 1), 0)
        o_ref[:, 0:512] = (a_ref[...] + jnp.where(row < t - WIN, b_ref[...], 0.0)).astype(BF16)
        lane = lax.broadcasted_iota(jnp.int32, (rows, 128), 1)
        dif_v = dif_ref[...]
        first = jnp.zeros((rows, 128), F32)
        for col in range(8):
            first = first + jnp.where(lane == col, dif_v[:, col:col + 1], 0.0)
        o_ref[:, 512:640] = first.astype(BF16)
        o_ref[:, 640:512 + IFW] = jnp.zeros((rows, IFW - 128), BF16)

    return pl.pallas_call(
        body, name="kv_combine", grid=(t // rows,),
        in_specs=[pl.BlockSpec((rows, 512), lambda n: (n, 0)), pl.BlockSpec((rows, 512), lambda n: (n, 0)),
                  pl.BlockSpec((rows, 8), lambda n: (n, 0)), _ANY],
        out_specs=pl.BlockSpec((rows, 512 + IFW), lambda n: (n, C_KV // (512 + IFW))),
        out_shape=jax.ShapeDtypeStruct((t, NP), BF16), input_output_aliases={3: 0}, compiler_params=_params(),
    )(dself, dnext, dif, dproj)


def _sds(t, n, dtype):
    return jax.ShapeDtypeStruct((t, n), dtype)


def _proj_in(h0, w_in):
    t = h0.shape[0]
    tn = 2 * IFW

    def epilogue(accs, ins, outs, i, j):
        outs[0][...] = accs[0].astype(BF16)

        @pl.when(j == C_IF // tn)
        def _():
            outs[1][...] = accs[0][:, C_IF % tn:C_IF % tn + 128]

    gate_cols = lambda tm, tn: pl.BlockSpec((tm, 128), lambda i, j, kk: (i, 0))
    return _mm_ep([(h0, w_in)], "nn", "mm_in", epilogue, [],
                  [(_sds(t, NP, BF16), _tile()), (_sds(t, 128, F32), gate_cols)], 1024, tn)


def _branch_merge(ya, yb, wa, wb, proj):
    t = ya.shape[0]

    def epilogue(accs, ins, outs, i, j):
        za, zb = accs
        merged = _sigmoid(ins[0][...].astype(F32)) * za + _sigmoid(ins[1][...].astype(F32)) * zb
        outs[0][...] = merged.astype(BF16)
        outs[1][...] = za.astype(BF16)
        outs[2][...] = zb.astype(BF16)

    return _mm_ep([(ya, wa), (yb, wb)], "nn", "mm_branch_merge", epilogue, [(proj, _tile(C_GA)), (proj, _tile(C_GB))],
                  [(_sds(t, D, BF16), _tile())] * 3, 1024, 512)


def _dmerged_bwd(dxb, w_out, proj, za, zb):
    t = dxb.shape[0]

    def epilogue(accs, ins, outs, i, j):
        dm = accs[0]
        sa, sb = _sigmoid(ins[0][...].astype(F32)), _sigmoid(ins[1][...].astype(F32))
        outs[0][...] = (dm * sa).astype(BF16)
        outs[1][...] = (dm * sb).astype(BF16)
        outs[2][:, 0:D] = (dm * ins[2][...].astype(F32) * sa * (1.0 - sa)).astype(BF16)
        outs[2][:, D:2 * D] = (dm * ins[3][...].astype(F32) * sb * (1.0 - sb)).astype(BF16)

    gate_cols = lambda tm, tn: pl.BlockSpec((tm, 2 * D), lambda i, j, kk: (i, C_GA // (2 * D)))
    return _mm_ep([(dxb, w_out)], "nt", "mm_dmerged_bwd", epilogue,
                  [(proj, _tile(C_GA)), (proj, _tile(C_GB)), (za, _tile()), (zb, _tile())],
                  [(_sds(t, D, BF16), _tile()), (_sds(t, D, BF16), _tile()), (_sds(t, NP, BF16), gate_cols)], 1024, D)


def _dya_bwd(dza, wa, hraw, proj, g, dproj):
    t = dza.shape[0]

    def epilogue(accs, ins, outs, i, j):
        h_ref, o_ref, g_ref, _ = ins
        dh_ref, do_ref, dg_ref = outs

        @pl.when(i == 0)
        def _():
            dg_ref[...] = jnp.zeros_like(dg_ref)

        dy = accs[0]
        so = _sigmoid(o_ref[...].astype(F32))
        for h in range(MLH):
            sl = slice(h * DV, (h + 1) * DV)
            xn, rstd = _rms(h_ref[:, sl])
            gs = g_ref[:, sl]
            do_ref[:, sl] = (dy[:, sl] * xn * gs * so[:, sl] * (1.0 - so[:, sl])).astype(BF16)
            dhn = dy[:, sl] * so[:, sl]
            dg_ref[:, sl] += jnp.sum(dhn * xn, axis=0, keepdims=True)
            dh_ref[:, sl] = _rms_bwd(xn, rstd, dhn * gs)

    return _mm_ep([(dza, wa)], "nt", "mm_dya_bwd", epilogue,
                  [(hraw, _tile()), (proj, _tile(C_O)), (g, _row()), (dproj, lambda tm, tn: _ANY)],
                  [(_sds(t, D, F32), _tile()), (_sds(t, NP, BF16), _tile(C_O)), (_sds(1, D, F32), _row())],
                  1024, D, aliases={3: 1})


def _up_act(hn, w_up):
    t = hn.shape[0]

    def epilogue(accs, ins, outs, i, j):
        r = jnp.maximum(accs[0], 0.0)
        outs[0][...] = (r * r).astype(BF16)
        outs[1][...] = accs[0].astype(BF16)

    return _mm_ep([(hn, w_up)], "nn", "mm_up_act", epilogue, [],
                  [(_sds(t, DFF, BF16), _tile()), (_sds(t, DFF, BF16), _tile())], 1024, 1024)


def _da_du(dxb, w_down, u):
    t = dxb.shape[0]

    def epilogue(accs, ins, outs, i, j):
        outs[0][...] = (accs[0] * 2.0 * jnp.maximum(ins[0][...].astype(F32), 0.0)).astype(BF16)

    return _mm_ep([(dxb, w_down)], "nt", "mm_da_du", epilogue, [(u, _tile())], [(_sds(t, DFF, BF16), _tile())],
                  1024, 1024)[0]


def _resid_norm_mm(a, w, x, g, name):
    t = x.shape[0]

    def epilogue(accs, ins, outs, i, j):
        x1 = ins[0][...] + accs[0]
        outs[0][...] = x1
        xn, _ = _rms(x1)
        outs[1][...] = (xn * ins[1][...]).astype(BF16)

    return _mm_ep([(a, w)], "nn", name, epilogue, [(x, _tile()), (g, _row())],
                  [(_sds(t, D, F32), _tile()), (_sds(t, D, BF16), _tile())], 1024, D)


def _norm_bwd_mm(dy, w, x, g, dres, name):
    t = x.shape[0]

    def epilogue(accs, ins, outs, i, j):
        @pl.when(i == 0)
        def _():
            outs[2][...] = jnp.zeros_like(outs[2])

        dh = accs[0]
        xn, rstd = _rms(ins[0][...])
        outs[2][...] += jnp.sum(dh * xn, axis=0, keepdims=True)
        dx = ins[2][...] + _rms_bwd(xn, rstd, dh * ins[1][...])
        outs[0][...] = dx
        outs[1][...] = dx.astype(BF16)

    return _mm_ep([(dy, w)], "nt", name, epilogue, [(x, _tile()), (g, _row()), (dres, _tile())],
                  [(_sds(t, D, F32), _tile()), (_sds(t, D, BF16), _tile()), (_sds(1, D, F32), _row())], 1024, D)


def _ple_final_mm(hn2, w_gate, x2, pp, target, gf):
    t = x2.shape[0]

    def epilogue(accs, ins, outs, i, j):
        loss_ref, dg_ref, dx_ref, dpp_ref, dgp_ref = outs

        @pl.when(i == 0)
        def _():
            loss_ref[...] = jnp.zeros_like(loss_ref)
            dg_ref[...] = jnp.zeros_like(dg_ref)

        gate = _sigmoid(accs[0])
        pp_v = ins[1][...]
        x3 = ins[0][...] + gate * pp_v
        xn, rstd = _rms(x3)
        gf_v = ins[3][...]
        err = xn * gf_v - ins[2][...]
        loss_ref[...] += (0.5 / D) * jnp.sum(jnp.sum(err * err, axis=1, keepdims=True), axis=0, keepdims=True)
        dy = err * (1.0 / D)
        dg_ref[...] += jnp.sum(dy * xn, axis=0, keepdims=True)
        dx3 = _rms_bwd(xn, rstd, dy * gf_v)
        dx_ref[...] = dx3
        dpp_ref[...] = (dx3 * gate).astype(BF16)
        dgp_ref[...] = (dx3 * pp_v * gate * (1.0 - gate)).astype(BF16)

    one = lambda tm, tn: pl.BlockSpec((1, 1), lambda i, j, kk: (0, 0))
    return _mm_ep([(hn2, w_gate)], "nn", "mm_ple_final", epilogue,
                  [(x2, _tile()), (pp, _tile()), (target, _tile()), (gf, _row())],
                  [(_sds(1, 1, F32), one), (_sds(1, D, F32), _row()), (_sds(t, D, F32), _tile()),
                   (_sds(t, D, BF16), _tile()), (_sds(t, D, BF16), _tile())], 512, D)


_WIN_SEGMENTS = ((0, 3072, C_QK), (3072, 8, C_IF), (3080, 1024, C_QSW), (4104, 256, C_KV), (4360, 256, C_KV + 256),
                 (4616, 1024, C_GA), (5640, 1024, C_GB))
_WIN_SHARD = N_IN // 4


def _win_pieces():
    out = []
    for src, width, dst in _WIN_SEGMENTS:
        while width:
            chip, col = divmod(src, _WIN_SHARD)
            n = min(width, _WIN_SHARD - col)
            out.append((chip, col, n, dst))
            src, dst, width = src + n, dst + n, width - n
    return out


def _win_pad(shards):
    rows = shards.shape[1]
    tr = _pick(rows, 256)

    def body(s_ref, o_ref):
        for chip, col, n, dst in _win_pieces():
            o_ref[:, dst:dst + n] = s_ref[chip, :, col:col + n]
        o_ref[:, C_IF + 8:NP] = jnp.zeros((tr, NP - C_IF - 8), shards.dtype)

    return pl.pallas_call(
        body, name="win_pad", grid=(rows // tr,), in_specs=[pl.BlockSpec((4, tr, _WIN_SHARD), lambda i: (0, i, 0))],
        out_specs=pl.BlockSpec((tr, NP), lambda i: (i, 0)), out_shape=jax.ShapeDtypeStruct((rows, NP), shards.dtype),
        compiler_params=_params(),
    )(shards)


def _win_unpad(wp):
    rows = wp.shape[0]
    tr = _pick(rows, 256)

    def body(p_ref, o_ref):
        for chip, col, n, dst in _win_pieces():
            o_ref[chip, :, col:col + n] = p_ref[:, dst:dst + n]

    return pl.pallas_call(
        body, name="win_unpad", grid=(rows // tr,), in_specs=[pl.BlockSpec((tr, NP), lambda i: (i, 0))],
        out_specs=pl.BlockSpec((4, tr, _WIN_SHARD), lambda i: (0, i, 0)),
        out_shape=jax.ShapeDtypeStruct((4, rows, _WIN_SHARD), wp.dtype), compiler_params=_params(),
    )(wp)


def _local_step(x, p, target, w, late_weights=None, early_grads=None, mid_grads=None, last_grad=None):
    t = x.shape[0]
    pb = p.astype(BF16)
    w = dict(w)

    h0 = _norm_fwd(x, w["norm_mix_g"], "norm_mix")
    proj, gates = _proj_in(h0, w["w_in"])
    qk = _conv_silu_fwd(proj, w["conv_qk"])
    grow, sneg_row = _gates_fwd(gates[:, 0:8].T, w["b_if"].reshape(8, 1))
    gcol, sneg_col = grow.T, sneg_row.T
    hraw, ya, cs, st = _mlstm_fwd(qk, proj, grow, gcol, w["mlstm_norm_g"])
    yb, lse = _swa_fwd(proj, w["sinks"])
    if late_weights is not None:
        w.update(late_weights(yb))
    merged, za, zb = _branch_merge(ya, yb, w["w_branch_a"], w["w_branch_b"], proj)
    x1, hn1 = _resid_norm_mm(merged, w["w_out"], x, w["norm_mlp_g"], "mm_out_norm")
    act, u = _up_act(hn1, w["w_up"])
    x2, hn2 = _resid_norm_mm(act, w["w_down"], x1, w["norm_ple_g"], "mm_down_norm")
    pp = _mm(pb, w["w_ple_proj"], "nn", F32, "mm_ple_proj")
    loss, d_final_g, dx3, dpp, dgpre = _ple_final_mm(hn2, w["w_ple_gate"], x2, pp, target, w["final_norm_g"])

    g = {"final_norm_g": d_final_g}
    g["w_ple_proj"] = _mm(pb, dpp, "tn", F32, "mm_d_ple_proj", out_chunks=4)
    g["w_ple_gate"] = _mm(hn2, dgpre, "tn", F32, "mm_d_ple_gate")
    dx2, dx2b, g["norm_ple_g"] = _norm_bwd_mm(dgpre, w["w_ple_gate"], x2, w["norm_ple_g"], dx3, "mm_dhn2_norm")
    g["w_down"] = _mm(act, dx2b, "tn", F32, "mm_d_down")
    du = _da_du(dx2b, w["w_down"], u)
    g["w_up"] = _mm(hn1, du, "tn", F32, "mm_d_up", out_chunks=4)
    dx1, dx1b, g["norm_mlp_g"] = _norm_bwd_mm(du, w["w_up"], x1, w["norm_mlp_g"], dx2, "mm_dhn1_norm")
    g["w_out"] = _mm(merged, dx1b, "tn", F32, "mm_d_out")
    dza, dzb, dproj = _dmerged_bwd(dx1b, w["w_out"], proj, za, zb)
    g["w_branch_a"] = _mm(ya, dza, "tn", F32, "mm_d_branch_a")
    g["w_branch_b"] = _mm(yb, dzb, "tn", F32, "mm_d_branch_b")
    gain = w["mlstm_norm_g"] if early_grads is None else w["mlstm_norm_g"] + early_grads(g)
    dyb = _mm(dzb, w["w_branch_b"], "nt", F32, "mm_dyb")
    dhraw, dproj, g["mlstm_norm_g"] = _dya_bwd(dza, w["w_branch_a"], hraw, proj, gain, dproj)
    if mid_grads is not None:
        sneg_col = sneg_col + mid_grads(dhraw)
    dqk, dproj, dif, g["b_if"] = _mlstm_bwd(qk, proj, grow, gcol, sneg_col, cs, st, hraw, dhraw, dproj)
    dc, g["conv_qk"] = _conv_silu_bwd_a(proj, w["conv_qk"], dqk)
    dproj = _conv_silu_bwd_b(dc, w["conv_qk"], dproj)
    dproj, dkv_self, dkv_prev, g["sinks"] = _swa_bwd(proj, w["sinks"], lse, dyb, dproj)
    dproj = _kv_combine(dkv_self, dkv_prev, dif, dproj)
    g["w_in"] = _mm(h0, dproj, "tn", F32, "mm_d_in")
    gain = w["norm_mix_g"] if last_grad is None else w["norm_mix_g"] + last_grad(g)
    grad_x, _, g["norm_mix_g"] = _norm_bwd_mm(dproj, w["w_in"], x, gain, dx1, "mm_dh0_norm")
    return loss, grad_x, g


_W4 = ("w_branch_a", "w_branch_b", "w_out", "w_ple_gate")
_SHARDED_NAMES = ("w_in", "w_up", "w_down", "w_ple_proj", "conv_qk") + _W4
_SMALL_ROWS = 16
_CONV_ROW = 8


def _group(s):
    return [s["w_in"], jnp.concatenate([s[n] for n in _W4], axis=0), s["w_up"], s["w_down"], s["w_ple_proj"]]


def _ungroup(arrs):
    out = {"w_in": arrs[0], "w_up": arrs[2], "w_down": arrs[3], "w_ple_proj": arrs[4]}
    rows = arrs[1].shape[0] // len(_W4)
    for i, n in enumerate(_W4):
        out[n] = arrs[1][i * rows:(i + 1) * rows]
    return out


def _rows_tile(rows):
    return 256 if rows % 256 == 0 else rows


_SMALL = ("norm_mix_g", "mlstm_norm_g", "norm_mlp_g", "norm_ple_g", "final_norm_g")


def _pack_small(vals, extra=None, conv=None):
    rows = [vals[n].reshape(1, D) for n in _SMALL]
    tail = [vals["b_if"].reshape(1, 8), vals["sinks"].reshape(1, SWH)]
    used = 8 + SWH
    if extra is not None:
        tail.append(extra.reshape(1, 1))
        used += 1
    tail.append(jnp.zeros((1, D - used), F32))
    rows.append(jnp.concatenate(tail, axis=1))
    rows.append(jnp.zeros((_CONV_ROW - len(rows), D), F32))
    rows.append(jnp.zeros((CONV, D), F32) if conv is None else conv)
    rows.append(jnp.zeros((_SMALL_ROWS - _CONV_ROW - CONV, D), F32))
    return jnp.concatenate(rows, axis=0)


def _unpack_small(slab, shapes):
    out = {n: slab[i].reshape(shapes[n]) for i, n in enumerate(_SMALL)}
    out["b_if"] = slab[5, 0:8].reshape(shapes["b_if"])
    out["sinks"] = slab[5, 8:8 + SWH].reshape(shapes["sinks"])
    return out


_MESH = pl.DeviceIdType.MESH
_HBM = pl.BlockSpec(memory_space=pltpu.HBM)
_VMEM = pl.BlockSpec(memory_space=pltpu.VMEM)


def _place():
    x, y, c = lax.axis_index("x"), lax.axis_index("y"), lax.axis_index("c")
    return x, y, c, 2 * x + y


def _chip_peer(x, y, r):
    return (x ^ (r >> 1), y ^ (r & 1))


def _half(ref, which):
    h = ref.shape[-2] // 2
    return pl.ds(which * h, h)


def _allgather_weights(shards, conv):
    n = len(shards)

    def body(*refs):
        ins, conv_ref = refs[:n], refs[n]
        outs, conv_out = refs[n + 1:2 * n + 1], refs[2 * n + 1]
        send_a, recv_a, send_b, recv_b, send_c, recv_c, local_sems = refs[2 * n + 2:]
        x, y, c, j = _place()
        sibling = (x, y, 1 - c)
        local = [pltpu.make_async_copy(ins[k], outs[k].at[j], local_sems.at[k]) for k in range(n)]
        local.append(pltpu.make_async_copy(conv_ref, conv_out.at[j], local_sems.at[n]))
        for cp in local:
            cp.start()

        def copy_a(k, r, chip):
            rows = _half(ins[k], c)
            return pltpu.make_async_remote_copy(
                src_ref=ins[k].at[rows], dst_ref=outs[k].at[chip, rows], send_sem=send_a.at[3 * k + r - 1],
                recv_sem=recv_a.at[3 * k + r - 1], device_id=(*_chip_peer(x, y, r), c), device_id_type=_MESH)

        def copy_b(k, r, chip, which):
            rows = _half(ins[k], which)
            return pltpu.make_async_remote_copy(
                src_ref=outs[k].at[chip, rows], dst_ref=outs[k].at[chip, rows], send_sem=send_b.at[3 * k + r - 1],
                recv_sem=recv_b.at[3 * k + r - 1], device_id=sibling, device_id_type=_MESH)

        def copy_c(r, chip):
            return pltpu.make_async_remote_copy(
                src_ref=conv_ref, dst_ref=conv_out.at[chip], send_sem=send_c.at[r - 1],
                recv_sem=recv_c.at[r - 1], device_id=(*_chip_peer(x, y, r), c), device_id_type=_MESH)

        for k in range(n):
            for r in (1, 2, 3):
                copy_a(k, r, j).start()
        for r in (1, 2, 3):
            copy_c(r, j).start()
        for k in range(n):
            for r in (1, 2, 3):
                copy_a(k, r, j ^ r).wait_recv()
                copy_b(k, r, j ^ r, c).start()
        for k in range(n):
            for r in (1, 2, 3):
                copy_b(k, r, j ^ r, 1 - c).wait_recv()
        for r in (1, 2, 3):
            copy_c(r, j ^ r).wait_recv()
        for k in range(n):
            for r in (1, 2, 3):
                copy_a(k, r, j).wait_send()
                copy_b(k, r, j ^ r, c).wait_send()
        for r in (1, 2, 3):
            copy_c(r, j).wait_send()
        for cp in local:
            cp.wait()

    return pl.pallas_call(
        body, name="allgather_weights",
        out_shape=[jax.ShapeDtypeStruct((4,) + s.shape, s.dtype) for s in shards]
        + [jax.ShapeDtypeStruct((4,) + conv.shape, F32)],
        in_specs=[_HBM] * (n + 1), out_specs=[_HBM] * (n + 1),
        scratch_shapes=[pltpu.SemaphoreType.DMA((3 * n,))] * 4 + [pltpu.SemaphoreType.DMA((3,))] * 2
        + [pltpu.SemaphoreType.DMA((n + 1,))],
    )(*shards, conv)


_SEM = pl.BlockSpec(memory_space=pltpu.SEMAPHORE)
_DATAFLOW = pltpu.SideEffectType.DATAFLOW_SIDE_EFFECTING


def _late_peer_copy(src_ref, land_ref, send_sems, recv_sems, x, y, c, j, r, chip):
    return pltpu.make_async_remote_copy(
        src_ref=src_ref, dst_ref=land_ref.at[chip], send_sem=send_sems.at[r - 1], recv_sem=recv_sems.at[r - 1],
        device_id=(*_chip_peer(x, y, r), c), device_id_type=_MESH)


def _late_gather_start(rest):
    def body(rest_ref, land_ref, send_sems, recv_sems, rest_thru, land_thru, token):
        x, y, c, j = _place()
        for r in (1, 2, 3):
            _late_peer_copy(rest_ref, land_ref, send_sems, recv_sems, x, y, c, j, r, j).start()
        token[...] = jnp.zeros_like(token)

    j = 2 * lax.axis_index("x") + lax.axis_index("y")
    land = lax.dynamic_update_slice(lax.empty((4,) + rest.shape, rest.dtype), rest[None], (j, 0, 0))
    return pl.pallas_call(
        body, name="late_gather_start",
        out_shape=(pltpu.SemaphoreType.DMA((3,)), pltpu.SemaphoreType.DMA((3,)), pltpu.HBM(rest.shape, rest.dtype),
                   pltpu.HBM(land.shape, land.dtype), jax.ShapeDtypeStruct((8, 128), F32)),
        in_specs=(_HBM, _HBM), out_specs=(_SEM, _SEM, _HBM, _HBM, _VMEM), input_output_aliases={0: 2, 1: 3},
        compiler_params=pltpu.CompilerParams(has_side_effects=_DATAFLOW),
    )(pltpu.with_memory_space_constraint(rest, pltpu.HBM), pltpu.with_memory_space_constraint(land, pltpu.HBM))


def _late_gather_wait(send_sems, recv_sems, rest_thru, land_thru, after):
    def body(rest_ref, land_ref, send_sems, recv_sems, after_ref, rest_dead, got_ref):
        x, y, c, j = _place()
        for r in (1, 2, 3):
            cp = _late_peer_copy(rest_ref, land_ref, send_sems, recv_sems, x, y, c, j, r, j ^ r)
            cp.wait_send()
            cp.wait_recv()

    return pl.pallas_call(
        body, name="late_gather_wait",
        out_shape=(pltpu.HBM(rest_thru.shape, rest_thru.dtype), pltpu.HBM(land_thru.shape, land_thru.dtype)),
        in_specs=(_HBM, _HBM, _SEM, _SEM, _ANY), out_specs=(_HBM, _HBM), input_output_aliases={0: 0, 1: 1},
        compiler_params=pltpu.CompilerParams(has_side_effects=_DATAFLOW),
    )(rest_thru, land_thru, send_sems, recv_sems, after)[1]


def _pair_exchange(gs, name):
    n = len(gs)

    def body(*refs):
        ins, outs, send_sems, recv_sems = refs[:n], refs[n:2 * n], refs[2 * n], refs[2 * n + 1]
        x, y, c, _ = _place()
        cps = [pltpu.make_async_remote_copy(
            src_ref=ins[k].at[:, _half(ins[k], 1 - c)], dst_ref=outs[k], send_sem=send_sems.at[k],
            recv_sem=recv_sems.at[k], device_id=(x, y, 1 - c), device_id_type=_MESH) for k in range(n)]
        for cp in cps:
            cp.start()
        for cp in cps:
            cp.wait()

    return pl.pallas_call(
        body, name=name,
        out_shape=[jax.ShapeDtypeStruct((4, g.shape[1] // 2, g.shape[2]), F32) for g in gs],
        in_specs=[_HBM] * n, out_specs=[_HBM] * n, scratch_shapes=[pltpu.SemaphoreType.DMA((n,))] * 2,
    )(*gs)


def _pair_sum(g, theirs, c, name):
    _, h, cols = theirs.shape
    tr = _rows_tile(h)
    nb = h // tr

    def body(c_ref, a_ref, b_ref, o_ref, ob_ref):
        s = a_ref[...] + b_ref[...]
        o_ref[...] = s
        ob_ref[...] = s.astype(BF16)

    blk = pl.BlockSpec((1, tr, cols), lambda k, i, c_ref: (k, i, 0))
    return pl.pallas_call(
        body, name=name,
        grid_spec=pltpu.PrefetchScalarGridSpec(
            num_scalar_prefetch=1, grid=(4, nb),
            in_specs=[pl.BlockSpec((1, tr, cols), lambda k, i, c_ref: (k, c_ref[0] * nb + i, 0)), blk],
            out_specs=[blk, blk]),
        out_shape=[jax.ShapeDtypeStruct(theirs.shape, F32), jax.ShapeDtypeStruct(theirs.shape, BF16)],
        compiler_params=_params(),
    )(c.reshape(1).astype(jnp.int32), g, theirs)


def _chip_copies(srcs, lands, send_sems, recv_sems):
    x, y, c, j = _place()
    return [pltpu.make_async_remote_copy(
        src_ref=srcs[k].at[j ^ r], dst_ref=lands[k].at[r - 1], send_sem=send_sems.at[3 * k + r - 1],
        recv_sem=recv_sems.at[3 * k + r - 1], device_id=(*_chip_peer(x, y, r), c), device_id_type=_MESH)
        for k in range(len(srcs)) for r in (1, 2, 3)]


def _pair_copies(srcs, lands, send_sems, recv_sems):
    x, y, c, _ = _place()
    return [pltpu.make_async_remote_copy(
        src_ref=srcs[k].at[:, _half(srcs[k], 1 - c)], dst_ref=lands[k], send_sem=send_sems.at[k],
        recv_sem=recv_sems.at[k], device_id=(x, y, 1 - c), device_id_type=_MESH) for k in range(len(srcs))]


def _split_start(name, srcs, lands, copies, n_sems):
    n = len(srcs)

    def body(*refs):
        for cp in copies(refs[:n], refs[n:2 * n], refs[2 * n], refs[2 * n + 1]):
            cp.start()
        refs[-1][...] = jnp.zeros_like(refs[-1])

    arrays = list(srcs) + list(lands)
    out = pl.pallas_call(
        body, name=name,
        out_shape=(pltpu.SemaphoreType.DMA((n_sems,)), pltpu.SemaphoreType.DMA((n_sems,)),
                   *[pltpu.HBM(a.shape, a.dtype) for a in arrays], jax.ShapeDtypeStruct((8, 128), F32)),
        in_specs=[_HBM] * (2 * n), out_specs=(_SEM, _SEM, *([_HBM] * (2 * n)), _VMEM),
        input_output_aliases={k: 2 + k for k in range(2 * n)},
        compiler_params=pltpu.CompilerParams(has_side_effects=_DATAFLOW),
    )(*[pltpu.with_memory_space_constraint(a, pltpu.HBM) for a in arrays])
    return out[0], out[1], list(out[2:2 + n]), list(out[2 + n:2 + 2 * n]), out[-1]


def _split_wait(name, send_sems, recv_sems, srcs_thru, lands_thru, after, copies):
    n = len(srcs_thru)

    def body(*refs):
        for cp in copies(refs[:n], refs[n:2 * n], refs[2 * n], refs[2 * n + 1]):
            cp.wait_send()
            cp.wait_recv()

    arrays = list(srcs_thru) + list(lands_thru)
    out = pl.pallas_call(
        body, name=name, out_shape=tuple(pltpu.HBM(a.shape, a.dtype) for a in arrays),
        in_specs=[_HBM] * (2 * n) + [_SEM, _SEM, _ANY], out_specs=tuple([_HBM] * (2 * n)),
        input_output_aliases={k: k for k in range(2 * n)},
        compiler_params=pltpu.CompilerParams(has_side_effects=_DATAFLOW),
    )(*arrays, send_sems, recv_sems, after)
    return list(out[:n]), list(out[n:])


def _chip_exchange_start(ss, tag):
    lands = [lax.empty((3,) + s.shape[1:], s.dtype) for s in ss]
    return _split_start("chip_exchange_start_" + tag, ss, lands, _chip_copies, 3 * len(ss))


def _chip_exchange_wait(send_sems, recv_sems, ss_thru, lands_thru, after, tag):
    return _split_wait("chip_exchange_wait_" + tag, send_sems, recv_sems, ss_thru, lands_thru, after, _chip_copies)[1]


def _pair_exchange_start(gs, tag):
    lands = [lax.empty((4, g.shape[1] // 2, g.shape[2]), g.dtype) for g in gs]
    return _split_start("pair_exchange_start_" + tag, gs, lands, _pair_copies, len(gs))


def _pair_exchange_wait(send_sems, recv_sems, gs_thru, lands_thru, after, tag):
    return _split_wait("pair_exchange_wait_" + tag, send_sems, recv_sems, gs_thru, lands_thru, after, _pair_copies)


def _reduce4(own, others, j, c, name):
    _, h, cols = own.shape
    tr = _rows_tile(h)
    nb = h // tr

    def body(idx_ref, s_ref, a0, a1, a2, o_ref):
        o_ref[...] = ((s_ref[0] + a0[0].astype(F32)) + a1[0].astype(F32)) + a2[0].astype(F32)

    def other(r):
        return pl.BlockSpec((1, tr, cols), lambda i, idx_ref: (r, i, 0))

    return pl.pallas_call(
        body, name=name,
        grid_spec=pltpu.PrefetchScalarGridSpec(
            num_scalar_prefetch=1, grid=(nb,),
            in_specs=[pl.BlockSpec((1, tr, cols), lambda i, idx_ref: (idx_ref[0], i, 0)), other(0), other(1), other(2)],
            out_specs=pl.BlockSpec((tr, cols), lambda i, idx_ref: (idx_ref[1] * nb + i, 0))),
        out_shape=jax.ShapeDtypeStruct((2 * h, cols), F32), compiler_params=_params(),
    )(jnp.stack([j, c]).astype(jnp.int32), own, others, others, others)


def _sibling_share(fulls):
    n = len(fulls)

    def body(*refs):
        outs, send_sems, recv_sems = refs[n:2 * n], refs[2 * n], refs[2 * n + 1]
        x, y, c, _ = _place()
        cps = [pltpu.make_async_remote_copy(
            src_ref=outs[k].at[_half(outs[k], c)], dst_ref=outs[k].at[_half(outs[k], c)], send_sem=send_sems.at[k],
            recv_sem=recv_sems.at[k], device_id=(x, y, 1 - c), device_id_type=_MESH) for k in range(n)]
        for cp in cps:
            cp.start()
        for cp in cps:
            cp.wait()

    return pl.pallas_call(
        body, name="sibling_share", out_shape=[jax.ShapeDtypeStruct(f.shape, F32) for f in fulls],
        in_specs=[_HBM] * n, out_specs=[_HBM] * n, input_output_aliases={k: k for k in range(n)},
        scratch_shapes=[pltpu.SemaphoreType.DMA((n,))] * 2,
    )(*fulls)


def _adamw(w, g, m, v):
    m1 = ADAM_B1 * m + (1.0 - ADAM_B1) * g
    v1 = ADAM_B2 * v + (1.0 - ADAM_B2) * (g * g)
    m_hat = m1 / (1.0 - ADAM_B1 ** ADAM_STEP)
    v_hat = v1 / (1.0 - ADAM_B2 ** ADAM_STEP)
    delta = -ADAM_LR * (m_hat / (jnp.sqrt(v_hat) + ADAM_EPS) + ADAM_WD * w)
    return delta, m1, v1


def _adamw_call(w, g, m, v, name):
    rows, cols = w.shape

    def body(w_ref, g_ref, m_ref, v_ref, d_out, m_out, v_out):
        delta, m1, v1 = _adamw(w_ref[...], g_ref[...], m_ref[...], v_ref[...])
        d_out[...] = delta
        m_out[...] = m1
        v_out[...] = v1

    if rows % 8 == 0:
        tr = _rows_tile(rows)
        blk, grid = pl.BlockSpec((tr, cols), lambda i: (i, 0)), (rows // tr,)
    else:
        blk, grid = pl.BlockSpec((rows, 128), lambda i: (0, i)), (cols // 128,)
    return pl.pallas_call(
        body, name=name, grid=grid, in_specs=[blk] * 4, out_specs=[blk] * 3,
        out_shape=[jax.ShapeDtypeStruct((rows, cols), F32)] * 3, compiler_params=_params(),
    )(w, g, m, v)


def _small_allreduce(vals):
    def body(v_ref, out_ref, buf, send_sems, recv_sems):
        x, y, c, j = _place()
        me = 2 * j + c
        buf[0] = v_ref[...]

        def copy(r):
            return pltpu.make_async_remote_copy(
                src_ref=v_ref, dst_ref=buf.at[r], send_sem=send_sems.at[r - 1], recv_sem=recv_sems.at[r - 1],
                device_id=(x ^ (r >> 2), y ^ ((r >> 1) & 1), c ^ (r & 1)), device_id_type=_MESH)

        for r in range(1, 8):
            copy(r).start()
        for r in range(1, 8):
            copy(r).wait()
        acc = buf[me ^ 0]
        for d in range(1, 8):
            acc = acc + buf[me ^ d]
        out_ref[...] = acc

    return pl.pallas_call(
        body, name="small_allreduce", out_shape=jax.ShapeDtypeStruct((_SMALL_ROWS, D), F32),
        in_specs=[_VMEM], out_specs=_VMEM,
        scratch_shapes=[pltpu.VMEM((8, _SMALL_ROWS, D), F32), pltpu.SemaphoreType.DMA((7,)),
                        pltpu.SemaphoreType.DMA((7,))],
    )(vals)


_NAMES = ("norm_mix_g", "w_in", "conv_qk", "b_if", "mlstm_norm_g", "sinks", "w_branch_a", "w_branch_b", "w_out",
          "norm_mlp_g", "w_up", "w_down", "norm_ple_g", "w_ple_gate", "w_ple_proj", "final_norm_g")
_GROUP_NAMES = ("w_in", "w4", "w_up", "w_down", "w_ple_proj")


def _step(x, p, target, w, m, v):
    c = lax.axis_index("c")
    j = 2 * lax.axis_index("x") + lax.axis_index("y")

    def shards(d):
        return {n: d[n][0] for n in _SHARDED_NAMES}

    ws = shards(w)
    w_in_all, conv_all = _allgather_weights([ws["w_in"].astype(BF16)], ws["conv_qk"])
    rows_pp = PLE * (D // 4) // D
    rest = jnp.concatenate([ws[n] for n in _W4] + [ws["w_up"], ws["w_down"], ws["w_ple_proj"].reshape(rows_pp, D)],
                           axis=0)
    rest = (rest + 0.0 * conv_all[0, 0, 0]).astype(BF16)
    send_sems, recv_sems, rest_thru, land_thru, token = _late_gather_start(rest)
    full = {n: w[n] for n in ("mlstm_norm_g", "norm_mlp_g", "norm_ple_g", "b_if", "sinks")}
    full["norm_mix_g"] = w["norm_mix_g"] + token[0, 0]
    full["final_norm_g"] = w["final_norm_g"].reshape(1, D)
    full["w_in"] = _win_pad(w_in_all)
    full["conv_qk"] = jnp.swapaxes(conv_all, 0, 1).reshape(CONV, D)

    def late_weights(after):
        land = _late_gather_wait(send_sems, recv_sems, rest_thru, land_thru, after)
        out = {n: land[:, i * (D // 4):(i + 1) * (D // 4)].reshape(D, D) for i, n in enumerate(_W4)}
        out["w_up"] = land[:, D:2 * D]
        out["w_down"] = land[:, 2 * D:3 * D].reshape(DFF, D)
        out["w_ple_proj"] = land[:, 3 * D:3 * D + rows_pp].reshape(4, PLE, D // 4)
        return out

    def pair_sums(by_dest, names, tag):
        theirs = _pair_exchange(by_dest, "pair_exchange_" + tag)
        return [_pair_sum(a, b, c, "pair_sum_" + n) for a, b, n in zip(by_dest, theirs, names)]

    early, last = {}, {}

    def early_grads(g):
        by_dest = [jnp.stack([g[n].reshape(4, D // 4, D) for n in _W4], axis=1).reshape(4, D, D),
                   g["w_up"], g["w_down"].reshape(4, DFF // 4, D), g["w_ple_proj"]]
        *early["pair"], token = _pair_exchange_start(by_dest, "early")
        return token[0, 0]

    def mid_grads(after):
        by_dest, theirs = _pair_exchange_wait(*early["pair"], after, "early")
        early["sums"] = [_pair_sum(a, b, c, "pair_sum_" + n) for a, b, n in zip(by_dest, theirs, _GROUP_NAMES[1:])]
        *early["flight"], token = _chip_exchange_start([s[1] for s in early["sums"]], "early")
        return token[0, 0]

    def last_grad(g):
        last["sums"] = pair_sums([_win_unpad(g["w_in"])], _GROUP_NAMES[:1], "w_in")
        *last["flight"], token = _chip_exchange_start([s[1] for s in last["sums"]], "w_in")
        return token[0, 0]

    loss, grad_x, g = _local_step(x[0], p[0, 0], target[0], full, late_weights, early_grads, mid_grads, last_grad)

    others = _chip_exchange_wait(*last["flight"], grad_x, "w_in")
    others += _chip_exchange_wait(*early["flight"], others[0], "early")
    sums = last["sums"] + early["sums"]
    halves = [_reduce4(s[0], b, j, c, "reduce4_" + n) for s, b, n in zip(sums, others, _GROUP_NAMES)]
    grads = _sibling_share(halves)

    small_g = _small_allreduce(_pack_small(g, extra=loss, conv=g["conv_qk"]))
    conv_g = lax.dynamic_slice(small_g[_CONV_ROW:_CONV_ROW + CONV], (0, j * (D // 4)), (CONV, D // 4))

    ms, vs = shards(m), shards(v)
    upd = [_adamw_call(wa, ga, ma, va, "adamw_" + n)
           for wa, ga, ma, va, n in list(zip(_group(ws), grads, _group(ms), _group(vs), _GROUP_NAMES))[1:]]
    upd_in = _adamw_call(*[jnp.swapaxes(a, 0, 1) for a in (ws["w_in"], grads[0], ms["w_in"], vs["w_in"])], "adamw_w_in")
    upd = [[jnp.swapaxes(a, 0, 1) for a in upd_in]] + upd
    conv_upd = _adamw_call(ws["conv_qk"], conv_g, ms["conv_qk"], vs["conv_qk"], "adamw_conv")
    small_upd = _adamw_call(_pack_small(w), small_g, _pack_small(m), _pack_small(v), "adamw_small")

    shapes = {n: w[n].shape for n in _NAMES}
    res = []
    for k in range(4):
        big = _ungroup(list(grads) if k == 0 else [u[k - 1] for u in upd])
        big["conv_qk"] = conv_g if k == 0 else conv_upd[k - 1]
        leaves = _unpack_small(small_g if k == 0 else small_upd[k - 1], shapes)
        leaves.update({n: a.reshape(shapes[n]) for n, a in big.items()})
        res.append(leaves)

    out = [small_g[5, 8 + SWH], grad_x[None]]
    for k in range(4):
        out += [res[k][n] for n in _NAMES]
    return tuple(out)


def kernel(x, p, norm_mix_g, w_in, conv_qk, b_if, mlstm_norm_g, sinks, w_branch_a, w_branch_b, w_out, norm_mlp_g, w_up, w_down, norm_ple_g, w_ple_gate, w_ple_proj, final_norm_g, loss_target, m_norm_mix_g, m_w_in, m_conv_qk, m_b_if, m_mlstm_norm_g, m_sinks, m_w_branch_a, m_w_branch_b, m_w_out, m_norm_mlp_g, m_w_up, m_w_down, m_norm_ple_g, m_w_ple_gate, m_w_ple_proj, m_final_norm_g, v_norm_mix_g, v_w_in, v_conv_qk, v_b_if, v_mlstm_norm_g, v_sinks, v_w_branch_a, v_w_branch_b, v_w_out, v_norm_mlp_g, v_w_up, v_w_down, v_norm_ple_g, v_w_ple_gate, v_w_ple_proj, v_final_norm_g):
    w = dict(zip(_NAMES, (norm_mix_g, w_in, conv_qk, b_if, mlstm_norm_g, sinks, w_branch_a, w_branch_b, w_out,
                          norm_mlp_g, w_up, w_down, norm_ple_g, w_ple_gate, w_ple_proj, final_norm_g)))
    m = dict(zip(_NAMES, (m_norm_mix_g, m_w_in, m_conv_qk, m_b_if, m_mlstm_norm_g, m_sinks, m_w_branch_a,
                          m_w_branch_b, m_w_out, m_norm_mlp_g, m_w_up, m_w_down, m_norm_ple_g, m_w_ple_gate,
                          m_w_ple_proj, m_final_norm_g)))
    v = dict(zip(_NAMES, (v_norm_mix_g, v_w_in, v_conv_qk, v_b_if, v_mlstm_norm_g, v_sinks, v_w_branch_a,
                          v_w_branch_b, v_w_out, v_norm_mlp_g, v_w_up, v_w_down, v_norm_ple_g, v_w_ple_gate,
                          v_w_ple_proj, v_final_norm_g)))
    return _step(x, p, loss_target, w, m, v)
```

```python
import jax
import jax.numpy as jnp
from jax import lax
from jax.experimental import pallas as pl
from jax.experimental.pallas import tpu as pltpu

F32 = jnp.float32
BF16 = jnp.bfloat16

D = 1024
PLE = 256
MLH = 4
DQK = 128
DV = 256
CONV = 4
CHUNK = 128
SWH = 16
SWKV = 4
SWG = SWH // SWKV
HD = 64
WIN = 128
DFF = 4096
EPS = 1e-6
N_IN = 6664
NP = 7168
C_QK, C_V, C_O, C_QSW, C_GA, C_GB, C_KV, C_IF = 0, 1024, 2048, 3072, 4096, 5120, 6144, 6656
IFW = NP - C_IF

ADAM_LR = 0.001
ADAM_B1 = 0.9
ADAM_B2 = 0.999
ADAM_EPS = 1e-08
ADAM_WD = 0.01
ADAM_STEP = 10

TOK_TILE = 512
VMEM_LIMIT = 58 * 1024 * 1024


def _params(**kw):
    return pltpu.CompilerParams(vmem_limit_bytes=VMEM_LIMIT, **kw)


def _pick(n, cap):
    if n <= cap:
        return n
    t = cap - cap % 128
    while t > 128 and n % t:
        t -= 128
    assert n % t == 0, (n, cap)
    return t


def _dot(a, b, dims):
    return lax.dot_general(a, b, (dims, ((), ())), preferred_element_type=F32)


def _dot_nn(a, b):
    return _dot(a, b, ((1,), (0,)))


def _dot_nt(a, b):
    return _dot(a, b, ((1,), (1,)))


def _dot_tn(a, b):
    return _dot(a, b, ((0,), (0,)))


def _sigmoid(x):
    return 1.0 / (1.0 + jnp.exp(-x))


def _mm(a, b, mode, out_dtype, name, out_chunks=1):
    bch = b.shape[0] if b.ndim == 3 else 1
    brows, bcols = b.shape[-2], b.shape[-1] * bch
    if mode == "nn":
        (m, k), (k2, n) = a.shape, (brows, bcols)
    elif mode == "nt":
        (m, k), (n, k2) = a.shape, (brows, bcols)
    else:
        (k, m), (k2, n) = a.shape, (brows, bcols)
    assert k == k2, (a.shape, b.shape, mode)
    n_cap = n // max(out_chunks, 1 if mode == "nt" else bch)
    k_cap = k // bch if mode == "nt" else k
    tm, tn, tk = _pick(m, 1024), _pick(n_cap, 1024), _pick(k_cap, 2048)
    nk = k // tk
    if mode == "nn":
        a_spec = pl.BlockSpec((tm, tk), lambda i, j, kk: (i, kk))
        if bch > 1:
            bpc = (n // bch) // tn
            b_spec = pl.BlockSpec((None, tk, tn), lambda i, j, kk: (j // bpc, kk, j % bpc))
        else:
            b_spec = pl.BlockSpec((tk, tn), lambda i, j, kk: (kk, j))
        dot = _dot_nn
    elif mode == "nt":
        a_spec = pl.BlockSpec((tm, tk), lambda i, j, kk: (i, kk))
        if bch > 1:
            bpc = (k // bch) // tk
            b_spec = pl.BlockSpec((None, tn, tk), lambda i, j, kk: (kk // bpc, j, kk % bpc))
        else:
            b_spec = pl.BlockSpec((tn, tk), lambda i, j, kk: (j, kk))
        dot = _dot_nt
    else:
        assert bch == 1
        a_spec = pl.BlockSpec((tk, tm), lambda i, j, kk: (kk, i))
        b_spec = pl.BlockSpec((tk, tn), lambda i, j, kk: (kk, j))
        dot = _dot_tn
    if out_chunks > 1:
        npc = (n // out_chunks) // tn
        out_spec = pl.BlockSpec((None, tm, tn), lambda i, j, kk: (j // npc, i, j % npc))
        out_shape = jax.ShapeDtypeStruct((out_chunks, m, n // out_chunks), out_dtype)
    else:
        out_spec = pl.BlockSpec((tm, tn), lambda i, j, kk: (i, j))
        out_shape = jax.ShapeDtypeStruct((m, n), out_dtype)

    def body(a_ref, b_ref, o_ref, acc_ref):
        kk = pl.program_id(2)

        @pl.when(kk == 0)
        def _():
            acc_ref[...] = jnp.zeros_like(acc_ref)

        acc_ref[...] += dot(a_ref[...], b_ref[...])

        @pl.when(kk == nk - 1)
        def _():
            o_ref[...] = acc_ref[...].astype(out_dtype)

    return pl.pallas_call(
        body, name=name, grid=(m // tm, n // tn, nk),
        in_specs=[a_spec, b_spec], out_specs=out_spec, out_shape=out_shape,
        scratch_shapes=[pltpu.VMEM((tm, tn), F32)],
        compiler_params=_params(dimension_semantics=("parallel", "parallel", "arbitrary")),
    )(a, b)


def _tile(col0=0):
    return lambda tm, tn: pl.BlockSpec((tm, tn), lambda i, j, kk: (i, col0 // tn + j))


def _row():
    return lambda tm, tn: pl.BlockSpec((1, tn), lambda i, j, kk: (0, j))


def _mm_ep(pairs, mode, name, epilogue, ins, outs, tm, tn, aliases=None):
    a0, b0 = pairs[0]
    bch = b0.shape[0] if b0.ndim == 3 else 1
    m, k = a0.shape
    tm = _pick(m, tm)
    n = b0.shape[-1] * bch if mode == "nn" else b0.shape[-2]
    tk = _pick(k // bch if mode == "nt" else k, 2048)
    nk = k // tk
    a_spec = pl.BlockSpec((tm, tk), lambda i, j, kk: (i, kk))
    if mode == "nn":
        dot = _dot_nn
        if bch > 1:
            bpc = (n // bch) // tn
            b_spec = pl.BlockSpec((None, tk, tn), lambda i, j, kk: (j // bpc, kk, j % bpc))
        else:
            b_spec = pl.BlockSpec((tk, tn), lambda i, j, kk: (kk, j))
    else:
        dot = _dot_nt
        if bch > 1:
            bpc = (k // bch) // tk
            b_spec = pl.BlockSpec((None, tn, tk), lambda i, j, kk: (kk // bpc, j, kk % bpc))
        else:
            b_spec = pl.BlockSpec((tn, tk), lambda i, j, kk: (j, kk))
    npair, nin, nout = len(pairs), len(ins), len(outs)

    def body(*refs):
        ab = refs[:2 * npair]
        in_refs = refs[2 * npair:2 * npair + nin]
        out_refs = refs[2 * npair + nin:2 * npair + nin + nout]
        accs = refs[2 * npair + nin + nout:]
        i, j, kk = pl.program_id(0), pl.program_id(1), pl.program_id(2)
        for p in range(npair):
            prod = dot(ab[2 * p][...], ab[2 * p + 1][...])

            @pl.when(kk == 0)
            def _():
                accs[p][...] = prod

            @pl.when(kk > 0)
            def _():
                accs[p][...] += prod

        @pl.when(kk == nk - 1)
        def _():
            epilogue([acc[...] for acc in accs], in_refs, out_refs, i, j)

    operands = [x for pair in pairs for x in pair] + [a for a, _ in ins]
    io_alias = {2 * npair + i: o for i, o in (aliases or {}).items()}
    return pl.pallas_call(
        body, name=name, grid=(m // tm, n // tn, nk),
        in_specs=[a_spec, b_spec] * npair + [mk(tm, tn) for _, mk in ins],
        out_specs=[mk(tm, tn) for _, mk in outs], out_shape=[s for s, _ in outs],
        scratch_shapes=[pltpu.VMEM((tm, tn), F32)] * npair, input_output_aliases=io_alias,
        compiler_params=_params(dimension_semantics=("arbitrary", "arbitrary", "arbitrary")),
    )(*operands)


def _tok(w, j=0):
    return pl.BlockSpec((TOK_TILE, w), lambda i: (i, j))


def _rep(shape):
    return pl.BlockSpec(shape, lambda i: (0,) * len(shape))


def _rms(x):
    rstd = lax.rsqrt(jnp.mean(x * x, axis=-1, keepdims=True) + EPS)
    return x * rstd, rstd


def _rms_bwd(xn, rstd, dxn):
    return rstd * (dxn - xn * jnp.mean(dxn * xn, axis=-1, keepdims=True))


def _norm_fwd(x, g, name):
    t = x.shape[0]

    def body(x_ref, g_ref, h_ref):
        xn, _ = _rms(x_ref[...])
        h_ref[...] = (xn * g_ref[...]).astype(BF16)

    return pl.pallas_call(
        body, name=name, grid=(t // TOK_TILE,), in_specs=[_tok(D), _rep((1, D))], out_specs=_tok(D),
        out_shape=jax.ShapeDtypeStruct((t, D), BF16), compiler_params=_params(),
    )(x, g)


def _halo_prev(w, j=0, rows=8):
    r = TOK_TILE // rows
    return pl.BlockSpec((rows, w), lambda i: (jnp.maximum(i * r - 1, 0), j))


def _last8(halo_ref):
    return halo_ref[...].astype(F32)[halo_ref.shape[0] - 8:]


def _halo_next(w, nt, j=0):
    r = TOK_TILE // 8
    return pl.BlockSpec((8, w), lambda i: (jnp.minimum((i + 1) * r, nt * r - 1), j))


def _shift_down(x, halo, s):
    if s == 0:
        return x
    r = pltpu.roll(x, s, 0)
    hs = pltpu.roll(halo, s, 0)
    row = lax.broadcasted_iota(jnp.int32, hs.shape, 0)
    top = jnp.where(row < s, hs, r[0:8])
    return jnp.concatenate([top, r[8:]], axis=0)


def _shift_up(x, halo, s):
    if s == 0:
        return x
    n = x.shape[0]
    r = pltpu.roll(x, n - s, 0)
    hs = pltpu.roll(halo, 8 - s, 0)
    row = lax.broadcasted_iota(jnp.int32, hs.shape, 0)
    bot = jnp.where(row >= 8 - s, hs, r[n - 8:])
    return jnp.concatenate([r[:n - 8], bot], axis=0)


def _bf(x):
    return x.astype(BF16).astype(F32)


def _conv_taps(x, halo, w):
    x, halo, w = _bf(x), _bf(halo), _bf(w)
    acc = x * w[CONV - 1:CONV, :]
    for j in range(CONV - 1):
        acc = acc + _shift_down(x, halo, CONV - 1 - j) * w[j:j + 1, :]
    return acc


_Q_SCALE = DQK ** -0.5


def _qscale_row():
    lane = lax.broadcasted_iota(jnp.int32, (1, D), 1)
    return jnp.where(lane < MLH * DQK, _Q_SCALE, 1.0).astype(F32)


def _conv_silu_fwd(proj, conv_w):
    t = proj.shape[0]

    def body(x_ref, halo_ref, w_ref, o_ref):
        halo = jnp.where(pl.program_id(0) > 0, _last8(halo_ref), 0.0)
        c = _conv_taps(x_ref[...].astype(F32), halo, w_ref[...])
        o_ref[...] = (c * _sigmoid(c) * _qscale_row()).astype(BF16)

    return pl.pallas_call(
        body, name="conv_silu_fwd", grid=(t // TOK_TILE,),
        in_specs=[_tok(D, C_QK // D), _halo_prev(D, C_QK // D, 16), _rep((CONV, D))], out_specs=_tok(D),
        out_shape=jax.ShapeDtypeStruct((t, D), BF16), compiler_params=_params(),
    )(proj, proj, conv_w)


def _conv_silu_bwd_a(proj, conv_w, dqk):
    t = proj.shape[0]

    def body(x_ref, halo_ref, w_ref, d_ref, dc_ref, dw_ref):
        @pl.when(pl.program_id(0) == 0)
        def _():
            dw_ref[...] = jnp.zeros_like(dw_ref)

        halo = jnp.where(pl.program_id(0) > 0, _last8(halo_ref), 0.0)
        x = x_ref[...].astype(F32)
        c = _conv_taps(x, halo, w_ref[...])
        s = _sigmoid(c)
        dc = d_ref[...] * _qscale_row() * (s * (1.0 + c * (1.0 - s)))
        dc_ref[...] = dc
        dcb, xb, halo_b = _bf(dc), _bf(x), _bf(halo)
        for j in range(CONV):
            dw_ref[j:j + 1, :] += jnp.sum(dcb * _shift_down(xb, halo_b, CONV - 1 - j), axis=0, keepdims=True)

    return pl.pallas_call(
        body, name="conv_silu_bwd_a", grid=(t // TOK_TILE,),
        in_specs=[_tok(D, C_QK // D), _halo_prev(D, C_QK // D, 16), _rep((CONV, D)), _tok(D)],
        out_specs=[_tok(D), _rep((CONV, D))],
        out_shape=[jax.ShapeDtypeStruct((t, D), F32), jax.ShapeDtypeStruct((CONV, D), F32)],
        compiler_params=_params(),
    )(proj, proj, conv_w, dqk)


def _conv_silu_bwd_b(dc, conv_w, dproj):
    t = dc.shape[0]
    nt = t // TOK_TILE

    def body(dc_ref, halo_ref, w_ref, _, dx_ref):
        halo = _bf(jnp.where(pl.program_id(0) < nt - 1, halo_ref[...], 0.0))
        dcv = _bf(dc_ref[...])
        w = _bf(w_ref[...])
        acc = dcv * w[CONV - 1:CONV, :]
        for j in range(CONV - 1):
            acc = acc + _shift_up(dcv, halo, CONV - 1 - j) * w[j:j + 1, :]
        dx_ref[...] = acc.astype(BF16)

    return pl.pallas_call(
        body, name="conv_silu_bwd_b", grid=(nt,), in_specs=[_tok(D), _halo_next(D, nt), _rep((CONV, D)), _ANY],
        out_specs=_tok(D, C_QK // D), out_shape=jax.ShapeDtypeStruct((t, NP), BF16),
        input_output_aliases={3: 0}, compiler_params=_params(),
    )(dc, dc, conv_w, dproj)


def _gates_fwd(pre_rows, bias_col):
    t = pre_rows.shape[1]

    def body(p_ref, b_ref, g_ref, s_ref):
        z = p_ref[...] + b_ref[...]
        lf = jnp.minimum(z, 0.0) - jnp.log(1.0 + jnp.exp(-jnp.abs(z)))
        lane = lax.broadcasted_iota(jnp.int32, z.shape, 1) % CHUNK
        cum = lf
        s = 1
        while s < CHUNK:
            cum = cum + jnp.where(lane >= s, pltpu.roll(cum, s, 1), 0.0)
            s *= 2
        sub = lax.broadcasted_iota(jnp.int32, z.shape, 0)
        g_ref[...] = jnp.where(sub < MLH, z, cum)
        s_ref[...] = _sigmoid(-z)

    return pl.pallas_call(
        body, name="gates_fwd",
        out_shape=[jax.ShapeDtypeStruct((8, t), F32), jax.ShapeDtypeStruct((8, t), F32)],
        compiler_params=_params(),
    )(pre_rows, bias_col)


def _chunk_terms(grow, gcol, h, m0):
    i_row, b_row = grow[h:h + 1, :], grow[MLH + h:MLH + h + 1, :]
    i_col, b_col = gcol[:, h:h + 1], gcol[:, MLH + h:MLH + h + 1]
    b_last = b_row[:, CHUNK - 1:CHUNK]
    tt = lax.broadcasted_iota(jnp.int32, (CHUNK, CHUNK), 0)
    ss = lax.broadcasted_iota(jnp.int32, (CHUNK, CHUNK), 1)
    log_d = jnp.where(tt >= ss, b_col - b_row + i_row, -jnp.inf)
    m_t = jnp.maximum(b_col + m0, jnp.max(log_d, axis=1, keepdims=True))
    dm = jnp.exp(log_d - m_t)
    wi = jnp.exp(b_col + m0 - m_t)
    m1 = jnp.maximum(b_last + m0, jnp.max(b_last - b_row + i_row, axis=1, keepdims=True))
    ws = jnp.exp(b_last - b_col + i_col - m1)
    dec = jnp.exp(b_last + m0 - m1)
    return dm, wi, m_t, ws, dec, m1


def _mlstm_fwd(qk, proj, grow, gcol, gain):
    t = qk.shape[0]
    nc = t // CHUNK

    def body(qk_ref, v_ref, o_ref, grow_ref, gcol_ref, g_ref, h_ref, y_ref, cs_ref, st_ref, c_scr, st_scr):
        @pl.when(pl.program_id(0) == 0)
        def _():
            c_scr[...] = jnp.zeros_like(c_scr)
            st_scr[...] = jnp.zeros_like(st_scr)

        grow_v, gcol_v = grow_ref[...], gcol_ref[...]
        heads = range(MLH)
        q = [qk_ref[:, h * DQK:(h + 1) * DQK] for h in heads]
        k = [qk_ref[:, MLH * DQK + h * DQK:MLH * DQK + (h + 1) * DQK] for h in heads]
        v = [v_ref[:, h * DV:(h + 1) * DV] for h in heads]
        c0 = [c_scr[h] for h in heads]
        n0 = [st_scr[h, 0:1, :] for h in heads]
        for h in heads:
            cs_ref[0, h] = c0[h]
            st_ref[0, h] = st_scr[h]
        terms = [_chunk_terms(grow_v, gcol_v, h, st_scr[h, 1:2, 0:1]) for h in heads]
        a = [_dot_nt(q[h], k[h]) for h in heads]
        qc = [_dot_nt(q[h], c0[h].astype(BF16)) for h in heads]
        s = [a[h] * terms[h][0] for h in heads]
        sv = [_dot_nn(s[h].astype(BF16), v[h]) for h in heads]
        upd = [_dot_tn((terms[h][3] * v[h]).astype(BF16), k[h]) for h in heads]
        den = [terms[h][1] * jnp.sum(q[h].astype(F32) * n0[h], axis=1, keepdims=True)
               + jnp.sum(s[h], axis=1, keepdims=True) for h in heads]
        hv = [(terms[h][1] * qc[h] + sv[h]) / jnp.maximum(jnp.abs(den[h]), jnp.exp(-terms[h][2])) for h in heads]
        for h in heads:
            sl = slice(h * DV, (h + 1) * DV)
            h_ref[:, sl] = hv[h]
            xn, _ = _rms(hv[h])
            y_ref[:, sl] = (_sigmoid(o_ref[:, sl].astype(F32)) * xn * g_ref[:, sl]).astype(BF16)
        for h in heads:
            dec, m1 = terms[h][4], terms[h][5]
            c_scr[h] = dec * c0[h] + upd[h]
            st_scr[h, 0:1, :] = dec * n0[h] + jnp.sum(terms[h][3] * k[h].astype(F32), axis=0, keepdims=True)
            st_scr[h, 1:2, :] = jnp.broadcast_to(m1, (1, DQK))

    return pl.pallas_call(
        body, name="mlstm_fwd", grid=(nc,),
        in_specs=[pl.BlockSpec((CHUNK, D), lambda c: (c, 0)), pl.BlockSpec((CHUNK, D), lambda c: (c, C_V // D)),
                  pl.BlockSpec((CHUNK, D), lambda c: (c, C_O // D)),
                  pl.BlockSpec((8, CHUNK), lambda c: (0, c)), pl.BlockSpec((CHUNK, 8), lambda c: (c, 0)),
                  pl.BlockSpec((1, D), lambda c: (0, 0))],
        out_specs=[pl.BlockSpec((CHUNK, D), lambda c: (c, 0)), pl.BlockSpec((CHUNK, D), lambda c: (c, 0)),
                   pl.BlockSpec((1, MLH, DV, DQK), lambda c: (c, 0, 0, 0)),
                   pl.BlockSpec((1, MLH, 8, DQK), lambda c: (c, 0, 0, 0))],
        out_shape=[jax.ShapeDtypeStruct((t, D), F32), jax.ShapeDtypeStruct((t, D), BF16),
                   jax.ShapeDtypeStruct((nc, MLH, DV, DQK), F32), jax.ShapeDtypeStruct((nc, MLH, 8, DQK), F32)],
        scratch_shapes=[pltpu.VMEM((MLH, DV, DQK), F32), pltpu.VMEM((MLH, 8, DQK), F32)],
        compiler_params=_params(dimension_semantics=("arbitrary",)),
    )(qk, proj, proj, grow, gcol, gain)


def _mlstm_bwd(qk, proj, grow, gcol, sneg_col, cs, st, hraw, dh, dproj):
    t = qk.shape[0]
    nc = t // CHUNK

    def rev(c):
        return nc - 1 - c

    def nxt(c):
        return jnp.minimum(nc - c, nc - 1)

    def body(qk_ref, v_ref, grow_ref, gcol_ref, sneg_ref, cs_ref, st_ref, cs1_ref, st1_ref, h_ref, dh_ref, _,
             dqk_ref, dv_ref, dif_ref, dbif_ref, dc_scr, dn_scr):
        @pl.when(pl.program_id(0) == 0)
        def _():
            dc_scr[...] = jnp.zeros_like(dc_scr)
            dn_scr[...] = jnp.zeros_like(dn_scr)
            dbif_ref[...] = jnp.zeros_like(dbif_ref)

        grow_v, gcol_v, sneg = grow_ref[...], gcol_ref[...], sneg_ref[...]
        tt = lax.broadcasted_iota(jnp.int32, (CHUNK, CHUNK), 0)
        ss = lax.broadcasted_iota(jnp.int32, (CHUNK, CHUNK), 1)
        lane8 = lax.broadcasted_iota(jnp.int32, (CHUNK, 8), 1)
        heads = range(MLH)
        q = [qk_ref[:, h * DQK:(h + 1) * DQK] for h in heads]
        k = [qk_ref[:, MLH * DQK + h * DQK:MLH * DQK + (h + 1) * DQK] for h in heads]
        qf, kf = [a.astype(F32) for a in q], [a.astype(F32) for a in k]
        vb = [v_ref[:, h * DV:(h + 1) * DV].astype(BF16) for h in heads]
        c0 = [cs_ref[0, h] for h in heads]
        n0 = [st_ref[0, h, 0:1, :] for h in heads]
        dc1 = [dc_scr[h] for h in heads]
        dn1 = [dn_scr[h, 0:1, :] for h in heads]
        terms = [_chunk_terms(grow_v, gcol_v, h, st_ref[0, h, 1:2, 0:1]) for h in heads]
        dm, wi, ws = [t[0] for t in terms], [t[1] for t in terms], [t[3] for t in terms]
        s = [_dot_nt(q[h], k[h]) * dm[h] for h in heads]
        den = [wi[h] * jnp.sum(qf[h] * n0[h], axis=1, keepdims=True) + jnp.sum(s[h], axis=1, keepdims=True)
               for h in heads]
        floor = [jnp.exp(-terms[h][2]) for h in heads]
        g = [jnp.maximum(jnp.abs(den[h]), floor[h]) for h in heads]
        dh_v = [dh_ref[:, h * DV:(h + 1) * DV] for h in heads]
        dnum = [dh_v[h] / g[h] for h in heads]
        dden = [-jnp.sum(dh_v[h] * h_ref[:, h * DV:(h + 1) * DV], axis=1, keepdims=True) / g[h] for h in heads]
        dden = [jnp.where(jnp.abs(den[h]) > floor[h], dden[h] * jnp.sign(den[h]), 0.0) for h in heads]
        dnum_b = [a.astype(BF16) for a in dnum]
        dc1_b = [a.astype(BF16) for a in dc1]
        da = [((_dot_nt(dnum_b[h], vb[h]) + dden[h]) * dm[h]).astype(BF16) for h in heads]
        dq_inter = [_dot_nn(dnum_b[h], c0[h].astype(BF16)) for h in heads]
        dk_inter = [_dot_nn(vb[h], dc1_b[h]) for h in heads]
        dv_inter = [_dot_nt(k[h], dc1_b[h]) for h in heads]
        dc_new = [_dot_tn((wi[h] * dnum[h]).astype(BF16), q[h]) for h in heads]
        dq = [_dot_nn(da[h], k[h]) + wi[h] * (dq_inter[h] + dden[h] * n0[h]) for h in heads]
        dk = [_dot_tn(da[h], q[h]) + ws[h] * (dk_inter[h] + dn1[h]) for h in heads]
        dv = [_dot_tn(s[h].astype(BF16), dnum_b[h]) + ws[h] * dv_inter[h] for h in heads]
        for h in heads:
            dqk_ref[:, h * DQK:(h + 1) * DQK] = dq[h]
            dqk_ref[:, MLH * DQK + h * DQK:MLH * DQK + (h + 1) * DQK] = dk[h]
            dv_ref[:, h * DV:(h + 1) * DV] = dv[h].astype(BF16)
        rk = [jnp.sum(kf[h] * dk[h], axis=1, keepdims=True) for h in heads]
        df = [jnp.sum(qf[h] * dq[h], axis=1, keepdims=True) - rk[h] for h in heads]
        df_row = [jnp.sum(jnp.where(tt == ss, df[h], 0.0), axis=0, keepdims=True) for h in heads]
        suffix = [jnp.sum(jnp.where(ss >= tt, df_row[h], 0.0), axis=1, keepdims=True) for h in heads]
        cross = [jnp.sum(jnp.sum(dc1[h] * cs1_ref[0, h], axis=1, keepdims=True), axis=0, keepdims=True)
                 + jnp.sum(dn1[h] * st1_ref[0, h, 0:1, :], axis=1, keepdims=True) for h in heads]
        dif = jnp.zeros((CHUNK, 8), F32)
        for h in heads:
            dpf = (suffix[h] + cross[h]) * sneg[:, MLH + h:MLH + h + 1]
            dif = dif + jnp.where(lane8 == h, rk[h], 0.0) + jnp.where(lane8 == MLH + h, dpf, 0.0)
            dc_scr[h] = terms[h][4] * dc1[h] + dc_new[h]
            dn_scr[h, 0:1, :] = terms[h][4] * dn1[h] + jnp.sum(wi[h] * dden[h] * qf[h], axis=0, keepdims=True)
        dif_ref[...] = dif
        dbif_ref[...] += jnp.sum(dif, axis=0, keepdims=True)

    return pl.pallas_call(
        body, name="mlstm_bwd", grid=(nc,),
        in_specs=[pl.BlockSpec((CHUNK, D), lambda c: (rev(c), 0)),
                  pl.BlockSpec((CHUNK, D), lambda c: (rev(c), C_V // D)),
                  pl.BlockSpec((8, CHUNK), lambda c: (0, rev(c))),
                  pl.BlockSpec((CHUNK, 8), lambda c: (rev(c), 0)),
                  pl.BlockSpec((CHUNK, 8), lambda c: (rev(c), 0)),
                  pl.BlockSpec((1, MLH, DV, DQK), lambda c: (rev(c), 0, 0, 0)),
                  pl.BlockSpec((1, MLH, 8, DQK), lambda c: (rev(c), 0, 0, 0)),
                  pl.BlockSpec((1, MLH, DV, DQK), lambda c: (nxt(c), 0, 0, 0)),
                  pl.BlockSpec((1, MLH, 8, DQK), lambda c: (nxt(c), 0, 0, 0)),
                  pl.BlockSpec((CHUNK, D), lambda c: (rev(c), 0)),
                  pl.BlockSpec((CHUNK, D), lambda c: (rev(c), 0)), _ANY],
        out_specs=[pl.BlockSpec((CHUNK, D), lambda c: (rev(c), 0)),
                   pl.BlockSpec((CHUNK, D), lambda c: (rev(c), C_V // D)),
                   pl.BlockSpec((CHUNK, 8), lambda c: (rev(c), 0)),
                   pl.BlockSpec((1, 8), lambda c: (0, 0))],
        out_shape=[jax.ShapeDtypeStruct((t, D), F32), jax.ShapeDtypeStruct((t, NP), BF16),
                   jax.ShapeDtypeStruct((t, 8), F32), jax.ShapeDtypeStruct((1, 8), F32)],
        scratch_shapes=[pltpu.VMEM((MLH, DV, DQK), F32), pltpu.VMEM((MLH, 8, DQK), F32)],
        input_output_aliases={11: 1}, compiler_params=_params(dimension_semantics=("arbitrary",)),
    )(qk, proj, grow, gcol, sneg_col, cs, st, cs, st, hraw, dh, dproj)


_ANY = pl.BlockSpec(memory_space=pl.ANY)


_SW_SCALE = HD ** -0.5
_KVB = C_KV // (2 * SWKV * HD)


def _swa_mask(n):
    ki = lax.broadcasted_iota(jnp.int32, (2 * WIN, SWG * WIN), 0)
    qi = lax.broadcasted_iota(jnp.int32, (2 * WIN, SWG * WIN), 1) % WIN
    return (ki > qi) & (ki <= qi + WIN) & ((n > 0) | (ki >= WIN))


def _group_rows(x_ref, hk):
    return jnp.concatenate([x_ref[:, (hk * SWG + g) * HD:(hk * SWG + g + 1) * HD] for g in range(SWG)], axis=0)


def _group_lanes(x_ref, hk):
    return jnp.concatenate([x_ref[hk * SWG + g:hk * SWG + g + 1, :] for g in range(SWG)], axis=1)


def _sink_lanes(sink_ref, hk):
    return jnp.concatenate([jnp.broadcast_to(sink_ref[:, hk * SWG + g:hk * SWG + g + 1], (1, WIN))
                            for g in range(SWG)], axis=1)


def _swa_fwd(proj, sinks):
    t = proj.shape[0]
    nb = t // WIN

    def body(q_ref, kvc_ref, kvp_ref, sink_ref, y_ref, lse_ref):
        valid = _swa_mask(pl.program_id(0))
        kvh = range(SWKV)
        kb = [jnp.concatenate([kvp_ref[:, hk * HD:(hk + 1) * HD], kvc_ref[:, hk * HD:(hk + 1) * HD]],
                              axis=0).astype(BF16) for hk in kvh]
        vb = [jnp.concatenate([kvp_ref[:, (SWKV + hk) * HD:(SWKV + hk + 1) * HD],
                               kvc_ref[:, (SWKV + hk) * HD:(SWKV + hk + 1) * HD]], axis=0).astype(BF16) for hk in kvh]
        sink = [_sink_lanes(sink_ref, hk) for hk in kvh]
        logits = [_dot_nt(kb[hk], _group_rows(q_ref, hk).astype(BF16)) for hk in kvh]
        logits = [jnp.where(valid, logits[hk] * _SW_SCALE, -jnp.inf) for hk in kvh]
        m = [jnp.maximum(jnp.max(logits[hk], axis=0, keepdims=True), sink[hk]) for hk in kvh]
        p = [jnp.exp(logits[hk] - m[hk]) for hk in kvh]
        denom = [jnp.sum(p[hk], axis=0, keepdims=True) + jnp.exp(sink[hk] - m[hk]) for hk in kvh]
        y4 = [_dot_tn((p[hk] / denom[hk]).astype(BF16), vb[hk]).astype(BF16) for hk in kvh]
        for hk in kvh:
            lse4 = m[hk] + jnp.log(denom[hk])
            for g in range(SWG):
                hq = hk * SWG + g
                y_ref[:, hq * HD:(hq + 1) * HD] = y4[hk][g * WIN:(g + 1) * WIN]
                lse_ref[hq:hq + 1, :] = lse4[:, g * WIN:(g + 1) * WIN]

    return pl.pallas_call(
        body, name="swa_fwd", grid=(nb,),
        in_specs=[pl.BlockSpec((WIN, D), lambda n: (n, C_QSW // D)),
                  pl.BlockSpec((WIN, 512), lambda n: (n, _KVB)),
                  pl.BlockSpec((WIN, 512), lambda n: (jnp.maximum(n - 1, 0), _KVB)),
                  pl.BlockSpec((1, SWH), lambda n: (0, 0))],
        out_specs=[pl.BlockSpec((WIN, D), lambda n: (n, 0)), pl.BlockSpec((SWH, WIN), lambda n: (0, n))],
        out_shape=[jax.ShapeDtypeStruct((t, D), BF16), jax.ShapeDtypeStruct((SWH, t), F32)],
        compiler_params=_params(),
    )(proj, proj, proj, sinks)


def _swa_bwd(proj, sinks, lse, dyb, dproj):
    t = proj.shape[0]
    nb = t // WIN

    def body(q_ref, kvc_ref, kvp_ref, sink_ref, lse_ref, dy_ref, _, dq_ref, dself_ref, dprev_ref, ds_ref):
        @pl.when(pl.program_id(0) == 0)
        def _():
            ds_ref[...] = jnp.zeros_like(ds_ref)

        valid = _swa_mask(pl.program_id(0))
        kvh = range(SWKV)
        ks = [slice(hk * HD, (hk + 1) * HD) for hk in kvh]
        vs = [slice(SWKV * HD + hk * HD, SWKV * HD + (hk + 1) * HD) for hk in kvh]
        kb = [jnp.concatenate([kvp_ref[:, ks[hk]], kvc_ref[:, ks[hk]]], axis=0).astype(BF16) for hk in kvh]
        vb = [jnp.concatenate([kvp_ref[:, vs[hk]], kvc_ref[:, vs[hk]]], axis=0).astype(BF16) for hk in kvh]
        qb = [_group_rows(q_ref, hk).astype(BF16) for hk in kvh]
        dyb_ = [_group_rows(dy_ref, hk).astype(BF16) for hk in kvh]
        lse4 = [_group_lanes(lse_ref, hk) for hk in kvh]
        logits = [_dot_nt(kb[hk], qb[hk]) for hk in kvh]
        dpt = [_dot_nt(vb[hk], dyb_[hk]) for hk in kvh]
        p = [jnp.exp(jnp.where(valid, logits[hk] * _SW_SCALE, -jnp.inf) - lse4[hk]) for hk in kvh]
        delta = [jnp.sum(p[hk] * dpt[hk], axis=0, keepdims=True) for hk in kvh]
        dsm = [(p[hk] * (dpt[hk] - delta[hk])).astype(BF16) for hk in kvh]
        dq4 = [(_dot_tn(dsm[hk], kb[hk]) * _SW_SCALE).astype(BF16) for hk in kvh]
        dkb = [_dot_nn(dsm[hk], qb[hk]) * _SW_SCALE for hk in kvh]
        dvb = [_dot_nn(p[hk].astype(BF16), dyb_[hk]) for hk in kvh]
        for hk in kvh:
            dsink4 = jnp.exp(_sink_lanes(sink_ref, hk) - lse4[hk]) * delta[hk]
            for g in range(SWG):
                hq = hk * SWG + g
                dq_ref[:, hq * HD:(hq + 1) * HD] = dq4[hk][g * WIN:(g + 1) * WIN]
                ds_ref[:, hq:hq + 1] += -jnp.sum(dsink4[:, g * WIN:(g + 1) * WIN], axis=1, keepdims=True)
            dprev_ref[:, ks[hk]] = dkb[hk][:WIN]
            dself_ref[:, ks[hk]] = dkb[hk][WIN:]
            dprev_ref[:, vs[hk]] = dvb[hk][:WIN]
            dself_ref[:, vs[hk]] = dvb[hk][WIN:]

    return pl.pallas_call(
        body, name="swa_bwd", grid=(nb,),
        in_specs=[pl.BlockSpec((WIN, D), lambda n: (n, C_QSW // D)),
                  pl.BlockSpec((WIN, 512), lambda n: (n, _KVB)),
                  pl.BlockSpec((WIN, 512), lambda n: (jnp.maximum(n - 1, 0), _KVB)),
                  pl.BlockSpec((1, SWH), lambda n: (0, 0)),
                  pl.BlockSpec((SWH, WIN), lambda n: (0, n)),
                  pl.BlockSpec((WIN, D), lambda n: (n, 0)), _ANY],
        out_specs=[pl.BlockSpec((WIN, D), lambda n: (n, C_QSW // D)), pl.BlockSpec((WIN, 512), lambda n: (n, 0)),
                   pl.BlockSpec((WIN, 512), lambda n: (jnp.maximum(n - 1, 0), 0)),
                   pl.BlockSpec((1, SWH), lambda n: (0, 0))],
        out_shape=[jax.ShapeDtypeStruct((t, NP), BF16), jax.ShapeDtypeStruct((t, 512), F32),
                   jax.ShapeDtypeStruct((t, 512), F32), jax.ShapeDtypeStruct((1, SWH), F32)],
        input_output_aliases={6: 0}, compiler_params=_params(),
    )(proj, proj, proj, sinks, lse, dyb, dproj)


def _kv_combine(dself, dnext, dif, dproj):
    t = dself.shape[0]
    rows = _pick(t, 512)

    def body(a_ref, b_ref, dif_ref, _, o_ref):
        row = pl.program_id(0) * rows + lax.broadcasted_iota(jnp.int32, (rows, 1), 0)
        o_ref[:, 0:512] = (a_ref[...] + jnp.where(row < t - WIN, b_ref[...], 0.0)).astype(BF16)
        lane = lax.broadcasted_iota(jnp.int32, (rows, 128), 1)
        dif_v = dif_ref[...]
        first = jnp.zeros((rows, 128), F32)
        for col in range(8):
            first = first + jnp.where(lane == col, dif_v[:, col:col + 1], 0.0)
        o_ref[:, 512:640] = first.astype(BF16)
        o_ref[:, 640:512 + IFW] = jnp.zeros((rows, IFW - 128), BF16)

    return pl.pallas_call(
        body, name="kv_combine", grid=(t // rows,),
        in_specs=[pl.BlockSpec((rows, 512), lambda n: (n, 0)), pl.BlockSpec((rows, 512), lambda n: (n, 0)),
                  pl.BlockSpec((rows, 8), lambda n: (n, 0)), _ANY],
        out_specs=pl.BlockSpec((rows, 512 + IFW), lambda n: (n, C_KV // (512 + IFW))),
        out_shape=jax.ShapeDtypeStruct((t, NP), BF16), input_output_aliases={3: 0}, compiler_params=_params(),
    )(dself, dnext, dif, dproj)


def _sds(t, n, dtype):
    return jax.ShapeDtypeStruct((t, n), dtype)


def _proj_in(h0, w_in):
    t = h0.shape[0]
    tn = 2 * IFW

    def epilogue(accs, ins, outs, i, j):
        outs[0][...] = accs[0].astype(BF16)

        @pl.when(j == C_IF // tn)
        def _():
            outs[1][...] = accs[0][:, C_IF % tn:C_IF % tn + 128]

    gate_cols = lambda tm, tn: pl.BlockSpec((tm, 128), lambda i, j, kk: (i, 0))
    return _mm_ep([(h0, w_in)], "nn", "mm_in", epilogue, [],
                  [(_sds(t, NP, BF16), _tile()), (_sds(t, 128, F32), gate_cols)], 1024, tn)


def _branch_merge(ya, yb, wa, wb, proj):
    t = ya.shape[0]

    def epilogue(accs, ins, outs, i, j):
        za, zb = accs
        merged = _sigmoid(ins[0][...].astype(F32)) * za + _sigmoid(ins[1][...].astype(F32)) * zb
        outs[0][...] = merged.astype(BF16)
        outs[1][...] = za.astype(BF16)
        outs[2][...] = zb.astype(BF16)

    return _mm_ep([(ya, wa), (yb, wb)], "nn", "mm_branch_merge", epilogue, [(proj, _tile(C_GA)), (proj, _tile(C_GB))],
                  [(_sds(t, D, BF16), _tile())] * 3, 1024, 512)


def _dmerged_bwd(dxb, w_out, proj, za, zb):
    t = dxb.shape[0]

    def epilogue(accs, ins, outs, i, j):
        dm = accs[0]
        sa, sb = _sigmoid(ins[0][...].astype(F32)), _sigmoid(ins[1][...].astype(F32))
        outs[0][...] = (dm * sa).astype(BF16)
        outs[1][...] = (dm * sb).astype(BF16)
        outs[2][:, 0:D] = (dm * ins[2][...].astype(F32) * sa * (1.0 - sa)).astype(BF16)
        outs[2][:, D:2 * D] = (dm * ins[3][...].astype(F32) * sb * (1.0 - sb)).astype(BF16)

    gate_cols = lambda tm, tn: pl.BlockSpec((tm, 2 * D), lambda i, j, kk: (i, C_GA // (2 * D)))
    return _mm_ep([(dxb, w_out)], "nt", "mm_dmerged_bwd", epilogue,
                  [(proj, _tile(C_GA)), (proj, _tile(C_GB)), (za, _tile()), (zb, _tile())],
                  [(_sds(t, D, BF16), _tile()), (_sds(t, D, BF16), _tile()), (_sds(t, NP, BF16), gate_cols)], 1024, D)


def _dya_bwd(dza, wa, hraw, proj, g, dproj):
    t = dza.shape[0]

    def epilogue(accs, ins, outs, i, j):
        h_ref, o_ref, g_ref, _ = ins
        dh_ref, do_ref, dg_ref = outs

        @pl.when(i == 0)
        def _():
            dg_ref[...] = jnp.zeros_like(dg_ref)

        dy = accs[0]
        so = _sigmoid(o_ref[...].astype(F32))
        for h in range(MLH):
            sl = slice(h * DV, (h + 1) * DV)
            xn, rstd = _rms(h_ref[:, sl])
            gs = g_ref[:, sl]
            do_ref[:, sl] = (dy[:, sl] * xn * gs * so[:, sl] * (1.0 - so[:, sl])).astype(BF16)
            dhn = dy[:, sl] * so[:, sl]
            dg_ref[:, sl] += jnp.sum(dhn * xn, axis=0, keepdims=True)
            dh_ref[:, sl] = _rms_bwd(xn, rstd, dhn * gs)

    return _mm_ep([(dza, wa)], "nt", "mm_dya_bwd", epilogue,
                  [(hraw, _tile()), (proj, _tile(C_O)), (g, _row()), (dproj, lambda tm, tn: _ANY)],
                  [(_sds(t, D, F32), _tile()), (_sds(t, NP, BF16), _tile(C_O)), (_sds(1, D, F32), _row())],
                  1024, D, aliases={3: 1})


def _up_act(hn, w_up):
    t = hn.shape[0]

    def epilogue(accs, ins, outs, i, j):
        r = jnp.maximum(accs[0], 0.0)
        outs[0][...] = (r * r).astype(BF16)
        outs[1][...] = accs[0].astype(BF16)

    return _mm_ep([(hn, w_up)], "nn", "mm_up_act", epilogue, [],
                  [(_sds(t, DFF, BF16), _tile()), (_sds(t, DFF, BF16), _tile())], 1024, 1024)


def _da_du(dxb, w_down, u):
    t = dxb.shape[0]

    def epilogue(accs, ins, outs, i, j):
        outs[0][...] = (accs[0] * 2.0 * jnp.maximum(ins[0][...].astype(F32), 0.0)).astype(BF16)

    return _mm_ep([(dxb, w_down)], "nt", "mm_da_du", epilogue, [(u, _tile())], [(_sds(t, DFF, BF16), _tile())],
                  1024, 1024)[0]


def _resid_norm_mm(a, w, x, g, name):
    t = x.shape[0]

    def epilogue(accs, ins, outs, i, j):
        x1 = ins[0][...] + accs[0]
        outs[0][...] = x1
        xn, _ = _rms(x1)
        outs[1][...] = (xn * ins[1][...]).astype(BF16)

    return _mm_ep([(a, w)], "nn", name, epilogue, [(x, _tile()), (g, _row())],
                  [(_sds(t, D, F32), _tile()), (_sds(t, D, BF16), _tile())], 1024, D)


def _norm_bwd_mm(dy, w, x, g, dres, name):
    t = x.shape[0]

    def epilogue(accs, ins, outs, i, j):
        @pl.when(i == 0)
        def _():
            outs[2][...] = jnp.zeros_like(outs[2])

        dh = accs[0]
        xn, rstd = _rms(ins[0][...])
        outs[2][...] += jnp.sum(dh * xn, axis=0, keepdims=True)
        dx = ins[2][...] + _rms_bwd(xn, rstd, dh * ins[1][...])
        outs[0][...] = dx
        outs[1][...] = dx.astype(BF16)

    return _mm_ep([(dy, w)], "nt", name, epilogue, [(x, _tile()), (g, _row()), (dres, _tile())],
                  [(_sds(t, D, F32), _tile()), (_sds(t, D, BF16), _tile()), (_sds(1, D, F32), _row())], 1024, D)


def _ple_final_mm(hn2, w_gate, x2, pp, target, gf):
    t = x2.shape[0]

    def epilogue(accs, ins, outs, i, j):
        loss_ref, dg_ref, dx_ref, dpp_ref, dgp_ref = outs

        @pl.when(i == 0)
        def _():
            loss_ref[...] = jnp.zeros_like(loss_ref)
            dg_ref[...] = jnp.zeros_like(dg_ref)

        gate = _sigmoid(accs[0])
        pp_v = ins[1][...]
        x3 = ins[0][...] + gate * pp_v
        xn, rstd = _rms(x3)
        gf_v = ins[3][...]
        err = xn * gf_v - ins[2][...]
        loss_ref[...] += (0.5 / D) * jnp.sum(jnp.sum(err * err, axis=1, keepdims=True), axis=0, keepdims=True)
        dy = err * (1.0 / D)
        dg_ref[...] += jnp.sum(dy * xn, axis=0, keepdims=True)
        dx3 = _rms_bwd(xn, rstd, dy * gf_v)
        dx_ref[...] = dx3
        dpp_ref[...] = (dx3 * gate).astype(BF16)
        dgp_ref[...] = (dx3 * pp_v * gate * (1.0 - gate)).astype(BF16)

    one = lambda tm, tn: pl.BlockSpec((1, 1), lambda i, j, kk: (0, 0))
    return _mm_ep([(hn2, w_gate)], "nn", "mm_ple_final", epilogue,
                  [(x2, _tile()), (pp, _tile()), (target, _tile()), (gf, _row())],
                  [(_sds(1, 1, F32), one), (_sds(1, D, F32), _row()), (_sds(t, D, F32), _tile()),
                   (_sds(t, D, BF16), _tile()), (_sds(t, D, BF16), _tile())], 512, D)


_WIN_SEGMENTS = ((0, 3072, C_QK), (3072, 8, C_IF), (3080, 1024, C_QSW), (4104, 256, C_KV), (4360, 256, C_KV + 256),
                 (4616, 1024, C_GA), (5640, 1024, C_GB))
_WIN_SHARD = N_IN // 4


def _win_pieces():
    out = []
    for src, width, dst in _WIN_SEGMENTS:
        while width:
            chip, col = divmod(src, _WIN_SHARD)
            n = min(width, _WIN_SHARD - col)
            out.append((chip, col, n, dst))
            src, dst, width = src + n, dst + n, width - n
    return out


def _win_pad(shards):
    rows = shards.shape[1]
    tr = _pick(rows, 256)

    def body(s_ref, o_ref):
        for chip, col, n, dst in _win_pieces():
            o_ref[:, dst:dst + n] = s_ref[chip, :, col:col + n]
        o_ref[:, C_IF + 8:NP] = jnp.zeros((tr, NP - C_IF - 8), shards.dtype)

    return pl.pallas_call(
        body, name="win_pad", grid=(rows // tr,), in_specs=[pl.BlockSpec((4, tr, _WIN_SHARD), lambda i: (0, i, 0))],
        out_specs=pl.BlockSpec((tr, NP), lambda i: (i, 0)), out_shape=jax.ShapeDtypeStruct((rows, NP), shards.dtype),
        compiler_params=_params(),
    )(shards)


def _win_unpad(wp):
    rows = wp.shape[0]
    tr = _pick(rows, 256)

    def body(p_ref, o_ref):
        for chip, col, n, dst in _win_pieces():
            o_ref[chip, :, col:col + n] = p_ref[:, dst:dst + n]

    return pl.pallas_call(
        body, name="win_unpad", grid=(rows // tr,), in_specs=[pl.BlockSpec((tr, NP), lambda i: (i, 0))],
        out_specs=pl.BlockSpec((4, tr, _WIN_SHARD), lambda i: (0, i, 0)),
        out_shape=jax.ShapeDtypeStruct((4, rows, _WIN_SHARD), wp.dtype), compiler_params=_params(),
    )(wp)


def _local_step(x, p, target, w, late_weights=None, early_grads=None, mid_grads=None, last_grad=None):
    t = x.shape[0]
    pb = p.astype(BF16)
    w = dict(w)

    h0 = _norm_fwd(x, w["norm_mix_g"], "norm_mix")
    proj, gates = _proj_in(h0, w["w_in"])
    qk = _conv_silu_fwd(proj, w["conv_qk"])
    grow, sneg_row = _gates_fwd(gates[:, 0:8].T, w["b_if"].reshape(8, 1))
    gcol, sneg_col = grow.T, sneg_row.T
    hraw, ya, cs, st = _mlstm_fwd(qk, proj, grow, gcol, w["mlstm_norm_g"])
    yb, lse = _swa_fwd(proj, w["sinks"])
    if late_weights is not None:
        w.update(late_weights(yb))
    merged, za, zb = _branch_merge(ya, yb, w["w_branch_a"], w["w_branch_b"], proj)
    x1, hn1 = _resid_norm_mm(merged, w["w_out"], x, w["norm_mlp_g"], "mm_out_norm")
    act, u = _up_act(hn1, w["w_up"])
    x2, hn2 = _resid_norm_mm(act, w["w_down"], x1, w["norm_ple_g"], "mm_down_norm")
    pp = _mm(pb, w["w_ple_proj"], "nn", F32, "mm_ple_proj")
    loss, d_final_g, dx3, dpp, dgpre = _ple_final_mm(hn2, w["w_ple_gate"], x2, pp, target, w["final_norm_g"])

    g = {"final_norm_g": d_final_g}
    g["w_ple_proj"] = _mm(pb, dpp, "tn", F32, "mm_d_ple_proj", out_chunks=4)
    g["w_ple_gate"] = _mm(hn2, dgpre, "tn", F32, "mm_d_ple_gate")
    dx2, dx2b, g["norm_ple_g"] = _norm_bwd_mm(dgpre, w["w_ple_gate"], x2, w["norm_ple_g"], dx3, "mm_dhn2_norm")
    g["w_down"] = _mm(act, dx2b, "tn", F32, "mm_d_down")
    du = _da_du(dx2b, w["w_down"], u)
    g["w_up"] = _mm(hn1, du, "tn", F32, "mm_d_up", out_chunks=4)
    dx1, dx1b, g["norm_mlp_g"] = _norm_bwd_mm(du, w["w_up"], x1, w["norm_mlp_g"], dx2, "mm_dhn1_norm")
    g["w_out"] = _mm(merged, dx1b, "tn", F32, "mm_d_out")
    dza, dzb, dproj = _dmerged_bwd(dx1b, w["w_out"], proj, za, zb)
    g["w_branch_a"] = _mm(ya, dza, "tn", F32, "mm_d_branch_a")
    g["w_branch_b"] = _mm(yb, dzb, "tn", F32, "mm_d_branch_b")
    gain = w["mlstm_norm_g"] if early_grads is None else w["mlstm_norm_g"] + early_grads(g)
    dyb = _mm(dzb, w["w_branch_b"], "nt", F32, "mm_dyb")
    dhraw, dproj, g["mlstm_norm_g"] = _dya_bwd(dza, w["w_branch_a"], hraw, proj, gain, dproj)
    if mid_grads is not None:
        sneg_col = sneg_col + mid_grads(dhraw)
    dqk, dproj, dif, g["b_if"] = _mlstm_bwd(qk, proj, grow, gcol, sneg_col, cs, st, hraw, dhraw, dproj)
    dc, g["conv_qk"] = _conv_silu_bwd_a(proj, w["conv_qk"], dqk)
    dproj = _conv_silu_bwd_b(dc, w["conv_qk"], dproj)
    dproj, dkv_self, dkv_prev, g["sinks"] = _swa_bwd(proj, w["sinks"], lse, dyb, dproj)
    dproj = _kv_combine(dkv_self, dkv_prev, dif, dproj)
    g["w_in"] = _mm(h0, dproj, "tn", F32, "mm_d_in")
    gain = w["norm_mix_g"] if last_grad is None else w["norm_mix_g"] + last_grad(g)
    grad_x, _, g["norm_mix_g"] = _norm_bwd_mm(dproj, w["w_in"], x, gain, dx1, "mm_dh0_norm")
    return loss, grad_x, g


_W4 = ("w_branch_a", "w_branch_b", "w_out", "w_ple_gate")
_SHARDED_NAMES = ("w_in", "w_up", "w_down", "w_ple_proj", "conv_qk") + _W4
_SMALL_ROWS = 16
_CONV_ROW = 8


def _group(s):
    return [s["w_in"], jnp.concatenate([s[n] for n in _W4], axis=0), s["w_up"], s["w_down"], s["w_ple_proj"]]


def _ungroup(arrs):
    out = {"w_in": arrs[0], "w_up": arrs[2], "w_down": arrs[3], "w_ple_proj": arrs[4]}
    rows = arrs[1].shape[0] // len(_W4)
    for i, n in enumerate(_W4):
        out[n] = arrs[1][i * rows:(i + 1) * rows]
    return out


def _rows_tile(rows):
    return 256 if rows % 256 == 0 else rows


_SMALL = ("norm_mix_g", "mlstm_norm_g", "norm_mlp_g", "norm_ple_g", "final_norm_g")


def _pack_small(vals, extra=None, conv=None):
    rows = [vals[n].reshape(1, D) for n in _SMALL]
    tail = [vals["b_if"].reshape(1, 8), vals["sinks"].reshape(1, SWH)]
    used = 8 + SWH
    if extra is not None:
        tail.append(extra.reshape(1, 1))
        used += 1
    tail.append(jnp.zeros((1, D - used), F32))
    rows.append(jnp.concatenate(tail, axis=1))
    rows.append(jnp.zeros((_CONV_ROW - len(rows), D), F32))
    rows.append(jnp.zeros((CONV, D), F32) if conv is None else conv)
    rows.append(jnp.zeros((_SMALL_ROWS - _CONV_ROW - CONV, D), F32))
    return jnp.concatenate(rows, axis=0)


def _unpack_small(slab, shapes):
    out = {n: slab[i].reshape(shapes[n]) for i, n in enumerate(_SMALL)}
    out["b_if"] = slab[5, 0:8].reshape(shapes["b_if"])
    out["sinks"] = slab[5, 8:8 + SWH].reshape(shapes["sinks"])
    return out


_MESH = pl.DeviceIdType.MESH
_HBM = pl.BlockSpec(memory_space=pltpu.HBM)
_VMEM = pl.BlockSpec(memory_space=pltpu.VMEM)


def _place():
    x, y, c = lax.axis_index("x"), lax.axis_index("y"), lax.axis_index("c")
    return x, y, c, 2 * x + y


def _chip_peer(x, y, r):
    return (x ^ (r >> 1), y ^ (r & 1))


def _half(ref, which):
    h = ref.shape[-2] // 2
    return pl.ds(which * h, h)


def _allgather_weights(shards, conv):
    n = len(shards)

    def body(*refs):
        ins, conv_ref = refs[:n], refs[n]
        outs, conv_out = refs[n + 1:2 * n + 1], refs[2 * n + 1]
        send_a, recv_a, send_b, recv_b, send_c, recv_c, local_sems = refs[2 * n + 2:]
        x, y, c, j = _place()
        sibling = (x, y, 1 - c)
        local = [pltpu.make_async_copy(ins[k], outs[k].at[j], local_sems.at[k]) for k in range(n)]
        local.append(pltpu.make_async_copy(conv_ref, conv_out.at[j], local_sems.at[n]))
        for cp in local:
            cp.start()

        def copy_a(k, r, chip):
            rows = _half(ins[k], c)
            return pltpu.make_async_remote_copy(
                src_ref=ins[k].at[rows], dst_ref=outs[k].at[chip, rows], send_sem=send_a.at[3 * k + r - 1],
                recv_sem=recv_a.at[3 * k + r - 1], device_id=(*_chip_peer(x, y, r), c), device_id_type=_MESH)

        def copy_b(k, r, chip, which):
            rows = _half(ins[k], which)
            return pltpu.make_async_remote_copy(
                src_ref=outs[k].at[chip, rows], dst_ref=outs[k].at[chip, rows], send_sem=send_b.at[3 * k + r - 1],
                recv_sem=recv_b.at[3 * k + r - 1], device_id=sibling, device_id_type=_MESH)

        def copy_c(r, chip):
            return pltpu.make_async_remote_copy(
                src_ref=conv_ref, dst_ref=conv_out.at[chip], send_sem=send_c.at[r - 1],
                recv_sem=recv_c.at[r - 1], device_id=(*_chip_peer(x, y, r), c), device_id_type=_MESH)

        for k in range(n):
            for r in (1, 2, 3):
                copy_a(k, r, j).start()
        for r in (1, 2, 3):
            copy_c(r, j).start()
        for k in range(n):
            for r in (1, 2, 3):
                copy_a(k, r, j ^ r).wait_recv()
                copy_b(k, r, j ^ r, c).start()
        for k in range(n):
            for r in (1, 2, 3):
                copy_b(k, r, j ^ r, 1 - c).wait_recv()
        for r in (1, 2, 3):
            copy_c(r, j ^ r).wait_recv()
        for k in range(n):
            for r in (1, 2, 3):
                copy_a(k, r, j).wait_send()
                copy_b(k, r, j ^ r, c).wait_send()
        for r in (1, 2, 3):
            copy_c(r, j).wait_send()
        for cp in local:
            cp.wait()

    return pl.pallas_call(
        body, name="allgather_weights",
        out_shape=[jax.ShapeDtypeStruct((4,) + s.shape, s.dtype) for s in shards]
        + [jax.ShapeDtypeStruct((4,) + conv.shape, F32)],
        in_specs=[_HBM] * (n + 1), out_specs=[_HBM] * (n + 1),
        scratch_shapes=[pltpu.SemaphoreType.DMA((3 * n,))] * 4 + [pltpu.SemaphoreType.DMA((3,))] * 2
        + [pltpu.SemaphoreType.DMA((n + 1,))],
    )(*shards, conv)


_SEM = pl.BlockSpec(memory_space=pltpu.SEMAPHORE)
_DATAFLOW = pltpu.SideEffectType.DATAFLOW_SIDE_EFFECTING


def _late_peer_copy(src_ref, land_ref, send_sems, recv_sems, x, y, c, j, r, chip):
    return pltpu.make_async_remote_copy(
        src_ref=src_ref, dst_ref=land_ref.at[chip], send_sem=send_sems.at[r - 1], recv_sem=recv_sems.at[r - 1],
        device_id=(*_chip_peer(x, y, r), c), device_id_type=_MESH)


def _late_gather_start(rest):
    def body(rest_ref, land_ref, send_sems, recv_sems, rest_thru, land_thru, token):
        x, y, c, j = _place()
        for r in (1, 2, 3):
            _late_peer_copy(rest_ref, land_ref, send_sems, recv_sems, x, y, c, j, r, j).start()
        token[...] = jnp.zeros_like(token)

    j = 2 * lax.axis_index("x") + lax.axis_index("y")
    land = lax.dynamic_update_slice(lax.empty((4,) + rest.shape, rest.dtype), rest[None], (j, 0, 0))
    return pl.pallas_call(
        body, name="late_gather_start",
        out_shape=(pltpu.SemaphoreType.DMA((3,)), pltpu.SemaphoreType.DMA((3,)), pltpu.HBM(rest.shape, rest.dtype),
                   pltpu.HBM(land.shape, land.dtype), jax.ShapeDtypeStruct((8, 128), F32)),
        in_specs=(_HBM, _HBM), out_specs=(_SEM, _SEM, _HBM, _HBM, _VMEM), input_output_aliases={0: 2, 1: 3},
        compiler_params=pltpu.CompilerParams(has_side_effects=_DATAFLOW),
    )(pltpu.with_memory_space_constraint(rest, pltpu.HBM), pltpu.with_memory_space_constraint(land, pltpu.HBM))


def _late_gather_wait(send_sems, recv_sems, rest_thru, land_thru, after):
    def body(rest_ref, land_ref, send_sems, recv_sems, after_ref, rest_dead, got_ref):
        x, y, c, j = _place()
        for r in (1, 2, 3):
            cp = _late_peer_copy(rest_ref, land_ref, send_sems, recv_sems, x, y, c, j, r, j ^ r)
            cp.wait_send()
            cp.wait_recv()

    return pl.pallas_call(
        body, name="late_gather_wait",
        out_shape=(pltpu.HBM(rest_thru.shape, rest_thru.dtype), pltpu.HBM(land_thru.shape, land_thru.dtype)),
        in_specs=(_HBM, _HBM, _SEM, _SEM, _ANY), out_specs=(_HBM, _HBM), input_output_aliases={0: 0, 1: 1},
        compiler_params=pltpu.CompilerParams(has_side_effects=_DATAFLOW),
    )(rest_thru, land_thru, send_sems, recv_sems, after)[1]


def _pair_exchange(gs, name):
    n = len(gs)

    def body(*refs):
        ins, outs, send_sems, recv_sems = refs[:n], refs[n:2 * n], refs[2 * n], refs[2 * n + 1]
        x, y, c, _ = _place()
        cps = [pltpu.make_async_remote_copy(
            src_ref=ins[k].at[:, _half(ins[k], 1 - c)], dst_ref=outs[k], send_sem=send_sems.at[k],
            recv_sem=recv_sems.at[k], device_id=(x, y, 1 - c), device_id_type=_MESH) for k in range(n)]
        for cp in cps:
            cp.start()
        for cp in cps:
            cp.wait()

    return pl.pallas_call(
        body, name=name,
        out_shape=[jax.ShapeDtypeStruct((4, g.shape[1] // 2, g.shape[2]), F32) for g in gs],
        in_specs=[_HBM] * n, out_specs=[_HBM] * n, scratch_shapes=[pltpu.SemaphoreType.DMA((n,))] * 2,
    )(*gs)


def _pair_sum(g, theirs, c, name):
    _, h, cols = theirs.shape
    tr = _rows_tile(h)
    nb = h // tr

    def body(c_ref, a_ref, b_ref, o_ref, ob_ref):
        s = a_ref[...] + b_ref[...]
        o_ref[...] = s
        ob_ref[...] = s.astype(BF16)

    blk = pl.BlockSpec((1, tr, cols), lambda k, i, c_ref: (k, i, 0))
    return pl.pallas_call(
        body, name=name,
        grid_spec=pltpu.PrefetchScalarGridSpec(
            num_scalar_prefetch=1, grid=(4, nb),
            in_specs=[pl.BlockSpec((1, tr, cols), lambda k, i, c_ref: (k, c_ref[0] * nb + i, 0)), blk],
            out_specs=[blk, blk]),
        out_shape=[jax.ShapeDtypeStruct(theirs.shape, F32), jax.ShapeDtypeStruct(theirs.shape, BF16)],
        compiler_params=_params(),
    )(c.reshape(1).astype(jnp.int32), g, theirs)


def _chip_copies(srcs, lands, send_sems, recv_sems):
    x, y, c, j = _place()
    return [pltpu.make_async_remote_copy(
        src_ref=srcs[k].at[j ^ r], dst_ref=lands[k].at[r - 1], send_sem=send_sems.at[3 * k + r - 1],
        recv_sem=recv_sems.at[3 * k + r - 1], device_id=(*_chip_peer(x, y, r), c), device_id_type=_MESH)
        for k in range(len(srcs)) for r in (1, 2, 3)]


def _pair_copies(srcs, lands, send_sems, recv_sems):
    x, y, c, _ = _place()
    return [pltpu.make_async_remote_copy(
        src_ref=srcs[k].at[:, _half(srcs[k], 1 - c)], dst_ref=lands[k], send_sem=send_sems.at[k],
        recv_sem=recv_sems.at[k], device_id=(x, y, 1 - c), device_id_type=_MESH) for k in range(len(srcs))]


def _split_start(name, srcs, lands, copies, n_sems):
    n = len(srcs)

    def body(*refs):
        for cp in copies(refs[:n], refs[n:2 * n], refs[2 * n], refs[2 * n + 1]):
            cp.start()
        refs[-1][...] = jnp.zeros_like(refs[-1])

    arrays = list(srcs) + list(lands)
    out = pl.pallas_call(
        body, name=name,
        out_shape=(pltpu.SemaphoreType.DMA((n_sems,)), pltpu.SemaphoreType.DMA((n_sems,)),
                   *[pltpu.HBM(a.shape, a.dtype) for a in arrays], jax.ShapeDtypeStruct((8, 128), F32)),
        in_specs=[_HBM] * (2 * n), out_specs=(_SEM, _SEM, *([_HBM] * (2 * n)), _VMEM),
        input_output_aliases={k: 2 + k for k in range(2 * n)},
        compiler_params=pltpu.CompilerParams(has_side_effects=_DATAFLOW),
    )(*[pltpu.with_memory_space_constraint(a, pltpu.HBM) for a in arrays])
    return out[0], out[1], list(out[2:2 + n]), list(out[2 + n:2 + 2 * n]), out[-1]


def _split_wait(name, send_sems, recv_sems, srcs_thru, lands_thru, after, copies):
    n = len(srcs_thru)

    def body(*refs):
        for cp in copies(refs[:n], refs[n:2 * n], refs[2 * n], refs[2 * n + 1]):
            cp.wait_send()
            cp.wait_recv()

    arrays = list(srcs_thru) + list(lands_thru)
    out = pl.pallas_call(
        body, name=name, out_shape=tuple(pltpu.HBM(a.shape, a.dtype) for a in arrays),
        in_specs=[_HBM] * (2 * n) + [_SEM, _SEM, _ANY], out_specs=tuple([_HBM] * (2 * n)),
        input_output_aliases={k: k for k in range(2 * n)},
        compiler_params=pltpu.CompilerParams(has_side_effects=_DATAFLOW),
    )(*arrays, send_sems, recv_sems, after)
    return list(out[:n]), list(out[n:])


def _chip_exchange_start(ss, tag):
    lands = [lax.empty((3,) + s.shape[1:], s.dtype) for s in ss]
    return _split_start("chip_exchange_start_" + tag, ss, lands, _chip_copies, 3 * len(ss))


def _chip_exchange_wait(send_sems, recv_sems, ss_thru, lands_thru, after, tag):
    return _split_wait("chip_exchange_wait_" + tag, send_sems, recv_sems, ss_thru, lands_thru, after, _chip_copies)[1]


def _pair_exchange_start(gs, tag):
    lands = [lax.empty((4, g.shape[1] // 2, g.shape[2]), g.dtype) for g in gs]
    return _split_start("pair_exchange_start_" + tag, gs, lands, _pair_copies, len(gs))


def _pair_exchange_wait(send_sems, recv_sems, gs_thru, lands_thru, after, tag):
    return _split_wait("pair_exchange_wait_" + tag, send_sems, recv_sems, gs_thru, lands_thru, after, _pair_copies)


def _reduce4(own, others, j, c, name):
    _, h, cols = own.shape
    tr = _rows_tile(h)
    nb = h // tr

    def body(idx_ref, s_ref, a0, a1, a2, o_ref):
        o_ref[...] = ((s_ref[0] + a0[0].astype(F32)) + a1[0].astype(F32)) + a2[0].astype(F32)

    def other(r):
        return pl.BlockSpec((1, tr, cols), lambda i, idx_ref: (r, i, 0))

    return pl.pallas_call(
        body, name=name,
        grid_spec=pltpu.PrefetchScalarGridSpec(
            num_scalar_prefetch=1, grid=(nb,),
            in_specs=[pl.BlockSpec((1, tr, cols), lambda i, idx_ref: (idx_ref[0], i, 0)), other(0), other(1), other(2)],
            out_specs=pl.BlockSpec((tr, cols), lambda i, idx_ref: (idx_ref[1] * nb + i, 0))),
        out_shape=jax.ShapeDtypeStruct((2 * h, cols), F32), compiler_params=_params(),
    )(jnp.stack([j, c]).astype(jnp.int32), own, others, others, others)


def _sibling_share(fulls):
    n = len(fulls)

    def body(*refs):
        outs, send_sems, recv_sems = refs[n:2 * n], refs[2 * n], refs[2 * n + 1]
        x, y, c, _ = _place()
        cps = [pltpu.make_async_remote_copy(
            src_ref=outs[k].at[_half(outs[k], c)], dst_ref=outs[k].at[_half(outs[k], c)], send_sem=send_sems.at[k],
            recv_sem=recv_sems.at[k], device_id=(x, y, 1 - c), device_id_type=_MESH) for k in range(n)]
        for cp in cps:
            cp.start()
        for cp in cps:
            cp.wait()

    return pl.pallas_call(
        body, name="sibling_share", out_shape=[jax.ShapeDtypeStruct(f.shape, F32) for f in fulls],
        in_specs=[_HBM] * n, out_specs=[_HBM] * n, input_output_aliases={k: k for k in range(n)},
        scratch_shapes=[pltpu.SemaphoreType.DMA((n,))] * 2,
    )(*fulls)


def _adamw(w, g, m, v):
    m1 = ADAM_B1 * m + (1.0 - ADAM_B1) * g
    v1 = ADAM_B2 * v + (1.0 - ADAM_B2) * (g * g)
    m_hat = m1 / (1.0 - ADAM_B1 ** ADAM_STEP)
    v_hat = v1 / (1.0 - ADAM_B2 ** ADAM_STEP)
    delta = -ADAM_LR * (m_hat / (jnp.sqrt(v_hat) + ADAM_EPS) + ADAM_WD * w)
    return delta, m1, v1


def _adamw_call(w, g, m, v, name):
    rows, cols = w.shape

    def body(w_ref, g_ref, m_ref, v_ref, d_out, m_out, v_out):
        delta, m1, v1 = _adamw(w_ref[...], g_ref[...], m_ref[...], v_ref[...])
        d_out[...] = delta
        m_out[...] = m1
        v_out[...] = v1

    if rows % 8 == 0:
        tr = _rows_tile(rows)
        blk, grid = pl.BlockSpec((tr, cols), lambda i: (i, 0)), (rows // tr,)
    else:
        blk, grid = pl.BlockSpec((rows, 128), lambda i: (0, i)), (cols // 128,)
    return pl.pallas_call(
        body, name=name, grid=grid, in_specs=[blk] * 4, out_specs=[blk] * 3,
        out_shape=[jax.ShapeDtypeStruct((rows, cols), F32)] * 3, compiler_params=_params(),
    )(w, g, m, v)


def _small_allreduce(vals):
    def body(v_ref, out_ref, buf, send_sems, recv_sems):
        x, y, c, j = _place()
        me = 2 * j + c
        buf[0] = v_ref[...]

        def copy(r):
            return pltpu.make_async_remote_copy(
                src_ref=v_ref, dst_ref=buf.at[r], send_sem=send_sems.at[r - 1], recv_sem=recv_sems.at[r - 1],
                device_id=(x ^ (r >> 2), y ^ ((r >> 1) & 1), c ^ (r & 1)), device_id_type=_MESH)

        for r in range(1, 8):
            copy(r).start()
        for r in range(1, 8):
            copy(r).wait()
        acc = buf[me ^ 0]
        for d in range(1, 8):
            acc = acc + buf[me ^ d]
        out_ref[...] = acc

    return pl.pallas_call(
        body, name="small_allreduce", out_shape=jax.ShapeDtypeStruct((_SMALL_ROWS, D), F32),
        in_specs=[_VMEM], out_specs=_VMEM,
        scratch_shapes=[pltpu.VMEM((8, _SMALL_ROWS, D), F32), pltpu.SemaphoreType.DMA((7,)),
                        pltpu.SemaphoreType.DMA((7,))],
    )(vals)


_NAMES = ("norm_mix_g", "w_in", "conv_qk", "b_if", "mlstm_norm_g", "sinks", "w_branch_a", "w_branch_b", "w_out",
          "norm_mlp_g", "w_up", "w_down", "norm_ple_g", "w_ple_gate", "w_ple_proj", "final_norm_g")
_GROUP_NAMES = ("w_in", "w4", "w_up", "w_down", "w_ple_proj")


def _step(x, p, target, w, m, v):
    c = lax.axis_index("c")
    j = 2 * lax.axis_index("x") + lax.axis_index("y")

    def shards(d):
        return {n: d[n][0] for n in _SHARDED_NAMES}

    ws = shards(w)
    w_in_all, conv_all = _allgather_weights([ws["w_in"].astype(BF16)], ws["conv_qk"])
    rows_pp = PLE * (D // 4) // D
    rest = jnp.concatenate([ws[n] for n in _W4] + [ws["w_up"], ws["w_down"], ws["w_ple_proj"].reshape(rows_pp, D)],
                           axis=0)
    rest = (rest + 0.0 * conv_all[0, 0, 0]).astype(BF16)
    send_sems, recv_sems, rest_thru, land_thru, token = _late_gather_start(rest)
    full = {n: w[n] for n in ("mlstm_norm_g", "norm_mlp_g", "norm_ple_g", "b_if", "sinks")}
    full["norm_mix_g"] = w["norm_mix_g"] + token[0, 0]
    full["final_norm_g"] = w["final_norm_g"].reshape(1, D)
    full["w_in"] = _win_pad(w_in_all)
    full["conv_qk"] = jnp.swapaxes(conv_all, 0, 1).reshape(CONV, D)

    def late_weights(after):
        land = _late_gather_wait(send_sems, recv_sems, rest_thru, land_thru, after)
        out = {n: land[:, i * (D // 4):(i + 1) * (D // 4)].reshape(D, D) for i, n in enumerate(_W4)}
        out["w_up"] = land[:, D:2 * D]
        out["w_down"] = land[:, 2 * D:3 * D].reshape(DFF, D)
        out["w_ple_proj"] = land[:, 3 * D:3 * D + rows_pp].reshape(4, PLE, D // 4)
        return out

    def pair_sums(by_dest, names, tag):
        theirs = _pair_exchange(by_dest, "pair_exchange_" + tag)
        return [_pair_sum(a, b, c, "pair_sum_" + n) for a, b, n in zip(by_dest, theirs, names)]

    early, last = {}, {}

    def early_grads(g):
        by_dest = [jnp.stack([g[n].reshape(4, D // 4, D) for n in _W4], axis=1).reshape(4, D, D),
                   g["w_up"], g["w_down"].reshape(4, DFF // 4, D), g["w_ple_proj"]]
        *early["pair"], token = _pair_exchange_start(by_dest, "early")
        return token[0, 0]

    def mid_grads(after):
        by_dest, theirs = _pair_exchange_wait(*early["pair"], after, "early")
        early["sums"] = [_pair_sum(a, b, c, "pair_sum_" + n) for a, b, n in zip(by_dest, theirs, _GROUP_NAMES[1:])]
        *early["flight"], token = _chip_exchange_start([s[1] for s in early["sums"]], "early")
        return token[0, 0]

    def last_grad(g):
        last["sums"] = pair_sums([_win_unpad(g["w_in"])], _GROUP_NAMES[:1], "w_in")
        *last["flight"], token = _chip_exchange_start([s[1] for s in last["sums"]], "w_in")
        return token[0, 0]

    loss, grad_x, g = _local_step(x[0], p[0, 0], target[0], full, late_weights, early_grads, mid_grads, last_grad)

    others = _chip_exchange_wait(*last["flight"], grad_x, "w_in")
    others += _chip_exchange_wait(*early["flight"], others[0], "early")
    sums = last["sums"] + early["sums"]
    halves = [_reduce4(s[0], b, j, c, "reduce4_" + n) for s, b, n in zip(sums, others, _GROUP_NAMES)]
    grads = _sibling_share(halves)

    small_g = _small_allreduce(_pack_small(g, extra=loss, conv=g["conv_qk"]))
    conv_g = lax.dynamic_slice(small_g[_CONV_ROW:_CONV_ROW + CONV], (0, j * (D // 4)), (CONV, D // 4))

    ms, vs = shards(m), shards(v)
    upd = [_adamw_call(wa, ga, ma, va, "adamw_" + n)
           for wa, ga, ma, va, n in list(zip(_group(ws), grads, _group(ms), _group(vs), _GROUP_NAMES))[1:]]
    upd_in = _adamw_call(*[jnp.swapaxes(a, 0, 1) for a in (ws["w_in"], grads[0], ms["w_in"], vs["w_in"])], "adamw_w_in")
    upd = [[jnp.swapaxes(a, 0, 1) for a in upd_in]] + upd
    conv_upd = _adamw_call(ws["conv_qk"], conv_g, ms["conv_qk"], vs["conv_qk"], "adamw_conv")
    small_upd = _adamw_call(_pack_small(w), small_g, _pack_small(m), _pack_small(v), "adamw_small")

    shapes = {n: w[n].shape for n in _NAMES}
    res = []
    for k in range(4):
        big = _ungroup(list(grads) if k == 0 else [u[k - 1] for u in upd])
        big["conv_qk"] = conv_g if k == 0 else conv_upd[k - 1]
        leaves = _unpack_small(small_g if k == 0 else small_upd[k - 1], shapes)
        leaves.update({n: a.reshape(shapes[n]) for n, a in big.items()})
        res.append(leaves)

    out = [small_g[5, 8 + SWH], grad_x[None]]
    for k in range(4):
        out += [res[k][n] for n in _NAMES]
    return tuple(out)


def kernel(x, p, norm_mix_g, w_in, conv_qk, b_if, mlstm_norm_g, sinks, w_branch_a, w_branch_b, w_out, norm_mlp_g, w_up, w_down, norm_ple_g, w_ple_gate, w_ple_proj, final_norm_g, loss_target, m_norm_mix_g, m_w_in, m_conv_qk, m_b_if, m_mlstm_norm_g, m_sinks, m_w_branch_a, m_w_branch_b, m_w_out, m_norm_mlp_g, m_w_up, m_w_down, m_norm_ple_g, m_w_ple_gate, m_w_ple_proj, m_final_norm_g, v_norm_mix_g, v_w_in, v_conv_qk, v_b_if, v_mlstm_norm_g, v_sinks, v_w_branch_a, v_w_branch_b, v_w_out, v_norm_mlp_g, v_w_up, v_w_down, v_norm_ple_g, v_w_ple_gate, v_w_ple_proj, v_final_norm_g):
    w = dict(zip(_NAMES, (norm_mix_g, w_in, conv_qk, b_if, mlstm_norm_g, sinks, w_branch_a, w_branch_b, w_out,
                          norm_mlp_g, w_up, w_down, norm_ple_g, w_ple_gate, w_ple_proj, final_norm_g)))
    m = dict(zip(_NAMES, (m_norm_mix_g, m_w_in, m_conv_qk, m_b_if, m_mlstm_norm_g, m_sinks, m_w_branch_a,
                          m_w_branch_b, m_w_out, m_norm_mlp_g, m_w_up, m_w_down, m_norm_ple_g, m_w_ple_gate,
                          m_w_ple_proj, m_final_norm_g)))
    v = dict(zip(_NAMES, (v_norm_mix_g, v_w_in, v_conv_qk, v_b_if, v_mlstm_norm_g, v_sinks, v_w_branch_a,
                          v_w_branch_b, v_w_out, v_norm_mlp_g, v_w_up, v_w_down, v_norm_ple_g, v_w_ple_gate,
                          v_w_ple_proj, v_final_norm_g)))
    return _step(x, p, loss_target, w, m, v)
```

```python
import jax
import jax.numpy as jnp
from jax import lax
from jax.experimental import pallas as pl
from jax.experimental.pallas import tpu as pltpu

F32 = jnp.float32
BF16 = jnp.bfloat16

D = 1024
PLE = 256
MLH = 4
DQK = 128
DV = 256
CONV = 4
CHUNK = 128
SWH = 16
SWKV = 4
SWG = SWH // SWKV
HD = 64
WIN = 128
DFF = 4096
EPS = 1e-6
N_IN = 6664
NP = 7168
C_QK, C_V, C_O, C_QSW, C_GA, C_GB, C_KV, C_IF = 0, 1024, 2048, 3072, 4096, 5120, 6144, 6656
IFW = NP - C_IF

ADAM_LR = 0.001
ADAM_B1 = 0.9
ADAM_B2 = 0.999
ADAM_EPS = 1e-08
ADAM_WD = 0.01
ADAM_STEP = 10

TOK_TILE = 512
VMEM_LIMIT = 58 * 1024 * 1024


def _params(**kw):
    return pltpu.CompilerParams(vmem_limit_bytes=VMEM_LIMIT, **kw)


def _pick(n, cap):
    if n <= cap:
        return n
    t = cap - cap % 128
    while t > 128 and n % t:
        t -= 128
    assert n % t == 0, (n, cap)
    return t


def _dot(a, b, dims):
    return lax.dot_general(a, b, (dims, ((), ())), preferred_element_type=F32)


def _dot_nn(a, b):
    return _dot(a, b, ((1,), (0,)))


def _dot_nt(a, b):
    return _dot(a, b, ((1,), (1,)))


def _dot_tn(a, b):
    return _dot(a, b, ((0,), (0,)))


def _sigmoid(x):
    return 1.0 / (1.0 + jnp.exp(-x))


def _mm(a, b, mode, out_dtype, name, out_chunks=1):
    bch = b.shape[0] if b.ndim == 3 else 1
    brows, bcols = b.shape[-2], b.shape[-1] * bch
    if mode == "nn":
        (m, k), (k2, n) = a.shape, (brows, bcols)
    elif mode == "nt":
        (m, k), (n, k2) = a.shape, (brows, bcols)
    else:
        (k, m), (k2, n) = a.shape, (brows, bcols)
    assert k == k2, (a.shape, b.shape, mode)
    n_cap = n // max(out_chunks, 1 if mode == "nt" else bch)
    k_cap = k // bch if mode == "nt" else k
    tm, tn, tk = _pick(m, 1024), _pick(n_cap, 1024), _pick(k_cap, 2048)
    nk = k // tk
    if mode == "nn":
        a_spec = pl.BlockSpec((tm, tk), lambda i, j, kk: (i, kk))
        if bch > 1:
            bpc = (n // bch) // tn
            b_spec = pl.BlockSpec((None, tk, tn), lambda i, j, kk: (j // bpc, kk, j % bpc))
        else:
            b_spec = pl.BlockSpec((tk, tn), lambda i, j, kk: (kk, j))
        dot = _dot_nn
    elif mode == "nt":
        a_spec = pl.BlockSpec((tm, tk), lambda i, j, kk: (i, kk))
        if bch > 1:
            bpc = (k // bch) // tk
            b_spec = pl.BlockSpec((None, tn, tk), lambda i, j, kk: (kk // bpc, j, kk % bpc))
        else:
            b_spec = pl.BlockSpec((tn, tk), lambda i, j, kk: (j, kk))
        dot = _dot_nt
    else:
        assert bch == 1
        a_spec = pl.BlockSpec((tk, tm), lambda i, j, kk: (kk, i))
        b_spec = pl.BlockSpec((tk, tn), lambda i, j, kk: (kk, j))
        dot = _dot_tn
    if out_chunks > 1:
        npc = (n // out_chunks) // tn
        out_spec = pl.BlockSpec((None, tm, tn), lambda i, j, kk: (j // npc, i, j % npc))
        out_shape = jax.ShapeDtypeStruct((out_chunks, m, n // out_chunks), out_dtype)
    else:
        out_spec = pl.BlockSpec((tm, tn), lambda i, j, kk: (i, j))
        out_shape = jax.ShapeDtypeStruct((m, n), out_dtype)

    def body(a_ref, b_ref, o_ref, acc_ref):
        kk = pl.program_id(2)

        @pl.when(kk == 0)
        def _():
            acc_ref[...] = jnp.zeros_like(acc_ref)

        acc_ref[...] += dot(a_ref[...], b_ref[...])

        @pl.when(kk == nk - 1)
        def _():
            o_ref[...] = acc_ref[...].astype(out_dtype)

    return pl.pallas_call(
        body, name=name, grid=(m // tm, n // tn, nk),
        in_specs=[a_spec, b_spec], out_specs=out_spec, out_shape=out_shape,
        scratch_shapes=[pltpu.VMEM((tm, tn), F32)],
        compiler_params=_params(dimension_semantics=("parallel", "parallel", "arbitrary")),
    )(a, b)


def _tile(col0=0):
    return lambda tm, tn: pl.BlockSpec((tm, tn), lambda i, j, kk: (i, col0 // tn + j))


def _row():
    return lambda tm, tn: pl.BlockSpec((1, tn), lambda i, j, kk: (0, j))


def _mm_ep(pairs, mode, name, epilogue, ins, outs, tm, tn, aliases=None):
    a0, b0 = pairs[0]
    bch = b0.shape[0] if b0.ndim == 3 else 1
    m, k = a0.shape
    tm = _pick(m, tm)
    n = b0.shape[-1] * bch if mode == "nn" else b0.shape[-2]
    tk = _pick(k // bch if mode == "nt" else k, 2048)
    nk = k // tk
    a_spec = pl.BlockSpec((tm, tk), lambda i, j, kk: (i, kk))
    if mode == "nn":
        dot = _dot_nn
        if bch > 1:
            bpc = (n // bch) // tn
            b_spec = pl.BlockSpec((None, tk, tn), lambda i, j, kk: (j // bpc, kk, j % bpc))
        else:
            b_spec = pl.BlockSpec((tk, tn), lambda i, j, kk: (kk, j))
    else:
        dot = _dot_nt
        if bch > 1:
            bpc = (k // bch) // tk
            b_spec = pl.BlockSpec((None, tn, tk), lambda i, j, kk: (kk // bpc, j, kk % bpc))
        else:
            b_spec = pl.BlockSpec((tn, tk), lambda i, j, kk: (j, kk))
    npair, nin, nout = len(pairs), len(ins), len(outs)

    def body(*refs):
        ab = refs[:2 * npair]
        in_refs = refs[2 * npair:2 * npair + nin]
        out_refs = refs[2 * npair + nin:2 * npair + nin + nout]
        accs = refs[2 * npair + nin + nout:]
        i, j, kk = pl.program_id(0), pl.program_id(1), pl.program_id(2)
        for p in range(npair):
            prod = dot(ab[2 * p][...], ab[2 * p + 1][...])

            @pl.when(kk == 0)
            def _():
                accs[p][...] = prod

            @pl.when(kk > 0)
            def _():
                accs[p][...] += prod

        @pl.when(kk == nk - 1)
        def _():
            epilogue([acc[...] for acc in accs], in_refs, out_refs, i, j)

    operands = [x for pair in pairs for x in pair] + [a for a, _ in ins]
    io_alias = {2 * npair + i: o for i, o in (aliases or {}).items()}
    return pl.pallas_call(
        body, name=name, grid=(m // tm, n // tn, nk),
        in_specs=[a_spec, b_spec] * npair + [mk(tm, tn) for _, mk in ins],
        out_specs=[mk(tm, tn) for _, mk in outs], out_shape=[s for s, _ in outs],
        scratch_shapes=[pltpu.VMEM((tm, tn), F32)] * npair, input_output_aliases=io_alias,
        compiler_params=_params(dimension_semantics=("arbitrary", "arbitrary", "arbitrary")),
    )(*operands)


def _tok(w, j=0):
    return pl.BlockSpec((TOK_TILE, w), lambda i: (i, j))


def _rep(shape):
    return pl.BlockSpec(shape, lambda i: (0,) * len(shape))


def _rms(x):
    rstd = lax.rsqrt(jnp.mean(x * x, axis=-1, keepdims=True) + EPS)
    return x * rstd, rstd


def _rms_bwd(xn, rstd, dxn):
    return rstd * (dxn - xn * jnp.mean(dxn * xn, axis=-1, keepdims=True))


def _norm_fwd(x, g, name):
    t = x.shape[0]

    def body(x_ref, g_ref, h_ref):
        xn, _ = _rms(x_ref[...])
        h_ref[...] = (xn * g_ref[...]).astype(BF16)

    return pl.pallas_call(
        body, name=name, grid=(t // TOK_TILE,), in_specs=[_tok(D), _rep((1, D))], out_specs=_tok(D),
        out_shape=jax.ShapeDtypeStruct((t, D), BF16), compiler_params=_params(),
    )(x, g)


def _halo_prev(w, j=0, rows=8):
    r = TOK_TILE // rows
    return pl.BlockSpec((rows, w), lambda i: (jnp.maximum(i * r - 1, 0), j))


def _last8(halo_ref):
    return halo_ref[...].astype(F32)[halo_ref.shape[0] - 8:]


def _halo_next(w, nt, j=0):
    r = TOK_TILE // 8
    return pl.BlockSpec((8, w), lambda i: (jnp.minimum((i + 1) * r, nt * r - 1), j))


def _shift_down(x, halo, s):
    if s == 0:
        return x
    r = pltpu.roll(x, s, 0)
    hs = pltpu.roll(halo, s, 0)
    row = lax.broadcasted_iota(jnp.int32, hs.shape, 0)
    top = jnp.where(row < s, hs, r[0:8])
    return top if x.shape[0] == 8 else jnp.concatenate([top, r[8:]], axis=0)


def _shift_up(x, halo, s):
    if s == 0:
        return x
    n = x.shape[0]
    r = pltpu.roll(x, n - s, 0)
    hs = pltpu.roll(halo, 8 - s, 0)
    row = lax.broadcasted_iota(jnp.int32, hs.shape, 0)
    bot = jnp.where(row >= 8 - s, hs, r[n - 8:])
    return jnp.concatenate([r[:n - 8], bot], axis=0)


def _bf(x):
    return x.astype(BF16).astype(F32)


def _conv_taps(x, halo, w):
    x, halo, w = _bf(x), _bf(halo), _bf(w)
    acc = x * w[CONV - 1:CONV, :]
    for j in range(CONV - 1):
        acc = acc + _shift_down(x, halo, CONV - 1 - j) * w[j:j + 1, :]
    return acc


_Q_SCALE = DQK ** -0.5


def _qscale_row():
    lane = lax.broadcasted_iota(jnp.int32, (1, D), 1)
    return jnp.where(lane < MLH * DQK, _Q_SCALE, 1.0).astype(F32)


def _conv_silu_fwd(proj, conv_w):
    t = proj.shape[0]

    def body(x_ref, halo_ref, w_ref, o_ref):
        halo = jnp.where(pl.program_id(0) > 0, _last8(halo_ref), 0.0)
        c = _conv_taps(x_ref[...].astype(F32), halo, w_ref[...])
        o_ref[...] = (c * _sigmoid(c) * _qscale_row()).astype(BF16)

    return pl.pallas_call(
        body, name="conv_silu_fwd", grid=(t // TOK_TILE,),
        in_specs=[_tok(D, C_QK // D), _halo_prev(D, C_QK // D, 16), _rep((CONV, D))], out_specs=_tok(D),
        out_shape=jax.ShapeDtypeStruct((t, D), BF16), compiler_params=_params(),
    )(proj, proj, conv_w)


def _conv_silu_bwd(proj, conv_w, dqk, dproj):
    t = proj.shape[0]
    nt = t // TOK_TILE

    def dconv(x, halo, d, w):
        c = _conv_taps(x, halo, w)
        s = _sigmoid(c)
        return d * _qscale_row() * (s * (1.0 + c * (1.0 - s)))

    def body(x_ref, prev_ref, next_ref, w_ref, d_ref, dnext_ref, _, dx_ref, dw_ref):
        @pl.when(pl.program_id(0) == 0)
        def _():
            dw_ref[...] = jnp.zeros_like(dw_ref)

        w = w_ref[...]
        halo = jnp.where(pl.program_id(0) > 0, _last8(prev_ref), 0.0)
        x = x_ref[...].astype(F32)
        dc = _bf(dconv(x, halo, d_ref[...], w))
        xb, halo_b = _bf(x), _bf(halo)
        for j in range(CONV):
            dw_ref[j:j + 1, :] += jnp.sum(dc * _shift_down(xb, halo_b, CONV - 1 - j), axis=0, keepdims=True)
        x_next = next_ref[...].astype(F32)[0:8]
        dc_next = _bf(dconv(x_next, x[TOK_TILE - 8:], dnext_ref[...], w))
        dc_next = jnp.where(pl.program_id(0) < nt - 1, dc_next, 0.0)
        wb = _bf(w)
        acc = dc * wb[CONV - 1:CONV, :]
        for j in range(CONV - 1):
            acc = acc + _shift_up(dc, dc_next, CONV - 1 - j) * wb[j:j + 1, :]
        dx_ref[...] = acc.astype(BF16)

    r16 = TOK_TILE // 16
    next16 = pl.BlockSpec((16, D), lambda i: (jnp.minimum((i + 1) * r16, nt * r16 - 1), C_QK // D))
    return pl.pallas_call(
        body, name="conv_silu_bwd", grid=(nt,),
        in_specs=[_tok(D, C_QK // D), _halo_prev(D, C_QK // D, 16), next16, _rep((CONV, D)), _tok(D),
                  _halo_next(D, nt), _ANY],
        out_specs=[_tok(D, C_QK // D), _rep((CONV, D))],
        out_shape=[jax.ShapeDtypeStruct((t, NP), BF16), jax.ShapeDtypeStruct((CONV, D), F32)],
        input_output_aliases={6: 0}, compiler_params=_params(),
    )(proj, proj, proj, conv_w, dqk, dqk, dproj)


def _gates_fwd(pre_rows, bias_col):
    t = pre_rows.shape[1]

    def body(p_ref, b_ref, g_ref, s_ref):
        z = p_ref[...] + b_ref[...]
        lf = jnp.minimum(z, 0.0) - jnp.log(1.0 + jnp.exp(-jnp.abs(z)))
        lane = lax.broadcasted_iota(jnp.int32, z.shape, 1) % CHUNK
        cum = lf
        s = 1
        while s < CHUNK:
            cum = cum + jnp.where(lane >= s, pltpu.roll(cum, s, 1), 0.0)
            s *= 2
        sub = lax.broadcasted_iota(jnp.int32, z.shape, 0)
        g_ref[...] = jnp.where(sub < MLH, z, cum)
        s_ref[...] = _sigmoid(-z)

    return pl.pallas_call(
        body, name="gates_fwd",
        out_shape=[jax.ShapeDtypeStruct((8, t), F32), jax.ShapeDtypeStruct((8, t), F32)],
        compiler_params=_params(),
    )(pre_rows, bias_col)


def _chunk_terms(grow, gcol, h, m0):
    i_row, b_row = grow[h:h + 1, :], grow[MLH + h:MLH + h + 1, :]
    i_col, b_col = gcol[:, h:h + 1], gcol[:, MLH + h:MLH + h + 1]
    b_last = b_row[:, CHUNK - 1:CHUNK]
    tt = lax.broadcasted_iota(jnp.int32, (CHUNK, CHUNK), 0)
    ss = lax.broadcasted_iota(jnp.int32, (CHUNK, CHUNK), 1)
    log_d = jnp.where(tt >= ss, b_col - b_row + i_row, -jnp.inf)
    m_t = jnp.maximum(b_col + m0, jnp.max(log_d, axis=1, keepdims=True))
    dm = jnp.exp(log_d - m_t)
    wi = jnp.exp(b_col + m0 - m_t)
    m1 = jnp.maximum(b_last + m0, jnp.max(b_last - b_row + i_row, axis=1, keepdims=True))
    ws = jnp.exp(b_last - b_col + i_col - m1)
    dec = jnp.exp(b_last + m0 - m1)
    return dm, wi, m_t, ws, dec, m1


def _mlstm_fwd(qk, proj, grow, gcol, gain):
    t = qk.shape[0]
    nc = t // CHUNK

    def body(qk_ref, v_ref, o_ref, grow_ref, gcol_ref, g_ref, h_ref, y_ref, cs_ref, st_ref, c_scr, st_scr):
        @pl.when(pl.program_id(0) == 0)
        def _():
            c_scr[...] = jnp.zeros_like(c_scr)
            st_scr[...] = jnp.zeros_like(st_scr)

        grow_v, gcol_v = grow_ref[...], gcol_ref[...]
        heads = range(MLH)
        q = [qk_ref[:, h * DQK:(h + 1) * DQK] for h in heads]
        k = [qk_ref[:, MLH * DQK + h * DQK:MLH * DQK + (h + 1) * DQK] for h in heads]
        v = [v_ref[:, h * DV:(h + 1) * DV] for h in heads]
        c0 = [c_scr[h] for h in heads]
        n0 = [st_scr[h, 0:1, :] for h in heads]
        for h in heads:
            cs_ref[0, h] = c0[h]
            st_ref[0, h] = st_scr[h]
        terms = [_chunk_terms(grow_v, gcol_v, h, st_scr[h, 1:2, 0:1]) for h in heads]
        a = [_dot_nt(q[h], k[h]) for h in heads]
        qc = [_dot_nt(q[h], c0[h].astype(BF16)) for h in heads]
        s = [a[h] * terms[h][0] for h in heads]
        sv = [_dot_nn(s[h].astype(BF16), v[h]) for h in heads]
        upd = [_dot_tn((terms[h][3] * v[h]).astype(BF16), k[h]) for h in heads]
        den = [terms[h][1] * jnp.sum(q[h].astype(F32) * n0[h], axis=1, keepdims=True)
               + jnp.sum(s[h], axis=1, keepdims=True) for h in heads]
        hv = [(terms[h][1] * qc[h] + sv[h]) / jnp.maximum(jnp.abs(den[h]), jnp.exp(-terms[h][2])) for h in heads]
        for h in heads:
            sl = slice(h * DV, (h + 1) * DV)
            h_ref[:, sl] = hv[h]
            xn, _ = _rms(hv[h])
            y_ref[:, sl] = (_sigmoid(o_ref[:, sl].astype(F32)) * xn * g_ref[:, sl]).astype(BF16)
        for h in heads:
            dec, m1 = terms[h][4], terms[h][5]
            c_scr[h] = dec * c0[h] + upd[h]
            st_scr[h, 0:1, :] = dec * n0[h] + jnp.sum(terms[h][3] * k[h].astype(F32), axis=0, keepdims=True)
            st_scr[h, 1:2, :] = jnp.broadcast_to(m1, (1, DQK))

    return pl.pallas_call(
        body, name="mlstm_fwd", grid=(nc,),
        in_specs=[pl.BlockSpec((CHUNK, D), lambda c: (c, 0)), pl.BlockSpec((CHUNK, D), lambda c: (c, C_V // D)),
                  pl.BlockSpec((CHUNK, D), lambda c: (c, C_O // D)),
                  pl.BlockSpec((8, CHUNK), lambda c: (0, c)), pl.BlockSpec((CHUNK, 8), lambda c: (c, 0)),
                  pl.BlockSpec((1, D), lambda c: (0, 0))],
        out_specs=[pl.BlockSpec((CHUNK, D), lambda c: (c, 0)), pl.BlockSpec((CHUNK, D), lambda c: (c, 0)),
                   pl.BlockSpec((1, MLH, DV, DQK), lambda c: (c, 0, 0, 0)),
                   pl.BlockSpec((1, MLH, 8, DQK), lambda c: (c, 0, 0, 0))],
        out_shape=[jax.ShapeDtypeStruct((t, D), F32), jax.ShapeDtypeStruct((t, D), BF16),
                   jax.ShapeDtypeStruct((nc, MLH, DV, DQK), F32), jax.ShapeDtypeStruct((nc, MLH, 8, DQK), F32)],
        scratch_shapes=[pltpu.VMEM((MLH, DV, DQK), F32), pltpu.VMEM((MLH, 8, DQK), F32)],
        compiler_params=_params(dimension_semantics=("arbitrary",)),
    )(qk, proj, proj, grow, gcol, gain)


def _mlstm_bwd(qk, proj, grow, gcol, sneg_col, cs, st, hraw, dh, dproj):
    t = qk.shape[0]
    nc = t // CHUNK

    def rev(c):
        return nc - 1 - c

    def nxt(c):
        return jnp.minimum(nc - c, nc - 1)

    def body(qk_ref, v_ref, grow_ref, gcol_ref, sneg_ref, cs_ref, st_ref, cs1_ref, st1_ref, h_ref, dh_ref, _,
             dqk_ref, dv_ref, dif_ref, dbif_ref, dc_scr, dn_scr):
        @pl.when(pl.program_id(0) == 0)
        def _():
            dc_scr[...] = jnp.zeros_like(dc_scr)
            dn_scr[...] = jnp.zeros_like(dn_scr)
            dbif_ref[...] = jnp.zeros_like(dbif_ref)

        grow_v, gcol_v, sneg = grow_ref[...], gcol_ref[...], sneg_ref[...]
        tt = lax.broadcasted_iota(jnp.int32, (CHUNK, CHUNK), 0)
        ss = lax.broadcasted_iota(jnp.int32, (CHUNK, CHUNK), 1)
        lane8 = lax.broadcasted_iota(jnp.int32, (CHUNK, 8), 1)
        heads = range(MLH)
        q = [qk_ref[:, h * DQK:(h + 1) * DQK] for h in heads]
        k = [qk_ref[:, MLH * DQK + h * DQK:MLH * DQK + (h + 1) * DQK] for h in heads]
        qf, kf = [a.astype(F32) for a in q], [a.astype(F32) for a in k]
        vb = [v_ref[:, h * DV:(h + 1) * DV].astype(BF16) for h in heads]
        c0 = [cs_ref[0, h] for h in heads]
        n0 = [st_ref[0, h, 0:1, :] for h in heads]
        dc1 = [dc_scr[h] for h in heads]
        dn1 = [dn_scr[h, 0:1, :] for h in heads]
        terms = [_chunk_terms(grow_v, gcol_v, h, st_ref[0, h, 1:2, 0:1]) for h in heads]
        dm, wi, ws = [t[0] for t in terms], [t[1] for t in terms], [t[3] for t in terms]
        s = [_dot_nt(q[h], k[h]) * dm[h] for h in heads]
        den = [wi[h] * jnp.sum(qf[h] * n0[h], axis=1, keepdims=True) + jnp.sum(s[h], axis=1, keepdims=True)
               for h in heads]
        floor = [jnp.exp(-terms[h][2]) for h in heads]
        g = [jnp.maximum(jnp.abs(den[h]), floor[h]) for h in heads]
        dh_v = [dh_ref[:, h * DV:(h + 1) * DV] for h in heads]
        dnum = [dh_v[h] / g[h] for h in heads]
        dden = [-jnp.sum(dh_v[h] * h_ref[:, h * DV:(h + 1) * DV], axis=1, keepdims=True) / g[h] for h in heads]
        dden = [jnp.where(jnp.abs(den[h]) > floor[h], dden[h] * jnp.sign(den[h]), 0.0) for h in heads]
        dnum_b = [a.astype(BF16) for a in dnum]
        dc1_b = [a.astype(BF16) for a in dc1]
        da = [((_dot_nt(dnum_b[h], vb[h]) + dden[h]) * dm[h]).astype(BF16) for h in heads]
        dq_inter = [_dot_nn(dnum_b[h], c0[h].astype(BF16)) for h in heads]
        dk_inter = [_dot_nn(vb[h], dc1_b[h]) for h in heads]
        dv_inter = [_dot_nt(k[h], dc1_b[h]) for h in heads]
        dc_new = [_dot_tn((wi[h] * dnum[h]).astype(BF16), q[h]) for h in heads]
        dq = [_dot_nn(da[h], k[h]) + wi[h] * (dq_inter[h] + dden[h] * n0[h]) for h in heads]
        dk = [_dot_tn(da[h], q[h]) + ws[h] * (dk_inter[h] + dn1[h]) for h in heads]
        dv = [_dot_tn(s[h].astype(BF16), dnum_b[h]) + ws[h] * dv_inter[h] for h in heads]
        for h in heads:
            dqk_ref[:, h * DQK:(h + 1) * DQK] = dq[h]
            dqk_ref[:, MLH * DQK + h * DQK:MLH * DQK + (h + 1) * DQK] = dk[h]
            dv_ref[:, h * DV:(h + 1) * DV] = dv[h].astype(BF16)
        rk = [jnp.sum(kf[h] * dk[h], axis=1, keepdims=True) for h in heads]
        df = [jnp.sum(qf[h] * dq[h], axis=1, keepdims=True) - rk[h] for h in heads]
        df_row = [jnp.sum(jnp.where(tt == ss, df[h], 0.0), axis=0, keepdims=True) for h in heads]
        suffix = [jnp.sum(jnp.where(ss >= tt, df_row[h], 0.0), axis=1, keepdims=True) for h in heads]
        cross = [jnp.sum(jnp.sum(dc1[h] * cs1_ref[0, h], axis=0, keepdims=True), axis=1, keepdims=True)
                 + jnp.sum(dn1[h] * st1_ref[0, h, 0:1, :], axis=1, keepdims=True) for h in heads]
        dif = jnp.zeros((CHUNK, 8), F32)
        for h in heads:
            dpf = (suffix[h] + cross[h]) * sneg[:, MLH + h:MLH + h + 1]
            dif = dif + jnp.where(lane8 == h, rk[h], 0.0) + jnp.where(lane8 == MLH + h, dpf, 0.0)
            dc_scr[h] = terms[h][4] * dc1[h] + dc_new[h]
            dn_scr[h, 0:1, :] = terms[h][4] * dn1[h] + jnp.sum(wi[h] * dden[h] * qf[h], axis=0, keepdims=True)
        dif_ref[...] = dif
        dbif_ref[...] += jnp.sum(dif, axis=0, keepdims=True)

    return pl.pallas_call(
        body, name="mlstm_bwd", grid=(nc,),
        in_specs=[pl.BlockSpec((CHUNK, D), lambda c: (rev(c), 0)),
                  pl.BlockSpec((CHUNK, D), lambda c: (rev(c), C_V // D)),
                  pl.BlockSpec((8, CHUNK), lambda c: (0, rev(c))),
                  pl.BlockSpec((CHUNK, 8), lambda c: (rev(c), 0)),
                  pl.BlockSpec((CHUNK, 8), lambda c: (rev(c), 0)),
                  pl.BlockSpec((1, MLH, DV, DQK), lambda c: (rev(c), 0, 0, 0)),
                  pl.BlockSpec((1, MLH, 8, DQK), lambda c: (rev(c), 0, 0, 0)),
                  pl.BlockSpec((1, MLH, DV, DQK), lambda c: (nxt(c), 0, 0, 0)),
                  pl.BlockSpec((1, MLH, 8, DQK), lambda c: (nxt(c), 0, 0, 0)),
                  pl.BlockSpec((CHUNK, D), lambda c: (rev(c), 0)),
                  pl.BlockSpec((CHUNK, D), lambda c: (rev(c), 0)), _ANY],
        out_specs=[pl.BlockSpec((CHUNK, D), lambda c: (rev(c), 0)),
                   pl.BlockSpec((CHUNK, D), lambda c: (rev(c), C_V // D)),
                   pl.BlockSpec((CHUNK, 8), lambda c: (rev(c), 0)),
                   pl.BlockSpec((1, 8), lambda c: (0, 0))],
        out_shape=[jax.ShapeDtypeStruct((t, D), F32), jax.ShapeDtypeStruct((t, NP), BF16),
                   jax.ShapeDtypeStruct((t, 8), F32), jax.ShapeDtypeStruct((1, 8), F32)],
        scratch_shapes=[pltpu.VMEM((MLH, DV, DQK), F32), pltpu.VMEM((MLH, 8, DQK), F32)],
        input_output_aliases={11: 1}, compiler_params=_params(dimension_semantics=("arbitrary",)),
    )(qk, proj, grow, gcol, sneg_col, cs, st, cs, st, hraw, dh, dproj)


_ANY = pl.BlockSpec(memory_space=pl.ANY)


_SW_SCALE = HD ** -0.5
_KVB = C_KV // (2 * SWKV * HD)


def _swa_mask(n):
    ki = lax.broadcasted_iota(jnp.int32, (2 * WIN, SWG * WIN), 0)
    qi = lax.broadcasted_iota(jnp.int32, (2 * WIN, SWG * WIN), 1) % WIN
    return (ki > qi) & (ki <= qi + WIN) & ((n > 0) | (ki >= WIN))


def _group_rows(x_ref, hk):
    return jnp.concatenate([x_ref[:, (hk * SWG + g) * HD:(hk * SWG + g + 1) * HD] for g in range(SWG)], axis=0)


def _group_lanes(x_ref, hk):
    return jnp.concatenate([x_ref[hk * SWG + g:hk * SWG + g + 1, :] for g in range(SWG)], axis=1)


def _sink_lanes(sink_ref, hk):
    return jnp.concatenate([jnp.broadcast_to(sink_ref[:, hk * SWG + g:hk * SWG + g + 1], (1, WIN))
                            for g in range(SWG)], axis=1)


def _swa_fwd(proj, sinks):
    t = proj.shape[0]
    nb = t // WIN

    def body(q_ref, kvc_ref, kvp_ref, sink_ref, y_ref, lse_ref):
        valid = _swa_mask(pl.program_id(0))
        kvh = range(SWKV)
        kb = [jnp.concatenate([kvp_ref[:, hk * HD:(hk + 1) * HD], kvc_ref[:, hk * HD:(hk + 1) * HD]],
                              axis=0).astype(BF16) for hk in kvh]
        vb = [jnp.concatenate([kvp_ref[:, (SWKV + hk) * HD:(SWKV + hk + 1) * HD],
                               kvc_ref[:, (SWKV + hk) * HD:(SWKV + hk + 1) * HD]], axis=0).astype(BF16) for hk in kvh]
        sink = [_sink_lanes(sink_ref, hk) for hk in kvh]
        logits = [_dot_nt(kb[hk], _group_rows(q_ref, hk).astype(BF16)) for hk in kvh]
        logits = [jnp.where(valid, logits[hk] * _SW_SCALE, -jnp.inf) for hk in kvh]
        m = [jnp.maximum(jnp.max(logits[hk], axis=0, keepdims=True), sink[hk]) for hk in kvh]
        p = [jnp.exp(logits[hk] - m[hk]) for hk in kvh]
        denom = [jnp.sum(p[hk], axis=0, keepdims=True) + jnp.exp(sink[hk] - m[hk]) for hk in kvh]
        y4 = [_dot_tn((p[hk] / denom[hk]).astype(BF16), vb[hk]).astype(BF16) for hk in kvh]
        for hk in kvh:
            lse4 = m[hk] + jnp.log(denom[hk])
            for g in range(SWG):
                hq = hk * SWG + g
                y_ref[:, hq * HD:(hq + 1) * HD] = y4[hk][g * WIN:(g + 1) * WIN]
                lse_ref[hq:hq + 1, :] = lse4[:, g * WIN:(g + 1) * WIN]

    return pl.pallas_call(
        body, name="swa_fwd", grid=(nb,),
        in_specs=[pl.BlockSpec((WIN, D), lambda n: (n, C_QSW // D)),
                  pl.BlockSpec((WIN, 512), lambda n: (n, _KVB)),
                  pl.BlockSpec((WIN, 512), lambda n: (jnp.maximum(n - 1, 0), _KVB)),
                  pl.BlockSpec((1, SWH), lambda n: (0, 0))],
        out_specs=[pl.BlockSpec((WIN, D), lambda n: (n, 0)), pl.BlockSpec((SWH, WIN), lambda n: (0, n))],
        out_shape=[jax.ShapeDtypeStruct((t, D), BF16), jax.ShapeDtypeStruct((SWH, t), F32)],
        compiler_params=_params(),
    )(proj, proj, proj, sinks)


def _swa_bwd(proj, sinks, lse, dyb, dproj):
    t = proj.shape[0]
    nb = t // WIN

    def body(q_ref, kvc_ref, kvp_ref, sink_ref, lse_ref, dy_ref, _, dq_ref, dself_ref, dprev_ref, ds_ref):
        @pl.when(pl.program_id(0) == 0)
        def _():
            ds_ref[...] = jnp.zeros_like(ds_ref)

        valid = _swa_mask(pl.program_id(0))
        kvh = range(SWKV)
        ks = [slice(hk * HD, (hk + 1) * HD) for hk in kvh]
        vs = [slice(SWKV * HD + hk * HD, SWKV * HD + (hk + 1) * HD) for hk in kvh]
        kb = [jnp.concatenate([kvp_ref[:, ks[hk]], kvc_ref[:, ks[hk]]], axis=0).astype(BF16) for hk in kvh]
        vb = [jnp.concatenate([kvp_ref[:, vs[hk]], kvc_ref[:, vs[hk]]], axis=0).astype(BF16) for hk in kvh]
        qb = [_group_rows(q_ref, hk).astype(BF16) for hk in kvh]
        dyb_ = [_group_rows(dy_ref, hk).astype(BF16) for hk in kvh]
        lse4 = [_group_lanes(lse_ref, hk) for hk in kvh]
        logits = [_dot_nt(kb[hk], qb[hk]) for hk in kvh]
        dpt = [_dot_nt(vb[hk], dyb_[hk]) for hk in kvh]
        p = [jnp.exp(jnp.where(valid, logits[hk] * _SW_SCALE, -jnp.inf) - lse4[hk]) for hk in kvh]
        delta = [jnp.sum(p[hk] * dpt[hk], axis=0, keepdims=True) for hk in kvh]
        dsm = [(p[hk] * (dpt[hk] - delta[hk])).astype(BF16) for hk in kvh]
        dq4 = [(_dot_tn(dsm[hk], kb[hk]) * _SW_SCALE).astype(BF16) for hk in kvh]
        dkb = [_dot_nn(dsm[hk], qb[hk]) * _SW_SCALE for hk in kvh]
        dvb = [_dot_nn(p[hk].astype(BF16), dyb_[hk]) for hk in kvh]
        for hk in kvh:
            dsink4 = jnp.exp(_sink_lanes(sink_ref, hk) - lse4[hk]) * delta[hk]
            for g in range(SWG):
                hq = hk * SWG + g
                dq_ref[:, hq * HD:(hq + 1) * HD] = dq4[hk][g * WIN:(g + 1) * WIN]
                ds_ref[:, hq:hq + 1] += -jnp.sum(dsink4[:, g * WIN:(g + 1) * WIN], axis=1, keepdims=True)
            dprev_ref[:, ks[hk]] = dkb[hk][:WIN]
            dself_ref[:, ks[hk]] = dkb[hk][WIN:]
            dprev_ref[:, vs[hk]] = dvb[hk][:WIN]
            dself_ref[:, vs[hk]] = dvb[hk][WIN:]

    return pl.pallas_call(
        body, name="swa_bwd", grid=(nb,),
        in_specs=[pl.BlockSpec((WIN, D), lambda n: (n, C_QSW // D)),
                  pl.BlockSpec((WIN, 512), lambda n: (n, _KVB)),
                  pl.BlockSpec((WIN, 512), lambda n: (jnp.maximum(n - 1, 0), _KVB)),
                  pl.BlockSpec((1, SWH), lambda n: (0, 0)),
                  pl.BlockSpec((SWH, WIN), lambda n: (0, n)),
                  pl.BlockSpec((WIN, D), lambda n: (n, 0)), _ANY],
        out_specs=[pl.BlockSpec((WIN, D), lambda n: (n, C_QSW // D)), pl.BlockSpec((WIN, 512), lambda n: (n, 0)),
                   pl.BlockSpec((WIN, 512), lambda n: (jnp.maximum(n - 1, 0), 0)),
                   pl.BlockSpec((1, SWH), lambda n: (0, 0))],
        out_shape=[jax.ShapeDtypeStruct((t, NP), BF16), jax.ShapeDtypeStruct((t, 512), F32),
                   jax.ShapeDtypeStruct((t, 512), F32), jax.ShapeDtypeStruct((1, SWH), F32)],
        input_output_aliases={6: 0}, compiler_params=_params(),
    )(proj, proj, proj, sinks, lse, dyb, dproj)


def _kv_combine(dself, dnext, dif, dproj):
    t = dself.shape[0]
    rows = _pick(t, 512)

    def body(a_ref, b_ref, dif_ref, _, o_ref):
        row = pl.program_id(0) * rows + lax.broadcasted_iota(jnp.int32, (rows, 1), 0)
        o_ref[:, 0:512] = (a_ref[...] + jnp.where(row < t - WIN, b_ref[...], 0.0)).astype(BF16)
        lane = lax.broadcasted_iota(jnp.int32, (rows, 128), 1)
        dif_v = dif_ref[...]
        first = jnp.zeros((rows, 128), F32)
        for col in range(8):
            first = first + jnp.where(lane == col, dif_v[:, col:col + 1], 0.0)
        o_ref[:, 512:640] = first.astype(BF16)
        o_ref[:, 640:512 + IFW] = jnp.zeros((rows, IFW - 128), BF16)

    return pl.pallas_call(
        body, name="kv_combine", grid=(t // rows,),
        in_specs=[pl.BlockSpec((rows, 512), lambda n: (n, 0)), pl.BlockSpec((rows, 512), lambda n: (n, 0)),
                  pl.BlockSpec((rows, 8), lambda n: (n, 0)), _ANY],
        out_specs=pl.BlockSpec((rows, 512 + IFW), lambda n: (n, C_KV // (512 + IFW))),
        out_shape=jax.ShapeDtypeStruct((t, NP), BF16), input_output_aliases={3: 0}, compiler_params=_params(),
    )(dself, dnext, dif, dproj)


def _sds(t, n, dtype):
    return jax.ShapeDtypeStruct((t, n), dtype)


def _proj_in(h0, w_in):
    t = h0.shape[0]
    tn = 2 * IFW

    def epilogue(accs, ins, outs, i, j):
        outs[0][...] = accs[0].astype(BF16)

        @pl.when(j == C_IF // tn)
        def _():
            outs[1][...] = accs[0][:, C_IF % tn:C_IF % tn + 128]

    gate_cols = lambda tm, tn: pl.BlockSpec((tm, 128), lambda i, j, kk: (i, 0))
    return _mm_ep([(h0, w_in)], "nn", "mm_in", epilogue, [],
                  [(_sds(t, NP, BF16), _tile()), (_sds(t, 128, F32), gate_cols)], 1024, tn)


def _branch_merge(ya, yb, wa, wb, proj):
    t = ya.shape[0]

    def epilogue(accs, ins, outs, i, j):
        za, zb = accs
        merged = _sigmoid(ins[0][...].astype(F32)) * za + _sigmoid(ins[1][...].astype(F32)) * zb
        outs[0][...] = merged.astype(BF16)
        outs[1][...] = za.astype(BF16)
        outs[2][...] = zb.astype(BF16)

    return _mm_ep([(ya, wa), (yb, wb)], "nn", "mm_branch_merge", epilogue, [(proj, _tile(C_GA)), (proj, _tile(C_GB))],
                  [(_sds(t, D, BF16), _tile())] * 3, 1024, 512)


def _dmerged_bwd(dxb, w_out, proj, za, zb):
    t = dxb.shape[0]

    def epilogue(accs, ins, outs, i, j):
        dm = accs[0]
        sa, sb = _sigmoid(ins[0][...].astype(F32)), _sigmoid(ins[1][...].astype(F32))
        outs[0][...] = (dm * sa).astype(BF16)
        outs[1][...] = (dm * sb).astype(BF16)
        outs[2][:, 0:D] = (dm * ins[2][...].astype(F32) * sa * (1.0 - sa)).astype(BF16)
        outs[2][:, D:2 * D] = (dm * ins[3][...].astype(F32) * sb * (1.0 - sb)).astype(BF16)

    gate_cols = lambda tm, tn: pl.BlockSpec((tm, 2 * D), lambda i, j, kk: (i, C_GA // (2 * D)))
    return _mm_ep([(dxb, w_out)], "nt", "mm_dmerged_bwd", epilogue,
                  [(proj, _tile(C_GA)), (proj, _tile(C_GB)), (za, _tile()), (zb, _tile())],
                  [(_sds(t, D, BF16), _tile()), (_sds(t, D, BF16), _tile()), (_sds(t, NP, BF16), gate_cols)], 1024, D)


def _dya_bwd(dza, wa, hraw, proj, g, dproj):
    t = dza.shape[0]

    def epilogue(accs, ins, outs, i, j):
        h_ref, o_ref, g_ref, _ = ins
        dh_ref, do_ref, dg_ref = outs

        @pl.when(i == 0)
        def _():
            dg_ref[...] = jnp.zeros_like(dg_ref)

        dy = accs[0]
        so = _sigmoid(o_ref[...].astype(F32))
        for h in range(MLH):
            sl = slice(h * DV, (h + 1) * DV)
            xn, rstd = _rms(h_ref[:, sl])
            gs = g_ref[:, sl]
            do_ref[:, sl] = (dy[:, sl] * xn * gs * so[:, sl] * (1.0 - so[:, sl])).astype(BF16)
            dhn = dy[:, sl] * so[:, sl]
            dg_ref[:, sl] += jnp.sum(dhn * xn, axis=0, keepdims=True)
            dh_ref[:, sl] = _rms_bwd(xn, rstd, dhn * gs)

    return _mm_ep([(dza, wa)], "nt", "mm_dya_bwd", epilogue,
                  [(hraw, _tile()), (proj, _tile(C_O)), (g, _row()), (dproj, lambda tm, tn: _ANY)],
                  [(_sds(t, D, F32), _tile()), (_sds(t, NP, BF16), _tile(C_O)), (_sds(1, D, F32), _row())],
                  1024, D, aliases={3: 1})


def _up_act(hn, w_up):
    t = hn.shape[0]

    def epilogue(accs, ins, outs, i, j):
        r = jnp.maximum(accs[0], 0.0)
        outs[0][...] = (r * r).astype(BF16)
        outs[1][...] = accs[0].astype(BF16)

    return _mm_ep([(hn, w_up)], "nn", "mm_up_act", epilogue, [],
                  [(_sds(t, DFF, BF16), _tile()), (_sds(t, DFF, BF16), _tile())], 1024, 1024)


def _da_du(dxb, w_down, u):
    t = dxb.shape[0]

    def epilogue(accs, ins, outs, i, j):
        outs[0][...] = (accs[0] * 2.0 * jnp.maximum(ins[0][...].astype(F32), 0.0)).astype(BF16)

    return _mm_ep([(dxb, w_down)], "nt", "mm_da_du", epilogue, [(u, _tile())], [(_sds(t, DFF, BF16), _tile())],
                  1024, 1024)[0]


def _resid_norm_mm(a, w, x, g, name):
    t = x.shape[0]

    def epilogue(accs, ins, outs, i, j):
        x1 = ins[0][...] + accs[0]
        outs[0][...] = x1
        xn, _ = _rms(x1)
        outs[1][...] = (xn * ins[1][...]).astype(BF16)

    return _mm_ep([(a, w)], "nn", name, epilogue, [(x, _tile()), (g, _row())],
                  [(_sds(t, D, F32), _tile()), (_sds(t, D, BF16), _tile())], 1024, D)


def _norm_bwd_mm(dy, w, x, g, dres, name):
    t = x.shape[0]

    def epilogue(accs, ins, outs, i, j):
        @pl.when(i == 0)
        def _():
            outs[2][...] = jnp.zeros_like(outs[2])

        dh = accs[0]
        xn, rstd = _rms(ins[0][...])
        outs[2][...] += jnp.sum(dh * xn, axis=0, keepdims=True)
        dx = ins[2][...] + _rms_bwd(xn, rstd, dh * ins[1][...])
        outs[0][...] = dx
        outs[1][...] = dx.astype(BF16)

    return _mm_ep([(dy, w)], "nt", name, epilogue, [(x, _tile()), (g, _row()), (dres, _tile())],
                  [(_sds(t, D, F32), _tile()), (_sds(t, D, BF16), _tile()), (_sds(1, D, F32), _row())], 1024, D)


def _ple_final_mm(hn2, w_gate, x2, pp, target, gf):
    t = x2.shape[0]

    def epilogue(accs, ins, outs, i, j):
        loss_ref, dg_ref, dx_ref, dpp_ref, dgp_ref = outs

        @pl.when(i == 0)
        def _():
            loss_ref[...] = jnp.zeros_like(loss_ref)
            dg_ref[...] = jnp.zeros_like(dg_ref)

        gate = _sigmoid(accs[0])
        pp_v = ins[1][...]
        x3 = ins[0][...] + gate * pp_v
        xn, rstd = _rms(x3)
        gf_v = ins[3][...]
        err = xn * gf_v - ins[2][...]
        loss_ref[...] += (0.5 / D) * jnp.sum(jnp.sum(err * err, axis=0, keepdims=True), axis=1, keepdims=True)
        dy = err * (1.0 / D)
        dg_ref[...] += jnp.sum(dy * xn, axis=0, keepdims=True)
        dx3 = _rms_bwd(xn, rstd, dy * gf_v)
        dx_ref[...] = dx3
        dpp_ref[...] = (dx3 * gate).astype(BF16)
        dgp_ref[...] = (dx3 * pp_v * gate * (1.0 - gate)).astype(BF16)

    one = lambda tm, tn: pl.BlockSpec((1, 1), lambda i, j, kk: (0, 0))
    return _mm_ep([(hn2, w_gate)], "nn", "mm_ple_final", epilogue,
                  [(x2, _tile()), (pp, _tile()), (target, _tile()), (gf, _row())],
                  [(_sds(1, 1, F32), one), (_sds(1, D, F32), _row()), (_sds(t, D, F32), _tile()),
                   (_sds(t, D, BF16), _tile()), (_sds(t, D, BF16), _tile())], 512, D)


_WIN_SEGMENTS = ((0, 3072, C_QK), (3072, 8, C_IF), (3080, 1024, C_QSW), (4104, 256, C_KV), (4360, 256, C_KV + 256),
                 (4616, 1024, C_GA), (5640, 1024, C_GB))
_WIN_SHARD = N_IN // 4


def _win_pieces():
    out = []
    for src, width, dst in _WIN_SEGMENTS:
        while width:
            chip, col = divmod(src, _WIN_SHARD)
            n = min(width, _WIN_SHARD - col)
            out.append((chip, col, n, dst))
            src, dst, width = src + n, dst + n, width - n
    return out


def _win_pad(shards):
    rows = shards.shape[1]
    tr = _pick(rows, 256)

    def body(s_ref, o_ref):
        for chip, col, n, dst in _win_pieces():
            o_ref[:, dst:dst + n] = s_ref[chip, :, col:col + n]
        o_ref[:, C_IF + 8:NP] = jnp.zeros((tr, NP - C_IF - 8), shards.dtype)

    return pl.pallas_call(
        body, name="win_pad", grid=(rows // tr,), in_specs=[pl.BlockSpec((4, tr, _WIN_SHARD), lambda i: (0, i, 0))],
        out_specs=pl.BlockSpec((tr, NP), lambda i: (i, 0)), out_shape=jax.ShapeDtypeStruct((rows, NP), shards.dtype),
        compiler_params=_params(),
    )(shards)


def _win_unpad(wp):
    rows = wp.shape[0]
    tr = _pick(rows, 256)

    def body(p_ref, o_ref):
        for chip, col, n, dst in _win_pieces():
            o_ref[chip, :, col:col + n] = p_ref[:, dst:dst + n]

    return pl.pallas_call(
        body, name="win_unpad", grid=(rows // tr,), in_specs=[pl.BlockSpec((tr, NP), lambda i: (i, 0))],
        out_specs=pl.BlockSpec((4, tr, _WIN_SHARD), lambda i: (0, i, 0)),
        out_shape=jax.ShapeDtypeStruct((4, rows, _WIN_SHARD), wp.dtype), compiler_params=_params(),
    )(wp)


def _local_step(x, p, target, w, late_weights=None, early_grads=None, mid_grads=None, last_grad=None):
    t = x.shape[0]
    pb = p.astype(BF16)
    w = dict(w)

    h0 = _norm_fwd(x, w["norm_mix_g"], "norm_mix")
    proj, gates = _proj_in(h0, w["w_in"])
    qk = _conv_silu_fwd(proj, w["conv_qk"])
    grow, sneg_row = _gates_fwd(gates[:, 0:8].T, w["b_if"].reshape(8, 1))
    gcol, sneg_col = grow.T, sneg_row.T
    hraw, ya, cs, st = _mlstm_fwd(qk, proj, grow, gcol, w["mlstm_norm_g"])
    yb, lse = _swa_fwd(proj, w["sinks"])
    if late_weights is not None:
        w.update(late_weights(yb))
    merged, za, zb = _branch_merge(ya, yb, w["w_branch_a"], w["w_branch_b"], proj)
    x1, hn1 = _resid_norm_mm(merged, w["w_out"], x, w["norm_mlp_g"], "mm_out_norm")
    act, u = _up_act(hn1, w["w_up"])
    x2, hn2 = _resid_norm_mm(act, w["w_down"], x1, w["norm_ple_g"], "mm_down_norm")
    pp = _mm(pb, w["w_ple_proj"], "nn", F32, "mm_ple_proj")
    loss, d_final_g, dx3, dpp, dgpre = _ple_final_mm(hn2, w["w_ple_gate"], x2, pp, target, w["final_norm_g"])

    g = {"final_norm_g": d_final_g}
    g["w_ple_proj"] = _mm(pb, dpp, "tn", F32, "mm_d_ple_proj", out_chunks=4)
    g["w_ple_gate"] = _mm(hn2, dgpre, "tn", F32, "mm_d_ple_gate")
    dx2, dx2b, g["norm_ple_g"] = _norm_bwd_mm(dgpre, w["w_ple_gate"], x2, w["norm_ple_g"], dx3, "mm_dhn2_norm")
    g["w_down"] = _mm(act, dx2b, "tn", F32, "mm_d_down")
    du = _da_du(dx2b, w["w_down"], u)
    g["w_up"] = _mm(hn1, du, "tn", F32, "mm_d_up", out_chunks=4)
    dx1, dx1b, g["norm_mlp_g"] = _norm_bwd_mm(du, w["w_up"], x1, w["norm_mlp_g"], dx2, "mm_dhn1_norm")
    g["w_out"] = _mm(merged, dx1b, "tn", F32, "mm_d_out")
    dza, dzb, dproj = _dmerged_bwd(dx1b, w["w_out"], proj, za, zb)
    g["w_branch_a"] = _mm(ya, dza, "tn", F32, "mm_d_branch_a")
    g["w_branch_b"] = _mm(yb, dzb, "tn", F32, "mm_d_branch_b")
    gain = w["mlstm_norm_g"] if early_grads is None else w["mlstm_norm_g"] + early_grads(g)
    dyb = _mm(dzb, w["w_branch_b"], "nt", F32, "mm_dyb")
    dhraw, dproj, g["mlstm_norm_g"] = _dya_bwd(dza, w["w_branch_a"], hraw, proj, gain, dproj)
    if mid_grads is not None:
        sneg_col = sneg_col + mid_grads(dhraw)
    dqk, dproj, dif, g["b_if"] = _mlstm_bwd(qk, proj, grow, gcol, sneg_col, cs, st, hraw, dhraw, dproj)
    dproj, g["conv_qk"] = _conv_silu_bwd(proj, w["conv_qk"], dqk, dproj)
    dproj, dkv_self, dkv_prev, g["sinks"] = _swa_bwd(proj, w["sinks"], lse, dyb, dproj)
    dproj = _kv_combine(dkv_self, dkv_prev, dif, dproj)
    g["w_in"] = _mm(h0, dproj, "tn", F32, "mm_d_in")
    gain = w["norm_mix_g"] if last_grad is None else w["norm_mix_g"] + last_grad(g)
    grad_x, _, g["norm_mix_g"] = _norm_bwd_mm(dproj, w["w_in"], x, gain, dx1, "mm_dh0_norm")
    return loss, grad_x, g


_W4 = ("w_branch_a", "w_branch_b", "w_out", "w_ple_gate")
_SHARDED_NAMES = ("w_in", "w_up", "w_down", "w_ple_proj", "conv_qk") + _W4
_SMALL_ROWS = 16
_CONV_ROW = 8


def _group(s):
    return [s["w_in"], jnp.concatenate([s[n] for n in _W4], axis=0), s["w_up"], s["w_down"], s["w_ple_proj"]]


def _ungroup(arrs):
    out = {"w_in": arrs[0], "w_up": arrs[2], "w_down": arrs[3], "w_ple_proj": arrs[4]}
    rows = arrs[1].shape[0] // len(_W4)
    for i, n in enumerate(_W4):
        out[n] = arrs[1][i * rows:(i + 1) * rows]
    return out


def _rows_tile(rows):
    return 256 if rows % 256 == 0 else rows


_SMALL = ("norm_mix_g", "mlstm_norm_g", "norm_mlp_g", "norm_ple_g", "final_norm_g")


def _pack_small(vals, extra=None, conv=None):
    rows = [vals[n].reshape(1, D) for n in _SMALL]
    tail = [vals["b_if"].reshape(1, 8), vals["sinks"].reshape(1, SWH)]
    used = 8 + SWH
    if extra is not None:
        tail.append(extra.reshape(1, 1))
        used += 1
    tail.append(jnp.zeros((1, D - used), F32))
    rows.append(jnp.concatenate(tail, axis=1))
    rows.append(jnp.zeros((_CONV_ROW - len(rows), D), F32))
    rows.append(jnp.zeros((CONV, D), F32) if conv is None else conv)
    rows.append(jnp.zeros((_SMALL_ROWS - _CONV_ROW - CONV, D), F32))
    return jnp.concatenate(rows, axis=0)


def _unpack_small(slab, shapes):
    out = {n: slab[i].reshape(shapes[n]) for i, n in enumerate(_SMALL)}
    out["b_if"] = slab[5, 0:8].reshape(shapes["b_if"])
    out["sinks"] = slab[5, 8:8 + SWH].reshape(shapes["sinks"])
    return out


_MESH = pl.DeviceIdType.MESH
_HBM = pl.BlockSpec(memory_space=pltpu.HBM)
_VMEM = pl.BlockSpec(memory_space=pltpu.VMEM)


def _place():
    x, y, c = lax.axis_index("x"), lax.axis_index("y"), lax.axis_index("c")
    return x, y, c, 2 * x + y


def _chip_peer(x, y, r):
    return (x ^ (r >> 1), y ^ (r & 1))


def _half(ref, which):
    h = ref.shape[-2] // 2
    return pl.ds(which * h, h)


def _allgather_weights(shards, conv):
    n = len(shards)

    def body(*refs):
        ins, conv_ref = refs[:n], refs[n]
        outs, conv_out = refs[n + 1:2 * n + 1], refs[2 * n + 1]
        send_a, recv_a, send_b, recv_b, send_c, recv_c, local_sems = refs[2 * n + 2:]
        x, y, c, j = _place()
        sibling = (x, y, 1 - c)
        local = [pltpu.make_async_copy(ins[k], outs[k].at[j], local_sems.at[k]) for k in range(n)]
        local.append(pltpu.make_async_copy(conv_ref, conv_out.at[j], local_sems.at[n]))
        for cp in local:
            cp.start()

        def copy_a(k, r, chip):
            rows = _half(ins[k], c)
            return pltpu.make_async_remote_copy(
                src_ref=ins[k].at[rows], dst_ref=outs[k].at[chip, rows], send_sem=send_a.at[3 * k + r - 1],
                recv_sem=recv_a.at[3 * k + r - 1], device_id=(*_chip_peer(x, y, r), c), device_id_type=_MESH)

        def copy_b(k, r, chip, which):
            rows = _half(ins[k], which)
            return pltpu.make_async_remote_copy(
                src_ref=outs[k].at[chip, rows], dst_ref=outs[k].at[chip, rows], send_sem=send_b.at[3 * k + r - 1],
                recv_sem=recv_b.at[3 * k + r - 1], device_id=sibling, device_id_type=_MESH)

        def copy_c(r, chip):
            return pltpu.make_async_remote_copy(
                src_ref=conv_ref, dst_ref=conv_out.at[chip], send_sem=send_c.at[r - 1],
                recv_sem=recv_c.at[r - 1], device_id=(*_chip_peer(x, y, r), c), device_id_type=_MESH)

        for k in range(n):
            for r in (1, 2, 3):
                copy_a(k, r, j).start()
        for r in (1, 2, 3):
            copy_c(r, j).start()
        for k in range(n):
            for r in (1, 2, 3):
                copy_a(k, r, j ^ r).wait_recv()
                copy_b(k, r, j ^ r, c).start()
        for k in range(n):
            for r in (1, 2, 3):
                copy_b(k, r, j ^ r, 1 - c).wait_recv()
        for r in (1, 2, 3):
            copy_c(r, j ^ r).wait_recv()
        for k in range(n):
            for r in (1, 2, 3):
                copy_a(k, r, j).wait_send()
                copy_b(k, r, j ^ r, c).wait_send()
        for r in (1, 2, 3):
            copy_c(r, j).wait_send()
        for cp in local:
            cp.wait()

    return pl.pallas_call(
        body, name="allgather_weights",
        out_shape=[jax.ShapeDtypeStruct((4,) + s.shape, s.dtype) for s in shards]
        + [jax.ShapeDtypeStruct((4,) + conv.shape, F32)],
        in_specs=[_HBM] * (n + 1), out_specs=[_HBM] * (n + 1),
        scratch_shapes=[pltpu.SemaphoreType.DMA((3 * n,))] * 4 + [pltpu.SemaphoreType.DMA((3,))] * 2
        + [pltpu.SemaphoreType.DMA((n + 1,))],
    )(*shards, conv)


_SEM = pl.BlockSpec(memory_space=pltpu.SEMAPHORE)
_DATAFLOW = pltpu.SideEffectType.DATAFLOW_SIDE_EFFECTING


def _late_peer_copy(src_ref, land_ref, send_sems, recv_sems, x, y, c, j, r, chip):
    return pltpu.make_async_remote_copy(
        src_ref=src_ref, dst_ref=land_ref.at[chip], send_sem=send_sems.at[r - 1], recv_sem=recv_sems.at[r - 1],
        device_id=(*_chip_peer(x, y, r), c), device_id_type=_MESH)


def _late_gather_start(rest):
    def body(rest_ref, land_ref, send_sems, recv_sems, rest_thru, land_thru, token):
        x, y, c, j = _place()
        for r in (1, 2, 3):
            _late_peer_copy(rest_ref, land_ref, send_sems, recv_sems, x, y, c, j, r, j).start()
        token[...] = jnp.zeros_like(token)

    j = 2 * lax.axis_index("x") + lax.axis_index("y")
    land = lax.dynamic_update_slice(lax.empty((4,) + rest.shape, rest.dtype), rest[None], (j, 0, 0))
    return pl.pallas_call(
        body, name="late_gather_start",
        out_shape=(pltpu.SemaphoreType.DMA((3,)), pltpu.SemaphoreType.DMA((3,)), pltpu.HBM(rest.shape, rest.dtype),
                   pltpu.HBM(land.shape, land.dtype), jax.ShapeDtypeStruct((8, 128), F32)),
        in_specs=(_HBM, _HBM), out_specs=(_SEM, _SEM, _HBM, _HBM, _VMEM), input_output_aliases={0: 2, 1: 3},
        compiler_params=pltpu.CompilerParams(has_side_effects=_DATAFLOW),
    )(pltpu.with_memory_space_constraint(rest, pltpu.HBM), pltpu.with_memory_space_constraint(land, pltpu.HBM))


def _late_gather_wait(send_sems, recv_sems, rest_thru, land_thru, after):
    def body(rest_ref, land_ref, send_sems, recv_sems, after_ref, rest_dead, got_ref):
        x, y, c, j = _place()
        for r in (1, 2, 3):
            cp = _late_peer_copy(rest_ref, land_ref, send_sems, recv_sems, x, y, c, j, r, j ^ r)
            cp.wait_send()
            cp.wait_recv()

    return pl.pallas_call(
        body, name="late_gather_wait",
        out_shape=(pltpu.HBM(rest_thru.shape, rest_thru.dtype), pltpu.HBM(land_thru.shape, land_thru.dtype)),
        in_specs=(_HBM, _HBM, _SEM, _SEM, _ANY), out_specs=(_HBM, _HBM), input_output_aliases={0: 0, 1: 1},
        compiler_params=pltpu.CompilerParams(has_side_effects=_DATAFLOW),
    )(rest_thru, land_thru, send_sems, recv_sems, after)[1]


def _pair_exchange(gs, name):
    n = len(gs)

    def body(*refs):
        ins, outs, send_sems, recv_sems = refs[:n], refs[n:2 * n], refs[2 * n], refs[2 * n + 1]
        x, y, c, _ = _place()
        cps = [pltpu.make_async_remote_copy(
            src_ref=ins[k].at[:, _half(ins[k], 1 - c)], dst_ref=outs[k], send_sem=send_sems.at[k],
            recv_sem=recv_sems.at[k], device_id=(x, y, 1 - c), device_id_type=_MESH) for k in range(n)]
        for cp in cps:
            cp.start()
        for cp in cps:
            cp.wait()

    return pl.pallas_call(
        body, name=name,
        out_shape=[jax.ShapeDtypeStruct((4, g.shape[1] // 2, g.shape[2]), F32) for g in gs],
        in_specs=[_HBM] * n, out_specs=[_HBM] * n, scratch_shapes=[pltpu.SemaphoreType.DMA((n,))] * 2,
    )(*gs)


def _pair_sum(g, theirs, c, name):
    _, h, cols = theirs.shape
    tr = _rows_tile(h)
    nb = h // tr

    def body(c_ref, a_ref, b_ref, o_ref, ob_ref):
        s = a_ref[...] + b_ref[...]
        o_ref[...] = s
        ob_ref[...] = s.astype(BF16)

    blk = pl.BlockSpec((1, tr, cols), lambda k, i, c_ref: (k, i, 0))
    return pl.pallas_call(
        body, name=name,
        grid_spec=pltpu.PrefetchScalarGridSpec(
            num_scalar_prefetch=1, grid=(4, nb),
            in_specs=[pl.BlockSpec((1, tr, cols), lambda k, i, c_ref: (k, c_ref[0] * nb + i, 0)), blk],
            out_specs=[blk, blk]),
        out_shape=[jax.ShapeDtypeStruct(theirs.shape, F32), jax.ShapeDtypeStruct(theirs.shape, BF16)],
        compiler_params=_params(),
    )(c.reshape(1).astype(jnp.int32), g, theirs)


def _chip_copies(srcs, lands, send_sems, recv_sems):
    x, y, c, j = _place()
    return [pltpu.make_async_remote_copy(
        src_ref=srcs[k].at[j ^ r], dst_ref=lands[k].at[r - 1], send_sem=send_sems.at[3 * k + r - 1],
        recv_sem=recv_sems.at[3 * k + r - 1], device_id=(*_chip_peer(x, y, r), c), device_id_type=_MESH)
        for k in range(len(srcs)) for r in (1, 2, 3)]


def _pair_copies(srcs, lands, send_sems, recv_sems):
    x, y, c, _ = _place()
    return [pltpu.make_async_remote_copy(
        src_ref=srcs[k].at[:, _half(srcs[k], 1 - c)], dst_ref=lands[k], send_sem=send_sems.at[k],
        recv_sem=recv_sems.at[k], device_id=(x, y, 1 - c), device_id_type=_MESH) for k in range(len(srcs))]


def _split_start(name, srcs, lands, copies, n_sems):
    n = len(srcs)

    def body(*refs):
        for cp in copies(refs[:n], refs[n:2 * n], refs[2 * n], refs[2 * n + 1]):
            cp.start()
        refs[-1][...] = jnp.zeros_like(refs[-1])

    arrays = list(srcs) + list(lands)
    out = pl.pallas_call(
        body, name=name,
        out_shape=(pltpu.SemaphoreType.DMA((n_sems,)), pltpu.SemaphoreType.DMA((n_sems,)),
                   *[pltpu.HBM(a.shape, a.dtype) for a in arrays], jax.ShapeDtypeStruct((8, 128), F32)),
        in_specs=[_HBM] * (2 * n), out_specs=(_SEM, _SEM, *([_HBM] * (2 * n)), _VMEM),
        input_output_aliases={k: 2 + k for k in range(2 * n)},
        compiler_params=pltpu.CompilerParams(has_side_effects=_DATAFLOW),
    )(*[pltpu.with_memory_space_constraint(a, pltpu.HBM) for a in arrays])
    return out[0], out[1], list(out[2:2 + n]), list(out[2 + n:2 + 2 * n]), out[-1]


def _split_wait(name, send_sems, recv_sems, srcs_thru, lands_thru, after, copies):
    n = len(srcs_thru)

    def body(*refs):
        for cp in copies(refs[:n], refs[n:2 * n], refs[2 * n], refs[2 * n + 1]):
            cp.wait_send()
            cp.wait_recv()

    arrays = list(srcs_thru) + list(lands_thru)
    out = pl.pallas_call(
        body, name=name, out_shape=tuple(pltpu.HBM(a.shape, a.dtype) for a in arrays),
        in_specs=[_HBM] * (2 * n) + [_SEM, _SEM, _ANY], out_specs=tuple([_HBM] * (2 * n)),
        input_output_aliases={k: k for k in range(2 * n)},
        compiler_params=pltpu.CompilerParams(has_side_effects=_DATAFLOW),
    )(*arrays, send_sems, recv_sems, after)
    return list(out[:n]), list(out[n:])


def _chip_exchange_start(ss, tag):
    lands = [lax.empty((3,) + s.shape[1:], s.dtype) for s in ss]
    return _split_start("chip_exchange_start_" + tag, ss, lands, _chip_copies, 3 * len(ss))


def _chip_exchange_wait(send_sems, recv_sems, ss_thru, lands_thru, after, tag):
    return _split_wait("chip_exchange_wait_" + tag, send_sems, recv_sems, ss_thru, lands_thru, after, _chip_copies)[1]


def _pair_exchange_start(gs, tag):
    lands = [lax.empty((4, g.shape[1] // 2, g.shape[2]), g.dtype) for g in gs]
    return _split_start("pair_exchange_start_" + tag, gs, lands, _pair_copies, len(gs))


def _pair_exchange_wait(send_sems, recv_sems, gs_thru, lands_thru, after, tag):
    return _split_wait("pair_exchange_wait_" + tag, send_sems, recv_sems, gs_thru, lands_thru, after, _pair_copies)


def _reduce4(own, others, j, c, name):
    _, h, cols = own.shape
    tr = _rows_tile(h)
    nb = h // tr

    def body(idx_ref, s_ref, a0, a1, a2, o_ref):
        o_ref[...] = ((s_ref[0] + a0[0].astype(F32)) + a1[0].astype(F32)) + a2[0].astype(F32)

    def other(r):
        return pl.BlockSpec((1, tr, cols), lambda i, idx_ref: (r, i, 0))

    return pl.pallas_call(
        body, name=name,
        grid_spec=pltpu.PrefetchScalarGridSpec(
            num_scalar_prefetch=1, grid=(nb,),
            in_specs=[pl.BlockSpec((1, tr, cols), lambda i, idx_ref: (idx_ref[0], i, 0)), other(0), other(1), other(2)],
            out_specs=pl.BlockSpec((tr, cols), lambda i, idx_ref: (idx_ref[1] * nb + i, 0))),
        out_shape=jax.ShapeDtypeStruct((2 * h, cols), F32), compiler_params=_params(),
    )(jnp.stack([j, c]).astype(jnp.int32), own, others, others, others)


def _sibling_share(fulls):
    n = len(fulls)

    def body(*refs):
        outs, send_sems, recv_sems = refs[n:2 * n], refs[2 * n], refs[2 * n + 1]
        x, y, c, _ = _place()
        cps = [pltpu.make_async_remote_copy(
            src_ref=outs[k].at[_half(outs[k], c)], dst_ref=outs[k].at[_half(outs[k], c)], send_sem=send_sems.at[k],
            recv_sem=recv_sems.at[k], device_id=(x, y, 1 - c), device_id_type=_MESH) for k in range(n)]
        for cp in cps:
            cp.start()
        for cp in cps:
            cp.wait()

    return pl.pallas_call(
        body, name="sibling_share", out_shape=[jax.ShapeDtypeStruct(f.shape, F32) for f in fulls],
        in_specs=[_HBM] * n, out_specs=[_HBM] * n, input_output_aliases={k: k for k in range(n)},
        scratch_shapes=[pltpu.SemaphoreType.DMA((n,))] * 2,
    )(*fulls)


def _adamw(w, g, m, v):
    m1 = ADAM_B1 * m + (1.0 - ADAM_B1) * g
    v1 = ADAM_B2 * v + (1.0 - ADAM_B2) * (g * g)
    m_hat = m1 / (1.0 - ADAM_B1 ** ADAM_STEP)
    v_hat = v1 / (1.0 - ADAM_B2 ** ADAM_STEP)
    delta = -ADAM_LR * (m_hat / (jnp.sqrt(v_hat) + ADAM_EPS) + ADAM_WD * w)
    return delta, m1, v1


def _adamw_call(w, g, m, v, name):
    rows, cols = w.shape

    def body(w_ref, g_ref, m_ref, v_ref, d_out, m_out, v_out):
        delta, m1, v1 = _adamw(w_ref[...], g_ref[...], m_ref[...], v_ref[...])
        d_out[...] = delta
        m_out[...] = m1
        v_out[...] = v1

    if rows % 8 == 0:
        tr = _rows_tile(rows)
        blk, grid = pl.BlockSpec((tr, cols), lambda i: (i, 0)), (rows // tr,)
    else:
        blk, grid = pl.BlockSpec((rows, 128), lambda i: (0, i)), (cols // 128,)
    return pl.pallas_call(
        body, name=name, grid=grid, in_specs=[blk] * 4, out_specs=[blk] * 3,
        out_shape=[jax.ShapeDtypeStruct((rows, cols), F32)] * 3, compiler_params=_params(),
    )(w, g, m, v)


def _small_allreduce(vals):
    def body(v_ref, out_ref, buf, send_sems, recv_sems):
        x, y, c, j = _place()
        me = 2 * j + c
        buf[0] = v_ref[...]

        def copy(r):
            return pltpu.make_async_remote_copy(
                src_ref=v_ref, dst_ref=buf.at[r], send_sem=send_sems.at[r - 1], recv_sem=recv_sems.at[r - 1],
                device_id=(x ^ (r >> 2), y ^ ((r >> 1) & 1), c ^ (r & 1)), device_id_type=_MESH)

        for r in range(1, 8):
            copy(r).start()
        for r in range(1, 8):
            copy(r).wait()
        acc = buf[me ^ 0]
        for d in range(1, 8):
            acc = acc + buf[me ^ d]
        out_ref[...] = acc

    return pl.pallas_call(
        body, name="small_allreduce", out_shape=jax.ShapeDtypeStruct((_SMALL_ROWS, D), F32),
        in_specs=[_VMEM], out_specs=_VMEM,
        scratch_shapes=[pltpu.VMEM((8, _SMALL_ROWS, D), F32), pltpu.SemaphoreType.DMA((7,)),
                        pltpu.SemaphoreType.DMA((7,))],
    )(vals)


_NAMES = ("norm_mix_g", "w_in", "conv_qk", "b_if", "mlstm_norm_g", "sinks", "w_branch_a", "w_branch_b", "w_out",
          "norm_mlp_g", "w_up", "w_down", "norm_ple_g", "w_ple_gate", "w_ple_proj", "final_norm_g")
_GROUP_NAMES = ("w_in", "w4", "w_up", "w_down", "w_ple_proj")


def _step(x, p, target, w, m, v):
    c = lax.axis_index("c")
    j = 2 * lax.axis_index("x") + lax.axis_index("y")

    def shards(d):
        return {n: d[n][0] for n in _SHARDED_NAMES}

    ws = shards(w)
    w_in_all, conv_all = _allgather_weights([ws["w_in"].astype(BF16)], ws["conv_qk"])
    rows_pp = PLE * (D // 4) // D
    rest = jnp.concatenate([ws[n] for n in _W4] + [ws["w_up"], ws["w_down"], ws["w_ple_proj"].reshape(rows_pp, D)],
                           axis=0)
    rest = (rest + 0.0 * conv_all[0, 0, 0]).astype(BF16)
    send_sems, recv_sems, rest_thru, land_thru, token = _late_gather_start(rest)
    full = {n: w[n] for n in ("mlstm_norm_g", "norm_mlp_g", "norm_ple_g", "b_if", "sinks")}
    full["norm_mix_g"] = w["norm_mix_g"] + token[0, 0]
    full["final_norm_g"] = w["final_norm_g"].reshape(1, D)
    full["w_in"] = _win_pad(w_in_all)
    full["conv_qk"] = jnp.swapaxes(conv_all, 0, 1).reshape(CONV, D)

    def late_weights(after):
        land = _late_gather_wait(send_sems, recv_sems, rest_thru, land_thru, after)
        out = {n: land[:, i * (D // 4):(i + 1) * (D // 4)].reshape(D, D) for i, n in enumerate(_W4)}
        out["w_up"] = land[:, D:2 * D]
        out["w_down"] = land[:, 2 * D:3 * D].reshape(DFF, D)
        out["w_ple_proj"] = jnp.swapaxes(land[:, 3 * D:3 * D + rows_pp].reshape(4, PLE, D // 4), 0, 1).reshape(PLE, D)
        return out

    def pair_sums(by_dest, names, tag):
        theirs = _pair_exchange(by_dest, "pair_exchange_" + tag)
        return [_pair_sum(a, b, c, "pair_sum_" + n) for a, b, n in zip(by_dest, theirs, names)]

    early, last = {}, {}

    def early_grads(g):
        by_dest = [jnp.stack([g[n].reshape(4, D // 4, D) for n in _W4], axis=1).reshape(4, D, D),
                   g["w_up"], g["w_down"].reshape(4, DFF // 4, D), g["w_ple_proj"]]
        *early["pair"], token = _pair_exchange_start(by_dest, "early")
        return token[0, 0]

    def mid_grads(after):
        by_dest, theirs = _pair_exchange_wait(*early["pair"], after, "early")
        early["sums"] = [_pair_sum(a, b, c, "pair_sum_" + n) for a, b, n in zip(by_dest, theirs, _GROUP_NAMES[1:])]
        *early["flight"], token = _chip_exchange_start([s[1] for s in early["sums"]], "early")
        return token[0, 0]

    def last_grad(g):
        last["sums"] = pair_sums([_win_unpad(g["w_in"])], _GROUP_NAMES[:1], "w_in")
        *last["flight"], token = _chip_exchange_start([s[1] for s in last["sums"]], "w_in")
        return token[0, 0]

    loss, grad_x, g = _local_step(x[0], p[0, 0], target[0], full, late_weights, early_grads, mid_grads, last_grad)

    others = _chip_exchange_wait(*last["flight"], grad_x, "w_in")
    others += _chip_exchange_wait(*early["flight"], others[0], "early")
    sums = last["sums"] + early["sums"]
    halves = [_reduce4(s[0], b, j, c, "reduce4_" + n) for s, b, n in zip(sums, others, _GROUP_NAMES)]
    grads = _sibling_share(halves)

    small_g = _small_allreduce(_pack_small(g, extra=loss, conv=g["conv_qk"]))
    conv_g = lax.dynamic_slice(small_g[_CONV_ROW:_CONV_ROW + CONV], (0, j * (D // 4)), (CONV, D // 4))

    ms, vs = shards(m), shards(v)
    upd = [_adamw_call(wa, ga, ma, va, "adamw_" + n)
           for wa, ga, ma, va, n in list(zip(_group(ws), grads, _group(ms), _group(vs), _GROUP_NAMES))[1:]]
    upd_in = _adamw_call(*[jnp.swapaxes(a, 0, 1) for a in (ws["w_in"], grads[0], ms["w_in"], vs["w_in"])], "adamw_w_in")
    upd = [[jnp.swapaxes(a, 0, 1) for a in upd_in]] + upd
    conv_upd = _adamw_call(ws["conv_qk"], conv_g, ms["conv_qk"], vs["conv_qk"], "adamw_conv")
    small_upd = _adamw_call(_pack_small(w), small_g, _pack_small(m), _pack_small(v), "adamw_small")

    shapes = {n: w[n].shape for n in _NAMES}
    res = []
    for k in range(4):
        big = _ungroup(list(grads) if k == 0 else [u[k - 1] for u in upd])
        big["conv_qk"] = conv_g if k == 0 else conv_upd[k - 1]
        leaves = _unpack_small(small_g if k == 0 else small_upd[k - 1], shapes)
        leaves.update({n: a.reshape(shapes[n]) for n, a in big.items()})
        res.append(leaves)

    out = [small_g[5, 8 + SWH], grad_x[None]]
    for k in range(4):
        out += [res[k][n] for n in _NAMES]
    return tuple(out)


def kernel(x, p, norm_mix_g, w_in, conv_qk, b_if, mlstm_norm_g, sinks, w_branch_a, w_branch_b, w_out, norm_mlp_g, w_up, w_down, norm_ple_g, w_ple_gate, w_ple_proj, final_norm_g, loss_target, m_norm_mix_g, m_w_in, m_conv_qk, m_b_if, m_mlstm_norm_g, m_sinks, m_w_branch_a, m_w_branch_b, m_w_out, m_norm_mlp_g, m_w_up, m_w_down, m_norm_ple_g, m_w_ple_gate, m_w_ple_proj, m_final_norm_g, v_norm_mix_g, v_w_in, v_conv_qk, v_b_if, v_mlstm_norm_g, v_sinks, v_w_branch_a, v_w_branch_b, v_w_out, v_norm_mlp_g, v_w_up, v_w_down, v_norm_ple_g, v_w_ple_gate, v_w_ple_proj, v_final_norm_g):
    w = dict(zip(_NAMES, (norm_mix_g, w_in, conv_qk, b_if, mlstm_norm_g, sinks, w_branch_a, w_branch_b, w_out,
                          norm_mlp_g, w_up, w_down, norm_ple_g, w_ple_gate, w_ple_proj, final_norm_g)))
    m = dict(zip(_NAMES, (m_norm_mix_g, m_w_in, m_conv_qk, m_b_if, m_mlstm_norm_g, m_sinks, m_w_branch_a,
                          m_w_branch_b, m_w_out, m_norm_mlp_g, m_w_up, m_w_down, m_norm_ple_g, m_w_ple_gate,
                          m_w_ple_proj, m_final_norm_g)))
    v = dict(zip(_NAMES, (v_norm_mix_g, v_w_in, v_conv_qk, v_b_if, v_mlstm_norm_g, v_sinks, v_w_branch_a,
                          v_w_branch_b, v_w_out, v_norm_mlp_g, v_w_up, v_w_down, v_norm_ple_g, v_w_ple_gate,
                          v_w_ple_proj, v_final_norm_g)))
    return _step(x, p, loss_target, w, m, v)
```

```python
import jax
import jax.numpy as jnp
from jax import lax
from jax.experimental import pallas as pl
from jax.experimental.pallas import tpu as pltpu

F32 = jnp.float32
BF16 = jnp.bfloat16

D = 1024
PLE = 256
MLH = 4
DQK = 128
DV = 256
CONV = 4
CHUNK = 128
SWH = 16
SWKV = 4
SWG = SWH // SWKV
HD = 64
WIN = 128
DFF = 4096
EPS = 1e-6
N_IN = 6664
NP = 7168
C_QK, C_V, C_O, C_QSW, C_GA, C_GB, C_KV, C_IF = 0, 1024, 2048, 3072, 4096, 5120, 6144, 6656
IFW = NP - C_IF

ADAM_LR = 0.001
ADAM_B1 = 0.9
ADAM_B2 = 0.999
ADAM_EPS = 1e-08
ADAM_WD = 0.01
ADAM_STEP = 10

TOK_TILE = 512
VMEM_LIMIT = 58 * 1024 * 1024


def _params(**kw):
    return pltpu.CompilerParams(vmem_limit_bytes=VMEM_LIMIT, **kw)


def _pick(n, cap):
    if n <= cap:
        return n
    t = cap - cap % 128
    while t > 128 and n % t:
        t -= 128
    assert n % t == 0, (n, cap)
    return t


def _dot(a, b, dims):
    return lax.dot_general(a, b, (dims, ((), ())), preferred_element_type=F32)


def _dot_nn(a, b):
    return _dot(a, b, ((1,), (0,)))


def _dot_nt(a, b):
    return _dot(a, b, ((1,), (1,)))


def _dot_tn(a, b):
    return _dot(a, b, ((0,), (0,)))


def _sigmoid(x):
    return 1.0 / (1.0 + jnp.exp(-x))


def _mm(a, b, mode, out_dtype, name, out_chunks=1):
    bch = b.shape[0] if b.ndim == 3 else 1
    brows, bcols = b.shape[-2], b.shape[-1] * bch
    if mode == "nn":
        (m, k), (k2, n) = a.shape, (brows, bcols)
    elif mode == "nt":
        (m, k), (n, k2) = a.shape, (brows, bcols)
    else:
        (k, m), (k2, n) = a.shape, (brows, bcols)
    assert k == k2, (a.shape, b.shape, mode)
    n_cap = n // max(out_chunks, 1 if mode == "nt" else bch)
    k_cap = k // bch if mode == "nt" else k
    tm, tn, tk = _pick(m, 1024), _pick(n_cap, 1024), _pick(k_cap, 2048)
    nk = k // tk
    if mode == "nn":
        a_spec = pl.BlockSpec((tm, tk), lambda i, j, kk: (i, kk))
        if bch > 1:
            bpc = (n // bch) // tn
            b_spec = pl.BlockSpec((None, tk, tn), lambda i, j, kk: (j // bpc, kk, j % bpc))
        else:
            b_spec = pl.BlockSpec((tk, tn), lambda i, j, kk: (kk, j))
        dot = _dot_nn
    elif mode == "nt":
        a_spec = pl.BlockSpec((tm, tk), lambda i, j, kk: (i, kk))
        if bch > 1:
            bpc = (k // bch) // tk
            b_spec = pl.BlockSpec((None, tn, tk), lambda i, j, kk: (kk // bpc, j, kk % bpc))
        else:
            b_spec = pl.BlockSpec((tn, tk), lambda i, j, kk: (j, kk))
        dot = _dot_nt
    else:
        assert bch == 1
        a_spec = pl.BlockSpec((tk, tm), lambda i, j, kk: (kk, i))
        b_spec = pl.BlockSpec((tk, tn), lambda i, j, kk: (kk, j))
        dot = _dot_tn
    if out_chunks > 1:
        npc = (n // out_chunks) // tn
        out_spec = pl.BlockSpec((None, tm, tn), lambda i, j, kk: (j // npc, i, j % npc))
        out_shape = jax.ShapeDtypeStruct((out_chunks, m, n // out_chunks), out_dtype)
    else:
        out_spec = pl.BlockSpec((tm, tn), lambda i, j, kk: (i, j))
        out_shape = jax.ShapeDtypeStruct((m, n), out_dtype)

    def body(a_ref, b_ref, o_ref, acc_ref):
        kk = pl.program_id(2)

        @pl.when(kk == 0)
        def _():
            acc_ref[...] = jnp.zeros_like(acc_ref)

        acc_ref[...] += dot(a_ref[...], b_ref[...])

        @pl.when(kk == nk - 1)
        def _():
            o_ref[...] = acc_ref[...].astype(out_dtype)

    return pl.pallas_call(
        body, name=name, grid=(m // tm, n // tn, nk),
        in_specs=[a_spec, b_spec], out_specs=out_spec, out_shape=out_shape,
        scratch_shapes=[pltpu.VMEM((tm, tn), F32)],
        compiler_params=_params(dimension_semantics=("parallel", "parallel", "arbitrary")),
    )(a, b)


def _tile(col0=0):
    return lambda tm, tn: pl.BlockSpec((tm, tn), lambda i, j, kk: (i, col0 // tn + j))


def _row():
    return lambda tm, tn: pl.BlockSpec((1, tn), lambda i, j, kk: (0, j))


def _mm_ep(pairs, mode, name, epilogue, ins, outs, tm, tn, aliases=None):
    a0, b0 = pairs[0]
    bch = b0.shape[0] if b0.ndim == 3 else 1
    m, k = a0.shape
    tm = _pick(m, tm)
    n = b0.shape[-1] * bch if mode == "nn" else b0.shape[-2]
    tk = _pick(k // bch if mode == "nt" else k, 2048)
    nk = k // tk
    a_spec = pl.BlockSpec((tm, tk), lambda i, j, kk: (i, kk))
    if mode == "nn":
        dot = _dot_nn
        if bch > 1:
            bpc = (n // bch) // tn
            b_spec = pl.BlockSpec((None, tk, tn), lambda i, j, kk: (j // bpc, kk, j % bpc))
        else:
            b_spec = pl.BlockSpec((tk, tn), lambda i, j, kk: (kk, j))
    else:
        dot = _dot_nt
        if bch > 1:
            bpc = (k // bch) // tk
            b_spec = pl.BlockSpec((None, tn, tk), lambda i, j, kk: (kk // bpc, j, kk % bpc))
        else:
            b_spec = pl.BlockSpec((tn, tk), lambda i, j, kk: (j, kk))
    npair, nin, nout = len(pairs), len(ins), len(outs)

    def body(*refs):
        ab = refs[:2 * npair]
        in_refs = refs[2 * npair:2 * npair + nin]
        out_refs = refs[2 * npair + nin:2 * npair + nin + nout]
        accs = refs[2 * npair + nin + nout:]
        i, j, kk = pl.program_id(0), pl.program_id(1), pl.program_id(2)
        for p in range(npair):
            prod = dot(ab[2 * p][...], ab[2 * p + 1][...])

            @pl.when(kk == 0)
            def _():
                accs[p][...] = prod

            @pl.when(kk > 0)
            def _():
                accs[p][...] += prod

        @pl.when(kk == nk - 1)
        def _():
            epilogue([acc[...] for acc in accs], in_refs, out_refs, i, j)

    operands = [x for pair in pairs for x in pair] + [a for a, _ in ins]
    io_alias = {2 * npair + i: o for i, o in (aliases or {}).items()}
    return pl.pallas_call(
        body, name=name, grid=(m // tm, n // tn, nk),
        in_specs=[a_spec, b_spec] * npair + [mk(tm, tn) for _, mk in ins],
        out_specs=[mk(tm, tn) for _, mk in outs], out_shape=[s for s, _ in outs],
        scratch_shapes=[pltpu.VMEM((tm, tn), F32)] * npair, input_output_aliases=io_alias,
        compiler_params=_params(dimension_semantics=("arbitrary", "arbitrary", "arbitrary")),
    )(*operands)


def _tok(w, j=0):
    return pl.BlockSpec((TOK_TILE, w), lambda i: (i, j))


def _rep(shape):
    return pl.BlockSpec(shape, lambda i: (0,) * len(shape))


def _rms(x):
    rstd = lax.rsqrt(jnp.mean(x * x, axis=-1, keepdims=True) + EPS)
    return x * rstd, rstd


def _rms_bwd(xn, rstd, dxn):
    return rstd * (dxn - xn * jnp.mean(dxn * xn, axis=-1, keepdims=True))


def _norm_fwd(x, g, name):
    t = x.shape[0]

    def body(x_ref, g_ref, h_ref):
        xn, _ = _rms(x_ref[...])
        h_ref[...] = (xn * g_ref[...]).astype(BF16)

    return pl.pallas_call(
        body, name=name, grid=(t // TOK_TILE,), in_specs=[_tok(D), _rep((1, D))], out_specs=_tok(D),
        out_shape=jax.ShapeDtypeStruct((t, D), BF16), compiler_params=_params(),
    )(x, g)


def _halo_prev(w, j=0, rows=8):
    r = TOK_TILE // rows
    return pl.BlockSpec((rows, w), lambda i: (jnp.maximum(i * r - 1, 0), j))


def _last8(halo_ref):
    return halo_ref[...].astype(F32)[halo_ref.shape[0] - 8:]


def _halo_next(w, nt, j=0):
    r = TOK_TILE // 8
    return pl.BlockSpec((8, w), lambda i: (jnp.minimum((i + 1) * r, nt * r - 1), j))


def _shift_down(x, halo, s):
    if s == 0:
        return x
    r = pltpu.roll(x, s, 0)
    hs = pltpu.roll(halo, s, 0)
    row = lax.broadcasted_iota(jnp.int32, hs.shape, 0)
    top = jnp.where(row < s, hs, r[0:8])
    return jnp.concatenate([top, r[8:]], axis=0)


def _shift_up(x, halo, s):
    if s == 0:
        return x
    n = x.shape[0]
    r = pltpu.roll(x, n - s, 0)
    hs = pltpu.roll(halo, 8 - s, 0)
    row = lax.broadcasted_iota(jnp.int32, hs.shape, 0)
    bot = jnp.where(row >= 8 - s, hs, r[n - 8:])
    return jnp.concatenate([r[:n - 8], bot], axis=0)


def _bf(x):
    return x.astype(BF16).astype(F32)


def _conv_taps(x, halo, w):
    x, halo, w = _bf(x), _bf(halo), _bf(w)
    acc = x * w[CONV - 1:CONV, :]
    for j in range(CONV - 1):
        acc = acc + _shift_down(x, halo, CONV - 1 - j) * w[j:j + 1, :]
    return acc


_Q_SCALE = DQK ** -0.5


def _qscale_row():
    lane = lax.broadcasted_iota(jnp.int32, (1, D), 1)
    return jnp.where(lane < MLH * DQK, _Q_SCALE, 1.0).astype(F32)


def _conv_silu_fwd(proj, conv_w):
    t = proj.shape[0]

    def body(x_ref, halo_ref, w_ref, o_ref):
        halo = jnp.where(pl.program_id(0) > 0, _last8(halo_ref), 0.0)
        c = _conv_taps(x_ref[...].astype(F32), halo, w_ref[...])
        o_ref[...] = (c * _sigmoid(c) * _qscale_row()).astype(BF16)

    return pl.pallas_call(
        body, name="conv_silu_fwd", grid=(t // TOK_TILE,),
        in_specs=[_tok(D, C_QK // D), _halo_prev(D, C_QK // D, 16), _rep((CONV, D))], out_specs=_tok(D),
        out_shape=jax.ShapeDtypeStruct((t, D), BF16), compiler_params=_params(),
    )(proj, proj, conv_w)


def _conv_silu_bwd_a(proj, conv_w, dqk):
    t = proj.shape[0]

    def body(x_ref, halo_ref, w_ref, d_ref, dc_ref, dw_ref):
        @pl.when(pl.program_id(0) == 0)
        def _():
            dw_ref[...] = jnp.zeros_like(dw_ref)

        halo = jnp.where(pl.program_id(0) > 0, _last8(halo_ref), 0.0)
        x = x_ref[...].astype(F32)
        c = _conv_taps(x, halo, w_ref[...])
        s = _sigmoid(c)
        dc = d_ref[...] * _qscale_row() * (s * (1.0 + c * (1.0 - s)))
        dc_ref[...] = dc
        dcb, xb, halo_b = _bf(dc), _bf(x), _bf(halo)
        for j in range(CONV):
            dw_ref[j:j + 1, :] += jnp.sum(dcb * _shift_down(xb, halo_b, CONV - 1 - j), axis=0, keepdims=True)

    return pl.pallas_call(
        body, name="conv_silu_bwd_a", grid=(t // TOK_TILE,),
        in_specs=[_tok(D, C_QK // D), _halo_prev(D, C_QK // D, 16), _rep((CONV, D)), _tok(D)],
        out_specs=[_tok(D), _rep((CONV, D))],
        out_shape=[jax.ShapeDtypeStruct((t, D), F32), jax.ShapeDtypeStruct((CONV, D), F32)],
        compiler_params=_params(),
    )(proj, proj, conv_w, dqk)


def _conv_silu_bwd_b(dc, conv_w, dproj):
    t = dc.shape[0]
    nt = t // TOK_TILE

    def body(dc_ref, halo_ref, w_ref, _, dx_ref):
        halo = _bf(jnp.where(pl.program_id(0) < nt - 1, halo_ref[...], 0.0))
        dcv = _bf(dc_ref[...])
        w = _bf(w_ref[...])
        acc = dcv * w[CONV - 1:CONV, :]
        for j in range(CONV - 1):
            acc = acc + _shift_up(dcv, halo, CONV - 1 - j) * w[j:j + 1, :]
        dx_ref[...] = acc.astype(BF16)

    return pl.pallas_call(
        body, name="conv_silu_bwd_b", grid=(nt,), in_specs=[_tok(D), _halo_next(D, nt), _rep((CONV, D)), _ANY],
        out_specs=_tok(D, C_QK // D), out_shape=jax.ShapeDtypeStruct((t, NP), BF16),
        input_output_aliases={3: 0}, compiler_params=_params(),
    )(dc, dc, conv_w, dproj)


def _gates_fwd(pre_rows, bias_col):
    t = pre_rows.shape[1]

    def body(p_ref, b_ref, g_ref, s_ref):
        z = p_ref[...] + b_ref[...]
        lf = jnp.minimum(z, 0.0) - jnp.log(1.0 + jnp.exp(-jnp.abs(z)))
        lane = lax.broadcasted_iota(jnp.int32, z.shape, 1) % CHUNK
        cum = lf
        s = 1
        while s < CHUNK:
            cum = cum + jnp.where(lane >= s, pltpu.roll(cum, s, 1), 0.0)
            s *= 2
        sub = lax.broadcasted_iota(jnp.int32, z.shape, 0)
        g_ref[...] = jnp.where(sub < MLH, z, cum)
        s_ref[...] = _sigmoid(-z)

    return pl.pallas_call(
        body, name="gates_fwd",
        out_shape=[jax.ShapeDtypeStruct((8, t), F32), jax.ShapeDtypeStruct((8, t), F32)],
        compiler_params=_params(),
    )(pre_rows, bias_col)


def _chunk_terms(grow, gcol, m0):
    heads = range(MLH)
    i_row = [grow[h:h + 1, :] for h in heads]
    b_row = [grow[MLH + h:MLH + h + 1, :] for h in heads]
    i_col = [gcol[:, h:h + 1] for h in heads]
    b_col = [gcol[:, MLH + h:MLH + h + 1] for h in heads]
    b_last = [b_row[h][:, CHUNK - 1:CHUNK] for h in heads]
    tt = lax.broadcasted_iota(jnp.int32, (CHUNK, CHUNK), 0)
    ss = lax.broadcasted_iota(jnp.int32, (CHUNK, CHUNK), 1)
    log_d = [jnp.where(tt >= ss, b_col[h] - b_row[h] + i_row[h], -jnp.inf) for h in heads]
    row_max = [jnp.max(log_d[h], axis=1, keepdims=True) for h in heads]
    last_max = [jnp.max(b_last[h] - b_row[h] + i_row[h], axis=1, keepdims=True) for h in heads]
    m_t = [jnp.maximum(b_col[h] + m0[h], row_max[h]) for h in heads]
    m1 = [jnp.maximum(b_last[h] + m0[h], last_max[h]) for h in heads]
    dm = [jnp.exp(log_d[h] - m_t[h]) for h in heads]
    wi = [jnp.exp(b_col[h] + m0[h] - m_t[h]) for h in heads]
    ws = [jnp.exp(b_last[h] - b_col[h] + i_col[h] - m1[h]) for h in heads]
    dec = [jnp.exp(b_last[h] + m0[h] - m1[h]) for h in heads]
    return [(dm[h], wi[h], m_t[h], ws[h], dec[h], m1[h]) for h in heads]


def _mlstm_fwd(qk, proj, grow, gcol, gain):
    t = qk.shape[0]
    nc = t // CHUNK

    def body(qk_ref, v_ref, o_ref, grow_ref, gcol_ref, g_ref, h_ref, y_ref, cs_ref, st_ref, c_scr, st_scr):
        @pl.when(pl.program_id(0) == 0)
        def _():
            c_scr[...] = jnp.zeros_like(c_scr)
            st_scr[...] = jnp.zeros_like(st_scr)

        grow_v, gcol_v = grow_ref[...], gcol_ref[...]
        heads = range(MLH)
        q = [qk_ref[:, h * DQK:(h + 1) * DQK] for h in heads]
        k = [qk_ref[:, MLH * DQK + h * DQK:MLH * DQK + (h + 1) * DQK] for h in heads]
        v = [v_ref[:, h * DV:(h + 1) * DV] for h in heads]
        c0 = [c_scr[h] for h in heads]
        n0 = [st_scr[h, 0:1, :] for h in heads]
        for h in heads:
            cs_ref[0, h] = c0[h]
            st_ref[0, h] = st_scr[h]
        terms = _chunk_terms(grow_v, gcol_v, [st_scr[h, 1:2, 0:1] for h in heads])
        a = [_dot_nt(q[h], k[h]) for h in heads]
        qc = [_dot_nt(q[h], c0[h].astype(BF16)) for h in heads]
        s = [a[h] * terms[h][0] for h in heads]
        sv = [_dot_nn(s[h].astype(BF16), v[h]) for h in heads]
        upd = [_dot_tn((terms[h][3] * v[h]).astype(BF16), k[h]) for h in heads]
        den = [terms[h][1] * jnp.sum(q[h].astype(F32) * n0[h], axis=1, keepdims=True)
               + jnp.sum(s[h], axis=1, keepdims=True) for h in heads]
        hv = [(terms[h][1] * qc[h] + sv[h]) / jnp.maximum(jnp.abs(den[h]), jnp.exp(-terms[h][2])) for h in heads]
        for h in heads:
            sl = slice(h * DV, (h + 1) * DV)
            h_ref[:, sl] = hv[h]
            xn, _ = _rms(hv[h])
            y_ref[:, sl] = (_sigmoid(o_ref[:, sl].astype(F32)) * xn * g_ref[:, sl]).astype(BF16)
        for h in heads:
            dec, m1 = terms[h][4], terms[h][5]
            c_scr[h] = dec * c0[h] + upd[h]
            st_scr[h, 0:1, :] = dec * n0[h] + jnp.sum(terms[h][3] * k[h].astype(F32), axis=0, keepdims=True)
            st_scr[h, 1:2, :] = jnp.broadcast_to(m1, (1, DQK))

    return pl.pallas_call(
        body, name="mlstm_fwd", grid=(nc,),
        in_specs=[pl.BlockSpec((CHUNK, D), lambda c: (c, 0)), pl.BlockSpec((CHUNK, D), lambda c: (c, C_V // D)),
                  pl.BlockSpec((CHUNK, D), lambda c: (c, C_O // D)),
                  pl.BlockSpec((8, CHUNK), lambda c: (0, c)), pl.BlockSpec((CHUNK, 8), lambda c: (c, 0)),
                  pl.BlockSpec((1, D), lambda c: (0, 0))],
        out_specs=[pl.BlockSpec((CHUNK, D), lambda c: (c, 0)), pl.BlockSpec((CHUNK, D), lambda c: (c, 0)),
                   pl.BlockSpec((1, MLH, DV, DQK), lambda c: (c, 0, 0, 0)),
                   pl.BlockSpec((1, MLH, 8, DQK), lambda c: (c, 0, 0, 0))],
        out_shape=[jax.ShapeDtypeStruct((t, D), F32), jax.ShapeDtypeStruct((t, D), BF16),
                   jax.ShapeDtypeStruct((nc, MLH, DV, DQK), F32), jax.ShapeDtypeStruct((nc, MLH, 8, DQK), F32)],
        scratch_shapes=[pltpu.VMEM((MLH, DV, DQK), F32), pltpu.VMEM((MLH, 8, DQK), F32)],
        compiler_params=_params(dimension_semantics=("arbitrary",)),
    )(qk, proj, proj, grow, gcol, gain)


def _mlstm_bwd(qk, proj, grow, gcol, sneg_col, cs, st, hraw, dh, dproj):
    t = qk.shape[0]
    nc = t // CHUNK

    def rev(c):
        return nc - 1 - c

    def nxt(c):
        return jnp.minimum(nc - c, nc - 1)

    def body(qk_ref, v_ref, grow_ref, gcol_ref, sneg_ref, cs_ref, st_ref, cs1_ref, st1_ref, h_ref, dh_ref, _,
             dqk_ref, dv_ref, dif_ref, dbif_ref, dc_scr, dn_scr):
        @pl.when(pl.program_id(0) == 0)
        def _():
            dc_scr[...] = jnp.zeros_like(dc_scr)
            dn_scr[...] = jnp.zeros_like(dn_scr)
            dbif_ref[...] = jnp.zeros_like(dbif_ref)

        grow_v, gcol_v, sneg = grow_ref[...], gcol_ref[...], sneg_ref[...]
        tt = lax.broadcasted_iota(jnp.int32, (CHUNK, CHUNK), 0)
        ss = lax.broadcasted_iota(jnp.int32, (CHUNK, CHUNK), 1)
        lane8 = lax.broadcasted_iota(jnp.int32, (CHUNK, 8), 1)
        heads = range(MLH)
        q = [qk_ref[:, h * DQK:(h + 1) * DQK] for h in heads]
        k = [qk_ref[:, MLH * DQK + h * DQK:MLH * DQK + (h + 1) * DQK] for h in heads]
        qf, kf = [a.astype(F32) for a in q], [a.astype(F32) for a in k]
        vb = [v_ref[:, h * DV:(h + 1) * DV].astype(BF16) for h in heads]
        c0 = [cs_ref[0, h] for h in heads]
        n0 = [st_ref[0, h, 0:1, :] for h in heads]
        dc1 = [dc_scr[h] for h in heads]
        dn1 = [dn_scr[h, 0:1, :] for h in heads]
        terms = _chunk_terms(grow_v, gcol_v, [st_ref[0, h, 1:2, 0:1] for h in heads])
        dm, wi, ws = [t[0] for t in terms], [t[1] for t in terms], [t[3] for t in terms]
        s = [_dot_nt(q[h], k[h]) * dm[h] for h in heads]
        den = [wi[h] * jnp.sum(qf[h] * n0[h], axis=1, keepdims=True) + jnp.sum(s[h], axis=1, keepdims=True)
               for h in heads]
        floor = [jnp.exp(-terms[h][2]) for h in heads]
        g = [jnp.maximum(jnp.abs(den[h]), floor[h]) for h in heads]
        dh_v = [dh_ref[:, h * DV:(h + 1) * DV] for h in heads]
        dnum = [dh_v[h] / g[h] for h in heads]
        dden = [-jnp.sum(dh_v[h] * h_ref[:, h * DV:(h + 1) * DV], axis=1, keepdims=True) / g[h] for h in heads]
        dden = [jnp.where(jnp.abs(den[h]) > floor[h], dden[h] * jnp.sign(den[h]), 0.0) for h in heads]
        dnum_b = [a.astype(BF16) for a in dnum]
        dc1_b = [a.astype(BF16) for a in dc1]
        da = [((_dot_nt(dnum_b[h], vb[h]) + dden[h]) * dm[h]).astype(BF16) for h in heads]
        dq_inter = [_dot_nn(dnum_b[h], c0[h].astype(BF16)) for h in heads]
        dk_inter = [_dot_nn(vb[h], dc1_b[h]) for h in heads]
        dv_inter = [_dot_nt(k[h], dc1_b[h]) for h in heads]
        dc_new = [_dot_tn((wi[h] * dnum[h]).astype(BF16), q[h]) for h in heads]
        dq = [_dot_nn(da[h], k[h]) + wi[h] * (dq_inter[h] + dden[h] * n0[h]) for h in heads]
        dk = [_dot_tn(da[h], q[h]) + ws[h] * (dk_inter[h] + dn1[h]) for h in heads]
        dv = [_dot_tn(s[h].astype(BF16), dnum_b[h]) + ws[h] * dv_inter[h] for h in heads]
        for h in heads:
            dqk_ref[:, h * DQK:(h + 1) * DQK] = dq[h]
            dqk_ref[:, MLH * DQK + h * DQK:MLH * DQK + (h + 1) * DQK] = dk[h]
            dv_ref[:, h * DV:(h + 1) * DV] = dv[h].astype(BF16)
        rk = [jnp.sum(kf[h] * dk[h], axis=1, keepdims=True) for h in heads]
        df = [jnp.sum(qf[h] * dq[h], axis=1, keepdims=True) - rk[h] for h in heads]
        df_row = [jnp.sum(jnp.where(tt == ss, df[h], 0.0), axis=0, keepdims=True) for h in heads]
        suffix = [jnp.sum(jnp.where(ss >= tt, df_row[h], 0.0), axis=1, keepdims=True) for h in heads]
        cross = [jnp.sum(jnp.sum(dc1[h] * cs1_ref[0, h], axis=0, keepdims=True), axis=1, keepdims=True)
                 + jnp.sum(dn1[h] * st1_ref[0, h, 0:1, :], axis=1, keepdims=True) for h in heads]
        dif = jnp.zeros((CHUNK, 8), F32)
        for h in heads:
            dpf = (suffix[h] + cross[h]) * sneg[:, MLH + h:MLH + h + 1]
            dif = dif + jnp.where(lane8 == h, rk[h], 0.0) + jnp.where(lane8 == MLH + h, dpf, 0.0)
            dc_scr[h] = terms[h][4] * dc1[h] + dc_new[h]
            dn_scr[h, 0:1, :] = terms[h][4] * dn1[h] + jnp.sum(wi[h] * dden[h] * qf[h], axis=0, keepdims=True)
        dif_ref[...] = dif
        dbif_ref[...] += jnp.sum(dif, axis=0, keepdims=True)

    return pl.pallas_call(
        body, name="mlstm_bwd", grid=(nc,),
        in_specs=[pl.BlockSpec((CHUNK, D), lambda c: (rev(c), 0)),
                  pl.BlockSpec((CHUNK, D), lambda c: (rev(c), C_V // D)),
                  pl.BlockSpec((8, CHUNK), lambda c: (0, rev(c))),
                  pl.BlockSpec((CHUNK, 8), lambda c: (rev(c), 0)),
                  pl.BlockSpec((CHUNK, 8), lambda c: (rev(c), 0)),
                  pl.BlockSpec((1, MLH, DV, DQK), lambda c: (rev(c), 0, 0, 0)),
                  pl.BlockSpec((1, MLH, 8, DQK), lambda c: (rev(c), 0, 0, 0)),
                  pl.BlockSpec((1, MLH, DV, DQK), lambda c: (nxt(c), 0, 0, 0)),
                  pl.BlockSpec((1, MLH, 8, DQK), lambda c: (nxt(c), 0, 0, 0)),
                  pl.BlockSpec((CHUNK, D), lambda c: (rev(c), 0)),
                  pl.BlockSpec((CHUNK, D), lambda c: (rev(c), 0)), _ANY],
        out_specs=[pl.BlockSpec((CHUNK, D), lambda c: (rev(c), 0)),
                   pl.BlockSpec((CHUNK, D), lambda c: (rev(c), C_V // D)),
                   pl.BlockSpec((CHUNK, 8), lambda c: (rev(c), 0)),
                   pl.BlockSpec((1, 8), lambda c: (0, 0))],
        out_shape=[jax.ShapeDtypeStruct((t, D), F32), jax.ShapeDtypeStruct((t, NP), BF16),
                   jax.ShapeDtypeStruct((t, 8), F32), jax.ShapeDtypeStruct((1, 8), F32)],
        scratch_shapes=[pltpu.VMEM((MLH, DV, DQK), F32), pltpu.VMEM((MLH, 8, DQK), F32)],
        input_output_aliases={11: 1}, compiler_params=_params(dimension_semantics=("arbitrary",)),
    )(qk, proj, grow, gcol, sneg_col, cs, st, cs, st, hraw, dh, dproj)


_ANY = pl.BlockSpec(memory_space=pl.ANY)


_SW_SCALE = HD ** -0.5
_KVB = C_KV // (2 * SWKV * HD)


def _swa_mask(n):
    ki = lax.broadcasted_iota(jnp.int32, (2 * WIN, SWG * WIN), 0)
    qi = lax.broadcasted_iota(jnp.int32, (2 * WIN, SWG * WIN), 1) % WIN
    return (ki > qi) & (ki <= qi + WIN) & ((n > 0) | (ki >= WIN))


def _group_rows(x_ref, hk):
    return jnp.concatenate([x_ref[:, (hk * SWG + g) * HD:(hk * SWG + g + 1) * HD] for g in range(SWG)], axis=0)


def _group_lanes(x_ref, hk):
    return jnp.concatenate([x_ref[hk * SWG + g:hk * SWG + g + 1, :] for g in range(SWG)], axis=1)


def _sink_lanes(sink_ref, hk):
    return jnp.concatenate([jnp.broadcast_to(sink_ref[:, hk * SWG + g:hk * SWG + g + 1], (1, WIN))
                            for g in range(SWG)], axis=1)


def _swa_fwd(proj, sinks):
    t = proj.shape[0]
    nb = t // WIN

    def body(q_ref, kvc_ref, kvp_ref, sink_ref, y_ref, lse_ref):
        valid = _swa_mask(pl.program_id(0))
        kvh = range(SWKV)
        kb = [jnp.concatenate([kvp_ref[:, hk * HD:(hk + 1) * HD], kvc_ref[:, hk * HD:(hk + 1) * HD]],
                              axis=0).astype(BF16) for hk in kvh]
        vb = [jnp.concatenate([kvp_ref[:, (SWKV + hk) * HD:(SWKV + hk + 1) * HD],
                               kvc_ref[:, (SWKV + hk) * HD:(SWKV + hk + 1) * HD]], axis=0).astype(BF16) for hk in kvh]
        sink = [_sink_lanes(sink_ref, hk) for hk in kvh]
        logits = [_dot_nt(kb[hk], _group_rows(q_ref, hk).astype(BF16)) for hk in kvh]
        logits = [jnp.where(valid, logits[hk] * _SW_SCALE, -jnp.inf) for hk in kvh]
        m = [jnp.maximum(jnp.max(logits[hk], axis=0, keepdims=True), sink[hk]) for hk in kvh]
        p = [jnp.exp(logits[hk] - m[hk]) for hk in kvh]
        denom = [jnp.sum(p[hk], axis=0, keepdims=True) + jnp.exp(sink[hk] - m[hk]) for hk in kvh]
        y4 = [_dot_tn((p[hk] / denom[hk]).astype(BF16), vb[hk]).astype(BF16) for hk in kvh]
        for hk in kvh:
            lse4 = m[hk] + jnp.log(denom[hk])
            for g in range(SWG):
                hq = hk * SWG + g
                y_ref[:, hq * HD:(hq + 1) * HD] = y4[hk][g * WIN:(g + 1) * WIN]
                lse_ref[hq:hq + 1, :] = lse4[:, g * WIN:(g + 1) * WIN]

    return pl.pallas_call(
        body, name="swa_fwd", grid=(nb,),
        in_specs=[pl.BlockSpec((WIN, D), lambda n: (n, C_QSW // D)),
                  pl.BlockSpec((WIN, 512), lambda n: (n, _KVB)),
                  pl.BlockSpec((WIN, 512), lambda n: (jnp.maximum(n - 1, 0), _KVB)),
                  pl.BlockSpec((1, SWH), lambda n: (0, 0))],
        out_specs=[pl.BlockSpec((WIN, D), lambda n: (n, 0)), pl.BlockSpec((SWH, WIN), lambda n: (0, n))],
        out_shape=[jax.ShapeDtypeStruct((t, D), BF16), jax.ShapeDtypeStruct((SWH, t), F32)],
        compiler_params=_params(),
    )(proj, proj, proj, sinks)


def _swa_bwd(proj, sinks, lse, dyb, dproj):
    t = proj.shape[0]
    nb = t // WIN

    def body(q_ref, kvc_ref, kvp_ref, sink_ref, lse_ref, dy_ref, _, dq_ref, dself_ref, dprev_ref, ds_ref):
        @pl.when(pl.program_id(0) == 0)
        def _():
            ds_ref[...] = jnp.zeros_like(ds_ref)

        valid = _swa_mask(pl.program_id(0))
        kvh = range(SWKV)
        ks = [slice(hk * HD, (hk + 1) * HD) for hk in kvh]
        vs = [slice(SWKV * HD + hk * HD, SWKV * HD + (hk + 1) * HD) for hk in kvh]
        kb = [jnp.concatenate([kvp_ref[:, ks[hk]], kvc_ref[:, ks[hk]]], axis=0).astype(BF16) for hk in kvh]
        vb = [jnp.concatenate([kvp_ref[:, vs[hk]], kvc_ref[:, vs[hk]]], axis=0).astype(BF16) for hk in kvh]
        qb = [_group_rows(q_ref, hk).astype(BF16) for hk in kvh]
        dyb_ = [_group_rows(dy_ref, hk).astype(BF16) for hk in kvh]
        lse4 = [_group_lanes(lse_ref, hk) for hk in kvh]
        logits = [_dot_nt(kb[hk], qb[hk]) for hk in kvh]
        dpt = [_dot_nt(vb[hk], dyb_[hk]) for hk in kvh]
        p = [jnp.exp(jnp.where(valid, logits[hk] * _SW_SCALE, -jnp.inf) - lse4[hk]) for hk in kvh]
        delta = [jnp.sum(p[hk] * dpt[hk], axis=0, keepdims=True) for hk in kvh]
        dsm = [(p[hk] * (dpt[hk] - delta[hk])).astype(BF16) for hk in kvh]
        dq4 = [(_dot_tn(dsm[hk], kb[hk]) * _SW_SCALE).astype(BF16) for hk in kvh]
        dkb = [_dot_nn(dsm[hk], qb[hk]) * _SW_SCALE for hk in kvh]
        dvb = [_dot_nn(p[hk].astype(BF16), dyb_[hk]) for hk in kvh]
        for hk in kvh:
            dsink4 = jnp.exp(_sink_lanes(sink_ref, hk) - lse4[hk]) * delta[hk]
            for g in range(SWG):
                hq = hk * SWG + g
                dq_ref[:, hq * HD:(hq + 1) * HD] = dq4[hk][g * WIN:(g + 1) * WIN]
                ds_ref[:, hq:hq + 1] += -jnp.sum(dsink4[:, g * WIN:(g + 1) * WIN], axis=1, keepdims=True)
            dprev_ref[:, ks[hk]] = dkb[hk][:WIN]
            dself_ref[:, ks[hk]] = dkb[hk][WIN:]
            dprev_ref[:, vs[hk]] = dvb[hk][:WIN]
            dself_ref[:, vs[hk]] = dvb[hk][WIN:]

    return pl.pallas_call(
        body, name="swa_bwd", grid=(nb,),
        in_specs=[pl.BlockSpec((WIN, D), lambda n: (n, C_QSW // D)),
                  pl.BlockSpec((WIN, 512), lambda n: (n, _KVB)),
                  pl.BlockSpec((WIN, 512), lambda n: (jnp.maximum(n - 1, 0), _KVB)),
                  pl.BlockSpec((1, SWH), lambda n: (0, 0)),
                  pl.BlockSpec((SWH, WIN), lambda n: (0, n)),
                  pl.BlockSpec((WIN, D), lambda n: (n, 0)), _ANY],
        out_specs=[pl.BlockSpec((WIN, D), lambda n: (n, C_QSW // D)), pl.BlockSpec((WIN, 512), lambda n: (n, 0)),
                   pl.BlockSpec((WIN, 512), lambda n: (jnp.maximum(n - 1, 0), 0)),
                   pl.BlockSpec((1, SWH), lambda n: (0, 0))],
        out_shape=[jax.ShapeDtypeStruct((t, NP), BF16), jax.ShapeDtypeStruct((t, 512), F32),
                   jax.ShapeDtypeStruct((t, 512), F32), jax.ShapeDtypeStruct((1, SWH), F32)],
        input_output_aliases={6: 0}, compiler_params=_params(),
    )(proj, proj, proj, sinks, lse, dyb, dproj)


def _kv_combine(dself, dnext, dif, dproj):
    t = dself.shape[0]
    rows = _pick(t, 512)

    def body(a_ref, b_ref, dif_ref, _, o_ref):
        row = pl.program_id(0) * rows + lax.broadcasted_iota(jnp.int32, (rows, 1), 0)
        o_ref[:, 0:512] = (a_ref[...] + jnp.where(row < t - WIN, b_ref[...], 0.0)).astype(BF16)
        lane = lax.broadcasted_iota(jnp.int32, (rows, 128), 1)
        dif_v = dif_ref[...]
        first = jnp.zeros((rows, 128), F32)
        for col in range(8):
            first = first + jnp.where(lane == col, dif_v[:, col:col + 1], 0.0)
        o_ref[:, 512:640] = first.astype(BF16)
        o_ref[:, 640:512 + IFW] = jnp.zeros((rows, IFW - 128), BF16)

    return pl.pallas_call(
        body, name="kv_combine", grid=(t // rows,),
        in_specs=[pl.BlockSpec((rows, 512), lambda n: (n, 0)), pl.BlockSpec((rows, 512), lambda n: (n, 0)),
                  pl.BlockSpec((rows, 8), lambda n: (n, 0)), _ANY],
        out_specs=pl.BlockSpec((rows, 512 + IFW), lambda n: (n, C_KV // (512 + IFW))),
        out_shape=jax.ShapeDtypeStruct((t, NP), BF16), input_output_aliases={3: 0}, compiler_params=_params(),
    )(dself, dnext, dif, dproj)


def _sds(t, n, dtype):
    return jax.ShapeDtypeStruct((t, n), dtype)


def _proj_in(h0, w_in):
    t = h0.shape[0]
    tn = 2 * IFW

    def epilogue(accs, ins, outs, i, j):
        outs[0][...] = accs[0].astype(BF16)

        @pl.when(j == C_IF // tn)
        def _():
            outs[1][...] = accs[0][:, C_IF % tn:C_IF % tn + 128]

    gate_cols = lambda tm, tn: pl.BlockSpec((tm, 128), lambda i, j, kk: (i, 0))
    return _mm_ep([(h0, w_in)], "nn", "mm_in", epilogue, [],
                  [(_sds(t, NP, BF16), _tile()), (_sds(t, 128, F32), gate_cols)], 1024, tn)


def _branch_merge(ya, yb, wa, wb, proj):
    t = ya.shape[0]

    def epilogue(accs, ins, outs, i, j):
        za, zb = accs
        merged = _sigmoid(ins[0][...].astype(F32)) * za + _sigmoid(ins[1][...].astype(F32)) * zb
        outs[0][...] = merged.astype(BF16)
        outs[1][...] = za.astype(BF16)
        outs[2][...] = zb.astype(BF16)

    return _mm_ep([(ya, wa), (yb, wb)], "nn", "mm_branch_merge", epilogue, [(proj, _tile(C_GA)), (proj, _tile(C_GB))],
                  [(_sds(t, D, BF16), _tile())] * 3, 1024, 512)


def _dmerged_bwd(dxb, w_out, proj, za, zb):
    t = dxb.shape[0]

    def epilogue(accs, ins, outs, i, j):
        dm = accs[0]
        sa, sb = _sigmoid(ins[0][...].astype(F32)), _sigmoid(ins[1][...].astype(F32))
        outs[0][...] = (dm * sa).astype(BF16)
        outs[1][...] = (dm * sb).astype(BF16)
        outs[2][:, 0:D] = (dm * ins[2][...].astype(F32) * sa * (1.0 - sa)).astype(BF16)
        outs[2][:, D:2 * D] = (dm * ins[3][...].astype(F32) * sb * (1.0 - sb)).astype(BF16)

    gate_cols = lambda tm, tn: pl.BlockSpec((tm, 2 * D), lambda i, j, kk: (i, C_GA // (2 * D)))
    return _mm_ep([(dxb, w_out)], "nt", "mm_dmerged_bwd", epilogue,
                  [(proj, _tile(C_GA)), (proj, _tile(C_GB)), (za, _tile()), (zb, _tile())],
                  [(_sds(t, D, BF16), _tile()), (_sds(t, D, BF16), _tile()), (_sds(t, NP, BF16), gate_cols)], 1024, D)


def _dya_bwd(dza, wa, hraw, proj, g, dproj):
    t = dza.shape[0]

    def epilogue(accs, ins, outs, i, j):
        h_ref, o_ref, g_ref, _ = ins
        dh_ref, do_ref, dg_ref = outs

        @pl.when(i == 0)
        def _():
            dg_ref[...] = jnp.zeros_like(dg_ref)

        dy = accs[0]
        so = _sigmoid(o_ref[...].astype(F32))
        for h in range(MLH):
            sl = slice(h * DV, (h + 1) * DV)
            xn, rstd = _rms(h_ref[:, sl])
            gs = g_ref[:, sl]
            do_ref[:, sl] = (dy[:, sl] * xn * gs * so[:, sl] * (1.0 - so[:, sl])).astype(BF16)
            dhn = dy[:, sl] * so[:, sl]
            dg_ref[:, sl] += jnp.sum(dhn * xn, axis=0, keepdims=True)
            dh_ref[:, sl] = _rms_bwd(xn, rstd, dhn * gs)

    return _mm_ep([(dza, wa)], "nt", "mm_dya_bwd", epilogue,
                  [(hraw, _tile()), (proj, _tile(C_O)), (g, _row()), (dproj, lambda tm, tn: _ANY)],
                  [(_sds(t, D, F32), _tile()), (_sds(t, NP, BF16), _tile(C_O)), (_sds(1, D, F32), _row())],
                  1024, D, aliases={3: 1})


def _up_act(hn, w_up):
    t = hn.shape[0]

    def epilogue(accs, ins, outs, i, j):
        r = jnp.maximum(accs[0], 0.0)
        outs[0][...] = (r * r).astype(BF16)
        outs[1][...] = accs[0].astype(BF16)

    return _mm_ep([(hn, w_up)], "nn", "mm_up_act", epilogue, [],
                  [(_sds(t, DFF, BF16), _tile()), (_sds(t, DFF, BF16), _tile())], 1024, 1024)


def _da_du(dxb, w_down, u):
    t = dxb.shape[0]

    def epilogue(accs, ins, outs, i, j):
        outs[0][...] = (accs[0] * 2.0 * jnp.maximum(ins[0][...].astype(F32), 0.0)).astype(BF16)

    return _mm_ep([(dxb, w_down)], "nt", "mm_da_du", epilogue, [(u, _tile())], [(_sds(t, DFF, BF16), _tile())],
                  1024, 1024)[0]


def _resid_norm_mm(a, w, x, g, name):
    t = x.shape[0]

    def epilogue(accs, ins, outs, i, j):
        x1 = ins[0][...] + accs[0]
        outs[0][...] = x1
        xn, _ = _rms(x1)
        outs[1][...] = (xn * ins[1][...]).astype(BF16)

    return _mm_ep([(a, w)], "nn", name, epilogue, [(x, _tile()), (g, _row())],
                  [(_sds(t, D, F32), _tile()), (_sds(t, D, BF16), _tile())], 1024, D)


def _norm_bwd_mm(dy, w, x, g, dres, name):
    t = x.shape[0]

    def epilogue(accs, ins, outs, i, j):
        @pl.when(i == 0)
        def _():
            outs[2][...] = jnp.zeros_like(outs[2])

        dh = accs[0]
        xn, rstd = _rms(ins[0][...])
        outs[2][...] += jnp.sum(dh * xn, axis=0, keepdims=True)
        dx = ins[2][...] + _rms_bwd(xn, rstd, dh * ins[1][...])
        outs[0][...] = dx
        outs[1][...] = dx.astype(BF16)

    return _mm_ep([(dy, w)], "nt", name, epilogue, [(x, _tile()), (g, _row()), (dres, _tile())],
                  [(_sds(t, D, F32), _tile()), (_sds(t, D, BF16), _tile()), (_sds(1, D, F32), _row())], 1024, D)


def _ple_final_mm(hn2, w_gate, x2, pp, target, gf):
    t = x2.shape[0]

    def epilogue(accs, ins, outs, i, j):
        loss_ref, dg_ref, dx_ref, dpp_ref, dgp_ref = outs

        @pl.when(i == 0)
        def _():
            loss_ref[...] = jnp.zeros_like(loss_ref)
            dg_ref[...] = jnp.zeros_like(dg_ref)

        gate = _sigmoid(accs[0])
        pp_v = ins[1][...]
        x3 = ins[0][...] + gate * pp_v
        xn, rstd = _rms(x3)
        gf_v = ins[3][...]
        err = xn * gf_v - ins[2][...]
        loss_ref[...] += (0.5 / D) * jnp.sum(jnp.sum(err * err, axis=1, keepdims=True), axis=0, keepdims=True)
        dy = err * (1.0 / D)
        dg_ref[...] += jnp.sum(dy * xn, axis=0, keepdims=True)
        dx3 = _rms_bwd(xn, rstd, dy * gf_v)
        dx_ref[...] = dx3
        dpp_ref[...] = (dx3 * gate).astype(BF16)
        dgp_ref[...] = (dx3 * pp_v * gate * (1.0 - gate)).astype(BF16)

    one = lambda tm, tn: pl.BlockSpec((1, 1), lambda i, j, kk: (0, 0))
    return _mm_ep([(hn2, w_gate)], "nn", "mm_ple_final", epilogue,
                  [(x2, _tile()), (pp, _tile()), (target, _tile()), (gf, _row())],
                  [(_sds(1, 1, F32), one), (_sds(1, D, F32), _row()), (_sds(t, D, F32), _tile()),
                   (_sds(t, D, BF16), _tile()), (_sds(t, D, BF16), _tile())], 512, D)


_WIN_SEGMENTS = ((0, 3072, C_QK), (3072, 8, C_IF), (3080, 1024, C_QSW), (4104, 256, C_KV), (4360, 256, C_KV + 256),
                 (4616, 1024, C_GA), (5640, 1024, C_GB))
_WIN_SHARD = N_IN // 4


def _win_pieces():
    out = []
    for src, width, dst in _WIN_SEGMENTS:
        while width:
            chip, col = divmod(src, _WIN_SHARD)
            n = min(width, _WIN_SHARD - col)
            out.append((chip, col, n, dst))
            src, dst, width = src + n, dst + n, width - n
    return out


def _win_pad(shards):
    rows = shards.shape[1]
    tr = _pick(rows, 256)

    def body(s_ref, o_ref):
        for chip, col, n, dst in _win_pieces():
            o_ref[:, dst:dst + n] = s_ref[chip, :, col:col + n]
        o_ref[:, C_IF + 8:NP] = jnp.zeros((tr, NP - C_IF - 8), shards.dtype)

    return pl.pallas_call(
        body, name="win_pad", grid=(rows // tr,), in_specs=[pl.BlockSpec((4, tr, _WIN_SHARD), lambda i: (0, i, 0))],
        out_specs=pl.BlockSpec((tr, NP), lambda i: (i, 0)), out_shape=jax.ShapeDtypeStruct((rows, NP), shards.dtype),
        compiler_params=_params(),
    )(shards)


def _win_unpad(wp):
    rows = wp.shape[0]
    tr = _pick(rows, 256)

    def body(p_ref, o_ref):
        for chip, col, n, dst in _win_pieces():
            o_ref[chip, :, col:col + n] = p_ref[:, dst:dst + n]

    return pl.pallas_call(
        body, name="win_unpad", grid=(rows // tr,), in_specs=[pl.BlockSpec((tr, NP), lambda i: (i, 0))],
        out_specs=pl.BlockSpec((4, tr, _WIN_SHARD), lambda i: (0, i, 0)),
        out_shape=jax.ShapeDtypeStruct((4, rows, _WIN_SHARD), wp.dtype), compiler_params=_params(),
    )(wp)


def _local_step(x, p, target, w, late_weights=None, early_grads=None, mid_grads=None, last_grad=None):
    t = x.shape[0]
    pb = p.astype(BF16)
    w = dict(w)

    h0 = _norm_fwd(x, w["norm_mix_g"], "norm_mix")
    proj, gates = _proj_in(h0, w["w_in"])
    qk = _conv_silu_fwd(proj, w["conv_qk"])
    grow, sneg_row = _gates_fwd(gates[:, 0:8].T, w["b_if"].reshape(8, 1))
    gcol, sneg_col = grow.T, sneg_row.T
    hraw, ya, cs, st = _mlstm_fwd(qk, proj, grow, gcol, w["mlstm_norm_g"])
    yb, lse = _swa_fwd(proj, w["sinks"])
    if late_weights is not None:
        w.update(late_weights(yb))
    merged, za, zb = _branch_merge(ya, yb, w["w_branch_a"], w["w_branch_b"], proj)
    x1, hn1 = _resid_norm_mm(merged, w["w_out"], x, w["norm_mlp_g"], "mm_out_norm")
    act, u = _up_act(hn1, w["w_up"])
    x2, hn2 = _resid_norm_mm(act, w["w_down"], x1, w["norm_ple_g"], "mm_down_norm")
    pp = _mm(pb, w["w_ple_proj"], "nn", F32, "mm_ple_proj")
    loss, d_final_g, dx3, dpp, dgpre = _ple_final_mm(hn2, w["w_ple_gate"], x2, pp, target, w["final_norm_g"])

    g = {"final_norm_g": d_final_g}
    g["w_ple_proj"] = _mm(pb, dpp, "tn", F32, "mm_d_ple_proj", out_chunks=4)
    g["w_ple_gate"] = _mm(hn2, dgpre, "tn", F32, "mm_d_ple_gate")
    dx2, dx2b, g["norm_ple_g"] = _norm_bwd_mm(dgpre, w["w_ple_gate"], x2, w["norm_ple_g"], dx3, "mm_dhn2_norm")
    g["w_down"] = _mm(act, dx2b, "tn", F32, "mm_d_down")
    du = _da_du(dx2b, w["w_down"], u)
    g["w_up"] = _mm(hn1, du, "tn", F32, "mm_d_up", out_chunks=4)
    dx1, dx1b, g["norm_mlp_g"] = _norm_bwd_mm(du, w["w_up"], x1, w["norm_mlp_g"], dx2, "mm_dhn1_norm")
    g["w_out"] = _mm(merged, dx1b, "tn", F32, "mm_d_out")
    dza, dzb, dproj = _dmerged_bwd(dx1b, w["w_out"], proj, za, zb)
    g["w_branch_a"] = _mm(ya, dza, "tn", F32, "mm_d_branch_a")
    g["w_branch_b"] = _mm(yb, dzb, "tn", F32, "mm_d_branch_b")
    gain = w["mlstm_norm_g"] if early_grads is None else w["mlstm_norm_g"] + early_grads(g)
    dyb = _mm(dzb, w["w_branch_b"], "nt", F32, "mm_dyb")
    dhraw, dproj, g["mlstm_norm_g"] = _dya_bwd(dza, w["w_branch_a"], hraw, proj, gain, dproj)
    if mid_grads is not None:
        sneg_col = sneg_col + mid_grads(dhraw)
    dqk, dproj, dif, g["b_if"] = _mlstm_bwd(qk, proj, grow, gcol, sneg_col, cs, st, hraw, dhraw, dproj)
    dc, g["conv_qk"] = _conv_silu_bwd_a(proj, w["conv_qk"], dqk)
    dproj = _conv_silu_bwd_b(dc, w["conv_qk"], dproj)
    dproj, dkv_self, dkv_prev, g["sinks"] = _swa_bwd(proj, w["sinks"], lse, dyb, dproj)
    dproj = _kv_combine(dkv_self, dkv_prev, dif, dproj)
    g["w_in"] = _mm(h0, dproj, "tn", F32, "mm_d_in")
    gain = w["norm_mix_g"] if last_grad is None else w["norm_mix_g"] + last_grad(g)
    grad_x, _, g["norm_mix_g"] = _norm_bwd_mm(dproj, w["w_in"], x, gain, dx1, "mm_dh0_norm")
    return loss, grad_x, g


_W4 = ("w_branch_a", "w_branch_b", "w_out", "w_ple_gate")
_SHARDED_NAMES = ("w_in", "w_up", "w_down", "w_ple_proj", "conv_qk") + _W4
_SMALL_ROWS = 16
_CONV_ROW = 8


def _group(s):
    return [s["w_in"], jnp.concatenate([s[n] for n in _W4], axis=0), s["w_up"], s["w_down"], s["w_ple_proj"]]


def _ungroup(arrs):
    out = {"w_in": arrs[0], "w_up": arrs[2], "w_down": arrs[3], "w_ple_proj": arrs[4]}
    rows = arrs[1].shape[0] // len(_W4)
    for i, n in enumerate(_W4):
        out[n] = arrs[1][i * rows:(i + 1) * rows]
    return out


def _rows_tile(rows):
    return 256 if rows % 256 == 0 else rows


_SMALL = ("norm_mix_g", "mlstm_norm_g", "norm_mlp_g", "norm_ple_g", "final_norm_g")


def _pack_small(vals, extra=None, conv=None):
    rows = [vals[n].reshape(1, D) for n in _SMALL]
    tail = [vals["b_if"].reshape(1, 8), vals["sinks"].reshape(1, SWH)]
    used = 8 + SWH
    if extra is not None:
        tail.append(extra.reshape(1, 1))
        used += 1
    tail.append(jnp.zeros((1, D - used), F32))
    rows.append(jnp.concatenate(tail, axis=1))
    rows.append(jnp.zeros((_CONV_ROW - len(rows), D), F32))
    rows.append(jnp.zeros((CONV, D), F32) if conv is None else conv)
    rows.append(jnp.zeros((_SMALL_ROWS - _CONV_ROW - CONV, D), F32))
    return jnp.concatenate(rows, axis=0)


def _unpack_small(slab, shapes):
    out = {n: slab[i].reshape(shapes[n]) for i, n in enumerate(_SMALL)}
    out["b_if"] = slab[5, 0:8].reshape(shapes["b_if"])
    out["sinks"] = slab[5, 8:8 + SWH].reshape(shapes["sinks"])
    return out


_MESH = pl.DeviceIdType.MESH
_HBM = pl.BlockSpec(memory_space=pltpu.HBM)
_VMEM = pl.BlockSpec(memory_space=pltpu.VMEM)


def _place():
    x, y, c = lax.axis_index("x"), lax.axis_index("y"), lax.axis_index("c")
    return x, y, c, 2 * x + y


def _chip_peer(x, y, r):
    return (x ^ (r >> 1), y ^ (r & 1))


def _half(ref, which):
    h = ref.shape[-2] // 2
    return pl.ds(which * h, h)


def _allgather_weights(shards, conv):
    n = len(shards)

    def body(*refs):
        ins, conv_ref = refs[:n], refs[n]
        outs, conv_out = refs[n + 1:2 * n + 1], refs[2 * n + 1]
        send_a, recv_a, send_b, recv_b, send_c, recv_c, local_sems = refs[2 * n + 2:]
        x, y, c, j = _place()
        sibling = (x, y, 1 - c)
        local = [pltpu.make_async_copy(ins[k], outs[k].at[j], local_sems.at[k]) for k in range(n)]
        local.append(pltpu.make_async_copy(conv_ref, conv_out.at[j], local_sems.at[n]))
        for cp in local:
            cp.start()

        def copy_a(k, r, chip):
            rows = _half(ins[k], c)
            return pltpu.make_async_remote_copy(
                src_ref=ins[k].at[rows], dst_ref=outs[k].at[chip, rows], send_sem=send_a.at[3 * k + r - 1],
                recv_sem=recv_a.at[3 * k + r - 1], device_id=(*_chip_peer(x, y, r), c), device_id_type=_MESH)

        def copy_b(k, r, chip, which):
            rows = _half(ins[k], which)
            return pltpu.make_async_remote_copy(
                src_ref=outs[k].at[chip, rows], dst_ref=outs[k].at[chip, rows], send_sem=send_b.at[3 * k + r - 1],
                recv_sem=recv_b.at[3 * k + r - 1], device_id=sibling, device_id_type=_MESH)

        def copy_c(r, chip):
            return pltpu.make_async_remote_copy(
                src_ref=conv_ref, dst_ref=conv_out.at[chip], send_sem=send_c.at[r - 1],
                recv_sem=recv_c.at[r - 1], device_id=(*_chip_peer(x, y, r), c), device_id_type=_MESH)

        for k in range(n):
            for r in (1, 2, 3):
                copy_a(k, r, j).start()
        for r in (1, 2, 3):
            copy_c(r, j).start()
        for k in range(n):
            for r in (1, 2, 3):
                copy_a(k, r, j ^ r).wait_recv()
                copy_b(k, r, j ^ r, c).start()
        for k in range(n):
            for r in (1, 2, 3):
                copy_b(k, r, j ^ r, 1 - c).wait_recv()
        for r in (1, 2, 3):
            copy_c(r, j ^ r).wait_recv()
        for k in range(n):
            for r in (1, 2, 3):
                copy_a(k, r, j).wait_send()
                copy_b(k, r, j ^ r, c).wait_send()
        for r in (1, 2, 3):
            copy_c(r, j).wait_send()
        for cp in local:
            cp.wait()

    return pl.pallas_call(
        body, name="allgather_weights",
        out_shape=[jax.ShapeDtypeStruct((4,) + s.shape, s.dtype) for s in shards]
        + [jax.ShapeDtypeStruct((4,) + conv.shape, F32)],
        in_specs=[_HBM] * (n + 1), out_specs=[_HBM] * (n + 1),
        scratch_shapes=[pltpu.SemaphoreType.DMA((3 * n,))] * 4 + [pltpu.SemaphoreType.DMA((3,))] * 2
        + [pltpu.SemaphoreType.DMA((n + 1,))],
    )(*shards, conv)


_SEM = pl.BlockSpec(memory_space=pltpu.SEMAPHORE)
_DATAFLOW = pltpu.SideEffectType.DATAFLOW_SIDE_EFFECTING


def _late_peer_copy(src_ref, land_ref, send_sems, recv_sems, x, y, c, j, r, chip):
    return pltpu.make_async_remote_copy(
        src_ref=src_ref, dst_ref=land_ref.at[chip], send_sem=send_sems.at[r - 1], recv_sem=recv_sems.at[r - 1],
        device_id=(*_chip_peer(x, y, r), c), device_id_type=_MESH)


def _late_gather_start(rest):
    def body(rest_ref, land_ref, send_sems, recv_sems, rest_thru, land_thru, token):
        x, y, c, j = _place()
        for r in (1, 2, 3):
            _late_peer_copy(rest_ref, land_ref, send_sems, recv_sems, x, y, c, j, r, j).start()
        token[...] = jnp.zeros_like(token)

    j = 2 * lax.axis_index("x") + lax.axis_index("y")
    land = lax.dynamic_update_slice(lax.empty((4,) + rest.shape, rest.dtype), rest[None], (j, 0, 0))
    return pl.pallas_call(
        body, name="late_gather_start",
        out_shape=(pltpu.SemaphoreType.DMA((3,)), pltpu.SemaphoreType.DMA((3,)), pltpu.HBM(rest.shape, rest.dtype),
                   pltpu.HBM(land.shape, land.dtype), jax.ShapeDtypeStruct((8, 128), F32)),
        in_specs=(_HBM, _HBM), out_specs=(_SEM, _SEM, _HBM, _HBM, _VMEM), input_output_aliases={0: 2, 1: 3},
        compiler_params=pltpu.CompilerParams(has_side_effects=_DATAFLOW),
    )(pltpu.with_memory_space_constraint(rest, pltpu.HBM), pltpu.with_memory_space_constraint(land, pltpu.HBM))


def _late_gather_wait(send_sems, recv_sems, rest_thru, land_thru, after):
    def body(rest_ref, land_ref, send_sems, recv_sems, after_ref, rest_dead, got_ref):
        x, y, c, j = _place()
        for r in (1, 2, 3):
            cp = _late_peer_copy(rest_ref, land_ref, send_sems, recv_sems, x, y, c, j, r, j ^ r)
            cp.wait_send()
            cp.wait_recv()

    return pl.pallas_call(
        body, name="late_gather_wait",
        out_shape=(pltpu.HBM(rest_thru.shape, rest_thru.dtype), pltpu.HBM(land_thru.shape, land_thru.dtype)),
        in_specs=(_HBM, _HBM, _SEM, _SEM, _ANY), out_specs=(_HBM, _HBM), input_output_aliases={0: 0, 1: 1},
        compiler_params=pltpu.CompilerParams(has_side_effects=_DATAFLOW),
    )(rest_thru, land_thru, send_sems, recv_sems, after)[1]


def _pair_exchange(gs, name):
    n = len(gs)

    def body(*refs):
        ins, outs, send_sems, recv_sems = refs[:n], refs[n:2 * n], refs[2 * n], refs[2 * n + 1]
        x, y, c, _ = _place()
        cps = [pltpu.make_async_remote_copy(
            src_ref=ins[k].at[:, _half(ins[k], 1 - c)], dst_ref=outs[k], send_sem=send_sems.at[k],
            recv_sem=recv_sems.at[k], device_id=(x, y, 1 - c), device_id_type=_MESH) for k in range(n)]
        for cp in cps:
            cp.start()
        for cp in cps:
            cp.wait()

    return pl.pallas_call(
        body, name=name,
        out_shape=[jax.ShapeDtypeStruct((4, g.shape[1] // 2, g.shape[2]), F32) for g in gs],
        in_specs=[_HBM] * n, out_specs=[_HBM] * n, scratch_shapes=[pltpu.SemaphoreType.DMA((n,))] * 2,
    )(*gs)


def _pair_sum(g, theirs, c, name):
    _, h, cols = theirs.shape
    tr = _rows_tile(h)
    nb = h // tr

    def body(c_ref, a_ref, b_ref, o_ref, ob_ref):
        s = a_ref[...] + b_ref[...]
        o_ref[...] = s
        ob_ref[...] = s.astype(BF16)

    blk = pl.BlockSpec((1, tr, cols), lambda k, i, c_ref: (k, i, 0))
    return pl.pallas_call(
        body, name=name,
        grid_spec=pltpu.PrefetchScalarGridSpec(
            num_scalar_prefetch=1, grid=(4, nb),
            in_specs=[pl.BlockSpec((1, tr, cols), lambda k, i, c_ref: (k, c_ref[0] * nb + i, 0)), blk],
            out_specs=[blk, blk]),
        out_shape=[jax.ShapeDtypeStruct(theirs.shape, F32), jax.ShapeDtypeStruct(theirs.shape, BF16)],
        compiler_params=_params(),
    )(c.reshape(1).astype(jnp.int32), g, theirs)


def _chip_copies(srcs, lands, send_sems, recv_sems):
    x, y, c, j = _place()
    return [pltpu.make_async_remote_copy(
        src_ref=srcs[k].at[j ^ r], dst_ref=lands[k].at[r - 1], send_sem=send_sems.at[3 * k + r - 1],
        recv_sem=recv_sems.at[3 * k + r - 1], device_id=(*_chip_peer(x, y, r), c), device_id_type=_MESH)
        for k in range(len(srcs)) for r in (1, 2, 3)]


def _pair_copies(srcs, lands, send_sems, recv_sems):
    x, y, c, _ = _place()
    return [pltpu.make_async_remote_copy(
        src_ref=srcs[k].at[:, _half(srcs[k], 1 - c)], dst_ref=lands[k], send_sem=send_sems.at[k],
        recv_sem=recv_sems.at[k], device_id=(x, y, 1 - c), device_id_type=_MESH) for k in range(len(srcs))]


def _split_start(name, srcs, lands, copies, n_sems):
    n = len(srcs)

    def body(*refs):
        for cp in copies(refs[:n], refs[n:2 * n], refs[2 * n], refs[2 * n + 1]):
            cp.start()
        refs[-1][...] = jnp.zeros_like(refs[-1])

    arrays = list(srcs) + list(lands)
    out = pl.pallas_call(
        body, name=name,
        out_shape=(pltpu.SemaphoreType.DMA((n_sems,)), pltpu.SemaphoreType.DMA((n_sems,)),
                   *[pltpu.HBM(a.shape, a.dtype) for a in arrays], jax.ShapeDtypeStruct((8, 128), F32)),
        in_specs=[_HBM] * (2 * n), out_specs=(_SEM, _SEM, *([_HBM] * (2 * n)), _VMEM),
        input_output_aliases={k: 2 + k for k in range(2 * n)},
        compiler_params=pltpu.CompilerParams(has_side_effects=_DATAFLOW),
    )(*[pltpu.with_memory_space_constraint(a, pltpu.HBM) for a in arrays])
    return out[0], out[1], list(out[2:2 + n]), list(out[2 + n:2 + 2 * n]), out[-1]


def _split_wait(name, send_sems, recv_sems, srcs_thru, lands_thru, after, copies):
    n = len(srcs_thru)

    def body(*refs):
        for cp in copies(refs[:n], refs[n:2 * n], refs[2 * n], refs[2 * n + 1]):
            cp.wait_send()
            cp.wait_recv()

    arrays = list(srcs_thru) + list(lands_thru)
    out = pl.pallas_call(
        body, name=name, out_shape=tuple(pltpu.HBM(a.shape, a.dtype) for a in arrays),
        in_specs=[_HBM] * (2 * n) + [_SEM, _SEM, _ANY], out_specs=tuple([_HBM] * (2 * n)),
        input_output_aliases={k: k for k in range(2 * n)},
        compiler_params=pltpu.CompilerParams(has_side_effects=_DATAFLOW),
    )(*arrays, send_sems, recv_sems, after)
    return list(out[:n]), list(out[n:])


def _chip_exchange_start(ss, tag):
    lands = [lax.empty((3,) + s.shape[1:], s.dtype) for s in ss]
    return _split_start("chip_exchange_start_" + tag, ss, lands, _chip_copies, 3 * len(ss))


def _chip_exchange_wait(send_sems, recv_sems, ss_thru, lands_thru, after, tag):
    return _split_wait("chip_exchange_wait_" + tag, send_sems, recv_sems, ss_thru, lands_thru, after, _chip_copies)[1]


def _pair_exchange_start(gs, tag):
    lands = [lax.empty((4, g.shape[1] // 2, g.shape[2]), g.dtype) for g in gs]
    return _split_start("pair_exchange_start_" + tag, gs, lands, _pair_copies, len(gs))


def _pair_exchange_wait(send_sems, recv_sems, gs_thru, lands_thru, after, tag):
    return _split_wait("pair_exchange_wait_" + tag, send_sems, recv_sems, gs_thru, lands_thru, after, _pair_copies)


def _reduce4(own, others, j, c, name):
    _, h, cols = own.shape
    tr = _rows_tile(h)
    nb = h // tr

    def body(idx_ref, s_ref, a0, a1, a2, o_ref):
        o_ref[...] = ((s_ref[0] + a0[0].astype(F32)) + a1[0].astype(F32)) + a2[0].astype(F32)

    def other(r):
        return pl.BlockSpec((1, tr, cols), lambda i, idx_ref: (r, i, 0))

    return pl.pallas_call(
        body, name=name,
        grid_spec=pltpu.PrefetchScalarGridSpec(
            num_scalar_prefetch=1, grid=(nb,),
            in_specs=[pl.BlockSpec((1, tr, cols), lambda i, idx_ref: (idx_ref[0], i, 0)), other(0), other(1), other(2)],
            out_specs=pl.BlockSpec((tr, cols), lambda i, idx_ref: (idx_ref[1] * nb + i, 0))),
        out_shape=jax.ShapeDtypeStruct((2 * h, cols), F32), compiler_params=_params(),
    )(jnp.stack([j, c]).astype(jnp.int32), own, others, others, others)


def _sibling_share(fulls):
    n = len(fulls)

    def body(*refs):
        outs, send_sems, recv_sems = refs[n:2 * n], refs[2 * n], refs[2 * n + 1]
        x, y, c, _ = _place()
        cps = [pltpu.make_async_remote_copy(
            src_ref=outs[k].at[_half(outs[k], c)], dst_ref=outs[k].at[_half(outs[k], c)], send_sem=send_sems.at[k],
            recv_sem=recv_sems.at[k], device_id=(x, y, 1 - c), device_id_type=_MESH) for k in range(n)]
        for cp in cps:
            cp.start()
        for cp in cps:
            cp.wait()

    return pl.pallas_call(
        body, name="sibling_share", out_shape=[jax.ShapeDtypeStruct(f.shape, F32) for f in fulls],
        in_specs=[_HBM] * n, out_specs=[_HBM] * n, input_output_aliases={k: k for k in range(n)},
        scratch_shapes=[pltpu.SemaphoreType.DMA((n,))] * 2,
    )(*fulls)


def _adamw(w, g, m, v):
    m1 = ADAM_B1 * m + (1.0 - ADAM_B1) * g
    v1 = ADAM_B2 * v + (1.0 - ADAM_B2) * (g * g)
    m_hat = m1 / (1.0 - ADAM_B1 ** ADAM_STEP)
    v_hat = v1 / (1.0 - ADAM_B2 ** ADAM_STEP)
    delta = -ADAM_LR * (m_hat / (jnp.sqrt(v_hat) + ADAM_EPS) + ADAM_WD * w)
    return delta, m1, v1


def _adamw_call(w, g, m, v, name):
    rows, cols = w.shape

    def body(w_ref, g_ref, m_ref, v_ref, d_out, m_out, v_out):
        delta, m1, v1 = _adamw(w_ref[...], g_ref[...], m_ref[...], v_ref[...])
        d_out[...] = delta
        m_out[...] = m1
        v_out[...] = v1

    if rows % 8 == 0:
        tr = _rows_tile(rows)
        blk, grid = pl.BlockSpec((tr, cols), lambda i: (i, 0)), (rows // tr,)
    else:
        blk, grid = pl.BlockSpec((rows, 128), lambda i: (0, i)), (cols // 128,)
    return pl.pallas_call(
        body, name=name, grid=grid, in_specs=[blk] * 4, out_specs=[blk] * 3,
        out_shape=[jax.ShapeDtypeStruct((rows, cols), F32)] * 3, compiler_params=_params(),
    )(w, g, m, v)


def _small_allreduce(vals):
    def body(v_ref, out_ref, buf, send_sems, recv_sems):
        x, y, c, j = _place()
        me = 2 * j + c
        buf[0] = v_ref[...]

        def copy(r):
            return pltpu.make_async_remote_copy(
                src_ref=v_ref, dst_ref=buf.at[r], send_sem=send_sems.at[r - 1], recv_sem=recv_sems.at[r - 1],
                device_id=(x ^ (r >> 2), y ^ ((r >> 1) & 1), c ^ (r & 1)), device_id_type=_MESH)

        for r in range(1, 8):
            copy(r).start()
        for r in range(1, 8):
            copy(r).wait()
        acc = buf[me ^ 0]
        for d in range(1, 8):
            acc = acc + buf[me ^ d]
        out_ref[...] = acc

    return pl.pallas_call(
        body, name="small_allreduce", out_shape=jax.ShapeDtypeStruct((_SMALL_ROWS, D), F32),
        in_specs=[_VMEM], out_specs=_VMEM,
        scratch_shapes=[pltpu.VMEM((8, _SMALL_ROWS, D), F32), pltpu.SemaphoreType.DMA((7,)),
                        pltpu.SemaphoreType.DMA((7,))],
    )(vals)


_NAMES = ("norm_mix_g", "w_in", "conv_qk", "b_if", "mlstm_norm_g", "sinks", "w_branch_a", "w_branch_b", "w_out",
          "norm_mlp_g", "w_up", "w_down", "norm_ple_g", "w_ple_gate", "w_ple_proj", "final_norm_g")
_GROUP_NAMES = ("w_in", "w4", "w_up", "w_down", "w_ple_proj")


def _step(x, p, target, w, m, v):
    c = lax.axis_index("c")
    j = 2 * lax.axis_index("x") + lax.axis_index("y")

    def shards(d):
        return {n: d[n][0] for n in _SHARDED_NAMES}

    ws = shards(w)
    w_in_all, conv_all = _allgather_weights([ws["w_in"].astype(BF16)], ws["conv_qk"])
    rows_pp = PLE * (D // 4) // D
    rest = jnp.concatenate([ws[n] for n in _W4] + [ws["w_up"], ws["w_down"], ws["w_ple_proj"].reshape(rows_pp, D)],
                           axis=0)
    rest = (rest + 0.0 * conv_all[0, 0, 0]).astype(BF16)
    send_sems, recv_sems, rest_thru, land_thru, token = _late_gather_start(rest)
    full = {n: w[n] for n in ("mlstm_norm_g", "norm_mlp_g", "norm_ple_g", "b_if", "sinks")}
    full["norm_mix_g"] = w["norm_mix_g"] + token[0, 0]
    full["final_norm_g"] = w["final_norm_g"].reshape(1, D)
    full["w_in"] = _win_pad(w_in_all)
    full["conv_qk"] = jnp.swapaxes(conv_all, 0, 1).reshape(CONV, D)

    def late_weights(after):
        land = _late_gather_wait(send_sems, recv_sems, rest_thru, land_thru, after)
        out = {n: land[:, i * (D // 4):(i + 1) * (D // 4)].reshape(D, D) for i, n in enumerate(_W4)}
        out["w_up"] = land[:, D:2 * D]
        out["w_down"] = land[:, 2 * D:3 * D].reshape(DFF, D)
        out["w_ple_proj"] = jnp.swapaxes(land[:, 3 * D:3 * D + rows_pp].reshape(4, PLE, D // 4), 0, 1).reshape(PLE, D)
        return out

    def pair_sums(by_dest, names, tag):
        theirs = _pair_exchange(by_dest, "pair_exchange_" + tag)
        return [_pair_sum(a, b, c, "pair_sum_" + n) for a, b, n in zip(by_dest, theirs, names)]

    early, last = {}, {}

    def early_grads(g):
        by_dest = [jnp.stack([g[n].reshape(4, D // 4, D) for n in _W4], axis=1).reshape(4, D, D),
                   g["w_up"], g["w_down"].reshape(4, DFF // 4, D), g["w_ple_proj"]]
        *early["pair"], token = _pair_exchange_start(by_dest, "early")
        return token[0, 0]

    def mid_grads(after):
        by_dest, theirs = _pair_exchange_wait(*early["pair"], after, "early")
        early["sums"] = [_pair_sum(a, b, c, "pair_sum_" + n) for a, b, n in zip(by_dest, theirs, _GROUP_NAMES[1:])]
        *early["flight"], token = _chip_exchange_start([s[1] for s in early["sums"]], "early")
        return token[0, 0]

    def last_grad(g):
        last["sums"] = pair_sums([_win_unpad(g["w_in"])], _GROUP_NAMES[:1], "w_in")
        *last["flight"], token = _chip_exchange_start([s[1] for s in last["sums"]], "w_in")
        return token[0, 0]

    loss, grad_x, g = _local_step(x[0], p[0, 0], target[0], full, late_weights, early_grads, mid_grads, last_grad)

    others = _chip_exchange_wait(*last["flight"], grad_x, "w_in")
    others += _chip_exchange_wait(*early["flight"], others[0], "early")
    sums = last["sums"] + early["sums"]
    halves = [_reduce4(s[0], b, j, c, "reduce4_" + n) for s, b, n in zip(sums, others, _GROUP_NAMES)]
    grads = _sibling_share(halves)

    small_g = _small_allreduce(_pack_small(g, extra=loss, conv=g["conv_qk"]))
    conv_g = lax.dynamic_slice(small_g[_CONV_ROW:_CONV_ROW + CONV], (0, j * (D // 4)), (CONV, D // 4))

    ms, vs = shards(m), shards(v)
    upd = [_adamw_call(wa, ga, ma, va, "adamw_" + n)
           for wa, ga, ma, va, n in list(zip(_group(ws), grads, _group(ms), _group(vs), _GROUP_NAMES))[1:]]
    upd_in = _adamw_call(*[jnp.swapaxes(a, 0, 1) for a in (ws["w_in"], grads[0], ms["w_in"], vs["w_in"])], "adamw_w_in")
    upd = [[jnp.swapaxes(a, 0, 1) for a in upd_in]] + upd
    conv_upd = _adamw_call(ws["conv_qk"], conv_g, ms["conv_qk"], vs["conv_qk"], "adamw_conv")
    small_upd = _adamw_call(_pack_small(w), small_g, _pack_small(m), _pack_small(v), "adamw_small")

    shapes = {n: w[n].shape for n in _NAMES}
    res = []
    for k in range(4):
        big = _ungroup(list(grads) if k == 0 else [u[k - 1] for u in upd])
        big["conv_qk"] = conv_g if k == 0 else conv_upd[k - 1]
        leaves = _unpack_small(small_g if k == 0 else small_upd[k - 1], shapes)
        leaves.update({n: a.reshape(shapes[n]) for n, a in big.items()})
        res.append(leaves)

    out = [small_g[5, 8 + SWH], grad_x[None]]
    for k in range(4):
        out += [res[k][n] for n in _NAMES]
    return tuple(out)


def kernel(x, p, norm_mix_g, w_in, conv_qk, b_if, mlstm_norm_g, sinks, w_branch_a, w_branch_b, w_out, norm_mlp_g, w_up, w_down, norm_ple_g, w_ple_gate, w_ple_proj, final_norm_g, loss_target, m_norm_mix_g, m_w_in, m_conv_qk, m_b_if, m_mlstm_norm_g, m_sinks, m_w_branch_a, m_w_branch_b, m_w_out, m_norm_mlp_g, m_w_up, m_w_down, m_norm_ple_g, m_w_ple_gate, m_w_ple_proj, m_final_norm_g, v_norm_mix_g, v_w_in, v_conv_qk, v_b_if, v_mlstm_norm_g, v_sinks, v_w_branch_a, v_w_branch_b, v_w_out, v_norm_mlp_g, v_w_up, v_w_down, v_norm_ple_g, v_w_ple_gate, v_w_ple_proj, v_final_norm_g):
    w = dict(zip(_NAMES, (norm_mix_g, w_in, conv_qk, b_if, mlstm_norm_g, sinks, w_branch_a, w_branch_b, w_out,
                          norm_mlp_g, w_up, w_down, norm_ple_g, w_ple_gate, w_ple_proj, final_norm_g)))
    m = dict(zip(_NAMES, (m_norm_mix_g, m_w_in, m_conv_qk, m_b_if, m_mlstm_norm_g, m_sinks, m_w_branch_a,
                          m_w_branch_b, m_w_out, m_norm_mlp_g, m_w_up, m_w_down, m_norm_ple_g, m_w_ple_gate,
                          m_w_ple_proj, m_final_norm_g)))
    v = dict(zip(_NAMES, (v_norm_mix_g, v_w_in, v_conv_qk, v_b_if, v_mlstm_norm_g, v_sinks, v_w_branch_a,
                          v_w_branch_b, v_w_out, v_norm_mlp_g, v_w_up, v_w_down, v_norm_ple_g, v_w_ple_gate,
                          v_w_ple_proj, v_final_norm_g)))
    return _step(x, p, loss_target, w, m, v)
```

```python
import jax
import jax.numpy as jnp
from jax import lax
from jax.experimental import pallas as pl
from jax.experimental.pallas import tpu as pltpu

F32 = jnp.float32
BF16 = jnp.bfloat16

D = 1024
PLE = 256
MLH = 4
DQK = 128
DV = 256
CONV = 4
CHUNK = 128
SWH = 16
SWKV = 4
SWG = SWH // SWKV
HD = 64
WIN = 128
DFF = 4096
EPS = 1e-6
N_IN = 6664
NP = 7168
C_QK, C_V, C_O, C_QSW, C_GA, C_GB, C_KV, C_IF = 0, 1024, 2048, 3072, 4096, 5120, 6144, 6656
IFW = NP - C_IF

ADAM_LR = 0.001
ADAM_B1 = 0.9
ADAM_B2 = 0.999
ADAM_EPS = 1e-08
ADAM_WD = 0.01
ADAM_STEP = 10

TOK_TILE = 512
VMEM_LIMIT = 58 * 1024 * 1024


def _params(**kw):
    return pltpu.CompilerParams(vmem_limit_bytes=VMEM_LIMIT, **kw)


def _pick(n, cap):
    if n <= cap:
        return n
    t = cap - cap % 128
    while t > 128 and n % t:
        t -= 128
    assert n % t == 0, (n, cap)
    return t


def _dot(a, b, dims):
    return lax.dot_general(a, b, (dims, ((), ())), preferred_element_type=F32)


def _dot_nn(a, b):
    return _dot(a, b, ((1,), (0,)))


def _dot_nt(a, b):
    return _dot(a, b, ((1,), (1,)))


def _dot_tn(a, b):
    return _dot(a, b, ((0,), (0,)))


def _sigmoid(x):
    return 1.0 / (1.0 + jnp.exp(-x))


def _mm(a, b, mode, out_dtype, name, out_chunks=1):
    bch = b.shape[0] if b.ndim == 3 else 1
    brows, bcols = b.shape[-2], b.shape[-1] * bch
    if mode == "nn":
        (m, k), (k2, n) = a.shape, (brows, bcols)
    elif mode == "nt":
        (m, k), (n, k2) = a.shape, (brows, bcols)
    else:
        (k, m), (k2, n) = a.shape, (brows, bcols)
    assert k == k2, (a.shape, b.shape, mode)
    n_cap = n // max(out_chunks, 1 if mode == "nt" else bch)
    k_cap = k // bch if mode == "nt" else k
    tm, tn, tk = _pick(m, 1024), _pick(n_cap, 1024), _pick(k_cap, 2048)
    nk = k // tk
    if mode == "nn":
        a_spec = pl.BlockSpec((tm, tk), lambda i, j, kk: (i, kk))
        if bch > 1:
            bpc = (n // bch) // tn
            b_spec = pl.BlockSpec((None, tk, tn), lambda i, j, kk: (j // bpc, kk, j % bpc))
        else:
            b_spec = pl.BlockSpec((tk, tn), lambda i, j, kk: (kk, j))
        dot = _dot_nn
    elif mode == "nt":
        a_spec = pl.BlockSpec((tm, tk), lambda i, j, kk: (i, kk))
        if bch > 1:
            bpc = (k // bch) // tk
            b_spec = pl.BlockSpec((None, tn, tk), lambda i, j, kk: (kk // bpc, j, kk % bpc))
        else:
            b_spec = pl.BlockSpec((tn, tk), lambda i, j, kk: (j, kk))
        dot = _dot_nt
    else:
        assert bch == 1
        a_spec = pl.BlockSpec((tk, tm), lambda i, j, kk: (kk, i))
        b_spec = pl.BlockSpec((tk, tn), lambda i, j, kk: (kk, j))
        dot = _dot_tn
    if out_chunks > 1:
        npc = (n // out_chunks) // tn
        out_spec = pl.BlockSpec((None, tm, tn), lambda i, j, kk: (j // npc, i, j % npc))
        out_shape = jax.ShapeDtypeStruct((out_chunks, m, n // out_chunks), out_dtype)
    else:
        out_spec = pl.BlockSpec((tm, tn), lambda i, j, kk: (i, j))
        out_shape = jax.ShapeDtypeStruct((m, n), out_dtype)

    def body(a_ref, b_ref, o_ref, acc_ref):
        kk = pl.program_id(2)

        @pl.when(kk == 0)
        def _():
            acc_ref[...] = jnp.zeros_like(acc_ref)

        acc_ref[...] += dot(a_ref[...], b_ref[...])

        @pl.when(kk == nk - 1)
        def _():
            o_ref[...] = acc_ref[...].astype(out_dtype)

    return pl.pallas_call(
        body, name=name, grid=(m // tm, n // tn, nk),
        in_specs=[a_spec, b_spec], out_specs=out_spec, out_shape=out_shape,
        scratch_shapes=[pltpu.VMEM((tm, tn), F32)],
        compiler_params=_params(dimension_semantics=("parallel", "parallel", "arbitrary")),
    )(a, b)


def _tile(col0=0):
    return lambda tm, tn: pl.BlockSpec((tm, tn), lambda i, j, kk: (i, col0 // tn + j))


def _row():
    return lambda tm, tn: pl.BlockSpec((1, tn), lambda i, j, kk: (0, j))


def _mm_ep(pairs, mode, name, epilogue, ins, outs, tm, tn, aliases=None):
    a0, b0 = pairs[0]
    bch = b0.shape[0] if b0.ndim == 3 else 1
    m, k = a0.shape
    tm = _pick(m, tm)
    n = b0.shape[-1] * bch if mode == "nn" else b0.shape[-2]
    tk = _pick(k // bch if mode == "nt" else k, 2048)
    nk = k // tk
    a_spec = pl.BlockSpec((tm, tk), lambda i, j, kk: (i, kk))
    if mode == "nn":
        dot = _dot_nn
        if bch > 1:
            bpc = (n // bch) // tn
            b_spec = pl.BlockSpec((None, tk, tn), lambda i, j, kk: (j // bpc, kk, j % bpc))
        else:
            b_spec = pl.BlockSpec((tk, tn), lambda i, j, kk: (kk, j))
    else:
        dot = _dot_nt
        if bch > 1:
            bpc = (k // bch) // tk
            b_spec = pl.BlockSpec((None, tn, tk), lambda i, j, kk: (kk // bpc, j, kk % bpc))
        else:
            b_spec = pl.BlockSpec((tn, tk), lambda i, j, kk: (j, kk))
    npair, nin, nout = len(pairs), len(ins), len(outs)

    def body(*refs):
        ab = refs[:2 * npair]
        in_refs = refs[2 * npair:2 * npair + nin]
        out_refs = refs[2 * npair + nin:2 * npair + nin + nout]
        accs = refs[2 * npair + nin + nout:]
        i, j, kk = pl.program_id(0), pl.program_id(1), pl.program_id(2)
        for p in range(npair):
            prod = dot(ab[2 * p][...], ab[2 * p + 1][...])

            @pl.when(kk == 0)
            def _():
                accs[p][...] = prod

            @pl.when(kk > 0)
            def _():
                accs[p][...] += prod

        @pl.when(kk == nk - 1)
        def _():
            epilogue([acc[...] for acc in accs], in_refs, out_refs, i, j)

    operands = [x for pair in pairs for x in pair] + [a for a, _ in ins]
    io_alias = {2 * npair + i: o for i, o in (aliases or {}).items()}
    return pl.pallas_call(
        body, name=name, grid=(m // tm, n // tn, nk),
        in_specs=[a_spec, b_spec] * npair + [mk(tm, tn) for _, mk in ins],
        out_specs=[mk(tm, tn) for _, mk in outs], out_shape=[s for s, _ in outs],
        scratch_shapes=[pltpu.VMEM((tm, tn), F32)] * npair, input_output_aliases=io_alias,
        compiler_params=_params(dimension_semantics=("arbitrary", "arbitrary", "arbitrary")),
    )(*operands)


def _tok(w, j=0):
    return pl.BlockSpec((TOK_TILE, w), lambda i: (i, j))


def _rep(shape):
    return pl.BlockSpec(shape, lambda i: (0,) * len(shape))


def _rms(x):
    rstd = lax.rsqrt(jnp.mean(x * x, axis=-1, keepdims=True) + EPS)
    return x * rstd, rstd


def _rms_bwd(xn, rstd, dxn):
    return rstd * (dxn - xn * jnp.mean(dxn * xn, axis=-1, keepdims=True))


def _halo_prev(w, j=0, rows=8):
    r = TOK_TILE // rows
    return pl.BlockSpec((rows, w), lambda i: (jnp.maximum(i * r - 1, 0), j))


def _last8(halo_ref):
    return halo_ref[...].astype(F32)[halo_ref.shape[0] - 8:]


def _halo_next(w, nt, j=0):
    r = TOK_TILE // 8
    return pl.BlockSpec((8, w), lambda i: (jnp.minimum((i + 1) * r, nt * r - 1), j))


def _shift_down(x, halo, s):
    if s == 0:
        return x
    r = pltpu.roll(x, s, 0)
    hs = pltpu.roll(halo, s, 0)
    row = lax.broadcasted_iota(jnp.int32, hs.shape, 0)
    top = jnp.where(row < s, hs, r[0:8])
    return jnp.concatenate([top, r[8:]], axis=0)


def _shift_up(x, halo, s):
    if s == 0:
        return x
    n = x.shape[0]
    r = pltpu.roll(x, n - s, 0)
    hs = pltpu.roll(halo, 8 - s, 0)
    row = lax.broadcasted_iota(jnp.int32, hs.shape, 0)
    bot = jnp.where(row >= 8 - s, hs, r[n - 8:])
    return jnp.concatenate([r[:n - 8], bot], axis=0)


def _bf(x):
    return x.astype(BF16).astype(F32)


def _conv_taps(x, halo, w):
    x, halo, w = _bf(x), _bf(halo), _bf(w)
    acc = x * w[CONV - 1:CONV, :]
    for j in range(CONV - 1):
        acc = acc + _shift_down(x, halo, CONV - 1 - j) * w[j:j + 1, :]
    return acc


_Q_SCALE = DQK ** -0.5


def _qscale_row():
    lane = lax.broadcasted_iota(jnp.int32, (1, D), 1)
    return jnp.where(lane < MLH * DQK, _Q_SCALE, 1.0).astype(F32)


def _conv_silu_fwd(proj, conv_w):
    t = proj.shape[0]

    def body(x_ref, halo_ref, w_ref, o_ref):
        halo = jnp.where(pl.program_id(0) > 0, _last8(halo_ref), 0.0)
        c = _conv_taps(x_ref[...].astype(F32), halo, w_ref[...])
        o_ref[...] = (c * _sigmoid(c) * _qscale_row()).astype(BF16)

    return pl.pallas_call(
        body, name="conv_silu_fwd", grid=(t // TOK_TILE,),
        in_specs=[_tok(D, C_QK // D), _halo_prev(D, C_QK // D, 16), _rep((CONV, D))], out_specs=_tok(D),
        out_shape=jax.ShapeDtypeStruct((t, D), BF16), compiler_params=_params(),
    )(proj, proj, conv_w)


def _conv_silu_bwd_a(proj, conv_w, dqk):
    t = proj.shape[0]

    def body(x_ref, halo_ref, w_ref, d_ref, dc_ref, dw_ref):
        @pl.when(pl.program_id(0) == 0)
        def _():
            dw_ref[...] = jnp.zeros_like(dw_ref)

        halo = jnp.where(pl.program_id(0) > 0, _last8(halo_ref), 0.0)
        x = x_ref[...].astype(F32)
        c = _conv_taps(x, halo, w_ref[...])
        s = _sigmoid(c)
        dc = d_ref[...] * _qscale_row() * (s * (1.0 + c * (1.0 - s)))
        dc_ref[...] = dc
        dcb, xb, halo_b = _bf(dc), _bf(x), _bf(halo)
        for j in range(CONV):
            dw_ref[j:j + 1, :] += jnp.sum(dcb * _shift_down(xb, halo_b, CONV - 1 - j), axis=0, keepdims=True)

    return pl.pallas_call(
        body, name="conv_silu_bwd_a", grid=(t // TOK_TILE,),
        in_specs=[_tok(D, C_QK // D), _halo_prev(D, C_QK // D, 16), _rep((CONV, D)), _tok(D)],
        out_specs=[_tok(D), _rep((CONV, D))],
        out_shape=[jax.ShapeDtypeStruct((t, D), F32), jax.ShapeDtypeStruct((CONV, D), F32)],
        compiler_params=_params(),
    )(proj, proj, conv_w, dqk)


def _conv_silu_bwd_b(dc, conv_w, dproj):
    t = dc.shape[0]
    nt = t // TOK_TILE

    def body(dc_ref, halo_ref, w_ref, _, dx_ref):
        halo = _bf(jnp.where(pl.program_id(0) < nt - 1, halo_ref[...], 0.0))
        dcv = _bf(dc_ref[...])
        w = _bf(w_ref[...])
        acc = dcv * w[CONV - 1:CONV, :]
        for j in range(CONV - 1):
            acc = acc + _shift_up(dcv, halo, CONV - 1 - j) * w[j:j + 1, :]
        dx_ref[...] = acc.astype(BF16)

    return pl.pallas_call(
        body, name="conv_silu_bwd_b", grid=(nt,), in_specs=[_tok(D), _halo_next(D, nt), _rep((CONV, D)), _ANY],
        out_specs=_tok(D, C_QK // D), out_shape=jax.ShapeDtypeStruct((t, NP), BF16),
        input_output_aliases={3: 0}, compiler_params=_params(),
    )(dc, dc, conv_w, dproj)


def _gates_fwd(pre_rows, bias_col):
    t = pre_rows.shape[1]

    def body(p_ref, b_ref, g_ref, s_ref):
        z = p_ref[...] + b_ref[...]
        lf = jnp.minimum(z, 0.0) - jnp.log(1.0 + jnp.exp(-jnp.abs(z)))
        lane = lax.broadcasted_iota(jnp.int32, z.shape, 1) % CHUNK
        cum = lf
        s = 1
        while s < CHUNK:
            cum = cum + jnp.where(lane >= s, pltpu.roll(cum, s, 1), 0.0)
            s *= 2
        sub = lax.broadcasted_iota(jnp.int32, z.shape, 0)
        g_ref[...] = jnp.where(sub < MLH, z, cum)
        s_ref[...] = _sigmoid(-z)

    return pl.pallas_call(
        body, name="gates_fwd",
        out_shape=[jax.ShapeDtypeStruct((8, t), F32), jax.ShapeDtypeStruct((8, t), F32)],
        compiler_params=_params(),
    )(pre_rows, bias_col)


def _chunk_terms(grow, gcol, m0):
    heads = range(MLH)
    i_row = [grow[h:h + 1, :] for h in heads]
    b_row = [grow[MLH + h:MLH + h + 1, :] for h in heads]
    i_col = [gcol[:, h:h + 1] for h in heads]
    b_col = [gcol[:, MLH + h:MLH + h + 1] for h in heads]
    b_last = [b_row[h][:, CHUNK - 1:CHUNK] for h in heads]
    tt = lax.broadcasted_iota(jnp.int32, (CHUNK, CHUNK), 0)
    ss = lax.broadcasted_iota(jnp.int32, (CHUNK, CHUNK), 1)
    log_d = [jnp.where(tt >= ss, b_col[h] - b_row[h] + i_row[h], -jnp.inf) for h in heads]
    row_max = [jnp.max(log_d[h], axis=1, keepdims=True) for h in heads]
    last_max = [jnp.max(b_last[h] - b_row[h] + i_row[h], axis=1, keepdims=True) for h in heads]
    m_t = [jnp.maximum(b_col[h] + m0[h], row_max[h]) for h in heads]
    m1 = [jnp.maximum(b_last[h] + m0[h], last_max[h]) for h in heads]
    dm = [jnp.exp(log_d[h] - m_t[h]) for h in heads]
    wi = [jnp.exp(b_col[h] + m0[h] - m_t[h]) for h in heads]
    ws = [jnp.exp(b_last[h] - b_col[h] + i_col[h] - m1[h]) for h in heads]
    dec = [jnp.exp(b_last[h] + m0[h] - m1[h]) for h in heads]
    return [(dm[h], wi[h], m_t[h], ws[h], dec[h], m1[h]) for h in heads]


def _mlstm_fwd(qk, proj, grow, gcol, gain):
    t = qk.shape[0]
    nc = t // CHUNK

    def body(qk_ref, v_ref, o_ref, grow_ref, gcol_ref, g_ref, h_ref, y_ref, cs_ref, st_ref, c_scr, st_scr):
        @pl.when(pl.program_id(0) == 0)
        def _():
            c_scr[...] = jnp.zeros_like(c_scr)
            st_scr[...] = jnp.zeros_like(st_scr)

        grow_v, gcol_v = grow_ref[...], gcol_ref[...]
        heads = range(MLH)
        q = [qk_ref[:, h * DQK:(h + 1) * DQK] for h in heads]
        k = [qk_ref[:, MLH * DQK + h * DQK:MLH * DQK + (h + 1) * DQK] for h in heads]
        v = [v_ref[:, h * DV:(h + 1) * DV] for h in heads]
        c0 = [c_scr[h] for h in heads]
        n0 = [st_scr[h, 0:1, :] for h in heads]
        for h in heads:
            cs_ref[0, h] = c0[h]
            st_ref[0, h] = st_scr[h]
        terms = _chunk_terms(grow_v, gcol_v, [st_scr[h, 1:2, 0:1] for h in heads])
        a = [_dot_nt(q[h], k[h]) for h in heads]
        qc = [_dot_nt(q[h], c0[h].astype(BF16)) for h in heads]
        s = [a[h] * terms[h][0] for h in heads]
        sv = [_dot_nn(s[h].astype(BF16), v[h]) for h in heads]
        upd = [_dot_tn((terms[h][3] * v[h]).astype(BF16), k[h]) for h in heads]
        den = [terms[h][1] * jnp.sum(q[h].astype(F32) * n0[h], axis=1, keepdims=True)
               + jnp.sum(s[h], axis=1, keepdims=True) for h in heads]
        hv = [(terms[h][1] * qc[h] + sv[h]) / jnp.maximum(jnp.abs(den[h]), jnp.exp(-terms[h][2])) for h in heads]
        for h in heads:
            sl = slice(h * DV, (h + 1) * DV)
            h_ref[:, sl] = hv[h]
            xn, _ = _rms(hv[h])
            y_ref[:, sl] = (_sigmoid(o_ref[:, sl].astype(F32)) * xn * g_ref[:, sl]).astype(BF16)
        for h in heads:
            dec, m1 = terms[h][4], terms[h][5]
            c_scr[h] = dec * c0[h] + upd[h]
            st_scr[h, 0:1, :] = dec * n0[h] + jnp.sum(terms[h][3] * k[h].astype(F32), axis=0, keepdims=True)
            st_scr[h, 1:2, :] = jnp.broadcast_to(m1, (1, DQK))

    return pl.pallas_call(
        body, name="mlstm_fwd", grid=(nc,),
        in_specs=[pl.BlockSpec((CHUNK, D), lambda c: (c, 0)), pl.BlockSpec((CHUNK, D), lambda c: (c, C_V // D)),
                  pl.BlockSpec((CHUNK, D), lambda c: (c, C_O // D)),
                  pl.BlockSpec((8, CHUNK), lambda c: (0, c)), pl.BlockSpec((CHUNK, 8), lambda c: (c, 0)),
                  pl.BlockSpec((1, D), lambda c: (0, 0))],
        out_specs=[pl.BlockSpec((CHUNK, D), lambda c: (c, 0)), pl.BlockSpec((CHUNK, D), lambda c: (c, 0)),
                   pl.BlockSpec((1, MLH, DV, DQK), lambda c: (c, 0, 0, 0)),
                   pl.BlockSpec((1, MLH, 8, DQK), lambda c: (c, 0, 0, 0))],
        out_shape=[jax.ShapeDtypeStruct((t, D), F32), jax.ShapeDtypeStruct((t, D), BF16),
                   jax.ShapeDtypeStruct((nc, MLH, DV, DQK), F32), jax.ShapeDtypeStruct((nc, MLH, 8, DQK), F32)],
        scratch_shapes=[pltpu.VMEM((MLH, DV, DQK), F32), pltpu.VMEM((MLH, 8, DQK), F32)],
        compiler_params=_params(dimension_semantics=("arbitrary",)),
    )(qk, proj, proj, grow, gcol, gain)


def _mlstm_bwd(qk, proj, grow, gcol, sneg_col, cs, st, hraw, dh, dproj):
    t = qk.shape[0]
    nc = t // CHUNK

    def rev(c):
        return nc - 1 - c

    def nxt(c):
        return jnp.minimum(nc - c, nc - 1)

    def body(qk_ref, v_ref, grow_ref, gcol_ref, sneg_ref, cs_ref, st_ref, cs1_ref, st1_ref, h_ref, dh_ref, _,
             dqk_ref, dv_ref, dif_ref, dbif_ref, dc_scr, dn_scr):
        @pl.when(pl.program_id(0) == 0)
        def _():
            dc_scr[...] = jnp.zeros_like(dc_scr)
            dn_scr[...] = jnp.zeros_like(dn_scr)
            dbif_ref[...] = jnp.zeros_like(dbif_ref)

        grow_v, gcol_v, sneg = grow_ref[...], gcol_ref[...], sneg_ref[...]
        tt = lax.broadcasted_iota(jnp.int32, (CHUNK, CHUNK), 0)
        ss = lax.broadcasted_iota(jnp.int32, (CHUNK, CHUNK), 1)
        lane8 = lax.broadcasted_iota(jnp.int32, (CHUNK, 8), 1)
        heads = range(MLH)
        q = [qk_ref[:, h * DQK:(h + 1) * DQK] for h in heads]
        k = [qk_ref[:, MLH * DQK + h * DQK:MLH * DQK + (h + 1) * DQK] for h in heads]
        qf, kf = [a.astype(F32) for a in q], [a.astype(F32) for a in k]
        vb = [v_ref[:, h * DV:(h + 1) * DV].astype(BF16) for h in heads]
        c0 = [cs_ref[0, h] for h in heads]
        n0 = [st_ref[0, h, 0:1, :] for h in heads]
        dc1 = [dc_scr[h] for h in heads]
        dn1 = [dn_scr[h, 0:1, :] for h in heads]
        terms = _chunk_terms(grow_v, gcol_v, [st_ref[0, h, 1:2, 0:1] for h in heads])
        dm, wi, ws = [t[0] for t in terms], [t[1] for t in terms], [t[3] for t in terms]
        s = [_dot_nt(q[h], k[h]) * dm[h] for h in heads]
        den = [wi[h] * jnp.sum(qf[h] * n0[h], axis=1, keepdims=True) + jnp.sum(s[h], axis=1, keepdims=True)
               for h in heads]
        floor = [jnp.exp(-terms[h][2]) for h in heads]
        g = [jnp.maximum(jnp.abs(den[h]), floor[h]) for h in heads]
        dh_v = [dh_ref[:, h * DV:(h + 1) * DV] for h in heads]
        dnum = [dh_v[h] / g[h] for h in heads]
        dden = [-jnp.sum(dh_v[h] * h_ref[:, h * DV:(h + 1) * DV], axis=1, keepdims=True) / g[h] for h in heads]
        dden = [jnp.where(jnp.abs(den[h]) > floor[h], dden[h] * jnp.sign(den[h]), 0.0) for h in heads]
        dnum_b = [a.astype(BF16) for a in dnum]
        dc1_b = [a.astype(BF16) for a in dc1]
        da = [((_dot_nt(dnum_b[h], vb[h]) + dden[h]) * dm[h]).astype(BF16) for h in heads]
        dq_inter = [_dot_nn(dnum_b[h], c0[h].astype(BF16)) for h in heads]
        dk_inter = [_dot_nn(vb[h], dc1_b[h]) for h in heads]
        dv_inter = [_dot_nt(k[h], dc1_b[h]) for h in heads]
        dc_new = [_dot_tn((wi[h] * dnum[h]).astype(BF16), q[h]) for h in heads]
        dq = [_dot_nn(da[h], k[h]) + wi[h] * (dq_inter[h] + dden[h] * n0[h]) for h in heads]
        dk = [_dot_tn(da[h], q[h]) + ws[h] * (dk_inter[h] + dn1[h]) for h in heads]
        dv = [_dot_tn(s[h].astype(BF16), dnum_b[h]) + ws[h] * dv_inter[h] for h in heads]
        for h in heads:
            dqk_ref[:, h * DQK:(h + 1) * DQK] = dq[h]
            dqk_ref[:, MLH * DQK + h * DQK:MLH * DQK + (h + 1) * DQK] = dk[h]
            dv_ref[:, h * DV:(h + 1) * DV] = dv[h].astype(BF16)
        rk = [jnp.sum(kf[h] * dk[h], axis=1, keepdims=True) for h in heads]
        df = [jnp.sum(qf[h] * dq[h], axis=1, keepdims=True) - rk[h] for h in heads]
        df_row = [jnp.sum(jnp.where(tt == ss, df[h], 0.0), axis=0, keepdims=True) for h in heads]
        suffix = [jnp.sum(jnp.where(ss >= tt, df_row[h], 0.0), axis=1, keepdims=True) for h in heads]
        cross = [jnp.sum(jnp.sum(dc1[h] * cs1_ref[0, h], axis=0, keepdims=True), axis=1, keepdims=True)
                 + jnp.sum(dn1[h] * st1_ref[0, h, 0:1, :], axis=1, keepdims=True) for h in heads]
        dif = jnp.zeros((CHUNK, 8), F32)
        for h in heads:
            dpf = (suffix[h] + cross[h]) * sneg[:, MLH + h:MLH + h + 1]
            dif = dif + jnp.where(lane8 == h, rk[h], 0.0) + jnp.where(lane8 == MLH + h, dpf, 0.0)
            dc_scr[h] = terms[h][4] * dc1[h] + dc_new[h]
            dn_scr[h, 0:1, :] = terms[h][4] * dn1[h] + jnp.sum(wi[h] * dden[h] * qf[h], axis=0, keepdims=True)
        dif_ref[...] = dif
        dbif_ref[...] += jnp.sum(dif, axis=0, keepdims=True)

    return pl.pallas_call(
        body, name="mlstm_bwd", grid=(nc,),
        in_specs=[pl.BlockSpec((CHUNK, D), lambda c: (rev(c), 0)),
                  pl.BlockSpec((CHUNK, D), lambda c: (rev(c), C_V // D)),
                  pl.BlockSpec((8, CHUNK), lambda c: (0, rev(c))),
                  pl.BlockSpec((CHUNK, 8), lambda c: (rev(c), 0)),
                  pl.BlockSpec((CHUNK, 8), lambda c: (rev(c), 0)),
                  pl.BlockSpec((1, MLH, DV, DQK), lambda c: (rev(c), 0, 0, 0)),
                  pl.BlockSpec((1, MLH, 8, DQK), lambda c: (rev(c), 0, 0, 0)),
                  pl.BlockSpec((1, MLH, DV, DQK), lambda c: (nxt(c), 0, 0, 0)),
                  pl.BlockSpec((1, MLH, 8, DQK), lambda c: (nxt(c), 0, 0, 0)),
                  pl.BlockSpec((CHUNK, D), lambda c: (rev(c), 0)),
                  pl.BlockSpec((CHUNK, D), lambda c: (rev(c), 0)), _ANY],
        out_specs=[pl.BlockSpec((CHUNK, D), lambda c: (rev(c), 0)),
                   pl.BlockSpec((CHUNK, D), lambda c: (rev(c), C_V // D)),
                   pl.BlockSpec((CHUNK, 8), lambda c: (rev(c), 0)),
                   pl.BlockSpec((1, 8), lambda c: (0, 0))],
        out_shape=[jax.ShapeDtypeStruct((t, D), F32), jax.ShapeDtypeStruct((t, NP), BF16),
                   jax.ShapeDtypeStruct((t, 8), F32), jax.ShapeDtypeStruct((1, 8), F32)],
        scratch_shapes=[pltpu.VMEM((MLH, DV, DQK), F32), pltpu.VMEM((MLH, 8, DQK), F32)],
        input_output_aliases={11: 1}, compiler_params=_params(dimension_semantics=("arbitrary",)),
    )(qk, proj, grow, gcol, sneg_col, cs, st, cs, st, hraw, dh, dproj)


_ANY = pl.BlockSpec(memory_space=pl.ANY)


_SW_SCALE = HD ** -0.5
_KVB = C_KV // (2 * SWKV * HD)


def _swa_mask(n):
    ki = lax.broadcasted_iota(jnp.int32, (2 * WIN, SWG * WIN), 0)
    qi = lax.broadcasted_iota(jnp.int32, (2 * WIN, SWG * WIN), 1) % WIN
    return (ki > qi) & (ki <= qi + WIN) & ((n > 0) | (ki >= WIN))


def _group_rows(x_ref, hk):
    return jnp.concatenate([x_ref[:, (hk * SWG + g) * HD:(hk * SWG + g + 1) * HD] for g in range(SWG)], axis=0)


def _group_lanes(x_ref, hk):
    return jnp.concatenate([x_ref[hk * SWG + g:hk * SWG + g + 1, :] for g in range(SWG)], axis=1)


def _sink_lanes(sink_ref, hk):
    return jnp.concatenate([jnp.broadcast_to(sink_ref[:, hk * SWG + g:hk * SWG + g + 1], (1, WIN))
                            for g in range(SWG)], axis=1)


def _swa_fwd(proj, sinks):
    t = proj.shape[0]
    nb = t // WIN

    def body(q_ref, kvc_ref, kvp_ref, sink_ref, y_ref, lse_ref):
        valid = _swa_mask(pl.program_id(0))
        kvh = range(SWKV)
        kb = [jnp.concatenate([kvp_ref[:, hk * HD:(hk + 1) * HD], kvc_ref[:, hk * HD:(hk + 1) * HD]],
                              axis=0).astype(BF16) for hk in kvh]
        vb = [jnp.concatenate([kvp_ref[:, (SWKV + hk) * HD:(SWKV + hk + 1) * HD],
                               kvc_ref[:, (SWKV + hk) * HD:(SWKV + hk + 1) * HD]], axis=0).astype(BF16) for hk in kvh]
        sink = [_sink_lanes(sink_ref, hk) for hk in kvh]
        logits = [_dot_nt(kb[hk], _group_rows(q_ref, hk).astype(BF16)) for hk in kvh]
        logits = [jnp.where(valid, logits[hk] * _SW_SCALE, -jnp.inf) for hk in kvh]
        m = [jnp.maximum(jnp.max(logits[hk], axis=0, keepdims=True), sink[hk]) for hk in kvh]
        p = [jnp.exp(logits[hk] - m[hk]) for hk in kvh]
        denom = [jnp.sum(p[hk], axis=0, keepdims=True) + jnp.exp(sink[hk] - m[hk]) for hk in kvh]
        y4 = [_dot_tn((p[hk] / denom[hk]).astype(BF16), vb[hk]).astype(BF16) for hk in kvh]
        for hk in kvh:
            lse4 = m[hk] + jnp.log(denom[hk])
            for g in range(SWG):
                hq = hk * SWG + g
                y_ref[:, hq * HD:(hq + 1) * HD] = y4[hk][g * WIN:(g + 1) * WIN]
                lse_ref[hq:hq + 1, :] = lse4[:, g * WIN:(g + 1) * WIN]

    return pl.pallas_call(
        body, name="swa_fwd", grid=(nb,),
        in_specs=[pl.BlockSpec((WIN, D), lambda n: (n, C_QSW // D)),
                  pl.BlockSpec((WIN, 512), lambda n: (n, _KVB)),
                  pl.BlockSpec((WIN, 512), lambda n: (jnp.maximum(n - 1, 0), _KVB)),
                  pl.BlockSpec((1, SWH), lambda n: (0, 0))],
        out_specs=[pl.BlockSpec((WIN, D), lambda n: (n, 0)), pl.BlockSpec((SWH, WIN), lambda n: (0, n))],
        out_shape=[jax.ShapeDtypeStruct((t, D), BF16), jax.ShapeDtypeStruct((SWH, t), F32)],
        compiler_params=_params(),
    )(proj, proj, proj, sinks)


def _swa_bwd(proj, sinks, lse, dyb, dproj):
    t = proj.shape[0]
    nb = t // WIN

    def body(q_ref, kvc_ref, kvp_ref, sink_ref, lse_ref, dy_ref, _, dq_ref, dself_ref, dprev_ref, ds_ref):
        @pl.when(pl.program_id(0) == 0)
        def _():
            ds_ref[...] = jnp.zeros_like(ds_ref)

        valid = _swa_mask(pl.program_id(0))
        kvh = range(SWKV)
        ks = [slice(hk * HD, (hk + 1) * HD) for hk in kvh]
        vs = [slice(SWKV * HD + hk * HD, SWKV * HD + (hk + 1) * HD) for hk in kvh]
        kb = [jnp.concatenate([kvp_ref[:, ks[hk]], kvc_ref[:, ks[hk]]], axis=0).astype(BF16) for hk in kvh]
        vb = [jnp.concatenate([kvp_ref[:, vs[hk]], kvc_ref[:, vs[hk]]], axis=0).astype(BF16) for hk in kvh]
        qb = [_group_rows(q_ref, hk).astype(BF16) for hk in kvh]
        dyb_ = [_group_rows(dy_ref, hk).astype(BF16) for hk in kvh]
        lse4 = [_group_lanes(lse_ref, hk) for hk in kvh]
        logits = [_dot_nt(kb[hk], qb[hk]) for hk in kvh]
        dpt = [_dot_nt(vb[hk], dyb_[hk]) for hk in kvh]
        p = [jnp.exp(jnp.where(valid, logits[hk] * _SW_SCALE, -jnp.inf) - lse4[hk]) for hk in kvh]
        delta = [jnp.sum(p[hk] * dpt[hk], axis=0, keepdims=True) for hk in kvh]
        dsm = [(p[hk] * (dpt[hk] - delta[hk])).astype(BF16) for hk in kvh]
        dq4 = [(_dot_tn(dsm[hk], kb[hk]) * _SW_SCALE).astype(BF16) for hk in kvh]
        dkb = [_dot_nn(dsm[hk], qb[hk]) * _SW_SCALE for hk in kvh]
        dvb = [_dot_nn(p[hk].astype(BF16), dyb_[hk]) for hk in kvh]
        for hk in kvh:
            dsink4 = jnp.exp(_sink_lanes(sink_ref, hk) - lse4[hk]) * delta[hk]
            for g in range(SWG):
                hq = hk * SWG + g
                dq_ref[:, hq * HD:(hq + 1) * HD] = dq4[hk][g * WIN:(g + 1) * WIN]
                ds_ref[:, hq:hq + 1] += -jnp.sum(dsink4[:, g * WIN:(g + 1) * WIN], axis=1, keepdims=True)
            dprev_ref[:, ks[hk]] = dkb[hk][:WIN]
            dself_ref[:, ks[hk]] = dkb[hk][WIN:]
            dprev_ref[:, vs[hk]] = dvb[hk][:WIN]
            dself_ref[:, vs[hk]] = dvb[hk][WIN:]

    return pl.pallas_call(
        body, name="swa_bwd", grid=(nb,),
        in_specs=[pl.BlockSpec((WIN, D), lambda n: (n, C_QSW // D)),
                  pl.BlockSpec((WIN, 512), lambda n: (n, _KVB)),
                  pl.BlockSpec((WIN, 512), lambda n: (jnp.maximum(n - 1, 0), _KVB)),
                  pl.BlockSpec((1, SWH), lambda n: (0, 0)),
                  pl.BlockSpec((SWH, WIN), lambda n: (0, n)),
                  pl.BlockSpec((WIN, D), lambda n: (n, 0)), _ANY],
        out_specs=[pl.BlockSpec((WIN, D), lambda n: (n, C_QSW // D)), pl.BlockSpec((WIN, 512), lambda n: (n, 0)),
                   pl.BlockSpec((WIN, 512), lambda n: (jnp.maximum(n - 1, 0), 0)),
                   pl.BlockSpec((1, SWH), lambda n: (0, 0))],
        out_shape=[jax.ShapeDtypeStruct((t, NP), BF16), jax.ShapeDtypeStruct((t, 512), F32),
                   jax.ShapeDtypeStruct((t, 512), F32), jax.ShapeDtypeStruct((1, SWH), F32)],
        input_output_aliases={6: 0}, compiler_params=_params(),
    )(proj, proj, proj, sinks, lse, dyb, dproj)


def _kv_combine(dself, dnext, dif, dproj):
    t = dself.shape[0]
    rows = _pick(t, 512)

    def body(a_ref, b_ref, dif_ref, _, o_ref):
        row = pl.program_id(0) * rows + lax.broadcasted_iota(jnp.int32, (rows, 1), 0)
        o_ref[:, 0:512] = (a_ref[...] + jnp.where(row < t - WIN, b_ref[...], 0.0)).astype(BF16)
        lane = lax.broadcasted_iota(jnp.int32, (rows, 128), 1)
        dif_v = dif_ref[...]
        first = jnp.zeros((rows, 128), F32)
        for col in range(8):
            first = first + jnp.where(lane == col, dif_v[:, col:col + 1], 0.0)
        o_ref[:, 512:640] = first.astype(BF16)
        o_ref[:, 640:512 + IFW] = jnp.zeros((rows, IFW - 128), BF16)

    return pl.pallas_call(
        body, name="kv_combine", grid=(t // rows,),
        in_specs=[pl.BlockSpec((rows, 512), lambda n: (n, 0)), pl.BlockSpec((rows, 512), lambda n: (n, 0)),
                  pl.BlockSpec((rows, 8), lambda n: (n, 0)), _ANY],
        out_specs=pl.BlockSpec((rows, 512 + IFW), lambda n: (n, C_KV // (512 + IFW))),
        out_shape=jax.ShapeDtypeStruct((t, NP), BF16), input_output_aliases={3: 0}, compiler_params=_params(),
    )(dself, dnext, dif, dproj)


def _sds(t, n, dtype):
    return jax.ShapeDtypeStruct((t, n), dtype)


def _proj_in(x, gain, w_in):
    t = x.shape[0]
    tm, tn = _pick(t, 1024), 2 * IFW

    def body(x_ref, g_ref, w_ref, h_ref, p_ref, gate_ref, h_scr):
        j = pl.program_id(1)

        @pl.when(j == 0)
        def _():
            xn, _ = _rms(x_ref[...])
            h = (xn * g_ref[...]).astype(BF16)
            h_scr[...] = h
            h_ref[...] = h

        acc = _dot_nn(h_scr[...], w_ref[...])
        p_ref[...] = acc.astype(BF16)

        @pl.when(j == C_IF // tn)
        def _():
            gate_ref[...] = acc[:, C_IF % tn:C_IF % tn + 128]

    return pl.pallas_call(
        body, name="mm_in", grid=(t // tm, NP // tn),
        in_specs=[pl.BlockSpec((tm, D), lambda i, j: (i, 0)), pl.BlockSpec((1, D), lambda i, j: (0, 0)),
                  pl.BlockSpec((D, tn), lambda i, j: (0, j))],
        out_specs=[pl.BlockSpec((tm, D), lambda i, j: (i, 0)), pl.BlockSpec((tm, tn), lambda i, j: (i, j)),
                   pl.BlockSpec((tm, 128), lambda i, j: (i, 0))],
        out_shape=[_sds(t, D, BF16), _sds(t, NP, BF16), _sds(t, 128, F32)],
        scratch_shapes=[pltpu.VMEM((tm, D), BF16)],
        compiler_params=_params(dimension_semantics=("arbitrary", "arbitrary")),
    )(x, gain, w_in)


def _branch_merge(ya, yb, wa, wb, proj):
    t = ya.shape[0]

    def epilogue(accs, ins, outs, i, j):
        za, zb = accs
        merged = _sigmoid(ins[0][...].astype(F32)) * za + _sigmoid(ins[1][...].astype(F32)) * zb
        outs[0][...] = merged.astype(BF16)
        outs[1][...] = za.astype(BF16)
        outs[2][...] = zb.astype(BF16)

    return _mm_ep([(ya, wa), (yb, wb)], "nn", "mm_branch_merge", epilogue, [(proj, _tile(C_GA)), (proj, _tile(C_GB))],
                  [(_sds(t, D, BF16), _tile())] * 3, 1024, 1024)


def _dmerged_bwd(dxb, w_out, proj, za, zb):
    t = dxb.shape[0]

    def epilogue(accs, ins, outs, i, j):
        dm = accs[0]
        sa, sb = _sigmoid(ins[0][...].astype(F32)), _sigmoid(ins[1][...].astype(F32))
        outs[0][...] = (dm * sa).astype(BF16)
        outs[1][...] = (dm * sb).astype(BF16)
        outs[2][:, 0:D] = (dm * ins[2][...].astype(F32) * sa * (1.0 - sa)).astype(BF16)
        outs[2][:, D:2 * D] = (dm * ins[3][...].astype(F32) * sb * (1.0 - sb)).astype(BF16)

    gate_cols = lambda tm, tn: pl.BlockSpec((tm, 2 * D), lambda i, j, kk: (i, C_GA // (2 * D)))
    return _mm_ep([(dxb, w_out)], "nt", "mm_dmerged_bwd", epilogue,
                  [(proj, _tile(C_GA)), (proj, _tile(C_GB)), (za, _tile()), (zb, _tile())],
                  [(_sds(t, D, BF16), _tile()), (_sds(t, D, BF16), _tile()), (_sds(t, NP, BF16), gate_cols)], 1024, D)


def _dya_bwd(dza, wa, hraw, proj, g, dproj):
    t = dza.shape[0]

    def epilogue(accs, ins, outs, i, j):
        h_ref, o_ref, g_ref, _ = ins
        dh_ref, do_ref, dg_ref = outs

        @pl.when(i == 0)
        def _():
            dg_ref[...] = jnp.zeros_like(dg_ref)

        dy = accs[0]
        so = _sigmoid(o_ref[...].astype(F32))
        for h in range(MLH):
            sl = slice(h * DV, (h + 1) * DV)
            xn, rstd = _rms(h_ref[:, sl])
            gs = g_ref[:, sl]
            do_ref[:, sl] = (dy[:, sl] * xn * gs * so[:, sl] * (1.0 - so[:, sl])).astype(BF16)
            dhn = dy[:, sl] * so[:, sl]
            dg_ref[:, sl] += jnp.sum(dhn * xn, axis=0, keepdims=True)
            dh_ref[:, sl] = _rms_bwd(xn, rstd, dhn * gs)

    return _mm_ep([(dza, wa)], "nt", "mm_dya_bwd", epilogue,
                  [(hraw, _tile()), (proj, _tile(C_O)), (g, _row()), (dproj, lambda tm, tn: _ANY)],
                  [(_sds(t, D, F32), _tile()), (_sds(t, NP, BF16), _tile(C_O)), (_sds(1, D, F32), _row())],
                  1024, D, aliases={3: 1})


def _up_act(hn, w_up):
    t = hn.shape[0]

    def epilogue(accs, ins, outs, i, j):
        r = jnp.maximum(accs[0], 0.0)
        outs[0][...] = (r * r).astype(BF16)
        outs[1][...] = accs[0].astype(BF16)

    return _mm_ep([(hn, w_up)], "nn", "mm_up_act", epilogue, [],
                  [(_sds(t, DFF, BF16), _tile()), (_sds(t, DFF, BF16), _tile())], 1024, 1024)


def _da_du(dxb, w_down, u):
    t = dxb.shape[0]

    def epilogue(accs, ins, outs, i, j):
        outs[0][...] = (accs[0] * 2.0 * jnp.maximum(ins[0][...].astype(F32), 0.0)).astype(BF16)

    return _mm_ep([(dxb, w_down)], "nt", "mm_da_du", epilogue, [(u, _tile())], [(_sds(t, DFF, BF16), _tile())],
                  1024, 1024)[0]


def _resid_norm_mm(a, w, x, g, name):
    t = x.shape[0]

    def epilogue(accs, ins, outs, i, j):
        x1 = ins[0][...] + accs[0]
        outs[0][...] = x1
        xn, _ = _rms(x1)
        outs[1][...] = (xn * ins[1][...]).astype(BF16)

    return _mm_ep([(a, w)], "nn", name, epilogue, [(x, _tile()), (g, _row())],
                  [(_sds(t, D, F32), _tile()), (_sds(t, D, BF16), _tile())], 1024, D)


def _norm_bwd_mm(dy, w, x, g, dres, name):
    t = x.shape[0]

    def epilogue(accs, ins, outs, i, j):
        @pl.when(i == 0)
        def _():
            outs[2][...] = jnp.zeros_like(outs[2])

        dh = accs[0]
        xn, rstd = _rms(ins[0][...])
        outs[2][...] += jnp.sum(dh * xn, axis=0, keepdims=True)
        dx = ins[2][...] + _rms_bwd(xn, rstd, dh * ins[1][...])
        outs[0][...] = dx
        outs[1][...] = dx.astype(BF16)

    return _mm_ep([(dy, w)], "nt", name, epilogue, [(x, _tile()), (g, _row()), (dres, _tile())],
                  [(_sds(t, D, F32), _tile()), (_sds(t, D, BF16), _tile()), (_sds(1, D, F32), _row())], 1024, D)


def _ple_final_mm(hn2, w_gate, x2, pp, target, gf):
    t = x2.shape[0]

    def epilogue(accs, ins, outs, i, j):
        loss_ref, dg_ref, dx_ref, dpp_ref, dgp_ref = outs

        @pl.when(i == 0)
        def _():
            loss_ref[...] = jnp.zeros_like(loss_ref)
            dg_ref[...] = jnp.zeros_like(dg_ref)

        gate = _sigmoid(accs[0])
        pp_v = ins[1][...]
        x3 = ins[0][...] + gate * pp_v
        xn, rstd = _rms(x3)
        gf_v = ins[3][...]
        err = xn * gf_v - ins[2][...]
        loss_ref[...] += (0.5 / D) * jnp.sum(jnp.sum(err * err, axis=1, keepdims=True), axis=0, keepdims=True)
        dy = err * (1.0 / D)
        dg_ref[...] += jnp.sum(dy * xn, axis=0, keepdims=True)
        dx3 = _rms_bwd(xn, rstd, dy * gf_v)
        dx_ref[...] = dx3
        dpp_ref[...] = (dx3 * gate).astype(BF16)
        dgp_ref[...] = (dx3 * pp_v * gate * (1.0 - gate)).astype(BF16)

    one = lambda tm, tn: pl.BlockSpec((1, 1), lambda i, j, kk: (0, 0))
    return _mm_ep([(hn2, w_gate)], "nn", "mm_ple_final", epilogue,
                  [(x2, _tile()), (pp, _tile()), (target, _tile()), (gf, _row())],
                  [(_sds(1, 1, F32), one), (_sds(1, D, F32), _row()), (_sds(t, D, F32), _tile()),
                   (_sds(t, D, BF16), _tile()), (_sds(t, D, BF16), _tile())], 512, D)


_WIN_SEGMENTS = ((0, 3072, C_QK), (3072, 8, C_IF), (3080, 1024, C_QSW), (4104, 256, C_KV), (4360, 256, C_KV + 256),
                 (4616, 1024, C_GA), (5640, 1024, C_GB))
_WIN_SHARD = N_IN // 4


def _win_pieces():
    out = []
    for src, width, dst in _WIN_SEGMENTS:
        while width:
            chip, col = divmod(src, _WIN_SHARD)
            n = min(width, _WIN_SHARD - col)
            out.append((chip, col, n, dst))
            src, dst, width = src + n, dst + n, width - n
    return out


def _win_pad(shards):
    rows = shards.shape[1]
    tr = _pick(rows, 256)

    def body(s_ref, o_ref):
        for chip, col, n, dst in _win_pieces():
            o_ref[:, dst:dst + n] = s_ref[chip, :, col:col + n]
        o_ref[:, C_IF + 8:NP] = jnp.zeros((tr, NP - C_IF - 8), shards.dtype)

    return pl.pallas_call(
        body, name="win_pad", grid=(rows // tr,), in_specs=[pl.BlockSpec((4, tr, _WIN_SHARD), lambda i: (0, i, 0))],
        out_specs=pl.BlockSpec((tr, NP), lambda i: (i, 0)), out_shape=jax.ShapeDtypeStruct((rows, NP), shards.dtype),
        compiler_params=_params(),
    )(shards)


def _win_unpad(wp):
    rows = wp.shape[0]
    tr = _pick(rows, 256)

    def body(p_ref, o_ref):
        for chip, col, n, dst in _win_pieces():
            o_ref[chip, :, col:col + n] = p_ref[:, dst:dst + n]

    return pl.pallas_call(
        body, name="win_unpad", grid=(rows // tr,), in_specs=[pl.BlockSpec((tr, NP), lambda i: (i, 0))],
        out_specs=pl.BlockSpec((4, tr, _WIN_SHARD), lambda i: (0, i, 0)),
        out_shape=jax.ShapeDtypeStruct((4, rows, _WIN_SHARD), wp.dtype), compiler_params=_params(),
    )(wp)


def _local_step(x, p, target, w, late_weights=None, early_grads=None, mid_grads=None, last_grad=None):
    t = x.shape[0]
    pb = p.astype(BF16)
    w = dict(w)

    h0, proj, gates = _proj_in(x, w["norm_mix_g"], w["w_in"])
    qk = _conv_silu_fwd(proj, w["conv_qk"])
    grow, sneg_row = _gates_fwd(gates[:, 0:8].T, w["b_if"].reshape(8, 1))
    gcol, sneg_col = grow.T, sneg_row.T
    hraw, ya, cs, st = _mlstm_fwd(qk, proj, grow, gcol, w["mlstm_norm_g"])
    yb, lse = _swa_fwd(proj, w["sinks"])
    if late_weights is not None:
        w.update(late_weights(yb))
    merged, za, zb = _branch_merge(ya, yb, w["w_branch_a"], w["w_branch_b"], proj)
    x1, hn1 = _resid_norm_mm(merged, w["w_out"], x, w["norm_mlp_g"], "mm_out_norm")
    act, u = _up_act(hn1, w["w_up"])
    x2, hn2 = _resid_norm_mm(act, w["w_down"], x1, w["norm_ple_g"], "mm_down_norm")
    pp = _mm(pb, w["w_ple_proj"], "nn", F32, "mm_ple_proj")
    loss, d_final_g, dx3, dpp, dgpre = _ple_final_mm(hn2, w["w_ple_gate"], x2, pp, target, w["final_norm_g"])

    g = {"final_norm_g": d_final_g}
    g["w_ple_proj"] = _mm(pb, dpp, "tn", F32, "mm_d_ple_proj", out_chunks=4)
    g["w_ple_gate"] = _mm(hn2, dgpre, "tn", F32, "mm_d_ple_gate")
    dx2, dx2b, g["norm_ple_g"] = _norm_bwd_mm(dgpre, w["w_ple_gate"], x2, w["norm_ple_g"], dx3, "mm_dhn2_norm")
    g["w_down"] = _mm(act, dx2b, "tn", F32, "mm_d_down")
    du = _da_du(dx2b, w["w_down"], u)
    g["w_up"] = _mm(hn1, du, "tn", F32, "mm_d_up", out_chunks=4)
    dx1, dx1b, g["norm_mlp_g"] = _norm_bwd_mm(du, w["w_up"], x1, w["norm_mlp_g"], dx2, "mm_dhn1_norm")
    g["w_out"] = _mm(merged, dx1b, "tn", F32, "mm_d_out")
    dza, dzb, dproj = _dmerged_bwd(dx1b, w["w_out"], proj, za, zb)
    g["w_branch_a"] = _mm(ya, dza, "tn", F32, "mm_d_branch_a")
    g["w_branch_b"] = _mm(yb, dzb, "tn", F32, "mm_d_branch_b")
    gain = w["mlstm_norm_g"] if early_grads is None else w["mlstm_norm_g"] + early_grads(g)
    dyb = _mm(dzb, w["w_branch_b"], "nt", F32, "mm_dyb")
    dhraw, dproj, g["mlstm_norm_g"] = _dya_bwd(dza, w["w_branch_a"], hraw, proj, gain, dproj)
    if mid_grads is not None:
        sneg_col = sneg_col + mid_grads(dhraw)
    dqk, dproj, dif, g["b_if"] = _mlstm_bwd(qk, proj, grow, gcol, sneg_col, cs, st, hraw, dhraw, dproj)
    dc, g["conv_qk"] = _conv_silu_bwd_a(proj, w["conv_qk"], dqk)
    dproj = _conv_silu_bwd_b(dc, w["conv_qk"], dproj)
    dproj, dkv_self, dkv_prev, g["sinks"] = _swa_bwd(proj, w["sinks"], lse, dyb, dproj)
    dproj = _kv_combine(dkv_self, dkv_prev, dif, dproj)
    g["w_in"] = _mm(h0, dproj, "tn", F32, "mm_d_in")
    gain = w["norm_mix_g"] if last_grad is None else w["norm_mix_g"] + last_grad(g)
    grad_x, _, g["norm_mix_g"] = _norm_bwd_mm(dproj, w["w_in"], x, gain, dx1, "mm_dh0_norm")
    return loss, grad_x, g


_W4 = ("w_branch_a", "w_branch_b", "w_out", "w_ple_gate")
_SHARDED_NAMES = ("w_in", "w_up", "w_down", "w_ple_proj", "conv_qk") + _W4
_SMALL_ROWS = 16
_CONV_ROW = 8


def _group(s):
    return [s["w_in"], jnp.concatenate([s[n] for n in _W4], axis=0), s["w_up"], s["w_down"], s["w_ple_proj"]]


def _ungroup(arrs):
    out = {"w_in": arrs[0], "w_up": arrs[2], "w_down": arrs[3], "w_ple_proj": arrs[4]}
    rows = arrs[1].shape[0] // len(_W4)
    for i, n in enumerate(_W4):
        out[n] = arrs[1][i * rows:(i + 1) * rows]
    return out


def _rows_tile(rows):
    return 256 if rows % 256 == 0 else rows


_SMALL = ("norm_mix_g", "mlstm_norm_g", "norm_mlp_g", "norm_ple_g", "final_norm_g")


def _pack_small(vals, extra=None, conv=None):
    rows = [vals[n].reshape(1, D) for n in _SMALL]
    tail = [vals["b_if"].reshape(1, 8), vals["sinks"].reshape(1, SWH)]
    used = 8 + SWH
    if extra is not None:
        tail.append(extra.reshape(1, 1))
        used += 1
    tail.append(jnp.zeros((1, D - used), F32))
    rows.append(jnp.concatenate(tail, axis=1))
    rows.append(jnp.zeros((_CONV_ROW - len(rows), D), F32))
    rows.append(jnp.zeros((CONV, D), F32) if conv is None else conv)
    rows.append(jnp.zeros((_SMALL_ROWS - _CONV_ROW - CONV, D), F32))
    return jnp.concatenate(rows, axis=0)


def _unpack_small(slab, shapes):
    out = {n: slab[i].reshape(shapes[n]) for i, n in enumerate(_SMALL)}
    out["b_if"] = slab[5, 0:8].reshape(shapes["b_if"])
    out["sinks"] = slab[5, 8:8 + SWH].reshape(shapes["sinks"])
    return out


_MESH = pl.DeviceIdType.MESH
_HBM = pl.BlockSpec(memory_space=pltpu.HBM)
_VMEM = pl.BlockSpec(memory_space=pltpu.VMEM)


def _place():
    x, y, c = lax.axis_index("x"), lax.axis_index("y"), lax.axis_index("c")
    return x, y, c, 2 * x + y


def _chip_peer(x, y, r):
    return (x ^ (r >> 1), y ^ (r & 1))


def _half(ref, which):
    h = ref.shape[-2] // 2
    return pl.ds(which * h, h)


def _allgather_weights(shards, conv):
    n = len(shards)

    def body(*refs):
        ins, conv_ref = refs[:n], refs[n]
        outs, conv_out = refs[n + 1:2 * n + 1], refs[2 * n + 1]
        send_a, recv_a, send_b, recv_b, send_c, recv_c, local_sems = refs[2 * n + 2:]
        x, y, c, j = _place()
        sibling = (x, y, 1 - c)
        local = [pltpu.make_async_copy(ins[k], outs[k].at[j], local_sems.at[k]) for k in range(n)]
        local.append(pltpu.make_async_copy(conv_ref, conv_out.at[j], local_sems.at[n]))
        for cp in local:
            cp.start()

        def copy_a(k, r, chip):
            rows = _half(ins[k], c)
            return pltpu.make_async_remote_copy(
                src_ref=ins[k].at[rows], dst_ref=outs[k].at[chip, rows], send_sem=send_a.at[3 * k + r - 1],
                recv_sem=recv_a.at[3 * k + r - 1], device_id=(*_chip_peer(x, y, r), c), device_id_type=_MESH)

        def copy_b(k, r, chip, which):
            rows = _half(ins[k], which)
            return pltpu.make_async_remote_copy(
                src_ref=outs[k].at[chip, rows], dst_ref=outs[k].at[chip, rows], send_sem=send_b.at[3 * k + r - 1],
                recv_sem=recv_b.at[3 * k + r - 1], device_id=sibling, device_id_type=_MESH)

        def copy_c(r, chip):
            return pltpu.make_async_remote_copy(
                src_ref=conv_ref, dst_ref=conv_out.at[chip], send_sem=send_c.at[r - 1],
                recv_sem=recv_c.at[r - 1], device_id=(*_chip_peer(x, y, r), c), device_id_type=_MESH)

        for k in range(n):
            for r in (1, 2, 3):
                copy_a(k, r, j).start()
        for r in (1, 2, 3):
            copy_c(r, j).start()
        for k in range(n):
            for r in (1, 2, 3):
                copy_a(k, r, j ^ r).wait_recv()
                copy_b(k, r, j ^ r, c).start()
        for k in range(n):
            for r in (1, 2, 3):
                copy_b(k, r, j ^ r, 1 - c).wait_recv()
        for r in (1, 2, 3):
            copy_c(r, j ^ r).wait_recv()
        for k in range(n):
            for r in (1, 2, 3):
                copy_a(k, r, j).wait_send()
                copy_b(k, r, j ^ r, c).wait_send()
        for r in (1, 2, 3):
            copy_c(r, j).wait_send()
        for cp in local:
            cp.wait()

    return pl.pallas_call(
        body, name="allgather_weights",
        out_shape=[jax.ShapeDtypeStruct((4,) + s.shape, s.dtype) for s in shards]
        + [jax.ShapeDtypeStruct((4,) + conv.shape, F32)],
        in_specs=[_HBM] * (n + 1), out_specs=[_HBM] * (n + 1),
        scratch_shapes=[pltpu.SemaphoreType.DMA((3 * n,))] * 4 + [pltpu.SemaphoreType.DMA((3,))] * 2
        + [pltpu.SemaphoreType.DMA((n + 1,))],
    )(*shards, conv)


_SEM = pl.BlockSpec(memory_space=pltpu.SEMAPHORE)
_DATAFLOW = pltpu.SideEffectType.DATAFLOW_SIDE_EFFECTING


def _late_peer_copy(src_ref, land_ref, send_sems, recv_sems, x, y, c, j, r, chip):
    return pltpu.make_async_remote_copy(
        src_ref=src_ref, dst_ref=land_ref.at[chip], send_sem=send_sems.at[r - 1], recv_sem=recv_sems.at[r - 1],
        device_id=(*_chip_peer(x, y, r), c), device_id_type=_MESH)


def _late_gather_start(rest):
    def body(rest_ref, land_ref, send_sems, recv_sems, rest_thru, land_thru, token):
        x, y, c, j = _place()
        for r in (1, 2, 3):
            _late_peer_copy(rest_ref, land_ref, send_sems, recv_sems, x, y, c, j, r, j).start()
        token[...] = jnp.zeros_like(token)

    j = 2 * lax.axis_index("x") + lax.axis_index("y")
    land = lax.dynamic_update_slice(lax.empty((4,) + rest.shape, rest.dtype), rest[None], (j, 0, 0))
    return pl.pallas_call(
        body, name="late_gather_start",
        out_shape=(pltpu.SemaphoreType.DMA((3,)), pltpu.SemaphoreType.DMA((3,)), pltpu.HBM(rest.shape, rest.dtype),
                   pltpu.HBM(land.shape, land.dtype), jax.ShapeDtypeStruct((8, 128), F32)),
        in_specs=(_HBM, _HBM), out_specs=(_SEM, _SEM, _HBM, _HBM, _VMEM), input_output_aliases={0: 2, 1: 3},
        compiler_params=pltpu.CompilerParams(has_side_effects=_DATAFLOW),
    )(pltpu.with_memory_space_constraint(rest, pltpu.HBM), pltpu.with_memory_space_constraint(land, pltpu.HBM))


def _late_gather_wait(send_sems, recv_sems, rest_thru, land_thru, after):
    def body(rest_ref, land_ref, send_sems, recv_sems, after_ref, rest_dead, got_ref):
        x, y, c, j = _place()
        for r in (1, 2, 3):
            cp = _late_peer_copy(rest_ref, land_ref, send_sems, recv_sems, x, y, c, j, r, j ^ r)
            cp.wait_send()
            cp.wait_recv()

    return pl.pallas_call(
        body, name="late_gather_wait",
        out_shape=(pltpu.HBM(rest_thru.shape, rest_thru.dtype), pltpu.HBM(land_thru.shape, land_thru.dtype)),
        in_specs=(_HBM, _HBM, _SEM, _SEM, _ANY), out_specs=(_HBM, _HBM), input_output_aliases={0: 0, 1: 1},
        compiler_params=pltpu.CompilerParams(has_side_effects=_DATAFLOW),
    )(rest_thru, land_thru, send_sems, recv_sems, after)[1]


def _pair_exchange(gs, name):
    n = len(gs)

    def body(*refs):
        ins, outs, send_sems, recv_sems = refs[:n], refs[n:2 * n], refs[2 * n], refs[2 * n + 1]
        x, y, c, _ = _place()
        cps = [pltpu.make_async_remote_copy(
            src_ref=ins[k].at[:, _half(ins[k], 1 - c)], dst_ref=outs[k], send_sem=send_sems.at[k],
            recv_sem=recv_sems.at[k], device_id=(x, y, 1 - c), device_id_type=_MESH) for k in range(n)]
        for cp in cps:
            cp.start()
        for cp in cps:
            cp.wait()

    return pl.pallas_call(
        body, name=name,
        out_shape=[jax.ShapeDtypeStruct((4, g.shape[1] // 2, g.shape[2]), F32) for g in gs],
        in_specs=[_HBM] * n, out_specs=[_HBM] * n, scratch_shapes=[pltpu.SemaphoreType.DMA((n,))] * 2,
    )(*gs)


def _pair_sum(g, theirs, c, name):
    _, h, cols = theirs.shape
    tr = _rows_tile(h)
    nb = h // tr

    def body(c_ref, a_ref, b_ref, o_ref, ob_ref):
        s = a_ref[...] + b_ref[...]
        o_ref[...] = s
        ob_ref[...] = s.astype(BF16)

    blk = pl.BlockSpec((1, tr, cols), lambda k, i, c_ref: (k, i, 0))
    return pl.pallas_call(
        body, name=name,
        grid_spec=pltpu.PrefetchScalarGridSpec(
            num_scalar_prefetch=1, grid=(4, nb),
            in_specs=[pl.BlockSpec((1, tr, cols), lambda k, i, c_ref: (k, c_ref[0] * nb + i, 0)), blk],
            out_specs=[blk, blk]),
        out_shape=[jax.ShapeDtypeStruct(theirs.shape, F32), jax.ShapeDtypeStruct(theirs.shape, BF16)],
        compiler_params=_params(),
    )(c.reshape(1).astype(jnp.int32), g, theirs)


def _chip_copies(srcs, lands, send_sems, recv_sems):
    x, y, c, j = _place()
    return [pltpu.make_async_remote_copy(
        src_ref=srcs[k].at[j ^ r], dst_ref=lands[k].at[r - 1], send_sem=send_sems.at[3 * k + r - 1],
        recv_sem=recv_sems.at[3 * k + r - 1], device_id=(*_chip_peer(x, y, r), c), device_id_type=_MESH)
        for k in range(len(srcs)) for r in (1, 2, 3)]


def _pair_copies(srcs, lands, send_sems, recv_sems):
    x, y, c, _ = _place()
    return [pltpu.make_async_remote_copy(
        src_ref=srcs[k].at[:, _half(srcs[k], 1 - c)], dst_ref=lands[k], send_sem=send_sems.at[k],
        recv_sem=recv_sems.at[k], device_id=(x, y, 1 - c), device_id_type=_MESH) for k in range(len(srcs))]


def _split_start(name, srcs, lands, copies, n_sems):
    n = len(srcs)

    def body(*refs):
        for cp in copies(refs[:n], refs[n:2 * n], refs[2 * n], refs[2 * n + 1]):
            cp.start()
        refs[-1][...] = jnp.zeros_like(refs[-1])

    arrays = list(srcs) + list(lands)
    out = pl.pallas_call(
        body, name=name,
        out_shape=(pltpu.SemaphoreType.DMA((n_sems,)), pltpu.SemaphoreType.DMA((n_sems,)),
                   *[pltpu.HBM(a.shape, a.dtype) for a in arrays], jax.ShapeDtypeStruct((8, 128), F32)),
        in_specs=[_HBM] * (2 * n), out_specs=(_SEM, _SEM, *([_HBM] * (2 * n)), _VMEM),
        input_output_aliases={k: 2 + k for k in range(2 * n)},
        compiler_params=pltpu.CompilerParams(has_side_effects=_DATAFLOW),
    )(*[pltpu.with_memory_space_constraint(a, pltpu.HBM) for a in arrays])
    return out[0], out[1], list(out[2:2 + n]), list(out[2 + n:2 + 2 * n]), out[-1]


def _split_wait(name, send_sems, recv_sems, srcs_thru, lands_thru, after, copies):
    n = len(srcs_thru)

    def body(*refs):
        for cp in copies(refs[:n], refs[n:2 * n], refs[2 * n], refs[2 * n + 1]):
            cp.wait_send()
            cp.wait_recv()

    arrays = list(srcs_thru) + list(lands_thru)
    out = pl.pallas_call(
        body, name=name, out_shape=tuple(pltpu.HBM(a.shape, a.dtype) for a in arrays),
        in_specs=[_HBM] * (2 * n) + [_SEM, _SEM, _ANY], out_specs=tuple([_HBM] * (2 * n)),
        input_output_aliases={k: k for k in range(2 * n)},
        compiler_params=pltpu.CompilerParams(has_side_effects=_DATAFLOW),
    )(*arrays, send_sems, recv_sems, after)
    return list(out[:n]), list(out[n:])


def _chip_exchange_start(ss, tag):
    lands = [lax.empty((3,) + s.shape[1:], s.dtype) for s in ss]
    return _split_start("chip_exchange_start_" + tag, ss, lands, _chip_copies, 3 * len(ss))


def _chip_exchange_wait(send_sems, recv_sems, ss_thru, lands_thru, after, tag):
    return _split_wait("chip_exchange_wait_" + tag, send_sems, recv_sems, ss_thru, lands_thru, after, _chip_copies)[1]


def _pair_exchange_start(gs, tag):
    lands = [lax.empty((4, g.shape[1] // 2, g.shape[2]), g.dtype) for g in gs]
    return _split_start("pair_exchange_start_" + tag, gs, lands, _pair_copies, len(gs))


def _pair_exchange_wait(send_sems, recv_sems, gs_thru, lands_thru, after, tag):
    return _split_wait("pair_exchange_wait_" + tag, send_sems, recv_sems, gs_thru, lands_thru, after, _pair_copies)


def _reduce4(own, others, j, c, name):
    _, h, cols = own.shape
    tr = _rows_tile(h)
    nb = h // tr

    def body(idx_ref, s_ref, a0, a1, a2, o_ref):
        o_ref[...] = ((s_ref[0] + a0[0].astype(F32)) + a1[0].astype(F32)) + a2[0].astype(F32)

    def other(r):
        return pl.BlockSpec((1, tr, cols), lambda i, idx_ref: (r, i, 0))

    return pl.pallas_call(
        body, name=name,
        grid_spec=pltpu.PrefetchScalarGridSpec(
            num_scalar_prefetch=1, grid=(nb,),
            in_specs=[pl.BlockSpec((1, tr, cols), lambda i, idx_ref: (idx_ref[0], i, 0)), other(0), other(1), other(2)],
            out_specs=pl.BlockSpec((tr, cols), lambda i, idx_ref: (idx_ref[1] * nb + i, 0))),
        out_shape=jax.ShapeDtypeStruct((2 * h, cols), F32), compiler_params=_params(),
    )(jnp.stack([j, c]).astype(jnp.int32), own, others, others, others)


def _sibling_share(fulls):
    n = len(fulls)

    def body(*refs):
        outs, send_sems, recv_sems = refs[n:2 * n], refs[2 * n], refs[2 * n + 1]
        x, y, c, _ = _place()
        cps = [pltpu.make_async_remote_copy(
            src_ref=outs[k].at[_half(outs[k], c)], dst_ref=outs[k].at[_half(outs[k], c)], send_sem=send_sems.at[k],
            recv_sem=recv_sems.at[k], device_id=(x, y, 1 - c), device_id_type=_MESH) for k in range(n)]
        for cp in cps:
            cp.start()
        for cp in cps:
            cp.wait()

    return pl.pallas_call(
        body, name="sibling_share", out_shape=[jax.ShapeDtypeStruct(f.shape, F32) for f in fulls],
        in_specs=[_HBM] * n, out_specs=[_HBM] * n, input_output_aliases={k: k for k in range(n)},
        scratch_shapes=[pltpu.SemaphoreType.DMA((n,))] * 2,
    )(*fulls)


def _adamw(w, g, m, v):
    m1 = ADAM_B1 * m + (1.0 - ADAM_B1) * g
    v1 = ADAM_B2 * v + (1.0 - ADAM_B2) * (g * g)
    m_hat = m1 / (1.0 - ADAM_B1 ** ADAM_STEP)
    v_hat = v1 / (1.0 - ADAM_B2 ** ADAM_STEP)
    delta = -ADAM_LR * (m_hat / (jnp.sqrt(v_hat) + ADAM_EPS) + ADAM_WD * w)
    return delta, m1, v1


def _adamw_call(w, g, m, v, name):
    rows, cols = w.shape

    def body(w_ref, g_ref, m_ref, v_ref, d_out, m_out, v_out):
        delta, m1, v1 = _adamw(w_ref[...], g_ref[...], m_ref[...], v_ref[...])
        d_out[...] = delta
        m_out[...] = m1
        v_out[...] = v1

    if rows % 8 == 0:
        tr = _rows_tile(rows)
        blk, grid = pl.BlockSpec((tr, cols), lambda i: (i, 0)), (rows // tr,)
    else:
        blk, grid = pl.BlockSpec((rows, 128), lambda i: (0, i)), (cols // 128,)
    return pl.pallas_call(
        body, name=name, grid=grid, in_specs=[blk] * 4, out_specs=[blk] * 3,
        out_shape=[jax.ShapeDtypeStruct((rows, cols), F32)] * 3, compiler_params=_params(),
    )(w, g, m, v)


def _small_allreduce(vals):
    def body(v_ref, out_ref, buf, send_sems, recv_sems):
        x, y, c, j = _place()
        me = 2 * j + c
        buf[0] = v_ref[...]

        def copy(r):
            return pltpu.make_async_remote_copy(
                src_ref=v_ref, dst_ref=buf.at[r], send_sem=send_sems.at[r - 1], recv_sem=recv_sems.at[r - 1],
                device_id=(x ^ (r >> 2), y ^ ((r >> 1) & 1), c ^ (r & 1)), device_id_type=_MESH)

        for r in range(1, 8):
            copy(r).start()
        for r in range(1, 8):
            copy(r).wait()
        acc = buf[me ^ 0]
        for d in range(1, 8):
            acc = acc + buf[me ^ d]
        out_ref[...] = acc

    return pl.pallas_call(
        body, name="small_allreduce", out_shape=jax.ShapeDtypeStruct((_SMALL_ROWS, D), F32),
        in_specs=[_VMEM], out_specs=_VMEM,
        scratch_shapes=[pltpu.VMEM((8, _SMALL_ROWS, D), F32), pltpu.SemaphoreType.DMA((7,)),
                        pltpu.SemaphoreType.DMA((7,))],
    )(vals)


_NAMES = ("norm_mix_g", "w_in", "conv_qk", "b_if", "mlstm_norm_g", "sinks", "w_branch_a", "w_branch_b", "w_out",
          "norm_mlp_g", "w_up", "w_down", "norm_ple_g", "w_ple_gate", "w_ple_proj", "final_norm_g")
_GROUP_NAMES = ("w_in", "w4", "w_up", "w_down", "w_ple_proj")


def _step(x, p, target, w, m, v):
    c = lax.axis_index("c")
    j = 2 * lax.axis_index("x") + lax.axis_index("y")

    def shards(d):
        return {n: d[n][0] for n in _SHARDED_NAMES}

    ws = shards(w)
    w_in_all, conv_all = _allgather_weights([ws["w_in"].astype(BF16)], ws["conv_qk"])
    rows_pp = PLE * (D // 4) // D
    rest = jnp.concatenate([ws[n] for n in _W4] + [ws["w_up"], ws["w_down"], ws["w_ple_proj"].reshape(rows_pp, D)],
                           axis=0)
    rest = (rest + 0.0 * conv_all[0, 0, 0]).astype(BF16)
    send_sems, recv_sems, rest_thru, land_thru, token = _late_gather_start(rest)
    full = {n: w[n] for n in ("mlstm_norm_g", "norm_mlp_g", "norm_ple_g", "b_if", "sinks")}
    full["norm_mix_g"] = w["norm_mix_g"] + token[0, 0]
    full["final_norm_g"] = w["final_norm_g"].reshape(1, D)
    full["w_in"] = _win_pad(w_in_all)
    full["conv_qk"] = jnp.swapaxes(conv_all, 0, 1).reshape(CONV, D)

    def late_weights(after):
        land = _late_gather_wait(send_sems, recv_sems, rest_thru, land_thru, after)
        out = {n: land[:, i * (D // 4):(i + 1) * (D // 4)].reshape(D, D) for i, n in enumerate(_W4)}
        out["w_up"] = land[:, D:2 * D]
        out["w_down"] = land[:, 2 * D:3 * D].reshape(DFF, D)
        out["w_ple_proj"] = jnp.swapaxes(land[:, 3 * D:3 * D + rows_pp].reshape(4, PLE, D // 4), 0, 1).reshape(PLE, D)
        return out

    def pair_sums(by_dest, names, tag):
        theirs = _pair_exchange(by_dest, "pair_exchange_" + tag)
        return [_pair_sum(a, b, c, "pair_sum_" + n) for a, b, n in zip(by_dest, theirs, names)]

    early, last = {}, {}

    def early_grads(g):
        by_dest = [jnp.stack([g[n].reshape(4, D // 4, D) for n in _W4], axis=1).reshape(4, D, D),
                   g["w_up"], g["w_down"].reshape(4, DFF // 4, D), g["w_ple_proj"]]
        *early["pair"], token = _pair_exchange_start(by_dest, "early")
        return token[0, 0]

    def mid_grads(after):
        by_dest, theirs = _pair_exchange_wait(*early["pair"], after, "early")
        early["sums"] = [_pair_sum(a, b, c, "pair_sum_" + n) for a, b, n in zip(by_dest, theirs, _GROUP_NAMES[1:])]
        *early["flight"], token = _chip_exchange_start([s[1] for s in early["sums"]], "early")
        return token[0, 0]

    def last_grad(g):
        last["sums"] = pair_sums([_win_unpad(g["w_in"])], _GROUP_NAMES[:1], "w_in")
        *last["flight"], token = _chip_exchange_start([s[1] for s in last["sums"]], "w_in")
        return token[0, 0]

    loss, grad_x, g = _local_step(x[0], p[0, 0], target[0], full, late_weights, early_grads, mid_grads, last_grad)

    others = _chip_exchange_wait(*last["flight"], grad_x, "w_in")
    others += _chip_exchange_wait(*early["flight"], others[0], "early")
    sums = last["sums"] + early["sums"]
    halves = [_reduce4(s[0], b, j, c, "reduce4_" + n) for s, b, n in zip(sums, others, _GROUP_NAMES)]
    grads = _sibling_share(halves)

    small_g = _small_allreduce(_pack_small(g, extra=loss, conv=g["conv_qk"]))
    conv_g = lax.dynamic_slice(small_g[_CONV_ROW:_CONV_ROW + CONV], (0, j * (D // 4)), (CONV, D // 4))

    ms, vs = shards(m), shards(v)
    upd = [_adamw_call(wa, ga, ma, va, "adamw_" + n)
           for wa, ga, ma, va, n in list(zip(_group(ws), grads, _group(ms), _group(vs), _GROUP_NAMES))[1:]]
    upd_in = _adamw_call(*[jnp.swapaxes(a, 0, 1) for a in (ws["w_in"], grads[0], ms["w_in"], vs["w_in"])], "adamw_w_in")
    upd = [[jnp.swapaxes(a, 0, 1) for a in upd_in]] + upd
    conv_upd = _adamw_call(ws["conv_qk"], conv_g, ms["conv_qk"], vs["conv_qk"], "adamw_conv")
    small_upd = _adamw_call(_pack_small(w), small_g, _pack_small(m), _pack_small(v), "adamw_small")

    shapes = {n: w[n].shape for n in _NAMES}
    res = []
    for k in range(4):
        big = _ungroup(list(grads) if k == 0 else [u[k - 1] for u in upd])
        big["conv_qk"] = conv_g if k == 0 else conv_upd[k - 1]
        leaves = _unpack_small(small_g if k == 0 else small_upd[k - 1], shapes)
        leaves.update({n: a.reshape(shapes[n]) for n, a in big.items()})
        res.append(leaves)

    out = [small_g[5, 8 + SWH], grad_x[None]]
    for k in range(4):
        out += [res[k][n] for n in _NAMES]
    return tuple(out)


def kernel(x, p, norm_mix_g, w_in, conv_qk, b_if, mlstm_norm_g, sinks, w_branch_a, w_branch_b, w_out, norm_mlp_g, w_up, w_down, norm_ple_g, w_ple_gate, w_ple_proj, final_norm_g, loss_target, m_norm_mix_g, m_w_in, m_conv_qk, m_b_if, m_mlstm_norm_g, m_sinks, m_w_branch_a, m_w_branch_b, m_w_out, m_norm_mlp_g, m_w_up, m_w_down, m_norm_ple_g, m_w_ple_gate, m_w_ple_proj, m_final_norm_g, v_norm_mix_g, v_w_in, v_conv_qk, v_b_if, v_mlstm_norm_g, v_sinks, v_w_branch_a, v_w_branch_b, v_w_out, v_norm_mlp_g, v_w_up, v_w_down, v_norm_ple_g, v_w_ple_gate, v_w_ple_proj, v_final_norm_g):
    w = dict(zip(_NAMES, (norm_mix_g, w_in, conv_qk, b_if, mlstm_norm_g, sinks, w_branch_a, w_branch_b, w_out,
                          norm_mlp_g, w_up, w_down, norm_ple_g, w_ple_gate, w_ple_proj, final_norm_g)))
    m = dict(zip(_NAMES, (m_norm_mix_g, m_w_in, m_conv_qk, m_b_if, m_mlstm_norm_g, m_sinks, m_w_branch_a,
                          m_w_branch_b, m_w_out, m_norm_mlp_g, m_w_up, m_w_down, m_norm_ple_g, m_w_ple_gate,
                          m_w_ple_proj, m_final_norm_g)))
    v = dict(zip(_NAMES, (v_norm_mix_g, v_w_in, v_conv_qk, v_b_if, v_mlstm_norm_g, v_sinks, v_w_branch_a,
                          v_w_branch_b, v_w_out, v_norm_mlp_g, v_w_up, v_w_down, v_norm_ple_g, v_w_ple_gate,
                          v_w_ple_proj, v_final_norm_g)))
    return _step(x, p, loss_target, w, m, v)
```

```python
import jax
import jax.numpy as jnp
from jax import lax
from jax.experimental import pallas as pl
from jax.experimental.pallas import tpu as pltpu

F32 = jnp.float32
BF16 = jnp.bfloat16

D = 1024
PLE = 256
MLH = 4
DQK = 128
DV = 256
CONV = 4
CHUNK = 128
SWH = 16
SWKV = 4
SWG = SWH // SWKV
HD = 64
WIN = 128
DFF = 4096
EPS = 1e-6
N_IN = 6664
NP = 7168
C_QK, C_V, C_O, C_QSW, C_GA, C_GB, C_KV, C_IF = 0, 1024, 2048, 3072, 4096, 5120, 6144, 6656
IFW = NP - C_IF

ADAM_LR = 0.001
ADAM_B1 = 0.9
ADAM_B2 = 0.999
ADAM_EPS = 1e-08
ADAM_WD = 0.01
ADAM_STEP = 10

TOK_TILE = 512
VMEM_LIMIT = 58 * 1024 * 1024


def _params(**kw):
    return pltpu.CompilerParams(vmem_limit_bytes=VMEM_LIMIT, **kw)


def _pick(n, cap):
    if n <= cap:
        return n
    t = cap - cap % 128
    while t > 128 and n % t:
        t -= 128
    assert n % t == 0, (n, cap)
    return t


def _dot(a, b, dims):
    return lax.dot_general(a, b, (dims, ((), ())), preferred_element_type=F32)


def _dot_nn(a, b):
    return _dot(a, b, ((1,), (0,)))


def _dot_nt(a, b):
    return _dot(a, b, ((1,), (1,)))


def _dot_tn(a, b):
    return _dot(a, b, ((0,), (0,)))


def _sigmoid(x):
    return 1.0 / (1.0 + jnp.exp(-x))


def _mm(a, b, mode, out_dtype, name, out_chunks=1):
    bch = b.shape[0] if b.ndim == 3 else 1
    brows, bcols = b.shape[-2], b.shape[-1] * bch
    if mode == "nn":
        (m, k), (k2, n) = a.shape, (brows, bcols)
    elif mode == "nt":
        (m, k), (n, k2) = a.shape, (brows, bcols)
    else:
        (k, m), (k2, n) = a.shape, (brows, bcols)
    assert k == k2, (a.shape, b.shape, mode)
    n_cap = n // max(out_chunks, 1 if mode == "nt" else bch)
    k_cap = k // bch if mode == "nt" else k
    tm, tn, tk = _pick(m, 1024), _pick(n_cap, 1024), _pick(k_cap, 2048)
    nk = k // tk
    if mode == "nn":
        a_spec = pl.BlockSpec((tm, tk), lambda i, j, kk: (i, kk))
        if bch > 1:
            bpc = (n // bch) // tn
            b_spec = pl.BlockSpec((None, tk, tn), lambda i, j, kk: (j // bpc, kk, j % bpc))
        else:
            b_spec = pl.BlockSpec((tk, tn), lambda i, j, kk: (kk, j))
        dot = _dot_nn
    elif mode == "nt":
        a_spec = pl.BlockSpec((tm, tk), lambda i, j, kk: (i, kk))
        if bch > 1:
            bpc = (k // bch) // tk
            b_spec = pl.BlockSpec((None, tn, tk), lambda i, j, kk: (kk // bpc, j, kk % bpc))
        else:
            b_spec = pl.BlockSpec((tn, tk), lambda i, j, kk: (j, kk))
        dot = _dot_nt
    else:
        assert bch == 1
        a_spec = pl.BlockSpec((tk, tm), lambda i, j, kk: (kk, i))
        b_spec = pl.BlockSpec((tk, tn), lambda i, j, kk: (kk, j))
        dot = _dot_tn
    if out_chunks > 1:
        npc = (n // out_chunks) // tn
        out_spec = pl.BlockSpec((None, tm, tn), lambda i, j, kk: (j // npc, i, j % npc))
        out_shape = jax.ShapeDtypeStruct((out_chunks, m, n // out_chunks), out_dtype)
    else:
        out_spec = pl.BlockSpec((tm, tn), lambda i, j, kk: (i, j))
        out_shape = jax.ShapeDtypeStruct((m, n), out_dtype)

    def body(a_ref, b_ref, o_ref, acc_ref):
        kk = pl.program_id(2)

        @pl.when(kk == 0)
        def _():
            acc_ref[...] = jnp.zeros_like(acc_ref)

        acc_ref[...] += dot(a_ref[...], b_ref[...])

        @pl.when(kk == nk - 1)
        def _():
            o_ref[...] = acc_ref[...].astype(out_dtype)

    return pl.pallas_call(
        body, name=name, grid=(m // tm, n // tn, nk),
        in_specs=[a_spec, b_spec], out_specs=out_spec, out_shape=out_shape,
        scratch_shapes=[pltpu.VMEM((tm, tn), F32)],
        compiler_params=_params(dimension_semantics=("parallel", "parallel", "arbitrary")),
    )(a, b)


def _tile(col0=0):
    return lambda tm, tn: pl.BlockSpec((tm, tn), lambda i, j, kk: (i, col0 // tn + j))


def _row():
    return lambda tm, tn: pl.BlockSpec((1, tn), lambda i, j, kk: (0, j))


def _mm_ep(pairs, mode, name, epilogue, ins, outs, tm, tn, aliases=None):
    a0, b0 = pairs[0]
    bch = b0.shape[0] if b0.ndim == 3 else 1
    m, k = a0.shape
    tm = _pick(m, tm)
    n = b0.shape[-1] * bch if mode == "nn" else b0.shape[-2]
    tk = _pick(k // bch if mode == "nt" else k, 2048)
    nk = k // tk
    a_spec = pl.BlockSpec((tm, tk), lambda i, j, kk: (i, kk))
    if mode == "nn":
        dot = _dot_nn
        if bch > 1:
            bpc = (n // bch) // tn
            b_spec = pl.BlockSpec((None, tk, tn), lambda i, j, kk: (j // bpc, kk, j % bpc))
        else:
            b_spec = pl.BlockSpec((tk, tn), lambda i, j, kk: (kk, j))
    else:
        dot = _dot_nt
        if bch > 1:
            bpc = (k // bch) // tk
            b_spec = pl.BlockSpec((None, tn, tk), lambda i, j, kk: (kk // bpc, j, kk % bpc))
        else:
            b_spec = pl.BlockSpec((tn, tk), lambda i, j, kk: (j, kk))
    npair, nin, nout = len(pairs), len(ins), len(outs)

    def body(*refs):
        ab = refs[:2 * npair]
        in_refs = refs[2 * npair:2 * npair + nin]
        out_refs = refs[2 * npair + nin:2 * npair + nin + nout]
        accs = refs[2 * npair + nin + nout:]
        i, j, kk = pl.program_id(0), pl.program_id(1), pl.program_id(2)
        for p in range(npair):
            prod = dot(ab[2 * p][...], ab[2 * p + 1][...])

            @pl.when(kk == 0)
            def _():
                accs[p][...] = prod

            @pl.when(kk > 0)
            def _():
                accs[p][...] += prod

        @pl.when(kk == nk - 1)
        def _():
            epilogue([acc[...] for acc in accs], in_refs, out_refs, i, j)

    operands = [x for pair in pairs for x in pair] + [a for a, _ in ins]
    io_alias = {2 * npair + i: o for i, o in (aliases or {}).items()}
    return pl.pallas_call(
        body, name=name, grid=(m // tm, n // tn, nk),
        in_specs=[a_spec, b_spec] * npair + [mk(tm, tn) for _, mk in ins],
        out_specs=[mk(tm, tn) for _, mk in outs], out_shape=[s for s, _ in outs],
        scratch_shapes=[pltpu.VMEM((tm, tn), F32)] * npair, input_output_aliases=io_alias,
        compiler_params=_params(dimension_semantics=("arbitrary", "arbitrary", "arbitrary")),
    )(*operands)


def _tok(w, j=0):
    return pl.BlockSpec((TOK_TILE, w), lambda i: (i, j))


def _rep(shape):
    return pl.BlockSpec(shape, lambda i: (0,) * len(shape))


def _rms(x):
    rstd = lax.rsqrt(jnp.mean(x * x, axis=-1, keepdims=True) + EPS)
    return x * rstd, rstd


def _rms_bwd(xn, rstd, dxn):
    return rstd * (dxn - xn * jnp.mean(dxn * xn, axis=-1, keepdims=True))


def _halo_prev(w, j=0, rows=8):
    r = TOK_TILE // rows
    return pl.BlockSpec((rows, w), lambda i: (jnp.maximum(i * r - 1, 0), j))


def _last8(halo_ref):
    return halo_ref[...].astype(F32)[halo_ref.shape[0] - 8:]


def _halo_next(w, nt, j=0):
    r = TOK_TILE // 8
    return pl.BlockSpec((8, w), lambda i: (jnp.minimum((i + 1) * r, nt * r - 1), j))


def _shift_down(x, halo, s):
    if s == 0:
        return x
    r = pltpu.roll(x, s, 0)
    hs = pltpu.roll(halo, s, 0)
    row = lax.broadcasted_iota(jnp.int32, hs.shape, 0)
    top = jnp.where(row < s, hs, r[0:8])
    return jnp.concatenate([top, r[8:]], axis=0)


def _shift_up(x, halo, s):
    if s == 0:
        return x
    n = x.shape[0]
    r = pltpu.roll(x, n - s, 0)
    hs = pltpu.roll(halo, 8 - s, 0)
    row = lax.broadcasted_iota(jnp.int32, hs.shape, 0)
    bot = jnp.where(row >= 8 - s, hs, r[n - 8:])
    return jnp.concatenate([r[:n - 8], bot], axis=0)


def _bf(x):
    return x.astype(BF16).astype(F32)


def _conv_taps(x, halo, w):
    x, halo, w = _bf(x), _bf(halo), _bf(w)
    acc = x * w[CONV - 1:CONV, :]
    for j in range(CONV - 1):
        acc = acc + _shift_down(x, halo, CONV - 1 - j) * w[j:j + 1, :]
    return acc


_Q_SCALE = DQK ** -0.5


def _qscale_row():
    lane = lax.broadcasted_iota(jnp.int32, (1, D), 1)
    return jnp.where(lane < MLH * DQK, _Q_SCALE, 1.0).astype(F32)


def _conv_silu_fwd(proj, conv_w):
    t = proj.shape[0]

    def body(x_ref, halo_ref, w_ref, o_ref):
        halo = jnp.where(pl.program_id(0) > 0, _last8(halo_ref), 0.0)
        c = _conv_taps(x_ref[...].astype(F32), halo, w_ref[...])
        o_ref[...] = (c * _sigmoid(c) * _qscale_row()).astype(BF16)

    return pl.pallas_call(
        body, name="conv_silu_fwd", grid=(t // TOK_TILE,),
        in_specs=[_tok(D, C_QK // D), _halo_prev(D, C_QK // D, 16), _rep((CONV, D))], out_specs=_tok(D),
        out_shape=jax.ShapeDtypeStruct((t, D), BF16), compiler_params=_params(),
    )(proj, proj, conv_w)


def _conv_silu_bwd_a(proj, conv_w, dqk):
    t = proj.shape[0]

    def body(x_ref, halo_ref, w_ref, d_ref, dc_ref, dw_ref):
        @pl.when(pl.program_id(0) == 0)
        def _():
            dw_ref[...] = jnp.zeros_like(dw_ref)

        halo = jnp.where(pl.program_id(0) > 0, _last8(halo_ref), 0.0)
        x = x_ref[...].astype(F32)
        c = _conv_taps(x, halo, w_ref[...])
        s = _sigmoid(c)
        dc = d_ref[...] * _qscale_row() * (s * (1.0 + c * (1.0 - s)))
        dc_ref[...] = dc
        dcb, xb, halo_b = _bf(dc), _bf(x), _bf(halo)
        for j in range(CONV):
            dw_ref[j:j + 1, :] += jnp.sum(dcb * _shift_down(xb, halo_b, CONV - 1 - j), axis=0, keepdims=True)

    return pl.pallas_call(
        body, name="conv_silu_bwd_a", grid=(t // TOK_TILE,),
        in_specs=[_tok(D, C_QK // D), _halo_prev(D, C_QK // D, 16), _rep((CONV, D)), _tok(D)],
        out_specs=[_tok(D), _rep((CONV, D))],
        out_shape=[jax.ShapeDtypeStruct((t, D), F32), jax.ShapeDtypeStruct((CONV, D), F32)],
        compiler_params=_params(),
    )(proj, proj, conv_w, dqk)


def _conv_silu_bwd_b(dc, conv_w, dproj):
    t = dc.shape[0]
    nt = t // TOK_TILE

    def body(dc_ref, halo_ref, w_ref, _, dx_ref):
        halo = _bf(jnp.where(pl.program_id(0) < nt - 1, halo_ref[...], 0.0))
        dcv = _bf(dc_ref[...])
        w = _bf(w_ref[...])
        acc = dcv * w[CONV - 1:CONV, :]
        for j in range(CONV - 1):
            acc = acc + _shift_up(dcv, halo, CONV - 1 - j) * w[j:j + 1, :]
        dx_ref[...] = acc.astype(BF16)

    return pl.pallas_call(
        body, name="conv_silu_bwd_b", grid=(nt,), in_specs=[_tok(D), _halo_next(D, nt), _rep((CONV, D)), _ANY],
        out_specs=_tok(D, C_QK // D), out_shape=jax.ShapeDtypeStruct((t, NP), BF16),
        input_output_aliases={3: 0}, compiler_params=_params(),
    )(dc, dc, conv_w, dproj)


def _gates_fwd(pre_rows, bias_col):
    t = pre_rows.shape[1]

    def body(p_ref, b_ref, g_ref, s_ref):
        z = p_ref[...] + b_ref[...]
        lf = jnp.minimum(z, 0.0) - jnp.log(1.0 + jnp.exp(-jnp.abs(z)))
        lane = lax.broadcasted_iota(jnp.int32, z.shape, 1) % CHUNK
        cum = lf
        s = 1
        while s < CHUNK:
            cum = cum + jnp.where(lane >= s, pltpu.roll(cum, s, 1), 0.0)
            s *= 2
        sub = lax.broadcasted_iota(jnp.int32, z.shape, 0)
        g_ref[...] = jnp.where(sub < MLH, z, cum)
        s_ref[...] = _sigmoid(-z)

    return pl.pallas_call(
        body, name="gates_fwd",
        out_shape=[jax.ShapeDtypeStruct((8, t), F32), jax.ShapeDtypeStruct((8, t), F32)],
        compiler_params=_params(),
    )(pre_rows, bias_col)


def _chunk_terms(grow, gcol, m0):
    heads = range(MLH)
    i_row = [grow[h:h + 1, :] for h in heads]
    b_row = [grow[MLH + h:MLH + h + 1, :] for h in heads]
    i_col = [gcol[:, h:h + 1] for h in heads]
    b_col = [gcol[:, MLH + h:MLH + h + 1] for h in heads]
    b_last = [b_row[h][:, CHUNK - 1:CHUNK] for h in heads]
    tt = lax.broadcasted_iota(jnp.int32, (CHUNK, CHUNK), 0)
    ss = lax.broadcasted_iota(jnp.int32, (CHUNK, CHUNK), 1)
    log_d = [jnp.where(tt >= ss, b_col[h] - b_row[h] + i_row[h], -jnp.inf) for h in heads]
    row_max = [jnp.max(log_d[h], axis=1, keepdims=True) for h in heads]
    last_max = [jnp.max(b_last[h] - b_row[h] + i_row[h], axis=1, keepdims=True) for h in heads]
    m_t = [jnp.maximum(b_col[h] + m0[h], row_max[h]) for h in heads]
    m1 = [jnp.maximum(b_last[h] + m0[h], last_max[h]) for h in heads]
    dm = [jnp.exp(log_d[h] - m_t[h]) for h in heads]
    wi = [jnp.exp(b_col[h] + m0[h] - m_t[h]) for h in heads]
    ws = [jnp.exp(b_last[h] - b_col[h] + i_col[h] - m1[h]) for h in heads]
    dec = [jnp.exp(b_last[h] + m0[h] - m1[h]) for h in heads]
    return [(dm[h], wi[h], m_t[h], ws[h], dec[h], m1[h]) for h in heads]


def _mlstm_fwd(qk, proj, grow, gcol, gain):
    t = qk.shape[0]
    nc = t // CHUNK

    def body(qk_ref, v_ref, o_ref, grow_ref, gcol_ref, g_ref, h_ref, y_ref, cs_ref, st_ref, c_scr, st_scr):
        @pl.when(pl.program_id(0) == 0)
        def _():
            c_scr[...] = jnp.zeros_like(c_scr)
            st_scr[...] = jnp.zeros_like(st_scr)

        grow_v, gcol_v = grow_ref[...], gcol_ref[...]
        heads = range(MLH)
        q = [qk_ref[:, h * DQK:(h + 1) * DQK] for h in heads]
        k = [qk_ref[:, MLH * DQK + h * DQK:MLH * DQK + (h + 1) * DQK] for h in heads]
        v = [v_ref[:, h * DV:(h + 1) * DV] for h in heads]
        c0 = [c_scr[h] for h in heads]
        n0 = [st_scr[h, 0:1, :] for h in heads]
        for h in heads:
            cs_ref[0, h] = c0[h]
            st_ref[0, h] = st_scr[h]
        terms = _chunk_terms(grow_v, gcol_v, [st_scr[h, 1:2, 0:1] for h in heads])
        a = [_dot_nt(q[h], k[h]) for h in heads]
        qc = [_dot_nt(q[h], c0[h].astype(BF16)) for h in heads]
        s = [a[h] * terms[h][0] for h in heads]
        sv = [_dot_nn(s[h].astype(BF16), v[h]) for h in heads]
        upd = [_dot_tn((terms[h][3] * v[h]).astype(BF16), k[h]) for h in heads]
        den = [terms[h][1] * jnp.sum(q[h].astype(F32) * n0[h], axis=1, keepdims=True)
               + jnp.sum(s[h], axis=1, keepdims=True) for h in heads]
        hv = [(terms[h][1] * qc[h] + sv[h]) / jnp.maximum(jnp.abs(den[h]), jnp.exp(-terms[h][2])) for h in heads]
        for h in heads:
            sl = slice(h * DV, (h + 1) * DV)
            h_ref[:, sl] = hv[h]
            xn, _ = _rms(hv[h])
            y_ref[:, sl] = (_sigmoid(o_ref[:, sl].astype(F32)) * xn * g_ref[:, sl]).astype(BF16)
        for h in heads:
            dec, m1 = terms[h][4], terms[h][5]
            c_scr[h] = dec * c0[h] + upd[h]
            st_scr[h, 0:1, :] = dec * n0[h] + jnp.sum(terms[h][3] * k[h].astype(F32), axis=0, keepdims=True)
            st_scr[h, 1:2, :] = jnp.broadcast_to(m1, (1, DQK))

    return pl.pallas_call(
        body, name="mlstm_fwd", grid=(nc,),
        in_specs=[pl.BlockSpec((CHUNK, D), lambda c: (c, 0)), pl.BlockSpec((CHUNK, D), lambda c: (c, C_V // D)),
                  pl.BlockSpec((CHUNK, D), lambda c: (c, C_O // D)),
                  pl.BlockSpec((8, CHUNK), lambda c: (0, c)), pl.BlockSpec((CHUNK, 8), lambda c: (c, 0)),
                  pl.BlockSpec((1, D), lambda c: (0, 0))],
        out_specs=[pl.BlockSpec((CHUNK, D), lambda c: (c, 0)), pl.BlockSpec((CHUNK, D), lambda c: (c, 0)),
                   pl.BlockSpec((1, MLH, DV, DQK), lambda c: (c, 0, 0, 0)),
                   pl.BlockSpec((1, MLH, 8, DQK), lambda c: (c, 0, 0, 0))],
        out_shape=[jax.ShapeDtypeStruct((t, D), F32), jax.ShapeDtypeStruct((t, D), BF16),
                   jax.ShapeDtypeStruct((nc, MLH, DV, DQK), F32), jax.ShapeDtypeStruct((nc, MLH, 8, DQK), F32)],
        scratch_shapes=[pltpu.VMEM((MLH, DV, DQK), F32), pltpu.VMEM((MLH, 8, DQK), F32)],
        compiler_params=_params(dimension_semantics=("arbitrary",)),
    )(qk, proj, proj, grow, gcol, gain)


def _mlstm_bwd(qk, proj, grow, gcol, sneg_col, cs, st, hraw, dh, dproj):
    t = qk.shape[0]
    nc = t // CHUNK

    def rev(c):
        return nc - 1 - c

    def nxt(c):
        return jnp.minimum(nc - c, nc - 1)

    def body(qk_ref, v_ref, grow_ref, gcol_ref, sneg_ref, cs_ref, st_ref, cs1_ref, st1_ref, h_ref, dh_ref, _,
             dqk_ref, dv_ref, dif_ref, dbif_ref, dc_scr, dn_scr):
        @pl.when(pl.program_id(0) == 0)
        def _():
            dc_scr[...] = jnp.zeros_like(dc_scr)
            dn_scr[...] = jnp.zeros_like(dn_scr)
            dbif_ref[...] = jnp.zeros_like(dbif_ref)

        grow_v, gcol_v, sneg = grow_ref[...], gcol_ref[...], sneg_ref[...]
        tt = lax.broadcasted_iota(jnp.int32, (CHUNK, CHUNK), 0)
        ss = lax.broadcasted_iota(jnp.int32, (CHUNK, CHUNK), 1)
        lane8 = lax.broadcasted_iota(jnp.int32, (CHUNK, 8), 1)
        heads = range(MLH)
        q = [qk_ref[:, h * DQK:(h + 1) * DQK] for h in heads]
        k = [qk_ref[:, MLH * DQK + h * DQK:MLH * DQK + (h + 1) * DQK] for h in heads]
        qf, kf = [a.astype(F32) for a in q], [a.astype(F32) for a in k]
        vb = [v_ref[:, h * DV:(h + 1) * DV].astype(BF16) for h in heads]
        c0 = [cs_ref[0, h] for h in heads]
        n0 = [st_ref[0, h, 0:1, :] for h in heads]
        dc1 = [dc_scr[h] for h in heads]
        dn1 = [dn_scr[h, 0:1, :] for h in heads]
        terms = _chunk_terms(grow_v, gcol_v, [st_ref[0, h, 1:2, 0:1] for h in heads])
        dm, wi, ws = [t[0] for t in terms], [t[1] for t in terms], [t[3] for t in terms]
        s = [_dot_nt(q[h], k[h]) * dm[h] for h in heads]
        den = [wi[h] * jnp.sum(qf[h] * n0[h], axis=1, keepdims=True) + jnp.sum(s[h], axis=1, keepdims=True)
               for h in heads]
        floor = [jnp.exp(-terms[h][2]) for h in heads]
        g = [jnp.maximum(jnp.abs(den[h]), floor[h]) for h in heads]
        dh_v = [dh_ref[:, h * DV:(h + 1) * DV] for h in heads]
        dnum = [dh_v[h] / g[h] for h in heads]
        dden = [-jnp.sum(dh_v[h] * h_ref[:, h * DV:(h + 1) * DV], axis=1, keepdims=True) / g[h] for h in heads]
        dden = [jnp.where(jnp.abs(den[h]) > floor[h], dden[h] * jnp.sign(den[h]), 0.0) for h in heads]
        dnum_b = [a.astype(BF16) for a in dnum]
        dc1_b = [a.astype(BF16) for a in dc1]
        da = [((_dot_nt(dnum_b[h], vb[h]) + dden[h]) * dm[h]).astype(BF16) for h in heads]
        dq_inter = [_dot_nn(dnum_b[h], c0[h].astype(BF16)) for h in heads]
        dk_inter = [_dot_nn(vb[h], dc1_b[h]) for h in heads]
        dv_inter = [_dot_nt(k[h], dc1_b[h]) for h in heads]
        dc_new = [_dot_tn((wi[h] * dnum[h]).astype(BF16), q[h]) for h in heads]
        dq = [_dot_nn(da[h], k[h]) + wi[h] * (dq_inter[h] + dden[h] * n0[h]) for h in heads]
        dk = [_dot_tn(da[h], q[h]) + ws[h] * (dk_inter[h] + dn1[h]) for h in heads]
        dv = [_dot_tn(s[h].astype(BF16), dnum_b[h]) + ws[h] * dv_inter[h] for h in heads]
        for h in heads:
            dqk_ref[:, h * DQK:(h + 1) * DQK] = dq[h]
            dqk_ref[:, MLH * DQK + h * DQK:MLH * DQK + (h + 1) * DQK] = dk[h]
            dv_ref[:, h * DV:(h + 1) * DV] = dv[h].astype(BF16)
        rk = [jnp.sum(kf[h] * dk[h], axis=1, keepdims=True) for h in heads]
        df = [jnp.sum(qf[h] * dq[h], axis=1, keepdims=True) - rk[h] for h in heads]
        df_row = [jnp.sum(jnp.where(tt == ss, df[h], 0.0), axis=0, keepdims=True) for h in heads]
        suffix = [jnp.sum(jnp.where(ss >= tt, df_row[h], 0.0), axis=1, keepdims=True) for h in heads]
        cross = [jnp.sum(jnp.sum(dc1[h] * cs1_ref[0, h], axis=0, keepdims=True), axis=1, keepdims=True)
                 + jnp.sum(dn1[h] * st1_ref[0, h, 0:1, :], axis=1, keepdims=True) for h in heads]
        dif = jnp.zeros((CHUNK, 8), F32)
        for h in heads:
            dpf = (suffix[h] + cross[h]) * sneg[:, MLH + h:MLH + h + 1]
            dif = dif + jnp.where(lane8 == h, rk[h], 0.0) + jnp.where(lane8 == MLH + h, dpf, 0.0)
            dc_scr[h] = terms[h][4] * dc1[h] + dc_new[h]
            dn_scr[h, 0:1, :] = terms[h][4] * dn1[h] + jnp.sum(wi[h] * dden[h] * qf[h], axis=0, keepdims=True)
        dif_ref[...] = dif
        dbif_ref[...] += jnp.sum(dif, axis=0, keepdims=True)

    return pl.pallas_call(
        body, name="mlstm_bwd", grid=(nc,),
        in_specs=[pl.BlockSpec((CHUNK, D), lambda c: (rev(c), 0)),
                  pl.BlockSpec((CHUNK, D), lambda c: (rev(c), C_V // D)),
                  pl.BlockSpec((8, CHUNK), lambda c: (0, rev(c))),
                  pl.BlockSpec((CHUNK, 8), lambda c: (rev(c), 0)),
                  pl.BlockSpec((CHUNK, 8), lambda c: (rev(c), 0)),
                  pl.BlockSpec((1, MLH, DV, DQK), lambda c: (rev(c), 0, 0, 0)),
                  pl.BlockSpec((1, MLH, 8, DQK), lambda c: (rev(c), 0, 0, 0)),
                  pl.BlockSpec((1, MLH, DV, DQK), lambda c: (nxt(c), 0, 0, 0)),
                  pl.BlockSpec((1, MLH, 8, DQK), lambda c: (nxt(c), 0, 0, 0)),
                  pl.BlockSpec((CHUNK, D), lambda c: (rev(c), 0)),
                  pl.BlockSpec((CHUNK, D), lambda c: (rev(c), 0)), _ANY],
        out_specs=[pl.BlockSpec((CHUNK, D), lambda c: (rev(c), 0)),
                   pl.BlockSpec((CHUNK, D), lambda c: (rev(c), C_V // D)),
                   pl.BlockSpec((CHUNK, 8), lambda c: (rev(c), 0)),
                   pl.BlockSpec((1, 8), lambda c: (0, 0))],
        out_shape=[jax.ShapeDtypeStruct((t, D), F32), jax.ShapeDtypeStruct((t, NP), BF16),
                   jax.ShapeDtypeStruct((t, 8), F32), jax.ShapeDtypeStruct((1, 8), F32)],
        scratch_shapes=[pltpu.VMEM((MLH, DV, DQK), F32), pltpu.VMEM((MLH, 8, DQK), F32)],
        input_output_aliases={11: 1}, compiler_params=_params(dimension_semantics=("arbitrary",)),
    )(qk, proj, grow, gcol, sneg_col, cs, st, cs, st, hraw, dh, dproj)


_ANY = pl.BlockSpec(memory_space=pl.ANY)


_SW_SCALE = HD ** -0.5
_KVB = C_KV // (2 * SWKV * HD)


def _swa_mask(n):
    ki = lax.broadcasted_iota(jnp.int32, (2 * WIN, SWG * WIN), 0)
    qi = lax.broadcasted_iota(jnp.int32, (2 * WIN, SWG * WIN), 1) % WIN
    return (ki > qi) & (ki <= qi + WIN) & ((n > 0) | (ki >= WIN))


def _group_rows(x_ref, hk):
    return jnp.concatenate([x_ref[:, (hk * SWG + g) * HD:(hk * SWG + g + 1) * HD] for g in range(SWG)], axis=0)


def _group_lanes(x_ref, hk):
    return jnp.concatenate([x_ref[hk * SWG + g:hk * SWG + g + 1, :] for g in range(SWG)], axis=1)


def _sink_lanes(sink_ref, hk):
    return jnp.concatenate([jnp.broadcast_to(sink_ref[:, hk * SWG + g:hk * SWG + g + 1], (1, WIN))
                            for g in range(SWG)], axis=1)


def _swa_fwd(proj, sinks):
    t = proj.shape[0]
    nb = t // WIN

    def body(q_ref, kvc_ref, kvp_ref, sink_ref, y_ref, lse_ref):
        valid = _swa_mask(pl.program_id(0))
        for hk in range(SWKV):
            ks = slice(hk * HD, (hk + 1) * HD)
            vs = slice(SWKV * HD + hk * HD, SWKV * HD + (hk + 1) * HD)
            kb = jnp.concatenate([kvp_ref[:, ks], kvc_ref[:, ks]], axis=0).astype(BF16)
            vb = jnp.concatenate([kvp_ref[:, vs], kvc_ref[:, vs]], axis=0).astype(BF16)
            q4 = _group_rows(q_ref, hk).astype(BF16)
            sink = _sink_lanes(sink_ref, hk)
            logits = jnp.where(valid, _dot_nt(kb, q4) * _SW_SCALE, -jnp.inf)
            m = jnp.maximum(jnp.max(logits, axis=0, keepdims=True), sink)
            p = jnp.exp(logits - m)
            denom = jnp.sum(p, axis=0, keepdims=True) + jnp.exp(sink - m)
            y4 = _dot_tn((p / denom).astype(BF16), vb).astype(BF16)
            lse4 = m + jnp.log(denom)
            for g in range(SWG):
                hq = hk * SWG + g
                y_ref[:, hq * HD:(hq + 1) * HD] = y4[g * WIN:(g + 1) * WIN]
                lse_ref[hq:hq + 1, :] = lse4[:, g * WIN:(g + 1) * WIN]

    return pl.pallas_call(
        body, name="swa_fwd", grid=(nb,),
        in_specs=[pl.BlockSpec((WIN, D), lambda n: (n, C_QSW // D)),
                  pl.BlockSpec((WIN, 512), lambda n: (n, _KVB)),
                  pl.BlockSpec((WIN, 512), lambda n: (jnp.maximum(n - 1, 0), _KVB)),
                  pl.BlockSpec((1, SWH), lambda n: (0, 0))],
        out_specs=[pl.BlockSpec((WIN, D), lambda n: (n, 0)), pl.BlockSpec((SWH, WIN), lambda n: (0, n))],
        out_shape=[jax.ShapeDtypeStruct((t, D), BF16), jax.ShapeDtypeStruct((SWH, t), F32)],
        compiler_params=_params(),
    )(proj, proj, proj, sinks)


def _swa_bwd(proj, sinks, lse, dyb, dproj):
    t = proj.shape[0]
    nb = t // WIN

    def body(q_ref, kvc_ref, kvp_ref, sink_ref, lse_ref, dy_ref, _, dq_ref, dself_ref, dprev_ref, ds_ref):
        @pl.when(pl.program_id(0) == 0)
        def _():
            ds_ref[...] = jnp.zeros_like(ds_ref)

        valid = _swa_mask(pl.program_id(0))
        kvh = range(SWKV)
        ks = [slice(hk * HD, (hk + 1) * HD) for hk in kvh]
        vs = [slice(SWKV * HD + hk * HD, SWKV * HD + (hk + 1) * HD) for hk in kvh]
        kb = [jnp.concatenate([kvp_ref[:, ks[hk]], kvc_ref[:, ks[hk]]], axis=0).astype(BF16) for hk in kvh]
        vb = [jnp.concatenate([kvp_ref[:, vs[hk]], kvc_ref[:, vs[hk]]], axis=0).astype(BF16) for hk in kvh]
        qb = [_group_rows(q_ref, hk).astype(BF16) for hk in kvh]
        dyb_ = [_group_rows(dy_ref, hk).astype(BF16) for hk in kvh]
        lse4 = [_group_lanes(lse_ref, hk) for hk in kvh]
        logits = [_dot_nt(kb[hk], qb[hk]) for hk in kvh]
        dpt = [_dot_nt(vb[hk], dyb_[hk]) for hk in kvh]
        p = [jnp.exp(jnp.where(valid, logits[hk] * _SW_SCALE, -jnp.inf) - lse4[hk]) for hk in kvh]
        delta = [jnp.sum(p[hk] * dpt[hk], axis=0, keepdims=True) for hk in kvh]
        dsm = [(p[hk] * (dpt[hk] - delta[hk])).astype(BF16) for hk in kvh]
        dq4 = [(_dot_tn(dsm[hk], kb[hk]) * _SW_SCALE).astype(BF16) for hk in kvh]
        dkb = [_dot_nn(dsm[hk], qb[hk]) * _SW_SCALE for hk in kvh]
        dvb = [_dot_nn(p[hk].astype(BF16), dyb_[hk]) for hk in kvh]
        for hk in kvh:
            dsink4 = jnp.exp(_sink_lanes(sink_ref, hk) - lse4[hk]) * delta[hk]
            for g in range(SWG):
                hq = hk * SWG + g
                dq_ref[:, hq * HD:(hq + 1) * HD] = dq4[hk][g * WIN:(g + 1) * WIN]
                ds_ref[:, hq:hq + 1] += -jnp.sum(dsink4[:, g * WIN:(g + 1) * WIN], axis=1, keepdims=True)
            dprev_ref[:, ks[hk]] = dkb[hk][:WIN]
            dself_ref[:, ks[hk]] = dkb[hk][WIN:]
            dprev_ref[:, vs[hk]] = dvb[hk][:WIN]
            dself_ref[:, vs[hk]] = dvb[hk][WIN:]

    return pl.pallas_call(
        body, name="swa_bwd", grid=(nb,),
        in_specs=[pl.BlockSpec((WIN, D), lambda n: (n, C_QSW // D)),
                  pl.BlockSpec((WIN, 512), lambda n: (n, _KVB)),
                  pl.BlockSpec((WIN, 512), lambda n: (jnp.maximum(n - 1, 0), _KVB)),
                  pl.BlockSpec((1, SWH), lambda n: (0, 0)),
                  pl.BlockSpec((SWH, WIN), lambda n: (0, n)),
                  pl.BlockSpec((WIN, D), lambda n: (n, 0)), _ANY],
        out_specs=[pl.BlockSpec((WIN, D), lambda n: (n, C_QSW // D)), pl.BlockSpec((WIN, 512), lambda n: (n, 0)),
                   pl.BlockSpec((WIN, 512), lambda n: (jnp.maximum(n - 1, 0), 0)),
                   pl.BlockSpec((1, SWH), lambda n: (0, 0))],
        out_shape=[jax.ShapeDtypeStruct((t, NP), BF16), jax.ShapeDtypeStruct((t, 512), F32),
                   jax.ShapeDtypeStruct((t, 512), F32), jax.ShapeDtypeStruct((1, SWH), F32)],
        input_output_aliases={6: 0}, compiler_params=_params(),
    )(proj, proj, proj, sinks, lse, dyb, dproj)


def _kv_combine(dself, dnext, dif, dproj):
    t = dself.shape[0]
    rows = _pick(t, 512)

    def body(a_ref, b_ref, dif_ref, _, o_ref):
        row = pl.program_id(0) * rows + lax.broadcasted_iota(jnp.int32, (rows, 1), 0)
        o_ref[:, 0:512] = (a_ref[...] + jnp.where(row < t - WIN, b_ref[...], 0.0)).astype(BF16)
        lane = lax.broadcasted_iota(jnp.int32, (rows, 128), 1)
        dif_v = dif_ref[...]
        first = jnp.zeros((rows, 128), F32)
        for col in range(8):
            first = first + jnp.where(lane == col, dif_v[:, col:col + 1], 0.0)
        o_ref[:, 512:640] = first.astype(BF16)
        o_ref[:, 640:512 + IFW] = jnp.zeros((rows, IFW - 128), BF16)

    return pl.pallas_call(
        body, name="kv_combine", grid=(t // rows,),
        in_specs=[pl.BlockSpec((rows, 512), lambda n: (n, 0)), pl.BlockSpec((rows, 512), lambda n: (n, 0)),
                  pl.BlockSpec((rows, 8), lambda n: (n, 0)), _ANY],
        out_specs=pl.BlockSpec((rows, 512 + IFW), lambda n: (n, C_KV // (512 + IFW))),
        out_shape=jax.ShapeDtypeStruct((t, NP), BF16), input_output_aliases={3: 0}, compiler_params=_params(),
    )(dself, dnext, dif, dproj)


def _sds(t, n, dtype):
    return jax.ShapeDtypeStruct((t, n), dtype)


def _proj_in(x, gain, w_in):
    t = x.shape[0]
    tm, tn = _pick(t, 1024), 2 * IFW

    def body(x_ref, g_ref, w_ref, h_ref, p_ref, gate_ref, h_scr):
        j = pl.program_id(1)

        @pl.when(j == 0)
        def _():
            xn, _ = _rms(x_ref[...])
            h = (xn * g_ref[...]).astype(BF16)
            h_scr[...] = h
            h_ref[...] = h

        acc = _dot_nn(h_scr[...], w_ref[...])
        p_ref[...] = acc.astype(BF16)

        @pl.when(j == C_IF // tn)
        def _():
            gate_ref[...] = acc[:, C_IF % tn:C_IF % tn + 128]

    return pl.pallas_call(
        body, name="mm_in", grid=(t // tm, NP // tn),
        in_specs=[pl.BlockSpec((tm, D), lambda i, j: (i, 0)), pl.BlockSpec((1, D), lambda i, j: (0, 0)),
                  pl.BlockSpec((D, tn), lambda i, j: (0, j))],
        out_specs=[pl.BlockSpec((tm, D), lambda i, j: (i, 0)), pl.BlockSpec((tm, tn), lambda i, j: (i, j)),
                   pl.BlockSpec((tm, 128), lambda i, j: (i, 0))],
        out_shape=[_sds(t, D, BF16), _sds(t, NP, BF16), _sds(t, 128, F32)],
        scratch_shapes=[pltpu.VMEM((tm, D), BF16)],
        compiler_params=_params(dimension_semantics=("arbitrary", "arbitrary")),
    )(x, gain, w_in)


def _branch_merge(ya, yb, wa, wb, proj):
    t = ya.shape[0]

    def epilogue(accs, ins, outs, i, j):
        za, zb = accs
        merged = _sigmoid(ins[0][...].astype(F32)) * za + _sigmoid(ins[1][...].astype(F32)) * zb
        outs[0][...] = merged.astype(BF16)
        outs[1][...] = za.astype(BF16)
        outs[2][...] = zb.astype(BF16)

    return _mm_ep([(ya, wa), (yb, wb)], "nn", "mm_branch_merge", epilogue, [(proj, _tile(C_GA)), (proj, _tile(C_GB))],
                  [(_sds(t, D, BF16), _tile())] * 3, 1024, 1024)


def _dmerged_bwd(dxb, w_out, proj, za, zb):
    t = dxb.shape[0]

    def epilogue(accs, ins, outs, i, j):
        dm = accs[0]
        sa, sb = _sigmoid(ins[0][...].astype(F32)), _sigmoid(ins[1][...].astype(F32))
        outs[0][...] = (dm * sa).astype(BF16)
        outs[1][...] = (dm * sb).astype(BF16)
        outs[2][:, 0:D] = (dm * ins[2][...].astype(F32) * sa * (1.0 - sa)).astype(BF16)
        outs[2][:, D:2 * D] = (dm * ins[3][...].astype(F32) * sb * (1.0 - sb)).astype(BF16)

    gate_cols = lambda tm, tn: pl.BlockSpec((tm, 2 * D), lambda i, j, kk: (i, C_GA // (2 * D)))
    return _mm_ep([(dxb, w_out)], "nt", "mm_dmerged_bwd", epilogue,
                  [(proj, _tile(C_GA)), (proj, _tile(C_GB)), (za, _tile()), (zb, _tile())],
                  [(_sds(t, D, BF16), _tile()), (_sds(t, D, BF16), _tile()), (_sds(t, NP, BF16), gate_cols)], 1024, D)


def _dya_bwd(dza, wa, hraw, proj, g, dproj):
    t = dza.shape[0]

    def epilogue(accs, ins, outs, i, j):
        h_ref, o_ref, g_ref, _ = ins
        dh_ref, do_ref, dg_ref = outs

        @pl.when(i == 0)
        def _():
            dg_ref[...] = jnp.zeros_like(dg_ref)

        dy = accs[0]
        so = _sigmoid(o_ref[...].astype(F32))
        for h in range(MLH):
            sl = slice(h * DV, (h + 1) * DV)
            xn, rstd = _rms(h_ref[:, sl])
            gs = g_ref[:, sl]
            do_ref[:, sl] = (dy[:, sl] * xn * gs * so[:, sl] * (1.0 - so[:, sl])).astype(BF16)
            dhn = dy[:, sl] * so[:, sl]
            dg_ref[:, sl] += jnp.sum(dhn * xn, axis=0, keepdims=True)
            dh_ref[:, sl] = _rms_bwd(xn, rstd, dhn * gs)

    return _mm_ep([(dza, wa)], "nt", "mm_dya_bwd", epilogue,
                  [(hraw, _tile()), (proj, _tile(C_O)), (g, _row()), (dproj, lambda tm, tn: _ANY)],
                  [(_sds(t, D, F32), _tile()), (_sds(t, NP, BF16), _tile(C_O)), (_sds(1, D, F32), _row())],
                  1024, D, aliases={3: 1})


def _up_act(hn, w_up):
    t = hn.shape[0]

    def epilogue(accs, ins, outs, i, j):
        r = jnp.maximum(accs[0], 0.0)
        outs[0][...] = (r * r).astype(BF16)
        outs[1][...] = accs[0].astype(BF16)

    return _mm_ep([(hn, w_up)], "nn", "mm_up_act", epilogue, [],
                  [(_sds(t, DFF, BF16), _tile()), (_sds(t, DFF, BF16), _tile())], 1024, 1024)


def _da_du(dxb, w_down, u):
    t = dxb.shape[0]

    def epilogue(accs, ins, outs, i, j):
        outs[0][...] = (accs[0] * 2.0 * jnp.maximum(ins[0][...].astype(F32), 0.0)).astype(BF16)

    return _mm_ep([(dxb, w_down)], "nt", "mm_da_du", epilogue, [(u, _tile())], [(_sds(t, DFF, BF16), _tile())],
                  1024, 1024)[0]


def _resid_norm_mm(a, w, x, g, name):
    t = x.shape[0]

    def epilogue(accs, ins, outs, i, j):
        x1 = ins[0][...] + accs[0]
        outs[0][...] = x1
        xn, _ = _rms(x1)
        outs[1][...] = (xn * ins[1][...]).astype(BF16)

    return _mm_ep([(a, w)], "nn", name, epilogue, [(x, _tile()), (g, _row())],
                  [(_sds(t, D, F32), _tile()), (_sds(t, D, BF16), _tile())], 1024, D)


def _norm_bwd_mm(dy, w, x, g, dres, name):
    t = x.shape[0]

    def epilogue(accs, ins, outs, i, j):
        @pl.when(i == 0)
        def _():
            outs[2][...] = jnp.zeros_like(outs[2])

        dh = accs[0]
        xn, rstd = _rms(ins[0][...])
        outs[2][...] += jnp.sum(dh * xn, axis=0, keepdims=True)
        dx = ins[2][...] + _rms_bwd(xn, rstd, dh * ins[1][...])
        outs[0][...] = dx
        outs[1][...] = dx.astype(BF16)

    return _mm_ep([(dy, w)], "nt", name, epilogue, [(x, _tile()), (g, _row()), (dres, _tile())],
                  [(_sds(t, D, F32), _tile()), (_sds(t, D, BF16), _tile()), (_sds(1, D, F32), _row())], 1024, D)


def _ple_final_mm(hn2, w_gate, x2, pp, target, gf):
    t = x2.shape[0]

    def epilogue(accs, ins, outs, i, j):
        loss_ref, dg_ref, dx_ref, dpp_ref, dgp_ref = outs

        @pl.when(i == 0)
        def _():
            loss_ref[...] = jnp.zeros_like(loss_ref)
            dg_ref[...] = jnp.zeros_like(dg_ref)

        gate = _sigmoid(accs[0])
        pp_v = ins[1][...]
        x3 = ins[0][...] + gate * pp_v
        xn, rstd = _rms(x3)
        gf_v = ins[3][...]
        err = xn * gf_v - ins[2][...]
        loss_ref[...] += (0.5 / D) * jnp.sum(jnp.sum(err * err, axis=1, keepdims=True), axis=0, keepdims=True)
        dy = err * (1.0 / D)
        dg_ref[...] += jnp.sum(dy * xn, axis=0, keepdims=True)
        dx3 = _rms_bwd(xn, rstd, dy * gf_v)
        dx_ref[...] = dx3
        dpp_ref[...] = (dx3 * gate).astype(BF16)
        dgp_ref[...] = (dx3 * pp_v * gate * (1.0 - gate)).astype(BF16)

    one = lambda tm, tn: pl.BlockSpec((1, 1), lambda i, j, kk: (0, 0))
    return _mm_ep([(hn2, w_gate)], "nn", "mm_ple_final", epilogue,
                  [(x2, _tile()), (pp, _tile()), (target, _tile()), (gf, _row())],
                  [(_sds(1, 1, F32), one), (_sds(1, D, F32), _row()), (_sds(t, D, F32), _tile()),
                   (_sds(t, D, BF16), _tile()), (_sds(t, D, BF16), _tile())], 512, D)


_WIN_SEGMENTS = ((0, 3072, C_QK), (3072, 8, C_IF), (3080, 1024, C_QSW), (4104, 256, C_KV), (4360, 256, C_KV + 256),
                 (4616, 1024, C_GA), (5640, 1024, C_GB))
_WIN_SHARD = N_IN // 4


def _win_pieces():
    out = []
    for src, width, dst in _WIN_SEGMENTS:
        while width:
            chip, col = divmod(src, _WIN_SHARD)
            n = min(width, _WIN_SHARD - col)
            out.append((chip, col, n, dst))
            src, dst, width = src + n, dst + n, width - n
    return out


def _win_pad(shards):
    rows = shards.shape[1]
    tr = _pick(rows, 256)

    def body(s_ref, o_ref):
        for chip, col, n, dst in _win_pieces():
            o_ref[:, dst:dst + n] = s_ref[chip, :, col:col + n]
        o_ref[:, C_IF + 8:NP] = jnp.zeros((tr, NP - C_IF - 8), shards.dtype)

    return pl.pallas_call(
        body, name="win_pad", grid=(rows // tr,), in_specs=[pl.BlockSpec((4, tr, _WIN_SHARD), lambda i: (0, i, 0))],
        out_specs=pl.BlockSpec((tr, NP), lambda i: (i, 0)), out_shape=jax.ShapeDtypeStruct((rows, NP), shards.dtype),
        compiler_params=_params(),
    )(shards)


def _win_unpad(wp):
    rows = wp.shape[0]
    tr = _pick(rows, 256)

    def body(p_ref, o_ref):
        for chip, col, n, dst in _win_pieces():
            o_ref[chip, :, col:col + n] = p_ref[:, dst:dst + n]

    return pl.pallas_call(
        body, name="win_unpad", grid=(rows // tr,), in_specs=[pl.BlockSpec((tr, NP), lambda i: (i, 0))],
        out_specs=pl.BlockSpec((4, tr, _WIN_SHARD), lambda i: (0, i, 0)),
        out_shape=jax.ShapeDtypeStruct((4, rows, _WIN_SHARD), wp.dtype), compiler_params=_params(),
    )(wp)


def _local_step(x, p, target, w, late_weights=None, early_grads=None, mid_grads=None, last_grad=None):
    t = x.shape[0]
    pb = p.astype(BF16)
    w = dict(w)

    h0, proj, gates = _proj_in(x, w["norm_mix_g"], w["w_in"])
    qk = _conv_silu_fwd(proj, w["conv_qk"])
    grow, sneg_row = _gates_fwd(gates[:, 0:8].T, w["b_if"].reshape(8, 1))
    gcol, sneg_col = grow.T, sneg_row.T
    hraw, ya, cs, st = _mlstm_fwd(qk, proj, grow, gcol, w["mlstm_norm_g"])
    yb, lse = _swa_fwd(proj, w["sinks"])
    if late_weights is not None:
        w.update(late_weights(yb))
    merged, za, zb = _branch_merge(ya, yb, w["w_branch_a"], w["w_branch_b"], proj)
    x1, hn1 = _resid_norm_mm(merged, w["w_out"], x, w["norm_mlp_g"], "mm_out_norm")
    act, u = _up_act(hn1, w["w_up"])
    x2, hn2 = _resid_norm_mm(act, w["w_down"], x1, w["norm_ple_g"], "mm_down_norm")
    pp = _mm(pb, w["w_ple_proj"], "nn", F32, "mm_ple_proj")
    loss, d_final_g, dx3, dpp, dgpre = _ple_final_mm(hn2, w["w_ple_gate"], x2, pp, target, w["final_norm_g"])

    g = {"final_norm_g": d_final_g}
    g["w_ple_proj"] = _mm(pb, dpp, "tn", F32, "mm_d_ple_proj", out_chunks=4)
    g["w_ple_gate"] = _mm(hn2, dgpre, "tn", F32, "mm_d_ple_gate")
    dx2, dx2b, g["norm_ple_g"] = _norm_bwd_mm(dgpre, w["w_ple_gate"], x2, w["norm_ple_g"], dx3, "mm_dhn2_norm")
    g["w_down"] = _mm(act, dx2b, "tn", F32, "mm_d_down")
    du = _da_du(dx2b, w["w_down"], u)
    g["w_up"] = _mm(hn1, du, "tn", F32, "mm_d_up", out_chunks=4)
    dx1, dx1b, g["norm_mlp_g"] = _norm_bwd_mm(du, w["w_up"], x1, w["norm_mlp_g"], dx2, "mm_dhn1_norm")
    g["w_out"] = _mm(merged, dx1b, "tn", F32, "mm_d_out")
    dza, dzb, dproj = _dmerged_bwd(dx1b, w["w_out"], proj, za, zb)
    g["w_branch_a"] = _mm(ya, dza, "tn", F32, "mm_d_branch_a")
    g["w_branch_b"] = _mm(yb, dzb, "tn", F32, "mm_d_branch_b")
    gain = w["mlstm_norm_g"] if early_grads is None else w["mlstm_norm_g"] + early_grads(g)
    dyb = _mm(dzb, w["w_branch_b"], "nt", F32, "mm_dyb")
    dhraw, dproj, g["mlstm_norm_g"] = _dya_bwd(dza, w["w_branch_a"], hraw, proj, gain, dproj)
    if mid_grads is not None:
        sneg_col = sneg_col + mid_grads(dhraw)
    dqk, dproj, dif, g["b_if"] = _mlstm_bwd(qk, proj, grow, gcol, sneg_col, cs, st, hraw, dhraw, dproj)
    dc, g["conv_qk"] = _conv_silu_bwd_a(proj, w["conv_qk"], dqk)
    dproj = _conv_silu_bwd_b(dc, w["conv_qk"], dproj)
    dproj, dkv_self, dkv_prev, g["sinks"] = _swa_bwd(proj, w["sinks"], lse, dyb, dproj)
    dproj = _kv_combine(dkv_self, dkv_prev, dif, dproj)
    g["w_in"] = _mm(h0, dproj, "tn", F32, "mm_d_in")
    gain = w["norm_mix_g"] if last_grad is None else w["norm_mix_g"] + last_grad(g)
    grad_x, _, g["norm_mix_g"] = _norm_bwd_mm(dproj, w["w_in"], x, gain, dx1, "mm_dh0_norm")
    return loss, grad_x, g


_W4 = ("w_branch_a", "w_branch_b", "w_out", "w_ple_gate")
_SHARDED_NAMES = ("w_in", "w_up", "w_down", "w_ple_proj", "conv_qk") + _W4
_SMALL_ROWS = 16
_CONV_ROW = 8


def _group(s):
    return [s["w_in"], jnp.concatenate([s[n] for n in _W4], axis=0), s["w_up"], s["w_down"], s["w_ple_proj"]]


def _ungroup(arrs):
    out = {"w_in": arrs[0], "w_up": arrs[2], "w_down": arrs[3], "w_ple_proj": arrs[4]}
    rows = arrs[1].shape[0] // len(_W4)
    for i, n in enumerate(_W4):
        out[n] = arrs[1][i * rows:(i + 1) * rows]
    return out


def _rows_tile(rows):
    return 256 if rows % 256 == 0 else rows


_SMALL = ("norm_mix_g", "mlstm_norm_g", "norm_mlp_g", "norm_ple_g", "final_norm_g")


def _pack_small(vals, extra=None, conv=None):
    rows = [vals[n].reshape(1, D) for n in _SMALL]
    tail = [vals["b_if"].reshape(1, 8), vals["sinks"].reshape(1, SWH)]
    used = 8 + SWH
    if extra is not None:
        tail.append(extra.reshape(1, 1))
        used += 1
    tail.append(jnp.zeros((1, D - used), F32))
    rows.append(jnp.concatenate(tail, axis=1))
    rows.append(jnp.zeros((_CONV_ROW - len(rows), D), F32))
    rows.append(jnp.zeros((CONV, D), F32) if conv is None else conv)
    rows.append(jnp.zeros((_SMALL_ROWS - _CONV_ROW - CONV, D), F32))
    return jnp.concatenate(rows, axis=0)


def _unpack_small(slab, shapes):
    out = {n: slab[i].reshape(shapes[n]) for i, n in enumerate(_SMALL)}
    out["b_if"] = slab[5, 0:8].reshape(shapes["b_if"])
    out["sinks"] = slab[5, 8:8 + SWH].reshape(shapes["sinks"])
    return out


_MESH = pl.DeviceIdType.MESH
_HBM = pl.BlockSpec(memory_space=pltpu.HBM)
_VMEM = pl.BlockSpec(memory_space=pltpu.VMEM)


def _place():
    x, y, c = lax.axis_index("x"), lax.axis_index("y"), lax.axis_index("c")
    return x, y, c, 2 * x + y


def _chip_peer(x, y, r):
    return (x ^ (r >> 1), y ^ (r & 1))


def _half(ref, which):
    h = ref.shape[-2] // 2
    return pl.ds(which * h, h)


def _allgather_weights(shards, conv):
    n = len(shards)

    def body(*refs):
        ins, conv_ref = refs[:n], refs[n]
        outs, conv_out = refs[n + 1:2 * n + 1], refs[2 * n + 1]
        send_a, recv_a, send_b, recv_b, send_c, recv_c, local_sems = refs[2 * n + 2:]
        x, y, c, j = _place()
        sibling = (x, y, 1 - c)
        local = [pltpu.make_async_copy(ins[k], outs[k].at[j], local_sems.at[k]) for k in range(n)]
        local.append(pltpu.make_async_copy(conv_ref, conv_out.at[j], local_sems.at[n]))
        for cp in local:
            cp.start()

        def copy_a(k, r, chip):
            rows = _half(ins[k], c)
            return pltpu.make_async_remote_copy(
                src_ref=ins[k].at[rows], dst_ref=outs[k].at[chip, rows], send_sem=send_a.at[3 * k + r - 1],
                recv_sem=recv_a.at[3 * k + r - 1], device_id=(*_chip_peer(x, y, r), c), device_id_type=_MESH)

        def copy_b(k, r, chip, which):
            rows = _half(ins[k], which)
            return pltpu.make_async_remote_copy(
                src_ref=outs[k].at[chip, rows], dst_ref=outs[k].at[chip, rows], send_sem=send_b.at[3 * k + r - 1],
                recv_sem=recv_b.at[3 * k + r - 1], device_id=sibling, device_id_type=_MESH)

        def copy_c(r, chip):
            return pltpu.make_async_remote_copy(
                src_ref=conv_ref, dst_ref=conv_out.at[chip], send_sem=send_c.at[r - 1],
                recv_sem=recv_c.at[r - 1], device_id=(*_chip_peer(x, y, r), c), device_id_type=_MESH)

        for k in range(n):
            for r in (1, 2, 3):
                copy_a(k, r, j).start()
        for r in (1, 2, 3):
            copy_c(r, j).start()
        for k in range(n):
            for r in (1, 2, 3):
                copy_a(k, r, j ^ r).wait_recv()
                copy_b(k, r, j ^ r, c).start()
        for k in range(n):
            for r in (1, 2, 3):
                copy_b(k, r, j ^ r, 1 - c).wait_recv()
        for r in (1, 2, 3):
            copy_c(r, j ^ r).wait_recv()
        for k in range(n):
            for r in (1, 2, 3):
                copy_a(k, r, j).wait_send()
                copy_b(k, r, j ^ r, c).wait_send()
        for r in (1, 2, 3):
            copy_c(r, j).wait_send()
        for cp in local:
            cp.wait()

    return pl.pallas_call(
        body, name="allgather_weights",
        out_shape=[jax.ShapeDtypeStruct((4,) + s.shape, s.dtype) for s in shards]
        + [jax.ShapeDtypeStruct((4,) + conv.shape, F32)],
        in_specs=[_HBM] * (n + 1), out_specs=[_HBM] * (n + 1),
        scratch_shapes=[pltpu.SemaphoreType.DMA((3 * n,))] * 4 + [pltpu.SemaphoreType.DMA((3,))] * 2
        + [pltpu.SemaphoreType.DMA((n + 1,))],
    )(*shards, conv)


_SEM = pl.BlockSpec(memory_space=pltpu.SEMAPHORE)
_DATAFLOW = pltpu.SideEffectType.DATAFLOW_SIDE_EFFECTING


def _late_peer_copy(src_ref, land_ref, send_sems, recv_sems, x, y, c, j, r, chip):
    return pltpu.make_async_remote_copy(
        src_ref=src_ref, dst_ref=land_ref.at[chip], send_sem=send_sems.at[r - 1], recv_sem=recv_sems.at[r - 1],
        device_id=(*_chip_peer(x, y, r), c), device_id_type=_MESH)


def _late_gather_start(rest):
    def body(rest_ref, land_ref, send_sems, recv_sems, rest_thru, land_thru, token):
        x, y, c, j = _place()
        for r in (1, 2, 3):
            _late_peer_copy(rest_ref, land_ref, send_sems, recv_sems, x, y, c, j, r, j).start()
        token[...] = jnp.zeros_like(token)

    j = 2 * lax.axis_index("x") + lax.axis_index("y")
    land = lax.dynamic_update_slice(lax.empty((4,) + rest.shape, rest.dtype), rest[None], (j, 0, 0))
    return pl.pallas_call(
        body, name="late_gather_start",
        out_shape=(pltpu.SemaphoreType.DMA((3,)), pltpu.SemaphoreType.DMA((3,)), pltpu.HBM(rest.shape, rest.dtype),
                   pltpu.HBM(land.shape, land.dtype), jax.ShapeDtypeStruct((8, 128), F32)),
        in_specs=(_HBM, _HBM), out_specs=(_SEM, _SEM, _HBM, _HBM, _VMEM), input_output_aliases={0: 2, 1: 3},
        compiler_params=pltpu.CompilerParams(has_side_effects=_DATAFLOW),
    )(pltpu.with_memory_space_constraint(rest, pltpu.HBM), pltpu.with_memory_space_constraint(land, pltpu.HBM))


def _late_gather_wait(send_sems, recv_sems, rest_thru, land_thru, after):
    def body(rest_ref, land_ref, send_sems, recv_sems, after_ref, rest_dead, got_ref):
        x, y, c, j = _place()
        for r in (1, 2, 3):
            cp = _late_peer_copy(rest_ref, land_ref, send_sems, recv_sems, x, y, c, j, r, j ^ r)
            cp.wait_send()
            cp.wait_recv()

    return pl.pallas_call(
        body, name="late_gather_wait",
        out_shape=(pltpu.HBM(rest_thru.shape, rest_thru.dtype), pltpu.HBM(land_thru.shape, land_thru.dtype)),
        in_specs=(_HBM, _HBM, _SEM, _SEM, _ANY), out_specs=(_HBM, _HBM), input_output_aliases={0: 0, 1: 1},
        compiler_params=pltpu.CompilerParams(has_side_effects=_DATAFLOW),
    )(rest_thru, land_thru, send_sems, recv_sems, after)[1]


def _pair_sum(g, theirs, j, c, name):
    _, h, cols = theirs.shape
    tr = _rows_tile(h)
    nb = h // tr

    def body(idx_ref, a_ref, b_ref, own_ref, ob_ref):
        s = a_ref[0] + b_ref[0]
        ob_ref[0] = s.astype(BF16)

        @pl.when(pl.program_id(1) == idx_ref[0])
        def _():
            own_ref[...] = s

    blk = pl.BlockSpec((1, tr, cols), lambda i, k, idx_ref: (k, i, 0))
    return pl.pallas_call(
        body, name=name,
        grid_spec=pltpu.PrefetchScalarGridSpec(
            num_scalar_prefetch=1, grid=(nb, 4),
            in_specs=[pl.BlockSpec((1, tr, cols), lambda i, k, idx_ref: (k, idx_ref[1] * nb + i, 0)), blk],
            out_specs=[pl.BlockSpec((tr, cols), lambda i, k, idx_ref: (i, 0)), blk]),
        out_shape=[jax.ShapeDtypeStruct((h, cols), F32), jax.ShapeDtypeStruct(theirs.shape, BF16)],
        compiler_params=_params(),
    )(jnp.stack([j, c]).astype(jnp.int32), g, theirs)


def _chip_copies(srcs, lands, send_sems, recv_sems):
    x, y, c, j = _place()
    return [pltpu.make_async_remote_copy(
        src_ref=srcs[k].at[j ^ r], dst_ref=lands[k].at[r - 1], send_sem=send_sems.at[3 * k + r - 1],
        recv_sem=recv_sems.at[3 * k + r - 1], device_id=(*_chip_peer(x, y, r), c), device_id_type=_MESH)
        for k in range(len(srcs)) for r in (1, 2, 3)]


def _pair_copies(srcs, lands, send_sems, recv_sems):
    x, y, c, _ = _place()
    return [pltpu.make_async_remote_copy(
        src_ref=srcs[k].at[:, _half(srcs[k], 1 - c)], dst_ref=lands[k], send_sem=send_sems.at[k],
        recv_sem=recv_sems.at[k], device_id=(x, y, 1 - c), device_id_type=_MESH) for k in range(len(srcs))]


def _split_start(name, srcs, lands, copies, n_sems):
    n = len(srcs)

    def body(*refs):
        for cp in copies(refs[:n], refs[n:2 * n], refs[2 * n], refs[2 * n + 1]):
            cp.start()
        refs[-1][...] = jnp.zeros_like(refs[-1])

    arrays = list(srcs) + list(lands)
    out = pl.pallas_call(
        body, name=name,
        out_shape=(pltpu.SemaphoreType.DMA((n_sems,)), pltpu.SemaphoreType.DMA((n_sems,)),
                   *[pltpu.HBM(a.shape, a.dtype) for a in arrays], jax.ShapeDtypeStruct((8, 128), F32)),
        in_specs=[_HBM] * (2 * n), out_specs=(_SEM, _SEM, *([_HBM] * (2 * n)), _VMEM),
        input_output_aliases={k: 2 + k for k in range(2 * n)},
        compiler_params=pltpu.CompilerParams(has_side_effects=_DATAFLOW),
    )(*[pltpu.with_memory_space_constraint(a, pltpu.HBM) for a in arrays])
    return out[0], out[1], list(out[2:2 + n]), list(out[2 + n:2 + 2 * n]), out[-1]


def _split_wait(name, send_sems, recv_sems, srcs_thru, lands_thru, after, copies):
    n = len(srcs_thru)

    def body(*refs):
        for cp in copies(refs[:n], refs[n:2 * n], refs[2 * n], refs[2 * n + 1]):
            cp.wait_send()
            cp.wait_recv()

    arrays = list(srcs_thru) + list(lands_thru)
    out = pl.pallas_call(
        body, name=name, out_shape=tuple(pltpu.HBM(a.shape, a.dtype) for a in arrays),
        in_specs=[_HBM] * (2 * n) + [_SEM, _SEM, _ANY], out_specs=tuple([_HBM] * (2 * n)),
        input_output_aliases={k: k for k in range(2 * n)},
        compiler_params=pltpu.CompilerParams(has_side_effects=_DATAFLOW),
    )(*arrays, send_sems, recv_sems, after)
    return list(out[:n]), list(out[n:])


def _chip_exchange_start(ss, tag):
    lands = [lax.empty((3,) + s.shape[1:], s.dtype) for s in ss]
    return _split_start("chip_exchange_start_" + tag, ss, lands, _chip_copies, 3 * len(ss))


def _chip_exchange_wait(send_sems, recv_sems, ss_thru, lands_thru, after, tag):
    return _split_wait("chip_exchange_wait_" + tag, send_sems, recv_sems, ss_thru, lands_thru, after, _chip_copies)[1]


def _pair_exchange_start(gs, tag):
    lands = [lax.empty((4, g.shape[1] // 2, g.shape[2]), g.dtype) for g in gs]
    return _split_start("pair_exchange_start_" + tag, gs, lands, _pair_copies, len(gs))


def _pair_exchange_wait(send_sems, recv_sems, gs_thru, lands_thru, after, tag):
    return _split_wait("pair_exchange_wait_" + tag, send_sems, recv_sems, gs_thru, lands_thru, after, _pair_copies)


def _reduce4(own, others, c, name):
    h, cols = own.shape
    tr = _rows_tile(h)
    nb = h // tr

    def body(c_ref, s_ref, a0, a1, a2, o_ref):
        o_ref[...] = ((s_ref[...] + a0[0].astype(F32)) + a1[0].astype(F32)) + a2[0].astype(F32)

    def other(r):
        return pl.BlockSpec((1, tr, cols), lambda i, c_ref: (r, i, 0))

    return pl.pallas_call(
        body, name=name,
        grid_spec=pltpu.PrefetchScalarGridSpec(
            num_scalar_prefetch=1, grid=(nb,),
            in_specs=[pl.BlockSpec((tr, cols), lambda i, c_ref: (i, 0)), other(0), other(1), other(2)],
            out_specs=pl.BlockSpec((tr, cols), lambda i, c_ref: (c_ref[0] * nb + i, 0))),
        out_shape=jax.ShapeDtypeStruct((2 * h, cols), F32), compiler_params=_params(),
    )(c.reshape(1).astype(jnp.int32), own, others, others, others)


def _sibling_share(fulls, name):
    n = len(fulls)

    def body(*refs):
        outs, send_sems, recv_sems = refs[n:2 * n], refs[2 * n], refs[2 * n + 1]
        x, y, c, _ = _place()
        cps = [pltpu.make_async_remote_copy(
            src_ref=outs[k].at[_half(outs[k], c)], dst_ref=outs[k].at[_half(outs[k], c)], send_sem=send_sems.at[k],
            recv_sem=recv_sems.at[k], device_id=(x, y, 1 - c), device_id_type=_MESH) for k in range(n)]
        for cp in cps:
            cp.start()
        for cp in cps:
            cp.wait()

    return pl.pallas_call(
        body, name=name, out_shape=[jax.ShapeDtypeStruct(f.shape, F32) for f in fulls],
        in_specs=[_HBM] * n, out_specs=[_HBM] * n, input_output_aliases={k: k for k in range(n)},
        scratch_shapes=[pltpu.SemaphoreType.DMA((n,))] * 2,
    )(*fulls)


def _adamw(w, g, m, v):
    m1 = ADAM_B1 * m + (1.0 - ADAM_B1) * g
    v1 = ADAM_B2 * v + (1.0 - ADAM_B2) * (g * g)
    m_hat = m1 / (1.0 - ADAM_B1 ** ADAM_STEP)
    v_hat = v1 / (1.0 - ADAM_B2 ** ADAM_STEP)
    delta = -ADAM_LR * (m_hat / (jnp.sqrt(v_hat) + ADAM_EPS) + ADAM_WD * w)
    return delta, m1, v1


def _adamw_call(w, g, m, v, name):
    rows, cols = w.shape

    def body(w_ref, g_ref, m_ref, v_ref, d_out, m_out, v_out):
        delta, m1, v1 = _adamw(w_ref[...], g_ref[...], m_ref[...], v_ref[...])
        d_out[...] = delta
        m_out[...] = m1
        v_out[...] = v1

    if rows % 8 == 0:
        tr = _rows_tile(rows)
        blk, grid = pl.BlockSpec((tr, cols), lambda i: (i, 0)), (rows // tr,)
    else:
        blk, grid = pl.BlockSpec((rows, 128), lambda i: (0, i)), (cols // 128,)
    return pl.pallas_call(
        body, name=name, grid=grid, in_specs=[blk] * 4, out_specs=[blk] * 3,
        out_shape=[jax.ShapeDtypeStruct((rows, cols), F32)] * 3, compiler_params=_params(),
    )(w, g, m, v)


def _small_allreduce(vals):
    def body(v_ref, out_ref, buf, send_sems, recv_sems):
        x, y, c, j = _place()
        me = 2 * j + c
        buf[0] = v_ref[...]

        def copy(r):
            return pltpu.make_async_remote_copy(
                src_ref=v_ref, dst_ref=buf.at[r], send_sem=send_sems.at[r - 1], recv_sem=recv_sems.at[r - 1],
                device_id=(x ^ (r >> 2), y ^ ((r >> 1) & 1), c ^ (r & 1)), device_id_type=_MESH)

        for r in range(1, 8):
            copy(r).start()
        for r in range(1, 8):
            copy(r).wait()
        acc = buf[me ^ 0]
        for d in range(1, 8):
            acc = acc + buf[me ^ d]
        out_ref[...] = acc

    return pl.pallas_call(
        body, name="small_allreduce", out_shape=jax.ShapeDtypeStruct((_SMALL_ROWS, D), F32),
        in_specs=[_VMEM], out_specs=_VMEM,
        scratch_shapes=[pltpu.VMEM((8, _SMALL_ROWS, D), F32), pltpu.SemaphoreType.DMA((7,)),
                        pltpu.SemaphoreType.DMA((7,))],
    )(vals)


_NAMES = ("norm_mix_g", "w_in", "conv_qk", "b_if", "mlstm_norm_g", "sinks", "w_branch_a", "w_branch_b", "w_out",
          "norm_mlp_g", "w_up", "w_down", "norm_ple_g", "w_ple_gate", "w_ple_proj", "final_norm_g")
_GROUP_NAMES = ("w_in", "w4", "w_up", "w_down", "w_ple_proj")


def _step(x, p, target, w, m, v):
    c = lax.axis_index("c")
    j = 2 * lax.axis_index("x") + lax.axis_index("y")

    def shards(d):
        return {n: d[n][0] for n in _SHARDED_NAMES}

    ws = shards(w)
    w_in_all, conv_all = _allgather_weights([ws["w_in"].astype(BF16)], ws["conv_qk"])
    rows_pp = PLE * (D // 4) // D
    rest = jnp.concatenate([ws[n] for n in _W4] + [ws["w_up"], ws["w_down"], ws["w_ple_proj"].reshape(rows_pp, D)],
                           axis=0)
    rest = (rest + 0.0 * conv_all[0, 0, 0]).astype(BF16)
    send_sems, recv_sems, rest_thru, land_thru, token = _late_gather_start(rest)
    full = {n: w[n] for n in ("mlstm_norm_g", "norm_mlp_g", "norm_ple_g", "b_if", "sinks")}
    full["norm_mix_g"] = w["norm_mix_g"] + token[0, 0]
    full["final_norm_g"] = w["final_norm_g"].reshape(1, D)
    full["w_in"] = _win_pad(w_in_all)
    full["conv_qk"] = jnp.swapaxes(conv_all, 0, 1).reshape(CONV, D)

    def late_weights(after):
        land = _late_gather_wait(send_sems, recv_sems, rest_thru, land_thru, after)
        out = {n: land[:, i * (D // 4):(i + 1) * (D // 4)].reshape(D, D) for i, n in enumerate(_W4)}
        out["w_up"] = land[:, D:2 * D]
        out["w_down"] = land[:, 2 * D:3 * D].reshape(DFF, D)
        out["w_ple_proj"] = jnp.swapaxes(land[:, 3 * D:3 * D + rows_pp].reshape(4, PLE, D // 4), 0, 1).reshape(PLE, D)
        return out

    early, last = {}, {}

    def pair_sums(by_dest, theirs, names):
        return [_pair_sum(a, b, j, c, "pair_sum_" + n) for a, b, n in zip(by_dest, theirs, names)]

    def early_grads(g):
        by_dest = [jnp.stack([g[n].reshape(4, D // 4, D) for n in _W4], axis=1).reshape(4, D, D),
                   g["w_up"], g["w_down"].reshape(4, DFF // 4, D), g["w_ple_proj"]]
        *early["pair"], token = _pair_exchange_start(by_dest, "early")
        return token[0, 0]

    def mid_grads(after):
        early["sums"] = pair_sums(*_pair_exchange_wait(*early["pair"], after, "early"), _GROUP_NAMES[1:])
        *early["flight"], token = _chip_exchange_start([s[1] for s in early["sums"]], "early")
        return token[0, 0]

    def last_grad(g):
        *last["pair"], token = _pair_exchange_start([_win_unpad(g["w_in"])], "w_in")
        return token[0, 0]

    loss, grad_x, g = _local_step(x[0], p[0, 0], target[0], full, late_weights, early_grads, mid_grads, last_grad)

    last["sums"] = pair_sums(*_pair_exchange_wait(*last["pair"], grad_x, "w_in"), _GROUP_NAMES[:1])
    *last["flight"], token = _chip_exchange_start([s[1] for s in last["sums"]], "w_in")

    def reduce_share(sums, others, names, tag):
        halves = [_reduce4(s[0], b, c, "reduce4_" + n) for s, b, n in zip(sums, others, names)]
        return list(_sibling_share(halves, "sibling_share_" + tag))

    ms, vs = shards(m), shards(v)
    grads = reduce_share(early["sums"], _chip_exchange_wait(*early["flight"], token, "early"), _GROUP_NAMES[1:], "early")
    upd = [_adamw_call(wa, ga, ma, va, "adamw_" + n)
           for wa, ga, ma, va, n in zip(_group(ws)[1:], grads, _group(ms)[1:], _group(vs)[1:], _GROUP_NAMES[1:])]
    small_g = _small_allreduce(_pack_small(g, extra=loss, conv=g["conv_qk"]))
    conv_g = lax.dynamic_slice(small_g[_CONV_ROW:_CONV_ROW + CONV], (0, j * (D // 4)), (CONV, D // 4))
    conv_upd = _adamw_call(ws["conv_qk"], conv_g, ms["conv_qk"], vs["conv_qk"], "adamw_conv")
    small_upd = _adamw_call(_pack_small(w), small_g, _pack_small(m), _pack_small(v), "adamw_small")

    others = _chip_exchange_wait(*last["flight"], small_upd[0], "w_in")
    grads = reduce_share(last["sums"], others, _GROUP_NAMES[:1], "w_in") + list(grads)
    upd_in = _adamw_call(*[jnp.swapaxes(a, 0, 1) for a in (ws["w_in"], grads[0], ms["w_in"], vs["w_in"])], "adamw_w_in")
    upd = [[jnp.swapaxes(a, 0, 1) for a in upd_in]] + upd

    shapes = {n: w[n].shape for n in _NAMES}
    res = []
    for k in range(4):
        big = _ungroup(list(grads) if k == 0 else [u[k - 1] for u in upd])
        big["conv_qk"] = conv_g if k == 0 else conv_upd[k - 1]
        leaves = _unpack_small(small_g if k == 0 else small_upd[k - 1], shapes)
        leaves.update({n: a.reshape(shapes[n]) for n, a in big.items()})
        res.append(leaves)

    out = [small_g[5, 8 + SWH], grad_x[None]]
    for k in range(4):
        out += [res[k][n] for n in _NAMES]
    return tuple(out)


def kernel(x, p, norm_mix_g, w_in, conv_qk, b_if, mlstm_norm_g, sinks, w_branch_a, w_branch_b, w_out, norm_mlp_g, w_up, w_down, norm_ple_g, w_ple_gate, w_ple_proj, final_norm_g, loss_target, m_norm_mix_g, m_w_in, m_conv_qk, m_b_if, m_mlstm_norm_g, m_sinks, m_w_branch_a, m_w_branch_b, m_w_out, m_norm_mlp_g, m_w_up, m_w_down, m_norm_ple_g, m_w_ple_gate, m_w_ple_proj, m_final_norm_g, v_norm_mix_g, v_w_in, v_conv_qk, v_b_if, v_mlstm_norm_g, v_sinks, v_w_branch_a, v_w_branch_b, v_w_out, v_norm_mlp_g, v_w_up, v_w_down, v_norm_ple_g, v_w_ple_gate, v_w_ple_proj, v_final_norm_g):
    w = dict(zip(_NAMES, (norm_mix_g, w_in, conv_qk, b_if, mlstm_norm_g, sinks, w_branch_a, w_branch_b, w_out,
                          norm_mlp_g, w_up, w_down, norm_ple_g, w_ple_gate, w_ple_proj, final_norm_g)))
    m = dict(zip(_NAMES, (m_norm_mix_g, m_w_in, m_conv_qk, m_b_if, m_mlstm_norm_g, m_sinks, m_w_branch_a,
                          m_w_branch_b, m_w_out, m_norm_mlp_g, m_w_up, m_w_down, m_norm_ple_g, m_w_ple_gate,
                          m_w_ple_proj, m_final_norm_g)))
    v = dict(zip(_NAMES, (v_norm_mix_g, v_w_in, v_conv_qk, v_b_if, v_mlstm_norm_g, v_sinks, v_w_branch_a,
                          v_w_branch_b, v_w_out, v_norm_mlp_g, v_w_up, v_w_down, v_norm_ple_g, v_w_ple_gate,
                          v_w_ple_proj, v_final_norm_g)))
    return _step(x, p, loss_target, w, m, v)
```

```python
import jax
import jax.numpy as jnp
from jax import lax
from jax.experimental import pallas as pl
from jax.experimental.pallas import tpu as pltpu

F32 = jnp.float32
BF16 = jnp.bfloat16

D = 1024
PLE = 256
MLH = 4
DQK = 128
DV = 256
CONV = 4
CHUNK = 128
SWH = 16
SWKV = 4
SWG = SWH // SWKV
HD = 64
WIN = 128
DFF = 4096
EPS = 1e-6
N_IN = 6664
NP = 7168
C_QK, C_V, C_O, C_QSW, C_GA, C_GB, C_KV, C_IF = 0, 1024, 2048, 3072, 4096, 5120, 6144, 6656
IFW = NP - C_IF

ADAM_LR = 0.001
ADAM_B1 = 0.9
ADAM_B2 = 0.999
ADAM_EPS = 1e-08
ADAM_WD = 0.01
ADAM_STEP = 10

TOK_TILE = 512
VMEM_LIMIT = 58 * 1024 * 1024


def _params(**kw):
    return pltpu.CompilerParams(vmem_limit_bytes=VMEM_LIMIT, **kw)


def _pick(n, cap):
    if n <= cap:
        return n
    t = cap - cap % 128
    while t > 128 and n % t:
        t -= 128
    assert n % t == 0, (n, cap)
    return t


def _dot(a, b, dims):
    return lax.dot_general(a, b, (dims, ((), ())), preferred_element_type=F32)


def _dot_nn(a, b):
    return _dot(a, b, ((1,), (0,)))


def _dot_nt(a, b):
    return _dot(a, b, ((1,), (1,)))


def _dot_tn(a, b):
    return _dot(a, b, ((0,), (0,)))


def _sigmoid(x):
    return 1.0 / (1.0 + jnp.exp(-x))


def _mm(a, b, mode, out_dtype, name, out_chunks=1):
    bch = b.shape[0] if b.ndim == 3 else 1
    brows, bcols = b.shape[-2], b.shape[-1] * bch
    if mode == "nn":
        (m, k), (k2, n) = a.shape, (brows, bcols)
    elif mode == "nt":
        (m, k), (n, k2) = a.shape, (brows, bcols)
    else:
        (k, m), (k2, n) = a.shape, (brows, bcols)
    assert k == k2, (a.shape, b.shape, mode)
    n_cap = n // max(out_chunks, 1 if mode == "nt" else bch)
    k_cap = k // bch if mode == "nt" else k
    tm, tn, tk = _pick(m, 1024), _pick(n_cap, 1024), _pick(k_cap, 2048)
    nk = k // tk
    if mode == "nn":
        a_spec = pl.BlockSpec((tm, tk), lambda i, j, kk: (i, kk))
        if bch > 1:
            bpc = (n // bch) // tn
            b_spec = pl.BlockSpec((None, tk, tn), lambda i, j, kk: (j // bpc, kk, j % bpc))
        else:
            b_spec = pl.BlockSpec((tk, tn), lambda i, j, kk: (kk, j))
        dot = _dot_nn
    elif mode == "nt":
        a_spec = pl.BlockSpec((tm, tk), lambda i, j, kk: (i, kk))
        if bch > 1:
            bpc = (k // bch) // tk
            b_spec = pl.BlockSpec((None, tn, tk), lambda i, j, kk: (kk // bpc, j, kk % bpc))
        else:
            b_spec = pl.BlockSpec((tn, tk), lambda i, j, kk: (j, kk))
        dot = _dot_nt
    else:
        assert bch == 1
        a_spec = pl.BlockSpec((tk, tm), lambda i, j, kk: (kk, i))
        b_spec = pl.BlockSpec((tk, tn), lambda i, j, kk: (kk, j))
        dot = _dot_tn
    if out_chunks > 1:
        npc = (n // out_chunks) // tn
        out_spec = pl.BlockSpec((None, tm, tn), lambda i, j, kk: (j // npc, i, j % npc))
        out_shape = jax.ShapeDtypeStruct((out_chunks, m, n // out_chunks), out_dtype)
    else:
        out_spec = pl.BlockSpec((tm, tn), lambda i, j, kk: (i, j))
        out_shape = jax.ShapeDtypeStruct((m, n), out_dtype)

    def body(a_ref, b_ref, o_ref, acc_ref):
        kk = pl.program_id(2)

        @pl.when(kk == 0)
        def _():
            acc_ref[...] = jnp.zeros_like(acc_ref)

        acc_ref[...] += dot(a_ref[...], b_ref[...])

        @pl.when(kk == nk - 1)
        def _():
            o_ref[...] = acc_ref[...].astype(out_dtype)

    return pl.pallas_call(
        body, name=name, grid=(m // tm, n // tn, nk),
        in_specs=[a_spec, b_spec], out_specs=out_spec, out_shape=out_shape,
        scratch_shapes=[pltpu.VMEM((tm, tn), F32)],
        compiler_params=_params(dimension_semantics=("parallel", "parallel", "arbitrary")),
    )(a, b)


def _tile(col0=0):
    return lambda tm, tn: pl.BlockSpec((tm, tn), lambda i, j, kk: (i, col0 // tn + j))


def _row():
    return lambda tm, tn: pl.BlockSpec((1, tn), lambda i, j, kk: (0, j))


def _mm_ep(pairs, mode, name, epilogue, ins, outs, tm, tn, aliases=None):
    a0, b0 = pairs[0]
    bch = b0.shape[0] if b0.ndim == 3 else 1
    m, k = a0.shape
    tm = _pick(m, tm)
    n = b0.shape[-1] * bch if mode == "nn" else b0.shape[-2]
    tk = _pick(k // bch if mode == "nt" else k, 2048)
    nk = k // tk
    a_spec = pl.BlockSpec((tm, tk), lambda i, j, kk: (i, kk))
    if mode == "nn":
        dot = _dot_nn
        if bch > 1:
            bpc = (n // bch) // tn
            b_spec = pl.BlockSpec((None, tk, tn), lambda i, j, kk: (j // bpc, kk, j % bpc))
        else:
            b_spec = pl.BlockSpec((tk, tn), lambda i, j, kk: (kk, j))
    else:
        dot = _dot_nt
        if bch > 1:
            bpc = (k // bch) // tk
            b_spec = pl.BlockSpec((None, tn, tk), lambda i, j, kk: (kk // bpc, j, kk % bpc))
        else:
            b_spec = pl.BlockSpec((tn, tk), lambda i, j, kk: (j, kk))
    npair, nin, nout = len(pairs), len(ins), len(outs)

    def body(*refs):
        ab = refs[:2 * npair]
        in_refs = refs[2 * npair:2 * npair + nin]
        out_refs = refs[2 * npair + nin:2 * npair + nin + nout]
        accs = refs[2 * npair + nin + nout:]
        i, j, kk = pl.program_id(0), pl.program_id(1), pl.program_id(2)
        for p in range(npair):
            prod = dot(ab[2 * p][...], ab[2 * p + 1][...])

            @pl.when(kk == 0)
            def _():
                accs[p][...] = prod

            @pl.when(kk > 0)
            def _():
                accs[p][...] += prod

        @pl.when(kk == nk - 1)
        def _():
            epilogue([acc[...] for acc in accs], in_refs, out_refs, i, j)

    operands = [x for pair in pairs for x in pair] + [a for a, _ in ins]
    io_alias = {2 * npair + i: o for i, o in (aliases or {}).items()}
    return pl.pallas_call(
        body, name=name, grid=(m // tm, n // tn, nk),
        in_specs=[a_spec, b_spec] * npair + [mk(tm, tn) for _, mk in ins],
        out_specs=[mk(tm, tn) for _, mk in outs], out_shape=[s for s, _ in outs],
        scratch_shapes=[pltpu.VMEM((tm, tn), F32)] * npair, input_output_aliases=io_alias,
        compiler_params=_params(dimension_semantics=("arbitrary", "arbitrary", "arbitrary")),
    )(*operands)


def _tok(w, j=0):
    return pl.BlockSpec((TOK_TILE, w), lambda i: (i, j))


def _rep(shape):
    return pl.BlockSpec(shape, lambda i: (0,) * len(shape))


def _rms(x):
    rstd = lax.rsqrt(jnp.mean(x * x, axis=-1, keepdims=True) + EPS)
    return x * rstd, rstd


def _rms_bwd(xn, rstd, dxn):
    return rstd * (dxn - xn * jnp.mean(dxn * xn, axis=-1, keepdims=True))


def _halo_prev(w, j=0, rows=8):
    r = TOK_TILE // rows
    return pl.BlockSpec((rows, w), lambda i: (jnp.maximum(i * r - 1, 0), j))


def _last8(halo_ref):
    return halo_ref[...].astype(F32)[halo_ref.shape[0] - 8:]


def _halo_next(w, nt, j=0):
    r = TOK_TILE // 8
    return pl.BlockSpec((8, w), lambda i: (jnp.minimum((i + 1) * r, nt * r - 1), j))


def _shift_down(x, halo, s):
    if s == 0:
        return x
    r = pltpu.roll(x, s, 0)
    hs = pltpu.roll(halo, s, 0)
    row = lax.broadcasted_iota(jnp.int32, hs.shape, 0)
    top = jnp.where(row < s, hs, r[0:8])
    return jnp.concatenate([top, r[8:]], axis=0)


def _shift_up(x, halo, s):
    if s == 0:
        return x
    n = x.shape[0]
    r = pltpu.roll(x, n - s, 0)
    hs = pltpu.roll(halo, 8 - s, 0)
    row = lax.broadcasted_iota(jnp.int32, hs.shape, 0)
    bot = jnp.where(row >= 8 - s, hs, r[n - 8:])
    return jnp.concatenate([r[:n - 8], bot], axis=0)


def _bf(x):
    return x.astype(BF16).astype(F32)


def _conv_taps(x, halo, w):
    x, halo, w = _bf(x), _bf(halo), _bf(w)
    acc = x * w[CONV - 1:CONV, :]
    for j in range(CONV - 1):
        acc = acc + _shift_down(x, halo, CONV - 1 - j) * w[j:j + 1, :]
    return acc


_Q_SCALE = DQK ** -0.5


def _qscale_row():
    lane = lax.broadcasted_iota(jnp.int32, (1, D), 1)
    return jnp.where(lane < MLH * DQK, _Q_SCALE, 1.0).astype(F32)


def _conv_silu_fwd(proj, conv_w):
    t = proj.shape[0]

    def body(x_ref, halo_ref, w_ref, o_ref):
        halo = jnp.where(pl.program_id(0) > 0, _last8(halo_ref), 0.0)
        c = _conv_taps(x_ref[...].astype(F32), halo, w_ref[...])
        o_ref[...] = (c * _sigmoid(c) * _qscale_row()).astype(BF16)

    return pl.pallas_call(
        body, name="conv_silu_fwd", grid=(t // TOK_TILE,),
        in_specs=[_tok(D, C_QK // D), _halo_prev(D, C_QK // D, 16), _rep((CONV, D))], out_specs=_tok(D),
        out_shape=jax.ShapeDtypeStruct((t, D), BF16), compiler_params=_params(),
    )(proj, proj, conv_w)


def _conv_silu_bwd_a(proj, conv_w, dqk):
    t = proj.shape[0]

    def body(x_ref, halo_ref, w_ref, d_ref, dc_ref, dw_ref):
        @pl.when(pl.program_id(0) == 0)
        def _():
            dw_ref[...] = jnp.zeros_like(dw_ref)

        halo = jnp.where(pl.program_id(0) > 0, _last8(halo_ref), 0.0)
        x = x_ref[...].astype(F32)
        c = _conv_taps(x, halo, w_ref[...])
        s = _sigmoid(c)
        dc = d_ref[...] * _qscale_row() * (s * (1.0 + c * (1.0 - s)))
        dc_ref[...] = dc
        dcb, xb, halo_b = _bf(dc), _bf(x), _bf(halo)
        for j in range(CONV):
            dw_ref[j:j + 1, :] += jnp.sum(dcb * _shift_down(xb, halo_b, CONV - 1 - j), axis=0, keepdims=True)

    return pl.pallas_call(
        body, name="conv_silu_bwd_a", grid=(t // TOK_TILE,),
        in_specs=[_tok(D, C_QK // D), _halo_prev(D, C_QK // D, 16), _rep((CONV, D)), _tok(D)],
        out_specs=[_tok(D), _rep((CONV, D))],
        out_shape=[jax.ShapeDtypeStruct((t, D), F32), jax.ShapeDtypeStruct((CONV, D), F32)],
        compiler_params=_params(),
    )(proj, proj, conv_w, dqk)


def _conv_silu_bwd_b(dc, conv_w, dproj):
    t = dc.shape[0]
    nt = t // TOK_TILE

    def body(dc_ref, halo_ref, w_ref, _, dx_ref):
        halo = _bf(jnp.where(pl.program_id(0) < nt - 1, halo_ref[...], 0.0))
        dcv = _bf(dc_ref[...])
        w = _bf(w_ref[...])
        acc = dcv * w[CONV - 1:CONV, :]
        for j in range(CONV - 1):
            acc = acc + _shift_up(dcv, halo, CONV - 1 - j) * w[j:j + 1, :]
        dx_ref[...] = acc.astype(BF16)

    return pl.pallas_call(
        body, name="conv_silu_bwd_b", grid=(nt,), in_specs=[_tok(D), _halo_next(D, nt), _rep((CONV, D)), _ANY],
        out_specs=_tok(D, C_QK // D), out_shape=jax.ShapeDtypeStruct((t, NP), BF16),
        input_output_aliases={3: 0}, compiler_params=_params(),
    )(dc, dc, conv_w, dproj)


def _gates_fwd(pre_rows, bias_col):
    t = pre_rows.shape[1]

    def body(p_ref, b_ref, g_ref, s_ref):
        z = p_ref[...] + b_ref[...]
        lf = jnp.minimum(z, 0.0) - jnp.log(1.0 + jnp.exp(-jnp.abs(z)))
        lane = lax.broadcasted_iota(jnp.int32, z.shape, 1) % CHUNK
        cum = lf
        s = 1
        while s < CHUNK:
            cum = cum + jnp.where(lane >= s, pltpu.roll(cum, s, 1), 0.0)
            s *= 2
        sub = lax.broadcasted_iota(jnp.int32, z.shape, 0)
        g_ref[...] = jnp.where(sub < MLH, z, cum)
        s_ref[...] = _sigmoid(-z)

    return pl.pallas_call(
        body, name="gates_fwd",
        out_shape=[jax.ShapeDtypeStruct((8, t), F32), jax.ShapeDtypeStruct((8, t), F32)],
        compiler_params=_params(),
    )(pre_rows, bias_col)


def _chunk_terms(grow, gcol, m0):
    heads = range(MLH)
    i_row = [grow[h:h + 1, :] for h in heads]
    b_row = [grow[MLH + h:MLH + h + 1, :] for h in heads]
    i_col = [gcol[:, h:h + 1] for h in heads]
    b_col = [gcol[:, MLH + h:MLH + h + 1] for h in heads]
    b_last = [b_row[h][:, CHUNK - 1:CHUNK] for h in heads]
    tt = lax.broadcasted_iota(jnp.int32, (CHUNK, CHUNK), 0)
    ss = lax.broadcasted_iota(jnp.int32, (CHUNK, CHUNK), 1)
    log_d = [jnp.where(tt >= ss, b_col[h] - b_row[h] + i_row[h], -jnp.inf) for h in heads]
    row_max = [jnp.max(log_d[h], axis=1, keepdims=True) for h in heads]
    last_max = [jnp.max(b_last[h] - b_row[h] + i_row[h], axis=1, keepdims=True) for h in heads]
    m_t = [jnp.maximum(b_col[h] + m0[h], row_max[h]) for h in heads]
    m1 = [jnp.maximum(b_last[h] + m0[h], last_max[h]) for h in heads]
    dm = [jnp.exp(log_d[h] - m_t[h]) for h in heads]
    wi = [jnp.exp(b_col[h] + m0[h] - m_t[h]) for h in heads]
    ws = [jnp.exp(b_last[h] - b_col[h] + i_col[h] - m1[h]) for h in heads]
    dec = [jnp.exp(b_last[h] + m0[h] - m1[h]) for h in heads]
    return [(dm[h], wi[h], m_t[h], ws[h], dec[h], m1[h]) for h in heads]


def _mlstm_fwd(qk, proj, grow, gcol, gain):
    t = qk.shape[0]
    nc = t // CHUNK

    def body(qk_ref, v_ref, o_ref, grow_ref, gcol_ref, g_ref, h_ref, y_ref, cs_ref, st_ref, c_scr, st_scr):
        @pl.when(pl.program_id(0) == 0)
        def _():
            c_scr[...] = jnp.zeros_like(c_scr)
            st_scr[...] = jnp.zeros_like(st_scr)

        grow_v, gcol_v = grow_ref[...], gcol_ref[...]
        heads = range(MLH)
        q = [qk_ref[:, h * DQK:(h + 1) * DQK] for h in heads]
        k = [qk_ref[:, MLH * DQK + h * DQK:MLH * DQK + (h + 1) * DQK] for h in heads]
        v = [v_ref[:, h * DV:(h + 1) * DV] for h in heads]
        c0 = [c_scr[h] for h in heads]
        n0 = [st_scr[h, 0:1, :] for h in heads]
        for h in heads:
            cs_ref[0, h] = c0[h]
            st_ref[0, h] = st_scr[h]
        terms = _chunk_terms(grow_v, gcol_v, [st_scr[h, 1:2, 0:1] for h in heads])
        a = [_dot_nt(q[h], k[h]) for h in heads]
        qc = [_dot_nt(q[h], c0[h].astype(BF16)) for h in heads]
        s = [a[h] * terms[h][0] for h in heads]
        sv = [_dot_nn(s[h].astype(BF16), v[h]) for h in heads]
        upd = [_dot_tn((terms[h][3] * v[h]).astype(BF16), k[h]) for h in heads]
        den = [terms[h][1] * jnp.sum(q[h].astype(F32) * n0[h], axis=1, keepdims=True)
               + jnp.sum(s[h], axis=1, keepdims=True) for h in heads]
        hv = [(terms[h][1] * qc[h] + sv[h]) / jnp.maximum(jnp.abs(den[h]), jnp.exp(-terms[h][2])) for h in heads]
        for h in heads:
            sl = slice(h * DV, (h + 1) * DV)
            h_ref[:, sl] = hv[h]
            xn, _ = _rms(hv[h])
            y_ref[:, sl] = (_sigmoid(o_ref[:, sl].astype(F32)) * xn * g_ref[:, sl]).astype(BF16)
        for h in heads:
            dec, m1 = terms[h][4], terms[h][5]
            c_scr[h] = dec * c0[h] + upd[h]
            st_scr[h, 0:1, :] = dec * n0[h] + jnp.sum(terms[h][3] * k[h].astype(F32), axis=0, keepdims=True)
            st_scr[h, 1:2, :] = jnp.broadcast_to(m1, (1, DQK))

    return pl.pallas_call(
        body, name="mlstm_fwd", grid=(nc,),
        in_specs=[pl.BlockSpec((CHUNK, D), lambda c: (c, 0)), pl.BlockSpec((CHUNK, D), lambda c: (c, C_V // D)),
                  pl.BlockSpec((CHUNK, D), lambda c: (c, C_O // D)),
                  pl.BlockSpec((8, CHUNK), lambda c: (0, c)), pl.BlockSpec((CHUNK, 8), lambda c: (c, 0)),
                  pl.BlockSpec((1, D), lambda c: (0, 0))],
        out_specs=[pl.BlockSpec((CHUNK, D), lambda c: (c, 0)), pl.BlockSpec((CHUNK, D), lambda c: (c, 0)),
                   pl.BlockSpec((1, MLH, DV, DQK), lambda c: (c, 0, 0, 0)),
                   pl.BlockSpec((1, MLH, 8, DQK), lambda c: (c, 0, 0, 0))],
        out_shape=[jax.ShapeDtypeStruct((t, D), F32), jax.ShapeDtypeStruct((t, D), BF16),
                   jax.ShapeDtypeStruct((nc, MLH, DV, DQK), F32), jax.ShapeDtypeStruct((nc, MLH, 8, DQK), F32)],
        scratch_shapes=[pltpu.VMEM((MLH, DV, DQK), F32), pltpu.VMEM((MLH, 8, DQK), F32)],
        compiler_params=_params(dimension_semantics=("arbitrary",)),
    )(qk, proj, proj, grow, gcol, gain)


def _mlstm_bwd(qk, proj, grow, gcol, sneg_col, cs, st, hraw, dh, dproj):
    t = qk.shape[0]
    nc = t // CHUNK

    def rev(c):
        return nc - 1 - c

    def nxt(c):
        return jnp.minimum(nc - c, nc - 1)

    def body(qk_ref, v_ref, grow_ref, gcol_ref, sneg_ref, cs_ref, st_ref, cs1_ref, st1_ref, h_ref, dh_ref, _,
             dqk_ref, dv_ref, dif_ref, dbif_ref, dc_scr, dn_scr):
        @pl.when(pl.program_id(0) == 0)
        def _():
            dc_scr[...] = jnp.zeros_like(dc_scr)
            dn_scr[...] = jnp.zeros_like(dn_scr)
            dbif_ref[...] = jnp.zeros_like(dbif_ref)

        grow_v, gcol_v, sneg = grow_ref[...], gcol_ref[...], sneg_ref[...]
        tt = lax.broadcasted_iota(jnp.int32, (CHUNK, CHUNK), 0)
        ss = lax.broadcasted_iota(jnp.int32, (CHUNK, CHUNK), 1)
        lane8 = lax.broadcasted_iota(jnp.int32, (CHUNK, 8), 1)
        heads = range(MLH)
        q = [qk_ref[:, h * DQK:(h + 1) * DQK] for h in heads]
        k = [qk_ref[:, MLH * DQK + h * DQK:MLH * DQK + (h + 1) * DQK] for h in heads]
        qf, kf = [a.astype(F32) for a in q], [a.astype(F32) for a in k]
        vb = [v_ref[:, h * DV:(h + 1) * DV].astype(BF16) for h in heads]
        c0 = [cs_ref[0, h] for h in heads]
        n0 = [st_ref[0, h, 0:1, :] for h in heads]
        dc1 = [dc_scr[h] for h in heads]
        dn1 = [dn_scr[h, 0:1, :] for h in heads]
        terms = _chunk_terms(grow_v, gcol_v, [st_ref[0, h, 1:2, 0:1] for h in heads])
        dm, wi, ws = [t[0] for t in terms], [t[1] for t in terms], [t[3] for t in terms]
        s = [_dot_nt(q[h], k[h]) * dm[h] for h in heads]
        den = [wi[h] * jnp.sum(qf[h] * n0[h], axis=1, keepdims=True) + jnp.sum(s[h], axis=1, keepdims=True)
               for h in heads]
        floor = [jnp.exp(-terms[h][2]) for h in heads]
        g = [jnp.maximum(jnp.abs(den[h]), floor[h]) for h in heads]
        dh_v = [dh_ref[:, h * DV:(h + 1) * DV] for h in heads]
        dnum = [dh_v[h] / g[h] for h in heads]
        dden = [-jnp.sum(dh_v[h] * h_ref[:, h * DV:(h + 1) * DV], axis=1, keepdims=True) / g[h] for h in heads]
        dden = [jnp.where(jnp.abs(den[h]) > floor[h], dden[h] * jnp.sign(den[h]), 0.0) for h in heads]
        dnum_b = [a.astype(BF16) for a in dnum]
        dc1_b = [a.astype(BF16) for a in dc1]
        da = [((_dot_nt(dnum_b[h], vb[h]) + dden[h]) * dm[h]).astype(BF16) for h in heads]
        dq_inter = [_dot_nn(dnum_b[h], c0[h].astype(BF16)) for h in heads]
        dk_inter = [_dot_nn(vb[h], dc1_b[h]) for h in heads]
        dv_inter = [_dot_nt(k[h], dc1_b[h]) for h in heads]
        dc_new = [_dot_tn((wi[h] * dnum[h]).astype(BF16), q[h]) for h in heads]
        dq = [_dot_nn(da[h], k[h]) + wi[h] * (dq_inter[h] + dden[h] * n0[h]) for h in heads]
        dk = [_dot_tn(da[h], q[h]) + ws[h] * (dk_inter[h] + dn1[h]) for h in heads]
        dv = [_dot_tn(s[h].astype(BF16), dnum_b[h]) + ws[h] * dv_inter[h] for h in heads]
        for h in heads:
            dqk_ref[:, h * DQK:(h + 1) * DQK] = dq[h]
            dqk_ref[:, MLH * DQK + h * DQK:MLH * DQK + (h + 1) * DQK] = dk[h]
            dv_ref[:, h * DV:(h + 1) * DV] = dv[h].astype(BF16)
        rk = [jnp.sum(kf[h] * dk[h], axis=1, keepdims=True) for h in heads]
        df = [jnp.sum(qf[h] * dq[h], axis=1, keepdims=True) - rk[h] for h in heads]
        df_row = [jnp.sum(jnp.where(tt == ss, df[h], 0.0), axis=0, keepdims=True) for h in heads]
        suffix = [jnp.sum(jnp.where(ss >= tt, df_row[h], 0.0), axis=1, keepdims=True) for h in heads]
        cross = [jnp.sum(jnp.sum(dc1[h] * cs1_ref[0, h], axis=0, keepdims=True), axis=1, keepdims=True)
                 + jnp.sum(dn1[h] * st1_ref[0, h, 0:1, :], axis=1, keepdims=True) for h in heads]
        dif = jnp.zeros((CHUNK, 8), F32)
        for h in heads:
            dpf = (suffix[h] + cross[h]) * sneg[:, MLH + h:MLH + h + 1]
            dif = dif + jnp.where(lane8 == h, rk[h], 0.0) + jnp.where(lane8 == MLH + h, dpf, 0.0)
            dc_scr[h] = terms[h][4] * dc1[h] + dc_new[h]
            dn_scr[h, 0:1, :] = terms[h][4] * dn1[h] + jnp.sum(wi[h] * dden[h] * qf[h], axis=0, keepdims=True)
        dif_ref[...] = dif
        dbif_ref[...] += jnp.sum(dif, axis=0, keepdims=True)

    return pl.pallas_call(
        body, name="mlstm_bwd", grid=(nc,),
        in_specs=[pl.BlockSpec((CHUNK, D), lambda c: (rev(c), 0)),
                  pl.BlockSpec((CHUNK, D), lambda c: (rev(c), C_V // D)),
                  pl.BlockSpec((8, CHUNK), lambda c: (0, rev(c))),
                  pl.BlockSpec((CHUNK, 8), lambda c: (rev(c), 0)),
                  pl.BlockSpec((CHUNK, 8), lambda c: (rev(c), 0)),
                  pl.BlockSpec((1, MLH, DV, DQK), lambda c: (rev(c), 0, 0, 0)),
                  pl.BlockSpec((1, MLH, 8, DQK), lambda c: (rev(c), 0, 0, 0)),
                  pl.BlockSpec((1, MLH, DV, DQK), lambda c: (nxt(c), 0, 0, 0)),
                  pl.BlockSpec((1, MLH, 8, DQK), lambda c: (nxt(c), 0, 0, 0)),
                  pl.BlockSpec((CHUNK, D), lambda c: (rev(c), 0)),
                  pl.BlockSpec((CHUNK, D), lambda c: (rev(c), 0)), _ANY],
        out_specs=[pl.BlockSpec((CHUNK, D), lambda c: (rev(c), 0)),
                   pl.BlockSpec((CHUNK, D), lambda c: (rev(c), C_V // D)),
                   pl.BlockSpec((CHUNK, 8), lambda c: (rev(c), 0)),
                   pl.BlockSpec((1, 8), lambda c: (0, 0))],
        out_shape=[jax.ShapeDtypeStruct((t, D), F32), jax.ShapeDtypeStruct((t, NP), BF16),
                   jax.ShapeDtypeStruct((t, 8), F32), jax.ShapeDtypeStruct((1, 8), F32)],
        scratch_shapes=[pltpu.VMEM((MLH, DV, DQK), F32), pltpu.VMEM((MLH, 8, DQK), F32)],
        input_output_aliases={11: 1}, compiler_params=_params(dimension_semantics=("arbitrary",)),
    )(qk, proj, grow, gcol, sneg_col, cs, st, cs, st, hraw, dh, dproj)


_ANY = pl.BlockSpec(memory_space=pl.ANY)


_SW_SCALE = HD ** -0.5
_KVB = C_KV // (2 * SWKV * HD)


def _swa_mask(n):
    ki = lax.broadcasted_iota(jnp.int32, (2 * WIN, SWG * WIN), 0)
    qi = lax.broadcasted_iota(jnp.int32, (2 * WIN, SWG * WIN), 1) % WIN
    return (ki > qi) & (ki <= qi + WIN) & ((n > 0) | (ki >= WIN))


def _group_rows(x_ref, hk):
    return jnp.concatenate([x_ref[:, (hk * SWG + g) * HD:(hk * SWG + g + 1) * HD] for g in range(SWG)], axis=0)


def _group_lanes(x_ref, hk):
    return jnp.concatenate([x_ref[hk * SWG + g:hk * SWG + g + 1, :] for g in range(SWG)], axis=1)


def _sink_lanes(sink_ref, hk):
    return jnp.concatenate([jnp.broadcast_to(sink_ref[:, hk * SWG + g:hk * SWG + g + 1], (1, WIN))
                            for g in range(SWG)], axis=1)


def _swa_fwd(proj, sinks):
    t = proj.shape[0]
    nb = t // WIN

    def body(q_ref, kvc_ref, kvp_ref, sink_ref, y_ref, lse_ref):
        valid = _swa_mask(pl.program_id(0))
        for hk in range(SWKV):
            ks = slice(hk * HD, (hk + 1) * HD)
            vs = slice(SWKV * HD + hk * HD, SWKV * HD + (hk + 1) * HD)
            kb = jnp.concatenate([kvp_ref[:, ks], kvc_ref[:, ks]], axis=0).astype(BF16)
            vb = jnp.concatenate([kvp_ref[:, vs], kvc_ref[:, vs]], axis=0).astype(BF16)
            q4 = _group_rows(q_ref, hk).astype(BF16)
            sink = _sink_lanes(sink_ref, hk)
            logits = jnp.where(valid, _dot_nt(kb, q4) * _SW_SCALE, -jnp.inf)
            m = jnp.maximum(jnp.max(logits, axis=0, keepdims=True), sink)
            p = jnp.exp(logits - m)
            denom = jnp.sum(p, axis=0, keepdims=True) + jnp.exp(sink - m)
            y4 = _dot_tn((p / denom).astype(BF16), vb).astype(BF16)
            lse4 = m + jnp.log(denom)
            for g in range(SWG):
                hq = hk * SWG + g
                y_ref[:, hq * HD:(hq + 1) * HD] = y4[g * WIN:(g + 1) * WIN]
                lse_ref[hq:hq + 1, :] = lse4[:, g * WIN:(g + 1) * WIN]

    return pl.pallas_call(
        body, name="swa_fwd", grid=(nb,),
        in_specs=[pl.BlockSpec((WIN, D), lambda n: (n, C_QSW // D)),
                  pl.BlockSpec((WIN, 512), lambda n: (n, _KVB)),
                  pl.BlockSpec((WIN, 512), lambda n: (jnp.maximum(n - 1, 0), _KVB)),
                  pl.BlockSpec((1, SWH), lambda n: (0, 0))],
        out_specs=[pl.BlockSpec((WIN, D), lambda n: (n, 0)), pl.BlockSpec((SWH, WIN), lambda n: (0, n))],
        out_shape=[jax.ShapeDtypeStruct((t, D), BF16), jax.ShapeDtypeStruct((SWH, t), F32)],
        compiler_params=_params(),
    )(proj, proj, proj, sinks)


def _swa_bwd(proj, sinks, lse, dyb, dproj):
    t = proj.shape[0]
    nb = t // WIN

    def body(q_ref, kvc_ref, kvp_ref, sink_ref, lse_ref, dy_ref, _, dq_ref, dself_ref, dprev_ref, ds_ref):
        @pl.when(pl.program_id(0) == 0)
        def _():
            ds_ref[...] = jnp.zeros_like(ds_ref)

        valid = _swa_mask(pl.program_id(0))
        kvh = range(SWKV)
        ks = [slice(hk * HD, (hk + 1) * HD) for hk in kvh]
        vs = [slice(SWKV * HD + hk * HD, SWKV * HD + (hk + 1) * HD) for hk in kvh]
        kb = [jnp.concatenate([kvp_ref[:, ks[hk]], kvc_ref[:, ks[hk]]], axis=0).astype(BF16) for hk in kvh]
        vb = [jnp.concatenate([kvp_ref[:, vs[hk]], kvc_ref[:, vs[hk]]], axis=0).astype(BF16) for hk in kvh]
        qb = [_group_rows(q_ref, hk).astype(BF16) for hk in kvh]
        dyb_ = [_group_rows(dy_ref, hk).astype(BF16) for hk in kvh]
        lse4 = [_group_lanes(lse_ref, hk) for hk in kvh]
        logits = [_dot_nt(kb[hk], qb[hk]) for hk in kvh]
        dpt = [_dot_nt(vb[hk], dyb_[hk]) for hk in kvh]
        p = [jnp.exp(jnp.where(valid, logits[hk] * _SW_SCALE, -jnp.inf) - lse4[hk]) for hk in kvh]
        delta = [jnp.sum(p[hk] * dpt[hk], axis=0, keepdims=True) for hk in kvh]
        dsm = [(p[hk] * (dpt[hk] - delta[hk])).astype(BF16) for hk in kvh]
        dq4 = [(_dot_tn(dsm[hk], kb[hk]) * _SW_SCALE).astype(BF16) for hk in kvh]
        dkb = [_dot_nn(dsm[hk], qb[hk]) * _SW_SCALE for hk in kvh]
        dvb = [_dot_nn(p[hk].astype(BF16), dyb_[hk]) for hk in kvh]
        for hk in kvh:
            dsink4 = jnp.exp(_sink_lanes(sink_ref, hk) - lse4[hk]) * delta[hk]
            for g in range(SWG):
                hq = hk * SWG + g
                dq_ref[:, hq * HD:(hq + 1) * HD] = dq4[hk][g * WIN:(g + 1) * WIN]
                ds_ref[:, hq:hq + 1] += -jnp.sum(dsink4[:, g * WIN:(g + 1) * WIN], axis=1, keepdims=True)
            dprev_ref[:, ks[hk]] = dkb[hk][:WIN]
            dself_ref[:, ks[hk]] = dkb[hk][WIN:]
            dprev_ref[:, vs[hk]] = dvb[hk][:WIN]
            dself_ref[:, vs[hk]] = dvb[hk][WIN:]

    return pl.pallas_call(
        body, name="swa_bwd", grid=(nb,),
        in_specs=[pl.BlockSpec((WIN, D), lambda n: (n, C_QSW // D)),
                  pl.BlockSpec((WIN, 512), lambda n: (n, _KVB)),
                  pl.BlockSpec((WIN, 512), lambda n: (jnp.maximum(n - 1, 0), _KVB)),
                  pl.BlockSpec((1, SWH), lambda n: (0, 0)),
                  pl.BlockSpec((SWH, WIN), lambda n: (0, n)),
                  pl.BlockSpec((WIN, D), lambda n: (n, 0)), _ANY],
        out_specs=[pl.BlockSpec((WIN, D), lambda n: (n, C_QSW // D)), pl.BlockSpec((WIN, 512), lambda n: (n, 0)),
                   pl.BlockSpec((WIN, 512), lambda n: (jnp.maximum(n - 1, 0), 0)),
                   pl.BlockSpec((1, SWH), lambda n: (0, 0))],
        out_shape=[jax.ShapeDtypeStruct((t, NP), BF16), jax.ShapeDtypeStruct((t, 512), F32),
                   jax.ShapeDtypeStruct((t, 512), F32), jax.ShapeDtypeStruct((1, SWH), F32)],
        input_output_aliases={6: 0}, compiler_params=_params(),
    )(proj, proj, proj, sinks, lse, dyb, dproj)


def _kv_combine(dself, dnext, dif, dproj):
    t = dself.shape[0]
    rows = _pick(t, 512)

    def body(a_ref, b_ref, dif_ref, _, o_ref):
        row = pl.program_id(0) * rows + lax.broadcasted_iota(jnp.int32, (rows, 1), 0)
        o_ref[:, 0:512] = (a_ref[...] + jnp.where(row < t - WIN, b_ref[...], 0.0)).astype(BF16)
        lane = lax.broadcasted_iota(jnp.int32, (rows, 128), 1)
        dif_v = dif_ref[...]
        first = jnp.zeros((rows, 128), F32)
        for col in range(8):
            first = first + jnp.where(lane == col, dif_v[:, col:col + 1], 0.0)
        o_ref[:, 512:640] = first.astype(BF16)
        o_ref[:, 640:512 + IFW] = jnp.zeros((rows, IFW - 128), BF16)

    return pl.pallas_call(
        body, name="kv_combine", grid=(t // rows,),
        in_specs=[pl.BlockSpec((rows, 512), lambda n: (n, 0)), pl.BlockSpec((rows, 512), lambda n: (n, 0)),
                  pl.BlockSpec((rows, 8), lambda n: (n, 0)), _ANY],
        out_specs=pl.BlockSpec((rows, 512 + IFW), lambda n: (n, C_KV // (512 + IFW))),
        out_shape=jax.ShapeDtypeStruct((t, NP), BF16), input_output_aliases={3: 0}, compiler_params=_params(),
    )(dself, dnext, dif, dproj)


def _sds(t, n, dtype):
    return jax.ShapeDtypeStruct((t, n), dtype)


def _proj_in(x, gain, w_in):
    t = x.shape[0]
    tm, tn = _pick(t, 1024), 2 * IFW

    def body(x_ref, g_ref, w_ref, h_ref, p_ref, gate_ref, h_scr):
        j = pl.program_id(1)

        @pl.when(j == 0)
        def _():
            xn, _ = _rms(x_ref[...])
            h = (xn * g_ref[...]).astype(BF16)
            h_scr[...] = h
            h_ref[...] = h

        acc = _dot_nn(h_scr[...], w_ref[...])
        p_ref[...] = acc.astype(BF16)

        @pl.when(j == C_IF // tn)
        def _():
            gate_ref[...] = acc[:, C_IF % tn:C_IF % tn + 128]

    return pl.pallas_call(
        body, name="mm_in", grid=(t // tm, NP // tn),
        in_specs=[pl.BlockSpec((tm, D), lambda i, j: (i, 0)), pl.BlockSpec((1, D), lambda i, j: (0, 0)),
                  pl.BlockSpec((D, tn), lambda i, j: (0, j))],
        out_specs=[pl.BlockSpec((tm, D), lambda i, j: (i, 0)), pl.BlockSpec((tm, tn), lambda i, j: (i, j)),
                   pl.BlockSpec((tm, 128), lambda i, j: (i, 0))],
        out_shape=[_sds(t, D, BF16), _sds(t, NP, BF16), _sds(t, 128, F32)],
        scratch_shapes=[pltpu.VMEM((tm, D), BF16)],
        compiler_params=_params(dimension_semantics=("arbitrary", "arbitrary")),
    )(x, gain, w_in)


def _branch_merge(ya, yb, wa, wb, proj):
    t = ya.shape[0]

    def epilogue(accs, ins, outs, i, j):
        za, zb = accs
        merged = _sigmoid(ins[0][...].astype(F32)) * za + _sigmoid(ins[1][...].astype(F32)) * zb
        outs[0][...] = merged.astype(BF16)
        outs[1][...] = za.astype(BF16)
        outs[2][...] = zb.astype(BF16)

    return _mm_ep([(ya, wa), (yb, wb)], "nn", "mm_branch_merge", epilogue, [(proj, _tile(C_GA)), (proj, _tile(C_GB))],
                  [(_sds(t, D, BF16), _tile())] * 3, 1024, 1024)


def _dmerged_bwd(dxb, w_out, proj, za, zb):
    t = dxb.shape[0]

    def epilogue(accs, ins, outs, i, j):
        dm = accs[0]
        sa, sb = _sigmoid(ins[0][...].astype(F32)), _sigmoid(ins[1][...].astype(F32))
        outs[0][...] = (dm * sa).astype(BF16)
        outs[1][...] = (dm * sb).astype(BF16)
        outs[2][:, 0:D] = (dm * ins[2][...].astype(F32) * sa * (1.0 - sa)).astype(BF16)
        outs[2][:, D:2 * D] = (dm * ins[3][...].astype(F32) * sb * (1.0 - sb)).astype(BF16)

    gate_cols = lambda tm, tn: pl.BlockSpec((tm, 2 * D), lambda i, j, kk: (i, C_GA // (2 * D)))
    return _mm_ep([(dxb, w_out)], "nt", "mm_dmerged_bwd", epilogue,
                  [(proj, _tile(C_GA)), (proj, _tile(C_GB)), (za, _tile()), (zb, _tile())],
                  [(_sds(t, D, BF16), _tile()), (_sds(t, D, BF16), _tile()), (_sds(t, NP, BF16), gate_cols)], 1024, D)


def _dya_bwd(dza, wa, hraw, proj, g, dproj):
    t = dza.shape[0]

    def epilogue(accs, ins, outs, i, j):
        h_ref, o_ref, g_ref, _ = ins
        dh_ref, do_ref, dg_ref = outs

        @pl.when(i == 0)
        def _():
            dg_ref[...] = jnp.zeros_like(dg_ref)

        dy = accs[0]
        so = _sigmoid(o_ref[...].astype(F32))
        for h in range(MLH):
            sl = slice(h * DV, (h + 1) * DV)
            xn, rstd = _rms(h_ref[:, sl])
            gs = g_ref[:, sl]
            do_ref[:, sl] = (dy[:, sl] * xn * gs * so[:, sl] * (1.0 - so[:, sl])).astype(BF16)
            dhn = dy[:, sl] * so[:, sl]
            dg_ref[:, sl] += jnp.sum(dhn * xn, axis=0, keepdims=True)
            dh_ref[:, sl] = _rms_bwd(xn, rstd, dhn * gs)

    return _mm_ep([(dza, wa)], "nt", "mm_dya_bwd", epilogue,
                  [(hraw, _tile()), (proj, _tile(C_O)), (g, _row()), (dproj, lambda tm, tn: _ANY)],
                  [(_sds(t, D, F32), _tile()), (_sds(t, NP, BF16), _tile(C_O)), (_sds(1, D, F32), _row())],
                  1024, D, aliases={3: 1})


def _up_act(hn, w_up):
    t = hn.shape[0]

    def epilogue(accs, ins, outs, i, j):
        r = jnp.maximum(accs[0], 0.0)
        outs[0][...] = (r * r).astype(BF16)
        outs[1][...] = accs[0].astype(BF16)

    return _mm_ep([(hn, w_up)], "nn", "mm_up_act", epilogue, [],
                  [(_sds(t, DFF, BF16), _tile()), (_sds(t, DFF, BF16), _tile())], 1024, 1024)


def _da_du(dxb, w_down, u):
    t = dxb.shape[0]

    def epilogue(accs, ins, outs, i, j):
        outs[0][...] = (accs[0] * 2.0 * jnp.maximum(ins[0][...].astype(F32), 0.0)).astype(BF16)

    return _mm_ep([(dxb, w_down)], "nt", "mm_da_du", epilogue, [(u, _tile())], [(_sds(t, DFF, BF16), _tile())],
                  1024, 1024)[0]


def _resid_norm_mm(a, w, x, g, name):
    t = x.shape[0]

    def epilogue(accs, ins, outs, i, j):
        x1 = ins[0][...] + accs[0]
        outs[0][...] = x1
        xn, _ = _rms(x1)
        outs[1][...] = (xn * ins[1][...]).astype(BF16)

    return _mm_ep([(a, w)], "nn", name, epilogue, [(x, _tile()), (g, _row())],
                  [(_sds(t, D, F32), _tile()), (_sds(t, D, BF16), _tile())], 1024, D)


def _norm_bwd_mm(dy, w, x, g, dres, name):
    t = x.shape[0]

    def epilogue(accs, ins, outs, i, j):
        @pl.when(i == 0)
        def _():
            outs[2][...] = jnp.zeros_like(outs[2])

        dh = accs[0]
        xn, rstd = _rms(ins[0][...])
        outs[2][...] += jnp.sum(dh * xn, axis=0, keepdims=True)
        dx = ins[2][...] + _rms_bwd(xn, rstd, dh * ins[1][...])
        outs[0][...] = dx
        outs[1][...] = dx.astype(BF16)

    return _mm_ep([(dy, w)], "nt", name, epilogue, [(x, _tile()), (g, _row()), (dres, _tile())],
                  [(_sds(t, D, F32), _tile()), (_sds(t, D, BF16), _tile()), (_sds(1, D, F32), _row())], 1024, D)


def _ple_final_mm(hn2, w_gate, x2, pp, target, gf):
    t = x2.shape[0]

    def epilogue(accs, ins, outs, i, j):
        loss_ref, dg_ref, dx_ref, dpp_ref, dgp_ref = outs

        @pl.when(i == 0)
        def _():
            loss_ref[...] = jnp.zeros_like(loss_ref)
            dg_ref[...] = jnp.zeros_like(dg_ref)

        gate = _sigmoid(accs[0])
        pp_v = ins[1][...]
        x3 = ins[0][...] + gate * pp_v
        xn, rstd = _rms(x3)
        gf_v = ins[3][...]
        err = xn * gf_v - ins[2][...]
        loss_ref[...] += (0.5 / D) * jnp.sum(jnp.sum(err * err, axis=1, keepdims=True), axis=0, keepdims=True)
        dy = err * (1.0 / D)
        dg_ref[...] += jnp.sum(dy * xn, axis=0, keepdims=True)
        dx3 = _rms_bwd(xn, rstd, dy * gf_v)
        dx_ref[...] = dx3
        dpp_ref[...] = (dx3 * gate).astype(BF16)
        dgp_ref[...] = (dx3 * pp_v * gate * (1.0 - gate)).astype(BF16)

    one = lambda tm, tn: pl.BlockSpec((1, 1), lambda i, j, kk: (0, 0))
    return _mm_ep([(hn2, w_gate)], "nn", "mm_ple_final", epilogue,
                  [(x2, _tile()), (pp, _tile()), (target, _tile()), (gf, _row())],
                  [(_sds(1, 1, F32), one), (_sds(1, D, F32), _row()), (_sds(t, D, F32), _tile()),
                   (_sds(t, D, BF16), _tile()), (_sds(t, D, BF16), _tile())], 512, D)


_WIN_SEGMENTS = ((0, 3072, C_QK), (3072, 8, C_IF), (3080, 1024, C_QSW), (4104, 256, C_KV), (4360, 256, C_KV + 256),
                 (4616, 1024, C_GA), (5640, 1024, C_GB))
_WIN_SHARD = N_IN // 4


def _win_pieces():
    out = []
    for src, width, dst in _WIN_SEGMENTS:
        while width:
            chip, col = divmod(src, _WIN_SHARD)
            n = min(width, _WIN_SHARD - col)
            out.append((chip, col, n, dst))
            src, dst, width = src + n, dst + n, width - n
    return out


def _win_pad(shards):
    rows = shards.shape[1]
    tr = _pick(rows, 256)

    def body(s_ref, o_ref):
        for chip, col, n, dst in _win_pieces():
            o_ref[:, dst:dst + n] = s_ref[chip, :, col:col + n]
        o_ref[:, C_IF + 8:NP] = jnp.zeros((tr, NP - C_IF - 8), shards.dtype)

    return pl.pallas_call(
        body, name="win_pad", grid=(rows // tr,), in_specs=[pl.BlockSpec((4, tr, _WIN_SHARD), lambda i: (0, i, 0))],
        out_specs=pl.BlockSpec((tr, NP), lambda i: (i, 0)), out_shape=jax.ShapeDtypeStruct((rows, NP), shards.dtype),
        compiler_params=_params(),
    )(shards)


def _win_unpad(wp):
    rows = wp.shape[0]
    tr = _pick(rows, 256)

    def body(p_ref, o_ref):
        for chip, col, n, dst in _win_pieces():
            o_ref[chip, :, col:col + n] = p_ref[:, dst:dst + n]

    return pl.pallas_call(
        body, name="win_unpad", grid=(rows // tr,), in_specs=[pl.BlockSpec((tr, NP), lambda i: (i, 0))],
        out_specs=pl.BlockSpec((4, tr, _WIN_SHARD), lambda i: (0, i, 0)),
        out_shape=jax.ShapeDtypeStruct((4, rows, _WIN_SHARD), wp.dtype), compiler_params=_params(),
    )(wp)


def _local_step(x, p, target, w, late_weights=None, early_grads=None, mid_grads=None, last_grad=None):
    t = x.shape[0]
    pb = p.astype(BF16)
    w = dict(w)

    h0, proj, gates = _proj_in(x, w["norm_mix_g"], w["w_in"])
    qk = _conv_silu_fwd(proj, w["conv_qk"])
    grow, sneg_row = _gates_fwd(gates[:, 0:8].T, w["b_if"].reshape(8, 1))
    gcol, sneg_col = grow.T, sneg_row.T
    hraw, ya, cs, st = _mlstm_fwd(qk, proj, grow, gcol, w["mlstm_norm_g"])
    yb, lse = _swa_fwd(proj, w["sinks"])
    if late_weights is not None:
        w.update(late_weights(yb))
    merged, za, zb = _branch_merge(ya, yb, w["w_branch_a"], w["w_branch_b"], proj)
    x1, hn1 = _resid_norm_mm(merged, w["w_out"], x, w["norm_mlp_g"], "mm_out_norm")
    act, u = _up_act(hn1, w["w_up"])
    x2, hn2 = _resid_norm_mm(act, w["w_down"], x1, w["norm_ple_g"], "mm_down_norm")
    pp = _mm(pb, w["w_ple_proj"], "nn", F32, "mm_ple_proj")
    loss, d_final_g, dx3, dpp, dgpre = _ple_final_mm(hn2, w["w_ple_gate"], x2, pp, target, w["final_norm_g"])

    g = {"final_norm_g": d_final_g}
    g["w_ple_proj"] = _mm(pb, dpp, "tn", F32, "mm_d_ple_proj", out_chunks=4)
    g["w_ple_gate"] = _mm(hn2, dgpre, "tn", F32, "mm_d_ple_gate")
    dx2, dx2b, g["norm_ple_g"] = _norm_bwd_mm(dgpre, w["w_ple_gate"], x2, w["norm_ple_g"], dx3, "mm_dhn2_norm")
    g["w_down"] = _mm(act, dx2b, "tn", F32, "mm_d_down")
    du = _da_du(dx2b, w["w_down"], u)
    g["w_up"] = _mm(hn1, du, "tn", F32, "mm_d_up", out_chunks=4)
    dx1, dx1b, g["norm_mlp_g"] = _norm_bwd_mm(du, w["w_up"], x1, w["norm_mlp_g"], dx2, "mm_dhn1_norm")
    g["w_out"] = _mm(merged, dx1b, "tn", F32, "mm_d_out")
    dza, dzb, dproj = _dmerged_bwd(dx1b, w["w_out"], proj, za, zb)
    g["w_branch_a"] = _mm(ya, dza, "tn", F32, "mm_d_branch_a")
    g["w_branch_b"] = _mm(yb, dzb, "tn", F32, "mm_d_branch_b")
    gain = w["mlstm_norm_g"] if early_grads is None else w["mlstm_norm_g"] + early_grads(g)
    dyb = _mm(dzb, w["w_branch_b"], "nt", F32, "mm_dyb")
    dhraw, dproj, g["mlstm_norm_g"] = _dya_bwd(dza, w["w_branch_a"], hraw, proj, gain, dproj)
    if mid_grads is not None:
        sneg_col = sneg_col + mid_grads(dhraw)
    dqk, dproj, dif, g["b_if"] = _mlstm_bwd(qk, proj, grow, gcol, sneg_col, cs, st, hraw, dhraw, dproj)
    dc, g["conv_qk"] = _conv_silu_bwd_a(proj, w["conv_qk"], dqk)
    dproj = _conv_silu_bwd_b(dc, w["conv_qk"], dproj)
    dproj, dkv_self, dkv_prev, g["sinks"] = _swa_bwd(proj, w["sinks"], lse, dyb, dproj)
    dproj = _kv_combine(dkv_self, dkv_prev, dif, dproj)
    g["w_in"] = _mm(h0, dproj, "tn", F32, "mm_d_in")
    gain = w["norm_mix_g"] if last_grad is None else w["norm_mix_g"] + last_grad(g)
    grad_x, _, g["norm_mix_g"] = _norm_bwd_mm(dproj, w["w_in"], x, gain, dx1, "mm_dh0_norm")
    return loss, grad_x, g


_W4 = ("w_branch_a", "w_branch_b", "w_out", "w_ple_gate")
_SHARDED_NAMES = ("w_in", "w_up", "w_down", "w_ple_proj", "conv_qk") + _W4
_SMALL_ROWS = 16
_CONV_ROW = 8


def _group(s):
    return [s["w_in"], jnp.concatenate([s[n] for n in _W4], axis=0), s["w_up"], s["w_down"], s["w_ple_proj"]]


def _ungroup(arrs):
    out = {"w_in": arrs[0], "w_up": arrs[2], "w_down": arrs[3], "w_ple_proj": arrs[4]}
    rows = arrs[1].shape[0] // len(_W4)
    for i, n in enumerate(_W4):
        out[n] = arrs[1][i * rows:(i + 1) * rows]
    return out


def _rows_tile(rows):
    return 256 if rows % 256 == 0 else rows


_SMALL = ("norm_mix_g", "mlstm_norm_g", "norm_mlp_g", "norm_ple_g", "final_norm_g")


def _pack_small(vals, extra=None, conv=None):
    rows = [vals[n].reshape(1, D) for n in _SMALL]
    tail = [vals["b_if"].reshape(1, 8), vals["sinks"].reshape(1, SWH)]
    used = 8 + SWH
    if extra is not None:
        tail.append(extra.reshape(1, 1))
        used += 1
    tail.append(jnp.zeros((1, D - used), F32))
    rows.append(jnp.concatenate(tail, axis=1))
    rows.append(jnp.zeros((_CONV_ROW - len(rows), D), F32))
    rows.append(jnp.zeros((CONV, D), F32) if conv is None else conv)
    rows.append(jnp.zeros((_SMALL_ROWS - _CONV_ROW - CONV, D), F32))
    return jnp.concatenate(rows, axis=0)


def _unpack_small(slab, shapes):
    out = {n: slab[i].reshape(shapes[n]) for i, n in enumerate(_SMALL)}
    out["b_if"] = slab[5, 0:8].reshape(shapes["b_if"])
    out["sinks"] = slab[5, 8:8 + SWH].reshape(shapes["sinks"])
    return out


_MESH = pl.DeviceIdType.MESH
_HBM = pl.BlockSpec(memory_space=pltpu.HBM)
_VMEM = pl.BlockSpec(memory_space=pltpu.VMEM)


def _place():
    x, y, c = lax.axis_index("x"), lax.axis_index("y"), lax.axis_index("c")
    return x, y, c, 2 * x + y


def _chip_peer(x, y, r):
    return (x ^ (r >> 1), y ^ (r & 1))


def _half(ref, which):
    h = ref.shape[-2] // 2
    return pl.ds(which * h, h)


def _allgather_weights(shards, conv):
    n = len(shards)

    def body(*refs):
        ins, conv_ref = refs[:n], refs[n]
        outs, conv_out = refs[n + 1:2 * n + 1], refs[2 * n + 1]
        send_a, recv_a, send_b, recv_b, send_c, recv_c, local_sems = refs[2 * n + 2:]
        x, y, c, j = _place()
        sibling = (x, y, 1 - c)
        local = [pltpu.make_async_copy(ins[k], outs[k].at[j], local_sems.at[k]) for k in range(n)]
        local.append(pltpu.make_async_copy(conv_ref, conv_out.at[j], local_sems.at[n]))
        for cp in local:
            cp.start()

        def copy_a(k, r, chip):
            rows = _half(ins[k], c)
            return pltpu.make_async_remote_copy(
                src_ref=ins[k].at[rows], dst_ref=outs[k].at[chip, rows], send_sem=send_a.at[3 * k + r - 1],
                recv_sem=recv_a.at[3 * k + r - 1], device_id=(*_chip_peer(x, y, r), c), device_id_type=_MESH)

        def copy_b(k, r, chip, which):
            rows = _half(ins[k], which)
            return pltpu.make_async_remote_copy(
                src_ref=outs[k].at[chip, rows], dst_ref=outs[k].at[chip, rows], send_sem=send_b.at[3 * k + r - 1],
                recv_sem=recv_b.at[3 * k + r - 1], device_id=sibling, device_id_type=_MESH)

        def copy_c(r, chip):
            return pltpu.make_async_remote_copy(
                src_ref=conv_ref, dst_ref=conv_out.at[chip], send_sem=send_c.at[r - 1],
                recv_sem=recv_c.at[r - 1], device_id=(*_chip_peer(x, y, r), c), device_id_type=_MESH)

        for k in range(n):
            for r in (1, 2, 3):
                copy_a(k, r, j).start()
        for r in (1, 2, 3):
            copy_c(r, j).start()
        for k in range(n):
            for r in (1, 2, 3):
                copy_a(k, r, j ^ r).wait_recv()
                copy_b(k, r, j ^ r, c).start()
        for k in range(n):
            for r in (1, 2, 3):
                copy_b(k, r, j ^ r, 1 - c).wait_recv()
        for r in (1, 2, 3):
            copy_c(r, j ^ r).wait_recv()
        for k in range(n):
            for r in (1, 2, 3):
                copy_a(k, r, j).wait_send()
                copy_b(k, r, j ^ r, c).wait_send()
        for r in (1, 2, 3):
            copy_c(r, j).wait_send()
        for cp in local:
            cp.wait()

    return pl.pallas_call(
        body, name="allgather_weights",
        out_shape=[jax.ShapeDtypeStruct((4,) + s.shape, s.dtype) for s in shards]
        + [jax.ShapeDtypeStruct((4,) + conv.shape, F32)],
        in_specs=[_HBM] * (n + 1), out_specs=[_HBM] * (n + 1),
        scratch_shapes=[pltpu.SemaphoreType.DMA((3 * n,))] * 4 + [pltpu.SemaphoreType.DMA((3,))] * 2
        + [pltpu.SemaphoreType.DMA((n + 1,))],
    )(*shards, conv)


_SEM = pl.BlockSpec(memory_space=pltpu.SEMAPHORE)
_DATAFLOW = pltpu.SideEffectType.DATAFLOW_SIDE_EFFECTING


def _late_peer_copy(src_ref, land_ref, send_sems, recv_sems, x, y, c, j, r, chip):
    return pltpu.make_async_remote_copy(
        src_ref=src_ref, dst_ref=land_ref.at[chip], send_sem=send_sems.at[r - 1], recv_sem=recv_sems.at[r - 1],
        device_id=(*_chip_peer(x, y, r), c), device_id_type=_MESH)


def _late_gather_start(rest):
    def body(rest_ref, land_ref, send_sems, recv_sems, rest_thru, land_thru, token):
        x, y, c, j = _place()
        for r in (1, 2, 3):
            _late_peer_copy(rest_ref, land_ref, send_sems, recv_sems, x, y, c, j, r, j).start()
        token[...] = jnp.zeros_like(token)

    j = 2 * lax.axis_index("x") + lax.axis_index("y")
    land = lax.dynamic_update_slice(lax.empty((4,) + rest.shape, rest.dtype), rest[None], (j, 0, 0))
    return pl.pallas_call(
        body, name="late_gather_start",
        out_shape=(pltpu.SemaphoreType.DMA((3,)), pltpu.SemaphoreType.DMA((3,)), pltpu.HBM(rest.shape, rest.dtype),
                   pltpu.HBM(land.shape, land.dtype), jax.ShapeDtypeStruct((8, 128), F32)),
        in_specs=(_HBM, _HBM), out_specs=(_SEM, _SEM, _HBM, _HBM, _VMEM), input_output_aliases={0: 2, 1: 3},
        compiler_params=pltpu.CompilerParams(has_side_effects=_DATAFLOW),
    )(pltpu.with_memory_space_constraint(rest, pltpu.HBM), pltpu.with_memory_space_constraint(land, pltpu.HBM))


def _late_gather_wait(send_sems, recv_sems, rest_thru, land_thru, after):
    def body(rest_ref, land_ref, send_sems, recv_sems, after_ref, rest_dead, got_ref):
        x, y, c, j = _place()
        for r in (1, 2, 3):
            cp = _late_peer_copy(rest_ref, land_ref, send_sems, recv_sems, x, y, c, j, r, j ^ r)
            cp.wait_send()
            cp.wait_recv()

    return pl.pallas_call(
        body, name="late_gather_wait",
        out_shape=(pltpu.HBM(rest_thru.shape, rest_thru.dtype), pltpu.HBM(land_thru.shape, land_thru.dtype)),
        in_specs=(_HBM, _HBM, _SEM, _SEM, _ANY), out_specs=(_HBM, _HBM), input_output_aliases={0: 0, 1: 1},
        compiler_params=pltpu.CompilerParams(has_side_effects=_DATAFLOW),
    )(rest_thru, land_thru, send_sems, recv_sems, after)[1]


def _pair_sum(g, theirs, j, c, name):
    _, h, cols = theirs.shape
    tr = _rows_tile(h)
    nb = h // tr

    def body(idx_ref, a_ref, b_ref, own_ref, ob_ref):
        s = a_ref[0] + b_ref[0]
        ob_ref[0] = s.astype(BF16)

        @pl.when(pl.program_id(1) == idx_ref[0])
        def _():
            own_ref[...] = s

    blk = pl.BlockSpec((1, tr, cols), lambda i, k, idx_ref: (k, i, 0))
    return pl.pallas_call(
        body, name=name,
        grid_spec=pltpu.PrefetchScalarGridSpec(
            num_scalar_prefetch=1, grid=(nb, 4),
            in_specs=[pl.BlockSpec((1, tr, cols), lambda i, k, idx_ref: (k, idx_ref[1] * nb + i, 0)), blk],
            out_specs=[pl.BlockSpec((tr, cols), lambda i, k, idx_ref: (i, 0)), blk]),
        out_shape=[jax.ShapeDtypeStruct((h, cols), F32), jax.ShapeDtypeStruct(theirs.shape, BF16)],
        compiler_params=_params(),
    )(jnp.stack([j, c]).astype(jnp.int32), g, theirs)


def _chip_copies(srcs, lands, send_sems, recv_sems):
    x, y, c, j = _place()
    return [pltpu.make_async_remote_copy(
        src_ref=srcs[k].at[j ^ r], dst_ref=lands[k].at[r - 1], send_sem=send_sems.at[3 * k + r - 1],
        recv_sem=recv_sems.at[3 * k + r - 1], device_id=(*_chip_peer(x, y, r), c), device_id_type=_MESH)
        for k in range(len(srcs)) for r in (1, 2, 3)]


def _pair_copies(srcs, lands, send_sems, recv_sems):
    x, y, c, _ = _place()
    return [pltpu.make_async_remote_copy(
        src_ref=srcs[k].at[:, _half(srcs[k], 1 - c)], dst_ref=lands[k], send_sem=send_sems.at[k],
        recv_sem=recv_sems.at[k], device_id=(x, y, 1 - c), device_id_type=_MESH) for k in range(len(srcs))]


def _split_start(name, srcs, lands, copies, n_sems):
    n = len(srcs)

    def body(*refs):
        for cp in copies(refs[:n], refs[n:2 * n], refs[2 * n], refs[2 * n + 1]):
            cp.start()
        refs[-1][...] = jnp.zeros_like(refs[-1])

    arrays = list(srcs) + list(lands)
    out = pl.pallas_call(
        body, name=name,
        out_shape=(pltpu.SemaphoreType.DMA((n_sems,)), pltpu.SemaphoreType.DMA((n_sems,)),
                   *[pltpu.HBM(a.shape, a.dtype) for a in arrays], jax.ShapeDtypeStruct((8, 128), F32)),
        in_specs=[_HBM] * (2 * n), out_specs=(_SEM, _SEM, *([_HBM] * (2 * n)), _VMEM),
        input_output_aliases={k: 2 + k for k in range(2 * n)},
        compiler_params=pltpu.CompilerParams(has_side_effects=_DATAFLOW),
    )(*[pltpu.with_memory_space_constraint(a, pltpu.HBM) for a in arrays])
    return out[0], out[1], list(out[2:2 + n]), list(out[2 + n:2 + 2 * n]), out[-1]


def _split_wait(name, send_sems, recv_sems, srcs_thru, lands_thru, after, copies):
    n = len(srcs_thru)

    def body(*refs):
        for cp in copies(refs[:n], refs[n:2 * n], refs[2 * n], refs[2 * n + 1]):
            cp.wait_send()
            cp.wait_recv()

    arrays = list(srcs_thru) + list(lands_thru)
    out = pl.pallas_call(
        body, name=name, out_shape=tuple(pltpu.HBM(a.shape, a.dtype) for a in arrays),
        in_specs=[_HBM] * (2 * n) + [_SEM, _SEM, _ANY], out_specs=tuple([_HBM] * (2 * n)),
        input_output_aliases={k: k for k in range(2 * n)},
        compiler_params=pltpu.CompilerParams(has_side_effects=_DATAFLOW),
    )(*arrays, send_sems, recv_sems, after)
    return list(out[:n]), list(out[n:])


def _chip_exchange_start(ss, tag):
    lands = [lax.empty((3,) + s.shape[1:], s.dtype) for s in ss]
    return _split_start("chip_exchange_start_" + tag, ss, lands, _chip_copies, 3 * len(ss))


def _chip_exchange_wait(send_sems, recv_sems, ss_thru, lands_thru, after, tag):
    return _split_wait("chip_exchange_wait_" + tag, send_sems, recv_sems, ss_thru, lands_thru, after, _chip_copies)[1]


def _pair_exchange_start(gs, tag):
    lands = [lax.empty((4, g.shape[1] // 2, g.shape[2]), g.dtype) for g in gs]
    return _split_start("pair_exchange_start_" + tag, gs, lands, _pair_copies, len(gs))


def _pair_exchange_wait(send_sems, recv_sems, gs_thru, lands_thru, after, tag):
    return _split_wait("pair_exchange_wait_" + tag, send_sems, recv_sems, gs_thru, lands_thru, after, _pair_copies)


def _reduce4(own, others, c, name):
    h, cols = own.shape
    tr = _rows_tile(h)
    nb = h // tr

    def body(c_ref, s_ref, a0, a1, a2, o_ref):
        o_ref[...] = ((s_ref[...] + a0[0].astype(F32)) + a1[0].astype(F32)) + a2[0].astype(F32)

    def other(r):
        return pl.BlockSpec((1, tr, cols), lambda i, c_ref: (r, i, 0))

    return pl.pallas_call(
        body, name=name,
        grid_spec=pltpu.PrefetchScalarGridSpec(
            num_scalar_prefetch=1, grid=(nb,),
            in_specs=[pl.BlockSpec((tr, cols), lambda i, c_ref: (i, 0)), other(0), other(1), other(2)],
            out_specs=pl.BlockSpec((tr, cols), lambda i, c_ref: (c_ref[0] * nb + i, 0))),
        out_shape=jax.ShapeDtypeStruct((2 * h, cols), F32), compiler_params=_params(),
    )(c.reshape(1).astype(jnp.int32), own, others, others, others)


def _sibling_share(fulls, name):
    n = len(fulls)

    def body(*refs):
        outs, send_sems, recv_sems = refs[n:2 * n], refs[2 * n], refs[2 * n + 1]
        x, y, c, _ = _place()
        cps = [pltpu.make_async_remote_copy(
            src_ref=outs[k].at[_half(outs[k], c)], dst_ref=outs[k].at[_half(outs[k], c)], send_sem=send_sems.at[k],
            recv_sem=recv_sems.at[k], device_id=(x, y, 1 - c), device_id_type=_MESH) for k in range(n)]
        for cp in cps:
            cp.start()
        for cp in cps:
            cp.wait()

    return pl.pallas_call(
        body, name=name, out_shape=[jax.ShapeDtypeStruct(f.shape, F32) for f in fulls],
        in_specs=[_HBM] * n, out_specs=[_HBM] * n, input_output_aliases={k: k for k in range(n)},
        scratch_shapes=[pltpu.SemaphoreType.DMA((n,))] * 2,
    )(*fulls)


def _adamw(w, g, m, v):
    m1 = ADAM_B1 * m + (1.0 - ADAM_B1) * g
    v1 = ADAM_B2 * v + (1.0 - ADAM_B2) * (g * g)
    m_hat = m1 / (1.0 - ADAM_B1 ** ADAM_STEP)
    v_hat = v1 / (1.0 - ADAM_B2 ** ADAM_STEP)
    delta = -ADAM_LR * (m_hat / (jnp.sqrt(v_hat) + ADAM_EPS) + ADAM_WD * w)
    return delta, m1, v1


def _adamw_call(w, g, m, v, name):
    rows, cols = w.shape

    def body(w_ref, g_ref, m_ref, v_ref, d_out, m_out, v_out):
        delta, m1, v1 = _adamw(w_ref[...], g_ref[...], m_ref[...], v_ref[...])
        d_out[...] = delta
        m_out[...] = m1
        v_out[...] = v1

    if rows % 8 == 0:
        tr = _rows_tile(rows)
        blk, grid = pl.BlockSpec((tr, cols), lambda i: (i, 0)), (rows // tr,)
    else:
        blk, grid = pl.BlockSpec((rows, 128), lambda i: (0, i)), (cols // 128,)
    return pl.pallas_call(
        body, name=name, grid=grid, in_specs=[blk] * 4, out_specs=[blk] * 3,
        out_shape=[jax.ShapeDtypeStruct((rows, cols), F32)] * 3, compiler_params=_params(),
    )(w, g, m, v)


def _small_allreduce(vals):
    def body(v_ref, out_ref, buf, send_sems, recv_sems):
        x, y, c, j = _place()
        me = 2 * j + c
        buf[0] = v_ref[...]

        def copy(r):
            return pltpu.make_async_remote_copy(
                src_ref=v_ref, dst_ref=buf.at[r], send_sem=send_sems.at[r - 1], recv_sem=recv_sems.at[r - 1],
                device_id=(x ^ (r >> 2), y ^ ((r >> 1) & 1), c ^ (r & 1)), device_id_type=_MESH)

        for r in range(1, 8):
            copy(r).start()
        for r in range(1, 8):
            copy(r).wait()
        acc = buf[me ^ 0]
        for d in range(1, 8):
            acc = acc + buf[me ^ d]
        out_ref[...] = acc

    return pl.pallas_call(
        body, name="small_allreduce", out_shape=jax.ShapeDtypeStruct((_SMALL_ROWS, D), F32),
        in_specs=[_VMEM], out_specs=_VMEM,
        scratch_shapes=[pltpu.VMEM((8, _SMALL_ROWS, D), F32), pltpu.SemaphoreType.DMA((7,)),
                        pltpu.SemaphoreType.DMA((7,))],
    )(vals)


_NAMES = ("norm_mix_g", "w_in", "conv_qk", "b_if", "mlstm_norm_g", "sinks", "w_branch_a", "w_branch_b", "w_out",
          "norm_mlp_g", "w_up", "w_down", "norm_ple_g", "w_ple_gate", "w_ple_proj", "final_norm_g")
_GROUP_NAMES = ("w_in", "w4", "w_up", "w_down", "w_ple_proj")


def _step(x, p, target, w, m, v):
    c = lax.axis_index("c")
    j = 2 * lax.axis_index("x") + lax.axis_index("y")

    def shards(d):
        return {n: d[n][0] for n in _SHARDED_NAMES}

    ws = shards(w)
    w_in_all, conv_all = _allgather_weights([ws["w_in"].astype(BF16)], ws["conv_qk"])
    rows_pp = PLE * (D // 4) // D
    rest = jnp.concatenate([ws[n] for n in _W4] + [ws["w_up"], ws["w_down"], ws["w_ple_proj"].reshape(rows_pp, D)],
                           axis=0)
    rest = (rest + 0.0 * conv_all[0, 0, 0]).astype(BF16)
    send_sems, recv_sems, rest_thru, land_thru, token = _late_gather_start(rest)
    full = {n: w[n] for n in ("mlstm_norm_g", "norm_mlp_g", "norm_ple_g", "b_if", "sinks")}
    full["norm_mix_g"] = w["norm_mix_g"] + token[0, 0]
    full["final_norm_g"] = w["final_norm_g"].reshape(1, D)
    full["w_in"] = _win_pad(w_in_all)
    full["conv_qk"] = jnp.swapaxes(conv_all, 0, 1).reshape(CONV, D)

    def late_weights(after):
        land = _late_gather_wait(send_sems, recv_sems, rest_thru, land_thru, after)
        out = {n: land[:, i * (D // 4):(i + 1) * (D // 4)].reshape(D, D) for i, n in enumerate(_W4)}
        out["w_up"] = land[:, D:2 * D]
        out["w_down"] = land[:, 2 * D:3 * D].reshape(DFF, D)
        out["w_ple_proj"] = jnp.swapaxes(land[:, 3 * D:3 * D + rows_pp].reshape(4, PLE, D // 4), 0, 1).reshape(PLE, D)
        return out

    early, last = {}, {}

    def pair_sums(by_dest, theirs, names):
        return [_pair_sum(a, b, j, c, "pair_sum_" + n) for a, b, n in zip(by_dest, theirs, names)]

    def early_grads(g):
        by_dest = [jnp.stack([g[n].reshape(4, D // 4, D) for n in _W4], axis=1).reshape(4, D, D),
                   g["w_up"], g["w_down"].reshape(4, DFF // 4, D), g["w_ple_proj"]]
        *early["pair"], token = _pair_exchange_start(by_dest, "early")
        return token[0, 0]

    def mid_grads(after):
        early["sums"] = pair_sums(*_pair_exchange_wait(*early["pair"], after, "early"), _GROUP_NAMES[1:])
        *early["flight"], token = _chip_exchange_start([s[1] for s in early["sums"]], "early")
        return token[0, 0]

    def last_grad(g):
        *last["pair"], token = _pair_exchange_start([_win_unpad(g["w_in"])], "w_in")
        return token[0, 0]

    loss, grad_x, g = _local_step(x[0], p[0, 0], target[0], full, late_weights, early_grads, mid_grads, last_grad)

    last["sums"] = pair_sums(*_pair_exchange_wait(*last["pair"], grad_x, "w_in"), _GROUP_NAMES[:1])
    *last["flight"], token = _chip_exchange_start([s[1] for s in last["sums"]], "w_in")

    def reduce_share(sums, others, names, tag):
        halves = [_reduce4(s[0], b, c, "reduce4_" + n) for s, b, n in zip(sums, others, names)]
        return list(_sibling_share(halves, "sibling_share_" + tag))

    ms, vs = shards(m), shards(v)
    grads = reduce_share(early["sums"], _chip_exchange_wait(*early["flight"], token, "early"), _GROUP_NAMES[1:], "early")
    upd = [_adamw_call(wa, ga, ma, va, "adamw_" + n)
           for wa, ga, ma, va, n in zip(_group(ws)[1:], grads, _group(ms)[1:], _group(vs)[1:], _GROUP_NAMES[1:])]
    small_g = _small_allreduce(_pack_small(g, extra=loss, conv=g["conv_qk"]))
    conv_g = lax.dynamic_slice(small_g[_CONV_ROW:_CONV_ROW + CONV], (0, j * (D // 4)), (CONV, D // 4))
    conv_upd = _adamw_call(ws["conv_qk"], conv_g, ms["conv_qk"], vs["conv_qk"], "adamw_conv")
    small_upd = _adamw_call(_pack_small(w), small_g, _pack_small(m), _pack_small(v), "adamw_small")

    done = sum(a[0][0:1, 0:1] for a in upd + [conv_upd, small_upd])
    others = _chip_exchange_wait(*last["flight"], done, "w_in")
    grads = reduce_share(last["sums"], others, _GROUP_NAMES[:1], "w_in") + list(grads)
    upd_in = _adamw_call(*[jnp.swapaxes(a, 0, 1) for a in (ws["w_in"], grads[0], ms["w_in"], vs["w_in"])], "adamw_w_in")
    upd = [[jnp.swapaxes(a, 0, 1) for a in upd_in]] + upd

    shapes = {n: w[n].shape for n in _NAMES}
    res = []
    for k in range(4):
        big = _ungroup(list(grads) if k == 0 else [u[k - 1] for u in upd])
        big["conv_qk"] = conv_g if k == 0 else conv_upd[k - 1]
        leaves = _unpack_small(small_g if k == 0 else small_upd[k - 1], shapes)
        leaves.update({n: a.reshape(shapes[n]) for n, a in big.items()})
        res.append(leaves)

    out = [small_g[5, 8 + SWH], grad_x[None]]
    for k in range(4):
        out += [res[k][n] for n in _NAMES]
    return tuple(out)


def kernel(x, p, norm_mix_g, w_in, conv_qk, b_if, mlstm_norm_g, sinks, w_branch_a, w_branch_b, w_out, norm_mlp_g, w_up, w_down, norm_ple_g, w_ple_gate, w_ple_proj, final_norm_g, loss_target, m_norm_mix_g, m_w_in, m_conv_qk, m_b_if, m_mlstm_norm_g, m_sinks, m_w_branch_a, m_w_branch_b, m_w_out, m_norm_mlp_g, m_w_up, m_w_down, m_norm_ple_g, m_w_ple_gate, m_w_ple_proj, m_final_norm_g, v_norm_mix_g, v_w_in, v_conv_qk, v_b_if, v_mlstm_norm_g, v_sinks, v_w_branch_a, v_w_branch_b, v_w_out, v_norm_mlp_g, v_w_up, v_w_down, v_norm_ple_g, v_w_ple_gate, v_w_ple_proj, v_final_norm_g):
    w = dict(zip(_NAMES, (norm_mix_g, w_in, conv_qk, b_if, mlstm_norm_g, sinks, w_branch_a, w_branch_b, w_out,
                          norm_mlp_g, w_up, w_down, norm_ple_g, w_ple_gate, w_ple_proj, final_norm_g)))
    m = dict(zip(_NAMES, (m_norm_mix_g, m_w_in, m_conv_qk, m_b_if, m_mlstm_norm_g, m_sinks, m_w_branch_a,
                          m_w_branch_b, m_w_out, m_norm_mlp_g, m_w_up, m_w_down, m_norm_ple_g, m_w_ple_gate,
                          m_w_ple_proj, m_final_norm_g)))
    v = dict(zip(_NAMES, (v_norm_mix_g, v_w_in, v_conv_qk, v_b_if, v_mlstm_norm_g, v_sinks, v_w_branch_a,
                          v_w_branch_b, v_w_out, v_norm_mlp_g, v_w_up, v_w_down, v_norm_ple_g, v_w_ple_gate,
                          v_w_ple_proj, v_final_norm_g)))
    return _step(x, p, loss_target, w, m, v)
```

```python
import jax
import jax.numpy as jnp
from jax import lax
from jax.experimental import pallas as pl
from jax.experimental.pallas import tpu as pltpu

F32 = jnp.float32
BF16 = jnp.bfloat16

D = 1024
PLE = 256
MLH = 4
DQK = 128
DV = 256
CONV = 4
CHUNK = 128
SWH = 16
SWKV = 4
SWG = SWH // SWKV
HD = 64
WIN = 128
DFF = 4096
EPS = 1e-6
N_IN = 6664
NP = 7168
C_QK, C_V, C_O, C_QSW, C_GA, C_GB, C_KV, C_IF = 0, 1024, 2048, 3072, 4096, 5120, 6144, 6656
IFW = NP - C_IF

ADAM_LR = 0.001
ADAM_B1 = 0.9
ADAM_B2 = 0.999
ADAM_EPS = 1e-08
ADAM_WD = 0.01
ADAM_STEP = 10

TOK_TILE = 512
V7X_VMEM_BYTES = 64 * 1024 * 1024
VMEM_LIMIT = V7X_VMEM_BYTES - 6 * 1024 * 1024


def _params(**kw):
    return pltpu.CompilerParams(vmem_limit_bytes=VMEM_LIMIT, **kw)


def _pick(n, cap):
    if n <= cap:
        return n
    t = cap - cap % 128
    while t > 128 and n % t:
        t -= 128
    assert n % t == 0, (n, cap)
    return t


def _dot(a, b, dims):
    return lax.dot_general(a, b, (dims, ((), ())), preferred_element_type=F32)


def _dot_nn(a, b):
    return _dot(a, b, ((1,), (0,)))


def _dot_nt(a, b):
    return _dot(a, b, ((1,), (1,)))


def _dot_tn(a, b):
    return _dot(a, b, ((0,), (0,)))


def _sigmoid(x):
    return 1.0 / (1.0 + jnp.exp(-x))


def _mm(a, b, mode, out_dtype, name, out_chunks=1):
    if mode == "nn":
        (m, k), (k2, n) = a.shape, b.shape
    elif mode == "nt":
        (m, k), (n, k2) = a.shape, b.shape
    else:
        (k, m), (k2, n) = a.shape, b.shape
    assert k == k2, (a.shape, b.shape, mode)
    tm, tn, tk = _pick(m, 1024), _pick(n // out_chunks, 1024), _pick(k, 2048)
    nk = k // tk
    if mode == "nn":
        a_spec = pl.BlockSpec((tm, tk), lambda i, j, kk: (i, kk))
        b_spec = pl.BlockSpec((tk, tn), lambda i, j, kk: (kk, j))
        dot = _dot_nn
    elif mode == "nt":
        a_spec = pl.BlockSpec((tm, tk), lambda i, j, kk: (i, kk))
        b_spec = pl.BlockSpec((tn, tk), lambda i, j, kk: (j, kk))
        dot = _dot_nt
    else:
        a_spec = pl.BlockSpec((tk, tm), lambda i, j, kk: (kk, i))
        b_spec = pl.BlockSpec((tk, tn), lambda i, j, kk: (kk, j))
        dot = _dot_tn
    if out_chunks > 1:
        npc = (n // out_chunks) // tn
        out_spec = pl.BlockSpec((None, tm, tn), lambda i, j, kk: (j // npc, i, j % npc))
        out_shape = jax.ShapeDtypeStruct((out_chunks, m, n // out_chunks), out_dtype)
    else:
        out_spec = pl.BlockSpec((tm, tn), lambda i, j, kk: (i, j))
        out_shape = jax.ShapeDtypeStruct((m, n), out_dtype)

    def body(a_ref, b_ref, o_ref, acc_ref):
        kk = pl.program_id(2)

        @pl.when(kk == 0)
        def _():
            acc_ref[...] = jnp.zeros_like(acc_ref)

        acc_ref[...] += dot(a_ref[...], b_ref[...])

        @pl.when(kk == nk - 1)
        def _():
            o_ref[...] = acc_ref[...].astype(out_dtype)

    return pl.pallas_call(
        body, name=name, grid=(m // tm, n // tn, nk),
        in_specs=[a_spec, b_spec], out_specs=out_spec, out_shape=out_shape,
        scratch_shapes=[pltpu.VMEM((tm, tn), F32)],
        compiler_params=_params(dimension_semantics=("parallel", "parallel", "arbitrary")),
    )(a, b)


def _tile(col0=0):
    return lambda tm, tn: pl.BlockSpec((tm, tn), lambda i, j, kk: (i, col0 // tn + j))


def _row():
    return lambda tm, tn: pl.BlockSpec((1, tn), lambda i, j, kk: (0, j))


def _mm_ep(pairs, mode, name, epilogue, ins, outs, tm, tn, aliases=None):
    a0, b0 = pairs[0]
    bch = b0.shape[0] if b0.ndim == 3 else 1
    m, k = a0.shape
    tm = _pick(m, tm)
    n = b0.shape[-1] * bch if mode == "nn" else b0.shape[-2]
    tk = _pick(k // bch if mode == "nt" else k, 2048)
    nk = k // tk
    a_spec = pl.BlockSpec((tm, tk), lambda i, j, kk: (i, kk))
    if mode == "nn":
        dot = _dot_nn
        if bch > 1:
            bpc = (n // bch) // tn
            b_spec = pl.BlockSpec((None, tk, tn), lambda i, j, kk: (j // bpc, kk, j % bpc))
        else:
            b_spec = pl.BlockSpec((tk, tn), lambda i, j, kk: (kk, j))
    else:
        dot = _dot_nt
        if bch > 1:
            bpc = (k // bch) // tk
            b_spec = pl.BlockSpec((None, tn, tk), lambda i, j, kk: (kk // bpc, j, kk % bpc))
        else:
            b_spec = pl.BlockSpec((tn, tk), lambda i, j, kk: (j, kk))
    npair, nin, nout = len(pairs), len(ins), len(outs)

    def body(*refs):
        ab = refs[:2 * npair]
        in_refs = refs[2 * npair:2 * npair + nin]
        out_refs = refs[2 * npair + nin:2 * npair + nin + nout]
        accs = refs[2 * npair + nin + nout:]
        i, j, kk = pl.program_id(0), pl.program_id(1), pl.program_id(2)
        for p in range(npair):
            prod = dot(ab[2 * p][...], ab[2 * p + 1][...])

            @pl.when(kk == 0)
            def _():
                accs[p][...] = prod

            @pl.when(kk > 0)
            def _():
                accs[p][...] += prod

        @pl.when(kk == nk - 1)
        def _():
            epilogue([acc[...] for acc in accs], in_refs, out_refs, i, j)

    operands = [x for pair in pairs for x in pair] + [a for a, _ in ins]
    io_alias = {2 * npair + i: o for i, o in (aliases or {}).items()}
    return pl.pallas_call(
        body, name=name, grid=(m // tm, n // tn, nk),
        in_specs=[a_spec, b_spec] * npair + [mk(tm, tn) for _, mk in ins],
        out_specs=[mk(tm, tn) for _, mk in outs], out_shape=[s for s, _ in outs],
        scratch_shapes=[pltpu.VMEM((tm, tn), F32)] * npair, input_output_aliases=io_alias,
        compiler_params=_params(dimension_semantics=("arbitrary", "arbitrary", "arbitrary")),
    )(*operands)


def _tok(w, j=0):
    return pl.BlockSpec((TOK_TILE, w), lambda i: (i, j))


def _rep(shape):
    return pl.BlockSpec(shape, lambda i: (0,) * len(shape))


def _rms(x):
    rstd = lax.rsqrt(jnp.mean(x * x, axis=-1, keepdims=True) + EPS)
    return x * rstd, rstd


def _rms_bwd(xn, rstd, dxn):
    return rstd * (dxn - xn * jnp.mean(dxn * xn, axis=-1, keepdims=True))


def _halo_prev(w, j=0, rows=8):
    r = TOK_TILE // rows
    return pl.BlockSpec((rows, w), lambda i: (jnp.maximum(i * r - 1, 0), j))


def _last8(halo_ref):
    return halo_ref[...].astype(F32)[halo_ref.shape[0] - 8:]


def _halo_next(w, nt, j=0):
    r = TOK_TILE // 8
    return pl.BlockSpec((8, w), lambda i: (jnp.minimum((i + 1) * r, nt * r - 1), j))


def _shift_down(x, halo, s):
    if s == 0:
        return x
    r = pltpu.roll(x, s, 0)
    hs = pltpu.roll(halo, s, 0)
    row = lax.broadcasted_iota(jnp.int32, hs.shape, 0)
    top = jnp.where(row < s, hs, r[0:8])
    return jnp.concatenate([top, r[8:]], axis=0)


def _shift_up(x, halo, s):
    if s == 0:
        return x
    n = x.shape[0]
    r = pltpu.roll(x, n - s, 0)
    hs = pltpu.roll(halo, 8 - s, 0)
    row = lax.broadcasted_iota(jnp.int32, hs.shape, 0)
    bot = jnp.where(row >= 8 - s, hs, r[n - 8:])
    return jnp.concatenate([r[:n - 8], bot], axis=0)


def _bf(x):
    return x.astype(BF16).astype(F32)


def _conv_taps(x, halo, w):
    x, halo, w = _bf(x), _bf(halo), _bf(w)
    acc = x * w[CONV - 1:CONV, :]
    for j in range(CONV - 1):
        acc = acc + _shift_down(x, halo, CONV - 1 - j) * w[j:j + 1, :]
    return acc


_Q_SCALE = DQK ** -0.5


def _qscale_row():
    lane = lax.broadcasted_iota(jnp.int32, (1, D), 1)
    return jnp.where(lane < MLH * DQK, _Q_SCALE, 1.0).astype(F32)


def _conv_silu_fwd(proj, conv_w):
    t = proj.shape[0]

    def body(x_ref, halo_ref, w_ref, o_ref):
        halo = jnp.where(pl.program_id(0) > 0, _last8(halo_ref), 0.0)
        c = _conv_taps(x_ref[...].astype(F32), halo, w_ref[...])
        o_ref[...] = (c * _sigmoid(c) * _qscale_row()).astype(BF16)

    return pl.pallas_call(
        body, name="conv_silu_fwd", grid=(t // TOK_TILE,),
        in_specs=[_tok(D, C_QK // D), _halo_prev(D, C_QK // D, 16), _rep((CONV, D))], out_specs=_tok(D),
        out_shape=jax.ShapeDtypeStruct((t, D), BF16), compiler_params=_params(),
    )(proj, proj, conv_w)


def _conv_silu_bwd_a(proj, conv_w, dqk):
    t = proj.shape[0]

    def body(x_ref, halo_ref, w_ref, d_ref, dc_ref, dw_ref):
        @pl.when(pl.program_id(0) == 0)
        def _():
            dw_ref[...] = jnp.zeros_like(dw_ref)

        halo = jnp.where(pl.program_id(0) > 0, _last8(halo_ref), 0.0)
        x = x_ref[...].astype(F32)
        c = _conv_taps(x, halo, w_ref[...])
        s = _sigmoid(c)
        dc = d_ref[...] * _qscale_row() * (s * (1.0 + c * (1.0 - s)))
        dc_ref[...] = dc
        dcb, xb, halo_b = _bf(dc), _bf(x), _bf(halo)
        for j in range(CONV):
            dw_ref[j:j + 1, :] += jnp.sum(dcb * _shift_down(xb, halo_b, CONV - 1 - j), axis=0, keepdims=True)

    return pl.pallas_call(
        body, name="conv_silu_bwd_a", grid=(t // TOK_TILE,),
        in_specs=[_tok(D, C_QK // D), _halo_prev(D, C_QK // D, 16), _rep((CONV, D)), _tok(D)],
        out_specs=[_tok(D), _rep((CONV, D))],
        out_shape=[jax.ShapeDtypeStruct((t, D), F32), jax.ShapeDtypeStruct((CONV, D), F32)],
        compiler_params=_params(),
    )(proj, proj, conv_w, dqk)


def _conv_silu_bwd_b(dc, conv_w, dproj):
    t = dc.shape[0]
    nt = t // TOK_TILE

    def body(dc_ref, halo_ref, w_ref, _, dx_ref):
        halo = _bf(jnp.where(pl.program_id(0) < nt - 1, halo_ref[...], 0.0))
        dcv = _bf(dc_ref[...])
        w = _bf(w_ref[...])
        acc = dcv * w[CONV - 1:CONV, :]
        for j in range(CONV - 1):
            acc = acc + _shift_up(dcv, halo, CONV - 1 - j) * w[j:j + 1, :]
        dx_ref[...] = acc.astype(BF16)

    return pl.pallas_call(
        body, name="conv_silu_bwd_b", grid=(nt,), in_specs=[_tok(D), _halo_next(D, nt), _rep((CONV, D)), _ANY],
        out_specs=_tok(D, C_QK // D), out_shape=jax.ShapeDtypeStruct((t, NP), BF16),
        input_output_aliases={3: 0}, compiler_params=_params(),
    )(dc, dc, conv_w, dproj)


def _gates_fwd(pre_rows, bias_col):
    t = pre_rows.shape[1]

    def body(p_ref, b_ref, g_ref, s_ref):
        z = p_ref[...] + b_ref[...]
        lf = jnp.minimum(z, 0.0) - jnp.log(1.0 + jnp.exp(-jnp.abs(z)))
        lane = lax.broadcasted_iota(jnp.int32, z.shape, 1) % CHUNK
        cum = lf
        s = 1
        while s < CHUNK:
            cum = cum + jnp.where(lane >= s, pltpu.roll(cum, s, 1), 0.0)
            s *= 2
        sub = lax.broadcasted_iota(jnp.int32, z.shape, 0)
        g_ref[...] = jnp.where(sub < MLH, z, cum)
        s_ref[...] = _sigmoid(-z)

    return pl.pallas_call(
        body, name="gates_fwd",
        out_shape=[jax.ShapeDtypeStruct((8, t), F32), jax.ShapeDtypeStruct((8, t), F32)],
        compiler_params=_params(),
    )(pre_rows, bias_col)


def _chunk_terms(grow, gcol, m0):
    heads = range(MLH)
    i_row = [grow[h:h + 1, :] for h in heads]
    b_row = [grow[MLH + h:MLH + h + 1, :] for h in heads]
    i_col = [gcol[:, h:h + 1] for h in heads]
    b_col = [gcol[:, MLH + h:MLH + h + 1] for h in heads]
    b_last = [b_row[h][:, CHUNK - 1:CHUNK] for h in heads]
    tt = lax.broadcasted_iota(jnp.int32, (CHUNK, CHUNK), 0)
    ss = lax.broadcasted_iota(jnp.int32, (CHUNK, CHUNK), 1)
    log_d = [jnp.where(tt >= ss, b_col[h] - b_row[h] + i_row[h], -jnp.inf) for h in heads]
    row_max = [jnp.max(log_d[h], axis=1, keepdims=True) for h in heads]
    last_max = [jnp.max(b_last[h] - b_row[h] + i_row[h], axis=1, keepdims=True) for h in heads]
    m_t = [jnp.maximum(b_col[h] + m0[h], row_max[h]) for h in heads]
    m1 = [jnp.maximum(b_last[h] + m0[h], last_max[h]) for h in heads]
    dm = [jnp.exp(log_d[h] - m_t[h]) for h in heads]
    wi = [jnp.exp(b_col[h] + m0[h] - m_t[h]) for h in heads]
    ws = [jnp.exp(b_last[h] - b_col[h] + i_col[h] - m1[h]) for h in heads]
    dec = [jnp.exp(b_last[h] + m0[h] - m1[h]) for h in heads]
    return [(dm[h], wi[h], m_t[h], ws[h], dec[h], m1[h]) for h in heads]


def _mlstm_fwd(qk, proj, grow, gcol, gain):
    t = qk.shape[0]
    nc = t // CHUNK

    def body(qk_ref, v_ref, o_ref, grow_ref, gcol_ref, g_ref, h_ref, y_ref, cs_ref, st_ref, c_scr, st_scr):
        @pl.when(pl.program_id(0) == 0)
        def _():
            c_scr[...] = jnp.zeros_like(c_scr)
            st_scr[...] = jnp.zeros_like(st_scr)

        grow_v, gcol_v = grow_ref[...], gcol_ref[...]
        heads = range(MLH)
        q = [qk_ref[:, h * DQK:(h + 1) * DQK] for h in heads]
        k = [qk_ref[:, MLH * DQK + h * DQK:MLH * DQK + (h + 1) * DQK] for h in heads]
        v = [v_ref[:, h * DV:(h + 1) * DV] for h in heads]
        c0 = [c_scr[h] for h in heads]
        n0 = [st_scr[h, 0:1, :] for h in heads]
        for h in heads:
            cs_ref[0, h] = c0[h]
            st_ref[0, h] = st_scr[h]
        terms = _chunk_terms(grow_v, gcol_v, [st_scr[h, 1:2, 0:1] for h in heads])
        a = [_dot_nt(q[h], k[h]) for h in heads]
        qc = [_dot_nt(q[h], c0[h].astype(BF16)) for h in heads]
        s = [a[h] * terms[h][0] for h in heads]
        sv = [_dot_nn(s[h].astype(BF16), v[h]) for h in heads]
        upd = [_dot_tn((terms[h][3] * v[h]).astype(BF16), k[h]) for h in heads]
        den = [terms[h][1] * jnp.sum(q[h].astype(F32) * n0[h], axis=1, keepdims=True)
               + jnp.sum(s[h], axis=1, keepdims=True) for h in heads]
        hv = [(terms[h][1] * qc[h] + sv[h]) / jnp.maximum(jnp.abs(den[h]), jnp.exp(-terms[h][2])) for h in heads]
        for h in heads:
            sl = slice(h * DV, (h + 1) * DV)
            h_ref[:, sl] = hv[h]
            xn, _ = _rms(hv[h])
            y_ref[:, sl] = (_sigmoid(o_ref[:, sl].astype(F32)) * xn * g_ref[:, sl]).astype(BF16)
        for h in heads:
            dec, m1 = terms[h][4], terms[h][5]
            c_scr[h] = dec * c0[h] + upd[h]
            st_scr[h, 0:1, :] = dec * n0[h] + jnp.sum(terms[h][3] * k[h].astype(F32), axis=0, keepdims=True)
            st_scr[h, 1:2, :] = jnp.broadcast_to(m1, (1, DQK))

    return pl.pallas_call(
        body, name="mlstm_fwd", grid=(nc,),
        in_specs=[pl.BlockSpec((CHUNK, D), lambda c: (c, 0)), pl.BlockSpec((CHUNK, D), lambda c: (c, C_V // D)),
                  pl.BlockSpec((CHUNK, D), lambda c: (c, C_O // D)),
                  pl.BlockSpec((8, CHUNK), lambda c: (0, c)), pl.BlockSpec((CHUNK, 8), lambda c: (c, 0)),
                  pl.BlockSpec((1, D), lambda c: (0, 0))],
        out_specs=[pl.BlockSpec((CHUNK, D), lambda c: (c, 0)), pl.BlockSpec((CHUNK, D), lambda c: (c, 0)),
                   pl.BlockSpec((1, MLH, DV, DQK), lambda c: (c, 0, 0, 0)),
                   pl.BlockSpec((1, MLH, 8, DQK), lambda c: (c, 0, 0, 0))],
        out_shape=[jax.ShapeDtypeStruct((t, D), F32), jax.ShapeDtypeStruct((t, D), BF16),
                   jax.ShapeDtypeStruct((nc, MLH, DV, DQK), F32), jax.ShapeDtypeStruct((nc, MLH, 8, DQK), F32)],
        scratch_shapes=[pltpu.VMEM((MLH, DV, DQK), F32), pltpu.VMEM((MLH, 8, DQK), F32)],
        compiler_params=_params(dimension_semantics=("arbitrary",)),
    )(qk, proj, proj, grow, gcol, gain)


def _mlstm_bwd(qk, proj, grow, gcol, sneg_col, cs, st, hraw, dh, dproj):
    t = qk.shape[0]
    nc = t // CHUNK

    def rev(c):
        return nc - 1 - c

    def nxt(c):
        return jnp.minimum(nc - c, nc - 1)

    def body(qk_ref, v_ref, grow_ref, gcol_ref, sneg_ref, cs_ref, st_ref, cs1_ref, st1_ref, h_ref, dh_ref, _,
             dqk_ref, dv_ref, dif_ref, dbif_ref, dc_scr, dn_scr):
        @pl.when(pl.program_id(0) == 0)
        def _():
            dc_scr[...] = jnp.zeros_like(dc_scr)
            dn_scr[...] = jnp.zeros_like(dn_scr)
            dbif_ref[...] = jnp.zeros_like(dbif_ref)

        grow_v, gcol_v, sneg = grow_ref[...], gcol_ref[...], sneg_ref[...]
        tt = lax.broadcasted_iota(jnp.int32, (CHUNK, CHUNK), 0)
        ss = lax.broadcasted_iota(jnp.int32, (CHUNK, CHUNK), 1)
        lane8 = lax.broadcasted_iota(jnp.int32, (CHUNK, 8), 1)
        heads = range(MLH)
        q = [qk_ref[:, h * DQK:(h + 1) * DQK] for h in heads]
        k = [qk_ref[:, MLH * DQK + h * DQK:MLH * DQK + (h + 1) * DQK] for h in heads]
        qf, kf = [a.astype(F32) for a in q], [a.astype(F32) for a in k]
        vb = [v_ref[:, h * DV:(h + 1) * DV].astype(BF16) for h in heads]
        c0 = [cs_ref[0, h] for h in heads]
        n0 = [st_ref[0, h, 0:1, :] for h in heads]
        dc1 = [dc_scr[h] for h in heads]
        dn1 = [dn_scr[h, 0:1, :] for h in heads]
        terms = _chunk_terms(grow_v, gcol_v, [st_ref[0, h, 1:2, 0:1] for h in heads])
        dm, wi, ws = [t[0] for t in terms], [t[1] for t in terms], [t[3] for t in terms]
        s = [_dot_nt(q[h], k[h]) * dm[h] for h in heads]
        den = [wi[h] * jnp.sum(qf[h] * n0[h], axis=1, keepdims=True) + jnp.sum(s[h], axis=1, keepdims=True)
               for h in heads]
        floor = [jnp.exp(-terms[h][2]) for h in heads]
        g = [jnp.maximum(jnp.abs(den[h]), floor[h]) for h in heads]
        dh_v = [dh_ref[:, h * DV:(h + 1) * DV] for h in heads]
        dnum = [dh_v[h] / g[h] for h in heads]
        dden = [-jnp.sum(dh_v[h] * h_ref[:, h * DV:(h + 1) * DV], axis=1, keepdims=True) / g[h] for h in heads]
        dden = [jnp.where(jnp.abs(den[h]) > floor[h], dden[h] * jnp.sign(den[h]), 0.0) for h in heads]
        dnum_b = [a.astype(BF16) for a in dnum]
        dc1_b = [a.astype(BF16) for a in dc1]
        da = [((_dot_nt(dnum_b[h], vb[h]) + dden[h]) * dm[h]).astype(BF16) for h in heads]
        dq_inter = [_dot_nn(dnum_b[h], c0[h].astype(BF16)) for h in heads]
        dk_inter = [_dot_nn(vb[h], dc1_b[h]) for h in heads]
        dv_inter = [_dot_nt(k[h], dc1_b[h]) for h in heads]
        dc_new = [_dot_tn((wi[h] * dnum[h]).astype(BF16), q[h]) for h in heads]
        dq = [_dot_nn(da[h], k[h]) + wi[h] * (dq_inter[h] + dden[h] * n0[h]) for h in heads]
        dk = [_dot_tn(da[h], q[h]) + ws[h] * (dk_inter[h] + dn1[h]) for h in heads]
        dv = [_dot_tn(s[h].astype(BF16), dnum_b[h]) + ws[h] * dv_inter[h] for h in heads]
        for h in heads:
            dqk_ref[:, h * DQK:(h + 1) * DQK] = dq[h]
            dqk_ref[:, MLH * DQK + h * DQK:MLH * DQK + (h + 1) * DQK] = dk[h]
            dv_ref[:, h * DV:(h + 1) * DV] = dv[h].astype(BF16)
        rk = [jnp.sum(kf[h] * dk[h], axis=1, keepdims=True) for h in heads]
        df = [jnp.sum(qf[h] * dq[h], axis=1, keepdims=True) - rk[h] for h in heads]
        df_row = [jnp.sum(jnp.where(tt == ss, df[h], 0.0), axis=0, keepdims=True) for h in heads]
        suffix = [jnp.sum(jnp.where(ss >= tt, df_row[h], 0.0), axis=1, keepdims=True) for h in heads]
        cross = [jnp.sum(jnp.sum(dc1[h] * cs1_ref[0, h], axis=0, keepdims=True), axis=1, keepdims=True)
                 + jnp.sum(dn1[h] * st1_ref[0, h, 0:1, :], axis=1, keepdims=True) for h in heads]
        dif = jnp.zeros((CHUNK, 8), F32)
        for h in heads:
            dpf = (suffix[h] + cross[h]) * sneg[:, MLH + h:MLH + h + 1]
            dif = dif + jnp.where(lane8 == h, rk[h], 0.0) + jnp.where(lane8 == MLH + h, dpf, 0.0)
            dc_scr[h] = terms[h][4] * dc1[h] + dc_new[h]
            dn_scr[h, 0:1, :] = terms[h][4] * dn1[h] + jnp.sum(wi[h] * dden[h] * qf[h], axis=0, keepdims=True)
        dif_ref[...] = dif
        dbif_ref[...] += jnp.sum(dif, axis=0, keepdims=True)

    return pl.pallas_call(
        body, name="mlstm_bwd", grid=(nc,),
        in_specs=[pl.BlockSpec((CHUNK, D), lambda c: (rev(c), 0)),
                  pl.BlockSpec((CHUNK, D), lambda c: (rev(c), C_V // D)),
                  pl.BlockSpec((8, CHUNK), lambda c: (0, rev(c))),
                  pl.BlockSpec((CHUNK, 8), lambda c: (rev(c), 0)),
                  pl.BlockSpec((CHUNK, 8), lambda c: (rev(c), 0)),
                  pl.BlockSpec((1, MLH, DV, DQK), lambda c: (rev(c), 0, 0, 0)),
                  pl.BlockSpec((1, MLH, 8, DQK), lambda c: (rev(c), 0, 0, 0)),
                  pl.BlockSpec((1, MLH, DV, DQK), lambda c: (nxt(c), 0, 0, 0)),
                  pl.BlockSpec((1, MLH, 8, DQK), lambda c: (nxt(c), 0, 0, 0)),
                  pl.BlockSpec((CHUNK, D), lambda c: (rev(c), 0)),
                  pl.BlockSpec((CHUNK, D), lambda c: (rev(c), 0)), _ANY],
        out_specs=[pl.BlockSpec((CHUNK, D), lambda c: (rev(c), 0)),
                   pl.BlockSpec((CHUNK, D), lambda c: (rev(c), C_V // D)),
                   pl.BlockSpec((CHUNK, 8), lambda c: (rev(c), 0)),
                   pl.BlockSpec((1, 8), lambda c: (0, 0))],
        out_shape=[jax.ShapeDtypeStruct((t, D), F32), jax.ShapeDtypeStruct((t, NP), BF16),
                   jax.ShapeDtypeStruct((t, 8), F32), jax.ShapeDtypeStruct((1, 8), F32)],
        scratch_shapes=[pltpu.VMEM((MLH, DV, DQK), F32), pltpu.VMEM((MLH, 8, DQK), F32)],
        input_output_aliases={11: 1}, compiler_params=_params(dimension_semantics=("arbitrary",)),
    )(qk, proj, grow, gcol, sneg_col, cs, st, cs, st, hraw, dh, dproj)


_ANY = pl.BlockSpec(memory_space=pl.ANY)


_SW_SCALE = HD ** -0.5
_KVB = C_KV // (2 * SWKV * HD)


def _swa_mask(n):
    ki = lax.broadcasted_iota(jnp.int32, (2 * WIN, SWG * WIN), 0)
    qi = lax.broadcasted_iota(jnp.int32, (2 * WIN, SWG * WIN), 1) % WIN
    return (ki > qi) & (ki <= qi + WIN) & ((n > 0) | (ki >= WIN))


def _group_rows(x_ref, hk):
    return jnp.concatenate([x_ref[:, (hk * SWG + g) * HD:(hk * SWG + g + 1) * HD] for g in range(SWG)], axis=0)


def _group_lanes(x_ref, hk):
    return jnp.concatenate([x_ref[hk * SWG + g:hk * SWG + g + 1, :] for g in range(SWG)], axis=1)


def _sink_lanes(sink_ref, hk):
    return jnp.concatenate([jnp.broadcast_to(sink_ref[:, hk * SWG + g:hk * SWG + g + 1], (1, WIN))
                            for g in range(SWG)], axis=1)


def _swa_fwd(proj, sinks):
    t = proj.shape[0]
    nb = t // WIN

    def body(q_ref, kvc_ref, kvp_ref, sink_ref, y_ref, lse_ref):
        valid = _swa_mask(pl.program_id(0))
        for hk in range(SWKV):
            ks = slice(hk * HD, (hk + 1) * HD)
            vs = slice(SWKV * HD + hk * HD, SWKV * HD + (hk + 1) * HD)
            kb = jnp.concatenate([kvp_ref[:, ks], kvc_ref[:, ks]], axis=0).astype(BF16)
            vb = jnp.concatenate([kvp_ref[:, vs], kvc_ref[:, vs]], axis=0).astype(BF16)
            q4 = _group_rows(q_ref, hk).astype(BF16)
            sink = _sink_lanes(sink_ref, hk)
            logits = jnp.where(valid, _dot_nt(kb, q4) * _SW_SCALE, -jnp.inf)
            m = jnp.maximum(jnp.max(logits, axis=0, keepdims=True), sink)
            p = jnp.exp(logits - m)
            denom = jnp.sum(p, axis=0, keepdims=True) + jnp.exp(sink - m)
            y4 = _dot_tn((p / denom).astype(BF16), vb).astype(BF16)
            lse4 = m + jnp.log(denom)
            for g in range(SWG):
                hq = hk * SWG + g
                y_ref[:, hq * HD:(hq + 1) * HD] = y4[g * WIN:(g + 1) * WIN]
                lse_ref[hq:hq + 1, :] = lse4[:, g * WIN:(g + 1) * WIN]

    return pl.pallas_call(
        body, name="swa_fwd", grid=(nb,),
        in_specs=[pl.BlockSpec((WIN, D), lambda n: (n, C_QSW // D)),
                  pl.BlockSpec((WIN, 512), lambda n: (n, _KVB)),
                  pl.BlockSpec((WIN, 512), lambda n: (jnp.maximum(n - 1, 0), _KVB)),
                  pl.BlockSpec((1, SWH), lambda n: (0, 0))],
        out_specs=[pl.BlockSpec((WIN, D), lambda n: (n, 0)), pl.BlockSpec((SWH, WIN), lambda n: (0, n))],
        out_shape=[jax.ShapeDtypeStruct((t, D), BF16), jax.ShapeDtypeStruct((SWH, t), F32)],
        compiler_params=_params(),
    )(proj, proj, proj, sinks)


def _swa_bwd(proj, sinks, lse, dyb, dproj):
    t = proj.shape[0]
    nb = t // WIN

    def body(q_ref, kvc_ref, kvp_ref, sink_ref, lse_ref, dy_ref, _, dq_ref, dself_ref, dprev_ref, ds_ref):
        @pl.when(pl.program_id(0) == 0)
        def _():
            ds_ref[...] = jnp.zeros_like(ds_ref)

        valid = _swa_mask(pl.program_id(0))
        kvh = range(SWKV)
        ks = [slice(hk * HD, (hk + 1) * HD) for hk in kvh]
        vs = [slice(SWKV * HD + hk * HD, SWKV * HD + (hk + 1) * HD) for hk in kvh]
        kb = [jnp.concatenate([kvp_ref[:, ks[hk]], kvc_ref[:, ks[hk]]], axis=0).astype(BF16) for hk in kvh]
        vb = [jnp.concatenate([kvp_ref[:, vs[hk]], kvc_ref[:, vs[hk]]], axis=0).astype(BF16) for hk in kvh]
        qb = [_group_rows(q_ref, hk).astype(BF16) for hk in kvh]
        dyb_ = [_group_rows(dy_ref, hk).astype(BF16) for hk in kvh]
        lse4 = [_group_lanes(lse_ref, hk) for hk in kvh]
        logits = [_dot_nt(kb[hk], qb[hk]) for hk in kvh]
        dpt = [_dot_nt(vb[hk], dyb_[hk]) for hk in kvh]
        p = [jnp.exp(jnp.where(valid, logits[hk] * _SW_SCALE, -jnp.inf) - lse4[hk]) for hk in kvh]
        delta = [jnp.sum(p[hk] * dpt[hk], axis=0, keepdims=True) for hk in kvh]
        dsm = [(p[hk] * (dpt[hk] - delta[hk])).astype(BF16) for hk in kvh]
        dq4 = [(_dot_tn(dsm[hk], kb[hk]) * _SW_SCALE).astype(BF16) for hk in kvh]
        dkb = [_dot_nn(dsm[hk], qb[hk]) * _SW_SCALE for hk in kvh]
        dvb = [_dot_nn(p[hk].astype(BF16), dyb_[hk]) for hk in kvh]
        for hk in kvh:
            dsink4 = jnp.exp(_sink_lanes(sink_ref, hk) - lse4[hk]) * delta[hk]
            for g in range(SWG):
                hq = hk * SWG + g
                dq_ref[:, hq * HD:(hq + 1) * HD] = dq4[hk][g * WIN:(g + 1) * WIN]
                ds_ref[:, hq:hq + 1] += -jnp.sum(dsink4[:, g * WIN:(g + 1) * WIN], axis=1, keepdims=True)
            dprev_ref[:, ks[hk]] = dkb[hk][:WIN]
            dself_ref[:, ks[hk]] = dkb[hk][WIN:]
            dprev_ref[:, vs[hk]] = dvb[hk][:WIN]
            dself_ref[:, vs[hk]] = dvb[hk][WIN:]

    return pl.pallas_call(
        body, name="swa_bwd", grid=(nb,),
        in_specs=[pl.BlockSpec((WIN, D), lambda n: (n, C_QSW // D)),
                  pl.BlockSpec((WIN, 512), lambda n: (n, _KVB)),
                  pl.BlockSpec((WIN, 512), lambda n: (jnp.maximum(n - 1, 0), _KVB)),
                  pl.BlockSpec((1, SWH), lambda n: (0, 0)),
                  pl.BlockSpec((SWH, WIN), lambda n: (0, n)),
                  pl.BlockSpec((WIN, D), lambda n: (n, 0)), _ANY],
        out_specs=[pl.BlockSpec((WIN, D), lambda n: (n, C_QSW // D)), pl.BlockSpec((WIN, 512), lambda n: (n, 0)),
                   pl.BlockSpec((WIN, 512), lambda n: (jnp.maximum(n - 1, 0), 0)),
                   pl.BlockSpec((1, SWH), lambda n: (0, 0))],
        out_shape=[jax.ShapeDtypeStruct((t, NP), BF16), jax.ShapeDtypeStruct((t, 512), F32),
                   jax.ShapeDtypeStruct((t, 512), F32), jax.ShapeDtypeStruct((1, SWH), F32)],
        input_output_aliases={6: 0}, compiler_params=_params(),
    )(proj, proj, proj, sinks, lse, dyb, dproj)


def _kv_combine(dself, dnext, dif, dproj):
    t = dself.shape[0]
    rows = _pick(t, 512)

    def body(a_ref, b_ref, dif_ref, _, o_ref):
        row = pl.program_id(0) * rows + lax.broadcasted_iota(jnp.int32, (rows, 1), 0)
        o_ref[:, 0:512] = (a_ref[...] + jnp.where(row < t - WIN, b_ref[...], 0.0)).astype(BF16)
        lane = lax.broadcasted_iota(jnp.int32, (rows, 128), 1)
        dif_v = dif_ref[...]
        first = jnp.zeros((rows, 128), F32)
        for col in range(8):
            first = first + jnp.where(lane == col, dif_v[:, col:col + 1], 0.0)
        o_ref[:, 512:640] = first.astype(BF16)
        o_ref[:, 640:512 + IFW] = jnp.zeros((rows, IFW - 128), BF16)

    return pl.pallas_call(
        body, name="kv_combine", grid=(t // rows,),
        in_specs=[pl.BlockSpec((rows, 512), lambda n: (n, 0)), pl.BlockSpec((rows, 512), lambda n: (n, 0)),
                  pl.BlockSpec((rows, 8), lambda n: (n, 0)), _ANY],
        out_specs=pl.BlockSpec((rows, 512 + IFW), lambda n: (n, C_KV // (512 + IFW))),
        out_shape=jax.ShapeDtypeStruct((t, NP), BF16), input_output_aliases={3: 0}, compiler_params=_params(),
    )(dself, dnext, dif, dproj)


def _sds(t, n, dtype):
    return jax.ShapeDtypeStruct((t, n), dtype)


def _proj_in(x, gain, w_in):
    t = x.shape[0]
    tm, tn = _pick(t, 1024), 2 * IFW

    def body(x_ref, g_ref, w_ref, h_ref, p_ref, gate_ref, h_scr):
        j = pl.program_id(1)

        @pl.when(j == 0)
        def _():
            xn, _ = _rms(x_ref[...])
            h = (xn * g_ref[...]).astype(BF16)
            h_scr[...] = h
            h_ref[...] = h

        acc = _dot_nn(h_scr[...], w_ref[...])
        p_ref[...] = acc.astype(BF16)

        @pl.when(j == C_IF // tn)
        def _():
            gate_ref[...] = acc[:, C_IF % tn:C_IF % tn + 128]

    return pl.pallas_call(
        body, name="mm_in", grid=(t // tm, NP // tn),
        in_specs=[pl.BlockSpec((tm, D), lambda i, j: (i, 0)), pl.BlockSpec((1, D), lambda i, j: (0, 0)),
                  pl.BlockSpec((D, tn), lambda i, j: (0, j))],
        out_specs=[pl.BlockSpec((tm, D), lambda i, j: (i, 0)), pl.BlockSpec((tm, tn), lambda i, j: (i, j)),
                   pl.BlockSpec((tm, 128), lambda i, j: (i, 0))],
        out_shape=[_sds(t, D, BF16), _sds(t, NP, BF16), _sds(t, 128, F32)],
        scratch_shapes=[pltpu.VMEM((tm, D), BF16)],
        compiler_params=_params(dimension_semantics=("arbitrary", "arbitrary")),
    )(x, gain, w_in)


def _branch_merge(ya, yb, wa, wb, proj):
    t = ya.shape[0]

    def epilogue(accs, ins, outs, i, j):
        za, zb = accs
        merged = _sigmoid(ins[0][...].astype(F32)) * za + _sigmoid(ins[1][...].astype(F32)) * zb
        outs[0][...] = merged.astype(BF16)
        outs[1][...] = za.astype(BF16)
        outs[2][...] = zb.astype(BF16)

    return _mm_ep([(ya, wa), (yb, wb)], "nn", "mm_branch_merge", epilogue, [(proj, _tile(C_GA)), (proj, _tile(C_GB))],
                  [(_sds(t, D, BF16), _tile())] * 3, 1024, 1024)


def _dmerged_bwd(dxb, w_out, proj, za, zb):
    t = dxb.shape[0]

    def epilogue(accs, ins, outs, i, j):
        dm = accs[0]
        sa, sb = _sigmoid(ins[0][...].astype(F32)), _sigmoid(ins[1][...].astype(F32))
        outs[0][...] = (dm * sa).astype(BF16)
        outs[1][...] = (dm * sb).astype(BF16)
        outs[2][:, 0:D] = (dm * ins[2][...].astype(F32) * sa * (1.0 - sa)).astype(BF16)
        outs[2][:, D:2 * D] = (dm * ins[3][...].astype(F32) * sb * (1.0 - sb)).astype(BF16)

    gate_cols = lambda tm, tn: pl.BlockSpec((tm, 2 * D), lambda i, j, kk: (i, C_GA // (2 * D)))
    return _mm_ep([(dxb, w_out)], "nt", "mm_dmerged_bwd", epilogue,
                  [(proj, _tile(C_GA)), (proj, _tile(C_GB)), (za, _tile()), (zb, _tile())],
                  [(_sds(t, D, BF16), _tile()), (_sds(t, D, BF16), _tile()), (_sds(t, NP, BF16), gate_cols)], 1024, D)


def _dya_bwd(dza, wa, hraw, proj, g, dproj):
    t = dza.shape[0]

    def epilogue(accs, ins, outs, i, j):
        h_ref, o_ref, g_ref, _ = ins
        dh_ref, do_ref, dg_ref = outs

        @pl.when(i == 0)
        def _():
            dg_ref[...] = jnp.zeros_like(dg_ref)

        dy = accs[0]
        so = _sigmoid(o_ref[...].astype(F32))
        for h in range(MLH):
            sl = slice(h * DV, (h + 1) * DV)
            xn, rstd = _rms(h_ref[:, sl])
            gs = g_ref[:, sl]
            do_ref[:, sl] = (dy[:, sl] * xn * gs * so[:, sl] * (1.0 - so[:, sl])).astype(BF16)
            dhn = dy[:, sl] * so[:, sl]
            dg_ref[:, sl] += jnp.sum(dhn * xn, axis=0, keepdims=True)
            dh_ref[:, sl] = _rms_bwd(xn, rstd, dhn * gs)

    return _mm_ep([(dza, wa)], "nt", "mm_dya_bwd", epilogue,
                  [(hraw, _tile()), (proj, _tile(C_O)), (g, _row()), (dproj, lambda tm, tn: _ANY)],
                  [(_sds(t, D, F32), _tile()), (_sds(t, NP, BF16), _tile(C_O)), (_sds(1, D, F32), _row())],
                  512, D, aliases={3: 1})


def _up_act(hn, w_up):
    t = hn.shape[0]

    def epilogue(accs, ins, outs, i, j):
        r = jnp.maximum(accs[0], 0.0)
        outs[0][...] = (r * r).astype(BF16)
        outs[1][...] = accs[0].astype(BF16)

    return _mm_ep([(hn, w_up)], "nn", "mm_up_act", epilogue, [],
                  [(_sds(t, DFF, BF16), _tile()), (_sds(t, DFF, BF16), _tile())], 1024, 1024)


def _da_du(dxb, w_down, u):
    t = dxb.shape[0]

    def epilogue(accs, ins, outs, i, j):
        outs[0][...] = (accs[0] * 2.0 * jnp.maximum(ins[0][...].astype(F32), 0.0)).astype(BF16)

    return _mm_ep([(dxb, w_down)], "nt", "mm_da_du", epilogue, [(u, _tile())], [(_sds(t, DFF, BF16), _tile())],
                  1024, 1024)[0]


def _resid_norm_mm(a, w, x, g, name):
    t = x.shape[0]

    def epilogue(accs, ins, outs, i, j):
        x1 = ins[0][...] + accs[0]
        outs[0][...] = x1
        xn, _ = _rms(x1)
        outs[1][...] = (xn * ins[1][...]).astype(BF16)

    return _mm_ep([(a, w)], "nn", name, epilogue, [(x, _tile()), (g, _row())],
                  [(_sds(t, D, F32), _tile()), (_sds(t, D, BF16), _tile())], 1024, D)


def _norm_bwd_mm(dy, w, x, g, dres, name):
    t = x.shape[0]

    def epilogue(accs, ins, outs, i, j):
        @pl.when(i == 0)
        def _():
            outs[2][...] = jnp.zeros_like(outs[2])

        dh = accs[0]
        xn, rstd = _rms(ins[0][...])
        outs[2][...] += jnp.sum(dh * xn, axis=0, keepdims=True)
        dx = ins[2][...] + _rms_bwd(xn, rstd, dh * ins[1][...])
        outs[0][...] = dx
        outs[1][...] = dx.astype(BF16)

    return _mm_ep([(dy, w)], "nt", name, epilogue, [(x, _tile()), (g, _row()), (dres, _tile())],
                  [(_sds(t, D, F32), _tile()), (_sds(t, D, BF16), _tile()), (_sds(1, D, F32), _row())], 1024, D)


def _ple_final_mm(hn2, w_gate, x2, pp, target, gf):
    t = x2.shape[0]

    def epilogue(accs, ins, outs, i, j):
        loss_ref, dg_ref, dx_ref, dpp_ref, dgp_ref = outs

        @pl.when(i == 0)
        def _():
            loss_ref[...] = jnp.zeros_like(loss_ref)
            dg_ref[...] = jnp.zeros_like(dg_ref)

        gate = _sigmoid(accs[0])
        pp_v = ins[1][...]
        x3 = ins[0][...] + gate * pp_v
        xn, rstd = _rms(x3)
        gf_v = ins[3][...]
        err = xn * gf_v - ins[2][...]
        loss_ref[...] += (0.5 / D) * jnp.sum(jnp.sum(err * err, axis=1, keepdims=True), axis=0, keepdims=True)
        dy = err * (1.0 / D)
        dg_ref[...] += jnp.sum(dy * xn, axis=0, keepdims=True)
        dx3 = _rms_bwd(xn, rstd, dy * gf_v)
        dx_ref[...] = dx3
        dpp_ref[...] = (dx3 * gate).astype(BF16)
        dgp_ref[...] = (dx3 * pp_v * gate * (1.0 - gate)).astype(BF16)

    one = lambda tm, tn: pl.BlockSpec((1, 1), lambda i, j, kk: (0, 0))
    return _mm_ep([(hn2, w_gate)], "nn", "mm_ple_final", epilogue,
                  [(x2, _tile()), (pp, _tile()), (target, _tile()), (gf, _row())],
                  [(_sds(1, 1, F32), one), (_sds(1, D, F32), _row()), (_sds(t, D, F32), _tile()),
                   (_sds(t, D, BF16), _tile()), (_sds(t, D, BF16), _tile())], 512, D)


_WIN_SEGMENTS = ((0, 3072, C_QK), (3072, 8, C_IF), (3080, 1024, C_QSW), (4104, 256, C_KV), (4360, 256, C_KV + 256),
                 (4616, 1024, C_GA), (5640, 1024, C_GB))
_WIN_SHARD = N_IN // 4


def _win_pieces():
    out = []
    for src, width, dst in _WIN_SEGMENTS:
        while width:
            chip, col = divmod(src, _WIN_SHARD)
            n = min(width, _WIN_SHARD - col)
            out.append((chip, col, n, dst))
            src, dst, width = src + n, dst + n, width - n
    return out


def _win_pad(shards):
    rows = shards.shape[1]
    tr = _pick(rows, 256)

    def body(s_ref, o_ref):
        for chip, col, n, dst in _win_pieces():
            o_ref[:, dst:dst + n] = s_ref[chip, :, col:col + n]
        o_ref[:, C_IF + 8:NP] = jnp.zeros((tr, NP - C_IF - 8), shards.dtype)

    return pl.pallas_call(
        body, name="win_pad", grid=(rows // tr,), in_specs=[pl.BlockSpec((4, tr, _WIN_SHARD), lambda i: (0, i, 0))],
        out_specs=pl.BlockSpec((tr, NP), lambda i: (i, 0)), out_shape=jax.ShapeDtypeStruct((rows, NP), shards.dtype),
        compiler_params=_params(),
    )(shards)


def _win_unpad(wp):
    rows = wp.shape[0]
    tr = _pick(rows, 256)

    def body(p_ref, o_ref):
        for chip, col, n, dst in _win_pieces():
            o_ref[chip, :, col:col + n] = p_ref[:, dst:dst + n]

    return pl.pallas_call(
        body, name="win_unpad", grid=(rows // tr,), in_specs=[pl.BlockSpec((tr, NP), lambda i: (i, 0))],
        out_specs=pl.BlockSpec((4, tr, _WIN_SHARD), lambda i: (0, i, 0)),
        out_shape=jax.ShapeDtypeStruct((4, rows, _WIN_SHARD), wp.dtype), compiler_params=_params(),
    )(wp)


def _local_step(x, p, target, w, late_weights=None, early_grads=None, mid_grads=None, last_grad=None):
    t = x.shape[0]
    pb = p.astype(BF16)
    w = dict(w)

    h0, proj, gates = _proj_in(x, w["norm_mix_g"], w["w_in"])
    qk = _conv_silu_fwd(proj, w["conv_qk"])
    grow, sneg_row = _gates_fwd(gates[:, 0:8].T, w["b_if"].reshape(8, 1))
    gcol, sneg_col = grow.T, sneg_row.T
    hraw, ya, cs, st = _mlstm_fwd(qk, proj, grow, gcol, w["mlstm_norm_g"])
    yb, lse = _swa_fwd(proj, w["sinks"])
    if late_weights is not None:
        w.update(late_weights(yb))
    merged, za, zb = _branch_merge(ya, yb, w["w_branch_a"], w["w_branch_b"], proj)
    x1, hn1 = _resid_norm_mm(merged, w["w_out"], x, w["norm_mlp_g"], "mm_out_norm")
    act, u = _up_act(hn1, w["w_up"])
    x2, hn2 = _resid_norm_mm(act, w["w_down"], x1, w["norm_ple_g"], "mm_down_norm")
    pp = _mm(pb, w["w_ple_proj"], "nn", F32, "mm_ple_proj")
    loss, d_final_g, dx3, dpp, dgpre = _ple_final_mm(hn2, w["w_ple_gate"], x2, pp, target, w["final_norm_g"])

    g = {"final_norm_g": d_final_g}
    g["w_ple_proj"] = _mm(pb, dpp, "tn", F32, "mm_d_ple_proj", out_chunks=4)
    g["w_ple_gate"] = _mm(hn2, dgpre, "tn", F32, "mm_d_ple_gate")
    dx2, dx2b, g["norm_ple_g"] = _norm_bwd_mm(dgpre, w["w_ple_gate"], x2, w["norm_ple_g"], dx3, "mm_dhn2_norm")
    g["w_down"] = _mm(act, dx2b, "tn", F32, "mm_d_down")
    du = _da_du(dx2b, w["w_down"], u)
    g["w_up"] = _mm(hn1, du, "tn", F32, "mm_d_up", out_chunks=4)
    dx1, dx1b, g["norm_mlp_g"] = _norm_bwd_mm(du, w["w_up"], x1, w["norm_mlp_g"], dx2, "mm_dhn1_norm")
    g["w_out"] = _mm(merged, dx1b, "tn", F32, "mm_d_out")
    dza, dzb, dproj = _dmerged_bwd(dx1b, w["w_out"], proj, za, zb)
    g["w_branch_a"] = _mm(ya, dza, "tn", F32, "mm_d_branch_a")
    g["w_branch_b"] = _mm(yb, dzb, "tn", F32, "mm_d_branch_b")
    gain = w["mlstm_norm_g"] if early_grads is None else w["mlstm_norm_g"] + early_grads(g)
    dyb = _mm(dzb, w["w_branch_b"], "nt", F32, "mm_dyb")
    dhraw, dproj, g["mlstm_norm_g"] = _dya_bwd(dza, w["w_branch_a"], hraw, proj, gain, dproj)
    if mid_grads is not None:
        sneg_col = sneg_col + mid_grads(dhraw)
    dqk, dproj, dif, g["b_if"] = _mlstm_bwd(qk, proj, grow, gcol, sneg_col, cs, st, hraw, dhraw, dproj)
    dc, g["conv_qk"] = _conv_silu_bwd_a(proj, w["conv_qk"], dqk)
    dproj = _conv_silu_bwd_b(dc, w["conv_qk"], dproj)
    dproj, dkv_self, dkv_prev, g["sinks"] = _swa_bwd(proj, w["sinks"], lse, dyb, dproj)
    dproj = _kv_combine(dkv_self, dkv_prev, dif, dproj)
    g["w_in"] = _mm(h0, dproj, "tn", F32, "mm_d_in")
    gain = w["norm_mix_g"] if last_grad is None else w["norm_mix_g"] + last_grad(g)
    grad_x, _, g["norm_mix_g"] = _norm_bwd_mm(dproj, w["w_in"], x, gain, dx1, "mm_dh0_norm")
    return loss, grad_x, g


_W4 = ("w_branch_a", "w_branch_b", "w_out", "w_ple_gate")
_SHARDED_NAMES = ("w_in", "w_up", "w_down", "w_ple_proj", "conv_qk") + _W4
_SMALL_ROWS = 16
_CONV_ROW = 8


def _group(s):
    return [s["w_in"], jnp.concatenate([s[n] for n in _W4], axis=0), s["w_up"], s["w_down"], s["w_ple_proj"]]


def _ungroup(arrs):
    out = {"w_in": arrs[0], "w_up": arrs[2], "w_down": arrs[3], "w_ple_proj": arrs[4]}
    rows = arrs[1].shape[0] // len(_W4)
    for i, n in enumerate(_W4):
        out[n] = arrs[1][i * rows:(i + 1) * rows]
    return out


def _rows_tile(rows):
    return 256 if rows % 256 == 0 else rows


_SMALL = ("norm_mix_g", "mlstm_norm_g", "norm_mlp_g", "norm_ple_g", "final_norm_g")


def _pack_small(vals, extra=None, conv=None):
    rows = [vals[n].reshape(1, D) for n in _SMALL]
    tail = [vals["b_if"].reshape(1, 8), vals["sinks"].reshape(1, SWH)]
    used = 8 + SWH
    if extra is not None:
        tail.append(extra.reshape(1, 1))
        used += 1
    tail.append(jnp.zeros((1, D - used), F32))
    rows.append(jnp.concatenate(tail, axis=1))
    rows.append(jnp.zeros((_CONV_ROW - len(rows), D), F32))
    rows.append(jnp.zeros((CONV, D), F32) if conv is None else conv)
    rows.append(jnp.zeros((_SMALL_ROWS - _CONV_ROW - CONV, D), F32))
    return jnp.concatenate(rows, axis=0)


def _unpack_small(slab, shapes):
    out = {n: slab[i].reshape(shapes[n]) for i, n in enumerate(_SMALL)}
    out["b_if"] = slab[5, 0:8].reshape(shapes["b_if"])
    out["sinks"] = slab[5, 8:8 + SWH].reshape(shapes["sinks"])
    return out


_MESH = pl.DeviceIdType.MESH
_HBM = pl.BlockSpec(memory_space=pltpu.HBM)
_VMEM = pl.BlockSpec(memory_space=pltpu.VMEM)


def _place():
    x, y, c = lax.axis_index("x"), lax.axis_index("y"), lax.axis_index("c")
    return x, y, c, 2 * x + y


def _chip_peer(x, y, r):
    return (x ^ (r >> 1), y ^ (r & 1))


def _half(ref, which):
    h = ref.shape[-2] // 2
    return pl.ds(which * h, h)


def _allgather_weights(shards, conv):
    n = len(shards)

    def body(*refs):
        ins, conv_ref = refs[:n], refs[n]
        outs, conv_out = refs[n + 1:2 * n + 1], refs[2 * n + 1]
        send_a, recv_a, send_b, recv_b, send_c, recv_c, local_sems = refs[2 * n + 2:]
        x, y, c, j = _place()
        sibling = (x, y, 1 - c)
        local = [pltpu.make_async_copy(ins[k], outs[k].at[j], local_sems.at[k]) for k in range(n)]
        local.append(pltpu.make_async_copy(conv_ref, conv_out.at[j], local_sems.at[n]))
        for cp in local:
            cp.start()

        def copy_a(k, r, chip):
            rows = _half(ins[k], c)
            return pltpu.make_async_remote_copy(
                src_ref=ins[k].at[rows], dst_ref=outs[k].at[chip, rows], send_sem=send_a.at[3 * k + r - 1],
                recv_sem=recv_a.at[3 * k + r - 1], device_id=(*_chip_peer(x, y, r), c), device_id_type=_MESH)

        def copy_b(k, r, chip, which):
            rows = _half(ins[k], which)
            return pltpu.make_async_remote_copy(
                src_ref=outs[k].at[chip, rows], dst_ref=outs[k].at[chip, rows], send_sem=send_b.at[3 * k + r - 1],
                recv_sem=recv_b.at[3 * k + r - 1], device_id=sibling, device_id_type=_MESH)

        def copy_c(r, chip):
            return pltpu.make_async_remote_copy(
                src_ref=conv_ref, dst_ref=conv_out.at[chip], send_sem=send_c.at[r - 1],
                recv_sem=recv_c.at[r - 1], device_id=(*_chip_peer(x, y, r), c), device_id_type=_MESH)

        for k in range(n):
            for r in (1, 2, 3):
                copy_a(k, r, j).start()
        for r in (1, 2, 3):
            copy_c(r, j).start()
        for k in range(n):
            for r in (1, 2, 3):
                copy_a(k, r, j ^ r).wait_recv()
                copy_b(k, r, j ^ r, c).start()
        for k in range(n):
            for r in (1, 2, 3):
                copy_b(k, r, j ^ r, 1 - c).wait_recv()
        for r in (1, 2, 3):
            copy_c(r, j ^ r).wait_recv()
        for k in range(n):
            for r in (1, 2, 3):
                copy_a(k, r, j).wait_send()
                copy_b(k, r, j ^ r, c).wait_send()
        for r in (1, 2, 3):
            copy_c(r, j).wait_send()
        for cp in local:
            cp.wait()

    return pl.pallas_call(
        body, name="allgather_weights",
        out_shape=[jax.ShapeDtypeStruct((4,) + s.shape, s.dtype) for s in shards]
        + [jax.ShapeDtypeStruct((4,) + conv.shape, F32)],
        in_specs=[_HBM] * (n + 1), out_specs=[_HBM] * (n + 1),
        scratch_shapes=[pltpu.SemaphoreType.DMA((3 * n,))] * 4 + [pltpu.SemaphoreType.DMA((3,))] * 2
        + [pltpu.SemaphoreType.DMA((n + 1,))],
    )(*shards, conv)


_SEM = pl.BlockSpec(memory_space=pltpu.SEMAPHORE)
_DATAFLOW = pltpu.SideEffectType.DATAFLOW_SIDE_EFFECTING


def _late_peer_copy(src_ref, land_ref, send_sems, recv_sems, x, y, c, j, r, chip):
    return pltpu.make_async_remote_copy(
        src_ref=src_ref, dst_ref=land_ref.at[chip], send_sem=send_sems.at[r - 1], recv_sem=recv_sems.at[r - 1],
        device_id=(*_chip_peer(x, y, r), c), device_id_type=_MESH)


def _late_gather_start(rest):
    def body(rest_ref, land_ref, send_sems, recv_sems, rest_thru, land_thru, token):
        x, y, c, j = _place()
        for r in (1, 2, 3):
            _late_peer_copy(rest_ref, land_ref, send_sems, recv_sems, x, y, c, j, r, j).start()
        token[...] = jnp.zeros_like(token)

    j = 2 * lax.axis_index("x") + lax.axis_index("y")
    land = lax.dynamic_update_slice(lax.empty((4,) + rest.shape, rest.dtype), rest[None], (j, 0, 0))
    return pl.pallas_call(
        body, name="late_gather_start",
        out_shape=(pltpu.SemaphoreType.DMA((3,)), pltpu.SemaphoreType.DMA((3,)), pltpu.HBM(rest.shape, rest.dtype),
                   pltpu.HBM(land.shape, land.dtype), jax.ShapeDtypeStruct((8, 128), F32)),
        in_specs=(_HBM, _HBM), out_specs=(_SEM, _SEM, _HBM, _HBM, _VMEM), input_output_aliases={0: 2, 1: 3},
        compiler_params=pltpu.CompilerParams(has_side_effects=_DATAFLOW),
    )(pltpu.with_memory_space_constraint(rest, pltpu.HBM), pltpu.with_memory_space_constraint(land, pltpu.HBM))


def _late_gather_wait(send_sems, recv_sems, rest_thru, land_thru, after):
    def body(rest_ref, land_ref, send_sems, recv_sems, after_ref, rest_dead, got_ref):
        x, y, c, j = _place()
        for r in (1, 2, 3):
            cp = _late_peer_copy(rest_ref, land_ref, send_sems, recv_sems, x, y, c, j, r, j ^ r)
            cp.wait_send()
            cp.wait_recv()

    return pl.pallas_call(
        body, name="late_gather_wait",
        out_shape=(pltpu.HBM(rest_thru.shape, rest_thru.dtype), pltpu.HBM(land_thru.shape, land_thru.dtype)),
        in_specs=(_HBM, _HBM, _SEM, _SEM, _ANY), out_specs=(_HBM, _HBM), input_output_aliases={0: 0, 1: 1},
        compiler_params=pltpu.CompilerParams(has_side_effects=_DATAFLOW),
    )(rest_thru, land_thru, send_sems, recv_sems, after)[1]


def _pair_sum(g, theirs, j, c, name):
    _, h, cols = theirs.shape
    tr = _rows_tile(h)
    nb = h // tr

    def body(idx_ref, a_ref, b_ref, own_ref, ob_ref):
        s = a_ref[0] + b_ref[0]
        ob_ref[0] = s.astype(BF16)

        @pl.when(pl.program_id(1) == idx_ref[0])
        def _():
            own_ref[...] = s

    blk = pl.BlockSpec((1, tr, cols), lambda i, k, idx_ref: (k, i, 0))
    return pl.pallas_call(
        body, name=name,
        grid_spec=pltpu.PrefetchScalarGridSpec(
            num_scalar_prefetch=1, grid=(nb, 4),
            in_specs=[pl.BlockSpec((1, tr, cols), lambda i, k, idx_ref: (k, idx_ref[1] * nb + i, 0)), blk],
            out_specs=[pl.BlockSpec((tr, cols), lambda i, k, idx_ref: (i, 0)), blk]),
        out_shape=[jax.ShapeDtypeStruct((h, cols), F32), jax.ShapeDtypeStruct(theirs.shape, BF16)],
        compiler_params=_params(),
    )(jnp.stack([j, c]).astype(jnp.int32), g, theirs)


def _chip_copies(srcs, lands, send_sems, recv_sems):
    x, y, c, j = _place()
    return [pltpu.make_async_remote_copy(
        src_ref=srcs[k].at[j ^ r], dst_ref=lands[k].at[r - 1], send_sem=send_sems.at[3 * k + r - 1],
        recv_sem=recv_sems.at[3 * k + r - 1], device_id=(*_chip_peer(x, y, r), c), device_id_type=_MESH)
        for k in range(len(srcs)) for r in (1, 2, 3)]


def _pair_copies(srcs, lands, send_sems, recv_sems):
    x, y, c, _ = _place()
    return [pltpu.make_async_remote_copy(
        src_ref=srcs[k].at[:, _half(srcs[k], 1 - c)], dst_ref=lands[k], send_sem=send_sems.at[k],
        recv_sem=recv_sems.at[k], device_id=(x, y, 1 - c), device_id_type=_MESH) for k in range(len(srcs))]


def _split_start(name, srcs, lands, copies, n_sems):
    n = len(srcs)

    def body(*refs):
        for cp in copies(refs[:n], refs[n:2 * n], refs[2 * n], refs[2 * n + 1]):
            cp.start()
        refs[-1][...] = jnp.zeros_like(refs[-1])

    arrays = list(srcs) + list(lands)
    out = pl.pallas_call(
        body, name=name,
        out_shape=(pltpu.SemaphoreType.DMA((n_sems,)), pltpu.SemaphoreType.DMA((n_sems,)),
                   *[pltpu.HBM(a.shape, a.dtype) for a in arrays], jax.ShapeDtypeStruct((8, 128), F32)),
        in_specs=[_HBM] * (2 * n), out_specs=(_SEM, _SEM, *([_HBM] * (2 * n)), _VMEM),
        input_output_aliases={k: 2 + k for k in range(2 * n)},
        compiler_params=pltpu.CompilerParams(has_side_effects=_DATAFLOW),
    )(*[pltpu.with_memory_space_constraint(a, pltpu.HBM) for a in arrays])
    return out[0], out[1], list(out[2:2 + n]), list(out[2 + n:2 + 2 * n]), out[-1]


def _split_wait(name, send_sems, recv_sems, srcs_thru, lands_thru, after, copies):
    n = len(srcs_thru)

    def body(*refs):
        for cp in copies(refs[:n], refs[n:2 * n], refs[2 * n], refs[2 * n + 1]):
            cp.wait_send()
            cp.wait_recv()

    arrays = list(srcs_thru) + list(lands_thru)
    out = pl.pallas_call(
        body, name=name, out_shape=tuple(pltpu.HBM(a.shape, a.dtype) for a in arrays),
        in_specs=[_HBM] * (2 * n) + [_SEM, _SEM, _ANY], out_specs=tuple([_HBM] * (2 * n)),
        input_output_aliases={k: k for k in range(2 * n)},
        compiler_params=pltpu.CompilerParams(has_side_effects=_DATAFLOW),
    )(*arrays, send_sems, recv_sems, after)
    return list(out[:n]), list(out[n:])


def _chip_exchange_start(ss, tag):
    lands = [lax.empty((3,) + s.shape[1:], s.dtype) for s in ss]
    return _split_start("chip_exchange_start_" + tag, ss, lands, _chip_copies, 3 * len(ss))


def _chip_exchange_wait(send_sems, recv_sems, ss_thru, lands_thru, after, tag):
    return _split_wait("chip_exchange_wait_" + tag, send_sems, recv_sems, ss_thru, lands_thru, after, _chip_copies)[1]


def _pair_exchange_start(gs, tag):
    lands = [lax.empty((4, g.shape[1] // 2, g.shape[2]), g.dtype) for g in gs]
    return _split_start("pair_exchange_start_" + tag, gs, lands, _pair_copies, len(gs))


def _pair_exchange_wait(send_sems, recv_sems, gs_thru, lands_thru, after, tag):
    return _split_wait("pair_exchange_wait_" + tag, send_sems, recv_sems, gs_thru, lands_thru, after, _pair_copies)


def _reduce4(own, others, c, name):
    h, cols = own.shape
    tr = _rows_tile(h)
    nb = h // tr

    def body(c_ref, s_ref, a0, a1, a2, o_ref):
        o_ref[...] = ((s_ref[...] + a0[0].astype(F32)) + a1[0].astype(F32)) + a2[0].astype(F32)

    def other(r):
        return pl.BlockSpec((1, tr, cols), lambda i, c_ref: (r, i, 0))

    return pl.pallas_call(
        body, name=name,
        grid_spec=pltpu.PrefetchScalarGridSpec(
            num_scalar_prefetch=1, grid=(nb,),
            in_specs=[pl.BlockSpec((tr, cols), lambda i, c_ref: (i, 0)), other(0), other(1), other(2)],
            out_specs=pl.BlockSpec((tr, cols), lambda i, c_ref: (c_ref[0] * nb + i, 0))),
        out_shape=jax.ShapeDtypeStruct((2 * h, cols), F32), compiler_params=_params(),
    )(c.reshape(1).astype(jnp.int32), own, others, others, others)


def _sibling_share(fulls, name):
    n = len(fulls)

    def body(*refs):
        outs, send_sems, recv_sems = refs[n:2 * n], refs[2 * n], refs[2 * n + 1]
        x, y, c, _ = _place()
        cps = [pltpu.make_async_remote_copy(
            src_ref=outs[k].at[_half(outs[k], c)], dst_ref=outs[k].at[_half(outs[k], c)], send_sem=send_sems.at[k],
            recv_sem=recv_sems.at[k], device_id=(x, y, 1 - c), device_id_type=_MESH) for k in range(n)]
        for cp in cps:
            cp.start()
        for cp in cps:
            cp.wait()

    return pl.pallas_call(
        body, name=name, out_shape=[jax.ShapeDtypeStruct(f.shape, F32) for f in fulls],
        in_specs=[_HBM] * n, out_specs=[_HBM] * n, input_output_aliases={k: k for k in range(n)},
        scratch_shapes=[pltpu.SemaphoreType.DMA((n,))] * 2,
    )(*fulls)


def _adamw(w, g, m, v):
    m1 = ADAM_B1 * m + (1.0 - ADAM_B1) * g
    v1 = ADAM_B2 * v + (1.0 - ADAM_B2) * (g * g)
    m_hat = m1 / (1.0 - ADAM_B1 ** ADAM_STEP)
    v_hat = v1 / (1.0 - ADAM_B2 ** ADAM_STEP)
    delta = -ADAM_LR * (m_hat / (jnp.sqrt(v_hat) + ADAM_EPS) + ADAM_WD * w)
    return delta, m1, v1


def _adamw_call(w, g, m, v, name):
    rows, cols = w.shape

    def body(w_ref, g_ref, m_ref, v_ref, d_out, m_out, v_out):
        delta, m1, v1 = _adamw(w_ref[...], g_ref[...], m_ref[...], v_ref[...])
        d_out[...] = delta
        m_out[...] = m1
        v_out[...] = v1

    if rows % 8 == 0:
        tr = _rows_tile(rows)
        blk, grid = pl.BlockSpec((tr, cols), lambda i: (i, 0)), (rows // tr,)
    else:
        blk, grid = pl.BlockSpec((rows, 128), lambda i: (0, i)), (cols // 128,)
    return pl.pallas_call(
        body, name=name, grid=grid, in_specs=[blk] * 4, out_specs=[blk] * 3,
        out_shape=[jax.ShapeDtypeStruct((rows, cols), F32)] * 3, compiler_params=_params(),
    )(w, g, m, v)


def _small_allreduce(vals):
    def body(v_ref, out_ref, buf, send_sems, recv_sems):
        x, y, c, j = _place()
        me = 2 * j + c
        buf[0] = v_ref[...]

        def copy(r):
            return pltpu.make_async_remote_copy(
                src_ref=v_ref, dst_ref=buf.at[r], send_sem=send_sems.at[r - 1], recv_sem=recv_sems.at[r - 1],
                device_id=(x ^ (r >> 2), y ^ ((r >> 1) & 1), c ^ (r & 1)), device_id_type=_MESH)

        for r in range(1, 8):
            copy(r).start()
        for r in range(1, 8):
            copy(r).wait()
        acc = buf[me ^ 0]
        for d in range(1, 8):
            acc = acc + buf[me ^ d]
        out_ref[...] = acc

    return pl.pallas_call(
        body, name="small_allreduce", out_shape=jax.ShapeDtypeStruct((_SMALL_ROWS, D), F32),
        in_specs=[_VMEM], out_specs=_VMEM,
        scratch_shapes=[pltpu.VMEM((8, _SMALL_ROWS, D), F32), pltpu.SemaphoreType.DMA((7,)),
                        pltpu.SemaphoreType.DMA((7,))],
    )(vals)


_NAMES = ("norm_mix_g", "w_in", "conv_qk", "b_if", "mlstm_norm_g", "sinks", "w_branch_a", "w_branch_b", "w_out",
          "norm_mlp_g", "w_up", "w_down", "norm_ple_g", "w_ple_gate", "w_ple_proj", "final_norm_g")
_GROUP_NAMES = ("w_in", "w4", "w_up", "w_down", "w_ple_proj")


def _step(x, p, target, w, m, v):
    c = lax.axis_index("c")
    j = 2 * lax.axis_index("x") + lax.axis_index("y")

    def shards(d):
        return {n: d[n][0] for n in _SHARDED_NAMES}

    ws = shards(w)
    w_in_all, conv_all = _allgather_weights([ws["w_in"].astype(BF16)], ws["conv_qk"])
    rows_pp = PLE * (D // 4) // D
    rest = jnp.concatenate([ws[n] for n in _W4] + [ws["w_up"], ws["w_down"], ws["w_ple_proj"].reshape(rows_pp, D)],
                           axis=0)
    rest = (rest + 0.0 * conv_all[0, 0, 0]).astype(BF16)
    send_sems, recv_sems, rest_thru, land_thru, token = _late_gather_start(rest)
    full = {n: w[n] for n in ("mlstm_norm_g", "norm_mlp_g", "norm_ple_g", "b_if", "sinks")}
    full["norm_mix_g"] = w["norm_mix_g"] + token[0, 0]
    full["final_norm_g"] = w["final_norm_g"].reshape(1, D)
    full["w_in"] = _win_pad(w_in_all)
    full["conv_qk"] = jnp.swapaxes(conv_all, 0, 1).reshape(CONV, D)

    def late_weights(after):
        land = _late_gather_wait(send_sems, recv_sems, rest_thru, land_thru, after)
        out = {n: land[:, i * (D // 4):(i + 1) * (D // 4)].reshape(D, D) for i, n in enumerate(_W4)}
        out["w_up"] = land[:, D:2 * D]
        out["w_down"] = land[:, 2 * D:3 * D].reshape(DFF, D)
        out["w_ple_proj"] = jnp.swapaxes(land[:, 3 * D:3 * D + rows_pp].reshape(4, PLE, D // 4), 0, 1).reshape(PLE, D)
        return out

    early, last = {}, {}

    def pair_sums(by_dest, theirs, names):
        return [_pair_sum(a, b, j, c, "pair_sum_" + n) for a, b, n in zip(by_dest, theirs, names)]

    def early_grads(g):
        by_dest = [jnp.stack([g[n].reshape(4, D // 4, D) for n in _W4], axis=1).reshape(4, D, D),
                   g["w_up"], g["w_down"].reshape(4, DFF // 4, D), g["w_ple_proj"]]
        *early["pair"], token = _pair_exchange_start(by_dest, "early")
        return token[0, 0]

    def mid_grads(after):
        early["sums"] = pair_sums(*_pair_exchange_wait(*early["pair"], after, "early"), _GROUP_NAMES[1:])
        *early["flight"], token = _chip_exchange_start([s[1] for s in early["sums"]], "early")
        return token[0, 0]

    def last_grad(g):
        *last["pair"], token = _pair_exchange_start([_win_unpad(g["w_in"])], "w_in")
        return token[0, 0]

    loss, grad_x, g = _local_step(x[0], p[0, 0], target[0], full, late_weights, early_grads, mid_grads, last_grad)

    last["sums"] = pair_sums(*_pair_exchange_wait(*last["pair"], grad_x, "w_in"), _GROUP_NAMES[:1])
    *last["flight"], token = _chip_exchange_start([s[1] for s in last["sums"]], "w_in")

    def reduce_share(sums, others, names, tag):
        halves = [_reduce4(s[0], b, c, "reduce4_" + n) for s, b, n in zip(sums, others, names)]
        return list(_sibling_share(halves, "sibling_share_" + tag))

    ms, vs = shards(m), shards(v)
    grads = reduce_share(early["sums"], _chip_exchange_wait(*early["flight"], token, "early"), _GROUP_NAMES[1:], "early")
    upd = [_adamw_call(wa, ga, ma, va, "adamw_" + n)
           for wa, ga, ma, va, n in zip(_group(ws)[1:], grads, _group(ms)[1:], _group(vs)[1:], _GROUP_NAMES[1:])]
    small_g = _small_allreduce(_pack_small(g, extra=loss, conv=g["conv_qk"]))
    conv_g = lax.dynamic_slice(small_g[_CONV_ROW:_CONV_ROW + CONV], (0, j * (D // 4)), (CONV, D // 4))
    conv_upd = _adamw_call(ws["conv_qk"], conv_g, ms["conv_qk"], vs["conv_qk"], "adamw_conv")
    small_upd = _adamw_call(_pack_small(w), small_g, _pack_small(m), _pack_small(v), "adamw_small")

    done = sum(a[0][0:1, 0:1] for a in upd + [conv_upd, small_upd])
    others = _chip_exchange_wait(*last["flight"], done, "w_in")
    grads = reduce_share(last["sums"], others, _GROUP_NAMES[:1], "w_in") + list(grads)
    upd_in = _adamw_call(*[jnp.swapaxes(a, 0, 1) for a in (ws["w_in"], grads[0], ms["w_in"], vs["w_in"])], "adamw_w_in")
    upd = [[jnp.swapaxes(a, 0, 1) for a in upd_in]] + upd

    shapes = {n: w[n].shape for n in _NAMES}
    res = []
    for k in range(4):
        big = _ungroup(list(grads) if k == 0 else [u[k - 1] for u in upd])
        big["conv_qk"] = conv_g if k == 0 else conv_upd[k - 1]
        leaves = _unpack_small(small_g if k == 0 else small_upd[k - 1], shapes)
        leaves.update({n: a.reshape(shapes[n]) for n, a in big.items()})
        res.append(leaves)

    out = [small_g[5, 8 + SWH], grad_x[None]]
    for k in range(4):
        out += [res[k][n] for n in _NAMES]
    return tuple(out)


def kernel(x, p, norm_mix_g, w_in, conv_qk, b_if, mlstm_norm_g, sinks, w_branch_a, w_branch_b, w_out, norm_mlp_g, w_up, w_down, norm_ple_g, w_ple_gate, w_ple_proj, final_norm_g, loss_target, m_norm_mix_g, m_w_in, m_conv_qk, m_b_if, m_mlstm_norm_g, m_sinks, m_w_branch_a, m_w_branch_b, m_w_out, m_norm_mlp_g, m_w_up, m_w_down, m_norm_ple_g, m_w_ple_gate, m_w_ple_proj, m_final_norm_g, v_norm_mix_g, v_w_in, v_conv_qk, v_b_if, v_mlstm_norm_g, v_sinks, v_w_branch_a, v_w_branch_b, v_w_out, v_norm_mlp_g, v_w_up, v_w_down, v_norm_ple_g, v_w_ple_gate, v_w_ple_proj, v_final_norm_g):
    w = dict(zip(_NAMES, (norm_mix_g, w_in, conv_qk, b_if, mlstm_norm_g, sinks, w_branch_a, w_branch_b, w_out,
                          norm_mlp_g, w_up, w_down, norm_ple_g, w_ple_gate, w_ple_proj, final_norm_g)))
    m = dict(zip(_NAMES, (m_norm_mix_g, m_w_in, m_conv_qk, m_b_if, m_mlstm_norm_g, m_sinks, m_w_branch_a,
                          m_w_branch_b, m_w_out, m_norm_mlp_g, m_w_up, m_w_down, m_norm_ple_g, m_w_ple_gate,
                          m_w_ple_proj, m_final_norm_g)))
    v = dict(zip(_NAMES, (v_norm_mix_g, v_w_in, v_conv_qk, v_b_if, v_mlstm_norm_g, v_sinks, v_w_branch_a,
                          v_w_branch_b, v_w_out, v_norm_mlp_g, v_w_up, v_w_down, v_norm_ple_g, v_w_ple_gate,
                          v_w_ple_proj, v_final_norm_g)))
    return _step(x, p, loss_target, w, m, v)
```

```python
import jax
import jax.numpy as jnp
from jax import lax
from jax.experimental import pallas as pl
from jax.experimental.pallas import tpu as pltpu

F32 = jnp.float32
BF16 = jnp.bfloat16

D = 1024
PLE = 256
MLH = 4
DQK = 128
DV = 256
CONV = 4
CHUNK = 256
SWH = 16
SWKV = 4
SWG = SWH // SWKV
HD = 64
WIN = 128
DFF = 4096
EPS = 1e-6
N_IN = 6664
NP = 7168
C_QK, C_V, C_O, C_QSW, C_GA, C_GB, C_KV, C_IF = 0, 1024, 2048, 3072, 4096, 5120, 6144, 6656
IFW = NP - C_IF

ADAM_LR = 0.001
ADAM_B1 = 0.9
ADAM_B2 = 0.999
ADAM_EPS = 1e-08
ADAM_WD = 0.01
ADAM_STEP = 10

TOK_TILE = 512
V7X_VMEM_BYTES = 64 * 1024 * 1024
VMEM_LIMIT = V7X_VMEM_BYTES - 6 * 1024 * 1024


def _params(**kw):
    return pltpu.CompilerParams(vmem_limit_bytes=VMEM_LIMIT, **kw)


def _pick(n, cap):
    if n <= cap:
        return n
    t = cap - cap % 128
    while t > 128 and n % t:
        t -= 128
    assert n % t == 0, (n, cap)
    return t


def _dot(a, b, dims):
    return lax.dot_general(a, b, (dims, ((), ())), preferred_element_type=F32)


def _dot_nn(a, b):
    return _dot(a, b, ((1,), (0,)))


def _dot_nt(a, b):
    return _dot(a, b, ((1,), (1,)))


def _dot_tn(a, b):
    return _dot(a, b, ((0,), (0,)))


def _sigmoid(x):
    return 1.0 / (1.0 + jnp.exp(-x))


def _mm(a, b, mode, out_dtype, name, out_chunks=1):
    if mode == "nn":
        (m, k), (k2, n) = a.shape, b.shape
    elif mode == "nt":
        (m, k), (n, k2) = a.shape, b.shape
    else:
        (k, m), (k2, n) = a.shape, b.shape
    assert k == k2, (a.shape, b.shape, mode)
    tm, tn, tk = _pick(m, 1024), _pick(n // out_chunks, 1024), _pick(k, 2048)
    nk = k // tk
    if mode == "nn":
        a_spec = pl.BlockSpec((tm, tk), lambda i, j, kk: (i, kk))
        b_spec = pl.BlockSpec((tk, tn), lambda i, j, kk: (kk, j))
        dot = _dot_nn
    elif mode == "nt":
        a_spec = pl.BlockSpec((tm, tk), lambda i, j, kk: (i, kk))
        b_spec = pl.BlockSpec((tn, tk), lambda i, j, kk: (j, kk))
        dot = _dot_nt
    else:
        a_spec = pl.BlockSpec((tk, tm), lambda i, j, kk: (kk, i))
        b_spec = pl.BlockSpec((tk, tn), lambda i, j, kk: (kk, j))
        dot = _dot_tn
    if out_chunks > 1:
        npc = (n // out_chunks) // tn
        out_spec = pl.BlockSpec((None, tm, tn), lambda i, j, kk: (j // npc, i, j % npc))
        out_shape = jax.ShapeDtypeStruct((out_chunks, m, n // out_chunks), out_dtype)
    else:
        out_spec = pl.BlockSpec((tm, tn), lambda i, j, kk: (i, j))
        out_shape = jax.ShapeDtypeStruct((m, n), out_dtype)

    def body(a_ref, b_ref, o_ref, acc_ref):
        kk = pl.program_id(2)

        @pl.when(kk == 0)
        def _():
            acc_ref[...] = jnp.zeros_like(acc_ref)

        acc_ref[...] += dot(a_ref[...], b_ref[...])

        @pl.when(kk == nk - 1)
        def _():
            o_ref[...] = acc_ref[...].astype(out_dtype)

    return pl.pallas_call(
        body, name=name, grid=(m // tm, n // tn, nk),
        in_specs=[a_spec, b_spec], out_specs=out_spec, out_shape=out_shape,
        scratch_shapes=[pltpu.VMEM((tm, tn), F32)],
        compiler_params=_params(dimension_semantics=("parallel", "parallel", "arbitrary")),
    )(a, b)


def _tile(col0=0):
    return lambda tm, tn: pl.BlockSpec((tm, tn), lambda i, j, kk: (i, col0 // tn + j))


def _row():
    return lambda tm, tn: pl.BlockSpec((1, tn), lambda i, j, kk: (0, j))


def _mm_ep(pairs, mode, name, epilogue, ins, outs, tm, tn, aliases=None):
    a0, b0 = pairs[0]
    bch = b0.shape[0] if b0.ndim == 3 else 1
    m, k = a0.shape
    tm = _pick(m, tm)
    n = b0.shape[-1] * bch if mode == "nn" else b0.shape[-2]
    tk = _pick(k // bch if mode == "nt" else k, 2048)
    nk = k // tk
    a_spec = pl.BlockSpec((tm, tk), lambda i, j, kk: (i, kk))
    if mode == "nn":
        dot = _dot_nn
        if bch > 1:
            bpc = (n // bch) // tn
            b_spec = pl.BlockSpec((None, tk, tn), lambda i, j, kk: (j // bpc, kk, j % bpc))
        else:
            b_spec = pl.BlockSpec((tk, tn), lambda i, j, kk: (kk, j))
    else:
        dot = _dot_nt
        if bch > 1:
            bpc = (k // bch) // tk
            b_spec = pl.BlockSpec((None, tn, tk), lambda i, j, kk: (kk // bpc, j, kk % bpc))
        else:
            b_spec = pl.BlockSpec((tn, tk), lambda i, j, kk: (j, kk))
    npair, nin, nout = len(pairs), len(ins), len(outs)

    def body(*refs):
        ab = refs[:2 * npair]
        in_refs = refs[2 * npair:2 * npair + nin]
        out_refs = refs[2 * npair + nin:2 * npair + nin + nout]
        accs = refs[2 * npair + nin + nout:]
        i, j, kk = pl.program_id(0), pl.program_id(1), pl.program_id(2)
        for p in range(npair):
            prod = dot(ab[2 * p][...], ab[2 * p + 1][...])

            @pl.when(kk == 0)
            def _():
                accs[p][...] = prod

            @pl.when(kk > 0)
            def _():
                accs[p][...] += prod

        @pl.when(kk == nk - 1)
        def _():
            epilogue([acc[...] for acc in accs], in_refs, out_refs, i, j)

    operands = [x for pair in pairs for x in pair] + [a for a, _ in ins]
    io_alias = {2 * npair + i: o for i, o in (aliases or {}).items()}
    return pl.pallas_call(
        body, name=name, grid=(m // tm, n // tn, nk),
        in_specs=[a_spec, b_spec] * npair + [mk(tm, tn) for _, mk in ins],
        out_specs=[mk(tm, tn) for _, mk in outs], out_shape=[s for s, _ in outs],
        scratch_shapes=[pltpu.VMEM((tm, tn), F32)] * npair, input_output_aliases=io_alias,
        compiler_params=_params(dimension_semantics=("arbitrary", "arbitrary", "arbitrary")),
    )(*operands)


def _tok(w, j=0):
    return pl.BlockSpec((TOK_TILE, w), lambda i: (i, j))


def _rep(shape):
    return pl.BlockSpec(shape, lambda i: (0,) * len(shape))


def _rms(x):
    rstd = lax.rsqrt(jnp.mean(x * x, axis=-1, keepdims=True) + EPS)
    return x * rstd, rstd


def _rms_bwd(xn, rstd, dxn):
    return rstd * (dxn - xn * jnp.mean(dxn * xn, axis=-1, keepdims=True))


def _halo_prev(w, j=0, rows=8):
    r = TOK_TILE // rows
    return pl.BlockSpec((rows, w), lambda i: (jnp.maximum(i * r - 1, 0), j))


def _last8(halo_ref):
    return halo_ref[...].astype(F32)[halo_ref.shape[0] - 8:]


def _halo_next(w, nt, j=0):
    r = TOK_TILE // 8
    return pl.BlockSpec((8, w), lambda i: (jnp.minimum((i + 1) * r, nt * r - 1), j))


def _shift_down(x, halo, s):
    if s == 0:
        return x
    r = pltpu.roll(x, s, 0)
    hs = pltpu.roll(halo, s, 0)
    row = lax.broadcasted_iota(jnp.int32, hs.shape, 0)
    top = jnp.where(row < s, hs, r[0:8])
    return jnp.concatenate([top, r[8:]], axis=0)


def _shift_up(x, halo, s):
    if s == 0:
        return x
    n = x.shape[0]
    r = pltpu.roll(x, n - s, 0)
    hs = pltpu.roll(halo, 8 - s, 0)
    row = lax.broadcasted_iota(jnp.int32, hs.shape, 0)
    bot = jnp.where(row >= 8 - s, hs, r[n - 8:])
    return jnp.concatenate([r[:n - 8], bot], axis=0)


def _bf(x):
    return x.astype(BF16).astype(F32)


def _conv_taps(x, halo, w):
    x, halo, w = _bf(x), _bf(halo), _bf(w)
    acc = x * w[CONV - 1:CONV, :]
    for j in range(CONV - 1):
        acc = acc + _shift_down(x, halo, CONV - 1 - j) * w[j:j + 1, :]
    return acc


_Q_SCALE = DQK ** -0.5


def _qscale_row():
    lane = lax.broadcasted_iota(jnp.int32, (1, D), 1)
    return jnp.where(lane < MLH * DQK, _Q_SCALE, 1.0).astype(F32)


def _conv_silu_fwd(proj, conv_w):
    t = proj.shape[0]

    def body(x_ref, halo_ref, w_ref, o_ref):
        halo = jnp.where(pl.program_id(0) > 0, _last8(halo_ref), 0.0)
        c = _conv_taps(x_ref[...].astype(F32), halo, w_ref[...])
        o_ref[...] = (c * _sigmoid(c) * _qscale_row()).astype(BF16)

    return pl.pallas_call(
        body, name="conv_silu_fwd", grid=(t // TOK_TILE,),
        in_specs=[_tok(D, C_QK // D), _halo_prev(D, C_QK // D, 16), _rep((CONV, D))], out_specs=_tok(D),
        out_shape=jax.ShapeDtypeStruct((t, D), BF16), compiler_params=_params(),
    )(proj, proj, conv_w)


def _conv_silu_bwd_a(proj, conv_w, dqk):
    t = proj.shape[0]

    def body(x_ref, halo_ref, w_ref, d_ref, dc_ref, dw_ref):
        @pl.when(pl.program_id(0) == 0)
        def _():
            dw_ref[...] = jnp.zeros_like(dw_ref)

        halo = jnp.where(pl.program_id(0) > 0, _last8(halo_ref), 0.0)
        x = x_ref[...].astype(F32)
        c = _conv_taps(x, halo, w_ref[...])
        s = _sigmoid(c)
        dc = d_ref[...] * _qscale_row() * (s * (1.0 + c * (1.0 - s)))
        dc_ref[...] = dc
        dcb, xb, halo_b = _bf(dc), _bf(x), _bf(halo)
        for j in range(CONV):
            dw_ref[j:j + 1, :] += jnp.sum(dcb * _shift_down(xb, halo_b, CONV - 1 - j), axis=0, keepdims=True)

    return pl.pallas_call(
        body, name="conv_silu_bwd_a", grid=(t // TOK_TILE,),
        in_specs=[_tok(D, C_QK // D), _halo_prev(D, C_QK // D, 16), _rep((CONV, D)), _tok(D)],
        out_specs=[_tok(D), _rep((CONV, D))],
        out_shape=[jax.ShapeDtypeStruct((t, D), F32), jax.ShapeDtypeStruct((CONV, D), F32)],
        compiler_params=_params(),
    )(proj, proj, conv_w, dqk)


def _conv_silu_bwd_b(dc, conv_w, dproj):
    t = dc.shape[0]
    nt = t // TOK_TILE

    def body(dc_ref, halo_ref, w_ref, _, dx_ref):
        halo = _bf(jnp.where(pl.program_id(0) < nt - 1, halo_ref[...], 0.0))
        dcv = _bf(dc_ref[...])
        w = _bf(w_ref[...])
        acc = dcv * w[CONV - 1:CONV, :]
        for j in range(CONV - 1):
            acc = acc + _shift_up(dcv, halo, CONV - 1 - j) * w[j:j + 1, :]
        dx_ref[...] = acc.astype(BF16)

    return pl.pallas_call(
        body, name="conv_silu_bwd_b", grid=(nt,), in_specs=[_tok(D), _halo_next(D, nt), _rep((CONV, D)), _ANY],
        out_specs=_tok(D, C_QK // D), out_shape=jax.ShapeDtypeStruct((t, NP), BF16),
        input_output_aliases={3: 0}, compiler_params=_params(),
    )(dc, dc, conv_w, dproj)


def _gates_fwd(pre_rows, bias_col):
    t = pre_rows.shape[1]

    def body(p_ref, b_ref, g_ref, s_ref):
        z = p_ref[...] + b_ref[...]
        lf = jnp.minimum(z, 0.0) - jnp.log(1.0 + jnp.exp(-jnp.abs(z)))
        lane = lax.broadcasted_iota(jnp.int32, z.shape, 1) % CHUNK
        cum = lf
        s = 1
        while s < CHUNK:
            cum = cum + jnp.where(lane >= s, pltpu.roll(cum, s, 1), 0.0)
            s *= 2
        sub = lax.broadcasted_iota(jnp.int32, z.shape, 0)
        g_ref[...] = jnp.where(sub < MLH, z, cum)
        s_ref[...] = _sigmoid(-z)

    return pl.pallas_call(
        body, name="gates_fwd",
        out_shape=[jax.ShapeDtypeStruct((8, t), F32), jax.ShapeDtypeStruct((8, t), F32)],
        compiler_params=_params(),
    )(pre_rows, bias_col)


def _chunk_terms(grow, gcol, m0):
    heads = range(MLH)
    i_row = [grow[h:h + 1, :] for h in heads]
    b_row = [grow[MLH + h:MLH + h + 1, :] for h in heads]
    i_col = [gcol[:, h:h + 1] for h in heads]
    b_col = [gcol[:, MLH + h:MLH + h + 1] for h in heads]
    b_last = [b_row[h][:, CHUNK - 1:CHUNK] for h in heads]
    tt = lax.broadcasted_iota(jnp.int32, (CHUNK, CHUNK), 0)
    ss = lax.broadcasted_iota(jnp.int32, (CHUNK, CHUNK), 1)
    log_d = [jnp.where(tt >= ss, b_col[h] - b_row[h] + i_row[h], -jnp.inf) for h in heads]
    row_max = [jnp.max(log_d[h], axis=1, keepdims=True) for h in heads]
    last_max = [jnp.max(b_last[h] - b_row[h] + i_row[h], axis=1, keepdims=True) for h in heads]
    m_t = [jnp.maximum(b_col[h] + m0[h], row_max[h]) for h in heads]
    m1 = [jnp.maximum(b_last[h] + m0[h], last_max[h]) for h in heads]
    dm = [jnp.exp(log_d[h] - m_t[h]) for h in heads]
    wi = [jnp.exp(b_col[h] + m0[h] - m_t[h]) for h in heads]
    ws = [jnp.exp(b_last[h] - b_col[h] + i_col[h] - m1[h]) for h in heads]
    dec = [jnp.exp(b_last[h] + m0[h] - m1[h]) for h in heads]
    return [(dm[h], wi[h], m_t[h], ws[h], dec[h], m1[h]) for h in heads]


def _mlstm_fwd(qk, proj, grow, gcol, gain):
    t = qk.shape[0]
    nc = t // CHUNK

    def body(qk_ref, v_ref, o_ref, grow_ref, gcol_ref, g_ref, h_ref, y_ref, cs_ref, st_ref, c_scr, st_scr):
        @pl.when(pl.program_id(0) == 0)
        def _():
            c_scr[...] = jnp.zeros_like(c_scr)
            st_scr[...] = jnp.zeros_like(st_scr)

        grow_v, gcol_v = grow_ref[...], gcol_ref[...]
        heads = range(MLH)
        q = [qk_ref[:, h * DQK:(h + 1) * DQK] for h in heads]
        k = [qk_ref[:, MLH * DQK + h * DQK:MLH * DQK + (h + 1) * DQK] for h in heads]
        v = [v_ref[:, h * DV:(h + 1) * DV] for h in heads]
        c0 = [c_scr[h] for h in heads]
        n0 = [st_scr[h, 0:1, :] for h in heads]
        for h in heads:
            cs_ref[0, h] = c0[h]
            st_ref[0, h] = st_scr[h]
        terms = _chunk_terms(grow_v, gcol_v, [st_scr[h, 1:2, 0:1] for h in heads])
        a = [_dot_nt(q[h], k[h]) for h in heads]
        qc = [_dot_nt(q[h], c0[h].astype(BF16)) for h in heads]
        s = [a[h] * terms[h][0] for h in heads]
        sv = [_dot_nn(s[h].astype(BF16), v[h]) for h in heads]
        upd = [_dot_tn((terms[h][3] * v[h]).astype(BF16), k[h]) for h in heads]
        den = [terms[h][1] * jnp.sum(q[h].astype(F32) * n0[h], axis=1, keepdims=True)
               + jnp.sum(s[h], axis=1, keepdims=True) for h in heads]
        hv = [(terms[h][1] * qc[h] + sv[h]) / jnp.maximum(jnp.abs(den[h]), jnp.exp(-terms[h][2])) for h in heads]
        for h in heads:
            sl = slice(h * DV, (h + 1) * DV)
            h_ref[:, sl] = hv[h]
            xn, _ = _rms(hv[h])
            y_ref[:, sl] = (_sigmoid(o_ref[:, sl].astype(F32)) * xn * g_ref[:, sl]).astype(BF16)
        for h in heads:
            dec, m1 = terms[h][4], terms[h][5]
            c_scr[h] = dec * c0[h] + upd[h]
            st_scr[h, 0:1, :] = dec * n0[h] + jnp.sum(terms[h][3] * k[h].astype(F32), axis=0, keepdims=True)
            st_scr[h, 1:2, :] = jnp.broadcast_to(m1, (1, DQK))

    return pl.pallas_call(
        body, name="mlstm_fwd", grid=(nc,),
        in_specs=[pl.BlockSpec((CHUNK, D), lambda c: (c, 0)), pl.BlockSpec((CHUNK, D), lambda c: (c, C_V // D)),
                  pl.BlockSpec((CHUNK, D), lambda c: (c, C_O // D)),
                  pl.BlockSpec((8, CHUNK), lambda c: (0, c)), pl.BlockSpec((CHUNK, 8), lambda c: (c, 0)),
                  pl.BlockSpec((1, D), lambda c: (0, 0))],
        out_specs=[pl.BlockSpec((CHUNK, D), lambda c: (c, 0)), pl.BlockSpec((CHUNK, D), lambda c: (c, 0)),
                   pl.BlockSpec((1, MLH, DV, DQK), lambda c: (c, 0, 0, 0)),
                   pl.BlockSpec((1, MLH, 8, DQK), lambda c: (c, 0, 0, 0))],
        out_shape=[jax.ShapeDtypeStruct((t, D), F32), jax.ShapeDtypeStruct((t, D), BF16),
                   jax.ShapeDtypeStruct((nc, MLH, DV, DQK), F32), jax.ShapeDtypeStruct((nc, MLH, 8, DQK), F32)],
        scratch_shapes=[pltpu.VMEM((MLH, DV, DQK), F32), pltpu.VMEM((MLH, 8, DQK), F32)],
        compiler_params=_params(dimension_semantics=("arbitrary",)),
    )(qk, proj, proj, grow, gcol, gain)


def _mlstm_bwd(qk, proj, grow, gcol, sneg_col, cs, st, hraw, dh, dproj):
    t = qk.shape[0]
    nc = t // CHUNK

    def rev(c):
        return nc - 1 - c

    def nxt(c):
        return jnp.minimum(nc - c, nc - 1)

    def body(qk_ref, v_ref, grow_ref, gcol_ref, sneg_ref, cs_ref, st_ref, cs1_ref, st1_ref, h_ref, dh_ref, _,
             dqk_ref, dv_ref, dif_ref, dbif_ref, dc_scr, dn_scr):
        @pl.when(pl.program_id(0) == 0)
        def _():
            dc_scr[...] = jnp.zeros_like(dc_scr)
            dn_scr[...] = jnp.zeros_like(dn_scr)
            dbif_ref[...] = jnp.zeros_like(dbif_ref)

        grow_v, gcol_v, sneg = grow_ref[...], gcol_ref[...], sneg_ref[...]
        tt = lax.broadcasted_iota(jnp.int32, (CHUNK, CHUNK), 0)
        ss = lax.broadcasted_iota(jnp.int32, (CHUNK, CHUNK), 1)
        lane8 = lax.broadcasted_iota(jnp.int32, (CHUNK, 8), 1)
        heads = range(MLH)
        q = [qk_ref[:, h * DQK:(h + 1) * DQK] for h in heads]
        k = [qk_ref[:, MLH * DQK + h * DQK:MLH * DQK + (h + 1) * DQK] for h in heads]
        qf, kf = [a.astype(F32) for a in q], [a.astype(F32) for a in k]
        vb = [v_ref[:, h * DV:(h + 1) * DV].astype(BF16) for h in heads]
        c0 = [cs_ref[0, h] for h in heads]
        n0 = [st_ref[0, h, 0:1, :] for h in heads]
        dc1 = [dc_scr[h] for h in heads]
        dn1 = [dn_scr[h, 0:1, :] for h in heads]
        terms = _chunk_terms(grow_v, gcol_v, [st_ref[0, h, 1:2, 0:1] for h in heads])
        dm, wi, ws = [t[0] for t in terms], [t[1] for t in terms], [t[3] for t in terms]
        s = [_dot_nt(q[h], k[h]) * dm[h] for h in heads]
        den = [wi[h] * jnp.sum(qf[h] * n0[h], axis=1, keepdims=True) + jnp.sum(s[h], axis=1, keepdims=True)
               for h in heads]
        floor = [jnp.exp(-terms[h][2]) for h in heads]
        g = [jnp.maximum(jnp.abs(den[h]), floor[h]) for h in heads]
        dh_v = [dh_ref[:, h * DV:(h + 1) * DV] for h in heads]
        dnum = [dh_v[h] / g[h] for h in heads]
        dden = [-jnp.sum(dh_v[h] * h_ref[:, h * DV:(h + 1) * DV], axis=1, keepdims=True) / g[h] for h in heads]
        dden = [jnp.where(jnp.abs(den[h]) > floor[h], dden[h] * jnp.sign(den[h]), 0.0) for h in heads]
        dnum_b = [a.astype(BF16) for a in dnum]
        dc1_b = [a.astype(BF16) for a in dc1]
        da = [((_dot_nt(dnum_b[h], vb[h]) + dden[h]) * dm[h]).astype(BF16) for h in heads]
        dq_inter = [_dot_nn(dnum_b[h], c0[h].astype(BF16)) for h in heads]
        dk_inter = [_dot_nn(vb[h], dc1_b[h]) for h in heads]
        dv_inter = [_dot_nt(k[h], dc1_b[h]) for h in heads]
        dc_new = [_dot_tn((wi[h] * dnum[h]).astype(BF16), q[h]) for h in heads]
        dq = [_dot_nn(da[h], k[h]) + wi[h] * (dq_inter[h] + dden[h] * n0[h]) for h in heads]
        dk = [_dot_tn(da[h], q[h]) + ws[h] * (dk_inter[h] + dn1[h]) for h in heads]
        dv = [_dot_tn(s[h].astype(BF16), dnum_b[h]) + ws[h] * dv_inter[h] for h in heads]
        for h in heads:
            dqk_ref[:, h * DQK:(h + 1) * DQK] = dq[h]
            dqk_ref[:, MLH * DQK + h * DQK:MLH * DQK + (h + 1) * DQK] = dk[h]
            dv_ref[:, h * DV:(h + 1) * DV] = dv[h].astype(BF16)
        rk = [jnp.sum(kf[h] * dk[h], axis=1, keepdims=True) for h in heads]
        df = [jnp.sum(qf[h] * dq[h], axis=1, keepdims=True) - rk[h] for h in heads]
        df_row = [jnp.sum(jnp.where(tt == ss, df[h], 0.0), axis=0, keepdims=True) for h in heads]
        suffix = [jnp.sum(jnp.where(ss >= tt, df_row[h], 0.0), axis=1, keepdims=True) for h in heads]
        cross = [jnp.sum(jnp.sum(dc1[h] * cs1_ref[0, h], axis=0, keepdims=True), axis=1, keepdims=True)
                 + jnp.sum(dn1[h] * st1_ref[0, h, 0:1, :], axis=1, keepdims=True) for h in heads]
        dif = jnp.zeros((CHUNK, 8), F32)
        for h in heads:
            dpf = (suffix[h] + cross[h]) * sneg[:, MLH + h:MLH + h + 1]
            dif = dif + jnp.where(lane8 == h, rk[h], 0.0) + jnp.where(lane8 == MLH + h, dpf, 0.0)
            dc_scr[h] = terms[h][4] * dc1[h] + dc_new[h]
            dn_scr[h, 0:1, :] = terms[h][4] * dn1[h] + jnp.sum(wi[h] * dden[h] * qf[h], axis=0, keepdims=True)
        dif_ref[...] = dif
        dbif_ref[...] += jnp.sum(dif, axis=0, keepdims=True)

    return pl.pallas_call(
        body, name="mlstm_bwd", grid=(nc,),
        in_specs=[pl.BlockSpec((CHUNK, D), lambda c: (rev(c), 0)),
                  pl.BlockSpec((CHUNK, D), lambda c: (rev(c), C_V // D)),
                  pl.BlockSpec((8, CHUNK), lambda c: (0, rev(c))),
                  pl.BlockSpec((CHUNK, 8), lambda c: (rev(c), 0)),
                  pl.BlockSpec((CHUNK, 8), lambda c: (rev(c), 0)),
                  pl.BlockSpec((1, MLH, DV, DQK), lambda c: (rev(c), 0, 0, 0)),
                  pl.BlockSpec((1, MLH, 8, DQK), lambda c: (rev(c), 0, 0, 0)),
                  pl.BlockSpec((1, MLH, DV, DQK), lambda c: (nxt(c), 0, 0, 0)),
                  pl.BlockSpec((1, MLH, 8, DQK), lambda c: (nxt(c), 0, 0, 0)),
                  pl.BlockSpec((CHUNK, D), lambda c: (rev(c), 0)),
                  pl.BlockSpec((CHUNK, D), lambda c: (rev(c), 0)), _ANY],
        out_specs=[pl.BlockSpec((CHUNK, D), lambda c: (rev(c), 0)),
                   pl.BlockSpec((CHUNK, D), lambda c: (rev(c), C_V // D)),
                   pl.BlockSpec((CHUNK, 8), lambda c: (rev(c), 0)),
                   pl.BlockSpec((1, 8), lambda c: (0, 0))],
        out_shape=[jax.ShapeDtypeStruct((t, D), F32), jax.ShapeDtypeStruct((t, NP), BF16),
                   jax.ShapeDtypeStruct((t, 8), F32), jax.ShapeDtypeStruct((1, 8), F32)],
        scratch_shapes=[pltpu.VMEM((MLH, DV, DQK), F32), pltpu.VMEM((MLH, 8, DQK), F32)],
        input_output_aliases={11: 1}, compiler_params=_params(dimension_semantics=("arbitrary",)),
    )(qk, proj, grow, gcol, sneg_col, cs, st, cs, st, hraw, dh, dproj)


_ANY = pl.BlockSpec(memory_space=pl.ANY)


_SW_SCALE = HD ** -0.5
_KVB = C_KV // (2 * SWKV * HD)


def _swa_mask(n):
    ki = lax.broadcasted_iota(jnp.int32, (2 * WIN, SWG * WIN), 0)
    qi = lax.broadcasted_iota(jnp.int32, (2 * WIN, SWG * WIN), 1) % WIN
    return (ki > qi) & (ki <= qi + WIN) & ((n > 0) | (ki >= WIN))


def _group_rows(x_ref, hk):
    return jnp.concatenate([x_ref[:, (hk * SWG + g) * HD:(hk * SWG + g + 1) * HD] for g in range(SWG)], axis=0)


def _group_lanes(x_ref, hk):
    return jnp.concatenate([x_ref[hk * SWG + g:hk * SWG + g + 1, :] for g in range(SWG)], axis=1)


def _sink_lanes(sink_ref, hk):
    return jnp.concatenate([jnp.broadcast_to(sink_ref[:, hk * SWG + g:hk * SWG + g + 1], (1, WIN))
                            for g in range(SWG)], axis=1)


def _swa_fwd(proj, sinks):
    t = proj.shape[0]
    nb = t // WIN

    def body(q_ref, kvc_ref, kvp_ref, sink_ref, y_ref, lse_ref):
        valid = _swa_mask(pl.program_id(0))
        for hk in range(SWKV):
            ks = slice(hk * HD, (hk + 1) * HD)
            vs = slice(SWKV * HD + hk * HD, SWKV * HD + (hk + 1) * HD)
            kb = jnp.concatenate([kvp_ref[:, ks], kvc_ref[:, ks]], axis=0).astype(BF16)
            vb = jnp.concatenate([kvp_ref[:, vs], kvc_ref[:, vs]], axis=0).astype(BF16)
            q4 = _group_rows(q_ref, hk).astype(BF16)
            sink = _sink_lanes(sink_ref, hk)
            logits = jnp.where(valid, _dot_nt(kb, q4) * _SW_SCALE, -jnp.inf)
            m = jnp.maximum(jnp.max(logits, axis=0, keepdims=True), sink)
            p = jnp.exp(logits - m)
            denom = jnp.sum(p, axis=0, keepdims=True) + jnp.exp(sink - m)
            y4 = _dot_tn((p / denom).astype(BF16), vb).astype(BF16)
            lse4 = m + jnp.log(denom)
            for g in range(SWG):
                hq = hk * SWG + g
                y_ref[:, hq * HD:(hq + 1) * HD] = y4[g * WIN:(g + 1) * WIN]
                lse_ref[hq:hq + 1, :] = lse4[:, g * WIN:(g + 1) * WIN]

    return pl.pallas_call(
        body, name="swa_fwd", grid=(nb,),
        in_specs=[pl.BlockSpec((WIN, D), lambda n: (n, C_QSW // D)),
                  pl.BlockSpec((WIN, 512), lambda n: (n, _KVB)),
                  pl.BlockSpec((WIN, 512), lambda n: (jnp.maximum(n - 1, 0), _KVB)),
                  pl.BlockSpec((1, SWH), lambda n: (0, 0))],
        out_specs=[pl.BlockSpec((WIN, D), lambda n: (n, 0)), pl.BlockSpec((SWH, WIN), lambda n: (0, n))],
        out_shape=[jax.ShapeDtypeStruct((t, D), BF16), jax.ShapeDtypeStruct((SWH, t), F32)],
        compiler_params=_params(),
    )(proj, proj, proj, sinks)


def _swa_bwd(proj, sinks, lse, dyb, dproj):
    t = proj.shape[0]
    nb = t // WIN

    def body(q_ref, kvc_ref, kvp_ref, sink_ref, lse_ref, dy_ref, _, dq_ref, dself_ref, dprev_ref, ds_ref):
        @pl.when(pl.program_id(0) == 0)
        def _():
            ds_ref[...] = jnp.zeros_like(ds_ref)

        valid = _swa_mask(pl.program_id(0))
        kvh = range(SWKV)
        ks = [slice(hk * HD, (hk + 1) * HD) for hk in kvh]
        vs = [slice(SWKV * HD + hk * HD, SWKV * HD + (hk + 1) * HD) for hk in kvh]
        kb = [jnp.concatenate([kvp_ref[:, ks[hk]], kvc_ref[:, ks[hk]]], axis=0).astype(BF16) for hk in kvh]
        vb = [jnp.concatenate([kvp_ref[:, vs[hk]], kvc_ref[:, vs[hk]]], axis=0).astype(BF16) for hk in kvh]
        qb = [_group_rows(q_ref, hk).astype(BF16) for hk in kvh]
        dyb_ = [_group_rows(dy_ref, hk).astype(BF16) for hk in kvh]
        lse4 = [_group_lanes(lse_ref, hk) for hk in kvh]
        logits = [_dot_nt(kb[hk], qb[hk]) for hk in kvh]
        dpt = [_dot_nt(vb[hk], dyb_[hk]) for hk in kvh]
        p = [jnp.exp(jnp.where(valid, logits[hk] * _SW_SCALE, -jnp.inf) - lse4[hk]) for hk in kvh]
        delta = [jnp.sum(p[hk] * dpt[hk], axis=0, keepdims=True) for hk in kvh]
        dsm = [(p[hk] * (dpt[hk] - delta[hk])).astype(BF16) for hk in kvh]
        dq4 = [(_dot_tn(dsm[hk], kb[hk]) * _SW_SCALE).astype(BF16) for hk in kvh]
        dkb = [_dot_nn(dsm[hk], qb[hk]) * _SW_SCALE for hk in kvh]
        dvb = [_dot_nn(p[hk].astype(BF16), dyb_[hk]) for hk in kvh]
        for hk in kvh:
            dsink4 = jnp.exp(_sink_lanes(sink_ref, hk) - lse4[hk]) * delta[hk]
            for g in range(SWG):
                hq = hk * SWG + g
                dq_ref[:, hq * HD:(hq + 1) * HD] = dq4[hk][g * WIN:(g + 1) * WIN]
                ds_ref[:, hq:hq + 1] += -jnp.sum(dsink4[:, g * WIN:(g + 1) * WIN], axis=1, keepdims=True)
            dprev_ref[:, ks[hk]] = dkb[hk][:WIN]
            dself_ref[:, ks[hk]] = dkb[hk][WIN:]
            dprev_ref[:, vs[hk]] = dvb[hk][:WIN]
            dself_ref[:, vs[hk]] = dvb[hk][WIN:]

    return pl.pallas_call(
        body, name="swa_bwd", grid=(nb,),
        in_specs=[pl.BlockSpec((WIN, D), lambda n: (n, C_QSW // D)),
                  pl.BlockSpec((WIN, 512), lambda n: (n, _KVB)),
                  pl.BlockSpec((WIN, 512), lambda n: (jnp.maximum(n - 1, 0), _KVB)),
                  pl.BlockSpec((1, SWH), lambda n: (0, 0)),
                  pl.BlockSpec((SWH, WIN), lambda n: (0, n)),
                  pl.BlockSpec((WIN, D), lambda n: (n, 0)), _ANY],
        out_specs=[pl.BlockSpec((WIN, D), lambda n: (n, C_QSW // D)), pl.BlockSpec((WIN, 512), lambda n: (n, 0)),
                   pl.BlockSpec((WIN, 512), lambda n: (jnp.maximum(n - 1, 0), 0)),
                   pl.BlockSpec((1, SWH), lambda n: (0, 0))],
        out_shape=[jax.ShapeDtypeStruct((t, NP), BF16), jax.ShapeDtypeStruct((t, 512), F32),
                   jax.ShapeDtypeStruct((t, 512), F32), jax.ShapeDtypeStruct((1, SWH), F32)],
        input_output_aliases={6: 0}, compiler_params=_params(),
    )(proj, proj, proj, sinks, lse, dyb, dproj)


def _kv_combine(dself, dnext, dif, dproj):
    t = dself.shape[0]
    rows = _pick(t, 512)

    def body(a_ref, b_ref, dif_ref, _, o_ref):
        row = pl.program_id(0) * rows + lax.broadcasted_iota(jnp.int32, (rows, 1), 0)
        o_ref[:, 0:512] = (a_ref[...] + jnp.where(row < t - WIN, b_ref[...], 0.0)).astype(BF16)
        lane = lax.broadcasted_iota(jnp.int32, (rows, 128), 1)
        dif_v = dif_ref[...]
        first = jnp.zeros((rows, 128), F32)
        for col in range(8):
            first = first + jnp.where(lane == col, dif_v[:, col:col + 1], 0.0)
        o_ref[:, 512:640] = first.astype(BF16)
        o_ref[:, 640:512 + IFW] = jnp.zeros((rows, IFW - 128), BF16)

    return pl.pallas_call(
        body, name="kv_combine", grid=(t // rows,),
        in_specs=[pl.BlockSpec((rows, 512), lambda n: (n, 0)), pl.BlockSpec((rows, 512), lambda n: (n, 0)),
                  pl.BlockSpec((rows, 8), lambda n: (n, 0)), _ANY],
        out_specs=pl.BlockSpec((rows, 512 + IFW), lambda n: (n, C_KV // (512 + IFW))),
        out_shape=jax.ShapeDtypeStruct((t, NP), BF16), input_output_aliases={3: 0}, compiler_params=_params(),
    )(dself, dnext, dif, dproj)


def _sds(t, n, dtype):
    return jax.ShapeDtypeStruct((t, n), dtype)


def _proj_in(x, gain, w_in):
    t = x.shape[0]
    tm, tn = _pick(t, 1024), 2 * IFW

    def body(x_ref, g_ref, w_ref, h_ref, p_ref, gate_ref, h_scr):
        j = pl.program_id(1)

        @pl.when(j == 0)
        def _():
            xn, _ = _rms(x_ref[...])
            h = (xn * g_ref[...]).astype(BF16)
            h_scr[...] = h
            h_ref[...] = h

        acc = _dot_nn(h_scr[...], w_ref[...])
        p_ref[...] = acc.astype(BF16)

        @pl.when(j == C_IF // tn)
        def _():
            gate_ref[...] = acc[:, C_IF % tn:C_IF % tn + 128]

    return pl.pallas_call(
        body, name="mm_in", grid=(t // tm, NP // tn),
        in_specs=[pl.BlockSpec((tm, D), lambda i, j: (i, 0)), pl.BlockSpec((1, D), lambda i, j: (0, 0)),
                  pl.BlockSpec((D, tn), lambda i, j: (0, j))],
        out_specs=[pl.BlockSpec((tm, D), lambda i, j: (i, 0)), pl.BlockSpec((tm, tn), lambda i, j: (i, j)),
                   pl.BlockSpec((tm, 128), lambda i, j: (i, 0))],
        out_shape=[_sds(t, D, BF16), _sds(t, NP, BF16), _sds(t, 128, F32)],
        scratch_shapes=[pltpu.VMEM((tm, D), BF16)],
        compiler_params=_params(dimension_semantics=("arbitrary", "arbitrary")),
    )(x, gain, w_in)


def _branch_merge(ya, yb, wa, wb, proj):
    t = ya.shape[0]

    def epilogue(accs, ins, outs, i, j):
        za, zb = accs
        merged = _sigmoid(ins[0][...].astype(F32)) * za + _sigmoid(ins[1][...].astype(F32)) * zb
        outs[0][...] = merged.astype(BF16)
        outs[1][...] = za.astype(BF16)
        outs[2][...] = zb.astype(BF16)

    return _mm_ep([(ya, wa), (yb, wb)], "nn", "mm_branch_merge", epilogue, [(proj, _tile(C_GA)), (proj, _tile(C_GB))],
                  [(_sds(t, D, BF16), _tile())] * 3, 1024, 1024)


def _dmerged_bwd(dxb, w_out, proj, za, zb):
    t = dxb.shape[0]

    def epilogue(accs, ins, outs, i, j):
        dm = accs[0]
        sa, sb = _sigmoid(ins[0][...].astype(F32)), _sigmoid(ins[1][...].astype(F32))
        outs[0][...] = (dm * sa).astype(BF16)
        outs[1][...] = (dm * sb).astype(BF16)
        outs[2][:, 0:D] = (dm * ins[2][...].astype(F32) * sa * (1.0 - sa)).astype(BF16)
        outs[2][:, D:2 * D] = (dm * ins[3][...].astype(F32) * sb * (1.0 - sb)).astype(BF16)

    gate_cols = lambda tm, tn: pl.BlockSpec((tm, 2 * D), lambda i, j, kk: (i, C_GA // (2 * D)))
    return _mm_ep([(dxb, w_out)], "nt", "mm_dmerged_bwd", epilogue,
                  [(proj, _tile(C_GA)), (proj, _tile(C_GB)), (za, _tile()), (zb, _tile())],
                  [(_sds(t, D, BF16), _tile()), (_sds(t, D, BF16), _tile()), (_sds(t, NP, BF16), gate_cols)], 1024, D)


def _dya_bwd(dza, wa, hraw, proj, g, dproj):
    t = dza.shape[0]

    def epilogue(accs, ins, outs, i, j):
        h_ref, o_ref, g_ref, _ = ins
        dh_ref, do_ref, dg_ref = outs

        @pl.when(i == 0)
        def _():
            dg_ref[...] = jnp.zeros_like(dg_ref)

        dy = accs[0]
        so = _sigmoid(o_ref[...].astype(F32))
        for h in range(MLH):
            sl = slice(h * DV, (h + 1) * DV)
            xn, rstd = _rms(h_ref[:, sl])
            gs = g_ref[:, sl]
            do_ref[:, sl] = (dy[:, sl] * xn * gs * so[:, sl] * (1.0 - so[:, sl])).astype(BF16)
            dhn = dy[:, sl] * so[:, sl]
            dg_ref[:, sl] += jnp.sum(dhn * xn, axis=0, keepdims=True)
            dh_ref[:, sl] = _rms_bwd(xn, rstd, dhn * gs)

    return _mm_ep([(dza, wa)], "nt", "mm_dya_bwd", epilogue,
                  [(hraw, _tile()), (proj, _tile(C_O)), (g, _row()), (dproj, lambda tm, tn: _ANY)],
                  [(_sds(t, D, F32), _tile()), (_sds(t, NP, BF16), _tile(C_O)), (_sds(1, D, F32), _row())],
                  512, D, aliases={3: 1})


def _up_act(hn, w_up):
    t = hn.shape[0]

    def epilogue(accs, ins, outs, i, j):
        r = jnp.maximum(accs[0], 0.0)
        outs[0][...] = (r * r).astype(BF16)
        outs[1][...] = accs[0].astype(BF16)

    return _mm_ep([(hn, w_up)], "nn", "mm_up_act", epilogue, [],
                  [(_sds(t, DFF, BF16), _tile()), (_sds(t, DFF, BF16), _tile())], 1024, 1024)


def _da_du(dxb, w_down, u):
    t = dxb.shape[0]

    def epilogue(accs, ins, outs, i, j):
        outs[0][...] = (accs[0] * 2.0 * jnp.maximum(ins[0][...].astype(F32), 0.0)).astype(BF16)

    return _mm_ep([(dxb, w_down)], "nt", "mm_da_du", epilogue, [(u, _tile())], [(_sds(t, DFF, BF16), _tile())],
                  1024, 1024)[0]


def _resid_norm_mm(a, w, x, g, name):
    t = x.shape[0]

    def epilogue(accs, ins, outs, i, j):
        x1 = ins[0][...] + accs[0]
        outs[0][...] = x1
        xn, _ = _rms(x1)
        outs[1][...] = (xn * ins[1][...]).astype(BF16)

    return _mm_ep([(a, w)], "nn", name, epilogue, [(x, _tile()), (g, _row())],
                  [(_sds(t, D, F32), _tile()), (_sds(t, D, BF16), _tile())], 1024, D)


def _norm_bwd_mm(dy, w, x, g, dres, name):
    t = x.shape[0]

    def epilogue(accs, ins, outs, i, j):
        @pl.when(i == 0)
        def _():
            outs[2][...] = jnp.zeros_like(outs[2])

        dh = accs[0]
        xn, rstd = _rms(ins[0][...])
        outs[2][...] += jnp.sum(dh * xn, axis=0, keepdims=True)
        dx = ins[2][...] + _rms_bwd(xn, rstd, dh * ins[1][...])
        outs[0][...] = dx
        outs[1][...] = dx.astype(BF16)

    return _mm_ep([(dy, w)], "nt", name, epilogue, [(x, _tile()), (g, _row()), (dres, _tile())],
                  [(_sds(t, D, F32), _tile()), (_sds(t, D, BF16), _tile()), (_sds(1, D, F32), _row())], 1024, D)


def _ple_final_mm(hn2, w_gate, x2, pp, target, gf):
    t = x2.shape[0]

    def epilogue(accs, ins, outs, i, j):
        loss_ref, dg_ref, dx_ref, dpp_ref, dgp_ref = outs

        @pl.when(i == 0)
        def _():
            loss_ref[...] = jnp.zeros_like(loss_ref)
            dg_ref[...] = jnp.zeros_like(dg_ref)

        gate = _sigmoid(accs[0])
        pp_v = ins[1][...]
        x3 = ins[0][...] + gate * pp_v
        xn, rstd = _rms(x3)
        gf_v = ins[3][...]
        err = xn * gf_v - ins[2][...]
        loss_ref[...] += (0.5 / D) * jnp.sum(jnp.sum(err * err, axis=1, keepdims=True), axis=0, keepdims=True)
        dy = err * (1.0 / D)
        dg_ref[...] += jnp.sum(dy * xn, axis=0, keepdims=True)
        dx3 = _rms_bwd(xn, rstd, dy * gf_v)
        dx_ref[...] = dx3
        dpp_ref[...] = (dx3 * gate).astype(BF16)
        dgp_ref[...] = (dx3 * pp_v * gate * (1.0 - gate)).astype(BF16)

    one = lambda tm, tn: pl.BlockSpec((1, 1), lambda i, j, kk: (0, 0))
    return _mm_ep([(hn2, w_gate)], "nn", "mm_ple_final", epilogue,
                  [(x2, _tile()), (pp, _tile()), (target, _tile()), (gf, _row())],
                  [(_sds(1, 1, F32), one), (_sds(1, D, F32), _row()), (_sds(t, D, F32), _tile()),
                   (_sds(t, D, BF16), _tile()), (_sds(t, D, BF16), _tile())], 512, D)


_WIN_SEGMENTS = ((0, 3072, C_QK), (3072, 8, C_IF), (3080, 1024, C_QSW), (4104, 256, C_KV), (4360, 256, C_KV + 256),
                 (4616, 1024, C_GA), (5640, 1024, C_GB))
_WIN_SHARD = N_IN // 4


def _win_pieces():
    out = []
    for src, width, dst in _WIN_SEGMENTS:
        while width:
            chip, col = divmod(src, _WIN_SHARD)
            n = min(width, _WIN_SHARD - col)
            out.append((chip, col, n, dst))
            src, dst, width = src + n, dst + n, width - n
    return out


def _win_pad(shards):
    rows = shards.shape[1]
    tr = _pick(rows, 256)

    def body(s_ref, o_ref):
        for chip, col, n, dst in _win_pieces():
            o_ref[:, dst:dst + n] = s_ref[chip, :, col:col + n]
        o_ref[:, C_IF + 8:NP] = jnp.zeros((tr, NP - C_IF - 8), shards.dtype)

    return pl.pallas_call(
        body, name="win_pad", grid=(rows // tr,), in_specs=[pl.BlockSpec((4, tr, _WIN_SHARD), lambda i: (0, i, 0))],
        out_specs=pl.BlockSpec((tr, NP), lambda i: (i, 0)), out_shape=jax.ShapeDtypeStruct((rows, NP), shards.dtype),
        compiler_params=_params(),
    )(shards)


def _win_unpad(wp):
    rows = wp.shape[0]
    tr = _pick(rows, 256)

    def body(p_ref, o_ref):
        for chip, col, n, dst in _win_pieces():
            o_ref[chip, :, col:col + n] = p_ref[:, dst:dst + n]

    return pl.pallas_call(
        body, name="win_unpad", grid=(rows // tr,), in_specs=[pl.BlockSpec((tr, NP), lambda i: (i, 0))],
        out_specs=pl.BlockSpec((4, tr, _WIN_SHARD), lambda i: (0, i, 0)),
        out_shape=jax.ShapeDtypeStruct((4, rows, _WIN_SHARD), wp.dtype), compiler_params=_params(),
    )(wp)


def _local_step(x, p, target, w, late_weights=None, early_grads=None, mid_grads=None, last_grad=None):
    t = x.shape[0]
    pb = p.astype(BF16)
    w = dict(w)

    h0, proj, gates = _proj_in(x, w["norm_mix_g"], w["w_in"])
    qk = _conv_silu_fwd(proj, w["conv_qk"])
    grow, sneg_row = _gates_fwd(gates[:, 0:8].T, w["b_if"].reshape(8, 1))
    gcol, sneg_col = grow.T, sneg_row.T
    hraw, ya, cs, st = _mlstm_fwd(qk, proj, grow, gcol, w["mlstm_norm_g"])
    yb, lse = _swa_fwd(proj, w["sinks"])
    if late_weights is not None:
        w.update(late_weights(yb))
    merged, za, zb = _branch_merge(ya, yb, w["w_branch_a"], w["w_branch_b"], proj)
    x1, hn1 = _resid_norm_mm(merged, w["w_out"], x, w["norm_mlp_g"], "mm_out_norm")
    act, u = _up_act(hn1, w["w_up"])
    x2, hn2 = _resid_norm_mm(act, w["w_down"], x1, w["norm_ple_g"], "mm_down_norm")
    pp = _mm(pb, w["w_ple_proj"], "nn", F32, "mm_ple_proj")
    loss, d_final_g, dx3, dpp, dgpre = _ple_final_mm(hn2, w["w_ple_gate"], x2, pp, target, w["final_norm_g"])

    g = {"final_norm_g": d_final_g}
    g["w_ple_proj"] = _mm(pb, dpp, "tn", F32, "mm_d_ple_proj", out_chunks=4)
    g["w_ple_gate"] = _mm(hn2, dgpre, "tn", F32, "mm_d_ple_gate")
    dx2, dx2b, g["norm_ple_g"] = _norm_bwd_mm(dgpre, w["w_ple_gate"], x2, w["norm_ple_g"], dx3, "mm_dhn2_norm")
    g["w_down"] = _mm(act, dx2b, "tn", F32, "mm_d_down")
    du = _da_du(dx2b, w["w_down"], u)
    g["w_up"] = _mm(hn1, du, "tn", F32, "mm_d_up", out_chunks=4)
    dx1, dx1b, g["norm_mlp_g"] = _norm_bwd_mm(du, w["w_up"], x1, w["norm_mlp_g"], dx2, "mm_dhn1_norm")
    g["w_out"] = _mm(merged, dx1b, "tn", F32, "mm_d_out")
    dza, dzb, dproj = _dmerged_bwd(dx1b, w["w_out"], proj, za, zb)
    g["w_branch_a"] = _mm(ya, dza, "tn", F32, "mm_d_branch_a")
    g["w_branch_b"] = _mm(yb, dzb, "tn", F32, "mm_d_branch_b")
    gain = w["mlstm_norm_g"] if early_grads is None else w["mlstm_norm_g"] + early_grads(g)
    dyb = _mm(dzb, w["w_branch_b"], "nt", F32, "mm_dyb")
    dhraw, dproj, g["mlstm_norm_g"] = _dya_bwd(dza, w["w_branch_a"], hraw, proj, gain, dproj)
    if mid_grads is not None:
        sneg_col = sneg_col + mid_grads(dhraw)
    dqk, dproj, dif, g["b_if"] = _mlstm_bwd(qk, proj, grow, gcol, sneg_col, cs, st, hraw, dhraw, dproj)
    dc, g["conv_qk"] = _conv_silu_bwd_a(proj, w["conv_qk"], dqk)
    dproj = _conv_silu_bwd_b(dc, w["conv_qk"], dproj)
    dproj, dkv_self, dkv_prev, g["sinks"] = _swa_bwd(proj, w["sinks"], lse, dyb, dproj)
    dproj = _kv_combine(dkv_self, dkv_prev, dif, dproj)
    g["w_in"] = _mm(h0, dproj, "tn", F32, "mm_d_in")
    gain = w["norm_mix_g"] if last_grad is None else w["norm_mix_g"] + last_grad(g)
    grad_x, _, g["norm_mix_g"] = _norm_bwd_mm(dproj, w["w_in"], x, gain, dx1, "mm_dh0_norm")
    return loss, grad_x, g


_W4 = ("w_branch_a", "w_branch_b", "w_out", "w_ple_gate")
_SHARDED_NAMES = ("w_in", "w_up", "w_down", "w_ple_proj", "conv_qk") + _W4
_SMALL_ROWS = 16
_CONV_ROW = 8


def _group(s):
    return [s["w_in"], jnp.concatenate([s[n] for n in _W4], axis=0), s["w_up"], s["w_down"], s["w_ple_proj"]]


def _ungroup(arrs):
    out = {"w_in": arrs[0], "w_up": arrs[2], "w_down": arrs[3], "w_ple_proj": arrs[4]}
    rows = arrs[1].shape[0] // len(_W4)
    for i, n in enumerate(_W4):
        out[n] = arrs[1][i * rows:(i + 1) * rows]
    return out


def _rows_tile(rows):
    return 256 if rows % 256 == 0 else rows


_SMALL = ("norm_mix_g", "mlstm_norm_g", "norm_mlp_g", "norm_ple_g", "final_norm_g")


def _pack_small(vals, extra=None, conv=None):
    rows = [vals[n].reshape(1, D) for n in _SMALL]
    tail = [vals["b_if"].reshape(1, 8), vals["sinks"].reshape(1, SWH)]
    used = 8 + SWH
    if extra is not None:
        tail.append(extra.reshape(1, 1))
        used += 1
    tail.append(jnp.zeros((1, D - used), F32))
    rows.append(jnp.concatenate(tail, axis=1))
    rows.append(jnp.zeros((_CONV_ROW - len(rows), D), F32))
    rows.append(jnp.zeros((CONV, D), F32) if conv is None else conv)
    rows.append(jnp.zeros((_SMALL_ROWS - _CONV_ROW - CONV, D), F32))
    return jnp.concatenate(rows, axis=0)


def _unpack_small(slab, shapes):
    out = {n: slab[i].reshape(shapes[n]) for i, n in enumerate(_SMALL)}
    out["b_if"] = slab[5, 0:8].reshape(shapes["b_if"])
    out["sinks"] = slab[5, 8:8 + SWH].reshape(shapes["sinks"])
    return out


_MESH = pl.DeviceIdType.MESH
_HBM = pl.BlockSpec(memory_space=pltpu.HBM)
_VMEM = pl.BlockSpec(memory_space=pltpu.VMEM)


def _place():
    x, y, c = lax.axis_index("x"), lax.axis_index("y"), lax.axis_index("c")
    return x, y, c, 2 * x + y


def _chip_peer(x, y, r):
    return (x ^ (r >> 1), y ^ (r & 1))


def _half(ref, which):
    h = ref.shape[-2] // 2
    return pl.ds(which * h, h)


def _allgather_weights(shards, conv):
    n = len(shards)

    def body(*refs):
        ins, conv_ref = refs[:n], refs[n]
        outs, conv_out = refs[n + 1:2 * n + 1], refs[2 * n + 1]
        send_a, recv_a, send_b, recv_b, send_c, recv_c, local_sems = refs[2 * n + 2:]
        x, y, c, j = _place()
        sibling = (x, y, 1 - c)
        local = [pltpu.make_async_copy(ins[k], outs[k].at[j], local_sems.at[k]) for k in range(n)]
        local.append(pltpu.make_async_copy(conv_ref, conv_out.at[j], local_sems.at[n]))
        for cp in local:
            cp.start()

        def copy_a(k, r, chip):
            rows = _half(ins[k], c)
            return pltpu.make_async_remote_copy(
                src_ref=ins[k].at[rows], dst_ref=outs[k].at[chip, rows], send_sem=send_a.at[3 * k + r - 1],
                recv_sem=recv_a.at[3 * k + r - 1], device_id=(*_chip_peer(x, y, r), c), device_id_type=_MESH)

        def copy_b(k, r, chip, which):
            rows = _half(ins[k], which)
            return pltpu.make_async_remote_copy(
                src_ref=outs[k].at[chip, rows], dst_ref=outs[k].at[chip, rows], send_sem=send_b.at[3 * k + r - 1],
                recv_sem=recv_b.at[3 * k + r - 1], device_id=sibling, device_id_type=_MESH)

        def copy_c(r, chip):
            return pltpu.make_async_remote_copy(
                src_ref=conv_ref, dst_ref=conv_out.at[chip], send_sem=send_c.at[r - 1],
                recv_sem=recv_c.at[r - 1], device_id=(*_chip_peer(x, y, r), c), device_id_type=_MESH)

        for k in range(n):
            for r in (1, 2, 3):
                copy_a(k, r, j).start()
        for r in (1, 2, 3):
            copy_c(r, j).start()
        for k in range(n):
            for r in (1, 2, 3):
                copy_a(k, r, j ^ r).wait_recv()
                copy_b(k, r, j ^ r, c).start()
        for k in range(n):
            for r in (1, 2, 3):
                copy_b(k, r, j ^ r, 1 - c).wait_recv()
        for r in (1, 2, 3):
            copy_c(r, j ^ r).wait_recv()
        for k in range(n):
            for r in (1, 2, 3):
                copy_a(k, r, j).wait_send()
                copy_b(k, r, j ^ r, c).wait_send()
        for r in (1, 2, 3):
            copy_c(r, j).wait_send()
        for cp in local:
            cp.wait()

    return pl.pallas_call(
        body, name="allgather_weights",
        out_shape=[jax.ShapeDtypeStruct((4,) + s.shape, s.dtype) for s in shards]
        + [jax.ShapeDtypeStruct((4,) + conv.shape, F32)],
        in_specs=[_HBM] * (n + 1), out_specs=[_HBM] * (n + 1),
        scratch_shapes=[pltpu.SemaphoreType.DMA((3 * n,))] * 4 + [pltpu.SemaphoreType.DMA((3,))] * 2
        + [pltpu.SemaphoreType.DMA((n + 1,))],
    )(*shards, conv)


_SEM = pl.BlockSpec(memory_space=pltpu.SEMAPHORE)
_DATAFLOW = pltpu.SideEffectType.DATAFLOW_SIDE_EFFECTING


def _late_peer_copy(src_ref, land_ref, send_sems, recv_sems, x, y, c, j, r, chip):
    return pltpu.make_async_remote_copy(
        src_ref=src_ref, dst_ref=land_ref.at[chip], send_sem=send_sems.at[r - 1], recv_sem=recv_sems.at[r - 1],
        device_id=(*_chip_peer(x, y, r), c), device_id_type=_MESH)


def _late_gather_start(rest):
    def body(rest_ref, land_ref, send_sems, recv_sems, rest_thru, land_thru, token):
        x, y, c, j = _place()
        for r in (1, 2, 3):
            _late_peer_copy(rest_ref, land_ref, send_sems, recv_sems, x, y, c, j, r, j).start()
        token[...] = jnp.zeros_like(token)

    j = 2 * lax.axis_index("x") + lax.axis_index("y")
    land = lax.dynamic_update_slice(lax.empty((4,) + rest.shape, rest.dtype), rest[None], (j, 0, 0))
    return pl.pallas_call(
        body, name="late_gather_start",
        out_shape=(pltpu.SemaphoreType.DMA((3,)), pltpu.SemaphoreType.DMA((3,)), pltpu.HBM(rest.shape, rest.dtype),
                   pltpu.HBM(land.shape, land.dtype), jax.ShapeDtypeStruct((8, 128), F32)),
        in_specs=(_HBM, _HBM), out_specs=(_SEM, _SEM, _HBM, _HBM, _VMEM), input_output_aliases={0: 2, 1: 3},
        compiler_params=pltpu.CompilerParams(has_side_effects=_DATAFLOW),
    )(pltpu.with_memory_space_constraint(rest, pltpu.HBM), pltpu.with_memory_space_constraint(land, pltpu.HBM))


def _late_gather_wait(send_sems, recv_sems, rest_thru, land_thru, after):
    def body(rest_ref, land_ref, send_sems, recv_sems, after_ref, rest_dead, got_ref):
        x, y, c, j = _place()
        for r in (1, 2, 3):
            cp = _late_peer_copy(rest_ref, land_ref, send_sems, recv_sems, x, y, c, j, r, j ^ r)
            cp.wait_send()
            cp.wait_recv()

    return pl.pallas_call(
        body, name="late_gather_wait",
        out_shape=(pltpu.HBM(rest_thru.shape, rest_thru.dtype), pltpu.HBM(land_thru.shape, land_thru.dtype)),
        in_specs=(_HBM, _HBM, _SEM, _SEM, _ANY), out_specs=(_HBM, _HBM), input_output_aliases={0: 0, 1: 1},
        compiler_params=pltpu.CompilerParams(has_side_effects=_DATAFLOW),
    )(rest_thru, land_thru, send_sems, recv_sems, after)[1]


def _pair_sum(g, theirs, j, c, name):
    _, h, cols = theirs.shape
    tr = _rows_tile(h)
    nb = h // tr

    def body(idx_ref, a_ref, b_ref, own_ref, ob_ref):
        s = a_ref[0] + b_ref[0]
        ob_ref[0] = s.astype(BF16)

        @pl.when(pl.program_id(1) == idx_ref[0])
        def _():
            own_ref[...] = s

    blk = pl.BlockSpec((1, tr, cols), lambda i, k, idx_ref: (k, i, 0))
    return pl.pallas_call(
        body, name=name,
        grid_spec=pltpu.PrefetchScalarGridSpec(
            num_scalar_prefetch=1, grid=(nb, 4),
            in_specs=[pl.BlockSpec((1, tr, cols), lambda i, k, idx_ref: (k, idx_ref[1] * nb + i, 0)), blk],
            out_specs=[pl.BlockSpec((tr, cols), lambda i, k, idx_ref: (i, 0)), blk]),
        out_shape=[jax.ShapeDtypeStruct((h, cols), F32), jax.ShapeDtypeStruct(theirs.shape, BF16)],
        compiler_params=_params(),
    )(jnp.stack([j, c]).astype(jnp.int32), g, theirs)


def _chip_copies(srcs, lands, send_sems, recv_sems):
    x, y, c, j = _place()
    return [pltpu.make_async_remote_copy(
        src_ref=srcs[k].at[j ^ r], dst_ref=lands[k].at[r - 1], send_sem=send_sems.at[3 * k + r - 1],
        recv_sem=recv_sems.at[3 * k + r - 1], device_id=(*_chip_peer(x, y, r), c), device_id_type=_MESH)
        for k in range(len(srcs)) for r in (1, 2, 3)]


def _pair_copies(srcs, lands, send_sems, recv_sems):
    x, y, c, _ = _place()
    return [pltpu.make_async_remote_copy(
        src_ref=srcs[k].at[:, _half(srcs[k], 1 - c)], dst_ref=lands[k], send_sem=send_sems.at[k],
        recv_sem=recv_sems.at[k], device_id=(x, y, 1 - c), device_id_type=_MESH) for k in range(len(srcs))]


def _split_start(name, srcs, lands, copies, n_sems):
    n = len(srcs)

    def body(*refs):
        for cp in copies(refs[:n], refs[n:2 * n], refs[2 * n], refs[2 * n + 1]):
            cp.start()
        refs[-1][...] = jnp.zeros_like(refs[-1])

    arrays = list(srcs) + list(lands)
    out = pl.pallas_call(
        body, name=name,
        out_shape=(pltpu.SemaphoreType.DMA((n_sems,)), pltpu.SemaphoreType.DMA((n_sems,)),
                   *[pltpu.HBM(a.shape, a.dtype) for a in arrays], jax.ShapeDtypeStruct((8, 128), F32)),
        in_specs=[_HBM] * (2 * n), out_specs=(_SEM, _SEM, *([_HBM] * (2 * n)), _VMEM),
        input_output_aliases={k: 2 + k for k in range(2 * n)},
        compiler_params=pltpu.CompilerParams(has_side_effects=_DATAFLOW),
    )(*[pltpu.with_memory_space_constraint(a, pltpu.HBM) for a in arrays])
    return out[0], out[1], list(out[2:2 + n]), list(out[2 + n:2 + 2 * n]), out[-1]


def _split_wait(name, send_sems, recv_sems, srcs_thru, lands_thru, after, copies):
    n = len(srcs_thru)

    def body(*refs):
        for cp in copies(refs[:n], refs[n:2 * n], refs[2 * n], refs[2 * n + 1]):
            cp.wait_send()
            cp.wait_recv()

    arrays = list(srcs_thru) + list(lands_thru)
    out = pl.pallas_call(
        body, name=name, out_shape=tuple(pltpu.HBM(a.shape, a.dtype) for a in arrays),
        in_specs=[_HBM] * (2 * n) + [_SEM, _SEM, _ANY], out_specs=tuple([_HBM] * (2 * n)),
        input_output_aliases={k: k for k in range(2 * n)},
        compiler_params=pltpu.CompilerParams(has_side_effects=_DATAFLOW),
    )(*arrays, send_sems, recv_sems, after)
    return list(out[:n]), list(out[n:])


def _chip_exchange_start(ss, tag):
    lands = [lax.empty((3,) + s.shape[1:], s.dtype) for s in ss]
    return _split_start("chip_exchange_start_" + tag, ss, lands, _chip_copies, 3 * len(ss))


def _chip_exchange_wait(send_sems, recv_sems, ss_thru, lands_thru, after, tag):
    return _split_wait("chip_exchange_wait_" + tag, send_sems, recv_sems, ss_thru, lands_thru, after, _chip_copies)[1]


def _pair_exchange_start(gs, tag):
    lands = [lax.empty((4, g.shape[1] // 2, g.shape[2]), g.dtype) for g in gs]
    return _split_start("pair_exchange_start_" + tag, gs, lands, _pair_copies, len(gs))


def _pair_exchange_wait(send_sems, recv_sems, gs_thru, lands_thru, after, tag):
    return _split_wait("pair_exchange_wait_" + tag, send_sems, recv_sems, gs_thru, lands_thru, after, _pair_copies)


def _reduce4(own, others, c, name):
    h, cols = own.shape
    tr = _rows_tile(h)
    nb = h // tr

    def body(c_ref, s_ref, a0, a1, a2, o_ref):
        o_ref[...] = ((s_ref[...] + a0[0].astype(F32)) + a1[0].astype(F32)) + a2[0].astype(F32)

    def other(r):
        return pl.BlockSpec((1, tr, cols), lambda i, c_ref: (r, i, 0))

    return pl.pallas_call(
        body, name=name,
        grid_spec=pltpu.PrefetchScalarGridSpec(
            num_scalar_prefetch=1, grid=(nb,),
            in_specs=[pl.BlockSpec((tr, cols), lambda i, c_ref: (i, 0)), other(0), other(1), other(2)],
            out_specs=pl.BlockSpec((tr, cols), lambda i, c_ref: (c_ref[0] * nb + i, 0))),
        out_shape=jax.ShapeDtypeStruct((2 * h, cols), F32), compiler_params=_params(),
    )(c.reshape(1).astype(jnp.int32), own, others, others, others)


def _sibling_share(fulls, name):
    n = len(fulls)

    def body(*refs):
        outs, send_sems, recv_sems = refs[n:2 * n], refs[2 * n], refs[2 * n + 1]
        x, y, c, _ = _place()
        cps = [pltpu.make_async_remote_copy(
            src_ref=outs[k].at[_half(outs[k], c)], dst_ref=outs[k].at[_half(outs[k], c)], send_sem=send_sems.at[k],
            recv_sem=recv_sems.at[k], device_id=(x, y, 1 - c), device_id_type=_MESH) for k in range(n)]
        for cp in cps:
            cp.start()
        for cp in cps:
            cp.wait()

    return pl.pallas_call(
        body, name=name, out_shape=[jax.ShapeDtypeStruct(f.shape, F32) for f in fulls],
        in_specs=[_HBM] * n, out_specs=[_HBM] * n, input_output_aliases={k: k for k in range(n)},
        scratch_shapes=[pltpu.SemaphoreType.DMA((n,))] * 2,
    )(*fulls)


def _adamw(w, g, m, v):
    m1 = ADAM_B1 * m + (1.0 - ADAM_B1) * g
    v1 = ADAM_B2 * v + (1.0 - ADAM_B2) * (g * g)
    m_hat = m1 / (1.0 - ADAM_B1 ** ADAM_STEP)
    v_hat = v1 / (1.0 - ADAM_B2 ** ADAM_STEP)
    delta = -ADAM_LR * (m_hat / (jnp.sqrt(v_hat) + ADAM_EPS) + ADAM_WD * w)
    return delta, m1, v1


def _adamw_call(w, g, m, v, name):
    rows, cols = w.shape

    def body(w_ref, g_ref, m_ref, v_ref, d_out, m_out, v_out):
        delta, m1, v1 = _adamw(w_ref[...], g_ref[...], m_ref[...], v_ref[...])
        d_out[...] = delta
        m_out[...] = m1
        v_out[...] = v1

    if rows % 8 == 0:
        tr = _rows_tile(rows)
        blk, grid = pl.BlockSpec((tr, cols), lambda i: (i, 0)), (rows // tr,)
    else:
        blk, grid = pl.BlockSpec((rows, 128), lambda i: (0, i)), (cols // 128,)
    return pl.pallas_call(
        body, name=name, grid=grid, in_specs=[blk] * 4, out_specs=[blk] * 3,
        out_shape=[jax.ShapeDtypeStruct((rows, cols), F32)] * 3, compiler_params=_params(),
    )(w, g, m, v)


def _small_allreduce(vals):
    def body(v_ref, out_ref, buf, send_sems, recv_sems):
        x, y, c, j = _place()
        me = 2 * j + c
        buf[0] = v_ref[...]

        def copy(r):
            return pltpu.make_async_remote_copy(
                src_ref=v_ref, dst_ref=buf.at[r], send_sem=send_sems.at[r - 1], recv_sem=recv_sems.at[r - 1],
                device_id=(x ^ (r >> 2), y ^ ((r >> 1) & 1), c ^ (r & 1)), device_id_type=_MESH)

        for r in range(1, 8):
            copy(r).start()
        for r in range(1, 8):
            copy(r).wait()
        acc = buf[me ^ 0]
        for d in range(1, 8):
            acc = acc + buf[me ^ d]
        out_ref[...] = acc

    return pl.pallas_call(
        body, name="small_allreduce", out_shape=jax.ShapeDtypeStruct((_SMALL_ROWS, D), F32),
        in_specs=[_VMEM], out_specs=_VMEM,
        scratch_shapes=[pltpu.VMEM((8, _SMALL_ROWS, D), F32), pltpu.SemaphoreType.DMA((7,)),
                        pltpu.SemaphoreType.DMA((7,))],
    )(vals)


_NAMES = ("norm_mix_g", "w_in", "conv_qk", "b_if", "mlstm_norm_g", "sinks", "w_branch_a", "w_branch_b", "w_out",
          "norm_mlp_g", "w_up", "w_down", "norm_ple_g", "w_ple_gate", "w_ple_proj", "final_norm_g")
_GROUP_NAMES = ("w_in", "w4", "w_up", "w_down", "w_ple_proj")


def _step(x, p, target, w, m, v):
    c = lax.axis_index("c")
    j = 2 * lax.axis_index("x") + lax.axis_index("y")

    def shards(d):
        return {n: d[n][0] for n in _SHARDED_NAMES}

    ws = shards(w)
    w_in_all, conv_all = _allgather_weights([ws["w_in"].astype(BF16)], ws["conv_qk"])
    rows_pp = PLE * (D // 4) // D
    rest = jnp.concatenate([ws[n] for n in _W4] + [ws["w_up"], ws["w_down"], ws["w_ple_proj"].reshape(rows_pp, D)],
                           axis=0)
    rest = (rest + 0.0 * conv_all[0, 0, 0]).astype(BF16)
    send_sems, recv_sems, rest_thru, land_thru, token = _late_gather_start(rest)
    full = {n: w[n] for n in ("mlstm_norm_g", "norm_mlp_g", "norm_ple_g", "b_if", "sinks")}
    full["norm_mix_g"] = w["norm_mix_g"] + token[0, 0]
    full["final_norm_g"] = w["final_norm_g"].reshape(1, D)
    full["w_in"] = _win_pad(w_in_all)
    full["conv_qk"] = jnp.swapaxes(conv_all, 0, 1).reshape(CONV, D)

    def late_weights(after):
        land = _late_gather_wait(send_sems, recv_sems, rest_thru, land_thru, after)
        out = {n: land[:, i * (D // 4):(i + 1) * (D // 4)].reshape(D, D) for i, n in enumerate(_W4)}
        out["w_up"] = land[:, D:2 * D]
        out["w_down"] = land[:, 2 * D:3 * D].reshape(DFF, D)
        out["w_ple_proj"] = jnp.swapaxes(land[:, 3 * D:3 * D + rows_pp].reshape(4, PLE, D // 4), 0, 1).reshape(PLE, D)
        return out

    early, last = {}, {}

    def pair_sums(by_dest, theirs, names):
        return [_pair_sum(a, b, j, c, "pair_sum_" + n) for a, b, n in zip(by_dest, theirs, names)]

    def early_grads(g):
        by_dest = [jnp.stack([g[n].reshape(4, D // 4, D) for n in _W4], axis=1).reshape(4, D, D),
                   g["w_up"], g["w_down"].reshape(4, DFF // 4, D), g["w_ple_proj"]]
        *early["pair"], token = _pair_exchange_start(by_dest, "early")
        return token[0, 0]

    def mid_grads(after):
        early["sums"] = pair_sums(*_pair_exchange_wait(*early["pair"], after, "early"), _GROUP_NAMES[1:])
        *early["flight"], token = _chip_exchange_start([s[1] for s in early["sums"]], "early")
        return token[0, 0]

    def last_grad(g):
        *last["pair"], token = _pair_exchange_start([_win_unpad(g["w_in"])], "w_in")
        return token[0, 0]

    loss, grad_x, g = _local_step(x[0], p[0, 0], target[0], full, late_weights, early_grads, mid_grads, last_grad)

    last["sums"] = pair_sums(*_pair_exchange_wait(*last["pair"], grad_x, "w_in"), _GROUP_NAMES[:1])
    *last["flight"], token = _chip_exchange_start([s[1] for s in last["sums"]], "w_in")

    def reduce_share(sums, others, names, tag):
        halves = [_reduce4(s[0], b, c, "reduce4_" + n) for s, b, n in zip(sums, others, names)]
        return list(_sibling_share(halves, "sibling_share_" + tag))

    ms, vs = shards(m), shards(v)
    grads = reduce_share(early["sums"], _chip_exchange_wait(*early["flight"], token, "early"), _GROUP_NAMES[1:], "early")
    upd = [_adamw_call(wa, ga, ma, va, "adamw_" + n)
           for wa, ga, ma, va, n in zip(_group(ws)[1:], grads, _group(ms)[1:], _group(vs)[1:], _GROUP_NAMES[1:])]
    small_g = _small_allreduce(_pack_small(g, extra=loss, conv=g["conv_qk"]))
    conv_g = lax.dynamic_slice(small_g[_CONV_ROW:_CONV_ROW + CONV], (0, j * (D // 4)), (CONV, D // 4))
    conv_upd = _adamw_call(ws["conv_qk"], conv_g, ms["conv_qk"], vs["conv_qk"], "adamw_conv")
    small_upd = _adamw_call(_pack_small(w), small_g, _pack_small(m), _pack_small(v), "adamw_small")

    done = sum(a[0][0:1, 0:1] for a in upd + [conv_upd, small_upd])
    others = _chip_exchange_wait(*last["flight"], done, "w_in")
    grads = reduce_share(last["sums"], others, _GROUP_NAMES[:1], "w_in") + list(grads)
    upd_in = _adamw_call(*[jnp.swapaxes(a, 0, 1) for a in (ws["w_in"], grads[0], ms["w_in"], vs["w_in"])], "adamw_w_in")
    upd = [[jnp.swapaxes(a, 0, 1) for a in upd_in]] + upd

    shapes = {n: w[n].shape for n in _NAMES}
    res = []
    for k in range(4):
        big = _ungroup(list(grads) if k == 0 else [u[k - 1] for u in upd])
        big["conv_qk"] = conv_g if k == 0 else conv_upd[k - 1]
        leaves = _unpack_small(small_g if k == 0 else small_upd[k - 1], shapes)
        leaves.update({n: a.reshape(shapes[n]) for n, a in big.items()})
        res.append(leaves)

    out = [small_g[5, 8 + SWH], grad_x[None]]
    for k in range(4):
        out += [res[k][n] for n in _NAMES]
    return tuple(out)


def kernel(x, p, norm_mix_g, w_in, conv_qk, b_if, mlstm_norm_g, sinks, w_branch_a, w_branch_b, w_out, norm_mlp_g, w_up, w_down, norm_ple_g, w_ple_gate, w_ple_proj, final_norm_g, loss_target, m_norm_mix_g, m_w_in, m_conv_qk, m_b_if, m_mlstm_norm_g, m_sinks, m_w_branch_a, m_w_branch_b, m_w_out, m_norm_mlp_g, m_w_up, m_w_down, m_norm_ple_g, m_w_ple_gate, m_w_ple_proj, m_final_norm_g, v_norm_mix_g, v_w_in, v_conv_qk, v_b_if, v_mlstm_norm_g, v_sinks, v_w_branch_a, v_w_branch_b, v_w_out, v_norm_mlp_g, v_w_up, v_w_down, v_norm_ple_g, v_w_ple_gate, v_w_ple_proj, v_final_norm_g):
    w = dict(zip(_NAMES, (norm_mix_g, w_in, conv_qk, b_if, mlstm_norm_g, sinks, w_branch_a, w_branch_b, w_out,
                          norm_mlp_g, w_up, w_down, norm_ple_g, w_ple_gate, w_ple_proj, final_norm_g)))
    m = dict(zip(_NAMES, (m_norm_mix_g, m_w_in, m_conv_qk, m_b_if, m_mlstm_norm_g, m_sinks, m_w_branch_a,
                          m_w_branch_b, m_w_out, m_norm_mlp_g, m_w_up, m_w_down, m_norm_ple_g, m_w_ple_gate,
                          m_w_ple_proj, m_final_norm_g)))
    v = dict(zip(_NAMES, (v_norm_mix_g, v_w_in, v_conv_qk, v_b_if, v_mlstm_norm_g, v_sinks, v_w_branch_a,
                          v_w_branch_b, v_w_out, v_norm_mlp_g, v_w_up, v_w_down, v_norm_ple_g, v_w_ple_gate,
                          v_w_ple_proj, v_final_norm_g)))
    return _step(x, p, loss_target, w, m, v)
```

```python
import jax
import jax.numpy as jnp
from jax import lax
from jax.experimental import pallas as pl
from jax.experimental.pallas import tpu as pltpu

F32 = jnp.float32
BF16 = jnp.bfloat16

D = 1024
PLE = 256
MLH = 4
DQK = 128
DV = 256
CONV = 4
CHUNK = 256
SWH = 16
SWKV = 4
SWG = SWH // SWKV
HD = 64
WIN = 128
DFF = 4096
EPS = 1e-6
N_IN = 6664
NP = 7168
C_QK, C_V, C_O, C_QSW, C_GA, C_GB, C_KV, C_IF = 0, 1024, 2048, 3072, 4096, 5120, 6144, 6656
IFW = NP - C_IF

ADAM_LR = 0.001
ADAM_B1 = 0.9
ADAM_B2 = 0.999
ADAM_EPS = 1e-08
ADAM_WD = 0.01
ADAM_STEP = 10

TOK_TILE = 512
V7X_VMEM_BYTES = 64 * 1024 * 1024
VMEM_LIMIT = V7X_VMEM_BYTES - 6 * 1024 * 1024


def _params(**kw):
    return pltpu.CompilerParams(vmem_limit_bytes=VMEM_LIMIT, **kw)


def _pick(n, cap):
    if n <= cap:
        return n
    t = cap - cap % 128
    while t > 128 and n % t:
        t -= 128
    assert n % t == 0, (n, cap)
    return t


def _dot(a, b, dims):
    return lax.dot_general(a, b, (dims, ((), ())), preferred_element_type=F32)


def _dot_nn(a, b):
    return _dot(a, b, ((1,), (0,)))


def _dot_nt(a, b):
    return _dot(a, b, ((1,), (1,)))


def _dot_tn(a, b):
    return _dot(a, b, ((0,), (0,)))


def _sigmoid(x):
    return 1.0 / (1.0 + jnp.exp(-x))


def _mm(a, b, mode, out_dtype, name, out_chunks=1):
    if mode == "nn":
        (m, k), (k2, n) = a.shape, b.shape
    elif mode == "nt":
        (m, k), (n, k2) = a.shape, b.shape
    else:
        (k, m), (k2, n) = a.shape, b.shape
    assert k == k2, (a.shape, b.shape, mode)
    tm, tn, tk = _pick(m, 1024), _pick(n // out_chunks, 1024), _pick(k, 2048)
    nk = k // tk
    if mode == "nn":
        a_spec = pl.BlockSpec((tm, tk), lambda i, j, kk: (i, kk))
        b_spec = pl.BlockSpec((tk, tn), lambda i, j, kk: (kk, j))
        dot = _dot_nn
    elif mode == "nt":
        a_spec = pl.BlockSpec((tm, tk), lambda i, j, kk: (i, kk))
        b_spec = pl.BlockSpec((tn, tk), lambda i, j, kk: (j, kk))
        dot = _dot_nt
    else:
        a_spec = pl.BlockSpec((tk, tm), lambda i, j, kk: (kk, i))
        b_spec = pl.BlockSpec((tk, tn), lambda i, j, kk: (kk, j))
        dot = _dot_tn
    if out_chunks > 1:
        npc = (n // out_chunks) // tn
        out_spec = pl.BlockSpec((None, tm, tn), lambda i, j, kk: (j // npc, i, j % npc))
        out_shape = jax.ShapeDtypeStruct((out_chunks, m, n // out_chunks), out_dtype)
    else:
        out_spec = pl.BlockSpec((tm, tn), lambda i, j, kk: (i, j))
        out_shape = jax.ShapeDtypeStruct((m, n), out_dtype)

    def body(a_ref, b_ref, o_ref, acc_ref):
        kk = pl.program_id(2)

        @pl.when(kk == 0)
        def _():
            acc_ref[...] = jnp.zeros_like(acc_ref)

        acc_ref[...] += dot(a_ref[...], b_ref[...])

        @pl.when(kk == nk - 1)
        def _():
            o_ref[...] = acc_ref[...].astype(out_dtype)

    return pl.pallas_call(
        body, name=name, grid=(m // tm, n // tn, nk),
        in_specs=[a_spec, b_spec], out_specs=out_spec, out_shape=out_shape,
        scratch_shapes=[pltpu.VMEM((tm, tn), F32)],
        compiler_params=_params(dimension_semantics=("parallel", "parallel", "arbitrary")),
    )(a, b)


def _tile(col0=0):
    return lambda tm, tn: pl.BlockSpec((tm, tn), lambda i, j, kk: (i, col0 // tn + j))


def _row():
    return lambda tm, tn: pl.BlockSpec((1, tn), lambda i, j, kk: (0, j))


def _mm_ep(pairs, mode, name, epilogue, ins, outs, tm, tn, aliases=None, row_split=1, init=None):
    a0, b0 = pairs[0]
    bch = b0.shape[0] if b0.ndim == 3 else 1
    m, k = a0.shape
    tm = _pick(m, tm)
    n = b0.shape[-1] * bch if mode == "nn" else b0.shape[-2]
    tk = _pick(k // bch if mode == "nt" else k, 2048)
    nk = k // tk
    a_spec = pl.BlockSpec((tm, tk), lambda i, j, kk: (i, kk))
    if mode == "nn":
        dot = _dot_nn
        if bch > 1:
            bpc = (n // bch) // tn
            b_spec = pl.BlockSpec((None, tk, tn), lambda i, j, kk: (j // bpc, kk, j % bpc))
        else:
            b_spec = pl.BlockSpec((tk, tn), lambda i, j, kk: (kk, j))
    else:
        dot = _dot_nt
        if bch > 1:
            bpc = (k // bch) // tk
            b_spec = pl.BlockSpec((None, tn, tk), lambda i, j, kk: (kk // bpc, j, kk % bpc))
        else:
            b_spec = pl.BlockSpec((tn, tk), lambda i, j, kk: (j, kk))
    npair, nin, nout = len(pairs), len(ins), len(outs)
    rows = tm // row_split
    assert init is None or row_split > 1

    def body_split(*refs):
        ab = refs[:2 * npair]
        in_refs = refs[2 * npair:2 * npair + nin]
        out_refs = refs[2 * npair + nin:2 * npair + nin + nout]
        accs = refs[2 * npair + nin + nout:]
        i, j, kk = pl.program_id(0), pl.program_id(1), pl.program_id(2)

        if init is not None:
            @pl.when((i == 0) & (kk == 0))
            def _():
                init(out_refs)

        @pl.when(kk < nk - 1)
        def _():
            for p in range(npair):
                prod = dot(ab[2 * p][...], ab[2 * p + 1][...])

                @pl.when(kk == 0)
                def _():
                    accs[p][...] = prod

                @pl.when(kk > 0)
                def _():
                    accs[p][...] += prod

        @pl.when(kk == nk - 1)
        def _():
            for r in range(row_split):
                rs = pl.ds(r * rows, rows)
                tot = []
                for p in range(npair):
                    prod = dot(ab[2 * p][rs, :], ab[2 * p + 1][...])
                    tot.append(prod if nk == 1 else accs[p][rs, :] + prod)

                def view(ref):
                    return ref.at[rs] if ref.shape[0] == tm else ref

                epilogue(tot, [view(x) for x in in_refs], [view(x) for x in out_refs], i * row_split + r, j)

    def body(*refs):
        ab = refs[:2 * npair]
        in_refs = refs[2 * npair:2 * npair + nin]
        out_refs = refs[2 * npair + nin:2 * npair + nin + nout]
        accs = refs[2 * npair + nin + nout:]
        i, j, kk = pl.program_id(0), pl.program_id(1), pl.program_id(2)
        for p in range(npair):
            prod = dot(ab[2 * p][...], ab[2 * p + 1][...])

            @pl.when(kk == 0)
            def _():
                accs[p][...] = prod

            @pl.when(kk > 0)
            def _():
                accs[p][...] += prod

        @pl.when(kk == nk - 1)
        def _():
            epilogue([acc[...] for acc in accs], in_refs, out_refs, i, j)

    operands = [x for pair in pairs for x in pair] + [a for a, _ in ins]
    io_alias = {2 * npair + i: o for i, o in (aliases or {}).items()}
    return pl.pallas_call(
        body if row_split == 1 else body_split, name=name, grid=(m // tm, n // tn, nk),
        in_specs=[a_spec, b_spec] * npair + [mk(tm, tn) for _, mk in ins],
        out_specs=[mk(tm, tn) for _, mk in outs], out_shape=[s for s, _ in outs],
        scratch_shapes=[pltpu.VMEM((tm, tn), F32)] * npair, input_output_aliases=io_alias,
        compiler_params=_params(dimension_semantics=("arbitrary", "arbitrary", "arbitrary")),
    )(*operands)


def _tok(w, j=0):
    return pl.BlockSpec((TOK_TILE, w), lambda i: (i, j))


def _rep(shape):
    return pl.BlockSpec(shape, lambda i: (0,) * len(shape))


def _rms(x):
    rstd = lax.rsqrt(jnp.mean(x * x, axis=-1, keepdims=True) + EPS)
    return x * rstd, rstd


def _rms_bwd(xn, rstd, dxn):
    return rstd * (dxn - xn * jnp.mean(dxn * xn, axis=-1, keepdims=True))


def _halo_prev(w, j=0, rows=8):
    r = TOK_TILE // rows
    return pl.BlockSpec((rows, w), lambda i: (jnp.maximum(i * r - 1, 0), j))


def _last8(halo_ref):
    return halo_ref[...].astype(F32)[halo_ref.shape[0] - 8:]


def _halo_next(w, nt, j=0):
    r = TOK_TILE // 8
    return pl.BlockSpec((8, w), lambda i: (jnp.minimum((i + 1) * r, nt * r - 1), j))


def _shift_down(x, halo, s):
    if s == 0:
        return x
    r = pltpu.roll(x, s, 0)
    hs = pltpu.roll(halo, s, 0)
    row = lax.broadcasted_iota(jnp.int32, hs.shape, 0)
    top = jnp.where(row < s, hs, r[0:8])
    return jnp.concatenate([top, r[8:]], axis=0)


def _shift_up(x, halo, s):
    if s == 0:
        return x
    n = x.shape[0]
    r = pltpu.roll(x, n - s, 0)
    hs = pltpu.roll(halo, 8 - s, 0)
    row = lax.broadcasted_iota(jnp.int32, hs.shape, 0)
    bot = jnp.where(row >= 8 - s, hs, r[n - 8:])
    return jnp.concatenate([r[:n - 8], bot], axis=0)


def _bf(x):
    return x.astype(BF16).astype(F32)


def _conv_taps(x, halo, w):
    x, halo, w = _bf(x), _bf(halo), _bf(w)
    acc = x * w[CONV - 1:CONV, :]
    for j in range(CONV - 1):
        acc = acc + _shift_down(x, halo, CONV - 1 - j) * w[j:j + 1, :]
    return acc


_Q_SCALE = DQK ** -0.5


def _qscale_row():
    lane = lax.broadcasted_iota(jnp.int32, (1, D), 1)
    return jnp.where(lane < MLH * DQK, _Q_SCALE, 1.0).astype(F32)


def _conv_silu_fwd(proj, conv_w):
    t = proj.shape[0]

    def body(x_ref, halo_ref, w_ref, o_ref):
        halo = jnp.where(pl.program_id(0) > 0, _last8(halo_ref), 0.0)
        c = _conv_taps(x_ref[...].astype(F32), halo, w_ref[...])
        o_ref[...] = (c * _sigmoid(c) * _qscale_row()).astype(BF16)

    return pl.pallas_call(
        body, name="conv_silu_fwd", grid=(t // TOK_TILE,),
        in_specs=[_tok(D, C_QK // D), _halo_prev(D, C_QK // D, 16), _rep((CONV, D))], out_specs=_tok(D),
        out_shape=jax.ShapeDtypeStruct((t, D), BF16), compiler_params=_params(),
    )(proj, proj, conv_w)


def _conv_silu_bwd_a(proj, conv_w, dqk):
    t = proj.shape[0]

    def body(x_ref, halo_ref, w_ref, d_ref, dc_ref, dw_ref):
        @pl.when(pl.program_id(0) == 0)
        def _():
            dw_ref[...] = jnp.zeros_like(dw_ref)

        halo = jnp.where(pl.program_id(0) > 0, _last8(halo_ref), 0.0)
        x = x_ref[...].astype(F32)
        c = _conv_taps(x, halo, w_ref[...])
        s = _sigmoid(c)
        dc = d_ref[...] * _qscale_row() * (s * (1.0 + c * (1.0 - s)))
        dc_ref[...] = dc
        dcb, xb, halo_b = _bf(dc), _bf(x), _bf(halo)
        for j in range(CONV):
            dw_ref[j:j + 1, :] += jnp.sum(dcb * _shift_down(xb, halo_b, CONV - 1 - j), axis=0, keepdims=True)

    return pl.pallas_call(
        body, name="conv_silu_bwd_a", grid=(t // TOK_TILE,),
        in_specs=[_tok(D, C_QK // D), _halo_prev(D, C_QK // D, 16), _rep((CONV, D)), _tok(D)],
        out_specs=[_tok(D), _rep((CONV, D))],
        out_shape=[jax.ShapeDtypeStruct((t, D), F32), jax.ShapeDtypeStruct((CONV, D), F32)],
        compiler_params=_params(),
    )(proj, proj, conv_w, dqk)


def _conv_silu_bwd_b(dc, conv_w, dproj):
    t = dc.shape[0]
    nt = t // TOK_TILE

    def body(dc_ref, halo_ref, w_ref, _, dx_ref):
        halo = _bf(jnp.where(pl.program_id(0) < nt - 1, halo_ref[...], 0.0))
        dcv = _bf(dc_ref[...])
        w = _bf(w_ref[...])
        acc = dcv * w[CONV - 1:CONV, :]
        for j in range(CONV - 1):
            acc = acc + _shift_up(dcv, halo, CONV - 1 - j) * w[j:j + 1, :]
        dx_ref[...] = acc.astype(BF16)

    return pl.pallas_call(
        body, name="conv_silu_bwd_b", grid=(nt,), in_specs=[_tok(D), _halo_next(D, nt), _rep((CONV, D)), _ANY],
        out_specs=_tok(D, C_QK // D), out_shape=jax.ShapeDtypeStruct((t, NP), BF16),
        input_output_aliases={3: 0}, compiler_params=_params(),
    )(dc, dc, conv_w, dproj)


def _gates_fwd(pre_rows, bias_col):
    t = pre_rows.shape[1]

    def body(p_ref, b_ref, g_ref, s_ref):
        z = p_ref[...] + b_ref[...]
        lf = jnp.minimum(z, 0.0) - jnp.log(1.0 + jnp.exp(-jnp.abs(z)))
        lane = lax.broadcasted_iota(jnp.int32, z.shape, 1) % CHUNK
        cum = lf
        s = 1
        while s < CHUNK:
            cum = cum + jnp.where(lane >= s, pltpu.roll(cum, s, 1), 0.0)
            s *= 2
        sub = lax.broadcasted_iota(jnp.int32, z.shape, 0)
        g_ref[...] = jnp.where(sub < MLH, z, cum)
        s_ref[...] = _sigmoid(-z)

    return pl.pallas_call(
        body, name="gates_fwd",
        out_shape=[jax.ShapeDtypeStruct((8, t), F32), jax.ShapeDtypeStruct((8, t), F32)],
        compiler_params=_params(),
    )(pre_rows, bias_col)


def _chunk_terms(grow, gcol, m0):
    heads = range(MLH)
    i_row = [grow[h:h + 1, :] for h in heads]
    b_row = [grow[MLH + h:MLH + h + 1, :] for h in heads]
    i_col = [gcol[:, h:h + 1] for h in heads]
    b_col = [gcol[:, MLH + h:MLH + h + 1] for h in heads]
    b_last = [b_row[h][:, CHUNK - 1:CHUNK] for h in heads]
    tt = lax.broadcasted_iota(jnp.int32, (CHUNK, CHUNK), 0)
    ss = lax.broadcasted_iota(jnp.int32, (CHUNK, CHUNK), 1)
    log_d = [jnp.where(tt >= ss, b_col[h] - b_row[h] + i_row[h], -jnp.inf) for h in heads]
    row_max = [jnp.max(log_d[h], axis=1, keepdims=True) for h in heads]
    last_max = [jnp.max(b_last[h] - b_row[h] + i_row[h], axis=1, keepdims=True) for h in heads]
    m_t = [jnp.maximum(b_col[h] + m0[h], row_max[h]) for h in heads]
    m1 = [jnp.maximum(b_last[h] + m0[h], last_max[h]) for h in heads]
    dm = [jnp.exp(log_d[h] - m_t[h]) for h in heads]
    wi = [jnp.exp(b_col[h] + m0[h] - m_t[h]) for h in heads]
    ws = [jnp.exp(b_last[h] - b_col[h] + i_col[h] - m1[h]) for h in heads]
    dec = [jnp.exp(b_last[h] + m0[h] - m1[h]) for h in heads]
    return [(dm[h], wi[h], m_t[h], ws[h], dec[h], m1[h]) for h in heads]


def _mlstm_fwd(qk, proj, grow, gcol, gain):
    t = qk.shape[0]
    nc = t // CHUNK

    def body(qk_ref, v_ref, o_ref, grow_ref, gcol_ref, g_ref, h_ref, y_ref, cs_ref, st_ref, c_scr, st_scr):
        @pl.when(pl.program_id(0) == 0)
        def _():
            c_scr[...] = jnp.zeros_like(c_scr)
            st_scr[...] = jnp.zeros_like(st_scr)

        grow_v, gcol_v = grow_ref[...], gcol_ref[...]
        heads = range(MLH)
        q = [qk_ref[:, h * DQK:(h + 1) * DQK] for h in heads]
        k = [qk_ref[:, MLH * DQK + h * DQK:MLH * DQK + (h + 1) * DQK] for h in heads]
        v = [v_ref[:, h * DV:(h + 1) * DV] for h in heads]
        c0 = [c_scr[h] for h in heads]
        n0 = [st_scr[h, 0:1, :] for h in heads]
        for h in heads:
            cs_ref[0, h] = c0[h]
            st_ref[0, h] = st_scr[h]
        terms = _chunk_terms(grow_v, gcol_v, [st_scr[h, 1:2, 0:1] for h in heads])
        a = [_dot_nt(q[h], k[h]) for h in heads]
        qc = [_dot_nt(q[h], c0[h].astype(BF16)) for h in heads]
        s = [a[h] * terms[h][0] for h in heads]
        sv = [_dot_nn(s[h].astype(BF16), v[h]) for h in heads]
        upd = [_dot_tn((terms[h][3] * v[h]).astype(BF16), k[h]) for h in heads]
        den = [terms[h][1] * jnp.sum(q[h].astype(F32) * n0[h], axis=1, keepdims=True)
               + jnp.sum(s[h], axis=1, keepdims=True) for h in heads]
        hv = [(terms[h][1] * qc[h] + sv[h]) / jnp.maximum(jnp.abs(den[h]), jnp.exp(-terms[h][2])) for h in heads]
        for h in heads:
            sl = slice(h * DV, (h + 1) * DV)
            h_ref[:, sl] = hv[h]
            xn, _ = _rms(hv[h])
            y_ref[:, sl] = (_sigmoid(o_ref[:, sl].astype(F32)) * xn * g_ref[:, sl]).astype(BF16)
        for h in heads:
            dec, m1 = terms[h][4], terms[h][5]
            c_scr[h] = dec * c0[h] + upd[h]
            st_scr[h, 0:1, :] = dec * n0[h] + jnp.sum(terms[h][3] * k[h].astype(F32), axis=0, keepdims=True)
            st_scr[h, 1:2, :] = jnp.broadcast_to(m1, (1, DQK))

    return pl.pallas_call(
        body, name="mlstm_fwd", grid=(nc,),
        in_specs=[pl.BlockSpec((CHUNK, D), lambda c: (c, 0)), pl.BlockSpec((CHUNK, D), lambda c: (c, C_V // D)),
                  pl.BlockSpec((CHUNK, D), lambda c: (c, C_O // D)),
                  pl.BlockSpec((8, CHUNK), lambda c: (0, c)), pl.BlockSpec((CHUNK, 8), lambda c: (c, 0)),
                  pl.BlockSpec((1, D), lambda c: (0, 0))],
        out_specs=[pl.BlockSpec((CHUNK, D), lambda c: (c, 0)), pl.BlockSpec((CHUNK, D), lambda c: (c, 0)),
                   pl.BlockSpec((1, MLH, DV, DQK), lambda c: (c, 0, 0, 0)),
                   pl.BlockSpec((1, MLH, 8, DQK), lambda c: (c, 0, 0, 0))],
        out_shape=[jax.ShapeDtypeStruct((t, D), F32), jax.ShapeDtypeStruct((t, D), BF16),
                   jax.ShapeDtypeStruct((nc, MLH, DV, DQK), F32), jax.ShapeDtypeStruct((nc, MLH, 8, DQK), F32)],
        scratch_shapes=[pltpu.VMEM((MLH, DV, DQK), F32), pltpu.VMEM((MLH, 8, DQK), F32)],
        compiler_params=_params(dimension_semantics=("arbitrary",)),
    )(qk, proj, proj, grow, gcol, gain)


def _mlstm_bwd(qk, proj, grow, gcol, sneg_col, cs, st, hraw, dh, dproj):
    t = qk.shape[0]
    nc = t // CHUNK

    def rev(c):
        return nc - 1 - c

    def nxt(c):
        return jnp.minimum(nc - c, nc - 1)

    def body(qk_ref, v_ref, grow_ref, gcol_ref, sneg_ref, cs_ref, st_ref, cs1_ref, st1_ref, h_ref, dh_ref, _,
             dqk_ref, dv_ref, dif_ref, dbif_ref, dc_scr, dn_scr):
        @pl.when(pl.program_id(0) == 0)
        def _():
            dc_scr[...] = jnp.zeros_like(dc_scr)
            dn_scr[...] = jnp.zeros_like(dn_scr)
            dbif_ref[...] = jnp.zeros_like(dbif_ref)

        grow_v, gcol_v, sneg = grow_ref[...], gcol_ref[...], sneg_ref[...]
        tt = lax.broadcasted_iota(jnp.int32, (CHUNK, CHUNK), 0)
        ss = lax.broadcasted_iota(jnp.int32, (CHUNK, CHUNK), 1)
        lane8 = lax.broadcasted_iota(jnp.int32, (CHUNK, 8), 1)
        heads = range(MLH)
        q = [qk_ref[:, h * DQK:(h + 1) * DQK] for h in heads]
        k = [qk_ref[:, MLH * DQK + h * DQK:MLH * DQK + (h + 1) * DQK] for h in heads]
        qf, kf = [a.astype(F32) for a in q], [a.astype(F32) for a in k]
        vb = [v_ref[:, h * DV:(h + 1) * DV].astype(BF16) for h in heads]
        c0 = [cs_ref[0, h] for h in heads]
        n0 = [st_ref[0, h, 0:1, :] for h in heads]
        dc1 = [dc_scr[h] for h in heads]
        dn1 = [dn_scr[h, 0:1, :] for h in heads]
        terms = _chunk_terms(grow_v, gcol_v, [st_ref[0, h, 1:2, 0:1] for h in heads])
        dm, wi, ws = [t[0] for t in terms], [t[1] for t in terms], [t[3] for t in terms]
        s = [_dot_nt(q[h], k[h]) * dm[h] for h in heads]
        den = [wi[h] * jnp.sum(qf[h] * n0[h], axis=1, keepdims=True) + jnp.sum(s[h], axis=1, keepdims=True)
               for h in heads]
        floor = [jnp.exp(-terms[h][2]) for h in heads]
        g = [jnp.maximum(jnp.abs(den[h]), floor[h]) for h in heads]
        dh_v = [dh_ref[:, h * DV:(h + 1) * DV] for h in heads]
        dnum = [dh_v[h] / g[h] for h in heads]
        dden = [-jnp.sum(dh_v[h] * h_ref[:, h * DV:(h + 1) * DV], axis=1, keepdims=True) / g[h] for h in heads]
        dden = [jnp.where(jnp.abs(den[h]) > floor[h], dden[h] * jnp.sign(den[h]), 0.0) for h in heads]
        dnum_b = [a.astype(BF16) for a in dnum]
        dc1_b = [a.astype(BF16) for a in dc1]
        da = [((_dot_nt(dnum_b[h], vb[h]) + dden[h]) * dm[h]).astype(BF16) for h in heads]
        dq_inter = [_dot_nn(dnum_b[h], c0[h].astype(BF16)) for h in heads]
        dk_inter = [_dot_nn(vb[h], dc1_b[h]) for h in heads]
        dv_inter = [_dot_nt(k[h], dc1_b[h]) for h in heads]
        dc_new = [_dot_tn((wi[h] * dnum[h]).astype(BF16), q[h]) for h in heads]
        dq = [_dot_nn(da[h], k[h]) + wi[h] * (dq_inter[h] + dden[h] * n0[h]) for h in heads]
        dk = [_dot_tn(da[h], q[h]) + ws[h] * (dk_inter[h] + dn1[h]) for h in heads]
        dv = [_dot_tn(s[h].astype(BF16), dnum_b[h]) + ws[h] * dv_inter[h] for h in heads]
        for h in heads:
            dqk_ref[:, h * DQK:(h + 1) * DQK] = dq[h]
            dqk_ref[:, MLH * DQK + h * DQK:MLH * DQK + (h + 1) * DQK] = dk[h]
            dv_ref[:, h * DV:(h + 1) * DV] = dv[h].astype(BF16)
        rk = [jnp.sum(kf[h] * dk[h], axis=1, keepdims=True) for h in heads]
        df = [jnp.sum(qf[h] * dq[h], axis=1, keepdims=True) - rk[h] for h in heads]
        df_row = [jnp.sum(jnp.where(tt == ss, df[h], 0.0), axis=0, keepdims=True) for h in heads]
        suffix = [jnp.sum(jnp.where(ss >= tt, df_row[h], 0.0), axis=1, keepdims=True) for h in heads]
        cross = [jnp.sum(jnp.sum(dc1[h] * cs1_ref[0, h], axis=0, keepdims=True), axis=1, keepdims=True)
                 + jnp.sum(dn1[h] * st1_ref[0, h, 0:1, :], axis=1, keepdims=True) for h in heads]
        dif = jnp.zeros((CHUNK, 8), F32)
        for h in heads:
            dpf = (suffix[h] + cross[h]) * sneg[:, MLH + h:MLH + h + 1]
            dif = dif + jnp.where(lane8 == h, rk[h], 0.0) + jnp.where(lane8 == MLH + h, dpf, 0.0)
            dc_scr[h] = terms[h][4] * dc1[h] + dc_new[h]
            dn_scr[h, 0:1, :] = terms[h][4] * dn1[h] + jnp.sum(wi[h] * dden[h] * qf[h], axis=0, keepdims=True)
        dif_ref[...] = dif
        dbif_ref[...] += jnp.sum(dif, axis=0, keepdims=True)

    return pl.pallas_call(
        body, name="mlstm_bwd", grid=(nc,),
        in_specs=[pl.BlockSpec((CHUNK, D), lambda c: (rev(c), 0)),
                  pl.BlockSpec((CHUNK, D), lambda c: (rev(c), C_V // D)),
                  pl.BlockSpec((8, CHUNK), lambda c: (0, rev(c))),
                  pl.BlockSpec((CHUNK, 8), lambda c: (rev(c), 0)),
                  pl.BlockSpec((CHUNK, 8), lambda c: (rev(c), 0)),
                  pl.BlockSpec((1, MLH, DV, DQK), lambda c: (rev(c), 0, 0, 0)),
                  pl.BlockSpec((1, MLH, 8, DQK), lambda c: (rev(c), 0, 0, 0)),
                  pl.BlockSpec((1, MLH, DV, DQK), lambda c: (nxt(c), 0, 0, 0)),
                  pl.BlockSpec((1, MLH, 8, DQK), lambda c: (nxt(c), 0, 0, 0)),
                  pl.BlockSpec((CHUNK, D), lambda c: (rev(c), 0)),
                  pl.BlockSpec((CHUNK, D), lambda c: (rev(c), 0)), _ANY],
        out_specs=[pl.BlockSpec((CHUNK, D), lambda c: (rev(c), 0)),
                   pl.BlockSpec((CHUNK, D), lambda c: (rev(c), C_V // D)),
                   pl.BlockSpec((CHUNK, 8), lambda c: (rev(c), 0)),
                   pl.BlockSpec((1, 8), lambda c: (0, 0))],
        out_shape=[jax.ShapeDtypeStruct((t, D), F32), jax.ShapeDtypeStruct((t, NP), BF16),
                   jax.ShapeDtypeStruct((t, 8), F32), jax.ShapeDtypeStruct((1, 8), F32)],
        scratch_shapes=[pltpu.VMEM((MLH, DV, DQK), F32), pltpu.VMEM((MLH, 8, DQK), F32)],
        input_output_aliases={11: 1}, compiler_params=_params(dimension_semantics=("arbitrary",)),
    )(qk, proj, grow, gcol, sneg_col, cs, st, cs, st, hraw, dh, dproj)


_ANY = pl.BlockSpec(memory_space=pl.ANY)


_SW_SCALE = HD ** -0.5
_KVB = C_KV // (2 * SWKV * HD)


def _swa_mask(n):
    ki = lax.broadcasted_iota(jnp.int32, (2 * WIN, SWG * WIN), 0)
    qi = lax.broadcasted_iota(jnp.int32, (2 * WIN, SWG * WIN), 1) % WIN
    return (ki > qi) & (ki <= qi + WIN) & ((n > 0) | (ki >= WIN))


def _group_rows(x_ref, hk):
    return jnp.concatenate([x_ref[:, (hk * SWG + g) * HD:(hk * SWG + g + 1) * HD] for g in range(SWG)], axis=0)


def _group_lanes(x_ref, hk):
    return jnp.concatenate([x_ref[hk * SWG + g:hk * SWG + g + 1, :] for g in range(SWG)], axis=1)


def _sink_lanes(sink_ref, hk):
    return jnp.concatenate([jnp.broadcast_to(sink_ref[:, hk * SWG + g:hk * SWG + g + 1], (1, WIN))
                            for g in range(SWG)], axis=1)


def _swa_fwd(proj, sinks):
    t = proj.shape[0]
    nb = t // WIN

    def body(q_ref, kvc_ref, kvp_ref, sink_ref, y_ref, lse_ref):
        valid = _swa_mask(pl.program_id(0))
        for hk in range(SWKV):
            ks = slice(hk * HD, (hk + 1) * HD)
            vs = slice(SWKV * HD + hk * HD, SWKV * HD + (hk + 1) * HD)
            kb = jnp.concatenate([kvp_ref[:, ks], kvc_ref[:, ks]], axis=0).astype(BF16)
            vb = jnp.concatenate([kvp_ref[:, vs], kvc_ref[:, vs]], axis=0).astype(BF16)
            q4 = _group_rows(q_ref, hk).astype(BF16)
            sink = _sink_lanes(sink_ref, hk)
            logits = jnp.where(valid, _dot_nt(kb, q4) * _SW_SCALE, -jnp.inf)
            m = jnp.maximum(jnp.max(logits, axis=0, keepdims=True), sink)
            p = jnp.exp(logits - m)
            denom = jnp.sum(p, axis=0, keepdims=True) + jnp.exp(sink - m)
            y4 = _dot_tn((p / denom).astype(BF16), vb).astype(BF16)
            lse4 = m + jnp.log(denom)
            for g in range(SWG):
                hq = hk * SWG + g
                y_ref[:, hq * HD:(hq + 1) * HD] = y4[g * WIN:(g + 1) * WIN]
                lse_ref[hq:hq + 1, :] = lse4[:, g * WIN:(g + 1) * WIN]

    return pl.pallas_call(
        body, name="swa_fwd", grid=(nb,),
        in_specs=[pl.BlockSpec((WIN, D), lambda n: (n, C_QSW // D)),
                  pl.BlockSpec((WIN, 512), lambda n: (n, _KVB)),
                  pl.BlockSpec((WIN, 512), lambda n: (jnp.maximum(n - 1, 0), _KVB)),
                  pl.BlockSpec((1, SWH), lambda n: (0, 0))],
        out_specs=[pl.BlockSpec((WIN, D), lambda n: (n, 0)), pl.BlockSpec((SWH, WIN), lambda n: (0, n))],
        out_shape=[jax.ShapeDtypeStruct((t, D), BF16), jax.ShapeDtypeStruct((SWH, t), F32)],
        compiler_params=_params(),
    )(proj, proj, proj, sinks)


def _swa_bwd(proj, sinks, lse, dyb, dproj):
    t = proj.shape[0]
    nb = t // WIN

    def body(q_ref, kvc_ref, kvp_ref, sink_ref, lse_ref, dy_ref, _, dq_ref, dself_ref, dprev_ref, ds_ref):
        @pl.when(pl.program_id(0) == 0)
        def _():
            ds_ref[...] = jnp.zeros_like(ds_ref)

        valid = _swa_mask(pl.program_id(0))
        kvh = range(SWKV)
        ks = [slice(hk * HD, (hk + 1) * HD) for hk in kvh]
        vs = [slice(SWKV * HD + hk * HD, SWKV * HD + (hk + 1) * HD) for hk in kvh]
        kb = [jnp.concatenate([kvp_ref[:, ks[hk]], kvc_ref[:, ks[hk]]], axis=0).astype(BF16) for hk in kvh]
        vb = [jnp.concatenate([kvp_ref[:, vs[hk]], kvc_ref[:, vs[hk]]], axis=0).astype(BF16) for hk in kvh]
        qb = [_group_rows(q_ref, hk).astype(BF16) for hk in kvh]
        dyb_ = [_group_rows(dy_ref, hk).astype(BF16) for hk in kvh]
        lse4 = [_group_lanes(lse_ref, hk) for hk in kvh]
        logits = [_dot_nt(kb[hk], qb[hk]) for hk in kvh]
        dpt = [_dot_nt(vb[hk], dyb_[hk]) for hk in kvh]
        p = [jnp.exp(jnp.where(valid, logits[hk] * _SW_SCALE, -jnp.inf) - lse4[hk]) for hk in kvh]
        delta = [jnp.sum(p[hk] * dpt[hk], axis=0, keepdims=True) for hk in kvh]
        dsm = [(p[hk] * (dpt[hk] - delta[hk])).astype(BF16) for hk in kvh]
        dq4 = [(_dot_tn(dsm[hk], kb[hk]) * _SW_SCALE).astype(BF16) for hk in kvh]
        dkb = [_dot_nn(dsm[hk], qb[hk]) * _SW_SCALE for hk in kvh]
        dvb = [_dot_nn(p[hk].astype(BF16), dyb_[hk]) for hk in kvh]
        for hk in kvh:
            dsink4 = jnp.exp(_sink_lanes(sink_ref, hk) - lse4[hk]) * delta[hk]
            for g in range(SWG):
                hq = hk * SWG + g
                dq_ref[:, hq * HD:(hq + 1) * HD] = dq4[hk][g * WIN:(g + 1) * WIN]
                ds_ref[:, hq:hq + 1] += -jnp.sum(dsink4[:, g * WIN:(g + 1) * WIN], axis=1, keepdims=True)
            dprev_ref[:, ks[hk]] = dkb[hk][:WIN]
            dself_ref[:, ks[hk]] = dkb[hk][WIN:]
            dprev_ref[:, vs[hk]] = dvb[hk][:WIN]
            dself_ref[:, vs[hk]] = dvb[hk][WIN:]

    return pl.pallas_call(
        body, name="swa_bwd", grid=(nb,),
        in_specs=[pl.BlockSpec((WIN, D), lambda n: (n, C_QSW // D)),
                  pl.BlockSpec((WIN, 512), lambda n: (n, _KVB)),
                  pl.BlockSpec((WIN, 512), lambda n: (jnp.maximum(n - 1, 0), _KVB)),
                  pl.BlockSpec((1, SWH), lambda n: (0, 0)),
                  pl.BlockSpec((SWH, WIN), lambda n: (0, n)),
                  pl.BlockSpec((WIN, D), lambda n: (n, 0)), _ANY],
        out_specs=[pl.BlockSpec((WIN, D), lambda n: (n, C_QSW // D)), pl.BlockSpec((WIN, 512), lambda n: (n, 0)),
                   pl.BlockSpec((WIN, 512), lambda n: (jnp.maximum(n - 1, 0), 0)),
                   pl.BlockSpec((1, SWH), lambda n: (0, 0))],
        out_shape=[jax.ShapeDtypeStruct((t, NP), BF16), jax.ShapeDtypeStruct((t, 512), F32),
                   jax.ShapeDtypeStruct((t, 512), F32), jax.ShapeDtypeStruct((1, SWH), F32)],
        input_output_aliases={6: 0}, compiler_params=_params(),
    )(proj, proj, proj, sinks, lse, dyb, dproj)


def _kv_combine(dself, dnext, dif, dproj):
    t = dself.shape[0]
    rows = _pick(t, 512)

    def body(a_ref, b_ref, dif_ref, _, o_ref):
        row = pl.program_id(0) * rows + lax.broadcasted_iota(jnp.int32, (rows, 1), 0)
        o_ref[:, 0:512] = (a_ref[...] + jnp.where(row < t - WIN, b_ref[...], 0.0)).astype(BF16)
        lane = lax.broadcasted_iota(jnp.int32, (rows, 128), 1)
        dif_v = dif_ref[...]
        first = jnp.zeros((rows, 128), F32)
        for col in range(8):
            first = first + jnp.where(lane == col, dif_v[:, col:col + 1], 0.0)
        o_ref[:, 512:640] = first.astype(BF16)
        o_ref[:, 640:512 + IFW] = jnp.zeros((rows, IFW - 128), BF16)

    return pl.pallas_call(
        body, name="kv_combine", grid=(t // rows,),
        in_specs=[pl.BlockSpec((rows, 512), lambda n: (n, 0)), pl.BlockSpec((rows, 512), lambda n: (n, 0)),
                  pl.BlockSpec((rows, 8), lambda n: (n, 0)), _ANY],
        out_specs=pl.BlockSpec((rows, 512 + IFW), lambda n: (n, C_KV // (512 + IFW))),
        out_shape=jax.ShapeDtypeStruct((t, NP), BF16), input_output_aliases={3: 0}, compiler_params=_params(),
    )(dself, dnext, dif, dproj)


def _sds(t, n, dtype):
    return jax.ShapeDtypeStruct((t, n), dtype)


def _proj_in(x, gain, w_in):
    t = x.shape[0]
    tm, tn = _pick(t, 1024), 2 * IFW

    def body(x_ref, g_ref, w_ref, h_ref, p_ref, gate_ref, h_scr):
        j = pl.program_id(1)

        @pl.when(j == 0)
        def _():
            xn, _ = _rms(x_ref[...])
            h = (xn * g_ref[...]).astype(BF16)
            h_scr[...] = h
            h_ref[...] = h

        acc = _dot_nn(h_scr[...], w_ref[...])
        p_ref[...] = acc.astype(BF16)

        @pl.when(j == C_IF // tn)
        def _():
            gate_ref[...] = acc[:, C_IF % tn:C_IF % tn + 128]

    return pl.pallas_call(
        body, name="mm_in", grid=(t // tm, NP // tn),
        in_specs=[pl.BlockSpec((tm, D), lambda i, j: (i, 0)), pl.BlockSpec((1, D), lambda i, j: (0, 0)),
                  pl.BlockSpec((D, tn), lambda i, j: (0, j))],
        out_specs=[pl.BlockSpec((tm, D), lambda i, j: (i, 0)), pl.BlockSpec((tm, tn), lambda i, j: (i, j)),
                   pl.BlockSpec((tm, 128), lambda i, j: (i, 0))],
        out_shape=[_sds(t, D, BF16), _sds(t, NP, BF16), _sds(t, 128, F32)],
        scratch_shapes=[pltpu.VMEM((tm, D), BF16)],
        compiler_params=_params(dimension_semantics=("arbitrary", "arbitrary")),
    )(x, gain, w_in)


def _branch_merge(ya, yb, wa, wb, proj):
    t = ya.shape[0]

    def epilogue(accs, ins, outs, i, j):
        za, zb = accs
        merged = _sigmoid(ins[0][...].astype(F32)) * za + _sigmoid(ins[1][...].astype(F32)) * zb
        outs[0][...] = merged.astype(BF16)
        outs[1][...] = za.astype(BF16)
        outs[2][...] = zb.astype(BF16)

    return _mm_ep([(ya, wa), (yb, wb)], "nn", "mm_branch_merge", epilogue, [(proj, _tile(C_GA)), (proj, _tile(C_GB))],
                  [(_sds(t, D, BF16), _tile())] * 3, 1024, 1024)


def _dmerged_bwd(dxb, w_out, proj, za, zb):
    t = dxb.shape[0]

    def epilogue(accs, ins, outs, i, j):
        dm = accs[0]
        sa, sb = _sigmoid(ins[0][...].astype(F32)), _sigmoid(ins[1][...].astype(F32))
        outs[0][...] = (dm * sa).astype(BF16)
        outs[1][...] = (dm * sb).astype(BF16)
        outs[2][:, 0:D] = (dm * ins[2][...].astype(F32) * sa * (1.0 - sa)).astype(BF16)
        outs[2][:, D:2 * D] = (dm * ins[3][...].astype(F32) * sb * (1.0 - sb)).astype(BF16)

    gate_cols = lambda tm, tn: pl.BlockSpec((tm, 2 * D), lambda i, j, kk: (i, C_GA // (2 * D)))
    return _mm_ep([(dxb, w_out)], "nt", "mm_dmerged_bwd", epilogue,
                  [(proj, _tile(C_GA)), (proj, _tile(C_GB)), (za, _tile()), (zb, _tile())],
                  [(_sds(t, D, BF16), _tile()), (_sds(t, D, BF16), _tile()), (_sds(t, NP, BF16), gate_cols)], 1024, D)


def _dya_bwd(dza, wa, hraw, proj, g, dproj):
    t = dza.shape[0]

    def epilogue(accs, ins, outs, i, j):
        h_ref, o_ref, g_ref, _ = ins
        dh_ref, do_ref, dg_ref = outs

        @pl.when(i == 0)
        def _():
            dg_ref[...] = jnp.zeros_like(dg_ref)

        dy = accs[0]
        so = _sigmoid(o_ref[...].astype(F32))
        for h in range(MLH):
            sl = slice(h * DV, (h + 1) * DV)
            xn, rstd = _rms(h_ref[:, sl])
            gs = g_ref[:, sl]
            do_ref[:, sl] = (dy[:, sl] * xn * gs * so[:, sl] * (1.0 - so[:, sl])).astype(BF16)
            dhn = dy[:, sl] * so[:, sl]
            dg_ref[:, sl] += jnp.sum(dhn * xn, axis=0, keepdims=True)
            dh_ref[:, sl] = _rms_bwd(xn, rstd, dhn * gs)

    return _mm_ep([(dza, wa)], "nt", "mm_dya_bwd", epilogue,
                  [(hraw, _tile()), (proj, _tile(C_O)), (g, _row()), (dproj, lambda tm, tn: _ANY)],
                  [(_sds(t, D, F32), _tile()), (_sds(t, NP, BF16), _tile(C_O)), (_sds(1, D, F32), _row())],
                  512, D, aliases={3: 1})


def _up_act(hn, w_up):
    t = hn.shape[0]

    def epilogue(accs, ins, outs, i, j):
        r = jnp.maximum(accs[0], 0.0)
        outs[0][...] = (r * r).astype(BF16)
        outs[1][...] = accs[0].astype(BF16)

    return _mm_ep([(hn, w_up)], "nn", "mm_up_act", epilogue, [],
                  [(_sds(t, DFF, BF16), _tile()), (_sds(t, DFF, BF16), _tile())], 1024, 1024)


def _da_du(dxb, w_down, u):
    t = dxb.shape[0]

    def epilogue(accs, ins, outs, i, j):
        outs[0][...] = (accs[0] * 2.0 * jnp.maximum(ins[0][...].astype(F32), 0.0)).astype(BF16)

    return _mm_ep([(dxb, w_down)], "nt", "mm_da_du", epilogue, [(u, _tile())], [(_sds(t, DFF, BF16), _tile())],
                  1024, 1024)[0]


def _resid_norm_mm(a, w, x, g, name):
    t = x.shape[0]

    def epilogue(accs, ins, outs, i, j):
        x1 = ins[0][...] + accs[0]
        outs[0][...] = x1
        xn, _ = _rms(x1)
        outs[1][...] = (xn * ins[1][...]).astype(BF16)

    return _mm_ep([(a, w)], "nn", name, epilogue, [(x, _tile()), (g, _row())],
                  [(_sds(t, D, F32), _tile()), (_sds(t, D, BF16), _tile())], 1024, D, row_split=4)


def _norm_bwd_mm(dy, w, x, g, dres, name):
    t = x.shape[0]

    def init(outs):
        outs[2][...] = jnp.zeros_like(outs[2])

    def epilogue(accs, ins, outs, i, j):
        dh = accs[0]
        xn, rstd = _rms(ins[0][...])
        outs[2][...] += jnp.sum(dh * xn, axis=0, keepdims=True)
        dx = ins[2][...] + _rms_bwd(xn, rstd, dh * ins[1][...])
        outs[0][...] = dx
        outs[1][...] = dx.astype(BF16)

    return _mm_ep([(dy, w)], "nt", name, epilogue, [(x, _tile()), (g, _row()), (dres, _tile())],
                  [(_sds(t, D, F32), _tile()), (_sds(t, D, BF16), _tile()), (_sds(1, D, F32), _row())], 1024, D,
                  row_split=4, init=init)


def _ple_final_mm(hn2, w_gate, x2, pp, target, gf):
    t = x2.shape[0]

    def epilogue(accs, ins, outs, i, j):
        loss_ref, dg_ref, dx_ref, dpp_ref, dgp_ref = outs

        @pl.when(i == 0)
        def _():
            loss_ref[...] = jnp.zeros_like(loss_ref)
            dg_ref[...] = jnp.zeros_like(dg_ref)

        gate = _sigmoid(accs[0])
        pp_v = ins[1][...]
        x3 = ins[0][...] + gate * pp_v
        xn, rstd = _rms(x3)
        gf_v = ins[3][...]
        err = xn * gf_v - ins[2][...]
        loss_ref[...] += (0.5 / D) * jnp.sum(jnp.sum(err * err, axis=1, keepdims=True), axis=0, keepdims=True)
        dy = err * (1.0 / D)
        dg_ref[...] += jnp.sum(dy * xn, axis=0, keepdims=True)
        dx3 = _rms_bwd(xn, rstd, dy * gf_v)
        dx_ref[...] = dx3
        dpp_ref[...] = (dx3 * gate).astype(BF16)
        dgp_ref[...] = (dx3 * pp_v * gate * (1.0 - gate)).astype(BF16)

    one = lambda tm, tn: pl.BlockSpec((1, 1), lambda i, j, kk: (0, 0))
    return _mm_ep([(hn2, w_gate)], "nn", "mm_ple_final", epilogue,
                  [(x2, _tile()), (pp, _tile()), (target, _tile()), (gf, _row())],
                  [(_sds(1, 1, F32), one), (_sds(1, D, F32), _row()), (_sds(t, D, F32), _tile()),
                   (_sds(t, D, BF16), _tile()), (_sds(t, D, BF16), _tile())], 512, D)


_WIN_SEGMENTS = ((0, 3072, C_QK), (3072, 8, C_IF), (3080, 1024, C_QSW), (4104, 256, C_KV), (4360, 256, C_KV + 256),
                 (4616, 1024, C_GA), (5640, 1024, C_GB))
_WIN_SHARD = N_IN // 4


def _win_pieces():
    out = []
    for src, width, dst in _WIN_SEGMENTS:
        while width:
            chip, col = divmod(src, _WIN_SHARD)
            n = min(width, _WIN_SHARD - col)
            out.append((chip, col, n, dst))
            src, dst, width = src + n, dst + n, width - n
    return out


def _win_pad(shards):
    rows = shards.shape[1]
    tr = _pick(rows, 256)

    def body(s_ref, o_ref):
        for chip, col, n, dst in _win_pieces():
            o_ref[:, dst:dst + n] = s_ref[chip, :, col:col + n]
        o_ref[:, C_IF + 8:NP] = jnp.zeros((tr, NP - C_IF - 8), shards.dtype)

    return pl.pallas_call(
        body, name="win_pad", grid=(rows // tr,), in_specs=[pl.BlockSpec((4, tr, _WIN_SHARD), lambda i: (0, i, 0))],
        out_specs=pl.BlockSpec((tr, NP), lambda i: (i, 0)), out_shape=jax.ShapeDtypeStruct((rows, NP), shards.dtype),
        compiler_params=_params(),
    )(shards)


def _win_unpad(wp):
    rows = wp.shape[0]
    tr = _pick(rows, 256)

    def body(p_ref, o_ref):
        for chip, col, n, dst in _win_pieces():
            o_ref[chip, :, col:col + n] = p_ref[:, dst:dst + n]

    return pl.pallas_call(
        body, name="win_unpad", grid=(rows // tr,), in_specs=[pl.BlockSpec((tr, NP), lambda i: (i, 0))],
        out_specs=pl.BlockSpec((4, tr, _WIN_SHARD), lambda i: (0, i, 0)),
        out_shape=jax.ShapeDtypeStruct((4, rows, _WIN_SHARD), wp.dtype), compiler_params=_params(),
    )(wp)


def _local_step(x, p, target, w, late_weights=None, early_grads=None, mid_grads=None, last_grad=None):
    t = x.shape[0]
    pb = p.astype(BF16)
    w = dict(w)

    h0, proj, gates = _proj_in(x, w["norm_mix_g"], w["w_in"])
    qk = _conv_silu_fwd(proj, w["conv_qk"])
    grow, sneg_row = _gates_fwd(gates[:, 0:8].T, w["b_if"].reshape(8, 1))
    gcol, sneg_col = grow.T, sneg_row.T
    hraw, ya, cs, st = _mlstm_fwd(qk, proj, grow, gcol, w["mlstm_norm_g"])
    yb, lse = _swa_fwd(proj, w["sinks"])
    if late_weights is not None:
        w.update(late_weights(yb))
    merged, za, zb = _branch_merge(ya, yb, w["w_branch_a"], w["w_branch_b"], proj)
    x1, hn1 = _resid_norm_mm(merged, w["w_out"], x, w["norm_mlp_g"], "mm_out_norm")
    act, u = _up_act(hn1, w["w_up"])
    x2, hn2 = _resid_norm_mm(act, w["w_down"], x1, w["norm_ple_g"], "mm_down_norm")
    pp = _mm(pb, w["w_ple_proj"], "nn", F32, "mm_ple_proj")
    loss, d_final_g, dx3, dpp, dgpre = _ple_final_mm(hn2, w["w_ple_gate"], x2, pp, target, w["final_norm_g"])

    g = {"final_norm_g": d_final_g}
    g["w_ple_proj"] = _mm(pb, dpp, "tn", F32, "mm_d_ple_proj", out_chunks=4)
    g["w_ple_gate"] = _mm(hn2, dgpre, "tn", F32, "mm_d_ple_gate")
    dx2, dx2b, g["norm_ple_g"] = _norm_bwd_mm(dgpre, w["w_ple_gate"], x2, w["norm_ple_g"], dx3, "mm_dhn2_norm")
    g["w_down"] = _mm(act, dx2b, "tn", F32, "mm_d_down")
    du = _da_du(dx2b, w["w_down"], u)
    g["w_up"] = _mm(hn1, du, "tn", F32, "mm_d_up", out_chunks=4)
    dx1, dx1b, g["norm_mlp_g"] = _norm_bwd_mm(du, w["w_up"], x1, w["norm_mlp_g"], dx2, "mm_dhn1_norm")
    g["w_out"] = _mm(merged, dx1b, "tn", F32, "mm_d_out")
    dza, dzb, dproj = _dmerged_bwd(dx1b, w["w_out"], proj, za, zb)
    g["w_branch_a"] = _mm(ya, dza, "tn", F32, "mm_d_branch_a")
    g["w_branch_b"] = _mm(yb, dzb, "tn", F32, "mm_d_branch_b")
    gain = w["mlstm_norm_g"] if early_grads is None else w["mlstm_norm_g"] + early_grads(g)
    dyb = _mm(dzb, w["w_branch_b"], "nt", F32, "mm_dyb")
    dhraw, dproj, g["mlstm_norm_g"] = _dya_bwd(dza, w["w_branch_a"], hraw, proj, gain, dproj)
    if mid_grads is not None:
        sneg_col = sneg_col + mid_grads(dhraw)
    dqk, dproj, dif, g["b_if"] = _mlstm_bwd(qk, proj, grow, gcol, sneg_col, cs, st, hraw, dhraw, dproj)
    dc, g["conv_qk"] = _conv_silu_bwd_a(proj, w["conv_qk"], dqk)
    dproj = _conv_silu_bwd_b(dc, w["conv_qk"], dproj)
    dproj, dkv_self, dkv_prev, g["sinks"] = _swa_bwd(proj, w["sinks"], lse, dyb, dproj)
    dproj = _kv_combine(dkv_self, dkv_prev, dif, dproj)
    g["w_in"] = _mm(h0, dproj, "tn", F32, "mm_d_in")
    gain = w["norm_mix_g"] if last_grad is None else w["norm_mix_g"] + last_grad(g)
    grad_x, _, g["norm_mix_g"] = _norm_bwd_mm(dproj, w["w_in"], x, gain, dx1, "mm_dh0_norm")
    return loss, grad_x, g


_W4 = ("w_branch_a", "w_branch_b", "w_out", "w_ple_gate")
_SHARDED_NAMES = ("w_in", "w_up", "w_down", "w_ple_proj", "conv_qk") + _W4
_SMALL_ROWS = 16
_CONV_ROW = 8


def _group(s):
    return [s["w_in"], jnp.concatenate([s[n] for n in _W4], axis=0), s["w_up"], s["w_down"], s["w_ple_proj"]]


def _ungroup(arrs):
    out = {"w_in": arrs[0], "w_up": arrs[2], "w_down": arrs[3], "w_ple_proj": arrs[4]}
    rows = arrs[1].shape[0] // len(_W4)
    for i, n in enumerate(_W4):
        out[n] = arrs[1][i * rows:(i + 1) * rows]
    return out


def _rows_tile(rows):
    return 256 if rows % 256 == 0 else rows


_SMALL = ("norm_mix_g", "mlstm_norm_g", "norm_mlp_g", "norm_ple_g", "final_norm_g")


def _pack_small(vals, extra=None, conv=None):
    rows = [vals[n].reshape(1, D) for n in _SMALL]
    tail = [vals["b_if"].reshape(1, 8), vals["sinks"].reshape(1, SWH)]
    used = 8 + SWH
    if extra is not None:
        tail.append(extra.reshape(1, 1))
        used += 1
    tail.append(jnp.zeros((1, D - used), F32))
    rows.append(jnp.concatenate(tail, axis=1))
    rows.append(jnp.zeros((_CONV_ROW - len(rows), D), F32))
    rows.append(jnp.zeros((CONV, D), F32) if conv is None else conv)
    rows.append(jnp.zeros((_SMALL_ROWS - _CONV_ROW - CONV, D), F32))
    return jnp.concatenate(rows, axis=0)


def _unpack_small(slab, shapes):
    out = {n: slab[i].reshape(shapes[n]) for i, n in enumerate(_SMALL)}
    out["b_if"] = slab[5, 0:8].reshape(shapes["b_if"])
    out["sinks"] = slab[5, 8:8 + SWH].reshape(shapes["sinks"])
    return out


_MESH = pl.DeviceIdType.MESH
_HBM = pl.BlockSpec(memory_space=pltpu.HBM)
_VMEM = pl.BlockSpec(memory_space=pltpu.VMEM)


def _place():
    x, y, c = lax.axis_index("x"), lax.axis_index("y"), lax.axis_index("c")
    return x, y, c, 2 * x + y


def _chip_peer(x, y, r):
    return (x ^ (r >> 1), y ^ (r & 1))


def _half(ref, which):
    h = ref.shape[-2] // 2
    return pl.ds(which * h, h)


def _allgather_weights(shards, conv):
    n = len(shards)

    def body(*refs):
        ins, conv_ref = refs[:n], refs[n]
        outs, conv_out = refs[n + 1:2 * n + 1], refs[2 * n + 1]
        send_a, recv_a, send_b, recv_b, send_c, recv_c, local_sems = refs[2 * n + 2:]
        x, y, c, j = _place()
        sibling = (x, y, 1 - c)
        local = [pltpu.make_async_copy(ins[k], outs[k].at[j], local_sems.at[k]) for k in range(n)]
        local.append(pltpu.make_async_copy(conv_ref, conv_out.at[j], local_sems.at[n]))
        for cp in local:
            cp.start()

        def copy_a(k, r, chip):
            rows = _half(ins[k], c)
            return pltpu.make_async_remote_copy(
                src_ref=ins[k].at[rows], dst_ref=outs[k].at[chip, rows], send_sem=send_a.at[3 * k + r - 1],
                recv_sem=recv_a.at[3 * k + r - 1], device_id=(*_chip_peer(x, y, r), c), device_id_type=_MESH)

        def copy_b(k, r, chip, which):
            rows = _half(ins[k], which)
            return pltpu.make_async_remote_copy(
                src_ref=outs[k].at[chip, rows], dst_ref=outs[k].at[chip, rows], send_sem=send_b.at[3 * k + r - 1],
                recv_sem=recv_b.at[3 * k + r - 1], device_id=sibling, device_id_type=_MESH)

        def copy_c(r, chip):
            return pltpu.make_async_remote_copy(
                src_ref=conv_ref, dst_ref=conv_out.at[chip], send_sem=send_c.at[r - 1],
                recv_sem=recv_c.at[r - 1], device_id=(*_chip_peer(x, y, r), c), device_id_type=_MESH)

        for k in range(n):
            for r in (1, 2, 3):
                copy_a(k, r, j).start()
        for r in (1, 2, 3):
            copy_c(r, j).start()
        for k in range(n):
            for r in (1, 2, 3):
                copy_a(k, r, j ^ r).wait_recv()
                copy_b(k, r, j ^ r, c).start()
        for k in range(n):
            for r in (1, 2, 3):
                copy_b(k, r, j ^ r, 1 - c).wait_recv()
        for r in (1, 2, 3):
            copy_c(r, j ^ r).wait_recv()
        for k in range(n):
            for r in (1, 2, 3):
                copy_a(k, r, j).wait_send()
                copy_b(k, r, j ^ r, c).wait_send()
        for r in (1, 2, 3):
            copy_c(r, j).wait_send()
        for cp in local:
            cp.wait()

    return pl.pallas_call(
        body, name="allgather_weights",
        out_shape=[jax.ShapeDtypeStruct((4,) + s.shape, s.dtype) for s in shards]
        + [jax.ShapeDtypeStruct((4,) + conv.shape, F32)],
        in_specs=[_HBM] * (n + 1), out_specs=[_HBM] * (n + 1),
        scratch_shapes=[pltpu.SemaphoreType.DMA((3 * n,))] * 4 + [pltpu.SemaphoreType.DMA((3,))] * 2
        + [pltpu.SemaphoreType.DMA((n + 1,))],
    )(*shards, conv)


_SEM = pl.BlockSpec(memory_space=pltpu.SEMAPHORE)
_DATAFLOW = pltpu.SideEffectType.DATAFLOW_SIDE_EFFECTING


def _late_peer_copy(src_ref, land_ref, send_sems, recv_sems, x, y, c, j, r, chip):
    return pltpu.make_async_remote_copy(
        src_ref=src_ref, dst_ref=land_ref.at[chip], send_sem=send_sems.at[r - 1], recv_sem=recv_sems.at[r - 1],
        device_id=(*_chip_peer(x, y, r), c), device_id_type=_MESH)


def _late_gather_start(rest):
    def body(rest_ref, land_ref, send_sems, recv_sems, rest_thru, land_thru, token):
        x, y, c, j = _place()
        for r in (1, 2, 3):
            _late_peer_copy(rest_ref, land_ref, send_sems, recv_sems, x, y, c, j, r, j).start()
        token[...] = jnp.zeros_like(token)

    j = 2 * lax.axis_index("x") + lax.axis_index("y")
    land = lax.dynamic_update_slice(lax.empty((4,) + rest.shape, rest.dtype), rest[None], (j, 0, 0))
    return pl.pallas_call(
        body, name="late_gather_start",
        out_shape=(pltpu.SemaphoreType.DMA((3,)), pltpu.SemaphoreType.DMA((3,)), pltpu.HBM(rest.shape, rest.dtype),
                   pltpu.HBM(land.shape, land.dtype), jax.ShapeDtypeStruct((8, 128), F32)),
        in_specs=(_HBM, _HBM), out_specs=(_SEM, _SEM, _HBM, _HBM, _VMEM), input_output_aliases={0: 2, 1: 3},
        compiler_params=pltpu.CompilerParams(has_side_effects=_DATAFLOW),
    )(pltpu.with_memory_space_constraint(rest, pltpu.HBM), pltpu.with_memory_space_constraint(land, pltpu.HBM))


def _late_gather_wait(send_sems, recv_sems, rest_thru, land_thru, after):
    def body(rest_ref, land_ref, send_sems, recv_sems, after_ref, rest_dead, got_ref):
        x, y, c, j = _place()
        for r in (1, 2, 3):
            cp = _late_peer_copy(rest_ref, land_ref, send_sems, recv_sems, x, y, c, j, r, j ^ r)
            cp.wait_send()
            cp.wait_recv()

    return pl.pallas_call(
        body, name="late_gather_wait",
        out_shape=(pltpu.HBM(rest_thru.shape, rest_thru.dtype), pltpu.HBM(land_thru.shape, land_thru.dtype)),
        in_specs=(_HBM, _HBM, _SEM, _SEM, _ANY), out_specs=(_HBM, _HBM), input_output_aliases={0: 0, 1: 1},
        compiler_params=pltpu.CompilerParams(has_side_effects=_DATAFLOW),
    )(rest_thru, land_thru, send_sems, recv_sems, after)[1]


def _pair_sum(g, theirs, j, c, name):
    _, h, cols = theirs.shape
    tr = _rows_tile(h)
    nb = h // tr

    def body(idx_ref, a_ref, b_ref, own_ref, ob_ref):
        s = a_ref[0] + b_ref[0]
        ob_ref[0] = s.astype(BF16)

        @pl.when(pl.program_id(1) == idx_ref[0])
        def _():
            own_ref[...] = s

    blk = pl.BlockSpec((1, tr, cols), lambda i, k, idx_ref: (k, i, 0))
    return pl.pallas_call(
        body, name=name,
        grid_spec=pltpu.PrefetchScalarGridSpec(
            num_scalar_prefetch=1, grid=(nb, 4),
            in_specs=[pl.BlockSpec((1, tr, cols), lambda i, k, idx_ref: (k, idx_ref[1] * nb + i, 0)), blk],
            out_specs=[pl.BlockSpec((tr, cols), lambda i, k, idx_ref: (i, 0)), blk]),
        out_shape=[jax.ShapeDtypeStruct((h, cols), F32), jax.ShapeDtypeStruct(theirs.shape, BF16)],
        compiler_params=_params(),
    )(jnp.stack([j, c]).astype(jnp.int32), g, theirs)


def _chip_copies(srcs, lands, send_sems, recv_sems):
    x, y, c, j = _place()
    return [pltpu.make_async_remote_copy(
        src_ref=srcs[k].at[j ^ r], dst_ref=lands[k].at[r - 1], send_sem=send_sems.at[3 * k + r - 1],
        recv_sem=recv_sems.at[3 * k + r - 1], device_id=(*_chip_peer(x, y, r), c), device_id_type=_MESH)
        for k in range(len(srcs)) for r in (1, 2, 3)]


def _pair_copies(srcs, lands, send_sems, recv_sems):
    x, y, c, _ = _place()
    return [pltpu.make_async_remote_copy(
        src_ref=srcs[k].at[:, _half(srcs[k], 1 - c)], dst_ref=lands[k], send_sem=send_sems.at[k],
        recv_sem=recv_sems.at[k], device_id=(x, y, 1 - c), device_id_type=_MESH) for k in range(len(srcs))]


def _split_start(name, srcs, lands, copies, n_sems):
    n = len(srcs)

    def body(*refs):
        for cp in copies(refs[:n], refs[n:2 * n], refs[2 * n], refs[2 * n + 1]):
            cp.start()
        refs[-1][...] = jnp.zeros_like(refs[-1])

    arrays = list(srcs) + list(lands)
    out = pl.pallas_call(
        body, name=name,
        out_shape=(pltpu.SemaphoreType.DMA((n_sems,)), pltpu.SemaphoreType.DMA((n_sems,)),
                   *[pltpu.HBM(a.shape, a.dtype) for a in arrays], jax.ShapeDtypeStruct((8, 128), F32)),
        in_specs=[_HBM] * (2 * n), out_specs=(_SEM, _SEM, *([_HBM] * (2 * n)), _VMEM),
        input_output_aliases={k: 2 + k for k in range(2 * n)},
        compiler_params=pltpu.CompilerParams(has_side_effects=_DATAFLOW),
    )(*[pltpu.with_memory_space_constraint(a, pltpu.HBM) for a in arrays])
    return out[0], out[1], list(out[2:2 + n]), list(out[2 + n:2 + 2 * n]), out[-1]


def _split_wait(name, send_sems, recv_sems, srcs_thru, lands_thru, after, copies):
    n = len(srcs_thru)

    def body(*refs):
        for cp in copies(refs[:n], refs[n:2 * n], refs[2 * n], refs[2 * n + 1]):
            cp.wait_send()
            cp.wait_recv()

    arrays = list(srcs_thru) + list(lands_thru)
    out = pl.pallas_call(
        body, name=name, out_shape=tuple(pltpu.HBM(a.shape, a.dtype) for a in arrays),
        in_specs=[_HBM] * (2 * n) + [_SEM, _SEM, _ANY], out_specs=tuple([_HBM] * (2 * n)),
        input_output_aliases={k: k for k in range(2 * n)},
        compiler_params=pltpu.CompilerParams(has_side_effects=_DATAFLOW),
    )(*arrays, send_sems, recv_sems, after)
    return list(out[:n]), list(out[n:])


def _chip_exchange_start(ss, tag):
    lands = [lax.empty((3,) + s.shape[1:], s.dtype) for s in ss]
    return _split_start("chip_exchange_start_" + tag, ss, lands, _chip_copies, 3 * len(ss))


def _chip_exchange_wait(send_sems, recv_sems, ss_thru, lands_thru, after, tag):
    return _split_wait("chip_exchange_wait_" + tag, send_sems, recv_sems, ss_thru, lands_thru, after, _chip_copies)[1]


def _pair_exchange_start(gs, tag):
    lands = [lax.empty((4, g.shape[1] // 2, g.shape[2]), g.dtype) for g in gs]
    return _split_start("pair_exchange_start_" + tag, gs, lands, _pair_copies, len(gs))


def _pair_exchange_wait(send_sems, recv_sems, gs_thru, lands_thru, after, tag):
    return _split_wait("pair_exchange_wait_" + tag, send_sems, recv_sems, gs_thru, lands_thru, after, _pair_copies)


def _reduce4(own, others, c, name):
    h, cols = own.shape
    tr = _rows_tile(h)
    nb = h // tr

    def body(c_ref, s_ref, a0, a1, a2, o_ref):
        o_ref[...] = ((s_ref[...] + a0[0].astype(F32)) + a1[0].astype(F32)) + a2[0].astype(F32)

    def other(r):
        return pl.BlockSpec((1, tr, cols), lambda i, c_ref: (r, i, 0))

    return pl.pallas_call(
        body, name=name,
        grid_spec=pltpu.PrefetchScalarGridSpec(
            num_scalar_prefetch=1, grid=(nb,),
            in_specs=[pl.BlockSpec((tr, cols), lambda i, c_ref: (i, 0)), other(0), other(1), other(2)],
            out_specs=pl.BlockSpec((tr, cols), lambda i, c_ref: (c_ref[0] * nb + i, 0))),
        out_shape=jax.ShapeDtypeStruct((2 * h, cols), F32), compiler_params=_params(),
    )(c.reshape(1).astype(jnp.int32), own, others, others, others)


def _sibling_share(fulls, name):
    n = len(fulls)

    def body(*refs):
        outs, send_sems, recv_sems = refs[n:2 * n], refs[2 * n], refs[2 * n + 1]
        x, y, c, _ = _place()
        cps = [pltpu.make_async_remote_copy(
            src_ref=outs[k].at[_half(outs[k], c)], dst_ref=outs[k].at[_half(outs[k], c)], send_sem=send_sems.at[k],
            recv_sem=recv_sems.at[k], device_id=(x, y, 1 - c), device_id_type=_MESH) for k in range(n)]
        for cp in cps:
            cp.start()
        for cp in cps:
            cp.wait()

    return pl.pallas_call(
        body, name=name, out_shape=[jax.ShapeDtypeStruct(f.shape, F32) for f in fulls],
        in_specs=[_HBM] * n, out_specs=[_HBM] * n, input_output_aliases={k: k for k in range(n)},
        scratch_shapes=[pltpu.SemaphoreType.DMA((n,))] * 2,
    )(*fulls)


def _adamw(w, g, m, v):
    m1 = ADAM_B1 * m + (1.0 - ADAM_B1) * g
    v1 = ADAM_B2 * v + (1.0 - ADAM_B2) * (g * g)
    m_hat = m1 / (1.0 - ADAM_B1 ** ADAM_STEP)
    v_hat = v1 / (1.0 - ADAM_B2 ** ADAM_STEP)
    delta = -ADAM_LR * (m_hat / (jnp.sqrt(v_hat) + ADAM_EPS) + ADAM_WD * w)
    return delta, m1, v1


def _adamw_call(w, g, m, v, name):
    rows, cols = w.shape

    def body(w_ref, g_ref, m_ref, v_ref, d_out, m_out, v_out):
        delta, m1, v1 = _adamw(w_ref[...], g_ref[...], m_ref[...], v_ref[...])
        d_out[...] = delta
        m_out[...] = m1
        v_out[...] = v1

    if rows % 8 == 0:
        tr = _rows_tile(rows)
        blk, grid = pl.BlockSpec((tr, cols), lambda i: (i, 0)), (rows // tr,)
    else:
        blk, grid = pl.BlockSpec((rows, 128), lambda i: (0, i)), (cols // 128,)
    return pl.pallas_call(
        body, name=name, grid=grid, in_specs=[blk] * 4, out_specs=[blk] * 3,
        out_shape=[jax.ShapeDtypeStruct((rows, cols), F32)] * 3, compiler_params=_params(),
    )(w, g, m, v)


def _small_allreduce(vals):
    def body(v_ref, out_ref, buf, send_sems, recv_sems):
        x, y, c, j = _place()
        me = 2 * j + c
        buf[0] = v_ref[...]

        def copy(r):
            return pltpu.make_async_remote_copy(
                src_ref=v_ref, dst_ref=buf.at[r], send_sem=send_sems.at[r - 1], recv_sem=recv_sems.at[r - 1],
                device_id=(x ^ (r >> 2), y ^ ((r >> 1) & 1), c ^ (r & 1)), device_id_type=_MESH)

        for r in range(1, 8):
            copy(r).start()
        for r in range(1, 8):
            copy(r).wait()
        acc = buf[me ^ 0]
        for d in range(1, 8):
            acc = acc + buf[me ^ d]
        out_ref[...] = acc

    return pl.pallas_call(
        body, name="small_allreduce", out_shape=jax.ShapeDtypeStruct((_SMALL_ROWS, D), F32),
        in_specs=[_VMEM], out_specs=_VMEM,
        scratch_shapes=[pltpu.VMEM((8, _SMALL_ROWS, D), F32), pltpu.SemaphoreType.DMA((7,)),
                        pltpu.SemaphoreType.DMA((7,))],
    )(vals)


_NAMES = ("norm_mix_g", "w_in", "conv_qk", "b_if", "mlstm_norm_g", "sinks", "w_branch_a", "w_branch_b", "w_out",
          "norm_mlp_g", "w_up", "w_down", "norm_ple_g", "w_ple_gate", "w_ple_proj", "final_norm_g")
_GROUP_NAMES = ("w_in", "w4", "w_up", "w_down", "w_ple_proj")


def _step(x, p, target, w, m, v):
    c = lax.axis_index("c")
    j = 2 * lax.axis_index("x") + lax.axis_index("y")

    def shards(d):
        return {n: d[n][0] for n in _SHARDED_NAMES}

    ws = shards(w)
    w_in_all, conv_all = _allgather_weights([ws["w_in"].astype(BF16)], ws["conv_qk"])
    rows_pp = PLE * (D // 4) // D
    rest = jnp.concatenate([ws[n] for n in _W4] + [ws["w_up"], ws["w_down"], ws["w_ple_proj"].reshape(rows_pp, D)],
                           axis=0)
    rest = (rest + 0.0 * conv_all[0, 0, 0]).astype(BF16)
    send_sems, recv_sems, rest_thru, land_thru, token = _late_gather_start(rest)
    full = {n: w[n] for n in ("mlstm_norm_g", "norm_mlp_g", "norm_ple_g", "b_if", "sinks")}
    full["norm_mix_g"] = w["norm_mix_g"] + token[0, 0]
    full["final_norm_g"] = w["final_norm_g"].reshape(1, D)
    full["w_in"] = _win_pad(w_in_all)
    full["conv_qk"] = jnp.swapaxes(conv_all, 0, 1).reshape(CONV, D)

    def late_weights(after):
        land = _late_gather_wait(send_sems, recv_sems, rest_thru, land_thru, after)
        out = {n: land[:, i * (D // 4):(i + 1) * (D // 4)].reshape(D, D) for i, n in enumerate(_W4)}
        out["w_up"] = land[:, D:2 * D]
        out["w_down"] = land[:, 2 * D:3 * D].reshape(DFF, D)
        out["w_ple_proj"] = jnp.swapaxes(land[:, 3 * D:3 * D + rows_pp].reshape(4, PLE, D // 4), 0, 1).reshape(PLE, D)
        return out

    early, last = {}, {}

    def pair_sums(by_dest, theirs, names):
        return [_pair_sum(a, b, j, c, "pair_sum_" + n) for a, b, n in zip(by_dest, theirs, names)]

    def early_grads(g):
        by_dest = [jnp.stack([g[n].reshape(4, D // 4, D) for n in _W4], axis=1).reshape(4, D, D),
                   g["w_up"], g["w_down"].reshape(4, DFF // 4, D), g["w_ple_proj"]]
        *early["pair"], token = _pair_exchange_start(by_dest, "early")
        return token[0, 0]

    def mid_grads(after):
        early["sums"] = pair_sums(*_pair_exchange_wait(*early["pair"], after, "early"), _GROUP_NAMES[1:])
        *early["flight"], token = _chip_exchange_start([s[1] for s in early["sums"]], "early")
        return token[0, 0]

    def last_grad(g):
        *last["pair"], token = _pair_exchange_start([_win_unpad(g["w_in"])], "w_in")
        return token[0, 0]

    loss, grad_x, g = _local_step(x[0], p[0, 0], target[0], full, late_weights, early_grads, mid_grads, last_grad)

    last["sums"] = pair_sums(*_pair_exchange_wait(*last["pair"], grad_x, "w_in"), _GROUP_NAMES[:1])
    *last["flight"], token = _chip_exchange_start([s[1] for s in last["sums"]], "w_in")

    def reduce_share(sums, others, names, tag):
        halves = [_reduce4(s[0], b, c, "reduce4_" + n) for s, b, n in zip(sums, others, names)]
        return list(_sibling_share(halves, "sibling_share_" + tag))

    ms, vs = shards(m), shards(v)
    grads = reduce_share(early["sums"], _chip_exchange_wait(*early["flight"], token, "early"), _GROUP_NAMES[1:], "early")
    upd = [_adamw_call(wa, ga, ma, va, "adamw_" + n)
           for wa, ga, ma, va, n in zip(_group(ws)[1:], grads, _group(ms)[1:], _group(vs)[1:], _GROUP_NAMES[1:])]
    small_g = _small_allreduce(_pack_small(g, extra=loss, conv=g["conv_qk"]))
    conv_g = lax.dynamic_slice(small_g[_CONV_ROW:_CONV_ROW + CONV], (0, j * (D // 4)), (CONV, D // 4))
    conv_upd = _adamw_call(ws["conv_qk"], conv_g, ms["conv_qk"], vs["conv_qk"], "adamw_conv")
    small_upd = _adamw_call(_pack_small(w), small_g, _pack_small(m), _pack_small(v), "adamw_small")

    done = sum(a[0][0:1, 0:1] for a in upd + [conv_upd, small_upd])
    others = _chip_exchange_wait(*last["flight"], done, "w_in")
    grads = reduce_share(last["sums"], others, _GROUP_NAMES[:1], "w_in") + list(grads)
    upd_in = _adamw_call(*[jnp.swapaxes(a, 0, 1) for a in (ws["w_in"], grads[0], ms["w_in"], vs["w_in"])], "adamw_w_in")
    upd = [[jnp.swapaxes(a, 0, 1) for a in upd_in]] + upd

    shapes = {n: w[n].shape for n in _NAMES}
    res = []
    for k in range(4):
        big = _ungroup(list(grads) if k == 0 else [u[k - 1] for u in upd])
        big["conv_qk"] = conv_g if k == 0 else conv_upd[k - 1]
        leaves = _unpack_small(small_g if k == 0 else small_upd[k - 1], shapes)
        leaves.update({n: a.reshape(shapes[n]) for n, a in big.items()})
        res.append(leaves)

    out = [small_g[5, 8 + SWH], grad_x[None]]
    for k in range(4):
        out += [res[k][n] for n in _NAMES]
    return tuple(out)


def kernel(x, p, norm_mix_g, w_in, conv_qk, b_if, mlstm_norm_g, sinks, w_branch_a, w_branch_b, w_out, norm_mlp_g, w_up, w_down, norm_ple_g, w_ple_gate, w_ple_proj, final_norm_g, loss_target, m_norm_mix_g, m_w_in, m_conv_qk, m_b_if, m_mlstm_norm_g, m_sinks, m_w_branch_a, m_w_branch_b, m_w_out, m_norm_mlp_g, m_w_up, m_w_down, m_norm_ple_g, m_w_ple_gate, m_w_ple_proj, m_final_norm_g, v_norm_mix_g, v_w_in, v_conv_qk, v_b_if, v_mlstm_norm_g, v_sinks, v_w_branch_a, v_w_branch_b, v_w_out, v_norm_mlp_g, v_w_up, v_w_down, v_norm_ple_g, v_w_ple_gate, v_w_ple_proj, v_final_norm_g):
    w = dict(zip(_NAMES, (norm_mix_g, w_in, conv_qk, b_if, mlstm_norm_g, sinks, w_branch_a, w_branch_b, w_out,
                          norm_mlp_g, w_up, w_down, norm_ple_g, w_ple_gate, w_ple_proj, final_norm_g)))
    m = dict(zip(_NAMES, (m_norm_mix_g, m_w_in, m_conv_qk, m_b_if, m_mlstm_norm_g, m_sinks, m_w_branch_a,
                          m_w_branch_b, m_w_out, m_norm_mlp_g, m_w_up, m_w_down, m_norm_ple_g, m_w_ple_gate,
                          m_w_ple_proj, m_final_norm_g)))
    v = dict(zip(_NAMES, (v_norm_mix_g, v_w_in, v_conv_qk, v_b_if, v_mlstm_norm_g, v_sinks, v_w_branch_a,
                          v_w_branch_b, v_w_out, v_norm_mlp_g, v_w_up, v_w_down, v_norm_ple_g, v_w_ple_gate,
                          v_w_ple_proj, v_final_norm_g)))
    return _step(x, p, loss_target, w, m, v)
```

```python
import jax
import jax.numpy as jnp
from jax import lax
from jax.experimental import pallas as pl
from jax.experimental.pallas import tpu as pltpu

F32 = jnp.float32
BF16 = jnp.bfloat16

D = 1024
PLE = 256
MLH = 4
DQK = 128
DV = 256
CONV = 4
CHUNK = 256
SWH = 16
SWKV = 4
SWG = SWH // SWKV
HD = 64
WIN = 128
DFF = 4096
EPS = 1e-6
N_IN = 6664
NP = 7168
C_QK, C_V, C_O, C_QSW, C_GA, C_GB, C_KV, C_IF = 0, 1024, 2048, 3072, 4096, 5120, 6144, 6656
IFW = NP - C_IF

ADAM_LR = 0.001
ADAM_B1 = 0.9
ADAM_B2 = 0.999
ADAM_EPS = 1e-08
ADAM_WD = 0.01
ADAM_STEP = 10

TOK_TILE = 512
V7X_VMEM_BYTES = 64 * 1024 * 1024
VMEM_LIMIT = V7X_VMEM_BYTES - 6 * 1024 * 1024


def _params(**kw):
    return pltpu.CompilerParams(vmem_limit_bytes=VMEM_LIMIT, **kw)


def _pick(n, cap):
    if n <= cap:
        return n
    t = cap - cap % 128
    while t > 128 and n % t:
        t -= 128
    assert n % t == 0, (n, cap)
    return t


def _dot(a, b, dims):
    return lax.dot_general(a, b, (dims, ((), ())), preferred_element_type=F32)


def _dot_nn(a, b):
    return _dot(a, b, ((1,), (0,)))


def _dot_nt(a, b):
    return _dot(a, b, ((1,), (1,)))


def _dot_tn(a, b):
    return _dot(a, b, ((0,), (0,)))


def _sigmoid(x):
    return 1.0 / (1.0 + jnp.exp(-x))


def _mm(a, b, mode, out_dtype, name, out_chunks=1):
    if mode == "nn":
        (m, k), (k2, n) = a.shape, b.shape
    elif mode == "nt":
        (m, k), (n, k2) = a.shape, b.shape
    else:
        (k, m), (k2, n) = a.shape, b.shape
    assert k == k2, (a.shape, b.shape, mode)
    tm, tn, tk = _pick(m, 1024), _pick(n // out_chunks, 1024), _pick(k, 2048)
    nk = k // tk
    if mode == "nn":
        a_spec = pl.BlockSpec((tm, tk), lambda i, j, kk: (i, kk))
        b_spec = pl.BlockSpec((tk, tn), lambda i, j, kk: (kk, j))
        dot = _dot_nn
    elif mode == "nt":
        a_spec = pl.BlockSpec((tm, tk), lambda i, j, kk: (i, kk))
        b_spec = pl.BlockSpec((tn, tk), lambda i, j, kk: (j, kk))
        dot = _dot_nt
    else:
        a_spec = pl.BlockSpec((tk, tm), lambda i, j, kk: (kk, i))
        b_spec = pl.BlockSpec((tk, tn), lambda i, j, kk: (kk, j))
        dot = _dot_tn
    if out_chunks > 1:
        npc = (n // out_chunks) // tn
        out_spec = pl.BlockSpec((None, tm, tn), lambda i, j, kk: (j // npc, i, j % npc))
        out_shape = jax.ShapeDtypeStruct((out_chunks, m, n // out_chunks), out_dtype)
    else:
        out_spec = pl.BlockSpec((tm, tn), lambda i, j, kk: (i, j))
        out_shape = jax.ShapeDtypeStruct((m, n), out_dtype)

    def body(a_ref, b_ref, o_ref, acc_ref):
        kk = pl.program_id(2)

        @pl.when(kk == 0)
        def _():
            acc_ref[...] = jnp.zeros_like(acc_ref)

        acc_ref[...] += dot(a_ref[...], b_ref[...])

        @pl.when(kk == nk - 1)
        def _():
            o_ref[...] = acc_ref[...].astype(out_dtype)

    return pl.pallas_call(
        body, name=name, grid=(m // tm, n // tn, nk),
        in_specs=[a_spec, b_spec], out_specs=out_spec, out_shape=out_shape,
        scratch_shapes=[pltpu.VMEM((tm, tn), F32)],
        compiler_params=_params(dimension_semantics=("parallel", "parallel", "arbitrary")),
    )(a, b)


def _tile(col0=0):
    return lambda tm, tn: pl.BlockSpec((tm, tn), lambda i, j, kk: (i, col0 // tn + j))


def _row():
    return lambda tm, tn: pl.BlockSpec((1, tn), lambda i, j, kk: (0, j))


def _mm_ep(pairs, mode, name, epilogue, ins, outs, tm, tn, aliases=None, row_split=1, init=None):
    a0, b0 = pairs[0]
    bch = b0.shape[0] if b0.ndim == 3 else 1
    m, k = a0.shape
    tm = _pick(m, tm)
    n = b0.shape[-1] * bch if mode == "nn" else b0.shape[-2]
    tk = _pick(k // bch if mode == "nt" else k, 2048)
    nk = k // tk
    a_spec = pl.BlockSpec((tm, tk), lambda i, j, kk: (i, kk))
    if mode == "nn":
        dot = _dot_nn
        if bch > 1:
            bpc = (n // bch) // tn
            b_spec = pl.BlockSpec((None, tk, tn), lambda i, j, kk: (j // bpc, kk, j % bpc))
        else:
            b_spec = pl.BlockSpec((tk, tn), lambda i, j, kk: (kk, j))
    else:
        dot = _dot_nt
        if bch > 1:
            bpc = (k // bch) // tk
            b_spec = pl.BlockSpec((None, tn, tk), lambda i, j, kk: (kk // bpc, j, kk % bpc))
        else:
            b_spec = pl.BlockSpec((tn, tk), lambda i, j, kk: (j, kk))
    npair, nin, nout = len(pairs), len(ins), len(outs)
    rows = tm // row_split
    assert init is None or row_split > 1

    def body_split(*refs):
        ab = refs[:2 * npair]
        in_refs = refs[2 * npair:2 * npair + nin]
        out_refs = refs[2 * npair + nin:2 * npair + nin + nout]
        accs = refs[2 * npair + nin + nout:]
        i, j, kk = pl.program_id(0), pl.program_id(1), pl.program_id(2)

        if init is not None:
            @pl.when((i == 0) & (kk == 0))
            def _():
                init(out_refs)

        @pl.when(kk < nk - 1)
        def _():
            for p in range(npair):
                prod = dot(ab[2 * p][...], ab[2 * p + 1][...])

                @pl.when(kk == 0)
                def _():
                    accs[p][...] = prod

                @pl.when(kk > 0)
                def _():
                    accs[p][...] += prod

        @pl.when(kk == nk - 1)
        def _():
            for r in range(row_split):
                rs = pl.ds(r * rows, rows)
                tot = []
                for p in range(npair):
                    prod = dot(ab[2 * p][rs, :], ab[2 * p + 1][...])
                    tot.append(prod if nk == 1 else accs[p][rs, :] + prod)

                def view(ref):
                    return ref.at[rs] if ref.shape[0] == tm else ref

                epilogue(tot, [view(x) for x in in_refs], [view(x) for x in out_refs], i * row_split + r, j)

    def body(*refs):
        ab = refs[:2 * npair]
        in_refs = refs[2 * npair:2 * npair + nin]
        out_refs = refs[2 * npair + nin:2 * npair + nin + nout]
        accs = refs[2 * npair + nin + nout:]
        i, j, kk = pl.program_id(0), pl.program_id(1), pl.program_id(2)
        for p in range(npair):
            prod = dot(ab[2 * p][...], ab[2 * p + 1][...])

            @pl.when(kk == 0)
            def _():
                accs[p][...] = prod

            @pl.when(kk > 0)
            def _():
                accs[p][...] += prod

        @pl.when(kk == nk - 1)
        def _():
            epilogue([acc[...] for acc in accs], in_refs, out_refs, i, j)

    operands = [x for pair in pairs for x in pair] + [a for a, _ in ins]
    io_alias = {2 * npair + i: o for i, o in (aliases or {}).items()}
    return pl.pallas_call(
        body if row_split == 1 else body_split, name=name, grid=(m // tm, n // tn, nk),
        in_specs=[a_spec, b_spec] * npair + [mk(tm, tn) for _, mk in ins],
        out_specs=[mk(tm, tn) for _, mk in outs], out_shape=[s for s, _ in outs],
        scratch_shapes=[pltpu.VMEM((tm, tn), F32)] * npair, input_output_aliases=io_alias,
        compiler_params=_params(dimension_semantics=("arbitrary", "arbitrary", "arbitrary")),
    )(*operands)


def _tok(w, j=0):
    return pl.BlockSpec((TOK_TILE, w), lambda i: (i, j))


def _rep(shape):
    return pl.BlockSpec(shape, lambda i: (0,) * len(shape))


def _rms(x):
    rstd = lax.rsqrt(jnp.mean(x * x, axis=-1, keepdims=True) + EPS)
    return x * rstd, rstd


def _rms_bwd(xn, rstd, dxn):
    return rstd * (dxn - xn * jnp.mean(dxn * xn, axis=-1, keepdims=True))


def _halo_prev(w, j=0, rows=8):
    r = TOK_TILE // rows
    return pl.BlockSpec((rows, w), lambda i: (jnp.maximum(i * r - 1, 0), j))


def _last8(halo_ref):
    return halo_ref[...].astype(F32)[halo_ref.shape[0] - 8:]


def _halo_next(w, nt, j=0):
    r = TOK_TILE // 8
    return pl.BlockSpec((8, w), lambda i: (jnp.minimum((i + 1) * r, nt * r - 1), j))


def _shift_down(x, halo, s):
    if s == 0:
        return x
    r = pltpu.roll(x, s, 0)
    hs = pltpu.roll(halo, s, 0)
    row = lax.broadcasted_iota(jnp.int32, hs.shape, 0)
    top = jnp.where(row < s, hs, r[0:8])
    return jnp.concatenate([top, r[8:]], axis=0)


def _shift_up(x, halo, s):
    if s == 0:
        return x
    n = x.shape[0]
    r = pltpu.roll(x, n - s, 0)
    hs = pltpu.roll(halo, 8 - s, 0)
    row = lax.broadcasted_iota(jnp.int32, hs.shape, 0)
    bot = jnp.where(row >= 8 - s, hs, r[n - 8:])
    return jnp.concatenate([r[:n - 8], bot], axis=0)


def _bf(x):
    return x.astype(BF16).astype(F32)


def _conv_taps(x, halo, w):
    x, halo, w = _bf(x), _bf(halo), _bf(w)
    acc = x * w[CONV - 1:CONV, :]
    for j in range(CONV - 1):
        acc = acc + _shift_down(x, halo, CONV - 1 - j) * w[j:j + 1, :]
    return acc


_Q_SCALE = DQK ** -0.5


def _qscale_row():
    lane = lax.broadcasted_iota(jnp.int32, (1, D), 1)
    return jnp.where(lane < MLH * DQK, _Q_SCALE, 1.0).astype(F32)


def _conv_silu_fwd(proj, conv_w):
    t = proj.shape[0]

    def body(x_ref, halo_ref, w_ref, o_ref):
        halo = jnp.where(pl.program_id(0) > 0, _last8(halo_ref), 0.0)
        c = _conv_taps(x_ref[...].astype(F32), halo, w_ref[...])
        o_ref[...] = (c * _sigmoid(c) * _qscale_row()).astype(BF16)

    return pl.pallas_call(
        body, name="conv_silu_fwd", grid=(t // TOK_TILE,),
        in_specs=[_tok(D, C_QK // D), _halo_prev(D, C_QK // D, 16), _rep((CONV, D))], out_specs=_tok(D),
        out_shape=jax.ShapeDtypeStruct((t, D), BF16), compiler_params=_params(),
    )(proj, proj, conv_w)


def _conv_silu_bwd_a(proj, conv_w, dqk):
    t = proj.shape[0]

    def body(x_ref, halo_ref, w_ref, d_ref, dc_ref, dw_ref):
        @pl.when(pl.program_id(0) == 0)
        def _():
            dw_ref[...] = jnp.zeros_like(dw_ref)

        halo = jnp.where(pl.program_id(0) > 0, _last8(halo_ref), 0.0)
        x = x_ref[...].astype(F32)
        c = _conv_taps(x, halo, w_ref[...])
        s = _sigmoid(c)
        dc = d_ref[...] * _qscale_row() * (s * (1.0 + c * (1.0 - s)))
        dc_ref[...] = dc
        dcb, xb, halo_b = _bf(dc), _bf(x), _bf(halo)
        for j in range(CONV):
            dw_ref[j:j + 1, :] += jnp.sum(dcb * _shift_down(xb, halo_b, CONV - 1 - j), axis=0, keepdims=True)

    return pl.pallas_call(
        body, name="conv_silu_bwd_a", grid=(t // TOK_TILE,),
        in_specs=[_tok(D, C_QK // D), _halo_prev(D, C_QK // D, 16), _rep((CONV, D)), _tok(D)],
        out_specs=[_tok(D), _rep((CONV, D))],
        out_shape=[jax.ShapeDtypeStruct((t, D), F32), jax.ShapeDtypeStruct((CONV, D), F32)],
        compiler_params=_params(),
    )(proj, proj, conv_w, dqk)


def _conv_silu_bwd_b(dc, conv_w, dproj):
    t = dc.shape[0]
    nt = t // TOK_TILE

    def body(dc_ref, halo_ref, w_ref, _, dx_ref):
        halo = _bf(jnp.where(pl.program_id(0) < nt - 1, halo_ref[...], 0.0))
        dcv = _bf(dc_ref[...])
        w = _bf(w_ref[...])
        acc = dcv * w[CONV - 1:CONV, :]
        for j in range(CONV - 1):
            acc = acc + _shift_up(dcv, halo, CONV - 1 - j) * w[j:j + 1, :]
        dx_ref[...] = acc.astype(BF16)

    return pl.pallas_call(
        body, name="conv_silu_bwd_b", grid=(nt,), in_specs=[_tok(D), _halo_next(D, nt), _rep((CONV, D)), _ANY],
        out_specs=_tok(D, C_QK // D), out_shape=jax.ShapeDtypeStruct((t, NP), BF16),
        input_output_aliases={3: 0}, compiler_params=_params(),
    )(dc, dc, conv_w, dproj)


def _gates_fwd(pre_rows, bias_col):
    t = pre_rows.shape[1]

    def body(p_ref, b_ref, g_ref, s_ref):
        z = p_ref[...] + b_ref[...]
        lf = jnp.minimum(z, 0.0) - jnp.log(1.0 + jnp.exp(-jnp.abs(z)))
        lane = lax.broadcasted_iota(jnp.int32, z.shape, 1) % CHUNK
        cum = lf
        s = 1
        while s < CHUNK:
            cum = cum + jnp.where(lane >= s, pltpu.roll(cum, s, 1), 0.0)
            s *= 2
        sub = lax.broadcasted_iota(jnp.int32, z.shape, 0)
        g_ref[...] = jnp.where(sub < MLH, z, cum)
        s_ref[...] = _sigmoid(-z)

    return pl.pallas_call(
        body, name="gates_fwd",
        out_shape=[jax.ShapeDtypeStruct((8, t), F32), jax.ShapeDtypeStruct((8, t), F32)],
        compiler_params=_params(),
    )(pre_rows, bias_col)


def _chunk_terms(grow, gcol, m0):
    heads = range(MLH)
    i_row = [grow[h:h + 1, :] for h in heads]
    b_row = [grow[MLH + h:MLH + h + 1, :] for h in heads]
    i_col = [gcol[:, h:h + 1] for h in heads]
    b_col = [gcol[:, MLH + h:MLH + h + 1] for h in heads]
    b_last = [b_row[h][:, CHUNK - 1:CHUNK] for h in heads]
    tt = lax.broadcasted_iota(jnp.int32, (CHUNK, CHUNK), 0)
    ss = lax.broadcasted_iota(jnp.int32, (CHUNK, CHUNK), 1)
    log_d = [jnp.where(tt >= ss, b_col[h] - b_row[h] + i_row[h], -jnp.inf) for h in heads]
    row_max = [jnp.max(log_d[h], axis=1, keepdims=True) for h in heads]
    last_max = [jnp.max(b_last[h] - b_row[h] + i_row[h], axis=1, keepdims=True) for h in heads]
    m_t = [jnp.maximum(b_col[h] + m0[h], row_max[h]) for h in heads]
    m1 = [jnp.maximum(b_last[h] + m0[h], last_max[h]) for h in heads]
    dm = [jnp.exp(log_d[h] - m_t[h]) for h in heads]
    wi = [jnp.exp(b_col[h] + m0[h] - m_t[h]) for h in heads]
    ws = [jnp.exp(b_last[h] - b_col[h] + i_col[h] - m1[h]) for h in heads]
    dec = [jnp.exp(b_last[h] + m0[h] - m1[h]) for h in heads]
    return [(dm[h], wi[h], m_t[h], ws[h], dec[h], m1[h]) for h in heads]


def _mlstm_fwd(qk, proj, grow, gcol, gain):
    t = qk.shape[0]
    nc = t // CHUNK

    def body(qk_ref, v_ref, o_ref, grow_ref, gcol_ref, g_ref, h_ref, y_ref, cs_ref, st_ref, c_scr, st_scr):
        @pl.when(pl.program_id(0) == 0)
        def _():
            c_scr[...] = jnp.zeros_like(c_scr)
            st_scr[...] = jnp.zeros_like(st_scr)

        grow_v, gcol_v = grow_ref[...], gcol_ref[...]
        heads = range(MLH)
        q = [qk_ref[:, h * DQK:(h + 1) * DQK] for h in heads]
        k = [qk_ref[:, MLH * DQK + h * DQK:MLH * DQK + (h + 1) * DQK] for h in heads]
        v = [v_ref[:, h * DV:(h + 1) * DV] for h in heads]
        c0 = [c_scr[h] for h in heads]
        n0 = [st_scr[h, 0:1, :] for h in heads]
        for h in heads:
            cs_ref[0, h] = c0[h]
            st_ref[0, h] = st_scr[h]
        terms = _chunk_terms(grow_v, gcol_v, [st_scr[h, 1:2, 0:1] for h in heads])
        a = [_dot_nt(q[h], k[h]) for h in heads]
        qc = [_dot_nt(q[h], c0[h].astype(BF16)) for h in heads]
        s = [a[h] * terms[h][0] for h in heads]
        sv = [_dot_nn(s[h].astype(BF16), v[h]) for h in heads]
        upd = [_dot_tn((terms[h][3] * v[h]).astype(BF16), k[h]) for h in heads]
        den = [terms[h][1] * jnp.sum(q[h].astype(F32) * n0[h], axis=1, keepdims=True)
               + jnp.sum(s[h], axis=1, keepdims=True) for h in heads]
        hv = [(terms[h][1] * qc[h] + sv[h]) / jnp.maximum(jnp.abs(den[h]), jnp.exp(-terms[h][2])) for h in heads]
        for h in heads:
            sl = slice(h * DV, (h + 1) * DV)
            h_ref[:, sl] = hv[h]
            xn, _ = _rms(hv[h])
            y_ref[:, sl] = (_sigmoid(o_ref[:, sl].astype(F32)) * xn * g_ref[:, sl]).astype(BF16)
        for h in heads:
            dec, m1 = terms[h][4], terms[h][5]
            c_scr[h] = dec * c0[h] + upd[h]
            st_scr[h, 0:1, :] = dec * n0[h] + jnp.sum(terms[h][3] * k[h].astype(F32), axis=0, keepdims=True)
            st_scr[h, 1:2, :] = jnp.broadcast_to(m1, (1, DQK))

    return pl.pallas_call(
        body, name="mlstm_fwd", grid=(nc,),
        in_specs=[pl.BlockSpec((CHUNK, D), lambda c: (c, 0)), pl.BlockSpec((CHUNK, D), lambda c: (c, C_V // D)),
                  pl.BlockSpec((CHUNK, D), lambda c: (c, C_O // D)),
                  pl.BlockSpec((8, CHUNK), lambda c: (0, c)), pl.BlockSpec((CHUNK, 8), lambda c: (c, 0)),
                  pl.BlockSpec((1, D), lambda c: (0, 0))],
        out_specs=[pl.BlockSpec((CHUNK, D), lambda c: (c, 0)), pl.BlockSpec((CHUNK, D), lambda c: (c, 0)),
                   pl.BlockSpec((1, MLH, DV, DQK), lambda c: (c, 0, 0, 0)),
                   pl.BlockSpec((1, MLH, 8, DQK), lambda c: (c, 0, 0, 0))],
        out_shape=[jax.ShapeDtypeStruct((t, D), F32), jax.ShapeDtypeStruct((t, D), BF16),
                   jax.ShapeDtypeStruct((nc, MLH, DV, DQK), F32), jax.ShapeDtypeStruct((nc, MLH, 8, DQK), F32)],
        scratch_shapes=[pltpu.VMEM((MLH, DV, DQK), F32), pltpu.VMEM((MLH, 8, DQK), F32)],
        compiler_params=_params(dimension_semantics=("arbitrary",)),
    )(qk, proj, proj, grow, gcol, gain)


def _mlstm_bwd(qk, proj, grow, gcol, sneg_col, cs, st, hraw, dh, dproj):
    t = qk.shape[0]
    nc = t // CHUNK

    def rev(c):
        return nc - 1 - c

    def nxt(c):
        return jnp.minimum(nc - c, nc - 1)

    def body(qk_ref, v_ref, grow_ref, gcol_ref, sneg_ref, cs_ref, st_ref, cs1_ref, st1_ref, h_ref, dh_ref, _,
             dqk_ref, dv_ref, dif_ref, dbif_ref, dc_scr, dn_scr):
        @pl.when(pl.program_id(0) == 0)
        def _():
            dc_scr[...] = jnp.zeros_like(dc_scr)
            dn_scr[...] = jnp.zeros_like(dn_scr)
            dbif_ref[...] = jnp.zeros_like(dbif_ref)

        grow_v, gcol_v, sneg = grow_ref[...], gcol_ref[...], sneg_ref[...]
        tt = lax.broadcasted_iota(jnp.int32, (CHUNK, CHUNK), 0)
        ss = lax.broadcasted_iota(jnp.int32, (CHUNK, CHUNK), 1)
        lane8 = lax.broadcasted_iota(jnp.int32, (CHUNK, 8), 1)
        heads = range(MLH)
        q = [qk_ref[:, h * DQK:(h + 1) * DQK] for h in heads]
        k = [qk_ref[:, MLH * DQK + h * DQK:MLH * DQK + (h + 1) * DQK] for h in heads]
        qf, kf = [a.astype(F32) for a in q], [a.astype(F32) for a in k]
        vb = [v_ref[:, h * DV:(h + 1) * DV].astype(BF16) for h in heads]
        c0 = [cs_ref[0, h] for h in heads]
        n0 = [st_ref[0, h, 0:1, :] for h in heads]
        dc1 = [dc_scr[h] for h in heads]
        dn1 = [dn_scr[h, 0:1, :] for h in heads]
        terms = _chunk_terms(grow_v, gcol_v, [st_ref[0, h, 1:2, 0:1] for h in heads])
        dm, wi, ws = [t[0] for t in terms], [t[1] for t in terms], [t[3] for t in terms]
        s = [_dot_nt(q[h], k[h]) * dm[h] for h in heads]
        den = [wi[h] * jnp.sum(qf[h] * n0[h], axis=1, keepdims=True) + jnp.sum(s[h], axis=1, keepdims=True)
               for h in heads]
        floor = [jnp.exp(-terms[h][2]) for h in heads]
        g = [jnp.maximum(jnp.abs(den[h]), floor[h]) for h in heads]
        dh_v = [dh_ref[:, h * DV:(h + 1) * DV] for h in heads]
        dnum = [dh_v[h] / g[h] for h in heads]
        dden = [-jnp.sum(dh_v[h] * h_ref[:, h * DV:(h + 1) * DV], axis=1, keepdims=True) / g[h] for h in heads]
        dden = [jnp.where(jnp.abs(den[h]) > floor[h], dden[h] * jnp.sign(den[h]), 0.0) for h in heads]
        dnum_b = [a.astype(BF16) for a in dnum]
        dc1_b = [a.astype(BF16) for a in dc1]
        da = [((_dot_nt(dnum_b[h], vb[h]) + dden[h]) * dm[h]).astype(BF16) for h in heads]
        dq_inter = [_dot_nn(dnum_b[h], c0[h].astype(BF16)) for h in heads]
        dk_inter = [_dot_nn(vb[h], dc1_b[h]) for h in heads]
        dv_inter = [_dot_nt(k[h], dc1_b[h]) for h in heads]
        dc_new = [_dot_tn((wi[h] * dnum[h]).astype(BF16), q[h]) for h in heads]
        dq = [_dot_nn(da[h], k[h]) + wi[h] * (dq_inter[h] + dden[h] * n0[h]) for h in heads]
        dk = [_dot_tn(da[h], q[h]) + ws[h] * (dk_inter[h] + dn1[h]) for h in heads]
        dv = [_dot_tn(s[h].astype(BF16), dnum_b[h]) + ws[h] * dv_inter[h] for h in heads]
        for h in heads:
            dqk_ref[:, h * DQK:(h + 1) * DQK] = dq[h]
            dqk_ref[:, MLH * DQK + h * DQK:MLH * DQK + (h + 1) * DQK] = dk[h]
            dv_ref[:, h * DV:(h + 1) * DV] = dv[h].astype(BF16)
        rk = [jnp.sum(kf[h] * dk[h], axis=1, keepdims=True) for h in heads]
        df = [jnp.sum(qf[h] * dq[h], axis=1, keepdims=True) - rk[h] for h in heads]
        df_row = [jnp.sum(jnp.where(tt == ss, df[h], 0.0), axis=0, keepdims=True) for h in heads]
        suffix = [jnp.sum(jnp.where(ss >= tt, df_row[h], 0.0), axis=1, keepdims=True) for h in heads]
        cross = [jnp.sum(jnp.sum(dc1[h] * cs1_ref[0, h], axis=0, keepdims=True), axis=1, keepdims=True)
                 + jnp.sum(dn1[h] * st1_ref[0, h, 0:1, :], axis=1, keepdims=True) for h in heads]
        dif = jnp.zeros((CHUNK, 8), F32)
        for h in heads:
            dpf = (suffix[h] + cross[h]) * sneg[:, MLH + h:MLH + h + 1]
            dif = dif + jnp.where(lane8 == h, rk[h], 0.0) + jnp.where(lane8 == MLH + h, dpf, 0.0)
            dc_scr[h] = terms[h][4] * dc1[h] + dc_new[h]
            dn_scr[h, 0:1, :] = terms[h][4] * dn1[h] + jnp.sum(wi[h] * dden[h] * qf[h], axis=0, keepdims=True)
        dif_ref[...] = dif
        dbif_ref[...] += jnp.sum(dif, axis=0, keepdims=True)

    return pl.pallas_call(
        body, name="mlstm_bwd", grid=(nc,),
        in_specs=[pl.BlockSpec((CHUNK, D), lambda c: (rev(c), 0)),
                  pl.BlockSpec((CHUNK, D), lambda c: (rev(c), C_V // D)),
                  pl.BlockSpec((8, CHUNK), lambda c: (0, rev(c))),
                  pl.BlockSpec((CHUNK, 8), lambda c: (rev(c), 0)),
                  pl.BlockSpec((CHUNK, 8), lambda c: (rev(c), 0)),
                  pl.BlockSpec((1, MLH, DV, DQK), lambda c: (rev(c), 0, 0, 0)),
                  pl.BlockSpec((1, MLH, 8, DQK), lambda c: (rev(c), 0, 0, 0)),
                  pl.BlockSpec((1, MLH, DV, DQK), lambda c: (nxt(c), 0, 0, 0)),
                  pl.BlockSpec((1, MLH, 8, DQK), lambda c: (nxt(c), 0, 0, 0)),
                  pl.BlockSpec((CHUNK, D), lambda c: (rev(c), 0)),
                  pl.BlockSpec((CHUNK, D), lambda c: (rev(c), 0)), _ANY],
        out_specs=[pl.BlockSpec((CHUNK, D), lambda c: (rev(c), 0)),
                   pl.BlockSpec((CHUNK, D), lambda c: (rev(c), C_V // D)),
                   pl.BlockSpec((CHUNK, 8), lambda c: (rev(c), 0)),
                   pl.BlockSpec((1, 8), lambda c: (0, 0))],
        out_shape=[jax.ShapeDtypeStruct((t, D), F32), jax.ShapeDtypeStruct((t, NP), BF16),
                   jax.ShapeDtypeStruct((t, 8), F32), jax.ShapeDtypeStruct((1, 8), F32)],
        scratch_shapes=[pltpu.VMEM((MLH, DV, DQK), F32), pltpu.VMEM((MLH, 8, DQK), F32)],
        input_output_aliases={11: 1}, compiler_params=_params(dimension_semantics=("arbitrary",)),
    )(qk, proj, grow, gcol, sneg_col, cs, st, cs, st, hraw, dh, dproj)


_ANY = pl.BlockSpec(memory_space=pl.ANY)


_SW_SCALE = HD ** -0.5
_KVB = C_KV // (2 * SWKV * HD)


def _swa_mask(n):
    ki = lax.broadcasted_iota(jnp.int32, (2 * WIN, SWG * WIN), 0)
    qi = lax.broadcasted_iota(jnp.int32, (2 * WIN, SWG * WIN), 1) % WIN
    return (ki > qi) & (ki <= qi + WIN) & ((n > 0) | (ki >= WIN))


def _group_rows(x_ref, hk):
    return jnp.concatenate([x_ref[:, (hk * SWG + g) * HD:(hk * SWG + g + 1) * HD] for g in range(SWG)], axis=0)


def _group_lanes(x_ref, hk):
    return jnp.concatenate([x_ref[hk * SWG + g:hk * SWG + g + 1, :] for g in range(SWG)], axis=1)


def _sink_lanes(sink_ref, hk):
    return jnp.concatenate([jnp.broadcast_to(sink_ref[:, hk * SWG + g:hk * SWG + g + 1], (1, WIN))
                            for g in range(SWG)], axis=1)


def _swa_fwd(proj, sinks):
    t = proj.shape[0]
    nb = t // WIN

    def body(q_ref, kvc_ref, kvp_ref, sink_ref, y_ref, lse_ref):
        valid = _swa_mask(pl.program_id(0))
        for hk in range(SWKV):
            ks = slice(hk * HD, (hk + 1) * HD)
            vs = slice(SWKV * HD + hk * HD, SWKV * HD + (hk + 1) * HD)
            kb = jnp.concatenate([kvp_ref[:, ks], kvc_ref[:, ks]], axis=0).astype(BF16)
            vb = jnp.concatenate([kvp_ref[:, vs], kvc_ref[:, vs]], axis=0).astype(BF16)
            q4 = _group_rows(q_ref, hk).astype(BF16)
            sink = _sink_lanes(sink_ref, hk)
            logits = jnp.where(valid, _dot_nt(kb, q4) * _SW_SCALE, -jnp.inf)
            m = jnp.maximum(jnp.max(logits, axis=0, keepdims=True), sink)
            p = jnp.exp(logits - m)
            denom = jnp.sum(p, axis=0, keepdims=True) + jnp.exp(sink - m)
            y4 = _dot_tn((p / denom).astype(BF16), vb).astype(BF16)
            lse4 = m + jnp.log(denom)
            for g in range(SWG):
                hq = hk * SWG + g
                y_ref[:, hq * HD:(hq + 1) * HD] = y4[g * WIN:(g + 1) * WIN]
                lse_ref[hq:hq + 1, :] = lse4[:, g * WIN:(g + 1) * WIN]

    return pl.pallas_call(
        body, name="swa_fwd", grid=(nb,),
        in_specs=[pl.BlockSpec((WIN, D), lambda n: (n, C_QSW // D)),
                  pl.BlockSpec((WIN, 512), lambda n: (n, _KVB)),
                  pl.BlockSpec((WIN, 512), lambda n: (jnp.maximum(n - 1, 0), _KVB)),
                  pl.BlockSpec((1, SWH), lambda n: (0, 0))],
        out_specs=[pl.BlockSpec((WIN, D), lambda n: (n, 0)), pl.BlockSpec((SWH, WIN), lambda n: (0, n))],
        out_shape=[jax.ShapeDtypeStruct((t, D), BF16), jax.ShapeDtypeStruct((SWH, t), F32)],
        compiler_params=_params(),
    )(proj, proj, proj, sinks)


def _swa_bwd(proj, sinks, lse, dyb, dproj):
    t = proj.shape[0]
    nb = t // WIN

    def body(q_ref, kvc_ref, kvp_ref, sink_ref, lse_ref, dy_ref, _, dq_ref, dself_ref, dprev_ref, ds_ref):
        @pl.when(pl.program_id(0) == 0)
        def _():
            ds_ref[...] = jnp.zeros_like(ds_ref)

        valid = _swa_mask(pl.program_id(0))
        kvh = range(SWKV)
        ks = [slice(hk * HD, (hk + 1) * HD) for hk in kvh]
        vs = [slice(SWKV * HD + hk * HD, SWKV * HD + (hk + 1) * HD) for hk in kvh]
        kb = [jnp.concatenate([kvp_ref[:, ks[hk]], kvc_ref[:, ks[hk]]], axis=0).astype(BF16) for hk in kvh]
        vb = [jnp.concatenate([kvp_ref[:, vs[hk]], kvc_ref[:, vs[hk]]], axis=0).astype(BF16) for hk in kvh]
        qb = [_group_rows(q_ref, hk).astype(BF16) for hk in kvh]
        dyb_ = [_group_rows(dy_ref, hk).astype(BF16) for hk in kvh]
        lse4 = [_group_lanes(lse_ref, hk) for hk in kvh]
        logits = [_dot_nt(kb[hk], qb[hk]) for hk in kvh]
        dpt = [_dot_nt(vb[hk], dyb_[hk]) for hk in kvh]
        p = [jnp.exp(jnp.where(valid, logits[hk] * _SW_SCALE, -jnp.inf) - lse4[hk]) for hk in kvh]
        delta = [jnp.sum(p[hk] * dpt[hk], axis=0, keepdims=True) for hk in kvh]
        dsm = [(p[hk] * (dpt[hk] - delta[hk])).astype(BF16) for hk in kvh]
        dq4 = [(_dot_tn(dsm[hk], kb[hk]) * _SW_SCALE).astype(BF16) for hk in kvh]
        dkb = [_dot_nn(dsm[hk], qb[hk]) * _SW_SCALE for hk in kvh]
        dvb = [_dot_nn(p[hk].astype(BF16), dyb_[hk]) for hk in kvh]
        for hk in kvh:
            dsink4 = jnp.exp(_sink_lanes(sink_ref, hk) - lse4[hk]) * delta[hk]
            for g in range(SWG):
                hq = hk * SWG + g
                dq_ref[:, hq * HD:(hq + 1) * HD] = dq4[hk][g * WIN:(g + 1) * WIN]
                ds_ref[:, hq:hq + 1] += -jnp.sum(dsink4[:, g * WIN:(g + 1) * WIN], axis=1, keepdims=True)
            dprev_ref[:, ks[hk]] = dkb[hk][:WIN]
            dself_ref[:, ks[hk]] = dkb[hk][WIN:]
            dprev_ref[:, vs[hk]] = dvb[hk][:WIN]
            dself_ref[:, vs[hk]] = dvb[hk][WIN:]

    return pl.pallas_call(
        body, name="swa_bwd", grid=(nb,),
        in_specs=[pl.BlockSpec((WIN, D), lambda n: (n, C_QSW // D)),
                  pl.BlockSpec((WIN, 512), lambda n: (n, _KVB)),
                  pl.BlockSpec((WIN, 512), lambda n: (jnp.maximum(n - 1, 0), _KVB)),
                  pl.BlockSpec((1, SWH), lambda n: (0, 0)),
                  pl.BlockSpec((SWH, WIN), lambda n: (0, n)),
                  pl.BlockSpec((WIN, D), lambda n: (n, 0)), _ANY],
        out_specs=[pl.BlockSpec((WIN, D), lambda n: (n, C_QSW // D)), pl.BlockSpec((WIN, 512), lambda n: (n, 0)),
                   pl.BlockSpec((WIN, 512), lambda n: (jnp.maximum(n - 1, 0), 0)),
                   pl.BlockSpec((1, SWH), lambda n: (0, 0))],
        out_shape=[jax.ShapeDtypeStruct((t, NP), BF16), jax.ShapeDtypeStruct((t, 512), F32),
                   jax.ShapeDtypeStruct((t, 512), F32), jax.ShapeDtypeStruct((1, SWH), F32)],
        input_output_aliases={6: 0}, compiler_params=_params(),
    )(proj, proj, proj, sinks, lse, dyb, dproj)


def _kv_combine(dself, dnext, dif, dproj):
    t = dself.shape[0]
    rows = _pick(t, 512)

    def body(a_ref, b_ref, dif_ref, _, o_ref):
        row = pl.program_id(0) * rows + lax.broadcasted_iota(jnp.int32, (rows, 1), 0)
        o_ref[:, 0:512] = (a_ref[...] + jnp.where(row < t - WIN, b_ref[...], 0.0)).astype(BF16)
        lane = lax.broadcasted_iota(jnp.int32, (rows, 128), 1)
        dif_v = dif_ref[...]
        first = jnp.zeros((rows, 128), F32)
        for col in range(8):
            first = first + jnp.where(lane == col, dif_v[:, col:col + 1], 0.0)
        o_ref[:, 512:640] = first.astype(BF16)
        o_ref[:, 640:512 + IFW] = jnp.zeros((rows, IFW - 128), BF16)

    return pl.pallas_call(
        body, name="kv_combine", grid=(t // rows,),
        in_specs=[pl.BlockSpec((rows, 512), lambda n: (n, 0)), pl.BlockSpec((rows, 512), lambda n: (n, 0)),
                  pl.BlockSpec((rows, 8), lambda n: (n, 0)), _ANY],
        out_specs=pl.BlockSpec((rows, 512 + IFW), lambda n: (n, C_KV // (512 + IFW))),
        out_shape=jax.ShapeDtypeStruct((t, NP), BF16), input_output_aliases={3: 0}, compiler_params=_params(),
    )(dself, dnext, dif, dproj)


def _sds(t, n, dtype):
    return jax.ShapeDtypeStruct((t, n), dtype)


def _proj_in(x, gain, w_in):
    t = x.shape[0]
    tm, tn = _pick(t, 1024), 2 * IFW

    def body(x_ref, g_ref, w_ref, h_ref, p_ref, gate_ref, h_scr):
        j = pl.program_id(1)

        @pl.when(j == 0)
        def _():
            xn, _ = _rms(x_ref[...])
            h = (xn * g_ref[...]).astype(BF16)
            h_scr[...] = h
            h_ref[...] = h

        acc = _dot_nn(h_scr[...], w_ref[...])
        p_ref[...] = acc.astype(BF16)

        @pl.when(j == C_IF // tn)
        def _():
            gate_ref[...] = acc[:, C_IF % tn:C_IF % tn + 128]

    return pl.pallas_call(
        body, name="mm_in", grid=(t // tm, NP // tn),
        in_specs=[pl.BlockSpec((tm, D), lambda i, j: (i, 0)), pl.BlockSpec((1, D), lambda i, j: (0, 0)),
                  pl.BlockSpec((D, tn), lambda i, j: (0, j))],
        out_specs=[pl.BlockSpec((tm, D), lambda i, j: (i, 0)), pl.BlockSpec((tm, tn), lambda i, j: (i, j)),
                   pl.BlockSpec((tm, 128), lambda i, j: (i, 0))],
        out_shape=[_sds(t, D, BF16), _sds(t, NP, BF16), _sds(t, 128, F32)],
        scratch_shapes=[pltpu.VMEM((tm, D), BF16)],
        compiler_params=_params(dimension_semantics=("arbitrary", "arbitrary")),
    )(x, gain, w_in)


def _branch_merge(ya, yb, wa, wb, proj):
    t = ya.shape[0]

    def epilogue(accs, ins, outs, i, j):
        za, zb = accs
        merged = _sigmoid(ins[0][...].astype(F32)) * za + _sigmoid(ins[1][...].astype(F32)) * zb
        outs[0][...] = merged.astype(BF16)
        outs[1][...] = za.astype(BF16)
        outs[2][...] = zb.astype(BF16)

    return _mm_ep([(ya, wa), (yb, wb)], "nn", "mm_branch_merge", epilogue, [(proj, _tile(C_GA)), (proj, _tile(C_GB))],
                  [(_sds(t, D, BF16), _tile())] * 3, 1024, 1024, row_split=4)


def _dmerged_bwd(dxb, w_out, proj, za, zb):
    t = dxb.shape[0]

    def epilogue(accs, ins, outs, i, j):
        dm = accs[0]
        sa, sb = _sigmoid(ins[0][...].astype(F32)), _sigmoid(ins[1][...].astype(F32))
        outs[0][...] = (dm * sa).astype(BF16)
        outs[1][...] = (dm * sb).astype(BF16)
        outs[2][:, 0:D] = (dm * ins[2][...].astype(F32) * sa * (1.0 - sa)).astype(BF16)
        outs[2][:, D:2 * D] = (dm * ins[3][...].astype(F32) * sb * (1.0 - sb)).astype(BF16)

    gate_cols = lambda tm, tn: pl.BlockSpec((tm, 2 * D), lambda i, j, kk: (i, C_GA // (2 * D)))
    return _mm_ep([(dxb, w_out)], "nt", "mm_dmerged_bwd", epilogue,
                  [(proj, _tile(C_GA)), (proj, _tile(C_GB)), (za, _tile()), (zb, _tile())],
                  [(_sds(t, D, BF16), _tile()), (_sds(t, D, BF16), _tile()), (_sds(t, NP, BF16), gate_cols)], 1024, D,
                  row_split=4)


def _dya_bwd(dza, wa, hraw, proj, g, dproj):
    t = dza.shape[0]

    def epilogue(accs, ins, outs, i, j):
        h_ref, o_ref, g_ref, _ = ins
        dh_ref, do_ref, dg_ref = outs

        @pl.when(i == 0)
        def _():
            dg_ref[...] = jnp.zeros_like(dg_ref)

        dy = accs[0]
        so = _sigmoid(o_ref[...].astype(F32))
        for h in range(MLH):
            sl = slice(h * DV, (h + 1) * DV)
            xn, rstd = _rms(h_ref[:, sl])
            gs = g_ref[:, sl]
            do_ref[:, sl] = (dy[:, sl] * xn * gs * so[:, sl] * (1.0 - so[:, sl])).astype(BF16)
            dhn = dy[:, sl] * so[:, sl]
            dg_ref[:, sl] += jnp.sum(dhn * xn, axis=0, keepdims=True)
            dh_ref[:, sl] = _rms_bwd(xn, rstd, dhn * gs)

    return _mm_ep([(dza, wa)], "nt", "mm_dya_bwd", epilogue,
                  [(hraw, _tile()), (proj, _tile(C_O)), (g, _row()), (dproj, lambda tm, tn: _ANY)],
                  [(_sds(t, D, F32), _tile()), (_sds(t, NP, BF16), _tile(C_O)), (_sds(1, D, F32), _row())],
                  512, D, aliases={3: 1})


def _up_act(hn, w_up):
    t = hn.shape[0]

    def epilogue(accs, ins, outs, i, j):
        r = jnp.maximum(accs[0], 0.0)
        outs[0][...] = (r * r).astype(BF16)
        outs[1][...] = accs[0].astype(BF16)

    return _mm_ep([(hn, w_up)], "nn", "mm_up_act", epilogue, [],
                  [(_sds(t, DFF, BF16), _tile()), (_sds(t, DFF, BF16), _tile())], 1024, 1024, row_split=4)


def _da_du(dxb, w_down, u):
    t = dxb.shape[0]

    def epilogue(accs, ins, outs, i, j):
        outs[0][...] = (accs[0] * 2.0 * jnp.maximum(ins[0][...].astype(F32), 0.0)).astype(BF16)

    return _mm_ep([(dxb, w_down)], "nt", "mm_da_du", epilogue, [(u, _tile())], [(_sds(t, DFF, BF16), _tile())],
                  1024, 1024, row_split=4)[0]


def _resid_norm_mm(a, w, x, g, name):
    t = x.shape[0]

    def epilogue(accs, ins, outs, i, j):
        x1 = ins[0][...] + accs[0]
        outs[0][...] = x1
        xn, _ = _rms(x1)
        outs[1][...] = (xn * ins[1][...]).astype(BF16)

    return _mm_ep([(a, w)], "nn", name, epilogue, [(x, _tile()), (g, _row())],
                  [(_sds(t, D, F32), _tile()), (_sds(t, D, BF16), _tile())], 1024, D, row_split=4)


def _norm_bwd_mm(dy, w, x, g, dres, name):
    t = x.shape[0]

    def init(outs):
        outs[2][...] = jnp.zeros_like(outs[2])

    def epilogue(accs, ins, outs, i, j):
        dh = accs[0]
        xn, rstd = _rms(ins[0][...])
        outs[2][...] += jnp.sum(dh * xn, axis=0, keepdims=True)
        dx = ins[2][...] + _rms_bwd(xn, rstd, dh * ins[1][...])
        outs[0][...] = dx
        outs[1][...] = dx.astype(BF16)

    return _mm_ep([(dy, w)], "nt", name, epilogue, [(x, _tile()), (g, _row()), (dres, _tile())],
                  [(_sds(t, D, F32), _tile()), (_sds(t, D, BF16), _tile()), (_sds(1, D, F32), _row())], 1024, D,
                  row_split=4, init=init)


def _ple_final_mm(hn2, w_gate, x2, pp, target, gf):
    t = x2.shape[0]

    def epilogue(accs, ins, outs, i, j):
        loss_ref, dg_ref, dx_ref, dpp_ref, dgp_ref = outs

        @pl.when(i == 0)
        def _():
            loss_ref[...] = jnp.zeros_like(loss_ref)
            dg_ref[...] = jnp.zeros_like(dg_ref)

        gate = _sigmoid(accs[0])
        pp_v = ins[1][...]
        x3 = ins[0][...] + gate * pp_v
        xn, rstd = _rms(x3)
        gf_v = ins[3][...]
        err = xn * gf_v - ins[2][...]
        loss_ref[...] += (0.5 / D) * jnp.sum(jnp.sum(err * err, axis=1, keepdims=True), axis=0, keepdims=True)
        dy = err * (1.0 / D)
        dg_ref[...] += jnp.sum(dy * xn, axis=0, keepdims=True)
        dx3 = _rms_bwd(xn, rstd, dy * gf_v)
        dx_ref[...] = dx3
        dpp_ref[...] = (dx3 * gate).astype(BF16)
        dgp_ref[...] = (dx3 * pp_v * gate * (1.0 - gate)).astype(BF16)

    one = lambda tm, tn: pl.BlockSpec((1, 1), lambda i, j, kk: (0, 0))
    return _mm_ep([(hn2, w_gate)], "nn", "mm_ple_final", epilogue,
                  [(x2, _tile()), (pp, _tile()), (target, _tile()), (gf, _row())],
                  [(_sds(1, 1, F32), one), (_sds(1, D, F32), _row()), (_sds(t, D, F32), _tile()),
                   (_sds(t, D, BF16), _tile()), (_sds(t, D, BF16), _tile())], 512, D)


_WIN_SEGMENTS = ((0, 3072, C_QK), (3072, 8, C_IF), (3080, 1024, C_QSW), (4104, 256, C_KV), (4360, 256, C_KV + 256),
                 (4616, 1024, C_GA), (5640, 1024, C_GB))
_WIN_SHARD = N_IN // 4


def _win_pieces():
    out = []
    for src, width, dst in _WIN_SEGMENTS:
        while width:
            chip, col = divmod(src, _WIN_SHARD)
            n = min(width, _WIN_SHARD - col)
            out.append((chip, col, n, dst))
            src, dst, width = src + n, dst + n, width - n
    return out


def _win_pad(shards):
    rows = shards.shape[1]
    tr = _pick(rows, 256)

    def body(s_ref, o_ref):
        for chip, col, n, dst in _win_pieces():
            o_ref[:, dst:dst + n] = s_ref[chip, :, col:col + n]
        o_ref[:, C_IF + 8:NP] = jnp.zeros((tr, NP - C_IF - 8), shards.dtype)

    return pl.pallas_call(
        body, name="win_pad", grid=(rows // tr,), in_specs=[pl.BlockSpec((4, tr, _WIN_SHARD), lambda i: (0, i, 0))],
        out_specs=pl.BlockSpec((tr, NP), lambda i: (i, 0)), out_shape=jax.ShapeDtypeStruct((rows, NP), shards.dtype),
        compiler_params=_params(),
    )(shards)


def _win_unpad(wp):
    rows = wp.shape[0]
    tr = _pick(rows, 256)

    def body(p_ref, o_ref):
        for chip, col, n, dst in _win_pieces():
            o_ref[chip, :, col:col + n] = p_ref[:, dst:dst + n]

    return pl.pallas_call(
        body, name="win_unpad", grid=(rows // tr,), in_specs=[pl.BlockSpec((tr, NP), lambda i: (i, 0))],
        out_specs=pl.BlockSpec((4, tr, _WIN_SHARD), lambda i: (0, i, 0)),
        out_shape=jax.ShapeDtypeStruct((4, rows, _WIN_SHARD), wp.dtype), compiler_params=_params(),
    )(wp)


def _local_step(x, p, target, w, late_weights=None, early_grads=None, mid_grads=None, last_grad=None):
    t = x.shape[0]
    pb = p.astype(BF16)
    w = dict(w)

    h0, proj, gates = _proj_in(x, w["norm_mix_g"], w["w_in"])
    qk = _conv_silu_fwd(proj, w["conv_qk"])
    grow, sneg_row = _gates_fwd(gates[:, 0:8].T, w["b_if"].reshape(8, 1))
    gcol, sneg_col = grow.T, sneg_row.T
    hraw, ya, cs, st = _mlstm_fwd(qk, proj, grow, gcol, w["mlstm_norm_g"])
    yb, lse = _swa_fwd(proj, w["sinks"])
    if late_weights is not None:
        w.update(late_weights(yb))
    merged, za, zb = _branch_merge(ya, yb, w["w_branch_a"], w["w_branch_b"], proj)
    x1, hn1 = _resid_norm_mm(merged, w["w_out"], x, w["norm_mlp_g"], "mm_out_norm")
    act, u = _up_act(hn1, w["w_up"])
    x2, hn2 = _resid_norm_mm(act, w["w_down"], x1, w["norm_ple_g"], "mm_down_norm")
    pp = _mm(pb, w["w_ple_proj"], "nn", F32, "mm_ple_proj")
    loss, d_final_g, dx3, dpp, dgpre = _ple_final_mm(hn2, w["w_ple_gate"], x2, pp, target, w["final_norm_g"])

    g = {"final_norm_g": d_final_g}
    g["w_ple_proj"] = _mm(pb, dpp, "tn", F32, "mm_d_ple_proj", out_chunks=4)
    g["w_ple_gate"] = _mm(hn2, dgpre, "tn", F32, "mm_d_ple_gate")
    dx2, dx2b, g["norm_ple_g"] = _norm_bwd_mm(dgpre, w["w_ple_gate"], x2, w["norm_ple_g"], dx3, "mm_dhn2_norm")
    g["w_down"] = _mm(act, dx2b, "tn", F32, "mm_d_down")
    du = _da_du(dx2b, w["w_down"], u)
    g["w_up"] = _mm(hn1, du, "tn", F32, "mm_d_up", out_chunks=4)
    dx1, dx1b, g["norm_mlp_g"] = _norm_bwd_mm(du, w["w_up"], x1, w["norm_mlp_g"], dx2, "mm_dhn1_norm")
    g["w_out"] = _mm(merged, dx1b, "tn", F32, "mm_d_out")
    dza, dzb, dproj = _dmerged_bwd(dx1b, w["w_out"], proj, za, zb)
    g["w_branch_a"] = _mm(ya, dza, "tn", F32, "mm_d_branch_a")
    g["w_branch_b"] = _mm(yb, dzb, "tn", F32, "mm_d_branch_b")
    gain = w["mlstm_norm_g"] if early_grads is None else w["mlstm_norm_g"] + early_grads(g)
    dyb = _mm(dzb, w["w_branch_b"], "nt", F32, "mm_dyb")
    dhraw, dproj, g["mlstm_norm_g"] = _dya_bwd(dza, w["w_branch_a"], hraw, proj, gain, dproj)
    if mid_grads is not None:
        sneg_col = sneg_col + mid_grads(dhraw)
    dqk, dproj, dif, g["b_if"] = _mlstm_bwd(qk, proj, grow, gcol, sneg_col, cs, st, hraw, dhraw, dproj)
    dc, g["conv_qk"] = _conv_silu_bwd_a(proj, w["conv_qk"], dqk)
    dproj = _conv_silu_bwd_b(dc, w["conv_qk"], dproj)
    dproj, dkv_self, dkv_prev, g["sinks"] = _swa_bwd(proj, w["sinks"], lse, dyb, dproj)
    dproj = _kv_combine(dkv_self, dkv_prev, dif, dproj)
    g["w_in"] = _mm(h0, dproj, "tn", F32, "mm_d_in")
    gain = w["norm_mix_g"] if last_grad is None else w["norm_mix_g"] + last_grad(g)
    grad_x, _, g["norm_mix_g"] = _norm_bwd_mm(dproj, w["w_in"], x, gain, dx1, "mm_dh0_norm")
    return loss, grad_x, g


_W4 = ("w_branch_a", "w_branch_b", "w_out", "w_ple_gate")
_SHARDED_NAMES = ("w_in", "w_up", "w_down", "w_ple_proj", "conv_qk") + _W4
_SMALL_ROWS = 16
_CONV_ROW = 8


def _group(s):
    return [s["w_in"], jnp.concatenate([s[n] for n in _W4], axis=0), s["w_up"], s["w_down"], s["w_ple_proj"]]


def _ungroup(arrs):
    out = {"w_in": arrs[0], "w_up": arrs[2], "w_down": arrs[3], "w_ple_proj": arrs[4]}
    rows = arrs[1].shape[0] // len(_W4)
    for i, n in enumerate(_W4):
        out[n] = arrs[1][i * rows:(i + 1) * rows]
    return out


def _rows_tile(rows):
    return 256 if rows % 256 == 0 else rows


_SMALL = ("norm_mix_g", "mlstm_norm_g", "norm_mlp_g", "norm_ple_g", "final_norm_g")


def _pack_small(vals, extra=None, conv=None):
    rows = [vals[n].reshape(1, D) for n in _SMALL]
    tail = [vals["b_if"].reshape(1, 8), vals["sinks"].reshape(1, SWH)]
    used = 8 + SWH
    if extra is not None:
        tail.append(extra.reshape(1, 1))
        used += 1
    tail.append(jnp.zeros((1, D - used), F32))
    rows.append(jnp.concatenate(tail, axis=1))
    rows.append(jnp.zeros((_CONV_ROW - len(rows), D), F32))
    rows.append(jnp.zeros((CONV, D), F32) if conv is None else conv)
    rows.append(jnp.zeros((_SMALL_ROWS - _CONV_ROW - CONV, D), F32))
    return jnp.concatenate(rows, axis=0)


def _unpack_small(slab, shapes):
    out = {n: slab[i].reshape(shapes[n]) for i, n in enumerate(_SMALL)}
    out["b_if"] = slab[5, 0:8].reshape(shapes["b_if"])
    out["sinks"] = slab[5, 8:8 + SWH].reshape(shapes["sinks"])
    return out


_MESH = pl.DeviceIdType.MESH
_HBM = pl.BlockSpec(memory_space=pltpu.HBM)
_VMEM = pl.BlockSpec(memory_space=pltpu.VMEM)


def _place():
    x, y, c = lax.axis_index("x"), lax.axis_index("y"), lax.axis_index("c")
    return x, y, c, 2 * x + y


def _chip_peer(x, y, r):
    return (x ^ (r >> 1), y ^ (r & 1))


def _half(ref, which):
    h = ref.shape[-2] // 2
    return pl.ds(which * h, h)


def _allgather_weights(shards, conv):
    n = len(shards)

    def body(*refs):
        ins, conv_ref = refs[:n], refs[n]
        outs, conv_out = refs[n + 1:2 * n + 1], refs[2 * n + 1]
        send_a, recv_a, send_b, recv_b, send_c, recv_c, local_sems = refs[2 * n + 2:]
        x, y, c, j = _place()
        sibling = (x, y, 1 - c)
        local = [pltpu.make_async_copy(ins[k], outs[k].at[j], local_sems.at[k]) for k in range(n)]
        local.append(pltpu.make_async_copy(conv_ref, conv_out.at[j], local_sems.at[n]))
        for cp in local:
            cp.start()

        def copy_a(k, r, chip):
            rows = _half(ins[k], c)
            return pltpu.make_async_remote_copy(
                src_ref=ins[k].at[rows], dst_ref=outs[k].at[chip, rows], send_sem=send_a.at[3 * k + r - 1],
                recv_sem=recv_a.at[3 * k + r - 1], device_id=(*_chip_peer(x, y, r), c), device_id_type=_MESH)

        def copy_b(k, r, chip, which):
            rows = _half(ins[k], which)
            return pltpu.make_async_remote_copy(
                src_ref=outs[k].at[chip, rows], dst_ref=outs[k].at[chip, rows], send_sem=send_b.at[3 * k + r - 1],
                recv_sem=recv_b.at[3 * k + r - 1], device_id=sibling, device_id_type=_MESH)

        def copy_c(r, chip):
            return pltpu.make_async_remote_copy(
                src_ref=conv_ref, dst_ref=conv_out.at[chip], send_sem=send_c.at[r - 1],
                recv_sem=recv_c.at[r - 1], device_id=(*_chip_peer(x, y, r), c), device_id_type=_MESH)

        for k in range(n):
            for r in (1, 2, 3):
                copy_a(k, r, j).start()
        for r in (1, 2, 3):
            copy_c(r, j).start()
        for k in range(n):
            for r in (1, 2, 3):
                copy_a(k, r, j ^ r).wait_recv()
                copy_b(k, r, j ^ r, c).start()
        for k in range(n):
            for r in (1, 2, 3):
                copy_b(k, r, j ^ r, 1 - c).wait_recv()
        for r in (1, 2, 3):
            copy_c(r, j ^ r).wait_recv()
        for k in range(n):
            for r in (1, 2, 3):
                copy_a(k, r, j).wait_send()
                copy_b(k, r, j ^ r, c).wait_send()
        for r in (1, 2, 3):
            copy_c(r, j).wait_send()
        for cp in local:
            cp.wait()

    return pl.pallas_call(
        body, name="allgather_weights",
        out_shape=[jax.ShapeDtypeStruct((4,) + s.shape, s.dtype) for s in shards]
        + [jax.ShapeDtypeStruct((4,) + conv.shape, F32)],
        in_specs=[_HBM] * (n + 1), out_specs=[_HBM] * (n + 1),
        scratch_shapes=[pltpu.SemaphoreType.DMA((3 * n,))] * 4 + [pltpu.SemaphoreType.DMA((3,))] * 2
        + [pltpu.SemaphoreType.DMA((n + 1,))],
    )(*shards, conv)


_SEM = pl.BlockSpec(memory_space=pltpu.SEMAPHORE)
_DATAFLOW = pltpu.SideEffectType.DATAFLOW_SIDE_EFFECTING


def _late_peer_copy(src_ref, land_ref, send_sems, recv_sems, x, y, c, j, r, chip):
    return pltpu.make_async_remote_copy(
        src_ref=src_ref, dst_ref=land_ref.at[chip], send_sem=send_sems.at[r - 1], recv_sem=recv_sems.at[r - 1],
        device_id=(*_chip_peer(x, y, r), c), device_id_type=_MESH)


def _late_gather_start(rest):
    def body(rest_ref, land_ref, send_sems, recv_sems, rest_thru, land_thru, token):
        x, y, c, j = _place()
        for r in (1, 2, 3):
            _late_peer_copy(rest_ref, land_ref, send_sems, recv_sems, x, y, c, j, r, j).start()
        token[...] = jnp.zeros_like(token)

    j = 2 * lax.axis_index("x") + lax.axis_index("y")
    land = lax.dynamic_update_slice(lax.empty((4,) + rest.shape, rest.dtype), rest[None], (j, 0, 0))
    return pl.pallas_call(
        body, name="late_gather_start",
        out_shape=(pltpu.SemaphoreType.DMA((3,)), pltpu.SemaphoreType.DMA((3,)), pltpu.HBM(rest.shape, rest.dtype),
                   pltpu.HBM(land.shape, land.dtype), jax.ShapeDtypeStruct((8, 128), F32)),
        in_specs=(_HBM, _HBM), out_specs=(_SEM, _SEM, _HBM, _HBM, _VMEM), input_output_aliases={0: 2, 1: 3},
        compiler_params=pltpu.CompilerParams(has_side_effects=_DATAFLOW),
    )(pltpu.with_memory_space_constraint(rest, pltpu.HBM), pltpu.with_memory_space_constraint(land, pltpu.HBM))


def _late_gather_wait(send_sems, recv_sems, rest_thru, land_thru, after):
    def body(rest_ref, land_ref, send_sems, recv_sems, after_ref, rest_dead, got_ref):
        x, y, c, j = _place()
        for r in (1, 2, 3):
            cp = _late_peer_copy(rest_ref, land_ref, send_sems, recv_sems, x, y, c, j, r, j ^ r)
            cp.wait_send()
            cp.wait_recv()

    return pl.pallas_call(
        body, name="late_gather_wait",
        out_shape=(pltpu.HBM(rest_thru.shape, rest_thru.dtype), pltpu.HBM(land_thru.shape, land_thru.dtype)),
        in_specs=(_HBM, _HBM, _SEM, _SEM, _ANY), out_specs=(_HBM, _HBM), input_output_aliases={0: 0, 1: 1},
        compiler_params=pltpu.CompilerParams(has_side_effects=_DATAFLOW),
    )(rest_thru, land_thru, send_sems, recv_sems, after)[1]


def _pair_sum(g, theirs, j, c, name):
    _, h, cols = theirs.shape
    tr = _rows_tile(h)
    nb = h // tr

    def body(idx_ref, a_ref, b_ref, own_ref, ob_ref):
        s = a_ref[0] + b_ref[0]
        ob_ref[0] = s.astype(BF16)

        @pl.when(pl.program_id(1) == idx_ref[0])
        def _():
            own_ref[...] = s

    blk = pl.BlockSpec((1, tr, cols), lambda i, k, idx_ref: (k, i, 0))
    return pl.pallas_call(
        body, name=name,
        grid_spec=pltpu.PrefetchScalarGridSpec(
            num_scalar_prefetch=1, grid=(nb, 4),
            in_specs=[pl.BlockSpec((1, tr, cols), lambda i, k, idx_ref: (k, idx_ref[1] * nb + i, 0)), blk],
            out_specs=[pl.BlockSpec((tr, cols), lambda i, k, idx_ref: (i, 0)), blk]),
        out_shape=[jax.ShapeDtypeStruct((h, cols), F32), jax.ShapeDtypeStruct(theirs.shape, BF16)],
        compiler_params=_params(),
    )(jnp.stack([j, c]).astype(jnp.int32), g, theirs)


def _chip_copies(srcs, lands, send_sems, recv_sems):
    x, y, c, j = _place()
    return [pltpu.make_async_remote_copy(
        src_ref=srcs[k].at[j ^ r], dst_ref=lands[k].at[r - 1], send_sem=send_sems.at[3 * k + r - 1],
        recv_sem=recv_sems.at[3 * k + r - 1], device_id=(*_chip_peer(x, y, r), c), device_id_type=_MESH)
        for k in range(len(srcs)) for r in (1, 2, 3)]


def _pair_copies(srcs, lands, send_sems, recv_sems):
    x, y, c, _ = _place()
    return [pltpu.make_async_remote_copy(
        src_ref=srcs[k].at[:, _half(srcs[k], 1 - c)], dst_ref=lands[k], send_sem=send_sems.at[k],
        recv_sem=recv_sems.at[k], device_id=(x, y, 1 - c), device_id_type=_MESH) for k in range(len(srcs))]


def _split_start(name, srcs, lands, copies, n_sems):
    n = len(srcs)

    def body(*refs):
        for cp in copies(refs[:n], refs[n:2 * n], refs[2 * n], refs[2 * n + 1]):
            cp.start()
        refs[-1][...] = jnp.zeros_like(refs[-1])

    arrays = list(srcs) + list(lands)
    out = pl.pallas_call(
        body, name=name,
        out_shape=(pltpu.SemaphoreType.DMA((n_sems,)), pltpu.SemaphoreType.DMA((n_sems,)),
                   *[pltpu.HBM(a.shape, a.dtype) for a in arrays], jax.ShapeDtypeStruct((8, 128), F32)),
        in_specs=[_HBM] * (2 * n), out_specs=(_SEM, _SEM, *([_HBM] * (2 * n)), _VMEM),
        input_output_aliases={k: 2 + k for k in range(2 * n)},
        compiler_params=pltpu.CompilerParams(has_side_effects=_DATAFLOW),
    )(*[pltpu.with_memory_space_constraint(a, pltpu.HBM) for a in arrays])
    return out[0], out[1], list(out[2:2 + n]), list(out[2 + n:2 + 2 * n]), out[-1]


def _split_wait(name, send_sems, recv_sems, srcs_thru, lands_thru, after, copies):
    n = len(srcs_thru)

    def body(*refs):
        for cp in copies(refs[:n], refs[n:2 * n], refs[2 * n], refs[2 * n + 1]):
            cp.wait_send()
            cp.wait_recv()

    arrays = list(srcs_thru) + list(lands_thru)
    out = pl.pallas_call(
        body, name=name, out_shape=tuple(pltpu.HBM(a.shape, a.dtype) for a in arrays),
        in_specs=[_HBM] * (2 * n) + [_SEM, _SEM, _ANY], out_specs=tuple([_HBM] * (2 * n)),
        input_output_aliases={k: k for k in range(2 * n)},
        compiler_params=pltpu.CompilerParams(has_side_effects=_DATAFLOW),
    )(*arrays, send_sems, recv_sems, after)
    return list(out[:n]), list(out[n:])


def _chip_exchange_start(ss, tag):
    lands = [lax.empty((3,) + s.shape[1:], s.dtype) for s in ss]
    return _split_start("chip_exchange_start_" + tag, ss, lands, _chip_copies, 3 * len(ss))


def _chip_exchange_wait(send_sems, recv_sems, ss_thru, lands_thru, after, tag):
    return _split_wait("chip_exchange_wait_" + tag, send_sems, recv_sems, ss_thru, lands_thru, after, _chip_copies)[1]


def _pair_exchange_start(gs, tag):
    lands = [lax.empty((4, g.shape[1] // 2, g.shape[2]), g.dtype) for g in gs]
    return _split_start("pair_exchange_start_" + tag, gs, lands, _pair_copies, len(gs))


def _pair_exchange_wait(send_sems, recv_sems, gs_thru, lands_thru, after, tag):
    return _split_wait("pair_exchange_wait_" + tag, send_sems, recv_sems, gs_thru, lands_thru, after, _pair_copies)


def _reduce4(own, others, c, name):
    h, cols = own.shape
    tr = _rows_tile(h)
    nb = h // tr

    def body(c_ref, s_ref, a0, a1, a2, o_ref):
        o_ref[...] = ((s_ref[...] + a0[0].astype(F32)) + a1[0].astype(F32)) + a2[0].astype(F32)

    def other(r):
        return pl.BlockSpec((1, tr, cols), lambda i, c_ref: (r, i, 0))

    return pl.pallas_call(
        body, name=name,
        grid_spec=pltpu.PrefetchScalarGridSpec(
            num_scalar_prefetch=1, grid=(nb,),
            in_specs=[pl.BlockSpec((tr, cols), lambda i, c_ref: (i, 0)), other(0), other(1), other(2)],
            out_specs=pl.BlockSpec((tr, cols), lambda i, c_ref: (c_ref[0] * nb + i, 0))),
        out_shape=jax.ShapeDtypeStruct((2 * h, cols), F32), compiler_params=_params(),
    )(c.reshape(1).astype(jnp.int32), own, others, others, others)


def _sibling_share(fulls, name):
    n = len(fulls)

    def body(*refs):
        outs, send_sems, recv_sems = refs[n:2 * n], refs[2 * n], refs[2 * n + 1]
        x, y, c, _ = _place()
        cps = [pltpu.make_async_remote_copy(
            src_ref=outs[k].at[_half(outs[k], c)], dst_ref=outs[k].at[_half(outs[k], c)], send_sem=send_sems.at[k],
            recv_sem=recv_sems.at[k], device_id=(x, y, 1 - c), device_id_type=_MESH) for k in range(n)]
        for cp in cps:
            cp.start()
        for cp in cps:
            cp.wait()

    return pl.pallas_call(
        body, name=name, out_shape=[jax.ShapeDtypeStruct(f.shape, F32) for f in fulls],
        in_specs=[_HBM] * n, out_specs=[_HBM] * n, input_output_aliases={k: k for k in range(n)},
        scratch_shapes=[pltpu.SemaphoreType.DMA((n,))] * 2,
    )(*fulls)


def _adamw(w, g, m, v):
    m1 = ADAM_B1 * m + (1.0 - ADAM_B1) * g
    v1 = ADAM_B2 * v + (1.0 - ADAM_B2) * (g * g)
    m_hat = m1 / (1.0 - ADAM_B1 ** ADAM_STEP)
    v_hat = v1 / (1.0 - ADAM_B2 ** ADAM_STEP)
    delta = -ADAM_LR * (m_hat / (jnp.sqrt(v_hat) + ADAM_EPS) + ADAM_WD * w)
    return delta, m1, v1


def _adamw_call(w, g, m, v, name):
    rows, cols = w.shape

    def body(w_ref, g_ref, m_ref, v_ref, d_out, m_out, v_out):
        delta, m1, v1 = _adamw(w_ref[...], g_ref[...], m_ref[...], v_ref[...])
        d_out[...] = delta
        m_out[...] = m1
        v_out[...] = v1

    if rows % 8 == 0:
        tr = _rows_tile(rows)
        blk, grid = pl.BlockSpec((tr, cols), lambda i: (i, 0)), (rows // tr,)
    else:
        blk, grid = pl.BlockSpec((rows, 128), lambda i: (0, i)), (cols // 128,)
    return pl.pallas_call(
        body, name=name, grid=grid, in_specs=[blk] * 4, out_specs=[blk] * 3,
        out_shape=[jax.ShapeDtypeStruct((rows, cols), F32)] * 3, compiler_params=_params(),
    )(w, g, m, v)


def _small_allreduce(vals):
    def body(v_ref, out_ref, buf, send_sems, recv_sems):
        x, y, c, j = _place()
        me = 2 * j + c
        buf[0] = v_ref[...]

        def copy(r):
            return pltpu.make_async_remote_copy(
                src_ref=v_ref, dst_ref=buf.at[r], send_sem=send_sems.at[r - 1], recv_sem=recv_sems.at[r - 1],
                device_id=(x ^ (r >> 2), y ^ ((r >> 1) & 1), c ^ (r & 1)), device_id_type=_MESH)

        for r in range(1, 8):
            copy(r).start()
        for r in range(1, 8):
            copy(r).wait()
        acc = buf[me ^ 0]
        for d in range(1, 8):
            acc = acc + buf[me ^ d]
        out_ref[...] = acc

    return pl.pallas_call(
        body, name="small_allreduce", out_shape=jax.ShapeDtypeStruct((_SMALL_ROWS, D), F32),
        in_specs=[_VMEM], out_specs=_VMEM,
        scratch_shapes=[pltpu.VMEM((8, _SMALL_ROWS, D), F32), pltpu.SemaphoreType.DMA((7,)),
                        pltpu.SemaphoreType.DMA((7,))],
    )(vals)


_NAMES = ("norm_mix_g", "w_in", "conv_qk", "b_if", "mlstm_norm_g", "sinks", "w_branch_a", "w_branch_b", "w_out",
          "norm_mlp_g", "w_up", "w_down", "norm_ple_g", "w_ple_gate", "w_ple_proj", "final_norm_g")
_GROUP_NAMES = ("w_in", "w4", "w_up", "w_down", "w_ple_proj")


def _step(x, p, target, w, m, v):
    c = lax.axis_index("c")
    j = 2 * lax.axis_index("x") + lax.axis_index("y")

    def shards(d):
        return {n: d[n][0] for n in _SHARDED_NAMES}

    ws = shards(w)
    w_in_all, conv_all = _allgather_weights([ws["w_in"].astype(BF16)], ws["conv_qk"])
    rows_pp = PLE * (D // 4) // D
    rest = jnp.concatenate([ws[n] for n in _W4] + [ws["w_up"], ws["w_down"], ws["w_ple_proj"].reshape(rows_pp, D)],
                           axis=0)
    rest = (rest + 0.0 * conv_all[0, 0, 0]).astype(BF16)
    send_sems, recv_sems, rest_thru, land_thru, token = _late_gather_start(rest)
    full = {n: w[n] for n in ("mlstm_norm_g", "norm_mlp_g", "norm_ple_g", "b_if", "sinks")}
    full["norm_mix_g"] = w["norm_mix_g"] + token[0, 0]
    full["final_norm_g"] = w["final_norm_g"].reshape(1, D)
    full["w_in"] = _win_pad(w_in_all)
    full["conv_qk"] = jnp.swapaxes(conv_all, 0, 1).reshape(CONV, D)

    def late_weights(after):
        land = _late_gather_wait(send_sems, recv_sems, rest_thru, land_thru, after)
        out = {n: land[:, i * (D // 4):(i + 1) * (D // 4)].reshape(D, D) for i, n in enumerate(_W4)}
        out["w_up"] = land[:, D:2 * D]
        out["w_down"] = land[:, 2 * D:3 * D].reshape(DFF, D)
        out["w_ple_proj"] = jnp.swapaxes(land[:, 3 * D:3 * D + rows_pp].reshape(4, PLE, D // 4), 0, 1).reshape(PLE, D)
        return out

    early, last = {}, {}

    def pair_sums(by_dest, theirs, names):
        return [_pair_sum(a, b, j, c, "pair_sum_" + n) for a, b, n in zip(by_dest, theirs, names)]

    def early_grads(g):
        by_dest = [jnp.stack([g[n].reshape(4, D // 4, D) for n in _W4], axis=1).reshape(4, D, D),
                   g["w_up"], g["w_down"].reshape(4, DFF // 4, D), g["w_ple_proj"]]
        *early["pair"], token = _pair_exchange_start(by_dest, "early")
        return token[0, 0]

    def mid_grads(after):
        early["sums"] = pair_sums(*_pair_exchange_wait(*early["pair"], after, "early"), _GROUP_NAMES[1:])
        *early["flight"], token = _chip_exchange_start([s[1] for s in early["sums"]], "early")
        return token[0, 0]

    def last_grad(g):
        *last["pair"], token = _pair_exchange_start([_win_unpad(g["w_in"])], "w_in")
        return token[0, 0]

    loss, grad_x, g = _local_step(x[0], p[0, 0], target[0], full, late_weights, early_grads, mid_grads, last_grad)

    last["sums"] = pair_sums(*_pair_exchange_wait(*last["pair"], grad_x, "w_in"), _GROUP_NAMES[:1])
    *last["flight"], token = _chip_exchange_start([s[1] for s in last["sums"]], "w_in")

    def reduce_share(sums, others, names, tag):
        halves = [_reduce4(s[0], b, c, "reduce4_" + n) for s, b, n in zip(sums, others, names)]
        return list(_sibling_share(halves, "sibling_share_" + tag))

    ms, vs = shards(m), shards(v)
    grads = reduce_share(early["sums"], _chip_exchange_wait(*early["flight"], token, "early"), _GROUP_NAMES[1:], "early")
    upd = [_adamw_call(wa, ga, ma, va, "adamw_" + n)
           for wa, ga, ma, va, n in zip(_group(ws)[1:], grads, _group(ms)[1:], _group(vs)[1:], _GROUP_NAMES[1:])]
    small_g = _small_allreduce(_pack_small(g, extra=loss, conv=g["conv_qk"]))
    conv_g = lax.dynamic_slice(small_g[_CONV_ROW:_CONV_ROW + CONV], (0, j * (D // 4)), (CONV, D // 4))
    conv_upd = _adamw_call(ws["conv_qk"], conv_g, ms["conv_qk"], vs["conv_qk"], "adamw_conv")
    small_upd = _adamw_call(_pack_small(w), small_g, _pack_small(m), _pack_small(v), "adamw_small")

    done = sum(a[0][0:1, 0:1] for a in upd + [conv_upd, small_upd])
    others = _chip_exchange_wait(*last["flight"], done, "w_in")
    grads = reduce_share(last["sums"], others, _GROUP_NAMES[:1], "w_in") + list(grads)
    upd_in = _adamw_call(*[jnp.swapaxes(a, 0, 1) for a in (ws["w_in"], grads[0], ms["w_in"], vs["w_in"])], "adamw_w_in")
    upd = [[jnp.swapaxes(a, 0, 1) for a in upd_in]] + upd

    shapes = {n: w[n].shape for n in _NAMES}
    res = []
    for k in range(4):
        big = _ungroup(list(grads) if k == 0 else [u[k - 1] for u in upd])
        big["conv_qk"] = conv_g if k == 0 else conv_upd[k - 1]
        leaves = _unpack_small(small_g if k == 0 else small_upd[k - 1], shapes)
        leaves.update({n: a.reshape(shapes[n]) for n, a in big.items()})
        res.append(leaves)

    out = [small_g[5, 8 + SWH], grad_x[None]]
    for k in range(4):
        out += [res[k][n] for n in _NAMES]
    return tuple(out)


def kernel(x, p, norm_mix_g, w_in, conv_qk, b_if, mlstm_norm_g, sinks, w_branch_a, w_branch_b, w_out, norm_mlp_g, w_up, w_down, norm_ple_g, w_ple_gate, w_ple_proj, final_norm_g, loss_target, m_norm_mix_g, m_w_in, m_conv_qk, m_b_if, m_mlstm_norm_g, m_sinks, m_w_branch_a, m_w_branch_b, m_w_out, m_norm_mlp_g, m_w_up, m_w_down, m_norm_ple_g, m_w_ple_gate, m_w_ple_proj, m_final_norm_g, v_norm_mix_g, v_w_in, v_conv_qk, v_b_if, v_mlstm_norm_g, v_sinks, v_w_branch_a, v_w_branch_b, v_w_out, v_norm_mlp_g, v_w_up, v_w_down, v_norm_ple_g, v_w_ple_gate, v_w_ple_proj, v_final_norm_g):
    w = dict(zip(_NAMES, (norm_mix_g, w_in, conv_qk, b_if, mlstm_norm_g, sinks, w_branch_a, w_branch_b, w_out,
                          norm_mlp_g, w_up, w_down, norm_ple_g, w_ple_gate, w_ple_proj, final_norm_g)))
    m = dict(zip(_NAMES, (m_norm_mix_g, m_w_in, m_conv_qk, m_b_if, m_mlstm_norm_g, m_sinks, m_w_branch_a,
                          m_w_branch_b, m_w_out, m_norm_mlp_g, m_w_up, m_w_down, m_norm_ple_g, m_w_ple_gate,
                          m_w_ple_proj, m_final_norm_g)))
    v = dict(zip(_NAMES, (v_norm_mix_g, v_w_in, v_conv_qk, v_b_if, v_mlstm_norm_g, v_sinks, v_w_branch_a,
                          v_w_branch_b, v_w_out, v_norm_mlp_g, v_w_up, v_w_down, v_norm_ple_g, v_w_ple_gate,
                          v_w_ple_proj, v_final_norm_g)))
    return _step(x, p, loss_target, w, m, v)
```

```python
import jax
import jax.numpy as jnp
from jax import lax
from jax.experimental import pallas as pl
from jax.experimental.pallas import tpu as pltpu

F32 = jnp.float32
BF16 = jnp.bfloat16

D = 1024
PLE = 256
MLH = 4
DQK = 128
DV = 256
CONV = 4
CHUNK = 256
SWH = 16
SWKV = 4
SWG = SWH // SWKV
HD = 64
WIN = 128
DFF = 4096
EPS = 1e-6
N_IN = 6664
NP = 7168
C_QK, C_V, C_O, C_QSW, C_GA, C_GB, C_KV, C_IF = 0, 1024, 2048, 3072, 4096, 5120, 6144, 6656
IFW = NP - C_IF

ADAM_LR = 0.001
ADAM_B1 = 0.9
ADAM_B2 = 0.999
ADAM_EPS = 1e-08
ADAM_WD = 0.01
ADAM_STEP = 10

TOK_TILE = 512
V7X_VMEM_BYTES = 64 * 1024 * 1024
VMEM_LIMIT = V7X_VMEM_BYTES - 6 * 1024 * 1024


def _params(**kw):
    return pltpu.CompilerParams(vmem_limit_bytes=VMEM_LIMIT, **kw)


def _pick(n, cap):
    if n <= cap:
        return n
    t = cap - cap % 128
    while t > 128 and n % t:
        t -= 128
    assert n % t == 0, (n, cap)
    return t


def _dot(a, b, dims):
    return lax.dot_general(a, b, (dims, ((), ())), preferred_element_type=F32)


def _dot_nn(a, b):
    return _dot(a, b, ((1,), (0,)))


def _dot_nt(a, b):
    return _dot(a, b, ((1,), (1,)))


def _dot_tn(a, b):
    return _dot(a, b, ((0,), (0,)))


def _sigmoid(x):
    return 1.0 / (1.0 + jnp.exp(-x))


def _mm(a, b, mode, out_dtype, name, out_chunks=1):
    if mode == "nn":
        (m, k), (k2, n) = a.shape, b.shape
    elif mode == "nt":
        (m, k), (n, k2) = a.shape, b.shape
    else:
        (k, m), (k2, n) = a.shape, b.shape
    assert k == k2, (a.shape, b.shape, mode)
    tm, tn, tk = _pick(m, 1024), _pick(n // out_chunks, 1024), _pick(k, 2048)
    nk = k // tk
    if mode == "nn":
        a_spec = pl.BlockSpec((tm, tk), lambda i, j, kk: (i, kk))
        b_spec = pl.BlockSpec((tk, tn), lambda i, j, kk: (kk, j))
        dot = _dot_nn
    elif mode == "nt":
        a_spec = pl.BlockSpec((tm, tk), lambda i, j, kk: (i, kk))
        b_spec = pl.BlockSpec((tn, tk), lambda i, j, kk: (j, kk))
        dot = _dot_nt
    else:
        a_spec = pl.BlockSpec((tk, tm), lambda i, j, kk: (kk, i))
        b_spec = pl.BlockSpec((tk, tn), lambda i, j, kk: (kk, j))
        dot = _dot_tn
    if out_chunks > 1:
        npc = (n // out_chunks) // tn
        out_spec = pl.BlockSpec((None, tm, tn), lambda i, j, kk: (j // npc, i, j % npc))
        out_shape = jax.ShapeDtypeStruct((out_chunks, m, n // out_chunks), out_dtype)
    else:
        out_spec = pl.BlockSpec((tm, tn), lambda i, j, kk: (i, j))
        out_shape = jax.ShapeDtypeStruct((m, n), out_dtype)

    def body(a_ref, b_ref, o_ref, acc_ref):
        kk = pl.program_id(2)

        @pl.when(kk == 0)
        def _():
            acc_ref[...] = jnp.zeros_like(acc_ref)

        acc_ref[...] += dot(a_ref[...], b_ref[...])

        @pl.when(kk == nk - 1)
        def _():
            o_ref[...] = acc_ref[...].astype(out_dtype)

    return pl.pallas_call(
        body, name=name, grid=(m // tm, n // tn, nk),
        in_specs=[a_spec, b_spec], out_specs=out_spec, out_shape=out_shape,
        scratch_shapes=[pltpu.VMEM((tm, tn), F32)],
        compiler_params=_params(dimension_semantics=("parallel", "parallel", "arbitrary")),
    )(a, b)


def _tile(col0=0):
    return lambda tm, tn: pl.BlockSpec((tm, tn), lambda i, j, kk: (i, col0 // tn + j))


def _row():
    return lambda tm, tn: pl.BlockSpec((1, tn), lambda i, j, kk: (0, j))


def _mm_ep(pairs, mode, name, epilogue, ins, outs, tm, tn, aliases=None, row_split=1, init=None):
    a0, b0 = pairs[0]
    bch = b0.shape[0] if b0.ndim == 3 else 1
    m, k = a0.shape
    tm = _pick(m, tm)
    n = b0.shape[-1] * bch if mode == "nn" else b0.shape[-2]
    tk = _pick(k // bch if mode == "nt" else k, 2048)
    nk = k // tk
    a_spec = pl.BlockSpec((tm, tk), lambda i, j, kk: (i, kk))
    if mode == "nn":
        dot = _dot_nn
        if bch > 1:
            bpc = (n // bch) // tn
            b_spec = pl.BlockSpec((None, tk, tn), lambda i, j, kk: (j // bpc, kk, j % bpc))
        else:
            b_spec = pl.BlockSpec((tk, tn), lambda i, j, kk: (kk, j))
    else:
        dot = _dot_nt
        if bch > 1:
            bpc = (k // bch) // tk
            b_spec = pl.BlockSpec((None, tn, tk), lambda i, j, kk: (kk // bpc, j, kk % bpc))
        else:
            b_spec = pl.BlockSpec((tn, tk), lambda i, j, kk: (j, kk))
    npair, nin, nout = len(pairs), len(ins), len(outs)
    rows = tm // row_split
    assert init is None or row_split > 1

    def body_split(*refs):
        ab = refs[:2 * npair]
        in_refs = refs[2 * npair:2 * npair + nin]
        out_refs = refs[2 * npair + nin:2 * npair + nin + nout]
        accs = refs[2 * npair + nin + nout:]
        i, j, kk = pl.program_id(0), pl.program_id(1), pl.program_id(2)

        if init is not None:
            @pl.when((i == 0) & (kk == 0))
            def _():
                init(out_refs)

        @pl.when(kk < nk - 1)
        def _():
            for p in range(npair):
                prod = dot(ab[2 * p][...], ab[2 * p + 1][...])

                @pl.when(kk == 0)
                def _():
                    accs[p][...] = prod

                @pl.when(kk > 0)
                def _():
                    accs[p][...] += prod

        @pl.when(kk == nk - 1)
        def _():
            for r in range(row_split):
                rs = pl.ds(r * rows, rows)
                tot = []
                for p in range(npair):
                    prod = dot(ab[2 * p][rs, :], ab[2 * p + 1][...])
                    tot.append(prod if nk == 1 else accs[p][rs, :] + prod)

                def view(ref):
                    return ref.at[rs] if ref.shape[0] == tm else ref

                epilogue(tot, [view(x) for x in in_refs], [view(x) for x in out_refs], i * row_split + r, j)

    def body(*refs):
        ab = refs[:2 * npair]
        in_refs = refs[2 * npair:2 * npair + nin]
        out_refs = refs[2 * npair + nin:2 * npair + nin + nout]
        accs = refs[2 * npair + nin + nout:]
        i, j, kk = pl.program_id(0), pl.program_id(1), pl.program_id(2)
        for p in range(npair):
            prod = dot(ab[2 * p][...], ab[2 * p + 1][...])

            @pl.when(kk == 0)
            def _():
                accs[p][...] = prod

            @pl.when(kk > 0)
            def _():
                accs[p][...] += prod

        @pl.when(kk == nk - 1)
        def _():
            epilogue([acc[...] for acc in accs], in_refs, out_refs, i, j)

    operands = [x for pair in pairs for x in pair] + [a for a, _ in ins]
    io_alias = {2 * npair + i: o for i, o in (aliases or {}).items()}
    return pl.pallas_call(
        body if row_split == 1 else body_split, name=name, grid=(m // tm, n // tn, nk),
        in_specs=[a_spec, b_spec] * npair + [mk(tm, tn) for _, mk in ins],
        out_specs=[mk(tm, tn) for _, mk in outs], out_shape=[s for s, _ in outs],
        scratch_shapes=[pltpu.VMEM((tm, tn), F32)] * npair, input_output_aliases=io_alias,
        compiler_params=_params(dimension_semantics=("arbitrary", "arbitrary", "arbitrary")),
    )(*operands)


def _tok(w, j=0):
    return pl.BlockSpec((TOK_TILE, w), lambda i: (i, j))


def _rep(shape):
    return pl.BlockSpec(shape, lambda i: (0,) * len(shape))


def _rms(x):
    rstd = lax.rsqrt(jnp.mean(x * x, axis=-1, keepdims=True) + EPS)
    return x * rstd, rstd


def _rms_bwd(xn, rstd, dxn):
    return rstd * (dxn - xn * jnp.mean(dxn * xn, axis=-1, keepdims=True))


def _halo_prev(w, j=0, rows=8):
    r = TOK_TILE // rows
    return pl.BlockSpec((rows, w), lambda i: (jnp.maximum(i * r - 1, 0), j))


def _last8(halo_ref):
    return halo_ref[...].astype(F32)[halo_ref.shape[0] - 8:]


def _halo_next(w, nt, j=0):
    r = TOK_TILE // 8
    return pl.BlockSpec((8, w), lambda i: (jnp.minimum((i + 1) * r, nt * r - 1), j))


def _shift_down(x, halo, s):
    if s == 0:
        return x
    r = pltpu.roll(x, s, 0)
    hs = pltpu.roll(halo, s, 0)
    row = lax.broadcasted_iota(jnp.int32, hs.shape, 0)
    top = jnp.where(row < s, hs, r[0:8])
    return jnp.concatenate([top, r[8:]], axis=0)


def _shift_up(x, halo, s):
    if s == 0:
        return x
    n = x.shape[0]
    r = pltpu.roll(x, n - s, 0)
    hs = pltpu.roll(halo, 8 - s, 0)
    row = lax.broadcasted_iota(jnp.int32, hs.shape, 0)
    bot = jnp.where(row >= 8 - s, hs, r[n - 8:])
    return jnp.concatenate([r[:n - 8], bot], axis=0)


def _bf(x):
    return x.astype(BF16).astype(F32)


def _conv_taps(x, halo, w):
    x, halo, w = _bf(x), _bf(halo), _bf(w)
    acc = x * w[CONV - 1:CONV, :]
    for j in range(CONV - 1):
        acc = acc + _shift_down(x, halo, CONV - 1 - j) * w[j:j + 1, :]
    return acc


_Q_SCALE = DQK ** -0.5


def _qscale_row():
    lane = lax.broadcasted_iota(jnp.int32, (1, D), 1)
    return jnp.where(lane < MLH * DQK, _Q_SCALE, 1.0).astype(F32)


def _conv_silu_fwd(proj, conv_w):
    t = proj.shape[0]

    def body(x_ref, halo_ref, w_ref, o_ref):
        halo = jnp.where(pl.program_id(0) > 0, _last8(halo_ref), 0.0)
        c = _conv_taps(x_ref[...].astype(F32), halo, w_ref[...])
        o_ref[...] = (c * _sigmoid(c) * _qscale_row()).astype(BF16)

    return pl.pallas_call(
        body, name="conv_silu_fwd", grid=(t // TOK_TILE,),
        in_specs=[_tok(D, C_QK // D), _halo_prev(D, C_QK // D, 16), _rep((CONV, D))], out_specs=_tok(D),
        out_shape=jax.ShapeDtypeStruct((t, D), BF16), compiler_params=_params(),
    )(proj, proj, conv_w)


def _conv_silu_bwd_a(proj, conv_w, dqk):
    t = proj.shape[0]

    def body(x_ref, halo_ref, w_ref, d_ref, dc_ref, dw_ref):
        @pl.when(pl.program_id(0) == 0)
        def _():
            dw_ref[...] = jnp.zeros_like(dw_ref)

        halo = jnp.where(pl.program_id(0) > 0, _last8(halo_ref), 0.0)
        x = x_ref[...].astype(F32)
        c = _conv_taps(x, halo, w_ref[...])
        s = _sigmoid(c)
        dc = d_ref[...] * _qscale_row() * (s * (1.0 + c * (1.0 - s)))
        dc_ref[...] = dc
        dcb, xb, halo_b = _bf(dc), _bf(x), _bf(halo)
        for j in range(CONV):
            dw_ref[j:j + 1, :] += jnp.sum(dcb * _shift_down(xb, halo_b, CONV - 1 - j), axis=0, keepdims=True)

    return pl.pallas_call(
        body, name="conv_silu_bwd_a", grid=(t // TOK_TILE,),
        in_specs=[_tok(D, C_QK // D), _halo_prev(D, C_QK // D, 16), _rep((CONV, D)), _tok(D)],
        out_specs=[_tok(D), _rep((CONV, D))],
        out_shape=[jax.ShapeDtypeStruct((t, D), F32), jax.ShapeDtypeStruct((CONV, D), F32)],
        compiler_params=_params(),
    )(proj, proj, conv_w, dqk)


def _conv_silu_bwd_b(dc, conv_w, dproj):
    t = dc.shape[0]
    nt = t // TOK_TILE

    def body(dc_ref, halo_ref, w_ref, _, dx_ref):
        halo = _bf(jnp.where(pl.program_id(0) < nt - 1, halo_ref[...], 0.0))
        dcv = _bf(dc_ref[...])
        w = _bf(w_ref[...])
        acc = dcv * w[CONV - 1:CONV, :]
        for j in range(CONV - 1):
            acc = acc + _shift_up(dcv, halo, CONV - 1 - j) * w[j:j + 1, :]
        dx_ref[...] = acc.astype(BF16)

    return pl.pallas_call(
        body, name="conv_silu_bwd_b", grid=(nt,), in_specs=[_tok(D), _halo_next(D, nt), _rep((CONV, D)), _ANY],
        out_specs=_tok(D, C_QK // D), out_shape=jax.ShapeDtypeStruct((t, NP), BF16),
        input_output_aliases={3: 0}, compiler_params=_params(),
    )(dc, dc, conv_w, dproj)


def _gates_fwd(pre_rows, bias_col):
    t = pre_rows.shape[1]

    def body(p_ref, b_ref, g_ref, s_ref):
        z = p_ref[...] + b_ref[...]
        lf = jnp.minimum(z, 0.0) - jnp.log(1.0 + jnp.exp(-jnp.abs(z)))
        lane = lax.broadcasted_iota(jnp.int32, z.shape, 1) % CHUNK
        cum = lf
        s = 1
        while s < CHUNK:
            cum = cum + jnp.where(lane >= s, pltpu.roll(cum, s, 1), 0.0)
            s *= 2
        sub = lax.broadcasted_iota(jnp.int32, z.shape, 0)
        g_ref[...] = jnp.where(sub < MLH, z, cum)
        s_ref[...] = _sigmoid(-z)

    return pl.pallas_call(
        body, name="gates_fwd",
        out_shape=[jax.ShapeDtypeStruct((8, t), F32), jax.ShapeDtypeStruct((8, t), F32)],
        compiler_params=_params(),
    )(pre_rows, bias_col)


def _chunk_terms(grow, gcol, m0):
    heads = range(MLH)
    i_row = [grow[h:h + 1, :] for h in heads]
    b_row = [grow[MLH + h:MLH + h + 1, :] for h in heads]
    i_col = [gcol[:, h:h + 1] for h in heads]
    b_col = [gcol[:, MLH + h:MLH + h + 1] for h in heads]
    b_last = [b_row[h][:, CHUNK - 1:CHUNK] for h in heads]
    tt = lax.broadcasted_iota(jnp.int32, (CHUNK, CHUNK), 0)
    ss = lax.broadcasted_iota(jnp.int32, (CHUNK, CHUNK), 1)
    log_d = [jnp.where(tt >= ss, b_col[h] - b_row[h] + i_row[h], -jnp.inf) for h in heads]
    row_max = [jnp.max(log_d[h], axis=1, keepdims=True) for h in heads]
    last_max = [jnp.max(b_last[h] - b_row[h] + i_row[h], axis=1, keepdims=True) for h in heads]
    m_t = [jnp.maximum(b_col[h] + m0[h], row_max[h]) for h in heads]
    m1 = [jnp.maximum(b_last[h] + m0[h], last_max[h]) for h in heads]
    dm = [jnp.exp(log_d[h] - m_t[h]) for h in heads]
    wi = [jnp.exp(b_col[h] + m0[h] - m_t[h]) for h in heads]
    ws = [jnp.exp(b_last[h] - b_col[h] + i_col[h] - m1[h]) for h in heads]
    dec = [jnp.exp(b_last[h] + m0[h] - m1[h]) for h in heads]
    return [(dm[h], wi[h], m_t[h], ws[h], dec[h], m1[h]) for h in heads]


def _mlstm_fwd(qk, proj, grow, gcol, gain):
    t = qk.shape[0]
    nc = t // CHUNK

    def body(qk_ref, v_ref, o_ref, grow_ref, gcol_ref, g_ref, h_ref, y_ref, cs_ref, st_ref, c_scr, st_scr):
        @pl.when(pl.program_id(0) == 0)
        def _():
            c_scr[...] = jnp.zeros_like(c_scr)
            st_scr[...] = jnp.zeros_like(st_scr)

        grow_v, gcol_v = grow_ref[...], gcol_ref[...]
        heads = range(MLH)
        q = [qk_ref[:, h * DQK:(h + 1) * DQK] for h in heads]
        k = [qk_ref[:, MLH * DQK + h * DQK:MLH * DQK + (h + 1) * DQK] for h in heads]
        v = [v_ref[:, h * DV:(h + 1) * DV] for h in heads]
        c0 = [c_scr[h] for h in heads]
        n0 = [st_scr[h, 0:1, :] for h in heads]
        for h in heads:
            cs_ref[0, h] = c0[h]
            st_ref[0, h] = st_scr[h]
        terms = _chunk_terms(grow_v, gcol_v, [st_scr[h, 1:2, 0:1] for h in heads])
        a = [_dot_nt(q[h], k[h]) for h in heads]
        qc = [_dot_nt(q[h], c0[h].astype(BF16)) for h in heads]
        s = [a[h] * terms[h][0] for h in heads]
        sv = [_dot_nn(s[h].astype(BF16), v[h]) for h in heads]
        upd = [_dot_tn((terms[h][3] * v[h]).astype(BF16), k[h]) for h in heads]
        den = [terms[h][1] * jnp.sum(q[h].astype(F32) * n0[h], axis=1, keepdims=True)
               + jnp.sum(s[h], axis=1, keepdims=True) for h in heads]
        hv = [(terms[h][1] * qc[h] + sv[h]) / jnp.maximum(jnp.abs(den[h]), jnp.exp(-terms[h][2])) for h in heads]
        for h in heads:
            sl = slice(h * DV, (h + 1) * DV)
            h_ref[:, sl] = hv[h]
            xn, _ = _rms(hv[h])
            y_ref[:, sl] = (_sigmoid(o_ref[:, sl].astype(F32)) * xn * g_ref[:, sl]).astype(BF16)
        for h in heads:
            dec, m1 = terms[h][4], terms[h][5]
            c_scr[h] = dec * c0[h] + upd[h]
            st_scr[h, 0:1, :] = dec * n0[h] + jnp.sum(terms[h][3] * k[h].astype(F32), axis=0, keepdims=True)
            st_scr[h, 1:2, :] = jnp.broadcast_to(m1, (1, DQK))

    return pl.pallas_call(
        body, name="mlstm_fwd", grid=(nc,),
        in_specs=[pl.BlockSpec((CHUNK, D), lambda c: (c, 0)), pl.BlockSpec((CHUNK, D), lambda c: (c, C_V // D)),
                  pl.BlockSpec((CHUNK, D), lambda c: (c, C_O // D)),
                  pl.BlockSpec((8, CHUNK), lambda c: (0, c)), pl.BlockSpec((CHUNK, 8), lambda c: (c, 0)),
                  pl.BlockSpec((1, D), lambda c: (0, 0))],
        out_specs=[pl.BlockSpec((CHUNK, D), lambda c: (c, 0)), pl.BlockSpec((CHUNK, D), lambda c: (c, 0)),
                   pl.BlockSpec((1, MLH, DV, DQK), lambda c: (c, 0, 0, 0)),
                   pl.BlockSpec((1, MLH, 8, DQK), lambda c: (c, 0, 0, 0))],
        out_shape=[jax.ShapeDtypeStruct((t, D), F32), jax.ShapeDtypeStruct((t, D), BF16),
                   jax.ShapeDtypeStruct((nc, MLH, DV, DQK), F32), jax.ShapeDtypeStruct((nc, MLH, 8, DQK), F32)],
        scratch_shapes=[pltpu.VMEM((MLH, DV, DQK), F32), pltpu.VMEM((MLH, 8, DQK), F32)],
        compiler_params=_params(dimension_semantics=("arbitrary",)),
    )(qk, proj, proj, grow, gcol, gain)


def _mlstm_bwd(qk, proj, grow, gcol, sneg_col, cs, st, hraw, dh, dproj):
    t = qk.shape[0]
    nc = t // CHUNK

    def rev(c):
        return nc - 1 - c

    def nxt(c):
        return jnp.minimum(nc - c, nc - 1)

    def body(qk_ref, v_ref, grow_ref, gcol_ref, sneg_ref, cs_ref, st_ref, cs1_ref, st1_ref, h_ref, dh_ref, _,
             dqk_ref, dv_ref, dif_ref, dbif_ref, dc_scr, dn_scr):
        @pl.when(pl.program_id(0) == 0)
        def _():
            dc_scr[...] = jnp.zeros_like(dc_scr)
            dn_scr[...] = jnp.zeros_like(dn_scr)
            dbif_ref[...] = jnp.zeros_like(dbif_ref)

        grow_v, gcol_v, sneg = grow_ref[...], gcol_ref[...], sneg_ref[...]
        tt = lax.broadcasted_iota(jnp.int32, (CHUNK, CHUNK), 0)
        ss = lax.broadcasted_iota(jnp.int32, (CHUNK, CHUNK), 1)
        lane8 = lax.broadcasted_iota(jnp.int32, (CHUNK, 8), 1)
        heads = range(MLH)
        q = [qk_ref[:, h * DQK:(h + 1) * DQK] for h in heads]
        k = [qk_ref[:, MLH * DQK + h * DQK:MLH * DQK + (h + 1) * DQK] for h in heads]
        qf, kf = [a.astype(F32) for a in q], [a.astype(F32) for a in k]
        vb = [v_ref[:, h * DV:(h + 1) * DV].astype(BF16) for h in heads]
        c0 = [cs_ref[0, h] for h in heads]
        n0 = [st_ref[0, h, 0:1, :] for h in heads]
        dc1 = [dc_scr[h] for h in heads]
        dn1 = [dn_scr[h, 0:1, :] for h in heads]
        terms = _chunk_terms(grow_v, gcol_v, [st_ref[0, h, 1:2, 0:1] for h in heads])
        dm, wi, ws = [t[0] for t in terms], [t[1] for t in terms], [t[3] for t in terms]
        s = [_dot_nt(q[h], k[h]) * dm[h] for h in heads]
        den = [wi[h] * jnp.sum(qf[h] * n0[h], axis=1, keepdims=True) + jnp.sum(s[h], axis=1, keepdims=True)
               for h in heads]
        floor = [jnp.exp(-terms[h][2]) for h in heads]
        g = [jnp.maximum(jnp.abs(den[h]), floor[h]) for h in heads]
        dh_v = [dh_ref[:, h * DV:(h + 1) * DV] for h in heads]
        dnum = [dh_v[h] / g[h] for h in heads]
        dden = [-jnp.sum(dh_v[h] * h_ref[:, h * DV:(h + 1) * DV], axis=1, keepdims=True) / g[h] for h in heads]
        dden = [jnp.where(jnp.abs(den[h]) > floor[h], dden[h] * jnp.sign(den[h]), 0.0) for h in heads]
        dnum_b = [a.astype(BF16) for a in dnum]
        dc1_b = [a.astype(BF16) for a in dc1]
        da = [((_dot_nt(dnum_b[h], vb[h]) + dden[h]) * dm[h]).astype(BF16) for h in heads]
        dq_inter = [_dot_nn(dnum_b[h], c0[h].astype(BF16)) for h in heads]
        dk_inter = [_dot_nn(vb[h], dc1_b[h]) for h in heads]
        dv_inter = [_dot_nt(k[h], dc1_b[h]) for h in heads]
        dc_new = [_dot_tn((wi[h] * dnum[h]).astype(BF16), q[h]) for h in heads]
        dq = [_dot_nn(da[h], k[h]) + wi[h] * (dq_inter[h] + dden[h] * n0[h]) for h in heads]
        dk = [_dot_tn(da[h], q[h]) + ws[h] * (dk_inter[h] + dn1[h]) for h in heads]
        dv = [_dot_tn(s[h].astype(BF16), dnum_b[h]) + ws[h] * dv_inter[h] for h in heads]
        for h in heads:
            dqk_ref[:, h * DQK:(h + 1) * DQK] = dq[h]
            dqk_ref[:, MLH * DQK + h * DQK:MLH * DQK + (h + 1) * DQK] = dk[h]
            dv_ref[:, h * DV:(h + 1) * DV] = dv[h].astype(BF16)
        rk = [jnp.sum(kf[h] * dk[h], axis=1, keepdims=True) for h in heads]
        df = [jnp.sum(qf[h] * dq[h], axis=1, keepdims=True) - rk[h] for h in heads]
        df_row = [jnp.sum(jnp.where(tt == ss, df[h], 0.0), axis=0, keepdims=True) for h in heads]
        suffix = [jnp.sum(jnp.where(ss >= tt, df_row[h], 0.0), axis=1, keepdims=True) for h in heads]
        cross = [jnp.sum(jnp.sum(dc1[h] * cs1_ref[0, h], axis=0, keepdims=True), axis=1, keepdims=True)
                 + jnp.sum(dn1[h] * st1_ref[0, h, 0:1, :], axis=1, keepdims=True) for h in heads]
        dif = jnp.zeros((CHUNK, 8), F32)
        for h in heads:
            dpf = (suffix[h] + cross[h]) * sneg[:, MLH + h:MLH + h + 1]
            dif = dif + jnp.where(lane8 == h, rk[h], 0.0) + jnp.where(lane8 == MLH + h, dpf, 0.0)
            dc_scr[h] = terms[h][4] * dc1[h] + dc_new[h]
            dn_scr[h, 0:1, :] = terms[h][4] * dn1[h] + jnp.sum(wi[h] * dden[h] * qf[h], axis=0, keepdims=True)
        dif_ref[...] = dif
        dbif_ref[...] += jnp.sum(dif, axis=0, keepdims=True)

    return pl.pallas_call(
        body, name="mlstm_bwd", grid=(nc,),
        in_specs=[pl.BlockSpec((CHUNK, D), lambda c: (rev(c), 0)),
                  pl.BlockSpec((CHUNK, D), lambda c: (rev(c), C_V // D)),
                  pl.BlockSpec((8, CHUNK), lambda c: (0, rev(c))),
                  pl.BlockSpec((CHUNK, 8), lambda c: (rev(c), 0)),
                  pl.BlockSpec((CHUNK, 8), lambda c: (rev(c), 0)),
                  pl.BlockSpec((1, MLH, DV, DQK), lambda c: (rev(c), 0, 0, 0)),
                  pl.BlockSpec((1, MLH, 8, DQK), lambda c: (rev(c), 0, 0, 0)),
                  pl.BlockSpec((1, MLH, DV, DQK), lambda c: (nxt(c), 0, 0, 0)),
                  pl.BlockSpec((1, MLH, 8, DQK), lambda c: (nxt(c), 0, 0, 0)),
                  pl.BlockSpec((CHUNK, D), lambda c: (rev(c), 0)),
                  pl.BlockSpec((CHUNK, D), lambda c: (rev(c), 0)), _ANY],
        out_specs=[pl.BlockSpec((CHUNK, D), lambda c: (rev(c), 0)),
                   pl.BlockSpec((CHUNK, D), lambda c: (rev(c), C_V // D)),
                   pl.BlockSpec((CHUNK, 8), lambda c: (rev(c), 0)),
                   pl.BlockSpec((1, 8), lambda c: (0, 0))],
        out_shape=[jax.ShapeDtypeStruct((t, D), F32), jax.ShapeDtypeStruct((t, NP), BF16),
                   jax.ShapeDtypeStruct((t, 8), F32), jax.ShapeDtypeStruct((1, 8), F32)],
        scratch_shapes=[pltpu.VMEM((MLH, DV, DQK), F32), pltpu.VMEM((MLH, 8, DQK), F32)],
        input_output_aliases={11: 1}, compiler_params=_params(dimension_semantics=("arbitrary",)),
    )(qk, proj, grow, gcol, sneg_col, cs, st, cs, st, hraw, dh, dproj)


_ANY = pl.BlockSpec(memory_space=pl.ANY)


_SW_SCALE = HD ** -0.5
_KVB = C_KV // (2 * SWKV * HD)


def _swa_mask(n):
    ki = lax.broadcasted_iota(jnp.int32, (2 * WIN, SWG * WIN), 0)
    qi = lax.broadcasted_iota(jnp.int32, (2 * WIN, SWG * WIN), 1) % WIN
    return (ki > qi) & (ki <= qi + WIN) & ((n > 0) | (ki >= WIN))


def _group_rows(x_ref, hk):
    return jnp.concatenate([x_ref[:, (hk * SWG + g) * HD:(hk * SWG + g + 1) * HD] for g in range(SWG)], axis=0)


def _group_lanes(x_ref, hk):
    return jnp.concatenate([x_ref[hk * SWG + g:hk * SWG + g + 1, :] for g in range(SWG)], axis=1)


def _sink_lanes(sink_ref, hk):
    return jnp.concatenate([jnp.broadcast_to(sink_ref[:, hk * SWG + g:hk * SWG + g + 1], (1, WIN))
                            for g in range(SWG)], axis=1)


def _swa_fwd(proj, sinks):
    t = proj.shape[0]
    nb = t // WIN

    def body(q_ref, kvc_ref, kvp_ref, sink_ref, y_ref, lse_ref):
        valid = _swa_mask(pl.program_id(0))
        for hk in range(SWKV):
            ks = slice(hk * HD, (hk + 1) * HD)
            vs = slice(SWKV * HD + hk * HD, SWKV * HD + (hk + 1) * HD)
            kb = jnp.concatenate([kvp_ref[:, ks], kvc_ref[:, ks]], axis=0).astype(BF16)
            vb = jnp.concatenate([kvp_ref[:, vs], kvc_ref[:, vs]], axis=0).astype(BF16)
            q4 = _group_rows(q_ref, hk).astype(BF16)
            sink = _sink_lanes(sink_ref, hk)
            logits = jnp.where(valid, _dot_nt(kb, q4) * _SW_SCALE, -jnp.inf)
            m = jnp.maximum(jnp.max(logits, axis=0, keepdims=True), sink)
            p = jnp.exp(logits - m)
            denom = jnp.sum(p, axis=0, keepdims=True) + jnp.exp(sink - m)
            y4 = _dot_tn((p / denom).astype(BF16), vb).astype(BF16)
            lse4 = m + jnp.log(denom)
            for g in range(SWG):
                hq = hk * SWG + g
                y_ref[:, hq * HD:(hq + 1) * HD] = y4[g * WIN:(g + 1) * WIN]
                lse_ref[hq:hq + 1, :] = lse4[:, g * WIN:(g + 1) * WIN]

    return pl.pallas_call(
        body, name="swa_fwd", grid=(nb,),
        in_specs=[pl.BlockSpec((WIN, D), lambda n: (n, C_QSW // D)),
                  pl.BlockSpec((WIN, 512), lambda n: (n, _KVB)),
                  pl.BlockSpec((WIN, 512), lambda n: (jnp.maximum(n - 1, 0), _KVB)),
                  pl.BlockSpec((1, SWH), lambda n: (0, 0))],
        out_specs=[pl.BlockSpec((WIN, D), lambda n: (n, 0)), pl.BlockSpec((SWH, WIN), lambda n: (0, n))],
        out_shape=[jax.ShapeDtypeStruct((t, D), BF16), jax.ShapeDtypeStruct((SWH, t), F32)],
        compiler_params=_params(),
    )(proj, proj, proj, sinks)


def _swa_bwd(proj, sinks, lse, dyb, dproj):
    t = proj.shape[0]
    nb = t // WIN

    def body(q_ref, kvc_ref, kvp_ref, sink_ref, lse_ref, dy_ref, _, dq_ref, dself_ref, dprev_ref, ds_ref):
        @pl.when(pl.program_id(0) == 0)
        def _():
            ds_ref[...] = jnp.zeros_like(ds_ref)

        valid = _swa_mask(pl.program_id(0))
        kvh = range(SWKV)
        ks = [slice(hk * HD, (hk + 1) * HD) for hk in kvh]
        vs = [slice(SWKV * HD + hk * HD, SWKV * HD + (hk + 1) * HD) for hk in kvh]
        kb = [jnp.concatenate([kvp_ref[:, ks[hk]], kvc_ref[:, ks[hk]]], axis=0).astype(BF16) for hk in kvh]
        vb = [jnp.concatenate([kvp_ref[:, vs[hk]], kvc_ref[:, vs[hk]]], axis=0).astype(BF16) for hk in kvh]
        qb = [_group_rows(q_ref, hk).astype(BF16) for hk in kvh]
        dyb_ = [_group_rows(dy_ref, hk).astype(BF16) for hk in kvh]
        lse4 = [_group_lanes(lse_ref, hk) for hk in kvh]
        logits = [_dot_nt(kb[hk], qb[hk]) for hk in kvh]
        dpt = [_dot_nt(vb[hk], dyb_[hk]) for hk in kvh]
        p = [jnp.exp(jnp.where(valid, logits[hk] * _SW_SCALE, -jnp.inf) - lse4[hk]) for hk in kvh]
        delta = [jnp.sum(p[hk] * dpt[hk], axis=0, keepdims=True) for hk in kvh]
        dsm = [(p[hk] * (dpt[hk] - delta[hk])).astype(BF16) for hk in kvh]
        dq4 = [(_dot_tn(dsm[hk], kb[hk]) * _SW_SCALE).astype(BF16) for hk in kvh]
        dkb = [_dot_nn(dsm[hk], qb[hk]) * _SW_SCALE for hk in kvh]
        dvb = [_dot_nn(p[hk].astype(BF16), dyb_[hk]) for hk in kvh]
        for hk in kvh:
            dsink4 = jnp.exp(_sink_lanes(sink_ref, hk) - lse4[hk]) * delta[hk]
            for g in range(SWG):
                hq = hk * SWG + g
                dq_ref[:, hq * HD:(hq + 1) * HD] = dq4[hk][g * WIN:(g + 1) * WIN]
                ds_ref[:, hq:hq + 1] += -jnp.sum(dsink4[:, g * WIN:(g + 1) * WIN], axis=1, keepdims=True)
            dprev_ref[:, ks[hk]] = dkb[hk][:WIN]
            dself_ref[:, ks[hk]] = dkb[hk][WIN:]
            dprev_ref[:, vs[hk]] = dvb[hk][:WIN]
            dself_ref[:, vs[hk]] = dvb[hk][WIN:]

    return pl.pallas_call(
        body, name="swa_bwd", grid=(nb,),
        in_specs=[pl.BlockSpec((WIN, D), lambda n: (n, C_QSW // D)),
                  pl.BlockSpec((WIN, 512), lambda n: (n, _KVB)),
                  pl.BlockSpec((WIN, 512), lambda n: (jnp.maximum(n - 1, 0), _KVB)),
                  pl.BlockSpec((1, SWH), lambda n: (0, 0)),
                  pl.BlockSpec((SWH, WIN), lambda n: (0, n)),
                  pl.BlockSpec((WIN, D), lambda n: (n, 0)), _ANY],
        out_specs=[pl.BlockSpec((WIN, D), lambda n: (n, C_QSW // D)), pl.BlockSpec((WIN, 512), lambda n: (n, 0)),
                   pl.BlockSpec((WIN, 512), lambda n: (jnp.maximum(n - 1, 0), 0)),
                   pl.BlockSpec((1, SWH), lambda n: (0, 0))],
        out_shape=[jax.ShapeDtypeStruct((t, NP), BF16), jax.ShapeDtypeStruct((t, 512), F32),
                   jax.ShapeDtypeStruct((t, 512), F32), jax.ShapeDtypeStruct((1, SWH), F32)],
        input_output_aliases={6: 0}, compiler_params=_params(),
    )(proj, proj, proj, sinks, lse, dyb, dproj)


def _kv_combine(dself, dnext, dif, dproj):
    t = dself.shape[0]
    rows = _pick(t, 512)

    def body(a_ref, b_ref, dif_ref, _, o_ref):
        row = pl.program_id(0) * rows + lax.broadcasted_iota(jnp.int32, (rows, 1), 0)
        o_ref[:, 0:512] = (a_ref[...] + jnp.where(row < t - WIN, b_ref[...], 0.0)).astype(BF16)
        lane = lax.broadcasted_iota(jnp.int32, (rows, 128), 1)
        dif_v = dif_ref[...]
        first = jnp.zeros((rows, 128), F32)
        for col in range(8):
            first = first + jnp.where(lane == col, dif_v[:, col:col + 1], 0.0)
        o_ref[:, 512:640] = first.astype(BF16)
        o_ref[:, 640:512 + IFW] = jnp.zeros((rows, IFW - 128), BF16)

    return pl.pallas_call(
        body, name="kv_combine", grid=(t // rows,),
        in_specs=[pl.BlockSpec((rows, 512), lambda n: (n, 0)), pl.BlockSpec((rows, 512), lambda n: (n, 0)),
                  pl.BlockSpec((rows, 8), lambda n: (n, 0)), _ANY],
        out_specs=pl.BlockSpec((rows, 512 + IFW), lambda n: (n, C_KV // (512 + IFW))),
        out_shape=jax.ShapeDtypeStruct((t, NP), BF16), input_output_aliases={3: 0}, compiler_params=_params(),
    )(dself, dnext, dif, dproj)


def _sds(t, n, dtype):
    return jax.ShapeDtypeStruct((t, n), dtype)


def _proj_in(x, gain, w_in):
    t = x.shape[0]
    tm, tn = _pick(t, 1024), NP // 4

    def body(x_ref, g_ref, w_ref, h_ref, p_ref, gate_ref, h_scr):
        j = pl.program_id(1)

        @pl.when(j == 0)
        def _():
            xn, _ = _rms(x_ref[...])
            h = (xn * g_ref[...]).astype(BF16)
            h_scr[...] = h
            h_ref[...] = h

        acc = _dot_nn(h_scr[...], w_ref[...])
        p_ref[...] = acc.astype(BF16)

        @pl.when(j == C_IF // tn)
        def _():
            gate_ref[...] = acc[:, C_IF % tn:C_IF % tn + 128]

    return pl.pallas_call(
        body, name="mm_in", grid=(t // tm, NP // tn),
        in_specs=[pl.BlockSpec((tm, D), lambda i, j: (i, 0)), pl.BlockSpec((1, D), lambda i, j: (0, 0)),
                  pl.BlockSpec((D, tn), lambda i, j: (0, j))],
        out_specs=[pl.BlockSpec((tm, D), lambda i, j: (i, 0)), pl.BlockSpec((tm, tn), lambda i, j: (i, j)),
                   pl.BlockSpec((tm, 128), lambda i, j: (i, 0))],
        out_shape=[_sds(t, D, BF16), _sds(t, NP, BF16), _sds(t, 128, F32)],
        scratch_shapes=[pltpu.VMEM((tm, D), BF16)],
        compiler_params=_params(dimension_semantics=("arbitrary", "arbitrary")),
    )(x, gain, w_in)


def _branch_merge(ya, yb, wa, wb, proj):
    t = ya.shape[0]

    def epilogue(accs, ins, outs, i, j):
        za, zb = accs
        merged = _sigmoid(ins[0][...].astype(F32)) * za + _sigmoid(ins[1][...].astype(F32)) * zb
        outs[0][...] = merged.astype(BF16)
        outs[1][...] = za.astype(BF16)
        outs[2][...] = zb.astype(BF16)

    return _mm_ep([(ya, wa), (yb, wb)], "nn", "mm_branch_merge", epilogue, [(proj, _tile(C_GA)), (proj, _tile(C_GB))],
                  [(_sds(t, D, BF16), _tile())] * 3, 1024, 1024)


def _dmerged_bwd(dxb, w_out, proj, za, zb):
    t = dxb.shape[0]

    def epilogue(accs, ins, outs, i, j):
        dm = accs[0]
        sa, sb = _sigmoid(ins[0][...].astype(F32)), _sigmoid(ins[1][...].astype(F32))
        outs[0][...] = (dm * sa).astype(BF16)
        outs[1][...] = (dm * sb).astype(BF16)
        outs[2][:, 0:D] = (dm * ins[2][...].astype(F32) * sa * (1.0 - sa)).astype(BF16)
        outs[2][:, D:2 * D] = (dm * ins[3][...].astype(F32) * sb * (1.0 - sb)).astype(BF16)

    gate_cols = lambda tm, tn: pl.BlockSpec((tm, 2 * D), lambda i, j, kk: (i, C_GA // (2 * D)))
    return _mm_ep([(dxb, w_out)], "nt", "mm_dmerged_bwd", epilogue,
                  [(proj, _tile(C_GA)), (proj, _tile(C_GB)), (za, _tile()), (zb, _tile())],
                  [(_sds(t, D, BF16), _tile()), (_sds(t, D, BF16), _tile()), (_sds(t, NP, BF16), gate_cols)], 1024, D)


def _dya_bwd(dza, wa, hraw, proj, g, dproj):
    t = dza.shape[0]

    def epilogue(accs, ins, outs, i, j):
        h_ref, o_ref, g_ref, _ = ins
        dh_ref, do_ref, dg_ref = outs

        @pl.when(i == 0)
        def _():
            dg_ref[...] = jnp.zeros_like(dg_ref)

        dy = accs[0]
        so = _sigmoid(o_ref[...].astype(F32))
        for h in range(MLH):
            sl = slice(h * DV, (h + 1) * DV)
            xn, rstd = _rms(h_ref[:, sl])
            gs = g_ref[:, sl]
            do_ref[:, sl] = (dy[:, sl] * xn * gs * so[:, sl] * (1.0 - so[:, sl])).astype(BF16)
            dhn = dy[:, sl] * so[:, sl]
            dg_ref[:, sl] += jnp.sum(dhn * xn, axis=0, keepdims=True)
            dh_ref[:, sl] = _rms_bwd(xn, rstd, dhn * gs)

    return _mm_ep([(dza, wa)], "nt", "mm_dya_bwd", epilogue,
                  [(hraw, _tile()), (proj, _tile(C_O)), (g, _row()), (dproj, lambda tm, tn: _ANY)],
                  [(_sds(t, D, F32), _tile()), (_sds(t, NP, BF16), _tile(C_O)), (_sds(1, D, F32), _row())],
                  512, D, aliases={3: 1})


def _up_act(hn, w_up):
    t = hn.shape[0]

    def epilogue(accs, ins, outs, i, j):
        r = jnp.maximum(accs[0], 0.0)
        outs[0][...] = (r * r).astype(BF16)
        outs[1][...] = accs[0].astype(BF16)

    return _mm_ep([(hn, w_up)], "nn", "mm_up_act", epilogue, [],
                  [(_sds(t, DFF, BF16), _tile()), (_sds(t, DFF, BF16), _tile())], 1024, 1024)


def _da_du(dxb, w_down, u):
    t = dxb.shape[0]

    def epilogue(accs, ins, outs, i, j):
        outs[0][...] = (accs[0] * 2.0 * jnp.maximum(ins[0][...].astype(F32), 0.0)).astype(BF16)

    return _mm_ep([(dxb, w_down)], "nt", "mm_da_du", epilogue, [(u, _tile())], [(_sds(t, DFF, BF16), _tile())],
                  1024, 1024)[0]


def _resid_norm_mm(a, w, x, g, name):
    t = x.shape[0]

    def epilogue(accs, ins, outs, i, j):
        x1 = ins[0][...] + accs[0]
        outs[0][...] = x1
        xn, _ = _rms(x1)
        outs[1][...] = (xn * ins[1][...]).astype(BF16)

    return _mm_ep([(a, w)], "nn", name, epilogue, [(x, _tile()), (g, _row())],
                  [(_sds(t, D, F32), _tile()), (_sds(t, D, BF16), _tile())], 1024, D, row_split=4)


def _norm_bwd_mm(dy, w, x, g, dres, name):
    t = x.shape[0]

    def init(outs):
        outs[2][...] = jnp.zeros_like(outs[2])

    def epilogue(accs, ins, outs, i, j):
        dh = accs[0]
        xn, rstd = _rms(ins[0][...])
        outs[2][...] += jnp.sum(dh * xn, axis=0, keepdims=True)
        dx = ins[2][...] + _rms_bwd(xn, rstd, dh * ins[1][...])
        outs[0][...] = dx
        outs[1][...] = dx.astype(BF16)

    return _mm_ep([(dy, w)], "nt", name, epilogue, [(x, _tile()), (g, _row()), (dres, _tile())],
                  [(_sds(t, D, F32), _tile()), (_sds(t, D, BF16), _tile()), (_sds(1, D, F32), _row())], 1024, D,
                  row_split=4, init=init)


def _ple_final_mm(hn2, w_gate, x2, pp, target, gf):
    t = x2.shape[0]

    def epilogue(accs, ins, outs, i, j):
        loss_ref, dg_ref, dx_ref, dpp_ref, dgp_ref = outs

        @pl.when(i == 0)
        def _():
            loss_ref[...] = jnp.zeros_like(loss_ref)
            dg_ref[...] = jnp.zeros_like(dg_ref)

        gate = _sigmoid(accs[0])
        pp_v = ins[1][...]
        x3 = ins[0][...] + gate * pp_v
        xn, rstd = _rms(x3)
        gf_v = ins[3][...]
        err = xn * gf_v - ins[2][...]
        loss_ref[...] += (0.5 / D) * jnp.sum(jnp.sum(err * err, axis=1, keepdims=True), axis=0, keepdims=True)
        dy = err * (1.0 / D)
        dg_ref[...] += jnp.sum(dy * xn, axis=0, keepdims=True)
        dx3 = _rms_bwd(xn, rstd, dy * gf_v)
        dx_ref[...] = dx3
        dpp_ref[...] = (dx3 * gate).astype(BF16)
        dgp_ref[...] = (dx3 * pp_v * gate * (1.0 - gate)).astype(BF16)

    one = lambda tm, tn: pl.BlockSpec((1, 1), lambda i, j, kk: (0, 0))
    return _mm_ep([(hn2, w_gate)], "nn", "mm_ple_final", epilogue,
                  [(x2, _tile()), (pp, _tile()), (target, _tile()), (gf, _row())],
                  [(_sds(1, 1, F32), one), (_sds(1, D, F32), _row()), (_sds(t, D, F32), _tile()),
                   (_sds(t, D, BF16), _tile()), (_sds(t, D, BF16), _tile())], 512, D)


_WIN_SEGMENTS = ((0, 3072, C_QK), (3072, 8, C_IF), (3080, 1024, C_QSW), (4104, 256, C_KV), (4360, 256, C_KV + 256),
                 (4616, 1024, C_GA), (5640, 1024, C_GB))
_WIN_SHARD = N_IN // 4


def _win_pieces():
    out = []
    for src, width, dst in _WIN_SEGMENTS:
        while width:
            chip, col = divmod(src, _WIN_SHARD)
            n = min(width, _WIN_SHARD - col)
            out.append((chip, col, n, dst))
            src, dst, width = src + n, dst + n, width - n
    return out


def _win_pad(shards):
    rows = shards.shape[1]
    tr = _pick(rows, 256)

    def body(s_ref, o_ref):
        for chip, col, n, dst in _win_pieces():
            o_ref[:, dst:dst + n] = s_ref[chip, :, col:col + n]
        o_ref[:, C_IF + 8:NP] = jnp.zeros((tr, NP - C_IF - 8), shards.dtype)

    return pl.pallas_call(
        body, name="win_pad", grid=(rows // tr,), in_specs=[pl.BlockSpec((4, tr, _WIN_SHARD), lambda i: (0, i, 0))],
        out_specs=pl.BlockSpec((tr, NP), lambda i: (i, 0)), out_shape=jax.ShapeDtypeStruct((rows, NP), shards.dtype),
        compiler_params=_params(),
    )(shards)


def _win_unpad(wp):
    rows = wp.shape[0]
    tr = _pick(rows, 256)

    def body(p_ref, o_ref):
        for chip, col, n, dst in _win_pieces():
            o_ref[chip, :, col:col + n] = p_ref[:, dst:dst + n]

    return pl.pallas_call(
        body, name="win_unpad", grid=(rows // tr,), in_specs=[pl.BlockSpec((tr, NP), lambda i: (i, 0))],
        out_specs=pl.BlockSpec((4, tr, _WIN_SHARD), lambda i: (0, i, 0)),
        out_shape=jax.ShapeDtypeStruct((4, rows, _WIN_SHARD), wp.dtype), compiler_params=_params(),
    )(wp)


def _local_step(x, p, target, w, late_weights=None, early_grads=None, mid_grads=None, last_grad=None):
    t = x.shape[0]
    pb = p.astype(BF16)
    w = dict(w)

    h0, proj, gates = _proj_in(x, w["norm_mix_g"], w["w_in"])
    qk = _conv_silu_fwd(proj, w["conv_qk"])
    grow, sneg_row = _gates_fwd(gates[:, 0:8].T, w["b_if"].reshape(8, 1))
    gcol, sneg_col = grow.T, sneg_row.T
    hraw, ya, cs, st = _mlstm_fwd(qk, proj, grow, gcol, w["mlstm_norm_g"])
    yb, lse = _swa_fwd(proj, w["sinks"])
    if late_weights is not None:
        w.update(late_weights(yb))
    merged, za, zb = _branch_merge(ya, yb, w["w_branch_a"], w["w_branch_b"], proj)
    x1, hn1 = _resid_norm_mm(merged, w["w_out"], x, w["norm_mlp_g"], "mm_out_norm")
    act, u = _up_act(hn1, w["w_up"])
    x2, hn2 = _resid_norm_mm(act, w["w_down"], x1, w["norm_ple_g"], "mm_down_norm")
    pp = _mm(pb, w["w_ple_proj"], "nn", F32, "mm_ple_proj")
    loss, d_final_g, dx3, dpp, dgpre = _ple_final_mm(hn2, w["w_ple_gate"], x2, pp, target, w["final_norm_g"])

    g = {"final_norm_g": d_final_g}
    g["w_ple_proj"] = _mm(pb, dpp, "tn", F32, "mm_d_ple_proj", out_chunks=4)
    g["w_ple_gate"] = _mm(hn2, dgpre, "tn", F32, "mm_d_ple_gate")
    dx2, dx2b, g["norm_ple_g"] = _norm_bwd_mm(dgpre, w["w_ple_gate"], x2, w["norm_ple_g"], dx3, "mm_dhn2_norm")
    g["w_down"] = _mm(act, dx2b, "tn", F32, "mm_d_down")
    du = _da_du(dx2b, w["w_down"], u)
    g["w_up"] = _mm(hn1, du, "tn", F32, "mm_d_up", out_chunks=4)
    dx1, dx1b, g["norm_mlp_g"] = _norm_bwd_mm(du, w["w_up"], x1, w["norm_mlp_g"], dx2, "mm_dhn1_norm")
    g["w_out"] = _mm(merged, dx1b, "tn", F32, "mm_d_out")
    dza, dzb, dproj = _dmerged_bwd(dx1b, w["w_out"], proj, za, zb)
    g["w_branch_a"] = _mm(ya, dza, "tn", F32, "mm_d_branch_a")
    g["w_branch_b"] = _mm(yb, dzb, "tn", F32, "mm_d_branch_b")
    gain = w["mlstm_norm_g"] if early_grads is None else w["mlstm_norm_g"] + early_grads(g)
    dyb = _mm(dzb, w["w_branch_b"], "nt", F32, "mm_dyb")
    dhraw, dproj, g["mlstm_norm_g"] = _dya_bwd(dza, w["w_branch_a"], hraw, proj, gain, dproj)
    if mid_grads is not None:
        sneg_col = sneg_col + mid_grads(dhraw)
    dqk, dproj, dif, g["b_if"] = _mlstm_bwd(qk, proj, grow, gcol, sneg_col, cs, st, hraw, dhraw, dproj)
    dc, g["conv_qk"] = _conv_silu_bwd_a(proj, w["conv_qk"], dqk)
    dproj = _conv_silu_bwd_b(dc, w["conv_qk"], dproj)
    dproj, dkv_self, dkv_prev, g["sinks"] = _swa_bwd(proj, w["sinks"], lse, dyb, dproj)
    dproj = _kv_combine(dkv_self, dkv_prev, dif, dproj)
    g["w_in"] = _mm(h0, dproj, "tn", F32, "mm_d_in")
    gain = w["norm_mix_g"] if last_grad is None else w["norm_mix_g"] + last_grad(g)
    grad_x, _, g["norm_mix_g"] = _norm_bwd_mm(dproj, w["w_in"], x, gain, dx1, "mm_dh0_norm")
    return loss, grad_x, g


_W4 = ("w_branch_a", "w_branch_b", "w_out", "w_ple_gate")
_SHARDED_NAMES = ("w_in", "w_up", "w_down", "w_ple_proj", "conv_qk") + _W4
_SMALL_ROWS = 16
_CONV_ROW = 8


def _group(s):
    return [s["w_in"], jnp.concatenate([s[n] for n in _W4], axis=0), s["w_up"], s["w_down"], s["w_ple_proj"]]


def _ungroup(arrs):
    out = {"w_in": arrs[0], "w_up": arrs[2], "w_down": arrs[3], "w_ple_proj": arrs[4]}
    rows = arrs[1].shape[0] // len(_W4)
    for i, n in enumerate(_W4):
        out[n] = arrs[1][i * rows:(i + 1) * rows]
    return out


def _rows_tile(rows):
    return 256 if rows % 256 == 0 else rows


_SMALL = ("norm_mix_g", "mlstm_norm_g", "norm_mlp_g", "norm_ple_g", "final_norm_g")


def _pack_small(vals, extra=None, conv=None):
    rows = [vals[n].reshape(1, D) for n in _SMALL]
    tail = [vals["b_if"].reshape(1, 8), vals["sinks"].reshape(1, SWH)]
    used = 8 + SWH
    if extra is not None:
        tail.append(extra.reshape(1, 1))
        used += 1
    tail.append(jnp.zeros((1, D - used), F32))
    rows.append(jnp.concatenate(tail, axis=1))
    rows.append(jnp.zeros((_CONV_ROW - len(rows), D), F32))
    rows.append(jnp.zeros((CONV, D), F32) if conv is None else conv)
    rows.append(jnp.zeros((_SMALL_ROWS - _CONV_ROW - CONV, D), F32))
    return jnp.concatenate(rows, axis=0)


def _unpack_small(slab, shapes):
    out = {n: slab[i].reshape(shapes[n]) for i, n in enumerate(_SMALL)}
    out["b_if"] = slab[5, 0:8].reshape(shapes["b_if"])
    out["sinks"] = slab[5, 8:8 + SWH].reshape(shapes["sinks"])
    return out


_MESH = pl.DeviceIdType.MESH
_HBM = pl.BlockSpec(memory_space=pltpu.HBM)
_VMEM = pl.BlockSpec(memory_space=pltpu.VMEM)


def _place():
    x, y, c = lax.axis_index("x"), lax.axis_index("y"), lax.axis_index("c")
    return x, y, c, 2 * x + y


def _chip_peer(x, y, r):
    return (x ^ (r >> 1), y ^ (r & 1))


def _half(ref, which):
    h = ref.shape[-2] // 2
    return pl.ds(which * h, h)


def _allgather_weights(shards, conv):
    n = len(shards)

    def body(*refs):
        ins, conv_ref = refs[:n], refs[n]
        outs, conv_out = refs[n + 1:2 * n + 1], refs[2 * n + 1]
        send_a, recv_a, send_b, recv_b, send_c, recv_c, local_sems = refs[2 * n + 2:]
        x, y, c, j = _place()
        sibling = (x, y, 1 - c)
        local = [pltpu.make_async_copy(ins[k], outs[k].at[j], local_sems.at[k]) for k in range(n)]
        local.append(pltpu.make_async_copy(conv_ref, conv_out.at[j], local_sems.at[n]))
        for cp in local:
            cp.start()

        def copy_a(k, r, chip):
            rows = _half(ins[k], c)
            return pltpu.make_async_remote_copy(
                src_ref=ins[k].at[rows], dst_ref=outs[k].at[chip, rows], send_sem=send_a.at[3 * k + r - 1],
                recv_sem=recv_a.at[3 * k + r - 1], device_id=(*_chip_peer(x, y, r), c), device_id_type=_MESH)

        def copy_b(k, r, chip, which):
            rows = _half(ins[k], which)
            return pltpu.make_async_remote_copy(
                src_ref=outs[k].at[chip, rows], dst_ref=outs[k].at[chip, rows], send_sem=send_b.at[3 * k + r - 1],
                recv_sem=recv_b.at[3 * k + r - 1], device_id=sibling, device_id_type=_MESH)

        def copy_c(r, chip):
            return pltpu.make_async_remote_copy(
                src_ref=conv_ref, dst_ref=conv_out.at[chip], send_sem=send_c.at[r - 1],
                recv_sem=recv_c.at[r - 1], device_id=(*_chip_peer(x, y, r), c), device_id_type=_MESH)

        for k in range(n):
            for r in (1, 2, 3):
                copy_a(k, r, j).start()
        for r in (1, 2, 3):
            copy_c(r, j).start()
        for k in range(n):
            for r in (1, 2, 3):
                copy_a(k, r, j ^ r).wait_recv()
                copy_b(k, r, j ^ r, c).start()
        for k in range(n):
            for r in (1, 2, 3):
                copy_b(k, r, j ^ r, 1 - c).wait_recv()
        for r in (1, 2, 3):
            copy_c(r, j ^ r).wait_recv()
        for k in range(n):
            for r in (1, 2, 3):
                copy_a(k, r, j).wait_send()
                copy_b(k, r, j ^ r, c).wait_send()
        for r in (1, 2, 3):
            copy_c(r, j).wait_send()
        for cp in local:
            cp.wait()

    return pl.pallas_call(
        body, name="allgather_weights",
        out_shape=[jax.ShapeDtypeStruct((4,) + s.shape, s.dtype) for s in shards]
        + [jax.ShapeDtypeStruct((4,) + conv.shape, F32)],
        in_specs=[_HBM] * (n + 1), out_specs=[_HBM] * (n + 1),
        scratch_shapes=[pltpu.SemaphoreType.DMA((3 * n,))] * 4 + [pltpu.SemaphoreType.DMA((3,))] * 2
        + [pltpu.SemaphoreType.DMA((n + 1,))],
    )(*shards, conv)


_SEM = pl.BlockSpec(memory_space=pltpu.SEMAPHORE)
_DATAFLOW = pltpu.SideEffectType.DATAFLOW_SIDE_EFFECTING


def _late_peer_copy(src_ref, land_ref, send_sems, recv_sems, x, y, c, j, r, chip):
    return pltpu.make_async_remote_copy(
        src_ref=src_ref, dst_ref=land_ref.at[chip], send_sem=send_sems.at[r - 1], recv_sem=recv_sems.at[r - 1],
        device_id=(*_chip_peer(x, y, r), c), device_id_type=_MESH)


def _late_gather_start(rest):
    def body(rest_ref, land_ref, send_sems, recv_sems, rest_thru, land_thru, token):
        x, y, c, j = _place()
        for r in (1, 2, 3):
            _late_peer_copy(rest_ref, land_ref, send_sems, recv_sems, x, y, c, j, r, j).start()
        token[...] = jnp.zeros_like(token)

    j = 2 * lax.axis_index("x") + lax.axis_index("y")
    land = lax.dynamic_update_slice(lax.empty((4,) + rest.shape, rest.dtype), rest[None], (j, 0, 0))
    return pl.pallas_call(
        body, name="late_gather_start",
        out_shape=(pltpu.SemaphoreType.DMA((3,)), pltpu.SemaphoreType.DMA((3,)), pltpu.HBM(rest.shape, rest.dtype),
                   pltpu.HBM(land.shape, land.dtype), jax.ShapeDtypeStruct((8, 128), F32)),
        in_specs=(_HBM, _HBM), out_specs=(_SEM, _SEM, _HBM, _HBM, _VMEM), input_output_aliases={0: 2, 1: 3},
        compiler_params=pltpu.CompilerParams(has_side_effects=_DATAFLOW),
    )(pltpu.with_memory_space_constraint(rest, pltpu.HBM), pltpu.with_memory_space_constraint(land, pltpu.HBM))


def _late_gather_wait(send_sems, recv_sems, rest_thru, land_thru, after):
    def body(rest_ref, land_ref, send_sems, recv_sems, after_ref, rest_dead, got_ref):
        x, y, c, j = _place()
        for r in (1, 2, 3):
            cp = _late_peer_copy(rest_ref, land_ref, send_sems, recv_sems, x, y, c, j, r, j ^ r)
            cp.wait_send()
            cp.wait_recv()

    return pl.pallas_call(
        body, name="late_gather_wait",
        out_shape=(pltpu.HBM(rest_thru.shape, rest_thru.dtype), pltpu.HBM(land_thru.shape, land_thru.dtype)),
        in_specs=(_HBM, _HBM, _SEM, _SEM, _ANY), out_specs=(_HBM, _HBM), input_output_aliases={0: 0, 1: 1},
        compiler_params=pltpu.CompilerParams(has_side_effects=_DATAFLOW),
    )(rest_thru, land_thru, send_sems, recv_sems, after)[1]


def _pair_sum(g, theirs, j, c, name):
    _, h, cols = theirs.shape
    tr = _rows_tile(h)
    nb = h // tr

    def body(idx_ref, a_ref, b_ref, own_ref, ob_ref):
        s = a_ref[0] + b_ref[0]
        ob_ref[0] = s.astype(BF16)

        @pl.when(pl.program_id(1) == idx_ref[0])
        def _():
            own_ref[...] = s

    blk = pl.BlockSpec((1, tr, cols), lambda i, k, idx_ref: (k, i, 0))
    return pl.pallas_call(
        body, name=name,
        grid_spec=pltpu.PrefetchScalarGridSpec(
            num_scalar_prefetch=1, grid=(nb, 4),
            in_specs=[pl.BlockSpec((1, tr, cols), lambda i, k, idx_ref: (k, idx_ref[1] * nb + i, 0)), blk],
            out_specs=[pl.BlockSpec((tr, cols), lambda i, k, idx_ref: (i, 0)), blk]),
        out_shape=[jax.ShapeDtypeStruct((h, cols), F32), jax.ShapeDtypeStruct(theirs.shape, BF16)],
        compiler_params=_params(),
    )(jnp.stack([j, c]).astype(jnp.int32), g, theirs)


def _chip_copies(srcs, lands, send_sems, recv_sems):
    x, y, c, j = _place()
    return [pltpu.make_async_remote_copy(
        src_ref=srcs[k].at[j ^ r], dst_ref=lands[k].at[r - 1], send_sem=send_sems.at[3 * k + r - 1],
        recv_sem=recv_sems.at[3 * k + r - 1], device_id=(*_chip_peer(x, y, r), c), device_id_type=_MESH)
        for k in range(len(srcs)) for r in (1, 2, 3)]


def _pair_copies(srcs, lands, send_sems, recv_sems):
    x, y, c, _ = _place()
    return [pltpu.make_async_remote_copy(
        src_ref=srcs[k].at[:, _half(srcs[k], 1 - c)], dst_ref=lands[k], send_sem=send_sems.at[k],
        recv_sem=recv_sems.at[k], device_id=(x, y, 1 - c), device_id_type=_MESH) for k in range(len(srcs))]


def _split_start(name, srcs, lands, copies, n_sems):
    n = len(srcs)

    def body(*refs):
        for cp in copies(refs[:n], refs[n:2 * n], refs[2 * n], refs[2 * n + 1]):
            cp.start()
        refs[-1][...] = jnp.zeros_like(refs[-1])

    arrays = list(srcs) + list(lands)
    out = pl.pallas_call(
        body, name=name,
        out_shape=(pltpu.SemaphoreType.DMA((n_sems,)), pltpu.SemaphoreType.DMA((n_sems,)),
                   *[pltpu.HBM(a.shape, a.dtype) for a in arrays], jax.ShapeDtypeStruct((8, 128), F32)),
        in_specs=[_HBM] * (2 * n), out_specs=(_SEM, _SEM, *([_HBM] * (2 * n)), _VMEM),
        input_output_aliases={k: 2 + k for k in range(2 * n)},
        compiler_params=pltpu.CompilerParams(has_side_effects=_DATAFLOW),
    )(*[pltpu.with_memory_space_constraint(a, pltpu.HBM) for a in arrays])
    return out[0], out[1], list(out[2:2 + n]), list(out[2 + n:2 + 2 * n]), out[-1]


def _split_wait(name, send_sems, recv_sems, srcs_thru, lands_thru, after, copies):
    n = len(srcs_thru)

    def body(*refs):
        for cp in copies(refs[:n], refs[n:2 * n], refs[2 * n], refs[2 * n + 1]):
            cp.wait_send()
            cp.wait_recv()

    arrays = list(srcs_thru) + list(lands_thru)
    out = pl.pallas_call(
        body, name=name, out_shape=tuple(pltpu.HBM(a.shape, a.dtype) for a in arrays),
        in_specs=[_HBM] * (2 * n) + [_SEM, _SEM, _ANY], out_specs=tuple([_HBM] * (2 * n)),
        input_output_aliases={k: k for k in range(2 * n)},
        compiler_params=pltpu.CompilerParams(has_side_effects=_DATAFLOW),
    )(*arrays, send_sems, recv_sems, after)
    return list(out[:n]), list(out[n:])


def _chip_exchange_start(ss, tag):
    lands = [lax.empty((3,) + s.shape[1:], s.dtype) for s in ss]
    return _split_start("chip_exchange_start_" + tag, ss, lands, _chip_copies, 3 * len(ss))


def _chip_exchange_wait(send_sems, recv_sems, ss_thru, lands_thru, after, tag):
    return _split_wait("chip_exchange_wait_" + tag, send_sems, recv_sems, ss_thru, lands_thru, after, _chip_copies)[1]


def _pair_exchange_start(gs, tag):
    lands = [lax.empty((4, g.shape[1] // 2, g.shape[2]), g.dtype) for g in gs]
    return _split_start("pair_exchange_start_" + tag, gs, lands, _pair_copies, len(gs))


def _pair_exchange_wait(send_sems, recv_sems, gs_thru, lands_thru, after, tag):
    return _split_wait("pair_exchange_wait_" + tag, send_sems, recv_sems, gs_thru, lands_thru, after, _pair_copies)


def _reduce4(own, others, c, name):
    h, cols = own.shape
    tr = _rows_tile(h)
    nb = h // tr

    def body(c_ref, s_ref, a0, a1, a2, o_ref):
        o_ref[...] = ((s_ref[...] + a0[0].astype(F32)) + a1[0].astype(F32)) + a2[0].astype(F32)

    def other(r):
        return pl.BlockSpec((1, tr, cols), lambda i, c_ref: (r, i, 0))

    return pl.pallas_call(
        body, name=name,
        grid_spec=pltpu.PrefetchScalarGridSpec(
            num_scalar_prefetch=1, grid=(nb,),
            in_specs=[pl.BlockSpec((tr, cols), lambda i, c_ref: (i, 0)), other(0), other(1), other(2)],
            out_specs=pl.BlockSpec((tr, cols), lambda i, c_ref: (c_ref[0] * nb + i, 0))),
        out_shape=jax.ShapeDtypeStruct((2 * h, cols), F32), compiler_params=_params(),
    )(c.reshape(1).astype(jnp.int32), own, others, others, others)


def _sibling_share(fulls, name):
    n = len(fulls)

    def body(*refs):
        outs, send_sems, recv_sems = refs[n:2 * n], refs[2 * n], refs[2 * n + 1]
        x, y, c, _ = _place()
        cps = [pltpu.make_async_remote_copy(
            src_ref=outs[k].at[_half(outs[k], c)], dst_ref=outs[k].at[_half(outs[k], c)], send_sem=send_sems.at[k],
            recv_sem=recv_sems.at[k], device_id=(x, y, 1 - c), device_id_type=_MESH) for k in range(n)]
        for cp in cps:
            cp.start()
        for cp in cps:
            cp.wait()

    return pl.pallas_call(
        body, name=name, out_shape=[jax.ShapeDtypeStruct(f.shape, F32) for f in fulls],
        in_specs=[_HBM] * n, out_specs=[_HBM] * n, input_output_aliases={k: k for k in range(n)},
        scratch_shapes=[pltpu.SemaphoreType.DMA((n,))] * 2,
    )(*fulls)


def _adamw(w, g, m, v):
    m1 = ADAM_B1 * m + (1.0 - ADAM_B1) * g
    v1 = ADAM_B2 * v + (1.0 - ADAM_B2) * (g * g)
    m_hat = m1 / (1.0 - ADAM_B1 ** ADAM_STEP)
    v_hat = v1 / (1.0 - ADAM_B2 ** ADAM_STEP)
    delta = -ADAM_LR * (m_hat / (jnp.sqrt(v_hat) + ADAM_EPS) + ADAM_WD * w)
    return delta, m1, v1


def _adamw_call(w, g, m, v, name):
    rows, cols = w.shape

    def body(w_ref, g_ref, m_ref, v_ref, d_out, m_out, v_out):
        delta, m1, v1 = _adamw(w_ref[...], g_ref[...], m_ref[...], v_ref[...])
        d_out[...] = delta
        m_out[...] = m1
        v_out[...] = v1

    if rows % 8 == 0:
        tr = _rows_tile(rows)
        blk, grid = pl.BlockSpec((tr, cols), lambda i: (i, 0)), (rows // tr,)
    else:
        blk, grid = pl.BlockSpec((rows, 128), lambda i: (0, i)), (cols // 128,)
    return pl.pallas_call(
        body, name=name, grid=grid, in_specs=[blk] * 4, out_specs=[blk] * 3,
        out_shape=[jax.ShapeDtypeStruct((rows, cols), F32)] * 3, compiler_params=_params(),
    )(w, g, m, v)


def _small_allreduce(vals):
    def body(v_ref, out_ref, buf, send_sems, recv_sems):
        x, y, c, j = _place()
        me = 2 * j + c
        buf[0] = v_ref[...]

        def copy(r):
            return pltpu.make_async_remote_copy(
                src_ref=v_ref, dst_ref=buf.at[r], send_sem=send_sems.at[r - 1], recv_sem=recv_sems.at[r - 1],
                device_id=(x ^ (r >> 2), y ^ ((r >> 1) & 1), c ^ (r & 1)), device_id_type=_MESH)

        for r in range(1, 8):
            copy(r).start()
        for r in range(1, 8):
            copy(r).wait()
        acc = buf[me ^ 0]
        for d in range(1, 8):
            acc = acc + buf[me ^ d]
        out_ref[...] = acc

    return pl.pallas_call(
        body, name="small_allreduce", out_shape=jax.ShapeDtypeStruct((_SMALL_ROWS, D), F32),
        in_specs=[_VMEM], out_specs=_VMEM,
        scratch_shapes=[pltpu.VMEM((8, _SMALL_ROWS, D), F32), pltpu.SemaphoreType.DMA((7,)),
                        pltpu.SemaphoreType.DMA((7,))],
    )(vals)


_NAMES = ("norm_mix_g", "w_in", "conv_qk", "b_if", "mlstm_norm_g", "sinks", "w_branch_a", "w_branch_b", "w_out",
          "norm_mlp_g", "w_up", "w_down", "norm_ple_g", "w_ple_gate", "w_ple_proj", "final_norm_g")
_GROUP_NAMES = ("w_in", "w4", "w_up", "w_down", "w_ple_proj")


def _step(x, p, target, w, m, v):
    c = lax.axis_index("c")
    j = 2 * lax.axis_index("x") + lax.axis_index("y")

    def shards(d):
        return {n: d[n][0] for n in _SHARDED_NAMES}

    ws = shards(w)
    w_in_all, conv_all = _allgather_weights([ws["w_in"].astype(BF16)], ws["conv_qk"])
    rows_pp = PLE * (D // 4) // D
    rest = jnp.concatenate([ws[n] for n in _W4] + [ws["w_up"], ws["w_down"], ws["w_ple_proj"].reshape(rows_pp, D)],
                           axis=0)
    rest = (rest + 0.0 * conv_all[0, 0, 0]).astype(BF16)
    send_sems, recv_sems, rest_thru, land_thru, token = _late_gather_start(rest)
    full = {n: w[n] for n in ("mlstm_norm_g", "norm_mlp_g", "norm_ple_g", "b_if", "sinks")}
    full["norm_mix_g"] = w["norm_mix_g"] + token[0, 0]
    full["final_norm_g"] = w["final_norm_g"].reshape(1, D)
    full["w_in"] = _win_pad(w_in_all)
    full["conv_qk"] = jnp.swapaxes(conv_all, 0, 1).reshape(CONV, D)

    def late_weights(after):
        land = _late_gather_wait(send_sems, recv_sems, rest_thru, land_thru, after)
        out = {n: land[:, i * (D // 4):(i + 1) * (D // 4)].reshape(D, D) for i, n in enumerate(_W4)}
        out["w_up"] = land[:, D:2 * D]
        out["w_down"] = land[:, 2 * D:3 * D].reshape(DFF, D)
        out["w_ple_proj"] = jnp.swapaxes(land[:, 3 * D:3 * D + rows_pp].reshape(4, PLE, D // 4), 0, 1).reshape(PLE, D)
        return out

    early, last = {}, {}

    def pair_sums(by_dest, theirs, names):
        return [_pair_sum(a, b, j, c, "pair_sum_" + n) for a, b, n in zip(by_dest, theirs, names)]

    def early_grads(g):
        by_dest = [jnp.stack([g[n].reshape(4, D // 4, D) for n in _W4], axis=1).reshape(4, D, D),
                   g["w_up"], g["w_down"].reshape(4, DFF // 4, D), g["w_ple_proj"]]
        *early["pair"], token = _pair_exchange_start(by_dest, "early")
        return token[0, 0]

    def mid_grads(after):
        early["sums"] = pair_sums(*_pair_exchange_wait(*early["pair"], after, "early"), _GROUP_NAMES[1:])
        *early["flight"], token = _chip_exchange_start([s[1] for s in early["sums"]], "early")
        return token[0, 0]

    def last_grad(g):
        *last["pair"], token = _pair_exchange_start([_win_unpad(g["w_in"])], "w_in")
        return token[0, 0]

    loss, grad_x, g = _local_step(x[0], p[0, 0], target[0], full, late_weights, early_grads, mid_grads, last_grad)

    last["sums"] = pair_sums(*_pair_exchange_wait(*last["pair"], grad_x, "w_in"), _GROUP_NAMES[:1])
    *last["flight"], token = _chip_exchange_start([s[1] for s in last["sums"]], "w_in")

    def reduce_share(sums, others, names, tag):
        halves = [_reduce4(s[0], b, c, "reduce4_" + n) for s, b, n in zip(sums, others, names)]
        return list(_sibling_share(halves, "sibling_share_" + tag))

    ms, vs = shards(m), shards(v)
    grads = reduce_share(early["sums"], _chip_exchange_wait(*early["flight"], token, "early"), _GROUP_NAMES[1:], "early")
    upd = [_adamw_call(wa, ga, ma, va, "adamw_" + n)
           for wa, ga, ma, va, n in zip(_group(ws)[1:], grads, _group(ms)[1:], _group(vs)[1:], _GROUP_NAMES[1:])]
    small_g = _small_allreduce(_pack_small(g, extra=loss, conv=g["conv_qk"]))
    conv_g = lax.dynamic_slice(small_g[_CONV_ROW:_CONV_ROW + CONV], (0, j * (D // 4)), (CONV, D // 4))
    conv_upd = _adamw_call(ws["conv_qk"], conv_g, ms["conv_qk"], vs["conv_qk"], "adamw_conv")
    small_upd = _adamw_call(_pack_small(w), small_g, _pack_small(m), _pack_small(v), "adamw_small")

    done = sum(a[0][0:1, 0:1] for a in upd + [conv_upd, small_upd])
    others = _chip_exchange_wait(*last["flight"], done, "w_in")
    grads = reduce_share(last["sums"], others, _GROUP_NAMES[:1], "w_in") + list(grads)
    upd_in = _adamw_call(*[jnp.swapaxes(a, 0, 1) for a in (ws["w_in"], grads[0], ms["w_in"], vs["w_in"])], "adamw_w_in")
    upd = [[jnp.swapaxes(a, 0, 1) for a in upd_in]] + upd

    shapes = {n: w[n].shape for n in _NAMES}
    res = []
    for k in range(4):
        big = _ungroup(list(grads) if k == 0 else [u[k - 1] for u in upd])
        big["conv_qk"] = conv_g if k == 0 else conv_upd[k - 1]
        leaves = _unpack_small(small_g if k == 0 else small_upd[k - 1], shapes)
        leaves.update({n: a.reshape(shapes[n]) for n, a in big.items()})
        res.append(leaves)

    out = [small_g[5, 8 + SWH], grad_x[None]]
    for k in range(4):
        out += [res[k][n] for n in _NAMES]
    return tuple(out)


def kernel(x, p, norm_mix_g, w_in, conv_qk, b_if, mlstm_norm_g, sinks, w_branch_a, w_branch_b, w_out, norm_mlp_g, w_up, w_down, norm_ple_g, w_ple_gate, w_ple_proj, final_norm_g, loss_target, m_norm_mix_g, m_w_in, m_conv_qk, m_b_if, m_mlstm_norm_g, m_sinks, m_w_branch_a, m_w_branch_b, m_w_out, m_norm_mlp_g, m_w_up, m_w_down, m_norm_ple_g, m_w_ple_gate, m_w_ple_proj, m_final_norm_g, v_norm_mix_g, v_w_in, v_conv_qk, v_b_if, v_mlstm_norm_g, v_sinks, v_w_branch_a, v_w_branch_b, v_w_out, v_norm_mlp_g, v_w_up, v_w_down, v_norm_ple_g, v_w_ple_gate, v_w_ple_proj, v_final_norm_g):
    w = dict(zip(_NAMES, (norm_mix_g, w_in, conv_qk, b_if, mlstm_norm_g, sinks, w_branch_a, w_branch_b, w_out,
                          norm_mlp_g, w_up, w_down, norm_ple_g, w_ple_gate, w_ple_proj, final_norm_g)))
    m = dict(zip(_NAMES, (m_norm_mix_g, m_w_in, m_conv_qk, m_b_if, m_mlstm_norm_g, m_sinks, m_w_branch_a,
                          m_w_branch_b, m_w_out, m_norm_mlp_g, m_w_up, m_w_down, m_norm_ple_g, m_w_ple_gate,
                          m_w_ple_proj, m_final_norm_g)))
    v = dict(zip(_NAMES, (v_norm_mix_g, v_w_in, v_conv_qk, v_b_if, v_mlstm_norm_g, v_sinks, v_w_branch_a,
                          v_w_branch_b, v_w_out, v_norm_mlp_g, v_w_up, v_w_down, v_norm_ple_g, v_w_ple_gate,
                          v_w_ple_proj, v_final_norm_g)))
    return _step(x, p, loss_target, w, m, v)
```

```python
import jax
import jax.numpy as jnp
from jax import lax
from jax.experimental import pallas as pl
from jax.experimental.pallas import tpu as pltpu

F32 = jnp.float32
BF16 = jnp.bfloat16

D = 1024
PLE = 256
MLH = 4
DQK = 128
DV = 256
CONV = 4
CHUNK = 256
SWH = 16
SWKV = 4
SWG = SWH // SWKV
HD = 64
WIN = 128
DFF = 4096
EPS = 1e-6
N_IN = 6664
NP = 7168
C_QK, C_V, C_O, C_QSW, C_GA, C_GB, C_KV, C_IF = 0, 1024, 2048, 3072, 4096, 5120, 6144, 6656
IFW = NP - C_IF

ADAM_LR = 0.001
ADAM_B1 = 0.9
ADAM_B2 = 0.999
ADAM_EPS = 1e-08
ADAM_WD = 0.01
ADAM_STEP = 10

TOK_TILE = 512
V7X_VMEM_BYTES = 64 * 1024 * 1024
VMEM_LIMIT = V7X_VMEM_BYTES - 6 * 1024 * 1024


def _params(**kw):
    return pltpu.CompilerParams(vmem_limit_bytes=VMEM_LIMIT, **kw)


def _pick(n, cap):
    if n <= cap:
        return n
    t = cap - cap % 128
    while t > 128 and n % t:
        t -= 128
    assert n % t == 0, (n, cap)
    return t


def _dot(a, b, dims):
    return lax.dot_general(a, b, (dims, ((), ())), preferred_element_type=F32)


def _dot_nn(a, b):
    return _dot(a, b, ((1,), (0,)))


def _dot_nt(a, b):
    return _dot(a, b, ((1,), (1,)))


def _dot_tn(a, b):
    return _dot(a, b, ((0,), (0,)))


def _sigmoid(x):
    return 1.0 / (1.0 + jnp.exp(-x))


def _mm(a, b, mode, out_dtype, name, out_chunks=1):
    if mode == "nn":
        (m, k), (k2, n) = a.shape, b.shape
    elif mode == "nt":
        (m, k), (n, k2) = a.shape, b.shape
    else:
        (k, m), (k2, n) = a.shape, b.shape
    assert k == k2, (a.shape, b.shape, mode)
    tm, tn, tk = _pick(m, 1024), _pick(n // out_chunks, 1024), _pick(k, 2048)
    nk = k // tk
    if mode == "nn":
        a_spec = pl.BlockSpec((tm, tk), lambda i, j, kk: (i, kk))
        b_spec = pl.BlockSpec((tk, tn), lambda i, j, kk: (kk, j))
        dot = _dot_nn
    elif mode == "nt":
        a_spec = pl.BlockSpec((tm, tk), lambda i, j, kk: (i, kk))
        b_spec = pl.BlockSpec((tn, tk), lambda i, j, kk: (j, kk))
        dot = _dot_nt
    else:
        a_spec = pl.BlockSpec((tk, tm), lambda i, j, kk: (kk, i))
        b_spec = pl.BlockSpec((tk, tn), lambda i, j, kk: (kk, j))
        dot = _dot_tn
    if out_chunks > 1:
        npc = (n // out_chunks) // tn
        out_spec = pl.BlockSpec((None, tm, tn), lambda i, j, kk: (j // npc, i, j % npc))
        out_shape = jax.ShapeDtypeStruct((out_chunks, m, n // out_chunks), out_dtype)
    else:
        out_spec = pl.BlockSpec((tm, tn), lambda i, j, kk: (i, j))
        out_shape = jax.ShapeDtypeStruct((m, n), out_dtype)

    def body(a_ref, b_ref, o_ref, acc_ref):
        kk = pl.program_id(2)

        @pl.when(kk == 0)
        def _():
            acc_ref[...] = jnp.zeros_like(acc_ref)

        acc_ref[...] += dot(a_ref[...], b_ref[...])

        @pl.when(kk == nk - 1)
        def _():
            o_ref[...] = acc_ref[...].astype(out_dtype)

    return pl.pallas_call(
        body, name=name, grid=(m // tm, n // tn, nk),
        in_specs=[a_spec, b_spec], out_specs=out_spec, out_shape=out_shape,
        scratch_shapes=[pltpu.VMEM((tm, tn), F32)],
        compiler_params=_params(dimension_semantics=("parallel", "parallel", "arbitrary")),
    )(a, b)


def _tile(col0=0):
    return lambda tm, tn: pl.BlockSpec((tm, tn), lambda i, j, kk: (i, col0 // tn + j))


def _row():
    return lambda tm, tn: pl.BlockSpec((1, tn), lambda i, j, kk: (0, j))


def _mm_ep(pairs, mode, name, epilogue, ins, outs, tm, tn, aliases=None, row_split=1, init=None):
    a0, b0 = pairs[0]
    bch = b0.shape[0] if b0.ndim == 3 else 1
    m, k = a0.shape
    tm = _pick(m, tm)
    n = b0.shape[-1] * bch if mode == "nn" else b0.shape[-2]
    tk = _pick(k // bch if mode == "nt" else k, 2048)
    nk = k // tk
    a_spec = pl.BlockSpec((tm, tk), lambda i, j, kk: (i, kk))
    if mode == "nn":
        dot = _dot_nn
        if bch > 1:
            bpc = (n // bch) // tn
            b_spec = pl.BlockSpec((None, tk, tn), lambda i, j, kk: (j // bpc, kk, j % bpc))
        else:
            b_spec = pl.BlockSpec((tk, tn), lambda i, j, kk: (kk, j))
    else:
        dot = _dot_nt
        if bch > 1:
            bpc = (k // bch) // tk
            b_spec = pl.BlockSpec((None, tn, tk), lambda i, j, kk: (kk // bpc, j, kk % bpc))
        else:
            b_spec = pl.BlockSpec((tn, tk), lambda i, j, kk: (j, kk))
    npair, nin, nout = len(pairs), len(ins), len(outs)
    rows = tm // row_split
    assert init is None or row_split > 1

    def body_split(*refs):
        ab = refs[:2 * npair]
        in_refs = refs[2 * npair:2 * npair + nin]
        out_refs = refs[2 * npair + nin:2 * npair + nin + nout]
        accs = refs[2 * npair + nin + nout:]
        i, j, kk = pl.program_id(0), pl.program_id(1), pl.program_id(2)

        if init is not None:
            @pl.when((i == 0) & (kk == 0))
            def _():
                init(out_refs)

        @pl.when(kk < nk - 1)
        def _():
            for p in range(npair):
                prod = dot(ab[2 * p][...], ab[2 * p + 1][...])

                @pl.when(kk == 0)
                def _():
                    accs[p][...] = prod

                @pl.when(kk > 0)
                def _():
                    accs[p][...] += prod

        @pl.when(kk == nk - 1)
        def _():
            for r in range(row_split):
                rs = pl.ds(r * rows, rows)
                tot = []
                for p in range(npair):
                    prod = dot(ab[2 * p][rs, :], ab[2 * p + 1][...])
                    tot.append(prod if nk == 1 else accs[p][rs, :] + prod)

                def view(ref):
                    return ref.at[rs] if ref.shape[0] == tm else ref

                epilogue(tot, [view(x) for x in in_refs], [view(x) for x in out_refs], i * row_split + r, j)

    def body(*refs):
        ab = refs[:2 * npair]
        in_refs = refs[2 * npair:2 * npair + nin]
        out_refs = refs[2 * npair + nin:2 * npair + nin + nout]
        accs = refs[2 * npair + nin + nout:]
        i, j, kk = pl.program_id(0), pl.program_id(1), pl.program_id(2)
        for p in range(npair):
            prod = dot(ab[2 * p][...], ab[2 * p + 1][...])

            @pl.when(kk == 0)
            def _():
                accs[p][...] = prod

            @pl.when(kk > 0)
            def _():
                accs[p][...] += prod

        @pl.when(kk == nk - 1)
        def _():
            epilogue([acc[...] for acc in accs], in_refs, out_refs, i, j)

    operands = [x for pair in pairs for x in pair] + [a for a, _ in ins]
    io_alias = {2 * npair + i: o for i, o in (aliases or {}).items()}
    return pl.pallas_call(
        body if row_split == 1 else body_split, name=name, grid=(m // tm, n // tn, nk),
        in_specs=[a_spec, b_spec] * npair + [mk(tm, tn) for _, mk in ins],
        out_specs=[mk(tm, tn) for _, mk in outs], out_shape=[s for s, _ in outs],
        scratch_shapes=[pltpu.VMEM((tm, tn), F32)] * npair, input_output_aliases=io_alias,
        compiler_params=_params(dimension_semantics=("arbitrary", "arbitrary", "arbitrary")),
    )(*operands)


def _tok(w, j=0):
    return pl.BlockSpec((TOK_TILE, w), lambda i: (i, j))


def _rep(shape):
    return pl.BlockSpec(shape, lambda i: (0,) * len(shape))


def _rms(x):
    rstd = lax.rsqrt(jnp.mean(x * x, axis=-1, keepdims=True) + EPS)
    return x * rstd, rstd


def _rms_bwd(xn, rstd, dxn):
    return rstd * (dxn - xn * jnp.mean(dxn * xn, axis=-1, keepdims=True))


def _halo_prev(w, j=0, rows=8):
    r = TOK_TILE // rows
    return pl.BlockSpec((rows, w), lambda i: (jnp.maximum(i * r - 1, 0), j))


def _last8(halo_ref):
    return halo_ref[...].astype(F32)[halo_ref.shape[0] - 8:]


def _halo_next(w, nt, j=0):
    r = TOK_TILE // 8
    return pl.BlockSpec((8, w), lambda i: (jnp.minimum((i + 1) * r, nt * r - 1), j))


def _shift_down(x, halo, s):
    if s == 0:
        return x
    r = pltpu.roll(x, s, 0)
    hs = pltpu.roll(halo, s, 0)
    row = lax.broadcasted_iota(jnp.int32, hs.shape, 0)
    top = jnp.where(row < s, hs, r[0:8])
    return jnp.concatenate([top, r[8:]], axis=0)


def _shift_up(x, halo, s):
    if s == 0:
        return x
    n = x.shape[0]
    r = pltpu.roll(x, n - s, 0)
    hs = pltpu.roll(halo, 8 - s, 0)
    row = lax.broadcasted_iota(jnp.int32, hs.shape, 0)
    bot = jnp.where(row >= 8 - s, hs, r[n - 8:])
    return jnp.concatenate([r[:n - 8], bot], axis=0)


def _bf(x):
    return x.astype(BF16).astype(F32)


def _conv_taps(x, halo, w):
    x, halo, w = _bf(x), _bf(halo), _bf(w)
    acc = x * w[CONV - 1:CONV, :]
    for j in range(CONV - 1):
        acc = acc + _shift_down(x, halo, CONV - 1 - j) * w[j:j + 1, :]
    return acc


_Q_SCALE = DQK ** -0.5


def _qscale_row():
    lane = lax.broadcasted_iota(jnp.int32, (1, D), 1)
    return jnp.where(lane < MLH * DQK, _Q_SCALE, 1.0).astype(F32)


def _conv_silu_fwd(proj, conv_w):
    t = proj.shape[0]

    def body(x_ref, halo_ref, w_ref, o_ref):
        halo = jnp.where(pl.program_id(0) > 0, _last8(halo_ref), 0.0)
        c = _conv_taps(x_ref[...].astype(F32), halo, w_ref[...])
        o_ref[...] = (c * _sigmoid(c) * _qscale_row()).astype(BF16)

    return pl.pallas_call(
        body, name="conv_silu_fwd", grid=(t // TOK_TILE,),
        in_specs=[_tok(D, C_QK // D), _halo_prev(D, C_QK // D, 16), _rep((CONV, D))], out_specs=_tok(D),
        out_shape=jax.ShapeDtypeStruct((t, D), BF16), compiler_params=_params(),
    )(proj, proj, conv_w)


def _conv_silu_bwd_a(proj, conv_w, dqk):
    t = proj.shape[0]

    def body(x_ref, halo_ref, w_ref, d_ref, dc_ref, dw_ref):
        @pl.when(pl.program_id(0) == 0)
        def _():
            dw_ref[...] = jnp.zeros_like(dw_ref)

        halo = jnp.where(pl.program_id(0) > 0, _last8(halo_ref), 0.0)
        x = x_ref[...].astype(F32)
        c = _conv_taps(x, halo, w_ref[...])
        s = _sigmoid(c)
        dc = d_ref[...] * _qscale_row() * (s * (1.0 + c * (1.0 - s)))
        dc_ref[...] = dc
        dcb, xb, halo_b = _bf(dc), _bf(x), _bf(halo)
        for j in range(CONV):
            dw_ref[j:j + 1, :] += jnp.sum(dcb * _shift_down(xb, halo_b, CONV - 1 - j), axis=0, keepdims=True)

    return pl.pallas_call(
        body, name="conv_silu_bwd_a", grid=(t // TOK_TILE,),
        in_specs=[_tok(D, C_QK // D), _halo_prev(D, C_QK // D, 16), _rep((CONV, D)), _tok(D)],
        out_specs=[_tok(D), _rep((CONV, D))],
        out_shape=[jax.ShapeDtypeStruct((t, D), F32), jax.ShapeDtypeStruct((CONV, D), F32)],
        compiler_params=_params(),
    )(proj, proj, conv_w, dqk)


def _conv_silu_bwd_b(dc, conv_w, dproj):
    t = dc.shape[0]
    nt = t // TOK_TILE

    def body(dc_ref, halo_ref, w_ref, _, dx_ref):
        halo = _bf(jnp.where(pl.program_id(0) < nt - 1, halo_ref[...], 0.0))
        dcv = _bf(dc_ref[...])
        w = _bf(w_ref[...])
        acc = dcv * w[CONV - 1:CONV, :]
        for j in range(CONV - 1):
            acc = acc + _shift_up(dcv, halo, CONV - 1 - j) * w[j:j + 1, :]
        dx_ref[...] = acc.astype(BF16)

    return pl.pallas_call(
        body, name="conv_silu_bwd_b", grid=(nt,), in_specs=[_tok(D), _halo_next(D, nt), _rep((CONV, D)), _ANY],
        out_specs=_tok(D, C_QK // D), out_shape=jax.ShapeDtypeStruct((t, NP), BF16),
        input_output_aliases={3: 0}, compiler_params=_params(),
    )(dc, dc, conv_w, dproj)


def _gates_fwd(pre_rows, bias_col):
    t = pre_rows.shape[1]

    def body(p_ref, b_ref, g_ref, s_ref):
        z = p_ref[...] + b_ref[...]
        lf = jnp.minimum(z, 0.0) - jnp.log(1.0 + jnp.exp(-jnp.abs(z)))
        lane = lax.broadcasted_iota(jnp.int32, z.shape, 1) % CHUNK
        cum = lf
        s = 1
        while s < CHUNK:
            cum = cum + jnp.where(lane >= s, pltpu.roll(cum, s, 1), 0.0)
            s *= 2
        sub = lax.broadcasted_iota(jnp.int32, z.shape, 0)
        g_ref[...] = jnp.where(sub < MLH, z, cum)
        s_ref[...] = _sigmoid(-z)

    return pl.pallas_call(
        body, name="gates_fwd",
        out_shape=[jax.ShapeDtypeStruct((8, t), F32), jax.ShapeDtypeStruct((8, t), F32)],
        compiler_params=_params(),
    )(pre_rows, bias_col)


def _chunk_terms(grow, gcol, m0):
    heads = range(MLH)
    i_row = [grow[h:h + 1, :] for h in heads]
    b_row = [grow[MLH + h:MLH + h + 1, :] for h in heads]
    i_col = [gcol[:, h:h + 1] for h in heads]
    b_col = [gcol[:, MLH + h:MLH + h + 1] for h in heads]
    b_last = [b_row[h][:, CHUNK - 1:CHUNK] for h in heads]
    tt = lax.broadcasted_iota(jnp.int32, (CHUNK, CHUNK), 0)
    ss = lax.broadcasted_iota(jnp.int32, (CHUNK, CHUNK), 1)
    log_d = [jnp.where(tt >= ss, b_col[h] - b_row[h] + i_row[h], -jnp.inf) for h in heads]
    row_max = [jnp.max(log_d[h], axis=1, keepdims=True) for h in heads]
    last_max = [jnp.max(b_last[h] - b_row[h] + i_row[h], axis=1, keepdims=True) for h in heads]
    m_t = [jnp.maximum(b_col[h] + m0[h], row_max[h]) for h in heads]
    m1 = [jnp.maximum(b_last[h] + m0[h], last_max[h]) for h in heads]
    dm = [jnp.exp(log_d[h] - m_t[h]) for h in heads]
    wi = [jnp.exp(b_col[h] + m0[h] - m_t[h]) for h in heads]
    ws = [jnp.exp(b_last[h] - b_col[h] + i_col[h] - m1[h]) for h in heads]
    dec = [jnp.exp(b_last[h] + m0[h] - m1[h]) for h in heads]
    return [(dm[h], wi[h], m_t[h], ws[h], dec[h], m1[h]) for h in heads]


def _mlstm_fwd(qk, proj, grow, gcol, gain):
    t = qk.shape[0]
    nc = t // CHUNK

    def body(qk_ref, v_ref, o_ref, grow_ref, gcol_ref, g_ref, h_ref, y_ref, cs_ref, st_ref, c_scr, st_scr):
        @pl.when(pl.program_id(0) == 0)
        def _():
            c_scr[...] = jnp.zeros_like(c_scr)
            st_scr[...] = jnp.zeros_like(st_scr)

        grow_v, gcol_v = grow_ref[...], gcol_ref[...]
        heads = range(MLH)
        q = [qk_ref[:, h * DQK:(h + 1) * DQK] for h in heads]
        k = [qk_ref[:, MLH * DQK + h * DQK:MLH * DQK + (h + 1) * DQK] for h in heads]
        v = [v_ref[:, h * DV:(h + 1) * DV] for h in heads]
        c0 = [c_scr[h] for h in heads]
        n0 = [st_scr[h, 0:1, :] for h in heads]
        for h in heads:
            cs_ref[0, h] = c0[h]
            st_ref[0, h] = st_scr[h]
        terms = _chunk_terms(grow_v, gcol_v, [st_scr[h, 1:2, 0:1] for h in heads])
        a = [_dot_nt(q[h], k[h]) for h in heads]
        qc = [_dot_nt(q[h], c0[h].astype(BF16)) for h in heads]
        s = [a[h] * terms[h][0] for h in heads]
        sv = [_dot_nn(s[h].astype(BF16), v[h]) for h in heads]
        upd = [_dot_tn((terms[h][3] * v[h]).astype(BF16), k[h]) for h in heads]
        den = [terms[h][1] * jnp.sum(q[h].astype(F32) * n0[h], axis=1, keepdims=True)
               + jnp.sum(s[h], axis=1, keepdims=True) for h in heads]
        hv = [(terms[h][1] * qc[h] + sv[h]) / jnp.maximum(jnp.abs(den[h]), jnp.exp(-terms[h][2])) for h in heads]
        for h in heads:
            sl = slice(h * DV, (h + 1) * DV)
            h_ref[:, sl] = hv[h]
            xn, _ = _rms(hv[h])
            y_ref[:, sl] = (_sigmoid(o_ref[:, sl].astype(F32)) * xn * g_ref[:, sl]).astype(BF16)
        for h in heads:
            dec, m1 = terms[h][4], terms[h][5]
            c_scr[h] = dec * c0[h] + upd[h]
            st_scr[h, 0:1, :] = dec * n0[h] + jnp.sum(terms[h][3] * k[h].astype(F32), axis=0, keepdims=True)
            st_scr[h, 1:2, :] = jnp.broadcast_to(m1, (1, DQK))

    return pl.pallas_call(
        body, name="mlstm_fwd", grid=(nc,),
        in_specs=[pl.BlockSpec((CHUNK, D), lambda c: (c, 0)), pl.BlockSpec((CHUNK, D), lambda c: (c, C_V // D)),
                  pl.BlockSpec((CHUNK, D), lambda c: (c, C_O // D)),
                  pl.BlockSpec((8, CHUNK), lambda c: (0, c)), pl.BlockSpec((CHUNK, 8), lambda c: (c, 0)),
                  pl.BlockSpec((1, D), lambda c: (0, 0))],
        out_specs=[pl.BlockSpec((CHUNK, D), lambda c: (c, 0)), pl.BlockSpec((CHUNK, D), lambda c: (c, 0)),
                   pl.BlockSpec((1, MLH, DV, DQK), lambda c: (c, 0, 0, 0)),
                   pl.BlockSpec((1, MLH, 8, DQK), lambda c: (c, 0, 0, 0))],
        out_shape=[jax.ShapeDtypeStruct((t, D), F32), jax.ShapeDtypeStruct((t, D), BF16),
                   jax.ShapeDtypeStruct((nc, MLH, DV, DQK), F32), jax.ShapeDtypeStruct((nc, MLH, 8, DQK), F32)],
        scratch_shapes=[pltpu.VMEM((MLH, DV, DQK), F32), pltpu.VMEM((MLH, 8, DQK), F32)],
        compiler_params=_params(dimension_semantics=("arbitrary",)),
    )(qk, proj, proj, grow, gcol, gain)


def _mlstm_bwd(qk, proj, grow, gcol, sneg_col, cs, st, hraw, dh, dproj):
    t = qk.shape[0]
    nc = t // CHUNK

    def rev(c):
        return nc - 1 - c

    def nxt(c):
        return jnp.minimum(nc - c, nc - 1)

    def body(qk_ref, v_ref, grow_ref, gcol_ref, sneg_ref, cs_ref, st_ref, cs1_ref, st1_ref, h_ref, dh_ref, _,
             dqk_ref, dv_ref, dif_ref, dbif_ref, dc_scr, dn_scr):
        @pl.when(pl.program_id(0) == 0)
        def _():
            dc_scr[...] = jnp.zeros_like(dc_scr)
            dn_scr[...] = jnp.zeros_like(dn_scr)
            dbif_ref[...] = jnp.zeros_like(dbif_ref)

        grow_v, gcol_v, sneg = grow_ref[...], gcol_ref[...], sneg_ref[...]
        tt = lax.broadcasted_iota(jnp.int32, (CHUNK, CHUNK), 0)
        ss = lax.broadcasted_iota(jnp.int32, (CHUNK, CHUNK), 1)
        lane8 = lax.broadcasted_iota(jnp.int32, (CHUNK, 8), 1)
        heads = range(MLH)
        q = [qk_ref[:, h * DQK:(h + 1) * DQK] for h in heads]
        k = [qk_ref[:, MLH * DQK + h * DQK:MLH * DQK + (h + 1) * DQK] for h in heads]
        qf, kf = [a.astype(F32) for a in q], [a.astype(F32) for a in k]
        vb = [v_ref[:, h * DV:(h + 1) * DV].astype(BF16) for h in heads]
        c0 = [cs_ref[0, h] for h in heads]
        n0 = [st_ref[0, h, 0:1, :] for h in heads]
        dc1 = [dc_scr[h] for h in heads]
        dn1 = [dn_scr[h, 0:1, :] for h in heads]
        terms = _chunk_terms(grow_v, gcol_v, [st_ref[0, h, 1:2, 0:1] for h in heads])
        dm, wi, ws = [t[0] for t in terms], [t[1] for t in terms], [t[3] for t in terms]
        s = [_dot_nt(q[h], k[h]) * dm[h] for h in heads]
        den = [wi[h] * jnp.sum(qf[h] * n0[h], axis=1, keepdims=True) + jnp.sum(s[h], axis=1, keepdims=True)
               for h in heads]
        floor = [jnp.exp(-terms[h][2]) for h in heads]
        g = [jnp.maximum(jnp.abs(den[h]), floor[h]) for h in heads]
        dh_v = [dh_ref[:, h * DV:(h + 1) * DV] for h in heads]
        dnum = [dh_v[h] / g[h] for h in heads]
        dden = [-jnp.sum(dh_v[h] * h_ref[:, h * DV:(h + 1) * DV], axis=1, keepdims=True) / g[h] for h in heads]
        dden = [jnp.where(jnp.abs(den[h]) > floor[h], dden[h] * jnp.sign(den[h]), 0.0) for h in heads]
        dnum_b = [a.astype(BF16) for a in dnum]
        dc1_b = [a.astype(BF16) for a in dc1]
        da = [((_dot_nt(dnum_b[h], vb[h]) + dden[h]) * dm[h]).astype(BF16) for h in heads]
        dq_inter = [_dot_nn(dnum_b[h], c0[h].astype(BF16)) for h in heads]
        dk_inter = [_dot_nn(vb[h], dc1_b[h]) for h in heads]
        dv_inter = [_dot_nt(k[h], dc1_b[h]) for h in heads]
        dc_new = [_dot_tn((wi[h] * dnum[h]).astype(BF16), q[h]) for h in heads]
        dq = [_dot_nn(da[h], k[h]) + wi[h] * (dq_inter[h] + dden[h] * n0[h]) for h in heads]
        dk = [_dot_tn(da[h], q[h]) + ws[h] * (dk_inter[h] + dn1[h]) for h in heads]
        dv = [_dot_tn(s[h].astype(BF16), dnum_b[h]) + ws[h] * dv_inter[h] for h in heads]
        for h in heads:
            dqk_ref[:, h * DQK:(h + 1) * DQK] = dq[h]
            dqk_ref[:, MLH * DQK + h * DQK:MLH * DQK + (h + 1) * DQK] = dk[h]
            dv_ref[:, h * DV:(h + 1) * DV] = dv[h].astype(BF16)
        rk = [jnp.sum(kf[h] * dk[h], axis=1, keepdims=True) for h in heads]
        df = [jnp.sum(qf[h] * dq[h], axis=1, keepdims=True) - rk[h] for h in heads]
        df_row = [jnp.sum(jnp.where(tt == ss, df[h], 0.0), axis=0, keepdims=True) for h in heads]
        suffix = [jnp.sum(jnp.where(ss >= tt, df_row[h], 0.0), axis=1, keepdims=True) for h in heads]
        cross = [jnp.sum(jnp.sum(dc1[h] * cs1_ref[0, h], axis=0, keepdims=True), axis=1, keepdims=True)
                 + jnp.sum(dn1[h] * st1_ref[0, h, 0:1, :], axis=1, keepdims=True) for h in heads]
        dif = jnp.zeros((CHUNK, 8), F32)
        for h in heads:
            dpf = (suffix[h] + cross[h]) * sneg[:, MLH + h:MLH + h + 1]
            dif = dif + jnp.where(lane8 == h, rk[h], 0.0) + jnp.where(lane8 == MLH + h, dpf, 0.0)
            dc_scr[h] = terms[h][4] * dc1[h] + dc_new[h]
            dn_scr[h, 0:1, :] = terms[h][4] * dn1[h] + jnp.sum(wi[h] * dden[h] * qf[h], axis=0, keepdims=True)
        dif_ref[...] = dif
        dbif_ref[...] += jnp.sum(dif, axis=0, keepdims=True)

    return pl.pallas_call(
        body, name="mlstm_bwd", grid=(nc,),
        in_specs=[pl.BlockSpec((CHUNK, D), lambda c: (rev(c), 0)),
                  pl.BlockSpec((CHUNK, D), lambda c: (rev(c), C_V // D)),
                  pl.BlockSpec((8, CHUNK), lambda c: (0, rev(c))),
                  pl.BlockSpec((CHUNK, 8), lambda c: (rev(c), 0)),
                  pl.BlockSpec((CHUNK, 8), lambda c: (rev(c), 0)),
                  pl.BlockSpec((1, MLH, DV, DQK), lambda c: (rev(c), 0, 0, 0)),
                  pl.BlockSpec((1, MLH, 8, DQK), lambda c: (rev(c), 0, 0, 0)),
                  pl.BlockSpec((1, MLH, DV, DQK), lambda c: (nxt(c), 0, 0, 0)),
                  pl.BlockSpec((1, MLH, 8, DQK), lambda c: (nxt(c), 0, 0, 0)),
                  pl.BlockSpec((CHUNK, D), lambda c: (rev(c), 0)),
                  pl.BlockSpec((CHUNK, D), lambda c: (rev(c), 0)), _ANY],
        out_specs=[pl.BlockSpec((CHUNK, D), lambda c: (rev(c), 0)),
                   pl.BlockSpec((CHUNK, D), lambda c: (rev(c), C_V // D)),
                   pl.BlockSpec((CHUNK, 8), lambda c: (rev(c), 0)),
                   pl.BlockSpec((1, 8), lambda c: (0, 0))],
        out_shape=[jax.ShapeDtypeStruct((t, D), F32), jax.ShapeDtypeStruct((t, NP), BF16),
                   jax.ShapeDtypeStruct((t, 8), F32), jax.ShapeDtypeStruct((1, 8), F32)],
        scratch_shapes=[pltpu.VMEM((MLH, DV, DQK), F32), pltpu.VMEM((MLH, 8, DQK), F32)],
        input_output_aliases={11: 1}, compiler_params=_params(dimension_semantics=("arbitrary",)),
    )(qk, proj, grow, gcol, sneg_col, cs, st, cs, st, hraw, dh, dproj)


_ANY = pl.BlockSpec(memory_space=pl.ANY)


_SW_SCALE = HD ** -0.5
_KVB = C_KV // (2 * SWKV * HD)


def _swa_mask(n):
    ki = lax.broadcasted_iota(jnp.int32, (2 * WIN, SWG * WIN), 0)
    qi = lax.broadcasted_iota(jnp.int32, (2 * WIN, SWG * WIN), 1) % WIN
    return (ki > qi) & (ki <= qi + WIN) & ((n > 0) | (ki >= WIN))


def _group_rows(x_ref, hk):
    return jnp.concatenate([x_ref[:, (hk * SWG + g) * HD:(hk * SWG + g + 1) * HD] for g in range(SWG)], axis=0)


def _group_lanes(x_ref, hk):
    return jnp.concatenate([x_ref[hk * SWG + g:hk * SWG + g + 1, :] for g in range(SWG)], axis=1)


def _sink_lanes(sink_ref, hk):
    return jnp.concatenate([jnp.broadcast_to(sink_ref[:, hk * SWG + g:hk * SWG + g + 1], (1, WIN))
                            for g in range(SWG)], axis=1)


def _swa_fwd(proj, sinks):
    t = proj.shape[0]
    nb = t // WIN

    def body(q_ref, kvc_ref, kvp_ref, sink_ref, y_ref, lse_ref):
        valid = _swa_mask(pl.program_id(0))
        for hk in range(SWKV):
            ks = slice(hk * HD, (hk + 1) * HD)
            vs = slice(SWKV * HD + hk * HD, SWKV * HD + (hk + 1) * HD)
            kb = jnp.concatenate([kvp_ref[:, ks], kvc_ref[:, ks]], axis=0).astype(BF16)
            vb = jnp.concatenate([kvp_ref[:, vs], kvc_ref[:, vs]], axis=0).astype(BF16)
            q4 = _group_rows(q_ref, hk).astype(BF16)
            sink = _sink_lanes(sink_ref, hk)
            logits = jnp.where(valid, _dot_nt(kb, q4) * _SW_SCALE, -jnp.inf)
            m = jnp.maximum(jnp.max(logits, axis=0, keepdims=True), sink)
            p = jnp.exp(logits - m)
            denom = jnp.sum(p, axis=0, keepdims=True) + jnp.exp(sink - m)
            y4 = _dot_tn((p / denom).astype(BF16), vb).astype(BF16)
            lse4 = m + jnp.log(denom)
            for g in range(SWG):
                hq = hk * SWG + g
                y_ref[:, hq * HD:(hq + 1) * HD] = y4[g * WIN:(g + 1) * WIN]
                lse_ref[hq:hq + 1, :] = lse4[:, g * WIN:(g + 1) * WIN]

    return pl.pallas_call(
        body, name="swa_fwd", grid=(nb,),
        in_specs=[pl.BlockSpec((WIN, D), lambda n: (n, C_QSW // D)),
                  pl.BlockSpec((WIN, 512), lambda n: (n, _KVB)),
                  pl.BlockSpec((WIN, 512), lambda n: (jnp.maximum(n - 1, 0), _KVB)),
                  pl.BlockSpec((1, SWH), lambda n: (0, 0))],
        out_specs=[pl.BlockSpec((WIN, D), lambda n: (n, 0)), pl.BlockSpec((SWH, WIN), lambda n: (0, n))],
        out_shape=[jax.ShapeDtypeStruct((t, D), BF16), jax.ShapeDtypeStruct((SWH, t), F32)],
        compiler_params=_params(),
    )(proj, proj, proj, sinks)


def _swa_bwd(proj, sinks, lse, dyb, dproj):
    t = proj.shape[0]
    nb = t // WIN

    def body(q_ref, kvc_ref, kvp_ref, sink_ref, lse_ref, dy_ref, _, dq_ref, dself_ref, dprev_ref, ds_ref):
        @pl.when(pl.program_id(0) == 0)
        def _():
            ds_ref[...] = jnp.zeros_like(ds_ref)

        valid = _swa_mask(pl.program_id(0))
        kvh = range(SWKV)
        ks = [slice(hk * HD, (hk + 1) * HD) for hk in kvh]
        vs = [slice(SWKV * HD + hk * HD, SWKV * HD + (hk + 1) * HD) for hk in kvh]
        kb = [jnp.concatenate([kvp_ref[:, ks[hk]], kvc_ref[:, ks[hk]]], axis=0).astype(BF16) for hk in kvh]
        vb = [jnp.concatenate([kvp_ref[:, vs[hk]], kvc_ref[:, vs[hk]]], axis=0).astype(BF16) for hk in kvh]
        qb = [_group_rows(q_ref, hk).astype(BF16) for hk in kvh]
        dyb_ = [_group_rows(dy_ref, hk).astype(BF16) for hk in kvh]
        lse4 = [_group_lanes(lse_ref, hk) for hk in kvh]
        logits = [_dot_nt(kb[hk], qb[hk]) for hk in kvh]
        dpt = [_dot_nt(vb[hk], dyb_[hk]) for hk in kvh]
        p = [jnp.exp(jnp.where(valid, logits[hk] * _SW_SCALE, -jnp.inf) - lse4[hk]) for hk in kvh]
        delta = [jnp.sum(p[hk] * dpt[hk], axis=0, keepdims=True) for hk in kvh]
        dsm = [(p[hk] * (dpt[hk] - delta[hk])).astype(BF16) for hk in kvh]
        dq4 = [(_dot_tn(dsm[hk], kb[hk]) * _SW_SCALE).astype(BF16) for hk in kvh]
        dkb = [_dot_nn(dsm[hk], qb[hk]) * _SW_SCALE for hk in kvh]
        dvb = [_dot_nn(p[hk].astype(BF16), dyb_[hk]) for hk in kvh]
        for hk in kvh:
            dsink4 = jnp.exp(_sink_lanes(sink_ref, hk) - lse4[hk]) * delta[hk]
            for g in range(SWG):
                hq = hk * SWG + g
                dq_ref[:, hq * HD:(hq + 1) * HD] = dq4[hk][g * WIN:(g + 1) * WIN]
                ds_ref[:, hq:hq + 1] += -jnp.sum(dsink4[:, g * WIN:(g + 1) * WIN], axis=1, keepdims=True)
            dprev_ref[:, ks[hk]] = dkb[hk][:WIN]
            dself_ref[:, ks[hk]] = dkb[hk][WIN:]
            dprev_ref[:, vs[hk]] = dvb[hk][:WIN]
            dself_ref[:, vs[hk]] = dvb[hk][WIN:]

    return pl.pallas_call(
        body, name="swa_bwd", grid=(nb,),
        in_specs=[pl.BlockSpec((WIN, D), lambda n: (n, C_QSW // D)),
                  pl.BlockSpec((WIN, 512), lambda n: (n, _KVB)),
                  pl.BlockSpec((WIN, 512), lambda n: (jnp.maximum(n - 1, 0), _KVB)),
                  pl.BlockSpec((1, SWH), lambda n: (0, 0)),
                  pl.BlockSpec((SWH, WIN), lambda n: (0, n)),
                  pl.BlockSpec((WIN, D), lambda n: (n, 0)), _ANY],
        out_specs=[pl.BlockSpec((WIN, D), lambda n: (n, C_QSW // D)), pl.BlockSpec((WIN, 512), lambda n: (n, 0)),
                   pl.BlockSpec((WIN, 512), lambda n: (jnp.maximum(n - 1, 0), 0)),
                   pl.BlockSpec((1, SWH), lambda n: (0, 0))],
        out_shape=[jax.ShapeDtypeStruct((t, NP), BF16), jax.ShapeDtypeStruct((t, 512), F32),
                   jax.ShapeDtypeStruct((t, 512), F32), jax.ShapeDtypeStruct((1, SWH), F32)],
        input_output_aliases={6: 0}, compiler_params=_params(),
    )(proj, proj, proj, sinks, lse, dyb, dproj)


def _kv_combine(dself, dnext, dif, dproj):
    t = dself.shape[0]
    rows = _pick(t, 512)

    def body(a_ref, b_ref, dif_ref, _, o_ref):
        row = pl.program_id(0) * rows + lax.broadcasted_iota(jnp.int32, (rows, 1), 0)
        o_ref[:, 0:512] = (a_ref[...] + jnp.where(row < t - WIN, b_ref[...], 0.0)).astype(BF16)
        lane = lax.broadcasted_iota(jnp.int32, (rows, 128), 1)
        dif_v = dif_ref[...]
        first = jnp.zeros((rows, 128), F32)
        for col in range(8):
            first = first + jnp.where(lane == col, dif_v[:, col:col + 1], 0.0)
        o_ref[:, 512:640] = first.astype(BF16)
        o_ref[:, 640:512 + IFW] = jnp.zeros((rows, IFW - 128), BF16)

    return pl.pallas_call(
        body, name="kv_combine", grid=(t // rows,),
        in_specs=[pl.BlockSpec((rows, 512), lambda n: (n, 0)), pl.BlockSpec((rows, 512), lambda n: (n, 0)),
                  pl.BlockSpec((rows, 8), lambda n: (n, 0)), _ANY],
        out_specs=pl.BlockSpec((rows, 512 + IFW), lambda n: (n, C_KV // (512 + IFW))),
        out_shape=jax.ShapeDtypeStruct((t, NP), BF16), input_output_aliases={3: 0}, compiler_params=_params(),
    )(dself, dnext, dif, dproj)


def _sds(t, n, dtype):
    return jax.ShapeDtypeStruct((t, n), dtype)


def _proj_in(x, gain, w_in):
    t = x.shape[0]
    tm, tn = _pick(t, 1024), NP // 4

    def body(x_ref, g_ref, w_ref, h_ref, p_ref, gate_ref, h_scr):
        j = pl.program_id(1)

        @pl.when(j == 0)
        def _():
            xn, _ = _rms(x_ref[...])
            h = (xn * g_ref[...]).astype(BF16)
            h_scr[...] = h
            h_ref[...] = h

        acc = _dot_nn(h_scr[...], w_ref[...])
        p_ref[...] = acc.astype(BF16)

        @pl.when(j == C_IF // tn)
        def _():
            gate_ref[...] = acc[:, C_IF % tn:C_IF % tn + 128]

    return pl.pallas_call(
        body, name="mm_in", grid=(t // tm, NP // tn),
        in_specs=[pl.BlockSpec((tm, D), lambda i, j: (i, 0)), pl.BlockSpec((1, D), lambda i, j: (0, 0)),
                  pl.BlockSpec((D, tn), lambda i, j: (0, j))],
        out_specs=[pl.BlockSpec((tm, D), lambda i, j: (i, 0)), pl.BlockSpec((tm, tn), lambda i, j: (i, j)),
                   pl.BlockSpec((tm, 128), lambda i, j: (i, 0))],
        out_shape=[_sds(t, D, BF16), _sds(t, NP, BF16), _sds(t, 128, F32)],
        scratch_shapes=[pltpu.VMEM((tm, D), BF16)],
        compiler_params=_params(dimension_semantics=("arbitrary", "arbitrary")),
    )(x, gain, w_in)


def _branch_merge(ya, yb, wa, wb, proj):
    t = ya.shape[0]

    def epilogue(accs, ins, outs, i, j):
        za, zb = accs
        merged = _sigmoid(ins[0][...].astype(F32)) * za + _sigmoid(ins[1][...].astype(F32)) * zb
        outs[0][...] = merged.astype(BF16)
        outs[1][...] = za.astype(BF16)
        outs[2][...] = zb.astype(BF16)

    return _mm_ep([(ya, wa), (yb, wb)], "nn", "mm_branch_merge", epilogue, [(proj, _tile(C_GA)), (proj, _tile(C_GB))],
                  [(_sds(t, D, BF16), _tile())] * 3, 1024, 1024)


def _dmerged_bwd(dxb, w_out, proj, za, zb):
    t = dxb.shape[0]

    def epilogue(accs, ins, outs, i, j):
        dm = accs[0]
        sa, sb = _sigmoid(ins[0][...].astype(F32)), _sigmoid(ins[1][...].astype(F32))
        outs[0][...] = (dm * sa).astype(BF16)
        outs[1][...] = (dm * sb).astype(BF16)
        outs[2][:, 0:D] = (dm * ins[2][...].astype(F32) * sa * (1.0 - sa)).astype(BF16)
        outs[2][:, D:2 * D] = (dm * ins[3][...].astype(F32) * sb * (1.0 - sb)).astype(BF16)

    gate_cols = lambda tm, tn: pl.BlockSpec((tm, 2 * D), lambda i, j, kk: (i, C_GA // (2 * D)))
    return _mm_ep([(dxb, w_out)], "nt", "mm_dmerged_bwd", epilogue,
                  [(proj, _tile(C_GA)), (proj, _tile(C_GB)), (za, _tile()), (zb, _tile())],
                  [(_sds(t, D, BF16), _tile()), (_sds(t, D, BF16), _tile()), (_sds(t, NP, BF16), gate_cols)], 1024, D)


def _dya_bwd(dza, wa, hraw, proj, g, dproj):
    t = dza.shape[0]

    def epilogue(accs, ins, outs, i, j):
        h_ref, o_ref, g_ref, _ = ins
        dh_ref, do_ref, dg_ref = outs

        @pl.when(i == 0)
        def _():
            dg_ref[...] = jnp.zeros_like(dg_ref)

        dy = accs[0]
        so = _sigmoid(o_ref[...].astype(F32))
        for h in range(MLH):
            sl = slice(h * DV, (h + 1) * DV)
            xn, rstd = _rms(h_ref[:, sl])
            gs = g_ref[:, sl]
            do_ref[:, sl] = (dy[:, sl] * xn * gs * so[:, sl] * (1.0 - so[:, sl])).astype(BF16)
            dhn = dy[:, sl] * so[:, sl]
            dg_ref[:, sl] += jnp.sum(dhn * xn, axis=0, keepdims=True)
            dh_ref[:, sl] = _rms_bwd(xn, rstd, dhn * gs)

    return _mm_ep([(dza, wa)], "nt", "mm_dya_bwd", epilogue,
                  [(hraw, _tile()), (proj, _tile(C_O)), (g, _row()), (dproj, lambda tm, tn: _ANY)],
                  [(_sds(t, D, F32), _tile()), (_sds(t, NP, BF16), _tile(C_O)), (_sds(1, D, F32), _row())],
                  512, D, aliases={3: 1})


def _up_act(hn, w_up):
    t = hn.shape[0]

    def epilogue(accs, ins, outs, i, j):
        r = jnp.maximum(accs[0], 0.0)
        outs[0][...] = (r * r).astype(BF16)
        outs[1][...] = accs[0].astype(BF16)

    return _mm_ep([(hn, w_up)], "nn", "mm_up_act", epilogue, [],
                  [(_sds(t, DFF, BF16), _tile()), (_sds(t, DFF, BF16), _tile())], 1024, 1024)


def _da_du(dxb, w_down, u):
    t = dxb.shape[0]

    def epilogue(accs, ins, outs, i, j):
        outs[0][...] = (accs[0] * 2.0 * jnp.maximum(ins[0][...].astype(F32), 0.0)).astype(BF16)

    return _mm_ep([(dxb, w_down)], "nt", "mm_da_du", epilogue, [(u, _tile())], [(_sds(t, DFF, BF16), _tile())],
                  1024, 2048)[0]


def _resid_norm_mm(a, w, x, g, name):
    t = x.shape[0]

    def epilogue(accs, ins, outs, i, j):
        x1 = ins[0][...] + accs[0]
        outs[0][...] = x1
        xn, _ = _rms(x1)
        outs[1][...] = (xn * ins[1][...]).astype(BF16)

    return _mm_ep([(a, w)], "nn", name, epilogue, [(x, _tile()), (g, _row())],
                  [(_sds(t, D, F32), _tile()), (_sds(t, D, BF16), _tile())], 1024, D, row_split=4)


def _norm_bwd_mm(dy, w, x, g, dres, name):
    t = x.shape[0]

    def init(outs):
        outs[2][...] = jnp.zeros_like(outs[2])

    def epilogue(accs, ins, outs, i, j):
        dh = accs[0]
        xn, rstd = _rms(ins[0][...])
        outs[2][...] += jnp.sum(dh * xn, axis=0, keepdims=True)
        dx = ins[2][...] + _rms_bwd(xn, rstd, dh * ins[1][...])
        outs[0][...] = dx
        outs[1][...] = dx.astype(BF16)

    return _mm_ep([(dy, w)], "nt", name, epilogue, [(x, _tile()), (g, _row()), (dres, _tile())],
                  [(_sds(t, D, F32), _tile()), (_sds(t, D, BF16), _tile()), (_sds(1, D, F32), _row())], 1024, D,
                  row_split=4, init=init)


def _ple_final_mm(hn2, w_gate, x2, pp, target, gf):
    t = x2.shape[0]

    def epilogue(accs, ins, outs, i, j):
        loss_ref, dg_ref, dx_ref, dpp_ref, dgp_ref = outs

        @pl.when(i == 0)
        def _():
            loss_ref[...] = jnp.zeros_like(loss_ref)
            dg_ref[...] = jnp.zeros_like(dg_ref)

        gate = _sigmoid(accs[0])
        pp_v = ins[1][...]
        x3 = ins[0][...] + gate * pp_v
        xn, rstd = _rms(x3)
        gf_v = ins[3][...]
        err = xn * gf_v - ins[2][...]
        loss_ref[...] += (0.5 / D) * jnp.sum(jnp.sum(err * err, axis=1, keepdims=True), axis=0, keepdims=True)
        dy = err * (1.0 / D)
        dg_ref[...] += jnp.sum(dy * xn, axis=0, keepdims=True)
        dx3 = _rms_bwd(xn, rstd, dy * gf_v)
        dx_ref[...] = dx3
        dpp_ref[...] = (dx3 * gate).astype(BF16)
        dgp_ref[...] = (dx3 * pp_v * gate * (1.0 - gate)).astype(BF16)

    one = lambda tm, tn: pl.BlockSpec((1, 1), lambda i, j, kk: (0, 0))
    return _mm_ep([(hn2, w_gate)], "nn", "mm_ple_final", epilogue,
                  [(x2, _tile()), (pp, _tile()), (target, _tile()), (gf, _row())],
                  [(_sds(1, 1, F32), one), (_sds(1, D, F32), _row()), (_sds(t, D, F32), _tile()),
                   (_sds(t, D, BF16), _tile()), (_sds(t, D, BF16), _tile())], 512, D)


_WIN_SEGMENTS = ((0, 3072, C_QK), (3072, 8, C_IF), (3080, 1024, C_QSW), (4104, 256, C_KV), (4360, 256, C_KV + 256),
                 (4616, 1024, C_GA), (5640, 1024, C_GB))
_WIN_SHARD = N_IN // 4


def _win_pieces():
    out = []
    for src, width, dst in _WIN_SEGMENTS:
        while width:
            chip, col = divmod(src, _WIN_SHARD)
            n = min(width, _WIN_SHARD - col)
            out.append((chip, col, n, dst))
            src, dst, width = src + n, dst + n, width - n
    return out


def _win_pad(shards):
    rows = shards.shape[1]
    tr = _pick(rows, 256)

    def body(s_ref, o_ref):
        for chip, col, n, dst in _win_pieces():
            o_ref[:, dst:dst + n] = s_ref[chip, :, col:col + n]
        o_ref[:, C_IF + 8:NP] = jnp.zeros((tr, NP - C_IF - 8), shards.dtype)

    return pl.pallas_call(
        body, name="win_pad", grid=(rows // tr,), in_specs=[pl.BlockSpec((4, tr, _WIN_SHARD), lambda i: (0, i, 0))],
        out_specs=pl.BlockSpec((tr, NP), lambda i: (i, 0)), out_shape=jax.ShapeDtypeStruct((rows, NP), shards.dtype),
        compiler_params=_params(),
    )(shards)


def _win_unpad(wp):
    rows = wp.shape[0]
    tr = _pick(rows, 256)

    def body(p_ref, o_ref):
        for chip, col, n, dst in _win_pieces():
            o_ref[chip, :, col:col + n] = p_ref[:, dst:dst + n]

    return pl.pallas_call(
        body, name="win_unpad", grid=(rows // tr,), in_specs=[pl.BlockSpec((tr, NP), lambda i: (i, 0))],
        out_specs=pl.BlockSpec((4, tr, _WIN_SHARD), lambda i: (0, i, 0)),
        out_shape=jax.ShapeDtypeStruct((4, rows, _WIN_SHARD), wp.dtype), compiler_params=_params(),
    )(wp)


def _local_step(x, p, target, w, late_weights=None, early_grads=None, mid_grads=None, last_grad=None):
    t = x.shape[0]
    pb = p.astype(BF16)
    w = dict(w)

    h0, proj, gates = _proj_in(x, w["norm_mix_g"], w["w_in"])
    qk = _conv_silu_fwd(proj, w["conv_qk"])
    grow, sneg_row = _gates_fwd(gates[:, 0:8].T, w["b_if"].reshape(8, 1))
    gcol, sneg_col = grow.T, sneg_row.T
    hraw, ya, cs, st = _mlstm_fwd(qk, proj, grow, gcol, w["mlstm_norm_g"])
    yb, lse = _swa_fwd(proj, w["sinks"])
    if late_weights is not None:
        w.update(late_weights(yb))
    merged, za, zb = _branch_merge(ya, yb, w["w_branch_a"], w["w_branch_b"], proj)
    x1, hn1 = _resid_norm_mm(merged, w["w_out"], x, w["norm_mlp_g"], "mm_out_norm")
    act, u = _up_act(hn1, w["w_up"])
    x2, hn2 = _resid_norm_mm(act, w["w_down"], x1, w["norm_ple_g"], "mm_down_norm")
    pp = _mm(pb, w["w_ple_proj"], "nn", F32, "mm_ple_proj")
    loss, d_final_g, dx3, dpp, dgpre = _ple_final_mm(hn2, w["w_ple_gate"], x2, pp, target, w["final_norm_g"])

    g = {"final_norm_g": d_final_g}
    g["w_ple_proj"] = _mm(pb, dpp, "tn", F32, "mm_d_ple_proj", out_chunks=4)
    g["w_ple_gate"] = _mm(hn2, dgpre, "tn", F32, "mm_d_ple_gate")
    dx2, dx2b, g["norm_ple_g"] = _norm_bwd_mm(dgpre, w["w_ple_gate"], x2, w["norm_ple_g"], dx3, "mm_dhn2_norm")
    g["w_down"] = _mm(act, dx2b, "tn", F32, "mm_d_down")
    du = _da_du(dx2b, w["w_down"], u)
    g["w_up"] = _mm(hn1, du, "tn", F32, "mm_d_up", out_chunks=4)
    dx1, dx1b, g["norm_mlp_g"] = _norm_bwd_mm(du, w["w_up"], x1, w["norm_mlp_g"], dx2, "mm_dhn1_norm")
    g["w_out"] = _mm(merged, dx1b, "tn", F32, "mm_d_out")
    dza, dzb, dproj = _dmerged_bwd(dx1b, w["w_out"], proj, za, zb)
    g["w_branch_a"] = _mm(ya, dza, "tn", F32, "mm_d_branch_a")
    g["w_branch_b"] = _mm(yb, dzb, "tn", F32, "mm_d_branch_b")
    gain = w["mlstm_norm_g"] if early_grads is None else w["mlstm_norm_g"] + early_grads(g)
    dyb = _mm(dzb, w["w_branch_b"], "nt", F32, "mm_dyb")
    dhraw, dproj, g["mlstm_norm_g"] = _dya_bwd(dza, w["w_branch_a"], hraw, proj, gain, dproj)
    if mid_grads is not None:
        sneg_col = sneg_col + mid_grads(dhraw)
    dqk, dproj, dif, g["b_if"] = _mlstm_bwd(qk, proj, grow, gcol, sneg_col, cs, st, hraw, dhraw, dproj)
    dc, g["conv_qk"] = _conv_silu_bwd_a(proj, w["conv_qk"], dqk)
    dproj = _conv_silu_bwd_b(dc, w["conv_qk"], dproj)
    dproj, dkv_self, dkv_prev, g["sinks"] = _swa_bwd(proj, w["sinks"], lse, dyb, dproj)
    dproj = _kv_combine(dkv_self, dkv_prev, dif, dproj)
    g["w_in"] = _mm(h0, dproj, "tn", F32, "mm_d_in")
    gain = w["norm_mix_g"] if last_grad is None else w["norm_mix_g"] + last_grad(g)
    grad_x, _, g["norm_mix_g"] = _norm_bwd_mm(dproj, w["w_in"], x, gain, dx1, "mm_dh0_norm")
    return loss, grad_x, g


_W4 = ("w_branch_a", "w_branch_b", "w_out", "w_ple_gate")
_SHARDED_NAMES = ("w_in", "w_up", "w_down", "w_ple_proj", "conv_qk") + _W4
_SMALL_ROWS = 16
_CONV_ROW = 8


def _group(s):
    return [s["w_in"], jnp.concatenate([s[n] for n in _W4], axis=0), s["w_up"], s["w_down"], s["w_ple_proj"]]


def _ungroup(arrs):
    out = {"w_in": arrs[0], "w_up": arrs[2], "w_down": arrs[3], "w_ple_proj": arrs[4]}
    rows = arrs[1].shape[0] // len(_W4)
    for i, n in enumerate(_W4):
        out[n] = arrs[1][i * rows:(i + 1) * rows]
    return out


def _rows_tile(rows):
    return 256 if rows % 256 == 0 else rows


_SMALL = ("norm_mix_g", "mlstm_norm_g", "norm_mlp_g", "norm_ple_g", "final_norm_g")


def _pack_small(vals, extra=None, conv=None):
    rows = [vals[n].reshape(1, D) for n in _SMALL]
    tail = [vals["b_if"].reshape(1, 8), vals["sinks"].reshape(1, SWH)]
    used = 8 + SWH
    if extra is not None:
        tail.append(extra.reshape(1, 1))
        used += 1
    tail.append(jnp.zeros((1, D - used), F32))
    rows.append(jnp.concatenate(tail, axis=1))
    rows.append(jnp.zeros((_CONV_ROW - len(rows), D), F32))
    rows.append(jnp.zeros((CONV, D), F32) if conv is None else conv)
    rows.append(jnp.zeros((_SMALL_ROWS - _CONV_ROW - CONV, D), F32))
    return jnp.concatenate(rows, axis=0)


def _unpack_small(slab, shapes):
    out = {n: slab[i].reshape(shapes[n]) for i, n in enumerate(_SMALL)}
    out["b_if"] = slab[5, 0:8].reshape(shapes["b_if"])
    out["sinks"] = slab[5, 8:8 + SWH].reshape(shapes["sinks"])
    return out


_MESH = pl.DeviceIdType.MESH
_HBM = pl.BlockSpec(memory_space=pltpu.HBM)
_VMEM = pl.BlockSpec(memory_space=pltpu.VMEM)


def _place():
    x, y, c = lax.axis_index("x"), lax.axis_index("y"), lax.axis_index("c")
    return x, y, c, 2 * x + y


def _chip_peer(x, y, r):
    return (x ^ (r >> 1), y ^ (r & 1))


def _half(ref, which):
    h = ref.shape[-2] // 2
    return pl.ds(which * h, h)


def _allgather_weights(shards, conv):
    n = len(shards)

    def body(*refs):
        ins, conv_ref = refs[:n], refs[n]
        outs, conv_out = refs[n + 1:2 * n + 1], refs[2 * n + 1]
        send_a, recv_a, send_b, recv_b, send_c, recv_c, local_sems = refs[2 * n + 2:]
        x, y, c, j = _place()
        sibling = (x, y, 1 - c)
        local = [pltpu.make_async_copy(ins[k], outs[k].at[j], local_sems.at[k]) for k in range(n)]
        local.append(pltpu.make_async_copy(conv_ref, conv_out.at[j], local_sems.at[n]))
        for cp in local:
            cp.start()

        def copy_a(k, r, chip):
            rows = _half(ins[k], c)
            return pltpu.make_async_remote_copy(
                src_ref=ins[k].at[rows], dst_ref=outs[k].at[chip, rows], send_sem=send_a.at[3 * k + r - 1],
                recv_sem=recv_a.at[3 * k + r - 1], device_id=(*_chip_peer(x, y, r), c), device_id_type=_MESH)

        def copy_b(k, r, chip, which):
            rows = _half(ins[k], which)
            return pltpu.make_async_remote_copy(
                src_ref=outs[k].at[chip, rows], dst_ref=outs[k].at[chip, rows], send_sem=send_b.at[3 * k + r - 1],
                recv_sem=recv_b.at[3 * k + r - 1], device_id=sibling, device_id_type=_MESH)

        def copy_c(r, chip):
            return pltpu.make_async_remote_copy(
                src_ref=conv_ref, dst_ref=conv_out.at[chip], send_sem=send_c.at[r - 1],
                recv_sem=recv_c.at[r - 1], device_id=(*_chip_peer(x, y, r), c), device_id_type=_MESH)

        for k in range(n):
            for r in (1, 2, 3):
                copy_a(k, r, j).start()
        for r in (1, 2, 3):
            copy_c(r, j).start()
        for k in range(n):
            for r in (1, 2, 3):
                copy_a(k, r, j ^ r).wait_recv()
                copy_b(k, r, j ^ r, c).start()
        for k in range(n):
            for r in (1, 2, 3):
                copy_b(k, r, j ^ r, 1 - c).wait_recv()
        for r in (1, 2, 3):
            copy_c(r, j ^ r).wait_recv()
        for k in range(n):
            for r in (1, 2, 3):
                copy_a(k, r, j).wait_send()
                copy_b(k, r, j ^ r, c).wait_send()
        for r in (1, 2, 3):
            copy_c(r, j).wait_send()
        for cp in local:
            cp.wait()

    return pl.pallas_call(
        body, name="allgather_weights",
        out_shape=[jax.ShapeDtypeStruct((4,) + s.shape, s.dtype) for s in shards]
        + [jax.ShapeDtypeStruct((4,) + conv.shape, F32)],
        in_specs=[_HBM] * (n + 1), out_specs=[_HBM] * (n + 1),
        scratch_shapes=[pltpu.SemaphoreType.DMA((3 * n,))] * 4 + [pltpu.SemaphoreType.DMA((3,))] * 2
        + [pltpu.SemaphoreType.DMA((n + 1,))],
    )(*shards, conv)


_SEM = pl.BlockSpec(memory_space=pltpu.SEMAPHORE)
_DATAFLOW = pltpu.SideEffectType.DATAFLOW_SIDE_EFFECTING


def _late_peer_copy(src_ref, land_ref, send_sems, recv_sems, x, y, c, j, r, chip):
    return pltpu.make_async_remote_copy(
        src_ref=src_ref, dst_ref=land_ref.at[chip], send_sem=send_sems.at[r - 1], recv_sem=recv_sems.at[r - 1],
        device_id=(*_chip_peer(x, y, r), c), device_id_type=_MESH)


def _late_gather_start(rest):
    def body(rest_ref, land_ref, send_sems, recv_sems, rest_thru, land_thru, token):
        x, y, c, j = _place()
        for r in (1, 2, 3):
            _late_peer_copy(rest_ref, land_ref, send_sems, recv_sems, x, y, c, j, r, j).start()
        token[...] = jnp.zeros_like(token)

    j = 2 * lax.axis_index("x") + lax.axis_index("y")
    land = lax.dynamic_update_slice(lax.empty((4,) + rest.shape, rest.dtype), rest[None], (j, 0, 0))
    return pl.pallas_call(
        body, name="late_gather_start",
        out_shape=(pltpu.SemaphoreType.DMA((3,)), pltpu.SemaphoreType.DMA((3,)), pltpu.HBM(rest.shape, rest.dtype),
                   pltpu.HBM(land.shape, land.dtype), jax.ShapeDtypeStruct((8, 128), F32)),
        in_specs=(_HBM, _HBM), out_specs=(_SEM, _SEM, _HBM, _HBM, _VMEM), input_output_aliases={0: 2, 1: 3},
        compiler_params=pltpu.CompilerParams(has_side_effects=_DATAFLOW),
    )(pltpu.with_memory_space_constraint(rest, pltpu.HBM), pltpu.with_memory_space_constraint(land, pltpu.HBM))


def _late_gather_wait(send_sems, recv_sems, rest_thru, land_thru, after):
    def body(rest_ref, land_ref, send_sems, recv_sems, after_ref, rest_dead, got_ref):
        x, y, c, j = _place()
        for r in (1, 2, 3):
            cp = _late_peer_copy(rest_ref, land_ref, send_sems, recv_sems, x, y, c, j, r, j ^ r)
            cp.wait_send()
            cp.wait_recv()

    return pl.pallas_call(
        body, name="late_gather_wait",
        out_shape=(pltpu.HBM(rest_thru.shape, rest_thru.dtype), pltpu.HBM(land_thru.shape, land_thru.dtype)),
        in_specs=(_HBM, _HBM, _SEM, _SEM, _ANY), out_specs=(_HBM, _HBM), input_output_aliases={0: 0, 1: 1},
        compiler_params=pltpu.CompilerParams(has_side_effects=_DATAFLOW),
    )(rest_thru, land_thru, send_sems, recv_sems, after)[1]


def _pair_sum(g, theirs, j, c, name):
    _, h, cols = theirs.shape
    tr = _rows_tile(h)
    nb = h // tr

    def body(idx_ref, a_ref, b_ref, own_ref, ob_ref):
        s = a_ref[0] + b_ref[0]
        ob_ref[0] = s.astype(BF16)

        @pl.when(pl.program_id(1) == idx_ref[0])
        def _():
            own_ref[...] = s

    blk = pl.BlockSpec((1, tr, cols), lambda i, k, idx_ref: (k, i, 0))
    return pl.pallas_call(
        body, name=name,
        grid_spec=pltpu.PrefetchScalarGridSpec(
            num_scalar_prefetch=1, grid=(nb, 4),
            in_specs=[pl.BlockSpec((1, tr, cols), lambda i, k, idx_ref: (k, idx_ref[1] * nb + i, 0)), blk],
            out_specs=[pl.BlockSpec((tr, cols), lambda i, k, idx_ref: (i, 0)), blk]),
        out_shape=[jax.ShapeDtypeStruct((h, cols), F32), jax.ShapeDtypeStruct(theirs.shape, BF16)],
        compiler_params=_params(),
    )(jnp.stack([j, c]).astype(jnp.int32), g, theirs)


def _chip_copies(srcs, lands, send_sems, recv_sems):
    x, y, c, j = _place()
    return [pltpu.make_async_remote_copy(
        src_ref=srcs[k].at[j ^ r], dst_ref=lands[k].at[r - 1], send_sem=send_sems.at[3 * k + r - 1],
        recv_sem=recv_sems.at[3 * k + r - 1], device_id=(*_chip_peer(x, y, r), c), device_id_type=_MESH)
        for k in range(len(srcs)) for r in (1, 2, 3)]


def _pair_copies(srcs, lands, send_sems, recv_sems):
    x, y, c, _ = _place()
    return [pltpu.make_async_remote_copy(
        src_ref=srcs[k].at[:, _half(srcs[k], 1 - c)], dst_ref=lands[k], send_sem=send_sems.at[k],
        recv_sem=recv_sems.at[k], device_id=(x, y, 1 - c), device_id_type=_MESH) for k in range(len(srcs))]


def _split_start(name, srcs, lands, copies, n_sems):
    n = len(srcs)

    def body(*refs):
        for cp in copies(refs[:n], refs[n:2 * n], refs[2 * n], refs[2 * n + 1]):
            cp.start()
        refs[-1][...] = jnp.zeros_like(refs[-1])

    arrays = list(srcs) + list(lands)
    out = pl.pallas_call(
        body, name=name,
        out_shape=(pltpu.SemaphoreType.DMA((n_sems,)), pltpu.SemaphoreType.DMA((n_sems,)),
                   *[pltpu.HBM(a.shape, a.dtype) for a in arrays], jax.ShapeDtypeStruct((8, 128), F32)),
        in_specs=[_HBM] * (2 * n), out_specs=(_SEM, _SEM, *([_HBM] * (2 * n)), _VMEM),
        input_output_aliases={k: 2 + k for k in range(2 * n)},
        compiler_params=pltpu.CompilerParams(has_side_effects=_DATAFLOW),
    )(*[pltpu.with_memory_space_constraint(a, pltpu.HBM) for a in arrays])
    return out[0], out[1], list(out[2:2 + n]), list(out[2 + n:2 + 2 * n]), out[-1]


def _split_wait(name, send_sems, recv_sems, srcs_thru, lands_thru, after, copies):
    n = len(srcs_thru)

    def body(*refs):
        for cp in copies(refs[:n], refs[n:2 * n], refs[2 * n], refs[2 * n + 1]):
            cp.wait_send()
            cp.wait_recv()

    arrays = list(srcs_thru) + list(lands_thru)
    out = pl.pallas_call(
        body, name=name, out_shape=tuple(pltpu.HBM(a.shape, a.dtype) for a in arrays),
        in_specs=[_HBM] * (2 * n) + [_SEM, _SEM, _ANY], out_specs=tuple([_HBM] * (2 * n)),
        input_output_aliases={k: k for k in range(2 * n)},
        compiler_params=pltpu.CompilerParams(has_side_effects=_DATAFLOW),
    )(*arrays, send_sems, recv_sems, after)
    return list(out[:n]), list(out[n:])


def _chip_exchange_start(ss, tag):
    lands = [lax.empty((3,) + s.shape[1:], s.dtype) for s in ss]
    return _split_start("chip_exchange_start_" + tag, ss, lands, _chip_copies, 3 * len(ss))


def _chip_exchange_wait(send_sems, recv_sems, ss_thru, lands_thru, after, tag):
    return _split_wait("chip_exchange_wait_" + tag, send_sems, recv_sems, ss_thru, lands_thru, after, _chip_copies)[1]


def _pair_exchange_start(gs, tag):
    lands = [lax.empty((4, g.shape[1] // 2, g.shape[2]), g.dtype) for g in gs]
    return _split_start("pair_exchange_start_" + tag, gs, lands, _pair_copies, len(gs))


def _pair_exchange_wait(send_sems, recv_sems, gs_thru, lands_thru, after, tag):
    return _split_wait("pair_exchange_wait_" + tag, send_sems, recv_sems, gs_thru, lands_thru, after, _pair_copies)


def _reduce4(own, others, c, name):
    h, cols = own.shape
    tr = _rows_tile(h)
    nb = h // tr

    def body(c_ref, s_ref, a0, a1, a2, o_ref):
        o_ref[...] = ((s_ref[...] + a0[0].astype(F32)) + a1[0].astype(F32)) + a2[0].astype(F32)

    def other(r):
        return pl.BlockSpec((1, tr, cols), lambda i, c_ref: (r, i, 0))

    return pl.pallas_call(
        body, name=name,
        grid_spec=pltpu.PrefetchScalarGridSpec(
            num_scalar_prefetch=1, grid=(nb,),
            in_specs=[pl.BlockSpec((tr, cols), lambda i, c_ref: (i, 0)), other(0), other(1), other(2)],
            out_specs=pl.BlockSpec((tr, cols), lambda i, c_ref: (c_ref[0] * nb + i, 0))),
        out_shape=jax.ShapeDtypeStruct((2 * h, cols), F32), compiler_params=_params(),
    )(c.reshape(1).astype(jnp.int32), own, others, others, others)


def _sibling_share(fulls, name):
    n = len(fulls)

    def body(*refs):
        outs, send_sems, recv_sems = refs[n:2 * n], refs[2 * n], refs[2 * n + 1]
        x, y, c, _ = _place()
        cps = [pltpu.make_async_remote_copy(
            src_ref=outs[k].at[_half(outs[k], c)], dst_ref=outs[k].at[_half(outs[k], c)], send_sem=send_sems.at[k],
            recv_sem=recv_sems.at[k], device_id=(x, y, 1 - c), device_id_type=_MESH) for k in range(n)]
        for cp in cps:
            cp.start()
        for cp in cps:
            cp.wait()

    return pl.pallas_call(
        body, name=name, out_shape=[jax.ShapeDtypeStruct(f.shape, F32) for f in fulls],
        in_specs=[_HBM] * n, out_specs=[_HBM] * n, input_output_aliases={k: k for k in range(n)},
        scratch_shapes=[pltpu.SemaphoreType.DMA((n,))] * 2,
    )(*fulls)


def _adamw(w, g, m, v):
    m1 = ADAM_B1 * m + (1.0 - ADAM_B1) * g
    v1 = ADAM_B2 * v + (1.0 - ADAM_B2) * (g * g)
    m_hat = m1 / (1.0 - ADAM_B1 ** ADAM_STEP)
    v_hat = v1 / (1.0 - ADAM_B2 ** ADAM_STEP)
    delta = -ADAM_LR * (m_hat / (jnp.sqrt(v_hat) + ADAM_EPS) + ADAM_WD * w)
    return delta, m1, v1


def _adamw_call(w, g, m, v, name):
    rows, cols = w.shape

    def body(w_ref, g_ref, m_ref, v_ref, d_out, m_out, v_out):
        delta, m1, v1 = _adamw(w_ref[...], g_ref[...], m_ref[...], v_ref[...])
        d_out[...] = delta
        m_out[...] = m1
        v_out[...] = v1

    if rows % 8 == 0:
        tr = _rows_tile(rows)
        blk, grid = pl.BlockSpec((tr, cols), lambda i: (i, 0)), (rows // tr,)
    else:
        blk, grid = pl.BlockSpec((rows, 128), lambda i: (0, i)), (cols // 128,)
    return pl.pallas_call(
        body, name=name, grid=grid, in_specs=[blk] * 4, out_specs=[blk] * 3,
        out_shape=[jax.ShapeDtypeStruct((rows, cols), F32)] * 3, compiler_params=_params(),
    )(w, g, m, v)


def _small_allreduce(vals):
    def body(v_ref, out_ref, buf, send_sems, recv_sems):
        x, y, c, j = _place()
        me = 2 * j + c
        buf[0] = v_ref[...]

        def copy(r):
            return pltpu.make_async_remote_copy(
                src_ref=v_ref, dst_ref=buf.at[r], send_sem=send_sems.at[r - 1], recv_sem=recv_sems.at[r - 1],
                device_id=(x ^ (r >> 2), y ^ ((r >> 1) & 1), c ^ (r & 1)), device_id_type=_MESH)

        for r in range(1, 8):
            copy(r).start()
        for r in range(1, 8):
            copy(r).wait()
        acc = buf[me ^ 0]
        for d in range(1, 8):
            acc = acc + buf[me ^ d]
        out_ref[...] = acc

    return pl.pallas_call(
        body, name="small_allreduce", out_shape=jax.ShapeDtypeStruct((_SMALL_ROWS, D), F32),
        in_specs=[_VMEM], out_specs=_VMEM,
        scratch_shapes=[pltpu.VMEM((8, _SMALL_ROWS, D), F32), pltpu.SemaphoreType.DMA((7,)),
                        pltpu.SemaphoreType.DMA((7,))],
    )(vals)


_NAMES = ("norm_mix_g", "w_in", "conv_qk", "b_if", "mlstm_norm_g", "sinks", "w_branch_a", "w_branch_b", "w_out",
          "norm_mlp_g", "w_up", "w_down", "norm_ple_g", "w_ple_gate", "w_ple_proj", "final_norm_g")
_GROUP_NAMES = ("w_in", "w4", "w_up", "w_down", "w_ple_proj")


def _step(x, p, target, w, m, v):
    c = lax.axis_index("c")
    j = 2 * lax.axis_index("x") + lax.axis_index("y")

    def shards(d):
        return {n: d[n][0] for n in _SHARDED_NAMES}

    ws = shards(w)
    w_in_all, conv_all = _allgather_weights([ws["w_in"].astype(BF16)], ws["conv_qk"])
    rows_pp = PLE * (D // 4) // D
    rest = jnp.concatenate([ws[n] for n in _W4] + [ws["w_up"], ws["w_down"], ws["w_ple_proj"].reshape(rows_pp, D)],
                           axis=0)
    rest = (rest + 0.0 * conv_all[0, 0, 0]).astype(BF16)
    send_sems, recv_sems, rest_thru, land_thru, token = _late_gather_start(rest)
    full = {n: w[n] for n in ("mlstm_norm_g", "norm_mlp_g", "norm_ple_g", "b_if", "sinks")}
    full["norm_mix_g"] = w["norm_mix_g"] + token[0, 0]
    full["final_norm_g"] = w["final_norm_g"].reshape(1, D)
    full["w_in"] = _win_pad(w_in_all)
    full["conv_qk"] = jnp.swapaxes(conv_all, 0, 1).reshape(CONV, D)

    def late_weights(after):
        land = _late_gather_wait(send_sems, recv_sems, rest_thru, land_thru, after)
        out = {n: land[:, i * (D // 4):(i + 1) * (D // 4)].reshape(D, D) for i, n in enumerate(_W4)}
        out["w_up"] = land[:, D:2 * D]
        out["w_down"] = land[:, 2 * D:3 * D].reshape(DFF, D)
        out["w_ple_proj"] = jnp.swapaxes(land[:, 3 * D:3 * D + rows_pp].reshape(4, PLE, D // 4), 0, 1).reshape(PLE, D)
        return out

    early, last = {}, {}

    def pair_sums(by_dest, theirs, names):
        return [_pair_sum(a, b, j, c, "pair_sum_" + n) for a, b, n in zip(by_dest, theirs, names)]

    def early_grads(g):
        by_dest = [jnp.stack([g[n].reshape(4, D // 4, D) for n in _W4], axis=1).reshape(4, D, D),
                   g["w_up"], g["w_down"].reshape(4, DFF // 4, D), g["w_ple_proj"]]
        *early["pair"], token = _pair_exchange_start(by_dest, "early")
        return token[0, 0]

    def mid_grads(after):
        early["sums"] = pair_sums(*_pair_exchange_wait(*early["pair"], after, "early"), _GROUP_NAMES[1:])
        *early["flight"], token = _chip_exchange_start([s[1] for s in early["sums"]], "early")
        return token[0, 0]

    def last_grad(g):
        *last["pair"], token = _pair_exchange_start([_win_unpad(g["w_in"])], "w_in")
        return token[0, 0]

    loss, grad_x, g = _local_step(x[0], p[0, 0], target[0], full, late_weights, early_grads, mid_grads, last_grad)

    last["sums"] = pair_sums(*_pair_exchange_wait(*last["pair"], grad_x, "w_in"), _GROUP_NAMES[:1])
    *last["flight"], token = _chip_exchange_start([s[1] for s in last["sums"]], "w_in")

    def reduce_share(sums, others, names, tag):
        halves = [_reduce4(s[0], b, c, "reduce4_" + n) for s, b, n in zip(sums, others, names)]
        return list(_sibling_share(halves, "sibling_share_" + tag))

    ms, vs = shards(m), shards(v)
    grads = reduce_share(early["sums"], _chip_exchange_wait(*early["flight"], token, "early"), _GROUP_NAMES[1:], "early")
    upd = [_adamw_call(wa, ga, ma, va, "adamw_" + n)
           for wa, ga, ma, va, n in zip(_group(ws)[1:], grads, _group(ms)[1:], _group(vs)[1:], _GROUP_NAMES[1:])]
    small_g = _small_allreduce(_pack_small(g, extra=loss, conv=g["conv_qk"]))
    conv_g = lax.dynamic_slice(small_g[_CONV_ROW:_CONV_ROW + CONV], (0, j * (D // 4)), (CONV, D // 4))
    conv_upd = _adamw_call(ws["conv_qk"], conv_g, ms["conv_qk"], vs["conv_qk"], "adamw_conv")
    small_upd = _adamw_call(_pack_small(w), small_g, _pack_small(m), _pack_small(v), "adamw_small")

    done = sum(a[0][0:1, 0:1] for a in upd + [conv_upd, small_upd])
    others = _chip_exchange_wait(*last["flight"], done, "w_in")
    grads = reduce_share(last["sums"], others, _GROUP_NAMES[:1], "w_in") + list(grads)
    upd_in = _adamw_call(*[jnp.swapaxes(a, 0, 1) for a in (ws["w_in"], grads[0], ms["w_in"], vs["w_in"])], "adamw_w_in")
    upd = [[jnp.swapaxes(a, 0, 1) for a in upd_in]] + upd

    shapes = {n: w[n].shape for n in _NAMES}
    res = []
    for k in range(4):
        big = _ungroup(list(grads) if k == 0 else [u[k - 1] for u in upd])
        big["conv_qk"] = conv_g if k == 0 else conv_upd[k - 1]
        leaves = _unpack_small(small_g if k == 0 else small_upd[k - 1], shapes)
        leaves.update({n: a.reshape(shapes[n]) for n, a in big.items()})
        res.append(leaves)

    out = [small_g[5, 8 + SWH], grad_x[None]]
    for k in range(4):
        out += [res[k][n] for n in _NAMES]
    return tuple(out)


def kernel(x, p, norm_mix_g, w_in, conv_qk, b_if, mlstm_norm_g, sinks, w_branch_a, w_branch_b, w_out, norm_mlp_g, w_up, w_down, norm_ple_g, w_ple_gate, w_ple_proj, final_norm_g, loss_target, m_norm_mix_g, m_w_in, m_conv_qk, m_b_if, m_mlstm_norm_g, m_sinks, m_w_branch_a, m_w_branch_b, m_w_out, m_norm_mlp_g, m_w_up, m_w_down, m_norm_ple_g, m_w_ple_gate, m_w_ple_proj, m_final_norm_g, v_norm_mix_g, v_w_in, v_conv_qk, v_b_if, v_mlstm_norm_g, v_sinks, v_w_branch_a, v_w_branch_b, v_w_out, v_norm_mlp_g, v_w_up, v_w_down, v_norm_ple_g, v_w_ple_gate, v_w_ple_proj, v_final_norm_g):
    w = dict(zip(_NAMES, (norm_mix_g, w_in, conv_qk, b_if, mlstm_norm_g, sinks, w_branch_a, w_branch_b, w_out,
                          norm_mlp_g, w_up, w_down, norm_ple_g, w_ple_gate, w_ple_proj, final_norm_g)))
    m = dict(zip(_NAMES, (m_norm_mix_g, m_w_in, m_conv_qk, m_b_if, m_mlstm_norm_g, m_sinks, m_w_branch_a,
                          m_w_branch_b, m_w_out, m_norm_mlp_g, m_w_up, m_w_down, m_norm_ple_g, m_w_ple_gate,
                          m_w_ple_proj, m_final_norm_g)))
    v = dict(zip(_NAMES, (v_norm_mix_g, v_w_in, v_conv_qk, v_b_if, v_mlstm_norm_g, v_sinks, v_w_branch_a,
                          v_w_branch_b, v_w_out, v_norm_mlp_g, v_w_up, v_w_down, v_norm_ple_g, v_w_ple_gate,
                          v_w_ple_proj, v_final_norm_g)))
    return _step(x, p, loss_target, w, m, v)
```

```python
import jax
import jax.numpy as jnp
from jax import lax
from jax.experimental import pallas as pl
from jax.experimental.pallas import tpu as pltpu

F32 = jnp.float32
BF16 = jnp.bfloat16

D = 1024
PLE = 256
MLH = 4
DQK = 128
DV = 256
CONV = 4
CHUNK = 256
SWH = 16
SWKV = 4
SWG = SWH // SWKV
HD = 64
WIN = 128
DFF = 4096
EPS = 1e-6
N_IN = 6664
NP = 7168
C_QK, C_V, C_O, C_QSW, C_GA, C_GB, C_KV, C_IF = 0, 1024, 2048, 3072, 4096, 5120, 6144, 6656
IFW = NP - C_IF

ADAM_LR = 0.001
ADAM_B1 = 0.9
ADAM_B2 = 0.999
ADAM_EPS = 1e-08
ADAM_WD = 0.01
ADAM_STEP = 10

TOK_TILE = 512
V7X_VMEM_BYTES = 64 * 1024 * 1024
VMEM_LIMIT = V7X_VMEM_BYTES - 6 * 1024 * 1024


def _params(**kw):
    return pltpu.CompilerParams(vmem_limit_bytes=VMEM_LIMIT, **kw)


def _pick(n, cap):
    if n <= cap:
        return n
    t = cap - cap % 128
    while t > 128 and n % t:
        t -= 128
    assert n % t == 0, (n, cap)
    return t


def _dot(a, b, dims):
    return lax.dot_general(a, b, (dims, ((), ())), preferred_element_type=F32)


def _dot_nn(a, b):
    return _dot(a, b, ((1,), (0,)))


def _dot_nt(a, b):
    return _dot(a, b, ((1,), (1,)))


def _dot_tn(a, b):
    return _dot(a, b, ((0,), (0,)))


def _sigmoid(x):
    return 1.0 / (1.0 + jnp.exp(-x))


def _mm(a, b, mode, out_dtype, name, out_chunks=1):
    if mode == "nn":
        (m, k), (k2, n) = a.shape, b.shape
    elif mode == "nt":
        (m, k), (n, k2) = a.shape, b.shape
    else:
        (k, m), (k2, n) = a.shape, b.shape
    assert k == k2, (a.shape, b.shape, mode)
    tm, tn, tk = _pick(m, 1024), _pick(n // out_chunks, 1024), _pick(k, 2048)
    nk = k // tk
    if mode == "nn":
        a_spec = pl.BlockSpec((tm, tk), lambda i, j, kk: (i, kk))
        b_spec = pl.BlockSpec((tk, tn), lambda i, j, kk: (kk, j))
        dot = _dot_nn
    elif mode == "nt":
        a_spec = pl.BlockSpec((tm, tk), lambda i, j, kk: (i, kk))
        b_spec = pl.BlockSpec((tn, tk), lambda i, j, kk: (j, kk))
        dot = _dot_nt
    else:
        a_spec = pl.BlockSpec((tk, tm), lambda i, j, kk: (kk, i))
        b_spec = pl.BlockSpec((tk, tn), lambda i, j, kk: (kk, j))
        dot = _dot_tn
    if out_chunks > 1:
        npc = (n // out_chunks) // tn
        out_spec = pl.BlockSpec((None, tm, tn), lambda i, j, kk: (j // npc, i, j % npc))
        out_shape = jax.ShapeDtypeStruct((out_chunks, m, n // out_chunks), out_dtype)
    else:
        out_spec = pl.BlockSpec((tm, tn), lambda i, j, kk: (i, j))
        out_shape = jax.ShapeDtypeStruct((m, n), out_dtype)

    def body(a_ref, b_ref, o_ref, acc_ref):
        kk = pl.program_id(2)

        @pl.when(kk == 0)
        def _():
            acc_ref[...] = jnp.zeros_like(acc_ref)

        acc_ref[...] += dot(a_ref[...], b_ref[...])

        @pl.when(kk == nk - 1)
        def _():
            o_ref[...] = acc_ref[...].astype(out_dtype)

    return pl.pallas_call(
        body, name=name, grid=(m // tm, n // tn, nk),
        in_specs=[a_spec, b_spec], out_specs=out_spec, out_shape=out_shape,
        scratch_shapes=[pltpu.VMEM((tm, tn), F32)],
        compiler_params=_params(dimension_semantics=("parallel", "parallel", "arbitrary")),
    )(a, b)


def _tile(col0=0):
    return lambda tm, tn: pl.BlockSpec((tm, tn), lambda i, j, kk: (i, col0 // tn + j))


def _row():
    return lambda tm, tn: pl.BlockSpec((1, tn), lambda i, j, kk: (0, j))


def _mm_ep(pairs, mode, name, epilogue, ins, outs, tm, tn, aliases=None, row_split=1, init=None):
    a0, b0 = pairs[0]
    bch = b0.shape[0] if b0.ndim == 3 else 1
    m, k = a0.shape
    tm = _pick(m, tm)
    n = b0.shape[-1] * bch if mode == "nn" else b0.shape[-2]
    tk = _pick(k // bch if mode == "nt" else k, 2048)
    nk = k // tk
    a_spec = pl.BlockSpec((tm, tk), lambda i, j, kk: (i, kk))
    if mode == "nn":
        dot = _dot_nn
        if bch > 1:
            bpc = (n // bch) // tn
            b_spec = pl.BlockSpec((None, tk, tn), lambda i, j, kk: (j // bpc, kk, j % bpc))
        else:
            b_spec = pl.BlockSpec((tk, tn), lambda i, j, kk: (kk, j))
    else:
        dot = _dot_nt
        if bch > 1:
            bpc = (k // bch) // tk
            b_spec = pl.BlockSpec((None, tn, tk), lambda i, j, kk: (kk // bpc, j, kk % bpc))
        else:
            b_spec = pl.BlockSpec((tn, tk), lambda i, j, kk: (j, kk))
    npair, nin, nout = len(pairs), len(ins), len(outs)
    rows = tm // row_split
    assert init is None or row_split > 1

    def body_split(*refs):
        ab = refs[:2 * npair]
        in_refs = refs[2 * npair:2 * npair + nin]
        out_refs = refs[2 * npair + nin:2 * npair + nin + nout]
        accs = refs[2 * npair + nin + nout:]
        i, j, kk = pl.program_id(0), pl.program_id(1), pl.program_id(2)

        if init is not None:
            @pl.when((i == 0) & (kk == 0))
            def _():
                init(out_refs)

        @pl.when(kk < nk - 1)
        def _():
            for p in range(npair):
                prod = dot(ab[2 * p][...], ab[2 * p + 1][...])

                @pl.when(kk == 0)
                def _():
                    accs[p][...] = prod

                @pl.when(kk > 0)
                def _():
                    accs[p][...] += prod

        @pl.when(kk == nk - 1)
        def _():
            for r in range(row_split):
                rs = pl.ds(r * rows, rows)
                tot = []
                for p in range(npair):
                    prod = dot(ab[2 * p][rs, :], ab[2 * p + 1][...])
                    tot.append(prod if nk == 1 else accs[p][rs, :] + prod)

                def view(ref):
                    return ref.at[rs] if ref.shape[0] == tm else ref

                epilogue(tot, [view(x) for x in in_refs], [view(x) for x in out_refs], i * row_split + r, j)

    def body(*refs):
        ab = refs[:2 * npair]
        in_refs = refs[2 * npair:2 * npair + nin]
        out_refs = refs[2 * npair + nin:2 * npair + nin + nout]
        accs = refs[2 * npair + nin + nout:]
        i, j, kk = pl.program_id(0), pl.program_id(1), pl.program_id(2)
        for p in range(npair):
            prod = dot(ab[2 * p][...], ab[2 * p + 1][...])

            @pl.when(kk == 0)
            def _():
                accs[p][...] = prod

            @pl.when(kk > 0)
            def _():
                accs[p][...] += prod

        @pl.when(kk == nk - 1)
        def _():
            epilogue([acc[...] for acc in accs], in_refs, out_refs, i, j)

    operands = [x for pair in pairs for x in pair] + [a for a, _ in ins]
    io_alias = {2 * npair + i: o for i, o in (aliases or {}).items()}
    return pl.pallas_call(
        body if row_split == 1 else body_split, name=name, grid=(m // tm, n // tn, nk),
        in_specs=[a_spec, b_spec] * npair + [mk(tm, tn) for _, mk in ins],
        out_specs=[mk(tm, tn) for _, mk in outs], out_shape=[s for s, _ in outs],
        scratch_shapes=[pltpu.VMEM((tm, tn), F32)] * npair, input_output_aliases=io_alias,
        compiler_params=_params(dimension_semantics=("arbitrary", "arbitrary", "arbitrary")),
    )(*operands)


def _tok(w, j=0):
    return pl.BlockSpec((TOK_TILE, w), lambda i: (i, j))


def _rep(shape):
    return pl.BlockSpec(shape, lambda i: (0,) * len(shape))


def _rms(x):
    rstd = lax.rsqrt(jnp.mean(x * x, axis=-1, keepdims=True) + EPS)
    return x * rstd, rstd


def _rms_bwd(xn, rstd, dxn):
    return rstd * (dxn - xn * jnp.mean(dxn * xn, axis=-1, keepdims=True))


def _halo_prev(w, j=0, rows=8):
    r = TOK_TILE // rows
    return pl.BlockSpec((rows, w), lambda i: (jnp.maximum(i * r - 1, 0), j))


def _last8(halo_ref):
    return halo_ref[...].astype(F32)[halo_ref.shape[0] - 8:]


def _halo_next(w, nt, j=0):
    r = TOK_TILE // 8
    return pl.BlockSpec((8, w), lambda i: (jnp.minimum((i + 1) * r, nt * r - 1), j))


def _shift_down(x, halo, s):
    if s == 0:
        return x
    r = pltpu.roll(x, s, 0)
    hs = pltpu.roll(halo, s, 0)
    row = lax.broadcasted_iota(jnp.int32, hs.shape, 0)
    top = jnp.where(row < s, hs, r[0:8])
    return jnp.concatenate([top, r[8:]], axis=0)


def _shift_up(x, halo, s):
    if s == 0:
        return x
    n = x.shape[0]
    r = pltpu.roll(x, n - s, 0)
    hs = pltpu.roll(halo, 8 - s, 0)
    row = lax.broadcasted_iota(jnp.int32, hs.shape, 0)
    bot = jnp.where(row >= 8 - s, hs, r[n - 8:])
    return jnp.concatenate([r[:n - 8], bot], axis=0)


def _bf(x):
    return x.astype(BF16).astype(F32)


def _conv_taps(x, halo, w):
    x, halo, w = _bf(x), _bf(halo), _bf(w)
    acc = x * w[CONV - 1:CONV, :]
    for j in range(CONV - 1):
        acc = acc + _shift_down(x, halo, CONV - 1 - j) * w[j:j + 1, :]
    return acc


_Q_SCALE = DQK ** -0.5


def _qscale_row():
    lane = lax.broadcasted_iota(jnp.int32, (1, D), 1)
    return jnp.where(lane < MLH * DQK, _Q_SCALE, 1.0).astype(F32)


def _conv_silu_fwd(proj, conv_w):
    t = proj.shape[0]

    def body(x_ref, halo_ref, w_ref, o_ref):
        halo = jnp.where(pl.program_id(0) > 0, _last8(halo_ref), 0.0)
        c = _conv_taps(x_ref[...].astype(F32), halo, w_ref[...])
        o_ref[...] = (c * _sigmoid(c) * _qscale_row()).astype(BF16)

    return pl.pallas_call(
        body, name="conv_silu_fwd", grid=(t // TOK_TILE,),
        in_specs=[_tok(D, C_QK // D), _halo_prev(D, C_QK // D, 16), _rep((CONV, D))], out_specs=_tok(D),
        out_shape=jax.ShapeDtypeStruct((t, D), BF16), compiler_params=_params(),
    )(proj, proj, conv_w)


def _conv_silu_bwd_a(proj, conv_w, dqk):
    t = proj.shape[0]

    def body(x_ref, halo_ref, w_ref, d_ref, dc_ref, dw_ref):
        @pl.when(pl.program_id(0) == 0)
        def _():
            dw_ref[...] = jnp.zeros_like(dw_ref)

        halo = jnp.where(pl.program_id(0) > 0, _last8(halo_ref), 0.0)
        x = x_ref[...].astype(F32)
        c = _conv_taps(x, halo, w_ref[...])
        s = _sigmoid(c)
        dc = d_ref[...] * _qscale_row() * (s * (1.0 + c * (1.0 - s)))
        dc_ref[...] = dc
        dcb, xb, halo_b = _bf(dc), _bf(x), _bf(halo)
        for j in range(CONV):
            dw_ref[j:j + 1, :] += jnp.sum(dcb * _shift_down(xb, halo_b, CONV - 1 - j), axis=0, keepdims=True)

    return pl.pallas_call(
        body, name="conv_silu_bwd_a", grid=(t // TOK_TILE,),
        in_specs=[_tok(D, C_QK // D), _halo_prev(D, C_QK // D, 16), _rep((CONV, D)), _tok(D)],
        out_specs=[_tok(D), _rep((CONV, D))],
        out_shape=[jax.ShapeDtypeStruct((t, D), F32), jax.ShapeDtypeStruct((CONV, D), F32)],
        compiler_params=_params(),
    )(proj, proj, conv_w, dqk)


def _conv_silu_bwd_b(dc, conv_w, dproj):
    t = dc.shape[0]
    nt = t // TOK_TILE

    def body(dc_ref, halo_ref, w_ref, _, dx_ref):
        halo = _bf(jnp.where(pl.program_id(0) < nt - 1, halo_ref[...], 0.0))
        dcv = _bf(dc_ref[...])
        w = _bf(w_ref[...])
        acc = dcv * w[CONV - 1:CONV, :]
        for j in range(CONV - 1):
            acc = acc + _shift_up(dcv, halo, CONV - 1 - j) * w[j:j + 1, :]
        dx_ref[...] = acc.astype(BF16)

    return pl.pallas_call(
        body, name="conv_silu_bwd_b", grid=(nt,), in_specs=[_tok(D), _halo_next(D, nt), _rep((CONV, D)), _ANY],
        out_specs=_tok(D, C_QK // D), out_shape=jax.ShapeDtypeStruct((t, NP), BF16),
        input_output_aliases={3: 0}, compiler_params=_params(),
    )(dc, dc, conv_w, dproj)


def _gates_fwd(pre_rows, bias_col):
    t = pre_rows.shape[1]

    def body(p_ref, b_ref, g_ref, s_ref):
        z = p_ref[...] + b_ref[...]
        lf = jnp.minimum(z, 0.0) - jnp.log(1.0 + jnp.exp(-jnp.abs(z)))
        lane = lax.broadcasted_iota(jnp.int32, z.shape, 1) % CHUNK
        cum = lf
        s = 1
        while s < CHUNK:
            cum = cum + jnp.where(lane >= s, pltpu.roll(cum, s, 1), 0.0)
            s *= 2
        sub = lax.broadcasted_iota(jnp.int32, z.shape, 0)
        g_ref[...] = jnp.where(sub < MLH, z, cum)
        s_ref[...] = _sigmoid(-z)

    return pl.pallas_call(
        body, name="gates_fwd",
        out_shape=[jax.ShapeDtypeStruct((8, t), F32), jax.ShapeDtypeStruct((8, t), F32)],
        compiler_params=_params(),
    )(pre_rows, bias_col)


def _chunk_terms(grow, gcol, m0):
    heads = range(MLH)
    i_row = [grow[h:h + 1, :] for h in heads]
    b_row = [grow[MLH + h:MLH + h + 1, :] for h in heads]
    i_col = [gcol[:, h:h + 1] for h in heads]
    b_col = [gcol[:, MLH + h:MLH + h + 1] for h in heads]
    b_last = [b_row[h][:, CHUNK - 1:CHUNK] for h in heads]
    tt = lax.broadcasted_iota(jnp.int32, (CHUNK, CHUNK), 0)
    ss = lax.broadcasted_iota(jnp.int32, (CHUNK, CHUNK), 1)
    log_d = [jnp.where(tt >= ss, b_col[h] - b_row[h] + i_row[h], -jnp.inf) for h in heads]
    row_max = [jnp.max(log_d[h], axis=1, keepdims=True) for h in heads]
    last_max = [jnp.max(b_last[h] - b_row[h] + i_row[h], axis=1, keepdims=True) for h in heads]
    m_t = [jnp.maximum(b_col[h] + m0[h], row_max[h]) for h in heads]
    m1 = [jnp.maximum(b_last[h] + m0[h], last_max[h]) for h in heads]
    dm = [jnp.exp(log_d[h] - m_t[h]) for h in heads]
    wi = [jnp.exp(b_col[h] + m0[h] - m_t[h]) for h in heads]
    ws = [jnp.exp(b_last[h] - b_col[h] + i_col[h] - m1[h]) for h in heads]
    dec = [jnp.exp(b_last[h] + m0[h] - m1[h]) for h in heads]
    return [(dm[h], wi[h], m_t[h], ws[h], dec[h], m1[h]) for h in heads]


def _mlstm_fwd(qk, proj, grow, gcol, gain):
    t = qk.shape[0]
    nc = t // CHUNK

    def body(qk_ref, v_ref, o_ref, grow_ref, gcol_ref, g_ref, h_ref, y_ref, cs_ref, st_ref, c_scr, st_scr):
        @pl.when(pl.program_id(0) == 0)
        def _():
            c_scr[...] = jnp.zeros_like(c_scr)
            st_scr[...] = jnp.zeros_like(st_scr)

        grow_v, gcol_v = grow_ref[...], gcol_ref[...]
        heads = range(MLH)
        q = [qk_ref[:, h * DQK:(h + 1) * DQK] for h in heads]
        k = [qk_ref[:, MLH * DQK + h * DQK:MLH * DQK + (h + 1) * DQK] for h in heads]
        v = [v_ref[:, h * DV:(h + 1) * DV] for h in heads]
        c0 = [c_scr[h] for h in heads]
        n0 = [st_scr[h, 0:1, :] for h in heads]
        for h in heads:
            cs_ref[0, h] = c0[h]
            st_ref[0, h] = st_scr[h]
        terms = _chunk_terms(grow_v, gcol_v, [st_scr[h, 1:2, 0:1] for h in heads])
        a = [_dot_nt(q[h], k[h]) for h in heads]
        qc = [_dot_nt(q[h], c0[h].astype(BF16)) for h in heads]
        s = [a[h] * terms[h][0] for h in heads]
        sv = [_dot_nn(s[h].astype(BF16), v[h]) for h in heads]
        upd = [_dot_tn((terms[h][3] * v[h]).astype(BF16), k[h]) for h in heads]
        den = [terms[h][1] * jnp.sum(q[h].astype(F32) * n0[h], axis=1, keepdims=True)
               + jnp.sum(s[h], axis=1, keepdims=True) for h in heads]
        hv = [(terms[h][1] * qc[h] + sv[h]) / jnp.maximum(jnp.abs(den[h]), jnp.exp(-terms[h][2])) for h in heads]
        for h in heads:
            sl = slice(h * DV, (h + 1) * DV)
            h_ref[:, sl] = hv[h]
            xn, _ = _rms(hv[h])
            y_ref[:, sl] = (_sigmoid(o_ref[:, sl].astype(F32)) * xn * g_ref[:, sl]).astype(BF16)
        for h in heads:
            dec, m1 = terms[h][4], terms[h][5]
            c_scr[h] = dec * c0[h] + upd[h]
            st_scr[h, 0:1, :] = dec * n0[h] + jnp.sum(terms[h][3] * k[h].astype(F32), axis=0, keepdims=True)
            st_scr[h, 1:2, :] = jnp.broadcast_to(m1, (1, DQK))

    return pl.pallas_call(
        body, name="mlstm_fwd", grid=(nc,),
        in_specs=[pl.BlockSpec((CHUNK, D), lambda c: (c, 0)), pl.BlockSpec((CHUNK, D), lambda c: (c, C_V // D)),
                  pl.BlockSpec((CHUNK, D), lambda c: (c, C_O // D)),
                  pl.BlockSpec((8, CHUNK), lambda c: (0, c)), pl.BlockSpec((CHUNK, 8), lambda c: (c, 0)),
                  pl.BlockSpec((1, D), lambda c: (0, 0))],
        out_specs=[pl.BlockSpec((CHUNK, D), lambda c: (c, 0)), pl.BlockSpec((CHUNK, D), lambda c: (c, 0)),
                   pl.BlockSpec((1, MLH, DV, DQK), lambda c: (c, 0, 0, 0)),
                   pl.BlockSpec((1, MLH, 8, DQK), lambda c: (c, 0, 0, 0))],
        out_shape=[jax.ShapeDtypeStruct((t, D), F32), jax.ShapeDtypeStruct((t, D), BF16),
                   jax.ShapeDtypeStruct((nc, MLH, DV, DQK), F32), jax.ShapeDtypeStruct((nc, MLH, 8, DQK), F32)],
        scratch_shapes=[pltpu.VMEM((MLH, DV, DQK), F32), pltpu.VMEM((MLH, 8, DQK), F32)],
        compiler_params=_params(dimension_semantics=("arbitrary",)),
    )(qk, proj, proj, grow, gcol, gain)


def _mlstm_bwd(qk, proj, grow, gcol, sneg_col, cs, st, hraw, dh, dproj):
    t = qk.shape[0]
    nc = t // CHUNK

    def rev(c):
        return nc - 1 - c

    def nxt(c):
        return jnp.minimum(nc - c, nc - 1)

    def body(qk_ref, v_ref, grow_ref, gcol_ref, sneg_ref, cs_ref, st_ref, cs1_ref, st1_ref, h_ref, dh_ref, _,
             dqk_ref, dv_ref, dif_ref, dbif_ref, dc_scr, dn_scr):
        @pl.when(pl.program_id(0) == 0)
        def _():
            dc_scr[...] = jnp.zeros_like(dc_scr)
            dn_scr[...] = jnp.zeros_like(dn_scr)
            dbif_ref[...] = jnp.zeros_like(dbif_ref)

        grow_v, gcol_v, sneg = grow_ref[...], gcol_ref[...], sneg_ref[...]
        tt = lax.broadcasted_iota(jnp.int32, (CHUNK, CHUNK), 0)
        ss = lax.broadcasted_iota(jnp.int32, (CHUNK, CHUNK), 1)
        lane8 = lax.broadcasted_iota(jnp.int32, (CHUNK, 8), 1)
        heads = range(MLH)
        q = [qk_ref[:, h * DQK:(h + 1) * DQK] for h in heads]
        k = [qk_ref[:, MLH * DQK + h * DQK:MLH * DQK + (h + 1) * DQK] for h in heads]
        qf, kf = [a.astype(F32) for a in q], [a.astype(F32) for a in k]
        vb = [v_ref[:, h * DV:(h + 1) * DV].astype(BF16) for h in heads]
        c0 = [cs_ref[0, h] for h in heads]
        n0 = [st_ref[0, h, 0:1, :] for h in heads]
        dc1 = [dc_scr[h] for h in heads]
        dn1 = [dn_scr[h, 0:1, :] for h in heads]
        terms = _chunk_terms(grow_v, gcol_v, [st_ref[0, h, 1:2, 0:1] for h in heads])
        dm, wi, ws = [t[0] for t in terms], [t[1] for t in terms], [t[3] for t in terms]
        s = [_dot_nt(q[h], k[h]) * dm[h] for h in heads]
        den = [wi[h] * jnp.sum(qf[h] * n0[h], axis=1, keepdims=True) + jnp.sum(s[h], axis=1, keepdims=True)
               for h in heads]
        floor = [jnp.exp(-terms[h][2]) for h in heads]
        g = [jnp.maximum(jnp.abs(den[h]), floor[h]) for h in heads]
        dh_v = [dh_ref[:, h * DV:(h + 1) * DV] for h in heads]
        dnum = [dh_v[h] / g[h] for h in heads]
        dden = [-jnp.sum(dh_v[h] * h_ref[:, h * DV:(h + 1) * DV], axis=1, keepdims=True) / g[h] for h in heads]
        dden = [jnp.where(jnp.abs(den[h]) > floor[h], dden[h] * jnp.sign(den[h]), 0.0) for h in heads]
        dnum_b = [a.astype(BF16) for a in dnum]
        dc1_b = [a.astype(BF16) for a in dc1]
        da = [((_dot_nt(dnum_b[h], vb[h]) + dden[h]) * dm[h]).astype(BF16) for h in heads]
        dq_inter = [_dot_nn(dnum_b[h], c0[h].astype(BF16)) for h in heads]
        dk_inter = [_dot_nn(vb[h], dc1_b[h]) for h in heads]
        dv_inter = [_dot_nt(k[h], dc1_b[h]) for h in heads]
        dc_new = [_dot_tn((wi[h] * dnum[h]).astype(BF16), q[h]) for h in heads]
        dq = [_dot_nn(da[h], k[h]) + wi[h] * (dq_inter[h] + dden[h] * n0[h]) for h in heads]
        dk = [_dot_tn(da[h], q[h]) + ws[h] * (dk_inter[h] + dn1[h]) for h in heads]
        dv = [_dot_tn(s[h].astype(BF16), dnum_b[h]) + ws[h] * dv_inter[h] for h in heads]
        for h in heads:
            dqk_ref[:, h * DQK:(h + 1) * DQK] = dq[h]
            dqk_ref[:, MLH * DQK + h * DQK:MLH * DQK + (h + 1) * DQK] = dk[h]
            dv_ref[:, h * DV:(h + 1) * DV] = dv[h].astype(BF16)
        rk = [jnp.sum(kf[h] * dk[h], axis=1, keepdims=True) for h in heads]
        df = [jnp.sum(qf[h] * dq[h], axis=1, keepdims=True) - rk[h] for h in heads]
        df_row = [jnp.sum(jnp.where(tt == ss, df[h], 0.0), axis=0, keepdims=True) for h in heads]
        suffix = [jnp.sum(jnp.where(ss >= tt, df_row[h], 0.0), axis=1, keepdims=True) for h in heads]
        cross = [jnp.sum(jnp.sum(dc1[h] * cs1_ref[0, h], axis=0, keepdims=True), axis=1, keepdims=True)
                 + jnp.sum(dn1[h] * st1_ref[0, h, 0:1, :], axis=1, keepdims=True) for h in heads]
        dif = jnp.zeros((CHUNK, 8), F32)
        for h in heads:
            dpf = (suffix[h] + cross[h]) * sneg[:, MLH + h:MLH + h + 1]
            dif = dif + jnp.where(lane8 == h, rk[h], 0.0) + jnp.where(lane8 == MLH + h, dpf, 0.0)
            dc_scr[h] = terms[h][4] * dc1[h] + dc_new[h]
            dn_scr[h, 0:1, :] = terms[h][4] * dn1[h] + jnp.sum(wi[h] * dden[h] * qf[h], axis=0, keepdims=True)
        dif_ref[...] = dif
        dbif_ref[...] += jnp.sum(dif, axis=0, keepdims=True)

    return pl.pallas_call(
        body, name="mlstm_bwd", grid=(nc,),
        in_specs=[pl.BlockSpec((CHUNK, D), lambda c: (rev(c), 0)),
                  pl.BlockSpec((CHUNK, D), lambda c: (rev(c), C_V // D)),
                  pl.BlockSpec((8, CHUNK), lambda c: (0, rev(c))),
                  pl.BlockSpec((CHUNK, 8), lambda c: (rev(c), 0)),
                  pl.BlockSpec((CHUNK, 8), lambda c: (rev(c), 0)),
                  pl.BlockSpec((1, MLH, DV, DQK), lambda c: (rev(c), 0, 0, 0)),
                  pl.BlockSpec((1, MLH, 8, DQK), lambda c: (rev(c), 0, 0, 0)),
                  pl.BlockSpec((1, MLH, DV, DQK), lambda c: (nxt(c), 0, 0, 0)),
                  pl.BlockSpec((1, MLH, 8, DQK), lambda c: (nxt(c), 0, 0, 0)),
                  pl.BlockSpec((CHUNK, D), lambda c: (rev(c), 0)),
                  pl.BlockSpec((CHUNK, D), lambda c: (rev(c), 0)), _ANY],
        out_specs=[pl.BlockSpec((CHUNK, D), lambda c: (rev(c), 0)),
                   pl.BlockSpec((CHUNK, D), lambda c: (rev(c), C_V // D)),
                   pl.BlockSpec((CHUNK, 8), lambda c: (rev(c), 0)),
                   pl.BlockSpec((1, 8), lambda c: (0, 0))],
        out_shape=[jax.ShapeDtypeStruct((t, D), F32), jax.ShapeDtypeStruct((t, NP), BF16),
                   jax.ShapeDtypeStruct((t, 8), F32), jax.ShapeDtypeStruct((1, 8), F32)],
        scratch_shapes=[pltpu.VMEM((MLH, DV, DQK), F32), pltpu.VMEM((MLH, 8, DQK), F32)],
        input_output_aliases={11: 1}, compiler_params=_params(dimension_semantics=("arbitrary",)),
    )(qk, proj, grow, gcol, sneg_col, cs, st, cs, st, hraw, dh, dproj)


_ANY = pl.BlockSpec(memory_space=pl.ANY)


_SW_SCALE = HD ** -0.5
_KVB = C_KV // (2 * SWKV * HD)


def _swa_mask(n):
    ki = lax.broadcasted_iota(jnp.int32, (2 * WIN, SWG * WIN), 0)
    qi = lax.broadcasted_iota(jnp.int32, (2 * WIN, SWG * WIN), 1) % WIN
    return (ki > qi) & (ki <= qi + WIN) & ((n > 0) | (ki >= WIN))


def _group_rows(x_ref, hk):
    return jnp.concatenate([x_ref[:, (hk * SWG + g) * HD:(hk * SWG + g + 1) * HD] for g in range(SWG)], axis=0)


def _group_lanes(x_ref, hk):
    return jnp.concatenate([x_ref[hk * SWG + g:hk * SWG + g + 1, :] for g in range(SWG)], axis=1)


def _sink_lanes(sink_ref, hk):
    return jnp.concatenate([jnp.broadcast_to(sink_ref[:, hk * SWG + g:hk * SWG + g + 1], (1, WIN))
                            for g in range(SWG)], axis=1)


def _swa_fwd(proj, sinks):
    t = proj.shape[0]
    nb = t // WIN

    def body(q_ref, kvc_ref, kvp_ref, sink_ref, y_ref, lse_ref):
        valid = _swa_mask(pl.program_id(0))
        for hk in range(SWKV):
            ks = slice(hk * HD, (hk + 1) * HD)
            vs = slice(SWKV * HD + hk * HD, SWKV * HD + (hk + 1) * HD)
            kb = jnp.concatenate([kvp_ref[:, ks], kvc_ref[:, ks]], axis=0).astype(BF16)
            vb = jnp.concatenate([kvp_ref[:, vs], kvc_ref[:, vs]], axis=0).astype(BF16)
            q4 = _group_rows(q_ref, hk).astype(BF16)
            sink = _sink_lanes(sink_ref, hk)
            logits = jnp.where(valid, _dot_nt(kb, q4) * _SW_SCALE, -jnp.inf)
            m = jnp.maximum(jnp.max(logits, axis=0, keepdims=True), sink)
            p = jnp.exp(logits - m)
            denom = jnp.sum(p, axis=0, keepdims=True) + jnp.exp(sink - m)
            y4 = _dot_tn((p / denom).astype(BF16), vb).astype(BF16)
            lse4 = m + jnp.log(denom)
            for g in range(SWG):
                hq = hk * SWG + g
                y_ref[:, hq * HD:(hq + 1) * HD] = y4[g * WIN:(g + 1) * WIN]
                lse_ref[hq:hq + 1, :] = lse4[:, g * WIN:(g + 1) * WIN]

    return pl.pallas_call(
        body, name="swa_fwd", grid=(nb,),
        in_specs=[pl.BlockSpec((WIN, D), lambda n: (n, C_QSW // D)),
                  pl.BlockSpec((WIN, 512), lambda n: (n, _KVB)),
                  pl.BlockSpec((WIN, 512), lambda n: (jnp.maximum(n - 1, 0), _KVB)),
                  pl.BlockSpec((1, SWH), lambda n: (0, 0))],
        out_specs=[pl.BlockSpec((WIN, D), lambda n: (n, 0)), pl.BlockSpec((SWH, WIN), lambda n: (0, n))],
        out_shape=[jax.ShapeDtypeStruct((t, D), BF16), jax.ShapeDtypeStruct((SWH, t), F32)],
        compiler_params=_params(),
    )(proj, proj, proj, sinks)


def _swa_bwd(proj, sinks, lse, dyb, dproj):
    t = proj.shape[0]
    nb = t // WIN

    def body(q_ref, kvc_ref, kvp_ref, sink_ref, lse_ref, dy_ref, _, dq_ref, dself_ref, dprev_ref, ds_ref):
        @pl.when(pl.program_id(0) == 0)
        def _():
            ds_ref[...] = jnp.zeros_like(ds_ref)

        valid = _swa_mask(pl.program_id(0))
        kvh = range(SWKV)
        ks = [slice(hk * HD, (hk + 1) * HD) for hk in kvh]
        vs = [slice(SWKV * HD + hk * HD, SWKV * HD + (hk + 1) * HD) for hk in kvh]
        kb = [jnp.concatenate([kvp_ref[:, ks[hk]], kvc_ref[:, ks[hk]]], axis=0).astype(BF16) for hk in kvh]
        vb = [jnp.concatenate([kvp_ref[:, vs[hk]], kvc_ref[:, vs[hk]]], axis=0).astype(BF16) for hk in kvh]
        qb = [_group_rows(q_ref, hk).astype(BF16) for hk in kvh]
        dyb_ = [_group_rows(dy_ref, hk).astype(BF16) for hk in kvh]
        lse4 = [_group_lanes(lse_ref, hk) for hk in kvh]
        logits = [_dot_nt(kb[hk], qb[hk]) for hk in kvh]
        dpt = [_dot_nt(vb[hk], dyb_[hk]) for hk in kvh]
        p = [jnp.exp(jnp.where(valid, logits[hk] * _SW_SCALE, -jnp.inf) - lse4[hk]) for hk in kvh]
        delta = [jnp.sum(p[hk] * dpt[hk], axis=0, keepdims=True) for hk in kvh]
        dsm = [(p[hk] * (dpt[hk] - delta[hk])).astype(BF16) for hk in kvh]
        dq4 = [(_dot_tn(dsm[hk], kb[hk]) * _SW_SCALE).astype(BF16) for hk in kvh]
        dkb = [_dot_nn(dsm[hk], qb[hk]) * _SW_SCALE for hk in kvh]
        dvb = [_dot_nn(p[hk].astype(BF16), dyb_[hk]) for hk in kvh]
        for hk in kvh:
            dsink4 = jnp.exp(_sink_lanes(sink_ref, hk) - lse4[hk]) * delta[hk]
            for g in range(SWG):
                hq = hk * SWG + g
                dq_ref[:, hq * HD:(hq + 1) * HD] = dq4[hk][g * WIN:(g + 1) * WIN]
                ds_ref[:, hq:hq + 1] += -jnp.sum(dsink4[:, g * WIN:(g + 1) * WIN], axis=1, keepdims=True)
            dprev_ref[:, ks[hk]] = dkb[hk][:WIN]
            dself_ref[:, ks[hk]] = dkb[hk][WIN:]
            dprev_ref[:, vs[hk]] = dvb[hk][:WIN]
            dself_ref[:, vs[hk]] = dvb[hk][WIN:]

    return pl.pallas_call(
        body, name="swa_bwd", grid=(nb,),
        in_specs=[pl.BlockSpec((WIN, D), lambda n: (n, C_QSW // D)),
                  pl.BlockSpec((WIN, 512), lambda n: (n, _KVB)),
                  pl.BlockSpec((WIN, 512), lambda n: (jnp.maximum(n - 1, 0), _KVB)),
                  pl.BlockSpec((1, SWH), lambda n: (0, 0)),
                  pl.BlockSpec((SWH, WIN), lambda n: (0, n)),
                  pl.BlockSpec((WIN, D), lambda n: (n, 0)), _ANY],
        out_specs=[pl.BlockSpec((WIN, D), lambda n: (n, C_QSW // D)), pl.BlockSpec((WIN, 512), lambda n: (n, 0)),
                   pl.BlockSpec((WIN, 512), lambda n: (jnp.maximum(n - 1, 0), 0)),
                   pl.BlockSpec((1, SWH), lambda n: (0, 0))],
        out_shape=[jax.ShapeDtypeStruct((t, NP), BF16), jax.ShapeDtypeStruct((t, 512), F32),
                   jax.ShapeDtypeStruct((t, 512), F32), jax.ShapeDtypeStruct((1, SWH), F32)],
        input_output_aliases={6: 0}, compiler_params=_params(),
    )(proj, proj, proj, sinks, lse, dyb, dproj)


def _kv_combine(dself, dnext, dif, dproj):
    t = dself.shape[0]
    rows = _pick(t, 512)

    def body(a_ref, b_ref, dif_ref, _, o_ref):
        row = pl.program_id(0) * rows + lax.broadcasted_iota(jnp.int32, (rows, 1), 0)
        o_ref[:, 0:512] = (a_ref[...] + jnp.where(row < t - WIN, b_ref[...], 0.0)).astype(BF16)
        lane = lax.broadcasted_iota(jnp.int32, (rows, 128), 1)
        dif_v = dif_ref[...]
        first = jnp.zeros((rows, 128), F32)
        for col in range(8):
            first = first + jnp.where(lane == col, dif_v[:, col:col + 1], 0.0)
        o_ref[:, 512:640] = first.astype(BF16)
        o_ref[:, 640:512 + IFW] = jnp.zeros((rows, IFW - 128), BF16)

    return pl.pallas_call(
        body, name="kv_combine", grid=(t // rows,),
        in_specs=[pl.BlockSpec((rows, 512), lambda n: (n, 0)), pl.BlockSpec((rows, 512), lambda n: (n, 0)),
                  pl.BlockSpec((rows, 8), lambda n: (n, 0)), _ANY],
        out_specs=pl.BlockSpec((rows, 512 + IFW), lambda n: (n, C_KV // (512 + IFW))),
        out_shape=jax.ShapeDtypeStruct((t, NP), BF16), input_output_aliases={3: 0}, compiler_params=_params(),
    )(dself, dnext, dif, dproj)


def _sds(t, n, dtype):
    return jax.ShapeDtypeStruct((t, n), dtype)


def _proj_in(x, gain, w_in):
    t = x.shape[0]
    tm, tn = _pick(t, 1024), NP // 4

    def body(x_ref, g_ref, w_ref, h_ref, p_ref, gate_ref, h_scr):
        j = pl.program_id(1)

        @pl.when(j == 0)
        def _():
            xn, _ = _rms(x_ref[...])
            h = (xn * g_ref[...]).astype(BF16)
            h_scr[...] = h
            h_ref[...] = h

        acc = _dot_nn(h_scr[...], w_ref[...])
        p_ref[...] = acc.astype(BF16)

        @pl.when(j == C_IF // tn)
        def _():
            gate_ref[...] = acc[:, C_IF % tn:C_IF % tn + 128]

    return pl.pallas_call(
        body, name="mm_in", grid=(t // tm, NP // tn),
        in_specs=[pl.BlockSpec((tm, D), lambda i, j: (i, 0)), pl.BlockSpec((1, D), lambda i, j: (0, 0)),
                  pl.BlockSpec((D, tn), lambda i, j: (0, j))],
        out_specs=[pl.BlockSpec((tm, D), lambda i, j: (i, 0)), pl.BlockSpec((tm, tn), lambda i, j: (i, j)),
                   pl.BlockSpec((tm, 128), lambda i, j: (i, 0))],
        out_shape=[_sds(t, D, BF16), _sds(t, NP, BF16), _sds(t, 128, F32)],
        scratch_shapes=[pltpu.VMEM((tm, D), BF16)],
        compiler_params=_params(dimension_semantics=("arbitrary", "arbitrary")),
    )(x, gain, w_in)


def _branch_merge(ya, yb, wa, wb, proj):
    t = ya.shape[0]

    def epilogue(accs, ins, outs, i, j):
        za, zb = accs
        merged = _sigmoid(ins[0][...].astype(F32)) * za + _sigmoid(ins[1][...].astype(F32)) * zb
        outs[0][...] = merged.astype(BF16)
        outs[1][...] = za.astype(BF16)
        outs[2][...] = zb.astype(BF16)

    return _mm_ep([(ya, wa), (yb, wb)], "nn", "mm_branch_merge", epilogue, [(proj, _tile(C_GA)), (proj, _tile(C_GB))],
                  [(_sds(t, D, BF16), _tile())] * 3, 1024, 1024)


def _dmerged_bwd(dxb, w_out, proj, za, zb):
    t = dxb.shape[0]

    def epilogue(accs, ins, outs, i, j):
        dm = accs[0]
        sa, sb = _sigmoid(ins[0][...].astype(F32)), _sigmoid(ins[1][...].astype(F32))
        outs[0][...] = (dm * sa).astype(BF16)
        outs[1][...] = (dm * sb).astype(BF16)
        outs[2][:, 0:D] = (dm * ins[2][...].astype(F32) * sa * (1.0 - sa)).astype(BF16)
        outs[2][:, D:2 * D] = (dm * ins[3][...].astype(F32) * sb * (1.0 - sb)).astype(BF16)

    gate_cols = lambda tm, tn: pl.BlockSpec((tm, 2 * D), lambda i, j, kk: (i, C_GA // (2 * D)))
    return _mm_ep([(dxb, w_out)], "nt", "mm_dmerged_bwd", epilogue,
                  [(proj, _tile(C_GA)), (proj, _tile(C_GB)), (za, _tile()), (zb, _tile())],
                  [(_sds(t, D, BF16), _tile()), (_sds(t, D, BF16), _tile()), (_sds(t, NP, BF16), gate_cols)], 1024, D)


def _dya_bwd(dza, wa, hraw, proj, g, dproj):
    t = dza.shape[0]

    def epilogue(accs, ins, outs, i, j):
        h_ref, o_ref, g_ref, _ = ins
        dh_ref, do_ref, dg_ref = outs

        @pl.when(i == 0)
        def _():
            dg_ref[...] = jnp.zeros_like(dg_ref)

        dy = accs[0]
        so = _sigmoid(o_ref[...].astype(F32))
        for h in range(MLH):
            sl = slice(h * DV, (h + 1) * DV)
            xn, rstd = _rms(h_ref[:, sl])
            gs = g_ref[:, sl]
            do_ref[:, sl] = (dy[:, sl] * xn * gs * so[:, sl] * (1.0 - so[:, sl])).astype(BF16)
            dhn = dy[:, sl] * so[:, sl]
            dg_ref[:, sl] += jnp.sum(dhn * xn, axis=0, keepdims=True)
            dh_ref[:, sl] = _rms_bwd(xn, rstd, dhn * gs)

    return _mm_ep([(dza, wa)], "nt", "mm_dya_bwd", epilogue,
                  [(hraw, _tile()), (proj, _tile(C_O)), (g, _row()), (dproj, lambda tm, tn: _ANY)],
                  [(_sds(t, D, F32), _tile()), (_sds(t, NP, BF16), _tile(C_O)), (_sds(1, D, F32), _row())],
                  512, D, aliases={3: 1})


def _up_act(hn, w_up):
    t = hn.shape[0]

    def epilogue(accs, ins, outs, i, j):
        r = jnp.maximum(accs[0], 0.0)
        outs[0][...] = (r * r).astype(BF16)
        outs[1][...] = accs[0].astype(BF16)

    return _mm_ep([(hn, w_up)], "nn", "mm_up_act", epilogue, [],
                  [(_sds(t, DFF, BF16), _tile()), (_sds(t, DFF, BF16), _tile())], 2048, 1024)


def _da_du(dxb, w_down, u):
    t = dxb.shape[0]

    def epilogue(accs, ins, outs, i, j):
        outs[0][...] = (accs[0] * 2.0 * jnp.maximum(ins[0][...].astype(F32), 0.0)).astype(BF16)

    return _mm_ep([(dxb, w_down)], "nt", "mm_da_du", epilogue, [(u, _tile())], [(_sds(t, DFF, BF16), _tile())],
                  1024, 2048)[0]


def _resid_norm_mm(a, w, x, g, name):
    t = x.shape[0]

    def epilogue(accs, ins, outs, i, j):
        x1 = ins[0][...] + accs[0]
        outs[0][...] = x1
        xn, _ = _rms(x1)
        outs[1][...] = (xn * ins[1][...]).astype(BF16)

    return _mm_ep([(a, w)], "nn", name, epilogue, [(x, _tile()), (g, _row())],
                  [(_sds(t, D, F32), _tile()), (_sds(t, D, BF16), _tile())], 1024, D, row_split=4)


def _norm_bwd_mm(dy, w, x, g, dres, name):
    t = x.shape[0]

    def init(outs):
        outs[2][...] = jnp.zeros_like(outs[2])

    def epilogue(accs, ins, outs, i, j):
        dh = accs[0]
        xn, rstd = _rms(ins[0][...])
        outs[2][...] += jnp.sum(dh * xn, axis=0, keepdims=True)
        dx = ins[2][...] + _rms_bwd(xn, rstd, dh * ins[1][...])
        outs[0][...] = dx
        outs[1][...] = dx.astype(BF16)

    return _mm_ep([(dy, w)], "nt", name, epilogue, [(x, _tile()), (g, _row()), (dres, _tile())],
                  [(_sds(t, D, F32), _tile()), (_sds(t, D, BF16), _tile()), (_sds(1, D, F32), _row())], 1024, D,
                  row_split=4, init=init)


def _ple_final_mm(hn2, w_gate, x2, pp, target, gf):
    t = x2.shape[0]

    def epilogue(accs, ins, outs, i, j):
        loss_ref, dg_ref, dx_ref, dpp_ref, dgp_ref = outs

        @pl.when(i == 0)
        def _():
            loss_ref[...] = jnp.zeros_like(loss_ref)
            dg_ref[...] = jnp.zeros_like(dg_ref)

        gate = _sigmoid(accs[0])
        pp_v = ins[1][...]
        x3 = ins[0][...] + gate * pp_v
        xn, rstd = _rms(x3)
        gf_v = ins[3][...]
        err = xn * gf_v - ins[2][...]
        loss_ref[...] += (0.5 / D) * jnp.sum(jnp.sum(err * err, axis=1, keepdims=True), axis=0, keepdims=True)
        dy = err * (1.0 / D)
        dg_ref[...] += jnp.sum(dy * xn, axis=0, keepdims=True)
        dx3 = _rms_bwd(xn, rstd, dy * gf_v)
        dx_ref[...] = dx3
        dpp_ref[...] = (dx3 * gate).astype(BF16)
        dgp_ref[...] = (dx3 * pp_v * gate * (1.0 - gate)).astype(BF16)

    one = lambda tm, tn: pl.BlockSpec((1, 1), lambda i, j, kk: (0, 0))
    return _mm_ep([(hn2, w_gate)], "nn", "mm_ple_final", epilogue,
                  [(x2, _tile()), (pp, _tile()), (target, _tile()), (gf, _row())],
                  [(_sds(1, 1, F32), one), (_sds(1, D, F32), _row()), (_sds(t, D, F32), _tile()),
                   (_sds(t, D, BF16), _tile()), (_sds(t, D, BF16), _tile())], 512, D)


_WIN_SEGMENTS = ((0, 3072, C_QK), (3072, 8, C_IF), (3080, 1024, C_QSW), (4104, 256, C_KV), (4360, 256, C_KV + 256),
                 (4616, 1024, C_GA), (5640, 1024, C_GB))
_WIN_SHARD = N_IN // 4


def _win_pieces():
    out = []
    for src, width, dst in _WIN_SEGMENTS:
        while width:
            chip, col = divmod(src, _WIN_SHARD)
            n = min(width, _WIN_SHARD - col)
            out.append((chip, col, n, dst))
            src, dst, width = src + n, dst + n, width - n
    return out


def _win_pad(shards):
    rows = shards.shape[1]
    tr = _pick(rows, 256)

    def body(s_ref, o_ref):
        for chip, col, n, dst in _win_pieces():
            o_ref[:, dst:dst + n] = s_ref[chip, :, col:col + n]
        o_ref[:, C_IF + 8:NP] = jnp.zeros((tr, NP - C_IF - 8), shards.dtype)

    return pl.pallas_call(
        body, name="win_pad", grid=(rows // tr,), in_specs=[pl.BlockSpec((4, tr, _WIN_SHARD), lambda i: (0, i, 0))],
        out_specs=pl.BlockSpec((tr, NP), lambda i: (i, 0)), out_shape=jax.ShapeDtypeStruct((rows, NP), shards.dtype),
        compiler_params=_params(),
    )(shards)


def _win_unpad(wp):
    rows = wp.shape[0]
    tr = _pick(rows, 256)

    def body(p_ref, o_ref):
        for chip, col, n, dst in _win_pieces():
            o_ref[chip, :, col:col + n] = p_ref[:, dst:dst + n]

    return pl.pallas_call(
        body, name="win_unpad", grid=(rows // tr,), in_specs=[pl.BlockSpec((tr, NP), lambda i: (i, 0))],
        out_specs=pl.BlockSpec((4, tr, _WIN_SHARD), lambda i: (0, i, 0)),
        out_shape=jax.ShapeDtypeStruct((4, rows, _WIN_SHARD), wp.dtype), compiler_params=_params(),
    )(wp)


def _local_step(x, p, target, w, late_weights=None, early_grads=None, mid_grads=None, last_grad=None):
    t = x.shape[0]
    pb = p.astype(BF16)
    w = dict(w)

    h0, proj, gates = _proj_in(x, w["norm_mix_g"], w["w_in"])
    qk = _conv_silu_fwd(proj, w["conv_qk"])
    grow, sneg_row = _gates_fwd(gates[:, 0:8].T, w["b_if"].reshape(8, 1))
    gcol, sneg_col = grow.T, sneg_row.T
    hraw, ya, cs, st = _mlstm_fwd(qk, proj, grow, gcol, w["mlstm_norm_g"])
    yb, lse = _swa_fwd(proj, w["sinks"])
    if late_weights is not None:
        w.update(late_weights(yb))
    merged, za, zb = _branch_merge(ya, yb, w["w_branch_a"], w["w_branch_b"], proj)
    x1, hn1 = _resid_norm_mm(merged, w["w_out"], x, w["norm_mlp_g"], "mm_out_norm")
    act, u = _up_act(hn1, w["w_up"])
    x2, hn2 = _resid_norm_mm(act, w["w_down"], x1, w["norm_ple_g"], "mm_down_norm")
    pp = _mm(pb, w["w_ple_proj"], "nn", F32, "mm_ple_proj")
    loss, d_final_g, dx3, dpp, dgpre = _ple_final_mm(hn2, w["w_ple_gate"], x2, pp, target, w["final_norm_g"])

    g = {"final_norm_g": d_final_g}
    g["w_ple_proj"] = _mm(pb, dpp, "tn", F32, "mm_d_ple_proj", out_chunks=4)
    g["w_ple_gate"] = _mm(hn2, dgpre, "tn", F32, "mm_d_ple_gate")
    dx2, dx2b, g["norm_ple_g"] = _norm_bwd_mm(dgpre, w["w_ple_gate"], x2, w["norm_ple_g"], dx3, "mm_dhn2_norm")
    g["w_down"] = _mm(act, dx2b, "tn", F32, "mm_d_down")
    du = _da_du(dx2b, w["w_down"], u)
    g["w_up"] = _mm(hn1, du, "tn", F32, "mm_d_up", out_chunks=4)
    dx1, dx1b, g["norm_mlp_g"] = _norm_bwd_mm(du, w["w_up"], x1, w["norm_mlp_g"], dx2, "mm_dhn1_norm")
    g["w_out"] = _mm(merged, dx1b, "tn", F32, "mm_d_out")
    dza, dzb, dproj = _dmerged_bwd(dx1b, w["w_out"], proj, za, zb)
    g["w_branch_a"] = _mm(ya, dza, "tn", F32, "mm_d_branch_a")
    g["w_branch_b"] = _mm(yb, dzb, "tn", F32, "mm_d_branch_b")
    gain = w["mlstm_norm_g"] if early_grads is None else w["mlstm_norm_g"] + early_grads(g)
    dyb = _mm(dzb, w["w_branch_b"], "nt", F32, "mm_dyb")
    dhraw, dproj, g["mlstm_norm_g"] = _dya_bwd(dza, w["w_branch_a"], hraw, proj, gain, dproj)
    if mid_grads is not None:
        sneg_col = sneg_col + mid_grads(dhraw)
    dqk, dproj, dif, g["b_if"] = _mlstm_bwd(qk, proj, grow, gcol, sneg_col, cs, st, hraw, dhraw, dproj)
    dc, g["conv_qk"] = _conv_silu_bwd_a(proj, w["conv_qk"], dqk)
    dproj = _conv_silu_bwd_b(dc, w["conv_qk"], dproj)
    dproj, dkv_self, dkv_prev, g["sinks"] = _swa_bwd(proj, w["sinks"], lse, dyb, dproj)
    dproj = _kv_combine(dkv_self, dkv_prev, dif, dproj)
    g["w_in"] = _mm(h0, dproj, "tn", F32, "mm_d_in")
    gain = w["norm_mix_g"] if last_grad is None else w["norm_mix_g"] + last_grad(g)
    grad_x, _, g["norm_mix_g"] = _norm_bwd_mm(dproj, w["w_in"], x, gain, dx1, "mm_dh0_norm")
    return loss, grad_x, g


_W4 = ("w_branch_a", "w_branch_b", "w_out", "w_ple_gate")
_SHARDED_NAMES = ("w_in", "w_up", "w_down", "w_ple_proj", "conv_qk") + _W4
_SMALL_ROWS = 16
_CONV_ROW = 8


def _group(s):
    return [s["w_in"], jnp.concatenate([s[n] for n in _W4], axis=0), s["w_up"], s["w_down"], s["w_ple_proj"]]


def _ungroup(arrs):
    out = {"w_in": arrs[0], "w_up": arrs[2], "w_down": arrs[3], "w_ple_proj": arrs[4]}
    rows = arrs[1].shape[0] // len(_W4)
    for i, n in enumerate(_W4):
        out[n] = arrs[1][i * rows:(i + 1) * rows]
    return out


def _rows_tile(rows):
    return 256 if rows % 256 == 0 else rows


_SMALL = ("norm_mix_g", "mlstm_norm_g", "norm_mlp_g", "norm_ple_g", "final_norm_g")


def _pack_small(vals, extra=None, conv=None):
    rows = [vals[n].reshape(1, D) for n in _SMALL]
    tail = [vals["b_if"].reshape(1, 8), vals["sinks"].reshape(1, SWH)]
    used = 8 + SWH
    if extra is not None:
        tail.append(extra.reshape(1, 1))
        used += 1
    tail.append(jnp.zeros((1, D - used), F32))
    rows.append(jnp.concatenate(tail, axis=1))
    rows.append(jnp.zeros((_CONV_ROW - len(rows), D), F32))
    rows.append(jnp.zeros((CONV, D), F32) if conv is None else conv)
    rows.append(jnp.zeros((_SMALL_ROWS - _CONV_ROW - CONV, D), F32))
    return jnp.concatenate(rows, axis=0)


def _unpack_small(slab, shapes):
    out = {n: slab[i].reshape(shapes[n]) for i, n in enumerate(_SMALL)}
    out["b_if"] = slab[5, 0:8].reshape(shapes["b_if"])
    out["sinks"] = slab[5, 8:8 + SWH].reshape(shapes["sinks"])
    return out


_MESH = pl.DeviceIdType.MESH
_HBM = pl.BlockSpec(memory_space=pltpu.HBM)
_VMEM = pl.BlockSpec(memory_space=pltpu.VMEM)


def _place():
    x, y, c = lax.axis_index("x"), lax.axis_index("y"), lax.axis_index("c")
    return x, y, c, 2 * x + y


def _chip_peer(x, y, r):
    return (x ^ (r >> 1), y ^ (r & 1))


def _half(ref, which):
    h = ref.shape[-2] // 2
    return pl.ds(which * h, h)


def _allgather_weights(shards, conv):
    n = len(shards)

    def body(*refs):
        ins, conv_ref = refs[:n], refs[n]
        outs, conv_out = refs[n + 1:2 * n + 1], refs[2 * n + 1]
        send_a, recv_a, send_b, recv_b, send_c, recv_c, local_sems = refs[2 * n + 2:]
        x, y, c, j = _place()
        sibling = (x, y, 1 - c)
        local = [pltpu.make_async_copy(ins[k], outs[k].at[j], local_sems.at[k]) for k in range(n)]
        local.append(pltpu.make_async_copy(conv_ref, conv_out.at[j], local_sems.at[n]))
        for cp in local:
            cp.start()

        def copy_a(k, r, chip):
            rows = _half(ins[k], c)
            return pltpu.make_async_remote_copy(
                src_ref=ins[k].at[rows], dst_ref=outs[k].at[chip, rows], send_sem=send_a.at[3 * k + r - 1],
                recv_sem=recv_a.at[3 * k + r - 1], device_id=(*_chip_peer(x, y, r), c), device_id_type=_MESH)

        def copy_b(k, r, chip, which):
            rows = _half(ins[k], which)
            return pltpu.make_async_remote_copy(
                src_ref=outs[k].at[chip, rows], dst_ref=outs[k].at[chip, rows], send_sem=send_b.at[3 * k + r - 1],
                recv_sem=recv_b.at[3 * k + r - 1], device_id=sibling, device_id_type=_MESH)

        def copy_c(r, chip):
            return pltpu.make_async_remote_copy(
                src_ref=conv_ref, dst_ref=conv_out.at[chip], send_sem=send_c.at[r - 1],
                recv_sem=recv_c.at[r - 1], device_id=(*_chip_peer(x, y, r), c), device_id_type=_MESH)

        for k in range(n):
            for r in (1, 2, 3):
                copy_a(k, r, j).start()
        for r in (1, 2, 3):
            copy_c(r, j).start()
        for k in range(n):
            for r in (1, 2, 3):
                copy_a(k, r, j ^ r).wait_recv()
                copy_b(k, r, j ^ r, c).start()
        for k in range(n):
            for r in (1, 2, 3):
                copy_b(k, r, j ^ r, 1 - c).wait_recv()
        for r in (1, 2, 3):
            copy_c(r, j ^ r).wait_recv()
        for k in range(n):
            for r in (1, 2, 3):
                copy_a(k, r, j).wait_send()
                copy_b(k, r, j ^ r, c).wait_send()
        for r in (1, 2, 3):
            copy_c(r, j).wait_send()
        for cp in local:
            cp.wait()

    return pl.pallas_call(
        body, name="allgather_weights",
        out_shape=[jax.ShapeDtypeStruct((4,) + s.shape, s.dtype) for s in shards]
        + [jax.ShapeDtypeStruct((4,) + conv.shape, F32)],
        in_specs=[_HBM] * (n + 1), out_specs=[_HBM] * (n + 1),
        scratch_shapes=[pltpu.SemaphoreType.DMA((3 * n,))] * 4 + [pltpu.SemaphoreType.DMA((3,))] * 2
        + [pltpu.SemaphoreType.DMA((n + 1,))],
    )(*shards, conv)


_SEM = pl.BlockSpec(memory_space=pltpu.SEMAPHORE)
_DATAFLOW = pltpu.SideEffectType.DATAFLOW_SIDE_EFFECTING


def _late_peer_copy(src_ref, land_ref, send_sems, recv_sems, x, y, c, j, r, chip):
    return pltpu.make_async_remote_copy(
        src_ref=src_ref, dst_ref=land_ref.at[chip], send_sem=send_sems.at[r - 1], recv_sem=recv_sems.at[r - 1],
        device_id=(*_chip_peer(x, y, r), c), device_id_type=_MESH)


def _late_gather_start(rest):
    def body(rest_ref, land_ref, send_sems, recv_sems, rest_thru, land_thru, token):
        x, y, c, j = _place()
        for r in (1, 2, 3):
            _late_peer_copy(rest_ref, land_ref, send_sems, recv_sems, x, y, c, j, r, j).start()
        token[...] = jnp.zeros_like(token)

    j = 2 * lax.axis_index("x") + lax.axis_index("y")
    land = lax.dynamic_update_slice(lax.empty((4,) + rest.shape, rest.dtype), rest[None], (j, 0, 0))
    return pl.pallas_call(
        body, name="late_gather_start",
        out_shape=(pltpu.SemaphoreType.DMA((3,)), pltpu.SemaphoreType.DMA((3,)), pltpu.HBM(rest.shape, rest.dtype),
                   pltpu.HBM(land.shape, land.dtype), jax.ShapeDtypeStruct((8, 128), F32)),
        in_specs=(_HBM, _HBM), out_specs=(_SEM, _SEM, _HBM, _HBM, _VMEM), input_output_aliases={0: 2, 1: 3},
        compiler_params=pltpu.CompilerParams(has_side_effects=_DATAFLOW),
    )(pltpu.with_memory_space_constraint(rest, pltpu.HBM), pltpu.with_memory_space_constraint(land, pltpu.HBM))


def _late_gather_wait(send_sems, recv_sems, rest_thru, land_thru, after):
    def body(rest_ref, land_ref, send_sems, recv_sems, after_ref, rest_dead, got_ref):
        x, y, c, j = _place()
        for r in (1, 2, 3):
            cp = _late_peer_copy(rest_ref, land_ref, send_sems, recv_sems, x, y, c, j, r, j ^ r)
            cp.wait_send()
            cp.wait_recv()

    return pl.pallas_call(
        body, name="late_gather_wait",
        out_shape=(pltpu.HBM(rest_thru.shape, rest_thru.dtype), pltpu.HBM(land_thru.shape, land_thru.dtype)),
        in_specs=(_HBM, _HBM, _SEM, _SEM, _ANY), out_specs=(_HBM, _HBM), input_output_aliases={0: 0, 1: 1},
        compiler_params=pltpu.CompilerParams(has_side_effects=_DATAFLOW),
    )(rest_thru, land_thru, send_sems, recv_sems, after)[1]


def _pair_sum(g, theirs, j, c, name):
    _, h, cols = theirs.shape
    tr = _rows_tile(h)
    nb = h // tr

    def body(idx_ref, a_ref, b_ref, own_ref, ob_ref):
        s = a_ref[0] + b_ref[0]
        ob_ref[0] = s.astype(BF16)

        @pl.when(pl.program_id(1) == idx_ref[0])
        def _():
            own_ref[...] = s

    blk = pl.BlockSpec((1, tr, cols), lambda i, k, idx_ref: (k, i, 0))
    return pl.pallas_call(
        body, name=name,
        grid_spec=pltpu.PrefetchScalarGridSpec(
            num_scalar_prefetch=1, grid=(nb, 4),
            in_specs=[pl.BlockSpec((1, tr, cols), lambda i, k, idx_ref: (k, idx_ref[1] * nb + i, 0)), blk],
            out_specs=[pl.BlockSpec((tr, cols), lambda i, k, idx_ref: (i, 0)), blk]),
        out_shape=[jax.ShapeDtypeStruct((h, cols), F32), jax.ShapeDtypeStruct(theirs.shape, BF16)],
        compiler_params=_params(),
    )(jnp.stack([j, c]).astype(jnp.int32), g, theirs)


def _chip_copies(srcs, lands, send_sems, recv_sems):
    x, y, c, j = _place()
    return [pltpu.make_async_remote_copy(
        src_ref=srcs[k].at[j ^ r], dst_ref=lands[k].at[r - 1], send_sem=send_sems.at[3 * k + r - 1],
        recv_sem=recv_sems.at[3 * k + r - 1], device_id=(*_chip_peer(x, y, r), c), device_id_type=_MESH)
        for k in range(len(srcs)) for r in (1, 2, 3)]


def _pair_copies(srcs, lands, send_sems, recv_sems):
    x, y, c, _ = _place()
    return [pltpu.make_async_remote_copy(
        src_ref=srcs[k].at[:, _half(srcs[k], 1 - c)], dst_ref=lands[k], send_sem=send_sems.at[k],
        recv_sem=recv_sems.at[k], device_id=(x, y, 1 - c), device_id_type=_MESH) for k in range(len(srcs))]


def _split_start(name, srcs, lands, copies, n_sems):
    n = len(srcs)

    def body(*refs):
        for cp in copies(refs[:n], refs[n:2 * n], refs[2 * n], refs[2 * n + 1]):
            cp.start()
        refs[-1][...] = jnp.zeros_like(refs[-1])

    arrays = list(srcs) + list(lands)
    out = pl.pallas_call(
        body, name=name,
        out_shape=(pltpu.SemaphoreType.DMA((n_sems,)), pltpu.SemaphoreType.DMA((n_sems,)),
                   *[pltpu.HBM(a.shape, a.dtype) for a in arrays], jax.ShapeDtypeStruct((8, 128), F32)),
        in_specs=[_HBM] * (2 * n), out_specs=(_SEM, _SEM, *([_HBM] * (2 * n)), _VMEM),
        input_output_aliases={k: 2 + k for k in range(2 * n)},
        compiler_params=pltpu.CompilerParams(has_side_effects=_DATAFLOW),
    )(*[pltpu.with_memory_space_constraint(a, pltpu.HBM) for a in arrays])
    return out[0], out[1], list(out[2:2 + n]), list(out[2 + n:2 + 2 * n]), out[-1]


def _split_wait(name, send_sems, recv_sems, srcs_thru, lands_thru, after, copies):
    n = len(srcs_thru)

    def body(*refs):
        for cp in copies(refs[:n], refs[n:2 * n], refs[2 * n], refs[2 * n + 1]):
            cp.wait_send()
            cp.wait_recv()

    arrays = list(srcs_thru) + list(lands_thru)
    out = pl.pallas_call(
        body, name=name, out_shape=tuple(pltpu.HBM(a.shape, a.dtype) for a in arrays),
        in_specs=[_HBM] * (2 * n) + [_SEM, _SEM, _ANY], out_specs=tuple([_HBM] * (2 * n)),
        input_output_aliases={k: k for k in range(2 * n)},
        compiler_params=pltpu.CompilerParams(has_side_effects=_DATAFLOW),
    )(*arrays, send_sems, recv_sems, after)
    return list(out[:n]), list(out[n:])


def _chip_exchange_start(ss, tag):
    lands = [lax.empty((3,) + s.shape[1:], s.dtype) for s in ss]
    return _split_start("chip_exchange_start_" + tag, ss, lands, _chip_copies, 3 * len(ss))


def _chip_exchange_wait(send_sems, recv_sems, ss_thru, lands_thru, after, tag):
    return _split_wait("chip_exchange_wait_" + tag, send_sems, recv_sems, ss_thru, lands_thru, after, _chip_copies)[1]


def _pair_exchange_start(gs, tag):
    lands = [lax.empty((4, g.shape[1] // 2, g.shape[2]), g.dtype) for g in gs]
    return _split_start("pair_exchange_start_" + tag, gs, lands, _pair_copies, len(gs))


def _pair_exchange_wait(send_sems, recv_sems, gs_thru, lands_thru, after, tag):
    return _split_wait("pair_exchange_wait_" + tag, send_sems, recv_sems, gs_thru, lands_thru, after, _pair_copies)


def _reduce4(own, others, c, name):
    h, cols = own.shape
    tr = _rows_tile(h)
    nb = h // tr

    def body(c_ref, s_ref, a0, a1, a2, o_ref):
        o_ref[...] = ((s_ref[...] + a0[0].astype(F32)) + a1[0].astype(F32)) + a2[0].astype(F32)

    def other(r):
        return pl.BlockSpec((1, tr, cols), lambda i, c_ref: (r, i, 0))

    return pl.pallas_call(
        body, name=name,
        grid_spec=pltpu.PrefetchScalarGridSpec(
            num_scalar_prefetch=1, grid=(nb,),
            in_specs=[pl.BlockSpec((tr, cols), lambda i, c_ref: (i, 0)), other(0), other(1), other(2)],
            out_specs=pl.BlockSpec((tr, cols), lambda i, c_ref: (c_ref[0] * nb + i, 0))),
        out_shape=jax.ShapeDtypeStruct((2 * h, cols), F32), compiler_params=_params(),
    )(c.reshape(1).astype(jnp.int32), own, others, others, others)


def _sibling_share(fulls, name):
    n = len(fulls)

    def body(*refs):
        outs, send_sems, recv_sems = refs[n:2 * n], refs[2 * n], refs[2 * n + 1]
        x, y, c, _ = _place()
        cps = [pltpu.make_async_remote_copy(
            src_ref=outs[k].at[_half(outs[k], c)], dst_ref=outs[k].at[_half(outs[k], c)], send_sem=send_sems.at[k],
            recv_sem=recv_sems.at[k], device_id=(x, y, 1 - c), device_id_type=_MESH) for k in range(n)]
        for cp in cps:
            cp.start()
        for cp in cps:
            cp.wait()

    return pl.pallas_call(
        body, name=name, out_shape=[jax.ShapeDtypeStruct(f.shape, F32) for f in fulls],
        in_specs=[_HBM] * n, out_specs=[_HBM] * n, input_output_aliases={k: k for k in range(n)},
        scratch_shapes=[pltpu.SemaphoreType.DMA((n,))] * 2,
    )(*fulls)


def _adamw(w, g, m, v):
    m1 = ADAM_B1 * m + (1.0 - ADAM_B1) * g
    v1 = ADAM_B2 * v + (1.0 - ADAM_B2) * (g * g)
    m_hat = m1 / (1.0 - ADAM_B1 ** ADAM_STEP)
    v_hat = v1 / (1.0 - ADAM_B2 ** ADAM_STEP)
    delta = -ADAM_LR * (m_hat / (jnp.sqrt(v_hat) + ADAM_EPS) + ADAM_WD * w)
    return delta, m1, v1


def _adamw_call(w, g, m, v, name):
    rows, cols = w.shape

    def body(w_ref, g_ref, m_ref, v_ref, d_out, m_out, v_out):
        delta, m1, v1 = _adamw(w_ref[...], g_ref[...], m_ref[...], v_ref[...])
        d_out[...] = delta
        m_out[...] = m1
        v_out[...] = v1

    if rows % 8 == 0:
        tr = _rows_tile(rows)
        blk, grid = pl.BlockSpec((tr, cols), lambda i: (i, 0)), (rows // tr,)
    else:
        blk, grid = pl.BlockSpec((rows, 128), lambda i: (0, i)), (cols // 128,)
    return pl.pallas_call(
        body, name=name, grid=grid, in_specs=[blk] * 4, out_specs=[blk] * 3,
        out_shape=[jax.ShapeDtypeStruct((rows, cols), F32)] * 3, compiler_params=_params(),
    )(w, g, m, v)


def _small_allreduce(vals):
    def body(v_ref, out_ref, buf, send_sems, recv_sems):
        x, y, c, j = _place()
        me = 2 * j + c
        buf[0] = v_ref[...]

        def copy(r):
            return pltpu.make_async_remote_copy(
                src_ref=v_ref, dst_ref=buf.at[r], send_sem=send_sems.at[r - 1], recv_sem=recv_sems.at[r - 1],
                device_id=(x ^ (r >> 2), y ^ ((r >> 1) & 1), c ^ (r & 1)), device_id_type=_MESH)

        for r in range(1, 8):
            copy(r).start()
        for r in range(1, 8):
            copy(r).wait()
        acc = buf[me ^ 0]
        for d in range(1, 8):
            acc = acc + buf[me ^ d]
        out_ref[...] = acc

    return pl.pallas_call(
        body, name="small_allreduce", out_shape=jax.ShapeDtypeStruct((_SMALL_ROWS, D), F32),
        in_specs=[_VMEM], out_specs=_VMEM,
        scratch_shapes=[pltpu.VMEM((8, _SMALL_ROWS, D), F32), pltpu.SemaphoreType.DMA((7,)),
                        pltpu.SemaphoreType.DMA((7,))],
    )(vals)


_NAMES = ("norm_mix_g", "w_in", "conv_qk", "b_if", "mlstm_norm_g", "sinks", "w_branch_a", "w_branch_b", "w_out",
          "norm_mlp_g", "w_up", "w_down", "norm_ple_g", "w_ple_gate", "w_ple_proj", "final_norm_g")
_GROUP_NAMES = ("w_in", "w4", "w_up", "w_down", "w_ple_proj")


def _step(x, p, target, w, m, v):
    c = lax.axis_index("c")
    j = 2 * lax.axis_index("x") + lax.axis_index("y")

    def shards(d):
        return {n: d[n][0] for n in _SHARDED_NAMES}

    ws = shards(w)
    w_in_all, conv_all = _allgather_weights([ws["w_in"].astype(BF16)], ws["conv_qk"])
    rows_pp = PLE * (D // 4) // D
    rest = jnp.concatenate([ws[n] for n in _W4] + [ws["w_up"], ws["w_down"], ws["w_ple_proj"].reshape(rows_pp, D)],
                           axis=0)
    rest = (rest + 0.0 * conv_all[0, 0, 0]).astype(BF16)
    send_sems, recv_sems, rest_thru, land_thru, token = _late_gather_start(rest)
    full = {n: w[n] for n in ("mlstm_norm_g", "norm_mlp_g", "norm_ple_g", "b_if", "sinks")}
    full["norm_mix_g"] = w["norm_mix_g"] + token[0, 0]
    full["final_norm_g"] = w["final_norm_g"].reshape(1, D)
    full["w_in"] = _win_pad(w_in_all)
    full["conv_qk"] = jnp.swapaxes(conv_all, 0, 1).reshape(CONV, D)

    def late_weights(after):
        land = _late_gather_wait(send_sems, recv_sems, rest_thru, land_thru, after)
        out = {n: land[:, i * (D // 4):(i + 1) * (D // 4)].reshape(D, D) for i, n in enumerate(_W4)}
        out["w_up"] = land[:, D:2 * D]
        out["w_down"] = land[:, 2 * D:3 * D].reshape(DFF, D)
        out["w_ple_proj"] = jnp.swapaxes(land[:, 3 * D:3 * D + rows_pp].reshape(4, PLE, D // 4), 0, 1).reshape(PLE, D)
        return out

    early, last = {}, {}

    def pair_sums(by_dest, theirs, names):
        return [_pair_sum(a, b, j, c, "pair_sum_" + n) for a, b, n in zip(by_dest, theirs, names)]

    def early_grads(g):
        by_dest = [jnp.stack([g[n].reshape(4, D // 4, D) for n in _W4], axis=1).reshape(4, D, D),
                   g["w_up"], g["w_down"].reshape(4, DFF // 4, D), g["w_ple_proj"]]
        *early["pair"], token = _pair_exchange_start(by_dest, "early")
        return token[0, 0]

    def mid_grads(after):
        early["sums"] = pair_sums(*_pair_exchange_wait(*early["pair"], after, "early"), _GROUP_NAMES[1:])
        *early["flight"], token = _chip_exchange_start([s[1] for s in early["sums"]], "early")
        return token[0, 0]

    def last_grad(g):
        *last["pair"], token = _pair_exchange_start([_win_unpad(g["w_in"])], "w_in")
        return token[0, 0]

    loss, grad_x, g = _local_step(x[0], p[0, 0], target[0], full, late_weights, early_grads, mid_grads, last_grad)

    last["sums"] = pair_sums(*_pair_exchange_wait(*last["pair"], grad_x, "w_in"), _GROUP_NAMES[:1])
    *last["flight"], token = _chip_exchange_start([s[1] for s in last["sums"]], "w_in")

    def reduce_share(sums, others, names, tag):
        halves = [_reduce4(s[0], b, c, "reduce4_" + n) for s, b, n in zip(sums, others, names)]
        return list(_sibling_share(halves, "sibling_share_" + tag))

    ms, vs = shards(m), shards(v)
    grads = reduce_share(early["sums"], _chip_exchange_wait(*early["flight"], token, "early"), _GROUP_NAMES[1:], "early")
    upd = [_adamw_call(wa, ga, ma, va, "adamw_" + n)
           for wa, ga, ma, va, n in zip(_group(ws)[1:], grads, _group(ms)[1:], _group(vs)[1:], _GROUP_NAMES[1:])]
    small_g = _small_allreduce(_pack_small(g, extra=loss, conv=g["conv_qk"]))
    conv_g = lax.dynamic_slice(small_g[_CONV_ROW:_CONV_ROW + CONV], (0, j * (D // 4)), (CONV, D // 4))
    conv_upd = _adamw_call(ws["conv_qk"], conv_g, ms["conv_qk"], vs["conv_qk"], "adamw_conv")
    small_upd = _adamw_call(_pack_small(w), small_g, _pack_small(m), _pack_small(v), "adamw_small")

    done = sum(a[0][0:1, 0:1] for a in upd + [conv_upd, small_upd])
    others = _chip_exchange_wait(*last["flight"], done, "w_in")
    grads = reduce_share(last["sums"], others, _GROUP_NAMES[:1], "w_in") + list(grads)
    upd_in = _adamw_call(*[jnp.swapaxes(a, 0, 1) for a in (ws["w_in"], grads[0], ms["w_in"], vs["w_in"])], "adamw_w_in")
    upd = [[jnp.swapaxes(a, 0, 1) for a in upd_in]] + upd

    shapes = {n: w[n].shape for n in _NAMES}
    res = []
    for k in range(4):
        big = _ungroup(list(grads) if k == 0 else [u[k - 1] for u in upd])
        big["conv_qk"] = conv_g if k == 0 else conv_upd[k - 1]
        leaves = _unpack_small(small_g if k == 0 else small_upd[k - 1], shapes)
        leaves.update({n: a.reshape(shapes[n]) for n, a in big.items()})
        res.append(leaves)

    out = [small_g[5, 8 + SWH], grad_x[None]]
    for k in range(4):
        out += [res[k][n] for n in _NAMES]
    return tuple(out)


def kernel(x, p, norm_mix_g, w_in, conv_qk, b_if, mlstm_norm_g, sinks, w_branch_a, w_branch_b, w_out, norm_mlp_g, w_up, w_down, norm_ple_g, w_ple_gate, w_ple_proj, final_norm_g, loss_target, m_norm_mix_g, m_w_in, m_conv_qk, m_b_if, m_mlstm_norm_g, m_sinks, m_w_branch_a, m_w_branch_b, m_w_out, m_norm_mlp_g, m_w_up, m_w_down, m_norm_ple_g, m_w_ple_gate, m_w_ple_proj, m_final_norm_g, v_norm_mix_g, v_w_in, v_conv_qk, v_b_if, v_mlstm_norm_g, v_sinks, v_w_branch_a, v_w_branch_b, v_w_out, v_norm_mlp_g, v_w_up, v_w_down, v_norm_ple_g, v_w_ple_gate, v_w_ple_proj, v_final_norm_g):
    w = dict(zip(_NAMES, (norm_mix_g, w_in, conv_qk, b_if, mlstm_norm_g, sinks, w_branch_a, w_branch_b, w_out,
                          norm_mlp_g, w_up, w_down, norm_ple_g, w_ple_gate, w_ple_proj, final_norm_g)))
    m = dict(zip(_NAMES, (m_norm_mix_g, m_w_in, m_conv_qk, m_b_if, m_mlstm_norm_g, m_sinks, m_w_branch_a,
                          m_w_branch_b, m_w_out, m_norm_mlp_g, m_w_up, m_w_down, m_norm_ple_g, m_w_ple_gate,
                          m_w_ple_proj, m_final_norm_g)))
    v = dict(zip(_NAMES, (v_norm_mix_g, v_w_in, v_conv_qk, v_b_if, v_mlstm_norm_g, v_sinks, v_w_branch_a,
                          v_w_branch_b, v_w_out, v_norm_mlp_g, v_w_up, v_w_down, v_norm_ple_g, v_w_ple_gate,
                          v_w_ple_proj, v_final_norm_g)))
    return _step(x, p, loss_target, w, m, v)
```
